```python
import math
import jax, jax.numpy as jnp
from jax import lax
import numpy as np

D_MODEL = 1024
BATCH = 8
SEQ = 8192
DEPTH = 4

N_MIXERS = 4
HEAD_DIM = 64
GROUP_W = D_MODEL // N_MIXERS
HEADS_PER_MIXER = GROUP_W // HEAD_DIM
D_MIX = N_MIXERS * GROUP_W
N_CHUNKS = 13
D_IN = N_CHUNKS * GROUP_W
SHORT_CONV_W = 3
RG_CONV_W = 4
RG_C = 8.0
GMLP_CHUNK = 128
ATTN_PATTERNS = ((128, 1), (512, 4), (2048, 16))
ATTN_BLOCK = 128
NORM_EPS = 1e-6

kernel_name = "hybrid_parallel_groups_conv_rglru_gmlp_dilated_attn"


def rms_norm(x, g):
    xf = x.astype(jnp.float32)
    y = xf * lax.rsqrt(jnp.mean(xf * xf, axis=-1, keepdims=True) + NORM_EPS)
    return (y * g.astype(jnp.float32)).astype(x.dtype)


def causal_depthwise_conv(x, w):
    K, C = w.shape
    return lax.conv_general_dilated(
        x, w[:, None, :].astype(x.dtype), window_strides=(1,), padding=[(K - 1, 0)],
        dimension_numbers=("NWC", "WIO", "NWC"), feature_group_count=C)


def rg_lru(xb, wa, ba, wx, bx, lam):
    B, S, C = xb.shape
    xh = xb.reshape(B, S, HEADS_PER_MIXER, HEAD_DIM)
    r = jax.nn.sigmoid(jnp.einsum('bshi,hij->bshj', xh, wa).reshape(B, S, C) + ba)
    i = jax.nn.sigmoid(jnp.einsum('bshi,hij->bshj', xh, wx).reshape(B, S, C) + bx)
    log_a = (-RG_C * r.astype(jnp.float32)) * jax.nn.softplus(-lam.astype(jnp.float32))
    a = jnp.exp(log_a)
    mult = jnp.sqrt(-jnp.expm1(2.0 * log_a))
    b = mult * (i * xb).astype(jnp.float32)

    def combine(e1, e2):
        a1, b1 = e1
        a2, b2 = e2
        return a1 * a2, a2 * b1 + b2

    _, h = lax.associative_scan(combine, (a, b), axis=1)
    return h.astype(xb.dtype)


def dilated_window_attention(q, k, v, slopes, window, dil):
    B, S, H, Dh = q.shape
    n_back = window // dil
    span = dil * ATTN_BLOCK
    Sp = -(-S // span) * span
    pad = Sp - S
    if pad:
        pw = ((0, 0), (0, pad), (0, 0), (0, 0))
        q, k, v = jnp.pad(q, pw), jnp.pad(k, pw), jnp.pad(v, pw)
    L = Sp // dil
    nb = L // ATTN_BLOCK
    to_blocks = lambda t: t.reshape(B, nb, ATTN_BLOCK, dil, H, Dh)
    qb, kb, vb = to_blocks(q), to_blocks(k), to_blocks(v)

    def with_prev(t):
        prev = jnp.pad(t[:, :-1], ((0, 0), (1, 0), (0, 0), (0, 0), (0, 0), (0, 0)))
        return jnp.concatenate([prev, t], axis=2)

    kk, vv = with_prev(kb), with_prev(vb)
    scale = 1.0 / math.sqrt(Dh)
    s = jnp.einsum('bnqrhd,bnkrhd->bnrhqk', qb.astype(jnp.float32),
                   kk.astype(jnp.float32)) * scale
    qi = jnp.arange(ATTN_BLOCK)[:, None]
    ki = jnp.arange(2 * ATTN_BLOCK)[None, :]
    delta = qi + ATTN_BLOCK - ki
    band = (delta >= 0) & (delta <= n_back)
    first_ok = (jnp.arange(nb)[:, None, None] > 0) | (ki[None] >= ATTN_BLOCK)
    mask = band[None] & first_ok
    bias = -slopes[:, None, None] * (delta * dil).astype(jnp.float32)[None]
    s = jnp.where(mask[None, :, None, None], s + bias, -jnp.inf)
    m = jnp.max(s, axis=-1, keepdims=True)
    p = jnp.exp(s - m)
    l = jnp.sum(p, axis=-1, keepdims=True)
    o = jnp.einsum('bnrhqk,bnkrhd->bnqrhd', p, vv.astype(jnp.float32))
    o = o / jnp.transpose(l, (0, 1, 4, 2, 3, 5))
    lse = jnp.transpose((m + jnp.log(l))[..., 0], (0, 1, 4, 2, 3))
    o = o.reshape(B, Sp, H, Dh)[:, :S]
    lse = lse.reshape(B, Sp, H)[:, :S]
    return o, lse


def _fwd_setup_inputs(seed: int = 0) -> dict:
    key = jax.random.key(seed)
    ks = jax.random.split(key, 16)
    f32 = jnp.float32
    nrm = lambda k, shape, sc: jax.random.normal(k, shape, f32) * sc
    x = jax.random.normal(ks[0], (BATCH, SEQ, D_MODEL), f32)
    norm_g = 1.0 + nrm(ks[1], (DEPTH, D_MODEL), 0.01)
    w_in = nrm(ks[2], (DEPTH, D_MODEL, D_IN), D_MODEL ** -0.5)
    conv_a_w = nrm(ks[3], (DEPTH, SHORT_CONV_W, GROUP_W), SHORT_CONV_W ** -0.5)
    conv_r_w = nrm(ks[4], (DEPTH, RG_CONV_W, GROUP_W), RG_CONV_W ** -0.5)
    conv_r_b = nrm(ks[5], (DEPTH, GROUP_W), 0.01)
    lru_wa = nrm(ks[6], (DEPTH, HEADS_PER_MIXER, HEAD_DIM, HEAD_DIM), HEAD_DIM ** -0.5)
    lru_ba = nrm(ks[7], (DEPTH, GROUP_W), 0.01)
    lru_wx = nrm(ks[8], (DEPTH, HEADS_PER_MIXER, HEAD_DIM, HEAD_DIM), HEAD_DIM ** -0.5)
    lru_bx = nrm(ks[9], (DEPTH, GROUP_W), 0.01)
    a_c = jax.random.uniform(ks[10], (DEPTH, GROUP_W), f32, 0.9, 0.999)
    sig = a_c ** (1.0 / RG_C)
    lru_lambda = jnp.log(sig) - jnp.log1p(-sig)
    gmlp_norm_g = 1.0 + nrm(ks[11], (DEPTH, GROUP_W), 0.01)
    gmlp_ws = nrm(ks[12], (DEPTH, HEADS_PER_MIXER, GMLP_CHUNK, GMLP_CHUNK), GMLP_CHUNK ** -0.5)
    gmlp_bs = 1.0 + nrm(ks[13], (DEPTH, HEADS_PER_MIXER, GMLP_CHUNK), 0.1)
    w_out = nrm(ks[14], (DEPTH, D_MIX, D_MODEL), D_MIX ** -0.5)
    final_g = 1.0 + nrm(ks[15], (D_MODEL,), 0.01)
    return {"x": x, "norm_g": norm_g, "w_in": w_in, "conv_a_w": conv_a_w,
            "conv_r_w": conv_r_w, "conv_r_b": conv_r_b, "lru_wa": lru_wa,
            "lru_ba": lru_ba, "lru_wx": lru_wx, "lru_bx": lru_bx,
            "lru_lambda": lru_lambda, "gmlp_norm_g": gmlp_norm_g,
            "gmlp_ws": gmlp_ws, "gmlp_bs": gmlp_bs, "w_out": w_out,
            "final_g": final_g}


def _fwd_reference(x, norm_g, w_in, conv_a_w, conv_r_w, conv_r_b, lru_wa, lru_ba,
              lru_wx, lru_bx, lru_lambda, gmlp_norm_g, gmlp_ws, gmlp_bs, w_out,
              final_g):
    B, S, _ = x.shape
    H = HEADS_PER_MIXER
    slopes = 2.0 ** (-8.0 * jnp.arange(1, H + 1, dtype=jnp.float32) / H)
    causal_chunk = jnp.tril(jnp.ones((GMLP_CHUNK, GMLP_CHUNK), dtype=bool))
    n_chunk = S // GMLP_CHUNK
    for l in range(DEPTH):
        h = rms_norm(x, norm_g[l])
        z = jnp.einsum('bsd,de->bse', h, w_in[l])
        (a_x, a_b, a_c, a_g,
         r_x, r_g,
         c_u, c_v, c_g,
         d_q, d_k, d_v, d_g) = jnp.split(z, N_CHUNKS, axis=-1)

        y_a = a_b * causal_depthwise_conv(a_c * a_x, conv_a_w[l]) * jax.nn.silu(a_g)

        xb = causal_depthwise_conv(r_x, conv_r_w[l]) + conv_r_b[l]
        y_b = rg_lru(xb, lru_wa[l], lru_ba[l], lru_wx[l], lru_bx[l], lru_lambda[l]) * jax.nn.silu(r_g)

        u = jax.nn.gelu(c_u)
        vv = rms_norm(jax.nn.gelu(c_v), gmlp_norm_g[l])
        vv = vv.reshape(B, n_chunk, GMLP_CHUNK, H, HEAD_DIM)
        ws = jnp.where(causal_chunk[None], gmlp_ws[l], 0.0).astype(vv.dtype)
        sp = jnp.einsum('hts,bnshc->bnthc', ws, vv) + jnp.transpose(gmlp_bs[l])[:, :, None]
        y_c = u * sp.reshape(B, S, GROUP_W) * jax.nn.silu(c_g)

        q = d_q.reshape(B, S, H, HEAD_DIM)
        k = d_k.reshape(B, S, H, HEAD_DIM)
        v = d_v.reshape(B, S, H, HEAD_DIM)
        outs, lses = [], []
        for window, dil in ATTN_PATTERNS:
            o_p, lse_p = dilated_window_attention(q, k, v, slopes, window, dil)
            outs.append(o_p)
            lses.append(lse_p)
        wts = jax.nn.softmax(jnp.stack(lses), axis=0)
        o = jnp.einsum('pbsh,pbshd->bshd', wts, jnp.stack(outs))
        y_d = o.reshape(B, S, GROUP_W).astype(x.dtype) * jax.nn.silu(d_g)

        y = jnp.concatenate([y_a, y_b, y_c, y_d], axis=-1)
        x = x + jnp.einsum('bse,ed->bsd', y, w_out[l])
    return rms_norm(x, final_g)


import jax as _jax
import jax.numpy as _jnp

TWIN_FORMAT = 'train_step'
FWD_PARAMS = ['x', 'norm_g', 'w_in', 'conv_a_w', 'conv_r_w', 'conv_r_b', 'lru_wa', 'lru_ba', 'lru_wx', 'lru_bx', 'lru_lambda', 'gmlp_norm_g', 'gmlp_ws', 'gmlp_bs', 'w_out', 'final_g']
TWIN_WEIGHTS = ['norm_g', 'w_in', 'conv_a_w', 'conv_r_w', 'conv_r_b', 'lru_wa', 'lru_ba', 'lru_wx', 'lru_bx', 'lru_lambda', 'gmlp_norm_g', 'gmlp_ws', 'gmlp_bs', 'w_out', 'final_g']
TWIN_DIFF_INPUT = 'x'
TWIN_INPUTS = ['x', 'norm_g', 'w_in', 'conv_a_w', 'conv_r_w', 'conv_r_b', 'lru_wa', 'lru_ba', 'lru_wx', 'lru_bx', 'lru_lambda', 'gmlp_norm_g', 'gmlp_ws', 'gmlp_bs', 'w_out', 'final_g', 'loss_target', 'm_norm_g', 'm_w_in', 'm_conv_a_w', 'm_conv_r_w', 'm_conv_r_b', 'm_lru_wa', 'm_lru_ba', 'm_lru_wx', 'm_lru_bx', 'm_lru_lambda', 'm_gmlp_norm_g', 'm_gmlp_ws', 'm_gmlp_bs', 'm_w_out', 'm_final_g', 'v_norm_g', 'v_w_in', 'v_conv_a_w', 'v_conv_r_w', 'v_conv_r_b', 'v_lru_wa', 'v_lru_ba', 'v_lru_wx', 'v_lru_bx', 'v_lru_lambda', 'v_gmlp_norm_g', 'v_gmlp_ws', 'v_gmlp_bs', 'v_w_out', 'v_final_g']
TWIN_OUTPUTS = ['loss', 'grad_x', 'grad_norm_g', 'grad_w_in', 'grad_conv_a_w', 'grad_conv_r_w', 'grad_conv_r_b', 'grad_lru_wa', 'grad_lru_ba', 'grad_lru_wx', 'grad_lru_bx', 'grad_lru_lambda', 'grad_gmlp_norm_g', 'grad_gmlp_ws', 'grad_gmlp_bs', 'grad_w_out', 'grad_final_g', 'delta_norm_g', 'delta_w_in', 'delta_conv_a_w', 'delta_conv_r_w', 'delta_conv_r_b', 'delta_lru_wa', 'delta_lru_ba', 'delta_lru_wx', 'delta_lru_bx', 'delta_lru_lambda', 'delta_gmlp_norm_g', 'delta_gmlp_ws', 'delta_gmlp_bs', 'delta_w_out', 'delta_final_g', 'new_m_norm_g', 'new_m_w_in', 'new_m_conv_a_w', 'new_m_conv_r_w', 'new_m_conv_r_b', 'new_m_lru_wa', 'new_m_lru_ba', 'new_m_lru_wx', 'new_m_lru_bx', 'new_m_lru_lambda', 'new_m_gmlp_norm_g', 'new_m_gmlp_ws', 'new_m_gmlp_bs', 'new_m_w_out', 'new_m_final_g', 'new_v_norm_g', 'new_v_w_in', 'new_v_conv_a_w', 'new_v_conv_r_w', 'new_v_conv_r_b', 'new_v_lru_wa', 'new_v_lru_ba', 'new_v_lru_wx', 'new_v_lru_bx', 'new_v_lru_lambda', 'new_v_gmlp_norm_g', 'new_v_gmlp_ws', 'new_v_gmlp_bs', 'new_v_w_out', 'new_v_final_g']
TWIN_LEAF_KINDS = {'loss': 'loss', 'grad_x': 'grad_x', 'grad_norm_g': 'grad_w', 'grad_w_in': 'grad_w', 'grad_conv_a_w': 'grad_w', 'grad_conv_r_w': 'grad_w', 'grad_conv_r_b': 'grad_w', 'grad_lru_wa': 'grad_w', 'grad_lru_ba': 'grad_w', 'grad_lru_wx': 'grad_w', 'grad_lru_bx': 'grad_w', 'grad_lru_lambda': 'grad_w', 'grad_gmlp_norm_g': 'grad_w', 'grad_gmlp_ws': 'grad_w', 'grad_gmlp_bs': 'grad_w', 'grad_w_out': 'grad_w', 'grad_final_g': 'grad_w', 'delta_norm_g': 'delta_w', 'delta_w_in': 'delta_w', 'delta_conv_a_w': 'delta_w', 'delta_conv_r_w': 'delta_w', 'delta_conv_r_b': 'delta_w', 'delta_lru_wa': 'delta_w', 'delta_lru_ba': 'delta_w', 'delta_lru_wx': 'delta_w', 'delta_lru_bx': 'delta_w', 'delta_lru_lambda': 'delta_w', 'delta_gmlp_norm_g': 'delta_w', 'delta_gmlp_ws': 'delta_w', 'delta_gmlp_bs': 'delta_w', 'delta_w_out': 'delta_w', 'delta_final_g': 'delta_w', 'new_m_norm_g': 'new_m', 'new_m_w_in': 'new_m', 'new_m_conv_a_w': 'new_m', 'new_m_conv_r_w': 'new_m', 'new_m_conv_r_b': 'new_m', 'new_m_lru_wa': 'new_m', 'new_m_lru_ba': 'new_m', 'new_m_lru_wx': 'new_m', 'new_m_lru_bx': 'new_m', 'new_m_lru_lambda': 'new_m', 'new_m_gmlp_norm_g': 'new_m', 'new_m_gmlp_ws': 'new_m', 'new_m_gmlp_bs': 'new_m', 'new_m_w_out': 'new_m', 'new_m_final_g': 'new_m', 'new_v_norm_g': 'new_v', 'new_v_w_in': 'new_v', 'new_v_conv_a_w': 'new_v', 'new_v_conv_r_w': 'new_v', 'new_v_conv_r_b': 'new_v', 'new_v_lru_wa': 'new_v', 'new_v_lru_ba': 'new_v', 'new_v_lru_wx': 'new_v', 'new_v_lru_bx': 'new_v', 'new_v_lru_lambda': 'new_v', 'new_v_gmlp_norm_g': 'new_v', 'new_v_gmlp_ws': 'new_v', 'new_v_gmlp_bs': 'new_v', 'new_v_w_out': 'new_v', 'new_v_final_g': 'new_v'}


def _forward(args):
    return _fwd_reference(*[args[k] for k in FWD_PARAMS])


def _output_shape():
    def fwd():
        inp = _fwd_setup_inputs(0)
        return _fwd_reference(*[inp[k] for k in FWD_PARAMS])
    out = _jax.eval_shape(fwd)
    return out.shape, out.dtype

N_MICROBATCH = 1
ADAM_LR = 0.001
ADAM_B1 = 0.9
ADAM_B2 = 0.999
ADAM_EPS = 1e-08
ADAM_WD = 0.01
ADAM_STEP = 10
PER_EXAMPLE_BATCH_AXIS = {'x': 0, 'loss_target': 0}
SHARED_INPUTS = []
_WEIGHT_DTYPES = {'norm_g': _jnp.float32, 'w_in': _jnp.float32, 'conv_a_w': _jnp.float32, 'conv_r_w': _jnp.float32, 'conv_r_b': _jnp.float32, 'lru_wa': _jnp.float32, 'lru_ba': _jnp.float32, 'lru_wx': _jnp.float32, 'lru_bx': _jnp.float32, 'lru_lambda': _jnp.float32, 'gmlp_norm_g': _jnp.float32, 'gmlp_ws': _jnp.float32, 'gmlp_bs': _jnp.float32, 'w_out': _jnp.float32, 'final_g': _jnp.float32}
MOMENT_SCALE = {'norm_g': 1.967080e-01, 'w_in': 1.114848e-01, 'conv_a_w': 1.497283e-01, 'conv_r_w': 1.399492e-01, 'conv_r_b': 1.451058e+00, 'lru_wa': 6.303205e-02, 'lru_ba': 4.414774e-02, 'lru_wx': 1.181202e-01, 'lru_bx': 5.109766e-02, 'lru_lambda': 7.067220e-02, 'gmlp_norm_g': 6.694316e-02, 'gmlp_ws': 4.927366e-02, 'gmlp_bs': 6.996659e-02, 'w_out': 1.223475e-01, 'final_g': 6.401436e+01}


def _to_microbatches(a, axis):
    t = _jnp.moveaxis(a, axis, 0)
    t = t.reshape((N_MICROBATCH, t.shape[0] // N_MICROBATCH) + t.shape[1:])
    return _jnp.moveaxis(t, 1, axis + 1)


def setup_inputs(seed: int = 0) -> dict:
    inp = _fwd_setup_inputs(seed)
    key = _jax.random.fold_in(_jax.random.key(seed), 7919)
    shape, _ = _output_shape()
    out = dict(inp)
    out["loss_target"] = _jax.random.normal(_jax.random.fold_in(key, 0), shape, _jnp.float32)
    for i, name in enumerate(TWIN_WEIGHTS):
        w = inp[name].astype(_jnp.float32)
        if MOMENT_SCALE is None:
            s = _jnp.sqrt(_jnp.mean(_jnp.square(w)) + 1e-30)
        else:
            s = MOMENT_SCALE[name]
        km, kv = _jax.random.split(_jax.random.fold_in(key, i + 1))
        out[name] = w
        out["m_" + name] = s * _jax.random.normal(km, w.shape, _jnp.float32)
        out["v_" + name] = (s * s) * _jax.random.uniform(kv, w.shape, _jnp.float32, 0.5, 1.5)
    if N_MICROBATCH > 1:
        for name, axis in PER_EXAMPLE_BATCH_AXIS.items():
            out[name] = _to_microbatches(out[name], axis)
    return {'x': out['x'], 'norm_g': out['norm_g'], 'w_in': out['w_in'], 'conv_a_w': out['conv_a_w'], 'conv_r_w': out['conv_r_w'], 'conv_r_b': out['conv_r_b'], 'lru_wa': out['lru_wa'], 'lru_ba': out['lru_ba'], 'lru_wx': out['lru_wx'], 'lru_bx': out['lru_bx'], 'lru_lambda': out['lru_lambda'], 'gmlp_norm_g': out['gmlp_norm_g'], 'gmlp_ws': out['gmlp_ws'], 'gmlp_bs': out['gmlp_bs'], 'w_out': out['w_out'], 'final_g': out['final_g'], 'loss_target': out['loss_target'], 'm_norm_g': out['m_norm_g'], 'm_w_in': out['m_w_in'], 'm_conv_a_w': out['m_conv_a_w'], 'm_conv_r_w': out['m_conv_r_w'], 'm_conv_r_b': out['m_conv_r_b'], 'm_lru_wa': out['m_lru_wa'], 'm_lru_ba': out['m_lru_ba'], 'm_lru_wx': out['m_lru_wx'], 'm_lru_bx': out['m_lru_bx'], 'm_lru_lambda': out['m_lru_lambda'], 'm_gmlp_norm_g': out['m_gmlp_norm_g'], 'm_gmlp_ws': out['m_gmlp_ws'], 'm_gmlp_bs': out['m_gmlp_bs'], 'm_w_out': out['m_w_out'], 'm_final_g': out['m_final_g'], 'v_norm_g': out['v_norm_g'], 'v_w_in': out['v_w_in'], 'v_conv_a_w': out['v_conv_a_w'], 'v_conv_r_w': out['v_conv_r_w'], 'v_conv_r_b': out['v_conv_r_b'], 'v_lru_wa': out['v_lru_wa'], 'v_lru_ba': out['v_lru_ba'], 'v_lru_wx': out['v_lru_wx'], 'v_lru_bx': out['v_lru_bx'], 'v_lru_lambda': out['v_lru_lambda'], 'v_gmlp_norm_g': out['v_gmlp_norm_g'], 'v_gmlp_ws': out['v_gmlp_ws'], 'v_gmlp_bs': out['v_gmlp_bs'], 'v_w_out': out['v_w_out'], 'v_final_g': out['v_final_g']}


def _loss(weights, diff, rest, loss_target):
    with _jax.named_scope("forward"):
        args = {**rest, TWIN_DIFF_INPUT: diff, **{k: w.astype(_WEIGHT_DTYPES[k]) for k, w in weights.items()}}
        y = _forward(args)
    with _jax.named_scope("loss_head"):
        err = _jnp.square(y.astype(_jnp.float32) - loss_target)
        return 0.5 * _jnp.sum(_jnp.mean(err, axis=-1)) if err.ndim else 0.5 * err


def _adamw(w, g, m, v):
    m = ADAM_B1 * m + (1.0 - ADAM_B1) * g
    v = ADAM_B2 * v + (1.0 - ADAM_B2) * _jnp.square(g)
    m_hat = m / (1.0 - ADAM_B1 ** ADAM_STEP)
    v_hat = v / (1.0 - ADAM_B2 ** ADAM_STEP)
    delta = -ADAM_LR * (m_hat / (_jnp.sqrt(v_hat) + ADAM_EPS) + ADAM_WD * w)
    return delta, m, v


def reference(x, norm_g, w_in, conv_a_w, conv_r_w, conv_r_b, lru_wa, lru_ba, lru_wx, lru_bx, lru_lambda, gmlp_norm_g, gmlp_ws, gmlp_bs, w_out, final_g, loss_target, m_norm_g, m_w_in, m_conv_a_w, m_conv_r_w, m_conv_r_b, m_lru_wa, m_lru_ba, m_lru_wx, m_lru_bx, m_lru_lambda, m_gmlp_norm_g, m_gmlp_ws, m_gmlp_bs, m_w_out, m_final_g, v_norm_g, v_w_in, v_conv_a_w, v_conv_r_w, v_conv_r_b, v_lru_wa, v_lru_ba, v_lru_wx, v_lru_bx, v_lru_lambda, v_gmlp_norm_g, v_gmlp_ws, v_gmlp_bs, v_w_out, v_final_g):
    given = dict(x=x, norm_g=norm_g, w_in=w_in, conv_a_w=conv_a_w, conv_r_w=conv_r_w, conv_r_b=conv_r_b, lru_wa=lru_wa, lru_ba=lru_ba, lru_wx=lru_wx, lru_bx=lru_bx, lru_lambda=lru_lambda, gmlp_norm_g=gmlp_norm_g, gmlp_ws=gmlp_ws, gmlp_bs=gmlp_bs, w_out=w_out, final_g=final_g, loss_target=loss_target, m_norm_g=m_norm_g, m_w_in=m_w_in, m_conv_a_w=m_conv_a_w, m_conv_r_w=m_conv_r_w, m_conv_r_b=m_conv_r_b, m_lru_wa=m_lru_wa, m_lru_ba=m_lru_ba, m_lru_wx=m_lru_wx, m_lru_bx=m_lru_bx, m_lru_lambda=m_lru_lambda, m_gmlp_norm_g=m_gmlp_norm_g, m_gmlp_ws=m_gmlp_ws, m_gmlp_bs=m_gmlp_bs, m_w_out=m_w_out, m_final_g=m_final_g, v_norm_g=v_norm_g, v_w_in=v_w_in, v_conv_a_w=v_conv_a_w, v_conv_r_w=v_conv_r_w, v_conv_r_b=v_conv_r_b, v_lru_wa=v_lru_wa, v_lru_ba=v_lru_ba, v_lru_wx=v_lru_wx, v_lru_bx=v_lru_bx, v_lru_lambda=v_lru_lambda, v_gmlp_norm_g=v_gmlp_norm_g, v_gmlp_ws=v_gmlp_ws, v_gmlp_bs=v_gmlp_bs, v_w_out=v_w_out, v_final_g=v_final_g)
    weights = {n: given[n] for n in TWIN_WEIGHTS}
    shared = {n: given[n] for n in SHARED_INPUTS}
    per_example = {n: given[n] for n in ['x']}
    grad_fn = _jax.value_and_grad(_loss, argnums=(0, 1))

    def one_microbatch(ex, loss_target):
        ex = dict(ex)
        diff = ex.pop(TWIN_DIFF_INPUT)
        return grad_fn(weights, diff, {**shared, **ex}, loss_target)

    if N_MICROBATCH == 1:
        loss, (grad_w, grad_x) = one_microbatch(per_example, given["loss_target"])
    else:
        def body(carry, xs):
            loss_sum, grad_sum = carry
            l_k, (gw_k, gx_k) = one_microbatch(xs[0], xs[1])
            with _jax.named_scope("update"):
                return (loss_sum + l_k, _jax.tree.map(_jnp.add, grad_sum, gw_k)), gx_k

        init = (_jnp.zeros((), _jnp.float32), _jax.tree.map(_jnp.zeros_like, weights))
        (loss, grad_w), grad_x = _jax.lax.scan(body, init, (per_example, given["loss_target"]))
    with _jax.named_scope("update"):
        delta_w, new_m, new_v = {}, {}, {}
        for n in TWIN_WEIGHTS:
            delta_w[n], new_m[n], new_v[n] = _adamw(weights[n], grad_w[n], given["m_" + n], given["v_" + n])
    return (loss, grad_x, *[grad_w[n] for n in TWIN_WEIGHTS], *[delta_w[n] for n in TWIN_WEIGHTS],
            *[new_m[n] for n in TWIN_WEIGHTS], *[new_v[n] for n in TWIN_WEIGHTS])
```

```python
import functools
import math

import jax
import jax.numpy as jnp
from jax import lax
from jax.experimental import pallas as pl
from jax.experimental.pallas import tpu as pltpu

F32 = jnp.float32
MXU_DTYPE = jnp.bfloat16
MESH = pl.DeviceIdType.MESH

N_DEV = 8
GROUP_W = 256
N_HEADS = 4
HEAD_DIM = 64
N_CHUNKS = 13
N_ABC = 9
GMLP_CHUNK = 128
ATTN_BLOCK = 128
ATTN_DILATIONS = (1, 4, 16)
NORM_EPS = 1e-6
RG_C = 8.0
SUBLANES = 8
LANES = 128
VMEM_LIMIT = 56 * 1024 * 1024

ADAM_LR = 0.001
ADAM_B1 = 0.9
ADAM_B2 = 0.999
ADAM_EPS = 1e-08
ADAM_WD = 0.01
ADAM_STEP = 10

TM_MIX = 256
TM_MM = 512


def _params(sem, vmem=VMEM_LIMIT):
    return pltpu.CompilerParams(dimension_semantics=sem, vmem_limit_bytes=vmem)


def _mm(a, b):
    return jnp.dot(a.astype(MXU_DTYPE), b.astype(MXU_DTYPE), preferred_element_type=F32)


def _mm_tn(a, b):
    return lax.dot_general(a.astype(MXU_DTYPE), b.astype(MXU_DTYPE), (((0,), (0,)), ((), ())),
                           preferred_element_type=F32)


def _mm_nt(a, b):
    return lax.dot_general(a.astype(MXU_DTYPE), b.astype(MXU_DTYPE), (((1,), (1,)), ((), ())),
                           preferred_element_type=F32)


def _sigmoid(x):
    return 1.0 / (1.0 + jnp.exp(-x))


def _silu_and_grad(x):
    s = _sigmoid(x)
    return x * s, s * (1.0 + x * (1.0 - s))


_GELU_K = math.sqrt(2.0 / math.pi)
_GELU_C = 0.044715


def _gelu_and_grad(x):
    x2 = x * x
    t = jnp.tanh(_GELU_K * (x + _GELU_C * x * x2))
    val = 0.5 * x * (1.0 + t)
    grad = 0.5 * (1.0 + t) + 0.5 * x * (1.0 - t * t) * (_GELU_K * (1.0 + 3.0 * _GELU_C * x2))
    return val, grad


def _gelu(x):
    return 0.5 * x * (1.0 + jnp.tanh(_GELU_K * (x + _GELU_C * x * x * x)))


def _expm1_nonpos(u):
    poly = u * (1.0 + u / 2.0 * (1.0 + u / 3.0 * (1.0 + u / 4.0 * (1.0 + u / 5.0 * (1.0 + u / 6.0 * (
        1.0 + u / 7.0 * (1.0 + u / 8.0)))))))
    return jnp.where(u > -0.25, poly, jnp.exp(u) - 1.0)


def _softplus(x):
    return jnp.maximum(x, 0.0) + jnp.log(1.0 + jnp.exp(-jnp.abs(x)))


def _shift_down(t, halo, k):
    rolled = pltpu.roll(t, k, 0)
    hr = pltpu.roll(halo, k, 0)
    row = lax.broadcasted_iota(jnp.int32, halo.shape, 0)
    first = jnp.where(row < k, hr, rolled[0:SUBLANES])
    return jnp.concatenate([first, rolled[SUBLANES:]], axis=0)


def _shift_up(t, nxt, k):
    tm = t.shape[0]
    rolled = pltpu.roll(t, tm - k, 0)
    nr = pltpu.roll(nxt, SUBLANES - k, 0)
    row = lax.broadcasted_iota(jnp.int32, nxt.shape, 0)
    last = jnp.where(row >= SUBLANES - k, nr, rolled[tm - SUBLANES:tm])
    return jnp.concatenate([rolled[:tm - SUBLANES], last], axis=0)


def _scan_fwd(a, b):
    tm = a.shape[0]
    row = lax.broadcasted_iota(jnp.int32, a.shape, 0)
    s = 1
    while s < tm:
        a_s = pltpu.roll(a, s, 0)
        b_s = pltpu.roll(b, s, 0)
        m = row >= s
        b = jnp.where(m, a * b_s + b, b)
        a = jnp.where(m, a * a_s, a)
        s *= 2
    return a, b


def _scan_rev(a, g):
    tm = a.shape[0]
    row = lax.broadcasted_iota(jnp.int32, a.shape, 0)
    s = 1
    while s < tm:
        a_s = pltpu.roll(a, tm - s, 0)
        g_s = pltpu.roll(g, tm - s, 0)
        m = row < tm - s
        g = jnp.where(m, g + a * g_s, g)
        a = jnp.where(m, a * a_s, a)
        s *= 2
    return g


def _head_masks(shape):
    lane = lax.broadcasted_iota(jnp.int32, shape, 1)
    return [(lane >= h * HEAD_DIM) & (lane < (h + 1) * HEAD_DIM) for h in range(N_HEADS)]


def _colsum(v):
    return jnp.sum(v, axis=0, keepdims=True)


def _norm_inproj(x, g, w, name):
    S, D = x.shape
    N = w.shape[1]
    tm = TM_MM

    def body(x_ref, g_ref, w_ref, z_ref):
        xv = x_ref[...]
        r = lax.rsqrt(jnp.mean(xv * xv, axis=-1, keepdims=True) + NORM_EPS)
        h = (xv * r) * g_ref[...]
        z_ref[...] = jnp.dot(h.astype(MXU_DTYPE), w_ref[...], preferred_element_type=F32)

    return pl.pallas_call(
        body, name=name, grid=(S // tm,),
        in_specs=[pl.BlockSpec((tm, D), lambda i: (i, 0)),
                  pl.BlockSpec((1, D), lambda i: (0, 0)),
                  pl.BlockSpec((D, N), lambda i: (0, 0))],
        out_specs=pl.BlockSpec((tm, N), lambda i: (i, 0)),
        out_shape=jax.ShapeDtypeStruct((S, N), F32),
        compiler_params=_params(("parallel",)),
    )(x, g, w)


def _conv_a(z_of, halo_of, w_ref):
    p = z_of(2) * z_of(0)
    p_h = halo_of(2) * halo_of(0)
    cv = w_ref[2:3, :] * p + w_ref[1:2, :] * _shift_down(p, p_h, 1) + w_ref[0:1, :] * _shift_down(p, p_h, 2)
    return p, p_h, cv


def _lru_gates(z_of, halo_of, wr_ref, vec_ref, wa_ref, wx_ref):
    rx = z_of(4)
    rx_h = halo_of(4)
    sh = [rx, _shift_down(rx, rx_h, 1), _shift_down(rx, rx_h, 2), _shift_down(rx, rx_h, 3)]
    xc = (wr_ref[3:4, :] * sh[0] + wr_ref[2:3, :] * sh[1] + wr_ref[1:2, :] * sh[2]
          + wr_ref[0:1, :] * sh[3] + vec_ref[0:1, :])
    ga = _sigmoid(jnp.dot(xc.astype(MXU_DTYPE), wa_ref[...], preferred_element_type=F32) + vec_ref[1:2, :])
    gi = _sigmoid(jnp.dot(xc.astype(MXU_DTYPE), wx_ref[...], preferred_element_type=F32) + vec_ref[2:3, :])
    sp = _softplus(-vec_ref[3:4, :])
    log_a = (-RG_C * ga) * sp
    a = jnp.exp(log_a)
    mult = jnp.sqrt(-_expm1_nonpos(2.0 * log_a))
    return xc, sh, ga, gi, a, mult, sp


def _gmlp_fwd(z_of, vec_ref, ws_ref, bs_ref, tm):
    u = _gelu(z_of(6))
    gv = _gelu(z_of(7))
    rr = lax.rsqrt(jnp.mean(gv * gv, axis=-1, keepdims=True) + NORM_EPS)
    vn = (gv * rr) * vec_ref[4:5, :]
    masks = _head_masks((GMLP_CHUNK, GROUP_W))
    parts = []
    for c in range(tm // GMLP_CHUNK):
        vc = vn[c * GMLP_CHUNK:(c + 1) * GMLP_CHUNK].astype(MXU_DTYPE)
        acc = bs_ref[...]
        for h in range(N_HEADS):
            acc = acc + jnp.where(masks[h], jnp.dot(ws_ref[h], vc, preferred_element_type=F32), 0.0)
        parts.append(acc)
    return u, gv, rr, vn, jnp.concatenate(parts, axis=0)


def _mix_specs(tm, S, order):
    const2 = lambda shape: pl.BlockSpec(shape, lambda i: (0, 0))
    return [const2((SUBLANES, GROUP_W)), const2((SUBLANES, GROUP_W)), const2((SUBLANES, GROUP_W)),
            const2((GROUP_W, GROUP_W)), const2((GROUP_W, GROUP_W)),
            pl.BlockSpec((N_HEADS, GMLP_CHUNK, GMLP_CHUNK), lambda i: (0, 0, 0)),
            const2((GMLP_CHUNK, GROUP_W))]


def _mix_fwd(z, mp, name):
    S = z.shape[0]
    tm = TM_MIX
    hb = tm // SUBLANES
    wcols = N_ABC * GROUP_W

    def body(z_ref, zh_ref, wA_ref, wR_ref, vec_ref, wa_ref, wx_ref, ws_ref, bs_ref, y_ref, h_ref, carry_ref):
        i = pl.program_id(0)

        @pl.when(i == 0)
        def _():
            carry_ref[...] = jnp.zeros_like(carry_ref)

        not_first = i > 0
        z_of = lambda c: z_ref[:, c * GROUP_W:(c + 1) * GROUP_W]
        halo_of = lambda c: jnp.where(not_first, zh_ref[:, c * GROUP_W:(c + 1) * GROUP_W], 0.0)

        _, _, cv = _conv_a(z_of, halo_of, wA_ref)
        y_ref[:, 0:GROUP_W] = z_of(1) * cv * _silu_and_grad(z_of(3))[0]

        xc, _, _, gi, a, mult, _ = _lru_gates(z_of, halo_of, wR_ref, vec_ref, wa_ref, wx_ref)
        b = mult * (gi * xc)
        acum, bcum = _scan_fwd(a, b)
        h = bcum + acum * carry_ref[SUBLANES - 1:SUBLANES, :]
        h_ref[...] = h
        carry_ref[...] = h[tm - SUBLANES:tm]
        y_ref[:, GROUP_W:2 * GROUP_W] = h * _silu_and_grad(z_of(5))[0]

        u, _, _, _, sp = _gmlp_fwd(z_of, vec_ref, ws_ref, bs_ref, tm)
        y_ref[:, 2 * GROUP_W:3 * GROUP_W] = u * sp * _silu_and_grad(z_of(8))[0]

    return pl.pallas_call(
        body, name=name, grid=(S // tm,),
        in_specs=[pl.BlockSpec((tm, wcols), lambda i: (i, 0)),
                  pl.BlockSpec((SUBLANES, wcols), lambda i: (jnp.maximum(i * hb - 1, 0), 0))]
                 + _mix_specs(tm, S, "fwd"),
        out_specs=[pl.BlockSpec((tm, 3 * GROUP_W), lambda i: (i, 0)),
                   pl.BlockSpec((tm, GROUP_W), lambda i: (i, 0))],
        out_shape=[jax.ShapeDtypeStruct((S, 3 * GROUP_W), F32), jax.ShapeDtypeStruct((S, GROUP_W), F32)],
        scratch_shapes=[pltpu.VMEM((SUBLANES, GROUP_W), F32)],
        compiler_params=_params(("arbitrary",)),
    )(z, z, mp["wA"], mp["wR"], mp["vec"], mp["wa"], mp["wx"], mp["ws"], mp["bs"])


def _attn_bias_and_mask(dil, which):
    qi = lax.broadcasted_iota(jnp.int32, (ATTN_BLOCK, ATTN_BLOCK), 0)
    ki = lax.broadcasted_iota(jnp.int32, (ATTN_BLOCK, ATTN_BLOCK), 1)
    if which == "cur":
        delta = qi - ki
    else:
        delta = qi + ATTN_BLOCK - ki
    valid = (delta >= 0) & (delta <= ATTN_BLOCK)
    dist = (delta * dil).astype(F32)
    return dist, valid


_NEG = -1e30


def _slope(h):
    return 2.0 ** (-8.0 * (h + 1) / N_HEADS)


def _attn_fwd(z, dil, name):
    S = z.shape[0]
    rows = S // dil
    nb = rows // ATTN_BLOCK
    zv = z.reshape(rows, dil * N_CHUNKS * GROUP_W)
    scale = 1.0 / math.sqrt(HEAD_DIM)

    def body(q_ref, kc_ref, kp_ref, vc_ref, vp_ref, o_ref, l_ref):
        n = pl.program_id(1)
        has_prev = n > 0
        q = q_ref[...]
        kc = kc_ref[...].astype(MXU_DTYPE)
        kp = kp_ref[...].astype(MXU_DTYPE)
        vc = vc_ref[...].astype(MXU_DTYPE)
        vp = vp_ref[...].astype(MXU_DTYPE)
        masks = _head_masks((ATTN_BLOCK, GROUP_W))
        dist_c, valid_c = _attn_bias_and_mask(dil, "cur")
        dist_p, valid_p = _attn_bias_and_mask(dil, "prev")
        valid_p = valid_p & has_prev
        o_acc = jnp.zeros((ATTN_BLOCK, GROUP_W), F32)
        l_acc = jnp.zeros((ATTN_BLOCK, GROUP_W), F32)
        for h in range(N_HEADS):
            qm = jnp.where(masks[h], q, 0.0).astype(MXU_DTYPE)
            s_c = jnp.where(valid_c, _mm_nt(qm, kc) * scale - _slope(h) * dist_c, _NEG)
            s_p = jnp.where(valid_p, _mm_nt(qm, kp) * scale - _slope(h) * dist_p, _NEG)
            m = jnp.maximum(jnp.max(s_c, axis=-1, keepdims=True), jnp.max(s_p, axis=-1, keepdims=True))
            p_c = jnp.exp(s_c - m)
            p_p = jnp.exp(s_p - m)
            l = jnp.sum(p_c, axis=-1, keepdims=True) + jnp.sum(p_p, axis=-1, keepdims=True)
            o = (jnp.dot(p_c.astype(MXU_DTYPE), vc, preferred_element_type=F32)
                 + jnp.dot(p_p.astype(MXU_DTYPE), vp, preferred_element_type=F32)) / l
            o_acc = jnp.where(masks[h], o, o_acc)
            l_acc = jnp.where(masks[h], m + jnp.log(l), l_acc)
        o_ref[...] = o_acc
        l_ref[...] = l_acc

    blk = (ATTN_BLOCK, GROUP_W)
    cur = lambda c: pl.BlockSpec(blk, lambda r, n: (n, r * N_CHUNKS + c))
    prev = lambda c: pl.BlockSpec(blk, lambda r, n: (jnp.maximum(n - 1, 0), r * N_CHUNKS + c))
    out = pl.BlockSpec(blk, lambda r, n: (n, r))
    o, l = pl.pallas_call(
        body, name=name, grid=(dil, nb),
        in_specs=[cur(9), cur(10), prev(10), cur(11), prev(11)],
        out_specs=[out, out],
        out_shape=[jax.ShapeDtypeStruct((rows, dil * GROUP_W), F32)] * 2,
        compiler_params=_params(("parallel", "parallel")),
    )(zv, zv, zv, zv, zv)
    return o.reshape(S, GROUP_W), l.reshape(S, GROUP_W)


def _outproj(x, z, y_abc, attn, w_out, name):
    S, D = x.shape
    tm = TM_MM
    n_abc = 3 * GROUP_W

    def body(x_ref, g_ref, yabc_ref, o1, l1, o2, l2, o3, l3, w_ref, xn_ref, y_ref, o_ref, lse_ref):
        la, lb, lc = l1[...], l2[...], l3[...]
        mx = jnp.maximum(jnp.maximum(la, lb), lc)
        ea, eb, ec = jnp.exp(la - mx), jnp.exp(lb - mx), jnp.exp(lc - mx)
        den = ea + eb + ec
        o = (ea * o1[...] + eb * o2[...] + ec * o3[...]) / den
        o_ref[...] = o
        lse_ref[...] = mx + jnp.log(den)
        y_d = o * _silu_and_grad(g_ref[...])[0]
        y_ref[:, 0:n_abc] = yabc_ref[...].astype(MXU_DTYPE)
        y_ref[:, n_abc:] = y_d.astype(MXU_DTYPE)
        xn_ref[...] = x_ref[...] + jnp.dot(y_ref[...], w_ref[...], preferred_element_type=F32)

    row = lambda w: pl.BlockSpec((tm, w), lambda i: (i, 0))
    (o1, l1), (o2, l2), (o3, l3) = attn
    return pl.pallas_call(
        body, name=name, grid=(S // tm,),
        in_specs=[row(D), pl.BlockSpec((tm, GROUP_W), lambda i: (i, N_CHUNKS - 1)), row(n_abc)]
                 + [row(GROUP_W)] * 6 + [pl.BlockSpec(w_out.shape, lambda i: (0, 0))],
        out_specs=[row(D), row(4 * GROUP_W), row(GROUP_W), row(GROUP_W)],
        out_shape=[jax.ShapeDtypeStruct((S, D), F32), jax.ShapeDtypeStruct((S, 4 * GROUP_W), MXU_DTYPE),
                   jax.ShapeDtypeStruct((S, GROUP_W), F32), jax.ShapeDtypeStruct((S, GROUP_W), F32)],
        compiler_params=_params(("parallel",)),
    )(x, z, y_abc, o1, l1, o2, l2, o3, l3, w_out)


def _loss_head(x, g, target, name):
    S, D = x.shape
    tm = TM_MM

    def body(x_ref, g_ref, t_ref, dx_ref, loss_ref, dg_ref):
        i = pl.program_id(0)

        @pl.when(i == 0)
        def _():
            loss_ref[...] = jnp.zeros_like(loss_ref)
            dg_ref[...] = jnp.zeros_like(dg_ref)

        xv = x_ref[...]
        r = lax.rsqrt(jnp.mean(xv * xv, axis=-1, keepdims=True) + NORM_EPS)
        xn = xv * r
        err = xn * g_ref[...] - t_ref[...]
        per_tok = jnp.mean(err * err, axis=-1, keepdims=True)
        loss_ref[...] += 0.5 * jnp.sum(per_tok, axis=0, keepdims=True)
        dout = err * (1.0 / D)
        dg_ref[...] += _colsum(dout * xn)
        dxn = dout * g_ref[...]
        dx_ref[...] = r * (dxn - xn * jnp.mean(dxn * xn, axis=-1, keepdims=True))

    row = pl.BlockSpec((tm, D), lambda i: (i, 0))
    return pl.pallas_call(
        body, name=name, grid=(S // tm,),
        in_specs=[row, pl.BlockSpec((1, D), lambda i: (0, 0)), row],
        out_specs=[row, pl.BlockSpec((1, LANES), lambda i: (0, 0)), pl.BlockSpec((1, D), lambda i: (0, 0))],
        out_shape=[jax.ShapeDtypeStruct((S, D), F32), jax.ShapeDtypeStruct((1, LANES), F32),
                   jax.ShapeDtypeStruct((1, D), F32)],
        compiler_params=_params(("arbitrary",)),
    )(x, g, target)


def _outproj_bwd(dx, y, w_out, name):
    S, D = dx.shape
    E = y.shape[1]
    tm = TM_MM

    def body(dx_ref, y_ref, w_ref, dy_ref, dw_ref):
        i = pl.program_id(0)

        @pl.when(i == 0)
        def _():
            dw_ref[...] = jnp.zeros_like(dw_ref)

        dxb = dx_ref[...].astype(MXU_DTYPE)
        dy_ref[...] = _mm_nt(dxb, w_ref[...])
        dw_ref[...] += _mm_tn(y_ref[...], dxb)

    return pl.pallas_call(
        body, name=name, grid=(S // tm,),
        in_specs=[pl.BlockSpec((tm, D), lambda i: (i, 0)), pl.BlockSpec((tm, E), lambda i: (i, 0)),
                  pl.BlockSpec((E, D), lambda i: (0, 0))],
        out_specs=[pl.BlockSpec((tm, E), lambda i: (i, 0)), pl.BlockSpec((E, D), lambda i: (0, 0))],
        out_shape=[jax.ShapeDtypeStruct((S, E), F32), jax.ShapeDtypeStruct((E, D), F32)],
        compiler_params=_params(("arbitrary",)),
    )(dx, y, w_out)


def _mix_bwd(z, dy, hs, o, mp, name):
    S = z.shape[0]
    tm = TM_MIX
    hb = tm // SUBLANES
    nT = S // tm
    last_blk = S // SUBLANES - 1
    wcols = N_ABC * GROUP_W

    def body(z_ref, zh_ref, zn_ref, zg_ref, dy_ref, dyn_ref, h_ref, hh_ref, o_ref,
             wA_ref, wR_ref, vec_ref, wa_ref, wx_ref, ws_ref, bs_ref,
             dz_ref, dzg_ref, do_ref, dl_ref, dwA_ref, dwR_ref, dvec_ref, dwa_ref, dwx_ref, dws_ref, dbs_ref,
             hcarry_ref, xcarry_ref, bsacc_ref):
        i = pl.program_id(0)
        ti = nT - 1 - i

        @pl.when(i == 0)
        def _():
            hcarry_ref[...] = jnp.zeros_like(hcarry_ref)
            xcarry_ref[...] = jnp.zeros_like(xcarry_ref)
            bsacc_ref[...] = jnp.zeros_like(bsacc_ref)
            dwA_ref[...] = jnp.zeros_like(dwA_ref)
            dwR_ref[...] = jnp.zeros_like(dwR_ref)
            dvec_ref[...] = jnp.zeros_like(dvec_ref)
            dwa_ref[...] = jnp.zeros_like(dwa_ref)
            dwx_ref[...] = jnp.zeros_like(dwx_ref)
            dws_ref[...] = jnp.zeros_like(dws_ref)
            dbs_ref[...] = jnp.zeros_like(dbs_ref)

        has_prev = ti > 0
        has_next = i > 0
        col = lambda c: slice(c * GROUP_W, (c + 1) * GROUP_W)
        z_of = lambda c: z_ref[:, col(c)]
        halo_of = lambda c: jnp.where(has_prev, zh_ref[:, col(c)], 0.0)
        next_of = lambda c: zn_ref[:, col(c)]

        p, p_h, cv = _conv_a(z_of, halo_of, wA_ref)
        sg, dsg = _silu_and_grad(z_of(3))
        a_b = z_of(1)
        dya = dy_ref[:, col(0)]
        dcv = dya * a_b * sg
        dcv_n = jnp.where(has_next, dyn_ref[...] * next_of(1) * _silu_and_grad(next_of(3))[0], 0.0)
        dp = (wA_ref[2:3, :] * dcv + wA_ref[1:2, :] * _shift_up(dcv, dcv_n, 1)
              + wA_ref[0:1, :] * _shift_up(dcv, dcv_n, 2))
        dwA_ref[2:3, :] += _colsum(dcv * p)
        dwA_ref[1:2, :] += _colsum(dcv * _shift_down(p, p_h, 1))
        dwA_ref[0:1, :] += _colsum(dcv * _shift_down(p, p_h, 2))
        dz_ref[:, col(0)] = dp * z_of(2)
        dz_ref[:, col(1)] = dya * cv * sg
        dz_ref[:, col(2)] = dp * z_of(0)
        dz_ref[:, col(3)] = dya * a_b * cv * dsg

        xc, sh, ga, gi, a, mult, sp = _lru_gates(z_of, halo_of, wR_ref, vec_ref, wa_ref, wx_ref)
        h = h_ref[...]
        h_prev = _shift_down(h, jnp.where(has_prev, hh_ref[...], 0.0), 1)
        sgr, dsgr = _silu_and_grad(z_of(5))
        dyb = dy_ref[:, col(1)]
        dz_ref[:, col(5)] = dyb * h * dsgr
        row = lax.broadcasted_iota(jnp.int32, (tm, GROUP_W), 0)
        g_in = dyb * sgr + jnp.where(row == tm - 1, hcarry_ref[0:1, :], 0.0)
        a_up = _shift_up(a, jnp.zeros((SUBLANES, GROUP_W), F32), 1)
        dH = _scan_rev(a_up, g_in)
        hcarry_ref[...] = (a * dH)[0:SUBLANES]
        da = dH * h_prev
        gx = gi * xc
        dmult = dH * gx
        dgi = dH * mult * xc
        dxc = dH * mult * gi
        dlog_a = da * a - dmult * (a * a) / mult
        dga = dlog_a * (-RG_C * sp)
        dlam_row = _colsum(dlog_a * (-RG_C * ga)) * (-_sigmoid(-vec_ref[3:4, :]))
        dpre_a = dga * ga * (1.0 - ga)
        dpre_i = dgi * gi * (1.0 - gi)
        dwa_ref[...] += _mm_tn(xc, dpre_a)
        dwx_ref[...] += _mm_tn(xc, dpre_i)
        dxc = dxc + _mm_nt(dpre_a, wa_ref[...]) + _mm_nt(dpre_i, wx_ref[...])
        dvec_ref[0:1, :] += _colsum(dxc)
        dvec_ref[1:2, :] += _colsum(dpre_a)
        dvec_ref[2:3, :] += _colsum(dpre_i)
        dvec_ref[3:4, :] += dlam_row
        for k in range(4):
            dwR_ref[k:k + 1, :] += _colsum(dxc * sh[3 - k])
        dxc_n = xcarry_ref[...]
        dz_ref[:, col(4)] = (wR_ref[3:4, :] * dxc + wR_ref[2:3, :] * _shift_up(dxc, dxc_n, 1)
                             + wR_ref[1:2, :] * _shift_up(dxc, dxc_n, 2)
                             + wR_ref[0:1, :] * _shift_up(dxc, dxc_n, 3))
        xcarry_ref[...] = dxc[0:SUBLANES]

        c_u, c_v = z_of(6), z_of(7)
        u, du_dx = _gelu_and_grad(c_u)
        gv, dgv_dx = _gelu_and_grad(c_v)
        rr = lax.rsqrt(jnp.mean(gv * gv, axis=-1, keepdims=True) + NORM_EPS)
        xhat = gv * rr
        g_c = vec_ref[4:5, :]
        vn = xhat * g_c
        masks = _head_masks((GMLP_CHUNK, GROUP_W))
        tri_r = lax.broadcasted_iota(jnp.int32, (GMLP_CHUNK, GMLP_CHUNK), 0)
        tri_c = lax.broadcasted_iota(jnp.int32, (GMLP_CHUNK, GMLP_CHUNK), 1)
        tril = tri_r >= tri_c
        sgc, dsgc = _silu_and_grad(z_of(8))
        dyc = dy_ref[:, col(2)]
        dsp_full = dyc * u * sgc
        sp_parts, dvn_parts = [], []
        for c in range(tm // GMLP_CHUNK):
            rs = slice(c * GMLP_CHUNK, (c + 1) * GMLP_CHUNK)
            vc = vn[rs].astype(MXU_DTYPE)
            dsp_c = dsp_full[rs]
            bsacc_ref[...] += dsp_c
            acc = bs_ref[...]
            dvn_c = jnp.zeros((GMLP_CHUNK, GROUP_W), F32)
            for h in range(N_HEADS):
                w_h = ws_ref[h]
                acc = acc + jnp.where(masks[h], jnp.dot(w_h, vc, preferred_element_type=F32), 0.0)
                dsp_h = jnp.where(masks[h], dsp_c, 0.0).astype(MXU_DTYPE)
                dvn_c = dvn_c + _mm_tn(w_h, dsp_h)
                dws_ref[h] += jnp.where(tril, _mm_nt(dsp_h, vc), 0.0)
            sp_parts.append(acc)
            dvn_parts.append(dvn_c)
        spv = jnp.concatenate(sp_parts, axis=0)
        dvn = jnp.concatenate(dvn_parts, axis=0)
        dz_ref[:, col(6)] = dyc * spv * sgc * du_dx
        dz_ref[:, col(8)] = dyc * u * spv * dsgc
        dvec_ref[4:5, :] += _colsum(dvn * xhat)
        dgvn = dvn * g_c
        dgv = rr * (dgvn - xhat * jnp.mean(dgvn * xhat, axis=-1, keepdims=True))
        dz_ref[:, col(7)] = dgv * dgv_dx

        sgd, dsgd = _silu_and_grad(zg_ref[...])
        dyd = dy_ref[:, col(3)]
        ov = o_ref[...]
        do = dyd * sgd
        do_ref[...] = do
        dzg_ref[...] = dyd * ov * dsgd
        prod = do * ov
        tmasks = _head_masks((tm, GROUP_W))
        dl = jnp.zeros((tm, GROUP_W), F32)
        for h in range(N_HEADS):
            dl = jnp.where(tmasks[h], jnp.sum(jnp.where(tmasks[h], prod, 0.0), axis=-1, keepdims=True), dl)
        dl_ref[...] = dl

        @pl.when(i == nT - 1)
        def _():
            acc = bsacc_ref[...]
            lane = lax.broadcasted_iota(jnp.int32, (GMLP_CHUNK, LANES), 1)
            out = jnp.zeros((GMLP_CHUNK, LANES), F32)
            for h in range(N_HEADS):
                out = jnp.where(lane == h, jnp.sum(jnp.where(masks[h], acc, 0.0), axis=-1, keepdims=True), out)
            dbs_ref[...] = out

    rev = lambda w, c=0: pl.BlockSpec((tm, w), lambda i: (nT - 1 - i, c))
    prev8 = lambda w: pl.BlockSpec((SUBLANES, w), lambda i: (jnp.maximum((nT - 1 - i) * hb - 1, 0), 0))
    next8 = lambda w: pl.BlockSpec((SUBLANES, w), lambda i: (jnp.minimum((nT - i) * hb, last_blk), 0))
    const2 = lambda shape: pl.BlockSpec(shape, lambda i: (0, 0))
    small = (SUBLANES, GROUP_W)
    sq = (GROUP_W, GROUP_W)
    ws_shape = (N_HEADS, GMLP_CHUNK, GMLP_CHUNK)
    return pl.pallas_call(
        body, name=name, grid=(nT,),
        in_specs=[rev(wcols), prev8(wcols), next8(wcols), rev(GROUP_W, N_CHUNKS - 1),
                  rev(4 * GROUP_W), next8(GROUP_W), rev(GROUP_W), prev8(GROUP_W), rev(GROUP_W)]
                 + _mix_specs(tm, S, "bwd"),
        out_specs=[rev(wcols), rev(GROUP_W), rev(GROUP_W), rev(GROUP_W),
                   const2(small), const2(small), const2(small), const2(sq), const2(sq),
                   pl.BlockSpec(ws_shape, lambda i: (0, 0, 0)), const2((GMLP_CHUNK, LANES))],
        out_shape=[jax.ShapeDtypeStruct((S, wcols), F32)] + [jax.ShapeDtypeStruct((S, GROUP_W), F32)] * 3
                  + [jax.ShapeDtypeStruct(small, F32)] * 3 + [jax.ShapeDtypeStruct(sq, F32)] * 2
                  + [jax.ShapeDtypeStruct(ws_shape, F32), jax.ShapeDtypeStruct((GMLP_CHUNK, LANES), F32)],
        scratch_shapes=[pltpu.VMEM(small, F32), pltpu.VMEM(small, F32), pltpu.VMEM((GMLP_CHUNK, GROUP_W), F32)],
        compiler_params=_params(("arbitrary",)),
    )(z, z, z, z, dy, dy, hs, hs, o, mp["wA"], mp["wR"], mp["vec"], mp["wa"], mp["wx"], mp["ws"], mp["bs"])


def _attn_bwd(z, do, lse, delta, dil, name):
    S = z.shape[0]
    rows = S // dil
    nb = rows // ATTN_BLOCK
    zv = z.reshape(rows, dil * N_CHUNKS * GROUP_W)
    view = lambda t: t.reshape(rows, dil * GROUP_W)
    scale = 1.0 / math.sqrt(HEAD_DIM)

    def body(qc_ref, qn_ref, k_ref, v_ref, doc_ref, don_ref, lc_ref, ln_ref, dc_ref, dn_ref,
             dq_ref, dk_ref, dv_ref, carry_ref):
        n = pl.program_id(1)

        @pl.when(n == 0)
        def _():
            carry_ref[...] = jnp.zeros_like(carry_ref)

        has_next = n < nb - 1
        kb = k_ref[...].astype(MXU_DTYPE)
        vb = v_ref[...].astype(MXU_DTYPE)
        masks = _head_masks((ATTN_BLOCK, GROUP_W))
        sides = []
        for which, q_ref, d_ref, l_ref, t_ref in (("cur", qc_ref, doc_ref, lc_ref, dc_ref),
                                                   ("prev", qn_ref, don_ref, ln_ref, dn_ref)):
            dist, valid = _attn_bias_and_mask(dil, which)
            if which == "prev":
                valid = valid & has_next
            sides.append((q_ref[...], d_ref[...], l_ref[...], t_ref[...], dist, valid))
        dq_parts = [jnp.zeros((ATTN_BLOCK, GROUP_W), F32), jnp.zeros((ATTN_BLOCK, GROUP_W), F32)]
        dk = jnp.zeros((ATTN_BLOCK, GROUP_W), F32)
        dv = jnp.zeros((ATTN_BLOCK, GROUP_W), F32)
        for h in range(N_HEADS):
            for side, (q, dov, lv, tv, dist, valid) in enumerate(sides):
                qm = jnp.where(masks[h], q, 0.0).astype(MXU_DTYPE)
                dom = jnp.where(masks[h], dov, 0.0).astype(MXU_DTYPE)
                lse_h = jnp.max(jnp.where(masks[h], lv, _NEG), axis=-1, keepdims=True)
                dl_h = jnp.max(jnp.where(masks[h], tv, _NEG), axis=-1, keepdims=True)
                s = _mm_nt(qm, kb) * scale - _slope(h) * dist
                p = jnp.where(valid, jnp.exp(jnp.where(valid, s, _NEG) - lse_h), 0.0)
                dp = _mm_nt(dom, vb)
                ds = (p * (dp - dl_h) * scale).astype(MXU_DTYPE)
                pb = p.astype(MXU_DTYPE)
                dv = dv + _mm_tn(pb, dom)
                dk = dk + _mm_tn(ds, qm)
                dq_parts[side] = dq_parts[side] + jnp.where(masks[h], jnp.dot(ds, kb, preferred_element_type=F32), 0.0)
        dq_ref[...] = carry_ref[...] + dq_parts[0]
        carry_ref[...] = dq_parts[1]
        dk_ref[...] = dk
        dv_ref[...] = dv

    blk = (ATTN_BLOCK, GROUP_W)
    zcur = lambda c: pl.BlockSpec(blk, lambda r, n: (n, r * N_CHUNKS + c))
    znext = lambda c: pl.BlockSpec(blk, lambda r, n: (jnp.minimum(n + 1, nb - 1), r * N_CHUNKS + c))
    cur = pl.BlockSpec(blk, lambda r, n: (n, r))
    nxt = pl.BlockSpec(blk, lambda r, n: (jnp.minimum(n + 1, nb - 1), r))
    dq, dk, dv = pl.pallas_call(
        body, name=name, grid=(dil, nb),
        in_specs=[zcur(9), znext(9), zcur(10), zcur(11), cur, nxt, cur, nxt, cur, nxt],
        out_specs=[cur, cur, cur],
        out_shape=[jax.ShapeDtypeStruct((rows, dil * GROUP_W), F32)] * 3,
        scratch_shapes=[pltpu.VMEM(blk, F32)],
        compiler_params=_params(("arbitrary", "arbitrary")),
    )(zv, zv, zv, zv, view(do), view(do), view(lse), view(lse), view(delta), view(delta))
    return dq.reshape(S, GROUP_W), dk.reshape(S, GROUP_W), dv.reshape(S, GROUP_W)


def _inproj_bwd(x, g, dxn, dz_abc, dqkv, dz_g, w_in, name):
    S, D = x.shape
    N = w_in.shape[1]
    tm = TM_MM
    n_abc = N_ABC * GROUP_W

    def body(x_ref, g_ref, dxn_ref, dabc_ref, q1, k1, v1, q2, k2, v2, q3, k3, v3, dg_ref, w_ref,
             dx_ref, dz_ref, h_ref, dgn_ref):
        i = pl.program_id(0)

        @pl.when(i == 0)
        def _():
            dgn_ref[...] = jnp.zeros_like(dgn_ref)

        dz_ref[:, 0:n_abc] = dabc_ref[...].astype(MXU_DTYPE)
        for j, parts in enumerate(((q1, q2, q3), (k1, k2, k3), (v1, v2, v3))):
            c0 = n_abc + j * GROUP_W
            dz_ref[:, c0:c0 + GROUP_W] = (parts[0][...] + parts[1][...] + parts[2][...]).astype(MXU_DTYPE)
        dz_ref[:, n_abc + 3 * GROUP_W:] = dg_ref[...].astype(MXU_DTYPE)
        dh = _mm_nt(dz_ref[...], w_ref[...])
        xv = x_ref[...]
        r = lax.rsqrt(jnp.mean(xv * xv, axis=-1, keepdims=True) + NORM_EPS)
        xn = xv * r
        gv = g_ref[...]
        h_ref[...] = (xn * gv).astype(MXU_DTYPE)
        dgn_ref[...] += _colsum(dh * xn)
        dn = dh * gv
        dx_ref[...] = dxn_ref[...] + r * (dn - xn * jnp.mean(dn * xn, axis=-1, keepdims=True))

    row = lambda w: pl.BlockSpec((tm, w), lambda i: (i, 0))
    flat = [t for p in dqkv for t in p]
    return pl.pallas_call(
        body, name=name, grid=(S // tm,),
        in_specs=[row(D), pl.BlockSpec((1, D), lambda i: (0, 0)), row(D), row(n_abc)] + [row(GROUP_W)] * 10
                 + [pl.BlockSpec((D, N), lambda i: (0, 0))],
        out_specs=[row(D), row(N), row(D), pl.BlockSpec((1, D), lambda i: (0, 0))],
        out_shape=[jax.ShapeDtypeStruct((S, D), F32), jax.ShapeDtypeStruct((S, N), MXU_DTYPE),
                   jax.ShapeDtypeStruct((S, D), MXU_DTYPE), jax.ShapeDtypeStruct((1, D), F32)],
        compiler_params=_params(("arbitrary",)),
    )(x, g, dxn, dz_abc, *flat, dz_g, w_in)


def _inproj_wgrad(h, dz, name):
    S, D = h.shape
    N = dz.shape[1]
    tm = TM_MM
    nj = 2
    cw = N // nj

    def body(h_ref, dz_ref, dw_ref):
        @pl.when(pl.program_id(1) == 0)
        def _():
            dw_ref[...] = jnp.zeros_like(dw_ref)

        dw_ref[...] += _mm_tn(h_ref[...], dz_ref[...])

    return pl.pallas_call(
        body, name=name, grid=(nj, S // tm),
        in_specs=[pl.BlockSpec((tm, D), lambda j, i: (i, 0)), pl.BlockSpec((tm, cw), lambda j, i: (i, j))],
        out_specs=pl.BlockSpec((D, cw), lambda j, i: (0, j)),
        out_shape=jax.ShapeDtypeStruct((D, N), F32),
        compiler_params=_params(("parallel", "arbitrary")),
    )(h, dz)


def _my_place():
    return lax.axis_index("x"), lax.axis_index("y"), lax.axis_index("c")


def _peer(x, y, c, k):
    px = 1 - x if k & 4 else x
    py = 1 - y if k & 2 else y
    pc = 1 - c if k & 1 else c
    return (px, py, pc), 4 * px + 2 * py + pc


def _exchange(arrays, gather, name):
    n_t = len(arrays)
    anyspec = pl.BlockSpec(memory_space=pl.ANY)

    def body(*refs):
        ins, outs = refs[:n_t], refs[n_t:2 * n_t]
        send_sems, recv_sems, local_sems = refs[2 * n_t:]
        x, y, c = _my_place()
        me = 4 * x + 2 * y + c
        copies = []
        for t in range(n_t):
            src_mine = ins[t] if gather else ins[t].at[me]
            local = pltpu.make_async_copy(src_mine, outs[t].at[me], local_sems.at[t])
            local.start()
            copies.append(local)
        remote = []
        for t in range(n_t):
            for k in range(1, N_DEV):
                peer, pidx = _peer(x, y, c, k)
                src = ins[t] if gather else ins[t].at[pidx]
                cp = pltpu.make_async_remote_copy(
                    src_ref=src, dst_ref=outs[t].at[me], send_sem=send_sems.at[t, k - 1],
                    recv_sem=recv_sems.at[t, k - 1], device_id=peer, device_id_type=MESH)
                cp.start()
                remote.append((cp, t, k, pidx))
        for cp, t, k, pidx in remote:
            src = ins[t] if gather else ins[t].at[pidx]
            pltpu.make_async_remote_copy(
                src_ref=src, dst_ref=outs[t].at[pidx], send_sem=send_sems.at[t, k - 1],
                recv_sem=recv_sems.at[t, k - 1], device_id=_peer(x, y, c, k)[0], device_id_type=MESH).wait_recv()
        for cp, t, k, pidx in remote:
            cp.wait_send()
        for local in copies:
            local.wait()

    out_shape = [jax.ShapeDtypeStruct(((N_DEV,) + a.shape) if gather else a.shape, a.dtype) for a in arrays]
    return pl.pallas_call(
        body, name=name,
        in_specs=[anyspec] * n_t, out_specs=[anyspec] * n_t, out_shape=out_shape,
        scratch_shapes=[pltpu.SemaphoreType.DMA((n_t, N_DEV - 1)), pltpu.SemaphoreType.DMA((n_t, N_DEV - 1)),
                        pltpu.SemaphoreType.DMA((n_t,))],
        compiler_params=pltpu.CompilerParams(has_side_effects=True),
    )(*arrays)


def _allreduce(buf, name):
    R = buf.shape[0]

    def body(x_ref, out_ref, recv_ref, send_sems, recv_sems):
        x, y, c = _my_place()
        out_ref[...] = x_ref[...]
        for s, k in enumerate((1, 4, 2)):
            peer, _ = _peer(x, y, c, k)
            cp = pltpu.make_async_remote_copy(
                src_ref=out_ref, dst_ref=recv_ref.at[s], send_sem=send_sems.at[s], recv_sem=recv_sems.at[s],
                device_id=peer, device_id_type=MESH)
            cp.start()
            cp.wait()
            out_ref[...] = out_ref[...] + recv_ref[s]

    vm = pl.BlockSpec(memory_space=pltpu.VMEM)
    return pl.pallas_call(
        body, name=name, in_specs=[vm], out_specs=vm,
        out_shape=jax.ShapeDtypeStruct((R, LANES), F32),
        scratch_shapes=[pltpu.VMEM((3, R, LANES), F32), pltpu.SemaphoreType.DMA((3,)), pltpu.SemaphoreType.DMA((3,))],
        compiler_params=pltpu.CompilerParams(has_side_effects=True, vmem_limit_bytes=VMEM_LIMIT),
    )(buf)


def _adamw_math(w, g, m, v):
    m = ADAM_B1 * m + (1.0 - ADAM_B1) * g
    v = ADAM_B2 * v + (1.0 - ADAM_B2) * (g * g)
    m_hat = m / (1.0 - ADAM_B1 ** ADAM_STEP)
    v_hat = v / (1.0 - ADAM_B2 ** ADAM_STEP)
    delta = -ADAM_LR * (m_hat / (jnp.sqrt(v_hat) + ADAM_EPS) + ADAM_WD * w)
    return delta, m, v


def _adamw_summed(parts, w, m, v, tr, name):
    R, C = w.shape

    def body(p_ref, w_ref, m_ref, v_ref, g_ref, d_ref, nm_ref, nv_ref):
        g = p_ref[0]
        for j in range(1, N_DEV):
            g = g + p_ref[j]
        g_ref[...] = g
        d_ref[...], nm_ref[...], nv_ref[...] = _adamw_math(w_ref[...], g, m_ref[...], v_ref[...])

    row = pl.BlockSpec((tr, C), lambda i: (i, 0))
    return pl.pallas_call(
        body, name=name, grid=(R // tr,),
        in_specs=[pl.BlockSpec((N_DEV, tr, C), lambda i: (0, i, 0)), row, row, row],
        out_specs=[row] * 4, out_shape=[jax.ShapeDtypeStruct((R, C), F32)] * 4,
        compiler_params=_params(("parallel",)),
    )(parts, w, m, v)


def _adamw_small(w, g, m, v, name):
    def body(w_ref, g_ref, m_ref, v_ref, d_ref, nm_ref, nv_ref):
        d_ref[...], nm_ref[...], nv_ref[...] = _adamw_math(w_ref[...], g_ref[...], m_ref[...], v_ref[...])

    vm = pl.BlockSpec(memory_space=pltpu.VMEM)
    return pl.pallas_call(
        body, name=name, in_specs=[vm] * 4, out_specs=[vm] * 3,
        out_shape=[jax.ShapeDtypeStruct(w.shape, F32)] * 3,
        compiler_params=pltpu.CompilerParams(vmem_limit_bytes=VMEM_LIMIT),
    )(w, g, m, v)


def _pack(arrays):
    flat = jnp.concatenate([a.reshape(-1) for a in arrays])
    pad = (-flat.shape[0]) % (SUBLANES * LANES)
    return jnp.pad(flat, (0, pad)).reshape(-1, LANES)


def _unpack(buf, like):
    flat = buf.reshape(-1)
    out, off = [], 0
    for a in like:
        out.append(flat[off:off + a.size].reshape(a.shape))
        off += a.size
    return out


def _block_diag(w):
    eye = jnp.eye(N_HEADS, dtype=w.dtype)
    return jnp.einsum('hij,hk->hikj', w, eye).reshape(GROUP_W, GROUP_W)


def _diag_blocks(w):
    return jnp.einsum('hihj->hij', w.reshape(N_HEADS, HEAD_DIM, N_HEADS, HEAD_DIM))


def _pad_rows(a):
    return jnp.pad(a, ((0, SUBLANES - a.shape[0]), (0, 0)))


def _mixer_params(l, conv_a_w, conv_r_w, conv_r_b, lru_wa, lru_ba, lru_wx, lru_bx, lru_lambda, gmlp_norm_g,
                  gmlp_ws, gmlp_bs):
    tril = jnp.tril(jnp.ones((GMLP_CHUNK, GMLP_CHUNK), dtype=bool))
    vec = jnp.stack([conv_r_b[l], lru_ba[l], lru_bx[l], lru_lambda[l], gmlp_norm_g[l]])
    return {
        "wA": _pad_rows(conv_a_w[l]), "wR": _pad_rows(conv_r_w[l]), "vec": _pad_rows(vec),
        "wa": _block_diag(lru_wa[l]).astype(MXU_DTYPE), "wx": _block_diag(lru_wx[l]).astype(MXU_DTYPE),
        "ws": jnp.where(tril[None], gmlp_ws[l], 0.0).astype(MXU_DTYPE),
        "bs": jnp.repeat(jnp.transpose(gmlp_bs[l]), HEAD_DIM, axis=1),
    }


def _local_step(x, loss_target, norm_g, w_in_full, w_out_full, conv_a_w, conv_r_w, conv_r_b, lru_wa, lru_ba,
                lru_wx, lru_bx, lru_lambda, gmlp_norm_g, gmlp_ws, gmlp_bs, final_g):
    depth = norm_g.shape[0]
    D = x.shape[1]
    small = (conv_a_w, conv_r_w, conv_r_b, lru_wa, lru_ba, lru_wx, lru_bx, lru_lambda, gmlp_norm_g, gmlp_ws, gmlp_bs)
    saved = []
    for l in range(depth):
        mp = _mixer_params(l, *small)
        z = _norm_inproj(x, norm_g[l].reshape(1, D), w_in_full[l], f"norm_inproj_{l}")
        y_abc, hs = _mix_fwd(z, mp, f"mix_fwd_{l}")
        attn = [_attn_fwd(z, dil, f"attn_fwd_d{dil}_{l}") for dil in ATTN_DILATIONS]
        x_new, y, o, lse = _outproj(x, z, y_abc, attn, w_out_full[l], f"outproj_{l}")
        saved.append((x, z, hs, y, o, lse, mp))
        x = x_new
    dx, loss, d_final_g = _loss_head(x, final_g.reshape(1, D), loss_target, "loss_head")
    grads = []
    for l in reversed(range(depth)):
        x_l, z, hs, y, o, lse, mp = saved[l]
        dy, dw_out = _outproj_bwd(dx, y, w_out_full[l], f"outproj_bwd_{l}")
        (dz_abc, dz_g, do, delta, dwA, dwR, dvec, dwa, dwx, dws, dbs) = _mix_bwd(z, dy, hs, o, mp, f"mix_bwd_{l}")
        dqkv = [_attn_bwd(z, do, lse, delta, dil, f"attn_bwd_d{dil}_{l}") for dil in ATTN_DILATIONS]
        dx, dz, h, dng = _inproj_bwd(x_l, norm_g[l].reshape(1, D), dx, dz_abc, dqkv, dz_g, w_in_full[l],
                                     f"inproj_bwd_{l}")
        dw_in = _inproj_wgrad(h, dz, f"inproj_wgrad_{l}")
        grads.append({
            "norm_g": dng[0], "w_in": dw_in, "w_out": dw_out,
            "conv_a_w": dwA[:conv_a_w.shape[1]], "conv_r_w": dwR[:conv_r_w.shape[1]],
            "conv_r_b": dvec[0], "lru_ba": dvec[1], "lru_bx": dvec[2], "lru_lambda": dvec[3], "gmlp_norm_g": dvec[4],
            "lru_wa": _diag_blocks(dwa), "lru_wx": _diag_blocks(dwx), "gmlp_ws": dws,
            "gmlp_bs": jnp.transpose(dbs[:, :N_HEADS]),
        })
    grads = grads[::-1]
    stacked = {k: jnp.stack([g[k] for g in grads]) for k in grads[0]}
    stacked["final_g"] = d_final_g[0]
    return loss[0, 0], dx, stacked


SMALL_NAMES = ("norm_g", "conv_a_w", "conv_r_w", "conv_r_b", "lru_wa", "lru_ba", "lru_wx", "lru_bx", "lru_lambda",
               "gmlp_norm_g", "gmlp_ws", "gmlp_bs", "final_g")
WEIGHT_NAMES = ("norm_g", "w_in", "conv_a_w", "conv_r_w", "conv_r_b", "lru_wa", "lru_ba", "lru_wx", "lru_bx",
                "lru_lambda", "gmlp_norm_g", "gmlp_ws", "gmlp_bs", "w_out", "final_g")


def kernel(x, norm_g, w_in, conv_a_w, conv_r_w, conv_r_b, lru_wa, lru_ba, lru_wx, lru_bx, lru_lambda, gmlp_norm_g, gmlp_ws, gmlp_bs, w_out, final_g, loss_target, m_norm_g, m_w_in, m_conv_a_w, m_conv_r_w, m_conv_r_b, m_lru_wa, m_lru_ba, m_lru_wx, m_lru_bx, m_lru_lambda, m_gmlp_norm_g, m_gmlp_ws, m_gmlp_bs, m_w_out, m_final_g, v_norm_g, v_w_in, v_conv_a_w, v_conv_r_w, v_conv_r_b, v_lru_wa, v_lru_ba, v_lru_wx, v_lru_bx, v_lru_lambda, v_gmlp_norm_g, v_gmlp_ws, v_gmlp_bs, v_w_out, v_final_g):
    w = dict(norm_g=norm_g, w_in=w_in, conv_a_w=conv_a_w, conv_r_w=conv_r_w, conv_r_b=conv_r_b, lru_wa=lru_wa,
             lru_ba=lru_ba, lru_wx=lru_wx, lru_bx=lru_bx, lru_lambda=lru_lambda, gmlp_norm_g=gmlp_norm_g,
             gmlp_ws=gmlp_ws, gmlp_bs=gmlp_bs, w_out=w_out, final_g=final_g)
    m = dict(norm_g=m_norm_g, w_in=m_w_in, conv_a_w=m_conv_a_w, conv_r_w=m_conv_r_w, conv_r_b=m_conv_r_b,
             lru_wa=m_lru_wa, lru_ba=m_lru_ba, lru_wx=m_lru_wx, lru_bx=m_lru_bx, lru_lambda=m_lru_lambda,
             gmlp_norm_g=m_gmlp_norm_g, gmlp_ws=m_gmlp_ws, gmlp_bs=m_gmlp_bs, w_out=m_w_out, final_g=m_final_g)
    v = dict(norm_g=v_norm_g, w_in=v_w_in, conv_a_w=v_conv_a_w, conv_r_w=v_conv_r_w, conv_r_b=v_conv_r_b,
             lru_wa=v_lru_wa, lru_ba=v_lru_ba, lru_wx=v_lru_wx, lru_bx=v_lru_bx, lru_lambda=v_lru_lambda,
             gmlp_norm_g=v_gmlp_norm_g, gmlp_ws=v_gmlp_ws, gmlp_bs=v_gmlp_bs, w_out=v_w_out, final_g=v_final_g)
    depth, D, n_loc = w_in.shape
    e_loc = w_out.shape[1]
    cx, cy, cc = _my_place()
    me = 4 * cx + 2 * cy + cc

    g_in, g_out = _exchange([w_in.astype(MXU_DTYPE), w_out.astype(MXU_DTYPE)], True, "gather_weights")
    w_in_full = jnp.transpose(g_in, (1, 2, 0, 3)).reshape(depth, D, N_DEV * n_loc)
    w_out_full = jnp.transpose(g_out, (1, 0, 2, 3)).reshape(depth, N_DEV * e_loc, D)
    c_loc = conv_a_w.shape[2]
    conv_full = [jnp.zeros(a.shape[:2] + (N_DEV * c_loc,), F32) for a in (conv_a_w, conv_r_w)]
    conv_full = [lax.dynamic_update_slice_in_dim(f, a, me * c_loc, axis=2)
                 for f, a in zip(conv_full, (conv_a_w, conv_r_w))]
    conv_a_full, conv_r_full = _unpack(_allreduce(_pack(conv_full), "gather_conv_taps"), conv_full)

    loss, grad_x, g = _local_step(
        x[0], loss_target[0], norm_g, w_in_full, w_out_full, conv_a_full, conv_r_full, conv_r_b, lru_wa, lru_ba,
        lru_wx, lru_bx, lru_lambda, gmlp_norm_g, gmlp_ws, gmlp_bs, final_g)
    loss = lax.psum(loss, ("x", "y", "c"))

    p_in = jnp.transpose(g["w_in"].reshape(depth, D, N_DEV, n_loc), (2, 0, 1, 3)).reshape(N_DEV, depth * D, n_loc)
    p_out = jnp.transpose(g["w_out"].reshape(depth, N_DEV, e_loc, D), (1, 0, 2, 3)).reshape(N_DEV, depth * e_loc, D)
    r_in, r_out = _exchange([p_in, p_out], False, "scatter_wgrads")
    flat_in = lambda a: a.reshape(depth * D, n_loc)
    flat_out = lambda a: a.reshape(depth * e_loc, D)
    big = {}
    big["w_in"] = [a.reshape(w_in.shape) for a in _adamw_summed(
        r_in, flat_in(w_in), flat_in(m_w_in), flat_in(v_w_in), 512, "adamw_w_in")]
    big["w_out"] = [a.reshape(w_out.shape) for a in _adamw_summed(
        r_out, flat_out(w_out), flat_out(m_w_out), flat_out(v_w_out), 128, "adamw_w_out")]

    g_small = [g[k] for k in SMALL_NAMES]
    g_small = _unpack(_allreduce(_pack(g_small), "allreduce_small_grads"), g_small)
    g_small = dict(zip(SMALL_NAMES, g_small))
    for k in ("conv_a_w", "conv_r_w"):
        g_small[k] = lax.dynamic_slice_in_dim(g_small[k], me * c_loc, c_loc, axis=2)
    packs = [_pack([d[k] for k in SMALL_NAMES]) for d in (w, g_small, m, v)]
    res = _adamw_small(*packs, "adamw_small")
    like = [w[k] for k in SMALL_NAMES]
    d_s, m_s, v_s = (dict(zip(SMALL_NAMES, _unpack(r, like))) for r in res)

    grad, delta, new_m, new_v = {}, {}, {}, {}
    for k in WEIGHT_NAMES:
        if k in big:
            grad[k], delta[k], new_m[k], new_v[k] = big[k]
        else:
            grad[k], delta[k], new_m[k], new_v[k] = g_small[k], d_s[k], m_s[k], v_s[k]
    return (loss, grad_x[None], *[grad[k] for k in WEIGHT_NAMES], *[delta[k] for k in WEIGHT_NAMES],
            *[new_m[k] for k in WEIGHT_NAMES], *[new_v[k] for k in WEIGHT_NAMES])
```

```python
import functools
import math

import jax
import jax.numpy as jnp
from jax import lax
from jax.experimental import pallas as pl
from jax.experimental.pallas import tpu as pltpu

F32 = jnp.float32
MXU_DTYPE = jnp.bfloat16
MESH = pl.DeviceIdType.MESH

N_DEV = 8
GROUP_W = 256
N_HEADS = 4
HEAD_DIM = 64
N_CHUNKS = 13
N_ABC = 9
GMLP_CHUNK = 128
ATTN_BLOCK = 128
ATTN_DILATIONS = (1, 4, 16)
NORM_EPS = 1e-6
RG_C = 8.0
SUBLANES = 8
LANES = 128
VMEM_LIMIT = 56 * 1024 * 1024

ADAM_LR = 0.001
ADAM_B1 = 0.9
ADAM_B2 = 0.999
ADAM_EPS = 1e-08
ADAM_WD = 0.01
ADAM_STEP = 10

TM_MIX = 256
TM_MM = 512


def _params(sem, vmem=VMEM_LIMIT):
    return pltpu.CompilerParams(dimension_semantics=sem, vmem_limit_bytes=vmem)


def _mm(a, b):
    return jnp.dot(a.astype(MXU_DTYPE), b.astype(MXU_DTYPE), preferred_element_type=F32)


def _mm_tn(a, b):
    return lax.dot_general(a.astype(MXU_DTYPE), b.astype(MXU_DTYPE), (((0,), (0,)), ((), ())),
                           preferred_element_type=F32)


def _mm_nt(a, b):
    return lax.dot_general(a.astype(MXU_DTYPE), b.astype(MXU_DTYPE), (((1,), (1,)), ((), ())),
                           preferred_element_type=F32)


def _sigmoid(x):
    return 1.0 / (1.0 + jnp.exp(-x))


def _silu_and_grad(x):
    s = _sigmoid(x)
    return x * s, s * (1.0 + x * (1.0 - s))


_GELU_K = math.sqrt(2.0 / math.pi)
_GELU_C = 0.044715


def _gelu_and_grad(x):
    x2 = x * x
    t = jnp.tanh(_GELU_K * (x + _GELU_C * x * x2))
    val = 0.5 * x * (1.0 + t)
    grad = 0.5 * (1.0 + t) + 0.5 * x * (1.0 - t * t) * (_GELU_K * (1.0 + 3.0 * _GELU_C * x2))
    return val, grad


def _gelu(x):
    return 0.5 * x * (1.0 + jnp.tanh(_GELU_K * (x + _GELU_C * x * x * x)))


def _expm1_nonpos(u):
    poly = u * (1.0 + u / 2.0 * (1.0 + u / 3.0 * (1.0 + u / 4.0 * (1.0 + u / 5.0 * (1.0 + u / 6.0 * (
        1.0 + u / 7.0 * (1.0 + u / 8.0)))))))
    return jnp.where(u > -0.25, poly, jnp.exp(u) - 1.0)


def _softplus(x):
    return jnp.maximum(x, 0.0) + jnp.log(1.0 + jnp.exp(-jnp.abs(x)))


def _shift_down(t, halo, k):
    rolled = pltpu.roll(t, k, 0)
    hr = pltpu.roll(halo, k, 0)
    row = lax.broadcasted_iota(jnp.int32, halo.shape, 0)
    first = jnp.where(row < k, hr, rolled[0:SUBLANES])
    return jnp.concatenate([first, rolled[SUBLANES:]], axis=0)


def _shift_up(t, nxt, k):
    tm = t.shape[0]
    rolled = pltpu.roll(t, tm - k, 0)
    nr = pltpu.roll(nxt, SUBLANES - k, 0)
    row = lax.broadcasted_iota(jnp.int32, nxt.shape, 0)
    last = jnp.where(row >= SUBLANES - k, nr, rolled[tm - SUBLANES:tm])
    return jnp.concatenate([rolled[:tm - SUBLANES], last], axis=0)


def _scan_fwd(a, b):
    tm = a.shape[0]
    row = lax.broadcasted_iota(jnp.int32, a.shape, 0)
    s = 1
    while s < tm:
        a_s = pltpu.roll(a, s, 0)
        b_s = pltpu.roll(b, s, 0)
        m = row >= s
        b = jnp.where(m, a * b_s + b, b)
        a = jnp.where(m, a * a_s, a)
        s *= 2
    return a, b


def _scan_rev(a, g):
    tm = a.shape[0]
    row = lax.broadcasted_iota(jnp.int32, a.shape, 0)
    s = 1
    while s < tm:
        a_s = pltpu.roll(a, tm - s, 0)
        g_s = pltpu.roll(g, tm - s, 0)
        m = row < tm - s
        g = jnp.where(m, g + a * g_s, g)
        a = jnp.where(m, a * a_s, a)
        s *= 2
    return g


def _lane_scratch(tm, w):
    return pltpu.VMEM((w // LANES, tm, LANES), F32)


def _put(scr_ref, val):
    for c in range(scr_ref.shape[0]):
        scr_ref[c] = val[:, c * LANES:(c + 1) * LANES].astype(F32)


def _get(scr_ref):
    return jnp.concatenate([scr_ref[c] for c in range(scr_ref.shape[0])], axis=1)


def _deinterleave(src_ref, dst_ref, dil):
    nc, tm, _ = src_ref.shape
    w = nc * LANES
    for r in range(dil):
        for c in range(nc):
            piece = src_ref[pl.ds(c, 1), pl.ds(r, tm // dil, stride=dil), :][0] if dil > 1 else src_ref[c]
            dst_ref[:, r * w + c * LANES:r * w + (c + 1) * LANES] = piece.astype(dst_ref.dtype)


def _interleave(src_ref, dst_ref, dil):
    nc, tm, _ = dst_ref.shape
    w = nc * LANES
    for r in range(dil):
        for c in range(nc):
            dst_ref[pl.ds(c, 1), pl.ds(r, tm // dil, stride=dil), :] = (
                src_ref[:, r * w + c * LANES:r * w + (c + 1) * LANES].astype(F32)[None])


def _dilated_spec(tm, w, dil, index=lambda i: i):
    return pl.BlockSpec((tm // dil, dil * w), lambda i: (index(i), 0))


def _dilated_shape(S, w, dil, dtype):
    return jax.ShapeDtypeStruct((S // dil, dil * w), dtype)


def _head_masks(shape):
    lane = lax.broadcasted_iota(jnp.int32, shape, 1)
    return [(lane >= h * HEAD_DIM) & (lane < (h + 1) * HEAD_DIM) for h in range(N_HEADS)]


def _colsum(v):
    return jnp.sum(v, axis=0, keepdims=True)


def _norm_inproj(x, g, w, name):
    S, D = x.shape
    N = w.shape[1]
    tm = TM_MM
    n_abc = N_ABC * GROUP_W
    n_qkv = 3 * GROUP_W

    def body(x_ref, g_ref, w_ref, zabc_ref, zg_ref, q1_ref, q4_ref, q16_ref, qkv_ref):
        xv = x_ref[...]
        r = lax.rsqrt(jnp.mean(xv * xv, axis=-1, keepdims=True) + NORM_EPS)
        h = ((xv * r) * g_ref[...]).astype(MXU_DTYPE)
        zabc_ref[...] = jnp.dot(h, w_ref[:, 0:n_abc], preferred_element_type=F32)
        _put(qkv_ref, jnp.dot(h, w_ref[:, n_abc:n_abc + n_qkv], preferred_element_type=F32))
        zg_ref[...] = jnp.dot(h, w_ref[:, n_abc + n_qkv:], preferred_element_type=F32)
        for dil, ref in zip(ATTN_DILATIONS, (q1_ref, q4_ref, q16_ref)):
            _deinterleave(qkv_ref, ref, dil)

    row = lambda wd: pl.BlockSpec((tm, wd), lambda i: (i, 0))
    return pl.pallas_call(
        body, name=name, grid=(S // tm,),
        in_specs=[row(D), pl.BlockSpec((1, D), lambda i: (0, 0)), pl.BlockSpec((D, N), lambda i: (0, 0))],
        out_specs=[row(n_abc), row(GROUP_W)] + [_dilated_spec(tm, n_qkv, dil) for dil in ATTN_DILATIONS],
        out_shape=[jax.ShapeDtypeStruct((S, n_abc), F32), jax.ShapeDtypeStruct((S, GROUP_W), F32)]
                  + [_dilated_shape(S, n_qkv, dil, MXU_DTYPE) for dil in ATTN_DILATIONS],
        scratch_shapes=[_lane_scratch(tm, n_qkv)],
        compiler_params=_params(("parallel",)),
    )(x, g, w)


def _conv_a(z_of, halo_of, w_ref):
    p = z_of(2) * z_of(0)
    p_h = halo_of(2) * halo_of(0)
    cv = w_ref[2:3, :] * p + w_ref[1:2, :] * _shift_down(p, p_h, 1) + w_ref[0:1, :] * _shift_down(p, p_h, 2)
    return p, p_h, cv


def _lru_gates(z_of, halo_of, wr_ref, vec_ref, wa_ref, wx_ref):
    rx = z_of(4)
    rx_h = halo_of(4)
    sh = [rx, _shift_down(rx, rx_h, 1), _shift_down(rx, rx_h, 2), _shift_down(rx, rx_h, 3)]
    xc = (wr_ref[3:4, :] * sh[0] + wr_ref[2:3, :] * sh[1] + wr_ref[1:2, :] * sh[2]
          + wr_ref[0:1, :] * sh[3] + vec_ref[0:1, :])
    ga = _sigmoid(jnp.dot(xc.astype(MXU_DTYPE), wa_ref[...], preferred_element_type=F32) + vec_ref[1:2, :])
    gi = _sigmoid(jnp.dot(xc.astype(MXU_DTYPE), wx_ref[...], preferred_element_type=F32) + vec_ref[2:3, :])
    sp = _softplus(-vec_ref[3:4, :])
    log_a = (-RG_C * ga) * sp
    a = jnp.exp(log_a)
    mult = jnp.sqrt(-_expm1_nonpos(2.0 * log_a))
    return xc, sh, ga, gi, a, mult, sp


def _gmlp_fwd(z_of, vec_ref, ws_ref, bs_ref, tm):
    u = _gelu(z_of(6))
    gv = _gelu(z_of(7))
    rr = lax.rsqrt(jnp.mean(gv * gv, axis=-1, keepdims=True) + NORM_EPS)
    vn = (gv * rr) * vec_ref[4:5, :]
    masks = _head_masks((GMLP_CHUNK, GROUP_W))
    parts = []
    for c in range(tm // GMLP_CHUNK):
        vc = vn[c * GMLP_CHUNK:(c + 1) * GMLP_CHUNK].astype(MXU_DTYPE)
        acc = bs_ref[...]
        for h in range(N_HEADS):
            acc = acc + jnp.where(masks[h], jnp.dot(ws_ref[h], vc, preferred_element_type=F32), 0.0)
        parts.append(acc)
    return u, gv, rr, vn, jnp.concatenate(parts, axis=0)


def _mix_specs(tm, S, order):
    const2 = lambda shape: pl.BlockSpec(shape, lambda i: (0, 0))
    return [const2((SUBLANES, GROUP_W)), const2((SUBLANES, GROUP_W)), const2((SUBLANES, GROUP_W)),
            const2((GROUP_W, GROUP_W)), const2((GROUP_W, GROUP_W)),
            pl.BlockSpec((N_HEADS, GMLP_CHUNK, GMLP_CHUNK), lambda i: (0, 0, 0)),
            const2((GMLP_CHUNK, GROUP_W))]


def _mix_fwd(z, mp, name):
    S = z.shape[0]
    tm = TM_MIX
    hb = tm // SUBLANES
    wcols = N_ABC * GROUP_W

    def body(z_ref, zh_ref, wA_ref, wR_ref, vec_ref, wa_ref, wx_ref, ws_ref, bs_ref, y_ref, h_ref, carry_ref):
        i = pl.program_id(0)

        @pl.when(i == 0)
        def _():
            carry_ref[...] = jnp.zeros_like(carry_ref)

        not_first = i > 0
        z_of = lambda c: z_ref[:, c * GROUP_W:(c + 1) * GROUP_W]
        halo_of = lambda c: jnp.where(not_first, zh_ref[:, c * GROUP_W:(c + 1) * GROUP_W], 0.0)

        _, _, cv = _conv_a(z_of, halo_of, wA_ref)
        y_ref[:, 0:GROUP_W] = z_of(1) * cv * _silu_and_grad(z_of(3))[0]

        xc, _, _, gi, a, mult, _ = _lru_gates(z_of, halo_of, wR_ref, vec_ref, wa_ref, wx_ref)
        b = mult * (gi * xc)
        acum, bcum = _scan_fwd(a, b)
        h = bcum + acum * carry_ref[SUBLANES - 1:SUBLANES, :]
        h_ref[...] = h
        carry_ref[...] = h[tm - SUBLANES:tm]
        y_ref[:, GROUP_W:2 * GROUP_W] = h * _silu_and_grad(z_of(5))[0]

        u, _, _, _, sp = _gmlp_fwd(z_of, vec_ref, ws_ref, bs_ref, tm)
        y_ref[:, 2 * GROUP_W:3 * GROUP_W] = u * sp * _silu_and_grad(z_of(8))[0]

    return pl.pallas_call(
        body, name=name, grid=(S // tm,),
        in_specs=[pl.BlockSpec((tm, wcols), lambda i: (i, 0)),
                  pl.BlockSpec((SUBLANES, wcols), lambda i: (jnp.maximum(i * hb - 1, 0), 0))]
                 + _mix_specs(tm, S, "fwd"),
        out_specs=[pl.BlockSpec((tm, 3 * GROUP_W), lambda i: (i, 0)),
                   pl.BlockSpec((tm, GROUP_W), lambda i: (i, 0))],
        out_shape=[jax.ShapeDtypeStruct((S, 3 * GROUP_W), F32), jax.ShapeDtypeStruct((S, GROUP_W), F32)],
        scratch_shapes=[pltpu.VMEM((SUBLANES, GROUP_W), F32)],
        compiler_params=_params(("arbitrary",)),
    )(z, z, mp["wA"], mp["wR"], mp["vec"], mp["wa"], mp["wx"], mp["ws"], mp["bs"])


def _attn_bias_and_mask(dil, which):
    qi = lax.broadcasted_iota(jnp.int32, (ATTN_BLOCK, ATTN_BLOCK), 0)
    ki = lax.broadcasted_iota(jnp.int32, (ATTN_BLOCK, ATTN_BLOCK), 1)
    if which == "cur":
        delta = qi - ki
    else:
        delta = qi + ATTN_BLOCK - ki
    valid = (delta >= 0) & (delta <= ATTN_BLOCK)
    dist = (delta * dil).astype(F32)
    return dist, valid


_NEG = -1e30


def _slope(h):
    return 2.0 ** (-8.0 * (h + 1) / N_HEADS)


def _attn_fwd(qkv, dil, name):
    rows = qkv.shape[0]
    nb = rows // ATTN_BLOCK
    scale = 1.0 / math.sqrt(HEAD_DIM)

    def body(q_ref, kc_ref, kp_ref, vc_ref, vp_ref, o_ref, l_ref):
        n = pl.program_id(1)
        has_prev = n > 0
        q = q_ref[...]
        kc, kp, vc, vp = kc_ref[...], kp_ref[...], vc_ref[...], vp_ref[...]
        masks = _head_masks((ATTN_BLOCK, GROUP_W))
        dist_c, valid_c = _attn_bias_and_mask(dil, "cur")
        dist_p, valid_p = _attn_bias_and_mask(dil, "prev")
        valid_p = valid_p & has_prev
        o_acc = jnp.zeros((ATTN_BLOCK, GROUP_W), F32)
        l_acc = jnp.zeros((ATTN_BLOCK, GROUP_W), F32)
        for h in range(N_HEADS):
            qm = jnp.where(masks[h], q, jnp.zeros_like(q))
            s_c = jnp.where(valid_c, _mm_nt(qm, kc) * scale - _slope(h) * dist_c, _NEG)
            s_p = jnp.where(valid_p, _mm_nt(qm, kp) * scale - _slope(h) * dist_p, _NEG)
            m = jnp.maximum(jnp.max(s_c, axis=-1, keepdims=True), jnp.max(s_p, axis=-1, keepdims=True))
            p_c = jnp.exp(s_c - m)
            p_p = jnp.exp(s_p - m)
            l = jnp.sum(p_c, axis=-1, keepdims=True) + jnp.sum(p_p, axis=-1, keepdims=True)
            o = (jnp.dot(p_c.astype(MXU_DTYPE), vc, preferred_element_type=F32)
                 + jnp.dot(p_p.astype(MXU_DTYPE), vp, preferred_element_type=F32)) / l
            o_acc = jnp.where(masks[h], o, o_acc)
            l_acc = jnp.where(masks[h], m + jnp.log(l), l_acc)
        o_ref[...] = o_acc
        l_ref[...] = l_acc

    blk = (ATTN_BLOCK, GROUP_W)
    cur = lambda c: pl.BlockSpec(blk, lambda r, n: (n, r * 3 + c))
    prev = lambda c: pl.BlockSpec(blk, lambda r, n: (jnp.maximum(n - 1, 0), r * 3 + c))
    out = pl.BlockSpec(blk, lambda r, n: (n, r))
    return pl.pallas_call(
        body, name=name, grid=(dil, nb),
        in_specs=[cur(0), cur(1), prev(1), cur(2), prev(2)],
        out_specs=[out, out],
        out_shape=[jax.ShapeDtypeStruct((rows, dil * GROUP_W), F32)] * 2,
        compiler_params=_params(("parallel", "parallel")),
    )(qkv, qkv, qkv, qkv, qkv)


def _outproj(x, z_g, y_abc, attn, w_out, name):
    S, D = x.shape
    tm = TM_MM
    n_abc = 3 * GROUP_W

    def body(x_ref, g_ref, yabc_ref, o1, l1, o2, l2, o3, l3, w_ref,
             xn_ref, y_ref, o_ref, lse1_ref, lse4_ref, lse16_ref, so2, sl2, so3, sl3, slse):
        for src, dst, dil in ((o2, so2, ATTN_DILATIONS[1]), (l2, sl2, ATTN_DILATIONS[1]),
                              (o3, so3, ATTN_DILATIONS[2]), (l3, sl3, ATTN_DILATIONS[2])):
            _interleave(src, dst, dil)
        la, lb, lc = l1[...], _get(sl2), _get(sl3)
        mx = jnp.maximum(jnp.maximum(la, lb), lc)
        ea, eb, ec = jnp.exp(la - mx), jnp.exp(lb - mx), jnp.exp(lc - mx)
        den = ea + eb + ec
        o = (ea * o1[...] + eb * _get(so2) + ec * _get(so3)) / den
        o_ref[...] = o
        _put(slse, mx + jnp.log(den))
        for dil, ref in zip(ATTN_DILATIONS, (lse1_ref, lse4_ref, lse16_ref)):
            _deinterleave(slse, ref, dil)
        y_d = o * _silu_and_grad(g_ref[...])[0]
        y_ref[:, 0:n_abc] = yabc_ref[...].astype(MXU_DTYPE)
        y_ref[:, n_abc:] = y_d.astype(MXU_DTYPE)
        xn_ref[...] = x_ref[...] + jnp.dot(y_ref[...], w_ref[...], preferred_element_type=F32)

    row = lambda w: pl.BlockSpec((tm, w), lambda i: (i, 0))
    dil_specs = [_dilated_spec(tm, GROUP_W, dil) for dil in ATTN_DILATIONS]
    (o1, l1), (o2, l2), (o3, l3) = attn
    return pl.pallas_call(
        body, name=name, grid=(S // tm,),
        in_specs=[row(D), row(GROUP_W), row(n_abc)] + [sp for sp in dil_specs for _ in range(2)]
                 + [pl.BlockSpec(w_out.shape, lambda i: (0, 0))],
        out_specs=[row(D), row(4 * GROUP_W), row(GROUP_W)] + dil_specs,
        out_shape=[jax.ShapeDtypeStruct((S, D), F32), jax.ShapeDtypeStruct((S, 4 * GROUP_W), MXU_DTYPE),
                   jax.ShapeDtypeStruct((S, GROUP_W), F32)]
                  + [_dilated_shape(S, GROUP_W, dil, F32) for dil in ATTN_DILATIONS],
        scratch_shapes=[_lane_scratch(tm, GROUP_W)] * 5,
        compiler_params=_params(("parallel",)),
    )(x, z_g, y_abc, o1, l1, o2, l2, o3, l3, w_out)


def _loss_head(x, g, target, name):
    S, D = x.shape
    tm = TM_MM

    def body(x_ref, g_ref, t_ref, dx_ref, loss_ref, dg_ref):
        i = pl.program_id(0)

        @pl.when(i == 0)
        def _():
            loss_ref[...] = jnp.zeros_like(loss_ref)
            dg_ref[...] = jnp.zeros_like(dg_ref)

        xv = x_ref[...]
        r = lax.rsqrt(jnp.mean(xv * xv, axis=-1, keepdims=True) + NORM_EPS)
        xn = xv * r
        err = xn * g_ref[...] - t_ref[...]
        per_tok = jnp.mean(err * err, axis=-1, keepdims=True)
        loss_ref[...] += 0.5 * jnp.sum(per_tok, axis=0, keepdims=True)
        dout = err * (1.0 / D)
        dg_ref[...] += _colsum(dout * xn)
        dxn = dout * g_ref[...]
        dx_ref[...] = r * (dxn - xn * jnp.mean(dxn * xn, axis=-1, keepdims=True))

    row = pl.BlockSpec((tm, D), lambda i: (i, 0))
    return pl.pallas_call(
        body, name=name, grid=(S // tm,),
        in_specs=[row, pl.BlockSpec((1, D), lambda i: (0, 0)), row],
        out_specs=[row, pl.BlockSpec((1, LANES), lambda i: (0, 0)), pl.BlockSpec((1, D), lambda i: (0, 0))],
        out_shape=[jax.ShapeDtypeStruct((S, D), F32), jax.ShapeDtypeStruct((1, LANES), F32),
                   jax.ShapeDtypeStruct((1, D), F32)],
        compiler_params=_params(("arbitrary",)),
    )(x, g, target)


def _outproj_bwd(dx, y, w_out, name):
    S, D = dx.shape
    E = y.shape[1]
    tm = TM_MM

    def body(dx_ref, y_ref, w_ref, dy_ref, dw_ref):
        i = pl.program_id(0)

        @pl.when(i == 0)
        def _():
            dw_ref[...] = jnp.zeros_like(dw_ref)

        dxb = dx_ref[...].astype(MXU_DTYPE)
        dy_ref[...] = _mm_nt(dxb, w_ref[...])
        dw_ref[...] += _mm_tn(y_ref[...], dxb)

    return pl.pallas_call(
        body, name=name, grid=(S // tm,),
        in_specs=[pl.BlockSpec((tm, D), lambda i: (i, 0)), pl.BlockSpec((tm, E), lambda i: (i, 0)),
                  pl.BlockSpec((E, D), lambda i: (0, 0))],
        out_specs=[pl.BlockSpec((tm, E), lambda i: (i, 0)), pl.BlockSpec((E, D), lambda i: (0, 0))],
        out_shape=[jax.ShapeDtypeStruct((S, E), F32), jax.ShapeDtypeStruct((E, D), F32)],
        compiler_params=_params(("arbitrary",)),
    )(dx, y, w_out)


def _mix_bwd(z, z_g, dy, hs, o, mp, name):
    S = z.shape[0]
    tm = TM_MIX
    hb = tm // SUBLANES
    nT = S // tm
    last_blk = S // SUBLANES - 1
    wcols = N_ABC * GROUP_W

    def body(z_ref, zh_ref, zn_ref, zg_ref, dy_ref, dyn_ref, h_ref, hh_ref, o_ref,
             wA_ref, wR_ref, vec_ref, wa_ref, wx_ref, ws_ref, bs_ref,
             dz_ref, dzg_ref, do1_ref, do4_ref, do16_ref, dl1_ref, dl4_ref, dl16_ref,
             dwA_ref, dwR_ref, dvec_ref, dwa_ref, dwx_ref, dws_ref, dbs_ref,
             hcarry_ref, xcarry_ref, bsacc_ref, do_ref, dl_ref):
        i = pl.program_id(0)
        ti = nT - 1 - i

        @pl.when(i == 0)
        def _():
            hcarry_ref[...] = jnp.zeros_like(hcarry_ref)
            xcarry_ref[...] = jnp.zeros_like(xcarry_ref)
            bsacc_ref[...] = jnp.zeros_like(bsacc_ref)
            dwA_ref[...] = jnp.zeros_like(dwA_ref)
            dwR_ref[...] = jnp.zeros_like(dwR_ref)
            dvec_ref[...] = jnp.zeros_like(dvec_ref)
            dwa_ref[...] = jnp.zeros_like(dwa_ref)
            dwx_ref[...] = jnp.zeros_like(dwx_ref)
            dws_ref[...] = jnp.zeros_like(dws_ref)
            dbs_ref[...] = jnp.zeros_like(dbs_ref)

        has_prev = ti > 0
        has_next = i > 0
        col = lambda c: slice(c * GROUP_W, (c + 1) * GROUP_W)
        z_of = lambda c: z_ref[:, col(c)]
        halo_of = lambda c: jnp.where(has_prev, zh_ref[:, col(c)], 0.0)
        next_of = lambda c: zn_ref[:, col(c)]

        p, p_h, cv = _conv_a(z_of, halo_of, wA_ref)
        sg, dsg = _silu_and_grad(z_of(3))
        a_b = z_of(1)
        dya = dy_ref[:, col(0)]
        dcv = dya * a_b * sg
        dcv_n = jnp.where(has_next, dyn_ref[...] * next_of(1) * _silu_and_grad(next_of(3))[0], 0.0)
        dp = (wA_ref[2:3, :] * dcv + wA_ref[1:2, :] * _shift_up(dcv, dcv_n, 1)
              + wA_ref[0:1, :] * _shift_up(dcv, dcv_n, 2))
        dwA_ref[2:3, :] += _colsum(dcv * p)
        dwA_ref[1:2, :] += _colsum(dcv * _shift_down(p, p_h, 1))
        dwA_ref[0:1, :] += _colsum(dcv * _shift_down(p, p_h, 2))
        dz_ref[:, col(0)] = dp * z_of(2)
        dz_ref[:, col(1)] = dya * cv * sg
        dz_ref[:, col(2)] = dp * z_of(0)
        dz_ref[:, col(3)] = dya * a_b * cv * dsg

        xc, sh, ga, gi, a, mult, sp = _lru_gates(z_of, halo_of, wR_ref, vec_ref, wa_ref, wx_ref)
        h = h_ref[...]
        h_prev = _shift_down(h, jnp.where(has_prev, hh_ref[...], 0.0), 1)
        sgr, dsgr = _silu_and_grad(z_of(5))
        dyb = dy_ref[:, col(1)]
        dz_ref[:, col(5)] = dyb * h * dsgr
        row = lax.broadcasted_iota(jnp.int32, (tm, GROUP_W), 0)
        g_in = dyb * sgr + jnp.where(row == tm - 1, hcarry_ref[0:1, :], 0.0)
        a_up = _shift_up(a, jnp.zeros((SUBLANES, GROUP_W), F32), 1)
        dH = _scan_rev(a_up, g_in)
        hcarry_ref[...] = (a * dH)[0:SUBLANES]
        da = dH * h_prev
        gx = gi * xc
        dmult = dH * gx
        dgi = dH * mult * xc
        dxc = dH * mult * gi
        dlog_a = da * a - dmult * (a * a) / mult
        dga = dlog_a * (-RG_C * sp)
        dlam_row = _colsum(dlog_a * (-RG_C * ga)) * (-_sigmoid(-vec_ref[3:4, :]))
        dpre_a = dga * ga * (1.0 - ga)
        dpre_i = dgi * gi * (1.0 - gi)
        dwa_ref[...] += _mm_tn(xc, dpre_a)
        dwx_ref[...] += _mm_tn(xc, dpre_i)
        dxc = dxc + _mm_nt(dpre_a, wa_ref[...]) + _mm_nt(dpre_i, wx_ref[...])
        dvec_ref[0:1, :] += _colsum(dxc)
        dvec_ref[1:2, :] += _colsum(dpre_a)
        dvec_ref[2:3, :] += _colsum(dpre_i)
        dvec_ref[3:4, :] += dlam_row
        for k in range(4):
            dwR_ref[k:k + 1, :] += _colsum(dxc * sh[3 - k])
        dxc_n = xcarry_ref[...]
        dz_ref[:, col(4)] = (wR_ref[3:4, :] * dxc + wR_ref[2:3, :] * _shift_up(dxc, dxc_n, 1)
                             + wR_ref[1:2, :] * _shift_up(dxc, dxc_n, 2)
                             + wR_ref[0:1, :] * _shift_up(dxc, dxc_n, 3))
        xcarry_ref[...] = dxc[0:SUBLANES]

        c_u, c_v = z_of(6), z_of(7)
        u, du_dx = _gelu_and_grad(c_u)
        gv, dgv_dx = _gelu_and_grad(c_v)
        rr = lax.rsqrt(jnp.mean(gv * gv, axis=-1, keepdims=True) + NORM_EPS)
        xhat = gv * rr
        g_c = vec_ref[4:5, :]
        vn = xhat * g_c
        masks = _head_masks((GMLP_CHUNK, GROUP_W))
        tri_r = lax.broadcasted_iota(jnp.int32, (GMLP_CHUNK, GMLP_CHUNK), 0)
        tri_c = lax.broadcasted_iota(jnp.int32, (GMLP_CHUNK, GMLP_CHUNK), 1)
        tril = tri_r >= tri_c
        sgc, dsgc = _silu_and_grad(z_of(8))
        dyc = dy_ref[:, col(2)]
        dsp_full = dyc * u * sgc
        sp_parts, dvn_parts = [], []
        for c in range(tm // GMLP_CHUNK):
            rs = slice(c * GMLP_CHUNK, (c + 1) * GMLP_CHUNK)
            vc = vn[rs].astype(MXU_DTYPE)
            dsp_c = dsp_full[rs]
            bsacc_ref[...] += dsp_c
            acc = bs_ref[...]
            dvn_c = jnp.zeros((GMLP_CHUNK, GROUP_W), F32)
            for h in range(N_HEADS):
                w_h = ws_ref[h]
                acc = acc + jnp.where(masks[h], jnp.dot(w_h, vc, preferred_element_type=F32), 0.0)
                dsp_h = jnp.where(masks[h], dsp_c, 0.0).astype(MXU_DTYPE)
                dvn_c = dvn_c + _mm_tn(w_h, dsp_h)
                dws_ref[h] += jnp.where(tril, _mm_nt(dsp_h, vc), 0.0)
            sp_parts.append(acc)
            dvn_parts.append(dvn_c)
        spv = jnp.concatenate(sp_parts, axis=0)
        dvn = jnp.concatenate(dvn_parts, axis=0)
        dz_ref[:, col(6)] = dyc * spv * sgc * du_dx
        dz_ref[:, col(8)] = dyc * u * spv * dsgc
        dvec_ref[4:5, :] += _colsum(dvn * xhat)
        dgvn = dvn * g_c
        dgv = rr * (dgvn - xhat * jnp.mean(dgvn * xhat, axis=-1, keepdims=True))
        dz_ref[:, col(7)] = dgv * dgv_dx

        sgd, dsgd = _silu_and_grad(zg_ref[...])
        dyd = dy_ref[:, col(3)]
        ov = o_ref[...]
        do = dyd * sgd
        _put(do_ref, do)
        dzg_ref[...] = dyd * ov * dsgd
        prod = do * ov
        tmasks = _head_masks((tm, GROUP_W))
        dl = jnp.zeros((tm, GROUP_W), F32)
        for h in range(N_HEADS):
            dl = jnp.where(tmasks[h], jnp.sum(jnp.where(tmasks[h], prod, 0.0), axis=-1, keepdims=True), dl)
        _put(dl_ref, dl)
        for dil, d_out, l_out in zip(ATTN_DILATIONS, (do1_ref, do4_ref, do16_ref), (dl1_ref, dl4_ref, dl16_ref)):
            _deinterleave(do_ref, d_out, dil)
            _deinterleave(dl_ref, l_out, dil)

        @pl.when(i == nT - 1)
        def _():
            acc = bsacc_ref[...]
            lane = lax.broadcasted_iota(jnp.int32, (GMLP_CHUNK, LANES), 1)
            out = jnp.zeros((GMLP_CHUNK, LANES), F32)
            for h in range(N_HEADS):
                out = jnp.where(lane == h, jnp.sum(jnp.where(masks[h], acc, 0.0), axis=-1, keepdims=True), out)
            dbs_ref[...] = out

    rev = lambda w: pl.BlockSpec((tm, w), lambda i: (nT - 1 - i, 0))
    prev8 = lambda w: pl.BlockSpec((SUBLANES, w), lambda i: (jnp.maximum((nT - 1 - i) * hb - 1, 0), 0))
    next8 = lambda w: pl.BlockSpec((SUBLANES, w), lambda i: (jnp.minimum((nT - i) * hb, last_blk), 0))
    const2 = lambda shape: pl.BlockSpec(shape, lambda i: (0, 0))
    dil_specs = [_dilated_spec(tm, GROUP_W, dil, lambda i: nT - 1 - i) for dil in ATTN_DILATIONS]
    dil_shapes = [_dilated_shape(S, GROUP_W, dil, F32) for dil in ATTN_DILATIONS]
    small = (SUBLANES, GROUP_W)
    sq = (GROUP_W, GROUP_W)
    ws_shape = (N_HEADS, GMLP_CHUNK, GMLP_CHUNK)
    return pl.pallas_call(
        body, name=name, grid=(nT,),
        in_specs=[rev(wcols), prev8(wcols), next8(wcols), rev(GROUP_W),
                  rev(4 * GROUP_W), next8(GROUP_W), rev(GROUP_W), prev8(GROUP_W), rev(GROUP_W)]
                 + _mix_specs(tm, S, "bwd"),
        out_specs=[rev(wcols), rev(GROUP_W)] + dil_specs + dil_specs
                  + [const2(small), const2(small), const2(small), const2(sq), const2(sq),
                     pl.BlockSpec(ws_shape, lambda i: (0, 0, 0)), const2((GMLP_CHUNK, LANES))],
        out_shape=[jax.ShapeDtypeStruct((S, wcols), F32), jax.ShapeDtypeStruct((S, GROUP_W), F32)]
                  + dil_shapes + dil_shapes
                  + [jax.ShapeDtypeStruct(small, F32)] * 3 + [jax.ShapeDtypeStruct(sq, F32)] * 2
                  + [jax.ShapeDtypeStruct(ws_shape, F32), jax.ShapeDtypeStruct((GMLP_CHUNK, LANES), F32)],
        scratch_shapes=[pltpu.VMEM(small, F32), pltpu.VMEM(small, F32), pltpu.VMEM((GMLP_CHUNK, GROUP_W), F32),
                        _lane_scratch(tm, GROUP_W), _lane_scratch(tm, GROUP_W)],
        compiler_params=_params(("arbitrary",)),
    )(z, z, z, z_g, dy, dy, hs, hs, o, mp["wA"], mp["wR"], mp["vec"], mp["wa"], mp["wx"], mp["ws"], mp["bs"])


def _attn_bwd(qkv, do, lse, delta, dil, name):
    rows = qkv.shape[0]
    nb = rows // ATTN_BLOCK
    scale = 1.0 / math.sqrt(HEAD_DIM)

    def body(qc_ref, qn_ref, k_ref, v_ref, doc_ref, don_ref, lc_ref, ln_ref, dc_ref, dn_ref,
             dq_ref, dk_ref, dv_ref, carry_ref):
        n = pl.program_id(1)

        @pl.when(n == 0)
        def _():
            carry_ref[...] = jnp.zeros_like(carry_ref)

        has_next = n < nb - 1
        kb = k_ref[...]
        vb = v_ref[...]
        masks = _head_masks((ATTN_BLOCK, GROUP_W))
        sides = []
        for which, q_ref, d_ref, l_ref, t_ref in (("cur", qc_ref, doc_ref, lc_ref, dc_ref),
                                                   ("prev", qn_ref, don_ref, ln_ref, dn_ref)):
            dist, valid = _attn_bias_and_mask(dil, which)
            if which == "prev":
                valid = valid & has_next
            sides.append((q_ref[...], d_ref[...], l_ref[...], t_ref[...], dist, valid))
        dq_parts = [jnp.zeros((ATTN_BLOCK, GROUP_W), F32), jnp.zeros((ATTN_BLOCK, GROUP_W), F32)]
        dk = jnp.zeros((ATTN_BLOCK, GROUP_W), F32)
        dv = jnp.zeros((ATTN_BLOCK, GROUP_W), F32)
        for h in range(N_HEADS):
            for side, (q, dov, lv, tv, dist, valid) in enumerate(sides):
                qm = jnp.where(masks[h], q, jnp.zeros_like(q))
                dom = jnp.where(masks[h], dov, 0.0).astype(MXU_DTYPE)
                lse_h = jnp.max(jnp.where(masks[h], lv, _NEG), axis=-1, keepdims=True)
                dl_h = jnp.max(jnp.where(masks[h], tv, _NEG), axis=-1, keepdims=True)
                s = _mm_nt(qm, kb) * scale - _slope(h) * dist
                p = jnp.where(valid, jnp.exp(jnp.where(valid, s, _NEG) - lse_h), 0.0)
                dp = _mm_nt(dom, vb)
                ds = (p * (dp - dl_h) * scale).astype(MXU_DTYPE)
                pb = p.astype(MXU_DTYPE)
                dv = dv + _mm_tn(pb, dom)
                dk = dk + _mm_tn(ds, qm)
                dq_parts[side] = dq_parts[side] + jnp.where(masks[h], jnp.dot(ds, kb, preferred_element_type=F32), 0.0)
        dq_ref[...] = carry_ref[...] + dq_parts[0]
        carry_ref[...] = dq_parts[1]
        dk_ref[...] = dk
        dv_ref[...] = dv

    blk = (ATTN_BLOCK, GROUP_W)
    zcur = lambda c: pl.BlockSpec(blk, lambda r, n: (n, r * 3 + c))
    znext = lambda c: pl.BlockSpec(blk, lambda r, n: (jnp.minimum(n + 1, nb - 1), r * 3 + c))
    cur = pl.BlockSpec(blk, lambda r, n: (n, r))
    nxt = pl.BlockSpec(blk, lambda r, n: (jnp.minimum(n + 1, nb - 1), r))
    return pl.pallas_call(
        body, name=name, grid=(dil, nb),
        in_specs=[zcur(0), znext(0), zcur(1), zcur(2), cur, nxt, cur, nxt, cur, nxt],
        out_specs=[cur, cur, cur],
        out_shape=[jax.ShapeDtypeStruct((rows, dil * GROUP_W), F32)] * 3,
        scratch_shapes=[pltpu.VMEM(blk, F32)],
        compiler_params=_params(("arbitrary", "arbitrary")),
    )(qkv, qkv, qkv, qkv, do, do, lse, lse, delta, delta)


def _inproj_bwd(x, g, dxn, dz_abc, dqkv, dz_g, w_in, name):
    S, D = x.shape
    N = w_in.shape[1]
    tm = TM_MM
    n_abc = N_ABC * GROUP_W

    def body(x_ref, g_ref, dxn_ref, dabc_ref, q1, k1, v1, q2, k2, v2, q3, k3, v3, dg_ref, w_ref,
             dx_ref, dz_ref, h_ref, dgn_ref, s4_ref, s16_ref):
        i = pl.program_id(0)

        @pl.when(i == 0)
        def _():
            dgn_ref[...] = jnp.zeros_like(dgn_ref)

        dz_ref[:, 0:n_abc] = dabc_ref[...].astype(MXU_DTYPE)
        for j, parts in enumerate(((q1, q2, q3), (k1, k2, k3), (v1, v2, v3))):
            c0 = n_abc + j * GROUP_W
            _interleave(parts[1], s4_ref, ATTN_DILATIONS[1])
            _interleave(parts[2], s16_ref, ATTN_DILATIONS[2])
            dz_ref[:, c0:c0 + GROUP_W] = (parts[0][...] + _get(s4_ref) + _get(s16_ref)).astype(MXU_DTYPE)
        dz_ref[:, n_abc + 3 * GROUP_W:] = dg_ref[...].astype(MXU_DTYPE)
        dh = _mm_nt(dz_ref[...], w_ref[...])
        xv = x_ref[...]
        r = lax.rsqrt(jnp.mean(xv * xv, axis=-1, keepdims=True) + NORM_EPS)
        xn = xv * r
        gv = g_ref[...]
        h_ref[...] = (xn * gv).astype(MXU_DTYPE)
        dgn_ref[...] += _colsum(dh * xn)
        dn = dh * gv
        dx_ref[...] = dxn_ref[...] + r * (dn - xn * jnp.mean(dn * xn, axis=-1, keepdims=True))

    row = lambda w: pl.BlockSpec((tm, w), lambda i: (i, 0))
    flat = [t for p in dqkv for t in p]
    dil_specs = [_dilated_spec(tm, GROUP_W, dil) for dil in ATTN_DILATIONS for _ in range(3)]
    return pl.pallas_call(
        body, name=name, grid=(S // tm,),
        in_specs=[row(D), pl.BlockSpec((1, D), lambda i: (0, 0)), row(D), row(n_abc)] + dil_specs
                 + [row(GROUP_W), pl.BlockSpec((D, N), lambda i: (0, 0))],
        out_specs=[row(D), row(N), row(D), pl.BlockSpec((1, D), lambda i: (0, 0))],
        out_shape=[jax.ShapeDtypeStruct((S, D), F32), jax.ShapeDtypeStruct((S, N), MXU_DTYPE),
                   jax.ShapeDtypeStruct((S, D), MXU_DTYPE), jax.ShapeDtypeStruct((1, D), F32)],
        scratch_shapes=[_lane_scratch(tm, GROUP_W)] * 2,
        compiler_params=_params(("arbitrary",)),
    )(x, g, dxn, dz_abc, *flat, dz_g, w_in)


def _inproj_wgrad(h, dz, name):
    S, D = h.shape
    N = dz.shape[1]
    tm = TM_MM
    nj = 2
    cw = N // nj

    def body(h_ref, dz_ref, dw_ref):
        @pl.when(pl.program_id(1) == 0)
        def _():
            dw_ref[...] = jnp.zeros_like(dw_ref)

        dw_ref[...] += _mm_tn(h_ref[...], dz_ref[...])

    return pl.pallas_call(
        body, name=name, grid=(nj, S // tm),
        in_specs=[pl.BlockSpec((tm, D), lambda j, i: (i, 0)), pl.BlockSpec((tm, cw), lambda j, i: (i, j))],
        out_specs=pl.BlockSpec((D, cw), lambda j, i: (0, j)),
        out_shape=jax.ShapeDtypeStruct((D, N), F32),
        compiler_params=_params(("parallel", "arbitrary")),
    )(h, dz)


def _my_place():
    return lax.axis_index("x"), lax.axis_index("y"), lax.axis_index("c")


def _peer(x, y, c, k):
    px = 1 - x if k & 4 else x
    py = 1 - y if k & 2 else y
    pc = 1 - c if k & 1 else c
    return (px, py, pc), 4 * px + 2 * py + pc


def _exchange(arrays, gather, name):
    n_t = len(arrays)
    anyspec = pl.BlockSpec(memory_space=pl.ANY)

    def body(*refs):
        ins, outs = refs[:n_t], refs[n_t:2 * n_t]
        send_sems, recv_sems, local_sems = refs[2 * n_t:]
        x, y, c = _my_place()
        me = 4 * x + 2 * y + c
        copies = []
        for t in range(n_t):
            src_mine = ins[t] if gather else ins[t].at[me]
            local = pltpu.make_async_copy(src_mine, outs[t].at[me], local_sems.at[t])
            local.start()
            copies.append(local)
        remote = []
        for t in range(n_t):
            for k in range(1, N_DEV):
                peer, pidx = _peer(x, y, c, k)
                src = ins[t] if gather else ins[t].at[pidx]
                cp = pltpu.make_async_remote_copy(
                    src_ref=src, dst_ref=outs[t].at[me], send_sem=send_sems.at[t, k - 1],
                    recv_sem=recv_sems.at[t, k - 1], device_id=peer, device_id_type=MESH)
                cp.start()
                remote.append((cp, t, k, pidx))
        for cp, t, k, pidx in remote:
            src = ins[t] if gather else ins[t].at[pidx]
            pltpu.make_async_remote_copy(
                src_ref=src, dst_ref=outs[t].at[pidx], send_sem=send_sems.at[t, k - 1],
                recv_sem=recv_sems.at[t, k - 1], device_id=_peer(x, y, c, k)[0], device_id_type=MESH).wait_recv()
        for cp, t, k, pidx in remote:
            cp.wait_send()
        for local in copies:
            local.wait()

    out_shape = [jax.ShapeDtypeStruct(((N_DEV,) + a.shape) if gather else a.shape, a.dtype) for a in arrays]
    return pl.pallas_call(
        body, name=name,
        in_specs=[anyspec] * n_t, out_specs=[anyspec] * n_t, out_shape=out_shape,
        scratch_shapes=[pltpu.SemaphoreType.DMA((n_t, N_DEV - 1)), pltpu.SemaphoreType.DMA((n_t, N_DEV - 1)),
                        pltpu.SemaphoreType.DMA((n_t,))],
        compiler_params=pltpu.CompilerParams(has_side_effects=True),
    )(*arrays)


def _allreduce(buf, name):
    R = buf.shape[0]

    def body(x_ref, out_ref, recv_ref, send_sems, recv_sems):
        x, y, c = _my_place()
        out_ref[...] = x_ref[...]
        for s, k in enumerate((1, 4, 2)):
            peer, _ = _peer(x, y, c, k)
            cp = pltpu.make_async_remote_copy(
                src_ref=out_ref, dst_ref=recv_ref.at[s], send_sem=send_sems.at[s], recv_sem=recv_sems.at[s],
                device_id=peer, device_id_type=MESH)
            cp.start()
            cp.wait()
            out_ref[...] = out_ref[...] + recv_ref[s]

    vm = pl.BlockSpec(memory_space=pltpu.VMEM)
    return pl.pallas_call(
        body, name=name, in_specs=[vm], out_specs=vm,
        out_shape=jax.ShapeDtypeStruct((R, LANES), F32),
        scratch_shapes=[pltpu.VMEM((3, R, LANES), F32), pltpu.SemaphoreType.DMA((3,)), pltpu.SemaphoreType.DMA((3,))],
        compiler_params=pltpu.CompilerParams(has_side_effects=True, vmem_limit_bytes=VMEM_LIMIT),
    )(buf)


def _adamw_math(w, g, m, v):
    m = ADAM_B1 * m + (1.0 - ADAM_B1) * g
    v = ADAM_B2 * v + (1.0 - ADAM_B2) * (g * g)
    m_hat = m / (1.0 - ADAM_B1 ** ADAM_STEP)
    v_hat = v / (1.0 - ADAM_B2 ** ADAM_STEP)
    delta = -ADAM_LR * (m_hat / (jnp.sqrt(v_hat) + ADAM_EPS) + ADAM_WD * w)
    return delta, m, v


def _adamw_summed(parts, w, m, v, tr, name):
    R, C = w.shape

    def body(p_ref, w_ref, m_ref, v_ref, g_ref, d_ref, nm_ref, nv_ref):
        g = p_ref[0]
        for j in range(1, N_DEV):
            g = g + p_ref[j]
        g_ref[...] = g
        d_ref[...], nm_ref[...], nv_ref[...] = _adamw_math(w_ref[...], g, m_ref[...], v_ref[...])

    row = pl.BlockSpec((tr, C), lambda i: (i, 0))
    return pl.pallas_call(
        body, name=name, grid=(R // tr,),
        in_specs=[pl.BlockSpec((N_DEV, tr, C), lambda i: (0, i, 0)), row, row, row],
        out_specs=[row] * 4, out_shape=[jax.ShapeDtypeStruct((R, C), F32)] * 4,
        compiler_params=_params(("parallel",)),
    )(parts, w, m, v)


def _adamw_small(w, g, m, v, name):
    def body(w_ref, g_ref, m_ref, v_ref, d_ref, nm_ref, nv_ref):
        d_ref[...], nm_ref[...], nv_ref[...] = _adamw_math(w_ref[...], g_ref[...], m_ref[...], v_ref[...])

    vm = pl.BlockSpec(memory_space=pltpu.VMEM)
    return pl.pallas_call(
        body, name=name, in_specs=[vm] * 4, out_specs=[vm] * 3,
        out_shape=[jax.ShapeDtypeStruct(w.shape, F32)] * 3,
        compiler_params=pltpu.CompilerParams(vmem_limit_bytes=VMEM_LIMIT),
    )(w, g, m, v)


def _pack(arrays):
    flat = jnp.concatenate([a.reshape(-1) for a in arrays])
    pad = (-flat.shape[0]) % (SUBLANES * LANES)
    return jnp.pad(flat, (0, pad)).reshape(-1, LANES)


def _unpack(buf, like):
    flat = buf.reshape(-1)
    out, off = [], 0
    for a in like:
        out.append(flat[off:off + a.size].reshape(a.shape))
        off += a.size
    return out


def _block_diag(w):
    eye = jnp.eye(N_HEADS, dtype=w.dtype)
    return jnp.einsum('hij,hk->hikj', w, eye).reshape(GROUP_W, GROUP_W)


def _diag_blocks(w):
    return jnp.einsum('hihj->hij', w.reshape(N_HEADS, HEAD_DIM, N_HEADS, HEAD_DIM))


def _pad_rows(a):
    return jnp.pad(a, ((0, SUBLANES - a.shape[0]), (0, 0)))


def _mixer_params(l, conv_a_w, conv_r_w, conv_r_b, lru_wa, lru_ba, lru_wx, lru_bx, lru_lambda, gmlp_norm_g,
                  gmlp_ws, gmlp_bs):
    tril = jnp.tril(jnp.ones((GMLP_CHUNK, GMLP_CHUNK), dtype=bool))
    vec = jnp.stack([conv_r_b[l], lru_ba[l], lru_bx[l], lru_lambda[l], gmlp_norm_g[l]])
    return {
        "wA": _pad_rows(conv_a_w[l]), "wR": _pad_rows(conv_r_w[l]), "vec": _pad_rows(vec),
        "wa": _block_diag(lru_wa[l]).astype(MXU_DTYPE), "wx": _block_diag(lru_wx[l]).astype(MXU_DTYPE),
        "ws": jnp.where(tril[None], gmlp_ws[l], 0.0).astype(MXU_DTYPE),
        "bs": jnp.repeat(jnp.transpose(gmlp_bs[l]), HEAD_DIM, axis=1),
    }


def _local_step(x, loss_target, norm_g, w_in_full, w_out_full, conv_a_w, conv_r_w, conv_r_b, lru_wa, lru_ba,
                lru_wx, lru_bx, lru_lambda, gmlp_norm_g, gmlp_ws, gmlp_bs, final_g):
    depth = norm_g.shape[0]
    D = x.shape[1]
    small = (conv_a_w, conv_r_w, conv_r_b, lru_wa, lru_ba, lru_wx, lru_bx, lru_lambda, gmlp_norm_g, gmlp_ws, gmlp_bs)
    saved = []
    for l in range(depth):
        mp = _mixer_params(l, *small)
        z, z_g, *qkv = _norm_inproj(x, norm_g[l].reshape(1, D), w_in_full[l], f"norm_inproj_{l}")
        y_abc, hs = _mix_fwd(z, mp, f"mix_fwd_{l}")
        attn = [_attn_fwd(qkv[p], dil, f"attn_fwd_d{dil}_{l}") for p, dil in enumerate(ATTN_DILATIONS)]
        x_new, y, o, *lse = _outproj(x, z_g, y_abc, attn, w_out_full[l], f"outproj_{l}")
        saved.append((x, z, z_g, qkv, hs, y, o, lse, mp))
        x = x_new
    dx, loss, d_final_g = _loss_head(x, final_g.reshape(1, D), loss_target, "loss_head")
    grads = []
    for l in reversed(range(depth)):
        x_l, z, z_g, qkv, hs, y, o, lse, mp = saved[l]
        dy, dw_out = _outproj_bwd(dx, y, w_out_full[l], f"outproj_bwd_{l}")
        (dz_abc, dz_g, do1, do4, do16, dl1, dl4, dl16, dwA, dwR, dvec, dwa, dwx, dws, dbs) = _mix_bwd(
            z, z_g, dy, hs, o, mp, f"mix_bwd_{l}")
        dqkv = [_attn_bwd(qkv[p], do, lse[p], dl, dil, f"attn_bwd_d{dil}_{l}")
                for p, (dil, do, dl) in enumerate(zip(ATTN_DILATIONS, (do1, do4, do16), (dl1, dl4, dl16)))]
        dx, dz, h, dng = _inproj_bwd(x_l, norm_g[l].reshape(1, D), dx, dz_abc, dqkv, dz_g, w_in_full[l],
                                     f"inproj_bwd_{l}")
        dw_in = _inproj_wgrad(h, dz, f"inproj_wgrad_{l}")
        grads.append({
            "norm_g": dng[0], "w_in": dw_in, "w_out": dw_out,
            "conv_a_w": dwA[:conv_a_w.shape[1]], "conv_r_w": dwR[:conv_r_w.shape[1]],
            "conv_r_b": dvec[0], "lru_ba": dvec[1], "lru_bx": dvec[2], "lru_lambda": dvec[3], "gmlp_norm_g": dvec[4],
            "lru_wa": _diag_blocks(dwa), "lru_wx": _diag_blocks(dwx), "gmlp_ws": dws,
            "gmlp_bs": jnp.transpose(dbs[:, :N_HEADS]),
        })
    grads = grads[::-1]
    stacked = {k: jnp.stack([g[k] for g in grads]) for k in grads[0]}
    stacked["final_g"] = d_final_g[0]
    return loss[0, 0], dx, stacked


SMALL_NAMES = ("norm_g", "conv_a_w", "conv_r_w", "conv_r_b", "lru_wa", "lru_ba", "lru_wx", "lru_bx", "lru_lambda",
               "gmlp_norm_g", "gmlp_ws", "gmlp_bs", "final_g")
WEIGHT_NAMES = ("norm_g", "w_in", "conv_a_w", "conv_r_w", "conv_r_b", "lru_wa", "lru_ba", "lru_wx", "lru_bx",
                "lru_lambda", "gmlp_norm_g", "gmlp_ws", "gmlp_bs", "w_out", "final_g")


def kernel(x, norm_g, w_in, conv_a_w, conv_r_w, conv_r_b, lru_wa, lru_ba, lru_wx, lru_bx, lru_lambda, gmlp_norm_g, gmlp_ws, gmlp_bs, w_out, final_g, loss_target, m_norm_g, m_w_in, m_conv_a_w, m_conv_r_w, m_conv_r_b, m_lru_wa, m_lru_ba, m_lru_wx, m_lru_bx, m_lru_lambda, m_gmlp_norm_g, m_gmlp_ws, m_gmlp_bs, m_w_out, m_final_g, v_norm_g, v_w_in, v_conv_a_w, v_conv_r_w, v_conv_r_b, v_lru_wa, v_lru_ba, v_lru_wx, v_lru_bx, v_lru_lambda, v_gmlp_norm_g, v_gmlp_ws, v_gmlp_bs, v_w_out, v_final_g):
    w = dict(norm_g=norm_g, w_in=w_in, conv_a_w=conv_a_w, conv_r_w=conv_r_w, conv_r_b=conv_r_b, lru_wa=lru_wa,
             lru_ba=lru_ba, lru_wx=lru_wx, lru_bx=lru_bx, lru_lambda=lru_lambda, gmlp_norm_g=gmlp_norm_g,
             gmlp_ws=gmlp_ws, gmlp_bs=gmlp_bs, w_out=w_out, final_g=final_g)
    m = dict(norm_g=m_norm_g, w_in=m_w_in, conv_a_w=m_conv_a_w, conv_r_w=m_conv_r_w, conv_r_b=m_conv_r_b,
             lru_wa=m_lru_wa, lru_ba=m_lru_ba, lru_wx=m_lru_wx, lru_bx=m_lru_bx, lru_lambda=m_lru_lambda,
             gmlp_norm_g=m_gmlp_norm_g, gmlp_ws=m_gmlp_ws, gmlp_bs=m_gmlp_bs, w_out=m_w_out, final_g=m_final_g)
    v = dict(norm_g=v_norm_g, w_in=v_w_in, conv_a_w=v_conv_a_w, conv_r_w=v_conv_r_w, conv_r_b=v_conv_r_b,
             lru_wa=v_lru_wa, lru_ba=v_lru_ba, lru_wx=v_lru_wx, lru_bx=v_lru_bx, lru_lambda=v_lru_lambda,
             gmlp_norm_g=v_gmlp_norm_g, gmlp_ws=v_gmlp_ws, gmlp_bs=v_gmlp_bs, w_out=v_w_out, final_g=v_final_g)
    depth, D, n_loc = w_in.shape
    e_loc = w_out.shape[1]
    cx, cy, cc = _my_place()
    me = 4 * cx + 2 * cy + cc

    g_in, g_out = _exchange([w_in.astype(MXU_DTYPE), w_out.astype(MXU_DTYPE)], True, "gather_weights")
    w_in_full = jnp.transpose(g_in, (1, 2, 0, 3)).reshape(depth, D, N_DEV * n_loc)
    w_out_full = jnp.transpose(g_out, (1, 0, 2, 3)).reshape(depth, N_DEV * e_loc, D)
    c_loc = conv_a_w.shape[2]
    conv_full = [jnp.zeros(a.shape[:2] + (N_DEV * c_loc,), F32) for a in (conv_a_w, conv_r_w)]
    conv_full = [lax.dynamic_update_slice_in_dim(f, a, me * c_loc, axis=2)
                 for f, a in zip(conv_full, (conv_a_w, conv_r_w))]
    conv_a_full, conv_r_full = _unpack(_allreduce(_pack(conv_full), "gather_conv_taps"), conv_full)

    loss, grad_x, g = _local_step(
        x[0], loss_target[0], norm_g, w_in_full, w_out_full, conv_a_full, conv_r_full, conv_r_b, lru_wa, lru_ba,
        lru_wx, lru_bx, lru_lambda, gmlp_norm_g, gmlp_ws, gmlp_bs, final_g)
    loss = lax.psum(loss, ("x", "y", "c"))

    p_in = jnp.transpose(g["w_in"].reshape(depth, D, N_DEV, n_loc), (2, 0, 1, 3)).reshape(N_DEV, depth * D, n_loc)
    p_out = jnp.transpose(g["w_out"].reshape(depth, N_DEV, e_loc, D), (1, 0, 2, 3)).reshape(N_DEV, depth * e_loc, D)
    r_in, r_out = _exchange([p_in, p_out], False, "scatter_wgrads")
    flat_in = lambda a: a.reshape(depth * D, n_loc)
    flat_out = lambda a: a.reshape(depth * e_loc, D)
    big = {}
    big["w_in"] = [a.reshape(w_in.shape) for a in _adamw_summed(
        r_in, flat_in(w_in), flat_in(m_w_in), flat_in(v_w_in), 512, "adamw_w_in")]
    big["w_out"] = [a.reshape(w_out.shape) for a in _adamw_summed(
        r_out, flat_out(w_out), flat_out(m_w_out), flat_out(v_w_out), 128, "adamw_w_out")]

    g_small = [g[k] for k in SMALL_NAMES]
    g_small = _unpack(_allreduce(_pack(g_small), "allreduce_small_grads"), g_small)
    g_small = dict(zip(SMALL_NAMES, g_small))
    for k in ("conv_a_w", "conv_r_w"):
        g_small[k] = lax.dynamic_slice_in_dim(g_small[k], me * c_loc, c_loc, axis=2)
    packs = [_pack([d[k] for k in SMALL_NAMES]) for d in (w, g_small, m, v)]
    res = _adamw_small(*packs, "adamw_small")
    like = [w[k] for k in SMALL_NAMES]
    d_s, m_s, v_s = (dict(zip(SMALL_NAMES, _unpack(r, like))) for r in res)

    grad, delta, new_m, new_v = {}, {}, {}, {}
    for k in WEIGHT_NAMES:
        if k in big:
            grad[k], delta[k], new_m[k], new_v[k] = big[k]
        else:
            grad[k], delta[k], new_m[k], new_v[k] = g_small[k], d_s[k], m_s[k], v_s[k]
    return (loss, grad_x[None], *[grad[k] for k in WEIGHT_NAMES], *[delta[k] for k in WEIGHT_NAMES],
            *[new_m[k] for k in WEIGHT_NAMES], *[new_v[k] for k in WEIGHT_NAMES])
```

```python
import functools
import math

import jax
import jax.numpy as jnp
from jax import lax
from jax.experimental import pallas as pl
from jax.experimental.pallas import tpu as pltpu

F32 = jnp.float32
MXU_DTYPE = jnp.bfloat16
WIRE_DTYPE = jnp.bfloat16
MESH = pl.DeviceIdType.MESH

N_DEV = 8
GROUP_W = 256
N_HEADS = 4
HEAD_DIM = 64
N_CHUNKS = 13
N_ABC = 9
GMLP_CHUNK = 128
ATTN_BLOCK = 128
ATTN_DILATIONS = (1, 4, 16)
NORM_EPS = 1e-6
RG_C = 8.0
SUBLANES = 8
LANES = 128
VMEM_LIMIT = 56 * 1024 * 1024

ADAM_LR = 0.001
ADAM_B1 = 0.9
ADAM_B2 = 0.999
ADAM_EPS = 1e-08
ADAM_WD = 0.01
ADAM_STEP = 10

TM_MIX = 256
TM_MM = 512


def _params(sem, vmem=VMEM_LIMIT):
    return pltpu.CompilerParams(dimension_semantics=sem, vmem_limit_bytes=vmem)


def _mm(a, b):
    return jnp.dot(a.astype(MXU_DTYPE), b.astype(MXU_DTYPE), preferred_element_type=F32)


def _mm_tn(a, b):
    return lax.dot_general(a.astype(MXU_DTYPE), b.astype(MXU_DTYPE), (((0,), (0,)), ((), ())),
                           preferred_element_type=F32)


def _mm_nt(a, b):
    return lax.dot_general(a.astype(MXU_DTYPE), b.astype(MXU_DTYPE), (((1,), (1,)), ((), ())),
                           preferred_element_type=F32)


def _sigmoid(x):
    return 1.0 / (1.0 + jnp.exp(-x))


def _silu_and_grad(x):
    s = _sigmoid(x)
    return x * s, s * (1.0 + x * (1.0 - s))


_GELU_K = math.sqrt(2.0 / math.pi)
_GELU_C = 0.044715


def _gelu_and_grad(x):
    x2 = x * x
    t = jnp.tanh(_GELU_K * (x + _GELU_C * x * x2))
    val = 0.5 * x * (1.0 + t)
    grad = 0.5 * (1.0 + t) + 0.5 * x * (1.0 - t * t) * (_GELU_K * (1.0 + 3.0 * _GELU_C * x2))
    return val, grad


def _gelu(x):
    return 0.5 * x * (1.0 + jnp.tanh(_GELU_K * (x + _GELU_C * x * x * x)))


def _expm1_nonpos(u):
    poly = u * (1.0 + u / 2.0 * (1.0 + u / 3.0 * (1.0 + u / 4.0 * (1.0 + u / 5.0 * (1.0 + u / 6.0 * (
        1.0 + u / 7.0 * (1.0 + u / 8.0)))))))
    return jnp.where(u > -0.25, poly, jnp.exp(u) - 1.0)


def _softplus(x):
    return jnp.maximum(x, 0.0) + jnp.log(1.0 + jnp.exp(-jnp.abs(x)))


def _shift_down(t, halo, k):
    rolled = pltpu.roll(t, k, 0)
    hr = pltpu.roll(halo, k, 0)
    row = lax.broadcasted_iota(jnp.int32, halo.shape, 0)
    first = jnp.where(row < k, hr, rolled[0:SUBLANES])
    return jnp.concatenate([first, rolled[SUBLANES:]], axis=0)


def _shift_up(t, nxt, k):
    tm = t.shape[0]
    rolled = pltpu.roll(t, tm - k, 0)
    nr = pltpu.roll(nxt, SUBLANES - k, 0)
    row = lax.broadcasted_iota(jnp.int32, nxt.shape, 0)
    last = jnp.where(row >= SUBLANES - k, nr, rolled[tm - SUBLANES:tm])
    return jnp.concatenate([rolled[:tm - SUBLANES], last], axis=0)


def _scan_fwd(a, b):
    tm = a.shape[0]
    row = lax.broadcasted_iota(jnp.int32, a.shape, 0)
    s = 1
    while s < tm:
        a_s = pltpu.roll(a, s, 0)
        b_s = pltpu.roll(b, s, 0)
        m = row >= s
        b = jnp.where(m, a * b_s + b, b)
        a = jnp.where(m, a * a_s, a)
        s *= 2
    return a, b


def _scan_rev(a, g):
    tm = a.shape[0]
    row = lax.broadcasted_iota(jnp.int32, a.shape, 0)
    s = 1
    while s < tm:
        a_s = pltpu.roll(a, tm - s, 0)
        g_s = pltpu.roll(g, tm - s, 0)
        m = row < tm - s
        g = jnp.where(m, g + a * g_s, g)
        a = jnp.where(m, a * a_s, a)
        s *= 2
    return g


def _lane_scratch(tm, w):
    return pltpu.VMEM((w // LANES, tm, LANES), F32)


def _put(scr_ref, val):
    for c in range(scr_ref.shape[0]):
        scr_ref[c] = val[:, c * LANES:(c + 1) * LANES].astype(F32)


def _get(scr_ref):
    return jnp.concatenate([scr_ref[c] for c in range(scr_ref.shape[0])], axis=1)


def _deinterleave(src_ref, dst_ref, dil):
    nc, tm, _ = src_ref.shape
    w = nc * LANES
    for r in range(dil):
        for c in range(nc):
            piece = src_ref[pl.ds(c, 1), pl.ds(r, tm // dil, stride=dil), :][0] if dil > 1 else src_ref[c]
            dst_ref[:, r * w + c * LANES:r * w + (c + 1) * LANES] = piece.astype(dst_ref.dtype)


def _interleave(src_ref, dst_ref, dil):
    nc, tm, _ = dst_ref.shape
    w = nc * LANES
    for r in range(dil):
        for c in range(nc):
            dst_ref[pl.ds(c, 1), pl.ds(r, tm // dil, stride=dil), :] = (
                src_ref[:, r * w + c * LANES:r * w + (c + 1) * LANES].astype(F32)[None])


def _dilated_spec(tm, w, dil, index=lambda i: i):
    return pl.BlockSpec((tm // dil, dil * w), lambda i: (index(i), 0))


def _dilated_shape(S, w, dil, dtype):
    return jax.ShapeDtypeStruct((S // dil, dil * w), dtype)


def _head_masks(shape):
    lane = lax.broadcasted_iota(jnp.int32, shape, 1)
    return [(lane >= h * HEAD_DIM) & (lane < (h + 1) * HEAD_DIM) for h in range(N_HEADS)]


def _colsum(v):
    return jnp.sum(v, axis=0, keepdims=True)


def _norm_inproj(x, g, w, name):
    S, D = x.shape
    N = w.shape[1]
    tm = TM_MM
    n_abc = N_ABC * GROUP_W
    n_qkv = 3 * GROUP_W

    def body(x_ref, g_ref, w_ref, zabc_ref, zg_ref, q1_ref, q4_ref, q16_ref, qkv_ref):
        xv = x_ref[...]
        r = lax.rsqrt(jnp.mean(xv * xv, axis=-1, keepdims=True) + NORM_EPS)
        h = ((xv * r) * g_ref[...]).astype(MXU_DTYPE)
        zabc_ref[...] = jnp.dot(h, w_ref[:, 0:n_abc], preferred_element_type=F32)
        _put(qkv_ref, jnp.dot(h, w_ref[:, n_abc:n_abc + n_qkv], preferred_element_type=F32))
        zg_ref[...] = jnp.dot(h, w_ref[:, n_abc + n_qkv:], preferred_element_type=F32)
        for dil, ref in zip(ATTN_DILATIONS, (q1_ref, q4_ref, q16_ref)):
            _deinterleave(qkv_ref, ref, dil)

    row = lambda wd: pl.BlockSpec((tm, wd), lambda i: (i, 0))
    return pl.pallas_call(
        body, name=name, grid=(S // tm,),
        in_specs=[row(D), pl.BlockSpec((1, D), lambda i: (0, 0)), pl.BlockSpec((D, N), lambda i: (0, 0))],
        out_specs=[row(n_abc), row(GROUP_W)] + [_dilated_spec(tm, n_qkv, dil) for dil in ATTN_DILATIONS],
        out_shape=[jax.ShapeDtypeStruct((S, n_abc), F32), jax.ShapeDtypeStruct((S, GROUP_W), F32)]
                  + [_dilated_shape(S, n_qkv, dil, MXU_DTYPE) for dil in ATTN_DILATIONS],
        scratch_shapes=[_lane_scratch(tm, n_qkv)],
        compiler_params=_params(("parallel",)),
    )(x, g, w)


def _conv_a(z_of, halo_of, w_ref):
    p = z_of(2) * z_of(0)
    p_h = halo_of(2) * halo_of(0)
    cv = w_ref[2:3, :] * p + w_ref[1:2, :] * _shift_down(p, p_h, 1) + w_ref[0:1, :] * _shift_down(p, p_h, 2)
    return p, p_h, cv


def _lru_gates(z_of, halo_of, wr_ref, vec_ref, wa_ref, wx_ref):
    rx = z_of(4)
    rx_h = halo_of(4)
    sh = [rx, _shift_down(rx, rx_h, 1), _shift_down(rx, rx_h, 2), _shift_down(rx, rx_h, 3)]
    xc = (wr_ref[3:4, :] * sh[0] + wr_ref[2:3, :] * sh[1] + wr_ref[1:2, :] * sh[2]
          + wr_ref[0:1, :] * sh[3] + vec_ref[0:1, :])
    ga = _sigmoid(jnp.dot(xc.astype(MXU_DTYPE), wa_ref[...], preferred_element_type=F32) + vec_ref[1:2, :])
    gi = _sigmoid(jnp.dot(xc.astype(MXU_DTYPE), wx_ref[...], preferred_element_type=F32) + vec_ref[2:3, :])
    sp = _softplus(-vec_ref[3:4, :])
    log_a = (-RG_C * ga) * sp
    a = jnp.exp(log_a)
    mult = jnp.sqrt(-_expm1_nonpos(2.0 * log_a))
    return xc, sh, ga, gi, a, mult, sp


def _gmlp_fwd(z_of, vec_ref, ws_ref, bs_ref, tm):
    u = _gelu(z_of(6))
    gv = _gelu(z_of(7))
    rr = lax.rsqrt(jnp.mean(gv * gv, axis=-1, keepdims=True) + NORM_EPS)
    vn = (gv * rr) * vec_ref[4:5, :]
    masks = _head_masks((GMLP_CHUNK, GROUP_W))
    parts = []
    for c in range(tm // GMLP_CHUNK):
        vc = vn[c * GMLP_CHUNK:(c + 1) * GMLP_CHUNK].astype(MXU_DTYPE)
        acc = bs_ref[...]
        for h in range(N_HEADS):
            acc = acc + jnp.where(masks[h], jnp.dot(ws_ref[h], vc, preferred_element_type=F32), 0.0)
        parts.append(acc)
    return u, gv, rr, vn, jnp.concatenate(parts, axis=0)


def _mix_specs(tm, S, order):
    const2 = lambda shape: pl.BlockSpec(shape, lambda i: (0, 0))
    return [const2((SUBLANES, GROUP_W)), const2((SUBLANES, GROUP_W)), const2((SUBLANES, GROUP_W)),
            const2((GROUP_W, GROUP_W)), const2((GROUP_W, GROUP_W)),
            pl.BlockSpec((N_HEADS, GMLP_CHUNK, GMLP_CHUNK), lambda i: (0, 0, 0)),
            const2((GMLP_CHUNK, GROUP_W))]


def _mix_fwd(z, mp, name):
    S = z.shape[0]
    tm = TM_MIX
    hb = tm // SUBLANES
    wcols = N_ABC * GROUP_W

    def body(z_ref, zh_ref, wA_ref, wR_ref, vec_ref, wa_ref, wx_ref, ws_ref, bs_ref, y_ref, h_ref, carry_ref):
        i = pl.program_id(0)

        @pl.when(i == 0)
        def _():
            carry_ref[...] = jnp.zeros_like(carry_ref)

        not_first = i > 0
        z_of = lambda c: z_ref[:, c * GROUP_W:(c + 1) * GROUP_W]
        halo_of = lambda c: jnp.where(not_first, zh_ref[:, c * GROUP_W:(c + 1) * GROUP_W], 0.0)

        _, _, cv = _conv_a(z_of, halo_of, wA_ref)
        y_ref[:, 0:GROUP_W] = z_of(1) * cv * _silu_and_grad(z_of(3))[0]

        xc, _, _, gi, a, mult, _ = _lru_gates(z_of, halo_of, wR_ref, vec_ref, wa_ref, wx_ref)
        b = mult * (gi * xc)
        acum, bcum = _scan_fwd(a, b)
        h = bcum + acum * carry_ref[SUBLANES - 1:SUBLANES, :]
        h_ref[...] = h
        carry_ref[...] = h[tm - SUBLANES:tm]
        y_ref[:, GROUP_W:2 * GROUP_W] = h * _silu_and_grad(z_of(5))[0]

        u, _, _, _, sp = _gmlp_fwd(z_of, vec_ref, ws_ref, bs_ref, tm)
        y_ref[:, 2 * GROUP_W:3 * GROUP_W] = u * sp * _silu_and_grad(z_of(8))[0]

    return pl.pallas_call(
        body, name=name, grid=(S // tm,),
        in_specs=[pl.BlockSpec((tm, wcols), lambda i: (i, 0)),
                  pl.BlockSpec((SUBLANES, wcols), lambda i: (jnp.maximum(i * hb - 1, 0), 0))]
                 + _mix_specs(tm, S, "fwd"),
        out_specs=[pl.BlockSpec((tm, 3 * GROUP_W), lambda i: (i, 0)),
                   pl.BlockSpec((tm, GROUP_W), lambda i: (i, 0))],
        out_shape=[jax.ShapeDtypeStruct((S, 3 * GROUP_W), F32), jax.ShapeDtypeStruct((S, GROUP_W), F32)],
        scratch_shapes=[pltpu.VMEM((SUBLANES, GROUP_W), F32)],
        compiler_params=_params(("arbitrary",)),
    )(z, z, mp["wA"], mp["wR"], mp["vec"], mp["wa"], mp["wx"], mp["ws"], mp["bs"])


def _attn_bias_and_mask(dil, which):
    qi = lax.broadcasted_iota(jnp.int32, (ATTN_BLOCK, ATTN_BLOCK), 0)
    ki = lax.broadcasted_iota(jnp.int32, (ATTN_BLOCK, ATTN_BLOCK), 1)
    if which == "cur":
        delta = qi - ki
    else:
        delta = qi + ATTN_BLOCK - ki
    valid = (delta >= 0) & (delta <= ATTN_BLOCK)
    dist = (delta * dil).astype(F32)
    return dist, valid


_NEG = -1e30


def _slope(h):
    return 2.0 ** (-8.0 * (h + 1) / N_HEADS)


def _attn_fwd(qkv, dil, name):
    rows = qkv.shape[0]
    nb = rows // ATTN_BLOCK
    scale = 1.0 / math.sqrt(HEAD_DIM)

    def body(q_ref, kc_ref, kp_ref, vc_ref, vp_ref, o_ref, l_ref):
        n = pl.program_id(1)
        has_prev = n > 0
        q = q_ref[...]
        kc, kp, vc, vp = kc_ref[...], kp_ref[...], vc_ref[...], vp_ref[...]
        masks = _head_masks((ATTN_BLOCK, GROUP_W))
        dist_c, valid_c = _attn_bias_and_mask(dil, "cur")
        dist_p, valid_p = _attn_bias_and_mask(dil, "prev")
        valid_p = valid_p & has_prev
        o_acc = jnp.zeros((ATTN_BLOCK, GROUP_W), F32)
        l_acc = jnp.zeros((ATTN_BLOCK, GROUP_W), F32)
        for h in range(N_HEADS):
            qm = jnp.where(masks[h], q, jnp.zeros_like(q))
            s_c = jnp.where(valid_c, _mm_nt(qm, kc) * scale - _slope(h) * dist_c, _NEG)
            s_p = jnp.where(valid_p, _mm_nt(qm, kp) * scale - _slope(h) * dist_p, _NEG)
            m = jnp.maximum(jnp.max(s_c, axis=-1, keepdims=True), jnp.max(s_p, axis=-1, keepdims=True))
            p_c = jnp.exp(s_c - m)
            p_p = jnp.exp(s_p - m)
            l = jnp.sum(p_c, axis=-1, keepdims=True) + jnp.sum(p_p, axis=-1, keepdims=True)
            o = (jnp.dot(p_c.astype(MXU_DTYPE), vc, preferred_element_type=F32)
                 + jnp.dot(p_p.astype(MXU_DTYPE), vp, preferred_element_type=F32)) / l
            o_acc = jnp.where(masks[h], o, o_acc)
            l_acc = jnp.where(masks[h], m + jnp.log(l), l_acc)
        o_ref[...] = o_acc
        l_ref[...] = l_acc

    blk = (ATTN_BLOCK, GROUP_W)
    cur = lambda c: pl.BlockSpec(blk, lambda r, n: (n, r * 3 + c))
    prev = lambda c: pl.BlockSpec(blk, lambda r, n: (jnp.maximum(n - 1, 0), r * 3 + c))
    out = pl.BlockSpec(blk, lambda r, n: (n, r))
    return pl.pallas_call(
        body, name=name, grid=(dil, nb),
        in_specs=[cur(0), cur(1), prev(1), cur(2), prev(2)],
        out_specs=[out, out],
        out_shape=[jax.ShapeDtypeStruct((rows, dil * GROUP_W), F32)] * 2,
        compiler_params=_params(("parallel", "parallel")),
    )(qkv, qkv, qkv, qkv, qkv)


def _outproj(x, z_g, y_abc, attn, w_out, name):
    S, D = x.shape
    tm = TM_MM
    n_abc = 3 * GROUP_W

    def body(x_ref, g_ref, yabc_ref, o1, l1, o2, l2, o3, l3, w_ref,
             xn_ref, y_ref, o_ref, lse1_ref, lse4_ref, lse16_ref, so2, sl2, so3, sl3, slse):
        for src, dst, dil in ((o2, so2, ATTN_DILATIONS[1]), (l2, sl2, ATTN_DILATIONS[1]),
                              (o3, so3, ATTN_DILATIONS[2]), (l3, sl3, ATTN_DILATIONS[2])):
            _interleave(src, dst, dil)
        la, lb, lc = l1[...], _get(sl2), _get(sl3)
        mx = jnp.maximum(jnp.maximum(la, lb), lc)
        ea, eb, ec = jnp.exp(la - mx), jnp.exp(lb - mx), jnp.exp(lc - mx)
        den = ea + eb + ec
        o = (ea * o1[...] + eb * _get(so2) + ec * _get(so3)) / den
        o_ref[...] = o
        _put(slse, mx + jnp.log(den))
        for dil, ref in zip(ATTN_DILATIONS, (lse1_ref, lse4_ref, lse16_ref)):
            _deinterleave(slse, ref, dil)
        y_d = o * _silu_and_grad(g_ref[...])[0]
        y_ref[:, 0:n_abc] = yabc_ref[...].astype(MXU_DTYPE)
        y_ref[:, n_abc:] = y_d.astype(MXU_DTYPE)
        xn_ref[...] = x_ref[...] + jnp.dot(y_ref[...], w_ref[...], preferred_element_type=F32)

    row = lambda w: pl.BlockSpec((tm, w), lambda i: (i, 0))
    dil_specs = [_dilated_spec(tm, GROUP_W, dil) for dil in ATTN_DILATIONS]
    (o1, l1), (o2, l2), (o3, l3) = attn
    return pl.pallas_call(
        body, name=name, grid=(S // tm,),
        in_specs=[row(D), row(GROUP_W), row(n_abc)] + [sp for sp in dil_specs for _ in range(2)]
                 + [pl.BlockSpec(w_out.shape, lambda i: (0, 0))],
        out_specs=[row(D), row(4 * GROUP_W), row(GROUP_W)] + dil_specs,
        out_shape=[jax.ShapeDtypeStruct((S, D), F32), jax.ShapeDtypeStruct((S, 4 * GROUP_W), MXU_DTYPE),
                   jax.ShapeDtypeStruct((S, GROUP_W), F32)]
                  + [_dilated_shape(S, GROUP_W, dil, F32) for dil in ATTN_DILATIONS],
        scratch_shapes=[_lane_scratch(tm, GROUP_W)] * 5,
        compiler_params=_params(("parallel",)),
    )(x, z_g, y_abc, o1, l1, o2, l2, o3, l3, w_out)


def _loss_head(x, g, target, name):
    S, D = x.shape
    tm = TM_MM

    def body(x_ref, g_ref, t_ref, dx_ref, loss_ref, dg_ref):
        i = pl.program_id(0)

        @pl.when(i == 0)
        def _():
            loss_ref[...] = jnp.zeros_like(loss_ref)
            dg_ref[...] = jnp.zeros_like(dg_ref)

        xv = x_ref[...]
        r = lax.rsqrt(jnp.mean(xv * xv, axis=-1, keepdims=True) + NORM_EPS)
        xn = xv * r
        err = xn * g_ref[...] - t_ref[...]
        per_tok = jnp.mean(err * err, axis=-1, keepdims=True)
        loss_ref[...] += 0.5 * jnp.sum(per_tok, axis=0, keepdims=True)
        dout = err * (1.0 / D)
        dg_ref[...] += _colsum(dout * xn)
        dxn = dout * g_ref[...]
        dx_ref[...] = r * (dxn - xn * jnp.mean(dxn * xn, axis=-1, keepdims=True))

    row = pl.BlockSpec((tm, D), lambda i: (i, 0))
    return pl.pallas_call(
        body, name=name, grid=(S // tm,),
        in_specs=[row, pl.BlockSpec((1, D), lambda i: (0, 0)), row],
        out_specs=[row, pl.BlockSpec((1, LANES), lambda i: (0, 0)), pl.BlockSpec((1, D), lambda i: (0, 0))],
        out_shape=[jax.ShapeDtypeStruct((S, D), F32), jax.ShapeDtypeStruct((1, LANES), F32),
                   jax.ShapeDtypeStruct((1, D), F32)],
        compiler_params=_params(("arbitrary",)),
    )(x, g, target)


def _outproj_bwd(dx, y, w_out, name):
    S, D = dx.shape
    E = y.shape[1]
    tm = TM_MM

    def body(dx_ref, y_ref, w_ref, dy_ref, dw_ref, acc_ref):
        i = pl.program_id(0)

        @pl.when(i == 0)
        def _():
            acc_ref[...] = jnp.zeros_like(acc_ref)

        dxb = dx_ref[...].astype(MXU_DTYPE)
        dy_ref[...] = _mm_nt(dxb, w_ref[...])
        acc_ref[...] += _mm_tn(y_ref[...], dxb)

        @pl.when(i == S // tm - 1)
        def _():
            dw_ref[...] = acc_ref[...].astype(dw_ref.dtype)

    return pl.pallas_call(
        body, name=name, grid=(S // tm,),
        in_specs=[pl.BlockSpec((tm, D), lambda i: (i, 0)), pl.BlockSpec((tm, E), lambda i: (i, 0)),
                  pl.BlockSpec((E, D), lambda i: (0, 0))],
        out_specs=[pl.BlockSpec((tm, E), lambda i: (i, 0)), pl.BlockSpec((E, D), lambda i: (0, 0))],
        out_shape=[jax.ShapeDtypeStruct((S, E), F32), jax.ShapeDtypeStruct((E, D), WIRE_DTYPE)],
        scratch_shapes=[pltpu.VMEM((E, D), F32)],
        compiler_params=_params(("arbitrary",)),
    )(dx, y, w_out)


def _mix_bwd(z, z_g, dy, hs, o, mp, name):
    S = z.shape[0]
    tm = TM_MIX
    hb = tm // SUBLANES
    nT = S // tm
    last_blk = S // SUBLANES - 1
    wcols = N_ABC * GROUP_W

    def body(z_ref, zh_ref, zn_ref, zg_ref, dy_ref, dyn_ref, h_ref, hh_ref, o_ref,
             wA_ref, wR_ref, vec_ref, wa_ref, wx_ref, ws_ref, bs_ref,
             dz_ref, dzg_ref, do1_ref, do4_ref, do16_ref, dl1_ref, dl4_ref, dl16_ref,
             dwA_ref, dwR_ref, dvec_ref, dwa_ref, dwx_ref, dws_ref, dbs_ref,
             hcarry_ref, xcarry_ref, bsacc_ref, do_ref, dl_ref):
        i = pl.program_id(0)
        ti = nT - 1 - i

        @pl.when(i == 0)
        def _():
            hcarry_ref[...] = jnp.zeros_like(hcarry_ref)
            xcarry_ref[...] = jnp.zeros_like(xcarry_ref)
            bsacc_ref[...] = jnp.zeros_like(bsacc_ref)
            dwA_ref[...] = jnp.zeros_like(dwA_ref)
            dwR_ref[...] = jnp.zeros_like(dwR_ref)
            dvec_ref[...] = jnp.zeros_like(dvec_ref)
            dwa_ref[...] = jnp.zeros_like(dwa_ref)
            dwx_ref[...] = jnp.zeros_like(dwx_ref)
            dws_ref[...] = jnp.zeros_like(dws_ref)
            dbs_ref[...] = jnp.zeros_like(dbs_ref)

        has_prev = ti > 0
        has_next = i > 0
        col = lambda c: slice(c * GROUP_W, (c + 1) * GROUP_W)
        z_of = lambda c: z_ref[:, col(c)]
        halo_of = lambda c: jnp.where(has_prev, zh_ref[:, col(c)], 0.0)
        next_of = lambda c: zn_ref[:, col(c)]

        p, p_h, cv = _conv_a(z_of, halo_of, wA_ref)
        sg, dsg = _silu_and_grad(z_of(3))
        a_b = z_of(1)
        dya = dy_ref[:, col(0)]
        dcv = dya * a_b * sg
        dcv_n = jnp.where(has_next, dyn_ref[...] * next_of(1) * _silu_and_grad(next_of(3))[0], 0.0)
        dp = (wA_ref[2:3, :] * dcv + wA_ref[1:2, :] * _shift_up(dcv, dcv_n, 1)
              + wA_ref[0:1, :] * _shift_up(dcv, dcv_n, 2))
        dwA_ref[2:3, :] += _colsum(dcv * p)
        dwA_ref[1:2, :] += _colsum(dcv * _shift_down(p, p_h, 1))
        dwA_ref[0:1, :] += _colsum(dcv * _shift_down(p, p_h, 2))
        dz_ref[:, col(0)] = dp * z_of(2)
        dz_ref[:, col(1)] = dya * cv * sg
        dz_ref[:, col(2)] = dp * z_of(0)
        dz_ref[:, col(3)] = dya * a_b * cv * dsg

        xc, sh, ga, gi, a, mult, sp = _lru_gates(z_of, halo_of, wR_ref, vec_ref, wa_ref, wx_ref)
        h = h_ref[...]
        h_prev = _shift_down(h, jnp.where(has_prev, hh_ref[...], 0.0), 1)
        sgr, dsgr = _silu_and_grad(z_of(5))
        dyb = dy_ref[:, col(1)]
        dz_ref[:, col(5)] = dyb * h * dsgr
        row = lax.broadcasted_iota(jnp.int32, (tm, GROUP_W), 0)
        g_in = dyb * sgr + jnp.where(row == tm - 1, hcarry_ref[0:1, :], 0.0)
        a_up = _shift_up(a, jnp.zeros((SUBLANES, GROUP_W), F32), 1)
        dH = _scan_rev(a_up, g_in)
        hcarry_ref[...] = (a * dH)[0:SUBLANES]
        da = dH * h_prev
        gx = gi * xc
        dmult = dH * gx
        dgi = dH * mult * xc
        dxc = dH * mult * gi
        dlog_a = da * a - dmult * (a * a) / mult
        dga = dlog_a * (-RG_C * sp)
        dlam_row = _colsum(dlog_a * (-RG_C * ga)) * (-_sigmoid(-vec_ref[3:4, :]))
        dpre_a = dga * ga * (1.0 - ga)
        dpre_i = dgi * gi * (1.0 - gi)
        dwa_ref[...] += _mm_tn(xc, dpre_a)
        dwx_ref[...] += _mm_tn(xc, dpre_i)
        dxc = dxc + _mm_nt(dpre_a, wa_ref[...]) + _mm_nt(dpre_i, wx_ref[...])
        dvec_ref[0:1, :] += _colsum(dxc)
        dvec_ref[1:2, :] += _colsum(dpre_a)
        dvec_ref[2:3, :] += _colsum(dpre_i)
        dvec_ref[3:4, :] += dlam_row
        for k in range(4):
            dwR_ref[k:k + 1, :] += _colsum(dxc * sh[3 - k])
        dxc_n = xcarry_ref[...]
        dz_ref[:, col(4)] = (wR_ref[3:4, :] * dxc + wR_ref[2:3, :] * _shift_up(dxc, dxc_n, 1)
                             + wR_ref[1:2, :] * _shift_up(dxc, dxc_n, 2)
                             + wR_ref[0:1, :] * _shift_up(dxc, dxc_n, 3))
        xcarry_ref[...] = dxc[0:SUBLANES]

        c_u, c_v = z_of(6), z_of(7)
        u, du_dx = _gelu_and_grad(c_u)
        gv, dgv_dx = _gelu_and_grad(c_v)
        rr = lax.rsqrt(jnp.mean(gv * gv, axis=-1, keepdims=True) + NORM_EPS)
        xhat = gv * rr
        g_c = vec_ref[4:5, :]
        vn = xhat * g_c
        masks = _head_masks((GMLP_CHUNK, GROUP_W))
        tri_r = lax.broadcasted_iota(jnp.int32, (GMLP_CHUNK, GMLP_CHUNK), 0)
        tri_c = lax.broadcasted_iota(jnp.int32, (GMLP_CHUNK, GMLP_CHUNK), 1)
        tril = tri_r >= tri_c
        sgc, dsgc = _silu_and_grad(z_of(8))
        dyc = dy_ref[:, col(2)]
        dsp_full = dyc * u * sgc
        sp_parts, dvn_parts = [], []
        for c in range(tm // GMLP_CHUNK):
            rs = slice(c * GMLP_CHUNK, (c + 1) * GMLP_CHUNK)
            vc = vn[rs].astype(MXU_DTYPE)
            dsp_c = dsp_full[rs]
            bsacc_ref[...] += dsp_c
            acc = bs_ref[...]
            dvn_c = jnp.zeros((GMLP_CHUNK, GROUP_W), F32)
            for h in range(N_HEADS):
                w_h = ws_ref[h]
                acc = acc + jnp.where(masks[h], jnp.dot(w_h, vc, preferred_element_type=F32), 0.0)
                dsp_h = jnp.where(masks[h], dsp_c, 0.0).astype(MXU_DTYPE)
                dvn_c = dvn_c + _mm_tn(w_h, dsp_h)
                dws_ref[h] += jnp.where(tril, _mm_nt(dsp_h, vc), 0.0)
            sp_parts.append(acc)
            dvn_parts.append(dvn_c)
        spv = jnp.concatenate(sp_parts, axis=0)
        dvn = jnp.concatenate(dvn_parts, axis=0)
        dz_ref[:, col(6)] = dyc * spv * sgc * du_dx
        dz_ref[:, col(8)] = dyc * u * spv * dsgc
        dvec_ref[4:5, :] += _colsum(dvn * xhat)
        dgvn = dvn * g_c
        dgv = rr * (dgvn - xhat * jnp.mean(dgvn * xhat, axis=-1, keepdims=True))
        dz_ref[:, col(7)] = dgv * dgv_dx

        sgd, dsgd = _silu_and_grad(zg_ref[...])
        dyd = dy_ref[:, col(3)]
        ov = o_ref[...]
        do = dyd * sgd
        _put(do_ref, do)
        dzg_ref[...] = dyd * ov * dsgd
        prod = do * ov
        tmasks = _head_masks((tm, GROUP_W))
        dl = jnp.zeros((tm, GROUP_W), F32)
        for h in range(N_HEADS):
            dl = jnp.where(tmasks[h], jnp.sum(jnp.where(tmasks[h], prod, 0.0), axis=-1, keepdims=True), dl)
        _put(dl_ref, dl)
        for dil, d_out, l_out in zip(ATTN_DILATIONS, (do1_ref, do4_ref, do16_ref), (dl1_ref, dl4_ref, dl16_ref)):
            _deinterleave(do_ref, d_out, dil)
            _deinterleave(dl_ref, l_out, dil)

        @pl.when(i == nT - 1)
        def _():
            acc = bsacc_ref[...]
            lane = lax.broadcasted_iota(jnp.int32, (GMLP_CHUNK, LANES), 1)
            out = jnp.zeros((GMLP_CHUNK, LANES), F32)
            for h in range(N_HEADS):
                out = jnp.where(lane == h, jnp.sum(jnp.where(masks[h], acc, 0.0), axis=-1, keepdims=True), out)
            dbs_ref[...] = out

    rev = lambda w: pl.BlockSpec((tm, w), lambda i: (nT - 1 - i, 0))
    prev8 = lambda w: pl.BlockSpec((SUBLANES, w), lambda i: (jnp.maximum((nT - 1 - i) * hb - 1, 0), 0))
    next8 = lambda w: pl.BlockSpec((SUBLANES, w), lambda i: (jnp.minimum((nT - i) * hb, last_blk), 0))
    const2 = lambda shape: pl.BlockSpec(shape, lambda i: (0, 0))
    dil_specs = [_dilated_spec(tm, GROUP_W, dil, lambda i: nT - 1 - i) for dil in ATTN_DILATIONS]
    dil_shapes = [_dilated_shape(S, GROUP_W, dil, F32) for dil in ATTN_DILATIONS]
    small = (SUBLANES, GROUP_W)
    sq = (GROUP_W, GROUP_W)
    ws_shape = (N_HEADS, GMLP_CHUNK, GMLP_CHUNK)
    return pl.pallas_call(
        body, name=name, grid=(nT,),
        in_specs=[rev(wcols), prev8(wcols), next8(wcols), rev(GROUP_W),
                  rev(4 * GROUP_W), next8(GROUP_W), rev(GROUP_W), prev8(GROUP_W), rev(GROUP_W)]
                 + _mix_specs(tm, S, "bwd"),
        out_specs=[rev(wcols), rev(GROUP_W)] + dil_specs + dil_specs
                  + [const2(small), const2(small), const2(small), const2(sq), const2(sq),
                     pl.BlockSpec(ws_shape, lambda i: (0, 0, 0)), const2((GMLP_CHUNK, LANES))],
        out_shape=[jax.ShapeDtypeStruct((S, wcols), F32), jax.ShapeDtypeStruct((S, GROUP_W), F32)]
                  + dil_shapes + dil_shapes
                  + [jax.ShapeDtypeStruct(small, F32)] * 3 + [jax.ShapeDtypeStruct(sq, F32)] * 2
                  + [jax.ShapeDtypeStruct(ws_shape, F32), jax.ShapeDtypeStruct((GMLP_CHUNK, LANES), F32)],
        scratch_shapes=[pltpu.VMEM(small, F32), pltpu.VMEM(small, F32), pltpu.VMEM((GMLP_CHUNK, GROUP_W), F32),
                        _lane_scratch(tm, GROUP_W), _lane_scratch(tm, GROUP_W)],
        compiler_params=_params(("arbitrary",)),
    )(z, z, z, z_g, dy, dy, hs, hs, o, mp["wA"], mp["wR"], mp["vec"], mp["wa"], mp["wx"], mp["ws"], mp["bs"])


def _attn_bwd(qkv, do, lse, delta, dil, name):
    rows = qkv.shape[0]
    nb = rows // ATTN_BLOCK
    scale = 1.0 / math.sqrt(HEAD_DIM)

    def body(qc_ref, qn_ref, k_ref, v_ref, doc_ref, don_ref, lc_ref, ln_ref, dc_ref, dn_ref,
             dq_ref, dk_ref, dv_ref, carry_ref):
        n = pl.program_id(1)

        @pl.when(n == 0)
        def _():
            carry_ref[...] = jnp.zeros_like(carry_ref)

        has_next = n < nb - 1
        kb = k_ref[...]
        vb = v_ref[...]
        masks = _head_masks((ATTN_BLOCK, GROUP_W))
        sides = []
        for which, q_ref, d_ref, l_ref, t_ref in (("cur", qc_ref, doc_ref, lc_ref, dc_ref),
                                                   ("prev", qn_ref, don_ref, ln_ref, dn_ref)):
            dist, valid = _attn_bias_and_mask(dil, which)
            if which == "prev":
                valid = valid & has_next
            sides.append((q_ref[...], d_ref[...], l_ref[...], t_ref[...], dist, valid))
        dq_parts = [jnp.zeros((ATTN_BLOCK, GROUP_W), F32), jnp.zeros((ATTN_BLOCK, GROUP_W), F32)]
        dk = jnp.zeros((ATTN_BLOCK, GROUP_W), F32)
        dv = jnp.zeros((ATTN_BLOCK, GROUP_W), F32)
        for h in range(N_HEADS):
            for side, (q, dov, lv, tv, dist, valid) in enumerate(sides):
                qm = jnp.where(masks[h], q, jnp.zeros_like(q))
                dom = jnp.where(masks[h], dov, 0.0).astype(MXU_DTYPE)
                lse_h = jnp.max(jnp.where(masks[h], lv, _NEG), axis=-1, keepdims=True)
                dl_h = jnp.max(jnp.where(masks[h], tv, _NEG), axis=-1, keepdims=True)
                s = _mm_nt(qm, kb) * scale - _slope(h) * dist
                p = jnp.where(valid, jnp.exp(jnp.where(valid, s, _NEG) - lse_h), 0.0)
                dp = _mm_nt(dom, vb)
                ds = (p * (dp - dl_h) * scale).astype(MXU_DTYPE)
                pb = p.astype(MXU_DTYPE)
                dv = dv + _mm_tn(pb, dom)
                dk = dk + _mm_tn(ds, qm)
                dq_parts[side] = dq_parts[side] + jnp.where(masks[h], jnp.dot(ds, kb, preferred_element_type=F32), 0.0)
        dq_ref[...] = carry_ref[...] + dq_parts[0]
        carry_ref[...] = dq_parts[1]
        dk_ref[...] = dk
        dv_ref[...] = dv

    blk = (ATTN_BLOCK, GROUP_W)
    zcur = lambda c: pl.BlockSpec(blk, lambda r, n: (n, r * 3 + c))
    znext = lambda c: pl.BlockSpec(blk, lambda r, n: (jnp.minimum(n + 1, nb - 1), r * 3 + c))
    cur = pl.BlockSpec(blk, lambda r, n: (n, r))
    nxt = pl.BlockSpec(blk, lambda r, n: (jnp.minimum(n + 1, nb - 1), r))
    return pl.pallas_call(
        body, name=name, grid=(dil, nb),
        in_specs=[zcur(0), znext(0), zcur(1), zcur(2), cur, nxt, cur, nxt, cur, nxt],
        out_specs=[cur, cur, cur],
        out_shape=[jax.ShapeDtypeStruct((rows, dil * GROUP_W), F32)] * 3,
        scratch_shapes=[pltpu.VMEM(blk, F32)],
        compiler_params=_params(("arbitrary", "arbitrary")),
    )(qkv, qkv, qkv, qkv, do, do, lse, lse, delta, delta)


def _inproj_bwd(x, g, dxn, dz_abc, dqkv, dz_g, w_in, name):
    S, D = x.shape
    N = w_in.shape[1]
    tm = TM_MM
    n_abc = N_ABC * GROUP_W

    def body(x_ref, g_ref, dxn_ref, dabc_ref, q1, k1, v1, q2, k2, v2, q3, k3, v3, dg_ref, w_ref,
             dx_ref, dz_ref, h_ref, dgn_ref, s4_ref, s16_ref):
        i = pl.program_id(0)

        @pl.when(i == 0)
        def _():
            dgn_ref[...] = jnp.zeros_like(dgn_ref)

        dz_ref[:, 0:n_abc] = dabc_ref[...].astype(MXU_DTYPE)
        for j, parts in enumerate(((q1, q2, q3), (k1, k2, k3), (v1, v2, v3))):
            c0 = n_abc + j * GROUP_W
            _interleave(parts[1], s4_ref, ATTN_DILATIONS[1])
            _interleave(parts[2], s16_ref, ATTN_DILATIONS[2])
            dz_ref[:, c0:c0 + GROUP_W] = (parts[0][...] + _get(s4_ref) + _get(s16_ref)).astype(MXU_DTYPE)
        dz_ref[:, n_abc + 3 * GROUP_W:] = dg_ref[...].astype(MXU_DTYPE)
        dh = _mm_nt(dz_ref[...], w_ref[...])
        xv = x_ref[...]
        r = lax.rsqrt(jnp.mean(xv * xv, axis=-1, keepdims=True) + NORM_EPS)
        xn = xv * r
        gv = g_ref[...]
        h_ref[...] = (xn * gv).astype(MXU_DTYPE)
        dgn_ref[...] += _colsum(dh * xn)
        dn = dh * gv
        dx_ref[...] = dxn_ref[...] + r * (dn - xn * jnp.mean(dn * xn, axis=-1, keepdims=True))

    row = lambda w: pl.BlockSpec((tm, w), lambda i: (i, 0))
    flat = [t for p in dqkv for t in p]
    dil_specs = [_dilated_spec(tm, GROUP_W, dil) for dil in ATTN_DILATIONS for _ in range(3)]
    return pl.pallas_call(
        body, name=name, grid=(S // tm,),
        in_specs=[row(D), pl.BlockSpec((1, D), lambda i: (0, 0)), row(D), row(n_abc)] + dil_specs
                 + [row(GROUP_W), pl.BlockSpec((D, N), lambda i: (0, 0))],
        out_specs=[row(D), row(N), row(D), pl.BlockSpec((1, D), lambda i: (0, 0))],
        out_shape=[jax.ShapeDtypeStruct((S, D), F32), jax.ShapeDtypeStruct((S, N), MXU_DTYPE),
                   jax.ShapeDtypeStruct((S, D), MXU_DTYPE), jax.ShapeDtypeStruct((1, D), F32)],
        scratch_shapes=[_lane_scratch(tm, GROUP_W)] * 2,
        compiler_params=_params(("arbitrary",)),
    )(x, g, dxn, dz_abc, *flat, dz_g, w_in)


def _inproj_wgrad(h, dz, name):
    S, D = h.shape
    N = dz.shape[1]
    tm = TM_MM
    nj = 2
    cw = N // nj
    per = N_DEV // nj
    n_loc = N // N_DEV

    def body(h_ref, dz_ref, dw_ref, acc_ref):
        i = pl.program_id(1)

        @pl.when(i == 0)
        def _():
            acc_ref[...] = jnp.zeros_like(acc_ref)

        acc_ref[...] += _mm_tn(h_ref[...], dz_ref[...])

        @pl.when(i == S // tm - 1)
        def _():
            for b in range(per):
                dw_ref[b] = acc_ref[:, b * n_loc:(b + 1) * n_loc].astype(dw_ref.dtype)

    return pl.pallas_call(
        body, name=name, grid=(nj, S // tm),
        in_specs=[pl.BlockSpec((tm, D), lambda j, i: (i, 0)), pl.BlockSpec((tm, cw), lambda j, i: (i, j))],
        out_specs=pl.BlockSpec((per, D, n_loc), lambda j, i: (j, 0, 0)),
        out_shape=jax.ShapeDtypeStruct((N_DEV, D, n_loc), WIRE_DTYPE),
        scratch_shapes=[pltpu.VMEM((D, cw), F32)],
        compiler_params=_params(("parallel", "arbitrary")),
    )(h, dz)


def _my_place():
    return lax.axis_index("x"), lax.axis_index("y"), lax.axis_index("c")


def _peer(x, y, c, k):
    px = 1 - x if k & 4 else x
    py = 1 - y if k & 2 else y
    pc = 1 - c if k & 1 else c
    return (px, py, pc), 4 * px + 2 * py + pc


HBM_SPEC = pl.BlockSpec(memory_space=pltpu.HBM)
SEM_SPEC = pl.BlockSpec(memory_space=pltpu.SEMAPHORE)
SPLIT_EFFECT = pltpu.SideEffectType.DATAFLOW_SIDE_EFFECTING
N_PEERS = N_DEV - 1


def _exchange_copies(srcs, lands, send_sems, recv_sems, gather, arrival):
    x, y, c = _my_place()
    me = 4 * x + 2 * y + c
    copies = []
    for t in range(len(srcs)):
        for k in range(1, N_DEV):
            peer, pidx = _peer(x, y, c, k)
            copies.append(pltpu.make_async_remote_copy(
                src_ref=srcs[t] if gather else srcs[t].at[pidx],
                dst_ref=lands[t].at[pidx if arrival else me], send_sem=send_sems.at[t * N_PEERS + k - 1],
                recv_sem=recv_sems.at[t * N_PEERS + k - 1], device_id=peer, device_id_type=MESH))
    return copies


def _exchange_start(groups, gather, name):
    sizes = [len(g) for g in groups]
    srcs = [pltpu.with_memory_space_constraint(a, pltpu.HBM) for g in groups for a in g]
    land_shape = lambda a: ((N_DEV,) + a.shape) if gather else a.shape
    lands = [pltpu.with_memory_space_constraint(lax.empty(land_shape(a), a.dtype), pltpu.HBM) for a in srcs]
    n = len(srcs)
    n_g = len(groups)

    def body(*refs):
        src_refs, land_refs = refs[:n], refs[n:2 * n]
        sem_refs = refs[4 * n:4 * n + 2 * n_g]
        token = refs[-1]
        off = 0
        for gi, sz in enumerate(sizes):
            for send in _exchange_copies(src_refs[off:off + sz], land_refs[off:off + sz],
                                         sem_refs[2 * gi], sem_refs[2 * gi + 1], gather, False):
                send.start()
            off += sz
        token[...] = jnp.zeros_like(token)

    sem_shapes = [pltpu.SemaphoreType.DMA((sz * N_PEERS,)) for sz in sizes for _ in range(2)]
    outs = pl.pallas_call(
        body, name=name,
        in_specs=[HBM_SPEC] * (2 * n),
        out_specs=[HBM_SPEC] * (2 * n) + [SEM_SPEC] * (2 * n_g) + [pl.BlockSpec(memory_space=pltpu.VMEM)],
        out_shape=[pltpu.HBM(a.shape, a.dtype) for a in srcs + lands] + sem_shapes
                  + [jax.ShapeDtypeStruct((SUBLANES, LANES), F32)],
        input_output_aliases={i: i for i in range(2 * n)},
        compiler_params=pltpu.CompilerParams(has_side_effects=SPLIT_EFFECT),
    )(*srcs, *lands)
    handles, off = [], 0
    for gi, sz in enumerate(sizes):
        handles.append((outs[2 * n + 2 * gi], outs[2 * n + 2 * gi + 1], outs[off:off + sz], outs[n + off:n + off + sz]))
        off += sz
    return handles, outs[-1]


def _exchange_wait(handle, after, gather, name):
    send_sems, recv_sems, srcs, lands = handle
    n = len(srcs)

    def body(*refs):
        src_refs, land_refs = refs[:n], refs[n:2 * n]
        for send in _exchange_copies(src_refs, land_refs, refs[2 * n], refs[2 * n + 1], gather, False):
            send.wait_send()
        for arrival in _exchange_copies(src_refs, land_refs, refs[2 * n], refs[2 * n + 1], gather, True):
            arrival.wait_recv()

    outs = pl.pallas_call(
        body, name=name,
        in_specs=[HBM_SPEC] * (2 * n) + [SEM_SPEC, SEM_SPEC, pl.BlockSpec(memory_space=pl.ANY)],
        out_specs=[HBM_SPEC] * (2 * n),
        out_shape=[pltpu.HBM(a.shape, a.dtype) for a in list(srcs) + list(lands)],
        input_output_aliases={i: i for i in range(2 * n)},
        compiler_params=pltpu.CompilerParams(has_side_effects=SPLIT_EFFECT),
    )(*srcs, *lands, send_sems, recv_sems, after)
    return outs[:n], outs[n:]


def _allreduce(buf, name):
    R = buf.shape[0]

    def body(x_ref, out_ref, recv_ref, send_sems, recv_sems):
        x, y, c = _my_place()
        out_ref[...] = x_ref[...]
        for s, k in enumerate((1, 4, 2)):
            peer, _ = _peer(x, y, c, k)
            cp = pltpu.make_async_remote_copy(
                src_ref=out_ref, dst_ref=recv_ref.at[s], send_sem=send_sems.at[s], recv_sem=recv_sems.at[s],
                device_id=peer, device_id_type=MESH)
            cp.start()
            cp.wait()
            out_ref[...] = out_ref[...] + recv_ref[s]

    vm = pl.BlockSpec(memory_space=pltpu.VMEM)
    return pl.pallas_call(
        body, name=name, in_specs=[vm], out_specs=vm,
        out_shape=jax.ShapeDtypeStruct((R, LANES), F32),
        scratch_shapes=[pltpu.VMEM((3, R, LANES), F32), pltpu.SemaphoreType.DMA((3,)), pltpu.SemaphoreType.DMA((3,))],
        compiler_params=pltpu.CompilerParams(has_side_effects=True, vmem_limit_bytes=VMEM_LIMIT),
    )(buf)


def _adamw_math(w, g, m, v):
    m = ADAM_B1 * m + (1.0 - ADAM_B1) * g
    v = ADAM_B2 * v + (1.0 - ADAM_B2) * (g * g)
    m_hat = m / (1.0 - ADAM_B1 ** ADAM_STEP)
    v_hat = v / (1.0 - ADAM_B2 ** ADAM_STEP)
    delta = -ADAM_LR * (m_hat / (jnp.sqrt(v_hat) + ADAM_EPS) + ADAM_WD * w)
    return delta, m, v


def _adamw_summed(parts, w, m, v, tr, name):
    R, C = w.shape

    def body(p_ref, w_ref, m_ref, v_ref, g_ref, d_ref, nm_ref, nv_ref):
        g = p_ref[0].astype(F32)
        for j in range(1, N_DEV):
            g = g + p_ref[j].astype(F32)
        g_ref[...] = g
        d_ref[...], nm_ref[...], nv_ref[...] = _adamw_math(w_ref[...], g, m_ref[...], v_ref[...])

    row = pl.BlockSpec((tr, C), lambda i: (i, 0))
    return pl.pallas_call(
        body, name=name, grid=(R // tr,),
        in_specs=[pl.BlockSpec((N_DEV, tr, C), lambda i: (0, i, 0)), row, row, row],
        out_specs=[row] * 4, out_shape=[jax.ShapeDtypeStruct((R, C), F32)] * 4,
        compiler_params=_params(("parallel",)),
    )(parts, w, m, v)


def _adamw_small(w, g, m, v, name):
    def body(w_ref, g_ref, m_ref, v_ref, d_ref, nm_ref, nv_ref):
        d_ref[...], nm_ref[...], nv_ref[...] = _adamw_math(w_ref[...], g_ref[...], m_ref[...], v_ref[...])

    vm = pl.BlockSpec(memory_space=pltpu.VMEM)
    return pl.pallas_call(
        body, name=name, in_specs=[vm] * 4, out_specs=[vm] * 3,
        out_shape=[jax.ShapeDtypeStruct(w.shape, F32)] * 3,
        compiler_params=pltpu.CompilerParams(vmem_limit_bytes=VMEM_LIMIT),
    )(w, g, m, v)


def _pack(arrays):
    flat = jnp.concatenate([a.reshape(-1) for a in arrays])
    pad = (-flat.shape[0]) % (SUBLANES * LANES)
    return jnp.pad(flat, (0, pad)).reshape(-1, LANES)


def _unpack(buf, like):
    flat = buf.reshape(-1)
    out, off = [], 0
    for a in like:
        out.append(flat[off:off + a.size].reshape(a.shape))
        off += a.size
    return out


def _block_diag(w):
    eye = jnp.eye(N_HEADS, dtype=w.dtype)
    return jnp.einsum('hij,hk->hikj', w, eye).reshape(GROUP_W, GROUP_W)


def _diag_blocks(w):
    return jnp.einsum('hihj->hij', w.reshape(N_HEADS, HEAD_DIM, N_HEADS, HEAD_DIM))


def _pad_rows(a):
    return jnp.pad(a, ((0, SUBLANES - a.shape[0]), (0, 0)))


def _mixer_params(l, conv_a_w, conv_r_w, conv_r_b, lru_wa, lru_ba, lru_wx, lru_bx, lru_lambda, gmlp_norm_g,
                  gmlp_ws, gmlp_bs):
    tril = jnp.tril(jnp.ones((GMLP_CHUNK, GMLP_CHUNK), dtype=bool))
    vec = jnp.stack([conv_r_b[l], lru_ba[l], lru_bx[l], lru_lambda[l], gmlp_norm_g[l]])
    return {
        "wA": _pad_rows(conv_a_w[l]), "wR": _pad_rows(conv_r_w[l]), "vec": _pad_rows(vec),
        "wa": _block_diag(lru_wa[l]).astype(MXU_DTYPE), "wx": _block_diag(lru_wx[l]).astype(MXU_DTYPE),
        "ws": jnp.where(tril[None], gmlp_ws[l], 0.0).astype(MXU_DTYPE),
        "bs": jnp.repeat(jnp.transpose(gmlp_bs[l]), HEAD_DIM, axis=1),
    }


def _local_step(x, loss_target, norm_g, get_weights, emit_wgrads, conv_a_w, conv_r_w, conv_r_b, lru_wa, lru_ba,
                lru_wx, lru_bx, lru_lambda, gmlp_norm_g, gmlp_ws, gmlp_bs, final_g):
    depth = norm_g.shape[0]
    D = x.shape[1]
    small = (conv_a_w, conv_r_w, conv_r_b, lru_wa, lru_ba, lru_wx, lru_bx, lru_lambda, gmlp_norm_g, gmlp_ws, gmlp_bs)
    saved = []
    for l in range(depth):
        mp = _mixer_params(l, *small)
        w_in_l, w_out_l = get_weights(l, x)
        z, z_g, *qkv = _norm_inproj(x, norm_g[l].reshape(1, D), w_in_l, f"norm_inproj_{l}")
        y_abc, hs = _mix_fwd(z, mp, f"mix_fwd_{l}")
        attn = [_attn_fwd(qkv[p], dil, f"attn_fwd_d{dil}_{l}") for p, dil in enumerate(ATTN_DILATIONS)]
        x_new, y, o, *lse = _outproj(x, z_g, y_abc, attn, w_out_l, f"outproj_{l}")
        saved.append((x, z, z_g, qkv, hs, y, o, lse, mp, w_in_l, w_out_l))
        x = x_new
    dx, loss, d_final_g = _loss_head(x, final_g.reshape(1, D), loss_target, "loss_head")
    grads = []
    token = None
    for l in reversed(range(depth)):
        x_l, z, z_g, qkv, hs, y, o, lse, mp, w_in_l, w_out_l = saved[l]
        if token is not None:
            mp = dict(mp, vec=mp["vec"] + token[0, 0])
        dy, dw_out = _outproj_bwd(dx, y, w_out_l, f"outproj_bwd_{l}")
        (dz_abc, dz_g, do1, do4, do16, dl1, dl4, dl16, dwA, dwR, dvec, dwa, dwx, dws, dbs) = _mix_bwd(
            z, z_g, dy, hs, o, mp, f"mix_bwd_{l}")
        dqkv = [_attn_bwd(qkv[p], do, lse[p], dl, dil, f"attn_bwd_d{dil}_{l}")
                for p, (dil, do, dl) in enumerate(zip(ATTN_DILATIONS, (do1, do4, do16), (dl1, dl4, dl16)))]
        dx, dz, h, dng = _inproj_bwd(x_l, norm_g[l].reshape(1, D), dx, dz_abc, dqkv, dz_g, w_in_l,
                                     f"inproj_bwd_{l}")
        dw_in = _inproj_wgrad(h, dz, f"inproj_wgrad_{l}")
        token = emit_wgrads(l, dw_in, dw_out)
        grads.append({
            "norm_g": dng[0],
            "conv_a_w": dwA[:conv_a_w.shape[1]], "conv_r_w": dwR[:conv_r_w.shape[1]],
            "conv_r_b": dvec[0], "lru_ba": dvec[1], "lru_bx": dvec[2], "lru_lambda": dvec[3], "gmlp_norm_g": dvec[4],
            "lru_wa": _diag_blocks(dwa), "lru_wx": _diag_blocks(dwx), "gmlp_ws": dws,
            "gmlp_bs": jnp.transpose(dbs[:, :N_HEADS]),
        })
    grads = grads[::-1]
    stacked = {k: jnp.stack([g[k] for g in grads]) for k in grads[0]}
    stacked["final_g"] = d_final_g[0]
    return loss[0, 0], dx, stacked


SMALL_NAMES = ("norm_g", "conv_a_w", "conv_r_w", "conv_r_b", "lru_wa", "lru_ba", "lru_wx", "lru_bx", "lru_lambda",
               "gmlp_norm_g", "gmlp_ws", "gmlp_bs", "final_g")
WEIGHT_NAMES = ("norm_g", "w_in", "conv_a_w", "conv_r_w", "conv_r_b", "lru_wa", "lru_ba", "lru_wx", "lru_bx",
                "lru_lambda", "gmlp_norm_g", "gmlp_ws", "gmlp_bs", "w_out", "final_g")


def kernel(x, norm_g, w_in, conv_a_w, conv_r_w, conv_r_b, lru_wa, lru_ba, lru_wx, lru_bx, lru_lambda, gmlp_norm_g, gmlp_ws, gmlp_bs, w_out, final_g, loss_target, m_norm_g, m_w_in, m_conv_a_w, m_conv_r_w, m_conv_r_b, m_lru_wa, m_lru_ba, m_lru_wx, m_lru_bx, m_lru_lambda, m_gmlp_norm_g, m_gmlp_ws, m_gmlp_bs, m_w_out, m_final_g, v_norm_g, v_w_in, v_conv_a_w, v_conv_r_w, v_conv_r_b, v_lru_wa, v_lru_ba, v_lru_wx, v_lru_bx, v_lru_lambda, v_gmlp_norm_g, v_gmlp_ws, v_gmlp_bs, v_w_out, v_final_g):
    w = dict(norm_g=norm_g, w_in=w_in, conv_a_w=conv_a_w, conv_r_w=conv_r_w, conv_r_b=conv_r_b, lru_wa=lru_wa,
             lru_ba=lru_ba, lru_wx=lru_wx, lru_bx=lru_bx, lru_lambda=lru_lambda, gmlp_norm_g=gmlp_norm_g,
             gmlp_ws=gmlp_ws, gmlp_bs=gmlp_bs, w_out=w_out, final_g=final_g)
    m = dict(norm_g=m_norm_g, w_in=m_w_in, conv_a_w=m_conv_a_w, conv_r_w=m_conv_r_w, conv_r_b=m_conv_r_b,
             lru_wa=m_lru_wa, lru_ba=m_lru_ba, lru_wx=m_lru_wx, lru_bx=m_lru_bx, lru_lambda=m_lru_lambda,
             gmlp_norm_g=m_gmlp_norm_g, gmlp_ws=m_gmlp_ws, gmlp_bs=m_gmlp_bs, w_out=m_w_out, final_g=m_final_g)
    v = dict(norm_g=v_norm_g, w_in=v_w_in, conv_a_w=v_conv_a_w, conv_r_w=v_conv_r_w, conv_r_b=v_conv_r_b,
             lru_wa=v_lru_wa, lru_ba=v_lru_ba, lru_wx=v_lru_wx, lru_bx=v_lru_bx, lru_lambda=v_lru_lambda,
             gmlp_norm_g=v_gmlp_norm_g, gmlp_ws=v_gmlp_ws, gmlp_bs=v_gmlp_bs, w_out=v_w_out, final_g=v_final_g)
    depth, D, n_loc = w_in.shape
    e_loc = w_out.shape[1]
    cx, cy, cc = _my_place()
    me = 4 * cx + 2 * cy + cc

    def with_own(srcs, lands, gather):
        own = [s[None] if gather else lax.dynamic_slice_in_dim(s, me, 1, axis=0) for s in srcs]
        return [lax.dynamic_update_slice_in_dim(ld, o, me, axis=0) for ld, o in zip(lands, own)]

    w_in_w, w_out_w = w_in.astype(MXU_DTYPE), w_out.astype(MXU_DTYPE)
    gathers, _ = _exchange_start([[w_in_w[l], w_out_w[l]] for l in range(depth)], True, "gather_start")

    def get_weights(l, after):
        srcs, lands = _exchange_wait(gathers[l], after, True, f"gather_wait_{l}")
        g_in, g_out = with_own(srcs, lands, True)
        return jnp.transpose(g_in, (1, 0, 2)).reshape(D, N_DEV * n_loc), g_out.reshape(N_DEV * e_loc, D)

    scatters = {}

    def emit_wgrads(l, dw_in, dw_out):
        handles, token = _exchange_start([[dw_in, dw_out.reshape(N_DEV, e_loc, D)]], False, f"scatter_start_{l}")
        scatters[l] = handles[0]
        return token

    c_loc = conv_a_w.shape[2]
    conv_full = [jnp.zeros(a.shape[:2] + (N_DEV * c_loc,), F32) for a in (conv_a_w, conv_r_w)]
    conv_full = [lax.dynamic_update_slice_in_dim(f, a, me * c_loc, axis=2)
                 for f, a in zip(conv_full, (conv_a_w, conv_r_w))]
    conv_a_full, conv_r_full = _unpack(_allreduce(_pack(conv_full), "gather_conv_taps"), conv_full)

    loss, grad_x, g = _local_step(
        x[0], loss_target[0], norm_g, get_weights, emit_wgrads, conv_a_full, conv_r_full, conv_r_b, lru_wa, lru_ba,
        lru_wx, lru_bx, lru_lambda, gmlp_norm_g, gmlp_ws, gmlp_bs, final_g)
    loss = lax.psum(loss, ("x", "y", "c"))

    per_layer = {"w_in": [], "w_out": []}
    for l in reversed(range(depth)):
        srcs, lands = _exchange_wait(scatters[l], grad_x, False, f"scatter_wait_{l}")
        r_in, r_out = with_own(srcs, lands, False)
        per_layer["w_in"].append(_adamw_summed(r_in, w_in[l], m_w_in[l], v_w_in[l], 512, f"adamw_w_in_{l}"))
        per_layer["w_out"].append(_adamw_summed(r_out, w_out[l], m_w_out[l], v_w_out[l], 128, f"adamw_w_out_{l}"))
    big = {k: [jnp.stack(parts) for parts in zip(*res[::-1])] for k, res in per_layer.items()}

    g_small = [g[k] for k in SMALL_NAMES]
    g_small = _unpack(_allreduce(_pack(g_small), "allreduce_small_grads"), g_small)
    g_small = dict(zip(SMALL_NAMES, g_small))
    for k in ("conv_a_w", "conv_r_w"):
        g_small[k] = lax.dynamic_slice_in_dim(g_small[k], me * c_loc, c_loc, axis=2)
    packs = [_pack([d[k] for k in SMALL_NAMES]) for d in (w, g_small, m, v)]
    res = _adamw_small(*packs, "adamw_small")
    like = [w[k] for k in SMALL_NAMES]
    d_s, m_s, v_s = (dict(zip(SMALL_NAMES, _unpack(r, like))) for r in res)

    grad, delta, new_m, new_v = {}, {}, {}, {}
    for k in WEIGHT_NAMES:
        if k in big:
            grad[k], delta[k], new_m[k], new_v[k] = big[k]
        else:
            grad[k], delta[k], new_m[k], new_v[k] = g_small[k], d_s[k], m_s[k], v_s[k]
    return (loss, grad_x[None], *[grad[k] for k in WEIGHT_NAMES], *[delta[k] for k in WEIGHT_NAMES],
            *[new_m[k] for k in WEIGHT_NAMES], *[new_v[k] for k in WEIGHT_NAMES])
```

```python
import functools
import math

import jax
import jax.numpy as jnp
from jax import lax
from jax.experimental import pallas as pl
from jax.experimental.pallas import tpu as pltpu

F32 = jnp.float32
MXU_DTYPE = jnp.bfloat16
WIRE_DTYPE = jnp.bfloat16
MESH = pl.DeviceIdType.MESH

N_DEV = 8
GROUP_W = 256
N_HEADS = 4
HEAD_DIM = 64
N_CHUNKS = 13
N_ABC = 9
GMLP_CHUNK = 128
ATTN_BLOCK = 128
ATTN_DILATIONS = (1, 4, 16)
NORM_EPS = 1e-6
RG_C = 8.0
SUBLANES = 8
LANES = 128
VMEM_LIMIT = 56 * 1024 * 1024

ADAM_LR = 0.001
ADAM_B1 = 0.9
ADAM_B2 = 0.999
ADAM_EPS = 1e-08
ADAM_WD = 0.01
ADAM_STEP = 10

TM_MIX = 256
TM_MM = 512


def _params(sem, vmem=VMEM_LIMIT):
    return pltpu.CompilerParams(dimension_semantics=sem, vmem_limit_bytes=vmem)


def _mm(a, b):
    return jnp.dot(a.astype(MXU_DTYPE), b.astype(MXU_DTYPE), preferred_element_type=F32)


def _mm_tn(a, b):
    return lax.dot_general(a.astype(MXU_DTYPE), b.astype(MXU_DTYPE), (((0,), (0,)), ((), ())),
                           preferred_element_type=F32)


def _mm_nt(a, b):
    return lax.dot_general(a.astype(MXU_DTYPE), b.astype(MXU_DTYPE), (((1,), (1,)), ((), ())),
                           preferred_element_type=F32)


def _sigmoid(x):
    return 1.0 / (1.0 + jnp.exp(-x))


def _silu_and_grad(x):
    s = _sigmoid(x)
    return x * s, s * (1.0 + x * (1.0 - s))


_GELU_K = math.sqrt(2.0 / math.pi)
_GELU_C = 0.044715


def _gelu_and_grad(x):
    x2 = x * x
    t = jnp.tanh(_GELU_K * (x + _GELU_C * x * x2))
    val = 0.5 * x * (1.0 + t)
    grad = 0.5 * (1.0 + t) + 0.5 * x * (1.0 - t * t) * (_GELU_K * (1.0 + 3.0 * _GELU_C * x2))
    return val, grad


def _gelu(x):
    return 0.5 * x * (1.0 + jnp.tanh(_GELU_K * (x + _GELU_C * x * x * x)))


def _expm1_nonpos(u):
    poly = u * (1.0 + u / 2.0 * (1.0 + u / 3.0 * (1.0 + u / 4.0 * (1.0 + u / 5.0 * (1.0 + u / 6.0 * (
        1.0 + u / 7.0 * (1.0 + u / 8.0)))))))
    return jnp.where(u > -0.25, poly, jnp.exp(u) - 1.0)


def _softplus(x):
    return jnp.maximum(x, 0.0) + jnp.log(1.0 + jnp.exp(-jnp.abs(x)))


def _shift_down(t, halo, k):
    rolled = pltpu.roll(t, k, 0)
    hr = pltpu.roll(halo, k, 0)
    row = lax.broadcasted_iota(jnp.int32, halo.shape, 0)
    first = jnp.where(row < k, hr, rolled[0:SUBLANES])
    return jnp.concatenate([first, rolled[SUBLANES:]], axis=0)


def _shift_up(t, nxt, k):
    tm = t.shape[0]
    rolled = pltpu.roll(t, tm - k, 0)
    nr = pltpu.roll(nxt, SUBLANES - k, 0)
    row = lax.broadcasted_iota(jnp.int32, nxt.shape, 0)
    last = jnp.where(row >= SUBLANES - k, nr, rolled[tm - SUBLANES:tm])
    return jnp.concatenate([rolled[:tm - SUBLANES], last], axis=0)


def _scan_fwd(a, b):
    tm = a.shape[0]
    row = lax.broadcasted_iota(jnp.int32, a.shape, 0)
    s = 1
    while s < tm:
        a_s = pltpu.roll(a, s, 0)
        b_s = pltpu.roll(b, s, 0)
        m = row >= s
        b = jnp.where(m, a * b_s + b, b)
        a = jnp.where(m, a * a_s, a)
        s *= 2
    return a, b


def _scan_rev(a, g):
    tm = a.shape[0]
    row = lax.broadcasted_iota(jnp.int32, a.shape, 0)
    s = 1
    while s < tm:
        a_s = pltpu.roll(a, tm - s, 0)
        g_s = pltpu.roll(g, tm - s, 0)
        m = row < tm - s
        g = jnp.where(m, g + a * g_s, g)
        a = jnp.where(m, a * a_s, a)
        s *= 2
    return g


def _lane_scratch(tm, w):
    return pltpu.VMEM((w // LANES, tm, LANES), F32)


def _put(scr_ref, val):
    for c in range(scr_ref.shape[0]):
        scr_ref[c] = val[:, c * LANES:(c + 1) * LANES].astype(F32)


def _get(scr_ref):
    return jnp.concatenate([scr_ref[c] for c in range(scr_ref.shape[0])], axis=1)


def _deinterleave(src_ref, dst_ref, dil):
    nc, tm, _ = src_ref.shape
    w = nc * LANES
    for r in range(dil):
        for c in range(nc):
            piece = src_ref[pl.ds(c, 1), pl.ds(r, tm // dil, stride=dil), :][0] if dil > 1 else src_ref[c]
            dst_ref[:, r * w + c * LANES:r * w + (c + 1) * LANES] = piece.astype(dst_ref.dtype)


def _interleave(src_ref, dst_ref, dil):
    nc, tm, _ = dst_ref.shape
    w = nc * LANES
    for r in range(dil):
        for c in range(nc):
            dst_ref[pl.ds(c, 1), pl.ds(r, tm // dil, stride=dil), :] = (
                src_ref[:, r * w + c * LANES:r * w + (c + 1) * LANES].astype(F32)[None])


def _dilated_spec(tm, w, dil, index=lambda i: i):
    return pl.BlockSpec((tm // dil, dil * w), lambda i: (index(i), 0))


def _dilated_shape(S, w, dil, dtype):
    return jax.ShapeDtypeStruct((S // dil, dil * w), dtype)


def _head_masks(shape):
    lane = lax.broadcasted_iota(jnp.int32, shape, 1)
    return [(lane >= h * HEAD_DIM) & (lane < (h + 1) * HEAD_DIM) for h in range(N_HEADS)]


def _colsum(v):
    return jnp.sum(v, axis=0, keepdims=True)


def _norm_inproj(x, g, w, name):
    S, D = x.shape
    N = w.shape[1]
    tm = TM_MM
    n_abc = N_ABC * GROUP_W
    n_qkv = 3 * GROUP_W

    def body(x_ref, g_ref, w_ref, zabc_ref, zg_ref, q1_ref, q4_ref, q16_ref, qkv_ref):
        xv = x_ref[...]
        r = lax.rsqrt(jnp.mean(xv * xv, axis=-1, keepdims=True) + NORM_EPS)
        h = ((xv * r) * g_ref[...]).astype(MXU_DTYPE)
        zabc_ref[...] = jnp.dot(h, w_ref[:, 0:n_abc], preferred_element_type=F32)
        _put(qkv_ref, jnp.dot(h, w_ref[:, n_abc:n_abc + n_qkv], preferred_element_type=F32))
        zg_ref[...] = jnp.dot(h, w_ref[:, n_abc + n_qkv:], preferred_element_type=F32)
        for dil, ref in zip(ATTN_DILATIONS, (q1_ref, q4_ref, q16_ref)):
            _deinterleave(qkv_ref, ref, dil)

    row = lambda wd: pl.BlockSpec((tm, wd), lambda i: (i, 0))
    return pl.pallas_call(
        body, name=name, grid=(S // tm,),
        in_specs=[row(D), pl.BlockSpec((1, D), lambda i: (0, 0)), pl.BlockSpec((D, N), lambda i: (0, 0))],
        out_specs=[row(n_abc), row(GROUP_W)] + [_dilated_spec(tm, n_qkv, dil) for dil in ATTN_DILATIONS],
        out_shape=[jax.ShapeDtypeStruct((S, n_abc), F32), jax.ShapeDtypeStruct((S, GROUP_W), F32)]
                  + [_dilated_shape(S, n_qkv, dil, MXU_DTYPE) for dil in ATTN_DILATIONS],
        scratch_shapes=[_lane_scratch(tm, n_qkv)],
        compiler_params=_params(("parallel",)),
    )(x, g, w)


def _conv_a(z_of, halo_of, w_ref):
    p = z_of(2) * z_of(0)
    p_h = halo_of(2) * halo_of(0)
    cv = w_ref[2:3, :] * p + w_ref[1:2, :] * _shift_down(p, p_h, 1) + w_ref[0:1, :] * _shift_down(p, p_h, 2)
    return p, p_h, cv


def _lru_gates(z_of, halo_of, wr_ref, vec_ref, wa_ref, wx_ref):
    rx = z_of(4)
    rx_h = halo_of(4)
    sh = [rx, _shift_down(rx, rx_h, 1), _shift_down(rx, rx_h, 2), _shift_down(rx, rx_h, 3)]
    xc = (wr_ref[3:4, :] * sh[0] + wr_ref[2:3, :] * sh[1] + wr_ref[1:2, :] * sh[2]
          + wr_ref[0:1, :] * sh[3] + vec_ref[0:1, :])
    ga = _sigmoid(jnp.dot(xc.astype(MXU_DTYPE), wa_ref[...], preferred_element_type=F32) + vec_ref[1:2, :])
    gi = _sigmoid(jnp.dot(xc.astype(MXU_DTYPE), wx_ref[...], preferred_element_type=F32) + vec_ref[2:3, :])
    sp = _softplus(-vec_ref[3:4, :])
    log_a = (-RG_C * ga) * sp
    a = jnp.exp(log_a)
    mult = jnp.sqrt(-_expm1_nonpos(2.0 * log_a))
    return xc, sh, ga, gi, a, mult, sp


def _gmlp_fwd(z_of, vec_ref, ws_ref, bs_ref, tm):
    u = _gelu(z_of(6))
    gv = _gelu(z_of(7))
    rr = lax.rsqrt(jnp.mean(gv * gv, axis=-1, keepdims=True) + NORM_EPS)
    vn = (gv * rr) * vec_ref[4:5, :]
    masks = _head_masks((GMLP_CHUNK, GROUP_W))
    parts = []
    for c in range(tm // GMLP_CHUNK):
        vc = vn[c * GMLP_CHUNK:(c + 1) * GMLP_CHUNK].astype(MXU_DTYPE)
        acc = bs_ref[...]
        for h in range(N_HEADS):
            acc = acc + jnp.where(masks[h], jnp.dot(ws_ref[h], vc, preferred_element_type=F32), 0.0)
        parts.append(acc)
    return u, gv, rr, vn, jnp.concatenate(parts, axis=0)


def _mix_specs(tm, S, order):
    const2 = lambda shape: pl.BlockSpec(shape, lambda i: (0, 0))
    return [const2((SUBLANES, GROUP_W)), const2((SUBLANES, GROUP_W)), const2((SUBLANES, GROUP_W)),
            const2((GROUP_W, GROUP_W)), const2((GROUP_W, GROUP_W)),
            pl.BlockSpec((N_HEADS, GMLP_CHUNK, GMLP_CHUNK), lambda i: (0, 0, 0)),
            const2((GMLP_CHUNK, GROUP_W))]


def _mix_fwd(z, mp, name):
    S = z.shape[0]
    tm = TM_MIX
    hb = tm // SUBLANES
    wcols = N_ABC * GROUP_W

    def body(z_ref, zh_ref, wA_ref, wR_ref, vec_ref, wa_ref, wx_ref, ws_ref, bs_ref, y_ref, h_ref, carry_ref):
        i = pl.program_id(0)

        @pl.when(i == 0)
        def _():
            carry_ref[...] = jnp.zeros_like(carry_ref)

        not_first = i > 0
        z_of = lambda c: z_ref[:, c * GROUP_W:(c + 1) * GROUP_W]
        halo_of = lambda c: jnp.where(not_first, zh_ref[:, c * GROUP_W:(c + 1) * GROUP_W], 0.0)

        _, _, cv = _conv_a(z_of, halo_of, wA_ref)
        y_ref[:, 0:GROUP_W] = z_of(1) * cv * _silu_and_grad(z_of(3))[0]

        xc, _, _, gi, a, mult, _ = _lru_gates(z_of, halo_of, wR_ref, vec_ref, wa_ref, wx_ref)
        b = mult * (gi * xc)
        acum, bcum = _scan_fwd(a, b)
        h = bcum + acum * carry_ref[SUBLANES - 1:SUBLANES, :]
        h_ref[...] = h
        carry_ref[...] = h[tm - SUBLANES:tm]
        y_ref[:, GROUP_W:2 * GROUP_W] = h * _silu_and_grad(z_of(5))[0]

        u, _, _, _, sp = _gmlp_fwd(z_of, vec_ref, ws_ref, bs_ref, tm)
        y_ref[:, 2 * GROUP_W:3 * GROUP_W] = u * sp * _silu_and_grad(z_of(8))[0]

    return pl.pallas_call(
        body, name=name, grid=(S // tm,),
        in_specs=[pl.BlockSpec((tm, wcols), lambda i: (i, 0)),
                  pl.BlockSpec((SUBLANES, wcols), lambda i: (jnp.maximum(i * hb - 1, 0), 0))]
                 + _mix_specs(tm, S, "fwd"),
        out_specs=[pl.BlockSpec((tm, 3 * GROUP_W), lambda i: (i, 0)),
                   pl.BlockSpec((tm, GROUP_W), lambda i: (i, 0))],
        out_shape=[jax.ShapeDtypeStruct((S, 3 * GROUP_W), F32), jax.ShapeDtypeStruct((S, GROUP_W), F32)],
        scratch_shapes=[pltpu.VMEM((SUBLANES, GROUP_W), F32)],
        compiler_params=_params(("arbitrary",)),
    )(z, z, mp["wA"], mp["wR"], mp["vec"], mp["wa"], mp["wx"], mp["ws"], mp["bs"])


_NEG = -1e30


def _slope(h):
    return 2.0 ** (-8.0 * (h + 1) / N_HEADS)


def _attn_bias(dil, offsets, n_keys):
    shape = (ATTN_BLOCK, n_keys)
    qi = lax.broadcasted_iota(jnp.int32, shape, 0)
    ki = lax.broadcasted_iota(jnp.int32, shape, 1)
    blocks = []
    for f in offsets:
        delta = qi + f - ki
        valid = (delta >= 0) & (delta <= ATTN_BLOCK)
        dist = (delta * dil).astype(F32)
        for h in range(N_HEADS):
            blocks.append(jnp.where(valid, -_slope(h) * dist, _NEG))
    return jnp.concatenate(blocks, axis=0)


def _stack_heads(t, masks):
    return jnp.concatenate([jnp.where(m, t, jnp.zeros_like(t)) for m in masks], axis=0)


def _unstack_heads(t4, masks, base=0):
    out = t4[base * ATTN_BLOCK:(base + 1) * ATTN_BLOCK]
    for h in range(1, N_HEADS):
        out = jnp.where(masks[h], t4[(base + h) * ATTN_BLOCK:(base + h + 1) * ATTN_BLOCK], out)
    return out


def _attn_fwd(qkv, dil, name):
    rows = qkv.shape[0]
    nb = rows // ATTN_BLOCK
    scale = 1.0 / math.sqrt(HEAD_DIM)

    def body(q_ref, kc_ref, kp_ref, vc_ref, vp_ref, o_ref, l_ref, bias_ref):
        n = pl.program_id(1)

        @pl.when(n == 0)
        def _():
            bias_ref[...] = _attn_bias(dil, (ATTN_BLOCK,), 2 * ATTN_BLOCK)

        masks = _head_masks((ATTN_BLOCK, GROUP_W))
        qs = _stack_heads(q_ref[...], masks)
        keys = jnp.concatenate([kp_ref[...], kc_ref[...]], axis=0)
        vals = jnp.concatenate([vp_ref[...], vc_ref[...]], axis=0)
        s = _mm_nt(qs, keys) * scale + bias_ref[...]
        key_col = lax.broadcasted_iota(jnp.int32, s.shape, 1)
        s = jnp.where((n == 0) & (key_col < ATTN_BLOCK), _NEG, s)
        m = jnp.max(s, axis=-1, keepdims=True)
        p = jnp.exp(s - m)
        l = jnp.sum(p, axis=-1, keepdims=True)
        o4 = jnp.dot(p.astype(MXU_DTYPE), vals, preferred_element_type=F32) / l
        o_ref[...] = _unstack_heads(o4, masks)
        l_ref[...] = _unstack_heads(jnp.broadcast_to(m + jnp.log(l), o4.shape), masks)

    blk = (ATTN_BLOCK, GROUP_W)
    cur = lambda c: pl.BlockSpec(blk, lambda r, n: (n, r * 3 + c))
    prev = lambda c: pl.BlockSpec(blk, lambda r, n: (jnp.maximum(n - 1, 0), r * 3 + c))
    out = pl.BlockSpec(blk, lambda r, n: (n, r))
    return pl.pallas_call(
        body, name=name, grid=(dil, nb),
        in_specs=[cur(0), cur(1), prev(1), cur(2), prev(2)],
        out_specs=[out, out],
        out_shape=[jax.ShapeDtypeStruct((rows, dil * GROUP_W), F32)] * 2,
        scratch_shapes=[pltpu.VMEM((N_HEADS * ATTN_BLOCK, 2 * ATTN_BLOCK), F32)],
        compiler_params=_params(("parallel", "arbitrary")),
    )(qkv, qkv, qkv, qkv, qkv)


def _outproj(x, z_g, y_abc, attn, w_out, name):
    S, D = x.shape
    tm = TM_MM
    n_abc = 3 * GROUP_W

    def body(x_ref, g_ref, yabc_ref, o1, l1, o2, l2, o3, l3, w_ref,
             xn_ref, y_ref, o_ref, lse1_ref, lse4_ref, lse16_ref, so2, sl2, so3, sl3, slse):
        for src, dst, dil in ((o2, so2, ATTN_DILATIONS[1]), (l2, sl2, ATTN_DILATIONS[1]),
                              (o3, so3, ATTN_DILATIONS[2]), (l3, sl3, ATTN_DILATIONS[2])):
            _interleave(src, dst, dil)
        la, lb, lc = l1[...], _get(sl2), _get(sl3)
        mx = jnp.maximum(jnp.maximum(la, lb), lc)
        ea, eb, ec = jnp.exp(la - mx), jnp.exp(lb - mx), jnp.exp(lc - mx)
        den = ea + eb + ec
        o = (ea * o1[...] + eb * _get(so2) + ec * _get(so3)) / den
        o_ref[...] = o
        _put(slse, mx + jnp.log(den))
        for dil, ref in zip(ATTN_DILATIONS, (lse1_ref, lse4_ref, lse16_ref)):
            _deinterleave(slse, ref, dil)
        y_d = o * _silu_and_grad(g_ref[...])[0]
        y_ref[:, 0:n_abc] = yabc_ref[...].astype(MXU_DTYPE)
        y_ref[:, n_abc:] = y_d.astype(MXU_DTYPE)
        xn_ref[...] = x_ref[...] + jnp.dot(y_ref[...], w_ref[...], preferred_element_type=F32)

    row = lambda w: pl.BlockSpec((tm, w), lambda i: (i, 0))
    dil_specs = [_dilated_spec(tm, GROUP_W, dil) for dil in ATTN_DILATIONS]
    (o1, l1), (o2, l2), (o3, l3) = attn
    return pl.pallas_call(
        body, name=name, grid=(S // tm,),
        in_specs=[row(D), row(GROUP_W), row(n_abc)] + [sp for sp in dil_specs for _ in range(2)]
                 + [pl.BlockSpec(w_out.shape, lambda i: (0, 0))],
        out_specs=[row(D), row(4 * GROUP_W), row(GROUP_W)] + dil_specs,
        out_shape=[jax.ShapeDtypeStruct((S, D), F32), jax.ShapeDtypeStruct((S, 4 * GROUP_W), MXU_DTYPE),
                   jax.ShapeDtypeStruct((S, GROUP_W), F32)]
                  + [_dilated_shape(S, GROUP_W, dil, F32) for dil in ATTN_DILATIONS],
        scratch_shapes=[_lane_scratch(tm, GROUP_W)] * 5,
        compiler_params=_params(("parallel",)),
    )(x, z_g, y_abc, o1, l1, o2, l2, o3, l3, w_out)


def _loss_head(x, g, target, name):
    S, D = x.shape
    tm = TM_MM

    def body(x_ref, g_ref, t_ref, dx_ref, loss_ref, dg_ref):
        i = pl.program_id(0)

        @pl.when(i == 0)
        def _():
            loss_ref[...] = jnp.zeros_like(loss_ref)
            dg_ref[...] = jnp.zeros_like(dg_ref)

        xv = x_ref[...]
        r = lax.rsqrt(jnp.mean(xv * xv, axis=-1, keepdims=True) + NORM_EPS)
        xn = xv * r
        err = xn * g_ref[...] - t_ref[...]
        per_tok = jnp.mean(err * err, axis=-1, keepdims=True)
        loss_ref[...] += 0.5 * jnp.sum(per_tok, axis=0, keepdims=True)
        dout = err * (1.0 / D)
        dg_ref[...] += _colsum(dout * xn)
        dxn = dout * g_ref[...]
        dx_ref[...] = r * (dxn - xn * jnp.mean(dxn * xn, axis=-1, keepdims=True))

    row = pl.BlockSpec((tm, D), lambda i: (i, 0))
    return pl.pallas_call(
        body, name=name, grid=(S // tm,),
        in_specs=[row, pl.BlockSpec((1, D), lambda i: (0, 0)), row],
        out_specs=[row, pl.BlockSpec((1, LANES), lambda i: (0, 0)), pl.BlockSpec((1, D), lambda i: (0, 0))],
        out_shape=[jax.ShapeDtypeStruct((S, D), F32), jax.ShapeDtypeStruct((1, LANES), F32),
                   jax.ShapeDtypeStruct((1, D), F32)],
        compiler_params=_params(("arbitrary",)),
    )(x, g, target)


def _outproj_bwd(dx, y, w_out, name):
    S, D = dx.shape
    E = y.shape[1]
    tm = TM_MM

    def body(dx_ref, y_ref, w_ref, dy_ref, dw_ref, acc_ref):
        i = pl.program_id(0)

        @pl.when(i == 0)
        def _():
            acc_ref[...] = jnp.zeros_like(acc_ref)

        dxb = dx_ref[...].astype(MXU_DTYPE)
        dy_ref[...] = _mm_nt(dxb, w_ref[...])
        acc_ref[...] += _mm_tn(y_ref[...], dxb)

        @pl.when(i == S // tm - 1)
        def _():
            dw_ref[...] = acc_ref[...].astype(dw_ref.dtype)

    return pl.pallas_call(
        body, name=name, grid=(S // tm,),
        in_specs=[pl.BlockSpec((tm, D), lambda i: (i, 0)), pl.BlockSpec((tm, E), lambda i: (i, 0)),
                  pl.BlockSpec((E, D), lambda i: (0, 0))],
        out_specs=[pl.BlockSpec((tm, E), lambda i: (i, 0)), pl.BlockSpec((E, D), lambda i: (0, 0))],
        out_shape=[jax.ShapeDtypeStruct((S, E), F32), jax.ShapeDtypeStruct((E, D), WIRE_DTYPE)],
        scratch_shapes=[pltpu.VMEM((E, D), F32)],
        compiler_params=_params(("arbitrary",)),
    )(dx, y, w_out)


def _mix_bwd(z, z_g, dy, hs, o, mp, name):
    S = z.shape[0]
    tm = TM_MIX
    hb = tm // SUBLANES
    nT = S // tm
    last_blk = S // SUBLANES - 1
    wcols = N_ABC * GROUP_W

    def body(z_ref, zh_ref, zn_ref, zg_ref, dy_ref, dyn_ref, h_ref, hh_ref, o_ref,
             wA_ref, wR_ref, vec_ref, wa_ref, wx_ref, ws_ref, bs_ref,
             dz_ref, dzg_ref, do1_ref, do4_ref, do16_ref, dl1_ref, dl4_ref, dl16_ref,
             dwA_ref, dwR_ref, dvec_ref, dwa_ref, dwx_ref, dws_ref, dbs_ref,
             hcarry_ref, xcarry_ref, bsacc_ref, do_ref, dl_ref):
        i = pl.program_id(0)
        ti = nT - 1 - i

        @pl.when(i == 0)
        def _():
            hcarry_ref[...] = jnp.zeros_like(hcarry_ref)
            xcarry_ref[...] = jnp.zeros_like(xcarry_ref)
            bsacc_ref[...] = jnp.zeros_like(bsacc_ref)
            dwA_ref[...] = jnp.zeros_like(dwA_ref)
            dwR_ref[...] = jnp.zeros_like(dwR_ref)
            dvec_ref[...] = jnp.zeros_like(dvec_ref)
            dwa_ref[...] = jnp.zeros_like(dwa_ref)
            dwx_ref[...] = jnp.zeros_like(dwx_ref)
            dws_ref[...] = jnp.zeros_like(dws_ref)
            dbs_ref[...] = jnp.zeros_like(dbs_ref)

        has_prev = ti > 0
        has_next = i > 0
        col = lambda c: slice(c * GROUP_W, (c + 1) * GROUP_W)
        z_of = lambda c: z_ref[:, col(c)]
        halo_of = lambda c: jnp.where(has_prev, zh_ref[:, col(c)], 0.0)
        next_of = lambda c: zn_ref[:, col(c)]

        p, p_h, cv = _conv_a(z_of, halo_of, wA_ref)
        sg, dsg = _silu_and_grad(z_of(3))
        a_b = z_of(1)
        dya = dy_ref[:, col(0)]
        dcv = dya * a_b * sg
        dcv_n = jnp.where(has_next, dyn_ref[...] * next_of(1) * _silu_and_grad(next_of(3))[0], 0.0)
        dp = (wA_ref[2:3, :] * dcv + wA_ref[1:2, :] * _shift_up(dcv, dcv_n, 1)
              + wA_ref[0:1, :] * _shift_up(dcv, dcv_n, 2))
        dwA_ref[2:3, :] += _colsum(dcv * p)
        dwA_ref[1:2, :] += _colsum(dcv * _shift_down(p, p_h, 1))
        dwA_ref[0:1, :] += _colsum(dcv * _shift_down(p, p_h, 2))
        dz_ref[:, col(0)] = dp * z_of(2)
        dz_ref[:, col(1)] = dya * cv * sg
        dz_ref[:, col(2)] = dp * z_of(0)
        dz_ref[:, col(3)] = dya * a_b * cv * dsg

        xc, sh, ga, gi, a, mult, sp = _lru_gates(z_of, halo_of, wR_ref, vec_ref, wa_ref, wx_ref)
        h = h_ref[...]
        h_prev = _shift_down(h, jnp.where(has_prev, hh_ref[...], 0.0), 1)
        sgr, dsgr = _silu_and_grad(z_of(5))
        dyb = dy_ref[:, col(1)]
        dz_ref[:, col(5)] = dyb * h * dsgr
        row = lax.broadcasted_iota(jnp.int32, (tm, GROUP_W), 0)
        g_in = dyb * sgr + jnp.where(row == tm - 1, hcarry_ref[0:1, :], 0.0)
        a_up = _shift_up(a, jnp.zeros((SUBLANES, GROUP_W), F32), 1)
        dH = _scan_rev(a_up, g_in)
        hcarry_ref[...] = (a * dH)[0:SUBLANES]
        da = dH * h_prev
        gx = gi * xc
        dmult = dH * gx
        dgi = dH * mult * xc
        dxc = dH * mult * gi
        dlog_a = da * a - dmult * (a * a) / mult
        dga = dlog_a * (-RG_C * sp)
        dlam_row = _colsum(dlog_a * (-RG_C * ga)) * (-_sigmoid(-vec_ref[3:4, :]))
        dpre_a = dga * ga * (1.0 - ga)
        dpre_i = dgi * gi * (1.0 - gi)
        dwa_ref[...] += _mm_tn(xc, dpre_a)
        dwx_ref[...] += _mm_tn(xc, dpre_i)
        dxc = dxc + _mm_nt(dpre_a, wa_ref[...]) + _mm_nt(dpre_i, wx_ref[...])
        dvec_ref[0:1, :] += _colsum(dxc)
        dvec_ref[1:2, :] += _colsum(dpre_a)
        dvec_ref[2:3, :] += _colsum(dpre_i)
        dvec_ref[3:4, :] += dlam_row
        for k in range(4):
            dwR_ref[k:k + 1, :] += _colsum(dxc * sh[3 - k])
        dxc_n = xcarry_ref[...]
        dz_ref[:, col(4)] = (wR_ref[3:4, :] * dxc + wR_ref[2:3, :] * _shift_up(dxc, dxc_n, 1)
                             + wR_ref[1:2, :] * _shift_up(dxc, dxc_n, 2)
                             + wR_ref[0:1, :] * _shift_up(dxc, dxc_n, 3))
        xcarry_ref[...] = dxc[0:SUBLANES]

        c_u, c_v = z_of(6), z_of(7)
        u, du_dx = _gelu_and_grad(c_u)
        gv, dgv_dx = _gelu_and_grad(c_v)
        rr = lax.rsqrt(jnp.mean(gv * gv, axis=-1, keepdims=True) + NORM_EPS)
        xhat = gv * rr
        g_c = vec_ref[4:5, :]
        vn = xhat * g_c
        masks = _head_masks((GMLP_CHUNK, GROUP_W))
        tri_r = lax.broadcasted_iota(jnp.int32, (GMLP_CHUNK, GMLP_CHUNK), 0)
        tri_c = lax.broadcasted_iota(jnp.int32, (GMLP_CHUNK, GMLP_CHUNK), 1)
        tril = tri_r >= tri_c
        sgc, dsgc = _silu_and_grad(z_of(8))
        dyc = dy_ref[:, col(2)]
        dsp_full = dyc * u * sgc
        sp_parts, dvn_parts = [], []
        for c in range(tm // GMLP_CHUNK):
            rs = slice(c * GMLP_CHUNK, (c + 1) * GMLP_CHUNK)
            vc = vn[rs].astype(MXU_DTYPE)
            dsp_c = dsp_full[rs]
            bsacc_ref[...] += dsp_c
            acc = bs_ref[...]
            dvn_c = jnp.zeros((GMLP_CHUNK, GROUP_W), F32)
            for h in range(N_HEADS):
                w_h = ws_ref[h]
                acc = acc + jnp.where(masks[h], jnp.dot(w_h, vc, preferred_element_type=F32), 0.0)
                dsp_h = jnp.where(masks[h], dsp_c, 0.0).astype(MXU_DTYPE)
                dvn_c = dvn_c + _mm_tn(w_h, dsp_h)
                dws_ref[h] += jnp.where(tril, _mm_nt(dsp_h, vc), 0.0)
            sp_parts.append(acc)
            dvn_parts.append(dvn_c)
        spv = jnp.concatenate(sp_parts, axis=0)
        dvn = jnp.concatenate(dvn_parts, axis=0)
        dz_ref[:, col(6)] = dyc * spv * sgc * du_dx
        dz_ref[:, col(8)] = dyc * u * spv * dsgc
        dvec_ref[4:5, :] += _colsum(dvn * xhat)
        dgvn = dvn * g_c
        dgv = rr * (dgvn - xhat * jnp.mean(dgvn * xhat, axis=-1, keepdims=True))
        dz_ref[:, col(7)] = dgv * dgv_dx

        sgd, dsgd = _silu_and_grad(zg_ref[...])
        dyd = dy_ref[:, col(3)]
        ov = o_ref[...]
        do = dyd * sgd
        _put(do_ref, do)
        dzg_ref[...] = dyd * ov * dsgd
        prod = do * ov
        tmasks = _head_masks((tm, GROUP_W))
        dl = jnp.zeros((tm, GROUP_W), F32)
        for h in range(N_HEADS):
            dl = jnp.where(tmasks[h], jnp.sum(jnp.where(tmasks[h], prod, 0.0), axis=-1, keepdims=True), dl)
        _put(dl_ref, dl)
        for dil, d_out, l_out in zip(ATTN_DILATIONS, (do1_ref, do4_ref, do16_ref), (dl1_ref, dl4_ref, dl16_ref)):
            _deinterleave(do_ref, d_out, dil)
            _deinterleave(dl_ref, l_out, dil)

        @pl.when(i == nT - 1)
        def _():
            acc = bsacc_ref[...]
            lane = lax.broadcasted_iota(jnp.int32, (GMLP_CHUNK, LANES), 1)
            out = jnp.zeros((GMLP_CHUNK, LANES), F32)
            for h in range(N_HEADS):
                out = jnp.where(lane == h, jnp.sum(jnp.where(masks[h], acc, 0.0), axis=-1, keepdims=True), out)
            dbs_ref[...] = out

    rev = lambda w: pl.BlockSpec((tm, w), lambda i: (nT - 1 - i, 0))
    prev8 = lambda w: pl.BlockSpec((SUBLANES, w), lambda i: (jnp.maximum((nT - 1 - i) * hb - 1, 0), 0))
    next8 = lambda w: pl.BlockSpec((SUBLANES, w), lambda i: (jnp.minimum((nT - i) * hb, last_blk), 0))
    const2 = lambda shape: pl.BlockSpec(shape, lambda i: (0, 0))
    dil_specs = [_dilated_spec(tm, GROUP_W, dil, lambda i: nT - 1 - i) for dil in ATTN_DILATIONS]
    dil_shapes = [_dilated_shape(S, GROUP_W, dil, F32) for dil in ATTN_DILATIONS]
    small = (SUBLANES, GROUP_W)
    sq = (GROUP_W, GROUP_W)
    ws_shape = (N_HEADS, GMLP_CHUNK, GMLP_CHUNK)
    return pl.pallas_call(
        body, name=name, grid=(nT,),
        in_specs=[rev(wcols), prev8(wcols), next8(wcols), rev(GROUP_W),
                  rev(4 * GROUP_W), next8(GROUP_W), rev(GROUP_W), prev8(GROUP_W), rev(GROUP_W)]
                 + _mix_specs(tm, S, "bwd"),
        out_specs=[rev(wcols), rev(GROUP_W)] + dil_specs + dil_specs
                  + [const2(small), const2(small), const2(small), const2(sq), const2(sq),
                     pl.BlockSpec(ws_shape, lambda i: (0, 0, 0)), const2((GMLP_CHUNK, LANES))],
        out_shape=[jax.ShapeDtypeStruct((S, wcols), F32), jax.ShapeDtypeStruct((S, GROUP_W), F32)]
                  + dil_shapes + dil_shapes
                  + [jax.ShapeDtypeStruct(small, F32)] * 3 + [jax.ShapeDtypeStruct(sq, F32)] * 2
                  + [jax.ShapeDtypeStruct(ws_shape, F32), jax.ShapeDtypeStruct((GMLP_CHUNK, LANES), F32)],
        scratch_shapes=[pltpu.VMEM(small, F32), pltpu.VMEM(small, F32), pltpu.VMEM((GMLP_CHUNK, GROUP_W), F32),
                        _lane_scratch(tm, GROUP_W), _lane_scratch(tm, GROUP_W)],
        compiler_params=_params(("arbitrary",)),
    )(z, z, z, z_g, dy, dy, hs, hs, o, mp["wA"], mp["wR"], mp["vec"], mp["wa"], mp["wx"], mp["ws"], mp["bs"])


def _attn_bwd(qkv, do, lse, delta, dil, name):
    rows = qkv.shape[0]
    nb = rows // ATTN_BLOCK
    scale = 1.0 / math.sqrt(HEAD_DIM)

    def body(qc_ref, qn_ref, k_ref, v_ref, doc_ref, don_ref, lc_ref, ln_ref, dc_ref, dn_ref,
             dq_ref, dk_ref, dv_ref, carry_ref, bias_ref):
        n = pl.program_id(1)

        @pl.when(n == 0)
        def _():
            carry_ref[...] = jnp.zeros_like(carry_ref)
            bias_ref[...] = _attn_bias(dil, (0, ATTN_BLOCK), ATTN_BLOCK)

        kb = k_ref[...]
        vb = v_ref[...]
        masks = _head_masks((ATTN_BLOCK, GROUP_W))
        qs = jnp.concatenate([_stack_heads(qc_ref[...], masks), _stack_heads(qn_ref[...], masks)], axis=0)
        dos = jnp.concatenate([_stack_heads(doc_ref[...].astype(MXU_DTYPE), masks),
                               _stack_heads(don_ref[...].astype(MXU_DTYPE), masks)], axis=0)

        def per_row(cur_ref, nxt_ref):
            cols = [jnp.max(jnp.where(masks[h], ref[...], _NEG), axis=-1, keepdims=True)
                    for ref in (cur_ref, nxt_ref) for h in range(N_HEADS)]
            return jnp.concatenate(cols, axis=0)

        lse_rows = per_row(lc_ref, ln_ref)
        dl_rows = per_row(dc_ref, dn_ref)
        s = _mm_nt(qs, kb) * scale + bias_ref[...]
        row = lax.broadcasted_iota(jnp.int32, s.shape, 0)
        s = jnp.where((n == nb - 1) & (row >= N_HEADS * ATTN_BLOCK), _NEG, s)
        p = jnp.exp(s - lse_rows)
        dp = _mm_nt(dos, vb)
        ds = (p * (dp - dl_rows) * scale).astype(MXU_DTYPE)
        dv_ref[...] = _mm_tn(p.astype(MXU_DTYPE), dos)
        dk_ref[...] = _mm_tn(ds, qs)
        dq4 = jnp.dot(ds, kb, preferred_element_type=F32)
        dq_ref[...] = carry_ref[...] + _unstack_heads(dq4, masks)
        carry_ref[...] = _unstack_heads(dq4, masks, N_HEADS)

    blk = (ATTN_BLOCK, GROUP_W)
    zcur = lambda c: pl.BlockSpec(blk, lambda r, n: (n, r * 3 + c))
    znext = lambda c: pl.BlockSpec(blk, lambda r, n: (jnp.minimum(n + 1, nb - 1), r * 3 + c))
    cur = pl.BlockSpec(blk, lambda r, n: (n, r))
    nxt = pl.BlockSpec(blk, lambda r, n: (jnp.minimum(n + 1, nb - 1), r))
    return pl.pallas_call(
        body, name=name, grid=(dil, nb),
        in_specs=[zcur(0), znext(0), zcur(1), zcur(2), cur, nxt, cur, nxt, cur, nxt],
        out_specs=[cur, cur, cur],
        out_shape=[jax.ShapeDtypeStruct((rows, dil * GROUP_W), F32)] * 3,
        scratch_shapes=[pltpu.VMEM(blk, F32), pltpu.VMEM((2 * N_HEADS * ATTN_BLOCK, ATTN_BLOCK), F32)],
        compiler_params=_params(("parallel", "arbitrary")),
    )(qkv, qkv, qkv, qkv, do, do, lse, lse, delta, delta)


def _inproj_bwd(x, g, dxn, dz_abc, dqkv, dz_g, w_in, name):
    S, D = x.shape
    N = w_in.shape[1]
    tm = TM_MM
    n_abc = N_ABC * GROUP_W

    def body(x_ref, g_ref, dxn_ref, dabc_ref, q1, k1, v1, q2, k2, v2, q3, k3, v3, dg_ref, w_ref,
             dx_ref, dz_ref, h_ref, dgn_ref, s4_ref, s16_ref):
        i = pl.program_id(0)

        @pl.when(i == 0)
        def _():
            dgn_ref[...] = jnp.zeros_like(dgn_ref)

        dz_ref[:, 0:n_abc] = dabc_ref[...].astype(MXU_DTYPE)
        for j, parts in enumerate(((q1, q2, q3), (k1, k2, k3), (v1, v2, v3))):
            c0 = n_abc + j * GROUP_W
            _interleave(parts[1], s4_ref, ATTN_DILATIONS[1])
            _interleave(parts[2], s16_ref, ATTN_DILATIONS[2])
            dz_ref[:, c0:c0 + GROUP_W] = (parts[0][...] + _get(s4_ref) + _get(s16_ref)).astype(MXU_DTYPE)
        dz_ref[:, n_abc + 3 * GROUP_W:] = dg_ref[...].astype(MXU_DTYPE)
        dh = _mm_nt(dz_ref[...], w_ref[...])
        xv = x_ref[...]
        r = lax.rsqrt(jnp.mean(xv * xv, axis=-1, keepdims=True) + NORM_EPS)
        xn = xv * r
        gv = g_ref[...]
        h_ref[...] = (xn * gv).astype(MXU_DTYPE)
        dgn_ref[...] += _colsum(dh * xn)
        dn = dh * gv
        dx_ref[...] = dxn_ref[...] + r * (dn - xn * jnp.mean(dn * xn, axis=-1, keepdims=True))

    row = lambda w: pl.BlockSpec((tm, w), lambda i: (i, 0))
    flat = [t for p in dqkv for t in p]
    dil_specs = [_dilated_spec(tm, GROUP_W, dil) for dil in ATTN_DILATIONS for _ in range(3)]
    return pl.pallas_call(
        body, name=name, grid=(S // tm,),
        in_specs=[row(D), pl.BlockSpec((1, D), lambda i: (0, 0)), row(D), row(n_abc)] + dil_specs
                 + [row(GROUP_W), pl.BlockSpec((D, N), lambda i: (0, 0))],
        out_specs=[row(D), row(N), row(D), pl.BlockSpec((1, D), lambda i: (0, 0))],
        out_shape=[jax.ShapeDtypeStruct((S, D), F32), jax.ShapeDtypeStruct((S, N), MXU_DTYPE),
                   jax.ShapeDtypeStruct((S, D), MXU_DTYPE), jax.ShapeDtypeStruct((1, D), F32)],
        scratch_shapes=[_lane_scratch(tm, GROUP_W)] * 2,
        compiler_params=_params(("arbitrary",)),
    )(x, g, dxn, dz_abc, *flat, dz_g, w_in)


def _inproj_wgrad(h, dz, name):
    S, D = h.shape
    N = dz.shape[1]
    tm = TM_MM
    nj = 2
    cw = N // nj
    per = N_DEV // nj
    n_loc = N // N_DEV

    def body(h_ref, dz_ref, dw_ref, acc_ref):
        i = pl.program_id(1)

        @pl.when(i == 0)
        def _():
            acc_ref[...] = jnp.zeros_like(acc_ref)

        acc_ref[...] += _mm_tn(h_ref[...], dz_ref[...])

        @pl.when(i == S // tm - 1)
        def _():
            for b in range(per):
                dw_ref[b] = acc_ref[:, b * n_loc:(b + 1) * n_loc].astype(dw_ref.dtype)

    return pl.pallas_call(
        body, name=name, grid=(nj, S // tm),
        in_specs=[pl.BlockSpec((tm, D), lambda j, i: (i, 0)), pl.BlockSpec((tm, cw), lambda j, i: (i, j))],
        out_specs=pl.BlockSpec((per, D, n_loc), lambda j, i: (j, 0, 0)),
        out_shape=jax.ShapeDtypeStruct((N_DEV, D, n_loc), WIRE_DTYPE),
        scratch_shapes=[pltpu.VMEM((D, cw), F32)],
        compiler_params=_params(("parallel", "arbitrary")),
    )(h, dz)


def _my_place():
    return lax.axis_index("x"), lax.axis_index("y"), lax.axis_index("c")


def _peer(x, y, c, k):
    px = 1 - x if k & 4 else x
    py = 1 - y if k & 2 else y
    pc = 1 - c if k & 1 else c
    return (px, py, pc), 4 * px + 2 * py + pc


HBM_SPEC = pl.BlockSpec(memory_space=pltpu.HBM)
SEM_SPEC = pl.BlockSpec(memory_space=pltpu.SEMAPHORE)
SPLIT_EFFECT = pltpu.SideEffectType.DATAFLOW_SIDE_EFFECTING
N_PEERS = N_DEV - 1


def _exchange_copies(srcs, lands, send_sems, recv_sems, gather, arrival):
    x, y, c = _my_place()
    me = 4 * x + 2 * y + c
    copies = []
    for t in range(len(srcs)):
        for k in range(1, N_DEV):
            peer, pidx = _peer(x, y, c, k)
            copies.append(pltpu.make_async_remote_copy(
                src_ref=srcs[t] if gather else srcs[t].at[pidx],
                dst_ref=lands[t].at[pidx if arrival else me], send_sem=send_sems.at[t * N_PEERS + k - 1],
                recv_sem=recv_sems.at[t * N_PEERS + k - 1], device_id=peer, device_id_type=MESH))
    return copies


def _exchange_start(groups, gather, name):
    sizes = [len(g) for g in groups]
    srcs = [pltpu.with_memory_space_constraint(a, pltpu.HBM) for g in groups for a in g]
    land_shape = lambda a: ((N_DEV,) + a.shape) if gather else a.shape
    lands = [pltpu.with_memory_space_constraint(lax.empty(land_shape(a), a.dtype), pltpu.HBM) for a in srcs]
    n = len(srcs)
    n_g = len(groups)

    def body(*refs):
        src_refs, land_refs = refs[:n], refs[n:2 * n]
        sem_refs = refs[4 * n:4 * n + 2 * n_g]
        token = refs[-1]
        off = 0
        for gi, sz in enumerate(sizes):
            for send in _exchange_copies(src_refs[off:off + sz], land_refs[off:off + sz],
                                         sem_refs[2 * gi], sem_refs[2 * gi + 1], gather, False):
                send.start()
            off += sz
        token[...] = jnp.zeros_like(token)

    sem_shapes = [pltpu.SemaphoreType.DMA((sz * N_PEERS,)) for sz in sizes for _ in range(2)]
    outs = pl.pallas_call(
        body, name=name,
        in_specs=[HBM_SPEC] * (2 * n),
        out_specs=[HBM_SPEC] * (2 * n) + [SEM_SPEC] * (2 * n_g) + [pl.BlockSpec(memory_space=pltpu.VMEM)],
        out_shape=[pltpu.HBM(a.shape, a.dtype) for a in srcs + lands] + sem_shapes
                  + [jax.ShapeDtypeStruct((SUBLANES, LANES), F32)],
        input_output_aliases={i: i for i in range(2 * n)},
        compiler_params=pltpu.CompilerParams(has_side_effects=SPLIT_EFFECT),
    )(*srcs, *lands)
    handles, off = [], 0
    for gi, sz in enumerate(sizes):
        handles.append((outs[2 * n + 2 * gi], outs[2 * n + 2 * gi + 1], outs[off:off + sz], outs[n + off:n + off + sz]))
        off += sz
    return handles, outs[-1]


def _exchange_wait(handle, after, gather, name):
    send_sems, recv_sems, srcs, lands = handle
    n = len(srcs)

    def body(*refs):
        src_refs, land_refs = refs[:n], refs[n:2 * n]
        for send in _exchange_copies(src_refs, land_refs, refs[2 * n], refs[2 * n + 1], gather, False):
            send.wait_send()
        for arrival in _exchange_copies(src_refs, land_refs, refs[2 * n], refs[2 * n + 1], gather, True):
            arrival.wait_recv()

    outs = pl.pallas_call(
        body, name=name,
        in_specs=[HBM_SPEC] * (2 * n) + [SEM_SPEC, SEM_SPEC, pl.BlockSpec(memory_space=pl.ANY)],
        out_specs=[HBM_SPEC] * (2 * n),
        out_shape=[pltpu.HBM(a.shape, a.dtype) for a in list(srcs) + list(lands)],
        input_output_aliases={i: i for i in range(2 * n)},
        compiler_params=pltpu.CompilerParams(has_side_effects=SPLIT_EFFECT),
    )(*srcs, *lands, send_sems, recv_sems, after)
    return outs[:n], outs[n:]


def _allreduce(buf, name):
    R = buf.shape[0]

    def body(x_ref, out_ref, recv_ref, send_sems, recv_sems):
        x, y, c = _my_place()
        out_ref[...] = x_ref[...]
        for s, k in enumerate((1, 4, 2)):
            peer, _ = _peer(x, y, c, k)
            cp = pltpu.make_async_remote_copy(
                src_ref=out_ref, dst_ref=recv_ref.at[s], send_sem=send_sems.at[s], recv_sem=recv_sems.at[s],
                device_id=peer, device_id_type=MESH)
            cp.start()
            cp.wait()
            out_ref[...] = out_ref[...] + recv_ref[s]

    vm = pl.BlockSpec(memory_space=pltpu.VMEM)
    return pl.pallas_call(
        body, name=name, in_specs=[vm], out_specs=vm,
        out_shape=jax.ShapeDtypeStruct((R, LANES), F32),
        scratch_shapes=[pltpu.VMEM((3, R, LANES), F32), pltpu.SemaphoreType.DMA((3,)), pltpu.SemaphoreType.DMA((3,))],
        compiler_params=pltpu.CompilerParams(has_side_effects=True, vmem_limit_bytes=VMEM_LIMIT),
    )(buf)


def _adamw_math(w, g, m, v):
    m = ADAM_B1 * m + (1.0 - ADAM_B1) * g
    v = ADAM_B2 * v + (1.0 - ADAM_B2) * (g * g)
    m_hat = m / (1.0 - ADAM_B1 ** ADAM_STEP)
    v_hat = v / (1.0 - ADAM_B2 ** ADAM_STEP)
    delta = -ADAM_LR * (m_hat / (jnp.sqrt(v_hat) + ADAM_EPS) + ADAM_WD * w)
    return delta, m, v


def _adamw_summed(parts, w, m, v, tr, name):
    R, C = w.shape

    def body(p_ref, w_ref, m_ref, v_ref, g_ref, d_ref, nm_ref, nv_ref):
        g = p_ref[0].astype(F32)
        for j in range(1, N_DEV):
            g = g + p_ref[j].astype(F32)
        g_ref[...] = g
        d_ref[...], nm_ref[...], nv_ref[...] = _adamw_math(w_ref[...], g, m_ref[...], v_ref[...])

    row = pl.BlockSpec((tr, C), lambda i: (i, 0))
    return pl.pallas_call(
        body, name=name, grid=(R // tr,),
        in_specs=[pl.BlockSpec((N_DEV, tr, C), lambda i: (0, i, 0)), row, row, row],
        out_specs=[row] * 4, out_shape=[jax.ShapeDtypeStruct((R, C), F32)] * 4,
        compiler_params=_params(("parallel",)),
    )(parts, w, m, v)


def _adamw_small(w, g, m, v, name):
    def body(w_ref, g_ref, m_ref, v_ref, d_ref, nm_ref, nv_ref):
        d_ref[...], nm_ref[...], nv_ref[...] = _adamw_math(w_ref[...], g_ref[...], m_ref[...], v_ref[...])

    vm = pl.BlockSpec(memory_space=pltpu.VMEM)
    return pl.pallas_call(
        body, name=name, in_specs=[vm] * 4, out_specs=[vm] * 3,
        out_shape=[jax.ShapeDtypeStruct(w.shape, F32)] * 3,
        compiler_params=pltpu.CompilerParams(vmem_limit_bytes=VMEM_LIMIT),
    )(w, g, m, v)


def _pack(arrays):
    flat = jnp.concatenate([a.reshape(-1) for a in arrays])
    pad = (-flat.shape[0]) % (SUBLANES * LANES)
    return jnp.pad(flat, (0, pad)).reshape(-1, LANES)


def _unpack(buf, like):
    flat = buf.reshape(-1)
    out, off = [], 0
    for a in like:
        out.append(flat[off:off + a.size].reshape(a.shape))
        off += a.size
    return out


def _block_diag(w):
    eye = jnp.eye(N_HEADS, dtype=w.dtype)
    return jnp.einsum('hij,hk->hikj', w, eye).reshape(GROUP_W, GROUP_W)


def _diag_blocks(w):
    return jnp.einsum('hihj->hij', w.reshape(N_HEADS, HEAD_DIM, N_HEADS, HEAD_DIM))


def _pad_rows(a):
    return jnp.pad(a, ((0, SUBLANES - a.shape[0]), (0, 0)))


def _mixer_params(l, conv_a_w, conv_r_w, conv_r_b, lru_wa, lru_ba, lru_wx, lru_bx, lru_lambda, gmlp_norm_g,
                  gmlp_ws, gmlp_bs):
    tril = jnp.tril(jnp.ones((GMLP_CHUNK, GMLP_CHUNK), dtype=bool))
    vec = jnp.stack([conv_r_b[l], lru_ba[l], lru_bx[l], lru_lambda[l], gmlp_norm_g[l]])
    return {
        "wA": _pad_rows(conv_a_w[l]), "wR": _pad_rows(conv_r_w[l]), "vec": _pad_rows(vec),
        "wa": _block_diag(lru_wa[l]).astype(MXU_DTYPE), "wx": _block_diag(lru_wx[l]).astype(MXU_DTYPE),
        "ws": jnp.where(tril[None], gmlp_ws[l], 0.0).astype(MXU_DTYPE),
        "bs": jnp.repeat(jnp.transpose(gmlp_bs[l]), HEAD_DIM, axis=1),
    }


def _local_step(x, loss_target, norm_g, get_weights, emit_wgrads, conv_a_w, conv_r_w, conv_r_b, lru_wa, lru_ba,
                lru_wx, lru_bx, lru_lambda, gmlp_norm_g, gmlp_ws, gmlp_bs, final_g):
    depth = norm_g.shape[0]
    D = x.shape[1]
    small = (conv_a_w, conv_r_w, conv_r_b, lru_wa, lru_ba, lru_wx, lru_bx, lru_lambda, gmlp_norm_g, gmlp_ws, gmlp_bs)
    saved = []
    for l in range(depth):
        mp = _mixer_params(l, *small)
        w_in_l, w_out_l = get_weights(l, x)
        z, z_g, *qkv = _norm_inproj(x, norm_g[l].reshape(1, D), w_in_l, f"norm_inproj_{l}")
        y_abc, hs = _mix_fwd(z, mp, f"mix_fwd_{l}")
        attn = [_attn_fwd(qkv[p], dil, f"attn_fwd_d{dil}_{l}") for p, dil in enumerate(ATTN_DILATIONS)]
        x_new, y, o, *lse = _outproj(x, z_g, y_abc, attn, w_out_l, f"outproj_{l}")
        saved.append((x, z, z_g, qkv, hs, y, o, lse, mp, w_in_l, w_out_l))
        x = x_new
    dx, loss, d_final_g = _loss_head(x, final_g.reshape(1, D), loss_target, "loss_head")
    grads = []
    token = None
    for l in reversed(range(depth)):
        x_l, z, z_g, qkv, hs, y, o, lse, mp, w_in_l, w_out_l = saved[l]
        if token is not None:
            mp = dict(mp, vec=mp["vec"] + token[0, 0])
        dy, dw_out = _outproj_bwd(dx, y, w_out_l, f"outproj_bwd_{l}")
        (dz_abc, dz_g, do1, do4, do16, dl1, dl4, dl16, dwA, dwR, dvec, dwa, dwx, dws, dbs) = _mix_bwd(
            z, z_g, dy, hs, o, mp, f"mix_bwd_{l}")
        dqkv = [_attn_bwd(qkv[p], do, lse[p], dl, dil, f"attn_bwd_d{dil}_{l}")
                for p, (dil, do, dl) in enumerate(zip(ATTN_DILATIONS, (do1, do4, do16), (dl1, dl4, dl16)))]
        dx, dz, h, dng = _inproj_bwd(x_l, norm_g[l].reshape(1, D), dx, dz_abc, dqkv, dz_g, w_in_l,
                                     f"inproj_bwd_{l}")
        dw_in = _inproj_wgrad(h, dz, f"inproj_wgrad_{l}")
        token = emit_wgrads(l, dw_in, dw_out)
        grads.append({
            "norm_g": dng[0],
            "conv_a_w": dwA[:conv_a_w.shape[1]], "conv_r_w": dwR[:conv_r_w.shape[1]],
            "conv_r_b": dvec[0], "lru_ba": dvec[1], "lru_bx": dvec[2], "lru_lambda": dvec[3], "gmlp_norm_g": dvec[4],
            "lru_wa": _diag_blocks(dwa), "lru_wx": _diag_blocks(dwx), "gmlp_ws": dws,
            "gmlp_bs": jnp.transpose(dbs[:, :N_HEADS]),
        })
    grads = grads[::-1]
    stacked = {k: jnp.stack([g[k] for g in grads]) for k in grads[0]}
    stacked["final_g"] = d_final_g[0]
    return loss[0, 0], dx, stacked


SMALL_NAMES = ("norm_g", "conv_a_w", "conv_r_w", "conv_r_b", "lru_wa", "lru_ba", "lru_wx", "lru_bx", "lru_lambda",
               "gmlp_norm_g", "gmlp_ws", "gmlp_bs", "final_g")
WEIGHT_NAMES = ("norm_g", "w_in", "conv_a_w", "conv_r_w", "conv_r_b", "lru_wa", "lru_ba", "lru_wx", "lru_bx",
                "lru_lambda", "gmlp_norm_g", "gmlp_ws", "gmlp_bs", "w_out", "final_g")


def kernel(x, norm_g, w_in, conv_a_w, conv_r_w, conv_r_b, lru_wa, lru_ba, lru_wx, lru_bx, lru_lambda, gmlp_norm_g, gmlp_ws, gmlp_bs, w_out, final_g, loss_target, m_norm_g, m_w_in, m_conv_a_w, m_conv_r_w, m_conv_r_b, m_lru_wa, m_lru_ba, m_lru_wx, m_lru_bx, m_lru_lambda, m_gmlp_norm_g, m_gmlp_ws, m_gmlp_bs, m_w_out, m_final_g, v_norm_g, v_w_in, v_conv_a_w, v_conv_r_w, v_conv_r_b, v_lru_wa, v_lru_ba, v_lru_wx, v_lru_bx, v_lru_lambda, v_gmlp_norm_g, v_gmlp_ws, v_gmlp_bs, v_w_out, v_final_g):
    w = dict(norm_g=norm_g, w_in=w_in, conv_a_w=conv_a_w, conv_r_w=conv_r_w, conv_r_b=conv_r_b, lru_wa=lru_wa,
             lru_ba=lru_ba, lru_wx=lru_wx, lru_bx=lru_bx, lru_lambda=lru_lambda, gmlp_norm_g=gmlp_norm_g,
             gmlp_ws=gmlp_ws, gmlp_bs=gmlp_bs, w_out=w_out, final_g=final_g)
    m = dict(norm_g=m_norm_g, w_in=m_w_in, conv_a_w=m_conv_a_w, conv_r_w=m_conv_r_w, conv_r_b=m_conv_r_b,
             lru_wa=m_lru_wa, lru_ba=m_lru_ba, lru_wx=m_lru_wx, lru_bx=m_lru_bx, lru_lambda=m_lru_lambda,
             gmlp_norm_g=m_gmlp_norm_g, gmlp_ws=m_gmlp_ws, gmlp_bs=m_gmlp_bs, w_out=m_w_out, final_g=m_final_g)
    v = dict(norm_g=v_norm_g, w_in=v_w_in, conv_a_w=v_conv_a_w, conv_r_w=v_conv_r_w, conv_r_b=v_conv_r_b,
             lru_wa=v_lru_wa, lru_ba=v_lru_ba, lru_wx=v_lru_wx, lru_bx=v_lru_bx, lru_lambda=v_lru_lambda,
             gmlp_norm_g=v_gmlp_norm_g, gmlp_ws=v_gmlp_ws, gmlp_bs=v_gmlp_bs, w_out=v_w_out, final_g=v_final_g)
    depth, D, n_loc = w_in.shape
    e_loc = w_out.shape[1]
    cx, cy, cc = _my_place()
    me = 4 * cx + 2 * cy + cc

    def with_own(srcs, lands, gather):
        own = [s[None] if gather else lax.dynamic_slice_in_dim(s, me, 1, axis=0) for s in srcs]
        return [lax.dynamic_update_slice_in_dim(ld, o, me, axis=0) for ld, o in zip(lands, own)]

    w_in_w, w_out_w = w_in.astype(MXU_DTYPE), w_out.astype(MXU_DTYPE)
    c_loc = conv_a_w.shape[2]
    taps = (conv_a_w, conv_r_w)
    groups = [[w_in_w[l], w_out_w[l]] for l in range(depth)]
    groups[0].append(_pack(taps))
    gathers, _ = _exchange_start(groups, True, "gather_start")

    def gathered(l, after):
        srcs, lands = _exchange_wait(gathers[l], after, True, f"gather_wait_{l}")
        g_in, g_out, *rest = with_own(srcs, lands, True)
        return (jnp.transpose(g_in, (1, 0, 2)).reshape(D, N_DEV * n_loc), g_out.reshape(N_DEV * e_loc, D)), rest

    first_weights, (g_taps,) = gathered(0, x)
    g_taps = g_taps.reshape(N_DEV, -1)
    conv_full, off = [], 0
    for a in taps:
        part = g_taps[:, off:off + a.size].reshape((N_DEV,) + a.shape)
        conv_full.append(jnp.transpose(part, (1, 2, 0, 3)).reshape(a.shape[:2] + (N_DEV * c_loc,)))
        off += a.size
    conv_a_full, conv_r_full = conv_full

    def get_weights(l, after):
        return first_weights if l == 0 else gathered(l, after)[0]

    scatters = {}

    def emit_wgrads(l, dw_in, dw_out):
        handles, token = _exchange_start([[dw_in, dw_out.reshape(N_DEV, e_loc, D)]], False, f"scatter_start_{l}")
        scatters[l] = handles[0]
        return token

    loss, grad_x, g = _local_step(
        x[0], loss_target[0], norm_g, get_weights, emit_wgrads, conv_a_full, conv_r_full, conv_r_b, lru_wa, lru_ba,
        lru_wx, lru_bx, lru_lambda, gmlp_norm_g, gmlp_ws, gmlp_bs, final_g)
    loss = lax.psum(loss, ("x", "y", "c"))

    per_layer = {"w_in": [], "w_out": []}
    for l in reversed(range(depth)):
        srcs, lands = _exchange_wait(scatters[l], grad_x, False, f"scatter_wait_{l}")
        r_in, r_out = with_own(srcs, lands, False)
        per_layer["w_in"].append(_adamw_summed(r_in, w_in[l], m_w_in[l], v_w_in[l], 512, f"adamw_w_in_{l}"))
        per_layer["w_out"].append(_adamw_summed(r_out, w_out[l], m_w_out[l], v_w_out[l], 128, f"adamw_w_out_{l}"))
    big = {k: [jnp.stack(parts) for parts in zip(*res[::-1])] for k, res in per_layer.items()}

    g_small = [g[k] for k in SMALL_NAMES]
    g_small = _unpack(_allreduce(_pack(g_small), "allreduce_small_grads"), g_small)
    g_small = dict(zip(SMALL_NAMES, g_small))
    for k in ("conv_a_w", "conv_r_w"):
        g_small[k] = lax.dynamic_slice_in_dim(g_small[k], me * c_loc, c_loc, axis=2)
    packs = [_pack([d[k] for k in SMALL_NAMES]) for d in (w, g_small, m, v)]
    res = _adamw_small(*packs, "adamw_small")
    like = [w[k] for k in SMALL_NAMES]
    d_s, m_s, v_s = (dict(zip(SMALL_NAMES, _unpack(r, like))) for r in res)

    grad, delta, new_m, new_v = {}, {}, {}, {}
    for k in WEIGHT_NAMES:
        if k in big:
            grad[k], delta[k], new_m[k], new_v[k] = big[k]
        else:
            grad[k], delta[k], new_m[k], new_v[k] = g_small[k], d_s[k], m_s[k], v_s[k]
    return (loss, grad_x[None], *[grad[k] for k in WEIGHT_NAMES], *[delta[k] for k in WEIGHT_NAMES],
            *[new_m[k] for k in WEIGHT_NAMES], *[new_v[k] for k in WEIGHT_NAMES])
```

```python
import functools
import math

import jax
import jax.numpy as jnp
from jax import lax
from jax.experimental import pallas as pl
from jax.experimental.pallas import tpu as pltpu

F32 = jnp.float32
MXU_DTYPE = jnp.bfloat16
WIRE_DTYPE = jnp.bfloat16
MESH = pl.DeviceIdType.MESH

N_DEV = 8
GROUP_W = 256
N_HEADS = 4
HEAD_DIM = 64
N_CHUNKS = 13
N_ABC = 9
GMLP_CHUNK = 128
ATTN_BLOCK = 128
ATTN_BLOCKS_PER_STEP = 2
ATTN_DILATIONS = (1, 4, 16)
NORM_EPS = 1e-6
RG_C = 8.0
SUBLANES = 8
LANES = 128
VMEM_LIMIT = 56 * 1024 * 1024

ADAM_LR = 0.001
ADAM_B1 = 0.9
ADAM_B2 = 0.999
ADAM_EPS = 1e-08
ADAM_WD = 0.01
ADAM_STEP = 10

TM_MIX = 256
TM_MM = 512


def _params(sem, vmem=VMEM_LIMIT):
    return pltpu.CompilerParams(dimension_semantics=sem, vmem_limit_bytes=vmem)


def _mm(a, b):
    return jnp.dot(a.astype(MXU_DTYPE), b.astype(MXU_DTYPE), preferred_element_type=F32)


def _mm_tn(a, b):
    return lax.dot_general(a.astype(MXU_DTYPE), b.astype(MXU_DTYPE), (((0,), (0,)), ((), ())),
                           preferred_element_type=F32)


def _mm_nt(a, b):
    return lax.dot_general(a.astype(MXU_DTYPE), b.astype(MXU_DTYPE), (((1,), (1,)), ((), ())),
                           preferred_element_type=F32)


def _sigmoid(x):
    return 0.5 * jnp.tanh(0.5 * x) + 0.5


def _silu_and_grad(x):
    s = _sigmoid(x)
    return x * s, s * (1.0 + x * (1.0 - s))


_GELU_K = math.sqrt(2.0 / math.pi)
_GELU_C = 0.044715


def _gelu_and_grad(x):
    x2 = x * x
    t = jnp.tanh(_GELU_K * (x + _GELU_C * x * x2))
    val = 0.5 * x * (1.0 + t)
    grad = 0.5 * (1.0 + t) + 0.5 * x * (1.0 - t * t) * (_GELU_K * (1.0 + 3.0 * _GELU_C * x2))
    return val, grad


def _gelu(x):
    return 0.5 * x * (1.0 + jnp.tanh(_GELU_K * (x + _GELU_C * x * x * x)))


def _expm1_nonpos(u):
    poly = u * (1.0 + u / 2.0 * (1.0 + u / 3.0 * (1.0 + u / 4.0 * (1.0 + u / 5.0 * (1.0 + u / 6.0 * (
        1.0 + u / 7.0 * (1.0 + u / 8.0)))))))
    return jnp.where(u > -0.25, poly, jnp.exp(u) - 1.0)


def _softplus(x):
    return jnp.maximum(x, 0.0) + jnp.log(1.0 + jnp.exp(-jnp.abs(x)))


def _shift_down(t, halo, k):
    rolled = pltpu.roll(t, k, 0)
    hr = pltpu.roll(halo, k, 0)
    row = lax.broadcasted_iota(jnp.int32, halo.shape, 0)
    first = jnp.where(row < k, hr, rolled[0:SUBLANES])
    return jnp.concatenate([first, rolled[SUBLANES:]], axis=0)


def _shift_up(t, nxt, k):
    tm = t.shape[0]
    rolled = pltpu.roll(t, tm - k, 0)
    nr = pltpu.roll(nxt, SUBLANES - k, 0)
    row = lax.broadcasted_iota(jnp.int32, nxt.shape, 0)
    last = jnp.where(row >= SUBLANES - k, nr, rolled[tm - SUBLANES:tm])
    return jnp.concatenate([rolled[:tm - SUBLANES], last], axis=0)


def _scan_fwd(a, b):
    tm = a.shape[0]
    row = lax.broadcasted_iota(jnp.int32, a.shape, 0)
    s = 1
    while s < tm:
        a_s = pltpu.roll(a, s, 0)
        b_s = pltpu.roll(b, s, 0)
        m = row >= s
        b = jnp.where(m, a * b_s + b, b)
        a = jnp.where(m, a * a_s, a)
        s *= 2
    return a, b


def _scan_rev(a, g):
    tm = a.shape[0]
    row = lax.broadcasted_iota(jnp.int32, a.shape, 0)
    s = 1
    while s < tm:
        a_s = pltpu.roll(a, tm - s, 0)
        g_s = pltpu.roll(g, tm - s, 0)
        m = row < tm - s
        g = jnp.where(m, g + a * g_s, g)
        a = jnp.where(m, a * a_s, a)
        s *= 2
    return g


def _lane_scratch(tm, w):
    return pltpu.VMEM((w // LANES, tm, LANES), F32)


def _put(scr_ref, val):
    for c in range(scr_ref.shape[0]):
        scr_ref[c] = val[:, c * LANES:(c + 1) * LANES].astype(F32)


def _get(scr_ref):
    return jnp.concatenate([scr_ref[c] for c in range(scr_ref.shape[0])], axis=1)


def _deinterleave(src_ref, dst_ref, dil):
    nc, tm, _ = src_ref.shape
    w = nc * LANES
    for r in range(dil):
        for c in range(nc):
            piece = src_ref[pl.ds(c, 1), pl.ds(r, tm // dil, stride=dil), :][0] if dil > 1 else src_ref[c]
            dst_ref[:, r * w + c * LANES:r * w + (c + 1) * LANES] = piece.astype(dst_ref.dtype)


def _interleave(src_ref, dst_ref, dil):
    nc, tm, _ = dst_ref.shape
    w = nc * LANES
    for r in range(dil):
        for c in range(nc):
            dst_ref[pl.ds(c, 1), pl.ds(r, tm // dil, stride=dil), :] = (
                src_ref[:, r * w + c * LANES:r * w + (c + 1) * LANES].astype(F32)[None])


def _dilated_spec(tm, w, dil, index=lambda i: i):
    return pl.BlockSpec((tm // dil, dil * w), lambda i: (index(i), 0))


def _dilated_shape(S, w, dil, dtype):
    return jax.ShapeDtypeStruct((S // dil, dil * w), dtype)


def _head_masks(shape):
    lane = lax.broadcasted_iota(jnp.int32, shape, 1)
    return [(lane >= h * HEAD_DIM) & (lane < (h + 1) * HEAD_DIM) for h in range(N_HEADS)]


def _colsum(v):
    return jnp.sum(v, axis=0, keepdims=True)


def _norm_inproj(x, g, w, name):
    S, D = x.shape
    N = w.shape[1]
    tm = TM_MM
    n_abc = N_ABC * GROUP_W
    n_qkv = 3 * GROUP_W

    def body(x_ref, g_ref, w_ref, zabc_ref, zg_ref, q1_ref, q4_ref, q16_ref, qkv_ref):
        xv = x_ref[...]
        r = lax.rsqrt(jnp.mean(xv * xv, axis=-1, keepdims=True) + NORM_EPS)
        h = ((xv * r) * g_ref[...]).astype(MXU_DTYPE)
        zabc_ref[...] = jnp.dot(h, w_ref[:, 0:n_abc], preferred_element_type=F32)
        _put(qkv_ref, jnp.dot(h, w_ref[:, n_abc:n_abc + n_qkv], preferred_element_type=F32))
        zg_ref[...] = jnp.dot(h, w_ref[:, n_abc + n_qkv:], preferred_element_type=F32)
        for dil, ref in zip(ATTN_DILATIONS, (q1_ref, q4_ref, q16_ref)):
            _deinterleave(qkv_ref, ref, dil)

    row = lambda wd: pl.BlockSpec((tm, wd), lambda i: (i, 0))
    return pl.pallas_call(
        body, name=name, grid=(S // tm,),
        in_specs=[row(D), pl.BlockSpec((1, D), lambda i: (0, 0)), pl.BlockSpec((D, N), lambda i: (0, 0))],
        out_specs=[row(n_abc), row(GROUP_W)] + [_dilated_spec(tm, n_qkv, dil) for dil in ATTN_DILATIONS],
        out_shape=[jax.ShapeDtypeStruct((S, n_abc), F32), jax.ShapeDtypeStruct((S, GROUP_W), F32)]
                  + [_dilated_shape(S, n_qkv, dil, MXU_DTYPE) for dil in ATTN_DILATIONS],
        scratch_shapes=[_lane_scratch(tm, n_qkv)],
        compiler_params=_params(("parallel",)),
    )(x, g, w)


def _conv_a(z_of, halo_of, w_ref):
    p = z_of(2) * z_of(0)
    p_h = halo_of(2) * halo_of(0)
    cv = w_ref[2:3, :] * p + w_ref[1:2, :] * _shift_down(p, p_h, 1) + w_ref[0:1, :] * _shift_down(p, p_h, 2)
    return p, p_h, cv


def _lru_gates(z_of, halo_of, wr_ref, vec_ref, wa_ref, wx_ref):
    rx = z_of(4)
    rx_h = halo_of(4)
    sh = [rx, _shift_down(rx, rx_h, 1), _shift_down(rx, rx_h, 2), _shift_down(rx, rx_h, 3)]
    xc = (wr_ref[3:4, :] * sh[0] + wr_ref[2:3, :] * sh[1] + wr_ref[1:2, :] * sh[2]
          + wr_ref[0:1, :] * sh[3] + vec_ref[0:1, :])
    ga = _sigmoid(jnp.dot(xc.astype(MXU_DTYPE), wa_ref[...], preferred_element_type=F32) + vec_ref[1:2, :])
    gi = _sigmoid(jnp.dot(xc.astype(MXU_DTYPE), wx_ref[...], preferred_element_type=F32) + vec_ref[2:3, :])
    sp = _softplus(-vec_ref[3:4, :])
    log_a = (-RG_C * ga) * sp
    a = jnp.exp(log_a)
    mult = jnp.sqrt(-_expm1_nonpos(2.0 * log_a))
    return xc, sh, ga, gi, a, mult, sp


def _gmlp_fwd(z_of, vec_ref, ws_ref, bs_ref, tm):
    u = _gelu(z_of(6))
    gv = _gelu(z_of(7))
    rr = lax.rsqrt(jnp.mean(gv * gv, axis=-1, keepdims=True) + NORM_EPS)
    vn = (gv * rr) * vec_ref[4:5, :]
    masks = _head_masks((GMLP_CHUNK, GROUP_W))
    parts = []
    for c in range(tm // GMLP_CHUNK):
        vc = vn[c * GMLP_CHUNK:(c + 1) * GMLP_CHUNK].astype(MXU_DTYPE)
        acc = bs_ref[...]
        for h in range(N_HEADS):
            acc = acc + jnp.where(masks[h], jnp.dot(ws_ref[h], vc, preferred_element_type=F32), 0.0)
        parts.append(acc)
    return u, gv, rr, vn, jnp.concatenate(parts, axis=0)


def _mix_specs(tm, S, order):
    const2 = lambda shape: pl.BlockSpec(shape, lambda i: (0, 0))
    return [const2((SUBLANES, GROUP_W)), const2((SUBLANES, GROUP_W)), const2((SUBLANES, GROUP_W)),
            const2((GROUP_W, GROUP_W)), const2((GROUP_W, GROUP_W)),
            pl.BlockSpec((N_HEADS, GMLP_CHUNK, GMLP_CHUNK), lambda i: (0, 0, 0)),
            const2((GMLP_CHUNK, GROUP_W))]


def _mix_fwd(z, mp, name):
    S = z.shape[0]
    tm = TM_MIX
    hb = tm // SUBLANES
    wcols = N_ABC * GROUP_W

    def body(z_ref, zh_ref, wA_ref, wR_ref, vec_ref, wa_ref, wx_ref, ws_ref, bs_ref, y_ref, h_ref, carry_ref):
        i = pl.program_id(0)

        @pl.when(i == 0)
        def _():
            carry_ref[...] = jnp.zeros_like(carry_ref)

        not_first = i > 0
        z_of = lambda c: z_ref[:, c * GROUP_W:(c + 1) * GROUP_W]
        halo_of = lambda c: jnp.where(not_first, zh_ref[:, c * GROUP_W:(c + 1) * GROUP_W], 0.0)

        _, _, cv = _conv_a(z_of, halo_of, wA_ref)
        y_ref[:, 0:GROUP_W] = (z_of(1) * cv * _silu_and_grad(z_of(3))[0]).astype(y_ref.dtype)

        xc, _, _, gi, a, mult, _ = _lru_gates(z_of, halo_of, wR_ref, vec_ref, wa_ref, wx_ref)
        b = mult * (gi * xc)
        acum, bcum = _scan_fwd(a, b)
        h = bcum + acum * carry_ref[SUBLANES - 1:SUBLANES, :]
        h_ref[...] = h
        carry_ref[...] = h[tm - SUBLANES:tm]
        y_ref[:, GROUP_W:2 * GROUP_W] = (h * _silu_and_grad(z_of(5))[0]).astype(y_ref.dtype)

        u, _, _, _, sp = _gmlp_fwd(z_of, vec_ref, ws_ref, bs_ref, tm)
        y_ref[:, 2 * GROUP_W:3 * GROUP_W] = (u * sp * _silu_and_grad(z_of(8))[0]).astype(y_ref.dtype)

    return pl.pallas_call(
        body, name=name, grid=(S // tm,),
        in_specs=[pl.BlockSpec((tm, wcols), lambda i: (i, 0)),
                  pl.BlockSpec((SUBLANES, wcols), lambda i: (jnp.maximum(i * hb - 1, 0), 0))]
                 + _mix_specs(tm, S, "fwd"),
        out_specs=[pl.BlockSpec((tm, 3 * GROUP_W), lambda i: (i, 0)),
                   pl.BlockSpec((tm, GROUP_W), lambda i: (i, 0))],
        out_shape=[jax.ShapeDtypeStruct((S, 3 * GROUP_W), MXU_DTYPE), jax.ShapeDtypeStruct((S, GROUP_W), F32)],
        scratch_shapes=[pltpu.VMEM((SUBLANES, GROUP_W), F32)],
        compiler_params=_params(("arbitrary",)),
    )(z, z, mp["wA"], mp["wR"], mp["vec"], mp["wa"], mp["wx"], mp["ws"], mp["bs"])


_NEG = -1e30


def _slope(h):
    return 2.0 ** (-8.0 * (h + 1) / N_HEADS)


def _attn_bias(dil, offsets, n_keys):
    shape = (ATTN_BLOCK, n_keys)
    qi = lax.broadcasted_iota(jnp.int32, shape, 0)
    ki = lax.broadcasted_iota(jnp.int32, shape, 1)
    blocks = []
    for f in offsets:
        delta = qi + f - ki
        valid = (delta >= 0) & (delta <= ATTN_BLOCK)
        dist = (delta * dil).astype(F32)
        for h in range(N_HEADS):
            blocks.append(jnp.where(valid, -_slope(h) * dist, _NEG))
    return jnp.concatenate(blocks, axis=0)


def _stack_heads(t, masks):
    return jnp.concatenate([jnp.where(m, t, jnp.zeros_like(t)) for m in masks], axis=0)


def _unstack_heads(t4, masks, base=0):
    out = t4[base * ATTN_BLOCK:(base + 1) * ATTN_BLOCK]
    for h in range(1, N_HEADS):
        out = jnp.where(masks[h], t4[(base + h) * ATTN_BLOCK:(base + h + 1) * ATTN_BLOCK], out)
    return out


def _attn_fwd(qkv, dil, name):
    rows = qkv.shape[0]
    nb = rows // ATTN_BLOCK
    scale = 1.0 / math.sqrt(HEAD_DIM)
    B = ATTN_BLOCK
    per_step = ATTN_BLOCKS_PER_STEP

    def body(q_ref, kc_ref, kp_ref, vc_ref, vp_ref, o_ref, l_ref, bias_ref):
        n = pl.program_id(1)

        @pl.when(n == 0)
        def _():
            bias_ref[...] = _attn_bias(dil, (B,), 2 * B)

        masks = _head_masks((B, GROUP_W))
        for j in range(per_step):
            own = slice(j * B, (j + 1) * B)
            before = slice((j - 1) * B, j * B)
            qs = _stack_heads(q_ref[own], masks)
            keys = jnp.concatenate([kp_ref[...] if j == 0 else kc_ref[before], kc_ref[own]], axis=0)
            vals = jnp.concatenate([vp_ref[...] if j == 0 else vc_ref[before], vc_ref[own]], axis=0)
            s = _mm_nt(qs, keys) * scale + bias_ref[...]
            if j == 0:
                key_col = lax.broadcasted_iota(jnp.int32, s.shape, 1)
                s = jnp.where((n == 0) & (key_col < B), _NEG, s)
            m = jnp.max(s, axis=-1, keepdims=True)
            p = jnp.exp(s - m)
            l = jnp.sum(p, axis=-1, keepdims=True)
            o4 = jnp.dot(p.astype(MXU_DTYPE), vals, preferred_element_type=F32)
            o_ref[own] = _unstack_heads(o4, masks) / _unstack_heads(jnp.broadcast_to(l, o4.shape), masks)
            l_ref[own] = _unstack_heads(jnp.broadcast_to(m + jnp.log(l), o4.shape), masks)

    blk = (per_step * B, GROUP_W)
    cur = lambda c: pl.BlockSpec(blk, lambda r, n: (n, r * 3 + c))
    prev = lambda c: pl.BlockSpec((B, GROUP_W), lambda r, n: (jnp.maximum(n * per_step - 1, 0), r * 3 + c))
    out = pl.BlockSpec(blk, lambda r, n: (n, r))
    return pl.pallas_call(
        body, name=name, grid=(dil, nb // per_step),
        in_specs=[cur(0), cur(1), prev(1), cur(2), prev(2)],
        out_specs=[out, out],
        out_shape=[jax.ShapeDtypeStruct((rows, dil * GROUP_W), F32)] * 2,
        scratch_shapes=[pltpu.VMEM((N_HEADS * ATTN_BLOCK, 2 * ATTN_BLOCK), F32)],
        compiler_params=_params(("parallel", "arbitrary")),
    )(qkv, qkv, qkv, qkv, qkv)


def _outproj(x, z_g, y_abc, attn, w_out, name):
    S, D = x.shape
    tm = TM_MM
    n_abc = 3 * GROUP_W

    def body(x_ref, g_ref, yabc_ref, o1, l1, o2, l2, o3, l3, w_ref,
             xn_ref, y_ref, o_ref, lse1_ref, lse4_ref, lse16_ref, so2, sl2, so3, sl3, slse):
        for src, dst, dil in ((o2, so2, ATTN_DILATIONS[1]), (l2, sl2, ATTN_DILATIONS[1]),
                              (o3, so3, ATTN_DILATIONS[2]), (l3, sl3, ATTN_DILATIONS[2])):
            _interleave(src, dst, dil)
        la, lb, lc = l1[...], _get(sl2), _get(sl3)
        mx = jnp.maximum(jnp.maximum(la, lb), lc)
        ea, eb, ec = jnp.exp(la - mx), jnp.exp(lb - mx), jnp.exp(lc - mx)
        den = ea + eb + ec
        o = (ea * o1[...] + eb * _get(so2) + ec * _get(so3)) / den
        o_ref[...] = o
        _put(slse, mx + jnp.log(den))
        for dil, ref in zip(ATTN_DILATIONS, (lse1_ref, lse4_ref, lse16_ref)):
            _deinterleave(slse, ref, dil)
        y_d = o * _silu_and_grad(g_ref[...])[0]
        y_ref[:, 0:n_abc] = yabc_ref[...].astype(MXU_DTYPE)
        y_ref[:, n_abc:] = y_d.astype(MXU_DTYPE)
        xn_ref[...] = x_ref[...] + jnp.dot(y_ref[...], w_ref[...], preferred_element_type=F32)

    row = lambda w: pl.BlockSpec((tm, w), lambda i: (i, 0))
    dil_specs = [_dilated_spec(tm, GROUP_W, dil) for dil in ATTN_DILATIONS]
    (o1, l1), (o2, l2), (o3, l3) = attn
    return pl.pallas_call(
        body, name=name, grid=(S // tm,),
        in_specs=[row(D), row(GROUP_W), row(n_abc)] + [sp for sp in dil_specs for _ in range(2)]
                 + [pl.BlockSpec(w_out.shape, lambda i: (0, 0))],
        out_specs=[row(D), row(4 * GROUP_W), row(GROUP_W)] + dil_specs,
        out_shape=[jax.ShapeDtypeStruct((S, D), F32), jax.ShapeDtypeStruct((S, 4 * GROUP_W), MXU_DTYPE),
                   jax.ShapeDtypeStruct((S, GROUP_W), F32)]
                  + [_dilated_shape(S, GROUP_W, dil, F32) for dil in ATTN_DILATIONS],
        scratch_shapes=[_lane_scratch(tm, GROUP_W)] * 5,
        compiler_params=_params(("parallel",)),
    )(x, z_g, y_abc, o1, l1, o2, l2, o3, l3, w_out)


def _loss_head(x, g, target, name):
    S, D = x.shape
    tm = TM_MM

    def body(x_ref, g_ref, t_ref, dx_ref, loss_ref, dg_ref):
        i = pl.program_id(0)

        @pl.when(i == 0)
        def _():
            loss_ref[...] = jnp.zeros_like(loss_ref)
            dg_ref[...] = jnp.zeros_like(dg_ref)

        xv = x_ref[...]
        r = lax.rsqrt(jnp.mean(xv * xv, axis=-1, keepdims=True) + NORM_EPS)
        xn = xv * r
        err = xn * g_ref[...] - t_ref[...]
        per_tok = jnp.mean(err * err, axis=-1, keepdims=True)
        loss_ref[...] += 0.5 * jnp.sum(per_tok, axis=0, keepdims=True)
        dout = err * (1.0 / D)
        dg_ref[...] += _colsum(dout * xn)
        dxn = dout * g_ref[...]
        dx_ref[...] = r * (dxn - xn * jnp.mean(dxn * xn, axis=-1, keepdims=True))

    row = pl.BlockSpec((tm, D), lambda i: (i, 0))
    return pl.pallas_call(
        body, name=name, grid=(S // tm,),
        in_specs=[row, pl.BlockSpec((1, D), lambda i: (0, 0)), row],
        out_specs=[row, pl.BlockSpec((1, LANES), lambda i: (0, 0)), pl.BlockSpec((1, D), lambda i: (0, 0))],
        out_shape=[jax.ShapeDtypeStruct((S, D), F32), jax.ShapeDtypeStruct((1, LANES), F32),
                   jax.ShapeDtypeStruct((1, D), F32)],
        compiler_params=_params(("arbitrary",)),
    )(x, g, target)


def _outproj_bwd(dx, y, w_out, name):
    S, D = dx.shape
    E = y.shape[1]
    tm = TM_MM

    def body(dx_ref, y_ref, w_ref, dy_ref, dw_ref, acc_ref):
        i = pl.program_id(0)

        @pl.when(i == 0)
        def _():
            acc_ref[...] = jnp.zeros_like(acc_ref)

        dxb = dx_ref[...].astype(MXU_DTYPE)
        dy_ref[...] = _mm_nt(dxb, w_ref[...])
        acc_ref[...] += _mm_tn(y_ref[...], dxb)

        @pl.when(i == S // tm - 1)
        def _():
            dw_ref[...] = acc_ref[...].astype(dw_ref.dtype)

    return pl.pallas_call(
        body, name=name, grid=(S // tm,),
        in_specs=[pl.BlockSpec((tm, D), lambda i: (i, 0)), pl.BlockSpec((tm, E), lambda i: (i, 0)),
                  pl.BlockSpec((E, D), lambda i: (0, 0))],
        out_specs=[pl.BlockSpec((tm, E), lambda i: (i, 0)), pl.BlockSpec((E, D), lambda i: (0, 0))],
        out_shape=[jax.ShapeDtypeStruct((S, E), F32), jax.ShapeDtypeStruct((E, D), WIRE_DTYPE)],
        scratch_shapes=[pltpu.VMEM((E, D), F32)],
        compiler_params=_params(("arbitrary",)),
    )(dx, y, w_out)


def _mix_bwd(z, z_g, dy, hs, o, mp, name):
    S = z.shape[0]
    tm = TM_MIX
    hb = tm // SUBLANES
    nT = S // tm
    last_blk = S // SUBLANES - 1
    wcols = N_ABC * GROUP_W

    def body(z_ref, zh_ref, zn_ref, zg_ref, dy_ref, dyn_ref, h_ref, hh_ref, o_ref,
             wA_ref, wR_ref, vec_ref, wa_ref, wx_ref, ws_ref, bs_ref,
             dz_ref, dzg_ref, do1_ref, do4_ref, do16_ref, dl1_ref, dl4_ref, dl16_ref,
             dwA_ref, dwR_ref, dvec_ref, dwa_ref, dwx_ref, dws_ref, dbs_ref,
             hcarry_ref, xcarry_ref, bsacc_ref, do_ref, dl_ref):
        i = pl.program_id(0)
        ti = nT - 1 - i

        @pl.when(i == 0)
        def _():
            hcarry_ref[...] = jnp.zeros_like(hcarry_ref)
            xcarry_ref[...] = jnp.zeros_like(xcarry_ref)
            bsacc_ref[...] = jnp.zeros_like(bsacc_ref)
            dwA_ref[...] = jnp.zeros_like(dwA_ref)
            dwR_ref[...] = jnp.zeros_like(dwR_ref)
            dvec_ref[...] = jnp.zeros_like(dvec_ref)
            dwa_ref[...] = jnp.zeros_like(dwa_ref)
            dwx_ref[...] = jnp.zeros_like(dwx_ref)
            dws_ref[...] = jnp.zeros_like(dws_ref)
            dbs_ref[...] = jnp.zeros_like(dbs_ref)

        has_prev = ti > 0
        has_next = i > 0
        col = lambda c: slice(c * GROUP_W, (c + 1) * GROUP_W)
        z_of = lambda c: z_ref[:, col(c)]
        halo_of = lambda c: jnp.where(has_prev, zh_ref[:, col(c)], 0.0)
        next_of = lambda c: zn_ref[:, col(c)]

        p, p_h, cv = _conv_a(z_of, halo_of, wA_ref)
        sg, dsg = _silu_and_grad(z_of(3))
        a_b = z_of(1)
        dya = dy_ref[:, col(0)]
        dcv = dya * a_b * sg
        dcv_n = jnp.where(has_next, dyn_ref[...] * next_of(1) * _silu_and_grad(next_of(3))[0], 0.0)
        dp = (wA_ref[2:3, :] * dcv + wA_ref[1:2, :] * _shift_up(dcv, dcv_n, 1)
              + wA_ref[0:1, :] * _shift_up(dcv, dcv_n, 2))
        dwA_ref[2:3, :] += _colsum(dcv * p)
        dwA_ref[1:2, :] += _colsum(dcv * _shift_down(p, p_h, 1))
        dwA_ref[0:1, :] += _colsum(dcv * _shift_down(p, p_h, 2))
        def put_dz(c, val):
            dz_ref[:, col(c)] = val.astype(dz_ref.dtype)

        put_dz(0, dp * z_of(2))
        put_dz(1, dya * cv * sg)
        put_dz(2, dp * z_of(0))
        put_dz(3, dya * a_b * cv * dsg)

        xc, sh, ga, gi, a, mult, sp = _lru_gates(z_of, halo_of, wR_ref, vec_ref, wa_ref, wx_ref)
        h = h_ref[...]
        h_prev = _shift_down(h, jnp.where(has_prev, hh_ref[...], 0.0), 1)
        sgr, dsgr = _silu_and_grad(z_of(5))
        dyb = dy_ref[:, col(1)]
        put_dz(5, dyb * h * dsgr)
        row = lax.broadcasted_iota(jnp.int32, (tm, GROUP_W), 0)
        g_in = dyb * sgr + jnp.where(row == tm - 1, hcarry_ref[0:1, :], 0.0)
        a_up = _shift_up(a, jnp.zeros((SUBLANES, GROUP_W), F32), 1)
        dH = _scan_rev(a_up, g_in)
        hcarry_ref[...] = (a * dH)[0:SUBLANES]
        da = dH * h_prev
        gx = gi * xc
        dmult = dH * gx
        dgi = dH * mult * xc
        dxc = dH * mult * gi
        dlog_a = da * a - dmult * (a * a) / mult
        dga = dlog_a * (-RG_C * sp)
        dlam_row = _colsum(dlog_a * (-RG_C * ga)) * (-_sigmoid(-vec_ref[3:4, :]))
        dpre_a = dga * ga * (1.0 - ga)
        dpre_i = dgi * gi * (1.0 - gi)
        dwa_ref[...] += _mm_tn(xc, dpre_a)
        dwx_ref[...] += _mm_tn(xc, dpre_i)
        dxc = dxc + _mm_nt(dpre_a, wa_ref[...]) + _mm_nt(dpre_i, wx_ref[...])
        dvec_ref[0:1, :] += _colsum(dxc)
        dvec_ref[1:2, :] += _colsum(dpre_a)
        dvec_ref[2:3, :] += _colsum(dpre_i)
        dvec_ref[3:4, :] += dlam_row
        for k in range(4):
            dwR_ref[k:k + 1, :] += _colsum(dxc * sh[3 - k])
        dxc_n = xcarry_ref[...]
        put_dz(4, wR_ref[3:4, :] * dxc + wR_ref[2:3, :] * _shift_up(dxc, dxc_n, 1)
               + wR_ref[1:2, :] * _shift_up(dxc, dxc_n, 2) + wR_ref[0:1, :] * _shift_up(dxc, dxc_n, 3))
        xcarry_ref[...] = dxc[0:SUBLANES]

        c_u, c_v = z_of(6), z_of(7)
        u, du_dx = _gelu_and_grad(c_u)
        gv, dgv_dx = _gelu_and_grad(c_v)
        rr = lax.rsqrt(jnp.mean(gv * gv, axis=-1, keepdims=True) + NORM_EPS)
        xhat = gv * rr
        g_c = vec_ref[4:5, :]
        vn = xhat * g_c
        masks = _head_masks((GMLP_CHUNK, GROUP_W))
        tri_r = lax.broadcasted_iota(jnp.int32, (GMLP_CHUNK, GMLP_CHUNK), 0)
        tri_c = lax.broadcasted_iota(jnp.int32, (GMLP_CHUNK, GMLP_CHUNK), 1)
        tril = tri_r >= tri_c
        sgc, dsgc = _silu_and_grad(z_of(8))
        dyc = dy_ref[:, col(2)]
        dsp_full = dyc * u * sgc
        sp_parts, dvn_parts = [], []
        for c in range(tm // GMLP_CHUNK):
            rs = slice(c * GMLP_CHUNK, (c + 1) * GMLP_CHUNK)
            vc = vn[rs].astype(MXU_DTYPE)
            dsp_c = dsp_full[rs]
            bsacc_ref[...] += dsp_c
            acc = bs_ref[...]
            dvn_c = jnp.zeros((GMLP_CHUNK, GROUP_W), F32)
            for h in range(N_HEADS):
                w_h = ws_ref[h]
                acc = acc + jnp.where(masks[h], jnp.dot(w_h, vc, preferred_element_type=F32), 0.0)
                dsp_h = jnp.where(masks[h], dsp_c, 0.0).astype(MXU_DTYPE)
                dvn_c = dvn_c + _mm_tn(w_h, dsp_h)
                dws_ref[h] += jnp.where(tril, _mm_nt(dsp_h, vc), 0.0)
            sp_parts.append(acc)
            dvn_parts.append(dvn_c)
        spv = jnp.concatenate(sp_parts, axis=0)
        dvn = jnp.concatenate(dvn_parts, axis=0)
        put_dz(6, dyc * spv * sgc * du_dx)
        put_dz(8, dyc * u * spv * dsgc)
        dvec_ref[4:5, :] += _colsum(dvn * xhat)
        dgvn = dvn * g_c
        dgv = rr * (dgvn - xhat * jnp.mean(dgvn * xhat, axis=-1, keepdims=True))
        put_dz(7, dgv * dgv_dx)

        sgd, dsgd = _silu_and_grad(zg_ref[...])
        dyd = dy_ref[:, col(3)]
        ov = o_ref[...]
        do = dyd * sgd
        _put(do_ref, do)
        dzg_ref[...] = (dyd * ov * dsgd).astype(dzg_ref.dtype)
        prod = do * ov
        tmasks = _head_masks((tm, GROUP_W))
        dl = jnp.zeros((tm, GROUP_W), F32)
        for h in range(N_HEADS):
            dl = jnp.where(tmasks[h], jnp.sum(jnp.where(tmasks[h], prod, 0.0), axis=-1, keepdims=True), dl)
        _put(dl_ref, dl)
        for dil, d_out, l_out in zip(ATTN_DILATIONS, (do1_ref, do4_ref, do16_ref), (dl1_ref, dl4_ref, dl16_ref)):
            _deinterleave(do_ref, d_out, dil)
            _deinterleave(dl_ref, l_out, dil)

        @pl.when(i == nT - 1)
        def _():
            acc = bsacc_ref[...]
            lane = lax.broadcasted_iota(jnp.int32, (GMLP_CHUNK, LANES), 1)
            out = jnp.zeros((GMLP_CHUNK, LANES), F32)
            for h in range(N_HEADS):
                out = jnp.where(lane == h, jnp.sum(jnp.where(masks[h], acc, 0.0), axis=-1, keepdims=True), out)
            dbs_ref[...] = out

    rev = lambda w: pl.BlockSpec((tm, w), lambda i: (nT - 1 - i, 0))
    prev8 = lambda w: pl.BlockSpec((SUBLANES, w), lambda i: (jnp.maximum((nT - 1 - i) * hb - 1, 0), 0))
    next8 = lambda w: pl.BlockSpec((SUBLANES, w), lambda i: (jnp.minimum((nT - i) * hb, last_blk), 0))
    const2 = lambda shape: pl.BlockSpec(shape, lambda i: (0, 0))
    dil_specs = [_dilated_spec(tm, GROUP_W, dil, lambda i: nT - 1 - i) for dil in ATTN_DILATIONS]
    dil_shapes = [_dilated_shape(S, GROUP_W, dil, F32) for dil in ATTN_DILATIONS]
    small = (SUBLANES, GROUP_W)
    sq = (GROUP_W, GROUP_W)
    ws_shape = (N_HEADS, GMLP_CHUNK, GMLP_CHUNK)
    return pl.pallas_call(
        body, name=name, grid=(nT,),
        in_specs=[rev(wcols), prev8(wcols), next8(wcols), rev(GROUP_W),
                  rev(4 * GROUP_W), next8(GROUP_W), rev(GROUP_W), prev8(GROUP_W), rev(GROUP_W)]
                 + _mix_specs(tm, S, "bwd"),
        out_specs=[rev(wcols), rev(GROUP_W)] + dil_specs + dil_specs
                  + [const2(small), const2(small), const2(small), const2(sq), const2(sq),
                     pl.BlockSpec(ws_shape, lambda i: (0, 0, 0)), const2((GMLP_CHUNK, LANES))],
        out_shape=[jax.ShapeDtypeStruct((S, wcols), MXU_DTYPE), jax.ShapeDtypeStruct((S, GROUP_W), MXU_DTYPE)]
                  + dil_shapes + dil_shapes
                  + [jax.ShapeDtypeStruct(small, F32)] * 3 + [jax.ShapeDtypeStruct(sq, F32)] * 2
                  + [jax.ShapeDtypeStruct(ws_shape, F32), jax.ShapeDtypeStruct((GMLP_CHUNK, LANES), F32)],
        scratch_shapes=[pltpu.VMEM(small, F32), pltpu.VMEM(small, F32), pltpu.VMEM((GMLP_CHUNK, GROUP_W), F32),
                        _lane_scratch(tm, GROUP_W), _lane_scratch(tm, GROUP_W)],
        compiler_params=_params(("arbitrary",)),
    )(z, z, z, z_g, dy, dy, hs, hs, o, mp["wA"], mp["wR"], mp["vec"], mp["wa"], mp["wx"], mp["ws"], mp["bs"])


def _attn_bwd(qkv, do, lse, delta, dil, name):
    rows = qkv.shape[0]
    nb = rows // ATTN_BLOCK
    scale = 1.0 / math.sqrt(HEAD_DIM)
    B = ATTN_BLOCK
    per_step = ATTN_BLOCKS_PER_STEP
    n_steps = nb // per_step

    def body(qc_ref, qn_ref, k_ref, v_ref, doc_ref, don_ref, lc_ref, ln_ref, dc_ref, dn_ref,
             dq_ref, dk_ref, dv_ref, carry_ref, bias_ref):
        n = pl.program_id(1)

        @pl.when(n == 0)
        def _():
            carry_ref[...] = jnp.zeros_like(carry_ref)
            bias_ref[...] = _attn_bias(dil, (0, B), B)

        masks = _head_masks((B, GROUP_W))

        def per_row(tiles):
            return jnp.concatenate([jnp.max(jnp.where(masks[h], t, _NEG), axis=-1, keepdims=True)
                                    for t in tiles for h in range(N_HEADS)], axis=0)

        dq_acc = carry_ref[...]
        for j in range(per_step):
            own = slice(j * B, (j + 1) * B)
            after = slice((j + 1) * B, (j + 2) * B)
            last = j == per_step - 1
            nxt = lambda cur_ref, nxt_ref: nxt_ref[...] if last else cur_ref[after]
            kb = k_ref[own]
            vb = v_ref[own]
            qs = jnp.concatenate([_stack_heads(qc_ref[own], masks), _stack_heads(nxt(qc_ref, qn_ref), masks)], axis=0)
            dos = jnp.concatenate([_stack_heads(doc_ref[own].astype(MXU_DTYPE), masks),
                                   _stack_heads(nxt(doc_ref, don_ref).astype(MXU_DTYPE), masks)], axis=0)
            lse_rows = per_row([lc_ref[own], nxt(lc_ref, ln_ref)])
            dl_rows = per_row([dc_ref[own], nxt(dc_ref, dn_ref)])
            s = _mm_nt(qs, kb) * scale + bias_ref[...]
            if last:
                row = lax.broadcasted_iota(jnp.int32, s.shape, 0)
                s = jnp.where((n == n_steps - 1) & (row >= N_HEADS * B), _NEG, s)
            p = jnp.exp(s - lse_rows)
            dp = _mm_nt(dos, vb)
            ds = (p * (dp - dl_rows) * scale).astype(MXU_DTYPE)
            dv_ref[own] = _mm_tn(p.astype(MXU_DTYPE), dos)
            dk_ref[own] = _mm_tn(ds, qs)
            dq4 = jnp.dot(ds, kb, preferred_element_type=F32)
            dq_ref[own] = dq_acc + _unstack_heads(dq4, masks)
            dq_acc = _unstack_heads(dq4, masks, N_HEADS)
        carry_ref[...] = dq_acc

    blk = (per_step * B, GROUP_W)
    one = (B, GROUP_W)
    nxt_idx = lambda n: jnp.minimum((n + 1) * per_step, nb - 1)
    zcur = lambda c: pl.BlockSpec(blk, lambda r, n: (n, r * 3 + c))
    znext = lambda c: pl.BlockSpec(one, lambda r, n: (nxt_idx(n), r * 3 + c))
    cur = pl.BlockSpec(blk, lambda r, n: (n, r))
    nxt = pl.BlockSpec(one, lambda r, n: (nxt_idx(n), r))
    return pl.pallas_call(
        body, name=name, grid=(dil, n_steps),
        in_specs=[zcur(0), znext(0), zcur(1), zcur(2), cur, nxt, cur, nxt, cur, nxt],
        out_specs=[cur, cur, cur],
        out_shape=[jax.ShapeDtypeStruct((rows, dil * GROUP_W), F32)] * 3,
        scratch_shapes=[pltpu.VMEM(one, F32), pltpu.VMEM((2 * N_HEADS * B, B), F32)],
        compiler_params=_params(("parallel", "arbitrary")),
    )(qkv, qkv, qkv, qkv, do, do, lse, lse, delta, delta)


def _inproj_bwd(x, g, dxn, dz_abc, dqkv, dz_g, w_in, name):
    S, D = x.shape
    N = w_in.shape[1]
    tm = TM_MM
    n_abc = N_ABC * GROUP_W

    def body(x_ref, g_ref, dxn_ref, dabc_ref, q1, k1, v1, q2, k2, v2, q3, k3, v3, dg_ref, w_ref,
             dx_ref, dz_ref, h_ref, dgn_ref, s4_ref, s16_ref):
        i = pl.program_id(0)

        @pl.when(i == 0)
        def _():
            dgn_ref[...] = jnp.zeros_like(dgn_ref)

        dz_ref[:, 0:n_abc] = dabc_ref[...].astype(MXU_DTYPE)
        for j, parts in enumerate(((q1, q2, q3), (k1, k2, k3), (v1, v2, v3))):
            c0 = n_abc + j * GROUP_W
            _interleave(parts[1], s4_ref, ATTN_DILATIONS[1])
            _interleave(parts[2], s16_ref, ATTN_DILATIONS[2])
            dz_ref[:, c0:c0 + GROUP_W] = (parts[0][...] + _get(s4_ref) + _get(s16_ref)).astype(MXU_DTYPE)
        dz_ref[:, n_abc + 3 * GROUP_W:] = dg_ref[...].astype(MXU_DTYPE)
        dh = _mm_nt(dz_ref[...], w_ref[...])
        xv = x_ref[...]
        r = lax.rsqrt(jnp.mean(xv * xv, axis=-1, keepdims=True) + NORM_EPS)
        xn = xv * r
        gv = g_ref[...]
        h_ref[...] = (xn * gv).astype(MXU_DTYPE)
        dgn_ref[...] += _colsum(dh * xn)
        dn = dh * gv
        dx_ref[...] = dxn_ref[...] + r * (dn - xn * jnp.mean(dn * xn, axis=-1, keepdims=True))

    row = lambda w: pl.BlockSpec((tm, w), lambda i: (i, 0))
    flat = [t for p in dqkv for t in p]
    dil_specs = [_dilated_spec(tm, GROUP_W, dil) for dil in ATTN_DILATIONS for _ in range(3)]
    return pl.pallas_call(
        body, name=name, grid=(S // tm,),
        in_specs=[row(D), pl.BlockSpec((1, D), lambda i: (0, 0)), row(D), row(n_abc)] + dil_specs
                 + [row(GROUP_W), pl.BlockSpec((D, N), lambda i: (0, 0))],
        out_specs=[row(D), row(N), row(D), pl.BlockSpec((1, D), lambda i: (0, 0))],
        out_shape=[jax.ShapeDtypeStruct((S, D), F32), jax.ShapeDtypeStruct((S, N), MXU_DTYPE),
                   jax.ShapeDtypeStruct((S, D), MXU_DTYPE), jax.ShapeDtypeStruct((1, D), F32)],
        scratch_shapes=[_lane_scratch(tm, GROUP_W)] * 2,
        compiler_params=_params(("arbitrary",)),
    )(x, g, dxn, dz_abc, *flat, dz_g, w_in)


def _inproj_wgrad(h, dz, name):
    S, D = h.shape
    N = dz.shape[1]
    tm = TM_MM
    nj = 2
    cw = N // nj
    per = N_DEV // nj
    n_loc = N // N_DEV

    def body(h_ref, dz_ref, dw_ref, acc_ref):
        i = pl.program_id(1)

        @pl.when(i == 0)
        def _():
            acc_ref[...] = jnp.zeros_like(acc_ref)

        acc_ref[...] += _mm_tn(h_ref[...], dz_ref[...])

        @pl.when(i == S // tm - 1)
        def _():
            for b in range(per):
                dw_ref[b] = acc_ref[:, b * n_loc:(b + 1) * n_loc].astype(dw_ref.dtype)

    return pl.pallas_call(
        body, name=name, grid=(nj, S // tm),
        in_specs=[pl.BlockSpec((tm, D), lambda j, i: (i, 0)), pl.BlockSpec((tm, cw), lambda j, i: (i, j))],
        out_specs=pl.BlockSpec((per, D, n_loc), lambda j, i: (j, 0, 0)),
        out_shape=jax.ShapeDtypeStruct((N_DEV, D, n_loc), WIRE_DTYPE),
        scratch_shapes=[pltpu.VMEM((D, cw), F32)],
        compiler_params=_params(("parallel", "arbitrary")),
    )(h, dz)


def _my_place():
    return lax.axis_index("x"), lax.axis_index("y"), lax.axis_index("c")


def _peer(x, y, c, k):
    px = 1 - x if k & 4 else x
    py = 1 - y if k & 2 else y
    pc = 1 - c if k & 1 else c
    return (px, py, pc), 4 * px + 2 * py + pc


HBM_SPEC = pl.BlockSpec(memory_space=pltpu.HBM)
SEM_SPEC = pl.BlockSpec(memory_space=pltpu.SEMAPHORE)
SPLIT_EFFECT = pltpu.SideEffectType.DATAFLOW_SIDE_EFFECTING
N_PEERS = N_DEV - 1


def _exchange_copies(srcs, lands, send_sems, recv_sems, gather, arrival):
    x, y, c = _my_place()
    me = 4 * x + 2 * y + c
    copies = []
    for t in range(len(srcs)):
        for k in range(1, N_DEV):
            peer, pidx = _peer(x, y, c, k)
            copies.append(pltpu.make_async_remote_copy(
                src_ref=srcs[t] if gather else srcs[t].at[pidx],
                dst_ref=lands[t].at[pidx if arrival else me], send_sem=send_sems.at[t * N_PEERS + k - 1],
                recv_sem=recv_sems.at[t * N_PEERS + k - 1], device_id=peer, device_id_type=MESH))
    return copies


def _exchange_start(groups, gather, name):
    sizes = [len(g) for g in groups]
    srcs = [pltpu.with_memory_space_constraint(a, pltpu.HBM) for g in groups for a in g]
    land_shape = lambda a: ((N_DEV,) + a.shape) if gather else a.shape
    lands = [pltpu.with_memory_space_constraint(lax.empty(land_shape(a), a.dtype), pltpu.HBM) for a in srcs]
    n = len(srcs)
    n_g = len(groups)

    def body(*refs):
        src_refs, land_refs = refs[:n], refs[n:2 * n]
        sem_refs = refs[4 * n:4 * n + 2 * n_g]
        token = refs[-1]
        off = 0
        for gi, sz in enumerate(sizes):
            for send in _exchange_copies(src_refs[off:off + sz], land_refs[off:off + sz],
                                         sem_refs[2 * gi], sem_refs[2 * gi + 1], gather, False):
                send.start()
            off += sz
        token[...] = jnp.zeros_like(token)

    sem_shapes = [pltpu.SemaphoreType.DMA((sz * N_PEERS,)) for sz in sizes for _ in range(2)]
    outs = pl.pallas_call(
        body, name=name,
        in_specs=[HBM_SPEC] * (2 * n),
        out_specs=[HBM_SPEC] * (2 * n) + [SEM_SPEC] * (2 * n_g) + [pl.BlockSpec(memory_space=pltpu.VMEM)],
        out_shape=[pltpu.HBM(a.shape, a.dtype) for a in srcs + lands] + sem_shapes
                  + [jax.ShapeDtypeStruct((SUBLANES, LANES), F32)],
        input_output_aliases={i: i for i in range(2 * n)},
        compiler_params=pltpu.CompilerParams(has_side_effects=SPLIT_EFFECT),
    )(*srcs, *lands)
    handles, off = [], 0
    for gi, sz in enumerate(sizes):
        handles.append((outs[2 * n + 2 * gi], outs[2 * n + 2 * gi + 1], outs[off:off + sz], outs[n + off:n + off + sz]))
        off += sz
    return handles, outs[-1]


def _exchange_wait(handle, after, gather, name):
    send_sems, recv_sems, srcs, lands = handle
    n = len(srcs)

    def body(*refs):
        src_refs, land_refs = refs[:n], refs[n:2 * n]
        for send in _exchange_copies(src_refs, land_refs, refs[2 * n], refs[2 * n + 1], gather, False):
            send.wait_send()
        for arrival in _exchange_copies(src_refs, land_refs, refs[2 * n], refs[2 * n + 1], gather, True):
            arrival.wait_recv()

    outs = pl.pallas_call(
        body, name=name,
        in_specs=[HBM_SPEC] * (2 * n) + [SEM_SPEC, SEM_SPEC, pl.BlockSpec(memory_space=pl.ANY)],
        out_specs=[HBM_SPEC] * (2 * n),
        out_shape=[pltpu.HBM(a.shape, a.dtype) for a in list(srcs) + list(lands)],
        input_output_aliases={i: i for i in range(2 * n)},
        compiler_params=pltpu.CompilerParams(has_side_effects=SPLIT_EFFECT),
    )(*srcs, *lands, send_sems, recv_sems, after)
    return outs[:n], outs[n:]


def _allreduce(buf, name):
    R = buf.shape[0]

    def body(x_ref, out_ref, recv_ref, send_sems, recv_sems):
        x, y, c = _my_place()
        out_ref[...] = x_ref[...]
        for s, k in enumerate((1, 4, 2)):
            peer, _ = _peer(x, y, c, k)
            cp = pltpu.make_async_remote_copy(
                src_ref=out_ref, dst_ref=recv_ref.at[s], send_sem=send_sems.at[s], recv_sem=recv_sems.at[s],
                device_id=peer, device_id_type=MESH)
            cp.start()
            cp.wait()
            out_ref[...] = out_ref[...] + recv_ref[s]

    vm = pl.BlockSpec(memory_space=pltpu.VMEM)
    return pl.pallas_call(
        body, name=name, in_specs=[vm], out_specs=vm,
        out_shape=jax.ShapeDtypeStruct((R, LANES), F32),
        scratch_shapes=[pltpu.VMEM((3, R, LANES), F32), pltpu.SemaphoreType.DMA((3,)), pltpu.SemaphoreType.DMA((3,))],
        compiler_params=pltpu.CompilerParams(has_side_effects=True, vmem_limit_bytes=VMEM_LIMIT),
    )(buf)


def _adamw_math(w, g, m, v):
    m = ADAM_B1 * m + (1.0 - ADAM_B1) * g
    v = ADAM_B2 * v + (1.0 - ADAM_B2) * (g * g)
    m_hat = m / (1.0 - ADAM_B1 ** ADAM_STEP)
    v_hat = v / (1.0 - ADAM_B2 ** ADAM_STEP)
    delta = -ADAM_LR * (m_hat / (jnp.sqrt(v_hat) + ADAM_EPS) + ADAM_WD * w)
    return delta, m, v


def _adamw_summed(parts, w, m, v, tr, name):
    R, C = w.shape

    def body(p_ref, w_ref, m_ref, v_ref, g_ref, d_ref, nm_ref, nv_ref):
        g = p_ref[0].astype(F32)
        for j in range(1, N_DEV):
            g = g + p_ref[j].astype(F32)
        g_ref[...] = g
        d_ref[...], nm_ref[...], nv_ref[...] = _adamw_math(w_ref[...], g, m_ref[...], v_ref[...])

    row = pl.BlockSpec((tr, C), lambda i: (i, 0))
    return pl.pallas_call(
        body, name=name, grid=(R // tr,),
        in_specs=[pl.BlockSpec((N_DEV, tr, C), lambda i: (0, i, 0)), row, row, row],
        out_specs=[row] * 4, out_shape=[jax.ShapeDtypeStruct((R, C), F32)] * 4,
        compiler_params=_params(("parallel",)),
    )(parts, w, m, v)


def _adamw_small(w, g, m, v, name):
    def body(w_ref, g_ref, m_ref, v_ref, d_ref, nm_ref, nv_ref):
        d_ref[...], nm_ref[...], nv_ref[...] = _adamw_math(w_ref[...], g_ref[...], m_ref[...], v_ref[...])

    vm = pl.BlockSpec(memory_space=pltpu.VMEM)
    return pl.pallas_call(
        body, name=name, in_specs=[vm] * 4, out_specs=[vm] * 3,
        out_shape=[jax.ShapeDtypeStruct(w.shape, F32)] * 3,
        compiler_params=pltpu.CompilerParams(vmem_limit_bytes=VMEM_LIMIT),
    )(w, g, m, v)


def _pack(arrays):
    flat = jnp.concatenate([a.reshape(-1) for a in arrays])
    pad = (-flat.shape[0]) % (SUBLANES * LANES)
    return jnp.pad(flat, (0, pad)).reshape(-1, LANES)


def _unpack(buf, like):
    flat = buf.reshape(-1)
    out, off = [], 0
    for a in like:
        out.append(flat[off:off + a.size].reshape(a.shape))
        off += a.size
    return out


def _block_diag(w):
    eye = jnp.eye(N_HEADS, dtype=w.dtype)
    return jnp.einsum('hij,hk->hikj', w, eye).reshape(GROUP_W, GROUP_W)


def _diag_blocks(w):
    return jnp.einsum('hihj->hij', w.reshape(N_HEADS, HEAD_DIM, N_HEADS, HEAD_DIM))


def _pad_rows(a):
    return jnp.pad(a, ((0, SUBLANES - a.shape[0]), (0, 0)))


def _mixer_params(l, conv_a_w, conv_r_w, conv_r_b, lru_wa, lru_ba, lru_wx, lru_bx, lru_lambda, gmlp_norm_g,
                  gmlp_ws, gmlp_bs):
    tril = jnp.tril(jnp.ones((GMLP_CHUNK, GMLP_CHUNK), dtype=bool))
    vec = jnp.stack([conv_r_b[l], lru_ba[l], lru_bx[l], lru_lambda[l], gmlp_norm_g[l]])
    return {
        "wA": _pad_rows(conv_a_w[l]), "wR": _pad_rows(conv_r_w[l]), "vec": _pad_rows(vec),
        "wa": _block_diag(lru_wa[l]).astype(MXU_DTYPE), "wx": _block_diag(lru_wx[l]).astype(MXU_DTYPE),
        "ws": jnp.where(tril[None], gmlp_ws[l], 0.0).astype(MXU_DTYPE),
        "bs": jnp.repeat(jnp.transpose(gmlp_bs[l]), HEAD_DIM, axis=1),
    }


def _local_step(x, loss_target, norm_g, get_weights, emit_wgrads, conv_a_w, conv_r_w, conv_r_b, lru_wa, lru_ba,
                lru_wx, lru_bx, lru_lambda, gmlp_norm_g, gmlp_ws, gmlp_bs, final_g):
    depth = norm_g.shape[0]
    D = x.shape[1]
    small = (conv_a_w, conv_r_w, conv_r_b, lru_wa, lru_ba, lru_wx, lru_bx, lru_lambda, gmlp_norm_g, gmlp_ws, gmlp_bs)
    saved = []
    for l in range(depth):
        mp = _mixer_params(l, *small)
        w_in_l, w_out_l = get_weights(l, x)
        z, z_g, *qkv = _norm_inproj(x, norm_g[l].reshape(1, D), w_in_l, f"norm_inproj_{l}")
        y_abc, hs = _mix_fwd(z, mp, f"mix_fwd_{l}")
        attn = [_attn_fwd(qkv[p], dil, f"attn_fwd_d{dil}_{l}") for p, dil in enumerate(ATTN_DILATIONS)]
        x_new, y, o, *lse = _outproj(x, z_g, y_abc, attn, w_out_l, f"outproj_{l}")
        saved.append((x, z, z_g, qkv, hs, y, o, lse, mp, w_in_l, w_out_l))
        x = x_new
    dx, loss, d_final_g = _loss_head(x, final_g.reshape(1, D), loss_target, "loss_head")
    grads = []
    token = None
    for l in reversed(range(depth)):
        x_l, z, z_g, qkv, hs, y, o, lse, mp, w_in_l, w_out_l = saved[l]
        if token is not None:
            mp = dict(mp, vec=mp["vec"] + token[0, 0])
        dy, dw_out = _outproj_bwd(dx, y, w_out_l, f"outproj_bwd_{l}")
        (dz_abc, dz_g, do1, do4, do16, dl1, dl4, dl16, dwA, dwR, dvec, dwa, dwx, dws, dbs) = _mix_bwd(
            z, z_g, dy, hs, o, mp, f"mix_bwd_{l}")
        dqkv = [_attn_bwd(qkv[p], do, lse[p], dl, dil, f"attn_bwd_d{dil}_{l}")
                for p, (dil, do, dl) in enumerate(zip(ATTN_DILATIONS, (do1, do4, do16), (dl1, dl4, dl16)))]
        dx, dz, h, dng = _inproj_bwd(x_l, norm_g[l].reshape(1, D), dx, dz_abc, dqkv, dz_g, w_in_l,
                                     f"inproj_bwd_{l}")
        dw_in = _inproj_wgrad(h, dz, f"inproj_wgrad_{l}")
        token = emit_wgrads(l, dw_in, dw_out)
        grads.append({
            "norm_g": dng[0],
            "conv_a_w": dwA[:conv_a_w.shape[1]], "conv_r_w": dwR[:conv_r_w.shape[1]],
            "conv_r_b": dvec[0], "lru_ba": dvec[1], "lru_bx": dvec[2], "lru_lambda": dvec[3], "gmlp_norm_g": dvec[4],
            "lru_wa": _diag_blocks(dwa), "lru_wx": _diag_blocks(dwx), "gmlp_ws": dws,
            "gmlp_bs": jnp.transpose(dbs[:, :N_HEADS]),
        })
    grads = grads[::-1]
    stacked = {k: jnp.stack([g[k] for g in grads]) for k in grads[0]}
    stacked["final_g"] = d_final_g[0]
    return loss[0, 0], dx, stacked


SMALL_NAMES = ("norm_g", "conv_a_w", "conv_r_w", "conv_r_b", "lru_wa", "lru_ba", "lru_wx", "lru_bx", "lru_lambda",
               "gmlp_norm_g", "gmlp_ws", "gmlp_bs", "final_g")
WEIGHT_NAMES = ("norm_g", "w_in", "conv_a_w", "conv_r_w", "conv_r_b", "lru_wa", "lru_ba", "lru_wx", "lru_bx",
                "lru_lambda", "gmlp_norm_g", "gmlp_ws", "gmlp_bs", "w_out", "final_g")


def kernel(x, norm_g, w_in, conv_a_w, conv_r_w, conv_r_b, lru_wa, lru_ba, lru_wx, lru_bx, lru_lambda, gmlp_norm_g, gmlp_ws, gmlp_bs, w_out, final_g, loss_target, m_norm_g, m_w_in, m_conv_a_w, m_conv_r_w, m_conv_r_b, m_lru_wa, m_lru_ba, m_lru_wx, m_lru_bx, m_lru_lambda, m_gmlp_norm_g, m_gmlp_ws, m_gmlp_bs, m_w_out, m_final_g, v_norm_g, v_w_in, v_conv_a_w, v_conv_r_w, v_conv_r_b, v_lru_wa, v_lru_ba, v_lru_wx, v_lru_bx, v_lru_lambda, v_gmlp_norm_g, v_gmlp_ws, v_gmlp_bs, v_w_out, v_final_g):
    w = dict(norm_g=norm_g, w_in=w_in, conv_a_w=conv_a_w, conv_r_w=conv_r_w, conv_r_b=conv_r_b, lru_wa=lru_wa,
             lru_ba=lru_ba, lru_wx=lru_wx, lru_bx=lru_bx, lru_lambda=lru_lambda, gmlp_norm_g=gmlp_norm_g,
             gmlp_ws=gmlp_ws, gmlp_bs=gmlp_bs, w_out=w_out, final_g=final_g)
    m = dict(norm_g=m_norm_g, w_in=m_w_in, conv_a_w=m_conv_a_w, conv_r_w=m_conv_r_w, conv_r_b=m_conv_r_b,
             lru_wa=m_lru_wa, lru_ba=m_lru_ba, lru_wx=m_lru_wx, lru_bx=m_lru_bx, lru_lambda=m_lru_lambda,
             gmlp_norm_g=m_gmlp_norm_g, gmlp_ws=m_gmlp_ws, gmlp_bs=m_gmlp_bs, w_out=m_w_out, final_g=m_final_g)
    v = dict(norm_g=v_norm_g, w_in=v_w_in, conv_a_w=v_conv_a_w, conv_r_w=v_conv_r_w, conv_r_b=v_conv_r_b,
             lru_wa=v_lru_wa, lru_ba=v_lru_ba, lru_wx=v_lru_wx, lru_bx=v_lru_bx, lru_lambda=v_lru_lambda,
             gmlp_norm_g=v_gmlp_norm_g, gmlp_ws=v_gmlp_ws, gmlp_bs=v_gmlp_bs, w_out=v_w_out, final_g=v_final_g)
    depth, D, n_loc = w_in.shape
    e_loc = w_out.shape[1]
    cx, cy, cc = _my_place()
    me = 4 * cx + 2 * cy + cc

    def with_own(srcs, lands, gather):
        own = [s[None] if gather else lax.dynamic_slice_in_dim(s, me, 1, axis=0) for s in srcs]
        return [lax.dynamic_update_slice_in_dim(ld, o, me, axis=0) for ld, o in zip(lands, own)]

    w_in_w, w_out_w = w_in.astype(MXU_DTYPE), w_out.astype(MXU_DTYPE)
    c_loc = conv_a_w.shape[2]
    taps = (conv_a_w, conv_r_w)
    groups = [[w_in_w[l], w_out_w[l]] for l in range(depth)]
    groups[0].append(_pack(taps))
    gathers, _ = _exchange_start(groups, True, "gather_start")

    def gathered(l, after):
        srcs, lands = _exchange_wait(gathers[l], after, True, f"gather_wait_{l}")
        g_in, g_out, *rest = with_own(srcs, lands, True)
        return (jnp.transpose(g_in, (1, 0, 2)).reshape(D, N_DEV * n_loc), g_out.reshape(N_DEV * e_loc, D)), rest

    first_weights, (g_taps,) = gathered(0, x)
    g_taps = g_taps.reshape(N_DEV, -1)
    conv_full, off = [], 0
    for a in taps:
        part = g_taps[:, off:off + a.size].reshape((N_DEV,) + a.shape)
        conv_full.append(jnp.transpose(part, (1, 2, 0, 3)).reshape(a.shape[:2] + (N_DEV * c_loc,)))
        off += a.size
    conv_a_full, conv_r_full = conv_full

    def get_weights(l, after):
        return first_weights if l == 0 else gathered(l, after)[0]

    scatters = {}

    def emit_wgrads(l, dw_in, dw_out):
        handles, token = _exchange_start([[dw_in, dw_out.reshape(N_DEV, e_loc, D)]], False, f"scatter_start_{l}")
        scatters[l] = handles[0]
        return token

    loss, grad_x, g = _local_step(
        x[0], loss_target[0], norm_g, get_weights, emit_wgrads, conv_a_full, conv_r_full, conv_r_b, lru_wa, lru_ba,
        lru_wx, lru_bx, lru_lambda, gmlp_norm_g, gmlp_ws, gmlp_bs, final_g)
    loss = lax.psum(loss, ("x", "y", "c"))

    per_layer = {"w_in": [], "w_out": []}
    for l in reversed(range(depth)):
        srcs, lands = _exchange_wait(scatters[l], grad_x, False, f"scatter_wait_{l}")
        r_in, r_out = with_own(srcs, lands, False)
        per_layer["w_in"].append(_adamw_summed(r_in, w_in[l], m_w_in[l], v_w_in[l], 512, f"adamw_w_in_{l}"))
        per_layer["w_out"].append(_adamw_summed(r_out, w_out[l], m_w_out[l], v_w_out[l], 128, f"adamw_w_out_{l}"))
    big = {k: [jnp.stack(parts) for parts in zip(*res[::-1])] for k, res in per_layer.items()}

    g_small = [g[k] for k in SMALL_NAMES]
    g_small = _unpack(_allreduce(_pack(g_small), "allreduce_small_grads"), g_small)
    g_small = dict(zip(SMALL_NAMES, g_small))
    for k in ("conv_a_w", "conv_r_w"):
        g_small[k] = lax.dynamic_slice_in_dim(g_small[k], me * c_loc, c_loc, axis=2)
    packs = [_pack([d[k] for k in SMALL_NAMES]) for d in (w, g_small, m, v)]
    res = _adamw_small(*packs, "adamw_small")
    like = [w[k] for k in SMALL_NAMES]
    d_s, m_s, v_s = (dict(zip(SMALL_NAMES, _unpack(r, like))) for r in res)

    grad, delta, new_m, new_v = {}, {}, {}, {}
    for k in WEIGHT_NAMES:
        if k in big:
            grad[k], delta[k], new_m[k], new_v[k] = big[k]
        else:
            grad[k], delta[k], new_m[k], new_v[k] = g_small[k], d_s[k], m_s[k], v_s[k]
    return (loss, grad_x[None], *[grad[k] for k in WEIGHT_NAMES], *[delta[k] for k in WEIGHT_NAMES],
            *[new_m[k] for k in WEIGHT_NAMES], *[new_v[k] for k in WEIGHT_NAMES])
```

```python
import functools
import math

import jax
import jax.numpy as jnp
from jax import lax
from jax.experimental import pallas as pl
from jax.experimental.pallas import tpu as pltpu

F32 = jnp.float32
MXU_DTYPE = jnp.bfloat16
WIRE_DTYPE = jnp.bfloat16
MESH = pl.DeviceIdType.MESH

N_DEV = 8
GROUP_W = 256
N_HEADS = 4
HEAD_DIM = 64
N_CHUNKS = 13
N_ABC = 9
GMLP_CHUNK = 128
ATTN_BLOCK = 128
ATTN_BLOCKS_PER_STEP = 4
ATTN_DILATIONS = (1, 4, 16)
NORM_EPS = 1e-6
RG_C = 8.0
SUBLANES = 8
LANES = 128
VMEM_LIMIT = 56 * 1024 * 1024

ADAM_LR = 0.001
ADAM_B1 = 0.9
ADAM_B2 = 0.999
ADAM_EPS = 1e-08
ADAM_WD = 0.01
ADAM_STEP = 10

TM_MIX = 256
TM_MM = 512


def _params(sem, vmem=VMEM_LIMIT):
    return pltpu.CompilerParams(dimension_semantics=sem, vmem_limit_bytes=vmem)


def _mm(a, b):
    return jnp.dot(a.astype(MXU_DTYPE), b.astype(MXU_DTYPE), preferred_element_type=F32)


def _mm_tn(a, b):
    return lax.dot_general(a.astype(MXU_DTYPE), b.astype(MXU_DTYPE), (((0,), (0,)), ((), ())),
                           preferred_element_type=F32)


def _mm_nt(a, b):
    return lax.dot_general(a.astype(MXU_DTYPE), b.astype(MXU_DTYPE), (((1,), (1,)), ((), ())),
                           preferred_element_type=F32)


def _sigmoid(x):
    return 0.5 * jnp.tanh(0.5 * x) + 0.5


def _silu_and_grad(x):
    s = _sigmoid(x)
    return x * s, s * (1.0 + x * (1.0 - s))


_GELU_K = math.sqrt(2.0 / math.pi)
_GELU_C = 0.044715


def _gelu_and_grad(x):
    x2 = x * x
    t = jnp.tanh(_GELU_K * (x + _GELU_C * x * x2))
    val = 0.5 * x * (1.0 + t)
    grad = 0.5 * (1.0 + t) + 0.5 * x * (1.0 - t * t) * (_GELU_K * (1.0 + 3.0 * _GELU_C * x2))
    return val, grad


def _gelu(x):
    return 0.5 * x * (1.0 + jnp.tanh(_GELU_K * (x + _GELU_C * x * x * x)))


def _expm1_nonpos(u):
    poly = 1.0 / math.factorial(9)
    for k in range(8, 0, -1):
        poly = poly * u + 1.0 / math.factorial(k)
    return jnp.where(u > -0.25, poly * u, jnp.exp(u) - 1.0)


def _softplus(x):
    return jnp.maximum(x, 0.0) + jnp.log(1.0 + jnp.exp(-jnp.abs(x)))


def _shift_down(t, halo, k):
    rolled = pltpu.roll(t, k, 0)
    hr = pltpu.roll(halo, k, 0)
    row = lax.broadcasted_iota(jnp.int32, halo.shape, 0)
    first = jnp.where(row < k, hr, rolled[0:SUBLANES])
    return jnp.concatenate([first, rolled[SUBLANES:]], axis=0)


def _shift_up(t, nxt, k):
    tm = t.shape[0]
    rolled = pltpu.roll(t, tm - k, 0)
    nr = pltpu.roll(nxt, SUBLANES - k, 0)
    row = lax.broadcasted_iota(jnp.int32, nxt.shape, 0)
    last = jnp.where(row >= SUBLANES - k, nr, rolled[tm - SUBLANES:tm])
    return jnp.concatenate([rolled[:tm - SUBLANES], last], axis=0)


def _scan_fwd(a, b):
    tm = a.shape[0]
    row = lax.broadcasted_iota(jnp.int32, a.shape, 0)
    s = 1
    while s < tm:
        a_s = pltpu.roll(a, s, 0)
        b_s = pltpu.roll(b, s, 0)
        m = row >= s
        b = jnp.where(m, a * b_s + b, b)
        a = jnp.where(m, a * a_s, a)
        s *= 2
    return a, b


def _scan_rev(a, g):
    tm = a.shape[0]
    row = lax.broadcasted_iota(jnp.int32, a.shape, 0)
    s = 1
    while s < tm:
        a_s = pltpu.roll(a, tm - s, 0)
        g_s = pltpu.roll(g, tm - s, 0)
        m = row < tm - s
        g = jnp.where(m, g + a * g_s, g)
        a = jnp.where(m, a * a_s, a)
        s *= 2
    return g


def _group_rows(scr_ref, row, n_groups):
    return jnp.concatenate([scr_ref[pl.ds(c, 1), pl.ds(row, n_groups, stride=SUBLANES), :][0]
                            for c in range(scr_ref.shape[0])], axis=1)


def _spread_rows(rows_ref, n_groups, w):
    return jnp.concatenate([jnp.broadcast_to(rows_ref[g:g + 1, :], (SUBLANES, w)) for g in range(n_groups)], axis=0)


def _scan_groups(a, b, reverse):
    tm, w = a.shape
    shape3 = (tm // SUBLANES, SUBLANES, w)
    a3, b3 = a.reshape(shape3), b.reshape(shape3)
    sub = lax.broadcasted_iota(jnp.int32, shape3, 1)
    s = 1
    while s < SUBLANES:
        shift = SUBLANES - s if reverse else s
        a_s = pltpu.roll(a3, shift, 1)
        b_s = pltpu.roll(b3, shift, 1)
        m = (sub < SUBLANES - s) if reverse else (sub >= s)
        b3 = jnp.where(m, a3 * b_s + b3, b3)
        a3 = jnp.where(m, a3 * a_s, a3)
        s *= 2
    return a3.reshape(tm, w), b3.reshape(tm, w)


def _scan_fwd_tile(a, b, h_in, sa_ref, sb_ref, sc_ref):
    tm, w = a.shape
    n_groups = tm // SUBLANES
    a_loc, b_loc = _scan_groups(a, b, False)
    _put(sa_ref, a_loc)
    _put(sb_ref, b_loc)
    a_end, b_end = _scan_fwd(_group_rows(sa_ref, SUBLANES - 1, n_groups), _group_rows(sb_ref, SUBLANES - 1, n_groups))
    h_end = b_end + a_end * h_in
    sc_ref[...] = _shift_down(h_end, jnp.broadcast_to(h_in, (SUBLANES, w)), 1)
    return b_loc + a_loc * _spread_rows(sc_ref, n_groups, w), h_end


def _scan_rev_tile(a, g, sa_ref, sb_ref, sc_ref):
    tm, w = a.shape
    n_groups = tm // SUBLANES
    a_loc, g_loc = _scan_groups(a, g, True)
    _put(sa_ref, a_loc)
    _put(sb_ref, g_loc)
    d_first = _scan_rev(_group_rows(sa_ref, 0, n_groups), _group_rows(sb_ref, 0, n_groups))
    sc_ref[...] = _shift_up(d_first, jnp.zeros((SUBLANES, w), F32), 1)
    return g_loc + a_loc * _spread_rows(sc_ref, n_groups, w)


def _lane_scratch(tm, w):
    return pltpu.VMEM((w // LANES, tm, LANES), F32)


def _put(scr_ref, val):
    for c in range(scr_ref.shape[0]):
        scr_ref[c] = val[:, c * LANES:(c + 1) * LANES].astype(F32)


def _get(scr_ref):
    return jnp.concatenate([scr_ref[c] for c in range(scr_ref.shape[0])], axis=1)


def _deinterleave(src_ref, dst_ref, dil):
    nc, tm, _ = src_ref.shape
    w = nc * LANES
    for r in range(dil):
        for c in range(nc):
            piece = src_ref[pl.ds(c, 1), pl.ds(r, tm // dil, stride=dil), :][0] if dil > 1 else src_ref[c]
            dst_ref[:, r * w + c * LANES:r * w + (c + 1) * LANES] = piece.astype(dst_ref.dtype)


def _interleave(src_ref, dst_ref, dil):
    nc, tm, _ = dst_ref.shape
    w = nc * LANES
    for r in range(dil):
        for c in range(nc):
            dst_ref[pl.ds(c, 1), pl.ds(r, tm // dil, stride=dil), :] = (
                src_ref[:, r * w + c * LANES:r * w + (c + 1) * LANES].astype(F32)[None])


def _dilated_spec(tm, w, dil, index=lambda i: i):
    return pl.BlockSpec((tm // dil, dil * w), lambda i: (index(i), 0))


def _dilated_shape(S, w, dil, dtype):
    return jax.ShapeDtypeStruct((S // dil, dil * w), dtype)


def _head_masks(shape):
    lane = lax.broadcasted_iota(jnp.int32, shape, 1)
    return [(lane >= h * HEAD_DIM) & (lane < (h + 1) * HEAD_DIM) for h in range(N_HEADS)]


def _colsum(v):
    return jnp.sum(v, axis=0, keepdims=True)


def _norm_inproj(x, g, w, name):
    S, D = x.shape
    N = w.shape[1]
    tm = TM_MM
    n_abc = N_ABC * GROUP_W
    n_qkv = 3 * GROUP_W

    def body(x_ref, g_ref, w_ref, zabc_ref, zg_ref, q1_ref, q4_ref, q16_ref, qkv_ref):
        xv = x_ref[...]
        r = lax.rsqrt(jnp.mean(xv * xv, axis=-1, keepdims=True) + NORM_EPS)
        h = ((xv * r) * g_ref[...]).astype(MXU_DTYPE)
        zabc_ref[...] = jnp.dot(h, w_ref[:, 0:n_abc], preferred_element_type=F32)
        _put(qkv_ref, jnp.dot(h, w_ref[:, n_abc:n_abc + n_qkv], preferred_element_type=F32))
        zg_ref[...] = jnp.dot(h, w_ref[:, n_abc + n_qkv:], preferred_element_type=F32)
        for dil, ref in zip(ATTN_DILATIONS, (q1_ref, q4_ref, q16_ref)):
            _deinterleave(qkv_ref, ref, dil)

    row = lambda wd: pl.BlockSpec((tm, wd), lambda i: (i, 0))
    return pl.pallas_call(
        body, name=name, grid=(S // tm,),
        in_specs=[row(D), pl.BlockSpec((1, D), lambda i: (0, 0)), pl.BlockSpec((D, N), lambda i: (0, 0))],
        out_specs=[row(n_abc), row(GROUP_W)] + [_dilated_spec(tm, n_qkv, dil) for dil in ATTN_DILATIONS],
        out_shape=[jax.ShapeDtypeStruct((S, n_abc), F32), jax.ShapeDtypeStruct((S, GROUP_W), F32)]
                  + [_dilated_shape(S, n_qkv, dil, MXU_DTYPE) for dil in ATTN_DILATIONS],
        scratch_shapes=[_lane_scratch(tm, n_qkv)],
        compiler_params=_params(("parallel",)),
    )(x, g, w)


def _conv_a(z_of, halo_of, w_ref):
    p = z_of(2) * z_of(0)
    p_h = halo_of(2) * halo_of(0)
    cv = w_ref[2:3, :] * p + w_ref[1:2, :] * _shift_down(p, p_h, 1) + w_ref[0:1, :] * _shift_down(p, p_h, 2)
    return p, p_h, cv


def _lru_gates(z_of, halo_of, wr_ref, vec_ref, wa_ref, wx_ref):
    rx = z_of(4)
    rx_h = halo_of(4)
    sh = [rx, _shift_down(rx, rx_h, 1), _shift_down(rx, rx_h, 2), _shift_down(rx, rx_h, 3)]
    xc = (wr_ref[3:4, :] * sh[0] + wr_ref[2:3, :] * sh[1] + wr_ref[1:2, :] * sh[2]
          + wr_ref[0:1, :] * sh[3] + vec_ref[0:1, :])
    ga = _sigmoid(jnp.dot(xc.astype(MXU_DTYPE), wa_ref[...], preferred_element_type=F32) + vec_ref[1:2, :])
    gi = _sigmoid(jnp.dot(xc.astype(MXU_DTYPE), wx_ref[...], preferred_element_type=F32) + vec_ref[2:3, :])
    sp = _softplus(-vec_ref[3:4, :])
    log_a = (-RG_C * ga) * sp
    a = jnp.exp(log_a)
    mult = jnp.sqrt(-_expm1_nonpos(2.0 * log_a))
    return xc, sh, ga, gi, a, mult, sp


def _gmlp_fwd(z_of, vec_ref, ws_ref, bs_ref, tm):
    u = _gelu(z_of(6))
    gv = _gelu(z_of(7))
    rr = lax.rsqrt(jnp.mean(gv * gv, axis=-1, keepdims=True) + NORM_EPS)
    vn = (gv * rr) * vec_ref[4:5, :]
    masks = _head_masks((GMLP_CHUNK, GROUP_W))
    parts = []
    for c in range(tm // GMLP_CHUNK):
        vc = vn[c * GMLP_CHUNK:(c + 1) * GMLP_CHUNK].astype(MXU_DTYPE)
        acc = bs_ref[...]
        for h in range(N_HEADS):
            acc = acc + jnp.where(masks[h], jnp.dot(ws_ref[h], vc, preferred_element_type=F32), 0.0)
        parts.append(acc)
    return u, gv, rr, vn, jnp.concatenate(parts, axis=0)


def _mix_specs(tm, S, order):
    const2 = lambda shape: pl.BlockSpec(shape, lambda i: (0, 0))
    return [const2((SUBLANES, GROUP_W)), const2((SUBLANES, GROUP_W)), const2((SUBLANES, GROUP_W)),
            const2((GROUP_W, GROUP_W)), const2((GROUP_W, GROUP_W)),
            pl.BlockSpec((N_HEADS, GMLP_CHUNK, GMLP_CHUNK), lambda i: (0, 0, 0)),
            const2((GMLP_CHUNK, GROUP_W))]


def _mix_fwd(z, mp, name):
    S = z.shape[0]
    tm = TM_MIX
    hb = tm // SUBLANES
    wcols = N_ABC * GROUP_W

    def body(z_ref, zh_ref, wA_ref, wR_ref, vec_ref, wa_ref, wx_ref, ws_ref, bs_ref, y_ref, h_ref, carry_ref,
             sa_ref, sb_ref, sc_ref):
        i = pl.program_id(0)

        @pl.when(i == 0)
        def _():
            carry_ref[...] = jnp.zeros_like(carry_ref)

        not_first = i > 0
        z_of = lambda c: z_ref[:, c * GROUP_W:(c + 1) * GROUP_W]
        halo_of = lambda c: jnp.where(not_first, zh_ref[:, c * GROUP_W:(c + 1) * GROUP_W], 0.0)

        _, _, cv = _conv_a(z_of, halo_of, wA_ref)
        y_ref[:, 0:GROUP_W] = (z_of(1) * cv * _silu_and_grad(z_of(3))[0]).astype(y_ref.dtype)

        xc, _, _, gi, a, mult, _ = _lru_gates(z_of, halo_of, wR_ref, vec_ref, wa_ref, wx_ref)
        b = mult * (gi * xc)
        h, h_end = _scan_fwd_tile(a, b, carry_ref[SUBLANES - 1:SUBLANES, :], sa_ref, sb_ref, sc_ref)
        h_ref[...] = h
        carry_ref[...] = h_end[hb - SUBLANES:hb]
        y_ref[:, GROUP_W:2 * GROUP_W] = (h * _silu_and_grad(z_of(5))[0]).astype(y_ref.dtype)

        u, _, _, _, sp = _gmlp_fwd(z_of, vec_ref, ws_ref, bs_ref, tm)
        y_ref[:, 2 * GROUP_W:3 * GROUP_W] = (u * sp * _silu_and_grad(z_of(8))[0]).astype(y_ref.dtype)

    return pl.pallas_call(
        body, name=name, grid=(S // tm,),
        in_specs=[pl.BlockSpec((tm, wcols), lambda i: (i, 0)),
                  pl.BlockSpec((SUBLANES, wcols), lambda i: (jnp.maximum(i * hb - 1, 0), 0))]
                 + _mix_specs(tm, S, "fwd"),
        out_specs=[pl.BlockSpec((tm, 3 * GROUP_W), lambda i: (i, 0)),
                   pl.BlockSpec((tm, GROUP_W), lambda i: (i, 0))],
        out_shape=[jax.ShapeDtypeStruct((S, 3 * GROUP_W), MXU_DTYPE), jax.ShapeDtypeStruct((S, GROUP_W), F32)],
        scratch_shapes=[pltpu.VMEM((SUBLANES, GROUP_W), F32), _lane_scratch(tm, GROUP_W), _lane_scratch(tm, GROUP_W),
                        pltpu.VMEM((hb, GROUP_W), F32)],
        compiler_params=_params(("arbitrary",)),
    )(z, z, mp["wA"], mp["wR"], mp["vec"], mp["wa"], mp["wx"], mp["ws"], mp["bs"])


_NEG = -1e30


def _slope(h):
    return 2.0 ** (-8.0 * (h + 1) / N_HEADS)


def _attn_bias(dil, offsets, n_keys):
    shape = (ATTN_BLOCK, n_keys)
    qi = lax.broadcasted_iota(jnp.int32, shape, 0)
    ki = lax.broadcasted_iota(jnp.int32, shape, 1)
    blocks = []
    for f in offsets:
        delta = qi + f - ki
        valid = (delta >= 0) & (delta <= ATTN_BLOCK)
        dist = (delta * dil).astype(F32)
        for h in range(N_HEADS):
            blocks.append(jnp.where(valid, -_slope(h) * dist, _NEG))
    return jnp.concatenate(blocks, axis=0)


def _stack_heads(t, masks):
    return jnp.concatenate([jnp.where(m, t, jnp.zeros_like(t)) for m in masks], axis=0)


def _unstack_heads(t4, masks, base=0):
    out = t4[base * ATTN_BLOCK:(base + 1) * ATTN_BLOCK]
    for h in range(1, N_HEADS):
        out = jnp.where(masks[h], t4[(base + h) * ATTN_BLOCK:(base + h + 1) * ATTN_BLOCK], out)
    return out


def _attn_fwd(qkv, dil, name):
    rows = qkv.shape[0]
    nb = rows // ATTN_BLOCK
    scale = 1.0 / math.sqrt(HEAD_DIM)
    B = ATTN_BLOCK
    per_step = ATTN_BLOCKS_PER_STEP

    def body(q_ref, kc_ref, kp_ref, vc_ref, vp_ref, o_ref, l_ref, bias_ref):
        n = pl.program_id(1)

        @pl.when(n == 0)
        def _():
            bias_ref[...] = _attn_bias(dil, (B,), 2 * B)

        masks = _head_masks((B, GROUP_W))
        for j in range(per_step):
            own = slice(j * B, (j + 1) * B)
            before = slice((j - 1) * B, j * B)
            qs = _stack_heads(q_ref[own], masks)
            keys = jnp.concatenate([kp_ref[...] if j == 0 else kc_ref[before], kc_ref[own]], axis=0)
            vals = jnp.concatenate([vp_ref[...] if j == 0 else vc_ref[before], vc_ref[own]], axis=0)
            s = _mm_nt(qs, keys) * scale + bias_ref[...]
            if j == 0:
                key_col = lax.broadcasted_iota(jnp.int32, s.shape, 1)
                s = jnp.where((n == 0) & (key_col < B), _NEG, s)
            m = jnp.max(s, axis=-1, keepdims=True)
            p = jnp.exp(s - m)
            l = jnp.sum(p, axis=-1, keepdims=True)
            o4 = jnp.dot(p.astype(MXU_DTYPE), vals, preferred_element_type=F32)
            o_ref[own] = _unstack_heads(o4, masks) / _unstack_heads(jnp.broadcast_to(l, o4.shape), masks)
            l_ref[own] = _unstack_heads(jnp.broadcast_to(m + jnp.log(l), o4.shape), masks)

    blk = (per_step * B, GROUP_W)
    cur = lambda c: pl.BlockSpec(blk, lambda r, n: (n, r * 3 + c))
    prev = lambda c: pl.BlockSpec((B, GROUP_W), lambda r, n: (jnp.maximum(n * per_step - 1, 0), r * 3 + c))
    out = pl.BlockSpec(blk, lambda r, n: (n, r))
    return pl.pallas_call(
        body, name=name, grid=(dil, nb // per_step),
        in_specs=[cur(0), cur(1), prev(1), cur(2), prev(2)],
        out_specs=[out, out],
        out_shape=[jax.ShapeDtypeStruct((rows, dil * GROUP_W), F32)] * 2,
        scratch_shapes=[pltpu.VMEM((N_HEADS * ATTN_BLOCK, 2 * ATTN_BLOCK), F32)],
        compiler_params=_params(("parallel", "arbitrary")),
    )(qkv, qkv, qkv, qkv, qkv)


def _outproj(x, z_g, y_abc, attn, w_out, name):
    S, D = x.shape
    tm = TM_MM
    n_abc = 3 * GROUP_W

    def body(x_ref, g_ref, yabc_ref, o1, l1, o2, l2, o3, l3, w_ref,
             xn_ref, y_ref, o_ref, lse1_ref, lse4_ref, lse16_ref, so2, sl2, so3, sl3, slse):
        for src, dst, dil in ((o2, so2, ATTN_DILATIONS[1]), (l2, sl2, ATTN_DILATIONS[1]),
                              (o3, so3, ATTN_DILATIONS[2]), (l3, sl3, ATTN_DILATIONS[2])):
            _interleave(src, dst, dil)
        la, lb, lc = l1[...], _get(sl2), _get(sl3)
        mx = jnp.maximum(jnp.maximum(la, lb), lc)
        ea, eb, ec = jnp.exp(la - mx), jnp.exp(lb - mx), jnp.exp(lc - mx)
        den = ea + eb + ec
        o = (ea * o1[...] + eb * _get(so2) + ec * _get(so3)) / den
        o_ref[...] = o
        _put(slse, mx + jnp.log(den))
        for dil, ref in zip(ATTN_DILATIONS, (lse1_ref, lse4_ref, lse16_ref)):
            _deinterleave(slse, ref, dil)
        y_d = o * _silu_and_grad(g_ref[...])[0]
        y_ref[:, 0:n_abc] = yabc_ref[...].astype(MXU_DTYPE)
        y_ref[:, n_abc:] = y_d.astype(MXU_DTYPE)
        xn_ref[...] = x_ref[...] + jnp.dot(y_ref[...], w_ref[...], preferred_element_type=F32)

    row = lambda w: pl.BlockSpec((tm, w), lambda i: (i, 0))
    dil_specs = [_dilated_spec(tm, GROUP_W, dil) for dil in ATTN_DILATIONS]
    (o1, l1), (o2, l2), (o3, l3) = attn
    return pl.pallas_call(
        body, name=name, grid=(S // tm,),
        in_specs=[row(D), row(GROUP_W), row(n_abc)] + [sp for sp in dil_specs for _ in range(2)]
                 + [pl.BlockSpec(w_out.shape, lambda i: (0, 0))],
        out_specs=[row(D), row(4 * GROUP_W), row(GROUP_W)] + dil_specs,
        out_shape=[jax.ShapeDtypeStruct((S, D), F32), jax.ShapeDtypeStruct((S, 4 * GROUP_W), MXU_DTYPE),
                   jax.ShapeDtypeStruct((S, GROUP_W), F32)]
                  + [_dilated_shape(S, GROUP_W, dil, F32) for dil in ATTN_DILATIONS],
        scratch_shapes=[_lane_scratch(tm, GROUP_W)] * 5,
        compiler_params=_params(("parallel",)),
    )(x, z_g, y_abc, o1, l1, o2, l2, o3, l3, w_out)


def _loss_head(x, g, target, name):
    S, D = x.shape
    tm = TM_MM

    def body(x_ref, g_ref, t_ref, dx_ref, loss_ref, dg_ref):
        i = pl.program_id(0)

        @pl.when(i == 0)
        def _():
            loss_ref[...] = jnp.zeros_like(loss_ref)
            dg_ref[...] = jnp.zeros_like(dg_ref)

        xv = x_ref[...]
        r = lax.rsqrt(jnp.mean(xv * xv, axis=-1, keepdims=True) + NORM_EPS)
        xn = xv * r
        err = xn * g_ref[...] - t_ref[...]
        per_tok = jnp.mean(err * err, axis=-1, keepdims=True)
        loss_ref[...] += 0.5 * jnp.sum(per_tok, axis=0, keepdims=True)
        dout = err * (1.0 / D)
        dg_ref[...] += _colsum(dout * xn)
        dxn = dout * g_ref[...]
        dx_ref[...] = r * (dxn - xn * jnp.mean(dxn * xn, axis=-1, keepdims=True))

    row = pl.BlockSpec((tm, D), lambda i: (i, 0))
    return pl.pallas_call(
        body, name=name, grid=(S // tm,),
        in_specs=[row, pl.BlockSpec((1, D), lambda i: (0, 0)), row],
        out_specs=[row, pl.BlockSpec((1, LANES), lambda i: (0, 0)), pl.BlockSpec((1, D), lambda i: (0, 0))],
        out_shape=[jax.ShapeDtypeStruct((S, D), F32), jax.ShapeDtypeStruct((1, LANES), F32),
                   jax.ShapeDtypeStruct((1, D), F32)],
        compiler_params=_params(("arbitrary",)),
    )(x, g, target)


def _outproj_bwd(dx, y, w_out, name):
    S, D = dx.shape
    E = y.shape[1]
    tm = TM_MM

    def body(dx_ref, y_ref, w_ref, dy_ref, dw_ref, acc_ref):
        i = pl.program_id(0)

        @pl.when(i == 0)
        def _():
            acc_ref[...] = jnp.zeros_like(acc_ref)

        dxb = dx_ref[...].astype(MXU_DTYPE)
        dy_ref[...] = _mm_nt(dxb, w_ref[...])
        acc_ref[...] += _mm_tn(y_ref[...], dxb)

        @pl.when(i == S // tm - 1)
        def _():
            dw_ref[...] = acc_ref[...].astype(dw_ref.dtype)

    return pl.pallas_call(
        body, name=name, grid=(S // tm,),
        in_specs=[pl.BlockSpec((tm, D), lambda i: (i, 0)), pl.BlockSpec((tm, E), lambda i: (i, 0)),
                  pl.BlockSpec((E, D), lambda i: (0, 0))],
        out_specs=[pl.BlockSpec((tm, E), lambda i: (i, 0)), pl.BlockSpec((E, D), lambda i: (0, 0))],
        out_shape=[jax.ShapeDtypeStruct((S, E), F32), jax.ShapeDtypeStruct((E, D), WIRE_DTYPE)],
        scratch_shapes=[pltpu.VMEM((E, D), F32)],
        compiler_params=_params(("arbitrary",)),
    )(dx, y, w_out)


def _mix_bwd(z, z_g, dy, hs, o, mp, name):
    S = z.shape[0]
    tm = TM_MIX
    hb = tm // SUBLANES
    nT = S // tm
    last_blk = S // SUBLANES - 1
    wcols = N_ABC * GROUP_W

    def body(z_ref, zh_ref, zn_ref, zg_ref, dy_ref, dyn_ref, h_ref, hh_ref, o_ref,
             wA_ref, wR_ref, vec_ref, wa_ref, wx_ref, ws_ref, bs_ref,
             dz_ref, dzg_ref, do1_ref, do4_ref, do16_ref, dl1_ref, dl4_ref, dl16_ref,
             dwA_ref, dwR_ref, dvec_ref, dwa_ref, dwx_ref, dws_ref, dbs_ref,
             hcarry_ref, xcarry_ref, bsacc_ref, do_ref, dl_ref, sa_ref, sb_ref, sc_ref):
        i = pl.program_id(0)
        ti = nT - 1 - i

        @pl.when(i == 0)
        def _():
            hcarry_ref[...] = jnp.zeros_like(hcarry_ref)
            xcarry_ref[...] = jnp.zeros_like(xcarry_ref)
            bsacc_ref[...] = jnp.zeros_like(bsacc_ref)
            dwA_ref[...] = jnp.zeros_like(dwA_ref)
            dwR_ref[...] = jnp.zeros_like(dwR_ref)
            dvec_ref[...] = jnp.zeros_like(dvec_ref)
            dwa_ref[...] = jnp.zeros_like(dwa_ref)
            dwx_ref[...] = jnp.zeros_like(dwx_ref)
            dws_ref[...] = jnp.zeros_like(dws_ref)
            dbs_ref[...] = jnp.zeros_like(dbs_ref)

        has_prev = ti > 0
        has_next = i > 0
        col = lambda c: slice(c * GROUP_W, (c + 1) * GROUP_W)
        z_of = lambda c: z_ref[:, col(c)]
        halo_of = lambda c: jnp.where(has_prev, zh_ref[:, col(c)], 0.0)
        next_of = lambda c: zn_ref[:, col(c)]

        p, p_h, cv = _conv_a(z_of, halo_of, wA_ref)
        sg, dsg = _silu_and_grad(z_of(3))
        a_b = z_of(1)
        dya = dy_ref[:, col(0)]
        dcv = dya * a_b * sg
        dcv_n = jnp.where(has_next, dyn_ref[...] * next_of(1) * _silu_and_grad(next_of(3))[0], 0.0)
        dp = (wA_ref[2:3, :] * dcv + wA_ref[1:2, :] * _shift_up(dcv, dcv_n, 1)
              + wA_ref[0:1, :] * _shift_up(dcv, dcv_n, 2))
        dwA_ref[2:3, :] += _colsum(dcv * p)
        dwA_ref[1:2, :] += _colsum(dcv * _shift_down(p, p_h, 1))
        dwA_ref[0:1, :] += _colsum(dcv * _shift_down(p, p_h, 2))
        def put_dz(c, val):
            dz_ref[:, col(c)] = val.astype(dz_ref.dtype)

        put_dz(0, dp * z_of(2))
        put_dz(1, dya * cv * sg)
        put_dz(2, dp * z_of(0))
        put_dz(3, dya * a_b * cv * dsg)

        xc, sh, ga, gi, a, mult, sp = _lru_gates(z_of, halo_of, wR_ref, vec_ref, wa_ref, wx_ref)
        h = h_ref[...]
        h_prev = _shift_down(h, jnp.where(has_prev, hh_ref[...], 0.0), 1)
        sgr, dsgr = _silu_and_grad(z_of(5))
        dyb = dy_ref[:, col(1)]
        put_dz(5, dyb * h * dsgr)
        row = lax.broadcasted_iota(jnp.int32, (tm, GROUP_W), 0)
        g_in = dyb * sgr + jnp.where(row == tm - 1, hcarry_ref[0:1, :], 0.0)
        a_up = _shift_up(a, jnp.zeros((SUBLANES, GROUP_W), F32), 1)
        dH = _scan_rev_tile(a_up, g_in, sa_ref, sb_ref, sc_ref)
        hcarry_ref[...] = (a * dH)[0:SUBLANES]
        da = dH * h_prev
        gx = gi * xc
        dmult = dH * gx
        dgi = dH * mult * xc
        dxc = dH * mult * gi
        dlog_a = da * a - dmult * (a * a) / mult
        dga = dlog_a * (-RG_C * sp)
        dlam_row = _colsum(dlog_a * (-RG_C * ga)) * (-_sigmoid(-vec_ref[3:4, :]))
        dpre_a = dga * ga * (1.0 - ga)
        dpre_i = dgi * gi * (1.0 - gi)
        dwa_ref[...] += _mm_tn(xc, dpre_a)
        dwx_ref[...] += _mm_tn(xc, dpre_i)
        dxc = dxc + _mm_nt(dpre_a, wa_ref[...]) + _mm_nt(dpre_i, wx_ref[...])
        dvec_ref[0:1, :] += _colsum(dxc)
        dvec_ref[1:2, :] += _colsum(dpre_a)
        dvec_ref[2:3, :] += _colsum(dpre_i)
        dvec_ref[3:4, :] += dlam_row
        for k in range(4):
            dwR_ref[k:k + 1, :] += _colsum(dxc * sh[3 - k])
        dxc_n = xcarry_ref[...]
        put_dz(4, wR_ref[3:4, :] * dxc + wR_ref[2:3, :] * _shift_up(dxc, dxc_n, 1)
               + wR_ref[1:2, :] * _shift_up(dxc, dxc_n, 2) + wR_ref[0:1, :] * _shift_up(dxc, dxc_n, 3))
        xcarry_ref[...] = dxc[0:SUBLANES]

        c_u, c_v = z_of(6), z_of(7)
        u, du_dx = _gelu_and_grad(c_u)
        gv, dgv_dx = _gelu_and_grad(c_v)
        rr = lax.rsqrt(jnp.mean(gv * gv, axis=-1, keepdims=True) + NORM_EPS)
        xhat = gv * rr
        g_c = vec_ref[4:5, :]
        vn = xhat * g_c
        masks = _head_masks((GMLP_CHUNK, GROUP_W))
        tri_r = lax.broadcasted_iota(jnp.int32, (GMLP_CHUNK, GMLP_CHUNK), 0)
        tri_c = lax.broadcasted_iota(jnp.int32, (GMLP_CHUNK, GMLP_CHUNK), 1)
        tril = tri_r >= tri_c
        sgc, dsgc = _silu_and_grad(z_of(8))
        dyc = dy_ref[:, col(2)]
        dsp_full = dyc * u * sgc
        sp_parts, dvn_parts = [], []
        for c in range(tm // GMLP_CHUNK):
            rs = slice(c * GMLP_CHUNK, (c + 1) * GMLP_CHUNK)
            vc = vn[rs].astype(MXU_DTYPE)
            dsp_c = dsp_full[rs]
            bsacc_ref[...] += dsp_c
            acc = bs_ref[...]
            dvn_c = jnp.zeros((GMLP_CHUNK, GROUP_W), F32)
            for h in range(N_HEADS):
                w_h = ws_ref[h]
                acc = acc + jnp.where(masks[h], jnp.dot(w_h, vc, preferred_element_type=F32), 0.0)
                dsp_h = jnp.where(masks[h], dsp_c, 0.0).astype(MXU_DTYPE)
                dvn_c = dvn_c + _mm_tn(w_h, dsp_h)
                dws_ref[h] += jnp.where(tril, _mm_nt(dsp_h, vc), 0.0)
            sp_parts.append(acc)
            dvn_parts.append(dvn_c)
        spv = jnp.concatenate(sp_parts, axis=0)
        dvn = jnp.concatenate(dvn_parts, axis=0)
        put_dz(6, dyc * spv * sgc * du_dx)
        put_dz(8, dyc * u * spv * dsgc)
        dvec_ref[4:5, :] += _colsum(dvn * xhat)
        dgvn = dvn * g_c
        dgv = rr * (dgvn - xhat * jnp.mean(dgvn * xhat, axis=-1, keepdims=True))
        put_dz(7, dgv * dgv_dx)

        sgd, dsgd = _silu_and_grad(zg_ref[...])
        dyd = dy_ref[:, col(3)]
        ov = o_ref[...]
        do = dyd * sgd
        _put(do_ref, do)
        dzg_ref[...] = (dyd * ov * dsgd).astype(dzg_ref.dtype)
        prod = do * ov
        tmasks = _head_masks((tm, GROUP_W))
        dl = jnp.zeros((tm, GROUP_W), F32)
        for h in range(N_HEADS):
            dl = jnp.where(tmasks[h], jnp.sum(jnp.where(tmasks[h], prod, 0.0), axis=-1, keepdims=True), dl)
        _put(dl_ref, dl)
        for dil, d_out, l_out in zip(ATTN_DILATIONS, (do1_ref, do4_ref, do16_ref), (dl1_ref, dl4_ref, dl16_ref)):
            _deinterleave(do_ref, d_out, dil)
            _deinterleave(dl_ref, l_out, dil)

        @pl.when(i == nT - 1)
        def _():
            acc = bsacc_ref[...]
            lane = lax.broadcasted_iota(jnp.int32, (GMLP_CHUNK, LANES), 1)
            out = jnp.zeros((GMLP_CHUNK, LANES), F32)
            for h in range(N_HEADS):
                out = jnp.where(lane == h, jnp.sum(jnp.where(masks[h], acc, 0.0), axis=-1, keepdims=True), out)
            dbs_ref[...] = out

    rev = lambda w: pl.BlockSpec((tm, w), lambda i: (nT - 1 - i, 0))
    prev8 = lambda w: pl.BlockSpec((SUBLANES, w), lambda i: (jnp.maximum((nT - 1 - i) * hb - 1, 0), 0))
    next8 = lambda w: pl.BlockSpec((SUBLANES, w), lambda i: (jnp.minimum((nT - i) * hb, last_blk), 0))
    const2 = lambda shape: pl.BlockSpec(shape, lambda i: (0, 0))
    dil_specs = [_dilated_spec(tm, GROUP_W, dil, lambda i: nT - 1 - i) for dil in ATTN_DILATIONS]
    dil_shapes = [_dilated_shape(S, GROUP_W, dil, F32) for dil in ATTN_DILATIONS]
    small = (SUBLANES, GROUP_W)
    sq = (GROUP_W, GROUP_W)
    ws_shape = (N_HEADS, GMLP_CHUNK, GMLP_CHUNK)
    return pl.pallas_call(
        body, name=name, grid=(nT,),
        in_specs=[rev(wcols), prev8(wcols), next8(wcols), rev(GROUP_W),
                  rev(4 * GROUP_W), next8(GROUP_W), rev(GROUP_W), prev8(GROUP_W), rev(GROUP_W)]
                 + _mix_specs(tm, S, "bwd"),
        out_specs=[rev(wcols), rev(GROUP_W)] + dil_specs + dil_specs
                  + [const2(small), const2(small), const2(small), const2(sq), const2(sq),
                     pl.BlockSpec(ws_shape, lambda i: (0, 0, 0)), const2((GMLP_CHUNK, LANES))],
        out_shape=[jax.ShapeDtypeStruct((S, wcols), MXU_DTYPE), jax.ShapeDtypeStruct((S, GROUP_W), MXU_DTYPE)]
                  + dil_shapes + dil_shapes
                  + [jax.ShapeDtypeStruct(small, F32)] * 3 + [jax.ShapeDtypeStruct(sq, F32)] * 2
                  + [jax.ShapeDtypeStruct(ws_shape, F32), jax.ShapeDtypeStruct((GMLP_CHUNK, LANES), F32)],
        scratch_shapes=[pltpu.VMEM(small, F32), pltpu.VMEM(small, F32), pltpu.VMEM((GMLP_CHUNK, GROUP_W), F32),
                        _lane_scratch(tm, GROUP_W), _lane_scratch(tm, GROUP_W),
                        _lane_scratch(tm, GROUP_W), _lane_scratch(tm, GROUP_W), pltpu.VMEM((hb, GROUP_W), F32)],
        compiler_params=_params(("arbitrary",)),
    )(z, z, z, z_g, dy, dy, hs, hs, o, mp["wA"], mp["wR"], mp["vec"], mp["wa"], mp["wx"], mp["ws"], mp["bs"])


def _attn_bwd(qkv, do, lse, delta, dil, name):
    rows = qkv.shape[0]
    nb = rows // ATTN_BLOCK
    scale = 1.0 / math.sqrt(HEAD_DIM)
    B = ATTN_BLOCK
    per_step = ATTN_BLOCKS_PER_STEP
    n_steps = nb // per_step

    def body(qc_ref, qn_ref, k_ref, v_ref, doc_ref, don_ref, lc_ref, ln_ref, dc_ref, dn_ref,
             dq_ref, dk_ref, dv_ref, carry_ref, bias_ref):
        n = pl.program_id(1)

        @pl.when(n == 0)
        def _():
            carry_ref[...] = jnp.zeros_like(carry_ref)
            bias_ref[...] = _attn_bias(dil, (0, B), B)

        masks = _head_masks((B, GROUP_W))

        def per_row(tiles):
            return jnp.concatenate([jnp.max(jnp.where(masks[h], t, _NEG), axis=-1, keepdims=True)
                                    for t in tiles for h in range(N_HEADS)], axis=0)

        dq_acc = carry_ref[...]
        for j in range(per_step):
            own = slice(j * B, (j + 1) * B)
            after = slice((j + 1) * B, (j + 2) * B)
            last = j == per_step - 1
            nxt = lambda cur_ref, nxt_ref: nxt_ref[...] if last else cur_ref[after]
            kb = k_ref[own]
            vb = v_ref[own]
            qs = jnp.concatenate([_stack_heads(qc_ref[own], masks), _stack_heads(nxt(qc_ref, qn_ref), masks)], axis=0)
            dos = jnp.concatenate([_stack_heads(doc_ref[own].astype(MXU_DTYPE), masks),
                                   _stack_heads(nxt(doc_ref, don_ref).astype(MXU_DTYPE), masks)], axis=0)
            lse_rows = per_row([lc_ref[own], nxt(lc_ref, ln_ref)])
            dl_rows = per_row([dc_ref[own], nxt(dc_ref, dn_ref)])
            s = _mm_nt(qs, kb) * scale + bias_ref[...]
            if last:
                row = lax.broadcasted_iota(jnp.int32, s.shape, 0)
                s = jnp.where((n == n_steps - 1) & (row >= N_HEADS * B), _NEG, s)
            p = jnp.exp(s - lse_rows)
            dp = _mm_nt(dos, vb)
            ds = (p * (dp - dl_rows) * scale).astype(MXU_DTYPE)
            dv_ref[own] = _mm_tn(p.astype(MXU_DTYPE), dos)
            dk_ref[own] = _mm_tn(ds, qs)
            dq4 = jnp.dot(ds, kb, preferred_element_type=F32)
            dq_ref[own] = dq_acc + _unstack_heads(dq4, masks)
            dq_acc = _unstack_heads(dq4, masks, N_HEADS)
        carry_ref[...] = dq_acc

    blk = (per_step * B, GROUP_W)
    one = (B, GROUP_W)
    nxt_idx = lambda n: jnp.minimum((n + 1) * per_step, nb - 1)
    zcur = lambda c: pl.BlockSpec(blk, lambda r, n: (n, r * 3 + c))
    znext = lambda c: pl.BlockSpec(one, lambda r, n: (nxt_idx(n), r * 3 + c))
    cur = pl.BlockSpec(blk, lambda r, n: (n, r))
    nxt = pl.BlockSpec(one, lambda r, n: (nxt_idx(n), r))
    return pl.pallas_call(
        body, name=name, grid=(dil, n_steps),
        in_specs=[zcur(0), znext(0), zcur(1), zcur(2), cur, nxt, cur, nxt, cur, nxt],
        out_specs=[cur, cur, cur],
        out_shape=[jax.ShapeDtypeStruct((rows, dil * GROUP_W), F32)] * 3,
        scratch_shapes=[pltpu.VMEM(one, F32), pltpu.VMEM((2 * N_HEADS * B, B), F32)],
        compiler_params=_params(("parallel", "arbitrary")),
    )(qkv, qkv, qkv, qkv, do, do, lse, lse, delta, delta)


def _inproj_bwd(x, g, dxn, dz_abc, dqkv, dz_g, w_in, name):
    S, D = x.shape
    N = w_in.shape[1]
    tm = TM_MM
    n_abc = N_ABC * GROUP_W

    def body(x_ref, g_ref, dxn_ref, dabc_ref, q1, k1, v1, q2, k2, v2, q3, k3, v3, dg_ref, w_ref,
             dx_ref, dz_ref, h_ref, dgn_ref, s4_ref, s16_ref):
        i = pl.program_id(0)

        @pl.when(i == 0)
        def _():
            dgn_ref[...] = jnp.zeros_like(dgn_ref)

        dz_ref[:, 0:n_abc] = dabc_ref[...].astype(MXU_DTYPE)
        for j, parts in enumerate(((q1, q2, q3), (k1, k2, k3), (v1, v2, v3))):
            c0 = n_abc + j * GROUP_W
            _interleave(parts[1], s4_ref, ATTN_DILATIONS[1])
            _interleave(parts[2], s16_ref, ATTN_DILATIONS[2])
            dz_ref[:, c0:c0 + GROUP_W] = (parts[0][...] + _get(s4_ref) + _get(s16_ref)).astype(MXU_DTYPE)
        dz_ref[:, n_abc + 3 * GROUP_W:] = dg_ref[...].astype(MXU_DTYPE)
        dh = _mm_nt(dz_ref[...], w_ref[...])
        xv = x_ref[...]
        r = lax.rsqrt(jnp.mean(xv * xv, axis=-1, keepdims=True) + NORM_EPS)
        xn = xv * r
        gv = g_ref[...]
        h_ref[...] = (xn * gv).astype(MXU_DTYPE)
        dgn_ref[...] += _colsum(dh * xn)
        dn = dh * gv
        dx_ref[...] = dxn_ref[...] + r * (dn - xn * jnp.mean(dn * xn, axis=-1, keepdims=True))

    row = lambda w: pl.BlockSpec((tm, w), lambda i: (i, 0))
    flat = [t for p in dqkv for t in p]
    dil_specs = [_dilated_spec(tm, GROUP_W, dil) for dil in ATTN_DILATIONS for _ in range(3)]
    return pl.pallas_call(
        body, name=name, grid=(S // tm,),
        in_specs=[row(D), pl.BlockSpec((1, D), lambda i: (0, 0)), row(D), row(n_abc)] + dil_specs
                 + [row(GROUP_W), pl.BlockSpec((D, N), lambda i: (0, 0))],
        out_specs=[row(D), row(N), row(D), pl.BlockSpec((1, D), lambda i: (0, 0))],
        out_shape=[jax.ShapeDtypeStruct((S, D), F32), jax.ShapeDtypeStruct((S, N), MXU_DTYPE),
                   jax.ShapeDtypeStruct((S, D), MXU_DTYPE), jax.ShapeDtypeStruct((1, D), F32)],
        scratch_shapes=[_lane_scratch(tm, GROUP_W)] * 2,
        compiler_params=_params(("arbitrary",)),
    )(x, g, dxn, dz_abc, *flat, dz_g, w_in)


def _inproj_wgrad(h, dz, name):
    S, D = h.shape
    N = dz.shape[1]
    tm = TM_MM
    nj = 2
    cw = N // nj
    per = N_DEV // nj
    n_loc = N // N_DEV

    def body(h_ref, dz_ref, dw_ref, acc_ref):
        i = pl.program_id(1)

        @pl.when(i == 0)
        def _():
            acc_ref[...] = jnp.zeros_like(acc_ref)

        acc_ref[...] += _mm_tn(h_ref[...], dz_ref[...])

        @pl.when(i == S // tm - 1)
        def _():
            for b in range(per):
                dw_ref[b] = acc_ref[:, b * n_loc:(b + 1) * n_loc].astype(dw_ref.dtype)

    return pl.pallas_call(
        body, name=name, grid=(nj, S // tm),
        in_specs=[pl.BlockSpec((tm, D), lambda j, i: (i, 0)), pl.BlockSpec((tm, cw), lambda j, i: (i, j))],
        out_specs=pl.BlockSpec((per, D, n_loc), lambda j, i: (j, 0, 0)),
        out_shape=jax.ShapeDtypeStruct((N_DEV, D, n_loc), WIRE_DTYPE),
        scratch_shapes=[pltpu.VMEM((D, cw), F32)],
        compiler_params=_params(("parallel", "arbitrary")),
    )(h, dz)


def _my_place():
    return lax.axis_index("x"), lax.axis_index("y"), lax.axis_index("c")


def _peer(x, y, c, k):
    px = 1 - x if k & 4 else x
    py = 1 - y if k & 2 else y
    pc = 1 - c if k & 1 else c
    return (px, py, pc), 4 * px + 2 * py + pc


HBM_SPEC = pl.BlockSpec(memory_space=pltpu.HBM)
SEM_SPEC = pl.BlockSpec(memory_space=pltpu.SEMAPHORE)
SPLIT_EFFECT = pltpu.SideEffectType.DATAFLOW_SIDE_EFFECTING
N_PEERS = N_DEV - 1


def _exchange_copies(srcs, lands, send_sems, recv_sems, gather, arrival):
    x, y, c = _my_place()
    me = 4 * x + 2 * y + c
    copies = []
    for t in range(len(srcs)):
        for k in range(1, N_DEV):
            peer, pidx = _peer(x, y, c, k)
            copies.append(pltpu.make_async_remote_copy(
                src_ref=srcs[t] if gather else srcs[t].at[pidx],
                dst_ref=lands[t].at[pidx if arrival else me], send_sem=send_sems.at[t * N_PEERS + k - 1],
                recv_sem=recv_sems.at[t * N_PEERS + k - 1], device_id=peer, device_id_type=MESH))
    return copies


def _exchange_start(groups, gather, name):
    sizes = [len(g) for g in groups]
    srcs = [pltpu.with_memory_space_constraint(a, pltpu.HBM) for g in groups for a in g]
    land_shape = lambda a: ((N_DEV,) + a.shape) if gather else a.shape
    lands = [pltpu.with_memory_space_constraint(lax.empty(land_shape(a), a.dtype), pltpu.HBM) for a in srcs]
    n = len(srcs)
    n_g = len(groups)

    def body(*refs):
        src_refs, land_refs = refs[:n], refs[n:2 * n]
        sem_refs = refs[4 * n:4 * n + 2 * n_g]
        token = refs[-1]
        off = 0
        for gi, sz in enumerate(sizes):
            for send in _exchange_copies(src_refs[off:off + sz], land_refs[off:off + sz],
                                         sem_refs[2 * gi], sem_refs[2 * gi + 1], gather, False):
                send.start()
            off += sz
        token[...] = jnp.zeros_like(token)

    sem_shapes = [pltpu.SemaphoreType.DMA((sz * N_PEERS,)) for sz in sizes for _ in range(2)]
    outs = pl.pallas_call(
        body, name=name,
        in_specs=[HBM_SPEC] * (2 * n),
        out_specs=[HBM_SPEC] * (2 * n) + [SEM_SPEC] * (2 * n_g) + [pl.BlockSpec(memory_space=pltpu.VMEM)],
        out_shape=[pltpu.HBM(a.shape, a.dtype) for a in srcs + lands] + sem_shapes
                  + [jax.ShapeDtypeStruct((SUBLANES, LANES), F32)],
        input_output_aliases={i: i for i in range(2 * n)},
        compiler_params=pltpu.CompilerParams(has_side_effects=SPLIT_EFFECT),
    )(*srcs, *lands)
    handles, off = [], 0
    for gi, sz in enumerate(sizes):
        handles.append((outs[2 * n + 2 * gi], outs[2 * n + 2 * gi + 1], outs[off:off + sz], outs[n + off:n + off + sz]))
        off += sz
    return handles, outs[-1]


def _exchange_wait(handle, after, gather, name):
    send_sems, recv_sems, srcs, lands = handle
    n = len(srcs)

    def body(*refs):
        src_refs, land_refs = refs[:n], refs[n:2 * n]
        for send in _exchange_copies(src_refs, land_refs, refs[2 * n], refs[2 * n + 1], gather, False):
            send.wait_send()
        for arrival in _exchange_copies(src_refs, land_refs, refs[2 * n], refs[2 * n + 1], gather, True):
            arrival.wait_recv()

    outs = pl.pallas_call(
        body, name=name,
        in_specs=[HBM_SPEC] * (2 * n) + [SEM_SPEC, SEM_SPEC, pl.BlockSpec(memory_space=pl.ANY)],
        out_specs=[HBM_SPEC] * (2 * n),
        out_shape=[pltpu.HBM(a.shape, a.dtype) for a in list(srcs) + list(lands)],
        input_output_aliases={i: i for i in range(2 * n)},
        compiler_params=pltpu.CompilerParams(has_side_effects=SPLIT_EFFECT),
    )(*srcs, *lands, send_sems, recv_sems, after)
    return outs[:n], outs[n:]


def _allreduce(buf, name):
    R = buf.shape[0]

    def body(x_ref, out_ref, recv_ref, send_sems, recv_sems):
        x, y, c = _my_place()
        out_ref[...] = x_ref[...]
        for s, k in enumerate((1, 4, 2)):
            peer, _ = _peer(x, y, c, k)
            cp = pltpu.make_async_remote_copy(
                src_ref=out_ref, dst_ref=recv_ref.at[s], send_sem=send_sems.at[s], recv_sem=recv_sems.at[s],
                device_id=peer, device_id_type=MESH)
            cp.start()
            cp.wait()
            out_ref[...] = out_ref[...] + recv_ref[s]

    vm = pl.BlockSpec(memory_space=pltpu.VMEM)
    return pl.pallas_call(
        body, name=name, in_specs=[vm], out_specs=vm,
        out_shape=jax.ShapeDtypeStruct((R, LANES), F32),
        scratch_shapes=[pltpu.VMEM((3, R, LANES), F32), pltpu.SemaphoreType.DMA((3,)), pltpu.SemaphoreType.DMA((3,))],
        compiler_params=pltpu.CompilerParams(has_side_effects=True, vmem_limit_bytes=VMEM_LIMIT),
    )(buf)


def _adamw_math(w, g, m, v):
    m = ADAM_B1 * m + (1.0 - ADAM_B1) * g
    v = ADAM_B2 * v + (1.0 - ADAM_B2) * (g * g)
    m_hat = m / (1.0 - ADAM_B1 ** ADAM_STEP)
    v_hat = v / (1.0 - ADAM_B2 ** ADAM_STEP)
    delta = -ADAM_LR * (m_hat / (jnp.sqrt(v_hat) + ADAM_EPS) + ADAM_WD * w)
    return delta, m, v


def _adamw_summed(parts, w, m, v, tr, name):
    R, C = w.shape

    def body(p_ref, w_ref, m_ref, v_ref, g_ref, d_ref, nm_ref, nv_ref):
        g = p_ref[0].astype(F32)
        for j in range(1, N_DEV):
            g = g + p_ref[j].astype(F32)
        g_ref[...] = g
        d_ref[...], nm_ref[...], nv_ref[...] = _adamw_math(w_ref[...], g, m_ref[...], v_ref[...])

    row = pl.BlockSpec((tr, C), lambda i: (i, 0))
    return pl.pallas_call(
        body, name=name, grid=(R // tr,),
        in_specs=[pl.BlockSpec((N_DEV, tr, C), lambda i: (0, i, 0)), row, row, row],
        out_specs=[row] * 4, out_shape=[jax.ShapeDtypeStruct((R, C), F32)] * 4,
        compiler_params=_params(("parallel",)),
    )(parts, w, m, v)


def _adamw_small(w, g, m, v, name):
    def body(w_ref, g_ref, m_ref, v_ref, d_ref, nm_ref, nv_ref):
        d_ref[...], nm_ref[...], nv_ref[...] = _adamw_math(w_ref[...], g_ref[...], m_ref[...], v_ref[...])

    vm = pl.BlockSpec(memory_space=pltpu.VMEM)
    return pl.pallas_call(
        body, name=name, in_specs=[vm] * 4, out_specs=[vm] * 3,
        out_shape=[jax.ShapeDtypeStruct(w.shape, F32)] * 3,
        compiler_params=pltpu.CompilerParams(vmem_limit_bytes=VMEM_LIMIT),
    )(w, g, m, v)


def _pack(arrays):
    flat = jnp.concatenate([a.reshape(-1) for a in arrays])
    pad = (-flat.shape[0]) % (SUBLANES * LANES)
    return jnp.pad(flat, (0, pad)).reshape(-1, LANES)


def _unpack(buf, like):
    flat = buf.reshape(-1)
    out, off = [], 0
    for a in like:
        out.append(flat[off:off + a.size].reshape(a.shape))
        off += a.size
    return out


def _block_diag(w):
    eye = jnp.eye(N_HEADS, dtype=w.dtype)
    return jnp.einsum('hij,hk->hikj', w, eye).reshape(GROUP_W, GROUP_W)


def _diag_blocks(w):
    return jnp.einsum('hihj->hij', w.reshape(N_HEADS, HEAD_DIM, N_HEADS, HEAD_DIM))


def _pad_rows(a):
    return jnp.pad(a, ((0, SUBLANES - a.shape[0]), (0, 0)))


def _mixer_params(l, conv_a_w, conv_r_w, conv_r_b, lru_wa, lru_ba, lru_wx, lru_bx, lru_lambda, gmlp_norm_g,
                  gmlp_ws, gmlp_bs):
    tril = jnp.tril(jnp.ones((GMLP_CHUNK, GMLP_CHUNK), dtype=bool))
    vec = jnp.stack([conv_r_b[l], lru_ba[l], lru_bx[l], lru_lambda[l], gmlp_norm_g[l]])
    return {
        "wA": _pad_rows(conv_a_w[l]), "wR": _pad_rows(conv_r_w[l]), "vec": _pad_rows(vec),
        "wa": _block_diag(lru_wa[l]).astype(MXU_DTYPE), "wx": _block_diag(lru_wx[l]).astype(MXU_DTYPE),
        "ws": jnp.where(tril[None], gmlp_ws[l], 0.0).astype(MXU_DTYPE),
        "bs": jnp.repeat(jnp.transpose(gmlp_bs[l]), HEAD_DIM, axis=1),
    }


def _local_step(x, loss_target, norm_g, get_weights, emit_wgrads, conv_a_w, conv_r_w, conv_r_b, lru_wa, lru_ba,
                lru_wx, lru_bx, lru_lambda, gmlp_norm_g, gmlp_ws, gmlp_bs, final_g):
    depth = norm_g.shape[0]
    D = x.shape[1]
    small = (conv_a_w, conv_r_w, conv_r_b, lru_wa, lru_ba, lru_wx, lru_bx, lru_lambda, gmlp_norm_g, gmlp_ws, gmlp_bs)
    saved = []
    for l in range(depth):
        mp = _mixer_params(l, *small)
        w_in_l, w_out_l = get_weights(l, x)
        z, z_g, *qkv = _norm_inproj(x, norm_g[l].reshape(1, D), w_in_l, f"norm_inproj_{l}")
        y_abc, hs = _mix_fwd(z, mp, f"mix_fwd_{l}")
        attn = [_attn_fwd(qkv[p], dil, f"attn_fwd_d{dil}_{l}") for p, dil in enumerate(ATTN_DILATIONS)]
        x_new, y, o, *lse = _outproj(x, z_g, y_abc, attn, w_out_l, f"outproj_{l}")
        saved.append((x, z, z_g, qkv, hs, y, o, lse, mp, w_in_l, w_out_l))
        x = x_new
    dx, loss, d_final_g = _loss_head(x, final_g.reshape(1, D), loss_target, "loss_head")
    grads = []
    token = None
    for l in reversed(range(depth)):
        x_l, z, z_g, qkv, hs, y, o, lse, mp, w_in_l, w_out_l = saved[l]
        if token is not None:
            mp = dict(mp, vec=mp["vec"] + token[0, 0])
        dy, dw_out = _outproj_bwd(dx, y, w_out_l, f"outproj_bwd_{l}")
        (dz_abc, dz_g, do1, do4, do16, dl1, dl4, dl16, dwA, dwR, dvec, dwa, dwx, dws, dbs) = _mix_bwd(
            z, z_g, dy, hs, o, mp, f"mix_bwd_{l}")
        dqkv = [_attn_bwd(qkv[p], do, lse[p], dl, dil, f"attn_bwd_d{dil}_{l}")
                for p, (dil, do, dl) in enumerate(zip(ATTN_DILATIONS, (do1, do4, do16), (dl1, dl4, dl16)))]
        dx, dz, h, dng = _inproj_bwd(x_l, norm_g[l].reshape(1, D), dx, dz_abc, dqkv, dz_g, w_in_l,
                                     f"inproj_bwd_{l}")
        dw_in = _inproj_wgrad(h, dz, f"inproj_wgrad_{l}")
        token = emit_wgrads(l, dw_in, dw_out)
        grads.append({
            "norm_g": dng[0],
            "conv_a_w": dwA[:conv_a_w.shape[1]], "conv_r_w": dwR[:conv_r_w.shape[1]],
            "conv_r_b": dvec[0], "lru_ba": dvec[1], "lru_bx": dvec[2], "lru_lambda": dvec[3], "gmlp_norm_g": dvec[4],
            "lru_wa": _diag_blocks(dwa), "lru_wx": _diag_blocks(dwx), "gmlp_ws": dws,
            "gmlp_bs": jnp.transpose(dbs[:, :N_HEADS]),
        })
    grads = grads[::-1]
    stacked = {k: jnp.stack([g[k] for g in grads]) for k in grads[0]}
    stacked["final_g"] = d_final_g[0]
    return loss[0, 0], dx, stacked


SMALL_NAMES = ("norm_g", "conv_a_w", "conv_r_w", "conv_r_b", "lru_wa", "lru_ba", "lru_wx", "lru_bx", "lru_lambda",
               "gmlp_norm_g", "gmlp_ws", "gmlp_bs", "final_g")
WEIGHT_NAMES = ("norm_g", "w_in", "conv_a_w", "conv_r_w", "conv_r_b", "lru_wa", "lru_ba", "lru_wx", "lru_bx",
                "lru_lambda", "gmlp_norm_g", "gmlp_ws", "gmlp_bs", "w_out", "final_g")


def kernel(x, norm_g, w_in, conv_a_w, conv_r_w, conv_r_b, lru_wa, lru_ba, lru_wx, lru_bx, lru_lambda, gmlp_norm_g, gmlp_ws, gmlp_bs, w_out, final_g, loss_target, m_norm_g, m_w_in, m_conv_a_w, m_conv_r_w, m_conv_r_b, m_lru_wa, m_lru_ba, m_lru_wx, m_lru_bx, m_lru_lambda, m_gmlp_norm_g, m_gmlp_ws, m_gmlp_bs, m_w_out, m_final_g, v_norm_g, v_w_in, v_conv_a_w, v_conv_r_w, v_conv_r_b, v_lru_wa, v_lru_ba, v_lru_wx, v_lru_bx, v_lru_lambda, v_gmlp_norm_g, v_gmlp_ws, v_gmlp_bs, v_w_out, v_final_g):
    w = dict(norm_g=norm_g, w_in=w_in, conv_a_w=conv_a_w, conv_r_w=conv_r_w, conv_r_b=conv_r_b, lru_wa=lru_wa,
             lru_ba=lru_ba, lru_wx=lru_wx, lru_bx=lru_bx, lru_lambda=lru_lambda, gmlp_norm_g=gmlp_norm_g,
             gmlp_ws=gmlp_ws, gmlp_bs=gmlp_bs, w_out=w_out, final_g=final_g)
    m = dict(norm_g=m_norm_g, w_in=m_w_in, conv_a_w=m_conv_a_w, conv_r_w=m_conv_r_w, conv_r_b=m_conv_r_b,
             lru_wa=m_lru_wa, lru_ba=m_lru_ba, lru_wx=m_lru_wx, lru_bx=m_lru_bx, lru_lambda=m_lru_lambda,
             gmlp_norm_g=m_gmlp_norm_g, gmlp_ws=m_gmlp_ws, gmlp_bs=m_gmlp_bs, w_out=m_w_out, final_g=m_final_g)
    v = dict(norm_g=v_norm_g, w_in=v_w_in, conv_a_w=v_conv_a_w, conv_r_w=v_conv_r_w, conv_r_b=v_conv_r_b,
             lru_wa=v_lru_wa, lru_ba=v_lru_ba, lru_wx=v_lru_wx, lru_bx=v_lru_bx, lru_lambda=v_lru_lambda,
             gmlp_norm_g=v_gmlp_norm_g, gmlp_ws=v_gmlp_ws, gmlp_bs=v_gmlp_bs, w_out=v_w_out, final_g=v_final_g)
    depth, D, n_loc = w_in.shape
    e_loc = w_out.shape[1]
    cx, cy, cc = _my_place()
    me = 4 * cx + 2 * cy + cc

    def with_own(srcs, lands, gather):
        own = [s[None] if gather else lax.dynamic_slice_in_dim(s, me, 1, axis=0) for s in srcs]
        return [lax.dynamic_update_slice_in_dim(ld, o, me, axis=0) for ld, o in zip(lands, own)]

    w_in_w, w_out_w = w_in.astype(MXU_DTYPE), w_out.astype(MXU_DTYPE)
    c_loc = conv_a_w.shape[2]
    taps = (conv_a_w, conv_r_w)
    groups = [[w_in_w[l], w_out_w[l]] for l in range(depth)]
    groups[0].append(_pack(taps))
    gathers, _ = _exchange_start(groups, True, "gather_start")

    def gathered(l, after):
        srcs, lands = _exchange_wait(gathers[l], after, True, f"gather_wait_{l}")
        g_in, g_out, *rest = with_own(srcs, lands, True)
        return (jnp.transpose(g_in, (1, 0, 2)).reshape(D, N_DEV * n_loc), g_out.reshape(N_DEV * e_loc, D)), rest

    first_weights, (g_taps,) = gathered(0, x)
    g_taps = g_taps.reshape(N_DEV, -1)
    conv_full, off = [], 0
    for a in taps:
        part = g_taps[:, off:off + a.size].reshape((N_DEV,) + a.shape)
        conv_full.append(jnp.transpose(part, (1, 2, 0, 3)).reshape(a.shape[:2] + (N_DEV * c_loc,)))
        off += a.size
    conv_a_full, conv_r_full = conv_full

    def get_weights(l, after):
        return first_weights if l == 0 else gathered(l, after)[0]

    scatters = {}

    def emit_wgrads(l, dw_in, dw_out):
        handles, token = _exchange_start([[dw_in, dw_out.reshape(N_DEV, e_loc, D)]], False, f"scatter_start_{l}")
        scatters[l] = handles[0]
        return token

    loss, grad_x, g = _local_step(
        x[0], loss_target[0], norm_g, get_weights, emit_wgrads, conv_a_full, conv_r_full, conv_r_b, lru_wa, lru_ba,
        lru_wx, lru_bx, lru_lambda, gmlp_norm_g, gmlp_ws, gmlp_bs, final_g)
    loss = lax.psum(loss, ("x", "y", "c"))

    per_layer = {"w_in": [], "w_out": []}
    for l in reversed(range(depth)):
        srcs, lands = _exchange_wait(scatters[l], grad_x, False, f"scatter_wait_{l}")
        r_in, r_out = with_own(srcs, lands, False)
        per_layer["w_in"].append(_adamw_summed(r_in, w_in[l], m_w_in[l], v_w_in[l], 512, f"adamw_w_in_{l}"))
        per_layer["w_out"].append(_adamw_summed(r_out, w_out[l], m_w_out[l], v_w_out[l], 128, f"adamw_w_out_{l}"))
    big = {k: [jnp.stack(parts) for parts in zip(*res[::-1])] for k, res in per_layer.items()}

    g_small = [g[k] for k in SMALL_NAMES]
    g_small = _unpack(_allreduce(_pack(g_small), "allreduce_small_grads"), g_small)
    g_small = dict(zip(SMALL_NAMES, g_small))
    for k in ("conv_a_w", "conv_r_w"):
        g_small[k] = lax.dynamic_slice_in_dim(g_small[k], me * c_loc, c_loc, axis=2)
    packs = [_pack([d[k] for k in SMALL_NAMES]) for d in (w, g_small, m, v)]
    res = _adamw_small(*packs, "adamw_small")
    like = [w[k] for k in SMALL_NAMES]
    d_s, m_s, v_s = (dict(zip(SMALL_NAMES, _unpack(r, like))) for r in res)

    grad, delta, new_m, new_v = {}, {}, {}, {}
    for k in WEIGHT_NAMES:
        if k in big:
            grad[k], delta[k], new_m[k], new_v[k] = big[k]
        else:
            grad[k], delta[k], new_m[k], new_v[k] = g_small[k], d_s[k], m_s[k], v_s[k]
    return (loss, grad_x[None], *[grad[k] for k in WEIGHT_NAMES], *[delta[k] for k in WEIGHT_NAMES],
            *[new_m[k] for k in WEIGHT_NAMES], *[new_v[k] for k in WEIGHT_NAMES])
```

```python
import functools
import math

import jax
import jax.numpy as jnp
from jax import lax
from jax.experimental import pallas as pl
from jax.experimental.pallas import tpu as pltpu

F32 = jnp.float32
MXU_DTYPE = jnp.bfloat16
WIRE_DTYPE = jnp.bfloat16
MESH = pl.DeviceIdType.MESH

N_DEV = 8
GROUP_W = 256
N_HEADS = 4
HEAD_DIM = 64
N_CHUNKS = 13
N_ABC = 9
GMLP_CHUNK = 128
ATTN_BLOCK = 128
ATTN_BLOCKS_PER_STEP = 4
ATTN_DILATIONS = (1, 4, 16)
NORM_EPS = 1e-6
RG_C = 8.0
SUBLANES = 8
LANES = 128
VMEM_LIMIT = 56 * 1024 * 1024

ADAM_LR = 0.001
ADAM_B1 = 0.9
ADAM_B2 = 0.999
ADAM_EPS = 1e-08
ADAM_WD = 0.01
ADAM_STEP = 10

TM_MIX = 512
TM_MM = 512


def _params(sem, vmem=VMEM_LIMIT):
    return pltpu.CompilerParams(dimension_semantics=sem, vmem_limit_bytes=vmem)


def _mm(a, b):
    return jnp.dot(a.astype(MXU_DTYPE), b.astype(MXU_DTYPE), preferred_element_type=F32)


def _mm_tn(a, b):
    return lax.dot_general(a.astype(MXU_DTYPE), b.astype(MXU_DTYPE), (((0,), (0,)), ((), ())),
                           preferred_element_type=F32)


def _mm_nt(a, b):
    return lax.dot_general(a.astype(MXU_DTYPE), b.astype(MXU_DTYPE), (((1,), (1,)), ((), ())),
                           preferred_element_type=F32)


def _sigmoid(x):
    return 0.5 * jnp.tanh(0.5 * x) + 0.5


def _silu_and_grad(x):
    s = _sigmoid(x)
    return x * s, s * (1.0 + x * (1.0 - s))


_GELU_K = math.sqrt(2.0 / math.pi)
_GELU_C = 0.044715


def _gelu_and_grad(x):
    x2 = x * x
    t = jnp.tanh(_GELU_K * (x + _GELU_C * x * x2))
    val = 0.5 * x * (1.0 + t)
    grad = 0.5 * (1.0 + t) + 0.5 * x * (1.0 - t * t) * (_GELU_K * (1.0 + 3.0 * _GELU_C * x2))
    return val, grad


def _gelu(x):
    return 0.5 * x * (1.0 + jnp.tanh(_GELU_K * (x + _GELU_C * x * x * x)))


def _expm1_nonpos(u):
    poly = 1.0 / math.factorial(9)
    for k in range(8, 0, -1):
        poly = poly * u + 1.0 / math.factorial(k)
    return jnp.where(u > -0.25, poly * u, jnp.exp(u) - 1.0)


def _softplus(x):
    return jnp.maximum(x, 0.0) + jnp.log(1.0 + jnp.exp(-jnp.abs(x)))


def _shift_down(t, halo, k):
    rolled = pltpu.roll(t, k, 0)
    hr = pltpu.roll(halo, k, 0)
    row = lax.broadcasted_iota(jnp.int32, halo.shape, 0)
    first = jnp.where(row < k, hr, rolled[0:SUBLANES])
    return jnp.concatenate([first, rolled[SUBLANES:]], axis=0)


def _shift_up(t, nxt, k):
    tm = t.shape[0]
    rolled = pltpu.roll(t, tm - k, 0)
    nr = pltpu.roll(nxt, SUBLANES - k, 0)
    row = lax.broadcasted_iota(jnp.int32, nxt.shape, 0)
    last = jnp.where(row >= SUBLANES - k, nr, rolled[tm - SUBLANES:tm])
    return jnp.concatenate([rolled[:tm - SUBLANES], last], axis=0)


def _scan_fwd(a, b):
    tm = a.shape[0]
    row = lax.broadcasted_iota(jnp.int32, a.shape, 0)
    s = 1
    while s < tm:
        a_s = pltpu.roll(a, s, 0)
        b_s = pltpu.roll(b, s, 0)
        m = row >= s
        b = jnp.where(m, a * b_s + b, b)
        a = jnp.where(m, a * a_s, a)
        s *= 2
    return a, b


def _scan_rev(a, g):
    tm = a.shape[0]
    row = lax.broadcasted_iota(jnp.int32, a.shape, 0)
    s = 1
    while s < tm:
        a_s = pltpu.roll(a, tm - s, 0)
        g_s = pltpu.roll(g, tm - s, 0)
        m = row < tm - s
        g = jnp.where(m, g + a * g_s, g)
        a = jnp.where(m, a * a_s, a)
        s *= 2
    return g


def _group_rows(scr_ref, row, n_groups):
    return jnp.concatenate([scr_ref[pl.ds(c, 1), pl.ds(row, n_groups, stride=SUBLANES), :][0]
                            for c in range(scr_ref.shape[0])], axis=1)


def _spread_rows(rows_ref, n_groups, w):
    return jnp.concatenate([jnp.broadcast_to(rows_ref[g:g + 1, :], (SUBLANES, w)) for g in range(n_groups)], axis=0)


def _scan_groups(a, b, reverse):
    tm, w = a.shape
    shape3 = (tm // SUBLANES, SUBLANES, w)
    a3, b3 = a.reshape(shape3), b.reshape(shape3)
    sub = lax.broadcasted_iota(jnp.int32, shape3, 1)
    s = 1
    while s < SUBLANES:
        shift = SUBLANES - s if reverse else s
        a_s = pltpu.roll(a3, shift, 1)
        b_s = pltpu.roll(b3, shift, 1)
        m = (sub < SUBLANES - s) if reverse else (sub >= s)
        b3 = jnp.where(m, a3 * b_s + b3, b3)
        a3 = jnp.where(m, a3 * a_s, a3)
        s *= 2
    return a3.reshape(tm, w), b3.reshape(tm, w)


def _scan_fwd_tile(a, b, h_in, sa_ref, sb_ref, sc_ref):
    tm, w = a.shape
    n_groups = tm // SUBLANES
    a_loc, b_loc = _scan_groups(a, b, False)
    _put(sa_ref, a_loc)
    _put(sb_ref, b_loc)
    a_end, b_end = _scan_fwd(_group_rows(sa_ref, SUBLANES - 1, n_groups), _group_rows(sb_ref, SUBLANES - 1, n_groups))
    h_end = b_end + a_end * h_in
    sc_ref[...] = _shift_down(h_end, jnp.broadcast_to(h_in, (SUBLANES, w)), 1)
    return b_loc + a_loc * _spread_rows(sc_ref, n_groups, w), h_end


def _scan_rev_tile(a, g, sa_ref, sb_ref, sc_ref):
    tm, w = a.shape
    n_groups = tm // SUBLANES
    a_loc, g_loc = _scan_groups(a, g, True)
    _put(sa_ref, a_loc)
    _put(sb_ref, g_loc)
    d_first = _scan_rev(_group_rows(sa_ref, 0, n_groups), _group_rows(sb_ref, 0, n_groups))
    sc_ref[...] = _shift_up(d_first, jnp.zeros((SUBLANES, w), F32), 1)
    return g_loc + a_loc * _spread_rows(sc_ref, n_groups, w)


def _lane_scratch(tm, w):
    return pltpu.VMEM((w // LANES, tm, LANES), F32)


def _put(scr_ref, val):
    for c in range(scr_ref.shape[0]):
        scr_ref[c] = val[:, c * LANES:(c + 1) * LANES].astype(F32)


def _get(scr_ref):
    return jnp.concatenate([scr_ref[c] for c in range(scr_ref.shape[0])], axis=1)


def _deinterleave(src_ref, dst_ref, dil):
    nc, tm, _ = src_ref.shape
    w = nc * LANES
    for r in range(dil):
        for c in range(nc):
            piece = src_ref[pl.ds(c, 1), pl.ds(r, tm // dil, stride=dil), :][0] if dil > 1 else src_ref[c]
            dst_ref[:, r * w + c * LANES:r * w + (c + 1) * LANES] = piece.astype(dst_ref.dtype)


def _interleave(src_ref, dst_ref, dil):
    nc, tm, _ = dst_ref.shape
    w = nc * LANES
    for r in range(dil):
        for c in range(nc):
            dst_ref[pl.ds(c, 1), pl.ds(r, tm // dil, stride=dil), :] = (
                src_ref[:, r * w + c * LANES:r * w + (c + 1) * LANES].astype(F32)[None])


def _dilated_spec(tm, w, dil, index=lambda i: i):
    return pl.BlockSpec((tm // dil, dil * w), lambda i: (index(i), 0))


def _dilated_shape(S, w, dil, dtype):
    return jax.ShapeDtypeStruct((S // dil, dil * w), dtype)


def _head_masks(shape):
    lane = lax.broadcasted_iota(jnp.int32, shape, 1)
    return [(lane >= h * HEAD_DIM) & (lane < (h + 1) * HEAD_DIM) for h in range(N_HEADS)]


def _colsum(v):
    return jnp.sum(v, axis=0, keepdims=True)


def _norm_inproj(x, g, w, name):
    S, D = x.shape
    N = w.shape[1]
    tm = TM_MM
    n_abc = N_ABC * GROUP_W
    n_qkv = 3 * GROUP_W

    def body(x_ref, g_ref, w_ref, zabc_ref, zg_ref, q1_ref, q4_ref, q16_ref, qkv_ref):
        xv = x_ref[...]
        r = lax.rsqrt(jnp.mean(xv * xv, axis=-1, keepdims=True) + NORM_EPS)
        h = ((xv * r) * g_ref[...]).astype(MXU_DTYPE)
        zabc_ref[...] = jnp.dot(h, w_ref[:, 0:n_abc], preferred_element_type=F32)
        _put(qkv_ref, jnp.dot(h, w_ref[:, n_abc:n_abc + n_qkv], preferred_element_type=F32))
        zg_ref[...] = jnp.dot(h, w_ref[:, n_abc + n_qkv:], preferred_element_type=F32)
        for dil, ref in zip(ATTN_DILATIONS, (q1_ref, q4_ref, q16_ref)):
            _deinterleave(qkv_ref, ref, dil)

    row = lambda wd: pl.BlockSpec((tm, wd), lambda i: (i, 0))
    return pl.pallas_call(
        body, name=name, grid=(S // tm,),
        in_specs=[row(D), pl.BlockSpec((1, D), lambda i: (0, 0)), pl.BlockSpec((D, N), lambda i: (0, 0))],
        out_specs=[row(n_abc), row(GROUP_W)] + [_dilated_spec(tm, n_qkv, dil) for dil in ATTN_DILATIONS],
        out_shape=[jax.ShapeDtypeStruct((S, n_abc), F32), jax.ShapeDtypeStruct((S, GROUP_W), F32)]
                  + [_dilated_shape(S, n_qkv, dil, MXU_DTYPE) for dil in ATTN_DILATIONS],
        scratch_shapes=[_lane_scratch(tm, n_qkv)],
        compiler_params=_params(("parallel",)),
    )(x, g, w)


def _conv_a(z_of, halo_of, w_ref):
    p = z_of(2) * z_of(0)
    p_h = halo_of(2) * halo_of(0)
    cv = w_ref[2:3, :] * p + w_ref[1:2, :] * _shift_down(p, p_h, 1) + w_ref[0:1, :] * _shift_down(p, p_h, 2)
    return p, p_h, cv


def _lru_gates(z_of, halo_of, wr_ref, vec_ref, wa_ref, wx_ref):
    rx = z_of(4)
    rx_h = halo_of(4)
    sh = [rx, _shift_down(rx, rx_h, 1), _shift_down(rx, rx_h, 2), _shift_down(rx, rx_h, 3)]
    xc = (wr_ref[3:4, :] * sh[0] + wr_ref[2:3, :] * sh[1] + wr_ref[1:2, :] * sh[2]
          + wr_ref[0:1, :] * sh[3] + vec_ref[0:1, :])
    ga = _sigmoid(jnp.dot(xc.astype(MXU_DTYPE), wa_ref[...], preferred_element_type=F32) + vec_ref[1:2, :])
    gi = _sigmoid(jnp.dot(xc.astype(MXU_DTYPE), wx_ref[...], preferred_element_type=F32) + vec_ref[2:3, :])
    sp = _softplus(-vec_ref[3:4, :])
    log_a = (-RG_C * ga) * sp
    a = jnp.exp(log_a)
    mult = jnp.sqrt(-_expm1_nonpos(2.0 * log_a))
    return xc, sh, ga, gi, a, mult, sp


def _gmlp_fwd(z_of, vec_ref, ws_ref, bs_ref, tm):
    u = _gelu(z_of(6))
    gv = _gelu(z_of(7))
    rr = lax.rsqrt(jnp.mean(gv * gv, axis=-1, keepdims=True) + NORM_EPS)
    vn = (gv * rr) * vec_ref[4:5, :]
    masks = _head_masks((GMLP_CHUNK, GROUP_W))
    parts = []
    for c in range(tm // GMLP_CHUNK):
        vc = vn[c * GMLP_CHUNK:(c + 1) * GMLP_CHUNK].astype(MXU_DTYPE)
        acc = bs_ref[...]
        for h in range(N_HEADS):
            acc = acc + jnp.where(masks[h], jnp.dot(ws_ref[h], vc, preferred_element_type=F32), 0.0)
        parts.append(acc)
    return u, gv, rr, vn, jnp.concatenate(parts, axis=0)


def _mix_specs(tm, S, order):
    const2 = lambda shape: pl.BlockSpec(shape, lambda i: (0, 0))
    return [const2((SUBLANES, GROUP_W)), const2((SUBLANES, GROUP_W)), const2((SUBLANES, GROUP_W)),
            const2((GROUP_W, GROUP_W)), const2((GROUP_W, GROUP_W)),
            pl.BlockSpec((N_HEADS, GMLP_CHUNK, GMLP_CHUNK), lambda i: (0, 0, 0)),
            const2((GMLP_CHUNK, GROUP_W))]


def _mix_fwd(z, mp, name):
    S = z.shape[0]
    tm = TM_MIX
    hb = tm // SUBLANES
    wcols = N_ABC * GROUP_W

    def body(z_ref, zh_ref, wA_ref, wR_ref, vec_ref, wa_ref, wx_ref, ws_ref, bs_ref, y_ref, h_ref, carry_ref,
             sa_ref, sb_ref, sc_ref):
        i = pl.program_id(0)

        @pl.when(i == 0)
        def _():
            carry_ref[...] = jnp.zeros_like(carry_ref)

        not_first = i > 0
        z_of = lambda c: z_ref[:, c * GROUP_W:(c + 1) * GROUP_W]
        halo_of = lambda c: jnp.where(not_first, zh_ref[:, c * GROUP_W:(c + 1) * GROUP_W], 0.0)

        _, _, cv = _conv_a(z_of, halo_of, wA_ref)
        y_ref[:, 0:GROUP_W] = (z_of(1) * cv * _silu_and_grad(z_of(3))[0]).astype(y_ref.dtype)

        xc, _, _, gi, a, mult, _ = _lru_gates(z_of, halo_of, wR_ref, vec_ref, wa_ref, wx_ref)
        b = mult * (gi * xc)
        h, h_end = _scan_fwd_tile(a, b, carry_ref[SUBLANES - 1:SUBLANES, :], sa_ref, sb_ref, sc_ref)
        h_ref[...] = h
        carry_ref[...] = h_end[hb - SUBLANES:hb]
        y_ref[:, GROUP_W:2 * GROUP_W] = (h * _silu_and_grad(z_of(5))[0]).astype(y_ref.dtype)

        u, _, _, _, sp = _gmlp_fwd(z_of, vec_ref, ws_ref, bs_ref, tm)
        y_ref[:, 2 * GROUP_W:3 * GROUP_W] = (u * sp * _silu_and_grad(z_of(8))[0]).astype(y_ref.dtype)

    return pl.pallas_call(
        body, name=name, grid=(S // tm,),
        in_specs=[pl.BlockSpec((tm, wcols), lambda i: (i, 0)),
                  pl.BlockSpec((SUBLANES, wcols), lambda i: (jnp.maximum(i * hb - 1, 0), 0))]
                 + _mix_specs(tm, S, "fwd"),
        out_specs=[pl.BlockSpec((tm, 3 * GROUP_W), lambda i: (i, 0)),
                   pl.BlockSpec((tm, GROUP_W), lambda i: (i, 0))],
        out_shape=[jax.ShapeDtypeStruct((S, 3 * GROUP_W), MXU_DTYPE), jax.ShapeDtypeStruct((S, GROUP_W), F32)],
        scratch_shapes=[pltpu.VMEM((SUBLANES, GROUP_W), F32), _lane_scratch(tm, GROUP_W), _lane_scratch(tm, GROUP_W),
                        pltpu.VMEM((hb, GROUP_W), F32)],
        compiler_params=_params(("arbitrary",)),
    )(z, z, mp["wA"], mp["wR"], mp["vec"], mp["wa"], mp["wx"], mp["ws"], mp["bs"])


_NEG = -1e30


def _slope(h):
    return 2.0 ** (-8.0 * (h + 1) / N_HEADS)


def _attn_bias(dil, offsets, n_keys):
    shape = (ATTN_BLOCK, n_keys)
    qi = lax.broadcasted_iota(jnp.int32, shape, 0)
    ki = lax.broadcasted_iota(jnp.int32, shape, 1)
    blocks = []
    for f in offsets:
        delta = qi + f - ki
        valid = (delta >= 0) & (delta <= ATTN_BLOCK)
        dist = (delta * dil).astype(F32)
        for h in range(N_HEADS):
            blocks.append(jnp.where(valid, -_slope(h) * dist, _NEG))
    return jnp.concatenate(blocks, axis=0)


def _stack_heads(t, masks):
    return jnp.concatenate([jnp.where(m, t, jnp.zeros_like(t)) for m in masks], axis=0)


def _unstack_heads(t4, masks, base=0):
    out = t4[base * ATTN_BLOCK:(base + 1) * ATTN_BLOCK]
    for h in range(1, N_HEADS):
        out = jnp.where(masks[h], t4[(base + h) * ATTN_BLOCK:(base + h + 1) * ATTN_BLOCK], out)
    return out


def _attn_fwd(qkv, dil, name):
    rows = qkv.shape[0]
    nb = rows // ATTN_BLOCK
    scale = 1.0 / math.sqrt(HEAD_DIM)
    B = ATTN_BLOCK
    per_step = ATTN_BLOCKS_PER_STEP

    def body(q_ref, kc_ref, kp_ref, vc_ref, vp_ref, o_ref, l_ref, bias_ref):
        n = pl.program_id(1)

        @pl.when(n == 0)
        def _():
            bias_ref[...] = _attn_bias(dil, (B,), 2 * B)

        masks = _head_masks((B, GROUP_W))
        for j in range(per_step):
            own = slice(j * B, (j + 1) * B)
            before = slice((j - 1) * B, j * B)
            qs = _stack_heads(q_ref[own], masks)
            keys = jnp.concatenate([kp_ref[...] if j == 0 else kc_ref[before], kc_ref[own]], axis=0)
            vals = jnp.concatenate([vp_ref[...] if j == 0 else vc_ref[before], vc_ref[own]], axis=0)
            s = _mm_nt(qs, keys) * scale + bias_ref[...]
            if j == 0:
                key_col = lax.broadcasted_iota(jnp.int32, s.shape, 1)
                s = jnp.where((n == 0) & (key_col < B), _NEG, s)
            m = jnp.max(s, axis=-1, keepdims=True)
            p = jnp.exp(s - m)
            l = jnp.sum(p, axis=-1, keepdims=True)
            o4 = jnp.dot(p.astype(MXU_DTYPE), vals, preferred_element_type=F32)
            o_ref[own] = _unstack_heads(o4, masks) / _unstack_heads(jnp.broadcast_to(l, o4.shape), masks)
            l_ref[own] = _unstack_heads(jnp.broadcast_to(m + jnp.log(l), o4.shape), masks)

    blk = (per_step * B, GROUP_W)
    cur = lambda c: pl.BlockSpec(blk, lambda r, n: (n, r * 3 + c))
    prev = lambda c: pl.BlockSpec((B, GROUP_W), lambda r, n: (jnp.maximum(n * per_step - 1, 0), r * 3 + c))
    out = pl.BlockSpec(blk, lambda r, n: (n, r))
    return pl.pallas_call(
        body, name=name, grid=(dil, nb // per_step),
        in_specs=[cur(0), cur(1), prev(1), cur(2), prev(2)],
        out_specs=[out, out],
        out_shape=[jax.ShapeDtypeStruct((rows, dil * GROUP_W), F32)] * 2,
        scratch_shapes=[pltpu.VMEM((N_HEADS * ATTN_BLOCK, 2 * ATTN_BLOCK), F32)],
        compiler_params=_params(("parallel", "arbitrary")),
    )(qkv, qkv, qkv, qkv, qkv)


def _outproj(x, z_g, y_abc, attn, w_out, name):
    S, D = x.shape
    tm = TM_MM
    n_abc = 3 * GROUP_W

    def body(x_ref, g_ref, yabc_ref, o1, l1, o2, l2, o3, l3, w_ref,
             xn_ref, y_ref, o_ref, lse1_ref, lse4_ref, lse16_ref, so2, sl2, so3, sl3, slse):
        for src, dst, dil in ((o2, so2, ATTN_DILATIONS[1]), (l2, sl2, ATTN_DILATIONS[1]),
                              (o3, so3, ATTN_DILATIONS[2]), (l3, sl3, ATTN_DILATIONS[2])):
            _interleave(src, dst, dil)
        la, lb, lc = l1[...], _get(sl2), _get(sl3)
        mx = jnp.maximum(jnp.maximum(la, lb), lc)
        ea, eb, ec = jnp.exp(la - mx), jnp.exp(lb - mx), jnp.exp(lc - mx)
        den = ea + eb + ec
        o = (ea * o1[...] + eb * _get(so2) + ec * _get(so3)) / den
        o_ref[...] = o
        _put(slse, mx + jnp.log(den))
        for dil, ref in zip(ATTN_DILATIONS, (lse1_ref, lse4_ref, lse16_ref)):
            _deinterleave(slse, ref, dil)
        y_d = o * _silu_and_grad(g_ref[...])[0]
        y_ref[:, 0:n_abc] = yabc_ref[...].astype(MXU_DTYPE)
        y_ref[:, n_abc:] = y_d.astype(MXU_DTYPE)
        xn_ref[...] = x_ref[...] + jnp.dot(y_ref[...], w_ref[...], preferred_element_type=F32)

    row = lambda w: pl.BlockSpec((tm, w), lambda i: (i, 0))
    dil_specs = [_dilated_spec(tm, GROUP_W, dil) for dil in ATTN_DILATIONS]
    (o1, l1), (o2, l2), (o3, l3) = attn
    return pl.pallas_call(
        body, name=name, grid=(S // tm,),
        in_specs=[row(D), row(GROUP_W), row(n_abc)] + [sp for sp in dil_specs for _ in range(2)]
                 + [pl.BlockSpec(w_out.shape, lambda i: (0, 0))],
        out_specs=[row(D), row(4 * GROUP_W), row(GROUP_W)] + dil_specs,
        out_shape=[jax.ShapeDtypeStruct((S, D), F32), jax.ShapeDtypeStruct((S, 4 * GROUP_W), MXU_DTYPE),
                   jax.ShapeDtypeStruct((S, GROUP_W), F32)]
                  + [_dilated_shape(S, GROUP_W, dil, F32) for dil in ATTN_DILATIONS],
        scratch_shapes=[_lane_scratch(tm, GROUP_W)] * 5,
        compiler_params=_params(("parallel",)),
    )(x, z_g, y_abc, o1, l1, o2, l2, o3, l3, w_out)


def _loss_head(x, g, target, name):
    S, D = x.shape
    tm = TM_MM

    def body(x_ref, g_ref, t_ref, dx_ref, loss_ref, dg_ref):
        i = pl.program_id(0)

        @pl.when(i == 0)
        def _():
            loss_ref[...] = jnp.zeros_like(loss_ref)
            dg_ref[...] = jnp.zeros_like(dg_ref)

        xv = x_ref[...]
        r = lax.rsqrt(jnp.mean(xv * xv, axis=-1, keepdims=True) + NORM_EPS)
        xn = xv * r
        err = xn * g_ref[...] - t_ref[...]
        per_tok = jnp.mean(err * err, axis=-1, keepdims=True)
        loss_ref[...] += 0.5 * jnp.sum(per_tok, axis=0, keepdims=True)
        dout = err * (1.0 / D)
        dg_ref[...] += _colsum(dout * xn)
        dxn = dout * g_ref[...]
        dx_ref[...] = r * (dxn - xn * jnp.mean(dxn * xn, axis=-1, keepdims=True))

    row = pl.BlockSpec((tm, D), lambda i: (i, 0))
    return pl.pallas_call(
        body, name=name, grid=(S // tm,),
        in_specs=[row, pl.BlockSpec((1, D), lambda i: (0, 0)), row],
        out_specs=[row, pl.BlockSpec((1, LANES), lambda i: (0, 0)), pl.BlockSpec((1, D), lambda i: (0, 0))],
        out_shape=[jax.ShapeDtypeStruct((S, D), F32), jax.ShapeDtypeStruct((1, LANES), F32),
                   jax.ShapeDtypeStruct((1, D), F32)],
        compiler_params=_params(("arbitrary",)),
    )(x, g, target)


def _outproj_bwd(dx, y, w_out, name):
    S, D = dx.shape
    E = y.shape[1]
    tm = TM_MM

    def body(dx_ref, y_ref, w_ref, dy_ref, dw_ref, acc_ref):
        i = pl.program_id(0)

        @pl.when(i == 0)
        def _():
            acc_ref[...] = jnp.zeros_like(acc_ref)

        dxb = dx_ref[...].astype(MXU_DTYPE)
        dy_ref[...] = _mm_nt(dxb, w_ref[...])
        acc_ref[...] += _mm_tn(y_ref[...], dxb)

        @pl.when(i == S // tm - 1)
        def _():
            dw_ref[...] = acc_ref[...].astype(dw_ref.dtype)

    return pl.pallas_call(
        body, name=name, grid=(S // tm,),
        in_specs=[pl.BlockSpec((tm, D), lambda i: (i, 0)), pl.BlockSpec((tm, E), lambda i: (i, 0)),
                  pl.BlockSpec((E, D), lambda i: (0, 0))],
        out_specs=[pl.BlockSpec((tm, E), lambda i: (i, 0)), pl.BlockSpec((E, D), lambda i: (0, 0))],
        out_shape=[jax.ShapeDtypeStruct((S, E), F32), jax.ShapeDtypeStruct((E, D), WIRE_DTYPE)],
        scratch_shapes=[pltpu.VMEM((E, D), F32)],
        compiler_params=_params(("arbitrary",)),
    )(dx, y, w_out)


def _mix_bwd(z, z_g, dy, hs, o, mp, name):
    S = z.shape[0]
    tm = TM_MIX
    hb = tm // SUBLANES
    nT = S // tm
    last_blk = S // SUBLANES - 1
    wcols = N_ABC * GROUP_W

    def body(z_ref, zh_ref, zn_ref, zg_ref, dy_ref, dyn_ref, h_ref, hh_ref, o_ref,
             wA_ref, wR_ref, vec_ref, wa_ref, wx_ref, ws_ref, bs_ref,
             dz_ref, dzg_ref, do1_ref, do4_ref, do16_ref, dl1_ref, dl4_ref, dl16_ref,
             dwA_ref, dwR_ref, dvec_ref, dwa_ref, dwx_ref, dws_ref, dbs_ref,
             hcarry_ref, xcarry_ref, bsacc_ref, do_ref, dl_ref, sa_ref, sb_ref, sc_ref):
        i = pl.program_id(0)
        ti = nT - 1 - i

        @pl.when(i == 0)
        def _():
            hcarry_ref[...] = jnp.zeros_like(hcarry_ref)
            xcarry_ref[...] = jnp.zeros_like(xcarry_ref)
            bsacc_ref[...] = jnp.zeros_like(bsacc_ref)
            dwA_ref[...] = jnp.zeros_like(dwA_ref)
            dwR_ref[...] = jnp.zeros_like(dwR_ref)
            dvec_ref[...] = jnp.zeros_like(dvec_ref)
            dwa_ref[...] = jnp.zeros_like(dwa_ref)
            dwx_ref[...] = jnp.zeros_like(dwx_ref)
            dws_ref[...] = jnp.zeros_like(dws_ref)
            dbs_ref[...] = jnp.zeros_like(dbs_ref)

        has_prev = ti > 0
        has_next = i > 0
        col = lambda c: slice(c * GROUP_W, (c + 1) * GROUP_W)
        z_of = lambda c: z_ref[:, col(c)]
        halo_of = lambda c: jnp.where(has_prev, zh_ref[:, col(c)], 0.0)
        next_of = lambda c: zn_ref[:, col(c)]

        p, p_h, cv = _conv_a(z_of, halo_of, wA_ref)
        sg, dsg = _silu_and_grad(z_of(3))
        a_b = z_of(1)
        dya = dy_ref[:, col(0)]
        dcv = dya * a_b * sg
        dcv_n = jnp.where(has_next, dyn_ref[...] * next_of(1) * _silu_and_grad(next_of(3))[0], 0.0)
        dp = (wA_ref[2:3, :] * dcv + wA_ref[1:2, :] * _shift_up(dcv, dcv_n, 1)
              + wA_ref[0:1, :] * _shift_up(dcv, dcv_n, 2))
        dwA_ref[2:3, :] += _colsum(dcv * p)
        dwA_ref[1:2, :] += _colsum(dcv * _shift_down(p, p_h, 1))
        dwA_ref[0:1, :] += _colsum(dcv * _shift_down(p, p_h, 2))
        def put_dz(c, val):
            dz_ref[:, col(c)] = val.astype(dz_ref.dtype)

        put_dz(0, dp * z_of(2))
        put_dz(1, dya * cv * sg)
        put_dz(2, dp * z_of(0))
        put_dz(3, dya * a_b * cv * dsg)

        xc, sh, ga, gi, a, mult, sp = _lru_gates(z_of, halo_of, wR_ref, vec_ref, wa_ref, wx_ref)
        h = h_ref[...]
        h_prev = _shift_down(h, jnp.where(has_prev, hh_ref[...], 0.0), 1)
        sgr, dsgr = _silu_and_grad(z_of(5))
        dyb = dy_ref[:, col(1)]
        put_dz(5, dyb * h * dsgr)
        row = lax.broadcasted_iota(jnp.int32, (tm, GROUP_W), 0)
        g_in = dyb * sgr + jnp.where(row == tm - 1, hcarry_ref[0:1, :], 0.0)
        a_up = _shift_up(a, jnp.zeros((SUBLANES, GROUP_W), F32), 1)
        dH = _scan_rev_tile(a_up, g_in, sa_ref, sb_ref, sc_ref)
        hcarry_ref[...] = (a * dH)[0:SUBLANES]
        da = dH * h_prev
        gx = gi * xc
        dmult = dH * gx
        dgi = dH * mult * xc
        dxc = dH * mult * gi
        dlog_a = da * a - dmult * (a * a) / mult
        dga = dlog_a * (-RG_C * sp)
        dlam_row = _colsum(dlog_a * (-RG_C * ga)) * (-_sigmoid(-vec_ref[3:4, :]))
        dpre_a = dga * ga * (1.0 - ga)
        dpre_i = dgi * gi * (1.0 - gi)
        dwa_ref[...] += _mm_tn(xc, dpre_a)
        dwx_ref[...] += _mm_tn(xc, dpre_i)
        dxc = dxc + _mm_nt(dpre_a, wa_ref[...]) + _mm_nt(dpre_i, wx_ref[...])
        dvec_ref[0:1, :] += _colsum(dxc)
        dvec_ref[1:2, :] += _colsum(dpre_a)
        dvec_ref[2:3, :] += _colsum(dpre_i)
        dvec_ref[3:4, :] += dlam_row
        for k in range(4):
            dwR_ref[k:k + 1, :] += _colsum(dxc * sh[3 - k])
        dxc_n = xcarry_ref[...]
        put_dz(4, wR_ref[3:4, :] * dxc + wR_ref[2:3, :] * _shift_up(dxc, dxc_n, 1)
               + wR_ref[1:2, :] * _shift_up(dxc, dxc_n, 2) + wR_ref[0:1, :] * _shift_up(dxc, dxc_n, 3))
        xcarry_ref[...] = dxc[0:SUBLANES]

        c_u, c_v = z_of(6), z_of(7)
        u, du_dx = _gelu_and_grad(c_u)
        gv, dgv_dx = _gelu_and_grad(c_v)
        rr = lax.rsqrt(jnp.mean(gv * gv, axis=-1, keepdims=True) + NORM_EPS)
        xhat = gv * rr
        g_c = vec_ref[4:5, :]
        vn = xhat * g_c
        masks = _head_masks((GMLP_CHUNK, GROUP_W))
        tri_r = lax.broadcasted_iota(jnp.int32, (GMLP_CHUNK, GMLP_CHUNK), 0)
        tri_c = lax.broadcasted_iota(jnp.int32, (GMLP_CHUNK, GMLP_CHUNK), 1)
        tril = tri_r >= tri_c
        sgc, dsgc = _silu_and_grad(z_of(8))
        dyc = dy_ref[:, col(2)]
        dsp_full = dyc * u * sgc
        sp_parts, dvn_parts = [], []
        for c in range(tm // GMLP_CHUNK):
            rs = slice(c * GMLP_CHUNK, (c + 1) * GMLP_CHUNK)
            vc = vn[rs].astype(MXU_DTYPE)
            dsp_c = dsp_full[rs]
            bsacc_ref[...] += dsp_c
            acc = bs_ref[...]
            dvn_c = jnp.zeros((GMLP_CHUNK, GROUP_W), F32)
            for h in range(N_HEADS):
                w_h = ws_ref[h]
                acc = acc + jnp.where(masks[h], jnp.dot(w_h, vc, preferred_element_type=F32), 0.0)
                dsp_h = jnp.where(masks[h], dsp_c, 0.0).astype(MXU_DTYPE)
                dvn_c = dvn_c + _mm_tn(w_h, dsp_h)
                dws_ref[h] += jnp.where(tril, _mm_nt(dsp_h, vc), 0.0)
            sp_parts.append(acc)
            dvn_parts.append(dvn_c)
        spv = jnp.concatenate(sp_parts, axis=0)
        dvn = jnp.concatenate(dvn_parts, axis=0)
        put_dz(6, dyc * spv * sgc * du_dx)
        put_dz(8, dyc * u * spv * dsgc)
        dvec_ref[4:5, :] += _colsum(dvn * xhat)
        dgvn = dvn * g_c
        dgv = rr * (dgvn - xhat * jnp.mean(dgvn * xhat, axis=-1, keepdims=True))
        put_dz(7, dgv * dgv_dx)

        sgd, dsgd = _silu_and_grad(zg_ref[...])
        dyd = dy_ref[:, col(3)]
        ov = o_ref[...]
        do = dyd * sgd
        _put(do_ref, do)
        dzg_ref[...] = (dyd * ov * dsgd).astype(dzg_ref.dtype)
        prod = do * ov
        tmasks = _head_masks((tm, GROUP_W))
        dl = jnp.zeros((tm, GROUP_W), F32)
        for h in range(N_HEADS):
            dl = jnp.where(tmasks[h], jnp.sum(jnp.where(tmasks[h], prod, 0.0), axis=-1, keepdims=True), dl)
        _put(dl_ref, dl)
        for dil, d_out, l_out in zip(ATTN_DILATIONS, (do1_ref, do4_ref, do16_ref), (dl1_ref, dl4_ref, dl16_ref)):
            _deinterleave(do_ref, d_out, dil)
            _deinterleave(dl_ref, l_out, dil)

        @pl.when(i == nT - 1)
        def _():
            acc = bsacc_ref[...]
            lane = lax.broadcasted_iota(jnp.int32, (GMLP_CHUNK, LANES), 1)
            out = jnp.zeros((GMLP_CHUNK, LANES), F32)
            for h in range(N_HEADS):
                out = jnp.where(lane == h, jnp.sum(jnp.where(masks[h], acc, 0.0), axis=-1, keepdims=True), out)
            dbs_ref[...] = out

    rev = lambda w: pl.BlockSpec((tm, w), lambda i: (nT - 1 - i, 0))
    prev8 = lambda w: pl.BlockSpec((SUBLANES, w), lambda i: (jnp.maximum((nT - 1 - i) * hb - 1, 0), 0))
    next8 = lambda w: pl.BlockSpec((SUBLANES, w), lambda i: (jnp.minimum((nT - i) * hb, last_blk), 0))
    const2 = lambda shape: pl.BlockSpec(shape, lambda i: (0, 0))
    dil_specs = [_dilated_spec(tm, GROUP_W, dil, lambda i: nT - 1 - i) for dil in ATTN_DILATIONS]
    dil_shapes = [_dilated_shape(S, GROUP_W, dil, F32) for dil in ATTN_DILATIONS]
    small = (SUBLANES, GROUP_W)
    sq = (GROUP_W, GROUP_W)
    ws_shape = (N_HEADS, GMLP_CHUNK, GMLP_CHUNK)
    return pl.pallas_call(
        body, name=name, grid=(nT,),
        in_specs=[rev(wcols), prev8(wcols), next8(wcols), rev(GROUP_W),
                  rev(4 * GROUP_W), next8(GROUP_W), rev(GROUP_W), prev8(GROUP_W), rev(GROUP_W)]
                 + _mix_specs(tm, S, "bwd"),
        out_specs=[rev(wcols), rev(GROUP_W)] + dil_specs + dil_specs
                  + [const2(small), const2(small), const2(small), const2(sq), const2(sq),
                     pl.BlockSpec(ws_shape, lambda i: (0, 0, 0)), const2((GMLP_CHUNK, LANES))],
        out_shape=[jax.ShapeDtypeStruct((S, wcols), MXU_DTYPE), jax.ShapeDtypeStruct((S, GROUP_W), MXU_DTYPE)]
                  + dil_shapes + dil_shapes
                  + [jax.ShapeDtypeStruct(small, F32)] * 3 + [jax.ShapeDtypeStruct(sq, F32)] * 2
                  + [jax.ShapeDtypeStruct(ws_shape, F32), jax.ShapeDtypeStruct((GMLP_CHUNK, LANES), F32)],
        scratch_shapes=[pltpu.VMEM(small, F32), pltpu.VMEM(small, F32), pltpu.VMEM((GMLP_CHUNK, GROUP_W), F32),
                        _lane_scratch(tm, GROUP_W), _lane_scratch(tm, GROUP_W),
                        _lane_scratch(tm, GROUP_W), _lane_scratch(tm, GROUP_W), pltpu.VMEM((hb, GROUP_W), F32)],
        compiler_params=_params(("arbitrary",)),
    )(z, z, z, z_g, dy, dy, hs, hs, o, mp["wA"], mp["wR"], mp["vec"], mp["wa"], mp["wx"], mp["ws"], mp["bs"])


def _attn_bwd(qkv, do, lse, delta, dil, name):
    rows = qkv.shape[0]
    nb = rows // ATTN_BLOCK
    scale = 1.0 / math.sqrt(HEAD_DIM)
    B = ATTN_BLOCK
    per_step = ATTN_BLOCKS_PER_STEP
    n_steps = nb // per_step

    def body(qc_ref, qn_ref, k_ref, v_ref, doc_ref, don_ref, lc_ref, ln_ref, dc_ref, dn_ref,
             dq_ref, dk_ref, dv_ref, carry_ref, bias_ref):
        n = pl.program_id(1)

        @pl.when(n == 0)
        def _():
            carry_ref[...] = jnp.zeros_like(carry_ref)
            bias_ref[...] = _attn_bias(dil, (0, B), B)

        masks = _head_masks((B, GROUP_W))

        def per_row(tiles):
            return jnp.concatenate([jnp.max(jnp.where(masks[h], t, _NEG), axis=-1, keepdims=True)
                                    for t in tiles for h in range(N_HEADS)], axis=0)

        dq_acc = carry_ref[...]
        for j in range(per_step):
            own = slice(j * B, (j + 1) * B)
            after = slice((j + 1) * B, (j + 2) * B)
            last = j == per_step - 1
            nxt = lambda cur_ref, nxt_ref: nxt_ref[...] if last else cur_ref[after]
            kb = k_ref[own]
            vb = v_ref[own]
            qs = jnp.concatenate([_stack_heads(qc_ref[own], masks), _stack_heads(nxt(qc_ref, qn_ref), masks)], axis=0)
            dos = jnp.concatenate([_stack_heads(doc_ref[own].astype(MXU_DTYPE), masks),
                                   _stack_heads(nxt(doc_ref, don_ref).astype(MXU_DTYPE), masks)], axis=0)
            lse_rows = per_row([lc_ref[own], nxt(lc_ref, ln_ref)])
            dl_rows = per_row([dc_ref[own], nxt(dc_ref, dn_ref)])
            s = _mm_nt(qs, kb) * scale + bias_ref[...]
            if last:
                row = lax.broadcasted_iota(jnp.int32, s.shape, 0)
                s = jnp.where((n == n_steps - 1) & (row >= N_HEADS * B), _NEG, s)
            p = jnp.exp(s - lse_rows)
            dp = _mm_nt(dos, vb)
            ds = (p * (dp - dl_rows) * scale).astype(MXU_DTYPE)
            dv_ref[own] = _mm_tn(p.astype(MXU_DTYPE), dos)
            dk_ref[own] = _mm_tn(ds, qs)
            dq4 = jnp.dot(ds, kb, preferred_element_type=F32)
            dq_ref[own] = dq_acc + _unstack_heads(dq4, masks)
            dq_acc = _unstack_heads(dq4, masks, N_HEADS)
        carry_ref[...] = dq_acc

    blk = (per_step * B, GROUP_W)
    one = (B, GROUP_W)
    nxt_idx = lambda n: jnp.minimum((n + 1) * per_step, nb - 1)
    zcur = lambda c: pl.BlockSpec(blk, lambda r, n: (n, r * 3 + c))
    znext = lambda c: pl.BlockSpec(one, lambda r, n: (nxt_idx(n), r * 3 + c))
    cur = pl.BlockSpec(blk, lambda r, n: (n, r))
    nxt = pl.BlockSpec(one, lambda r, n: (nxt_idx(n), r))
    return pl.pallas_call(
        body, name=name, grid=(dil, n_steps),
        in_specs=[zcur(0), znext(0), zcur(1), zcur(2), cur, nxt, cur, nxt, cur, nxt],
        out_specs=[cur, cur, cur],
        out_shape=[jax.ShapeDtypeStruct((rows, dil * GROUP_W), F32)] * 3,
        scratch_shapes=[pltpu.VMEM(one, F32), pltpu.VMEM((2 * N_HEADS * B, B), F32)],
        compiler_params=_params(("parallel", "arbitrary")),
    )(qkv, qkv, qkv, qkv, do, do, lse, lse, delta, delta)


def _inproj_bwd(x, g, dxn, dz_abc, dqkv, dz_g, w_in, name):
    S, D = x.shape
    N = w_in.shape[1]
    tm = TM_MM
    n_abc = N_ABC * GROUP_W

    def body(x_ref, g_ref, dxn_ref, dabc_ref, q1, k1, v1, q2, k2, v2, q3, k3, v3, dg_ref, w_ref,
             dx_ref, dz_ref, h_ref, dgn_ref, s4_ref, s16_ref):
        i = pl.program_id(0)

        @pl.when(i == 0)
        def _():
            dgn_ref[...] = jnp.zeros_like(dgn_ref)

        dz_ref[:, 0:n_abc] = dabc_ref[...].astype(MXU_DTYPE)
        for j, parts in enumerate(((q1, q2, q3), (k1, k2, k3), (v1, v2, v3))):
            c0 = n_abc + j * GROUP_W
            _interleave(parts[1], s4_ref, ATTN_DILATIONS[1])
            _interleave(parts[2], s16_ref, ATTN_DILATIONS[2])
            dz_ref[:, c0:c0 + GROUP_W] = (parts[0][...] + _get(s4_ref) + _get(s16_ref)).astype(MXU_DTYPE)
        dz_ref[:, n_abc + 3 * GROUP_W:] = dg_ref[...].astype(MXU_DTYPE)
        dh = _mm_nt(dz_ref[...], w_ref[...])
        xv = x_ref[...]
        r = lax.rsqrt(jnp.mean(xv * xv, axis=-1, keepdims=True) + NORM_EPS)
        xn = xv * r
        gv = g_ref[...]
        h_ref[...] = (xn * gv).astype(MXU_DTYPE)
        dgn_ref[...] += _colsum(dh * xn)
        dn = dh * gv
        dx_ref[...] = dxn_ref[...] + r * (dn - xn * jnp.mean(dn * xn, axis=-1, keepdims=True))

    row = lambda w: pl.BlockSpec((tm, w), lambda i: (i, 0))
    flat = [t for p in dqkv for t in p]
    dil_specs = [_dilated_spec(tm, GROUP_W, dil) for dil in ATTN_DILATIONS for _ in range(3)]
    return pl.pallas_call(
        body, name=name, grid=(S // tm,),
        in_specs=[row(D), pl.BlockSpec((1, D), lambda i: (0, 0)), row(D), row(n_abc)] + dil_specs
                 + [row(GROUP_W), pl.BlockSpec((D, N), lambda i: (0, 0))],
        out_specs=[row(D), row(N), row(D), pl.BlockSpec((1, D), lambda i: (0, 0))],
        out_shape=[jax.ShapeDtypeStruct((S, D), F32), jax.ShapeDtypeStruct((S, N), MXU_DTYPE),
                   jax.ShapeDtypeStruct((S, D), MXU_DTYPE), jax.ShapeDtypeStruct((1, D), F32)],
        scratch_shapes=[_lane_scratch(tm, GROUP_W)] * 2,
        compiler_params=_params(("arbitrary",)),
    )(x, g, dxn, dz_abc, *flat, dz_g, w_in)


def _inproj_wgrad(h, dz, name):
    S, D = h.shape
    N = dz.shape[1]
    tm = TM_MM
    nj = 2
    cw = N // nj
    per = N_DEV // nj
    n_loc = N // N_DEV

    def body(h_ref, dz_ref, dw_ref, acc_ref):
        i = pl.program_id(1)

        @pl.when(i == 0)
        def _():
            acc_ref[...] = jnp.zeros_like(acc_ref)

        acc_ref[...] += _mm_tn(h_ref[...], dz_ref[...])

        @pl.when(i == S // tm - 1)
        def _():
            for b in range(per):
                dw_ref[b] = acc_ref[:, b * n_loc:(b + 1) * n_loc].astype(dw_ref.dtype)

    return pl.pallas_call(
        body, name=name, grid=(nj, S // tm),
        in_specs=[pl.BlockSpec((tm, D), lambda j, i: (i, 0)), pl.BlockSpec((tm, cw), lambda j, i: (i, j))],
        out_specs=pl.BlockSpec((per, D, n_loc), lambda j, i: (j, 0, 0)),
        out_shape=jax.ShapeDtypeStruct((N_DEV, D, n_loc), WIRE_DTYPE),
        scratch_shapes=[pltpu.VMEM((D, cw), F32)],
        compiler_params=_params(("parallel", "arbitrary")),
    )(h, dz)


def _my_place():
    return lax.axis_index("x"), lax.axis_index("y"), lax.axis_index("c")


def _peer(x, y, c, k):
    px = 1 - x if k & 4 else x
    py = 1 - y if k & 2 else y
    pc = 1 - c if k & 1 else c
    return (px, py, pc), 4 * px + 2 * py + pc


HBM_SPEC = pl.BlockSpec(memory_space=pltpu.HBM)
SEM_SPEC = pl.BlockSpec(memory_space=pltpu.SEMAPHORE)
SPLIT_EFFECT = pltpu.SideEffectType.DATAFLOW_SIDE_EFFECTING
N_PEERS = N_DEV - 1


def _exchange_copies(srcs, lands, send_sems, recv_sems, whole, arrival):
    x, y, c = _my_place()
    me = 4 * x + 2 * y + c
    copies = []
    for t in range(len(srcs)):
        for k in range(1, N_DEV):
            peer, pidx = _peer(x, y, c, k)
            copies.append(pltpu.make_async_remote_copy(
                src_ref=srcs[t] if whole[t] else srcs[t].at[pidx],
                dst_ref=lands[t].at[pidx if arrival else me], send_sem=send_sems.at[t * N_PEERS + k - 1],
                recv_sem=recv_sems.at[t * N_PEERS + k - 1], device_id=peer, device_id_type=MESH))
    return copies


def _exchange_start(groups, name):
    sizes = [len(g) for g in groups]
    whole = [w for g in groups for _, w in g]
    srcs = [pltpu.with_memory_space_constraint(a, pltpu.HBM) for g in groups for a, _ in g]
    lands = [pltpu.with_memory_space_constraint(lax.empty(((N_DEV,) + a.shape) if w else a.shape, a.dtype), pltpu.HBM)
             for a, w in zip(srcs, whole)]
    n = len(srcs)
    n_g = len(groups)

    def body(*refs):
        src_refs, land_refs = refs[:n], refs[n:2 * n]
        sem_refs = refs[4 * n:4 * n + 2 * n_g]
        token = refs[-1]
        off = 0
        for gi, sz in enumerate(sizes):
            for send in _exchange_copies(src_refs[off:off + sz], land_refs[off:off + sz],
                                         sem_refs[2 * gi], sem_refs[2 * gi + 1], whole[off:off + sz], False):
                send.start()
            off += sz
        token[...] = jnp.zeros_like(token)

    sem_shapes = [pltpu.SemaphoreType.DMA((sz * N_PEERS,)) for sz in sizes for _ in range(2)]
    outs = pl.pallas_call(
        body, name=name,
        in_specs=[HBM_SPEC] * (2 * n),
        out_specs=[HBM_SPEC] * (2 * n) + [SEM_SPEC] * (2 * n_g) + [pl.BlockSpec(memory_space=pltpu.VMEM)],
        out_shape=[pltpu.HBM(a.shape, a.dtype) for a in srcs + lands] + sem_shapes
                  + [jax.ShapeDtypeStruct((SUBLANES, LANES), F32)],
        input_output_aliases={i: i for i in range(2 * n)},
        compiler_params=pltpu.CompilerParams(has_side_effects=SPLIT_EFFECT),
    )(*srcs, *lands)
    handles, off = [], 0
    for gi, sz in enumerate(sizes):
        handles.append((outs[2 * n + 2 * gi], outs[2 * n + 2 * gi + 1], outs[off:off + sz], outs[n + off:n + off + sz],
                        whole[off:off + sz]))
        off += sz
    return handles, outs[-1]


def _exchange_wait(handle, after, name):
    send_sems, recv_sems, srcs, lands, whole = handle
    n = len(srcs)

    def body(*refs):
        src_refs, land_refs = refs[:n], refs[n:2 * n]
        for send in _exchange_copies(src_refs, land_refs, refs[2 * n], refs[2 * n + 1], whole, False):
            send.wait_send()
        for arrival in _exchange_copies(src_refs, land_refs, refs[2 * n], refs[2 * n + 1], whole, True):
            arrival.wait_recv()

    outs = pl.pallas_call(
        body, name=name,
        in_specs=[HBM_SPEC] * (2 * n) + [SEM_SPEC, SEM_SPEC, pl.BlockSpec(memory_space=pl.ANY)],
        out_specs=[HBM_SPEC] * (2 * n),
        out_shape=[pltpu.HBM(a.shape, a.dtype) for a in list(srcs) + list(lands)],
        input_output_aliases={i: i for i in range(2 * n)},
        compiler_params=pltpu.CompilerParams(has_side_effects=SPLIT_EFFECT),
    )(*srcs, *lands, send_sems, recv_sems, after)
    x, y, c = _my_place()
    me = 4 * x + 2 * y + c
    own = [s[None] if w else lax.dynamic_slice_in_dim(s, me, 1, axis=0) for s, w in zip(outs[:n], whole)]
    return [lax.dynamic_update_slice_in_dim(ld, o, me, axis=0) for ld, o in zip(outs[n:], own)]


def _sum_slots(parts, name):
    n = len(parts)

    def body(*refs):
        for p_ref, o_ref in zip(refs[:n], refs[n:]):
            acc = p_ref[0]
            for j in range(1, N_DEV):
                acc = acc + p_ref[j]
            o_ref[...] = acc

    vm = pl.BlockSpec(memory_space=pltpu.VMEM)
    return pl.pallas_call(
        body, name=name, in_specs=[vm] * n, out_specs=[vm] * n,
        out_shape=[jax.ShapeDtypeStruct(p.shape[1:], F32) for p in parts],
        compiler_params=pltpu.CompilerParams(vmem_limit_bytes=VMEM_LIMIT),
    )(*parts)


def _adamw_math(w, g, m, v):
    m = ADAM_B1 * m + (1.0 - ADAM_B1) * g
    v = ADAM_B2 * v + (1.0 - ADAM_B2) * (g * g)
    m_hat = m / (1.0 - ADAM_B1 ** ADAM_STEP)
    v_hat = v / (1.0 - ADAM_B2 ** ADAM_STEP)
    delta = -ADAM_LR * (m_hat / (jnp.sqrt(v_hat) + ADAM_EPS) + ADAM_WD * w)
    return delta, m, v


def _adamw_summed(parts, w, m, v, tr, name):
    R, C = w.shape

    def body(p_ref, w_ref, m_ref, v_ref, g_ref, d_ref, nm_ref, nv_ref):
        g = p_ref[0].astype(F32)
        for j in range(1, N_DEV):
            g = g + p_ref[j].astype(F32)
        g_ref[...] = g
        d_ref[...], nm_ref[...], nv_ref[...] = _adamw_math(w_ref[...], g, m_ref[...], v_ref[...])

    row = pl.BlockSpec((tr, C), lambda i: (i, 0))
    return pl.pallas_call(
        body, name=name, grid=(R // tr,),
        in_specs=[pl.BlockSpec((N_DEV, tr, C), lambda i: (0, i, 0)), row, row, row],
        out_specs=[row] * 4, out_shape=[jax.ShapeDtypeStruct((R, C), F32)] * 4,
        compiler_params=_params(("parallel",)),
    )(parts, w, m, v)


def _adamw_small(w, g, m, v, name):
    def body(w_ref, g_ref, m_ref, v_ref, d_ref, nm_ref, nv_ref):
        d_ref[...], nm_ref[...], nv_ref[...] = _adamw_math(w_ref[...], g_ref[...], m_ref[...], v_ref[...])

    vm = pl.BlockSpec(memory_space=pltpu.VMEM)
    return pl.pallas_call(
        body, name=name, in_specs=[vm] * 4, out_specs=[vm] * 3,
        out_shape=[jax.ShapeDtypeStruct(w.shape, F32)] * 3,
        compiler_params=pltpu.CompilerParams(vmem_limit_bytes=VMEM_LIMIT),
    )(w, g, m, v)


def _pack(arrays):
    flat = jnp.concatenate([a.reshape(-1) for a in arrays])
    pad = (-flat.shape[0]) % (SUBLANES * LANES)
    return jnp.pad(flat, (0, pad)).reshape(-1, LANES)


def _unpack(buf, like):
    flat = buf.reshape(-1)
    out, off = [], 0
    for a in like:
        out.append(flat[off:off + a.size].reshape(a.shape))
        off += a.size
    return out


def _block_diag(w):
    eye = jnp.eye(N_HEADS, dtype=w.dtype)
    return jnp.einsum('hij,hk->hikj', w, eye).reshape(GROUP_W, GROUP_W)


def _diag_blocks(w):
    return jnp.einsum('hihj->hij', w.reshape(N_HEADS, HEAD_DIM, N_HEADS, HEAD_DIM))


def _pad_rows(a):
    return jnp.pad(a, ((0, SUBLANES - a.shape[0]), (0, 0)))


def _mixer_params(l, conv_a_w, conv_r_w, conv_r_b, lru_wa, lru_ba, lru_wx, lru_bx, lru_lambda, gmlp_norm_g,
                  gmlp_ws, gmlp_bs):
    tril = jnp.tril(jnp.ones((GMLP_CHUNK, GMLP_CHUNK), dtype=bool))
    vec = jnp.stack([conv_r_b[l], lru_ba[l], lru_bx[l], lru_lambda[l], gmlp_norm_g[l]])
    return {
        "wA": _pad_rows(conv_a_w[l]), "wR": _pad_rows(conv_r_w[l]), "vec": _pad_rows(vec),
        "wa": _block_diag(lru_wa[l]).astype(MXU_DTYPE), "wx": _block_diag(lru_wx[l]).astype(MXU_DTYPE),
        "ws": jnp.where(tril[None], gmlp_ws[l], 0.0).astype(MXU_DTYPE),
        "bs": jnp.repeat(jnp.transpose(gmlp_bs[l]), HEAD_DIM, axis=1),
    }


MIXER_NAMES = ("conv_a_w", "conv_r_w", "conv_r_b", "lru_wa", "lru_ba", "lru_wx", "lru_bx", "lru_lambda",
               "gmlp_norm_g", "gmlp_ws", "gmlp_bs")
SMALL_NAMES = ("norm_g",) + MIXER_NAMES + ("final_g",)


def _local_step(x, loss_target, norm_g, get_w_in, get_w_out, emit_early, emit_late, conv_a_w, conv_r_w, conv_r_b,
                lru_wa, lru_ba, lru_wx, lru_bx, lru_lambda, gmlp_norm_g, gmlp_ws, gmlp_bs, final_g):
    depth = norm_g.shape[0]
    D = x.shape[1]
    small = (conv_a_w, conv_r_w, conv_r_b, lru_wa, lru_ba, lru_wx, lru_bx, lru_lambda, gmlp_norm_g, gmlp_ws, gmlp_bs)
    saved = []
    for l in range(depth):
        mp = _mixer_params(l, *small)
        w_in_l = get_w_in(l, x)
        z, z_g, *qkv = _norm_inproj(x, norm_g[l].reshape(1, D), w_in_l, f"norm_inproj_{l}")
        y_abc, hs = _mix_fwd(z, mp, f"mix_fwd_{l}")
        attn = [_attn_fwd(qkv[p], dil, f"attn_fwd_d{dil}_{l}") for p, dil in enumerate(ATTN_DILATIONS)]
        w_out_l = get_w_out(l, y_abc)
        x_new, y, o, *lse = _outproj(x, z_g, y_abc, attn, w_out_l, f"outproj_{l}")
        saved.append((x, z, z_g, qkv, hs, y, o, lse, mp, w_in_l, w_out_l))
        x = x_new
    dx, loss, d_final_g = _loss_head(x, final_g.reshape(1, D), loss_target, "loss_head")
    token = None
    for l in reversed(range(depth)):
        x_l, z, z_g, qkv, hs, y, o, lse, mp, w_in_l, w_out_l = saved[l]
        if token is not None:
            mp = dict(mp, vec=mp["vec"] + token[0, 0])
        dy, dw_out = _outproj_bwd(dx, y, w_out_l, f"outproj_bwd_{l}")
        (dz_abc, dz_g, do1, do4, do16, dl1, dl4, dl16, dwA, dwR, dvec, dwa, dwx, dws, dbs) = _mix_bwd(
            z, z_g, dy, hs, o, mp, f"mix_bwd_{l}")
        token = emit_early(l, dw_out, [
            dwA[:conv_a_w.shape[1]], dwR[:conv_r_w.shape[1]], dvec[0], _diag_blocks(dwa), dvec[1], _diag_blocks(dwx),
            dvec[2], dvec[3], dvec[4], dws, jnp.transpose(dbs[:, :N_HEADS])])
        g_row = norm_g[l].reshape(1, D)
        if token is not None:
            g_row = g_row + token[0, 0]
        dqkv = [_attn_bwd(qkv[p], do, lse[p], dl, dil, f"attn_bwd_d{dil}_{l}")
                for p, (dil, do, dl) in enumerate(zip(ATTN_DILATIONS, (do1, do4, do16), (dl1, dl4, dl16)))]
        dx, dz, h, dng = _inproj_bwd(x_l, g_row, dx, dz_abc, dqkv, dz_g, w_in_l, f"inproj_bwd_{l}")
        dw_in = _inproj_wgrad(h, dz, f"inproj_wgrad_{l}")
        token = emit_late(l, dw_in, [dng[0]] + ([d_final_g[0]] if l == depth - 1 else []))
    return loss[0, 0], dx
WEIGHT_NAMES = ("norm_g", "w_in", "conv_a_w", "conv_r_w", "conv_r_b", "lru_wa", "lru_ba", "lru_wx", "lru_bx",
                "lru_lambda", "gmlp_norm_g", "gmlp_ws", "gmlp_bs", "w_out", "final_g")


def kernel(x, norm_g, w_in, conv_a_w, conv_r_w, conv_r_b, lru_wa, lru_ba, lru_wx, lru_bx, lru_lambda, gmlp_norm_g, gmlp_ws, gmlp_bs, w_out, final_g, loss_target, m_norm_g, m_w_in, m_conv_a_w, m_conv_r_w, m_conv_r_b, m_lru_wa, m_lru_ba, m_lru_wx, m_lru_bx, m_lru_lambda, m_gmlp_norm_g, m_gmlp_ws, m_gmlp_bs, m_w_out, m_final_g, v_norm_g, v_w_in, v_conv_a_w, v_conv_r_w, v_conv_r_b, v_lru_wa, v_lru_ba, v_lru_wx, v_lru_bx, v_lru_lambda, v_gmlp_norm_g, v_gmlp_ws, v_gmlp_bs, v_w_out, v_final_g):
    w = dict(norm_g=norm_g, w_in=w_in, conv_a_w=conv_a_w, conv_r_w=conv_r_w, conv_r_b=conv_r_b, lru_wa=lru_wa,
             lru_ba=lru_ba, lru_wx=lru_wx, lru_bx=lru_bx, lru_lambda=lru_lambda, gmlp_norm_g=gmlp_norm_g,
             gmlp_ws=gmlp_ws, gmlp_bs=gmlp_bs, w_out=w_out, final_g=final_g)
    m = dict(norm_g=m_norm_g, w_in=m_w_in, conv_a_w=m_conv_a_w, conv_r_w=m_conv_r_w, conv_r_b=m_conv_r_b,
             lru_wa=m_lru_wa, lru_ba=m_lru_ba, lru_wx=m_lru_wx, lru_bx=m_lru_bx, lru_lambda=m_lru_lambda,
             gmlp_norm_g=m_gmlp_norm_g, gmlp_ws=m_gmlp_ws, gmlp_bs=m_gmlp_bs, w_out=m_w_out, final_g=m_final_g)
    v = dict(norm_g=v_norm_g, w_in=v_w_in, conv_a_w=v_conv_a_w, conv_r_w=v_conv_r_w, conv_r_b=v_conv_r_b,
             lru_wa=v_lru_wa, lru_ba=v_lru_ba, lru_wx=v_lru_wx, lru_bx=v_lru_bx, lru_lambda=v_lru_lambda,
             gmlp_norm_g=v_gmlp_norm_g, gmlp_ws=v_gmlp_ws, gmlp_bs=v_gmlp_bs, w_out=v_w_out, final_g=v_final_g)
    depth, D, n_loc = w_in.shape
    e_loc = w_out.shape[1]
    cx, cy, cc = _my_place()
    me = 4 * cx + 2 * cy + cc

    w_in_w, w_out_w = w_in.astype(MXU_DTYPE), w_out.astype(MXU_DTYPE)
    c_loc = conv_a_w.shape[2]
    taps = (conv_a_w, conv_r_w)
    groups = [[(w_in_w[0], True), (_pack(taps), True)], [(w_out_w[0], True)]]
    groups += [[(w_in_w[l], True), (w_out_w[l], True)] for l in range(1, depth)]
    gathers, _ = _exchange_start(groups, "gather_start")
    full_in = lambda g: jnp.transpose(g, (1, 0, 2)).reshape(D, N_DEV * n_loc)
    full_out = lambda g: g.reshape(N_DEV * e_loc, D)

    g_in0, g_taps = _exchange_wait(gathers[0], x, "gather_wait_in_0")
    g_taps = g_taps.reshape(N_DEV, -1)
    conv_full, off = [], 0
    for a in taps:
        part = g_taps[:, off:off + a.size].reshape((N_DEV,) + a.shape)
        conv_full.append(jnp.transpose(part, (1, 2, 0, 3)).reshape(a.shape[:2] + (N_DEV * c_loc,)))
        off += a.size
    conv_a_full, conv_r_full = conv_full
    later = {}

    def get_w_in(l, after):
        if l == 0:
            return full_in(g_in0)
        g_in, later[l] = _exchange_wait(gathers[l + 1], after, f"gather_wait_{l}")
        return full_in(g_in)

    def get_w_out(l, after):
        if l == 0:
            return full_out(_exchange_wait(gathers[1], after, "gather_wait_out_0")[0])
        return full_out(later[l])

    early, late = {}, {}

    def emit_early(l, dw_out, mixer_grads):
        handles, token = _exchange_start(
            [[(dw_out.reshape(N_DEV, e_loc, D), False), (_pack(mixer_grads), True)]], f"early_start_{l}")
        early[l] = (handles[0], mixer_grads)
        return token

    def emit_late(l, dw_in, norm_grads):
        handles, token = _exchange_start([[(dw_in, False), (_pack(norm_grads), True)]], f"late_start_{l}")
        late[l] = (handles[0], norm_grads)
        return token

    loss, grad_x = _local_step(
        x[0], loss_target[0], norm_g, get_w_in, get_w_out, emit_early, emit_late, conv_a_full, conv_r_full, conv_r_b,
        lru_wa, lru_ba, lru_wx, lru_bx, lru_lambda, gmlp_norm_g, gmlp_ws, gmlp_bs, final_g)
    loss = lax.psum(loss, ("x", "y", "c"))

    per_layer = {"w_in": [], "w_out": []}
    small_parts = []
    for l in reversed(range(depth)):
        r_out, r_mix = _exchange_wait(early[l][0], grad_x, f"early_wait_{l}")
        r_in, r_norm = _exchange_wait(late[l][0], grad_x, f"late_wait_{l}")
        small_parts += [r_mix, r_norm]
        per_layer["w_in"].append(_adamw_summed(r_in, w_in[l], m_w_in[l], v_w_in[l], 512, f"adamw_w_in_{l}"))
        per_layer["w_out"].append(_adamw_summed(r_out, w_out[l], m_w_out[l], v_w_out[l], 128, f"adamw_w_out_{l}"))
    big = {k: [jnp.stack(parts) for parts in zip(*res[::-1])] for k, res in per_layer.items()}

    sums = _sum_slots(small_parts, "sum_small_grads")
    by_layer = {}
    for i, l in enumerate(reversed(range(depth))):
        mix = _unpack(sums[2 * i], early[l][1])
        nrm = _unpack(sums[2 * i + 1], late[l][1])
        by_layer[l] = dict(zip(MIXER_NAMES, mix), norm_g=nrm[0])
        if l == depth - 1:
            g_final = nrm[1]
    g_small = {k: jnp.stack([by_layer[l][k] for l in range(depth)]) for k in ("norm_g",) + MIXER_NAMES}
    g_small["final_g"] = g_final
    for k in ("conv_a_w", "conv_r_w"):
        g_small[k] = lax.dynamic_slice_in_dim(g_small[k], me * c_loc, c_loc, axis=2)
    packs = [_pack([d[k] for k in SMALL_NAMES]) for d in (w, g_small, m, v)]
    res = _adamw_small(*packs, "adamw_small")
    like = [w[k] for k in SMALL_NAMES]
    d_s, m_s, v_s = (dict(zip(SMALL_NAMES, _unpack(r, like))) for r in res)

    grad, delta, new_m, new_v = {}, {}, {}, {}
    for k in WEIGHT_NAMES:
        if k in big:
            grad[k], delta[k], new_m[k], new_v[k] = big[k]
        else:
            grad[k], delta[k], new_m[k], new_v[k] = g_small[k], d_s[k], m_s[k], v_s[k]
    return (loss, grad_x[None], *[grad[k] for k in WEIGHT_NAMES], *[delta[k] for k in WEIGHT_NAMES],
            *[new_m[k] for k in WEIGHT_NAMES], *[new_v[k] for k in WEIGHT_NAMES])
```

```python
import functools
import math

import jax
import jax.numpy as jnp
from jax import lax
from jax.experimental import pallas as pl
from jax.experimental.pallas import tpu as pltpu

F32 = jnp.float32
MXU_DTYPE = jnp.bfloat16
WIRE_DTYPE = jnp.bfloat16
MESH = pl.DeviceIdType.MESH

N_DEV = 8
GROUP_W = 256
N_HEADS = 4
HEAD_DIM = 64
N_CHUNKS = 13
N_ABC = 9
GMLP_CHUNK = 128
ATTN_BLOCK = 128
ATTN_BLOCKS_PER_STEP = 4
ATTN_DILATIONS = (1, 4, 16)
NORM_EPS = 1e-6
RG_C = 8.0
SUBLANES = 8
LANES = 128
VMEM_LIMIT = 56 * 1024 * 1024

ADAM_LR = 0.001
ADAM_B1 = 0.9
ADAM_B2 = 0.999
ADAM_EPS = 1e-08
ADAM_WD = 0.01
ADAM_STEP = 10

TM_MIX = 512
TM_MM = 512
WGRAD_TAIL_PARTS = 2


def _params(sem, vmem=VMEM_LIMIT):
    return pltpu.CompilerParams(dimension_semantics=sem, vmem_limit_bytes=vmem)


def _mm(a, b):
    return jnp.dot(a.astype(MXU_DTYPE), b.astype(MXU_DTYPE), preferred_element_type=F32)


def _mm_tn(a, b):
    return lax.dot_general(a.astype(MXU_DTYPE), b.astype(MXU_DTYPE), (((0,), (0,)), ((), ())),
                           preferred_element_type=F32)


def _mm_nt(a, b):
    return lax.dot_general(a.astype(MXU_DTYPE), b.astype(MXU_DTYPE), (((1,), (1,)), ((), ())),
                           preferred_element_type=F32)


def _sigmoid(x):
    return 0.5 * jnp.tanh(0.5 * x) + 0.5


def _silu_and_grad(x):
    s = _sigmoid(x)
    return x * s, s * (1.0 + x * (1.0 - s))


_GELU_K = math.sqrt(2.0 / math.pi)
_GELU_C = 0.044715


def _gelu_and_grad(x):
    x2 = x * x
    t = jnp.tanh(_GELU_K * (x + _GELU_C * x * x2))
    val = 0.5 * x * (1.0 + t)
    grad = 0.5 * (1.0 + t) + 0.5 * x * (1.0 - t * t) * (_GELU_K * (1.0 + 3.0 * _GELU_C * x2))
    return val, grad


def _gelu(x):
    return 0.5 * x * (1.0 + jnp.tanh(_GELU_K * (x + _GELU_C * x * x * x)))


def _expm1_nonpos(u):
    poly = 1.0 / math.factorial(9)
    for k in range(8, 0, -1):
        poly = poly * u + 1.0 / math.factorial(k)
    return jnp.where(u > -0.25, poly * u, jnp.exp(u) - 1.0)


def _softplus(x):
    return jnp.maximum(x, 0.0) + jnp.log(1.0 + jnp.exp(-jnp.abs(x)))


def _shift_down(t, halo, k):
    rolled = pltpu.roll(t, k, 0)
    hr = pltpu.roll(halo, k, 0)
    row = lax.broadcasted_iota(jnp.int32, halo.shape, 0)
    first = jnp.where(row < k, hr, rolled[0:SUBLANES])
    return jnp.concatenate([first, rolled[SUBLANES:]], axis=0)


def _shift_up(t, nxt, k):
    tm = t.shape[0]
    rolled = pltpu.roll(t, tm - k, 0)
    nr = pltpu.roll(nxt, SUBLANES - k, 0)
    row = lax.broadcasted_iota(jnp.int32, nxt.shape, 0)
    last = jnp.where(row >= SUBLANES - k, nr, rolled[tm - SUBLANES:tm])
    return jnp.concatenate([rolled[:tm - SUBLANES], last], axis=0)


def _scan_fwd(a, b):
    tm = a.shape[0]
    row = lax.broadcasted_iota(jnp.int32, a.shape, 0)
    s = 1
    while s < tm:
        a_s = pltpu.roll(a, s, 0)
        b_s = pltpu.roll(b, s, 0)
        m = row >= s
        b = jnp.where(m, a * b_s + b, b)
        a = jnp.where(m, a * a_s, a)
        s *= 2
    return a, b


def _scan_rev(a, g):
    tm = a.shape[0]
    row = lax.broadcasted_iota(jnp.int32, a.shape, 0)
    s = 1
    while s < tm:
        a_s = pltpu.roll(a, tm - s, 0)
        g_s = pltpu.roll(g, tm - s, 0)
        m = row < tm - s
        g = jnp.where(m, g + a * g_s, g)
        a = jnp.where(m, a * a_s, a)
        s *= 2
    return g


def _group_rows(scr_ref, row, n_groups):
    return jnp.concatenate([scr_ref[pl.ds(c, 1), pl.ds(row, n_groups, stride=SUBLANES), :][0]
                            for c in range(scr_ref.shape[0])], axis=1)


def _spread_rows(rows_ref, n_groups, w):
    return jnp.concatenate([jnp.broadcast_to(rows_ref[g:g + 1, :], (SUBLANES, w)) for g in range(n_groups)], axis=0)


def _scan_groups(a, b, reverse):
    tm, w = a.shape
    shape3 = (tm // SUBLANES, SUBLANES, w)
    a3, b3 = a.reshape(shape3), b.reshape(shape3)
    sub = lax.broadcasted_iota(jnp.int32, shape3, 1)
    s = 1
    while s < SUBLANES:
        shift = SUBLANES - s if reverse else s
        a_s = pltpu.roll(a3, shift, 1)
        b_s = pltpu.roll(b3, shift, 1)
        m = (sub < SUBLANES - s) if reverse else (sub >= s)
        b3 = jnp.where(m, a3 * b_s + b3, b3)
        a3 = jnp.where(m, a3 * a_s, a3)
        s *= 2
    return a3.reshape(tm, w), b3.reshape(tm, w)


def _scan_fwd_tile(a, b, h_in, sa_ref, sb_ref, sc_ref):
    tm, w = a.shape
    n_groups = tm // SUBLANES
    a_loc, b_loc = _scan_groups(a, b, False)
    _put(sa_ref, a_loc)
    _put(sb_ref, b_loc)
    a_end, b_end = _scan_fwd(_group_rows(sa_ref, SUBLANES - 1, n_groups), _group_rows(sb_ref, SUBLANES - 1, n_groups))
    h_end = b_end + a_end * h_in
    sc_ref[...] = _shift_down(h_end, jnp.broadcast_to(h_in, (SUBLANES, w)), 1)
    return b_loc + a_loc * _spread_rows(sc_ref, n_groups, w), h_end


def _scan_rev_tile(a, g, sa_ref, sb_ref, sc_ref):
    tm, w = a.shape
    n_groups = tm // SUBLANES
    a_loc, g_loc = _scan_groups(a, g, True)
    _put(sa_ref, a_loc)
    _put(sb_ref, g_loc)
    d_first = _scan_rev(_group_rows(sa_ref, 0, n_groups), _group_rows(sb_ref, 0, n_groups))
    sc_ref[...] = _shift_up(d_first, jnp.zeros((SUBLANES, w), F32), 1)
    return g_loc + a_loc * _spread_rows(sc_ref, n_groups, w)


def _lane_scratch(tm, w):
    return pltpu.VMEM((w // LANES, tm, LANES), F32)


def _put(scr_ref, val):
    for c in range(scr_ref.shape[0]):
        scr_ref[c] = val[:, c * LANES:(c + 1) * LANES].astype(F32)


def _get(scr_ref):
    return jnp.concatenate([scr_ref[c] for c in range(scr_ref.shape[0])], axis=1)


def _deinterleave(src_ref, dst_ref, dil):
    nc, tm, _ = src_ref.shape
    w = nc * LANES
    for r in range(dil):
        for c in range(nc):
            piece = src_ref[pl.ds(c, 1), pl.ds(r, tm // dil, stride=dil), :][0] if dil > 1 else src_ref[c]
            dst_ref[:, r * w + c * LANES:r * w + (c + 1) * LANES] = piece.astype(dst_ref.dtype)


def _interleave(src_ref, dst_ref, dil):
    nc, tm, _ = dst_ref.shape
    w = nc * LANES
    for r in range(dil):
        for c in range(nc):
            dst_ref[pl.ds(c, 1), pl.ds(r, tm // dil, stride=dil), :] = (
                src_ref[:, r * w + c * LANES:r * w + (c + 1) * LANES].astype(F32)[None])


def _dilated_spec(tm, w, dil, index=lambda i: i):
    return pl.BlockSpec((tm // dil, dil * w), lambda i: (index(i), 0))


def _dilated_shape(S, w, dil, dtype):
    return jax.ShapeDtypeStruct((S // dil, dil * w), dtype)


def _head_masks(shape):
    lane = lax.broadcasted_iota(jnp.int32, shape, 1)
    return [(lane >= h * HEAD_DIM) & (lane < (h + 1) * HEAD_DIM) for h in range(N_HEADS)]


def _colsum(v):
    return jnp.sum(v, axis=0, keepdims=True)


def _norm_inproj(x, g, w, name):
    S, D = x.shape
    N = w.shape[1]
    tm = TM_MM
    n_abc = N_ABC * GROUP_W
    n_qkv = 3 * GROUP_W

    def body(x_ref, g_ref, w_ref, zabc_ref, zg_ref, q1_ref, q4_ref, q16_ref, qkv_ref):
        xv = x_ref[...]
        r = lax.rsqrt(jnp.mean(xv * xv, axis=-1, keepdims=True) + NORM_EPS)
        h = ((xv * r) * g_ref[...]).astype(MXU_DTYPE)
        zabc_ref[...] = jnp.dot(h, w_ref[:, 0:n_abc], preferred_element_type=F32)
        _put(qkv_ref, jnp.dot(h, w_ref[:, n_abc:n_abc + n_qkv], preferred_element_type=F32))
        zg_ref[...] = jnp.dot(h, w_ref[:, n_abc + n_qkv:], preferred_element_type=F32)
        for dil, ref in zip(ATTN_DILATIONS, (q1_ref, q4_ref, q16_ref)):
            _deinterleave(qkv_ref, ref, dil)

    row = lambda wd: pl.BlockSpec((tm, wd), lambda i: (i, 0))
    return pl.pallas_call(
        body, name=name, grid=(S // tm,),
        in_specs=[row(D), pl.BlockSpec((1, D), lambda i: (0, 0)), pl.BlockSpec((D, N), lambda i: (0, 0))],
        out_specs=[row(n_abc), row(GROUP_W)] + [_dilated_spec(tm, n_qkv, dil) for dil in ATTN_DILATIONS],
        out_shape=[jax.ShapeDtypeStruct((S, n_abc), F32), jax.ShapeDtypeStruct((S, GROUP_W), F32)]
                  + [_dilated_shape(S, n_qkv, dil, MXU_DTYPE) for dil in ATTN_DILATIONS],
        scratch_shapes=[_lane_scratch(tm, n_qkv)],
        compiler_params=_params(("parallel",)),
    )(x, g, w)


def _conv_a(z_of, halo_of, w_ref):
    p = z_of(2) * z_of(0)
    p_h = halo_of(2) * halo_of(0)
    cv = w_ref[2:3, :] * p + w_ref[1:2, :] * _shift_down(p, p_h, 1) + w_ref[0:1, :] * _shift_down(p, p_h, 2)
    return p, p_h, cv


def _lru_gates(z_of, halo_of, wr_ref, vec_ref, wa_ref, wx_ref):
    rx = z_of(4)
    rx_h = halo_of(4)
    sh = [rx, _shift_down(rx, rx_h, 1), _shift_down(rx, rx_h, 2), _shift_down(rx, rx_h, 3)]
    xc = (wr_ref[3:4, :] * sh[0] + wr_ref[2:3, :] * sh[1] + wr_ref[1:2, :] * sh[2]
          + wr_ref[0:1, :] * sh[3] + vec_ref[0:1, :])
    ga = _sigmoid(jnp.dot(xc.astype(MXU_DTYPE), wa_ref[...], preferred_element_type=F32) + vec_ref[1:2, :])
    gi = _sigmoid(jnp.dot(xc.astype(MXU_DTYPE), wx_ref[...], preferred_element_type=F32) + vec_ref[2:3, :])
    sp = _softplus(-vec_ref[3:4, :])
    log_a = (-RG_C * ga) * sp
    a = jnp.exp(log_a)
    mult = jnp.sqrt(-_expm1_nonpos(2.0 * log_a))
    return xc, sh, ga, gi, a, mult, sp


def _gmlp_fwd(z_of, vec_ref, ws_ref, bs_ref, tm):
    u = _gelu(z_of(6))
    gv = _gelu(z_of(7))
    rr = lax.rsqrt(jnp.mean(gv * gv, axis=-1, keepdims=True) + NORM_EPS)
    vn = (gv * rr) * vec_ref[4:5, :]
    masks = _head_masks((GMLP_CHUNK, GROUP_W))
    parts = []
    for c in range(tm // GMLP_CHUNK):
        vc = vn[c * GMLP_CHUNK:(c + 1) * GMLP_CHUNK].astype(MXU_DTYPE)
        acc = bs_ref[...]
        for h in range(N_HEADS):
            acc = acc + jnp.where(masks[h], jnp.dot(ws_ref[h], vc, preferred_element_type=F32), 0.0)
        parts.append(acc)
    return u, gv, rr, vn, jnp.concatenate(parts, axis=0)


def _mix_specs(tm, S, order):
    const2 = lambda shape: pl.BlockSpec(shape, lambda i: (0, 0))
    return [const2((SUBLANES, GROUP_W)), const2((SUBLANES, GROUP_W)), const2((SUBLANES, GROUP_W)),
            const2((GROUP_W, GROUP_W)), const2((GROUP_W, GROUP_W)),
            pl.BlockSpec((N_HEADS, GMLP_CHUNK, GMLP_CHUNK), lambda i: (0, 0, 0)),
            const2((GMLP_CHUNK, GROUP_W))]


def _mix_fwd(z, mp, name):
    S = z.shape[0]
    tm = TM_MIX
    hb = tm // SUBLANES
    wcols = N_ABC * GROUP_W

    def body(z_ref, zh_ref, wA_ref, wR_ref, vec_ref, wa_ref, wx_ref, ws_ref, bs_ref, y_ref, h_ref, carry_ref,
             sa_ref, sb_ref, sc_ref):
        i = pl.program_id(0)

        @pl.when(i == 0)
        def _():
            carry_ref[...] = jnp.zeros_like(carry_ref)

        not_first = i > 0
        z_of = lambda c: z_ref[:, c * GROUP_W:(c + 1) * GROUP_W]
        halo_of = lambda c: jnp.where(not_first, zh_ref[:, c * GROUP_W:(c + 1) * GROUP_W], 0.0)

        _, _, cv = _conv_a(z_of, halo_of, wA_ref)
        y_ref[:, 0:GROUP_W] = (z_of(1) * cv * _silu_and_grad(z_of(3))[0]).astype(y_ref.dtype)

        xc, _, _, gi, a, mult, _ = _lru_gates(z_of, halo_of, wR_ref, vec_ref, wa_ref, wx_ref)
        b = mult * (gi * xc)
        h, h_end = _scan_fwd_tile(a, b, carry_ref[SUBLANES - 1:SUBLANES, :], sa_ref, sb_ref, sc_ref)
        h_ref[...] = h
        carry_ref[...] = h_end[hb - SUBLANES:hb]
        y_ref[:, GROUP_W:2 * GROUP_W] = (h * _silu_and_grad(z_of(5))[0]).astype(y_ref.dtype)

        u, _, _, _, sp = _gmlp_fwd(z_of, vec_ref, ws_ref, bs_ref, tm)
        y_ref[:, 2 * GROUP_W:3 * GROUP_W] = (u * sp * _silu_and_grad(z_of(8))[0]).astype(y_ref.dtype)

    return pl.pallas_call(
        body, name=name, grid=(S // tm,),
        in_specs=[pl.BlockSpec((tm, wcols), lambda i: (i, 0)),
                  pl.BlockSpec((SUBLANES, wcols), lambda i: (jnp.maximum(i * hb - 1, 0), 0))]
                 + _mix_specs(tm, S, "fwd"),
        out_specs=[pl.BlockSpec((tm, 3 * GROUP_W), lambda i: (i, 0)),
                   pl.BlockSpec((tm, GROUP_W), lambda i: (i, 0))],
        out_shape=[jax.ShapeDtypeStruct((S, 3 * GROUP_W), MXU_DTYPE), jax.ShapeDtypeStruct((S, GROUP_W), F32)],
        scratch_shapes=[pltpu.VMEM((SUBLANES, GROUP_W), F32), _lane_scratch(tm, GROUP_W), _lane_scratch(tm, GROUP_W),
                        pltpu.VMEM((hb, GROUP_W), F32)],
        compiler_params=_params(("arbitrary",)),
    )(z, z, mp["wA"], mp["wR"], mp["vec"], mp["wa"], mp["wx"], mp["ws"], mp["bs"])


_NEG = -1e30


def _slope(h):
    return 2.0 ** (-8.0 * (h + 1) / N_HEADS)


def _attn_bias(dil, offsets, n_keys):
    shape = (ATTN_BLOCK, n_keys)
    qi = lax.broadcasted_iota(jnp.int32, shape, 0)
    ki = lax.broadcasted_iota(jnp.int32, shape, 1)
    blocks = []
    for f in offsets:
        delta = qi + f - ki
        valid = (delta >= 0) & (delta <= ATTN_BLOCK)
        dist = (delta * dil).astype(F32)
        for h in range(N_HEADS):
            blocks.append(jnp.where(valid, -_slope(h) * dist, _NEG))
    return jnp.concatenate(blocks, axis=0)


def _stack_heads(t, masks):
    return jnp.concatenate([jnp.where(m, t, jnp.zeros_like(t)) for m in masks], axis=0)


def _unstack_heads(t4, masks, base=0):
    out = t4[base * ATTN_BLOCK:(base + 1) * ATTN_BLOCK]
    for h in range(1, N_HEADS):
        out = jnp.where(masks[h], t4[(base + h) * ATTN_BLOCK:(base + h + 1) * ATTN_BLOCK], out)
    return out


def _attn_fwd(qkv, dil, name):
    rows = qkv.shape[0]
    nb = rows // ATTN_BLOCK
    scale = 1.0 / math.sqrt(HEAD_DIM)
    B = ATTN_BLOCK
    per_step = ATTN_BLOCKS_PER_STEP

    def body(q_ref, kc_ref, kp_ref, vc_ref, vp_ref, o_ref, l_ref, bias_ref):
        n = pl.program_id(1)

        @pl.when(n == 0)
        def _():
            bias_ref[...] = _attn_bias(dil, (B,), 2 * B)

        masks = _head_masks((B, GROUP_W))
        for j in range(per_step):
            own = slice(j * B, (j + 1) * B)
            before = slice((j - 1) * B, j * B)
            qs = _stack_heads(q_ref[own], masks)
            keys = jnp.concatenate([kp_ref[...] if j == 0 else kc_ref[before], kc_ref[own]], axis=0)
            vals = jnp.concatenate([vp_ref[...] if j == 0 else vc_ref[before], vc_ref[own]], axis=0)
            s = _mm_nt(qs, keys) * scale + bias_ref[...]
            if j == 0:
                key_col = lax.broadcasted_iota(jnp.int32, s.shape, 1)
                s = jnp.where((n == 0) & (key_col < B), _NEG, s)
            m = jnp.max(s, axis=-1, keepdims=True)
            p = jnp.exp(s - m)
            l = jnp.sum(p, axis=-1, keepdims=True)
            o4 = jnp.dot(p.astype(MXU_DTYPE), vals, preferred_element_type=F32)
            o_ref[own] = _unstack_heads(o4, masks) / _unstack_heads(jnp.broadcast_to(l, o4.shape), masks)
            l_ref[own] = _unstack_heads(jnp.broadcast_to(m + jnp.log(l), o4.shape), masks)

    blk = (per_step * B, GROUP_W)
    cur = lambda c: pl.BlockSpec(blk, lambda r, n: (n, r * 3 + c))
    prev = lambda c: pl.BlockSpec((B, GROUP_W), lambda r, n: (jnp.maximum(n * per_step - 1, 0), r * 3 + c))
    out = pl.BlockSpec(blk, lambda r, n: (n, r))
    return pl.pallas_call(
        body, name=name, grid=(dil, nb // per_step),
        in_specs=[cur(0), cur(1), prev(1), cur(2), prev(2)],
        out_specs=[out, out],
        out_shape=[jax.ShapeDtypeStruct((rows, dil * GROUP_W), F32)] * 2,
        scratch_shapes=[pltpu.VMEM((N_HEADS * ATTN_BLOCK, 2 * ATTN_BLOCK), F32)],
        compiler_params=_params(("parallel", "arbitrary")),
    )(qkv, qkv, qkv, qkv, qkv)


def _outproj(x, z_g, y_abc, attn, w_out, name):
    S, D = x.shape
    tm = TM_MM
    n_abc = 3 * GROUP_W

    def body(x_ref, g_ref, yabc_ref, o1, l1, o2, l2, o3, l3, w_ref,
             xn_ref, y_ref, o_ref, lse1_ref, lse4_ref, lse16_ref, so2, sl2, so3, sl3, slse):
        for src, dst, dil in ((o2, so2, ATTN_DILATIONS[1]), (l2, sl2, ATTN_DILATIONS[1]),
                              (o3, so3, ATTN_DILATIONS[2]), (l3, sl3, ATTN_DILATIONS[2])):
            _interleave(src, dst, dil)
        la, lb, lc = l1[...], _get(sl2), _get(sl3)
        mx = jnp.maximum(jnp.maximum(la, lb), lc)
        ea, eb, ec = jnp.exp(la - mx), jnp.exp(lb - mx), jnp.exp(lc - mx)
        den = ea + eb + ec
        o = (ea * o1[...] + eb * _get(so2) + ec * _get(so3)) / den
        o_ref[...] = o
        _put(slse, mx + jnp.log(den))
        for dil, ref in zip(ATTN_DILATIONS, (lse1_ref, lse4_ref, lse16_ref)):
            _deinterleave(slse, ref, dil)
        y_d = o * _silu_and_grad(g_ref[...])[0]
        y_ref[:, 0:n_abc] = yabc_ref[...].astype(MXU_DTYPE)
        y_ref[:, n_abc:] = y_d.astype(MXU_DTYPE)
        xn_ref[...] = x_ref[...] + jnp.dot(y_ref[...], w_ref[...], preferred_element_type=F32)

    row = lambda w: pl.BlockSpec((tm, w), lambda i: (i, 0))
    dil_specs = [_dilated_spec(tm, GROUP_W, dil) for dil in ATTN_DILATIONS]
    (o1, l1), (o2, l2), (o3, l3) = attn
    return pl.pallas_call(
        body, name=name, grid=(S // tm,),
        in_specs=[row(D), row(GROUP_W), row(n_abc)] + [sp for sp in dil_specs for _ in range(2)]
                 + [pl.BlockSpec(w_out.shape, lambda i: (0, 0))],
        out_specs=[row(D), row(4 * GROUP_W), row(GROUP_W)] + dil_specs,
        out_shape=[jax.ShapeDtypeStruct((S, D), F32), jax.ShapeDtypeStruct((S, 4 * GROUP_W), MXU_DTYPE),
                   jax.ShapeDtypeStruct((S, GROUP_W), F32)]
                  + [_dilated_shape(S, GROUP_W, dil, F32) for dil in ATTN_DILATIONS],
        scratch_shapes=[_lane_scratch(tm, GROUP_W)] * 5,
        compiler_params=_params(("parallel",)),
    )(x, z_g, y_abc, o1, l1, o2, l2, o3, l3, w_out)


def _loss_head(x, g, target, name):
    S, D = x.shape
    tm = TM_MM

    def body(x_ref, g_ref, t_ref, dx_ref, loss_ref, dg_ref):
        i = pl.program_id(0)

        @pl.when(i == 0)
        def _():
            loss_ref[...] = jnp.zeros_like(loss_ref)
            dg_ref[...] = jnp.zeros_like(dg_ref)

        xv = x_ref[...]
        r = lax.rsqrt(jnp.mean(xv * xv, axis=-1, keepdims=True) + NORM_EPS)
        xn = xv * r
        err = xn * g_ref[...] - t_ref[...]
        per_tok = jnp.mean(err * err, axis=-1, keepdims=True)
        loss_ref[...] += 0.5 * jnp.sum(per_tok, axis=0, keepdims=True)
        dout = err * (1.0 / D)
        dg_ref[...] += _colsum(dout * xn)
        dxn = dout * g_ref[...]
        dx_ref[...] = r * (dxn - xn * jnp.mean(dxn * xn, axis=-1, keepdims=True))

    row = pl.BlockSpec((tm, D), lambda i: (i, 0))
    return pl.pallas_call(
        body, name=name, grid=(S // tm,),
        in_specs=[row, pl.BlockSpec((1, D), lambda i: (0, 0)), row],
        out_specs=[row, pl.BlockSpec((1, LANES), lambda i: (0, 0)), pl.BlockSpec((1, D), lambda i: (0, 0))],
        out_shape=[jax.ShapeDtypeStruct((S, D), F32), jax.ShapeDtypeStruct((1, LANES), F32),
                   jax.ShapeDtypeStruct((1, D), F32)],
        compiler_params=_params(("arbitrary",)),
    )(x, g, target)


def _outproj_bwd(dx, y, w_out, name):
    S, D = dx.shape
    E = y.shape[1]
    tm = TM_MM

    def body(dx_ref, y_ref, w_ref, dy_ref, dw_ref, acc_ref):
        i = pl.program_id(0)

        @pl.when(i == 0)
        def _():
            acc_ref[...] = jnp.zeros_like(acc_ref)

        dxb = dx_ref[...].astype(MXU_DTYPE)
        dy_ref[...] = _mm_nt(dxb, w_ref[...])
        acc_ref[...] += _mm_tn(y_ref[...], dxb)

        @pl.when(i == S // tm - 1)
        def _():
            dw_ref[...] = acc_ref[...].astype(dw_ref.dtype)

    return pl.pallas_call(
        body, name=name, grid=(S // tm,),
        in_specs=[pl.BlockSpec((tm, D), lambda i: (i, 0)), pl.BlockSpec((tm, E), lambda i: (i, 0)),
                  pl.BlockSpec((E, D), lambda i: (0, 0))],
        out_specs=[pl.BlockSpec((tm, E), lambda i: (i, 0)), pl.BlockSpec((E, D), lambda i: (0, 0))],
        out_shape=[jax.ShapeDtypeStruct((S, E), F32), jax.ShapeDtypeStruct((E, D), WIRE_DTYPE)],
        scratch_shapes=[pltpu.VMEM((E, D), F32)],
        compiler_params=_params(("arbitrary",)),
    )(dx, y, w_out)


def _mix_bwd(z, z_g, dy, hs, o, mp, name):
    S = z.shape[0]
    tm = TM_MIX
    hb = tm // SUBLANES
    nT = S // tm
    last_blk = S // SUBLANES - 1
    wcols = N_ABC * GROUP_W

    def body(z_ref, zh_ref, zn_ref, zg_ref, dy_ref, dyn_ref, h_ref, hh_ref, o_ref,
             wA_ref, wR_ref, vec_ref, wa_ref, wx_ref, ws_ref, bs_ref,
             dz_ref, dzg_ref, do1_ref, do4_ref, do16_ref, dl1_ref, dl4_ref, dl16_ref,
             dwA_ref, dwR_ref, dvec_ref, dwa_ref, dwx_ref, dws_ref, dbs_ref,
             hcarry_ref, xcarry_ref, bsacc_ref, do_ref, dl_ref, sa_ref, sb_ref, sc_ref):
        i = pl.program_id(0)
        ti = nT - 1 - i

        @pl.when(i == 0)
        def _():
            hcarry_ref[...] = jnp.zeros_like(hcarry_ref)
            xcarry_ref[...] = jnp.zeros_like(xcarry_ref)
            bsacc_ref[...] = jnp.zeros_like(bsacc_ref)
            dwA_ref[...] = jnp.zeros_like(dwA_ref)
            dwR_ref[...] = jnp.zeros_like(dwR_ref)
            dvec_ref[...] = jnp.zeros_like(dvec_ref)
            dwa_ref[...] = jnp.zeros_like(dwa_ref)
            dwx_ref[...] = jnp.zeros_like(dwx_ref)
            dws_ref[...] = jnp.zeros_like(dws_ref)
            dbs_ref[...] = jnp.zeros_like(dbs_ref)

        has_prev = ti > 0
        has_next = i > 0
        col = lambda c: slice(c * GROUP_W, (c + 1) * GROUP_W)
        z_of = lambda c: z_ref[:, col(c)]
        halo_of = lambda c: jnp.where(has_prev, zh_ref[:, col(c)], 0.0)
        next_of = lambda c: zn_ref[:, col(c)]

        p, p_h, cv = _conv_a(z_of, halo_of, wA_ref)
        sg, dsg = _silu_and_grad(z_of(3))
        a_b = z_of(1)
        dya = dy_ref[:, col(0)]
        dcv = dya * a_b * sg
        dcv_n = jnp.where(has_next, dyn_ref[...] * next_of(1) * _silu_and_grad(next_of(3))[0], 0.0)
        dp = (wA_ref[2:3, :] * dcv + wA_ref[1:2, :] * _shift_up(dcv, dcv_n, 1)
              + wA_ref[0:1, :] * _shift_up(dcv, dcv_n, 2))
        dwA_ref[2:3, :] += _colsum(dcv * p)
        dwA_ref[1:2, :] += _colsum(dcv * _shift_down(p, p_h, 1))
        dwA_ref[0:1, :] += _colsum(dcv * _shift_down(p, p_h, 2))
        def put_dz(c, val):
            dz_ref[:, col(c)] = val.astype(dz_ref.dtype)

        put_dz(0, dp * z_of(2))
        put_dz(1, dya * cv * sg)
        put_dz(2, dp * z_of(0))
        put_dz(3, dya * a_b * cv * dsg)

        xc, sh, ga, gi, a, mult, sp = _lru_gates(z_of, halo_of, wR_ref, vec_ref, wa_ref, wx_ref)
        h = h_ref[...]
        h_prev = _shift_down(h, jnp.where(has_prev, hh_ref[...], 0.0), 1)
        sgr, dsgr = _silu_and_grad(z_of(5))
        dyb = dy_ref[:, col(1)]
        put_dz(5, dyb * h * dsgr)
        row = lax.broadcasted_iota(jnp.int32, (tm, GROUP_W), 0)
        g_in = dyb * sgr + jnp.where(row == tm - 1, hcarry_ref[0:1, :], 0.0)
        a_up = _shift_up(a, jnp.zeros((SUBLANES, GROUP_W), F32), 1)
        dH = _scan_rev_tile(a_up, g_in, sa_ref, sb_ref, sc_ref)
        hcarry_ref[...] = (a * dH)[0:SUBLANES]
        da = dH * h_prev
        gx = gi * xc
        dmult = dH * gx
        dgi = dH * mult * xc
        dxc = dH * mult * gi
        dlog_a = da * a - dmult * (a * a) / mult
        dga = dlog_a * (-RG_C * sp)
        dlam_row = _colsum(dlog_a * (-RG_C * ga)) * (-_sigmoid(-vec_ref[3:4, :]))
        dpre_a = dga * ga * (1.0 - ga)
        dpre_i = dgi * gi * (1.0 - gi)
        dwa_ref[...] += _mm_tn(xc, dpre_a)
        dwx_ref[...] += _mm_tn(xc, dpre_i)
        dxc = dxc + _mm_nt(dpre_a, wa_ref[...]) + _mm_nt(dpre_i, wx_ref[...])
        dvec_ref[0:1, :] += _colsum(dxc)
        dvec_ref[1:2, :] += _colsum(dpre_a)
        dvec_ref[2:3, :] += _colsum(dpre_i)
        dvec_ref[3:4, :] += dlam_row
        for k in range(4):
            dwR_ref[k:k + 1, :] += _colsum(dxc * sh[3 - k])
        dxc_n = xcarry_ref[...]
        put_dz(4, wR_ref[3:4, :] * dxc + wR_ref[2:3, :] * _shift_up(dxc, dxc_n, 1)
               + wR_ref[1:2, :] * _shift_up(dxc, dxc_n, 2) + wR_ref[0:1, :] * _shift_up(dxc, dxc_n, 3))
        xcarry_ref[...] = dxc[0:SUBLANES]

        c_u, c_v = z_of(6), z_of(7)
        u, du_dx = _gelu_and_grad(c_u)
        gv, dgv_dx = _gelu_and_grad(c_v)
        rr = lax.rsqrt(jnp.mean(gv * gv, axis=-1, keepdims=True) + NORM_EPS)
        xhat = gv * rr
        g_c = vec_ref[4:5, :]
        vn = xhat * g_c
        masks = _head_masks((GMLP_CHUNK, GROUP_W))
        tri_r = lax.broadcasted_iota(jnp.int32, (GMLP_CHUNK, GMLP_CHUNK), 0)
        tri_c = lax.broadcasted_iota(jnp.int32, (GMLP_CHUNK, GMLP_CHUNK), 1)
        tril = tri_r >= tri_c
        sgc, dsgc = _silu_and_grad(z_of(8))
        dyc = dy_ref[:, col(2)]
        dsp_full = dyc * u * sgc
        sp_parts, dvn_parts = [], []
        for c in range(tm // GMLP_CHUNK):
            rs = slice(c * GMLP_CHUNK, (c + 1) * GMLP_CHUNK)
            vc = vn[rs].astype(MXU_DTYPE)
            dsp_c = dsp_full[rs]
            bsacc_ref[...] += dsp_c
            acc = bs_ref[...]
            dvn_c = jnp.zeros((GMLP_CHUNK, GROUP_W), F32)
            for h in range(N_HEADS):
                w_h = ws_ref[h]
                acc = acc + jnp.where(masks[h], jnp.dot(w_h, vc, preferred_element_type=F32), 0.0)
                dsp_h = jnp.where(masks[h], dsp_c, 0.0).astype(MXU_DTYPE)
                dvn_c = dvn_c + _mm_tn(w_h, dsp_h)
                dws_ref[h] += jnp.where(tril, _mm_nt(dsp_h, vc), 0.0)
            sp_parts.append(acc)
            dvn_parts.append(dvn_c)
        spv = jnp.concatenate(sp_parts, axis=0)
        dvn = jnp.concatenate(dvn_parts, axis=0)
        put_dz(6, dyc * spv * sgc * du_dx)
        put_dz(8, dyc * u * spv * dsgc)
        dvec_ref[4:5, :] += _colsum(dvn * xhat)
        dgvn = dvn * g_c
        dgv = rr * (dgvn - xhat * jnp.mean(dgvn * xhat, axis=-1, keepdims=True))
        put_dz(7, dgv * dgv_dx)

        sgd, dsgd = _silu_and_grad(zg_ref[...])
        dyd = dy_ref[:, col(3)]
        ov = o_ref[...]
        do = dyd * sgd
        _put(do_ref, do)
        dzg_ref[...] = (dyd * ov * dsgd).astype(dzg_ref.dtype)
        prod = do * ov
        tmasks = _head_masks((tm, GROUP_W))
        dl = jnp.zeros((tm, GROUP_W), F32)
        for h in range(N_HEADS):
            dl = jnp.where(tmasks[h], jnp.sum(jnp.where(tmasks[h], prod, 0.0), axis=-1, keepdims=True), dl)
        _put(dl_ref, dl)
        for dil, d_out, l_out in zip(ATTN_DILATIONS, (do1_ref, do4_ref, do16_ref), (dl1_ref, dl4_ref, dl16_ref)):
            _deinterleave(do_ref, d_out, dil)
            _deinterleave(dl_ref, l_out, dil)

        @pl.when(i == nT - 1)
        def _():
            acc = bsacc_ref[...]
            lane = lax.broadcasted_iota(jnp.int32, (GMLP_CHUNK, LANES), 1)
            out = jnp.zeros((GMLP_CHUNK, LANES), F32)
            for h in range(N_HEADS):
                out = jnp.where(lane == h, jnp.sum(jnp.where(masks[h], acc, 0.0), axis=-1, keepdims=True), out)
            dbs_ref[...] = out

    rev = lambda w: pl.BlockSpec((tm, w), lambda i: (nT - 1 - i, 0))
    prev8 = lambda w: pl.BlockSpec((SUBLANES, w), lambda i: (jnp.maximum((nT - 1 - i) * hb - 1, 0), 0))
    next8 = lambda w: pl.BlockSpec((SUBLANES, w), lambda i: (jnp.minimum((nT - i) * hb, last_blk), 0))
    const2 = lambda shape: pl.BlockSpec(shape, lambda i: (0, 0))
    dil_specs = [_dilated_spec(tm, GROUP_W, dil, lambda i: nT - 1 - i) for dil in ATTN_DILATIONS]
    dil_shapes = [_dilated_shape(S, GROUP_W, dil, F32) for dil in ATTN_DILATIONS]
    small = (SUBLANES, GROUP_W)
    sq = (GROUP_W, GROUP_W)
    ws_shape = (N_HEADS, GMLP_CHUNK, GMLP_CHUNK)
    return pl.pallas_call(
        body, name=name, grid=(nT,),
        in_specs=[rev(wcols), prev8(wcols), next8(wcols), rev(GROUP_W),
                  rev(4 * GROUP_W), next8(GROUP_W), rev(GROUP_W), prev8(GROUP_W), rev(GROUP_W)]
                 + _mix_specs(tm, S, "bwd"),
        out_specs=[rev(wcols), rev(GROUP_W)] + dil_specs + dil_specs
                  + [const2(small), const2(small), const2(small), const2(sq), const2(sq),
                     pl.BlockSpec(ws_shape, lambda i: (0, 0, 0)), const2((GMLP_CHUNK, LANES))],
        out_shape=[jax.ShapeDtypeStruct((S, wcols), MXU_DTYPE), jax.ShapeDtypeStruct((S, GROUP_W), MXU_DTYPE)]
                  + dil_shapes + dil_shapes
                  + [jax.ShapeDtypeStruct(small, F32)] * 3 + [jax.ShapeDtypeStruct(sq, F32)] * 2
                  + [jax.ShapeDtypeStruct(ws_shape, F32), jax.ShapeDtypeStruct((GMLP_CHUNK, LANES), F32)],
        scratch_shapes=[pltpu.VMEM(small, F32), pltpu.VMEM(small, F32), pltpu.VMEM((GMLP_CHUNK, GROUP_W), F32),
                        _lane_scratch(tm, GROUP_W), _lane_scratch(tm, GROUP_W),
                        _lane_scratch(tm, GROUP_W), _lane_scratch(tm, GROUP_W), pltpu.VMEM((hb, GROUP_W), F32)],
        compiler_params=_params(("arbitrary",)),
    )(z, z, z, z_g, dy, dy, hs, hs, o, mp["wA"], mp["wR"], mp["vec"], mp["wa"], mp["wx"], mp["ws"], mp["bs"])


def _attn_bwd(qkv, do, lse, delta, dil, name):
    rows = qkv.shape[0]
    nb = rows // ATTN_BLOCK
    scale = 1.0 / math.sqrt(HEAD_DIM)
    B = ATTN_BLOCK
    per_step = ATTN_BLOCKS_PER_STEP
    n_steps = nb // per_step

    def body(qc_ref, qn_ref, k_ref, v_ref, doc_ref, don_ref, lc_ref, ln_ref, dc_ref, dn_ref,
             dq_ref, dk_ref, dv_ref, carry_ref, bias_ref):
        n = pl.program_id(1)

        @pl.when(n == 0)
        def _():
            carry_ref[...] = jnp.zeros_like(carry_ref)
            bias_ref[...] = _attn_bias(dil, (0, B), B)

        masks = _head_masks((B, GROUP_W))

        def per_row(tiles):
            return jnp.concatenate([jnp.max(jnp.where(masks[h], t, _NEG), axis=-1, keepdims=True)
                                    for t in tiles for h in range(N_HEADS)], axis=0)

        dq_acc = carry_ref[...]
        for j in range(per_step):
            own = slice(j * B, (j + 1) * B)
            after = slice((j + 1) * B, (j + 2) * B)
            last = j == per_step - 1
            nxt = lambda cur_ref, nxt_ref: nxt_ref[...] if last else cur_ref[after]
            kb = k_ref[own]
            vb = v_ref[own]
            qs = jnp.concatenate([_stack_heads(qc_ref[own], masks), _stack_heads(nxt(qc_ref, qn_ref), masks)], axis=0)
            dos = jnp.concatenate([_stack_heads(doc_ref[own].astype(MXU_DTYPE), masks),
                                   _stack_heads(nxt(doc_ref, don_ref).astype(MXU_DTYPE), masks)], axis=0)
            lse_rows = per_row([lc_ref[own], nxt(lc_ref, ln_ref)])
            dl_rows = per_row([dc_ref[own], nxt(dc_ref, dn_ref)])
            s = _mm_nt(qs, kb) * scale + bias_ref[...]
            if last:
                row = lax.broadcasted_iota(jnp.int32, s.shape, 0)
                s = jnp.where((n == n_steps - 1) & (row >= N_HEADS * B), _NEG, s)
            p = jnp.exp(s - lse_rows)
            dp = _mm_nt(dos, vb)
            ds = (p * (dp - dl_rows) * scale).astype(MXU_DTYPE)
            dv_ref[own] = _mm_tn(p.astype(MXU_DTYPE), dos)
            dk_ref[own] = _mm_tn(ds, qs)
            dq4 = jnp.dot(ds, kb, preferred_element_type=F32)
            dq_ref[own] = dq_acc + _unstack_heads(dq4, masks)
            dq_acc = _unstack_heads(dq4, masks, N_HEADS)
        carry_ref[...] = dq_acc

    blk = (per_step * B, GROUP_W)
    one = (B, GROUP_W)
    nxt_idx = lambda n: jnp.minimum((n + 1) * per_step, nb - 1)
    zcur = lambda c: pl.BlockSpec(blk, lambda r, n: (n, r * 3 + c))
    znext = lambda c: pl.BlockSpec(one, lambda r, n: (nxt_idx(n), r * 3 + c))
    cur = pl.BlockSpec(blk, lambda r, n: (n, r))
    nxt = pl.BlockSpec(one, lambda r, n: (nxt_idx(n), r))
    return pl.pallas_call(
        body, name=name, grid=(dil, n_steps),
        in_specs=[zcur(0), znext(0), zcur(1), zcur(2), cur, nxt, cur, nxt, cur, nxt],
        out_specs=[cur, cur, cur],
        out_shape=[jax.ShapeDtypeStruct((rows, dil * GROUP_W), F32)] * 3,
        scratch_shapes=[pltpu.VMEM(one, F32), pltpu.VMEM((2 * N_HEADS * B, B), F32)],
        compiler_params=_params(("parallel", "arbitrary")),
    )(qkv, qkv, qkv, qkv, do, do, lse, lse, delta, delta)


def _inproj_bwd(x, g, dxn, dz_abc, dqkv, dz_g, w_in, name):
    S, D = x.shape
    N = w_in.shape[1]
    tm = TM_MM
    n_abc = N_ABC * GROUP_W

    def body(x_ref, g_ref, dxn_ref, dabc_ref, q1, k1, v1, q2, k2, v2, q3, k3, v3, dg_ref, w_ref,
             dx_ref, dz_ref, h_ref, dgn_ref, s4_ref, s16_ref):
        i = pl.program_id(0)

        @pl.when(i == 0)
        def _():
            dgn_ref[...] = jnp.zeros_like(dgn_ref)

        dz_ref[:, 0:n_abc] = dabc_ref[...].astype(MXU_DTYPE)
        for j, parts in enumerate(((q1, q2, q3), (k1, k2, k3), (v1, v2, v3))):
            c0 = n_abc + j * GROUP_W
            _interleave(parts[1], s4_ref, ATTN_DILATIONS[1])
            _interleave(parts[2], s16_ref, ATTN_DILATIONS[2])
            dz_ref[:, c0:c0 + GROUP_W] = (parts[0][...] + _get(s4_ref) + _get(s16_ref)).astype(MXU_DTYPE)
        dz_ref[:, n_abc + 3 * GROUP_W:] = dg_ref[...].astype(MXU_DTYPE)
        dh = _mm_nt(dz_ref[...], w_ref[...])
        xv = x_ref[...]
        r = lax.rsqrt(jnp.mean(xv * xv, axis=-1, keepdims=True) + NORM_EPS)
        xn = xv * r
        gv = g_ref[...]
        h_ref[...] = (xn * gv).astype(MXU_DTYPE)
        dgn_ref[...] += _colsum(dh * xn)
        dn = dh * gv
        dx_ref[...] = dxn_ref[...] + r * (dn - xn * jnp.mean(dn * xn, axis=-1, keepdims=True))

    row = lambda w: pl.BlockSpec((tm, w), lambda i: (i, 0))
    flat = [t for p in dqkv for t in p]
    dil_specs = [_dilated_spec(tm, GROUP_W, dil) for dil in ATTN_DILATIONS for _ in range(3)]
    return pl.pallas_call(
        body, name=name, grid=(S // tm,),
        in_specs=[row(D), pl.BlockSpec((1, D), lambda i: (0, 0)), row(D), row(n_abc)] + dil_specs
                 + [row(GROUP_W), pl.BlockSpec((D, N), lambda i: (0, 0))],
        out_specs=[row(D), row(N), row(D), pl.BlockSpec((1, D), lambda i: (0, 0))],
        out_shape=[jax.ShapeDtypeStruct((S, D), F32), jax.ShapeDtypeStruct((S, N), MXU_DTYPE),
                   jax.ShapeDtypeStruct((S, D), MXU_DTYPE), jax.ShapeDtypeStruct((1, D), F32)],
        scratch_shapes=[_lane_scratch(tm, GROUP_W)] * 2,
        compiler_params=_params(("arbitrary",)),
    )(x, g, dxn, dz_abc, *flat, dz_g, w_in)


def _inproj_wgrad(h, dz, name, n_row_parts=1, row_part=0):
    S = h.shape[0]
    D = h.shape[1] // n_row_parts
    N = dz.shape[1]
    tm = TM_MM
    nj = 2
    cw = N // nj
    per = N_DEV // nj
    n_loc = N // N_DEV

    def body(h_ref, dz_ref, dw_ref, acc_ref):
        i = pl.program_id(1)

        @pl.when(i == 0)
        def _():
            acc_ref[...] = jnp.zeros_like(acc_ref)

        acc_ref[...] += _mm_tn(h_ref[...], dz_ref[...])

        @pl.when(i == S // tm - 1)
        def _():
            for b in range(per):
                dw_ref[b] = acc_ref[:, b * n_loc:(b + 1) * n_loc].astype(dw_ref.dtype)

    return pl.pallas_call(
        body, name=name, grid=(nj, S // tm),
        in_specs=[pl.BlockSpec((tm, D), lambda j, i: (i, row_part)), pl.BlockSpec((tm, cw), lambda j, i: (i, j))],
        out_specs=pl.BlockSpec((per, D, n_loc), lambda j, i: (j, 0, 0)),
        out_shape=jax.ShapeDtypeStruct((N_DEV, D, n_loc), WIRE_DTYPE),
        scratch_shapes=[pltpu.VMEM((D, cw), F32)],
        compiler_params=_params(("parallel", "arbitrary")),
    )(h, dz)


def _my_place():
    return lax.axis_index("x"), lax.axis_index("y"), lax.axis_index("c")


def _peer(x, y, c, k):
    px = 1 - x if k & 4 else x
    py = 1 - y if k & 2 else y
    pc = 1 - c if k & 1 else c
    return (px, py, pc), 4 * px + 2 * py + pc


HBM_SPEC = pl.BlockSpec(memory_space=pltpu.HBM)
SEM_SPEC = pl.BlockSpec(memory_space=pltpu.SEMAPHORE)
SPLIT_EFFECT = pltpu.SideEffectType.DATAFLOW_SIDE_EFFECTING
N_PEERS = N_DEV - 1


def _exchange_copies(srcs, lands, send_sems, recv_sems, whole, arrival):
    x, y, c = _my_place()
    me = 4 * x + 2 * y + c
    copies = []
    for t in range(len(srcs)):
        for k in range(1, N_DEV):
            peer, pidx = _peer(x, y, c, k)
            copies.append(pltpu.make_async_remote_copy(
                src_ref=srcs[t] if whole[t] else srcs[t].at[pidx],
                dst_ref=lands[t].at[pidx if arrival else me], send_sem=send_sems.at[t * N_PEERS + k - 1],
                recv_sem=recv_sems.at[t * N_PEERS + k - 1], device_id=peer, device_id_type=MESH))
    return copies


def _exchange_start(groups, name):
    sizes = [len(g) for g in groups]
    whole = [w for g in groups for _, w in g]
    srcs = [pltpu.with_memory_space_constraint(a, pltpu.HBM) for g in groups for a, _ in g]
    lands = [pltpu.with_memory_space_constraint(lax.empty(((N_DEV,) + a.shape) if w else a.shape, a.dtype), pltpu.HBM)
             for a, w in zip(srcs, whole)]
    n = len(srcs)
    n_g = len(groups)

    def body(*refs):
        src_refs, land_refs = refs[:n], refs[n:2 * n]
        sem_refs = refs[4 * n:4 * n + 2 * n_g]
        token = refs[-1]
        off = 0
        for gi, sz in enumerate(sizes):
            for send in _exchange_copies(src_refs[off:off + sz], land_refs[off:off + sz],
                                         sem_refs[2 * gi], sem_refs[2 * gi + 1], whole[off:off + sz], False):
                send.start()
            off += sz
        token[...] = jnp.zeros_like(token)

    sem_shapes = [pltpu.SemaphoreType.DMA((sz * N_PEERS,)) for sz in sizes for _ in range(2)]
    outs = pl.pallas_call(
        body, name=name,
        in_specs=[HBM_SPEC] * (2 * n),
        out_specs=[HBM_SPEC] * (2 * n) + [SEM_SPEC] * (2 * n_g) + [pl.BlockSpec(memory_space=pltpu.VMEM)],
        out_shape=[pltpu.HBM(a.shape, a.dtype) for a in srcs + lands] + sem_shapes
                  + [jax.ShapeDtypeStruct((SUBLANES, LANES), F32)],
        input_output_aliases={i: i for i in range(2 * n)},
        compiler_params=pltpu.CompilerParams(has_side_effects=SPLIT_EFFECT),
    )(*srcs, *lands)
    handles, off = [], 0
    for gi, sz in enumerate(sizes):
        handles.append((outs[2 * n + 2 * gi], outs[2 * n + 2 * gi + 1], outs[off:off + sz], outs[n + off:n + off + sz],
                        whole[off:off + sz]))
        off += sz
    return handles, outs[-1]


def _exchange_wait(handle, after, name):
    send_sems, recv_sems, srcs, lands, whole = handle
    n = len(srcs)

    def body(*refs):
        src_refs, land_refs = refs[:n], refs[n:2 * n]
        for send in _exchange_copies(src_refs, land_refs, refs[2 * n], refs[2 * n + 1], whole, False):
            send.wait_send()
        for arrival in _exchange_copies(src_refs, land_refs, refs[2 * n], refs[2 * n + 1], whole, True):
            arrival.wait_recv()

    outs = pl.pallas_call(
        body, name=name,
        in_specs=[HBM_SPEC] * (2 * n) + [SEM_SPEC, SEM_SPEC, pl.BlockSpec(memory_space=pl.ANY)],
        out_specs=[HBM_SPEC] * (2 * n),
        out_shape=[pltpu.HBM(a.shape, a.dtype) for a in list(srcs) + list(lands)],
        input_output_aliases={i: i for i in range(2 * n)},
        compiler_params=pltpu.CompilerParams(has_side_effects=SPLIT_EFFECT),
    )(*srcs, *lands, send_sems, recv_sems, after)
    x, y, c = _my_place()
    me = 4 * x + 2 * y + c
    own = [s[None] if w else lax.dynamic_slice_in_dim(s, me, 1, axis=0) for s, w in zip(outs[:n], whole)]
    return [lax.dynamic_update_slice_in_dim(ld, o, me, axis=0) for ld, o in zip(outs[n:], own)]


def _sum_slots(parts, name):
    n = len(parts)

    def body(*refs):
        for p_ref, o_ref in zip(refs[:n], refs[n:]):
            acc = p_ref[0]
            for j in range(1, N_DEV):
                acc = acc + p_ref[j]
            o_ref[...] = acc

    vm = pl.BlockSpec(memory_space=pltpu.VMEM)
    return pl.pallas_call(
        body, name=name, in_specs=[vm] * n, out_specs=[vm] * n,
        out_shape=[jax.ShapeDtypeStruct(p.shape[1:], F32) for p in parts],
        compiler_params=pltpu.CompilerParams(vmem_limit_bytes=VMEM_LIMIT),
    )(*parts)


def _adamw_math(w, g, m, v):
    m = ADAM_B1 * m + (1.0 - ADAM_B1) * g
    v = ADAM_B2 * v + (1.0 - ADAM_B2) * (g * g)
    m_hat = m / (1.0 - ADAM_B1 ** ADAM_STEP)
    v_hat = v / (1.0 - ADAM_B2 ** ADAM_STEP)
    delta = -ADAM_LR * (m_hat / (jnp.sqrt(v_hat) + ADAM_EPS) + ADAM_WD * w)
    return delta, m, v


def _adamw_summed(parts, w, m, v, tr, name):
    depth, R, C = w.shape

    def body(*refs):
        p_refs = refs[:depth]
        w_ref, m_ref, v_ref, g_ref, d_ref, nm_ref, nv_ref = refs[depth:]
        lay = pl.program_id(0)
        for l in range(depth):
            @pl.when(lay == l)
            def _(p_ref=p_refs[l]):
                g = p_ref[0].astype(F32)
                for j in range(1, N_DEV):
                    g = g + p_ref[j].astype(F32)
                g_ref[0] = g
        d_ref[0], nm_ref[0], nv_ref[0] = _adamw_math(w_ref[0], g_ref[0], m_ref[0], v_ref[0])

    part_spec = lambda l: pl.BlockSpec((N_DEV, tr, C), lambda lay, i: (0, jnp.where(lay == l, i, 0), 0))
    row = pl.BlockSpec((1, tr, C), lambda lay, i: (lay, i, 0))
    return pl.pallas_call(
        body, name=name, grid=(depth, R // tr),
        in_specs=[part_spec(l) for l in range(depth)] + [row, row, row],
        out_specs=[row] * 4, out_shape=[jax.ShapeDtypeStruct((depth, R, C), F32)] * 4,
        compiler_params=_params(("arbitrary", "arbitrary")),
    )(*parts, w, m, v)


def _adamw_small(w, g, m, v, name):
    def body(w_ref, g_ref, m_ref, v_ref, d_ref, nm_ref, nv_ref):
        d_ref[...], nm_ref[...], nv_ref[...] = _adamw_math(w_ref[...], g_ref[...], m_ref[...], v_ref[...])

    vm = pl.BlockSpec(memory_space=pltpu.VMEM)
    return pl.pallas_call(
        body, name=name, in_specs=[vm] * 4, out_specs=[vm] * 3,
        out_shape=[jax.ShapeDtypeStruct(w.shape, F32)] * 3,
        compiler_params=pltpu.CompilerParams(vmem_limit_bytes=VMEM_LIMIT),
    )(w, g, m, v)


def _pack(arrays):
    flat = jnp.concatenate([a.reshape(-1) for a in arrays])
    pad = (-flat.shape[0]) % (SUBLANES * LANES)
    return jnp.pad(flat, (0, pad)).reshape(-1, LANES)


def _unpack(buf, like):
    flat = buf.reshape(-1)
    out, off = [], 0
    for a in like:
        out.append(flat[off:off + a.size].reshape(a.shape))
        off += a.size
    return out


def _block_diag(w):
    eye = jnp.eye(N_HEADS, dtype=w.dtype)
    return jnp.einsum('hij,hk->hikj', w, eye).reshape(GROUP_W, GROUP_W)


def _diag_blocks(w):
    return jnp.einsum('hihj->hij', w.reshape(N_HEADS, HEAD_DIM, N_HEADS, HEAD_DIM))


def _pad_rows(a):
    return jnp.pad(a, ((0, SUBLANES - a.shape[0]), (0, 0)))


def _mixer_params(l, conv_a_w, conv_r_w, conv_r_b, lru_wa, lru_ba, lru_wx, lru_bx, lru_lambda, gmlp_norm_g,
                  gmlp_ws, gmlp_bs):
    tril = jnp.tril(jnp.ones((GMLP_CHUNK, GMLP_CHUNK), dtype=bool))
    vec = jnp.stack([conv_r_b[l], lru_ba[l], lru_bx[l], lru_lambda[l], gmlp_norm_g[l]])
    return {
        "wA": _pad_rows(conv_a_w[l]), "wR": _pad_rows(conv_r_w[l]), "vec": _pad_rows(vec),
        "wa": _block_diag(lru_wa[l]).astype(MXU_DTYPE), "wx": _block_diag(lru_wx[l]).astype(MXU_DTYPE),
        "ws": jnp.where(tril[None], gmlp_ws[l], 0.0).astype(MXU_DTYPE),
        "bs": jnp.repeat(jnp.transpose(gmlp_bs[l]), HEAD_DIM, axis=1),
    }


MIXER_NAMES = ("conv_a_w", "conv_r_w", "conv_r_b", "lru_wa", "lru_ba", "lru_wx", "lru_bx", "lru_lambda",
               "gmlp_norm_g", "gmlp_ws", "gmlp_bs")
SMALL_NAMES = ("norm_g",) + MIXER_NAMES + ("final_g",)


def _local_step(x, loss_target, norm_g, get_w_in, get_w_out, emit_early, emit_late, conv_a_w, conv_r_w, conv_r_b,
                lru_wa, lru_ba, lru_wx, lru_bx, lru_lambda, gmlp_norm_g, gmlp_ws, gmlp_bs, final_g):
    depth = norm_g.shape[0]
    D = x.shape[1]
    small = (conv_a_w, conv_r_w, conv_r_b, lru_wa, lru_ba, lru_wx, lru_bx, lru_lambda, gmlp_norm_g, gmlp_ws, gmlp_bs)
    saved = []
    for l in range(depth):
        mp = _mixer_params(l, *small)
        w_in_l = get_w_in(l, x)
        z, z_g, *qkv = _norm_inproj(x, norm_g[l].reshape(1, D), w_in_l, f"norm_inproj_{l}")
        y_abc, hs = _mix_fwd(z, mp, f"mix_fwd_{l}")
        attn = [_attn_fwd(qkv[p], dil, f"attn_fwd_d{dil}_{l}") for p, dil in enumerate(ATTN_DILATIONS)]
        w_out_l = get_w_out(l, y_abc)
        x_new, y, o, *lse = _outproj(x, z_g, y_abc, attn, w_out_l, f"outproj_{l}")
        saved.append((x, z, z_g, qkv, hs, y, o, lse, mp, w_in_l, w_out_l))
        x = x_new
    dx, loss, d_final_g = _loss_head(x, final_g.reshape(1, D), loss_target, "loss_head")
    token = None
    for l in reversed(range(depth)):
        x_l, z, z_g, qkv, hs, y, o, lse, mp, w_in_l, w_out_l = saved[l]
        if token is not None:
            mp = dict(mp, vec=mp["vec"] + token[0, 0])
        dy, dw_out = _outproj_bwd(dx, y, w_out_l, f"outproj_bwd_{l}")
        (dz_abc, dz_g, do1, do4, do16, dl1, dl4, dl16, dwA, dwR, dvec, dwa, dwx, dws, dbs) = _mix_bwd(
            z, z_g, dy, hs, o, mp, f"mix_bwd_{l}")
        token = emit_early(l, dw_out, [
            dwA[:conv_a_w.shape[1]], dwR[:conv_r_w.shape[1]], dvec[0], _diag_blocks(dwa), dvec[1], _diag_blocks(dwx),
            dvec[2], dvec[3], dvec[4], dws, jnp.transpose(dbs[:, :N_HEADS])])
        g_row = norm_g[l].reshape(1, D)
        if token is not None:
            g_row = g_row + token[0, 0]
        dqkv = [_attn_bwd(qkv[p], do, lse[p], dl, dil, f"attn_bwd_d{dil}_{l}")
                for p, (dil, do, dl) in enumerate(zip(ATTN_DILATIONS, (do1, do4, do16), (dl1, dl4, dl16)))]
        dx, dz, h, dng = _inproj_bwd(x_l, g_row, dx, dz_abc, dqkv, dz_g, w_in_l, f"inproj_bwd_{l}")
        n_parts = WGRAD_TAIL_PARTS if l == 0 else 1
        for part in range(n_parts):
            dw_in = _inproj_wgrad(h, dz, f"inproj_wgrad_{l}_{part}", n_parts, part)
            norm_grads = [dng[0]] + ([d_final_g[0]] if l == depth - 1 else [])
            token = emit_late(l, part, dw_in, norm_grads if part == 0 else [])
    return loss[0, 0], dx
WEIGHT_NAMES = ("norm_g", "w_in", "conv_a_w", "conv_r_w", "conv_r_b", "lru_wa", "lru_ba", "lru_wx", "lru_bx",
                "lru_lambda", "gmlp_norm_g", "gmlp_ws", "gmlp_bs", "w_out", "final_g")


def kernel(x, norm_g, w_in, conv_a_w, conv_r_w, conv_r_b, lru_wa, lru_ba, lru_wx, lru_bx, lru_lambda, gmlp_norm_g, gmlp_ws, gmlp_bs, w_out, final_g, loss_target, m_norm_g, m_w_in, m_conv_a_w, m_conv_r_w, m_conv_r_b, m_lru_wa, m_lru_ba, m_lru_wx, m_lru_bx, m_lru_lambda, m_gmlp_norm_g, m_gmlp_ws, m_gmlp_bs, m_w_out, m_final_g, v_norm_g, v_w_in, v_conv_a_w, v_conv_r_w, v_conv_r_b, v_lru_wa, v_lru_ba, v_lru_wx, v_lru_bx, v_lru_lambda, v_gmlp_norm_g, v_gmlp_ws, v_gmlp_bs, v_w_out, v_final_g):
    w = dict(norm_g=norm_g, w_in=w_in, conv_a_w=conv_a_w, conv_r_w=conv_r_w, conv_r_b=conv_r_b, lru_wa=lru_wa,
             lru_ba=lru_ba, lru_wx=lru_wx, lru_bx=lru_bx, lru_lambda=lru_lambda, gmlp_norm_g=gmlp_norm_g,
             gmlp_ws=gmlp_ws, gmlp_bs=gmlp_bs, w_out=w_out, final_g=final_g)
    m = dict(norm_g=m_norm_g, w_in=m_w_in, conv_a_w=m_conv_a_w, conv_r_w=m_conv_r_w, conv_r_b=m_conv_r_b,
             lru_wa=m_lru_wa, lru_ba=m_lru_ba, lru_wx=m_lru_wx, lru_bx=m_lru_bx, lru_lambda=m_lru_lambda,
             gmlp_norm_g=m_gmlp_norm_g, gmlp_ws=m_gmlp_ws, gmlp_bs=m_gmlp_bs, w_out=m_w_out, final_g=m_final_g)
    v = dict(norm_g=v_norm_g, w_in=v_w_in, conv_a_w=v_conv_a_w, conv_r_w=v_conv_r_w, conv_r_b=v_conv_r_b,
             lru_wa=v_lru_wa, lru_ba=v_lru_ba, lru_wx=v_lru_wx, lru_bx=v_lru_bx, lru_lambda=v_lru_lambda,
             gmlp_norm_g=v_gmlp_norm_g, gmlp_ws=v_gmlp_ws, gmlp_bs=v_gmlp_bs, w_out=v_w_out, final_g=v_final_g)
    depth, D, n_loc = w_in.shape
    e_loc = w_out.shape[1]
    cx, cy, cc = _my_place()
    me = 4 * cx + 2 * cy + cc

    w_in_w, w_out_w = w_in.astype(MXU_DTYPE), w_out.astype(MXU_DTYPE)
    c_loc = conv_a_w.shape[2]
    taps = (conv_a_w, conv_r_w)
    groups = [[(w_in_w[0], True), (_pack(taps), True)], [(w_out_w[0], True)]]
    groups += [[(w_in_w[l], True), (w_out_w[l], True)] for l in range(1, depth)]
    gathers, _ = _exchange_start(groups, "gather_start")
    full_in = lambda g: jnp.transpose(g, (1, 0, 2)).reshape(D, N_DEV * n_loc)
    full_out = lambda g: g.reshape(N_DEV * e_loc, D)

    g_in0, g_taps = _exchange_wait(gathers[0], x, "gather_wait_in_0")
    g_taps = g_taps.reshape(N_DEV, -1)
    conv_full, off = [], 0
    for a in taps:
        part = g_taps[:, off:off + a.size].reshape((N_DEV,) + a.shape)
        conv_full.append(jnp.transpose(part, (1, 2, 0, 3)).reshape(a.shape[:2] + (N_DEV * c_loc,)))
        off += a.size
    conv_a_full, conv_r_full = conv_full
    later = {}

    def get_w_in(l, after):
        if l == 0:
            return full_in(g_in0)
        g_in, later[l] = _exchange_wait(gathers[l + 1], after, f"gather_wait_{l}")
        return full_in(g_in)

    def get_w_out(l, after):
        if l == 0:
            return full_out(_exchange_wait(gathers[1], after, "gather_wait_out_0")[0])
        return full_out(later[l])

    early, late = {}, {}

    def emit_early(l, dw_out, mixer_grads):
        handles, token = _exchange_start(
            [[(dw_out.reshape(N_DEV, e_loc, D), False), (_pack(mixer_grads), True)]], f"early_start_{l}")
        early[l] = (handles[0], mixer_grads)
        return token

    def emit_late(l, part, dw_in, norm_grads):
        group = [(dw_in, False)] + ([(_pack(norm_grads), True)] if norm_grads else [])
        handles, token = _exchange_start([group], f"late_start_{l}_{part}")
        late.setdefault(l, []).append((handles[0], norm_grads))
        return token

    loss, grad_x = _local_step(
        x[0], loss_target[0], norm_g, get_w_in, get_w_out, emit_early, emit_late, conv_a_full, conv_r_full, conv_r_b,
        lru_wa, lru_ba, lru_wx, lru_bx, lru_lambda, gmlp_norm_g, gmlp_ws, gmlp_bs, final_g)
    loss = lax.psum(loss, ("x", "y", "c"))

    r_in, r_out, small_parts = {}, {}, []
    for l in reversed(range(depth)):
        r_out[l], r_mix = _exchange_wait(early[l][0], grad_x, f"early_wait_{l}")
        slabs = []
        for part, (handle, _) in enumerate(late[l]):
            slab, *r_norm = _exchange_wait(handle, grad_x, f"late_wait_{l}_{part}")
            slabs.append(slab)
            if part == 0:
                small_parts += [r_mix, r_norm[0]]
        r_in[l] = slabs[0] if len(slabs) == 1 else jnp.concatenate(slabs, axis=1)
    big = {
        "w_in": _adamw_summed([r_in[l] for l in range(depth)], w_in, m_w_in, v_w_in, 512, "adamw_w_in"),
        "w_out": _adamw_summed([r_out[l] for l in range(depth)], w_out, m_w_out, v_w_out, 128, "adamw_w_out"),
    }

    sums = _sum_slots(small_parts, "sum_small_grads")
    by_layer = {}
    for i, l in enumerate(reversed(range(depth))):
        mix = _unpack(sums[2 * i], early[l][1])
        nrm = _unpack(sums[2 * i + 1], late[l][0][1])
        by_layer[l] = dict(zip(MIXER_NAMES, mix), norm_g=nrm[0])
        if l == depth - 1:
            g_final = nrm[1]
    g_small = {k: jnp.stack([by_layer[l][k] for l in range(depth)]) for k in ("norm_g",) + MIXER_NAMES}
    g_small["final_g"] = g_final
    for k in ("conv_a_w", "conv_r_w"):
        g_small[k] = lax.dynamic_slice_in_dim(g_small[k], me * c_loc, c_loc, axis=2)
    packs = [_pack([d[k] for k in SMALL_NAMES]) for d in (w, g_small, m, v)]
    res = _adamw_small(*packs, "adamw_small")
    like = [w[k] for k in SMALL_NAMES]
    d_s, m_s, v_s = (dict(zip(SMALL_NAMES, _unpack(r, like))) for r in res)

    grad, delta, new_m, new_v = {}, {}, {}, {}
    for k in WEIGHT_NAMES:
        if k in big:
            grad[k], delta[k], new_m[k], new_v[k] = big[k]
        else:
            grad[k], delta[k], new_m[k], new_v[k] = g_small[k], d_s[k], m_s[k], v_s[k]
    return (loss, grad_x[None], *[grad[k] for k in WEIGHT_NAMES], *[delta[k] for k in WEIGHT_NAMES],
            *[new_m[k] for k in WEIGHT_NAMES], *[new_v[k] for k in WEIGHT_NAMES])
```

```python
import functools
import math

import jax
import jax.numpy as jnp
from jax import lax
from jax.experimental import pallas as pl
from jax.experimental.pallas import tpu as pltpu

F32 = jnp.float32
MXU_DTYPE = jnp.bfloat16
WIRE_DTYPE = jnp.bfloat16
MESH = pl.DeviceIdType.MESH

N_DEV = 8
GROUP_W = 256
N_HEADS = 4
HEAD_DIM = 64
N_CHUNKS = 13
N_ABC = 9
GMLP_CHUNK = 128
ATTN_BLOCK = 128
ATTN_BLOCKS_PER_STEP = 4
ATTN_DILATIONS = (1, 4, 16)
NORM_EPS = 1e-6
RG_C = 8.0
SUBLANES = 8
LANES = 128
VMEM_LIMIT = 56 * 1024 * 1024

ADAM_LR = 0.001
ADAM_B1 = 0.9
ADAM_B2 = 0.999
ADAM_EPS = 1e-08
ADAM_WD = 0.01
ADAM_STEP = 10

TM_MIX = 512
TM_MM = 512
WGRAD_TAIL_PARTS = 2


def _params(sem, vmem=VMEM_LIMIT):
    return pltpu.CompilerParams(dimension_semantics=sem, vmem_limit_bytes=vmem)


def _mm(a, b):
    return jnp.dot(a.astype(MXU_DTYPE), b.astype(MXU_DTYPE), preferred_element_type=F32)


def _mm_tn(a, b):
    return lax.dot_general(a.astype(MXU_DTYPE), b.astype(MXU_DTYPE), (((0,), (0,)), ((), ())),
                           preferred_element_type=F32)


def _mm_nt(a, b):
    return lax.dot_general(a.astype(MXU_DTYPE), b.astype(MXU_DTYPE), (((1,), (1,)), ((), ())),
                           preferred_element_type=F32)


def _sigmoid(x):
    return 0.5 * jnp.tanh(0.5 * x) + 0.5


def _silu_and_grad(x):
    s = _sigmoid(x)
    return x * s, s * (1.0 + x * (1.0 - s))


_GELU_K = math.sqrt(2.0 / math.pi)
_GELU_C = 0.044715


def _gelu_and_grad(x):
    x2 = x * x
    t = jnp.tanh(_GELU_K * (x + _GELU_C * x * x2))
    val = 0.5 * x * (1.0 + t)
    grad = 0.5 * (1.0 + t) + 0.5 * x * (1.0 - t * t) * (_GELU_K * (1.0 + 3.0 * _GELU_C * x2))
    return val, grad


def _gelu(x):
    return 0.5 * x * (1.0 + jnp.tanh(_GELU_K * (x + _GELU_C * x * x * x)))


def _expm1_nonpos(u):
    poly = 1.0 / math.factorial(9)
    for k in range(8, 0, -1):
        poly = poly * u + 1.0 / math.factorial(k)
    return jnp.where(u > -0.25, poly * u, jnp.exp(u) - 1.0)


def _softplus(x):
    return jnp.maximum(x, 0.0) + jnp.log(1.0 + jnp.exp(-jnp.abs(x)))


def _shift_down(t, halo, k):
    rolled = pltpu.roll(t, k, 0)
    hr = pltpu.roll(halo, k, 0)
    row = lax.broadcasted_iota(jnp.int32, halo.shape, 0)
    first = jnp.where(row < k, hr, rolled[0:SUBLANES])
    return jnp.concatenate([first, rolled[SUBLANES:]], axis=0)


def _shift_up(t, nxt, k):
    tm = t.shape[0]
    rolled = pltpu.roll(t, tm - k, 0)
    nr = pltpu.roll(nxt, SUBLANES - k, 0)
    row = lax.broadcasted_iota(jnp.int32, nxt.shape, 0)
    last = jnp.where(row >= SUBLANES - k, nr, rolled[tm - SUBLANES:tm])
    return jnp.concatenate([rolled[:tm - SUBLANES], last], axis=0)


def _scan_fwd(a, b):
    tm = a.shape[0]
    row = lax.broadcasted_iota(jnp.int32, a.shape, 0)
    s = 1
    while s < tm:
        a_s = pltpu.roll(a, s, 0)
        b_s = pltpu.roll(b, s, 0)
        m = row >= s
        b = jnp.where(m, a * b_s + b, b)
        a = jnp.where(m, a * a_s, a)
        s *= 2
    return a, b


def _scan_rev(a, g):
    tm = a.shape[0]
    row = lax.broadcasted_iota(jnp.int32, a.shape, 0)
    s = 1
    while s < tm:
        a_s = pltpu.roll(a, tm - s, 0)
        g_s = pltpu.roll(g, tm - s, 0)
        m = row < tm - s
        g = jnp.where(m, g + a * g_s, g)
        a = jnp.where(m, a * a_s, a)
        s *= 2
    return g


def _group_rows(scr_ref, row, n_groups):
    return jnp.concatenate([scr_ref[pl.ds(c, 1), pl.ds(row, n_groups, stride=SUBLANES), :][0]
                            for c in range(scr_ref.shape[0])], axis=1)


def _spread_rows(rows_ref, n_groups, w):
    return jnp.concatenate([jnp.broadcast_to(rows_ref[g:g + 1, :], (SUBLANES, w)) for g in range(n_groups)], axis=0)


def _scan_groups(a, b, reverse):
    tm, w = a.shape
    shape3 = (tm // SUBLANES, SUBLANES, w)
    a3, b3 = a.reshape(shape3), b.reshape(shape3)
    sub = lax.broadcasted_iota(jnp.int32, shape3, 1)
    s = 1
    while s < SUBLANES:
        shift = SUBLANES - s if reverse else s
        a_s = pltpu.roll(a3, shift, 1)
        b_s = pltpu.roll(b3, shift, 1)
        m = (sub < SUBLANES - s) if reverse else (sub >= s)
        b3 = jnp.where(m, a3 * b_s + b3, b3)
        a3 = jnp.where(m, a3 * a_s, a3)
        s *= 2
    return a3.reshape(tm, w), b3.reshape(tm, w)


def _scan_fwd_tile(a, b, h_in, sa_ref, sb_ref, sc_ref):
    tm, w = a.shape
    n_groups = tm // SUBLANES
    a_loc, b_loc = _scan_groups(a, b, False)
    _put(sa_ref, a_loc)
    _put(sb_ref, b_loc)
    a_end, b_end = _scan_fwd(_group_rows(sa_ref, SUBLANES - 1, n_groups), _group_rows(sb_ref, SUBLANES - 1, n_groups))
    h_end = b_end + a_end * h_in
    sc_ref[...] = _shift_down(h_end, jnp.broadcast_to(h_in, (SUBLANES, w)), 1)
    return b_loc + a_loc * _spread_rows(sc_ref, n_groups, w), h_end


def _scan_rev_tile(a, g, sa_ref, sb_ref, sc_ref):
    tm, w = a.shape
    n_groups = tm // SUBLANES
    a_loc, g_loc = _scan_groups(a, g, True)
    _put(sa_ref, a_loc)
    _put(sb_ref, g_loc)
    d_first = _scan_rev(_group_rows(sa_ref, 0, n_groups), _group_rows(sb_ref, 0, n_groups))
    sc_ref[...] = _shift_up(d_first, jnp.zeros((SUBLANES, w), F32), 1)
    return g_loc + a_loc * _spread_rows(sc_ref, n_groups, w)


def _lane_scratch(tm, w):
    return pltpu.VMEM((w // LANES, tm, LANES), F32)


def _put(scr_ref, val):
    for c in range(scr_ref.shape[0]):
        scr_ref[c] = val[:, c * LANES:(c + 1) * LANES].astype(F32)


def _get(scr_ref):
    return jnp.concatenate([scr_ref[c] for c in range(scr_ref.shape[0])], axis=1)


def _deinterleave(src_ref, dst_ref, dil):
    nc, tm, _ = src_ref.shape
    w = nc * LANES
    for r in range(dil):
        for c in range(nc):
            piece = src_ref[pl.ds(c, 1), pl.ds(r, tm // dil, stride=dil), :][0] if dil > 1 else src_ref[c]
            dst_ref[:, r * w + c * LANES:r * w + (c + 1) * LANES] = piece.astype(dst_ref.dtype)


def _interleave(src_ref, dst_ref, dil):
    nc, tm, _ = dst_ref.shape
    w = nc * LANES
    for r in range(dil):
        for c in range(nc):
            dst_ref[pl.ds(c, 1), pl.ds(r, tm // dil, stride=dil), :] = (
                src_ref[:, r * w + c * LANES:r * w + (c + 1) * LANES].astype(F32)[None])


def _dilated_spec(tm, w, dil, index=lambda i: i):
    return pl.BlockSpec((tm // dil, dil * w), lambda i: (index(i), 0))


def _dilated_shape(S, w, dil, dtype):
    return jax.ShapeDtypeStruct((S // dil, dil * w), dtype)


def _head_masks(shape):
    lane = lax.broadcasted_iota(jnp.int32, shape, 1)
    return [(lane >= h * HEAD_DIM) & (lane < (h + 1) * HEAD_DIM) for h in range(N_HEADS)]


def _colsum(v):
    return jnp.sum(v, axis=0, keepdims=True)


def _norm_inproj(x, g, w, name):
    S, D = x.shape
    N = w.shape[1]
    tm = TM_MM
    n_abc = N_ABC * GROUP_W
    n_qkv = 3 * GROUP_W

    def body(x_ref, g_ref, w_ref, zabc_ref, zg_ref, q1_ref, q4_ref, q16_ref, qkv_ref):
        xv = x_ref[...]
        r = lax.rsqrt(jnp.mean(xv * xv, axis=-1, keepdims=True) + NORM_EPS)
        h = ((xv * r) * g_ref[...]).astype(MXU_DTYPE)
        zabc_ref[...] = jnp.dot(h, w_ref[:, 0:n_abc], preferred_element_type=F32)
        _put(qkv_ref, jnp.dot(h, w_ref[:, n_abc:n_abc + n_qkv], preferred_element_type=F32))
        zg_ref[...] = jnp.dot(h, w_ref[:, n_abc + n_qkv:], preferred_element_type=F32)
        for dil, ref in zip(ATTN_DILATIONS, (q1_ref, q4_ref, q16_ref)):
            _deinterleave(qkv_ref, ref, dil)

    row = lambda wd: pl.BlockSpec((tm, wd), lambda i: (i, 0))
    return pl.pallas_call(
        body, name=name, grid=(S // tm,),
        in_specs=[row(D), pl.BlockSpec((1, D), lambda i: (0, 0)), pl.BlockSpec((D, N), lambda i: (0, 0))],
        out_specs=[row(n_abc), row(GROUP_W)] + [_dilated_spec(tm, n_qkv, dil) for dil in ATTN_DILATIONS],
        out_shape=[jax.ShapeDtypeStruct((S, n_abc), F32), jax.ShapeDtypeStruct((S, GROUP_W), F32)]
                  + [_dilated_shape(S, n_qkv, dil, MXU_DTYPE) for dil in ATTN_DILATIONS],
        scratch_shapes=[_lane_scratch(tm, n_qkv)],
        compiler_params=_params(("parallel",)),
    )(x, g, w)


def _conv_a(z_of, halo_of, w_ref):
    p = z_of(2) * z_of(0)
    p_h = halo_of(2) * halo_of(0)
    cv = w_ref[2:3, :] * p + w_ref[1:2, :] * _shift_down(p, p_h, 1) + w_ref[0:1, :] * _shift_down(p, p_h, 2)
    return p, p_h, cv


def _lru_gates(z_of, halo_of, wr_ref, vec_ref, wa_ref, wx_ref):
    rx = z_of(4)
    rx_h = halo_of(4)
    sh = [rx, _shift_down(rx, rx_h, 1), _shift_down(rx, rx_h, 2), _shift_down(rx, rx_h, 3)]
    xc = (wr_ref[3:4, :] * sh[0] + wr_ref[2:3, :] * sh[1] + wr_ref[1:2, :] * sh[2]
          + wr_ref[0:1, :] * sh[3] + vec_ref[0:1, :])
    ga = _sigmoid(jnp.dot(xc.astype(MXU_DTYPE), wa_ref[...], preferred_element_type=F32) + vec_ref[1:2, :])
    gi = _sigmoid(jnp.dot(xc.astype(MXU_DTYPE), wx_ref[...], preferred_element_type=F32) + vec_ref[2:3, :])
    sp = _softplus(-vec_ref[3:4, :])
    log_a = (-RG_C * ga) * sp
    a = jnp.exp(log_a)
    mult = jnp.sqrt(-_expm1_nonpos(2.0 * log_a))
    return xc, sh, ga, gi, a, mult, sp


def _gmlp_fwd(z_of, vec_ref, ws_ref, bs_ref, tm):
    u = _gelu(z_of(6))
    gv = _gelu(z_of(7))
    rr = lax.rsqrt(jnp.mean(gv * gv, axis=-1, keepdims=True) + NORM_EPS)
    vn = (gv * rr) * vec_ref[4:5, :]
    masks = _head_masks((GMLP_CHUNK, GROUP_W))
    parts = []
    for c in range(tm // GMLP_CHUNK):
        vc = vn[c * GMLP_CHUNK:(c + 1) * GMLP_CHUNK].astype(MXU_DTYPE)
        acc = bs_ref[...]
        for h in range(N_HEADS):
            acc = acc + jnp.where(masks[h], jnp.dot(ws_ref[h], vc, preferred_element_type=F32), 0.0)
        parts.append(acc)
    return u, gv, rr, vn, jnp.concatenate(parts, axis=0)


def _mix_specs(tm, S, order):
    const2 = lambda shape: pl.BlockSpec(shape, lambda i: (0, 0))
    return [const2((SUBLANES, GROUP_W)), const2((SUBLANES, GROUP_W)), const2((SUBLANES, GROUP_W)),
            const2((GROUP_W, GROUP_W)), const2((GROUP_W, GROUP_W)),
            pl.BlockSpec((N_HEADS, GMLP_CHUNK, GMLP_CHUNK), lambda i: (0, 0, 0)),
            const2((GMLP_CHUNK, GROUP_W))]


def _mix_fwd(z, mp, name):
    S = z.shape[0]
    tm = TM_MIX
    hb = tm // SUBLANES
    wcols = N_ABC * GROUP_W

    def body(z_ref, zh_ref, wA_ref, wR_ref, vec_ref, wa_ref, wx_ref, ws_ref, bs_ref, y_ref, h_ref, carry_ref,
             sa_ref, sb_ref, sc_ref):
        i = pl.program_id(0)

        @pl.when(i == 0)
        def _():
            carry_ref[...] = jnp.zeros_like(carry_ref)

        not_first = i > 0
        z_of = lambda c: z_ref[:, c * GROUP_W:(c + 1) * GROUP_W]
        halo_of = lambda c: jnp.where(not_first, zh_ref[:, c * GROUP_W:(c + 1) * GROUP_W], 0.0)

        _, _, cv = _conv_a(z_of, halo_of, wA_ref)
        y_ref[:, 0:GROUP_W] = (z_of(1) * cv * _silu_and_grad(z_of(3))[0]).astype(y_ref.dtype)

        xc, _, _, gi, a, mult, _ = _lru_gates(z_of, halo_of, wR_ref, vec_ref, wa_ref, wx_ref)
        b = mult * (gi * xc)
        h, h_end = _scan_fwd_tile(a, b, carry_ref[SUBLANES - 1:SUBLANES, :], sa_ref, sb_ref, sc_ref)
        h_ref[...] = h
        carry_ref[...] = h_end[hb - SUBLANES:hb]
        y_ref[:, GROUP_W:2 * GROUP_W] = (h * _silu_and_grad(z_of(5))[0]).astype(y_ref.dtype)

        u, _, _, _, sp = _gmlp_fwd(z_of, vec_ref, ws_ref, bs_ref, tm)
        y_ref[:, 2 * GROUP_W:3 * GROUP_W] = (u * sp * _silu_and_grad(z_of(8))[0]).astype(y_ref.dtype)

    return pl.pallas_call(
        body, name=name, grid=(S // tm,),
        in_specs=[pl.BlockSpec((tm, wcols), lambda i: (i, 0)),
                  pl.BlockSpec((SUBLANES, wcols), lambda i: (jnp.maximum(i * hb - 1, 0), 0))]
                 + _mix_specs(tm, S, "fwd"),
        out_specs=[pl.BlockSpec((tm, 3 * GROUP_W), lambda i: (i, 0)),
                   pl.BlockSpec((tm, GROUP_W), lambda i: (i, 0))],
        out_shape=[jax.ShapeDtypeStruct((S, 3 * GROUP_W), MXU_DTYPE), jax.ShapeDtypeStruct((S, GROUP_W), F32)],
        scratch_shapes=[pltpu.VMEM((SUBLANES, GROUP_W), F32), _lane_scratch(tm, GROUP_W), _lane_scratch(tm, GROUP_W),
                        pltpu.VMEM((hb, GROUP_W), F32)],
        compiler_params=_params(("arbitrary",)),
    )(z, z, mp["wA"], mp["wR"], mp["vec"], mp["wa"], mp["wx"], mp["ws"], mp["bs"])


_NEG = -1e30


def _slope(h):
    return 2.0 ** (-8.0 * (h + 1) / N_HEADS)


def _attn_bias(dil, offsets, n_keys):
    shape = (ATTN_BLOCK, n_keys)
    qi = lax.broadcasted_iota(jnp.int32, shape, 0)
    ki = lax.broadcasted_iota(jnp.int32, shape, 1)
    blocks = []
    for f in offsets:
        delta = qi + f - ki
        valid = (delta >= 0) & (delta <= ATTN_BLOCK)
        dist = (delta * dil).astype(F32)
        for h in range(N_HEADS):
            blocks.append(jnp.where(valid, -_slope(h) * dist, _NEG))
    return jnp.concatenate(blocks, axis=0)


def _stack_heads(t, masks):
    return jnp.concatenate([jnp.where(m, t, jnp.zeros_like(t)) for m in masks], axis=0)


def _unstack_heads(t4, masks, base=0):
    out = t4[base * ATTN_BLOCK:(base + 1) * ATTN_BLOCK]
    for h in range(1, N_HEADS):
        out = jnp.where(masks[h], t4[(base + h) * ATTN_BLOCK:(base + h + 1) * ATTN_BLOCK], out)
    return out


def _attn_fwd(qkv, dil, name):
    rows = qkv.shape[0]
    nb = rows // ATTN_BLOCK
    scale = 1.0 / math.sqrt(HEAD_DIM)
    B = ATTN_BLOCK
    per_step = ATTN_BLOCKS_PER_STEP

    def body(q_ref, kc_ref, kp_ref, vc_ref, vp_ref, o_ref, l_ref, bias_ref):
        n = pl.program_id(1)

        @pl.when(n == 0)
        def _():
            bias_ref[...] = _attn_bias(dil, (B,), 2 * B)

        masks = _head_masks((B, GROUP_W))
        for j in range(per_step):
            own = slice(j * B, (j + 1) * B)
            before = slice((j - 1) * B, j * B)
            qs = _stack_heads(q_ref[own], masks)
            keys = jnp.concatenate([kp_ref[...] if j == 0 else kc_ref[before], kc_ref[own]], axis=0)
            vals = jnp.concatenate([vp_ref[...] if j == 0 else vc_ref[before], vc_ref[own]], axis=0)
            s = _mm_nt(qs, keys) * scale + bias_ref[...]
            if j == 0:
                key_col = lax.broadcasted_iota(jnp.int32, s.shape, 1)
                s = jnp.where((n == 0) & (key_col < B), _NEG, s)
            m = jnp.max(s, axis=-1, keepdims=True)
            p = jnp.exp(s - m)
            l = jnp.sum(p, axis=-1, keepdims=True)
            o4 = jnp.dot(p.astype(MXU_DTYPE), vals, preferred_element_type=F32)
            o_ref[own] = _unstack_heads(o4, masks) / _unstack_heads(jnp.broadcast_to(l, o4.shape), masks)
            l_ref[own] = _unstack_heads(jnp.broadcast_to(m + jnp.log(l), o4.shape), masks)

    blk = (per_step * B, GROUP_W)
    cur = lambda c: pl.BlockSpec(blk, lambda r, n: (n, r * 3 + c))
    prev = lambda c: pl.BlockSpec((B, GROUP_W), lambda r, n: (jnp.maximum(n * per_step - 1, 0), r * 3 + c))
    out = pl.BlockSpec(blk, lambda r, n: (n, r))
    return pl.pallas_call(
        body, name=name, grid=(dil, nb // per_step),
        in_specs=[cur(0), cur(1), prev(1), cur(2), prev(2)],
        out_specs=[out, out],
        out_shape=[jax.ShapeDtypeStruct((rows, dil * GROUP_W), F32)] * 2,
        scratch_shapes=[pltpu.VMEM((N_HEADS * ATTN_BLOCK, 2 * ATTN_BLOCK), F32)],
        compiler_params=_params(("parallel", "arbitrary")),
    )(qkv, qkv, qkv, qkv, qkv)


def _outproj(x, z_g, y_abc, attn, w_out, name):
    S, D = x.shape
    tm = TM_MM
    n_abc = 3 * GROUP_W

    def body(x_ref, g_ref, yabc_ref, o1, l1, o2, l2, o3, l3, w_ref,
             xn_ref, y_ref, o_ref, lse1_ref, lse4_ref, lse16_ref, so2, sl2, so3, sl3, slse):
        for src, dst, dil in ((o2, so2, ATTN_DILATIONS[1]), (l2, sl2, ATTN_DILATIONS[1]),
                              (o3, so3, ATTN_DILATIONS[2]), (l3, sl3, ATTN_DILATIONS[2])):
            _interleave(src, dst, dil)
        la, lb, lc = l1[...], _get(sl2), _get(sl3)
        mx = jnp.maximum(jnp.maximum(la, lb), lc)
        ea, eb, ec = jnp.exp(la - mx), jnp.exp(lb - mx), jnp.exp(lc - mx)
        den = ea + eb + ec
        o = (ea * o1[...] + eb * _get(so2) + ec * _get(so3)) / den
        o_ref[...] = o
        _put(slse, mx + jnp.log(den))
        for dil, ref in zip(ATTN_DILATIONS, (lse1_ref, lse4_ref, lse16_ref)):
            _deinterleave(slse, ref, dil)
        y_d = o * _silu_and_grad(g_ref[...])[0]
        y_ref[:, 0:n_abc] = yabc_ref[...].astype(MXU_DTYPE)
        y_ref[:, n_abc:] = y_d.astype(MXU_DTYPE)
        xn_ref[...] = x_ref[...] + jnp.dot(y_ref[...], w_ref[...], preferred_element_type=F32)

    row = lambda w: pl.BlockSpec((tm, w), lambda i: (i, 0))
    dil_specs = [_dilated_spec(tm, GROUP_W, dil) for dil in ATTN_DILATIONS]
    (o1, l1), (o2, l2), (o3, l3) = attn
    return pl.pallas_call(
        body, name=name, grid=(S // tm,),
        in_specs=[row(D), row(GROUP_W), row(n_abc)] + [sp for sp in dil_specs for _ in range(2)]
                 + [pl.BlockSpec(w_out.shape, lambda i: (0, 0))],
        out_specs=[row(D), row(4 * GROUP_W), row(GROUP_W)] + dil_specs,
        out_shape=[jax.ShapeDtypeStruct((S, D), F32), jax.ShapeDtypeStruct((S, 4 * GROUP_W), MXU_DTYPE),
                   jax.ShapeDtypeStruct((S, GROUP_W), F32)]
                  + [_dilated_shape(S, GROUP_W, dil, F32) for dil in ATTN_DILATIONS],
        scratch_shapes=[_lane_scratch(tm, GROUP_W)] * 5,
        compiler_params=_params(("parallel",)),
    )(x, z_g, y_abc, o1, l1, o2, l2, o3, l3, w_out)


def _loss_head(x, g, target, name):
    S, D = x.shape
    tm = TM_MM

    def body(x_ref, g_ref, t_ref, dx_ref, loss_ref, dg_ref):
        i = pl.program_id(0)

        @pl.when(i == 0)
        def _():
            loss_ref[...] = jnp.zeros_like(loss_ref)
            dg_ref[...] = jnp.zeros_like(dg_ref)

        xv = x_ref[...]
        r = lax.rsqrt(jnp.mean(xv * xv, axis=-1, keepdims=True) + NORM_EPS)
        xn = xv * r
        err = xn * g_ref[...] - t_ref[...]
        per_tok = jnp.mean(err * err, axis=-1, keepdims=True)
        loss_ref[...] += 0.5 * jnp.sum(per_tok, axis=0, keepdims=True)
        dout = err * (1.0 / D)
        dg_ref[...] += _colsum(dout * xn)
        dxn = dout * g_ref[...]
        dx_ref[...] = r * (dxn - xn * jnp.mean(dxn * xn, axis=-1, keepdims=True))

    row = pl.BlockSpec((tm, D), lambda i: (i, 0))
    return pl.pallas_call(
        body, name=name, grid=(S // tm,),
        in_specs=[row, pl.BlockSpec((1, D), lambda i: (0, 0)), row],
        out_specs=[row, pl.BlockSpec((1, LANES), lambda i: (0, 0)), pl.BlockSpec((1, D), lambda i: (0, 0))],
        out_shape=[jax.ShapeDtypeStruct((S, D), F32), jax.ShapeDtypeStruct((1, LANES), F32),
                   jax.ShapeDtypeStruct((1, D), F32)],
        compiler_params=_params(("arbitrary",)),
    )(x, g, target)


def _outproj_bwd(dx, y, w_out, name):
    S, D = dx.shape
    E = y.shape[1]
    tm = TM_MM

    def body(dx_ref, y_ref, w_ref, dy_ref, dw_ref, acc_ref):
        i = pl.program_id(0)

        @pl.when(i == 0)
        def _():
            acc_ref[...] = jnp.zeros_like(acc_ref)

        dxb = dx_ref[...].astype(MXU_DTYPE)
        dy_ref[...] = _mm_nt(dxb, w_ref[...])
        acc_ref[...] += _mm_tn(y_ref[...], dxb)

        @pl.when(i == S // tm - 1)
        def _():
            dw_ref[...] = acc_ref[...].astype(dw_ref.dtype)

    return pl.pallas_call(
        body, name=name, grid=(S // tm,),
        in_specs=[pl.BlockSpec((tm, D), lambda i: (i, 0)), pl.BlockSpec((tm, E), lambda i: (i, 0)),
                  pl.BlockSpec((E, D), lambda i: (0, 0))],
        out_specs=[pl.BlockSpec((tm, E), lambda i: (i, 0)), pl.BlockSpec((E, D), lambda i: (0, 0))],
        out_shape=[jax.ShapeDtypeStruct((S, E), F32), jax.ShapeDtypeStruct((E, D), WIRE_DTYPE)],
        scratch_shapes=[pltpu.VMEM((E, D), F32)],
        compiler_params=_params(("arbitrary",)),
    )(dx, y, w_out)


def _mix_bwd(z, z_g, dy, hs, o, mp, name):
    S = z.shape[0]
    tm = TM_MIX
    hb = tm // SUBLANES
    nT = S // tm
    last_blk = S // SUBLANES - 1
    wcols = N_ABC * GROUP_W

    def body(z_ref, zh_ref, zn_ref, zg_ref, dy_ref, dyn_ref, h_ref, hh_ref, o_ref,
             wA_ref, wR_ref, vec_ref, wa_ref, wx_ref, ws_ref, bs_ref,
             dz_ref, dzg_ref, do1_ref, do4_ref, do16_ref, dl1_ref, dl4_ref, dl16_ref,
             dwA_ref, dwR_ref, dvec_ref, dwa_ref, dwx_ref, dws_ref, dbs_ref,
             hcarry_ref, xcarry_ref, bsacc_ref, do_ref, dl_ref, sa_ref, sb_ref, sc_ref):
        i = pl.program_id(0)
        ti = nT - 1 - i

        @pl.when(i == 0)
        def _():
            hcarry_ref[...] = jnp.zeros_like(hcarry_ref)
            xcarry_ref[...] = jnp.zeros_like(xcarry_ref)
            bsacc_ref[...] = jnp.zeros_like(bsacc_ref)
            dwA_ref[...] = jnp.zeros_like(dwA_ref)
            dwR_ref[...] = jnp.zeros_like(dwR_ref)
            dvec_ref[...] = jnp.zeros_like(dvec_ref)
            dwa_ref[...] = jnp.zeros_like(dwa_ref)
            dwx_ref[...] = jnp.zeros_like(dwx_ref)
            dws_ref[...] = jnp.zeros_like(dws_ref)
            dbs_ref[...] = jnp.zeros_like(dbs_ref)

        has_prev = ti > 0
        has_next = i > 0
        col = lambda c: slice(c * GROUP_W, (c + 1) * GROUP_W)
        z_of = lambda c: z_ref[:, col(c)]
        halo_of = lambda c: jnp.where(has_prev, zh_ref[:, col(c)], 0.0)
        next_of = lambda c: zn_ref[:, col(c)]

        p, p_h, cv = _conv_a(z_of, halo_of, wA_ref)
        sg, dsg = _silu_and_grad(z_of(3))
        a_b = z_of(1)
        dya = dy_ref[:, col(0)]
        dcv = dya * a_b * sg
        dcv_n = jnp.where(has_next, dyn_ref[...] * next_of(1) * _silu_and_grad(next_of(3))[0], 0.0)
        dp = (wA_ref[2:3, :] * dcv + wA_ref[1:2, :] * _shift_up(dcv, dcv_n, 1)
              + wA_ref[0:1, :] * _shift_up(dcv, dcv_n, 2))
        dwA_ref[2:3, :] += _colsum(dcv * p)
        dwA_ref[1:2, :] += _colsum(dcv * _shift_down(p, p_h, 1))
        dwA_ref[0:1, :] += _colsum(dcv * _shift_down(p, p_h, 2))
        def put_dz(c, val):
            dz_ref[:, col(c)] = val.astype(dz_ref.dtype)

        put_dz(0, dp * z_of(2))
        put_dz(1, dya * cv * sg)
        put_dz(2, dp * z_of(0))
        put_dz(3, dya * a_b * cv * dsg)

        xc, sh, ga, gi, a, mult, sp = _lru_gates(z_of, halo_of, wR_ref, vec_ref, wa_ref, wx_ref)
        h = h_ref[...]
        h_prev = _shift_down(h, jnp.where(has_prev, hh_ref[...], 0.0), 1)
        sgr, dsgr = _silu_and_grad(z_of(5))
        dyb = dy_ref[:, col(1)]
        put_dz(5, dyb * h * dsgr)
        row = lax.broadcasted_iota(jnp.int32, (tm, GROUP_W), 0)
        g_in = dyb * sgr + jnp.where(row == tm - 1, hcarry_ref[0:1, :], 0.0)
        a_up = _shift_up(a, jnp.zeros((SUBLANES, GROUP_W), F32), 1)
        dH = _scan_rev_tile(a_up, g_in, sa_ref, sb_ref, sc_ref)
        hcarry_ref[...] = (a * dH)[0:SUBLANES]
        da = dH * h_prev
        gx = gi * xc
        dmult = dH * gx
        dgi = dH * mult * xc
        dxc = dH * mult * gi
        dlog_a = da * a - dmult * (a * a) / mult
        dga = dlog_a * (-RG_C * sp)
        dlam_row = _colsum(dlog_a * (-RG_C * ga)) * (-_sigmoid(-vec_ref[3:4, :]))
        dpre_a = dga * ga * (1.0 - ga)
        dpre_i = dgi * gi * (1.0 - gi)
        dwa_ref[...] += _mm_tn(xc, dpre_a)
        dwx_ref[...] += _mm_tn(xc, dpre_i)
        dxc = dxc + _mm_nt(dpre_a, wa_ref[...]) + _mm_nt(dpre_i, wx_ref[...])
        dvec_ref[0:1, :] += _colsum(dxc)
        dvec_ref[1:2, :] += _colsum(dpre_a)
        dvec_ref[2:3, :] += _colsum(dpre_i)
        dvec_ref[3:4, :] += dlam_row
        for k in range(4):
            dwR_ref[k:k + 1, :] += _colsum(dxc * sh[3 - k])
        dxc_n = xcarry_ref[...]
        put_dz(4, wR_ref[3:4, :] * dxc + wR_ref[2:3, :] * _shift_up(dxc, dxc_n, 1)
               + wR_ref[1:2, :] * _shift_up(dxc, dxc_n, 2) + wR_ref[0:1, :] * _shift_up(dxc, dxc_n, 3))
        xcarry_ref[...] = dxc[0:SUBLANES]

        c_u, c_v = z_of(6), z_of(7)
        u, du_dx = _gelu_and_grad(c_u)
        gv, dgv_dx = _gelu_and_grad(c_v)
        rr = lax.rsqrt(jnp.mean(gv * gv, axis=-1, keepdims=True) + NORM_EPS)
        xhat = gv * rr
        g_c = vec_ref[4:5, :]
        vn = xhat * g_c
        masks = _head_masks((GMLP_CHUNK, GROUP_W))
        tri_r = lax.broadcasted_iota(jnp.int32, (GMLP_CHUNK, GMLP_CHUNK), 0)
        tri_c = lax.broadcasted_iota(jnp.int32, (GMLP_CHUNK, GMLP_CHUNK), 1)
        tril = tri_r >= tri_c
        sgc, dsgc = _silu_and_grad(z_of(8))
        dyc = dy_ref[:, col(2)]
        dsp_full = dyc * u * sgc
        sp_parts, dvn_parts = [], []
        for c in range(tm // GMLP_CHUNK):
            rs = slice(c * GMLP_CHUNK, (c + 1) * GMLP_CHUNK)
            vc = vn[rs].astype(MXU_DTYPE)
            dsp_c = dsp_full[rs]
            bsacc_ref[...] += dsp_c
            acc = bs_ref[...]
            dvn_c = jnp.zeros((GMLP_CHUNK, GROUP_W), F32)
            for h in range(N_HEADS):
                w_h = ws_ref[h]
                acc = acc + jnp.where(masks[h], jnp.dot(w_h, vc, preferred_element_type=F32), 0.0)
                dsp_h = jnp.where(masks[h], dsp_c, 0.0).astype(MXU_DTYPE)
                dvn_c = dvn_c + _mm_tn(w_h, dsp_h)
                dws_ref[h] += jnp.where(tril, _mm_nt(dsp_h, vc), 0.0)
            sp_parts.append(acc)
            dvn_parts.append(dvn_c)
        spv = jnp.concatenate(sp_parts, axis=0)
        dvn = jnp.concatenate(dvn_parts, axis=0)
        put_dz(6, dyc * spv * sgc * du_dx)
        put_dz(8, dyc * u * spv * dsgc)
        dvec_ref[4:5, :] += _colsum(dvn * xhat)
        dgvn = dvn * g_c
        dgv = rr * (dgvn - xhat * jnp.mean(dgvn * xhat, axis=-1, keepdims=True))
        put_dz(7, dgv * dgv_dx)

        sgd, dsgd = _silu_and_grad(zg_ref[...])
        dyd = dy_ref[:, col(3)]
        ov = o_ref[...]
        do = dyd * sgd
        _put(do_ref, do)
        dzg_ref[...] = (dyd * ov * dsgd).astype(dzg_ref.dtype)
        prod = do * ov
        tmasks = _head_masks((tm, GROUP_W))
        dl = jnp.zeros((tm, GROUP_W), F32)
        for h in range(N_HEADS):
            dl = jnp.where(tmasks[h], jnp.sum(jnp.where(tmasks[h], prod, 0.0), axis=-1, keepdims=True), dl)
        _put(dl_ref, dl)
        for dil, d_out, l_out in zip(ATTN_DILATIONS, (do1_ref, do4_ref, do16_ref), (dl1_ref, dl4_ref, dl16_ref)):
            _deinterleave(do_ref, d_out, dil)
            _deinterleave(dl_ref, l_out, dil)

        @pl.when(i == nT - 1)
        def _():
            acc = bsacc_ref[...]
            lane = lax.broadcasted_iota(jnp.int32, (GMLP_CHUNK, LANES), 1)
            out = jnp.zeros((GMLP_CHUNK, LANES), F32)
            for h in range(N_HEADS):
                out = jnp.where(lane == h, jnp.sum(jnp.where(masks[h], acc, 0.0), axis=-1, keepdims=True), out)
            dbs_ref[...] = out

    rev = lambda w: pl.BlockSpec((tm, w), lambda i: (nT - 1 - i, 0))
    prev8 = lambda w: pl.BlockSpec((SUBLANES, w), lambda i: (jnp.maximum((nT - 1 - i) * hb - 1, 0), 0))
    next8 = lambda w: pl.BlockSpec((SUBLANES, w), lambda i: (jnp.minimum((nT - i) * hb, last_blk), 0))
    const2 = lambda shape: pl.BlockSpec(shape, lambda i: (0, 0))
    dil_specs = [_dilated_spec(tm, GROUP_W, dil, lambda i: nT - 1 - i) for dil in ATTN_DILATIONS]
    dil_shapes = [_dilated_shape(S, GROUP_W, dil, F32) for dil in ATTN_DILATIONS]
    small = (SUBLANES, GROUP_W)
    sq = (GROUP_W, GROUP_W)
    ws_shape = (N_HEADS, GMLP_CHUNK, GMLP_CHUNK)
    return pl.pallas_call(
        body, name=name, grid=(nT,),
        in_specs=[rev(wcols), prev8(wcols), next8(wcols), rev(GROUP_W),
                  rev(4 * GROUP_W), next8(GROUP_W), rev(GROUP_W), prev8(GROUP_W), rev(GROUP_W)]
                 + _mix_specs(tm, S, "bwd"),
        out_specs=[rev(wcols), rev(GROUP_W)] + dil_specs + dil_specs
                  + [const2(small), const2(small), const2(small), const2(sq), const2(sq),
                     pl.BlockSpec(ws_shape, lambda i: (0, 0, 0)), const2((GMLP_CHUNK, LANES))],
        out_shape=[jax.ShapeDtypeStruct((S, wcols), MXU_DTYPE), jax.ShapeDtypeStruct((S, GROUP_W), MXU_DTYPE)]
                  + dil_shapes + dil_shapes
                  + [jax.ShapeDtypeStruct(small, F32)] * 3 + [jax.ShapeDtypeStruct(sq, F32)] * 2
                  + [jax.ShapeDtypeStruct(ws_shape, F32), jax.ShapeDtypeStruct((GMLP_CHUNK, LANES), F32)],
        scratch_shapes=[pltpu.VMEM(small, F32), pltpu.VMEM(small, F32), pltpu.VMEM((GMLP_CHUNK, GROUP_W), F32),
                        _lane_scratch(tm, GROUP_W), _lane_scratch(tm, GROUP_W),
                        _lane_scratch(tm, GROUP_W), _lane_scratch(tm, GROUP_W), pltpu.VMEM((hb, GROUP_W), F32)],
        compiler_params=_params(("arbitrary",)),
    )(z, z, z, z_g, dy, dy, hs, hs, o, mp["wA"], mp["wR"], mp["vec"], mp["wa"], mp["wx"], mp["ws"], mp["bs"])


def _attn_bwd(qkv, do, lse, delta, dil, name):
    rows = qkv.shape[0]
    nb = rows // ATTN_BLOCK
    scale = 1.0 / math.sqrt(HEAD_DIM)
    B = ATTN_BLOCK
    per_step = ATTN_BLOCKS_PER_STEP
    n_steps = nb // per_step

    def body(qc_ref, qn_ref, k_ref, v_ref, doc_ref, don_ref, lc_ref, ln_ref, dc_ref, dn_ref,
             dq_ref, dk_ref, dv_ref, carry_ref, bias_ref):
        n = pl.program_id(1)

        @pl.when(n == 0)
        def _():
            carry_ref[...] = jnp.zeros_like(carry_ref)
            bias_ref[...] = _attn_bias(dil, (0, B), B)

        masks = _head_masks((B, GROUP_W))

        def per_row(tiles):
            return jnp.concatenate([jnp.max(jnp.where(masks[h], t, _NEG), axis=-1, keepdims=True)
                                    for t in tiles for h in range(N_HEADS)], axis=0)

        dq_acc = carry_ref[...]
        for j in range(per_step):
            own = slice(j * B, (j + 1) * B)
            after = slice((j + 1) * B, (j + 2) * B)
            last = j == per_step - 1
            nxt = lambda cur_ref, nxt_ref: nxt_ref[...] if last else cur_ref[after]
            kb = k_ref[own]
            vb = v_ref[own]
            qs = jnp.concatenate([_stack_heads(qc_ref[own], masks), _stack_heads(nxt(qc_ref, qn_ref), masks)], axis=0)
            dos = jnp.concatenate([_stack_heads(doc_ref[own].astype(MXU_DTYPE), masks),
                                   _stack_heads(nxt(doc_ref, don_ref).astype(MXU_DTYPE), masks)], axis=0)
            lse_rows = per_row([lc_ref[own], nxt(lc_ref, ln_ref)])
            dl_rows = per_row([dc_ref[own], nxt(dc_ref, dn_ref)])
            s = _mm_nt(qs, kb) * scale + bias_ref[...]
            if last:
                row = lax.broadcasted_iota(jnp.int32, s.shape, 0)
                s = jnp.where((n == n_steps - 1) & (row >= N_HEADS * B), _NEG, s)
            p = jnp.exp(s - lse_rows)
            dp = _mm_nt(dos, vb)
            ds = (p * (dp - dl_rows) * scale).astype(MXU_DTYPE)
            dv_ref[own] = _mm_tn(p.astype(MXU_DTYPE), dos)
            dk_ref[own] = _mm_tn(ds, qs)
            dq4 = jnp.dot(ds, kb, preferred_element_type=F32)
            dq_ref[own] = dq_acc + _unstack_heads(dq4, masks)
            dq_acc = _unstack_heads(dq4, masks, N_HEADS)
        carry_ref[...] = dq_acc

    blk = (per_step * B, GROUP_W)
    one = (B, GROUP_W)
    nxt_idx = lambda n: jnp.minimum((n + 1) * per_step, nb - 1)
    zcur = lambda c: pl.BlockSpec(blk, lambda r, n: (n, r * 3 + c))
    znext = lambda c: pl.BlockSpec(one, lambda r, n: (nxt_idx(n), r * 3 + c))
    cur = pl.BlockSpec(blk, lambda r, n: (n, r))
    nxt = pl.BlockSpec(one, lambda r, n: (nxt_idx(n), r))
    return pl.pallas_call(
        body, name=name, grid=(dil, n_steps),
        in_specs=[zcur(0), znext(0), zcur(1), zcur(2), cur, nxt, cur, nxt, cur, nxt],
        out_specs=[cur, cur, cur],
        out_shape=[jax.ShapeDtypeStruct((rows, dil * GROUP_W), F32)] * 3,
        scratch_shapes=[pltpu.VMEM(one, F32), pltpu.VMEM((2 * N_HEADS * B, B), F32)],
        compiler_params=_params(("parallel", "arbitrary")),
    )(qkv, qkv, qkv, qkv, do, do, lse, lse, delta, delta)


def _inproj_bwd(x, g, dxn, dz_abc, dqkv, dz_g, w_in, name):
    S, D = x.shape
    N = w_in.shape[1]
    tm = TM_MM
    n_abc = N_ABC * GROUP_W

    def body(x_ref, g_ref, dxn_ref, dabc_ref, q1, k1, v1, q2, k2, v2, q3, k3, v3, dg_ref, w_ref,
             dx_ref, dz_ref, h_ref, dgn_ref, s4_ref, s16_ref):
        i = pl.program_id(0)

        @pl.when(i == 0)
        def _():
            dgn_ref[...] = jnp.zeros_like(dgn_ref)

        dz_ref[:, 0:n_abc] = dabc_ref[...].astype(MXU_DTYPE)
        for j, parts in enumerate(((q1, q2, q3), (k1, k2, k3), (v1, v2, v3))):
            c0 = n_abc + j * GROUP_W
            _interleave(parts[1], s4_ref, ATTN_DILATIONS[1])
            _interleave(parts[2], s16_ref, ATTN_DILATIONS[2])
            dz_ref[:, c0:c0 + GROUP_W] = (parts[0][...] + _get(s4_ref) + _get(s16_ref)).astype(MXU_DTYPE)
        dz_ref[:, n_abc + 3 * GROUP_W:] = dg_ref[...].astype(MXU_DTYPE)
        dh = _mm_nt(dz_ref[...], w_ref[...])
        xv = x_ref[...]
        r = lax.rsqrt(jnp.mean(xv * xv, axis=-1, keepdims=True) + NORM_EPS)
        xn = xv * r
        gv = g_ref[...]
        h_ref[...] = (xn * gv).astype(MXU_DTYPE)
        dgn_ref[...] += _colsum(dh * xn)
        dn = dh * gv
        dx_ref[...] = dxn_ref[...] + r * (dn - xn * jnp.mean(dn * xn, axis=-1, keepdims=True))

    row = lambda w: pl.BlockSpec((tm, w), lambda i: (i, 0))
    flat = [t for p in dqkv for t in p]
    dil_specs = [_dilated_spec(tm, GROUP_W, dil) for dil in ATTN_DILATIONS for _ in range(3)]
    return pl.pallas_call(
        body, name=name, grid=(S // tm,),
        in_specs=[row(D), pl.BlockSpec((1, D), lambda i: (0, 0)), row(D), row(n_abc)] + dil_specs
                 + [row(GROUP_W), pl.BlockSpec((D, N), lambda i: (0, 0))],
        out_specs=[row(D), row(N), row(D), pl.BlockSpec((1, D), lambda i: (0, 0))],
        out_shape=[jax.ShapeDtypeStruct((S, D), F32), jax.ShapeDtypeStruct((S, N), MXU_DTYPE),
                   jax.ShapeDtypeStruct((S, D), MXU_DTYPE), jax.ShapeDtypeStruct((1, D), F32)],
        scratch_shapes=[_lane_scratch(tm, GROUP_W)] * 2,
        compiler_params=_params(("arbitrary",)),
    )(x, g, dxn, dz_abc, *flat, dz_g, w_in)


def _inproj_wgrad(h, dz, name, n_row_parts=1, row_part=0):
    S = h.shape[0]
    D = h.shape[1] // n_row_parts
    N = dz.shape[1]
    tm = TM_MM
    nj = 2
    cw = N // nj
    per = N_DEV // nj
    n_loc = N // N_DEV

    def body(h_ref, dz_ref, dw_ref, acc_ref):
        i = pl.program_id(1)

        @pl.when(i == 0)
        def _():
            acc_ref[...] = jnp.zeros_like(acc_ref)

        acc_ref[...] += _mm_tn(h_ref[...], dz_ref[...])

        @pl.when(i == S // tm - 1)
        def _():
            for b in range(per):
                dw_ref[b] = acc_ref[:, b * n_loc:(b + 1) * n_loc].astype(dw_ref.dtype)

    return pl.pallas_call(
        body, name=name, grid=(nj, S // tm),
        in_specs=[pl.BlockSpec((tm, D), lambda j, i: (i, row_part)), pl.BlockSpec((tm, cw), lambda j, i: (i, j))],
        out_specs=pl.BlockSpec((per, D, n_loc), lambda j, i: (j, 0, 0)),
        out_shape=jax.ShapeDtypeStruct((N_DEV, D, n_loc), WIRE_DTYPE),
        scratch_shapes=[pltpu.VMEM((D, cw), F32)],
        compiler_params=_params(("parallel", "arbitrary")),
    )(h, dz)


def _my_place():
    return lax.axis_index("x"), lax.axis_index("y"), lax.axis_index("c")


def _peer(x, y, c, k):
    px = 1 - x if k & 4 else x
    py = 1 - y if k & 2 else y
    pc = 1 - c if k & 1 else c
    return (px, py, pc), 4 * px + 2 * py + pc


HBM_SPEC = pl.BlockSpec(memory_space=pltpu.HBM)
SEM_SPEC = pl.BlockSpec(memory_space=pltpu.SEMAPHORE)
SPLIT_EFFECT = pltpu.SideEffectType.DATAFLOW_SIDE_EFFECTING
N_PEERS = N_DEV - 1


def _exchange_copies(srcs, lands, send_sems, recv_sems, whole, arrival):
    x, y, c = _my_place()
    me = 4 * x + 2 * y + c
    copies = []
    for t in range(len(srcs)):
        for k in range(1, N_DEV):
            peer, pidx = _peer(x, y, c, k)
            copies.append(pltpu.make_async_remote_copy(
                src_ref=srcs[t] if whole[t] else srcs[t].at[pidx],
                dst_ref=lands[t].at[pidx if arrival else me], send_sem=send_sems.at[t * N_PEERS + k - 1],
                recv_sem=recv_sems.at[t * N_PEERS + k - 1], device_id=peer, device_id_type=MESH))
    return copies


def _exchange_start(groups, name):
    sizes = [len(g) for g in groups]
    whole = [w for g in groups for _, w in g]
    srcs = [pltpu.with_memory_space_constraint(a, pltpu.HBM) for g in groups for a, _ in g]
    lands = [pltpu.with_memory_space_constraint(lax.empty(((N_DEV,) + a.shape) if w else a.shape, a.dtype), pltpu.HBM)
             for a, w in zip(srcs, whole)]
    n = len(srcs)
    n_g = len(groups)

    def body(*refs):
        src_refs, land_refs = refs[:n], refs[n:2 * n]
        sem_refs = refs[4 * n:4 * n + 2 * n_g]
        token = refs[-1]
        off = 0
        for gi, sz in enumerate(sizes):
            for send in _exchange_copies(src_refs[off:off + sz], land_refs[off:off + sz],
                                         sem_refs[2 * gi], sem_refs[2 * gi + 1], whole[off:off + sz], False):
                send.start()
            off += sz
        token[...] = jnp.zeros_like(token)

    sem_shapes = [pltpu.SemaphoreType.DMA((sz * N_PEERS,)) for sz in sizes for _ in range(2)]
    outs = pl.pallas_call(
        body, name=name,
        in_specs=[HBM_SPEC] * (2 * n),
        out_specs=[HBM_SPEC] * (2 * n) + [SEM_SPEC] * (2 * n_g) + [pl.BlockSpec(memory_space=pltpu.VMEM)],
        out_shape=[pltpu.HBM(a.shape, a.dtype) for a in srcs + lands] + sem_shapes
                  + [jax.ShapeDtypeStruct((SUBLANES, LANES), F32)],
        input_output_aliases={i: i for i in range(2 * n)},
        compiler_params=pltpu.CompilerParams(has_side_effects=SPLIT_EFFECT),
    )(*srcs, *lands)
    handles, off = [], 0
    for gi, sz in enumerate(sizes):
        handles.append((outs[2 * n + 2 * gi], outs[2 * n + 2 * gi + 1], outs[off:off + sz], outs[n + off:n + off + sz],
                        whole[off:off + sz]))
        off += sz
    return handles, outs[-1]


def _exchange_wait(handle, after, name):
    send_sems, recv_sems, srcs, lands, whole = handle
    n = len(srcs)

    def body(*refs):
        src_refs, land_refs = refs[:n], refs[n:2 * n]
        for send in _exchange_copies(src_refs, land_refs, refs[2 * n], refs[2 * n + 1], whole, False):
            send.wait_send()
        for arrival in _exchange_copies(src_refs, land_refs, refs[2 * n], refs[2 * n + 1], whole, True):
            arrival.wait_recv()

    outs = pl.pallas_call(
        body, name=name,
        in_specs=[HBM_SPEC] * (2 * n) + [SEM_SPEC, SEM_SPEC, pl.BlockSpec(memory_space=pl.ANY)],
        out_specs=[HBM_SPEC] * (2 * n),
        out_shape=[pltpu.HBM(a.shape, a.dtype) for a in list(srcs) + list(lands)],
        input_output_aliases={i: i for i in range(2 * n)},
        compiler_params=pltpu.CompilerParams(has_side_effects=SPLIT_EFFECT),
    )(*srcs, *lands, send_sems, recv_sems, after)
    x, y, c = _my_place()
    me = 4 * x + 2 * y + c
    own = [s[None] if w else lax.dynamic_slice_in_dim(s, me, 1, axis=0) for s, w in zip(outs[:n], whole)]
    return [lax.dynamic_update_slice_in_dim(ld, o, me, axis=0) for ld, o in zip(outs[n:], own)]


def _sum_slots(parts, name):
    n = len(parts)

    def body(*refs):
        for p_ref, o_ref in zip(refs[:n], refs[n:]):
            acc = p_ref[0]
            for j in range(1, N_DEV):
                acc = acc + p_ref[j]
            o_ref[...] = acc

    vm = pl.BlockSpec(memory_space=pltpu.VMEM)
    return pl.pallas_call(
        body, name=name, in_specs=[vm] * n, out_specs=[vm] * n,
        out_shape=[jax.ShapeDtypeStruct(p.shape[1:], F32) for p in parts],
        compiler_params=pltpu.CompilerParams(vmem_limit_bytes=VMEM_LIMIT),
    )(*parts)


def _adamw_math(w, g, m, v):
    m = ADAM_B1 * m + (1.0 - ADAM_B1) * g
    v = ADAM_B2 * v + (1.0 - ADAM_B2) * (g * g)
    m_hat = m / (1.0 - ADAM_B1 ** ADAM_STEP)
    v_hat = v / (1.0 - ADAM_B2 ** ADAM_STEP)
    delta = -ADAM_LR * (m_hat / (jnp.sqrt(v_hat) + ADAM_EPS) + ADAM_WD * w)
    return delta, m, v


def _adamw_summed(parts, w, m, v, tr, name):
    depth, R, C = w.shape

    def body(*refs):
        p_refs = refs[:depth]
        w_ref, m_ref, v_ref, g_ref, d_ref, nm_ref, nv_ref = refs[depth:]
        lay = pl.program_id(0)
        for l in range(depth):
            @pl.when(lay == l)
            def _(p_ref=p_refs[l]):
                g = p_ref[0].astype(F32)
                for j in range(1, N_DEV):
                    g = g + p_ref[j].astype(F32)
                g_ref[0] = g
        d_ref[0], nm_ref[0], nv_ref[0] = _adamw_math(w_ref[0], g_ref[0], m_ref[0], v_ref[0])

    part_spec = lambda l: pl.BlockSpec((N_DEV, tr, C), lambda lay, i: (0, jnp.where(lay == l, i, 0), 0))
    row = pl.BlockSpec((1, tr, C), lambda lay, i: (lay, i, 0))
    return pl.pallas_call(
        body, name=name, grid=(depth, R // tr),
        in_specs=[part_spec(l) for l in range(depth)] + [row, row, row],
        out_specs=[row] * 4, out_shape=[jax.ShapeDtypeStruct((depth, R, C), F32)] * 4,
        compiler_params=_params(("arbitrary", "arbitrary")),
    )(*parts, w, m, v)


def _adamw_small(w, g, m, v, name):
    def body(w_ref, g_ref, m_ref, v_ref, d_ref, nm_ref, nv_ref):
        d_ref[...], nm_ref[...], nv_ref[...] = _adamw_math(w_ref[...], g_ref[...], m_ref[...], v_ref[...])

    vm = pl.BlockSpec(memory_space=pltpu.VMEM)
    return pl.pallas_call(
        body, name=name, in_specs=[vm] * 4, out_specs=[vm] * 3,
        out_shape=[jax.ShapeDtypeStruct(w.shape, F32)] * 3,
        compiler_params=pltpu.CompilerParams(vmem_limit_bytes=VMEM_LIMIT),
    )(w, g, m, v)


def _pack(arrays):
    flat = jnp.concatenate([a.reshape(-1) for a in arrays])
    pad = (-flat.shape[0]) % (SUBLANES * LANES)
    return jnp.pad(flat, (0, pad)).reshape(-1, LANES)


def _unpack(buf, like):
    flat = buf.reshape(-1)
    out, off = [], 0
    for a in like:
        out.append(flat[off:off + a.size].reshape(a.shape))
        off += a.size
    return out


def _block_diag(w):
    eye = jnp.eye(N_HEADS, dtype=w.dtype)
    return jnp.einsum('hij,hk->hikj', w, eye).reshape(GROUP_W, GROUP_W)


def _diag_blocks(w):
    return jnp.einsum('hihj->hij', w.reshape(N_HEADS, HEAD_DIM, N_HEADS, HEAD_DIM))


def _pad_rows(a):
    return jnp.pad(a, ((0, SUBLANES - a.shape[0]), (0, 0)))


def _mixer_params(l, conv_a_w, conv_r_w, conv_r_b, lru_wa, lru_ba, lru_wx, lru_bx, lru_lambda, gmlp_norm_g,
                  gmlp_ws, gmlp_bs):
    tril = jnp.tril(jnp.ones((GMLP_CHUNK, GMLP_CHUNK), dtype=bool))
    vec = jnp.stack([conv_r_b[l], lru_ba[l], lru_bx[l], lru_lambda[l], gmlp_norm_g[l]])
    return {
        "wA": _pad_rows(conv_a_w[l]), "wR": _pad_rows(conv_r_w[l]), "vec": _pad_rows(vec),
        "wa": _block_diag(lru_wa[l]).astype(MXU_DTYPE), "wx": _block_diag(lru_wx[l]).astype(MXU_DTYPE),
        "ws": jnp.where(tril[None], gmlp_ws[l], 0.0).astype(MXU_DTYPE),
        "bs": jnp.repeat(jnp.transpose(gmlp_bs[l]), HEAD_DIM, axis=1),
    }


MIXER_NAMES = ("conv_a_w", "conv_r_w", "conv_r_b", "lru_wa", "lru_ba", "lru_wx", "lru_bx", "lru_lambda",
               "gmlp_norm_g", "gmlp_ws", "gmlp_bs")
SMALL_NAMES = ("norm_g",) + MIXER_NAMES + ("final_g",)


def _local_step(x, loss_target, norm_g, get_w_in, get_w_out, emit_early, emit_late, conv_a_w, conv_r_w, conv_r_b,
                lru_wa, lru_ba, lru_wx, lru_bx, lru_lambda, gmlp_norm_g, gmlp_ws, gmlp_bs, final_g):
    depth = norm_g.shape[0]
    D = x.shape[1]
    small = (conv_a_w, conv_r_w, conv_r_b, lru_wa, lru_ba, lru_wx, lru_bx, lru_lambda, gmlp_norm_g, gmlp_ws, gmlp_bs)
    saved = []
    for l in range(depth):
        mp = _mixer_params(l, *small)
        w_in_l = get_w_in(l, x)
        z, z_g, *qkv = _norm_inproj(x, norm_g[l].reshape(1, D), w_in_l, f"norm_inproj_{l}")
        y_abc, hs = _mix_fwd(z, mp, f"mix_fwd_{l}")
        attn = [_attn_fwd(qkv[p], dil, f"attn_fwd_d{dil}_{l}") for p, dil in enumerate(ATTN_DILATIONS)]
        w_out_l = get_w_out(l, y_abc)
        x_new, y, o, *lse = _outproj(x, z_g, y_abc, attn, w_out_l, f"outproj_{l}")
        saved.append((x, z, z_g, qkv, hs, y, o, lse, mp, w_in_l, w_out_l))
        x = x_new
    dx, loss, d_final_g = _loss_head(x, final_g.reshape(1, D), loss_target, "loss_head")
    token = None
    for l in reversed(range(depth)):
        x_l, z, z_g, qkv, hs, y, o, lse, mp, w_in_l, w_out_l = saved[l]
        if token is not None:
            mp = dict(mp, vec=mp["vec"] + token[0, 0])
        dy, dw_out = _outproj_bwd(dx, y, w_out_l, f"outproj_bwd_{l}")
        (dz_abc, dz_g, do1, do4, do16, dl1, dl4, dl16, dwA, dwR, dvec, dwa, dwx, dws, dbs) = _mix_bwd(
            z, z_g, dy, hs, o, mp, f"mix_bwd_{l}")
        token = emit_early(l, dw_out, [
            dwA[:conv_a_w.shape[1]], dwR[:conv_r_w.shape[1]], dvec[0], _diag_blocks(dwa), dvec[1], _diag_blocks(dwx),
            dvec[2], dvec[3], dvec[4], dws, jnp.transpose(dbs[:, :N_HEADS])])
        g_row = norm_g[l].reshape(1, D)
        if token is not None:
            g_row = g_row + token[0, 0]
        dqkv = [_attn_bwd(qkv[p], do, lse[p], dl, dil, f"attn_bwd_d{dil}_{l}")
                for p, (dil, do, dl) in enumerate(zip(ATTN_DILATIONS, (do1, do4, do16), (dl1, dl4, dl16)))]
        dx, dz, h, dng = _inproj_bwd(x_l, g_row, dx, dz_abc, dqkv, dz_g, w_in_l, f"inproj_bwd_{l}")
        n_parts = WGRAD_TAIL_PARTS if l == 0 else 1
        for part in range(n_parts):
            dw_in = _inproj_wgrad(h, dz, f"inproj_wgrad_{l}_{part}", n_parts, part)
            norm_grads = [dng[0]] + ([d_final_g[0]] if l == depth - 1 else [])
            token = emit_late(l, part, dw_in, norm_grads if part == 0 else [])
    return loss[0, 0], dx
WEIGHT_NAMES = ("norm_g", "w_in", "conv_a_w", "conv_r_w", "conv_r_b", "lru_wa", "lru_ba", "lru_wx", "lru_bx",
                "lru_lambda", "gmlp_norm_g", "gmlp_ws", "gmlp_bs", "w_out", "final_g")


def kernel(x, norm_g, w_in, conv_a_w, conv_r_w, conv_r_b, lru_wa, lru_ba, lru_wx, lru_bx, lru_lambda, gmlp_norm_g, gmlp_ws, gmlp_bs, w_out, final_g, loss_target, m_norm_g, m_w_in, m_conv_a_w, m_conv_r_w, m_conv_r_b, m_lru_wa, m_lru_ba, m_lru_wx, m_lru_bx, m_lru_lambda, m_gmlp_norm_g, m_gmlp_ws, m_gmlp_bs, m_w_out, m_final_g, v_norm_g, v_w_in, v_conv_a_w, v_conv_r_w, v_conv_r_b, v_lru_wa, v_lru_ba, v_lru_wx, v_lru_bx, v_lru_lambda, v_gmlp_norm_g, v_gmlp_ws, v_gmlp_bs, v_w_out, v_final_g):
    w = dict(norm_g=norm_g, w_in=w_in, conv_a_w=conv_a_w, conv_r_w=conv_r_w, conv_r_b=conv_r_b, lru_wa=lru_wa,
             lru_ba=lru_ba, lru_wx=lru_wx, lru_bx=lru_bx, lru_lambda=lru_lambda, gmlp_norm_g=gmlp_norm_g,
             gmlp_ws=gmlp_ws, gmlp_bs=gmlp_bs, w_out=w_out, final_g=final_g)
    m = dict(norm_g=m_norm_g, w_in=m_w_in, conv_a_w=m_conv_a_w, conv_r_w=m_conv_r_w, conv_r_b=m_conv_r_b,
             lru_wa=m_lru_wa, lru_ba=m_lru_ba, lru_wx=m_lru_wx, lru_bx=m_lru_bx, lru_lambda=m_lru_lambda,
             gmlp_norm_g=m_gmlp_norm_g, gmlp_ws=m_gmlp_ws, gmlp_bs=m_gmlp_bs, w_out=m_w_out, final_g=m_final_g)
    v = dict(norm_g=v_norm_g, w_in=v_w_in, conv_a_w=v_conv_a_w, conv_r_w=v_conv_r_w, conv_r_b=v_conv_r_b,
             lru_wa=v_lru_wa, lru_ba=v_lru_ba, lru_wx=v_lru_wx, lru_bx=v_lru_bx, lru_lambda=v_lru_lambda,
             gmlp_norm_g=v_gmlp_norm_g, gmlp_ws=v_gmlp_ws, gmlp_bs=v_gmlp_bs, w_out=v_w_out, final_g=v_final_g)
    depth, D, n_loc = w_in.shape
    e_loc = w_out.shape[1]
    cx, cy, cc = _my_place()
    me = 4 * cx + 2 * cy + cc

    w_in_w, w_out_w = w_in.astype(MXU_DTYPE), w_out.astype(MXU_DTYPE)
    c_loc = conv_a_w.shape[2]
    taps = (conv_a_w, conv_r_w)
    groups = [[(w_in_w[0], True), (_pack(taps), True)], [(w_out_w[0], True)]]
    groups += [[(w_in_w[l], True), (w_out_w[l], True)] for l in range(1, depth)]
    gathers, _ = _exchange_start(groups, "gather_start")
    full_in = lambda g: jnp.transpose(g, (1, 0, 2)).reshape(D, N_DEV * n_loc)
    full_out = lambda g: g.reshape(N_DEV * e_loc, D)

    g_in0, g_taps = _exchange_wait(gathers[0], x, "gather_wait_in_0")
    g_taps = g_taps.reshape(N_DEV, -1)
    conv_full, off = [], 0
    for a in taps:
        part = g_taps[:, off:off + a.size].reshape((N_DEV,) + a.shape)
        conv_full.append(jnp.transpose(part, (1, 2, 0, 3)).reshape(a.shape[:2] + (N_DEV * c_loc,)))
        off += a.size
    conv_a_full, conv_r_full = conv_full
    later = {}

    def get_w_in(l, after):
        if l == 0:
            return full_in(g_in0)
        g_in, later[l] = _exchange_wait(gathers[l + 1], after, f"gather_wait_{l}")
        return full_in(g_in)

    def get_w_out(l, after):
        if l == 0:
            return full_out(_exchange_wait(gathers[1], after, "gather_wait_out_0")[0])
        return full_out(later[l])

    early, late, last_token = {}, {}, [None]

    def emit_early(l, dw_out, mixer_grads):
        handles, token = _exchange_start(
            [[(dw_out.reshape(N_DEV, e_loc, D), False), (_pack(mixer_grads), True)]], f"early_start_{l}")
        early[l] = (handles[0], mixer_grads)
        return token

    def emit_late(l, part, dw_in, norm_grads):
        group = [(dw_in, False)] + ([(_pack(norm_grads), True)] if norm_grads else [])
        handles, token = _exchange_start([group], f"late_start_{l}_{part}")
        late.setdefault(l, []).append((handles[0], norm_grads))
        last_token[0] = token
        return token

    loss, grad_x = _local_step(
        x[0], loss_target[0], norm_g, get_w_in, get_w_out, emit_early, emit_late, conv_a_full, conv_r_full, conv_r_b,
        lru_wa, lru_ba, lru_wx, lru_bx, lru_lambda, gmlp_norm_g, gmlp_ws, gmlp_bs, final_g)
    loss = lax.psum(loss, ("x", "y", "c"))

    slabs, r_out, small_parts = {}, {}, []
    for l in reversed(range(depth)):
        r_out[l], r_mix = _exchange_wait(early[l][0], last_token[0], f"early_wait_{l}")
        slabs[l] = []
        for part, (handle, _) in enumerate(late[l][:-1] if l == 0 and len(late[l]) > 1 else late[l]):
            slab, *r_norm = _exchange_wait(handle, last_token[0], f"late_wait_{l}_{part}")
            slabs[l].append(slab)
            if part == 0:
                small_parts += [r_mix, r_norm[0]]
    big = {"w_out": _adamw_summed([r_out[l] for l in range(depth)], w_out, m_w_out, v_w_out, 128, "adamw_w_out")}

    sums = _sum_slots(small_parts, "sum_small_grads")
    by_layer = {}
    for i, l in enumerate(reversed(range(depth))):
        mix = _unpack(sums[2 * i], early[l][1])
        nrm = _unpack(sums[2 * i + 1], late[l][0][1])
        by_layer[l] = dict(zip(MIXER_NAMES, mix), norm_g=nrm[0])
        if l == depth - 1:
            g_final = nrm[1]
    g_small = {k: jnp.stack([by_layer[l][k] for l in range(depth)]) for k in ("norm_g",) + MIXER_NAMES}
    g_small["final_g"] = g_final
    for k in ("conv_a_w", "conv_r_w"):
        g_small[k] = lax.dynamic_slice_in_dim(g_small[k], me * c_loc, c_loc, axis=2)
    packs = [_pack([d[k] for k in SMALL_NAMES]) for d in (w, g_small, m, v)]
    res = _adamw_small(*packs, "adamw_small")
    like = [w[k] for k in SMALL_NAMES]
    d_s, m_s, v_s = (dict(zip(SMALL_NAMES, _unpack(r, like))) for r in res)

    if len(late[0]) > 1:
        slabs[0] += _exchange_wait(late[0][-1][0], res[0], f"late_wait_0_{len(late[0]) - 1}")
    r_in = [s[0] if len(s) == 1 else jnp.concatenate(s, axis=1) for s in (slabs[l] for l in range(depth))]
    big["w_in"] = _adamw_summed(r_in, w_in, m_w_in, v_w_in, 512, "adamw_w_in")

    grad, delta, new_m, new_v = {}, {}, {}, {}
    for k in WEIGHT_NAMES:
        if k in big:
            grad[k], delta[k], new_m[k], new_v[k] = big[k]
        else:
            grad[k], delta[k], new_m[k], new_v[k] = g_small[k], d_s[k], m_s[k], v_s[k]
    return (loss, grad_x[None], *[grad[k] for k in WEIGHT_NAMES], *[delta[k] for k in WEIGHT_NAMES],
            *[new_m[k] for k in WEIGHT_NAMES], *[new_v[k] for k in WEIGHT_NAMES])
```

```python
import functools
import math

import jax
import jax.numpy as jnp
from jax import lax
from jax.experimental import pallas as pl
from jax.experimental.pallas import tpu as pltpu

F32 = jnp.float32
MXU_DTYPE = jnp.bfloat16
WIRE_DTYPE = jnp.bfloat16
MESH = pl.DeviceIdType.MESH

N_DEV = 8
GROUP_W = 256
N_HEADS = 4
HEAD_DIM = 64
N_CHUNKS = 13
N_ABC = 9
GMLP_CHUNK = 128
ATTN_BLOCK = 128
ATTN_BLOCKS_PER_STEP = 4
ATTN_DILATIONS = (1, 4, 16)
NORM_EPS = 1e-6
RG_C = 8.0
SUBLANES = 8
LANES = 128
VMEM_LIMIT = 56 * 1024 * 1024

ADAM_LR = 0.001
ADAM_B1 = 0.9
ADAM_B2 = 0.999
ADAM_EPS = 1e-08
ADAM_WD = 0.01
ADAM_STEP = 10

TM_MIX = 512
TM_MM = 512
TM_INPROJ = 1024


def _params(sem, vmem=VMEM_LIMIT):
    return pltpu.CompilerParams(dimension_semantics=sem, vmem_limit_bytes=vmem)


def _mm(a, b):
    return jnp.dot(a.astype(MXU_DTYPE), b.astype(MXU_DTYPE), preferred_element_type=F32)


def _mm_tn(a, b):
    return lax.dot_general(a.astype(MXU_DTYPE), b.astype(MXU_DTYPE), (((0,), (0,)), ((), ())),
                           preferred_element_type=F32)


def _mm_nt(a, b):
    return lax.dot_general(a.astype(MXU_DTYPE), b.astype(MXU_DTYPE), (((1,), (1,)), ((), ())),
                           preferred_element_type=F32)


def _sigmoid(x):
    return 0.5 * jnp.tanh(0.5 * x) + 0.5


def _silu_and_grad(x):
    s = _sigmoid(x)
    return x * s, s * (1.0 + x * (1.0 - s))


_GELU_K = math.sqrt(2.0 / math.pi)
_GELU_C = 0.044715


def _gelu_and_grad(x):
    x2 = x * x
    t = jnp.tanh(_GELU_K * (x + _GELU_C * x * x2))
    val = 0.5 * x * (1.0 + t)
    grad = 0.5 * (1.0 + t) + 0.5 * x * (1.0 - t * t) * (_GELU_K * (1.0 + 3.0 * _GELU_C * x2))
    return val, grad


def _gelu(x):
    return 0.5 * x * (1.0 + jnp.tanh(_GELU_K * (x + _GELU_C * x * x * x)))


def _expm1_nonpos(u):
    poly = 1.0 / math.factorial(9)
    for k in range(8, 0, -1):
        poly = poly * u + 1.0 / math.factorial(k)
    return jnp.where(u > -0.25, poly * u, jnp.exp(u) - 1.0)


def _softplus(x):
    return jnp.maximum(x, 0.0) + jnp.log(1.0 + jnp.exp(-jnp.abs(x)))


def _shift_down(t, halo, k):
    rolled = pltpu.roll(t, k, 0)
    hr = pltpu.roll(halo, k, 0)
    row = lax.broadcasted_iota(jnp.int32, halo.shape, 0)
    first = jnp.where(row < k, hr, rolled[0:SUBLANES])
    return jnp.concatenate([first, rolled[SUBLANES:]], axis=0)


def _shift_up(t, nxt, k):
    tm = t.shape[0]
    rolled = pltpu.roll(t, tm - k, 0)
    nr = pltpu.roll(nxt, SUBLANES - k, 0)
    row = lax.broadcasted_iota(jnp.int32, nxt.shape, 0)
    last = jnp.where(row >= SUBLANES - k, nr, rolled[tm - SUBLANES:tm])
    return jnp.concatenate([rolled[:tm - SUBLANES], last], axis=0)


def _scan_fwd(a, b):
    tm = a.shape[0]
    row = lax.broadcasted_iota(jnp.int32, a.shape, 0)
    s = 1
    while s < tm:
        a_s = pltpu.roll(a, s, 0)
        b_s = pltpu.roll(b, s, 0)
        m = row >= s
        b = jnp.where(m, a * b_s + b, b)
        a = jnp.where(m, a * a_s, a)
        s *= 2
    return a, b


def _scan_rev(a, g):
    tm = a.shape[0]
    row = lax.broadcasted_iota(jnp.int32, a.shape, 0)
    s = 1
    while s < tm:
        a_s = pltpu.roll(a, tm - s, 0)
        g_s = pltpu.roll(g, tm - s, 0)
        m = row < tm - s
        g = jnp.where(m, g + a * g_s, g)
        a = jnp.where(m, a * a_s, a)
        s *= 2
    return g


def _group_rows(scr_ref, row, n_groups):
    return jnp.concatenate([scr_ref[pl.ds(c, 1), pl.ds(row, n_groups, stride=SUBLANES), :][0]
                            for c in range(scr_ref.shape[0])], axis=1)


def _spread_rows(rows_ref, n_groups, w):
    return jnp.concatenate([jnp.broadcast_to(rows_ref[g:g + 1, :], (SUBLANES, w)) for g in range(n_groups)], axis=0)


def _scan_groups(a, b, reverse):
    tm, w = a.shape
    shape3 = (tm // SUBLANES, SUBLANES, w)
    a3, b3 = a.reshape(shape3), b.reshape(shape3)
    sub = lax.broadcasted_iota(jnp.int32, shape3, 1)
    s = 1
    while s < SUBLANES:
        shift = SUBLANES - s if reverse else s
        a_s = pltpu.roll(a3, shift, 1)
        b_s = pltpu.roll(b3, shift, 1)
        m = (sub < SUBLANES - s) if reverse else (sub >= s)
        b3 = jnp.where(m, a3 * b_s + b3, b3)
        a3 = jnp.where(m, a3 * a_s, a3)
        s *= 2
    return a3.reshape(tm, w), b3.reshape(tm, w)


def _scan_fwd_tile(a, b, h_in, sa_ref, sb_ref, sc_ref):
    tm, w = a.shape
    n_groups = tm // SUBLANES
    a_loc, b_loc = _scan_groups(a, b, False)
    _put(sa_ref, a_loc)
    _put(sb_ref, b_loc)
    a_end, b_end = _scan_fwd(_group_rows(sa_ref, SUBLANES - 1, n_groups), _group_rows(sb_ref, SUBLANES - 1, n_groups))
    h_end = b_end + a_end * h_in
    sc_ref[...] = _shift_down(h_end, jnp.broadcast_to(h_in, (SUBLANES, w)), 1)
    return b_loc + a_loc * _spread_rows(sc_ref, n_groups, w), h_end


def _scan_rev_tile(a, g, sa_ref, sb_ref, sc_ref):
    tm, w = a.shape
    n_groups = tm // SUBLANES
    a_loc, g_loc = _scan_groups(a, g, True)
    _put(sa_ref, a_loc)
    _put(sb_ref, g_loc)
    d_first = _scan_rev(_group_rows(sa_ref, 0, n_groups), _group_rows(sb_ref, 0, n_groups))
    sc_ref[...] = _shift_up(d_first, jnp.zeros((SUBLANES, w), F32), 1)
    return g_loc + a_loc * _spread_rows(sc_ref, n_groups, w)


def _lane_scratch(tm, w):
    return pltpu.VMEM((w // LANES, tm, LANES), F32)


def _put(scr_ref, val):
    for c in range(scr_ref.shape[0]):
        scr_ref[c] = val[:, c * LANES:(c + 1) * LANES].astype(F32)


def _get(scr_ref):
    return jnp.concatenate([scr_ref[c] for c in range(scr_ref.shape[0])], axis=1)


def _deinterleave(src_ref, dst_ref, dil):
    nc, tm, _ = src_ref.shape
    w = nc * LANES
    for r in range(dil):
        for c in range(nc):
            piece = src_ref[pl.ds(c, 1), pl.ds(r, tm // dil, stride=dil), :][0] if dil > 1 else src_ref[c]
            dst_ref[:, r * w + c * LANES:r * w + (c + 1) * LANES] = piece.astype(dst_ref.dtype)


def _interleave(src_ref, dst_ref, dil):
    nc, tm, _ = dst_ref.shape
    w = nc * LANES
    for r in range(dil):
        for c in range(nc):
            dst_ref[pl.ds(c, 1), pl.ds(r, tm // dil, stride=dil), :] = (
                src_ref[:, r * w + c * LANES:r * w + (c + 1) * LANES].astype(F32)[None])


def _dilated_spec(tm, w, dil, index=lambda i: i):
    return pl.BlockSpec((tm // dil, dil * w), lambda i: (index(i), 0))


def _dilated_shape(S, w, dil, dtype):
    return jax.ShapeDtypeStruct((S // dil, dil * w), dtype)


def _head_masks(shape):
    lane = lax.broadcasted_iota(jnp.int32, shape, 1)
    return [(lane >= h * HEAD_DIM) & (lane < (h + 1) * HEAD_DIM) for h in range(N_HEADS)]


def _colsum(v):
    return jnp.sum(v, axis=0, keepdims=True)


def _norm_inproj(x, g, w, name):
    S, D = x.shape
    N = w.shape[1]
    tm = TM_INPROJ
    n_abc = N_ABC * GROUP_W
    n_qkv = 3 * GROUP_W

    def body(x_ref, g_ref, w_ref, zabc_ref, zg_ref, q1_ref, q4_ref, q16_ref, qkv_ref):
        xv = x_ref[...]
        r = lax.rsqrt(jnp.mean(xv * xv, axis=-1, keepdims=True) + NORM_EPS)
        h = ((xv * r) * g_ref[...]).astype(MXU_DTYPE)
        zabc_ref[...] = jnp.dot(h, w_ref[:, 0:n_abc], preferred_element_type=F32)
        _put(qkv_ref, jnp.dot(h, w_ref[:, n_abc:n_abc + n_qkv], preferred_element_type=F32))
        zg_ref[...] = jnp.dot(h, w_ref[:, n_abc + n_qkv:], preferred_element_type=F32)
        for dil, ref in zip(ATTN_DILATIONS, (q1_ref, q4_ref, q16_ref)):
            _deinterleave(qkv_ref, ref, dil)

    row = lambda wd: pl.BlockSpec((tm, wd), lambda i: (i, 0))
    return pl.pallas_call(
        body, name=name, grid=(S // tm,),
        in_specs=[row(D), pl.BlockSpec((1, D), lambda i: (0, 0)),
                  pl.BlockSpec((D, N), lambda i: (0, 0), pipeline_mode=pl.Buffered(1))],
        out_specs=[row(n_abc), row(GROUP_W)] + [_dilated_spec(tm, n_qkv, dil) for dil in ATTN_DILATIONS],
        out_shape=[jax.ShapeDtypeStruct((S, n_abc), F32), jax.ShapeDtypeStruct((S, GROUP_W), F32)]
                  + [_dilated_shape(S, n_qkv, dil, MXU_DTYPE) for dil in ATTN_DILATIONS],
        scratch_shapes=[_lane_scratch(tm, n_qkv)],
        compiler_params=_params(("parallel",)),
    )(x, g, w)


def _conv_a(z_of, halo_of, w_ref):
    p = z_of(2) * z_of(0)
    p_h = halo_of(2) * halo_of(0)
    cv = w_ref[2:3, :] * p + w_ref[1:2, :] * _shift_down(p, p_h, 1) + w_ref[0:1, :] * _shift_down(p, p_h, 2)
    return p, p_h, cv


def _lru_gates(z_of, halo_of, wr_ref, vec_ref, wa_ref, wx_ref):
    rx = z_of(4)
    rx_h = halo_of(4)
    sh = [rx, _shift_down(rx, rx_h, 1), _shift_down(rx, rx_h, 2), _shift_down(rx, rx_h, 3)]
    xc = (wr_ref[3:4, :] * sh[0] + wr_ref[2:3, :] * sh[1] + wr_ref[1:2, :] * sh[2]
          + wr_ref[0:1, :] * sh[3] + vec_ref[0:1, :])
    ga = _sigmoid(jnp.dot(xc.astype(MXU_DTYPE), wa_ref[...], preferred_element_type=F32) + vec_ref[1:2, :])
    gi = _sigmoid(jnp.dot(xc.astype(MXU_DTYPE), wx_ref[...], preferred_element_type=F32) + vec_ref[2:3, :])
    sp = _softplus(-vec_ref[3:4, :])
    log_a = (-RG_C * ga) * sp
    a = jnp.exp(log_a)
    mult = jnp.sqrt(-_expm1_nonpos(2.0 * log_a))
    return xc, sh, ga, gi, a, mult, sp


def _gmlp_fwd(z_of, vec_ref, ws_ref, bs_ref, tm):
    u = _gelu(z_of(6))
    gv = _gelu(z_of(7))
    rr = lax.rsqrt(jnp.mean(gv * gv, axis=-1, keepdims=True) + NORM_EPS)
    vn = (gv * rr) * vec_ref[4:5, :]
    masks = _head_masks((GMLP_CHUNK, GROUP_W))
    parts = []
    for c in range(tm // GMLP_CHUNK):
        vc = vn[c * GMLP_CHUNK:(c + 1) * GMLP_CHUNK].astype(MXU_DTYPE)
        acc = bs_ref[...]
        for h in range(N_HEADS):
            acc = acc + jnp.where(masks[h], jnp.dot(ws_ref[h], vc, preferred_element_type=F32), 0.0)
        parts.append(acc)
    return u, gv, rr, vn, jnp.concatenate(parts, axis=0)


def _mix_specs(tm, S, order):
    const2 = lambda shape: pl.BlockSpec(shape, lambda i: (0, 0))
    return [const2((SUBLANES, GROUP_W)), const2((SUBLANES, GROUP_W)), const2((SUBLANES, GROUP_W)),
            const2((GROUP_W, GROUP_W)), const2((GROUP_W, GROUP_W)),
            pl.BlockSpec((N_HEADS, GMLP_CHUNK, GMLP_CHUNK), lambda i: (0, 0, 0)),
            const2((GMLP_CHUNK, GROUP_W))]


def _mix_fwd(z, mp, name):
    S = z.shape[0]
    tm = TM_MIX
    hb = tm // SUBLANES
    wcols = N_ABC * GROUP_W

    def body(z_ref, zh_ref, wA_ref, wR_ref, vec_ref, wa_ref, wx_ref, ws_ref, bs_ref, y_ref, h_ref, carry_ref,
             sa_ref, sb_ref, sc_ref):
        i = pl.program_id(0)

        @pl.when(i == 0)
        def _():
            carry_ref[...] = jnp.zeros_like(carry_ref)

        not_first = i > 0
        z_of = lambda c: z_ref[:, c * GROUP_W:(c + 1) * GROUP_W]
        halo_of = lambda c: jnp.where(not_first, zh_ref[:, c * GROUP_W:(c + 1) * GROUP_W], 0.0)

        _, _, cv = _conv_a(z_of, halo_of, wA_ref)
        y_ref[:, 0:GROUP_W] = (z_of(1) * cv * _silu_and_grad(z_of(3))[0]).astype(y_ref.dtype)

        xc, _, _, gi, a, mult, _ = _lru_gates(z_of, halo_of, wR_ref, vec_ref, wa_ref, wx_ref)
        b = mult * (gi * xc)
        h, h_end = _scan_fwd_tile(a, b, carry_ref[SUBLANES - 1:SUBLANES, :], sa_ref, sb_ref, sc_ref)
        h_ref[...] = h
        carry_ref[...] = h_end[hb - SUBLANES:hb]
        y_ref[:, GROUP_W:2 * GROUP_W] = (h * _silu_and_grad(z_of(5))[0]).astype(y_ref.dtype)

        u, _, _, _, sp = _gmlp_fwd(z_of, vec_ref, ws_ref, bs_ref, tm)
        y_ref[:, 2 * GROUP_W:3 * GROUP_W] = (u * sp * _silu_and_grad(z_of(8))[0]).astype(y_ref.dtype)

    return pl.pallas_call(
        body, name=name, grid=(S // tm,),
        in_specs=[pl.BlockSpec((tm, wcols), lambda i: (i, 0)),
                  pl.BlockSpec((SUBLANES, wcols), lambda i: (jnp.maximum(i * hb - 1, 0), 0))]
                 + _mix_specs(tm, S, "fwd"),
        out_specs=[pl.BlockSpec((tm, 3 * GROUP_W), lambda i: (i, 0)),
                   pl.BlockSpec((tm, GROUP_W), lambda i: (i, 0))],
        out_shape=[jax.ShapeDtypeStruct((S, 3 * GROUP_W), MXU_DTYPE), jax.ShapeDtypeStruct((S, GROUP_W), F32)],
        scratch_shapes=[pltpu.VMEM((SUBLANES, GROUP_W), F32), _lane_scratch(tm, GROUP_W), _lane_scratch(tm, GROUP_W),
                        pltpu.VMEM((hb, GROUP_W), F32)],
        compiler_params=_params(("arbitrary",)),
    )(z, z, mp["wA"], mp["wR"], mp["vec"], mp["wa"], mp["wx"], mp["ws"], mp["bs"])


_NEG = -1e30


def _slope(h):
    return 2.0 ** (-8.0 * (h + 1) / N_HEADS)


def _attn_bias(dil, offsets, n_keys):
    shape = (ATTN_BLOCK, n_keys)
    qi = lax.broadcasted_iota(jnp.int32, shape, 0)
    ki = lax.broadcasted_iota(jnp.int32, shape, 1)
    blocks = []
    for f in offsets:
        delta = qi + f - ki
        valid = (delta >= 0) & (delta <= ATTN_BLOCK)
        dist = (delta * dil).astype(F32)
        for h in range(N_HEADS):
            blocks.append(jnp.where(valid, -_slope(h) * dist, _NEG))
    return jnp.concatenate(blocks, axis=0)


def _stack_heads(t, masks):
    return jnp.concatenate([jnp.where(m, t, jnp.zeros_like(t)) for m in masks], axis=0)


def _unstack_heads(t4, masks, base=0):
    out = t4[base * ATTN_BLOCK:(base + 1) * ATTN_BLOCK]
    for h in range(1, N_HEADS):
        out = jnp.where(masks[h], t4[(base + h) * ATTN_BLOCK:(base + h + 1) * ATTN_BLOCK], out)
    return out


def _attn_fwd(qkv, dil, name):
    rows = qkv.shape[0]
    nb = rows // ATTN_BLOCK
    scale = 1.0 / math.sqrt(HEAD_DIM)
    B = ATTN_BLOCK
    per_step = ATTN_BLOCKS_PER_STEP

    def body(q_ref, kc_ref, kp_ref, vc_ref, vp_ref, o_ref, l_ref, bias_ref):
        n = pl.program_id(1)

        @pl.when(n == 0)
        def _():
            bias_ref[...] = _attn_bias(dil, (B,), 2 * B)

        masks = _head_masks((B, GROUP_W))
        for j in range(per_step):
            own = slice(j * B, (j + 1) * B)
            before = slice((j - 1) * B, j * B)
            qs = _stack_heads(q_ref[own], masks)
            keys = jnp.concatenate([kp_ref[...] if j == 0 else kc_ref[before], kc_ref[own]], axis=0)
            vals = jnp.concatenate([vp_ref[...] if j == 0 else vc_ref[before], vc_ref[own]], axis=0)
            s = _mm_nt(qs, keys) * scale + bias_ref[...]
            if j == 0:
                key_col = lax.broadcasted_iota(jnp.int32, s.shape, 1)
                s = jnp.where((n == 0) & (key_col < B), _NEG, s)
            m = jnp.max(s, axis=-1, keepdims=True)
            p = jnp.exp(s - m)
            l = jnp.sum(p, axis=-1, keepdims=True)
            o4 = jnp.dot(p.astype(MXU_DTYPE), vals, preferred_element_type=F32)
            o_ref[own] = _unstack_heads(o4, masks) / _unstack_heads(jnp.broadcast_to(l, o4.shape), masks)
            l_ref[own] = _unstack_heads(jnp.broadcast_to(m + jnp.log(l), o4.shape), masks)

    blk = (per_step * B, GROUP_W)
    cur = lambda c: pl.BlockSpec(blk, lambda r, n: (n, r * 3 + c))
    prev = lambda c: pl.BlockSpec((B, GROUP_W), lambda r, n: (jnp.maximum(n * per_step - 1, 0), r * 3 + c))
    out = pl.BlockSpec(blk, lambda r, n: (n, r))
    return pl.pallas_call(
        body, name=name, grid=(dil, nb // per_step),
        in_specs=[cur(0), cur(1), prev(1), cur(2), prev(2)],
        out_specs=[out, out],
        out_shape=[jax.ShapeDtypeStruct((rows, dil * GROUP_W), F32)] * 2,
        scratch_shapes=[pltpu.VMEM((N_HEADS * ATTN_BLOCK, 2 * ATTN_BLOCK), F32)],
        compiler_params=_params(("parallel", "arbitrary")),
    )(qkv, qkv, qkv, qkv, qkv)


def _outproj(x, z_g, y_abc, attn, w_out, name):
    S, D = x.shape
    tm = TM_MM
    n_abc = 3 * GROUP_W

    def body(x_ref, g_ref, yabc_ref, o1, l1, o2, l2, o3, l3, w_ref,
             xn_ref, y_ref, o_ref, lse1_ref, lse4_ref, lse16_ref, so2, sl2, so3, sl3, slse):
        for src, dst, dil in ((o2, so2, ATTN_DILATIONS[1]), (l2, sl2, ATTN_DILATIONS[1]),
                              (o3, so3, ATTN_DILATIONS[2]), (l3, sl3, ATTN_DILATIONS[2])):
            _interleave(src, dst, dil)
        la, lb, lc = l1[...], _get(sl2), _get(sl3)
        mx = jnp.maximum(jnp.maximum(la, lb), lc)
        ea, eb, ec = jnp.exp(la - mx), jnp.exp(lb - mx), jnp.exp(lc - mx)
        den = ea + eb + ec
        o = (ea * o1[...] + eb * _get(so2) + ec * _get(so3)) / den
        o_ref[...] = o
        _put(slse, mx + jnp.log(den))
        for dil, ref in zip(ATTN_DILATIONS, (lse1_ref, lse4_ref, lse16_ref)):
            _deinterleave(slse, ref, dil)
        y_d = o * _silu_and_grad(g_ref[...])[0]
        y_ref[:, 0:n_abc] = yabc_ref[...].astype(MXU_DTYPE)
        y_ref[:, n_abc:] = y_d.astype(MXU_DTYPE)
        xn_ref[...] = x_ref[...] + jnp.dot(y_ref[...], w_ref[...], preferred_element_type=F32)

    row = lambda w: pl.BlockSpec((tm, w), lambda i: (i, 0))
    dil_specs = [_dilated_spec(tm, GROUP_W, dil) for dil in ATTN_DILATIONS]
    (o1, l1), (o2, l2), (o3, l3) = attn
    return pl.pallas_call(
        body, name=name, grid=(S // tm,),
        in_specs=[row(D), row(GROUP_W), row(n_abc)] + [sp for sp in dil_specs for _ in range(2)]
                 + [pl.BlockSpec(w_out.shape, lambda i: (0, 0))],
        out_specs=[row(D), row(4 * GROUP_W), row(GROUP_W)] + dil_specs,
        out_shape=[jax.ShapeDtypeStruct((S, D), F32), jax.ShapeDtypeStruct((S, 4 * GROUP_W), MXU_DTYPE),
                   jax.ShapeDtypeStruct((S, GROUP_W), F32)]
                  + [_dilated_shape(S, GROUP_W, dil, F32) for dil in ATTN_DILATIONS],
        scratch_shapes=[_lane_scratch(tm, GROUP_W)] * 5,
        compiler_params=_params(("parallel",)),
    )(x, z_g, y_abc, o1, l1, o2, l2, o3, l3, w_out)


def _loss_head(x, g, target, name):
    S, D = x.shape
    tm = TM_MM

    def body(x_ref, g_ref, t_ref, dx_ref, loss_ref, dg_ref):
        i = pl.program_id(0)

        @pl.when(i == 0)
        def _():
            loss_ref[...] = jnp.zeros_like(loss_ref)
            dg_ref[...] = jnp.zeros_like(dg_ref)

        xv = x_ref[...]
        r = lax.rsqrt(jnp.mean(xv * xv, axis=-1, keepdims=True) + NORM_EPS)
        xn = xv * r
        err = xn * g_ref[...] - t_ref[...]
        per_tok = jnp.mean(err * err, axis=-1, keepdims=True)
        loss_ref[...] += 0.5 * jnp.sum(per_tok, axis=0, keepdims=True)
        dout = err * (1.0 / D)
        dg_ref[...] += _colsum(dout * xn)
        dxn = dout * g_ref[...]
        dx_ref[...] = r * (dxn - xn * jnp.mean(dxn * xn, axis=-1, keepdims=True))

    row = pl.BlockSpec((tm, D), lambda i: (i, 0))
    return pl.pallas_call(
        body, name=name, grid=(S // tm,),
        in_specs=[row, pl.BlockSpec((1, D), lambda i: (0, 0)), row],
        out_specs=[row, pl.BlockSpec((1, LANES), lambda i: (0, 0)), pl.BlockSpec((1, D), lambda i: (0, 0))],
        out_shape=[jax.ShapeDtypeStruct((S, D), F32), jax.ShapeDtypeStruct((1, LANES), F32),
                   jax.ShapeDtypeStruct((1, D), F32)],
        compiler_params=_params(("arbitrary",)),
    )(x, g, target)


def _outproj_bwd(dx, y, w_out, name):
    S, D = dx.shape
    E = y.shape[1]
    tm = TM_MM

    def body(dx_ref, y_ref, w_ref, dy_ref, dw_ref, acc_ref):
        i = pl.program_id(0)

        @pl.when(i == 0)
        def _():
            acc_ref[...] = jnp.zeros_like(acc_ref)

        dxb = dx_ref[...].astype(MXU_DTYPE)
        dy_ref[...] = _mm_nt(dxb, w_ref[...])
        acc_ref[...] += _mm_tn(y_ref[...], dxb)

        @pl.when(i == S // tm - 1)
        def _():
            dw_ref[...] = acc_ref[...].astype(dw_ref.dtype)

    return pl.pallas_call(
        body, name=name, grid=(S // tm,),
        in_specs=[pl.BlockSpec((tm, D), lambda i: (i, 0)), pl.BlockSpec((tm, E), lambda i: (i, 0)),
                  pl.BlockSpec((E, D), lambda i: (0, 0))],
        out_specs=[pl.BlockSpec((tm, E), lambda i: (i, 0)), pl.BlockSpec((E, D), lambda i: (0, 0))],
        out_shape=[jax.ShapeDtypeStruct((S, E), F32), jax.ShapeDtypeStruct((E, D), WIRE_DTYPE)],
        scratch_shapes=[pltpu.VMEM((E, D), F32)],
        compiler_params=_params(("arbitrary",)),
    )(dx, y, w_out)


def _mix_bwd(z, z_g, dy, hs, o, mp, name):
    S = z.shape[0]
    tm = TM_MIX
    hb = tm // SUBLANES
    nT = S // tm
    last_blk = S // SUBLANES - 1
    wcols = N_ABC * GROUP_W

    def body(z_ref, zh_ref, zn_ref, zg_ref, dy_ref, dyn_ref, h_ref, hh_ref, o_ref,
             wA_ref, wR_ref, vec_ref, wa_ref, wx_ref, ws_ref, bs_ref,
             dz_ref, dzg_ref, do1_ref, do4_ref, do16_ref, dl1_ref, dl4_ref, dl16_ref,
             dwA_ref, dwR_ref, dvec_ref, dwa_ref, dwx_ref, dws_ref, dbs_ref,
             hcarry_ref, xcarry_ref, bsacc_ref, do_ref, dl_ref, sa_ref, sb_ref, sc_ref):
        i = pl.program_id(0)
        ti = nT - 1 - i

        @pl.when(i == 0)
        def _():
            hcarry_ref[...] = jnp.zeros_like(hcarry_ref)
            xcarry_ref[...] = jnp.zeros_like(xcarry_ref)
            bsacc_ref[...] = jnp.zeros_like(bsacc_ref)
            dwA_ref[...] = jnp.zeros_like(dwA_ref)
            dwR_ref[...] = jnp.zeros_like(dwR_ref)
            dvec_ref[...] = jnp.zeros_like(dvec_ref)
            dwa_ref[...] = jnp.zeros_like(dwa_ref)
            dwx_ref[...] = jnp.zeros_like(dwx_ref)
            dws_ref[...] = jnp.zeros_like(dws_ref)
            dbs_ref[...] = jnp.zeros_like(dbs_ref)

        has_prev = ti > 0
        has_next = i > 0
        col = lambda c: slice(c * GROUP_W, (c + 1) * GROUP_W)
        z_of = lambda c: z_ref[:, col(c)]
        halo_of = lambda c: jnp.where(has_prev, zh_ref[:, col(c)], 0.0)
        next_of = lambda c: zn_ref[:, col(c)]

        p, p_h, cv = _conv_a(z_of, halo_of, wA_ref)
        sg, dsg = _silu_and_grad(z_of(3))
        a_b = z_of(1)
        dya = dy_ref[:, col(0)]
        dcv = dya * a_b * sg
        dcv_n = jnp.where(has_next, dyn_ref[...] * next_of(1) * _silu_and_grad(next_of(3))[0], 0.0)
        dp = (wA_ref[2:3, :] * dcv + wA_ref[1:2, :] * _shift_up(dcv, dcv_n, 1)
              + wA_ref[0:1, :] * _shift_up(dcv, dcv_n, 2))
        dwA_ref[2:3, :] += _colsum(dcv * p)
        dwA_ref[1:2, :] += _colsum(dcv * _shift_down(p, p_h, 1))
        dwA_ref[0:1, :] += _colsum(dcv * _shift_down(p, p_h, 2))
        def put_dz(c, val):
            dz_ref[:, col(c)] = val.astype(dz_ref.dtype)

        put_dz(0, dp * z_of(2))
        put_dz(1, dya * cv * sg)
        put_dz(2, dp * z_of(0))
        put_dz(3, dya * a_b * cv * dsg)

        xc, sh, ga, gi, a, mult, sp = _lru_gates(z_of, halo_of, wR_ref, vec_ref, wa_ref, wx_ref)
        h = h_ref[...]
        h_prev = _shift_down(h, jnp.where(has_prev, hh_ref[...], 0.0), 1)
        sgr, dsgr = _silu_and_grad(z_of(5))
        dyb = dy_ref[:, col(1)]
        put_dz(5, dyb * h * dsgr)
        row = lax.broadcasted_iota(jnp.int32, (tm, GROUP_W), 0)
        g_in = dyb * sgr + jnp.where(row == tm - 1, hcarry_ref[0:1, :], 0.0)
        a_up = _shift_up(a, jnp.zeros((SUBLANES, GROUP_W), F32), 1)
        dH = _scan_rev_tile(a_up, g_in, sa_ref, sb_ref, sc_ref)
        hcarry_ref[...] = (a * dH)[0:SUBLANES]
        da = dH * h_prev
        gx = gi * xc
        dmult = dH * gx
        dgi = dH * mult * xc
        dxc = dH * mult * gi
        dlog_a = da * a - dmult * (a * a) / mult
        dga = dlog_a * (-RG_C * sp)
        dlam_row = _colsum(dlog_a * (-RG_C * ga)) * (-_sigmoid(-vec_ref[3:4, :]))
        dpre_a = dga * ga * (1.0 - ga)
        dpre_i = dgi * gi * (1.0 - gi)
        dwa_ref[...] += _mm_tn(xc, dpre_a)
        dwx_ref[...] += _mm_tn(xc, dpre_i)
        dxc = dxc + _mm_nt(dpre_a, wa_ref[...]) + _mm_nt(dpre_i, wx_ref[...])
        dvec_ref[0:1, :] += _colsum(dxc)
        dvec_ref[1:2, :] += _colsum(dpre_a)
        dvec_ref[2:3, :] += _colsum(dpre_i)
        dvec_ref[3:4, :] += dlam_row
        for k in range(4):
            dwR_ref[k:k + 1, :] += _colsum(dxc * sh[3 - k])
        dxc_n = xcarry_ref[...]
        put_dz(4, wR_ref[3:4, :] * dxc + wR_ref[2:3, :] * _shift_up(dxc, dxc_n, 1)
               + wR_ref[1:2, :] * _shift_up(dxc, dxc_n, 2) + wR_ref[0:1, :] * _shift_up(dxc, dxc_n, 3))
        xcarry_ref[...] = dxc[0:SUBLANES]

        c_u, c_v = z_of(6), z_of(7)
        u, du_dx = _gelu_and_grad(c_u)
        gv, dgv_dx = _gelu_and_grad(c_v)
        rr = lax.rsqrt(jnp.mean(gv * gv, axis=-1, keepdims=True) + NORM_EPS)
        xhat = gv * rr
        g_c = vec_ref[4:5, :]
        vn = xhat * g_c
        masks = _head_masks((GMLP_CHUNK, GROUP_W))
        tri_r = lax.broadcasted_iota(jnp.int32, (GMLP_CHUNK, GMLP_CHUNK), 0)
        tri_c = lax.broadcasted_iota(jnp.int32, (GMLP_CHUNK, GMLP_CHUNK), 1)
        tril = tri_r >= tri_c
        sgc, dsgc = _silu_and_grad(z_of(8))
        dyc = dy_ref[:, col(2)]
        dsp_full = dyc * u * sgc
        sp_parts, dvn_parts = [], []
        for c in range(tm // GMLP_CHUNK):
            rs = slice(c * GMLP_CHUNK, (c + 1) * GMLP_CHUNK)
            vc = vn[rs].astype(MXU_DTYPE)
            dsp_c = dsp_full[rs]
            bsacc_ref[...] += dsp_c
            acc = bs_ref[...]
            dvn_c = jnp.zeros((GMLP_CHUNK, GROUP_W), F32)
            for h in range(N_HEADS):
                w_h = ws_ref[h]
                acc = acc + jnp.where(masks[h], jnp.dot(w_h, vc, preferred_element_type=F32), 0.0)
                dsp_h = jnp.where(masks[h], dsp_c, 0.0).astype(MXU_DTYPE)
                dvn_c = dvn_c + _mm_tn(w_h, dsp_h)
                dws_ref[h] += jnp.where(tril, _mm_nt(dsp_h, vc), 0.0)
            sp_parts.append(acc)
            dvn_parts.append(dvn_c)
        spv = jnp.concatenate(sp_parts, axis=0)
        dvn = jnp.concatenate(dvn_parts, axis=0)
        put_dz(6, dyc * spv * sgc * du_dx)
        put_dz(8, dyc * u * spv * dsgc)
        dvec_ref[4:5, :] += _colsum(dvn * xhat)
        dgvn = dvn * g_c
        dgv = rr * (dgvn - xhat * jnp.mean(dgvn * xhat, axis=-1, keepdims=True))
        put_dz(7, dgv * dgv_dx)

        sgd, dsgd = _silu_and_grad(zg_ref[...])
        dyd = dy_ref[:, col(3)]
        ov = o_ref[...]
        do = dyd * sgd
        _put(do_ref, do)
        dzg_ref[...] = (dyd * ov * dsgd).astype(dzg_ref.dtype)
        prod = do * ov
        tmasks = _head_masks((tm, GROUP_W))
        dl = jnp.zeros((tm, GROUP_W), F32)
        for h in range(N_HEADS):
            dl = jnp.where(tmasks[h], jnp.sum(jnp.where(tmasks[h], prod, 0.0), axis=-1, keepdims=True), dl)
        _put(dl_ref, dl)
        for dil, d_out, l_out in zip(ATTN_DILATIONS, (do1_ref, do4_ref, do16_ref), (dl1_ref, dl4_ref, dl16_ref)):
            _deinterleave(do_ref, d_out, dil)
            _deinterleave(dl_ref, l_out, dil)

        @pl.when(i == nT - 1)
        def _():
            acc = bsacc_ref[...]
            lane = lax.broadcasted_iota(jnp.int32, (GMLP_CHUNK, LANES), 1)
            out = jnp.zeros((GMLP_CHUNK, LANES), F32)
            for h in range(N_HEADS):
                out = jnp.where(lane == h, jnp.sum(jnp.where(masks[h], acc, 0.0), axis=-1, keepdims=True), out)
            dbs_ref[...] = out

    rev = lambda w: pl.BlockSpec((tm, w), lambda i: (nT - 1 - i, 0))
    prev8 = lambda w: pl.BlockSpec((SUBLANES, w), lambda i: (jnp.maximum((nT - 1 - i) * hb - 1, 0), 0))
    next8 = lambda w: pl.BlockSpec((SUBLANES, w), lambda i: (jnp.minimum((nT - i) * hb, last_blk), 0))
    const2 = lambda shape: pl.BlockSpec(shape, lambda i: (0, 0))
    dil_specs = [_dilated_spec(tm, GROUP_W, dil, lambda i: nT - 1 - i) for dil in ATTN_DILATIONS]
    dil_shapes = [_dilated_shape(S, GROUP_W, dil, F32) for dil in ATTN_DILATIONS]
    small = (SUBLANES, GROUP_W)
    sq = (GROUP_W, GROUP_W)
    ws_shape = (N_HEADS, GMLP_CHUNK, GMLP_CHUNK)
    return pl.pallas_call(
        body, name=name, grid=(nT,),
        in_specs=[rev(wcols), prev8(wcols), next8(wcols), rev(GROUP_W),
                  rev(4 * GROUP_W), next8(GROUP_W), rev(GROUP_W), prev8(GROUP_W), rev(GROUP_W)]
                 + _mix_specs(tm, S, "bwd"),
        out_specs=[rev(wcols), rev(GROUP_W)] + dil_specs + dil_specs
                  + [const2(small), const2(small), const2(small), const2(sq), const2(sq),
                     pl.BlockSpec(ws_shape, lambda i: (0, 0, 0)), const2((GMLP_CHUNK, LANES))],
        out_shape=[jax.ShapeDtypeStruct((S, wcols), MXU_DTYPE), jax.ShapeDtypeStruct((S, GROUP_W), MXU_DTYPE)]
                  + dil_shapes + dil_shapes
                  + [jax.ShapeDtypeStruct(small, F32)] * 3 + [jax.ShapeDtypeStruct(sq, F32)] * 2
                  + [jax.ShapeDtypeStruct(ws_shape, F32), jax.ShapeDtypeStruct((GMLP_CHUNK, LANES), F32)],
        scratch_shapes=[pltpu.VMEM(small, F32), pltpu.VMEM(small, F32), pltpu.VMEM((GMLP_CHUNK, GROUP_W), F32),
                        _lane_scratch(tm, GROUP_W), _lane_scratch(tm, GROUP_W),
                        _lane_scratch(tm, GROUP_W), _lane_scratch(tm, GROUP_W), pltpu.VMEM((hb, GROUP_W), F32)],
        compiler_params=_params(("arbitrary",)),
    )(z, z, z, z_g, dy, dy, hs, hs, o, mp["wA"], mp["wR"], mp["vec"], mp["wa"], mp["wx"], mp["ws"], mp["bs"])


def _attn_bwd(qkv, do, lse, delta, dil, name):
    rows = qkv.shape[0]
    nb = rows // ATTN_BLOCK
    scale = 1.0 / math.sqrt(HEAD_DIM)
    B = ATTN_BLOCK
    per_step = ATTN_BLOCKS_PER_STEP
    n_steps = nb // per_step

    def body(qc_ref, qn_ref, k_ref, v_ref, doc_ref, don_ref, lc_ref, ln_ref, dc_ref, dn_ref,
             dq_ref, dk_ref, dv_ref, carry_ref, bias_ref):
        n = pl.program_id(1)

        @pl.when(n == 0)
        def _():
            carry_ref[...] = jnp.zeros_like(carry_ref)
            bias_ref[...] = _attn_bias(dil, (0, B), B)

        masks = _head_masks((B, GROUP_W))

        def per_row(tiles):
            return jnp.concatenate([jnp.max(jnp.where(masks[h], t, _NEG), axis=-1, keepdims=True)
                                    for t in tiles for h in range(N_HEADS)], axis=0)

        dq_acc = carry_ref[...]
        for j in range(per_step):
            own = slice(j * B, (j + 1) * B)
            after = slice((j + 1) * B, (j + 2) * B)
            last = j == per_step - 1
            nxt = lambda cur_ref, nxt_ref: nxt_ref[...] if last else cur_ref[after]
            kb = k_ref[own]
            vb = v_ref[own]
            qs = jnp.concatenate([_stack_heads(qc_ref[own], masks), _stack_heads(nxt(qc_ref, qn_ref), masks)], axis=0)
            dos = jnp.concatenate([_stack_heads(doc_ref[own].astype(MXU_DTYPE), masks),
                                   _stack_heads(nxt(doc_ref, don_ref).astype(MXU_DTYPE), masks)], axis=0)
            lse_rows = per_row([lc_ref[own], nxt(lc_ref, ln_ref)])
            dl_rows = per_row([dc_ref[own], nxt(dc_ref, dn_ref)])
            s = _mm_nt(qs, kb) * scale + bias_ref[...]
            if last:
                row = lax.broadcasted_iota(jnp.int32, s.shape, 0)
                s = jnp.where((n == n_steps - 1) & (row >= N_HEADS * B), _NEG, s)
            p = jnp.exp(s - lse_rows)
            dp = _mm_nt(dos, vb)
            ds = (p * (dp - dl_rows) * scale).astype(MXU_DTYPE)
            dv_ref[own] = _mm_tn(p.astype(MXU_DTYPE), dos)
            dk_ref[own] = _mm_tn(ds, qs)
            dq4 = jnp.dot(ds, kb, preferred_element_type=F32)
            dq_ref[own] = dq_acc + _unstack_heads(dq4, masks)
            dq_acc = _unstack_heads(dq4, masks, N_HEADS)
        carry_ref[...] = dq_acc

    blk = (per_step * B, GROUP_W)
    one = (B, GROUP_W)
    nxt_idx = lambda n: jnp.minimum((n + 1) * per_step, nb - 1)
    zcur = lambda c: pl.BlockSpec(blk, lambda r, n: (n, r * 3 + c))
    znext = lambda c: pl.BlockSpec(one, lambda r, n: (nxt_idx(n), r * 3 + c))
    cur = pl.BlockSpec(blk, lambda r, n: (n, r))
    nxt = pl.BlockSpec(one, lambda r, n: (nxt_idx(n), r))
    return pl.pallas_call(
        body, name=name, grid=(dil, n_steps),
        in_specs=[zcur(0), znext(0), zcur(1), zcur(2), cur, nxt, cur, nxt, cur, nxt],
        out_specs=[cur, cur, cur],
        out_shape=[jax.ShapeDtypeStruct((rows, dil * GROUP_W), F32)] * 3,
        scratch_shapes=[pltpu.VMEM(one, F32), pltpu.VMEM((2 * N_HEADS * B, B), F32)],
        compiler_params=_params(("parallel", "arbitrary")),
    )(qkv, qkv, qkv, qkv, do, do, lse, lse, delta, delta)


def _inproj_bwd(x, g, dxn, dz_abc, dqkv, dz_g, w_in, name):
    S, D = x.shape
    N = w_in.shape[1]
    tm = TM_MM
    n_abc = N_ABC * GROUP_W

    def body(x_ref, g_ref, dxn_ref, dabc_ref, q1, k1, v1, q2, k2, v2, q3, k3, v3, dg_ref, w_ref,
             dx_ref, dz_ref, h_ref, dgn_ref, s4_ref, s16_ref):
        i = pl.program_id(0)

        @pl.when(i == 0)
        def _():
            dgn_ref[...] = jnp.zeros_like(dgn_ref)

        dz_ref[:, 0:n_abc] = dabc_ref[...].astype(MXU_DTYPE)
        for j, parts in enumerate(((q1, q2, q3), (k1, k2, k3), (v1, v2, v3))):
            c0 = n_abc + j * GROUP_W
            _interleave(parts[1], s4_ref, ATTN_DILATIONS[1])
            _interleave(parts[2], s16_ref, ATTN_DILATIONS[2])
            dz_ref[:, c0:c0 + GROUP_W] = (parts[0][...] + _get(s4_ref) + _get(s16_ref)).astype(MXU_DTYPE)
        dz_ref[:, n_abc + 3 * GROUP_W:] = dg_ref[...].astype(MXU_DTYPE)
        dh = _mm_nt(dz_ref[...], w_ref[...])
        xv = x_ref[...]
        r = lax.rsqrt(jnp.mean(xv * xv, axis=-1, keepdims=True) + NORM_EPS)
        xn = xv * r
        gv = g_ref[...]
        h_ref[...] = (xn * gv).astype(MXU_DTYPE)
        dgn_ref[...] += _colsum(dh * xn)
        dn = dh * gv
        dx_ref[...] = dxn_ref[...] + r * (dn - xn * jnp.mean(dn * xn, axis=-1, keepdims=True))

    row = lambda w: pl.BlockSpec((tm, w), lambda i: (i, 0))
    flat = [t for p in dqkv for t in p]
    dil_specs = [_dilated_spec(tm, GROUP_W, dil) for dil in ATTN_DILATIONS for _ in range(3)]
    return pl.pallas_call(
        body, name=name, grid=(S // tm,),
        in_specs=[row(D), pl.BlockSpec((1, D), lambda i: (0, 0)), row(D), row(n_abc)] + dil_specs
                 + [row(GROUP_W), pl.BlockSpec((D, N), lambda i: (0, 0))],
        out_specs=[row(D), row(N), row(D), pl.BlockSpec((1, D), lambda i: (0, 0))],
        out_shape=[jax.ShapeDtypeStruct((S, D), F32), jax.ShapeDtypeStruct((S, N), MXU_DTYPE),
                   jax.ShapeDtypeStruct((S, D), MXU_DTYPE), jax.ShapeDtypeStruct((1, D), F32)],
        scratch_shapes=[_lane_scratch(tm, GROUP_W)] * 2,
        compiler_params=_params(("arbitrary",)),
    )(x, g, dxn, dz_abc, *flat, dz_g, w_in)


def _inproj_wgrad(h, dz, name):
    S, D = h.shape
    N = dz.shape[1]
    tm = TM_MM
    nj = 2
    cw = N // nj
    per = N_DEV // nj
    n_loc = N // N_DEV

    def body(h_ref, dz_ref, dw_ref, acc_ref):
        i = pl.program_id(1)

        @pl.when(i == 0)
        def _():
            acc_ref[...] = jnp.zeros_like(acc_ref)

        acc_ref[...] += _mm_tn(h_ref[...], dz_ref[...])

        @pl.when(i == S // tm - 1)
        def _():
            for b in range(per):
                dw_ref[b] = acc_ref[:, b * n_loc:(b + 1) * n_loc].astype(dw_ref.dtype)

    return pl.pallas_call(
        body, name=name, grid=(nj, S // tm),
        in_specs=[pl.BlockSpec((tm, D), lambda j, i: (i, 0)), pl.BlockSpec((tm, cw), lambda j, i: (i, j))],
        out_specs=pl.BlockSpec((per, D, n_loc), lambda j, i: (j, 0, 0)),
        out_shape=jax.ShapeDtypeStruct((N_DEV, D, n_loc), WIRE_DTYPE),
        scratch_shapes=[pltpu.VMEM((D, cw), F32)],
        compiler_params=_params(("parallel", "arbitrary")),
    )(h, dz)


def _my_place():
    return lax.axis_index("x"), lax.axis_index("y"), lax.axis_index("c")


def _peer(x, y, c, k):
    px = 1 - x if k & 4 else x
    py = 1 - y if k & 2 else y
    pc = 1 - c if k & 1 else c
    return (px, py, pc), 4 * px + 2 * py + pc


HBM_SPEC = pl.BlockSpec(memory_space=pltpu.HBM)
SEM_SPEC = pl.BlockSpec(memory_space=pltpu.SEMAPHORE)
SPLIT_EFFECT = pltpu.SideEffectType.DATAFLOW_SIDE_EFFECTING
N_PEERS = N_DEV - 1


def _exchange_copies(srcs, lands, send_sems, recv_sems, whole, arrival):
    x, y, c = _my_place()
    me = 4 * x + 2 * y + c
    copies = []
    for t in range(len(srcs)):
        for k in range(1, N_DEV):
            peer, pidx = _peer(x, y, c, k)
            copies.append(pltpu.make_async_remote_copy(
                src_ref=srcs[t] if whole[t] else srcs[t].at[pidx],
                dst_ref=lands[t].at[pidx if arrival else me], send_sem=send_sems.at[t * N_PEERS + k - 1],
                recv_sem=recv_sems.at[t * N_PEERS + k - 1], device_id=peer, device_id_type=MESH))
    return copies


def _exchange_start(groups, name):
    sizes = [len(g) for g in groups]
    whole = [w for g in groups for _, w in g]
    srcs = [pltpu.with_memory_space_constraint(a, pltpu.HBM) for g in groups for a, _ in g]
    lands = [pltpu.with_memory_space_constraint(lax.empty(((N_DEV,) + a.shape) if w else a.shape, a.dtype), pltpu.HBM)
             for a, w in zip(srcs, whole)]
    n = len(srcs)
    n_g = len(groups)

    def body(*refs):
        src_refs, land_refs = refs[:n], refs[n:2 * n]
        sem_refs = refs[4 * n:4 * n + 2 * n_g]
        token = refs[-1]
        off = 0
        for gi, sz in enumerate(sizes):
            for send in _exchange_copies(src_refs[off:off + sz], land_refs[off:off + sz],
                                         sem_refs[2 * gi], sem_refs[2 * gi + 1], whole[off:off + sz], False):
                send.start()
            off += sz
        token[...] = jnp.zeros_like(token)

    sem_shapes = [pltpu.SemaphoreType.DMA((sz * N_PEERS,)) for sz in sizes for _ in range(2)]
    outs = pl.pallas_call(
        body, name=name,
        in_specs=[HBM_SPEC] * (2 * n),
        out_specs=[HBM_SPEC] * (2 * n) + [SEM_SPEC] * (2 * n_g) + [pl.BlockSpec(memory_space=pltpu.VMEM)],
        out_shape=[pltpu.HBM(a.shape, a.dtype) for a in srcs + lands] + sem_shapes
                  + [jax.ShapeDtypeStruct((SUBLANES, LANES), F32)],
        input_output_aliases={i: i for i in range(2 * n)},
        compiler_params=pltpu.CompilerParams(has_side_effects=SPLIT_EFFECT),
    )(*srcs, *lands)
    handles, off = [], 0
    for gi, sz in enumerate(sizes):
        handles.append((outs[2 * n + 2 * gi], outs[2 * n + 2 * gi + 1], outs[off:off + sz], outs[n + off:n + off + sz],
                        whole[off:off + sz]))
        off += sz
    return handles, outs[-1]


def _exchange_wait(handle, after, name):
    send_sems, recv_sems, srcs, lands, whole = handle
    n = len(srcs)

    def body(*refs):
        src_refs, land_refs = refs[:n], refs[n:2 * n]
        for send in _exchange_copies(src_refs, land_refs, refs[2 * n], refs[2 * n + 1], whole, False):
            send.wait_send()
        for arrival in _exchange_copies(src_refs, land_refs, refs[2 * n], refs[2 * n + 1], whole, True):
            arrival.wait_recv()

    outs = pl.pallas_call(
        body, name=name,
        in_specs=[HBM_SPEC] * (2 * n) + [SEM_SPEC, SEM_SPEC, pl.BlockSpec(memory_space=pl.ANY)],
        out_specs=[HBM_SPEC] * (2 * n),
        out_shape=[pltpu.HBM(a.shape, a.dtype) for a in list(srcs) + list(lands)],
        input_output_aliases={i: i for i in range(2 * n)},
        compiler_params=pltpu.CompilerParams(has_side_effects=SPLIT_EFFECT),
    )(*srcs, *lands, send_sems, recv_sems, after)
    x, y, c = _my_place()
    me = 4 * x + 2 * y + c
    own = [s[None] if w else lax.dynamic_slice_in_dim(s, me, 1, axis=0) for s, w in zip(outs[:n], whole)]
    return [lax.dynamic_update_slice_in_dim(ld, o, me, axis=0) for ld, o in zip(outs[n:], own)]


def _sum_slots(parts, name):
    n = len(parts)

    def body(*refs):
        for p_ref, o_ref in zip(refs[:n], refs[n:]):
            acc = p_ref[0]
            for j in range(1, N_DEV):
                acc = acc + p_ref[j]
            o_ref[...] = acc

    vm = pl.BlockSpec(memory_space=pltpu.VMEM)
    return pl.pallas_call(
        body, name=name, in_specs=[vm] * n, out_specs=[vm] * n,
        out_shape=[jax.ShapeDtypeStruct(p.shape[1:], F32) for p in parts],
        compiler_params=pltpu.CompilerParams(vmem_limit_bytes=VMEM_LIMIT),
    )(*parts)


def _adamw_math(w, g, m, v):
    m = ADAM_B1 * m + (1.0 - ADAM_B1) * g
    v = ADAM_B2 * v + (1.0 - ADAM_B2) * (g * g)
    m_hat = m / (1.0 - ADAM_B1 ** ADAM_STEP)
    v_hat = v / (1.0 - ADAM_B2 ** ADAM_STEP)
    delta = -ADAM_LR * (m_hat / (jnp.sqrt(v_hat) + ADAM_EPS) + ADAM_WD * w)
    return delta, m, v


def _adamw_summed(parts, w, m, v, tr, name):
    depth, R, C = w.shape

    def body(*refs):
        p_refs = refs[:depth]
        w_ref, m_ref, v_ref, g_ref, d_ref, nm_ref, nv_ref = refs[depth:]
        lay = pl.program_id(0)
        for l in range(depth):
            @pl.when(lay == l)
            def _(p_ref=p_refs[l]):
                g = p_ref[0].astype(F32)
                for j in range(1, N_DEV):
                    g = g + p_ref[j].astype(F32)
                g_ref[0] = g
        d_ref[0], nm_ref[0], nv_ref[0] = _adamw_math(w_ref[0], g_ref[0], m_ref[0], v_ref[0])

    part_spec = lambda l: pl.BlockSpec((N_DEV, tr, C), lambda lay, i: (0, jnp.where(lay == l, i, 0), 0))
    row = pl.BlockSpec((1, tr, C), lambda lay, i: (lay, i, 0))
    return pl.pallas_call(
        body, name=name, grid=(depth, R // tr),
        in_specs=[part_spec(l) for l in range(depth)] + [row, row, row],
        out_specs=[row] * 4, out_shape=[jax.ShapeDtypeStruct((depth, R, C), F32)] * 4,
        compiler_params=_params(("arbitrary", "arbitrary")),
    )(*parts, w, m, v)


def _adamw_small(w, g, m, v, name):
    def body(w_ref, g_ref, m_ref, v_ref, d_ref, nm_ref, nv_ref):
        d_ref[...], nm_ref[...], nv_ref[...] = _adamw_math(w_ref[...], g_ref[...], m_ref[...], v_ref[...])

    vm = pl.BlockSpec(memory_space=pltpu.VMEM)
    return pl.pallas_call(
        body, name=name, in_specs=[vm] * 4, out_specs=[vm] * 3,
        out_shape=[jax.ShapeDtypeStruct(w.shape, F32)] * 3,
        compiler_params=pltpu.CompilerParams(vmem_limit_bytes=VMEM_LIMIT),
    )(w, g, m, v)


def _pack(arrays):
    flat = jnp.concatenate([a.reshape(-1) for a in arrays])
    pad = (-flat.shape[0]) % (SUBLANES * LANES)
    return jnp.pad(flat, (0, pad)).reshape(-1, LANES)


def _unpack(buf, like):
    flat = buf.reshape(-1)
    out, off = [], 0
    for a in like:
        out.append(flat[off:off + a.size].reshape(a.shape))
        off += a.size
    return out


def _block_diag(w):
    eye = jnp.eye(N_HEADS, dtype=w.dtype)
    return jnp.einsum('hij,hk->hikj', w, eye).reshape(GROUP_W, GROUP_W)


def _diag_blocks(w):
    return jnp.einsum('hihj->hij', w.reshape(N_HEADS, HEAD_DIM, N_HEADS, HEAD_DIM))


def _pad_rows(a):
    return jnp.pad(a, ((0, SUBLANES - a.shape[0]), (0, 0)))


def _mixer_params(l, conv_a_w, conv_r_w, conv_r_b, lru_wa, lru_ba, lru_wx, lru_bx, lru_lambda, gmlp_norm_g,
                  gmlp_ws, gmlp_bs):
    tril = jnp.tril(jnp.ones((GMLP_CHUNK, GMLP_CHUNK), dtype=bool))
    vec = jnp.stack([conv_r_b[l], lru_ba[l], lru_bx[l], lru_lambda[l], gmlp_norm_g[l]])
    return {
        "wA": _pad_rows(conv_a_w[l]), "wR": _pad_rows(conv_r_w[l]), "vec": _pad_rows(vec),
        "wa": _block_diag(lru_wa[l]).astype(MXU_DTYPE), "wx": _block_diag(lru_wx[l]).astype(MXU_DTYPE),
        "ws": jnp.where(tril[None], gmlp_ws[l], 0.0).astype(MXU_DTYPE),
        "bs": jnp.repeat(jnp.transpose(gmlp_bs[l]), HEAD_DIM, axis=1),
    }


MIXER_NAMES = ("conv_a_w", "conv_r_w", "conv_r_b", "lru_wa", "lru_ba", "lru_wx", "lru_bx", "lru_lambda",
               "gmlp_norm_g", "gmlp_ws", "gmlp_bs")
SMALL_NAMES = ("norm_g",) + MIXER_NAMES + ("final_g",)


def _local_step(x, loss_target, norm_g, get_w_in, get_w_out, emit_early, emit_late, conv_a_w, conv_r_w, conv_r_b,
                lru_wa, lru_ba, lru_wx, lru_bx, lru_lambda, gmlp_norm_g, gmlp_ws, gmlp_bs, final_g):
    depth = norm_g.shape[0]
    D = x.shape[1]
    small = (conv_a_w, conv_r_w, conv_r_b, lru_wa, lru_ba, lru_wx, lru_bx, lru_lambda, gmlp_norm_g, gmlp_ws, gmlp_bs)
    saved = []
    for l in range(depth):
        mp = _mixer_params(l, *small)
        w_in_l = get_w_in(l, x)
        z, z_g, *qkv = _norm_inproj(x, norm_g[l].reshape(1, D), w_in_l, f"norm_inproj_{l}")
        y_abc, hs = _mix_fwd(z, mp, f"mix_fwd_{l}")
        attn = [_attn_fwd(qkv[p], dil, f"attn_fwd_d{dil}_{l}") for p, dil in enumerate(ATTN_DILATIONS)]
        w_out_l = get_w_out(l, y_abc)
        x_new, y, o, *lse = _outproj(x, z_g, y_abc, attn, w_out_l, f"outproj_{l}")
        saved.append((x, z, z_g, qkv, hs, y, o, lse, mp, w_in_l, w_out_l))
        x = x_new
    dx, loss, d_final_g = _loss_head(x, final_g.reshape(1, D), loss_target, "loss_head")
    token = None
    for l in reversed(range(depth)):
        x_l, z, z_g, qkv, hs, y, o, lse, mp, w_in_l, w_out_l = saved[l]
        if token is not None:
            mp = dict(mp, vec=mp["vec"] + token[0, 0])
        dy, dw_out = _outproj_bwd(dx, y, w_out_l, f"outproj_bwd_{l}")
        (dz_abc, dz_g, do1, do4, do16, dl1, dl4, dl16, dwA, dwR, dvec, dwa, dwx, dws, dbs) = _mix_bwd(
            z, z_g, dy, hs, o, mp, f"mix_bwd_{l}")
        token = emit_early(l, dw_out, [
            dwA[:conv_a_w.shape[1]], dwR[:conv_r_w.shape[1]], dvec[0], _diag_blocks(dwa), dvec[1], _diag_blocks(dwx),
            dvec[2], dvec[3], dvec[4], dws, jnp.transpose(dbs[:, :N_HEADS])])
        g_row = norm_g[l].reshape(1, D)
        if token is not None:
            g_row = g_row + token[0, 0]
        dqkv = [_attn_bwd(qkv[p], do, lse[p], dl, dil, f"attn_bwd_d{dil}_{l}")
                for p, (dil, do, dl) in enumerate(zip(ATTN_DILATIONS, (do1, do4, do16), (dl1, dl4, dl16)))]
        dx, dz, h, dng = _inproj_bwd(x_l, g_row, dx, dz_abc, dqkv, dz_g, w_in_l, f"inproj_bwd_{l}")
        dw_in = _inproj_wgrad(h, dz, f"inproj_wgrad_{l}")
        token = emit_late(l, dw_in, [dng[0]] + ([d_final_g[0]] if l == depth - 1 else []))
    return loss[0, 0], dx
WEIGHT_NAMES = ("norm_g", "w_in", "conv_a_w", "conv_r_w", "conv_r_b", "lru_wa", "lru_ba", "lru_wx", "lru_bx",
                "lru_lambda", "gmlp_norm_g", "gmlp_ws", "gmlp_bs", "w_out", "final_g")


def kernel(x, norm_g, w_in, conv_a_w, conv_r_w, conv_r_b, lru_wa, lru_ba, lru_wx, lru_bx, lru_lambda, gmlp_norm_g, gmlp_ws, gmlp_bs, w_out, final_g, loss_target, m_norm_g, m_w_in, m_conv_a_w, m_conv_r_w, m_conv_r_b, m_lru_wa, m_lru_ba, m_lru_wx, m_lru_bx, m_lru_lambda, m_gmlp_norm_g, m_gmlp_ws, m_gmlp_bs, m_w_out, m_final_g, v_norm_g, v_w_in, v_conv_a_w, v_conv_r_w, v_conv_r_b, v_lru_wa, v_lru_ba, v_lru_wx, v_lru_bx, v_lru_lambda, v_gmlp_norm_g, v_gmlp_ws, v_gmlp_bs, v_w_out, v_final_g):
    w = dict(norm_g=norm_g, w_in=w_in, conv_a_w=conv_a_w, conv_r_w=conv_r_w, conv_r_b=conv_r_b, lru_wa=lru_wa,
             lru_ba=lru_ba, lru_wx=lru_wx, lru_bx=lru_bx, lru_lambda=lru_lambda, gmlp_norm_g=gmlp_norm_g,
             gmlp_ws=gmlp_ws, gmlp_bs=gmlp_bs, w_out=w_out, final_g=final_g)
    m = dict(norm_g=m_norm_g, w_in=m_w_in, conv_a_w=m_conv_a_w, conv_r_w=m_conv_r_w, conv_r_b=m_conv_r_b,
             lru_wa=m_lru_wa, lru_ba=m_lru_ba, lru_wx=m_lru_wx, lru_bx=m_lru_bx, lru_lambda=m_lru_lambda,
             gmlp_norm_g=m_gmlp_norm_g, gmlp_ws=m_gmlp_ws, gmlp_bs=m_gmlp_bs, w_out=m_w_out, final_g=m_final_g)
    v = dict(norm_g=v_norm_g, w_in=v_w_in, conv_a_w=v_conv_a_w, conv_r_w=v_conv_r_w, conv_r_b=v_conv_r_b,
             lru_wa=v_lru_wa, lru_ba=v_lru_ba, lru_wx=v_lru_wx, lru_bx=v_lru_bx, lru_lambda=v_lru_lambda,
             gmlp_norm_g=v_gmlp_norm_g, gmlp_ws=v_gmlp_ws, gmlp_bs=v_gmlp_bs, w_out=v_w_out, final_g=v_final_g)
    depth, D, n_loc = w_in.shape
    e_loc = w_out.shape[1]
    cx, cy, cc = _my_place()
    me = 4 * cx + 2 * cy + cc

    w_in_w, w_out_w = w_in.astype(MXU_DTYPE), w_out.astype(MXU_DTYPE)
    c_loc = conv_a_w.shape[2]
    taps = (conv_a_w, conv_r_w)
    groups = [[(w_in_w[0], True), (_pack(taps), True)], [(w_out_w[0], True)]]
    groups += [[(w_in_w[l], True), (w_out_w[l], True)] for l in range(1, depth)]
    gathers, _ = _exchange_start(groups, "gather_start")
    full_in = lambda g: jnp.transpose(g, (1, 0, 2)).reshape(D, N_DEV * n_loc)
    full_out = lambda g: g.reshape(N_DEV * e_loc, D)

    g_in0, g_taps = _exchange_wait(gathers[0], x, "gather_wait_in_0")
    g_taps = g_taps.reshape(N_DEV, -1)
    conv_full, off = [], 0
    for a in taps:
        part = g_taps[:, off:off + a.size].reshape((N_DEV,) + a.shape)
        conv_full.append(jnp.transpose(part, (1, 2, 0, 3)).reshape(a.shape[:2] + (N_DEV * c_loc,)))
        off += a.size
    conv_a_full, conv_r_full = conv_full
    later = {}

    def get_w_in(l, after):
        if l == 0:
            return full_in(g_in0)
        g_in, later[l] = _exchange_wait(gathers[l + 1], after, f"gather_wait_{l}")
        return full_in(g_in)

    def get_w_out(l, after):
        if l == 0:
            return full_out(_exchange_wait(gathers[1], after, "gather_wait_out_0")[0])
        return full_out(later[l])

    early, late, last_token = {}, {}, [None]

    def emit_early(l, dw_out, mixer_grads):
        handles, token = _exchange_start(
            [[(dw_out.reshape(N_DEV, e_loc, D), False), (_pack(mixer_grads), True)]], f"early_start_{l}")
        early[l] = (handles[0], mixer_grads)
        return token

    def emit_late(l, dw_in, norm_grads):
        handles, token = _exchange_start([[(_pack(norm_grads), True)], [(dw_in, False)]], f"late_start_{l}")
        late[l] = (handles[0], handles[1], norm_grads)
        last_token[0] = token
        return token

    loss, grad_x = _local_step(
        x[0], loss_target[0], norm_g, get_w_in, get_w_out, emit_early, emit_late, conv_a_full, conv_r_full, conv_r_b,
        lru_wa, lru_ba, lru_wx, lru_bx, lru_lambda, gmlp_norm_g, gmlp_ws, gmlp_bs, final_g)
    loss = lax.psum(loss, ("x", "y", "c"))

    r_in, r_out, small_parts = {}, {}, []
    for l in reversed(range(depth)):
        r_out[l], r_mix = _exchange_wait(early[l][0], last_token[0], f"early_wait_{l}")
        (r_norm,) = _exchange_wait(late[l][0], last_token[0], f"late_wait_norm_{l}")
        small_parts += [r_mix, r_norm]
        if l > 0:
            (r_in[l],) = _exchange_wait(late[l][1], last_token[0], f"late_wait_{l}")
    big = {"w_out": _adamw_summed([r_out[l] for l in range(depth)], w_out, m_w_out, v_w_out, 128, "adamw_w_out")}

    sums = _sum_slots(small_parts, "sum_small_grads")
    by_layer = {}
    for i, l in enumerate(reversed(range(depth))):
        mix = _unpack(sums[2 * i], early[l][1])
        nrm = _unpack(sums[2 * i + 1], late[l][2])
        by_layer[l] = dict(zip(MIXER_NAMES, mix), norm_g=nrm[0])
        if l == depth - 1:
            g_final = nrm[1]
    g_small = {k: jnp.stack([by_layer[l][k] for l in range(depth)]) for k in ("norm_g",) + MIXER_NAMES}
    g_small["final_g"] = g_final
    for k in ("conv_a_w", "conv_r_w"):
        g_small[k] = lax.dynamic_slice_in_dim(g_small[k], me * c_loc, c_loc, axis=2)
    packs = [_pack([d[k] for k in SMALL_NAMES]) for d in (w, g_small, m, v)]
    res = _adamw_small(*packs, "adamw_small")
    like = [w[k] for k in SMALL_NAMES]
    d_s, m_s, v_s = (dict(zip(SMALL_NAMES, _unpack(r, like))) for r in res)

    (r_in[0],) = _exchange_wait(late[0][1], res[0], "late_wait_0")
    big["w_in"] = _adamw_summed([r_in[l] for l in range(depth)], w_in, m_w_in, v_w_in, 512, "adamw_w_in")

    grad, delta, new_m, new_v = {}, {}, {}, {}
    for k in WEIGHT_NAMES:
        if k in big:
            grad[k], delta[k], new_m[k], new_v[k] = big[k]
        else:
            grad[k], delta[k], new_m[k], new_v[k] = g_small[k], d_s[k], m_s[k], v_s[k]
    return (loss, grad_x[None], *[grad[k] for k in WEIGHT_NAMES], *[delta[k] for k in WEIGHT_NAMES],
            *[new_m[k] for k in WEIGHT_NAMES], *[new_v[k] for k in WEIGHT_NAMES])
```

```python
import functools
import math

import jax
import jax.numpy as jnp
from jax import lax
from jax.experimental import pallas as pl
from jax.experimental.pallas import tpu as pltpu

F32 = jnp.float32
MXU_DTYPE = jnp.bfloat16
WIRE_DTYPE = jnp.bfloat16
MESH = pl.DeviceIdType.MESH

N_DEV = 8
GROUP_W = 256
N_HEADS = 4
HEAD_DIM = 64
N_CHUNKS = 13
N_ABC = 9
GMLP_CHUNK = 128
ATTN_BLOCK = 128
ATTN_BLOCKS_PER_STEP = 4
ATTN_DILATIONS = (1, 4, 16)
NORM_EPS = 1e-6
RG_C = 8.0
SUBLANES = 8
LANES = 128
VMEM_LIMIT = 56 * 1024 * 1024

ADAM_LR = 0.001
ADAM_B1 = 0.9
ADAM_B2 = 0.999
ADAM_EPS = 1e-08
ADAM_WD = 0.01
ADAM_STEP = 10

TM_MIX = 512
TM_MM = 512
TM_INPROJ = 1024


def _params(sem, vmem=VMEM_LIMIT):
    return pltpu.CompilerParams(dimension_semantics=sem, vmem_limit_bytes=vmem)


def _mm(a, b):
    return jnp.dot(a.astype(MXU_DTYPE), b.astype(MXU_DTYPE), preferred_element_type=F32)


def _mm_tn(a, b):
    return lax.dot_general(a.astype(MXU_DTYPE), b.astype(MXU_DTYPE), (((0,), (0,)), ((), ())),
                           preferred_element_type=F32)


def _mm_nt(a, b):
    return lax.dot_general(a.astype(MXU_DTYPE), b.astype(MXU_DTYPE), (((1,), (1,)), ((), ())),
                           preferred_element_type=F32)


def _sigmoid(x):
    return 0.5 * jnp.tanh(0.5 * x) + 0.5


def _silu_and_grad(x):
    s = _sigmoid(x)
    return x * s, s * (1.0 + x * (1.0 - s))


_GELU_K = math.sqrt(2.0 / math.pi)
_GELU_C = 0.044715


def _gelu_and_grad(x):
    x2 = x * x
    t = jnp.tanh(_GELU_K * (x + _GELU_C * x * x2))
    val = 0.5 * x * (1.0 + t)
    grad = 0.5 * (1.0 + t) + 0.5 * x * (1.0 - t * t) * (_GELU_K * (1.0 + 3.0 * _GELU_C * x2))
    return val, grad


def _gelu(x):
    return 0.5 * x * (1.0 + jnp.tanh(_GELU_K * (x + _GELU_C * x * x * x)))


def _expm1_nonpos(u):
    poly = 1.0 / math.factorial(9)
    for k in range(8, 0, -1):
        poly = poly * u + 1.0 / math.factorial(k)
    return jnp.where(u > -0.25, poly * u, jnp.exp(u) - 1.0)


def _softplus(x):
    return jnp.maximum(x, 0.0) + jnp.log(1.0 + jnp.exp(-jnp.abs(x)))


def _shift_down(t, halo, k):
    rolled = pltpu.roll(t, k, 0)
    hr = pltpu.roll(halo, k, 0)
    row = lax.broadcasted_iota(jnp.int32, halo.shape, 0)
    first = jnp.where(row < k, hr, rolled[0:SUBLANES])
    return jnp.concatenate([first, rolled[SUBLANES:]], axis=0)


def _shift_up(t, nxt, k):
    tm = t.shape[0]
    rolled = pltpu.roll(t, tm - k, 0)
    nr = pltpu.roll(nxt, SUBLANES - k, 0)
    row = lax.broadcasted_iota(jnp.int32, nxt.shape, 0)
    last = jnp.where(row >= SUBLANES - k, nr, rolled[tm - SUBLANES:tm])
    return jnp.concatenate([rolled[:tm - SUBLANES], last], axis=0)


def _scan_fwd(a, b):
    tm = a.shape[0]
    row = lax.broadcasted_iota(jnp.int32, a.shape, 0)
    s = 1
    while s < tm:
        a_s = pltpu.roll(a, s, 0)
        b_s = pltpu.roll(b, s, 0)
        m = row >= s
        b = jnp.where(m, a * b_s + b, b)
        a = jnp.where(m, a * a_s, a)
        s *= 2
    return a, b


def _scan_rev(a, g):
    tm = a.shape[0]
    row = lax.broadcasted_iota(jnp.int32, a.shape, 0)
    s = 1
    while s < tm:
        a_s = pltpu.roll(a, tm - s, 0)
        g_s = pltpu.roll(g, tm - s, 0)
        m = row < tm - s
        g = jnp.where(m, g + a * g_s, g)
        a = jnp.where(m, a * a_s, a)
        s *= 2
    return g


def _group_rows(scr_ref, row, n_groups):
    return jnp.concatenate([scr_ref[pl.ds(c, 1), pl.ds(row, n_groups, stride=SUBLANES), :][0]
                            for c in range(scr_ref.shape[0])], axis=1)


def _spread_rows(rows_ref, n_groups, w):
    return jnp.concatenate([jnp.broadcast_to(rows_ref[g:g + 1, :], (SUBLANES, w)) for g in range(n_groups)], axis=0)


def _scan_groups(a, b, reverse):
    tm, w = a.shape
    shape3 = (tm // SUBLANES, SUBLANES, w)
    a3, b3 = a.reshape(shape3), b.reshape(shape3)
    sub = lax.broadcasted_iota(jnp.int32, shape3, 1)
    s = 1
    while s < SUBLANES:
        shift = SUBLANES - s if reverse else s
        a_s = pltpu.roll(a3, shift, 1)
        b_s = pltpu.roll(b3, shift, 1)
        m = (sub < SUBLANES - s) if reverse else (sub >= s)
        b3 = jnp.where(m, a3 * b_s + b3, b3)
        a3 = jnp.where(m, a3 * a_s, a3)
        s *= 2
    return a3.reshape(tm, w), b3.reshape(tm, w)


def _scan_fwd_tile(a, b, h_in, sa_ref, sb_ref, sc_ref):
    tm, w = a.shape
    n_groups = tm // SUBLANES
    a_loc, b_loc = _scan_groups(a, b, False)
    _put(sa_ref, a_loc)
    _put(sb_ref, b_loc)
    a_end, b_end = _scan_fwd(_group_rows(sa_ref, SUBLANES - 1, n_groups), _group_rows(sb_ref, SUBLANES - 1, n_groups))
    h_end = b_end + a_end * h_in
    sc_ref[...] = _shift_down(h_end, jnp.broadcast_to(h_in, (SUBLANES, w)), 1)
    return b_loc + a_loc * _spread_rows(sc_ref, n_groups, w), h_end


def _scan_rev_tile(a, g, sa_ref, sb_ref, sc_ref):
    tm, w = a.shape
    n_groups = tm // SUBLANES
    a_loc, g_loc = _scan_groups(a, g, True)
    _put(sa_ref, a_loc)
    _put(sb_ref, g_loc)
    d_first = _scan_rev(_group_rows(sa_ref, 0, n_groups), _group_rows(sb_ref, 0, n_groups))
    sc_ref[...] = _shift_up(d_first, jnp.zeros((SUBLANES, w), F32), 1)
    return g_loc + a_loc * _spread_rows(sc_ref, n_groups, w)


def _lane_scratch(tm, w):
    return pltpu.VMEM((w // LANES, tm, LANES), F32)


def _put(scr_ref, val):
    for c in range(scr_ref.shape[0]):
        scr_ref[c] = val[:, c * LANES:(c + 1) * LANES].astype(F32)


def _get(scr_ref):
    return jnp.concatenate([scr_ref[c] for c in range(scr_ref.shape[0])], axis=1)


def _deinterleave(src_ref, dst_ref, dil):
    nc, tm, _ = src_ref.shape
    w = nc * LANES
    for r in range(dil):
        for c in range(nc):
            piece = src_ref[pl.ds(c, 1), pl.ds(r, tm // dil, stride=dil), :][0] if dil > 1 else src_ref[c]
            dst_ref[:, r * w + c * LANES:r * w + (c + 1) * LANES] = piece.astype(dst_ref.dtype)


def _interleave(src_ref, dst_ref, dil):
    nc, tm, _ = dst_ref.shape
    w = nc * LANES
    for r in range(dil):
        for c in range(nc):
            dst_ref[pl.ds(c, 1), pl.ds(r, tm // dil, stride=dil), :] = (
                src_ref[:, r * w + c * LANES:r * w + (c + 1) * LANES].astype(F32)[None])


def _dilated_spec(tm, w, dil, index=lambda i: i):
    return pl.BlockSpec((tm // dil, dil * w), lambda i: (index(i), 0))


def _dilated_shape(S, w, dil, dtype):
    return jax.ShapeDtypeStruct((S // dil, dil * w), dtype)


def _head_masks(shape):
    lane = lax.broadcasted_iota(jnp.int32, shape, 1)
    return [(lane >= h * HEAD_DIM) & (lane < (h + 1) * HEAD_DIM) for h in range(N_HEADS)]


def _colsum(v):
    return jnp.sum(v, axis=0, keepdims=True)


def _norm_inproj(x, g, w, name):
    S, D = x.shape
    N = w.shape[1]
    tm = TM_INPROJ
    n_abc = N_ABC * GROUP_W
    n_qkv = 3 * GROUP_W

    def body(x_ref, g_ref, w_ref, zabc_ref, zg_ref, q1_ref, q4_ref, q16_ref, qkv_ref):
        xv = x_ref[...]
        r = lax.rsqrt(jnp.mean(xv * xv, axis=-1, keepdims=True) + NORM_EPS)
        h = ((xv * r) * g_ref[...]).astype(MXU_DTYPE)
        zabc_ref[...] = jnp.dot(h, w_ref[:, 0:n_abc], preferred_element_type=F32)
        _put(qkv_ref, jnp.dot(h, w_ref[:, n_abc:n_abc + n_qkv], preferred_element_type=F32))
        zg_ref[...] = jnp.dot(h, w_ref[:, n_abc + n_qkv:], preferred_element_type=F32)
        for dil, ref in zip(ATTN_DILATIONS, (q1_ref, q4_ref, q16_ref)):
            _deinterleave(qkv_ref, ref, dil)

    row = lambda wd: pl.BlockSpec((tm, wd), lambda i: (i, 0))
    return pl.pallas_call(
        body, name=name, grid=(S // tm,),
        in_specs=[row(D), pl.BlockSpec((1, D), lambda i: (0, 0)),
                  pl.BlockSpec((D, N), lambda i: (0, 0), pipeline_mode=pl.Buffered(1))],
        out_specs=[row(n_abc), row(GROUP_W)] + [_dilated_spec(tm, n_qkv, dil) for dil in ATTN_DILATIONS],
        out_shape=[jax.ShapeDtypeStruct((S, n_abc), F32), jax.ShapeDtypeStruct((S, GROUP_W), F32)]
                  + [_dilated_shape(S, n_qkv, dil, MXU_DTYPE) for dil in ATTN_DILATIONS],
        scratch_shapes=[_lane_scratch(tm, n_qkv)],
        compiler_params=_params(("parallel",)),
    )(x, g, w)


def _conv_a(z_of, halo_of, w_ref):
    p = z_of(2) * z_of(0)
    p_h = halo_of(2) * halo_of(0)
    cv = w_ref[2:3, :] * p + w_ref[1:2, :] * _shift_down(p, p_h, 1) + w_ref[0:1, :] * _shift_down(p, p_h, 2)
    return p, p_h, cv


def _lru_gates(z_of, halo_of, wr_ref, vec_ref, wa_ref, wx_ref):
    rx = z_of(4)
    rx_h = halo_of(4)
    sh = [rx, _shift_down(rx, rx_h, 1), _shift_down(rx, rx_h, 2), _shift_down(rx, rx_h, 3)]
    xc = (wr_ref[3:4, :] * sh[0] + wr_ref[2:3, :] * sh[1] + wr_ref[1:2, :] * sh[2]
          + wr_ref[0:1, :] * sh[3] + vec_ref[0:1, :])
    ga = _sigmoid(jnp.dot(xc.astype(MXU_DTYPE), wa_ref[...], preferred_element_type=F32) + vec_ref[1:2, :])
    gi = _sigmoid(jnp.dot(xc.astype(MXU_DTYPE), wx_ref[...], preferred_element_type=F32) + vec_ref[2:3, :])
    sp = _softplus(-vec_ref[3:4, :])
    log_a = (-RG_C * ga) * sp
    a = jnp.exp(log_a)
    mult = jnp.sqrt(-_expm1_nonpos(2.0 * log_a))
    return xc, sh, ga, gi, a, mult, sp


def _gmlp_fwd(z_of, vec_ref, ws_ref, bs_ref, tm):
    u = _gelu(z_of(6))
    gv = _gelu(z_of(7))
    rr = lax.rsqrt(jnp.mean(gv * gv, axis=-1, keepdims=True) + NORM_EPS)
    vn = (gv * rr) * vec_ref[4:5, :]
    masks = _head_masks((GMLP_CHUNK, GROUP_W))
    parts = []
    for c in range(tm // GMLP_CHUNK):
        vc = vn[c * GMLP_CHUNK:(c + 1) * GMLP_CHUNK].astype(MXU_DTYPE)
        acc = bs_ref[...]
        for h in range(N_HEADS):
            acc = acc + jnp.where(masks[h], jnp.dot(ws_ref[h], vc, preferred_element_type=F32), 0.0)
        parts.append(acc)
    return u, gv, rr, vn, jnp.concatenate(parts, axis=0)


def _mix_specs(tm, S, order):
    const2 = lambda shape: pl.BlockSpec(shape, lambda i: (0, 0))
    return [const2((SUBLANES, GROUP_W)), const2((SUBLANES, GROUP_W)), const2((SUBLANES, GROUP_W)),
            const2((GROUP_W, GROUP_W)), const2((GROUP_W, GROUP_W)),
            pl.BlockSpec((N_HEADS, GMLP_CHUNK, GMLP_CHUNK), lambda i: (0, 0, 0)),
            const2((GMLP_CHUNK, GROUP_W))]


def _mix_fwd(z, mp, name):
    S = z.shape[0]
    tm = TM_MIX
    hb = tm // SUBLANES
    wcols = N_ABC * GROUP_W

    def body(z_ref, zh_ref, wA_ref, wR_ref, vec_ref, wa_ref, wx_ref, ws_ref, bs_ref, y_ref, h_ref, carry_ref,
             sa_ref, sb_ref, sc_ref):
        i = pl.program_id(0)

        @pl.when(i == 0)
        def _():
            carry_ref[...] = jnp.zeros_like(carry_ref)

        not_first = i > 0
        z_of = lambda c: z_ref[:, c * GROUP_W:(c + 1) * GROUP_W]
        halo_of = lambda c: jnp.where(not_first, zh_ref[:, c * GROUP_W:(c + 1) * GROUP_W], 0.0)

        _, _, cv = _conv_a(z_of, halo_of, wA_ref)
        y_ref[:, 0:GROUP_W] = (z_of(1) * cv * _silu_and_grad(z_of(3))[0]).astype(y_ref.dtype)

        xc, _, _, gi, a, mult, _ = _lru_gates(z_of, halo_of, wR_ref, vec_ref, wa_ref, wx_ref)
        b = mult * (gi * xc)
        h, h_end = _scan_fwd_tile(a, b, carry_ref[SUBLANES - 1:SUBLANES, :], sa_ref, sb_ref, sc_ref)
        h_ref[...] = h
        carry_ref[...] = h_end[hb - SUBLANES:hb]
        y_ref[:, GROUP_W:2 * GROUP_W] = (h * _silu_and_grad(z_of(5))[0]).astype(y_ref.dtype)

        u, _, _, _, sp = _gmlp_fwd(z_of, vec_ref, ws_ref, bs_ref, tm)
        y_ref[:, 2 * GROUP_W:3 * GROUP_W] = (u * sp * _silu_and_grad(z_of(8))[0]).astype(y_ref.dtype)

    return pl.pallas_call(
        body, name=name, grid=(S // tm,),
        in_specs=[pl.BlockSpec((tm, wcols), lambda i: (i, 0)),
                  pl.BlockSpec((SUBLANES, wcols), lambda i: (jnp.maximum(i * hb - 1, 0), 0))]
                 + _mix_specs(tm, S, "fwd"),
        out_specs=[pl.BlockSpec((tm, 3 * GROUP_W), lambda i: (i, 0)),
                   pl.BlockSpec((tm, GROUP_W), lambda i: (i, 0))],
        out_shape=[jax.ShapeDtypeStruct((S, 3 * GROUP_W), MXU_DTYPE), jax.ShapeDtypeStruct((S, GROUP_W), F32)],
        scratch_shapes=[pltpu.VMEM((SUBLANES, GROUP_W), F32), _lane_scratch(tm, GROUP_W), _lane_scratch(tm, GROUP_W),
                        pltpu.VMEM((hb, GROUP_W), F32)],
        compiler_params=_params(("arbitrary",)),
    )(z, z, mp["wA"], mp["wR"], mp["vec"], mp["wa"], mp["wx"], mp["ws"], mp["bs"])


_NEG = -1e30


def _slope(h):
    return 2.0 ** (-8.0 * (h + 1) / N_HEADS)


def _attn_bias(dil, offsets, n_keys):
    shape = (ATTN_BLOCK, n_keys)
    qi = lax.broadcasted_iota(jnp.int32, shape, 0)
    ki = lax.broadcasted_iota(jnp.int32, shape, 1)
    blocks = []
    for f in offsets:
        delta = qi + f - ki
        valid = (delta >= 0) & (delta <= ATTN_BLOCK)
        dist = (delta * dil).astype(F32)
        for h in range(N_HEADS):
            blocks.append(jnp.where(valid, -_slope(h) * dist, _NEG))
    return jnp.concatenate(blocks, axis=0)


def _stack_heads(t, masks):
    return jnp.concatenate([jnp.where(m, t, jnp.zeros_like(t)) for m in masks], axis=0)


def _unstack_heads(t4, masks, base=0):
    out = t4[base * ATTN_BLOCK:(base + 1) * ATTN_BLOCK]
    for h in range(1, N_HEADS):
        out = jnp.where(masks[h], t4[(base + h) * ATTN_BLOCK:(base + h + 1) * ATTN_BLOCK], out)
    return out


def _attn_fwd(qkv, dil, name):
    rows = qkv.shape[0]
    nb = rows // ATTN_BLOCK
    scale = 1.0 / math.sqrt(HEAD_DIM)
    B = ATTN_BLOCK
    per_step = ATTN_BLOCKS_PER_STEP

    def body(q_ref, kc_ref, kp_ref, vc_ref, vp_ref, o_ref, l_ref, bias_ref):
        n = pl.program_id(1)

        @pl.when(n == 0)
        def _():
            bias_ref[...] = _attn_bias(dil, (B,), 2 * B)

        masks = _head_masks((B, GROUP_W))
        for j in range(per_step):
            own = slice(j * B, (j + 1) * B)
            before = slice((j - 1) * B, j * B)
            qs = _stack_heads(q_ref[own], masks)
            keys = jnp.concatenate([kp_ref[...] if j == 0 else kc_ref[before], kc_ref[own]], axis=0)
            vals = jnp.concatenate([vp_ref[...] if j == 0 else vc_ref[before], vc_ref[own]], axis=0)
            s = _mm_nt(qs, keys) * scale + bias_ref[...]
            if j == 0:
                key_col = lax.broadcasted_iota(jnp.int32, s.shape, 1)
                s = jnp.where((n == 0) & (key_col < B), _NEG, s)
            m = jnp.max(s, axis=-1, keepdims=True)
            p = jnp.exp(s - m)
            l = jnp.sum(p, axis=-1, keepdims=True)
            o4 = jnp.dot(p.astype(MXU_DTYPE), vals, preferred_element_type=F32)
            o_ref[own] = _unstack_heads(o4, masks) / _unstack_heads(jnp.broadcast_to(l, o4.shape), masks)
            l_ref[own] = _unstack_heads(jnp.broadcast_to(m + jnp.log(l), o4.shape), masks)

    blk = (per_step * B, GROUP_W)
    cur = lambda c: pl.BlockSpec(blk, lambda r, n: (n, r * 3 + c))
    prev = lambda c: pl.BlockSpec((B, GROUP_W), lambda r, n: (jnp.maximum(n * per_step - 1, 0), r * 3 + c))
    out = pl.BlockSpec(blk, lambda r, n: (n, r))
    return pl.pallas_call(
        body, name=name, grid=(dil, nb // per_step),
        in_specs=[cur(0), cur(1), prev(1), cur(2), prev(2)],
        out_specs=[out, out],
        out_shape=[jax.ShapeDtypeStruct((rows, dil * GROUP_W), F32)] * 2,
        scratch_shapes=[pltpu.VMEM((N_HEADS * ATTN_BLOCK, 2 * ATTN_BLOCK), F32)],
        compiler_params=_params(("parallel", "arbitrary")),
    )(qkv, qkv, qkv, qkv, qkv)


def _outproj(x, z_g, y_abc, attn, w_out, name):
    S, D = x.shape
    tm = TM_MM
    n_abc = 3 * GROUP_W

    def body(x_ref, g_ref, yabc_ref, o1, l1, o2, l2, o3, l3, w_ref,
             xn_ref, y_ref, o_ref, lse1_ref, lse4_ref, lse16_ref, so2, sl2, so3, sl3, slse):
        for src, dst, dil in ((o2, so2, ATTN_DILATIONS[1]), (l2, sl2, ATTN_DILATIONS[1]),
                              (o3, so3, ATTN_DILATIONS[2]), (l3, sl3, ATTN_DILATIONS[2])):
            _interleave(src, dst, dil)
        la, lb, lc = l1[...], _get(sl2), _get(sl3)
        mx = jnp.maximum(jnp.maximum(la, lb), lc)
        ea, eb, ec = jnp.exp(la - mx), jnp.exp(lb - mx), jnp.exp(lc - mx)
        den = ea + eb + ec
        o = (ea * o1[...] + eb * _get(so2) + ec * _get(so3)) / den
        o_ref[...] = o
        _put(slse, mx + jnp.log(den))
        for dil, ref in zip(ATTN_DILATIONS, (lse1_ref, lse4_ref, lse16_ref)):
            _deinterleave(slse, ref, dil)
        y_d = o * _silu_and_grad(g_ref[...])[0]
        y_ref[:, 0:n_abc] = yabc_ref[...].astype(MXU_DTYPE)
        y_ref[:, n_abc:] = y_d.astype(MXU_DTYPE)
        xn_ref[...] = x_ref[...] + jnp.dot(y_ref[...], w_ref[...], preferred_element_type=F32)

    row = lambda w: pl.BlockSpec((tm, w), lambda i: (i, 0))
    dil_specs = [_dilated_spec(tm, GROUP_W, dil) for dil in ATTN_DILATIONS]
    (o1, l1), (o2, l2), (o3, l3) = attn
    return pl.pallas_call(
        body, name=name, grid=(S // tm,),
        in_specs=[row(D), row(GROUP_W), row(n_abc)] + [sp for sp in dil_specs for _ in range(2)]
                 + [pl.BlockSpec(w_out.shape, lambda i: (0, 0))],
        out_specs=[row(D), row(4 * GROUP_W), row(GROUP_W)] + dil_specs,
        out_shape=[jax.ShapeDtypeStruct((S, D), F32), jax.ShapeDtypeStruct((S, 4 * GROUP_W), MXU_DTYPE),
                   jax.ShapeDtypeStruct((S, GROUP_W), F32)]
                  + [_dilated_shape(S, GROUP_W, dil, F32) for dil in ATTN_DILATIONS],
        scratch_shapes=[_lane_scratch(tm, GROUP_W)] * 5,
        compiler_params=_params(("parallel",)),
    )(x, z_g, y_abc, o1, l1, o2, l2, o3, l3, w_out)


def _loss_head(x, g, target, name):
    S, D = x.shape
    tm = TM_MM

    def body(x_ref, g_ref, t_ref, dx_ref, loss_ref, dg_ref):
        i = pl.program_id(0)

        @pl.when(i == 0)
        def _():
            loss_ref[...] = jnp.zeros_like(loss_ref)
            dg_ref[...] = jnp.zeros_like(dg_ref)

        xv = x_ref[...]
        r = lax.rsqrt(jnp.mean(xv * xv, axis=-1, keepdims=True) + NORM_EPS)
        xn = xv * r
        err = xn * g_ref[...] - t_ref[...]
        per_tok = jnp.mean(err * err, axis=-1, keepdims=True)
        loss_ref[...] += 0.5 * jnp.sum(per_tok, axis=0, keepdims=True)
        dout = err * (1.0 / D)
        dg_ref[...] += _colsum(dout * xn)
        dxn = dout * g_ref[...]
        dx_ref[...] = r * (dxn - xn * jnp.mean(dxn * xn, axis=-1, keepdims=True))

    row = pl.BlockSpec((tm, D), lambda i: (i, 0))
    return pl.pallas_call(
        body, name=name, grid=(S // tm,),
        in_specs=[row, pl.BlockSpec((1, D), lambda i: (0, 0)), row],
        out_specs=[row, pl.BlockSpec((1, LANES), lambda i: (0, 0)), pl.BlockSpec((1, D), lambda i: (0, 0))],
        out_shape=[jax.ShapeDtypeStruct((S, D), F32), jax.ShapeDtypeStruct((1, LANES), F32),
                   jax.ShapeDtypeStruct((1, D), F32)],
        compiler_params=_params(("arbitrary",)),
    )(x, g, target)


def _outproj_bwd(dx, y, w_out, name):
    S, D = dx.shape
    E = y.shape[1]
    tm = TM_MM

    def body(dx_ref, y_ref, w_ref, dy_ref, dw_ref, acc_ref):
        i = pl.program_id(0)

        @pl.when(i == 0)
        def _():
            acc_ref[...] = jnp.zeros_like(acc_ref)

        dxb = dx_ref[...].astype(MXU_DTYPE)
        dy_ref[...] = _mm_nt(dxb, w_ref[...])
        acc_ref[...] += _mm_tn(y_ref[...], dxb)

        @pl.when(i == S // tm - 1)
        def _():
            dw_ref[...] = acc_ref[...].astype(dw_ref.dtype)

    return pl.pallas_call(
        body, name=name, grid=(S // tm,),
        in_specs=[pl.BlockSpec((tm, D), lambda i: (i, 0)), pl.BlockSpec((tm, E), lambda i: (i, 0)),
                  pl.BlockSpec((E, D), lambda i: (0, 0))],
        out_specs=[pl.BlockSpec((tm, E), lambda i: (i, 0)), pl.BlockSpec((E, D), lambda i: (0, 0))],
        out_shape=[jax.ShapeDtypeStruct((S, E), F32), jax.ShapeDtypeStruct((E, D), WIRE_DTYPE)],
        scratch_shapes=[pltpu.VMEM((E, D), F32)],
        compiler_params=_params(("arbitrary",)),
    )(dx, y, w_out)


def _mix_bwd(z, z_g, dy, hs, o, mp, name):
    S = z.shape[0]
    tm = TM_MIX
    hb = tm // SUBLANES
    nT = S // tm
    last_blk = S // SUBLANES - 1
    wcols = N_ABC * GROUP_W

    def body(z_ref, zh_ref, zn_ref, zg_ref, dy_ref, dyn_ref, h_ref, hh_ref, o_ref,
             wA_ref, wR_ref, vec_ref, wa_ref, wx_ref, ws_ref, bs_ref,
             dz_ref, dzg_ref, do1_ref, do4_ref, do16_ref, dl1_ref, dl4_ref, dl16_ref,
             dwA_ref, dwR_ref, dvec_ref, dwa_ref, dwx_ref, dws_ref, dbs_ref,
             hcarry_ref, xcarry_ref, bsacc_ref, do_ref, dl_ref, sa_ref, sb_ref, sc_ref):
        i = pl.program_id(0)
        ti = nT - 1 - i

        @pl.when(i == 0)
        def _():
            hcarry_ref[...] = jnp.zeros_like(hcarry_ref)
            xcarry_ref[...] = jnp.zeros_like(xcarry_ref)
            bsacc_ref[...] = jnp.zeros_like(bsacc_ref)
            dwA_ref[...] = jnp.zeros_like(dwA_ref)
            dwR_ref[...] = jnp.zeros_like(dwR_ref)
            dvec_ref[...] = jnp.zeros_like(dvec_ref)
            dwa_ref[...] = jnp.zeros_like(dwa_ref)
            dwx_ref[...] = jnp.zeros_like(dwx_ref)
            dws_ref[...] = jnp.zeros_like(dws_ref)
            dbs_ref[...] = jnp.zeros_like(dbs_ref)

        has_prev = ti > 0
        has_next = i > 0
        col = lambda c: slice(c * GROUP_W, (c + 1) * GROUP_W)
        z_of = lambda c: z_ref[:, col(c)]
        halo_of = lambda c: jnp.where(has_prev, zh_ref[:, col(c)], 0.0)
        next_of = lambda c: zn_ref[:, col(c)]

        p, p_h, cv = _conv_a(z_of, halo_of, wA_ref)
        sg, dsg = _silu_and_grad(z_of(3))
        a_b = z_of(1)
        dya = dy_ref[:, col(0)]
        dcv = dya * a_b * sg
        dcv_n = jnp.where(has_next, dyn_ref[...] * next_of(1) * _silu_and_grad(next_of(3))[0], 0.0)
        dp = (wA_ref[2:3, :] * dcv + wA_ref[1:2, :] * _shift_up(dcv, dcv_n, 1)
              + wA_ref[0:1, :] * _shift_up(dcv, dcv_n, 2))
        dwA_ref[2:3, :] += _colsum(dcv * p)
        dwA_ref[1:2, :] += _colsum(dcv * _shift_down(p, p_h, 1))
        dwA_ref[0:1, :] += _colsum(dcv * _shift_down(p, p_h, 2))
        def put_dz(c, val):
            dz_ref[:, col(c)] = val.astype(dz_ref.dtype)

        put_dz(0, dp * z_of(2))
        put_dz(1, dya * cv * sg)
        put_dz(2, dp * z_of(0))
        put_dz(3, dya * a_b * cv * dsg)

        xc, sh, ga, gi, a, mult, sp = _lru_gates(z_of, halo_of, wR_ref, vec_ref, wa_ref, wx_ref)
        h = h_ref[...]
        h_prev = _shift_down(h, jnp.where(has_prev, hh_ref[...], 0.0), 1)
        sgr, dsgr = _silu_and_grad(z_of(5))
        dyb = dy_ref[:, col(1)]
        put_dz(5, dyb * h * dsgr)
        row = lax.broadcasted_iota(jnp.int32, (tm, GROUP_W), 0)
        g_in = dyb * sgr + jnp.where(row == tm - 1, hcarry_ref[0:1, :], 0.0)
        a_up = _shift_up(a, jnp.zeros((SUBLANES, GROUP_W), F32), 1)
        dH = _scan_rev_tile(a_up, g_in, sa_ref, sb_ref, sc_ref)
        hcarry_ref[...] = (a * dH)[0:SUBLANES]
        da = dH * h_prev
        gx = gi * xc
        dmult = dH * gx
        dgi = dH * mult * xc
        dxc = dH * mult * gi
        dlog_a = da * a - dmult * (a * a) / mult
        dga = dlog_a * (-RG_C * sp)
        dlam_row = _colsum(dlog_a * (-RG_C * ga)) * (-_sigmoid(-vec_ref[3:4, :]))
        dpre_a = dga * ga * (1.0 - ga)
        dpre_i = dgi * gi * (1.0 - gi)
        dwa_ref[...] += _mm_tn(xc, dpre_a)
        dwx_ref[...] += _mm_tn(xc, dpre_i)
        dxc = dxc + _mm_nt(dpre_a, wa_ref[...]) + _mm_nt(dpre_i, wx_ref[...])
        dvec_ref[0:1, :] += _colsum(dxc)
        dvec_ref[1:2, :] += _colsum(dpre_a)
        dvec_ref[2:3, :] += _colsum(dpre_i)
        dvec_ref[3:4, :] += dlam_row
        for k in range(4):
            dwR_ref[k:k + 1, :] += _colsum(dxc * sh[3 - k])
        dxc_n = xcarry_ref[...]
        put_dz(4, wR_ref[3:4, :] * dxc + wR_ref[2:3, :] * _shift_up(dxc, dxc_n, 1)
               + wR_ref[1:2, :] * _shift_up(dxc, dxc_n, 2) + wR_ref[0:1, :] * _shift_up(dxc, dxc_n, 3))
        xcarry_ref[...] = dxc[0:SUBLANES]

        c_u, c_v = z_of(6), z_of(7)
        u, du_dx = _gelu_and_grad(c_u)
        gv, dgv_dx = _gelu_and_grad(c_v)
        rr = lax.rsqrt(jnp.mean(gv * gv, axis=-1, keepdims=True) + NORM_EPS)
        xhat = gv * rr
        g_c = vec_ref[4:5, :]
        vn = xhat * g_c
        masks = _head_masks((GMLP_CHUNK, GROUP_W))
        tri_r = lax.broadcasted_iota(jnp.int32, (GMLP_CHUNK, GMLP_CHUNK), 0)
        tri_c = lax.broadcasted_iota(jnp.int32, (GMLP_CHUNK, GMLP_CHUNK), 1)
        tril = tri_r >= tri_c
        sgc, dsgc = _silu_and_grad(z_of(8))
        dyc = dy_ref[:, col(2)]
        dsp_full = dyc * u * sgc
        sp_parts, dvn_parts = [], []
        for c in range(tm // GMLP_CHUNK):
            rs = slice(c * GMLP_CHUNK, (c + 1) * GMLP_CHUNK)
            vc = vn[rs].astype(MXU_DTYPE)
            dsp_c = dsp_full[rs]
            bsacc_ref[...] += dsp_c
            acc = bs_ref[...]
            dvn_c = jnp.zeros((GMLP_CHUNK, GROUP_W), F32)
            for h in range(N_HEADS):
                w_h = ws_ref[h]
                acc = acc + jnp.where(masks[h], jnp.dot(w_h, vc, preferred_element_type=F32), 0.0)
                dsp_h = jnp.where(masks[h], dsp_c, 0.0).astype(MXU_DTYPE)
                dvn_c = dvn_c + _mm_tn(w_h, dsp_h)
                dws_ref[h] += jnp.where(tril, _mm_nt(dsp_h, vc), 0.0)
            sp_parts.append(acc)
            dvn_parts.append(dvn_c)
        spv = jnp.concatenate(sp_parts, axis=0)
        dvn = jnp.concatenate(dvn_parts, axis=0)
        put_dz(6, dyc * spv * sgc * du_dx)
        put_dz(8, dyc * u * spv * dsgc)
        dvec_ref[4:5, :] += _colsum(dvn * xhat)
        dgvn = dvn * g_c
        dgv = rr * (dgvn - xhat * jnp.mean(dgvn * xhat, axis=-1, keepdims=True))
        put_dz(7, dgv * dgv_dx)

        sgd, dsgd = _silu_and_grad(zg_ref[...])
        dyd = dy_ref[:, col(3)]
        ov = o_ref[...]
        do = dyd * sgd
        _put(do_ref, do)
        dzg_ref[...] = (dyd * ov * dsgd).astype(dzg_ref.dtype)
        prod = do * ov
        tmasks = _head_masks((tm, GROUP_W))
        dl = jnp.zeros((tm, GROUP_W), F32)
        for h in range(N_HEADS):
            dl = jnp.where(tmasks[h], jnp.sum(jnp.where(tmasks[h], prod, 0.0), axis=-1, keepdims=True), dl)
        _put(dl_ref, dl)
        for dil, d_out, l_out in zip(ATTN_DILATIONS, (do1_ref, do4_ref, do16_ref), (dl1_ref, dl4_ref, dl16_ref)):
            _deinterleave(do_ref, d_out, dil)
            _deinterleave(dl_ref, l_out, dil)

        @pl.when(i == nT - 1)
        def _():
            acc = bsacc_ref[...]
            lane = lax.broadcasted_iota(jnp.int32, (GMLP_CHUNK, LANES), 1)
            out = jnp.zeros((GMLP_CHUNK, LANES), F32)
            for h in range(N_HEADS):
                out = jnp.where(lane == h, jnp.sum(jnp.where(masks[h], acc, 0.0), axis=-1, keepdims=True), out)
            dbs_ref[...] = out

    rev = lambda w: pl.BlockSpec((tm, w), lambda i: (nT - 1 - i, 0))
    prev8 = lambda w: pl.BlockSpec((SUBLANES, w), lambda i: (jnp.maximum((nT - 1 - i) * hb - 1, 0), 0))
    next8 = lambda w: pl.BlockSpec((SUBLANES, w), lambda i: (jnp.minimum((nT - i) * hb, last_blk), 0))
    const2 = lambda shape: pl.BlockSpec(shape, lambda i: (0, 0))
    dil_specs = [_dilated_spec(tm, GROUP_W, dil, lambda i: nT - 1 - i) for dil in ATTN_DILATIONS]
    dil_shapes = [_dilated_shape(S, GROUP_W, dil, F32) for dil in ATTN_DILATIONS]
    small = (SUBLANES, GROUP_W)
    sq = (GROUP_W, GROUP_W)
    ws_shape = (N_HEADS, GMLP_CHUNK, GMLP_CHUNK)
    return pl.pallas_call(
        body, name=name, grid=(nT,),
        in_specs=[rev(wcols), prev8(wcols), next8(wcols), rev(GROUP_W),
                  rev(4 * GROUP_W), next8(GROUP_W), rev(GROUP_W), prev8(GROUP_W), rev(GROUP_W)]
                 + _mix_specs(tm, S, "bwd"),
        out_specs=[rev(wcols), rev(GROUP_W)] + dil_specs + dil_specs
                  + [const2(small), const2(small), const2(small), const2(sq), const2(sq),
                     pl.BlockSpec(ws_shape, lambda i: (0, 0, 0)), const2((GMLP_CHUNK, LANES))],
        out_shape=[jax.ShapeDtypeStruct((S, wcols), MXU_DTYPE), jax.ShapeDtypeStruct((S, GROUP_W), MXU_DTYPE)]
                  + [_dilated_shape(S, GROUP_W, dil, MXU_DTYPE) for dil in ATTN_DILATIONS] + dil_shapes
                  + [jax.ShapeDtypeStruct(small, F32)] * 3 + [jax.ShapeDtypeStruct(sq, F32)] * 2
                  + [jax.ShapeDtypeStruct(ws_shape, F32), jax.ShapeDtypeStruct((GMLP_CHUNK, LANES), F32)],
        scratch_shapes=[pltpu.VMEM(small, F32), pltpu.VMEM(small, F32), pltpu.VMEM((GMLP_CHUNK, GROUP_W), F32),
                        _lane_scratch(tm, GROUP_W), _lane_scratch(tm, GROUP_W),
                        _lane_scratch(tm, GROUP_W), _lane_scratch(tm, GROUP_W), pltpu.VMEM((hb, GROUP_W), F32)],
        compiler_params=_params(("arbitrary",)),
    )(z, z, z, z_g, dy, dy, hs, hs, o, mp["wA"], mp["wR"], mp["vec"], mp["wa"], mp["wx"], mp["ws"], mp["bs"])


def _attn_bwd(qkv, do, lse, delta, dil, name):
    rows = qkv.shape[0]
    nb = rows // ATTN_BLOCK
    scale = 1.0 / math.sqrt(HEAD_DIM)
    B = ATTN_BLOCK
    per_step = ATTN_BLOCKS_PER_STEP
    n_steps = nb // per_step

    def body(qc_ref, qn_ref, kc_ref, kp_ref, vc_ref, vp_ref, doc_ref, don_ref, lc_ref, ln_ref, dc_ref, dn_ref,
             dq_ref, dk_ref, dv_ref, bias_ref, bias_next_ref):
        n = pl.program_id(1)

        @pl.when(n == 0)
        def _():
            bias_ref[...] = _attn_bias(dil, (B,), 2 * B)
            bias_next_ref[...] = _attn_bias(dil, (B,), B)

        masks = _head_masks((B, GROUP_W))

        def per_row(tile):
            return jnp.concatenate([jnp.max(jnp.where(masks[h], tile, _NEG), axis=-1, keepdims=True)
                                    for h in range(N_HEADS)], axis=0)

        def grads(q, dov, lse_tile, dl_tile, keys, vals, bias, dead):
            qs = _stack_heads(q, masks)
            dos = _stack_heads(dov.astype(MXU_DTYPE), masks)
            s = _mm_nt(qs, keys) * scale + bias
            if dead is not None:
                s = jnp.where(dead(s.shape), _NEG, s)
            p = jnp.exp(s - per_row(lse_tile))
            ds = (p * (_mm_nt(dos, vals) - per_row(dl_tile)) * scale).astype(MXU_DTYPE)
            return ds, _mm_tn(ds, qs), _mm_tn(p.astype(MXU_DTYPE), dos)

        for j in range(per_step):
            own = slice(j * B, (j + 1) * B)
            before = slice((j - 1) * B, j * B)
            keys = jnp.concatenate([kp_ref[...] if j == 0 else kc_ref[before], kc_ref[own]], axis=0)
            vals = jnp.concatenate([vp_ref[...] if j == 0 else vc_ref[before], vc_ref[own]], axis=0)
            dead = (lambda shape: (n == 0) & (lax.broadcasted_iota(jnp.int32, shape, 1) < B)) if j == 0 else None
            ds, dk2, dv2 = grads(qc_ref[own], doc_ref[own], lc_ref[own], dc_ref[own], keys, vals, bias_ref[...], dead)
            dq_ref[own] = _unstack_heads(jnp.dot(ds, keys, preferred_element_type=F32), masks).astype(dq_ref.dtype)
            dk_ref[own] = dk2[B:]
            dv_ref[own] = dv2[B:]
            if j > 0:
                dk_ref[before] += dk2[:B]
                dv_ref[before] += dv2[:B]
        last = slice((per_step - 1) * B, per_step * B)
        _, dk1, dv1 = grads(qn_ref[...], don_ref[...], ln_ref[...], dn_ref[...], kc_ref[last], vc_ref[last],
                            bias_next_ref[...], lambda shape: n == n_steps - 1)
        dk_ref[last] += dk1
        dv_ref[last] += dv1

    blk = (per_step * B, GROUP_W)
    one = (B, GROUP_W)
    nxt_idx = lambda n: jnp.minimum((n + 1) * per_step, nb - 1)
    prv_idx = lambda n: jnp.maximum(n * per_step - 1, 0)
    zcur = lambda c: pl.BlockSpec(blk, lambda r, n: (n, r * 3 + c))
    znext = lambda c: pl.BlockSpec(one, lambda r, n: (nxt_idx(n), r * 3 + c))
    zprev = lambda c: pl.BlockSpec(one, lambda r, n: (prv_idx(n), r * 3 + c))
    cur = pl.BlockSpec(blk, lambda r, n: (n, r))
    nxt = pl.BlockSpec(one, lambda r, n: (nxt_idx(n), r))
    return pl.pallas_call(
        body, name=name, grid=(dil, n_steps),
        in_specs=[zcur(0), znext(0), zcur(1), zprev(1), zcur(2), zprev(2), cur, nxt, cur, nxt, cur, nxt],
        out_specs=[cur, cur, cur],
        out_shape=[jax.ShapeDtypeStruct((rows, dil * GROUP_W), F32)] * 3,
        scratch_shapes=[pltpu.VMEM((N_HEADS * B, 2 * B), F32), pltpu.VMEM((N_HEADS * B, B), F32)],
        compiler_params=_params(("parallel", "arbitrary")),
    )(qkv, qkv, qkv, qkv, qkv, qkv, do, do, lse, lse, delta, delta)


def _inproj_bwd(x, g, dxn, dz_abc, dqkv, dz_g, w_in, name):
    S, D = x.shape
    N = w_in.shape[1]
    tm = TM_MM
    n_abc = N_ABC * GROUP_W

    def body(x_ref, g_ref, dxn_ref, dabc_ref, q1, k1, v1, q2, k2, v2, q3, k3, v3, dg_ref, w_ref,
             dx_ref, dz_ref, h_ref, dgn_ref, s4_ref, s16_ref):
        i = pl.program_id(0)

        @pl.when(i == 0)
        def _():
            dgn_ref[...] = jnp.zeros_like(dgn_ref)

        dz_ref[:, 0:n_abc] = dabc_ref[...].astype(MXU_DTYPE)
        for j, parts in enumerate(((q1, q2, q3), (k1, k2, k3), (v1, v2, v3))):
            c0 = n_abc + j * GROUP_W
            _interleave(parts[1], s4_ref, ATTN_DILATIONS[1])
            _interleave(parts[2], s16_ref, ATTN_DILATIONS[2])
            dz_ref[:, c0:c0 + GROUP_W] = (parts[0][...] + _get(s4_ref) + _get(s16_ref)).astype(MXU_DTYPE)
        dz_ref[:, n_abc + 3 * GROUP_W:] = dg_ref[...].astype(MXU_DTYPE)
        dh = _mm_nt(dz_ref[...], w_ref[...])
        xv = x_ref[...]
        r = lax.rsqrt(jnp.mean(xv * xv, axis=-1, keepdims=True) + NORM_EPS)
        xn = xv * r
        gv = g_ref[...]
        h_ref[...] = (xn * gv).astype(MXU_DTYPE)
        dgn_ref[...] += _colsum(dh * xn)
        dn = dh * gv
        dx_ref[...] = dxn_ref[...] + r * (dn - xn * jnp.mean(dn * xn, axis=-1, keepdims=True))

    row = lambda w: pl.BlockSpec((tm, w), lambda i: (i, 0))
    flat = [t for p in dqkv for t in p]
    dil_specs = [_dilated_spec(tm, GROUP_W, dil) for dil in ATTN_DILATIONS for _ in range(3)]
    return pl.pallas_call(
        body, name=name, grid=(S // tm,),
        in_specs=[row(D), pl.BlockSpec((1, D), lambda i: (0, 0)), row(D), row(n_abc)] + dil_specs
                 + [row(GROUP_W), pl.BlockSpec((D, N), lambda i: (0, 0))],
        out_specs=[row(D), row(N), row(D), pl.BlockSpec((1, D), lambda i: (0, 0))],
        out_shape=[jax.ShapeDtypeStruct((S, D), F32), jax.ShapeDtypeStruct((S, N), MXU_DTYPE),
                   jax.ShapeDtypeStruct((S, D), MXU_DTYPE), jax.ShapeDtypeStruct((1, D), F32)],
        scratch_shapes=[_lane_scratch(tm, GROUP_W)] * 2,
        compiler_params=_params(("arbitrary",)),
    )(x, g, dxn, dz_abc, *flat, dz_g, w_in)


def _inproj_wgrad(h, dz, name):
    S, D = h.shape
    N = dz.shape[1]
    tm = TM_MM
    nj = 2
    cw = N // nj
    per = N_DEV // nj
    n_loc = N // N_DEV

    def body(h_ref, dz_ref, dw_ref, acc_ref):
        i = pl.program_id(1)

        @pl.when(i == 0)
        def _():
            acc_ref[...] = jnp.zeros_like(acc_ref)

        acc_ref[...] += _mm_tn(h_ref[...], dz_ref[...])

        @pl.when(i == S // tm - 1)
        def _():
            for b in range(per):
                dw_ref[b] = acc_ref[:, b * n_loc:(b + 1) * n_loc].astype(dw_ref.dtype)

    return pl.pallas_call(
        body, name=name, grid=(nj, S // tm),
        in_specs=[pl.BlockSpec((tm, D), lambda j, i: (i, 0)), pl.BlockSpec((tm, cw), lambda j, i: (i, j))],
        out_specs=pl.BlockSpec((per, D, n_loc), lambda j, i: (j, 0, 0)),
        out_shape=jax.ShapeDtypeStruct((N_DEV, D, n_loc), WIRE_DTYPE),
        scratch_shapes=[pltpu.VMEM((D, cw), F32)],
        compiler_params=_params(("parallel", "arbitrary")),
    )(h, dz)


def _my_place():
    return lax.axis_index("x"), lax.axis_index("y"), lax.axis_index("c")


def _peer(x, y, c, k):
    px = 1 - x if k & 4 else x
    py = 1 - y if k & 2 else y
    pc = 1 - c if k & 1 else c
    return (px, py, pc), 4 * px + 2 * py + pc


HBM_SPEC = pl.BlockSpec(memory_space=pltpu.HBM)
SEM_SPEC = pl.BlockSpec(memory_space=pltpu.SEMAPHORE)
SPLIT_EFFECT = pltpu.SideEffectType.DATAFLOW_SIDE_EFFECTING
N_PEERS = N_DEV - 1


def _exchange_copies(srcs, lands, send_sems, recv_sems, whole, arrival):
    x, y, c = _my_place()
    me = 4 * x + 2 * y + c
    copies = []
    for t in range(len(srcs)):
        for k in range(1, N_DEV):
            peer, pidx = _peer(x, y, c, k)
            copies.append(pltpu.make_async_remote_copy(
                src_ref=srcs[t] if whole[t] else srcs[t].at[pidx],
                dst_ref=lands[t].at[pidx if arrival else me], send_sem=send_sems.at[t * N_PEERS + k - 1],
                recv_sem=recv_sems.at[t * N_PEERS + k - 1], device_id=peer, device_id_type=MESH))
    return copies


def _exchange_start(groups, name, after=None):
    sizes = [len(g) for g in groups]
    whole = [w for g in groups for _, w in g]
    srcs = [pltpu.with_memory_space_constraint(a, pltpu.HBM) for g in groups for a, _ in g]
    lands = [pltpu.with_memory_space_constraint(lax.empty(((N_DEV,) + a.shape) if w else a.shape, a.dtype), pltpu.HBM)
             for a, w in zip(srcs, whole)]
    n = len(srcs)
    n_g = len(groups)
    extra = [] if after is None else [after]
    n_in = 2 * n + len(extra)

    def body(*refs):
        src_refs, land_refs = refs[:n], refs[n:2 * n]
        sem_refs = refs[n_in + 2 * n:n_in + 2 * n + 2 * n_g]
        token = refs[-1]
        off = 0
        for gi, sz in enumerate(sizes):
            for send in _exchange_copies(src_refs[off:off + sz], land_refs[off:off + sz],
                                         sem_refs[2 * gi], sem_refs[2 * gi + 1], whole[off:off + sz], False):
                send.start()
            off += sz
        token[...] = jnp.zeros_like(token)

    sem_shapes = [pltpu.SemaphoreType.DMA((sz * N_PEERS,)) for sz in sizes for _ in range(2)]
    outs = pl.pallas_call(
        body, name=name,
        in_specs=[HBM_SPEC] * (2 * n) + [pl.BlockSpec(memory_space=pl.ANY)] * len(extra),
        out_specs=[HBM_SPEC] * (2 * n) + [SEM_SPEC] * (2 * n_g) + [pl.BlockSpec(memory_space=pltpu.VMEM)],
        out_shape=[pltpu.HBM(a.shape, a.dtype) for a in srcs + lands] + sem_shapes
                  + [jax.ShapeDtypeStruct((SUBLANES, LANES), F32)],
        input_output_aliases={i: i for i in range(2 * n)},
        compiler_params=pltpu.CompilerParams(has_side_effects=SPLIT_EFFECT),
    )(*srcs, *lands, *extra)
    handles, off = [], 0
    for gi, sz in enumerate(sizes):
        handles.append((outs[2 * n + 2 * gi], outs[2 * n + 2 * gi + 1], outs[off:off + sz], outs[n + off:n + off + sz],
                        whole[off:off + sz]))
        off += sz
    return handles, outs[-1]


def _exchange_wait(handle, after, name):
    send_sems, recv_sems, srcs, lands, whole = handle
    n = len(srcs)

    def body(*refs):
        src_refs, land_refs = refs[:n], refs[n:2 * n]
        for send in _exchange_copies(src_refs, land_refs, refs[2 * n], refs[2 * n + 1], whole, False):
            send.wait_send()
        for arrival in _exchange_copies(src_refs, land_refs, refs[2 * n], refs[2 * n + 1], whole, True):
            arrival.wait_recv()

    outs = pl.pallas_call(
        body, name=name,
        in_specs=[HBM_SPEC] * (2 * n) + [SEM_SPEC, SEM_SPEC, pl.BlockSpec(memory_space=pl.ANY)],
        out_specs=[HBM_SPEC] * (2 * n),
        out_shape=[pltpu.HBM(a.shape, a.dtype) for a in list(srcs) + list(lands)],
        input_output_aliases={i: i for i in range(2 * n)},
        compiler_params=pltpu.CompilerParams(has_side_effects=SPLIT_EFFECT),
    )(*srcs, *lands, send_sems, recv_sems, after)
    x, y, c = _my_place()
    me = 4 * x + 2 * y + c
    own = [s[None] if w else lax.dynamic_slice_in_dim(s, me, 1, axis=0) for s, w in zip(outs[:n], whole)]
    return [lax.dynamic_update_slice_in_dim(ld, o, me, axis=0) for ld, o in zip(outs[n:], own)]


def _sum_slots(parts, name):
    n = len(parts)

    def body(*refs):
        for p_ref, o_ref in zip(refs[:n], refs[n:]):
            acc = p_ref[0]
            for j in range(1, N_DEV):
                acc = acc + p_ref[j]
            o_ref[...] = acc

    vm = pl.BlockSpec(memory_space=pltpu.VMEM)
    return pl.pallas_call(
        body, name=name, in_specs=[vm] * n, out_specs=[vm] * n,
        out_shape=[jax.ShapeDtypeStruct(p.shape[1:], F32) for p in parts],
        compiler_params=pltpu.CompilerParams(vmem_limit_bytes=VMEM_LIMIT),
    )(*parts)


def _adamw_math(w, g, m, v):
    m = ADAM_B1 * m + (1.0 - ADAM_B1) * g
    v = ADAM_B2 * v + (1.0 - ADAM_B2) * (g * g)
    m_hat = m / (1.0 - ADAM_B1 ** ADAM_STEP)
    v_hat = v / (1.0 - ADAM_B2 ** ADAM_STEP)
    delta = -ADAM_LR * (m_hat / (jnp.sqrt(v_hat) + ADAM_EPS) + ADAM_WD * w)
    return delta, m, v


def _adamw_summed(parts, w, m, v, tr, name):
    depth, R, C = w.shape

    def body(*refs):
        p_refs = refs[:depth]
        w_ref, m_ref, v_ref, g_ref, d_ref, nm_ref, nv_ref = refs[depth:]
        lay = pl.program_id(0)
        for l in range(depth):
            @pl.when(lay == l)
            def _(p_ref=p_refs[l]):
                g = p_ref[0].astype(F32)
                for j in range(1, N_DEV):
                    g = g + p_ref[j].astype(F32)
                g_ref[0] = g
        d_ref[0], nm_ref[0], nv_ref[0] = _adamw_math(w_ref[0], g_ref[0], m_ref[0], v_ref[0])

    part_spec = lambda l: pl.BlockSpec((N_DEV, tr, C), lambda lay, i: (0, jnp.where(lay == l, i, 0), 0))
    row = pl.BlockSpec((1, tr, C), lambda lay, i: (lay, i, 0))
    return pl.pallas_call(
        body, name=name, grid=(depth, R // tr),
        in_specs=[part_spec(l) for l in range(depth)] + [row, row, row],
        out_specs=[row] * 4, out_shape=[jax.ShapeDtypeStruct((depth, R, C), F32)] * 4,
        compiler_params=_params(("arbitrary", "arbitrary")),
    )(*parts, w, m, v)


def _adamw_small(w, g, m, v, name):
    def body(w_ref, g_ref, m_ref, v_ref, d_ref, nm_ref, nv_ref):
        d_ref[...], nm_ref[...], nv_ref[...] = _adamw_math(w_ref[...], g_ref[...], m_ref[...], v_ref[...])

    vm = pl.BlockSpec(memory_space=pltpu.VMEM)
    return pl.pallas_call(
        body, name=name, in_specs=[vm] * 4, out_specs=[vm] * 3,
        out_shape=[jax.ShapeDtypeStruct(w.shape, F32)] * 3,
        compiler_params=pltpu.CompilerParams(vmem_limit_bytes=VMEM_LIMIT),
    )(w, g, m, v)


def _pack(arrays):
    flat = jnp.concatenate([a.reshape(-1) for a in arrays])
    pad = (-flat.shape[0]) % (SUBLANES * LANES)
    return jnp.pad(flat, (0, pad)).reshape(-1, LANES)


def _unpack(buf, like):
    flat = buf.reshape(-1)
    out, off = [], 0
    for a in like:
        out.append(flat[off:off + a.size].reshape(a.shape))
        off += a.size
    return out


def _block_diag(w):
    eye = jnp.eye(N_HEADS, dtype=w.dtype)
    return jnp.einsum('hij,hk->hikj', w, eye).reshape(GROUP_W, GROUP_W)


def _diag_blocks(w):
    return jnp.einsum('hihj->hij', w.reshape(N_HEADS, HEAD_DIM, N_HEADS, HEAD_DIM))


def _pad_rows(a):
    return jnp.pad(a, ((0, SUBLANES - a.shape[0]), (0, 0)))


def _mixer_params(l, conv_a_w, conv_r_w, conv_r_b, lru_wa, lru_ba, lru_wx, lru_bx, lru_lambda, gmlp_norm_g,
                  gmlp_ws, gmlp_bs):
    tril = jnp.tril(jnp.ones((GMLP_CHUNK, GMLP_CHUNK), dtype=bool))
    vec = jnp.stack([conv_r_b[l], lru_ba[l], lru_bx[l], lru_lambda[l], gmlp_norm_g[l]])
    return {
        "wA": _pad_rows(conv_a_w[l]), "wR": _pad_rows(conv_r_w[l]), "vec": _pad_rows(vec),
        "wa": _block_diag(lru_wa[l]).astype(MXU_DTYPE), "wx": _block_diag(lru_wx[l]).astype(MXU_DTYPE),
        "ws": jnp.where(tril[None], gmlp_ws[l], 0.0).astype(MXU_DTYPE),
        "bs": jnp.repeat(jnp.transpose(gmlp_bs[l]), HEAD_DIM, axis=1),
    }


MIXER_NAMES = ("conv_a_w", "conv_r_w", "conv_r_b", "lru_wa", "lru_ba", "lru_wx", "lru_bx", "lru_lambda",
               "gmlp_norm_g", "gmlp_ws", "gmlp_bs")
SMALL_NAMES = ("norm_g",) + MIXER_NAMES + ("final_g",)


def _local_step(x, loss_target, norm_g, get_w_in, get_w_out, emit_early, emit_late, conv_a_w, conv_r_w, conv_r_b,
                lru_wa, lru_ba, lru_wx, lru_bx, lru_lambda, gmlp_norm_g, gmlp_ws, gmlp_bs, final_g):
    depth = norm_g.shape[0]
    D = x.shape[1]
    small = (conv_a_w, conv_r_w, conv_r_b, lru_wa, lru_ba, lru_wx, lru_bx, lru_lambda, gmlp_norm_g, gmlp_ws, gmlp_bs)
    saved = []
    for l in range(depth):
        mp = _mixer_params(l, *small)
        w_in_l = get_w_in(l, x)
        z, z_g, *qkv = _norm_inproj(x, norm_g[l].reshape(1, D), w_in_l, f"norm_inproj_{l}")
        y_abc, hs = _mix_fwd(z, mp, f"mix_fwd_{l}")
        attn = [_attn_fwd(qkv[p], dil, f"attn_fwd_d{dil}_{l}") for p, dil in enumerate(ATTN_DILATIONS)]
        w_out_l = get_w_out(l, y_abc)
        x_new, y, o, *lse = _outproj(x, z_g, y_abc, attn, w_out_l, f"outproj_{l}")
        saved.append((x, z, z_g, qkv, hs, y, o, lse, mp, w_in_l, w_out_l))
        x = x_new
    dx, loss, d_final_g = _loss_head(x, final_g.reshape(1, D), loss_target, "loss_head")
    token = None
    for l in reversed(range(depth)):
        x_l, z, z_g, qkv, hs, y, o, lse, mp, w_in_l, w_out_l = saved[l]
        if token is not None:
            mp = dict(mp, vec=mp["vec"] + token[0, 0])
        dy, dw_out = _outproj_bwd(dx, y, w_out_l, f"outproj_bwd_{l}")
        (dz_abc, dz_g, do1, do4, do16, dl1, dl4, dl16, dwA, dwR, dvec, dwa, dwx, dws, dbs) = _mix_bwd(
            z, z_g, dy, hs, o, mp, f"mix_bwd_{l}")
        token = emit_early(l, dw_out, [
            dwA[:conv_a_w.shape[1]], dwR[:conv_r_w.shape[1]], dvec[0], _diag_blocks(dwa), dvec[1], _diag_blocks(dwx),
            dvec[2], dvec[3], dvec[4], dws, jnp.transpose(dbs[:, :N_HEADS])])
        g_row = norm_g[l].reshape(1, D)
        if token is not None:
            g_row = g_row + token[0, 0]
        dqkv = [_attn_bwd(qkv[p], do, lse[p], dl, dil, f"attn_bwd_d{dil}_{l}")
                for p, (dil, do, dl) in enumerate(zip(ATTN_DILATIONS, (do1, do4, do16), (dl1, dl4, dl16)))]
        dx, dz, h, dng = _inproj_bwd(x_l, g_row, dx, dz_abc, dqkv, dz_g, w_in_l, f"inproj_bwd_{l}")
        dw_in = _inproj_wgrad(h, dz, f"inproj_wgrad_{l}")
        token = emit_late(l, dw_in, [dng[0]] + ([d_final_g[0]] if l == depth - 1 else []))
    return loss[0, 0], dx
WEIGHT_NAMES = ("norm_g", "w_in", "conv_a_w", "conv_r_w", "conv_r_b", "lru_wa", "lru_ba", "lru_wx", "lru_bx",
                "lru_lambda", "gmlp_norm_g", "gmlp_ws", "gmlp_bs", "w_out", "final_g")


def kernel(x, norm_g, w_in, conv_a_w, conv_r_w, conv_r_b, lru_wa, lru_ba, lru_wx, lru_bx, lru_lambda, gmlp_norm_g, gmlp_ws, gmlp_bs, w_out, final_g, loss_target, m_norm_g, m_w_in, m_conv_a_w, m_conv_r_w, m_conv_r_b, m_lru_wa, m_lru_ba, m_lru_wx, m_lru_bx, m_lru_lambda, m_gmlp_norm_g, m_gmlp_ws, m_gmlp_bs, m_w_out, m_final_g, v_norm_g, v_w_in, v_conv_a_w, v_conv_r_w, v_conv_r_b, v_lru_wa, v_lru_ba, v_lru_wx, v_lru_bx, v_lru_lambda, v_gmlp_norm_g, v_gmlp_ws, v_gmlp_bs, v_w_out, v_final_g):
    w = dict(norm_g=norm_g, w_in=w_in, conv_a_w=conv_a_w, conv_r_w=conv_r_w, conv_r_b=conv_r_b, lru_wa=lru_wa,
             lru_ba=lru_ba, lru_wx=lru_wx, lru_bx=lru_bx, lru_lambda=lru_lambda, gmlp_norm_g=gmlp_norm_g,
             gmlp_ws=gmlp_ws, gmlp_bs=gmlp_bs, w_out=w_out, final_g=final_g)
    m = dict(norm_g=m_norm_g, w_in=m_w_in, conv_a_w=m_conv_a_w, conv_r_w=m_conv_r_w, conv_r_b=m_conv_r_b,
             lru_wa=m_lru_wa, lru_ba=m_lru_ba, lru_wx=m_lru_wx, lru_bx=m_lru_bx, lru_lambda=m_lru_lambda,
             gmlp_norm_g=m_gmlp_norm_g, gmlp_ws=m_gmlp_ws, gmlp_bs=m_gmlp_bs, w_out=m_w_out, final_g=m_final_g)
    v = dict(norm_g=v_norm_g, w_in=v_w_in, conv_a_w=v_conv_a_w, conv_r_w=v_conv_r_w, conv_r_b=v_conv_r_b,
             lru_wa=v_lru_wa, lru_ba=v_lru_ba, lru_wx=v_lru_wx, lru_bx=v_lru_bx, lru_lambda=v_lru_lambda,
             gmlp_norm_g=v_gmlp_norm_g, gmlp_ws=v_gmlp_ws, gmlp_bs=v_gmlp_bs, w_out=v_w_out, final_g=v_final_g)
    depth, D, n_loc = w_in.shape
    e_loc = w_out.shape[1]
    cx, cy, cc = _my_place()
    me = 4 * cx + 2 * cy + cc

    w_in_w, w_out_w = w_in.astype(MXU_DTYPE), w_out.astype(MXU_DTYPE)
    c_loc = conv_a_w.shape[2]
    taps = (conv_a_w, conv_r_w)
    first, _ = _exchange_start([[(w_in_w[0], True), (_pack(taps), True)]], "gather_start_first")
    full_in = lambda g: jnp.transpose(g, (1, 0, 2)).reshape(D, N_DEV * n_loc)
    full_out = lambda g: g.reshape(N_DEV * e_loc, D)

    g_in0, g_taps = _exchange_wait(first[0], x, "gather_wait_in_0")
    groups = [[(w_out_w[0], True)]] + [[(w_in_w[l], True), (w_out_w[l], True)] for l in range(1, depth)]
    gathers, _ = _exchange_start(groups, "gather_start_rest", after=g_taps)
    g_taps = g_taps.reshape(N_DEV, -1)
    conv_full, off = [], 0
    for a in taps:
        part = g_taps[:, off:off + a.size].reshape((N_DEV,) + a.shape)
        conv_full.append(jnp.transpose(part, (1, 2, 0, 3)).reshape(a.shape[:2] + (N_DEV * c_loc,)))
        off += a.size
    conv_a_full, conv_r_full = conv_full
    later = {}

    def get_w_in(l, after):
        if l == 0:
            return full_in(g_in0)
        g_in, later[l] = _exchange_wait(gathers[l], after, f"gather_wait_{l}")
        return full_in(g_in)

    def get_w_out(l, after):
        if l == 0:
            return full_out(_exchange_wait(gathers[0], after, "gather_wait_out_0")[0])
        return full_out(later[l])

    early, late, last_token = {}, {}, [None]

    def emit_early(l, dw_out, mixer_grads):
        handles, token = _exchange_start(
            [[(dw_out.reshape(N_DEV, e_loc, D), False), (_pack(mixer_grads), True)]], f"early_start_{l}")
        early[l] = (handles[0], mixer_grads)
        return token

    def emit_late(l, dw_in, norm_grads):
        handles, token = _exchange_start([[(_pack(norm_grads), True)], [(dw_in, False)]], f"late_start_{l}")
        late[l] = (handles[0], handles[1], norm_grads)
        last_token[0] = token
        return token

    loss, grad_x = _local_step(
        x[0], loss_target[0], norm_g, get_w_in, get_w_out, emit_early, emit_late, conv_a_full, conv_r_full, conv_r_b,
        lru_wa, lru_ba, lru_wx, lru_bx, lru_lambda, gmlp_norm_g, gmlp_ws, gmlp_bs, final_g)
    loss = lax.psum(loss, ("x", "y", "c"))

    r_in, r_out, small_parts = {}, {}, []
    for l in reversed(range(depth)):
        r_out[l], r_mix = _exchange_wait(early[l][0], last_token[0], f"early_wait_{l}")
        (r_norm,) = _exchange_wait(late[l][0], last_token[0], f"late_wait_norm_{l}")
        small_parts += [r_mix, r_norm]
        if l > 0:
            (r_in[l],) = _exchange_wait(late[l][1], last_token[0], f"late_wait_{l}")
    big = {"w_out": _adamw_summed([r_out[l] for l in range(depth)], w_out, m_w_out, v_w_out, 128, "adamw_w_out")}

    sums = _sum_slots(small_parts, "sum_small_grads")
    by_layer = {}
    for i, l in enumerate(reversed(range(depth))):
        mix = _unpack(sums[2 * i], early[l][1])
        nrm = _unpack(sums[2 * i + 1], late[l][2])
        by_layer[l] = dict(zip(MIXER_NAMES, mix), norm_g=nrm[0])
        if l == depth - 1:
            g_final = nrm[1]
    g_small = {k: jnp.stack([by_layer[l][k] for l in range(depth)]) for k in ("norm_g",) + MIXER_NAMES}
    g_small["final_g"] = g_final
    for k in ("conv_a_w", "conv_r_w"):
        g_small[k] = lax.dynamic_slice_in_dim(g_small[k], me * c_loc, c_loc, axis=2)
    packs = [_pack([d[k] for k in SMALL_NAMES]) for d in (w, g_small, m, v)]
    res = _adamw_small(*packs, "adamw_small")
    like = [w[k] for k in SMALL_NAMES]
    d_s, m_s, v_s = (dict(zip(SMALL_NAMES, _unpack(r, like))) for r in res)

    (r_in[0],) = _exchange_wait(late[0][1], res[0], "late_wait_0")
    big["w_in"] = _adamw_summed([r_in[l] for l in range(depth)], w_in, m_w_in, v_w_in, 512, "adamw_w_in")

    grad, delta, new_m, new_v = {}, {}, {}, {}
    for k in WEIGHT_NAMES:
        if k in big:
            grad[k], delta[k], new_m[k], new_v[k] = big[k]
        else:
            grad[k], delta[k], new_m[k], new_v[k] = g_small[k], d_s[k], m_s[k], v_s[k]
    return (loss, grad_x[None], *[grad[k] for k in WEIGHT_NAMES], *[delta[k] for k in WEIGHT_NAMES],
            *[new_m[k] for k in WEIGHT_NAMES], *[new_v[k] for k in WEIGHT_NAMES])
```

```python
import functools
import math

import jax
import jax.numpy as jnp
from jax import lax
from jax.experimental import pallas as pl
from jax.experimental.pallas import tpu as pltpu

F32 = jnp.float32
MXU_DTYPE = jnp.bfloat16
WIRE_DTYPE = jnp.bfloat16
MESH = pl.DeviceIdType.MESH

N_DEV = 8
GROUP_W = 256
N_HEADS = 4
HEAD_DIM = 64
N_CHUNKS = 13
N_ABC = 9
GMLP_CHUNK = 128
ATTN_BLOCK = 128
ATTN_BLOCKS_PER_STEP = 4
ATTN_DILATIONS = (1, 4, 16)
NORM_EPS = 1e-6
RG_C = 8.0
SUBLANES = 8
LANES = 128
VMEM_LIMIT = 56 * 1024 * 1024

ADAM_LR = 0.001
ADAM_B1 = 0.9
ADAM_B2 = 0.999
ADAM_EPS = 1e-08
ADAM_WD = 0.01
ADAM_STEP = 10

TM_MIX = 512
TM_MM = 512


def _params(sem, vmem=VMEM_LIMIT):
    return pltpu.CompilerParams(dimension_semantics=sem, vmem_limit_bytes=vmem)


def _mm(a, b):
    return jnp.dot(a.astype(MXU_DTYPE), b.astype(MXU_DTYPE), preferred_element_type=F32)


def _mm_tn(a, b):
    return lax.dot_general(a.astype(MXU_DTYPE), b.astype(MXU_DTYPE), (((0,), (0,)), ((), ())),
                           preferred_element_type=F32)


def _mm_nt(a, b):
    return lax.dot_general(a.astype(MXU_DTYPE), b.astype(MXU_DTYPE), (((1,), (1,)), ((), ())),
                           preferred_element_type=F32)


def _sigmoid(x):
    return 0.5 * jnp.tanh(0.5 * x) + 0.5


def _silu_and_grad(x):
    s = _sigmoid(x)
    return x * s, s * (1.0 + x * (1.0 - s))


_GELU_K = math.sqrt(2.0 / math.pi)
_GELU_C = 0.044715


def _gelu_and_grad(x):
    x2 = x * x
    t = jnp.tanh(_GELU_K * (x + _GELU_C * x * x2))
    val = 0.5 * x * (1.0 + t)
    grad = 0.5 * (1.0 + t) + 0.5 * x * (1.0 - t * t) * (_GELU_K * (1.0 + 3.0 * _GELU_C * x2))
    return val, grad


def _gelu(x):
    return 0.5 * x * (1.0 + jnp.tanh(_GELU_K * (x + _GELU_C * x * x * x)))


def _expm1_nonpos(u):
    poly = 1.0 / math.factorial(9)
    for k in range(8, 0, -1):
        poly = poly * u + 1.0 / math.factorial(k)
    return jnp.where(u > -0.25, poly * u, jnp.exp(u) - 1.0)


def _softplus(x):
    return jnp.maximum(x, 0.0) + jnp.log(1.0 + jnp.exp(-jnp.abs(x)))


def _shift_down(t, halo, k):
    rolled = pltpu.roll(t, k, 0)
    hr = pltpu.roll(halo, k, 0)
    row = lax.broadcasted_iota(jnp.int32, halo.shape, 0)
    first = jnp.where(row < k, hr, rolled[0:SUBLANES])
    return jnp.concatenate([first, rolled[SUBLANES:]], axis=0)


def _shift_up(t, nxt, k):
    tm = t.shape[0]
    rolled = pltpu.roll(t, tm - k, 0)
    nr = pltpu.roll(nxt, SUBLANES - k, 0)
    row = lax.broadcasted_iota(jnp.int32, nxt.shape, 0)
    last = jnp.where(row >= SUBLANES - k, nr, rolled[tm - SUBLANES:tm])
    return jnp.concatenate([rolled[:tm - SUBLANES], last], axis=0)


def _scan_fwd(a, b):
    tm = a.shape[0]
    row = lax.broadcasted_iota(jnp.int32, a.shape, 0)
    s = 1
    while s < tm:
        a_s = pltpu.roll(a, s, 0)
        b_s = pltpu.roll(b, s, 0)
        m = row >= s
        b = jnp.where(m, a * b_s + b, b)
        a = jnp.where(m, a * a_s, a)
        s *= 2
    return a, b


def _scan_rev(a, g):
    tm = a.shape[0]
    row = lax.broadcasted_iota(jnp.int32, a.shape, 0)
    s = 1
    while s < tm:
        a_s = pltpu.roll(a, tm - s, 0)
        g_s = pltpu.roll(g, tm - s, 0)
        m = row < tm - s
        g = jnp.where(m, g + a * g_s, g)
        a = jnp.where(m, a * a_s, a)
        s *= 2
    return g


def _group_rows(scr_ref, row, n_groups):
    return jnp.concatenate([scr_ref[pl.ds(c, 1), pl.ds(row, n_groups, stride=SUBLANES), :][0]
                            for c in range(scr_ref.shape[0])], axis=1)


def _spread_rows(rows_ref, n_groups, w):
    return jnp.concatenate([jnp.broadcast_to(rows_ref[g:g + 1, :], (SUBLANES, w)) for g in range(n_groups)], axis=0)


def _scan_groups(a, b, reverse):
    tm, w = a.shape
    shape3 = (tm // SUBLANES, SUBLANES, w)
    a3, b3 = a.reshape(shape3), b.reshape(shape3)
    sub = lax.broadcasted_iota(jnp.int32, shape3, 1)
    s = 1
    while s < SUBLANES:
        shift = SUBLANES - s if reverse else s
        a_s = pltpu.roll(a3, shift, 1)
        b_s = pltpu.roll(b3, shift, 1)
        m = (sub < SUBLANES - s) if reverse else (sub >= s)
        b3 = jnp.where(m, a3 * b_s + b3, b3)
        a3 = jnp.where(m, a3 * a_s, a3)
        s *= 2
    return a3.reshape(tm, w), b3.reshape(tm, w)


def _scan_fwd_tile(a, b, h_in, sa_ref, sb_ref, sc_ref):
    tm, w = a.shape
    n_groups = tm // SUBLANES
    a_loc, b_loc = _scan_groups(a, b, False)
    _put(sa_ref, a_loc)
    _put(sb_ref, b_loc)
    a_end, b_end = _scan_fwd(_group_rows(sa_ref, SUBLANES - 1, n_groups), _group_rows(sb_ref, SUBLANES - 1, n_groups))
    h_end = b_end + a_end * h_in
    sc_ref[...] = _shift_down(h_end, jnp.broadcast_to(h_in, (SUBLANES, w)), 1)
    return b_loc + a_loc * _spread_rows(sc_ref, n_groups, w), h_end


def _scan_rev_tile(a, g, sa_ref, sb_ref, sc_ref):
    tm, w = a.shape
    n_groups = tm // SUBLANES
    a_loc, g_loc = _scan_groups(a, g, True)
    _put(sa_ref, a_loc)
    _put(sb_ref, g_loc)
    d_first = _scan_rev(_group_rows(sa_ref, 0, n_groups), _group_rows(sb_ref, 0, n_groups))
    sc_ref[...] = _shift_up(d_first, jnp.zeros((SUBLANES, w), F32), 1)
    return g_loc + a_loc * _spread_rows(sc_ref, n_groups, w)


def _lane_scratch(tm, w):
    return pltpu.VMEM((w // LANES, tm, LANES), F32)


def _put(scr_ref, val):
    for c in range(scr_ref.shape[0]):
        scr_ref[c] = val[:, c * LANES:(c + 1) * LANES].astype(F32)


def _get(scr_ref):
    return jnp.concatenate([scr_ref[c] for c in range(scr_ref.shape[0])], axis=1)


def _deinterleave(src_ref, dst_ref, dil):
    nc, tm, _ = src_ref.shape
    w = nc * LANES
    for r in range(dil):
        for c in range(nc):
            piece = src_ref[pl.ds(c, 1), pl.ds(r, tm // dil, stride=dil), :][0] if dil > 1 else src_ref[c]
            dst_ref[:, r * w + c * LANES:r * w + (c + 1) * LANES] = piece.astype(dst_ref.dtype)


def _interleave(src_ref, dst_ref, dil):
    nc, tm, _ = dst_ref.shape
    w = nc * LANES
    for r in range(dil):
        for c in range(nc):
            dst_ref[pl.ds(c, 1), pl.ds(r, tm // dil, stride=dil), :] = (
                src_ref[:, r * w + c * LANES:r * w + (c + 1) * LANES].astype(F32)[None])


def _dilated_spec(tm, w, dil, index=lambda i: i):
    return pl.BlockSpec((tm // dil, dil * w), lambda i: (index(i), 0))


def _dilated_shape(S, w, dil, dtype):
    return jax.ShapeDtypeStruct((S // dil, dil * w), dtype)


def _head_masks(shape):
    lane = lax.broadcasted_iota(jnp.int32, shape, 1)
    return [(lane >= h * HEAD_DIM) & (lane < (h + 1) * HEAD_DIM) for h in range(N_HEADS)]


def _colsum(v):
    return jnp.sum(v, axis=0, keepdims=True)


def _assemble_columns(blocks_ref, full_ref):
    c = blocks_ref.shape[2]
    for j in range(blocks_ref.shape[0]):
        full_ref[:, j * c:(j + 1) * c] = blocks_ref[j]


def _norm_inproj(x, g, w8, name):
    S, D = x.shape
    N = w8.shape[0] * w8.shape[2]
    tm = TM_MM
    n_abc = N_ABC * GROUP_W
    n_qkv = 3 * GROUP_W

    def body(x_ref, g_ref, w8_ref, zabc_ref, zg_ref, q1_ref, q4_ref, q16_ref, qkv_ref, w_ref):
        @pl.when(pl.program_id(0) == 0)
        def _():
            _assemble_columns(w8_ref, w_ref)

        xv = x_ref[...]
        r = lax.rsqrt(jnp.mean(xv * xv, axis=-1, keepdims=True) + NORM_EPS)
        h = ((xv * r) * g_ref[...]).astype(MXU_DTYPE)
        zabc_ref[...] = jnp.dot(h, w_ref[:, 0:n_abc], preferred_element_type=F32)
        _put(qkv_ref, jnp.dot(h, w_ref[:, n_abc:n_abc + n_qkv], preferred_element_type=F32))
        zg_ref[...] = jnp.dot(h, w_ref[:, n_abc + n_qkv:], preferred_element_type=F32)
        for dil, ref in zip(ATTN_DILATIONS, (q1_ref, q4_ref, q16_ref)):
            _deinterleave(qkv_ref, ref, dil)

    row = lambda wd: pl.BlockSpec((tm, wd), lambda i: (i, 0))
    return pl.pallas_call(
        body, name=name, grid=(S // tm,),
        in_specs=[row(D), pl.BlockSpec((1, D), lambda i: (0, 0)),
                  pl.BlockSpec(w8.shape, lambda i: (0, 0, 0), pipeline_mode=pl.Buffered(1))],
        out_specs=[row(n_abc), row(GROUP_W)] + [_dilated_spec(tm, n_qkv, dil) for dil in ATTN_DILATIONS],
        out_shape=[jax.ShapeDtypeStruct((S, n_abc), F32), jax.ShapeDtypeStruct((S, GROUP_W), F32)]
                  + [_dilated_shape(S, n_qkv, dil, MXU_DTYPE) for dil in ATTN_DILATIONS],
        scratch_shapes=[_lane_scratch(tm, n_qkv), pltpu.VMEM((D, N), w8.dtype)],
        compiler_params=_params(("arbitrary",)),
    )(x, g, w8)


def _conv_a(z_of, halo_of, w_ref):
    p = z_of(2) * z_of(0)
    p_h = halo_of(2) * halo_of(0)
    cv = w_ref[2:3, :] * p + w_ref[1:2, :] * _shift_down(p, p_h, 1) + w_ref[0:1, :] * _shift_down(p, p_h, 2)
    return p, p_h, cv


def _lru_gates(z_of, halo_of, wr_ref, vec_ref, wa_ref, wx_ref):
    rx = z_of(4)
    rx_h = halo_of(4)
    sh = [rx, _shift_down(rx, rx_h, 1), _shift_down(rx, rx_h, 2), _shift_down(rx, rx_h, 3)]
    xc = (wr_ref[3:4, :] * sh[0] + wr_ref[2:3, :] * sh[1] + wr_ref[1:2, :] * sh[2]
          + wr_ref[0:1, :] * sh[3] + vec_ref[0:1, :])
    ga = _sigmoid(jnp.dot(xc.astype(MXU_DTYPE), wa_ref[...], preferred_element_type=F32) + vec_ref[1:2, :])
    gi = _sigmoid(jnp.dot(xc.astype(MXU_DTYPE), wx_ref[...], preferred_element_type=F32) + vec_ref[2:3, :])
    sp = _softplus(-vec_ref[3:4, :])
    log_a = (-RG_C * ga) * sp
    a = jnp.exp(log_a)
    mult = jnp.sqrt(-_expm1_nonpos(2.0 * log_a))
    return xc, sh, ga, gi, a, mult, sp


def _gmlp_fwd(z_of, vec_ref, ws_ref, bs_ref, tm):
    u = _gelu(z_of(6))
    gv = _gelu(z_of(7))
    rr = lax.rsqrt(jnp.mean(gv * gv, axis=-1, keepdims=True) + NORM_EPS)
    vn = (gv * rr) * vec_ref[4:5, :]
    masks = _head_masks((GMLP_CHUNK, GROUP_W))
    parts = []
    for c in range(tm // GMLP_CHUNK):
        vc = vn[c * GMLP_CHUNK:(c + 1) * GMLP_CHUNK].astype(MXU_DTYPE)
        acc = bs_ref[...]
        for h in range(N_HEADS):
            acc = acc + jnp.where(masks[h], jnp.dot(ws_ref[h], vc, preferred_element_type=F32), 0.0)
        parts.append(acc)
    return u, gv, rr, vn, jnp.concatenate(parts, axis=0)


def _mix_specs(tm, S, order):
    const2 = lambda shape: pl.BlockSpec(shape, lambda i: (0, 0))
    return [const2((SUBLANES, GROUP_W)), const2((SUBLANES, GROUP_W)), const2((SUBLANES, GROUP_W)),
            const2((GROUP_W, GROUP_W)), const2((GROUP_W, GROUP_W)),
            pl.BlockSpec((N_HEADS, GMLP_CHUNK, GMLP_CHUNK), lambda i: (0, 0, 0)),
            const2((GMLP_CHUNK, GROUP_W))]


def _mix_fwd(z, mp, name):
    S = z.shape[0]
    tm = TM_MIX
    hb = tm // SUBLANES
    wcols = N_ABC * GROUP_W

    def body(z_ref, zh_ref, wA_ref, wR_ref, vec_ref, wa_ref, wx_ref, ws_ref, bs_ref, y_ref, h_ref, carry_ref,
             sa_ref, sb_ref, sc_ref):
        i = pl.program_id(0)

        @pl.when(i == 0)
        def _():
            carry_ref[...] = jnp.zeros_like(carry_ref)

        not_first = i > 0
        z_of = lambda c: z_ref[:, c * GROUP_W:(c + 1) * GROUP_W]
        halo_of = lambda c: jnp.where(not_first, zh_ref[:, c * GROUP_W:(c + 1) * GROUP_W], 0.0)

        _, _, cv = _conv_a(z_of, halo_of, wA_ref)
        y_ref[:, 0:GROUP_W] = (z_of(1) * cv * _silu_and_grad(z_of(3))[0]).astype(y_ref.dtype)

        xc, _, _, gi, a, mult, _ = _lru_gates(z_of, halo_of, wR_ref, vec_ref, wa_ref, wx_ref)
        b = mult * (gi * xc)
        h, h_end = _scan_fwd_tile(a, b, carry_ref[SUBLANES - 1:SUBLANES, :], sa_ref, sb_ref, sc_ref)
        h_ref[...] = h
        carry_ref[...] = h_end[hb - SUBLANES:hb]
        y_ref[:, GROUP_W:2 * GROUP_W] = (h * _silu_and_grad(z_of(5))[0]).astype(y_ref.dtype)

        u, _, _, _, sp = _gmlp_fwd(z_of, vec_ref, ws_ref, bs_ref, tm)
        y_ref[:, 2 * GROUP_W:3 * GROUP_W] = (u * sp * _silu_and_grad(z_of(8))[0]).astype(y_ref.dtype)

    return pl.pallas_call(
        body, name=name, grid=(S // tm,),
        in_specs=[pl.BlockSpec((tm, wcols), lambda i: (i, 0)),
                  pl.BlockSpec((SUBLANES, wcols), lambda i: (jnp.maximum(i * hb - 1, 0), 0))]
                 + _mix_specs(tm, S, "fwd"),
        out_specs=[pl.BlockSpec((tm, 3 * GROUP_W), lambda i: (i, 0)),
                   pl.BlockSpec((tm, GROUP_W), lambda i: (i, 0))],
        out_shape=[jax.ShapeDtypeStruct((S, 3 * GROUP_W), MXU_DTYPE), jax.ShapeDtypeStruct((S, GROUP_W), F32)],
        scratch_shapes=[pltpu.VMEM((SUBLANES, GROUP_W), F32), _lane_scratch(tm, GROUP_W), _lane_scratch(tm, GROUP_W),
                        pltpu.VMEM((hb, GROUP_W), F32)],
        compiler_params=_params(("arbitrary",)),
    )(z, z, mp["wA"], mp["wR"], mp["vec"], mp["wa"], mp["wx"], mp["ws"], mp["bs"])


_NEG = -1e30


def _slope(h):
    return 2.0 ** (-8.0 * (h + 1) / N_HEADS)


def _attn_bias(dil, offsets, n_keys):
    shape = (ATTN_BLOCK, n_keys)
    qi = lax.broadcasted_iota(jnp.int32, shape, 0)
    ki = lax.broadcasted_iota(jnp.int32, shape, 1)
    blocks = []
    for f in offsets:
        delta = qi + f - ki
        valid = (delta >= 0) & (delta <= ATTN_BLOCK)
        dist = (delta * dil).astype(F32)
        for h in range(N_HEADS):
            blocks.append(jnp.where(valid, -_slope(h) * dist, _NEG))
    return jnp.concatenate(blocks, axis=0)


def _stack_heads(t, masks):
    return jnp.concatenate([jnp.where(m, t, jnp.zeros_like(t)) for m in masks], axis=0)


def _unstack_heads(t4, masks, base=0):
    out = t4[base * ATTN_BLOCK:(base + 1) * ATTN_BLOCK]
    for h in range(1, N_HEADS):
        out = jnp.where(masks[h], t4[(base + h) * ATTN_BLOCK:(base + h + 1) * ATTN_BLOCK], out)
    return out


def _attn_fwd(qkv, dil, name):
    rows = qkv.shape[0]
    nb = rows // ATTN_BLOCK
    scale = 1.0 / math.sqrt(HEAD_DIM)
    B = ATTN_BLOCK
    per_step = ATTN_BLOCKS_PER_STEP

    def body(q_ref, kc_ref, kp_ref, vc_ref, vp_ref, o_ref, l_ref, bias_ref):
        n = pl.program_id(1)

        @pl.when(n == 0)
        def _():
            bias_ref[...] = _attn_bias(dil, (B,), 2 * B)

        masks = _head_masks((B, GROUP_W))
        for j in range(per_step):
            own = slice(j * B, (j + 1) * B)
            before = slice((j - 1) * B, j * B)
            qs = _stack_heads(q_ref[own], masks)
            keys = jnp.concatenate([kp_ref[...] if j == 0 else kc_ref[before], kc_ref[own]], axis=0)
            vals = jnp.concatenate([vp_ref[...] if j == 0 else vc_ref[before], vc_ref[own]], axis=0)
            s = _mm_nt(qs, keys) * scale + bias_ref[...]
            if j == 0:
                key_col = lax.broadcasted_iota(jnp.int32, s.shape, 1)
                s = jnp.where((n == 0) & (key_col < B), _NEG, s)
            m = jnp.max(s, axis=-1, keepdims=True)
            p = jnp.exp(s - m)
            l = jnp.sum(p, axis=-1, keepdims=True)
            o4 = jnp.dot(p.astype(MXU_DTYPE), vals, preferred_element_type=F32)
            o_ref[own] = _unstack_heads(o4, masks) / _unstack_heads(jnp.broadcast_to(l, o4.shape), masks)
            l_ref[own] = _unstack_heads(jnp.broadcast_to(m + jnp.log(l), o4.shape), masks)

    blk = (per_step * B, GROUP_W)
    cur = lambda c: pl.BlockSpec(blk, lambda r, n: (n, r * 3 + c))
    prev = lambda c: pl.BlockSpec((B, GROUP_W), lambda r, n: (jnp.maximum(n * per_step - 1, 0), r * 3 + c))
    out = pl.BlockSpec(blk, lambda r, n: (n, r))
    return pl.pallas_call(
        body, name=name, grid=(dil, nb // per_step),
        in_specs=[cur(0), cur(1), prev(1), cur(2), prev(2)],
        out_specs=[out, out],
        out_shape=[jax.ShapeDtypeStruct((rows, dil * GROUP_W), F32)] * 2,
        scratch_shapes=[pltpu.VMEM((N_HEADS * ATTN_BLOCK, 2 * ATTN_BLOCK), F32)],
        compiler_params=_params(("parallel", "arbitrary")),
    )(qkv, qkv, qkv, qkv, qkv)


def _outproj(x, z_g, y_abc, attn, w_out, name):
    S, D = x.shape
    tm = TM_MM
    n_abc = 3 * GROUP_W

    def body(x_ref, g_ref, yabc_ref, o1, l1, o2, l2, o3, l3, w_ref,
             xn_ref, y_ref, o_ref, lse1_ref, lse4_ref, lse16_ref, so2, sl2, so3, sl3, slse):
        for src, dst, dil in ((o2, so2, ATTN_DILATIONS[1]), (l2, sl2, ATTN_DILATIONS[1]),
                              (o3, so3, ATTN_DILATIONS[2]), (l3, sl3, ATTN_DILATIONS[2])):
            _interleave(src, dst, dil)
        la, lb, lc = l1[...], _get(sl2), _get(sl3)
        mx = jnp.maximum(jnp.maximum(la, lb), lc)
        ea, eb, ec = jnp.exp(la - mx), jnp.exp(lb - mx), jnp.exp(lc - mx)
        den = ea + eb + ec
        o = (ea * o1[...] + eb * _get(so2) + ec * _get(so3)) / den
        o_ref[...] = o
        _put(slse, mx + jnp.log(den))
        for dil, ref in zip(ATTN_DILATIONS, (lse1_ref, lse4_ref, lse16_ref)):
            _deinterleave(slse, ref, dil)
        y_d = o * _silu_and_grad(g_ref[...])[0]
        y_ref[:, 0:n_abc] = yabc_ref[...].astype(MXU_DTYPE)
        y_ref[:, n_abc:] = y_d.astype(MXU_DTYPE)
        xn_ref[...] = x_ref[...] + jnp.dot(y_ref[...], w_ref[...], preferred_element_type=F32)

    row = lambda w: pl.BlockSpec((tm, w), lambda i: (i, 0))
    dil_specs = [_dilated_spec(tm, GROUP_W, dil) for dil in ATTN_DILATIONS]
    (o1, l1), (o2, l2), (o3, l3) = attn
    return pl.pallas_call(
        body, name=name, grid=(S // tm,),
        in_specs=[row(D), row(GROUP_W), row(n_abc)] + [sp for sp in dil_specs for _ in range(2)]
                 + [pl.BlockSpec(w_out.shape, lambda i: (0, 0))],
        out_specs=[row(D), row(4 * GROUP_W), row(GROUP_W)] + dil_specs,
        out_shape=[jax.ShapeDtypeStruct((S, D), F32), jax.ShapeDtypeStruct((S, 4 * GROUP_W), MXU_DTYPE),
                   jax.ShapeDtypeStruct((S, GROUP_W), F32)]
                  + [_dilated_shape(S, GROUP_W, dil, F32) for dil in ATTN_DILATIONS],
        scratch_shapes=[_lane_scratch(tm, GROUP_W)] * 5,
        compiler_params=_params(("parallel",)),
    )(x, z_g, y_abc, o1, l1, o2, l2, o3, l3, w_out)


def _loss_head(x, g, target, name):
    S, D = x.shape
    tm = TM_MM

    def body(x_ref, g_ref, t_ref, dx_ref, loss_ref, dg_ref):
        i = pl.program_id(0)

        @pl.when(i == 0)
        def _():
            loss_ref[...] = jnp.zeros_like(loss_ref)
            dg_ref[...] = jnp.zeros_like(dg_ref)

        xv = x_ref[...]
        r = lax.rsqrt(jnp.mean(xv * xv, axis=-1, keepdims=True) + NORM_EPS)
        xn = xv * r
        err = xn * g_ref[...] - t_ref[...]
        per_tok = jnp.mean(err * err, axis=-1, keepdims=True)
        loss_ref[...] += 0.5 * jnp.sum(per_tok, axis=0, keepdims=True)
        dout = err * (1.0 / D)
        dg_ref[...] += _colsum(dout * xn)
        dxn = dout * g_ref[...]
        dx_ref[...] = r * (dxn - xn * jnp.mean(dxn * xn, axis=-1, keepdims=True))

    row = pl.BlockSpec((tm, D), lambda i: (i, 0))
    return pl.pallas_call(
        body, name=name, grid=(S // tm,),
        in_specs=[row, pl.BlockSpec((1, D), lambda i: (0, 0)), row],
        out_specs=[row, pl.BlockSpec((1, LANES), lambda i: (0, 0)), pl.BlockSpec((1, D), lambda i: (0, 0))],
        out_shape=[jax.ShapeDtypeStruct((S, D), F32), jax.ShapeDtypeStruct((1, LANES), F32),
                   jax.ShapeDtypeStruct((1, D), F32)],
        compiler_params=_params(("arbitrary",)),
    )(x, g, target)


def _outproj_bwd(dx, y, w_out, name):
    S, D = dx.shape
    E = y.shape[1]
    tm = TM_MM

    def body(dx_ref, y_ref, w_ref, dy_ref, dw_ref, acc_ref):
        i = pl.program_id(0)

        @pl.when(i == 0)
        def _():
            acc_ref[...] = jnp.zeros_like(acc_ref)

        dxb = dx_ref[...].astype(MXU_DTYPE)
        dy_ref[...] = _mm_nt(dxb, w_ref[...])
        acc_ref[...] += _mm_tn(y_ref[...], dxb)

        @pl.when(i == S // tm - 1)
        def _():
            dw_ref[...] = acc_ref[...].astype(dw_ref.dtype)

    return pl.pallas_call(
        body, name=name, grid=(S // tm,),
        in_specs=[pl.BlockSpec((tm, D), lambda i: (i, 0)), pl.BlockSpec((tm, E), lambda i: (i, 0)),
                  pl.BlockSpec((E, D), lambda i: (0, 0))],
        out_specs=[pl.BlockSpec((tm, E), lambda i: (i, 0)), pl.BlockSpec((E, D), lambda i: (0, 0))],
        out_shape=[jax.ShapeDtypeStruct((S, E), F32), jax.ShapeDtypeStruct((E, D), WIRE_DTYPE)],
        scratch_shapes=[pltpu.VMEM((E, D), F32)],
        compiler_params=_params(("arbitrary",)),
    )(dx, y, w_out)


def _mix_bwd(z, z_g, dy, hs, o, mp, name):
    S = z.shape[0]
    tm = TM_MIX
    hb = tm // SUBLANES
    nT = S // tm
    last_blk = S // SUBLANES - 1
    wcols = N_ABC * GROUP_W

    def body(z_ref, zh_ref, zn_ref, zg_ref, dy_ref, dyn_ref, h_ref, hh_ref, o_ref,
             wA_ref, wR_ref, vec_ref, wa_ref, wx_ref, ws_ref, bs_ref,
             dz_ref, dzg_ref, do1_ref, do4_ref, do16_ref, dl1_ref, dl4_ref, dl16_ref,
             dwA_ref, dwR_ref, dvec_ref, dwa_ref, dwx_ref, dws_ref, dbs_ref,
             hcarry_ref, xcarry_ref, bsacc_ref, do_ref, dl_ref, sa_ref, sb_ref, sc_ref):
        i = pl.program_id(0)
        ti = nT - 1 - i

        @pl.when(i == 0)
        def _():
            hcarry_ref[...] = jnp.zeros_like(hcarry_ref)
            xcarry_ref[...] = jnp.zeros_like(xcarry_ref)
            bsacc_ref[...] = jnp.zeros_like(bsacc_ref)
            dwA_ref[...] = jnp.zeros_like(dwA_ref)
            dwR_ref[...] = jnp.zeros_like(dwR_ref)
            dvec_ref[...] = jnp.zeros_like(dvec_ref)
            dwa_ref[...] = jnp.zeros_like(dwa_ref)
            dwx_ref[...] = jnp.zeros_like(dwx_ref)
            dws_ref[...] = jnp.zeros_like(dws_ref)
            dbs_ref[...] = jnp.zeros_like(dbs_ref)

        has_prev = ti > 0
        has_next = i > 0
        col = lambda c: slice(c * GROUP_W, (c + 1) * GROUP_W)
        z_of = lambda c: z_ref[:, col(c)]
        halo_of = lambda c: jnp.where(has_prev, zh_ref[:, col(c)], 0.0)
        next_of = lambda c: zn_ref[:, col(c)]

        p, p_h, cv = _conv_a(z_of, halo_of, wA_ref)
        sg, dsg = _silu_and_grad(z_of(3))
        a_b = z_of(1)
        dya = dy_ref[:, col(0)]
        dcv = dya * a_b * sg
        dcv_n = jnp.where(has_next, dyn_ref[...] * next_of(1) * _silu_and_grad(next_of(3))[0], 0.0)
        dp = (wA_ref[2:3, :] * dcv + wA_ref[1:2, :] * _shift_up(dcv, dcv_n, 1)
              + wA_ref[0:1, :] * _shift_up(dcv, dcv_n, 2))
        dwA_ref[2:3, :] += _colsum(dcv * p)
        dwA_ref[1:2, :] += _colsum(dcv * _shift_down(p, p_h, 1))
        dwA_ref[0:1, :] += _colsum(dcv * _shift_down(p, p_h, 2))
        def put_dz(c, val):
            dz_ref[:, col(c)] = val.astype(dz_ref.dtype)

        put_dz(0, dp * z_of(2))
        put_dz(1, dya * cv * sg)
        put_dz(2, dp * z_of(0))
        put_dz(3, dya * a_b * cv * dsg)

        xc, sh, ga, gi, a, mult, sp = _lru_gates(z_of, halo_of, wR_ref, vec_ref, wa_ref, wx_ref)
        h = h_ref[...]
        h_prev = _shift_down(h, jnp.where(has_prev, hh_ref[...], 0.0), 1)
        sgr, dsgr = _silu_and_grad(z_of(5))
        dyb = dy_ref[:, col(1)]
        put_dz(5, dyb * h * dsgr)
        row = lax.broadcasted_iota(jnp.int32, (tm, GROUP_W), 0)
        g_in = dyb * sgr + jnp.where(row == tm - 1, hcarry_ref[0:1, :], 0.0)
        a_up = _shift_up(a, jnp.zeros((SUBLANES, GROUP_W), F32), 1)
        dH = _scan_rev_tile(a_up, g_in, sa_ref, sb_ref, sc_ref)
        hcarry_ref[...] = (a * dH)[0:SUBLANES]
        da = dH * h_prev
        gx = gi * xc
        dmult = dH * gx
        dgi = dH * mult * xc
        dxc = dH * mult * gi
        dlog_a = da * a - dmult * (a * a) / mult
        dga = dlog_a * (-RG_C * sp)
        dlam_row = _colsum(dlog_a * (-RG_C * ga)) * (-_sigmoid(-vec_ref[3:4, :]))
        dpre_a = dga * ga * (1.0 - ga)
        dpre_i = dgi * gi * (1.0 - gi)
        dwa_ref[...] += _mm_tn(xc, dpre_a)
        dwx_ref[...] += _mm_tn(xc, dpre_i)
        dxc = dxc + _mm_nt(dpre_a, wa_ref[...]) + _mm_nt(dpre_i, wx_ref[...])
        dvec_ref[0:1, :] += _colsum(dxc)
        dvec_ref[1:2, :] += _colsum(dpre_a)
        dvec_ref[2:3, :] += _colsum(dpre_i)
        dvec_ref[3:4, :] += dlam_row
        for k in range(4):
            dwR_ref[k:k + 1, :] += _colsum(dxc * sh[3 - k])
        dxc_n = xcarry_ref[...]
        put_dz(4, wR_ref[3:4, :] * dxc + wR_ref[2:3, :] * _shift_up(dxc, dxc_n, 1)
               + wR_ref[1:2, :] * _shift_up(dxc, dxc_n, 2) + wR_ref[0:1, :] * _shift_up(dxc, dxc_n, 3))
        xcarry_ref[...] = dxc[0:SUBLANES]

        c_u, c_v = z_of(6), z_of(7)
        u, du_dx = _gelu_and_grad(c_u)
        gv, dgv_dx = _gelu_and_grad(c_v)
        rr = lax.rsqrt(jnp.mean(gv * gv, axis=-1, keepdims=True) + NORM_EPS)
        xhat = gv * rr
        g_c = vec_ref[4:5, :]
        vn = xhat * g_c
        masks = _head_masks((GMLP_CHUNK, GROUP_W))
        tri_r = lax.broadcasted_iota(jnp.int32, (GMLP_CHUNK, GMLP_CHUNK), 0)
        tri_c = lax.broadcasted_iota(jnp.int32, (GMLP_CHUNK, GMLP_CHUNK), 1)
        tril = tri_r >= tri_c
        sgc, dsgc = _silu_and_grad(z_of(8))
        dyc = dy_ref[:, col(2)]
        dsp_full = dyc * u * sgc
        sp_parts, dvn_parts = [], []
        for c in range(tm // GMLP_CHUNK):
            rs = slice(c * GMLP_CHUNK, (c + 1) * GMLP_CHUNK)
            vc = vn[rs].astype(MXU_DTYPE)
            dsp_c = dsp_full[rs]
            bsacc_ref[...] += dsp_c
            acc = bs_ref[...]
            dvn_c = jnp.zeros((GMLP_CHUNK, GROUP_W), F32)
            for h in range(N_HEADS):
                w_h = ws_ref[h]
                acc = acc + jnp.where(masks[h], jnp.dot(w_h, vc, preferred_element_type=F32), 0.0)
                dsp_h = jnp.where(masks[h], dsp_c, 0.0).astype(MXU_DTYPE)
                dvn_c = dvn_c + _mm_tn(w_h, dsp_h)
                dws_ref[h] += jnp.where(tril, _mm_nt(dsp_h, vc), 0.0)
            sp_parts.append(acc)
            dvn_parts.append(dvn_c)
        spv = jnp.concatenate(sp_parts, axis=0)
        dvn = jnp.concatenate(dvn_parts, axis=0)
        put_dz(6, dyc * spv * sgc * du_dx)
        put_dz(8, dyc * u * spv * dsgc)
        dvec_ref[4:5, :] += _colsum(dvn * xhat)
        dgvn = dvn * g_c
        dgv = rr * (dgvn - xhat * jnp.mean(dgvn * xhat, axis=-1, keepdims=True))
        put_dz(7, dgv * dgv_dx)

        sgd, dsgd = _silu_and_grad(zg_ref[...])
        dyd = dy_ref[:, col(3)]
        ov = o_ref[...]
        do = dyd * sgd
        _put(do_ref, do)
        dzg_ref[...] = (dyd * ov * dsgd).astype(dzg_ref.dtype)
        prod = do * ov
        tmasks = _head_masks((tm, GROUP_W))
        dl = jnp.zeros((tm, GROUP_W), F32)
        for h in range(N_HEADS):
            dl = jnp.where(tmasks[h], jnp.sum(jnp.where(tmasks[h], prod, 0.0), axis=-1, keepdims=True), dl)
        _put(dl_ref, dl)
        for dil, d_out, l_out in zip(ATTN_DILATIONS, (do1_ref, do4_ref, do16_ref), (dl1_ref, dl4_ref, dl16_ref)):
            _deinterleave(do_ref, d_out, dil)
            _deinterleave(dl_ref, l_out, dil)

        @pl.when(i == nT - 1)
        def _():
            acc = bsacc_ref[...]
            lane = lax.broadcasted_iota(jnp.int32, (GMLP_CHUNK, LANES), 1)
            out = jnp.zeros((GMLP_CHUNK, LANES), F32)
            for h in range(N_HEADS):
                out = jnp.where(lane == h, jnp.sum(jnp.where(masks[h], acc, 0.0), axis=-1, keepdims=True), out)
            dbs_ref[...] = out

    rev = lambda w: pl.BlockSpec((tm, w), lambda i: (nT - 1 - i, 0))
    prev8 = lambda w: pl.BlockSpec((SUBLANES, w), lambda i: (jnp.maximum((nT - 1 - i) * hb - 1, 0), 0))
    next8 = lambda w: pl.BlockSpec((SUBLANES, w), lambda i: (jnp.minimum((nT - i) * hb, last_blk), 0))
    const2 = lambda shape: pl.BlockSpec(shape, lambda i: (0, 0))
    dil_specs = [_dilated_spec(tm, GROUP_W, dil, lambda i: nT - 1 - i) for dil in ATTN_DILATIONS]
    dil_shapes = [_dilated_shape(S, GROUP_W, dil, F32) for dil in ATTN_DILATIONS]
    small = (SUBLANES, GROUP_W)
    sq = (GROUP_W, GROUP_W)
    ws_shape = (N_HEADS, GMLP_CHUNK, GMLP_CHUNK)
    return pl.pallas_call(
        body, name=name, grid=(nT,),
        in_specs=[rev(wcols), prev8(wcols), next8(wcols), rev(GROUP_W),
                  rev(4 * GROUP_W), next8(GROUP_W), rev(GROUP_W), prev8(GROUP_W), rev(GROUP_W)]
                 + _mix_specs(tm, S, "bwd"),
        out_specs=[rev(wcols), rev(GROUP_W)] + dil_specs + dil_specs
                  + [const2(small), const2(small), const2(small), const2(sq), const2(sq),
                     pl.BlockSpec(ws_shape, lambda i: (0, 0, 0)), const2((GMLP_CHUNK, LANES))],
        out_shape=[jax.ShapeDtypeStruct((S, wcols), MXU_DTYPE), jax.ShapeDtypeStruct((S, GROUP_W), MXU_DTYPE)]
                  + [_dilated_shape(S, GROUP_W, dil, MXU_DTYPE) for dil in ATTN_DILATIONS] + dil_shapes
                  + [jax.ShapeDtypeStruct(small, F32)] * 3 + [jax.ShapeDtypeStruct(sq, F32)] * 2
                  + [jax.ShapeDtypeStruct(ws_shape, F32), jax.ShapeDtypeStruct((GMLP_CHUNK, LANES), F32)],
        scratch_shapes=[pltpu.VMEM(small, F32), pltpu.VMEM(small, F32), pltpu.VMEM((GMLP_CHUNK, GROUP_W), F32),
                        _lane_scratch(tm, GROUP_W), _lane_scratch(tm, GROUP_W),
                        _lane_scratch(tm, GROUP_W), _lane_scratch(tm, GROUP_W), pltpu.VMEM((hb, GROUP_W), F32)],
        compiler_params=_params(("arbitrary",)),
    )(z, z, z, z_g, dy, dy, hs, hs, o, mp["wA"], mp["wR"], mp["vec"], mp["wa"], mp["wx"], mp["ws"], mp["bs"])


def _attn_bwd(qkv, do, lse, delta, dil, name):
    rows = qkv.shape[0]
    nb = rows // ATTN_BLOCK
    scale = 1.0 / math.sqrt(HEAD_DIM)
    B = ATTN_BLOCK
    per_step = ATTN_BLOCKS_PER_STEP
    n_steps = nb // per_step

    def body(qc_ref, qn_ref, kc_ref, kp_ref, vc_ref, vp_ref, doc_ref, don_ref, lc_ref, ln_ref, dc_ref, dn_ref,
             dq_ref, dk_ref, dv_ref, bias_ref, bias_next_ref):
        n = pl.program_id(1)

        @pl.when(n == 0)
        def _():
            bias_ref[...] = _attn_bias(dil, (B,), 2 * B)
            bias_next_ref[...] = _attn_bias(dil, (B,), B)

        masks = _head_masks((B, GROUP_W))

        def per_row(tile):
            return jnp.concatenate([jnp.max(jnp.where(masks[h], tile, _NEG), axis=-1, keepdims=True)
                                    for h in range(N_HEADS)], axis=0)

        def grads(q, dov, lse_tile, dl_tile, keys, vals, bias, dead):
            qs = _stack_heads(q, masks)
            dos = _stack_heads(dov.astype(MXU_DTYPE), masks)
            s = _mm_nt(qs, keys) * scale + bias
            if dead is not None:
                s = jnp.where(dead(s.shape), _NEG, s)
            p = jnp.exp(s - per_row(lse_tile))
            ds = (p * (_mm_nt(dos, vals) - per_row(dl_tile)) * scale).astype(MXU_DTYPE)
            return ds, _mm_tn(ds, qs), _mm_tn(p.astype(MXU_DTYPE), dos)

        for j in range(per_step):
            own = slice(j * B, (j + 1) * B)
            before = slice((j - 1) * B, j * B)
            keys = jnp.concatenate([kp_ref[...] if j == 0 else kc_ref[before], kc_ref[own]], axis=0)
            vals = jnp.concatenate([vp_ref[...] if j == 0 else vc_ref[before], vc_ref[own]], axis=0)
            dead = (lambda shape: (n == 0) & (lax.broadcasted_iota(jnp.int32, shape, 1) < B)) if j == 0 else None
            ds, dk2, dv2 = grads(qc_ref[own], doc_ref[own], lc_ref[own], dc_ref[own], keys, vals, bias_ref[...], dead)
            dq_ref[own] = _unstack_heads(jnp.dot(ds, keys, preferred_element_type=F32), masks).astype(dq_ref.dtype)
            if j > 0:
                dk_ref[before] = (dk_own + dk2[:B]).astype(dk_ref.dtype)
                dv_ref[before] = (dv_own + dv2[:B]).astype(dv_ref.dtype)
            dk_own, dv_own = dk2[B:], dv2[B:]
        last = slice((per_step - 1) * B, per_step * B)
        _, dk1, dv1 = grads(qn_ref[...], don_ref[...], ln_ref[...], dn_ref[...], kc_ref[last], vc_ref[last],
                            bias_next_ref[...], lambda shape: n == n_steps - 1)
        dk_ref[last] = (dk_own + dk1).astype(dk_ref.dtype)
        dv_ref[last] = (dv_own + dv1).astype(dv_ref.dtype)

    blk = (per_step * B, GROUP_W)
    one = (B, GROUP_W)
    nxt_idx = lambda n: jnp.minimum((n + 1) * per_step, nb - 1)
    prv_idx = lambda n: jnp.maximum(n * per_step - 1, 0)
    zcur = lambda c: pl.BlockSpec(blk, lambda r, n: (n, r * 3 + c))
    znext = lambda c: pl.BlockSpec(one, lambda r, n: (nxt_idx(n), r * 3 + c))
    zprev = lambda c: pl.BlockSpec(one, lambda r, n: (prv_idx(n), r * 3 + c))
    cur = pl.BlockSpec(blk, lambda r, n: (n, r))
    nxt = pl.BlockSpec(one, lambda r, n: (nxt_idx(n), r))
    return pl.pallas_call(
        body, name=name, grid=(dil, n_steps),
        in_specs=[zcur(0), znext(0), zcur(1), zprev(1), zcur(2), zprev(2), cur, nxt, cur, nxt, cur, nxt],
        out_specs=[cur, cur, cur],
        out_shape=[jax.ShapeDtypeStruct((rows, dil * GROUP_W), WIRE_DTYPE)] * 3,
        scratch_shapes=[pltpu.VMEM((N_HEADS * B, 2 * B), F32), pltpu.VMEM((N_HEADS * B, B), F32)],
        compiler_params=_params(("parallel", "arbitrary")),
    )(qkv, qkv, qkv, qkv, qkv, qkv, do, do, lse, lse, delta, delta)


def _inproj_bwd(x, g, dxn, dz_abc, dqkv, dz_g, w8, name):
    S, D = x.shape
    N = w8.shape[0] * w8.shape[2]
    tm = TM_MM
    n_abc = N_ABC * GROUP_W

    def body(x_ref, g_ref, dxn_ref, dabc_ref, q1, k1, v1, q2, k2, v2, q3, k3, v3, dg_ref, w8_ref,
             dx_ref, dz_ref, h_ref, dgn_ref, s4_ref, s16_ref, w_ref):
        i = pl.program_id(0)

        @pl.when(i == 0)
        def _():
            dgn_ref[...] = jnp.zeros_like(dgn_ref)
            _assemble_columns(w8_ref, w_ref)

        dz_ref[:, 0:n_abc] = dabc_ref[...].astype(MXU_DTYPE)
        for j, parts in enumerate(((q1, q2, q3), (k1, k2, k3), (v1, v2, v3))):
            c0 = n_abc + j * GROUP_W
            _interleave(parts[1], s4_ref, ATTN_DILATIONS[1])
            _interleave(parts[2], s16_ref, ATTN_DILATIONS[2])
            dz_ref[:, c0:c0 + GROUP_W] = (parts[0][...] + _get(s4_ref) + _get(s16_ref)).astype(MXU_DTYPE)
        dz_ref[:, n_abc + 3 * GROUP_W:] = dg_ref[...].astype(MXU_DTYPE)
        dh = _mm_nt(dz_ref[...], w_ref[...])
        xv = x_ref[...]
        r = lax.rsqrt(jnp.mean(xv * xv, axis=-1, keepdims=True) + NORM_EPS)
        xn = xv * r
        gv = g_ref[...]
        h_ref[...] = (xn * gv).astype(MXU_DTYPE)
        dgn_ref[...] += _colsum(dh * xn)
        dn = dh * gv
        dx_ref[...] = dxn_ref[...] + r * (dn - xn * jnp.mean(dn * xn, axis=-1, keepdims=True))

    row = lambda w: pl.BlockSpec((tm, w), lambda i: (i, 0))
    flat = [t for p in dqkv for t in p]
    dil_specs = [_dilated_spec(tm, GROUP_W, dil) for dil in ATTN_DILATIONS for _ in range(3)]
    return pl.pallas_call(
        body, name=name, grid=(S // tm,),
        in_specs=[row(D), pl.BlockSpec((1, D), lambda i: (0, 0)), row(D), row(n_abc)] + dil_specs
                 + [row(GROUP_W), pl.BlockSpec(w8.shape, lambda i: (0, 0, 0), pipeline_mode=pl.Buffered(1))],
        out_specs=[row(D), row(N), row(D), pl.BlockSpec((1, D), lambda i: (0, 0))],
        out_shape=[jax.ShapeDtypeStruct((S, D), F32), jax.ShapeDtypeStruct((S, N), MXU_DTYPE),
                   jax.ShapeDtypeStruct((S, D), MXU_DTYPE), jax.ShapeDtypeStruct((1, D), F32)],
        scratch_shapes=[_lane_scratch(tm, GROUP_W)] * 2 + [pltpu.VMEM((D, N), w8.dtype)],
        compiler_params=_params(("arbitrary",)),
    )(x, g, dxn, dz_abc, *flat, dz_g, w8)


def _inproj_wgrad(h, dz, name):
    S, D = h.shape
    N = dz.shape[1]
    tm = TM_MM
    nj = 2
    cw = N // nj
    per = N_DEV // nj
    n_loc = N // N_DEV

    def body(h_ref, dz_ref, dw_ref, acc_ref):
        i = pl.program_id(1)

        @pl.when(i == 0)
        def _():
            acc_ref[...] = jnp.zeros_like(acc_ref)

        acc_ref[...] += _mm_tn(h_ref[...], dz_ref[...])

        @pl.when(i == S // tm - 1)
        def _():
            for b in range(per):
                dw_ref[b] = acc_ref[:, b * n_loc:(b + 1) * n_loc].astype(dw_ref.dtype)

    return pl.pallas_call(
        body, name=name, grid=(nj, S // tm),
        in_specs=[pl.BlockSpec((tm, D), lambda j, i: (i, 0)), pl.BlockSpec((tm, cw), lambda j, i: (i, j))],
        out_specs=pl.BlockSpec((per, D, n_loc), lambda j, i: (j, 0, 0)),
        out_shape=jax.ShapeDtypeStruct((N_DEV, D, n_loc), WIRE_DTYPE),
        scratch_shapes=[pltpu.VMEM((D, cw), F32)],
        compiler_params=_params(("parallel", "arbitrary")),
    )(h, dz)


def _my_place():
    return lax.axis_index("x"), lax.axis_index("y"), lax.axis_index("c")


def _peer(x, y, c, k):
    px = 1 - x if k & 4 else x
    py = 1 - y if k & 2 else y
    pc = 1 - c if k & 1 else c
    return (px, py, pc), 4 * px + 2 * py + pc


HBM_SPEC = pl.BlockSpec(memory_space=pltpu.HBM)
SEM_SPEC = pl.BlockSpec(memory_space=pltpu.SEMAPHORE)
SPLIT_EFFECT = pltpu.SideEffectType.DATAFLOW_SIDE_EFFECTING
N_PEERS = N_DEV - 1


def _exchange_copies(srcs, lands, send_sems, recv_sems, whole, arrival):
    x, y, c = _my_place()
    me = 4 * x + 2 * y + c
    copies = []
    for t in range(len(srcs)):
        for k in range(1, N_DEV):
            peer, pidx = _peer(x, y, c, k)
            copies.append(pltpu.make_async_remote_copy(
                src_ref=srcs[t] if whole[t] else srcs[t].at[pidx],
                dst_ref=lands[t].at[pidx if arrival else me], send_sem=send_sems.at[t * N_PEERS + k - 1],
                recv_sem=recv_sems.at[t * N_PEERS + k - 1], device_id=peer, device_id_type=MESH))
    return copies


def _exchange_start(groups, name, after=None):
    sizes = [len(g) for g in groups]
    whole = [w for g in groups for _, w in g]
    srcs = [pltpu.with_memory_space_constraint(a, pltpu.HBM) for g in groups for a, _ in g]
    lands = [pltpu.with_memory_space_constraint(lax.empty(((N_DEV,) + a.shape) if w else a.shape, a.dtype), pltpu.HBM)
             for a, w in zip(srcs, whole)]
    n = len(srcs)
    n_g = len(groups)
    extra = [] if after is None else [after]
    n_in = 2 * n + len(extra)

    def body(*refs):
        src_refs, land_refs = refs[:n], refs[n:2 * n]
        sem_refs = refs[n_in + 2 * n:n_in + 2 * n + 2 * n_g]
        token = refs[-1]
        off = 0
        for gi, sz in enumerate(sizes):
            for send in _exchange_copies(src_refs[off:off + sz], land_refs[off:off + sz],
                                         sem_refs[2 * gi], sem_refs[2 * gi + 1], whole[off:off + sz], False):
                send.start()
            off += sz
        token[...] = jnp.zeros_like(token)

    sem_shapes = [pltpu.SemaphoreType.DMA((sz * N_PEERS,)) for sz in sizes for _ in range(2)]
    outs = pl.pallas_call(
        body, name=name,
        in_specs=[HBM_SPEC] * (2 * n) + [pl.BlockSpec(memory_space=pl.ANY)] * len(extra),
        out_specs=[HBM_SPEC] * (2 * n) + [SEM_SPEC] * (2 * n_g) + [pl.BlockSpec(memory_space=pltpu.VMEM)],
        out_shape=[pltpu.HBM(a.shape, a.dtype) for a in srcs + lands] + sem_shapes
                  + [jax.ShapeDtypeStruct((SUBLANES, LANES), F32)],
        input_output_aliases={i: i for i in range(2 * n)},
        compiler_params=pltpu.CompilerParams(has_side_effects=SPLIT_EFFECT),
    )(*srcs, *lands, *extra)
    handles, off = [], 0
    for gi, sz in enumerate(sizes):
        handles.append((outs[2 * n + 2 * gi], outs[2 * n + 2 * gi + 1], outs[off:off + sz], outs[n + off:n + off + sz],
                        whole[off:off + sz]))
        off += sz
    return handles, outs[-1]


def _exchange_wait(handle, after, name):
    send_sems, recv_sems, srcs, lands, whole = handle
    n = len(srcs)

    def body(*refs):
        src_refs, land_refs = refs[:n], refs[n:2 * n]
        for send in _exchange_copies(src_refs, land_refs, refs[2 * n], refs[2 * n + 1], whole, False):
            send.wait_send()
        for arrival in _exchange_copies(src_refs, land_refs, refs[2 * n], refs[2 * n + 1], whole, True):
            arrival.wait_recv()

    outs = pl.pallas_call(
        body, name=name,
        in_specs=[HBM_SPEC] * (2 * n) + [SEM_SPEC, SEM_SPEC, pl.BlockSpec(memory_space=pl.ANY)],
        out_specs=[HBM_SPEC] * (2 * n),
        out_shape=[pltpu.HBM(a.shape, a.dtype) for a in list(srcs) + list(lands)],
        input_output_aliases={i: i for i in range(2 * n)},
        compiler_params=pltpu.CompilerParams(has_side_effects=SPLIT_EFFECT),
    )(*srcs, *lands, send_sems, recv_sems, after)
    x, y, c = _my_place()
    me = 4 * x + 2 * y + c
    own = [s[None] if w else lax.dynamic_slice_in_dim(s, me, 1, axis=0) for s, w in zip(outs[:n], whole)]
    return [lax.dynamic_update_slice_in_dim(ld, o, me, axis=0) for ld, o in zip(outs[n:], own)]


def _sum_slots(parts, name):
    n = len(parts)

    def body(*refs):
        for p_ref, o_ref in zip(refs[:n], refs[n:]):
            acc = p_ref[0]
            for j in range(1, N_DEV):
                acc = acc + p_ref[j]
            o_ref[...] = acc

    vm = pl.BlockSpec(memory_space=pltpu.VMEM)
    return pl.pallas_call(
        body, name=name, in_specs=[vm] * n, out_specs=[vm] * n,
        out_shape=[jax.ShapeDtypeStruct(p.shape[1:], F32) for p in parts],
        compiler_params=pltpu.CompilerParams(vmem_limit_bytes=VMEM_LIMIT),
    )(*parts)


def _adamw_math(w, g, m, v):
    m = ADAM_B1 * m + (1.0 - ADAM_B1) * g
    v = ADAM_B2 * v + (1.0 - ADAM_B2) * (g * g)
    m_hat = m / (1.0 - ADAM_B1 ** ADAM_STEP)
    v_hat = v / (1.0 - ADAM_B2 ** ADAM_STEP)
    delta = -ADAM_LR * (m_hat / (jnp.sqrt(v_hat) + ADAM_EPS) + ADAM_WD * w)
    return delta, m, v


def _adamw_summed(parts, w, m, v, tr, name):
    depth, R, C = w.shape

    def body(*refs):
        p_refs = refs[:depth]
        w_ref, m_ref, v_ref, g_ref, d_ref, nm_ref, nv_ref = refs[depth:]
        lay = pl.program_id(0)
        for l in range(depth):
            @pl.when(lay == l)
            def _(p_ref=p_refs[l]):
                g = p_ref[0].astype(F32)
                for j in range(1, N_DEV):
                    g = g + p_ref[j].astype(F32)
                g_ref[0] = g
        d_ref[0], nm_ref[0], nv_ref[0] = _adamw_math(w_ref[0], g_ref[0], m_ref[0], v_ref[0])

    part_spec = lambda l: pl.BlockSpec((N_DEV, tr, C), lambda lay, i: (0, jnp.where(lay == l, i, 0), 0))
    row = pl.BlockSpec((1, tr, C), lambda lay, i: (lay, i, 0))
    return pl.pallas_call(
        body, name=name, grid=(depth, R // tr),
        in_specs=[part_spec(l) for l in range(depth)] + [row, row, row],
        out_specs=[row] * 4, out_shape=[jax.ShapeDtypeStruct((depth, R, C), F32)] * 4,
        compiler_params=_params(("arbitrary", "arbitrary")),
    )(*parts, w, m, v)


def _adamw_small(w, g, m, v, name):
    def body(w_ref, g_ref, m_ref, v_ref, d_ref, nm_ref, nv_ref):
        d_ref[...], nm_ref[...], nv_ref[...] = _adamw_math(w_ref[...], g_ref[...], m_ref[...], v_ref[...])

    vm = pl.BlockSpec(memory_space=pltpu.VMEM)
    return pl.pallas_call(
        body, name=name, in_specs=[vm] * 4, out_specs=[vm] * 3,
        out_shape=[jax.ShapeDtypeStruct(w.shape, F32)] * 3,
        compiler_params=pltpu.CompilerParams(vmem_limit_bytes=VMEM_LIMIT),
    )(w, g, m, v)


def _pack(arrays):
    flat = jnp.concatenate([a.reshape(-1) for a in arrays])
    pad = (-flat.shape[0]) % (SUBLANES * LANES)
    return jnp.pad(flat, (0, pad)).reshape(-1, LANES)


def _unpack(buf, like):
    flat = buf.reshape(-1)
    out, off = [], 0
    for a in like:
        out.append(flat[off:off + a.size].reshape(a.shape))
        off += a.size
    return out


def _block_diag(w):
    eye = jnp.eye(N_HEADS, dtype=w.dtype)
    return jnp.einsum('hij,hk->hikj', w, eye).reshape(GROUP_W, GROUP_W)


def _diag_blocks(w):
    return jnp.einsum('hihj->hij', w.reshape(N_HEADS, HEAD_DIM, N_HEADS, HEAD_DIM))


def _pad_rows(a):
    return jnp.pad(a, ((0, SUBLANES - a.shape[0]), (0, 0)))


def _mixer_params(l, conv_a_w, conv_r_w, conv_r_b, lru_wa, lru_ba, lru_wx, lru_bx, lru_lambda, gmlp_norm_g,
                  gmlp_ws, gmlp_bs):
    tril = jnp.tril(jnp.ones((GMLP_CHUNK, GMLP_CHUNK), dtype=bool))
    vec = jnp.stack([conv_r_b[l], lru_ba[l], lru_bx[l], lru_lambda[l], gmlp_norm_g[l]])
    return {
        "wA": _pad_rows(conv_a_w[l]), "wR": _pad_rows(conv_r_w[l]), "vec": _pad_rows(vec),
        "wa": _block_diag(lru_wa[l]).astype(MXU_DTYPE), "wx": _block_diag(lru_wx[l]).astype(MXU_DTYPE),
        "ws": jnp.where(tril[None], gmlp_ws[l], 0.0).astype(MXU_DTYPE),
        "bs": jnp.repeat(jnp.transpose(gmlp_bs[l]), HEAD_DIM, axis=1),
    }


MIXER_NAMES = ("conv_a_w", "conv_r_w", "conv_r_b", "lru_wa", "lru_ba", "lru_wx", "lru_bx", "lru_lambda",
               "gmlp_norm_g", "gmlp_ws", "gmlp_bs")
SMALL_NAMES = ("norm_g",) + MIXER_NAMES + ("final_g",)


def _local_step(x, loss_target, norm_g, get_w_in, get_w_out, emit_early, emit_late, conv_a_w, conv_r_w, conv_r_b,
                lru_wa, lru_ba, lru_wx, lru_bx, lru_lambda, gmlp_norm_g, gmlp_ws, gmlp_bs, final_g):
    depth = norm_g.shape[0]
    D = x.shape[1]
    small = (conv_a_w, conv_r_w, conv_r_b, lru_wa, lru_ba, lru_wx, lru_bx, lru_lambda, gmlp_norm_g, gmlp_ws, gmlp_bs)
    saved = []
    for l in range(depth):
        mp = _mixer_params(l, *small)
        w_in_l = get_w_in(l, x)
        z, z_g, *qkv = _norm_inproj(x, norm_g[l].reshape(1, D), w_in_l, f"norm_inproj_{l}")
        y_abc, hs = _mix_fwd(z, mp, f"mix_fwd_{l}")
        attn = [_attn_fwd(qkv[p], dil, f"attn_fwd_d{dil}_{l}") for p, dil in enumerate(ATTN_DILATIONS)]
        w_out_l = get_w_out(l, y_abc)
        x_new, y, o, *lse = _outproj(x, z_g, y_abc, attn, w_out_l, f"outproj_{l}")
        saved.append((x, z, z_g, qkv, hs, y, o, lse, mp, w_in_l, w_out_l))
        x = x_new
    dx, loss, d_final_g = _loss_head(x, final_g.reshape(1, D), loss_target, "loss_head")
    token = None
    for l in reversed(range(depth)):
        x_l, z, z_g, qkv, hs, y, o, lse, mp, w_in_l, w_out_l = saved[l]
        if token is not None:
            mp = dict(mp, vec=mp["vec"] + token[0, 0])
        dy, dw_out = _outproj_bwd(dx, y, w_out_l, f"outproj_bwd_{l}")
        (dz_abc, dz_g, do1, do4, do16, dl1, dl4, dl16, dwA, dwR, dvec, dwa, dwx, dws, dbs) = _mix_bwd(
            z, z_g, dy, hs, o, mp, f"mix_bwd_{l}")
        token = emit_early(l, dw_out, [
            dwA[:conv_a_w.shape[1]], dwR[:conv_r_w.shape[1]], dvec[0], _diag_blocks(dwa), dvec[1], _diag_blocks(dwx),
            dvec[2], dvec[3], dvec[4], dws, jnp.transpose(dbs[:, :N_HEADS])])
        g_row = norm_g[l].reshape(1, D)
        if token is not None:
            g_row = g_row + token[0, 0]
        dqkv = [_attn_bwd(qkv[p], do, lse[p], dl, dil, f"attn_bwd_d{dil}_{l}")
                for p, (dil, do, dl) in enumerate(zip(ATTN_DILATIONS, (do1, do4, do16), (dl1, dl4, dl16)))]
        dx, dz, h, dng = _inproj_bwd(x_l, g_row, dx, dz_abc, dqkv, dz_g, w_in_l, f"inproj_bwd_{l}")
        dw_in = _inproj_wgrad(h, dz, f"inproj_wgrad_{l}")
        token = emit_late(l, dw_in, [dng[0]] + ([d_final_g[0]] if l == depth - 1 else []))
    return loss[0, 0], dx
WEIGHT_NAMES = ("norm_g", "w_in", "conv_a_w", "conv_r_w", "conv_r_b", "lru_wa", "lru_ba", "lru_wx", "lru_bx",
                "lru_lambda", "gmlp_norm_g", "gmlp_ws", "gmlp_bs", "w_out", "final_g")


def kernel(x, norm_g, w_in, conv_a_w, conv_r_w, conv_r_b, lru_wa, lru_ba, lru_wx, lru_bx, lru_lambda, gmlp_norm_g, gmlp_ws, gmlp_bs, w_out, final_g, loss_target, m_norm_g, m_w_in, m_conv_a_w, m_conv_r_w, m_conv_r_b, m_lru_wa, m_lru_ba, m_lru_wx, m_lru_bx, m_lru_lambda, m_gmlp_norm_g, m_gmlp_ws, m_gmlp_bs, m_w_out, m_final_g, v_norm_g, v_w_in, v_conv_a_w, v_conv_r_w, v_conv_r_b, v_lru_wa, v_lru_ba, v_lru_wx, v_lru_bx, v_lru_lambda, v_gmlp_norm_g, v_gmlp_ws, v_gmlp_bs, v_w_out, v_final_g):
    w = dict(norm_g=norm_g, w_in=w_in, conv_a_w=conv_a_w, conv_r_w=conv_r_w, conv_r_b=conv_r_b, lru_wa=lru_wa,
             lru_ba=lru_ba, lru_wx=lru_wx, lru_bx=lru_bx, lru_lambda=lru_lambda, gmlp_norm_g=gmlp_norm_g,
             gmlp_ws=gmlp_ws, gmlp_bs=gmlp_bs, w_out=w_out, final_g=final_g)
    m = dict(norm_g=m_norm_g, w_in=m_w_in, conv_a_w=m_conv_a_w, conv_r_w=m_conv_r_w, conv_r_b=m_conv_r_b,
             lru_wa=m_lru_wa, lru_ba=m_lru_ba, lru_wx=m_lru_wx, lru_bx=m_lru_bx, lru_lambda=m_lru_lambda,
             gmlp_norm_g=m_gmlp_norm_g, gmlp_ws=m_gmlp_ws, gmlp_bs=m_gmlp_bs, w_out=m_w_out, final_g=m_final_g)
    v = dict(norm_g=v_norm_g, w_in=v_w_in, conv_a_w=v_conv_a_w, conv_r_w=v_conv_r_w, conv_r_b=v_conv_r_b,
             lru_wa=v_lru_wa, lru_ba=v_lru_ba, lru_wx=v_lru_wx, lru_bx=v_lru_bx, lru_lambda=v_lru_lambda,
             gmlp_norm_g=v_gmlp_norm_g, gmlp_ws=v_gmlp_ws, gmlp_bs=v_gmlp_bs, w_out=v_w_out, final_g=v_final_g)
    depth, D, n_loc = w_in.shape
    e_loc = w_out.shape[1]
    cx, cy, cc = _my_place()
    me = 4 * cx + 2 * cy + cc

    w_in_w, w_out_w = w_in.astype(MXU_DTYPE), w_out.astype(MXU_DTYPE)
    c_loc = conv_a_w.shape[2]
    taps = (conv_a_w, conv_r_w)
    first, _ = _exchange_start([[(w_in_w[0], True), (_pack(taps), True)]], "gather_start_first")
    full_out = lambda g: g.reshape(N_DEV * e_loc, D)

    g_in0, g_taps = _exchange_wait(first[0], x, "gather_wait_in_0")
    groups = [[(w_out_w[0], True)]] + [[(w_in_w[l], True), (w_out_w[l], True)] for l in range(1, depth)]
    gathers, _ = _exchange_start(groups, "gather_start_rest", after=g_taps)
    g_taps = g_taps.reshape(N_DEV, -1)
    conv_full, off = [], 0
    for a in taps:
        part = g_taps[:, off:off + a.size].reshape((N_DEV,) + a.shape)
        conv_full.append(jnp.transpose(part, (1, 2, 0, 3)).reshape(a.shape[:2] + (N_DEV * c_loc,)))
        off += a.size
    conv_a_full, conv_r_full = conv_full
    later = {}

    def get_w_in(l, after):
        if l == 0:
            return g_in0
        g_in, later[l] = _exchange_wait(gathers[l], after, f"gather_wait_{l}")
        return g_in

    def get_w_out(l, after):
        if l == 0:
            return full_out(_exchange_wait(gathers[0], after, "gather_wait_out_0")[0])
        return full_out(later[l])

    early, late, last_token = {}, {}, [None]

    def emit_early(l, dw_out, mixer_grads):
        handles, token = _exchange_start(
            [[(dw_out.reshape(N_DEV, e_loc, D), False), (_pack(mixer_grads), True)]], f"early_start_{l}")
        early[l] = (handles[0], mixer_grads)
        return token

    def emit_late(l, dw_in, norm_grads):
        handles, token = _exchange_start([[(_pack(norm_grads), True)], [(dw_in, False)]], f"late_start_{l}")
        late[l] = (handles[0], handles[1], norm_grads)
        last_token[0] = token
        return token

    loss, grad_x = _local_step(
        x[0], loss_target[0], norm_g, get_w_in, get_w_out, emit_early, emit_late, conv_a_full, conv_r_full, conv_r_b,
        lru_wa, lru_ba, lru_wx, lru_bx, lru_lambda, gmlp_norm_g, gmlp_ws, gmlp_bs, final_g)
    loss = lax.psum(loss, ("x", "y", "c"))

    r_in, r_out, small_parts = {}, {}, []
    for l in reversed(range(depth)):
        r_out[l], r_mix = _exchange_wait(early[l][0], last_token[0], f"early_wait_{l}")
        (r_norm,) = _exchange_wait(late[l][0], last_token[0], f"late_wait_norm_{l}")
        small_parts += [r_mix, r_norm]
        if l > 0:
            (r_in[l],) = _exchange_wait(late[l][1], last_token[0], f"late_wait_{l}")
    big = {"w_out": _adamw_summed([r_out[l] for l in range(depth)], w_out, m_w_out, v_w_out, 128, "adamw_w_out")}

    sums = _sum_slots(small_parts, "sum_small_grads")
    by_layer = {}
    for i, l in enumerate(reversed(range(depth))):
        mix = _unpack(sums[2 * i], early[l][1])
        nrm = _unpack(sums[2 * i + 1], late[l][2])
        by_layer[l] = dict(zip(MIXER_NAMES, mix), norm_g=nrm[0])
        if l == depth - 1:
            g_final = nrm[1]
    g_small = {k: jnp.stack([by_layer[l][k] for l in range(depth)]) for k in ("norm_g",) + MIXER_NAMES}
    g_small["final_g"] = g_final
    for k in ("conv_a_w", "conv_r_w"):
        g_small[k] = lax.dynamic_slice_in_dim(g_small[k], me * c_loc, c_loc, axis=2)
    packs = [_pack([d[k] for k in SMALL_NAMES]) for d in (w, g_small, m, v)]
    res = _adamw_small(*packs, "adamw_small")
    like = [w[k] for k in SMALL_NAMES]
    d_s, m_s, v_s = (dict(zip(SMALL_NAMES, _unpack(r, like))) for r in res)

    (r_in[0],) = _exchange_wait(late[0][1], res[0], "late_wait_0")
    big["w_in"] = _adamw_summed([r_in[l] for l in range(depth)], w_in, m_w_in, v_w_in, 512, "adamw_w_in")

    grad, delta, new_m, new_v = {}, {}, {}, {}
    for k in WEIGHT_NAMES:
        if k in big:
            grad[k], delta[k], new_m[k], new_v[k] = big[k]
        else:
            grad[k], delta[k], new_m[k], new_v[k] = g_small[k], d_s[k], m_s[k], v_s[k]
    return (loss, grad_x[None], *[grad[k] for k in WEIGHT_NAMES], *[delta[k] for k in WEIGHT_NAMES],
            *[new_m[k] for k in WEIGHT_NAMES], *[new_v[k] for k in WEIGHT_NAMES])
```

```python
import functools
import math

import jax
import jax.numpy as jnp
from jax import lax
from jax.experimental import pallas as pl
from jax.experimental.pallas import tpu as pltpu

F32 = jnp.float32
MXU_DTYPE = jnp.bfloat16
WIRE_DTYPE = jnp.bfloat16
MESH = pl.DeviceIdType.MESH

N_DEV = 8
GROUP_W = 256
N_HEADS = 4
HEAD_DIM = 64
N_CHUNKS = 13
N_ABC = 9
GMLP_CHUNK = 128
ATTN_BLOCK = 128
ATTN_BLOCKS_PER_STEP = 4
ATTN_DILATIONS = (1, 4, 16)
NORM_EPS = 1e-6
RG_C = 8.0
SUBLANES = 8
LANES = 128
VMEM_LIMIT = 56 * 1024 * 1024

ADAM_LR = 0.001
ADAM_B1 = 0.9
ADAM_B2 = 0.999
ADAM_EPS = 1e-08
ADAM_WD = 0.01
ADAM_STEP = 10

TM_MIX = 512
TM_MM = 512
TM_WGRAD = 1024


def _params(sem, vmem=VMEM_LIMIT):
    return pltpu.CompilerParams(dimension_semantics=sem, vmem_limit_bytes=vmem)


def _mm(a, b):
    return jnp.dot(a.astype(MXU_DTYPE), b.astype(MXU_DTYPE), preferred_element_type=F32)


def _mm_tn(a, b):
    return lax.dot_general(a.astype(MXU_DTYPE), b.astype(MXU_DTYPE), (((0,), (0,)), ((), ())),
                           preferred_element_type=F32)


def _mm_nt(a, b):
    return lax.dot_general(a.astype(MXU_DTYPE), b.astype(MXU_DTYPE), (((1,), (1,)), ((), ())),
                           preferred_element_type=F32)


def _sigmoid(x):
    return 0.5 * jnp.tanh(0.5 * x) + 0.5


def _silu_and_grad(x):
    s = _sigmoid(x)
    return x * s, s * (1.0 + x * (1.0 - s))


_GELU_K = math.sqrt(2.0 / math.pi)
_GELU_C = 0.044715


def _gelu_and_grad(x):
    x2 = x * x
    t = jnp.tanh(_GELU_K * (x + _GELU_C * x * x2))
    val = 0.5 * x * (1.0 + t)
    grad = 0.5 * (1.0 + t) + 0.5 * x * (1.0 - t * t) * (_GELU_K * (1.0 + 3.0 * _GELU_C * x2))
    return val, grad


def _gelu(x):
    return 0.5 * x * (1.0 + jnp.tanh(_GELU_K * (x + _GELU_C * x * x * x)))


def _expm1_nonpos(u):
    poly = 1.0 / math.factorial(9)
    for k in range(8, 0, -1):
        poly = poly * u + 1.0 / math.factorial(k)
    return jnp.where(u > -0.25, poly * u, jnp.exp(u) - 1.0)


def _softplus(x):
    return jnp.maximum(x, 0.0) + jnp.log(1.0 + jnp.exp(-jnp.abs(x)))


def _shift_down(t, halo, k):
    rolled = pltpu.roll(t, k, 0)
    hr = pltpu.roll(halo, k, 0)
    row = lax.broadcasted_iota(jnp.int32, halo.shape, 0)
    first = jnp.where(row < k, hr, rolled[0:SUBLANES])
    return jnp.concatenate([first, rolled[SUBLANES:]], axis=0)


def _shift_up(t, nxt, k):
    tm = t.shape[0]
    rolled = pltpu.roll(t, tm - k, 0)
    nr = pltpu.roll(nxt, SUBLANES - k, 0)
    row = lax.broadcasted_iota(jnp.int32, nxt.shape, 0)
    last = jnp.where(row >= SUBLANES - k, nr, rolled[tm - SUBLANES:tm])
    return jnp.concatenate([rolled[:tm - SUBLANES], last], axis=0)


def _scan_fwd(a, b):
    tm = a.shape[0]
    row = lax.broadcasted_iota(jnp.int32, a.shape, 0)
    s = 1
    while s < tm:
        a_s = pltpu.roll(a, s, 0)
        b_s = pltpu.roll(b, s, 0)
        m = row >= s
        b = jnp.where(m, a * b_s + b, b)
        a = jnp.where(m, a * a_s, a)
        s *= 2
    return a, b


def _scan_rev(a, g):
    tm = a.shape[0]
    row = lax.broadcasted_iota(jnp.int32, a.shape, 0)
    s = 1
    while s < tm:
        a_s = pltpu.roll(a, tm - s, 0)
        g_s = pltpu.roll(g, tm - s, 0)
        m = row < tm - s
        g = jnp.where(m, g + a * g_s, g)
        a = jnp.where(m, a * a_s, a)
        s *= 2
    return g


def _group_rows(scr_ref, row, n_groups):
    return jnp.concatenate([scr_ref[pl.ds(c, 1), pl.ds(row, n_groups, stride=SUBLANES), :][0]
                            for c in range(scr_ref.shape[0])], axis=1)


def _spread_rows(rows_ref, n_groups, w):
    return jnp.concatenate([jnp.broadcast_to(rows_ref[g:g + 1, :], (SUBLANES, w)) for g in range(n_groups)], axis=0)


def _scan_groups(a, b, reverse):
    tm, w = a.shape
    shape3 = (tm // SUBLANES, SUBLANES, w)
    a3, b3 = a.reshape(shape3), b.reshape(shape3)
    sub = lax.broadcasted_iota(jnp.int32, shape3, 1)
    s = 1
    while s < SUBLANES:
        shift = SUBLANES - s if reverse else s
        a_s = pltpu.roll(a3, shift, 1)
        b_s = pltpu.roll(b3, shift, 1)
        m = (sub < SUBLANES - s) if reverse else (sub >= s)
        b3 = jnp.where(m, a3 * b_s + b3, b3)
        a3 = jnp.where(m, a3 * a_s, a3)
        s *= 2
    return a3.reshape(tm, w), b3.reshape(tm, w)


def _scan_fwd_tile(a, b, h_in, sa_ref, sb_ref, sc_ref):
    tm, w = a.shape
    n_groups = tm // SUBLANES
    a_loc, b_loc = _scan_groups(a, b, False)
    _put(sa_ref, a_loc)
    _put(sb_ref, b_loc)
    a_end, b_end = _scan_fwd(_group_rows(sa_ref, SUBLANES - 1, n_groups), _group_rows(sb_ref, SUBLANES - 1, n_groups))
    h_end = b_end + a_end * h_in
    sc_ref[...] = _shift_down(h_end, jnp.broadcast_to(h_in, (SUBLANES, w)), 1)
    return b_loc + a_loc * _spread_rows(sc_ref, n_groups, w), h_end


def _scan_rev_tile(a, g, sa_ref, sb_ref, sc_ref):
    tm, w = a.shape
    n_groups = tm // SUBLANES
    a_loc, g_loc = _scan_groups(a, g, True)
    _put(sa_ref, a_loc)
    _put(sb_ref, g_loc)
    d_first = _scan_rev(_group_rows(sa_ref, 0, n_groups), _group_rows(sb_ref, 0, n_groups))
    sc_ref[...] = _shift_up(d_first, jnp.zeros((SUBLANES, w), F32), 1)
    return g_loc + a_loc * _spread_rows(sc_ref, n_groups, w)


def _lane_scratch(tm, w):
    return pltpu.VMEM((w // LANES, tm, LANES), F32)


def _put(scr_ref, val):
    for c in range(scr_ref.shape[0]):
        scr_ref[c] = val[:, c * LANES:(c + 1) * LANES].astype(F32)


def _get(scr_ref):
    return jnp.concatenate([scr_ref[c] for c in range(scr_ref.shape[0])], axis=1)


def _deinterleave(src_ref, dst_ref, dil):
    nc, tm, _ = src_ref.shape
    w = nc * LANES
    for r in range(dil):
        for c in range(nc):
            piece = src_ref[pl.ds(c, 1), pl.ds(r, tm // dil, stride=dil), :][0] if dil > 1 else src_ref[c]
            dst_ref[:, r * w + c * LANES:r * w + (c + 1) * LANES] = piece.astype(dst_ref.dtype)


def _interleave(src_ref, dst_ref, dil):
    nc, tm, _ = dst_ref.shape
    w = nc * LANES
    for r in range(dil):
        for c in range(nc):
            dst_ref[pl.ds(c, 1), pl.ds(r, tm // dil, stride=dil), :] = (
                src_ref[:, r * w + c * LANES:r * w + (c + 1) * LANES].astype(F32)[None])


def _dilated_spec(tm, w, dil, index=lambda i: i):
    return pl.BlockSpec((tm // dil, dil * w), lambda i: (index(i), 0))


def _dilated_shape(S, w, dil, dtype):
    return jax.ShapeDtypeStruct((S // dil, dil * w), dtype)


def _head_masks(shape):
    lane = lax.broadcasted_iota(jnp.int32, shape, 1)
    return [(lane >= h * HEAD_DIM) & (lane < (h + 1) * HEAD_DIM) for h in range(N_HEADS)]


def _colsum(v):
    return jnp.sum(v, axis=0, keepdims=True)


def _assemble_columns(blocks_ref, full_ref):
    c = blocks_ref.shape[2]
    for j in range(blocks_ref.shape[0]):
        full_ref[:, j * c:(j + 1) * c] = blocks_ref[j]


def _conv_a(z_of, halo_of, w_ref):
    p = z_of(2) * z_of(0)
    p_h = halo_of(2) * halo_of(0)
    cv = w_ref[2:3, :] * p + w_ref[1:2, :] * _shift_down(p, p_h, 1) + w_ref[0:1, :] * _shift_down(p, p_h, 2)
    return p, p_h, cv


def _lru_gates(z_of, halo_of, wr_ref, vec_ref, wa_ref, wx_ref):
    rx = z_of(4)
    rx_h = halo_of(4)
    sh = [rx, _shift_down(rx, rx_h, 1), _shift_down(rx, rx_h, 2), _shift_down(rx, rx_h, 3)]
    xc = (wr_ref[3:4, :] * sh[0] + wr_ref[2:3, :] * sh[1] + wr_ref[1:2, :] * sh[2]
          + wr_ref[0:1, :] * sh[3] + vec_ref[0:1, :])
    ga = _sigmoid(jnp.dot(xc.astype(MXU_DTYPE), wa_ref[...], preferred_element_type=F32) + vec_ref[1:2, :])
    gi = _sigmoid(jnp.dot(xc.astype(MXU_DTYPE), wx_ref[...], preferred_element_type=F32) + vec_ref[2:3, :])
    sp = _softplus(-vec_ref[3:4, :])
    log_a = (-RG_C * ga) * sp
    a = jnp.exp(log_a)
    mult = jnp.sqrt(-_expm1_nonpos(2.0 * log_a))
    return xc, sh, ga, gi, a, mult, sp


def _gmlp_fwd(z_of, vec_ref, ws_ref, bs_ref, tm):
    u = _gelu(z_of(6))
    gv = _gelu(z_of(7))
    rr = lax.rsqrt(jnp.mean(gv * gv, axis=-1, keepdims=True) + NORM_EPS)
    vn = (gv * rr) * vec_ref[4:5, :]
    masks = _head_masks((GMLP_CHUNK, GROUP_W))
    parts = []
    for c in range(tm // GMLP_CHUNK):
        vc = vn[c * GMLP_CHUNK:(c + 1) * GMLP_CHUNK].astype(MXU_DTYPE)
        acc = bs_ref[...]
        for h in range(N_HEADS):
            acc = acc + jnp.where(masks[h], jnp.dot(ws_ref[h], vc, preferred_element_type=F32), 0.0)
        parts.append(acc)
    return u, gv, rr, vn, jnp.concatenate(parts, axis=0)


def _mix_specs(tm, S, order):
    const2 = lambda shape: pl.BlockSpec(shape, lambda i: (0, 0))
    return [const2((SUBLANES, GROUP_W)), const2((SUBLANES, GROUP_W)), const2((SUBLANES, GROUP_W)),
            const2((GROUP_W, GROUP_W)), const2((GROUP_W, GROUP_W)),
            pl.BlockSpec((N_HEADS, GMLP_CHUNK, GMLP_CHUNK), lambda i: (0, 0, 0)),
            const2((GMLP_CHUNK, GROUP_W))]


def _inproj_mix_fwd(x, g, w8, mp, name):
    S, D = x.shape
    N = w8.shape[0] * w8.shape[2]
    tm = TM_MIX
    hb = tm // SUBLANES
    n_abc = N_ABC * GROUP_W
    n_qkv = 3 * GROUP_W

    def body(x_ref, g_ref, w8_ref, wA_ref, wR_ref, vec_ref, wa_ref, wx_ref, ws_ref, bs_ref,
             z_ref, zg_ref, q1_ref, q4_ref, q16_ref, y_ref, h_ref,
             qkv_ref, w_ref, halo_ref, carry_ref, sa_ref, sb_ref, sc_ref):
        @pl.when(pl.program_id(0) == 0)
        def _():
            _assemble_columns(w8_ref, w_ref)
            halo_ref[...] = jnp.zeros_like(halo_ref)
            carry_ref[...] = jnp.zeros_like(carry_ref)

        xv = x_ref[...]
        r = lax.rsqrt(jnp.mean(xv * xv, axis=-1, keepdims=True) + NORM_EPS)
        hn = ((xv * r) * g_ref[...]).astype(MXU_DTYPE)
        z_ref[...] = jnp.dot(hn, w_ref[:, 0:n_abc], preferred_element_type=F32)
        _put(qkv_ref, jnp.dot(hn, w_ref[:, n_abc:n_abc + n_qkv], preferred_element_type=F32))
        zg_ref[...] = jnp.dot(hn, w_ref[:, n_abc + n_qkv:], preferred_element_type=F32)
        for dil, ref in zip(ATTN_DILATIONS, (q1_ref, q4_ref, q16_ref)):
            _deinterleave(qkv_ref, ref, dil)

        z_of = lambda c: z_ref[:, c * GROUP_W:(c + 1) * GROUP_W]
        halo_of = lambda c: halo_ref[:, c * GROUP_W:(c + 1) * GROUP_W]

        _, _, cv = _conv_a(z_of, halo_of, wA_ref)
        y_ref[:, 0:GROUP_W] = (z_of(1) * cv * _silu_and_grad(z_of(3))[0]).astype(y_ref.dtype)

        xc, _, _, gi, a, mult, _ = _lru_gates(z_of, halo_of, wR_ref, vec_ref, wa_ref, wx_ref)
        b = mult * (gi * xc)
        h, h_end = _scan_fwd_tile(a, b, carry_ref[SUBLANES - 1:SUBLANES, :], sa_ref, sb_ref, sc_ref)
        h_ref[...] = h
        carry_ref[...] = h_end[hb - SUBLANES:hb]
        y_ref[:, GROUP_W:2 * GROUP_W] = (h * _silu_and_grad(z_of(5))[0]).astype(y_ref.dtype)

        u, _, _, _, sp = _gmlp_fwd(z_of, vec_ref, ws_ref, bs_ref, tm)
        y_ref[:, 2 * GROUP_W:3 * GROUP_W] = (u * sp * _silu_and_grad(z_of(8))[0]).astype(y_ref.dtype)
        halo_ref[...] = z_ref[tm - SUBLANES:tm, :]

    row = lambda wd: pl.BlockSpec((tm, wd), lambda i: (i, 0))
    return pl.pallas_call(
        body, name=name, grid=(S // tm,),
        in_specs=[row(D), pl.BlockSpec((1, D), lambda i: (0, 0)),
                  pl.BlockSpec(w8.shape, lambda i: (0, 0, 0), pipeline_mode=pl.Buffered(1))] + _mix_specs(tm, S, "fwd"),
        out_specs=[row(n_abc), row(GROUP_W)] + [_dilated_spec(tm, n_qkv, dil) for dil in ATTN_DILATIONS]
                  + [row(3 * GROUP_W), row(GROUP_W)],
        out_shape=[jax.ShapeDtypeStruct((S, n_abc), F32), jax.ShapeDtypeStruct((S, GROUP_W), F32)]
                  + [_dilated_shape(S, n_qkv, dil, MXU_DTYPE) for dil in ATTN_DILATIONS]
                  + [jax.ShapeDtypeStruct((S, 3 * GROUP_W), MXU_DTYPE), jax.ShapeDtypeStruct((S, GROUP_W), F32)],
        scratch_shapes=[_lane_scratch(tm, n_qkv), pltpu.VMEM((D, N), w8.dtype), pltpu.VMEM((SUBLANES, n_abc), F32),
                        pltpu.VMEM((SUBLANES, GROUP_W), F32), _lane_scratch(tm, GROUP_W), _lane_scratch(tm, GROUP_W),
                        pltpu.VMEM((hb, GROUP_W), F32)],
        compiler_params=_params(("arbitrary",)),
    )(x, g, w8, mp["wA"], mp["wR"], mp["vec"], mp["wa"], mp["wx"], mp["ws"], mp["bs"])


_NEG = -1e30


def _slope(h):
    return 2.0 ** (-8.0 * (h + 1) / N_HEADS)


def _attn_bias(dil, offsets, n_keys):
    shape = (ATTN_BLOCK, n_keys)
    qi = lax.broadcasted_iota(jnp.int32, shape, 0)
    ki = lax.broadcasted_iota(jnp.int32, shape, 1)
    blocks = []
    for f in offsets:
        delta = qi + f - ki
        valid = (delta >= 0) & (delta <= ATTN_BLOCK)
        dist = (delta * dil).astype(F32)
        for h in range(N_HEADS):
            blocks.append(jnp.where(valid, -_slope(h) * dist, _NEG))
    return jnp.concatenate(blocks, axis=0)


def _stack_heads(t, masks):
    return jnp.concatenate([jnp.where(m, t, jnp.zeros_like(t)) for m in masks], axis=0)


def _unstack_heads(t4, masks, base=0):
    out = t4[base * ATTN_BLOCK:(base + 1) * ATTN_BLOCK]
    for h in range(1, N_HEADS):
        out = jnp.where(masks[h], t4[(base + h) * ATTN_BLOCK:(base + h + 1) * ATTN_BLOCK], out)
    return out


def _attn_fwd(qkv, dil, name):
    rows = qkv.shape[0]
    nb = rows // ATTN_BLOCK
    scale = 1.0 / math.sqrt(HEAD_DIM)
    B = ATTN_BLOCK
    per_step = ATTN_BLOCKS_PER_STEP

    def body(q_ref, kc_ref, kp_ref, vc_ref, vp_ref, o_ref, l_ref, bias_ref):
        n = pl.program_id(1)

        @pl.when(n == 0)
        def _():
            bias_ref[...] = _attn_bias(dil, (B,), 2 * B)

        masks = _head_masks((B, GROUP_W))
        for j in range(per_step):
            own = slice(j * B, (j + 1) * B)
            before = slice((j - 1) * B, j * B)
            qs = _stack_heads(q_ref[own], masks)
            keys = jnp.concatenate([kp_ref[...] if j == 0 else kc_ref[before], kc_ref[own]], axis=0)
            vals = jnp.concatenate([vp_ref[...] if j == 0 else vc_ref[before], vc_ref[own]], axis=0)
            s = _mm_nt(qs, keys) * scale + bias_ref[...]
            if j == 0:
                key_col = lax.broadcasted_iota(jnp.int32, s.shape, 1)
                s = jnp.where((n == 0) & (key_col < B), _NEG, s)
            m = jnp.max(s, axis=-1, keepdims=True)
            p = jnp.exp(s - m)
            l = jnp.sum(p, axis=-1, keepdims=True)
            o4 = jnp.dot(p.astype(MXU_DTYPE), vals, preferred_element_type=F32)
            o_ref[own] = _unstack_heads(o4, masks) / _unstack_heads(jnp.broadcast_to(l, o4.shape), masks)
            l_ref[own] = _unstack_heads(jnp.broadcast_to(m + jnp.log(l), o4.shape), masks)

    blk = (per_step * B, GROUP_W)
    cur = lambda c: pl.BlockSpec(blk, lambda r, n: (n, r * 3 + c))
    prev = lambda c: pl.BlockSpec((B, GROUP_W), lambda r, n: (jnp.maximum(n * per_step - 1, 0), r * 3 + c))
    out = pl.BlockSpec(blk, lambda r, n: (n, r))
    return pl.pallas_call(
        body, name=name, grid=(dil, nb // per_step),
        in_specs=[cur(0), cur(1), prev(1), cur(2), prev(2)],
        out_specs=[out, out],
        out_shape=[jax.ShapeDtypeStruct((rows, dil * GROUP_W), F32)] * 2,
        scratch_shapes=[pltpu.VMEM((N_HEADS * ATTN_BLOCK, 2 * ATTN_BLOCK), F32)],
        compiler_params=_params(("parallel", "arbitrary")),
    )(qkv, qkv, qkv, qkv, qkv)


def _outproj(x, z_g, y_abc, attn, w_out, name):
    S, D = x.shape
    tm = TM_MM
    n_abc = 3 * GROUP_W

    def body(x_ref, g_ref, yabc_ref, o1, l1, o2, l2, o3, l3, w_ref,
             xn_ref, y_ref, o_ref, lse1_ref, lse4_ref, lse16_ref, so2, sl2, so3, sl3, slse):
        for src, dst, dil in ((o2, so2, ATTN_DILATIONS[1]), (l2, sl2, ATTN_DILATIONS[1]),
                              (o3, so3, ATTN_DILATIONS[2]), (l3, sl3, ATTN_DILATIONS[2])):
            _interleave(src, dst, dil)
        la, lb, lc = l1[...], _get(sl2), _get(sl3)
        mx = jnp.maximum(jnp.maximum(la, lb), lc)
        ea, eb, ec = jnp.exp(la - mx), jnp.exp(lb - mx), jnp.exp(lc - mx)
        den = ea + eb + ec
        o = (ea * o1[...] + eb * _get(so2) + ec * _get(so3)) / den
        o_ref[...] = o
        _put(slse, mx + jnp.log(den))
        for dil, ref in zip(ATTN_DILATIONS, (lse1_ref, lse4_ref, lse16_ref)):
            _deinterleave(slse, ref, dil)
        y_d = o * _silu_and_grad(g_ref[...])[0]
        y_ref[:, 0:n_abc] = yabc_ref[...].astype(MXU_DTYPE)
        y_ref[:, n_abc:] = y_d.astype(MXU_DTYPE)
        xn_ref[...] = x_ref[...] + jnp.dot(y_ref[...], w_ref[...], preferred_element_type=F32)

    row = lambda w: pl.BlockSpec((tm, w), lambda i: (i, 0))
    dil_specs = [_dilated_spec(tm, GROUP_W, dil) for dil in ATTN_DILATIONS]
    (o1, l1), (o2, l2), (o3, l3) = attn
    return pl.pallas_call(
        body, name=name, grid=(S // tm,),
        in_specs=[row(D), row(GROUP_W), row(n_abc)] + [sp for sp in dil_specs for _ in range(2)]
                 + [pl.BlockSpec(w_out.shape, lambda i: (0, 0))],
        out_specs=[row(D), row(4 * GROUP_W), row(GROUP_W)] + dil_specs,
        out_shape=[jax.ShapeDtypeStruct((S, D), F32), jax.ShapeDtypeStruct((S, 4 * GROUP_W), MXU_DTYPE),
                   jax.ShapeDtypeStruct((S, GROUP_W), F32)]
                  + [_dilated_shape(S, GROUP_W, dil, F32) for dil in ATTN_DILATIONS],
        scratch_shapes=[_lane_scratch(tm, GROUP_W)] * 5,
        compiler_params=_params(("parallel",)),
    )(x, z_g, y_abc, o1, l1, o2, l2, o3, l3, w_out)


def _loss_head(x, g, target, name):
    S, D = x.shape
    tm = TM_MM

    def body(x_ref, g_ref, t_ref, dx_ref, loss_ref, dg_ref):
        i = pl.program_id(0)

        @pl.when(i == 0)
        def _():
            loss_ref[...] = jnp.zeros_like(loss_ref)
            dg_ref[...] = jnp.zeros_like(dg_ref)

        xv = x_ref[...]
        r = lax.rsqrt(jnp.mean(xv * xv, axis=-1, keepdims=True) + NORM_EPS)
        xn = xv * r
        err = xn * g_ref[...] - t_ref[...]
        per_tok = jnp.mean(err * err, axis=-1, keepdims=True)
        loss_ref[...] += 0.5 * jnp.sum(per_tok, axis=0, keepdims=True)
        dout = err * (1.0 / D)
        dg_ref[...] += _colsum(dout * xn)
        dxn = dout * g_ref[...]
        dx_ref[...] = r * (dxn - xn * jnp.mean(dxn * xn, axis=-1, keepdims=True))

    row = pl.BlockSpec((tm, D), lambda i: (i, 0))
    return pl.pallas_call(
        body, name=name, grid=(S // tm,),
        in_specs=[row, pl.BlockSpec((1, D), lambda i: (0, 0)), row],
        out_specs=[row, pl.BlockSpec((1, LANES), lambda i: (0, 0)), pl.BlockSpec((1, D), lambda i: (0, 0))],
        out_shape=[jax.ShapeDtypeStruct((S, D), F32), jax.ShapeDtypeStruct((1, LANES), F32),
                   jax.ShapeDtypeStruct((1, D), F32)],
        compiler_params=_params(("arbitrary",)),
    )(x, g, target)


def _outproj_bwd(dx, y, w_out, name):
    S, D = dx.shape
    E = y.shape[1]
    tm = TM_WGRAD

    def body(dx_ref, y_ref, w_ref, dy_ref, dw_ref, acc_ref):
        i = pl.program_id(0)

        @pl.when(i == 0)
        def _():
            acc_ref[...] = jnp.zeros_like(acc_ref)

        dxb = dx_ref[...].astype(MXU_DTYPE)
        dy_ref[...] = _mm_nt(dxb, w_ref[...])
        acc_ref[...] += _mm_tn(y_ref[...], dxb)

        @pl.when(i == S // tm - 1)
        def _():
            dw_ref[...] = acc_ref[...].astype(dw_ref.dtype)

    return pl.pallas_call(
        body, name=name, grid=(S // tm,),
        in_specs=[pl.BlockSpec((tm, D), lambda i: (i, 0)), pl.BlockSpec((tm, E), lambda i: (i, 0)),
                  pl.BlockSpec((E, D), lambda i: (0, 0))],
        out_specs=[pl.BlockSpec((tm, E), lambda i: (i, 0)), pl.BlockSpec((E, D), lambda i: (0, 0))],
        out_shape=[jax.ShapeDtypeStruct((S, E), F32), jax.ShapeDtypeStruct((E, D), WIRE_DTYPE)],
        scratch_shapes=[pltpu.VMEM((E, D), F32)],
        compiler_params=_params(("arbitrary",)),
    )(dx, y, w_out)


def _mix_bwd(z, z_g, dy, hs, o, mp, name):
    S = z.shape[0]
    tm = TM_MIX
    hb = tm // SUBLANES
    nT = S // tm
    last_blk = S // SUBLANES - 1
    wcols = N_ABC * GROUP_W

    def body(z_ref, zh_ref, zn_ref, zg_ref, dy_ref, dyn_ref, h_ref, hh_ref, o_ref,
             wA_ref, wR_ref, vec_ref, wa_ref, wx_ref, ws_ref, bs_ref,
             dz_ref, dzg_ref, do1_ref, do4_ref, do16_ref, dl1_ref, dl4_ref, dl16_ref,
             dwA_ref, dwR_ref, dvec_ref, dwa_ref, dwx_ref, dws_ref, dbs_ref,
             hcarry_ref, xcarry_ref, bsacc_ref, do_ref, dl_ref, sa_ref, sb_ref, sc_ref):
        i = pl.program_id(0)
        ti = nT - 1 - i

        @pl.when(i == 0)
        def _():
            hcarry_ref[...] = jnp.zeros_like(hcarry_ref)
            xcarry_ref[...] = jnp.zeros_like(xcarry_ref)
            bsacc_ref[...] = jnp.zeros_like(bsacc_ref)
            dwA_ref[...] = jnp.zeros_like(dwA_ref)
            dwR_ref[...] = jnp.zeros_like(dwR_ref)
            dvec_ref[...] = jnp.zeros_like(dvec_ref)
            dwa_ref[...] = jnp.zeros_like(dwa_ref)
            dwx_ref[...] = jnp.zeros_like(dwx_ref)
            dws_ref[...] = jnp.zeros_like(dws_ref)
            dbs_ref[...] = jnp.zeros_like(dbs_ref)

        has_prev = ti > 0
        has_next = i > 0
        col = lambda c: slice(c * GROUP_W, (c + 1) * GROUP_W)
        z_of = lambda c: z_ref[:, col(c)]
        halo_of = lambda c: jnp.where(has_prev, zh_ref[:, col(c)], 0.0)
        next_of = lambda c: zn_ref[:, col(c)]

        p, p_h, cv = _conv_a(z_of, halo_of, wA_ref)
        sg, dsg = _silu_and_grad(z_of(3))
        a_b = z_of(1)
        dya = dy_ref[:, col(0)]
        dcv = dya * a_b * sg
        dcv_n = jnp.where(has_next, dyn_ref[...] * next_of(1) * _silu_and_grad(next_of(3))[0], 0.0)
        dp = (wA_ref[2:3, :] * dcv + wA_ref[1:2, :] * _shift_up(dcv, dcv_n, 1)
              + wA_ref[0:1, :] * _shift_up(dcv, dcv_n, 2))
        dwA_ref[2:3, :] += _colsum(dcv * p)
        dwA_ref[1:2, :] += _colsum(dcv * _shift_down(p, p_h, 1))
        dwA_ref[0:1, :] += _colsum(dcv * _shift_down(p, p_h, 2))
        def put_dz(c, val):
            dz_ref[:, col(c)] = val.astype(dz_ref.dtype)

        put_dz(0, dp * z_of(2))
        put_dz(1, dya * cv * sg)
        put_dz(2, dp * z_of(0))
        put_dz(3, dya * a_b * cv * dsg)

        xc, sh, ga, gi, a, mult, sp = _lru_gates(z_of, halo_of, wR_ref, vec_ref, wa_ref, wx_ref)
        h = h_ref[...]
        h_prev = _shift_down(h, jnp.where(has_prev, hh_ref[...], 0.0), 1)
        sgr, dsgr = _silu_and_grad(z_of(5))
        dyb = dy_ref[:, col(1)]
        put_dz(5, dyb * h * dsgr)
        row = lax.broadcasted_iota(jnp.int32, (tm, GROUP_W), 0)
        g_in = dyb * sgr + jnp.where(row == tm - 1, hcarry_ref[0:1, :], 0.0)
        a_up = _shift_up(a, jnp.zeros((SUBLANES, GROUP_W), F32), 1)
        dH = _scan_rev_tile(a_up, g_in, sa_ref, sb_ref, sc_ref)
        hcarry_ref[...] = (a * dH)[0:SUBLANES]
        da = dH * h_prev
        gx = gi * xc
        dmult = dH * gx
        dgi = dH * mult * xc
        dxc = dH * mult * gi
        dlog_a = da * a - dmult * (a * a) / mult
        dga = dlog_a * (-RG_C * sp)
        dlam_row = _colsum(dlog_a * (-RG_C * ga)) * (-_sigmoid(-vec_ref[3:4, :]))
        dpre_a = dga * ga * (1.0 - ga)
        dpre_i = dgi * gi * (1.0 - gi)
        dwa_ref[...] += _mm_tn(xc, dpre_a)
        dwx_ref[...] += _mm_tn(xc, dpre_i)
        dxc = dxc + _mm_nt(dpre_a, wa_ref[...]) + _mm_nt(dpre_i, wx_ref[...])
        dvec_ref[0:1, :] += _colsum(dxc)
        dvec_ref[1:2, :] += _colsum(dpre_a)
        dvec_ref[2:3, :] += _colsum(dpre_i)
        dvec_ref[3:4, :] += dlam_row
        for k in range(4):
            dwR_ref[k:k + 1, :] += _colsum(dxc * sh[3 - k])
        dxc_n = xcarry_ref[...]
        put_dz(4, wR_ref[3:4, :] * dxc + wR_ref[2:3, :] * _shift_up(dxc, dxc_n, 1)
               + wR_ref[1:2, :] * _shift_up(dxc, dxc_n, 2) + wR_ref[0:1, :] * _shift_up(dxc, dxc_n, 3))
        xcarry_ref[...] = dxc[0:SUBLANES]

        c_u, c_v = z_of(6), z_of(7)
        u, du_dx = _gelu_and_grad(c_u)
        gv, dgv_dx = _gelu_and_grad(c_v)
        rr = lax.rsqrt(jnp.mean(gv * gv, axis=-1, keepdims=True) + NORM_EPS)
        xhat = gv * rr
        g_c = vec_ref[4:5, :]
        vn = xhat * g_c
        masks = _head_masks((GMLP_CHUNK, GROUP_W))
        tri_r = lax.broadcasted_iota(jnp.int32, (GMLP_CHUNK, GMLP_CHUNK), 0)
        tri_c = lax.broadcasted_iota(jnp.int32, (GMLP_CHUNK, GMLP_CHUNK), 1)
        tril = tri_r >= tri_c
        sgc, dsgc = _silu_and_grad(z_of(8))
        dyc = dy_ref[:, col(2)]
        dsp_full = dyc * u * sgc
        sp_parts, dvn_parts = [], []
        for c in range(tm // GMLP_CHUNK):
            rs = slice(c * GMLP_CHUNK, (c + 1) * GMLP_CHUNK)
            vc = vn[rs].astype(MXU_DTYPE)
            dsp_c = dsp_full[rs]
            bsacc_ref[...] += dsp_c
            acc = bs_ref[...]
            dvn_c = jnp.zeros((GMLP_CHUNK, GROUP_W), F32)
            for h in range(N_HEADS):
                w_h = ws_ref[h]
                acc = acc + jnp.where(masks[h], jnp.dot(w_h, vc, preferred_element_type=F32), 0.0)
                dsp_h = jnp.where(masks[h], dsp_c, 0.0).astype(MXU_DTYPE)
                dvn_c = dvn_c + _mm_tn(w_h, dsp_h)
                dws_ref[h] += jnp.where(tril, _mm_nt(dsp_h, vc), 0.0)
            sp_parts.append(acc)
            dvn_parts.append(dvn_c)
        spv = jnp.concatenate(sp_parts, axis=0)
        dvn = jnp.concatenate(dvn_parts, axis=0)
        put_dz(6, dyc * spv * sgc * du_dx)
        put_dz(8, dyc * u * spv * dsgc)
        dvec_ref[4:5, :] += _colsum(dvn * xhat)
        dgvn = dvn * g_c
        dgv = rr * (dgvn - xhat * jnp.mean(dgvn * xhat, axis=-1, keepdims=True))
        put_dz(7, dgv * dgv_dx)

        sgd, dsgd = _silu_and_grad(zg_ref[...])
        dyd = dy_ref[:, col(3)]
        ov = o_ref[...]
        do = dyd * sgd
        _put(do_ref, do)
        dzg_ref[...] = (dyd * ov * dsgd).astype(dzg_ref.dtype)
        prod = do * ov
        tmasks = _head_masks((tm, GROUP_W))
        dl = jnp.zeros((tm, GROUP_W), F32)
        for h in range(N_HEADS):
            dl = jnp.where(tmasks[h], jnp.sum(jnp.where(tmasks[h], prod, 0.0), axis=-1, keepdims=True), dl)
        _put(dl_ref, dl)
        for dil, d_out, l_out in zip(ATTN_DILATIONS, (do1_ref, do4_ref, do16_ref), (dl1_ref, dl4_ref, dl16_ref)):
            _deinterleave(do_ref, d_out, dil)
            _deinterleave(dl_ref, l_out, dil)

        @pl.when(i == nT - 1)
        def _():
            acc = bsacc_ref[...]
            lane = lax.broadcasted_iota(jnp.int32, (GMLP_CHUNK, LANES), 1)
            out = jnp.zeros((GMLP_CHUNK, LANES), F32)
            for h in range(N_HEADS):
                out = jnp.where(lane == h, jnp.sum(jnp.where(masks[h], acc, 0.0), axis=-1, keepdims=True), out)
            dbs_ref[...] = out

    rev = lambda w: pl.BlockSpec((tm, w), lambda i: (nT - 1 - i, 0))
    prev8 = lambda w: pl.BlockSpec((SUBLANES, w), lambda i: (jnp.maximum((nT - 1 - i) * hb - 1, 0), 0))
    next8 = lambda w: pl.BlockSpec((SUBLANES, w), lambda i: (jnp.minimum((nT - i) * hb, last_blk), 0))
    const2 = lambda shape: pl.BlockSpec(shape, lambda i: (0, 0))
    dil_specs = [_dilated_spec(tm, GROUP_W, dil, lambda i: nT - 1 - i) for dil in ATTN_DILATIONS]
    dil_shapes = [_dilated_shape(S, GROUP_W, dil, F32) for dil in ATTN_DILATIONS]
    small = (SUBLANES, GROUP_W)
    sq = (GROUP_W, GROUP_W)
    ws_shape = (N_HEADS, GMLP_CHUNK, GMLP_CHUNK)
    return pl.pallas_call(
        body, name=name, grid=(nT,),
        in_specs=[rev(wcols), prev8(wcols), next8(wcols), rev(GROUP_W),
                  rev(4 * GROUP_W), next8(GROUP_W), rev(GROUP_W), prev8(GROUP_W), rev(GROUP_W)]
                 + _mix_specs(tm, S, "bwd"),
        out_specs=[rev(wcols), rev(GROUP_W)] + dil_specs + dil_specs
                  + [const2(small), const2(small), const2(small), const2(sq), const2(sq),
                     pl.BlockSpec(ws_shape, lambda i: (0, 0, 0)), const2((GMLP_CHUNK, LANES))],
        out_shape=[jax.ShapeDtypeStruct((S, wcols), MXU_DTYPE), jax.ShapeDtypeStruct((S, GROUP_W), MXU_DTYPE)]
                  + [_dilated_shape(S, GROUP_W, dil, MXU_DTYPE) for dil in ATTN_DILATIONS] + dil_shapes
                  + [jax.ShapeDtypeStruct(small, F32)] * 3 + [jax.ShapeDtypeStruct(sq, F32)] * 2
                  + [jax.ShapeDtypeStruct(ws_shape, F32), jax.ShapeDtypeStruct((GMLP_CHUNK, LANES), F32)],
        scratch_shapes=[pltpu.VMEM(small, F32), pltpu.VMEM(small, F32), pltpu.VMEM((GMLP_CHUNK, GROUP_W), F32),
                        _lane_scratch(tm, GROUP_W), _lane_scratch(tm, GROUP_W),
                        _lane_scratch(tm, GROUP_W), _lane_scratch(tm, GROUP_W), pltpu.VMEM((hb, GROUP_W), F32)],
        compiler_params=_params(("arbitrary",)),
    )(z, z, z, z_g, dy, dy, hs, hs, o, mp["wA"], mp["wR"], mp["vec"], mp["wa"], mp["wx"], mp["ws"], mp["bs"])


def _attn_bwd(qkv, do, lse, delta, dil, name):
    rows = qkv.shape[0]
    nb = rows // ATTN_BLOCK
    scale = 1.0 / math.sqrt(HEAD_DIM)
    B = ATTN_BLOCK
    per_step = ATTN_BLOCKS_PER_STEP
    n_steps = nb // per_step

    def body(qc_ref, qn_ref, kc_ref, kp_ref, vc_ref, vp_ref, doc_ref, don_ref, lc_ref, ln_ref, dc_ref, dn_ref,
             dq_ref, dk_ref, dv_ref, bias_ref, bias_next_ref):
        n = pl.program_id(1)

        @pl.when(n == 0)
        def _():
            bias_ref[...] = _attn_bias(dil, (B,), 2 * B)
            bias_next_ref[...] = _attn_bias(dil, (B,), B)

        masks = _head_masks((B, GROUP_W))

        def per_row(tile):
            return jnp.concatenate([jnp.max(jnp.where(masks[h], tile, _NEG), axis=-1, keepdims=True)
                                    for h in range(N_HEADS)], axis=0)

        def grads(q, dov, lse_tile, dl_tile, keys, vals, bias, dead):
            qs = _stack_heads(q, masks)
            dos = _stack_heads(dov.astype(MXU_DTYPE), masks)
            s = _mm_nt(qs, keys) * scale + bias
            if dead is not None:
                s = jnp.where(dead(s.shape), _NEG, s)
            p = jnp.exp(s - per_row(lse_tile))
            ds = (p * (_mm_nt(dos, vals) - per_row(dl_tile)) * scale).astype(MXU_DTYPE)
            return ds, _mm_tn(ds, qs), _mm_tn(p.astype(MXU_DTYPE), dos)

        for j in range(per_step):
            own = slice(j * B, (j + 1) * B)
            before = slice((j - 1) * B, j * B)
            keys = jnp.concatenate([kp_ref[...] if j == 0 else kc_ref[before], kc_ref[own]], axis=0)
            vals = jnp.concatenate([vp_ref[...] if j == 0 else vc_ref[before], vc_ref[own]], axis=0)
            dead = (lambda shape: (n == 0) & (lax.broadcasted_iota(jnp.int32, shape, 1) < B)) if j == 0 else None
            ds, dk2, dv2 = grads(qc_ref[own], doc_ref[own], lc_ref[own], dc_ref[own], keys, vals, bias_ref[...], dead)
            dq_ref[own] = _unstack_heads(jnp.dot(ds, keys, preferred_element_type=F32), masks).astype(dq_ref.dtype)
            if j > 0:
                dk_ref[before] = (dk_own + dk2[:B]).astype(dk_ref.dtype)
                dv_ref[before] = (dv_own + dv2[:B]).astype(dv_ref.dtype)
            dk_own, dv_own = dk2[B:], dv2[B:]
        last = slice((per_step - 1) * B, per_step * B)
        _, dk1, dv1 = grads(qn_ref[...], don_ref[...], ln_ref[...], dn_ref[...], kc_ref[last], vc_ref[last],
                            bias_next_ref[...], lambda shape: n == n_steps - 1)
        dk_ref[last] = (dk_own + dk1).astype(dk_ref.dtype)
        dv_ref[last] = (dv_own + dv1).astype(dv_ref.dtype)

    blk = (per_step * B, GROUP_W)
    one = (B, GROUP_W)
    nxt_idx = lambda n: jnp.minimum((n + 1) * per_step, nb - 1)
    prv_idx = lambda n: jnp.maximum(n * per_step - 1, 0)
    zcur = lambda c: pl.BlockSpec(blk, lambda r, n: (n, r * 3 + c))
    znext = lambda c: pl.BlockSpec(one, lambda r, n: (nxt_idx(n), r * 3 + c))
    zprev = lambda c: pl.BlockSpec(one, lambda r, n: (prv_idx(n), r * 3 + c))
    cur = pl.BlockSpec(blk, lambda r, n: (n, r))
    nxt = pl.BlockSpec(one, lambda r, n: (nxt_idx(n), r))
    return pl.pallas_call(
        body, name=name, grid=(dil, n_steps),
        in_specs=[zcur(0), znext(0), zcur(1), zprev(1), zcur(2), zprev(2), cur, nxt, cur, nxt, cur, nxt],
        out_specs=[cur, cur, cur],
        out_shape=[jax.ShapeDtypeStruct((rows, dil * GROUP_W), WIRE_DTYPE)] * 3,
        scratch_shapes=[pltpu.VMEM((N_HEADS * B, 2 * B), F32), pltpu.VMEM((N_HEADS * B, B), F32)],
        compiler_params=_params(("parallel", "arbitrary")),
    )(qkv, qkv, qkv, qkv, qkv, qkv, do, do, lse, lse, delta, delta)


def _inproj_bwd(x, g, dxn, dz_abc, dqkv, dz_g, w8, name):
    S, D = x.shape
    N = w8.shape[0] * w8.shape[2]
    tm = TM_MM
    n_abc = N_ABC * GROUP_W

    def body(x_ref, g_ref, dxn_ref, dabc_ref, q1, k1, v1, q2, k2, v2, q3, k3, v3, dg_ref, w8_ref,
             dx_ref, dz_ref, h_ref, dgn_ref, s4_ref, s16_ref, w_ref):
        i = pl.program_id(0)

        @pl.when(i == 0)
        def _():
            dgn_ref[...] = jnp.zeros_like(dgn_ref)
            _assemble_columns(w8_ref, w_ref)

        dz_ref[:, 0:n_abc] = dabc_ref[...].astype(MXU_DTYPE)
        for j, parts in enumerate(((q1, q2, q3), (k1, k2, k3), (v1, v2, v3))):
            c0 = n_abc + j * GROUP_W
            _interleave(parts[1], s4_ref, ATTN_DILATIONS[1])
            _interleave(parts[2], s16_ref, ATTN_DILATIONS[2])
            dz_ref[:, c0:c0 + GROUP_W] = (parts[0][...] + _get(s4_ref) + _get(s16_ref)).astype(MXU_DTYPE)
        dz_ref[:, n_abc + 3 * GROUP_W:] = dg_ref[...].astype(MXU_DTYPE)
        dh = _mm_nt(dz_ref[...], w_ref[...])
        xv = x_ref[...]
        r = lax.rsqrt(jnp.mean(xv * xv, axis=-1, keepdims=True) + NORM_EPS)
        xn = xv * r
        gv = g_ref[...]
        h_ref[...] = (xn * gv).astype(MXU_DTYPE)
        dgn_ref[...] += _colsum(dh * xn)
        dn = dh * gv
        dx_ref[...] = dxn_ref[...] + r * (dn - xn * jnp.mean(dn * xn, axis=-1, keepdims=True))

    row = lambda w: pl.BlockSpec((tm, w), lambda i: (i, 0))
    flat = [t for p in dqkv for t in p]
    dil_specs = [_dilated_spec(tm, GROUP_W, dil) for dil in ATTN_DILATIONS for _ in range(3)]
    return pl.pallas_call(
        body, name=name, grid=(S // tm,),
        in_specs=[row(D), pl.BlockSpec((1, D), lambda i: (0, 0)), row(D), row(n_abc)] + dil_specs
                 + [row(GROUP_W), pl.BlockSpec(w8.shape, lambda i: (0, 0, 0), pipeline_mode=pl.Buffered(1))],
        out_specs=[row(D), row(N), row(D), pl.BlockSpec((1, D), lambda i: (0, 0))],
        out_shape=[jax.ShapeDtypeStruct((S, D), F32), jax.ShapeDtypeStruct((S, N), MXU_DTYPE),
                   jax.ShapeDtypeStruct((S, D), MXU_DTYPE), jax.ShapeDtypeStruct((1, D), F32)],
        scratch_shapes=[_lane_scratch(tm, GROUP_W)] * 2 + [pltpu.VMEM((D, N), w8.dtype)],
        compiler_params=_params(("arbitrary",)),
    )(x, g, dxn, dz_abc, *flat, dz_g, w8)


def _inproj_wgrad(h, dz, name):
    S, D = h.shape
    N = dz.shape[1]
    tm = TM_WGRAD
    nj = 2
    cw = N // nj
    per = N_DEV // nj
    n_loc = N // N_DEV

    def body(h_ref, dz_ref, dw_ref, acc_ref):
        i = pl.program_id(1)

        @pl.when(i == 0)
        def _():
            acc_ref[...] = jnp.zeros_like(acc_ref)

        acc_ref[...] += _mm_tn(h_ref[...], dz_ref[...])

        @pl.when(i == S // tm - 1)
        def _():
            for b in range(per):
                dw_ref[b] = acc_ref[:, b * n_loc:(b + 1) * n_loc].astype(dw_ref.dtype)

    return pl.pallas_call(
        body, name=name, grid=(nj, S // tm),
        in_specs=[pl.BlockSpec((tm, D), lambda j, i: (i, 0)), pl.BlockSpec((tm, cw), lambda j, i: (i, j))],
        out_specs=pl.BlockSpec((per, D, n_loc), lambda j, i: (j, 0, 0)),
        out_shape=jax.ShapeDtypeStruct((N_DEV, D, n_loc), WIRE_DTYPE),
        scratch_shapes=[pltpu.VMEM((D, cw), F32)],
        compiler_params=_params(("parallel", "arbitrary")),
    )(h, dz)


def _my_place():
    return lax.axis_index("x"), lax.axis_index("y"), lax.axis_index("c")


def _peer(x, y, c, k):
    px = 1 - x if k & 4 else x
    py = 1 - y if k & 2 else y
    pc = 1 - c if k & 1 else c
    return (px, py, pc), 4 * px + 2 * py + pc


HBM_SPEC = pl.BlockSpec(memory_space=pltpu.HBM)
SEM_SPEC = pl.BlockSpec(memory_space=pltpu.SEMAPHORE)
SPLIT_EFFECT = pltpu.SideEffectType.DATAFLOW_SIDE_EFFECTING
N_PEERS = N_DEV - 1


def _exchange_copies(srcs, lands, send_sems, recv_sems, whole, arrival):
    x, y, c = _my_place()
    me = 4 * x + 2 * y + c
    copies = []
    for t in range(len(srcs)):
        for k in range(1, N_DEV):
            peer, pidx = _peer(x, y, c, k)
            copies.append(pltpu.make_async_remote_copy(
                src_ref=srcs[t] if whole[t] else srcs[t].at[pidx],
                dst_ref=lands[t].at[pidx if arrival else me], send_sem=send_sems.at[t * N_PEERS + k - 1],
                recv_sem=recv_sems.at[t * N_PEERS + k - 1], device_id=peer, device_id_type=MESH))
    return copies


def _exchange_start(groups, name, after=None):
    sizes = [len(g) for g in groups]
    whole = [w for g in groups for _, w in g]
    srcs = [pltpu.with_memory_space_constraint(a, pltpu.HBM) for g in groups for a, _ in g]
    lands = [pltpu.with_memory_space_constraint(lax.empty(((N_DEV,) + a.shape) if w else a.shape, a.dtype), pltpu.HBM)
             for a, w in zip(srcs, whole)]
    n = len(srcs)
    n_g = len(groups)
    extra = [] if after is None else [after]
    n_in = 2 * n + len(extra)

    def body(*refs):
        src_refs, land_refs = refs[:n], refs[n:2 * n]
        sem_refs = refs[n_in + 2 * n:n_in + 2 * n + 2 * n_g]
        token = refs[-1]
        off = 0
        for gi, sz in enumerate(sizes):
            for send in _exchange_copies(src_refs[off:off + sz], land_refs[off:off + sz],
                                         sem_refs[2 * gi], sem_refs[2 * gi + 1], whole[off:off + sz], False):
                send.start()
            off += sz
        token[...] = jnp.zeros_like(token)

    sem_shapes = [pltpu.SemaphoreType.DMA((sz * N_PEERS,)) for sz in sizes for _ in range(2)]
    outs = pl.pallas_call(
        body, name=name,
        in_specs=[HBM_SPEC] * (2 * n) + [pl.BlockSpec(memory_space=pl.ANY)] * len(extra),
        out_specs=[HBM_SPEC] * (2 * n) + [SEM_SPEC] * (2 * n_g) + [pl.BlockSpec(memory_space=pltpu.VMEM)],
        out_shape=[pltpu.HBM(a.shape, a.dtype) for a in srcs + lands] + sem_shapes
                  + [jax.ShapeDtypeStruct((SUBLANES, LANES), F32)],
        input_output_aliases={i: i for i in range(2 * n)},
        compiler_params=pltpu.CompilerParams(has_side_effects=SPLIT_EFFECT),
    )(*srcs, *lands, *extra)
    handles, off = [], 0
    for gi, sz in enumerate(sizes):
        handles.append((outs[2 * n + 2 * gi], outs[2 * n + 2 * gi + 1], outs[off:off + sz], outs[n + off:n + off + sz],
                        whole[off:off + sz]))
        off += sz
    return handles, outs[-1]


def _exchange_wait(handle, after, name):
    send_sems, recv_sems, srcs, lands, whole = handle
    n = len(srcs)

    def body(*refs):
        src_refs, land_refs = refs[:n], refs[n:2 * n]
        for send in _exchange_copies(src_refs, land_refs, refs[2 * n], refs[2 * n + 1], whole, False):
            send.wait_send()
        for arrival in _exchange_copies(src_refs, land_refs, refs[2 * n], refs[2 * n + 1], whole, True):
            arrival.wait_recv()

    outs = pl.pallas_call(
        body, name=name,
        in_specs=[HBM_SPEC] * (2 * n) + [SEM_SPEC, SEM_SPEC, pl.BlockSpec(memory_space=pl.ANY)],
        out_specs=[HBM_SPEC] * (2 * n),
        out_shape=[pltpu.HBM(a.shape, a.dtype) for a in list(srcs) + list(lands)],
        input_output_aliases={i: i for i in range(2 * n)},
        compiler_params=pltpu.CompilerParams(has_side_effects=SPLIT_EFFECT),
    )(*srcs, *lands, send_sems, recv_sems, after)
    x, y, c = _my_place()
    me = 4 * x + 2 * y + c
    own = [s[None] if w else lax.dynamic_slice_in_dim(s, me, 1, axis=0) for s, w in zip(outs[:n], whole)]
    return [lax.dynamic_update_slice_in_dim(ld, o, me, axis=0) for ld, o in zip(outs[n:], own)]


def _sum_slots(parts, name):
    n = len(parts)

    def body(*refs):
        for p_ref, o_ref in zip(refs[:n], refs[n:]):
            acc = p_ref[0]
            for j in range(1, N_DEV):
                acc = acc + p_ref[j]
            o_ref[...] = acc

    vm = pl.BlockSpec(memory_space=pltpu.VMEM)
    return pl.pallas_call(
        body, name=name, in_specs=[vm] * n, out_specs=[vm] * n,
        out_shape=[jax.ShapeDtypeStruct(p.shape[1:], F32) for p in parts],
        compiler_params=pltpu.CompilerParams(vmem_limit_bytes=VMEM_LIMIT),
    )(*parts)


def _adamw_math(w, g, m, v):
    m = ADAM_B1 * m + (1.0 - ADAM_B1) * g
    v = ADAM_B2 * v + (1.0 - ADAM_B2) * (g * g)
    m_hat = m / (1.0 - ADAM_B1 ** ADAM_STEP)
    v_hat = v / (1.0 - ADAM_B2 ** ADAM_STEP)
    delta = -ADAM_LR * (m_hat / (jnp.sqrt(v_hat) + ADAM_EPS) + ADAM_WD * w)
    return delta, m, v


def _adamw_summed(parts, w, m, v, tr, name):
    depth, R, C = w.shape

    def body(*refs):
        p_refs = refs[:depth]
        w_ref, m_ref, v_ref, g_ref, d_ref, nm_ref, nv_ref = refs[depth:]
        lay = pl.program_id(0)
        for l in range(depth):
            @pl.when(lay == l)
            def _(p_ref=p_refs[l]):
                g = p_ref[0].astype(F32)
                for j in range(1, N_DEV):
                    g = g + p_ref[j].astype(F32)
                g_ref[0] = g
        d_ref[0], nm_ref[0], nv_ref[0] = _adamw_math(w_ref[0], g_ref[0], m_ref[0], v_ref[0])

    part_spec = lambda l: pl.BlockSpec((N_DEV, tr, C), lambda lay, i: (0, jnp.where(lay == l, i, 0), 0))
    row = pl.BlockSpec((1, tr, C), lambda lay, i: (lay, i, 0))
    return pl.pallas_call(
        body, name=name, grid=(depth, R // tr),
        in_specs=[part_spec(l) for l in range(depth)] + [row, row, row],
        out_specs=[row] * 4, out_shape=[jax.ShapeDtypeStruct((depth, R, C), F32)] * 4,
        compiler_params=_params(("arbitrary", "arbitrary")),
    )(*parts, w, m, v)


def _adamw_small(w, g, m, v, name):
    def body(w_ref, g_ref, m_ref, v_ref, d_ref, nm_ref, nv_ref):
        d_ref[...], nm_ref[...], nv_ref[...] = _adamw_math(w_ref[...], g_ref[...], m_ref[...], v_ref[...])

    vm = pl.BlockSpec(memory_space=pltpu.VMEM)
    return pl.pallas_call(
        body, name=name, in_specs=[vm] * 4, out_specs=[vm] * 3,
        out_shape=[jax.ShapeDtypeStruct(w.shape, F32)] * 3,
        compiler_params=pltpu.CompilerParams(vmem_limit_bytes=VMEM_LIMIT),
    )(w, g, m, v)


def _pack(arrays):
    flat = jnp.concatenate([a.reshape(-1) for a in arrays])
    pad = (-flat.shape[0]) % (SUBLANES * LANES)
    return jnp.pad(flat, (0, pad)).reshape(-1, LANES)


def _unpack(buf, like):
    flat = buf.reshape(-1)
    out, off = [], 0
    for a in like:
        out.append(flat[off:off + a.size].reshape(a.shape))
        off += a.size
    return out


def _block_diag(w):
    eye = jnp.eye(N_HEADS, dtype=w.dtype)
    return jnp.einsum('hij,hk->hikj', w, eye).reshape(GROUP_W, GROUP_W)


def _diag_blocks(w):
    return jnp.einsum('hihj->hij', w.reshape(N_HEADS, HEAD_DIM, N_HEADS, HEAD_DIM))


def _pad_rows(a):
    return jnp.pad(a, ((0, SUBLANES - a.shape[0]), (0, 0)))


def _mixer_params(l, conv_a_w, conv_r_w, conv_r_b, lru_wa, lru_ba, lru_wx, lru_bx, lru_lambda, gmlp_norm_g,
                  gmlp_ws, gmlp_bs):
    tril = jnp.tril(jnp.ones((GMLP_CHUNK, GMLP_CHUNK), dtype=bool))
    vec = jnp.stack([conv_r_b[l], lru_ba[l], lru_bx[l], lru_lambda[l], gmlp_norm_g[l]])
    return {
        "wA": _pad_rows(conv_a_w[l]), "wR": _pad_rows(conv_r_w[l]), "vec": _pad_rows(vec),
        "wa": _block_diag(lru_wa[l]).astype(MXU_DTYPE), "wx": _block_diag(lru_wx[l]).astype(MXU_DTYPE),
        "ws": jnp.where(tril[None], gmlp_ws[l], 0.0).astype(MXU_DTYPE),
        "bs": jnp.repeat(jnp.transpose(gmlp_bs[l]), HEAD_DIM, axis=1),
    }


MIXER_NAMES = ("conv_a_w", "conv_r_w", "conv_r_b", "lru_wa", "lru_ba", "lru_wx", "lru_bx", "lru_lambda",
               "gmlp_norm_g", "gmlp_ws", "gmlp_bs")
SMALL_NAMES = ("norm_g",) + MIXER_NAMES + ("final_g",)


def _local_step(x, loss_target, norm_g, get_w_in, get_w_out, emit_early, emit_late, conv_a_w, conv_r_w, conv_r_b,
                lru_wa, lru_ba, lru_wx, lru_bx, lru_lambda, gmlp_norm_g, gmlp_ws, gmlp_bs, final_g):
    depth = norm_g.shape[0]
    D = x.shape[1]
    small = (conv_a_w, conv_r_w, conv_r_b, lru_wa, lru_ba, lru_wx, lru_bx, lru_lambda, gmlp_norm_g, gmlp_ws, gmlp_bs)
    saved = []
    for l in range(depth):
        mp = _mixer_params(l, *small)
        w_in_l = get_w_in(l, x)
        z, z_g, *qkv, y_abc, hs = _inproj_mix_fwd(x, norm_g[l].reshape(1, D), w_in_l, mp, f"inproj_mix_fwd_{l}")
        attn =[_attn_fwd(qkv[p], dil, f"attn_fwd_d{dil}_{l}") for p, dil in enumerate(ATTN_DILATIONS)]
        w_out_l = get_w_out(l, y_abc)
        x_new, y, o, *lse = _outproj(x, z_g, y_abc, attn, w_out_l, f"outproj_{l}")
        saved.append((x, z, z_g, qkv, hs, y, o, lse, mp, w_in_l, w_out_l))
        x = x_new
    dx, loss, d_final_g = _loss_head(x, final_g.reshape(1, D), loss_target, "loss_head")
    token = None
    for l in reversed(range(depth)):
        x_l, z, z_g, qkv, hs, y, o, lse, mp, w_in_l, w_out_l = saved[l]
        if token is not None:
            mp = dict(mp, vec=mp["vec"] + token[0, 0])
        dy, dw_out = _outproj_bwd(dx, y, w_out_l, f"outproj_bwd_{l}")
        (dz_abc, dz_g, do1, do4, do16, dl1, dl4, dl16, dwA, dwR, dvec, dwa, dwx, dws, dbs) = _mix_bwd(
            z, z_g, dy, hs, o, mp, f"mix_bwd_{l}")
        token = emit_early(l, dw_out, [
            dwA[:conv_a_w.shape[1]], dwR[:conv_r_w.shape[1]], dvec[0], _diag_blocks(dwa), dvec[1], _diag_blocks(dwx),
            dvec[2], dvec[3], dvec[4], dws, jnp.transpose(dbs[:, :N_HEADS])])
        g_row = norm_g[l].reshape(1, D)
        if token is not None:
            g_row = g_row + token[0, 0]
        dqkv = [_attn_bwd(qkv[p], do, lse[p], dl, dil, f"attn_bwd_d{dil}_{l}")
                for p, (dil, do, dl) in enumerate(zip(ATTN_DILATIONS, (do1, do4, do16), (dl1, dl4, dl16)))]
        dx, dz, h, dng = _inproj_bwd(x_l, g_row, dx, dz_abc, dqkv, dz_g, w_in_l, f"inproj_bwd_{l}")
        dw_in = _inproj_wgrad(h, dz, f"inproj_wgrad_{l}")
        token = emit_late(l, dw_in, [dng[0]] + ([d_final_g[0]] if l == depth - 1 else []))
    return loss[0, 0], dx
WEIGHT_NAMES = ("norm_g", "w_in", "conv_a_w", "conv_r_w", "conv_r_b", "lru_wa", "lru_ba", "lru_wx", "lru_bx",
                "lru_lambda", "gmlp_norm_g", "gmlp_ws", "gmlp_bs", "w_out", "final_g")


def kernel(x, norm_g, w_in, conv_a_w, conv_r_w, conv_r_b, lru_wa, lru_ba, lru_wx, lru_bx, lru_lambda, gmlp_norm_g, gmlp_ws, gmlp_bs, w_out, final_g, loss_target, m_norm_g, m_w_in, m_conv_a_w, m_conv_r_w, m_conv_r_b, m_lru_wa, m_lru_ba, m_lru_wx, m_lru_bx, m_lru_lambda, m_gmlp_norm_g, m_gmlp_ws, m_gmlp_bs, m_w_out, m_final_g, v_norm_g, v_w_in, v_conv_a_w, v_conv_r_w, v_conv_r_b, v_lru_wa, v_lru_ba, v_lru_wx, v_lru_bx, v_lru_lambda, v_gmlp_norm_g, v_gmlp_ws, v_gmlp_bs, v_w_out, v_final_g):
    w = dict(norm_g=norm_g, w_in=w_in, conv_a_w=conv_a_w, conv_r_w=conv_r_w, conv_r_b=conv_r_b, lru_wa=lru_wa,
             lru_ba=lru_ba, lru_wx=lru_wx, lru_bx=lru_bx, lru_lambda=lru_lambda, gmlp_norm_g=gmlp_norm_g,
             gmlp_ws=gmlp_ws, gmlp_bs=gmlp_bs, w_out=w_out, final_g=final_g)
    m = dict(norm_g=m_norm_g, w_in=m_w_in, conv_a_w=m_conv_a_w, conv_r_w=m_conv_r_w, conv_r_b=m_conv_r_b,
             lru_wa=m_lru_wa, lru_ba=m_lru_ba, lru_wx=m_lru_wx, lru_bx=m_lru_bx, lru_lambda=m_lru_lambda,
             gmlp_norm_g=m_gmlp_norm_g, gmlp_ws=m_gmlp_ws, gmlp_bs=m_gmlp_bs, w_out=m_w_out, final_g=m_final_g)
    v = dict(norm_g=v_norm_g, w_in=v_w_in, conv_a_w=v_conv_a_w, conv_r_w=v_conv_r_w, conv_r_b=v_conv_r_b,
             lru_wa=v_lru_wa, lru_ba=v_lru_ba, lru_wx=v_lru_wx, lru_bx=v_lru_bx, lru_lambda=v_lru_lambda,
             gmlp_norm_g=v_gmlp_norm_g, gmlp_ws=v_gmlp_ws, gmlp_bs=v_gmlp_bs, w_out=v_w_out, final_g=v_final_g)
    depth, D, n_loc = w_in.shape
    e_loc = w_out.shape[1]
    cx, cy, cc = _my_place()
    me = 4 * cx + 2 * cy + cc

    w_in_w, w_out_w = w_in.astype(MXU_DTYPE), w_out.astype(MXU_DTYPE)
    c_loc = conv_a_w.shape[2]
    taps = (conv_a_w, conv_r_w)
    first, _ = _exchange_start([[(w_in_w[0], True), (_pack(taps), True)]], "gather_start_first")
    full_out = lambda g: g.reshape(N_DEV * e_loc, D)

    g_in0, g_taps = _exchange_wait(first[0], x, "gather_wait_in_0")
    groups = [[(w_out_w[0], True)]] + [[(w_in_w[l], True), (w_out_w[l], True)] for l in range(1, depth)]
    gathers, _ = _exchange_start(groups, "gather_start_rest", after=g_taps)
    g_taps = g_taps.reshape(N_DEV, -1)
    conv_full, off = [], 0
    for a in taps:
        part = g_taps[:, off:off + a.size].reshape((N_DEV,) + a.shape)
        conv_full.append(jnp.transpose(part, (1, 2, 0, 3)).reshape(a.shape[:2] + (N_DEV * c_loc,)))
        off += a.size
    conv_a_full, conv_r_full = conv_full
    later = {}

    def get_w_in(l, after):
        if l == 0:
            return g_in0
        g_in, later[l] = _exchange_wait(gathers[l], after, f"gather_wait_{l}")
        return g_in

    def get_w_out(l, after):
        if l == 0:
            return full_out(_exchange_wait(gathers[0], after, "gather_wait_out_0")[0])
        return full_out(later[l])

    early, late, last_token = {}, {}, [None]

    def emit_early(l, dw_out, mixer_grads):
        handles, token = _exchange_start(
            [[(dw_out.reshape(N_DEV, e_loc, D), False), (_pack(mixer_grads), True)]], f"early_start_{l}")
        early[l] = (handles[0], mixer_grads)
        return token

    def emit_late(l, dw_in, norm_grads):
        handles, token = _exchange_start([[(_pack(norm_grads), True)], [(dw_in, False)]], f"late_start_{l}")
        late[l] = (handles[0], handles[1], norm_grads)
        last_token[0] = token
        return token

    loss, grad_x = _local_step(
        x[0], loss_target[0], norm_g, get_w_in, get_w_out, emit_early, emit_late, conv_a_full, conv_r_full, conv_r_b,
        lru_wa, lru_ba, lru_wx, lru_bx, lru_lambda, gmlp_norm_g, gmlp_ws, gmlp_bs, final_g)
    loss = lax.psum(loss, ("x", "y", "c"))

    r_in, r_out, small_parts = {}, {}, []
    for l in reversed(range(depth)):
        r_out[l], r_mix = _exchange_wait(early[l][0], last_token[0], f"early_wait_{l}")
        (r_norm,) = _exchange_wait(late[l][0], last_token[0], f"late_wait_norm_{l}")
        small_parts += [r_mix, r_norm]
        if l > 0:
            (r_in[l],) = _exchange_wait(late[l][1], last_token[0], f"late_wait_{l}")
    big = {"w_out": _adamw_summed([r_out[l] for l in range(depth)], w_out, m_w_out, v_w_out, 128, "adamw_w_out")}

    sums = _sum_slots(small_parts, "sum_small_grads")
    by_layer = {}
    for i, l in enumerate(reversed(range(depth))):
        mix = _unpack(sums[2 * i], early[l][1])
        nrm = _unpack(sums[2 * i + 1], late[l][2])
        by_layer[l] = dict(zip(MIXER_NAMES, mix), norm_g=nrm[0])
        if l == depth - 1:
            g_final = nrm[1]
    g_small = {k: jnp.stack([by_layer[l][k] for l in range(depth)]) for k in ("norm_g",) + MIXER_NAMES}
    g_small["final_g"] = g_final
    for k in ("conv_a_w", "conv_r_w"):
        g_small[k] = lax.dynamic_slice_in_dim(g_small[k], me * c_loc, c_loc, axis=2)
    packs = [_pack([d[k] for k in SMALL_NAMES]) for d in (w, g_small, m, v)]
    res = _adamw_small(*packs, "adamw_small")
    like = [w[k] for k in SMALL_NAMES]
    d_s, m_s, v_s = (dict(zip(SMALL_NAMES, _unpack(r, like))) for r in res)

    (r_in[0],) = _exchange_wait(late[0][1], res[0], "late_wait_0")
    big["w_in"] = _adamw_summed([r_in[l] for l in range(depth)], w_in, m_w_in, v_w_in, 512, "adamw_w_in")

    grad, delta, new_m, new_v = {}, {}, {}, {}
    for k in WEIGHT_NAMES:
        if k in big:
            grad[k], delta[k], new_m[k], new_v[k] = big[k]
        else:
            grad[k], delta[k], new_m[k], new_v[k] = g_small[k], d_s[k], m_s[k], v_s[k]
    return (loss, grad_x[None], *[grad[k] for k in WEIGHT_NAMES], *[delta[k] for k in WEIGHT_NAMES],
            *[new_m[k] for k in WEIGHT_NAMES], *[new_v[k] for k in WEIGHT_NAMES])
```

```python
import functools
import math

import jax
import jax.numpy as jnp
from jax import lax
from jax.experimental import pallas as pl
from jax.experimental.pallas import tpu as pltpu

F32 = jnp.float32
MXU_DTYPE = jnp.bfloat16
WIRE_DTYPE = jnp.bfloat16
MESH = pl.DeviceIdType.MESH

N_DEV = 8
GROUP_W = 256
N_HEADS = 4
HEAD_DIM = 64
N_CHUNKS = 13
N_ABC = 9
GMLP_CHUNK = 128
ATTN_BLOCK = 128
ATTN_BLOCKS_PER_STEP = 4
ATTN_DILATIONS = (1, 4, 16)
NORM_EPS = 1e-6
RG_C = 8.0
SUBLANES = 8
LANES = 128
VMEM_LIMIT = 56 * 1024 * 1024

ADAM_LR = 0.001
ADAM_B1 = 0.9
ADAM_B2 = 0.999
ADAM_EPS = 1e-08
ADAM_WD = 0.01
ADAM_STEP = 10

TM_MIX = 512
TM_MM = 512
TM_WGRAD = 1024


def _params(sem, vmem=VMEM_LIMIT):
    return pltpu.CompilerParams(dimension_semantics=sem, vmem_limit_bytes=vmem)


def _mm(a, b):
    return jnp.dot(a.astype(MXU_DTYPE), b.astype(MXU_DTYPE), preferred_element_type=F32)


def _mm_tn(a, b):
    return lax.dot_general(a.astype(MXU_DTYPE), b.astype(MXU_DTYPE), (((0,), (0,)), ((), ())),
                           preferred_element_type=F32)


def _mm_nt(a, b):
    return lax.dot_general(a.astype(MXU_DTYPE), b.astype(MXU_DTYPE), (((1,), (1,)), ((), ())),
                           preferred_element_type=F32)


def _sigmoid(x):
    return 0.5 * jnp.tanh(0.5 * x) + 0.5


def _silu_and_grad(x):
    s = _sigmoid(x)
    return x * s, s * (1.0 + x * (1.0 - s))


_GELU_K = math.sqrt(2.0 / math.pi)
_GELU_C = 0.044715


def _gelu_and_grad(x):
    x2 = x * x
    t = jnp.tanh(_GELU_K * (x + _GELU_C * x * x2))
    val = 0.5 * x * (1.0 + t)
    grad = 0.5 * (1.0 + t) + 0.5 * x * (1.0 - t * t) * (_GELU_K * (1.0 + 3.0 * _GELU_C * x2))
    return val, grad


def _gelu(x):
    return 0.5 * x * (1.0 + jnp.tanh(_GELU_K * (x + _GELU_C * x * x * x)))


def _expm1_nonpos(u):
    poly = 1.0 / math.factorial(9)
    for k in range(8, 0, -1):
        poly = poly * u + 1.0 / math.factorial(k)
    return jnp.where(u > -0.25, poly * u, jnp.exp(u) - 1.0)


def _softplus(x):
    return jnp.maximum(x, 0.0) + jnp.log(1.0 + jnp.exp(-jnp.abs(x)))


def _shift_down(t, halo, k):
    rolled = pltpu.roll(t, k, 0)
    hr = pltpu.roll(halo, k, 0)
    row = lax.broadcasted_iota(jnp.int32, halo.shape, 0)
    first = jnp.where(row < k, hr, rolled[0:SUBLANES])
    return jnp.concatenate([first, rolled[SUBLANES:]], axis=0)


def _shift_up(t, nxt, k):
    tm = t.shape[0]
    rolled = pltpu.roll(t, tm - k, 0)
    nr = pltpu.roll(nxt, SUBLANES - k, 0)
    row = lax.broadcasted_iota(jnp.int32, nxt.shape, 0)
    last = jnp.where(row >= SUBLANES - k, nr, rolled[tm - SUBLANES:tm])
    return jnp.concatenate([rolled[:tm - SUBLANES], last], axis=0)


def _scan_fwd(a, b):
    tm = a.shape[0]
    row = lax.broadcasted_iota(jnp.int32, a.shape, 0)
    s = 1
    while s < tm:
        a_s = pltpu.roll(a, s, 0)
        b_s = pltpu.roll(b, s, 0)
        m = row >= s
        b = jnp.where(m, a * b_s + b, b)
        a = jnp.where(m, a * a_s, a)
        s *= 2
    return a, b


def _scan_rev(a, g):
    tm = a.shape[0]
    row = lax.broadcasted_iota(jnp.int32, a.shape, 0)
    s = 1
    while s < tm:
        a_s = pltpu.roll(a, tm - s, 0)
        g_s = pltpu.roll(g, tm - s, 0)
        m = row < tm - s
        g = jnp.where(m, g + a * g_s, g)
        a = jnp.where(m, a * a_s, a)
        s *= 2
    return g


def _group_rows(scr_ref, row, n_groups):
    return jnp.concatenate([scr_ref[pl.ds(c, 1), pl.ds(row, n_groups, stride=SUBLANES), :][0]
                            for c in range(scr_ref.shape[0])], axis=1)


def _spread_rows(rows_ref, n_groups, w):
    return jnp.concatenate([jnp.broadcast_to(rows_ref[g:g + 1, :], (SUBLANES, w)) for g in range(n_groups)], axis=0)


def _scan_groups(a, b, reverse):
    tm, w = a.shape
    shape3 = (tm // SUBLANES, SUBLANES, w)
    a3, b3 = a.reshape(shape3), b.reshape(shape3)
    sub = lax.broadcasted_iota(jnp.int32, shape3, 1)
    s = 1
    while s < SUBLANES:
        shift = SUBLANES - s if reverse else s
        a_s = pltpu.roll(a3, shift, 1)
        b_s = pltpu.roll(b3, shift, 1)
        m = (sub < SUBLANES - s) if reverse else (sub >= s)
        b3 = jnp.where(m, a3 * b_s + b3, b3)
        a3 = jnp.where(m, a3 * a_s, a3)
        s *= 2
    return a3.reshape(tm, w), b3.reshape(tm, w)


def _scan_fwd_tile(a, b, h_in, sa_ref, sb_ref, sc_ref):
    tm, w = a.shape
    n_groups = tm // SUBLANES
    a_loc, b_loc = _scan_groups(a, b, False)
    _put(sa_ref, a_loc)
    _put(sb_ref, b_loc)
    a_end, b_end = _scan_fwd(_group_rows(sa_ref, SUBLANES - 1, n_groups), _group_rows(sb_ref, SUBLANES - 1, n_groups))
    h_end = b_end + a_end * h_in
    sc_ref[...] = _shift_down(h_end, jnp.broadcast_to(h_in, (SUBLANES, w)), 1)
    return b_loc + a_loc * _spread_rows(sc_ref, n_groups, w), h_end


def _scan_rev_tile(a, g, sa_ref, sb_ref, sc_ref):
    tm, w = a.shape
    n_groups = tm // SUBLANES
    a_loc, g_loc = _scan_groups(a, g, True)
    _put(sa_ref, a_loc)
    _put(sb_ref, g_loc)
    d_first = _scan_rev(_group_rows(sa_ref, 0, n_groups), _group_rows(sb_ref, 0, n_groups))
    sc_ref[...] = _shift_up(d_first, jnp.zeros((SUBLANES, w), F32), 1)
    return g_loc + a_loc * _spread_rows(sc_ref, n_groups, w)


def _lane_scratch(tm, w):
    return pltpu.VMEM((w // LANES, tm, LANES), F32)


def _put(scr_ref, val):
    for c in range(scr_ref.shape[0]):
        scr_ref[c] = val[:, c * LANES:(c + 1) * LANES].astype(F32)


def _get(scr_ref):
    return jnp.concatenate([scr_ref[c] for c in range(scr_ref.shape[0])], axis=1)


def _deinterleave(src_ref, dst_ref, dil):
    nc, tm, _ = src_ref.shape
    w = nc * LANES
    for r in range(dil):
        for c in range(nc):
            piece = src_ref[pl.ds(c, 1), pl.ds(r, tm // dil, stride=dil), :][0] if dil > 1 else src_ref[c]
            dst_ref[:, r * w + c * LANES:r * w + (c + 1) * LANES] = piece.astype(dst_ref.dtype)


def _interleave(src_ref, dst_ref, dil):
    nc, tm, _ = dst_ref.shape
    w = nc * LANES
    for r in range(dil):
        for c in range(nc):
            dst_ref[pl.ds(c, 1), pl.ds(r, tm // dil, stride=dil), :] = (
                src_ref[:, r * w + c * LANES:r * w + (c + 1) * LANES].astype(F32)[None])


def _dilated_spec(tm, w, dil, index=lambda i: i):
    return pl.BlockSpec((tm // dil, dil * w), lambda i: (index(i), 0))


def _dilated_shape(S, w, dil, dtype):
    return jax.ShapeDtypeStruct((S // dil, dil * w), dtype)


def _head_masks(shape):
    lane = lax.broadcasted_iota(jnp.int32, shape, 1)
    return [(lane >= h * HEAD_DIM) & (lane < (h + 1) * HEAD_DIM) for h in range(N_HEADS)]


def _colsum(v):
    return jnp.sum(v, axis=0, keepdims=True)


def _assemble_columns(blocks_ref, full_ref):
    c = blocks_ref.shape[2]
    for j in range(blocks_ref.shape[0]):
        full_ref[:, j * c:(j + 1) * c] = blocks_ref[j]


def _conv_a(z_of, halo_of, w_ref):
    p = z_of(2) * z_of(0)
    p_h = halo_of(2) * halo_of(0)
    cv = w_ref[2:3, :] * p + w_ref[1:2, :] * _shift_down(p, p_h, 1) + w_ref[0:1, :] * _shift_down(p, p_h, 2)
    return p, p_h, cv


def _lru_gates(z_of, halo_of, wr_ref, vec_ref, wa_ref, wx_ref):
    rx = z_of(4)
    rx_h = halo_of(4)
    sh = [rx, _shift_down(rx, rx_h, 1), _shift_down(rx, rx_h, 2), _shift_down(rx, rx_h, 3)]
    xc = (wr_ref[3:4, :] * sh[0] + wr_ref[2:3, :] * sh[1] + wr_ref[1:2, :] * sh[2]
          + wr_ref[0:1, :] * sh[3] + vec_ref[0:1, :])
    ga = _sigmoid(jnp.dot(xc.astype(MXU_DTYPE), wa_ref[...], preferred_element_type=F32) + vec_ref[1:2, :])
    gi = _sigmoid(jnp.dot(xc.astype(MXU_DTYPE), wx_ref[...], preferred_element_type=F32) + vec_ref[2:3, :])
    sp = _softplus(-vec_ref[3:4, :])
    log_a = (-RG_C * ga) * sp
    a = jnp.exp(log_a)
    mult = jnp.sqrt(-_expm1_nonpos(2.0 * log_a))
    return xc, sh, ga, gi, a, mult, sp


def _gmlp_fwd(z_of, vec_ref, ws_ref, bs_ref, tm):
    u = _gelu(z_of(6))
    gv = _gelu(z_of(7))
    rr = lax.rsqrt(jnp.mean(gv * gv, axis=-1, keepdims=True) + NORM_EPS)
    vn = (gv * rr) * vec_ref[4:5, :]
    masks = _head_masks((GMLP_CHUNK, GROUP_W))
    parts = []
    for c in range(tm // GMLP_CHUNK):
        vc = vn[c * GMLP_CHUNK:(c + 1) * GMLP_CHUNK].astype(MXU_DTYPE)
        acc = bs_ref[...]
        for h in range(N_HEADS):
            acc = acc + jnp.where(masks[h], jnp.dot(ws_ref[h], vc, preferred_element_type=F32), 0.0)
        parts.append(acc)
    return u, gv, rr, vn, jnp.concatenate(parts, axis=0)


def _mix_specs(tm, S, order):
    const2 = lambda shape: pl.BlockSpec(shape, lambda i: (0, 0))
    return [const2((SUBLANES, GROUP_W)), const2((SUBLANES, GROUP_W)), const2((SUBLANES, GROUP_W)),
            const2((GROUP_W, GROUP_W)), const2((GROUP_W, GROUP_W)),
            pl.BlockSpec((N_HEADS, GMLP_CHUNK, GMLP_CHUNK), lambda i: (0, 0, 0)),
            const2((GMLP_CHUNK, GROUP_W))]


def _inproj_mix_fwd(x, g, w8, mp, name):
    S, D = x.shape
    N = w8.shape[0] * w8.shape[2]
    tm = TM_MIX
    hb = tm // SUBLANES
    n_abc = N_ABC * GROUP_W
    n_qkv = 3 * GROUP_W

    def body(x_ref, g_ref, w8_ref, wA_ref, wR_ref, vec_ref, wa_ref, wx_ref, ws_ref, bs_ref,
             z_ref, zg_ref, q1_ref, q4_ref, q16_ref, y_ref, h_ref,
             qkv_ref, w_ref, halo_ref, carry_ref, sa_ref, sb_ref, sc_ref):
        @pl.when(pl.program_id(0) == 0)
        def _():
            _assemble_columns(w8_ref, w_ref)
            halo_ref[...] = jnp.zeros_like(halo_ref)
            carry_ref[...] = jnp.zeros_like(carry_ref)

        xv = x_ref[...]
        r = lax.rsqrt(jnp.mean(xv * xv, axis=-1, keepdims=True) + NORM_EPS)
        hn = ((xv * r) * g_ref[...]).astype(MXU_DTYPE)
        z_ref[...] = jnp.dot(hn, w_ref[:, 0:n_abc], preferred_element_type=F32)
        _put(qkv_ref, jnp.dot(hn, w_ref[:, n_abc:n_abc + n_qkv], preferred_element_type=F32))
        zg_ref[...] = jnp.dot(hn, w_ref[:, n_abc + n_qkv:], preferred_element_type=F32)
        for dil, ref in zip(ATTN_DILATIONS, (q1_ref, q4_ref, q16_ref)):
            _deinterleave(qkv_ref, ref, dil)

        z_of = lambda c: z_ref[:, c * GROUP_W:(c + 1) * GROUP_W]
        halo_of = lambda c: halo_ref[:, c * GROUP_W:(c + 1) * GROUP_W]

        _, _, cv = _conv_a(z_of, halo_of, wA_ref)
        y_ref[:, 0:GROUP_W] = (z_of(1) * cv * _silu_and_grad(z_of(3))[0]).astype(y_ref.dtype)

        xc, _, _, gi, a, mult, _ = _lru_gates(z_of, halo_of, wR_ref, vec_ref, wa_ref, wx_ref)
        b = mult * (gi * xc)
        h, h_end = _scan_fwd_tile(a, b, carry_ref[SUBLANES - 1:SUBLANES, :], sa_ref, sb_ref, sc_ref)
        h_ref[...] = h
        carry_ref[...] = h_end[hb - SUBLANES:hb]
        y_ref[:, GROUP_W:2 * GROUP_W] = (h * _silu_and_grad(z_of(5))[0]).astype(y_ref.dtype)

        u, _, _, _, sp = _gmlp_fwd(z_of, vec_ref, ws_ref, bs_ref, tm)
        y_ref[:, 2 * GROUP_W:3 * GROUP_W] = (u * sp * _silu_and_grad(z_of(8))[0]).astype(y_ref.dtype)
        halo_ref[...] = z_ref[tm - SUBLANES:tm, :]

    row = lambda wd: pl.BlockSpec((tm, wd), lambda i: (i, 0))
    return pl.pallas_call(
        body, name=name, grid=(S // tm,),
        in_specs=[row(D), pl.BlockSpec((1, D), lambda i: (0, 0)),
                  pl.BlockSpec(w8.shape, lambda i: (0, 0, 0), pipeline_mode=pl.Buffered(1))] + _mix_specs(tm, S, "fwd"),
        out_specs=[row(n_abc), row(GROUP_W)] + [_dilated_spec(tm, n_qkv, dil) for dil in ATTN_DILATIONS]
                  + [row(3 * GROUP_W), row(GROUP_W)],
        out_shape=[jax.ShapeDtypeStruct((S, n_abc), F32), jax.ShapeDtypeStruct((S, GROUP_W), F32)]
                  + [_dilated_shape(S, n_qkv, dil, MXU_DTYPE) for dil in ATTN_DILATIONS]
                  + [jax.ShapeDtypeStruct((S, 3 * GROUP_W), MXU_DTYPE), jax.ShapeDtypeStruct((S, GROUP_W), F32)],
        scratch_shapes=[_lane_scratch(tm, n_qkv), pltpu.VMEM((D, N), w8.dtype), pltpu.VMEM((SUBLANES, n_abc), F32),
                        pltpu.VMEM((SUBLANES, GROUP_W), F32), _lane_scratch(tm, GROUP_W), _lane_scratch(tm, GROUP_W),
                        pltpu.VMEM((hb, GROUP_W), F32)],
        compiler_params=_params(("arbitrary",)),
    )(x, g, w8, mp["wA"], mp["wR"], mp["vec"], mp["wa"], mp["wx"], mp["ws"], mp["bs"])


_NEG = -1e30


def _slope(h):
    return 2.0 ** (-8.0 * (h + 1) / N_HEADS)


def _attn_bias(dil, offsets, n_keys):
    shape = (ATTN_BLOCK, n_keys)
    qi = lax.broadcasted_iota(jnp.int32, shape, 0)
    ki = lax.broadcasted_iota(jnp.int32, shape, 1)
    blocks = []
    for f in offsets:
        delta = qi + f - ki
        valid = (delta >= 0) & (delta <= ATTN_BLOCK)
        dist = (delta * dil).astype(F32)
        for h in range(N_HEADS):
            blocks.append(jnp.where(valid, -_slope(h) * dist, _NEG))
    return jnp.concatenate(blocks, axis=0)


def _stack_heads(t, masks):
    return jnp.concatenate([jnp.where(m, t, jnp.zeros_like(t)) for m in masks], axis=0)


def _unstack_heads(t4, masks, base=0):
    out = t4[base * ATTN_BLOCK:(base + 1) * ATTN_BLOCK]
    for h in range(1, N_HEADS):
        out = jnp.where(masks[h], t4[(base + h) * ATTN_BLOCK:(base + h + 1) * ATTN_BLOCK], out)
    return out


def _attn_fwd(qkv, dil, name):
    rows = qkv.shape[0]
    nb = rows // ATTN_BLOCK
    scale = 1.0 / math.sqrt(HEAD_DIM)
    B = ATTN_BLOCK
    per_step = ATTN_BLOCKS_PER_STEP

    def body(q_ref, kc_ref, kp_ref, vc_ref, vp_ref, o_ref, l_ref, bias_ref):
        n = pl.program_id(1)

        @pl.when(n == 0)
        def _():
            bias_ref[...] = _attn_bias(dil, (B,), 2 * B)

        masks = _head_masks((B, GROUP_W))
        for j in range(per_step):
            own = slice(j * B, (j + 1) * B)
            before = slice((j - 1) * B, j * B)
            qs = _stack_heads(q_ref[own], masks)
            keys = jnp.concatenate([kp_ref[...] if j == 0 else kc_ref[before], kc_ref[own]], axis=0)
            vals = jnp.concatenate([vp_ref[...] if j == 0 else vc_ref[before], vc_ref[own]], axis=0)
            s = _mm_nt(qs, keys) * scale + bias_ref[...]
            if j == 0:
                key_col = lax.broadcasted_iota(jnp.int32, s.shape, 1)
                s = jnp.where((n == 0) & (key_col < B), _NEG, s)
            m = jnp.max(s, axis=-1, keepdims=True)
            p = jnp.exp(s - m)
            l = jnp.sum(p, axis=-1, keepdims=True)
            o4 = jnp.dot(p.astype(MXU_DTYPE), vals, preferred_element_type=F32)
            o_ref[own] = _unstack_heads(o4, masks) / _unstack_heads(jnp.broadcast_to(l, o4.shape), masks)
            l_ref[own] = _unstack_heads(jnp.broadcast_to(m + jnp.log(l), o4.shape), masks)

    blk = (per_step * B, GROUP_W)
    cur = lambda c: pl.BlockSpec(blk, lambda r, n: (n, r * 3 + c))
    prev = lambda c: pl.BlockSpec((B, GROUP_W), lambda r, n: (jnp.maximum(n * per_step - 1, 0), r * 3 + c))
    out = pl.BlockSpec(blk, lambda r, n: (n, r))
    return pl.pallas_call(
        body, name=name, grid=(dil, nb // per_step),
        in_specs=[cur(0), cur(1), prev(1), cur(2), prev(2)],
        out_specs=[out, out],
        out_shape=[jax.ShapeDtypeStruct((rows, dil * GROUP_W), F32)] * 2,
        scratch_shapes=[pltpu.VMEM((N_HEADS * ATTN_BLOCK, 2 * ATTN_BLOCK), F32)],
        compiler_params=_params(("parallel", "arbitrary")),
    )(qkv, qkv, qkv, qkv, qkv)


def _outproj(x, z_g, y_abc, attn, w_out, name):
    S, D = x.shape
    tm = TM_MM
    n_abc = 3 * GROUP_W

    def body(x_ref, g_ref, yabc_ref, o1, l1, o2, l2, o3, l3, w_ref,
             xn_ref, y_ref, o_ref, lse1_ref, lse4_ref, lse16_ref, so2, sl2, so3, sl3, slse):
        for src, dst, dil in ((o2, so2, ATTN_DILATIONS[1]), (l2, sl2, ATTN_DILATIONS[1]),
                              (o3, so3, ATTN_DILATIONS[2]), (l3, sl3, ATTN_DILATIONS[2])):
            _interleave(src, dst, dil)
        la, lb, lc = l1[...], _get(sl2), _get(sl3)
        mx = jnp.maximum(jnp.maximum(la, lb), lc)
        ea, eb, ec = jnp.exp(la - mx), jnp.exp(lb - mx), jnp.exp(lc - mx)
        den = ea + eb + ec
        o = (ea * o1[...] + eb * _get(so2) + ec * _get(so3)) / den
        o_ref[...] = o
        _put(slse, mx + jnp.log(den))
        for dil, ref in zip(ATTN_DILATIONS, (lse1_ref, lse4_ref, lse16_ref)):
            _deinterleave(slse, ref, dil)
        y_d = o * _silu_and_grad(g_ref[...])[0]
        y_ref[:, 0:n_abc] = yabc_ref[...].astype(MXU_DTYPE)
        y_ref[:, n_abc:] = y_d.astype(MXU_DTYPE)
        xn_ref[...] = x_ref[...] + jnp.dot(y_ref[...], w_ref[...], preferred_element_type=F32)

    row = lambda w: pl.BlockSpec((tm, w), lambda i: (i, 0))
    dil_specs = [_dilated_spec(tm, GROUP_W, dil) for dil in ATTN_DILATIONS]
    (o1, l1), (o2, l2), (o3, l3) = attn
    return pl.pallas_call(
        body, name=name, grid=(S // tm,),
        in_specs=[row(D), row(GROUP_W), row(n_abc)] + [sp for sp in dil_specs for _ in range(2)]
                 + [pl.BlockSpec(w_out.shape, lambda i: (0, 0))],
        out_specs=[row(D), row(4 * GROUP_W), row(GROUP_W)] + dil_specs,
        out_shape=[jax.ShapeDtypeStruct((S, D), F32), jax.ShapeDtypeStruct((S, 4 * GROUP_W), MXU_DTYPE),
                   jax.ShapeDtypeStruct((S, GROUP_W), F32)]
                  + [_dilated_shape(S, GROUP_W, dil, F32) for dil in ATTN_DILATIONS],
        scratch_shapes=[_lane_scratch(tm, GROUP_W)] * 5,
        compiler_params=_params(("parallel",)),
    )(x, z_g, y_abc, o1, l1, o2, l2, o3, l3, w_out)


def _loss_head(x, g, target, name):
    S, D = x.shape
    tm = TM_MM

    def body(x_ref, g_ref, t_ref, dx_ref, loss_ref, dg_ref):
        i = pl.program_id(0)

        @pl.when(i == 0)
        def _():
            loss_ref[...] = jnp.zeros_like(loss_ref)
            dg_ref[...] = jnp.zeros_like(dg_ref)

        xv = x_ref[...]
        r = lax.rsqrt(jnp.mean(xv * xv, axis=-1, keepdims=True) + NORM_EPS)
        xn = xv * r
        err = xn * g_ref[...] - t_ref[...]
        per_tok = jnp.mean(err * err, axis=-1, keepdims=True)
        loss_ref[...] += 0.5 * jnp.sum(per_tok, axis=0, keepdims=True)
        dout = err * (1.0 / D)
        dg_ref[...] += _colsum(dout * xn)
        dxn = dout * g_ref[...]
        dx_ref[...] = r * (dxn - xn * jnp.mean(dxn * xn, axis=-1, keepdims=True))

    row = pl.BlockSpec((tm, D), lambda i: (i, 0))
    return pl.pallas_call(
        body, name=name, grid=(S // tm,),
        in_specs=[row, pl.BlockSpec((1, D), lambda i: (0, 0)), row],
        out_specs=[row, pl.BlockSpec((1, LANES), lambda i: (0, 0)), pl.BlockSpec((1, D), lambda i: (0, 0))],
        out_shape=[jax.ShapeDtypeStruct((S, D), F32), jax.ShapeDtypeStruct((1, LANES), F32),
                   jax.ShapeDtypeStruct((1, D), F32)],
        compiler_params=_params(("arbitrary",)),
    )(x, g, target)


def _outproj_mix_bwd(dx, y, w_out, z, z_g, hs, o, mp, name):
    S, D = dx.shape
    E = y.shape[1]
    tm = TM_MIX
    hb = tm // SUBLANES
    nT = S // tm
    last_blk = S // SUBLANES - 1
    wcols = N_ABC * GROUP_W

    def body(dx_ref, y_ref, w_ref, z_ref, zh_ref, zn_ref, zg_ref, h_ref, hh_ref, o_ref,
             wA_ref, wR_ref, vec_ref, wa_ref, wx_ref, ws_ref, bs_ref,
             dw_ref, dz_ref, dzg_ref, do1_ref, do4_ref, do16_ref, dl1_ref, dl4_ref, dl16_ref,
             dwA_ref, dwR_ref, dvec_ref, dwa_ref, dwx_ref, dws_ref, dbs_ref,
             hcarry_ref, xcarry_ref, bsacc_ref, do_ref, dl_ref, sa_ref, sb_ref, sc_ref, dy_ref, dyn_ref, acc_ref):
        i = pl.program_id(0)
        ti = nT - 1 - i

        @pl.when(i == 0)
        def _():
            acc_ref[...] = jnp.zeros_like(acc_ref)
            dyn_ref[...] = jnp.zeros_like(dyn_ref)
            hcarry_ref[...] = jnp.zeros_like(hcarry_ref)
            xcarry_ref[...] = jnp.zeros_like(xcarry_ref)
            bsacc_ref[...] = jnp.zeros_like(bsacc_ref)
            dwA_ref[...] = jnp.zeros_like(dwA_ref)
            dwR_ref[...] = jnp.zeros_like(dwR_ref)
            dvec_ref[...] = jnp.zeros_like(dvec_ref)
            dwa_ref[...] = jnp.zeros_like(dwa_ref)
            dwx_ref[...] = jnp.zeros_like(dwx_ref)
            dws_ref[...] = jnp.zeros_like(dws_ref)
            dbs_ref[...] = jnp.zeros_like(dbs_ref)

        dxb = dx_ref[...].astype(MXU_DTYPE)
        dy_ref[...] = _mm_nt(dxb, w_ref[...])
        acc_ref[...] += _mm_tn(y_ref[...], dxb)

        @pl.when(i == nT - 1)
        def _():
            dw_ref[...] = acc_ref[...].astype(dw_ref.dtype)

        has_prev = ti > 0
        has_next = i > 0
        col = lambda c: slice(c * GROUP_W, (c + 1) * GROUP_W)
        z_of = lambda c: z_ref[:, col(c)]
        halo_of = lambda c: jnp.where(has_prev, zh_ref[:, col(c)], 0.0)
        next_of = lambda c: zn_ref[:, col(c)]

        p, p_h, cv = _conv_a(z_of, halo_of, wA_ref)
        sg, dsg = _silu_and_grad(z_of(3))
        a_b = z_of(1)
        dya = dy_ref[:, col(0)]
        dcv = dya * a_b * sg
        dcv_n = jnp.where(has_next, dyn_ref[...] * next_of(1) * _silu_and_grad(next_of(3))[0], 0.0)
        dp = (wA_ref[2:3, :] * dcv + wA_ref[1:2, :] * _shift_up(dcv, dcv_n, 1)
              + wA_ref[0:1, :] * _shift_up(dcv, dcv_n, 2))
        dwA_ref[2:3, :] += _colsum(dcv * p)
        dwA_ref[1:2, :] += _colsum(dcv * _shift_down(p, p_h, 1))
        dwA_ref[0:1, :] += _colsum(dcv * _shift_down(p, p_h, 2))
        def put_dz(c, val):
            dz_ref[:, col(c)] = val.astype(dz_ref.dtype)

        put_dz(0, dp * z_of(2))
        put_dz(1, dya * cv * sg)
        put_dz(2, dp * z_of(0))
        put_dz(3, dya * a_b * cv * dsg)

        xc, sh, ga, gi, a, mult, sp = _lru_gates(z_of, halo_of, wR_ref, vec_ref, wa_ref, wx_ref)
        h = h_ref[...]
        h_prev = _shift_down(h, jnp.where(has_prev, hh_ref[...], 0.0), 1)
        sgr, dsgr = _silu_and_grad(z_of(5))
        dyb = dy_ref[:, col(1)]
        put_dz(5, dyb * h * dsgr)
        row = lax.broadcasted_iota(jnp.int32, (tm, GROUP_W), 0)
        g_in = dyb * sgr + jnp.where(row == tm - 1, hcarry_ref[0:1, :], 0.0)
        a_up = _shift_up(a, jnp.zeros((SUBLANES, GROUP_W), F32), 1)
        dH = _scan_rev_tile(a_up, g_in, sa_ref, sb_ref, sc_ref)
        hcarry_ref[...] = (a * dH)[0:SUBLANES]
        da = dH * h_prev
        gx = gi * xc
        dmult = dH * gx
        dgi = dH * mult * xc
        dxc = dH * mult * gi
        dlog_a = da * a - dmult * (a * a) / mult
        dga = dlog_a * (-RG_C * sp)
        dlam_row = _colsum(dlog_a * (-RG_C * ga)) * (-_sigmoid(-vec_ref[3:4, :]))
        dpre_a = dga * ga * (1.0 - ga)
        dpre_i = dgi * gi * (1.0 - gi)
        dwa_ref[...] += _mm_tn(xc, dpre_a)
        dwx_ref[...] += _mm_tn(xc, dpre_i)
        dxc = dxc + _mm_nt(dpre_a, wa_ref[...]) + _mm_nt(dpre_i, wx_ref[...])
        dvec_ref[0:1, :] += _colsum(dxc)
        dvec_ref[1:2, :] += _colsum(dpre_a)
        dvec_ref[2:3, :] += _colsum(dpre_i)
        dvec_ref[3:4, :] += dlam_row
        for k in range(4):
            dwR_ref[k:k + 1, :] += _colsum(dxc * sh[3 - k])
        dxc_n = xcarry_ref[...]
        put_dz(4, wR_ref[3:4, :] * dxc + wR_ref[2:3, :] * _shift_up(dxc, dxc_n, 1)
               + wR_ref[1:2, :] * _shift_up(dxc, dxc_n, 2) + wR_ref[0:1, :] * _shift_up(dxc, dxc_n, 3))
        xcarry_ref[...] = dxc[0:SUBLANES]

        c_u, c_v = z_of(6), z_of(7)
        u, du_dx = _gelu_and_grad(c_u)
        gv, dgv_dx = _gelu_and_grad(c_v)
        rr = lax.rsqrt(jnp.mean(gv * gv, axis=-1, keepdims=True) + NORM_EPS)
        xhat = gv * rr
        g_c = vec_ref[4:5, :]
        vn = xhat * g_c
        masks = _head_masks((GMLP_CHUNK, GROUP_W))
        tri_r = lax.broadcasted_iota(jnp.int32, (GMLP_CHUNK, GMLP_CHUNK), 0)
        tri_c = lax.broadcasted_iota(jnp.int32, (GMLP_CHUNK, GMLP_CHUNK), 1)
        tril = tri_r >= tri_c
        sgc, dsgc = _silu_and_grad(z_of(8))
        dyc = dy_ref[:, col(2)]
        dsp_full = dyc * u * sgc
        sp_parts, dvn_parts = [], []
        for c in range(tm // GMLP_CHUNK):
            rs = slice(c * GMLP_CHUNK, (c + 1) * GMLP_CHUNK)
            vc = vn[rs].astype(MXU_DTYPE)
            dsp_c = dsp_full[rs]
            bsacc_ref[...] += dsp_c
            acc = bs_ref[...]
            dvn_c = jnp.zeros((GMLP_CHUNK, GROUP_W), F32)
            for h in range(N_HEADS):
                w_h = ws_ref[h]
                acc = acc + jnp.where(masks[h], jnp.dot(w_h, vc, preferred_element_type=F32), 0.0)
                dsp_h = jnp.where(masks[h], dsp_c, 0.0).astype(MXU_DTYPE)
                dvn_c = dvn_c + _mm_tn(w_h, dsp_h)
                dws_ref[h] += jnp.where(tril, _mm_nt(dsp_h, vc), 0.0)
            sp_parts.append(acc)
            dvn_parts.append(dvn_c)
        spv = jnp.concatenate(sp_parts, axis=0)
        dvn = jnp.concatenate(dvn_parts, axis=0)
        put_dz(6, dyc * spv * sgc * du_dx)
        put_dz(8, dyc * u * spv * dsgc)
        dvec_ref[4:5, :] += _colsum(dvn * xhat)
        dgvn = dvn * g_c
        dgv = rr * (dgvn - xhat * jnp.mean(dgvn * xhat, axis=-1, keepdims=True))
        put_dz(7, dgv * dgv_dx)

        sgd, dsgd = _silu_and_grad(zg_ref[...])
        dyd = dy_ref[:, col(3)]
        ov = o_ref[...]
        do = dyd * sgd
        _put(do_ref, do)
        dzg_ref[...] = (dyd * ov * dsgd).astype(dzg_ref.dtype)
        prod = do * ov
        tmasks = _head_masks((tm, GROUP_W))
        dl = jnp.zeros((tm, GROUP_W), F32)
        for h in range(N_HEADS):
            dl = jnp.where(tmasks[h], jnp.sum(jnp.where(tmasks[h], prod, 0.0), axis=-1, keepdims=True), dl)
        _put(dl_ref, dl)
        for dil, d_out, l_out in zip(ATTN_DILATIONS, (do1_ref, do4_ref, do16_ref), (dl1_ref, dl4_ref, dl16_ref)):
            _deinterleave(do_ref, d_out, dil)
            _deinterleave(dl_ref, l_out, dil)

        @pl.when(i == nT - 1)
        def _():
            acc = bsacc_ref[...]
            lane = lax.broadcasted_iota(jnp.int32, (GMLP_CHUNK, LANES), 1)
            out = jnp.zeros((GMLP_CHUNK, LANES), F32)
            for h in range(N_HEADS):
                out = jnp.where(lane == h, jnp.sum(jnp.where(masks[h], acc, 0.0), axis=-1, keepdims=True), out)
            dbs_ref[...] = out

        dyn_ref[...] = dy_ref[0:SUBLANES, 0:GROUP_W]

    rev = lambda w: pl.BlockSpec((tm, w), lambda i: (nT - 1 - i, 0))
    prev8 = lambda w: pl.BlockSpec((SUBLANES, w), lambda i: (jnp.maximum((nT - 1 - i) * hb - 1, 0), 0))
    next8 = lambda w: pl.BlockSpec((SUBLANES, w), lambda i: (jnp.minimum((nT - i) * hb, last_blk), 0))
    const2 = lambda shape: pl.BlockSpec(shape, lambda i: (0, 0))
    dil_specs = [_dilated_spec(tm, GROUP_W, dil, lambda i: nT - 1 - i) for dil in ATTN_DILATIONS]
    dil_shapes = [_dilated_shape(S, GROUP_W, dil, F32) for dil in ATTN_DILATIONS]
    small = (SUBLANES, GROUP_W)
    sq = (GROUP_W, GROUP_W)
    ws_shape = (N_HEADS, GMLP_CHUNK, GMLP_CHUNK)
    return pl.pallas_call(
        body, name=name, grid=(nT,),
        in_specs=[rev(D), rev(E), pl.BlockSpec((E, D), lambda i: (0, 0), pipeline_mode=pl.Buffered(1)),
                  rev(wcols), prev8(wcols), next8(wcols), rev(GROUP_W), rev(GROUP_W), prev8(GROUP_W), rev(GROUP_W)]
                 + _mix_specs(tm, S, "bwd"),
        out_specs=[const2((E, D)), rev(wcols), rev(GROUP_W)] + dil_specs + dil_specs
                  + [const2(small), const2(small), const2(small), const2(sq), const2(sq),
                     pl.BlockSpec(ws_shape, lambda i: (0, 0, 0)), const2((GMLP_CHUNK, LANES))],
        out_shape=[jax.ShapeDtypeStruct((E, D), WIRE_DTYPE),
                   jax.ShapeDtypeStruct((S, wcols), MXU_DTYPE), jax.ShapeDtypeStruct((S, GROUP_W), MXU_DTYPE)]
                  + [_dilated_shape(S, GROUP_W, dil, MXU_DTYPE) for dil in ATTN_DILATIONS] + dil_shapes
                  + [jax.ShapeDtypeStruct(small, F32)] * 3 + [jax.ShapeDtypeStruct(sq, F32)] * 2
                  + [jax.ShapeDtypeStruct(ws_shape, F32), jax.ShapeDtypeStruct((GMLP_CHUNK, LANES), F32)],
        scratch_shapes=[pltpu.VMEM(small, F32), pltpu.VMEM(small, F32), pltpu.VMEM((GMLP_CHUNK, GROUP_W), F32),
                        _lane_scratch(tm, GROUP_W), _lane_scratch(tm, GROUP_W),
                        _lane_scratch(tm, GROUP_W), _lane_scratch(tm, GROUP_W), pltpu.VMEM((hb, GROUP_W), F32),
                        pltpu.VMEM((tm, E), F32), pltpu.VMEM(small, F32), pltpu.VMEM((E, D), F32)],
        compiler_params=_params(("arbitrary",)),
    )(dx, y, w_out, z, z, z, z_g, hs, hs, o, mp["wA"], mp["wR"], mp["vec"], mp["wa"], mp["wx"], mp["ws"], mp["bs"])


def _attn_bwd(qkv, do, lse, delta, dil, name):
    rows = qkv.shape[0]
    nb = rows // ATTN_BLOCK
    scale = 1.0 / math.sqrt(HEAD_DIM)
    B = ATTN_BLOCK
    per_step = ATTN_BLOCKS_PER_STEP
    n_steps = nb // per_step

    def body(qc_ref, qn_ref, kc_ref, kp_ref, vc_ref, vp_ref, doc_ref, don_ref, lc_ref, ln_ref, dc_ref, dn_ref,
             dq_ref, dk_ref, dv_ref, bias_ref, bias_next_ref):
        n = pl.program_id(1)

        @pl.when(n == 0)
        def _():
            bias_ref[...] = _attn_bias(dil, (B,), 2 * B)
            bias_next_ref[...] = _attn_bias(dil, (B,), B)

        masks = _head_masks((B, GROUP_W))

        def per_row(tile):
            return jnp.concatenate([jnp.max(jnp.where(masks[h], tile, _NEG), axis=-1, keepdims=True)
                                    for h in range(N_HEADS)], axis=0)

        def grads(q, dov, lse_tile, dl_tile, keys, vals, bias, dead):
            qs = _stack_heads(q, masks)
            dos = _stack_heads(dov.astype(MXU_DTYPE), masks)
            s = _mm_nt(qs, keys) * scale + bias
            if dead is not None:
                s = jnp.where(dead(s.shape), _NEG, s)
            p = jnp.exp(s - per_row(lse_tile))
            ds = (p * (_mm_nt(dos, vals) - per_row(dl_tile)) * scale).astype(MXU_DTYPE)
            return ds, _mm_tn(ds, qs), _mm_tn(p.astype(MXU_DTYPE), dos)

        for j in range(per_step):
            own = slice(j * B, (j + 1) * B)
            before = slice((j - 1) * B, j * B)
            keys = jnp.concatenate([kp_ref[...] if j == 0 else kc_ref[before], kc_ref[own]], axis=0)
            vals = jnp.concatenate([vp_ref[...] if j == 0 else vc_ref[before], vc_ref[own]], axis=0)
            dead = (lambda shape: (n == 0) & (lax.broadcasted_iota(jnp.int32, shape, 1) < B)) if j == 0 else None
            ds, dk2, dv2 = grads(qc_ref[own], doc_ref[own], lc_ref[own], dc_ref[own], keys, vals, bias_ref[...], dead)
            dq_ref[own] = _unstack_heads(jnp.dot(ds, keys, preferred_element_type=F32), masks).astype(dq_ref.dtype)
            if j > 0:
                dk_ref[before] = (dk_own + dk2[:B]).astype(dk_ref.dtype)
                dv_ref[before] = (dv_own + dv2[:B]).astype(dv_ref.dtype)
            dk_own, dv_own = dk2[B:], dv2[B:]
        last = slice((per_step - 1) * B, per_step * B)
        _, dk1, dv1 = grads(qn_ref[...], don_ref[...], ln_ref[...], dn_ref[...], kc_ref[last], vc_ref[last],
                            bias_next_ref[...], lambda shape: n == n_steps - 1)
        dk_ref[last] = (dk_own + dk1).astype(dk_ref.dtype)
        dv_ref[last] = (dv_own + dv1).astype(dv_ref.dtype)

    blk = (per_step * B, GROUP_W)
    one = (B, GROUP_W)
    nxt_idx = lambda n: jnp.minimum((n + 1) * per_step, nb - 1)
    prv_idx = lambda n: jnp.maximum(n * per_step - 1, 0)
    zcur = lambda c: pl.BlockSpec(blk, lambda r, n: (n, r * 3 + c))
    znext = lambda c: pl.BlockSpec(one, lambda r, n: (nxt_idx(n), r * 3 + c))
    zprev = lambda c: pl.BlockSpec(one, lambda r, n: (prv_idx(n), r * 3 + c))
    cur = pl.BlockSpec(blk, lambda r, n: (n, r))
    nxt = pl.BlockSpec(one, lambda r, n: (nxt_idx(n), r))
    return pl.pallas_call(
        body, name=name, grid=(dil, n_steps),
        in_specs=[zcur(0), znext(0), zcur(1), zprev(1), zcur(2), zprev(2), cur, nxt, cur, nxt, cur, nxt],
        out_specs=[cur, cur, cur],
        out_shape=[jax.ShapeDtypeStruct((rows, dil * GROUP_W), WIRE_DTYPE)] * 3,
        scratch_shapes=[pltpu.VMEM((N_HEADS * B, 2 * B), F32), pltpu.VMEM((N_HEADS * B, B), F32)],
        compiler_params=_params(("parallel", "arbitrary")),
    )(qkv, qkv, qkv, qkv, qkv, qkv, do, do, lse, lse, delta, delta)


def _inproj_bwd(x, g, dxn, dz_abc, dqkv, dz_g, w8, name):
    S, D = x.shape
    N = w8.shape[0] * w8.shape[2]
    tm = TM_MM
    n_abc = N_ABC * GROUP_W

    def body(x_ref, g_ref, dxn_ref, dabc_ref, q1, k1, v1, q2, k2, v2, q3, k3, v3, dg_ref, w8_ref,
             dx_ref, dz_ref, h_ref, dgn_ref, s4_ref, s16_ref, w_ref):
        i = pl.program_id(0)

        @pl.when(i == 0)
        def _():
            dgn_ref[...] = jnp.zeros_like(dgn_ref)
            _assemble_columns(w8_ref, w_ref)

        dz_ref[:, 0:n_abc] = dabc_ref[...].astype(MXU_DTYPE)
        for j, parts in enumerate(((q1, q2, q3), (k1, k2, k3), (v1, v2, v3))):
            c0 = n_abc + j * GROUP_W
            _interleave(parts[1], s4_ref, ATTN_DILATIONS[1])
            _interleave(parts[2], s16_ref, ATTN_DILATIONS[2])
            dz_ref[:, c0:c0 + GROUP_W] = (parts[0][...] + _get(s4_ref) + _get(s16_ref)).astype(MXU_DTYPE)
        dz_ref[:, n_abc + 3 * GROUP_W:] = dg_ref[...].astype(MXU_DTYPE)
        dh = _mm_nt(dz_ref[...], w_ref[...])
        xv = x_ref[...]
        r = lax.rsqrt(jnp.mean(xv * xv, axis=-1, keepdims=True) + NORM_EPS)
        xn = xv * r
        gv = g_ref[...]
        h_ref[...] = (xn * gv).astype(MXU_DTYPE)
        dgn_ref[...] += _colsum(dh * xn)
        dn = dh * gv
        dx_ref[...] = dxn_ref[...] + r * (dn - xn * jnp.mean(dn * xn, axis=-1, keepdims=True))

    row = lambda w: pl.BlockSpec((tm, w), lambda i: (i, 0))
    flat = [t for p in dqkv for t in p]
    dil_specs = [_dilated_spec(tm, GROUP_W, dil) for dil in ATTN_DILATIONS for _ in range(3)]
    return pl.pallas_call(
        body, name=name, grid=(S // tm,),
        in_specs=[row(D), pl.BlockSpec((1, D), lambda i: (0, 0)), row(D), row(n_abc)] + dil_specs
                 + [row(GROUP_W), pl.BlockSpec(w8.shape, lambda i: (0, 0, 0), pipeline_mode=pl.Buffered(1))],
        out_specs=[row(D), row(N), row(D), pl.BlockSpec((1, D), lambda i: (0, 0))],
        out_shape=[jax.ShapeDtypeStruct((S, D), F32), jax.ShapeDtypeStruct((S, N), MXU_DTYPE),
                   jax.ShapeDtypeStruct((S, D), MXU_DTYPE), jax.ShapeDtypeStruct((1, D), F32)],
        scratch_shapes=[_lane_scratch(tm, GROUP_W)] * 2 + [pltpu.VMEM((D, N), w8.dtype)],
        compiler_params=_params(("arbitrary",)),
    )(x, g, dxn, dz_abc, *flat, dz_g, w8)


def _inproj_wgrad(h, dz, name):
    S, D = h.shape
    N = dz.shape[1]
    tm = TM_WGRAD
    nj = 2
    cw = N // nj
    per = N_DEV // nj
    n_loc = N // N_DEV

    def body(h_ref, dz_ref, dw_ref, acc_ref):
        i = pl.program_id(1)

        @pl.when(i == 0)
        def _():
            acc_ref[...] = jnp.zeros_like(acc_ref)

        acc_ref[...] += _mm_tn(h_ref[...], dz_ref[...])

        @pl.when(i == S // tm - 1)
        def _():
            for b in range(per):
                dw_ref[b] = acc_ref[:, b * n_loc:(b + 1) * n_loc].astype(dw_ref.dtype)

    return pl.pallas_call(
        body, name=name, grid=(nj, S // tm),
        in_specs=[pl.BlockSpec((tm, D), lambda j, i: (i, 0)), pl.BlockSpec((tm, cw), lambda j, i: (i, j))],
        out_specs=pl.BlockSpec((per, D, n_loc), lambda j, i: (j, 0, 0)),
        out_shape=jax.ShapeDtypeStruct((N_DEV, D, n_loc), WIRE_DTYPE),
        scratch_shapes=[pltpu.VMEM((D, cw), F32)],
        compiler_params=_params(("parallel", "arbitrary")),
    )(h, dz)


def _my_place():
    return lax.axis_index("x"), lax.axis_index("y"), lax.axis_index("c")


def _peer(x, y, c, k):
    px = 1 - x if k & 4 else x
    py = 1 - y if k & 2 else y
    pc = 1 - c if k & 1 else c
    return (px, py, pc), 4 * px + 2 * py + pc


HBM_SPEC = pl.BlockSpec(memory_space=pltpu.HBM)
SEM_SPEC = pl.BlockSpec(memory_space=pltpu.SEMAPHORE)
SPLIT_EFFECT = pltpu.SideEffectType.DATAFLOW_SIDE_EFFECTING
N_PEERS = N_DEV - 1


def _exchange_copies(srcs, lands, send_sems, recv_sems, whole, arrival):
    x, y, c = _my_place()
    me = 4 * x + 2 * y + c
    copies = []
    for t in range(len(srcs)):
        for k in range(1, N_DEV):
            peer, pidx = _peer(x, y, c, k)
            copies.append(pltpu.make_async_remote_copy(
                src_ref=srcs[t] if whole[t] else srcs[t].at[pidx],
                dst_ref=lands[t].at[pidx if arrival else me], send_sem=send_sems.at[t * N_PEERS + k - 1],
                recv_sem=recv_sems.at[t * N_PEERS + k - 1], device_id=peer, device_id_type=MESH))
    return copies


def _exchange_start(groups, name, after=None):
    sizes = [len(g) for g in groups]
    whole = [w for g in groups for _, w in g]
    srcs = [pltpu.with_memory_space_constraint(a, pltpu.HBM) for g in groups for a, _ in g]
    lands = [pltpu.with_memory_space_constraint(lax.empty(((N_DEV,) + a.shape) if w else a.shape, a.dtype), pltpu.HBM)
             for a, w in zip(srcs, whole)]
    n = len(srcs)
    n_g = len(groups)
    extra = [] if after is None else [after]
    n_in = 2 * n + len(extra)

    def body(*refs):
        src_refs, land_refs = refs[:n], refs[n:2 * n]
        sem_refs = refs[n_in + 2 * n:n_in + 2 * n + 2 * n_g]
        token = refs[-1]
        off = 0
        for gi, sz in enumerate(sizes):
            for send in _exchange_copies(src_refs[off:off + sz], land_refs[off:off + sz],
                                         sem_refs[2 * gi], sem_refs[2 * gi + 1], whole[off:off + sz], False):
                send.start()
            off += sz
        token[...] = jnp.zeros_like(token)

    sem_shapes = [pltpu.SemaphoreType.DMA((sz * N_PEERS,)) for sz in sizes for _ in range(2)]
    outs = pl.pallas_call(
        body, name=name,
        in_specs=[HBM_SPEC] * (2 * n) + [pl.BlockSpec(memory_space=pl.ANY)] * len(extra),
        out_specs=[HBM_SPEC] * (2 * n) + [SEM_SPEC] * (2 * n_g) + [pl.BlockSpec(memory_space=pltpu.VMEM)],
        out_shape=[pltpu.HBM(a.shape, a.dtype) for a in srcs + lands] + sem_shapes
                  + [jax.ShapeDtypeStruct((SUBLANES, LANES), F32)],
        input_output_aliases={i: i for i in range(2 * n)},
        compiler_params=pltpu.CompilerParams(has_side_effects=SPLIT_EFFECT),
    )(*srcs, *lands, *extra)
    handles, off = [], 0
    for gi, sz in enumerate(sizes):
        handles.append((outs[2 * n + 2 * gi], outs[2 * n + 2 * gi + 1], outs[off:off + sz], outs[n + off:n + off + sz],
                        whole[off:off + sz]))
        off += sz
    return handles, outs[-1]


def _exchange_wait(handle, after, name):
    send_sems, recv_sems, srcs, lands, whole = handle
    n = len(srcs)

    def body(*refs):
        src_refs, land_refs = refs[:n], refs[n:2 * n]
        for send in _exchange_copies(src_refs, land_refs, refs[2 * n], refs[2 * n + 1], whole, False):
            send.wait_send()
        for arrival in _exchange_copies(src_refs, land_refs, refs[2 * n], refs[2 * n + 1], whole, True):
            arrival.wait_recv()

    outs = pl.pallas_call(
        body, name=name,
        in_specs=[HBM_SPEC] * (2 * n) + [SEM_SPEC, SEM_SPEC, pl.BlockSpec(memory_space=pl.ANY)],
        out_specs=[HBM_SPEC] * (2 * n),
        out_shape=[pltpu.HBM(a.shape, a.dtype) for a in list(srcs) + list(lands)],
        input_output_aliases={i: i for i in range(2 * n)},
        compiler_params=pltpu.CompilerParams(has_side_effects=SPLIT_EFFECT),
    )(*srcs, *lands, send_sems, recv_sems, after)
    x, y, c = _my_place()
    me = 4 * x + 2 * y + c
    own = [s[None] if w else lax.dynamic_slice_in_dim(s, me, 1, axis=0) for s, w in zip(outs[:n], whole)]
    return [lax.dynamic_update_slice_in_dim(ld, o, me, axis=0) for ld, o in zip(outs[n:], own)]


def _sum_slots(parts, name):
    n = len(parts)

    def body(*refs):
        for p_ref, o_ref in zip(refs[:n], refs[n:]):
            acc = p_ref[0]
            for j in range(1, N_DEV):
                acc = acc + p_ref[j]
            o_ref[...] = acc

    vm = pl.BlockSpec(memory_space=pltpu.VMEM)
    return pl.pallas_call(
        body, name=name, in_specs=[vm] * n, out_specs=[vm] * n,
        out_shape=[jax.ShapeDtypeStruct(p.shape[1:], F32) for p in parts],
        compiler_params=pltpu.CompilerParams(vmem_limit_bytes=VMEM_LIMIT),
    )(*parts)


def _adamw_math(w, g, m, v):
    m = ADAM_B1 * m + (1.0 - ADAM_B1) * g
    v = ADAM_B2 * v + (1.0 - ADAM_B2) * (g * g)
    m_hat = m / (1.0 - ADAM_B1 ** ADAM_STEP)
    v_hat = v / (1.0 - ADAM_B2 ** ADAM_STEP)
    delta = -ADAM_LR * (m_hat / (jnp.sqrt(v_hat) + ADAM_EPS) + ADAM_WD * w)
    return delta, m, v


def _adamw_summed(parts, w, m, v, tr, name):
    depth, R, C = w.shape

    def body(*refs):
        p_refs = refs[:depth]
        w_ref, m_ref, v_ref, g_ref, d_ref, nm_ref, nv_ref = refs[depth:]
        lay = pl.program_id(0)
        for l in range(depth):
            @pl.when(lay == l)
            def _(p_ref=p_refs[l]):
                g = p_ref[0].astype(F32)
                for j in range(1, N_DEV):
                    g = g + p_ref[j].astype(F32)
                g_ref[0] = g
        d_ref[0], nm_ref[0], nv_ref[0] = _adamw_math(w_ref[0], g_ref[0], m_ref[0], v_ref[0])

    part_spec = lambda l: pl.BlockSpec((N_DEV, tr, C), lambda lay, i: (0, jnp.where(lay == l, i, 0), 0))
    row = pl.BlockSpec((1, tr, C), lambda lay, i: (lay, i, 0))
    return pl.pallas_call(
        body, name=name, grid=(depth, R // tr),
        in_specs=[part_spec(l) for l in range(depth)] + [row, row, row],
        out_specs=[row] * 4, out_shape=[jax.ShapeDtypeStruct((depth, R, C), F32)] * 4,
        compiler_params=_params(("arbitrary", "arbitrary")),
    )(*parts, w, m, v)


def _adamw_small(w, g, m, v, name):
    def body(w_ref, g_ref, m_ref, v_ref, d_ref, nm_ref, nv_ref):
        d_ref[...], nm_ref[...], nv_ref[...] = _adamw_math(w_ref[...], g_ref[...], m_ref[...], v_ref[...])

    vm = pl.BlockSpec(memory_space=pltpu.VMEM)
    return pl.pallas_call(
        body, name=name, in_specs=[vm] * 4, out_specs=[vm] * 3,
        out_shape=[jax.ShapeDtypeStruct(w.shape, F32)] * 3,
        compiler_params=pltpu.CompilerParams(vmem_limit_bytes=VMEM_LIMIT),
    )(w, g, m, v)


def _pack(arrays):
    flat = jnp.concatenate([a.reshape(-1) for a in arrays])
    pad = (-flat.shape[0]) % (SUBLANES * LANES)
    return jnp.pad(flat, (0, pad)).reshape(-1, LANES)


def _unpack(buf, like):
    flat = buf.reshape(-1)
    out, off = [], 0
    for a in like:
        out.append(flat[off:off + a.size].reshape(a.shape))
        off += a.size
    return out


def _block_diag(w):
    eye = jnp.eye(N_HEADS, dtype=w.dtype)
    return jnp.einsum('hij,hk->hikj', w, eye).reshape(GROUP_W, GROUP_W)


def _diag_blocks(w):
    return jnp.einsum('hihj->hij', w.reshape(N_HEADS, HEAD_DIM, N_HEADS, HEAD_DIM))


def _pad_rows(a):
    return jnp.pad(a, ((0, SUBLANES - a.shape[0]), (0, 0)))


def _mixer_params(l, conv_a_w, conv_r_w, conv_r_b, lru_wa, lru_ba, lru_wx, lru_bx, lru_lambda, gmlp_norm_g,
                  gmlp_ws, gmlp_bs):
    tril = jnp.tril(jnp.ones((GMLP_CHUNK, GMLP_CHUNK), dtype=bool))
    vec = jnp.stack([conv_r_b[l], lru_ba[l], lru_bx[l], lru_lambda[l], gmlp_norm_g[l]])
    return {
        "wA": _pad_rows(conv_a_w[l]), "wR": _pad_rows(conv_r_w[l]), "vec": _pad_rows(vec),
        "wa": _block_diag(lru_wa[l]).astype(MXU_DTYPE), "wx": _block_diag(lru_wx[l]).astype(MXU_DTYPE),
        "ws": jnp.where(tril[None], gmlp_ws[l], 0.0).astype(MXU_DTYPE),
        "bs": jnp.repeat(jnp.transpose(gmlp_bs[l]), HEAD_DIM, axis=1),
    }


MIXER_NAMES = ("conv_a_w", "conv_r_w", "conv_r_b", "lru_wa", "lru_ba", "lru_wx", "lru_bx", "lru_lambda",
               "gmlp_norm_g", "gmlp_ws", "gmlp_bs")
SMALL_NAMES = ("norm_g",) + MIXER_NAMES + ("final_g",)


def _local_step(x, loss_target, norm_g, get_w_in, get_w_out, emit_early, emit_late, conv_a_w, conv_r_w, conv_r_b,
                lru_wa, lru_ba, lru_wx, lru_bx, lru_lambda, gmlp_norm_g, gmlp_ws, gmlp_bs, final_g):
    depth = norm_g.shape[0]
    D = x.shape[1]
    small = (conv_a_w, conv_r_w, conv_r_b, lru_wa, lru_ba, lru_wx, lru_bx, lru_lambda, gmlp_norm_g, gmlp_ws, gmlp_bs)
    saved = []
    for l in range(depth):
        mp = _mixer_params(l, *small)
        w_in_l = get_w_in(l, x)
        z, z_g, *qkv, y_abc, hs = _inproj_mix_fwd(x, norm_g[l].reshape(1, D), w_in_l, mp, f"inproj_mix_fwd_{l}")
        attn =[_attn_fwd(qkv[p], dil, f"attn_fwd_d{dil}_{l}") for p, dil in enumerate(ATTN_DILATIONS)]
        w_out_l = get_w_out(l, y_abc)
        x_new, y, o, *lse = _outproj(x, z_g, y_abc, attn, w_out_l, f"outproj_{l}")
        saved.append((x, z, z_g, qkv, hs, y, o, lse, mp, w_in_l, w_out_l))
        x = x_new
    dx, loss, d_final_g = _loss_head(x, final_g.reshape(1, D), loss_target, "loss_head")
    token = None
    for l in reversed(range(depth)):
        x_l, z, z_g, qkv, hs, y, o, lse, mp, w_in_l, w_out_l = saved[l]
        if token is not None:
            mp = dict(mp, vec=mp["vec"] + token[0, 0])
        (dw_out, dz_abc, dz_g, do1, do4, do16, dl1, dl4, dl16, dwA, dwR, dvec, dwa, dwx, dws, dbs) = _outproj_mix_bwd(
            dx, y, w_out_l, z, z_g, hs, o, mp, f"outproj_mix_bwd_{l}")
        token = emit_early(l, dw_out, [
            dwA[:conv_a_w.shape[1]], dwR[:conv_r_w.shape[1]], dvec[0], _diag_blocks(dwa), dvec[1], _diag_blocks(dwx),
            dvec[2], dvec[3], dvec[4], dws, jnp.transpose(dbs[:, :N_HEADS])])
        g_row = norm_g[l].reshape(1, D)
        if token is not None:
            g_row = g_row + token[0, 0]
        dqkv = [_attn_bwd(qkv[p], do, lse[p], dl, dil, f"attn_bwd_d{dil}_{l}")
                for p, (dil, do, dl) in enumerate(zip(ATTN_DILATIONS, (do1, do4, do16), (dl1, dl4, dl16)))]
        dx, dz, h, dng = _inproj_bwd(x_l, g_row, dx, dz_abc, dqkv, dz_g, w_in_l, f"inproj_bwd_{l}")
        dw_in = _inproj_wgrad(h, dz, f"inproj_wgrad_{l}")
        token = emit_late(l, dw_in, [dng[0]] + ([d_final_g[0]] if l == depth - 1 else []))
    return loss[0, 0], dx
WEIGHT_NAMES = ("norm_g", "w_in", "conv_a_w", "conv_r_w", "conv_r_b", "lru_wa", "lru_ba", "lru_wx", "lru_bx",
                "lru_lambda", "gmlp_norm_g", "gmlp_ws", "gmlp_bs", "w_out", "final_g")


def kernel(x, norm_g, w_in, conv_a_w, conv_r_w, conv_r_b, lru_wa, lru_ba, lru_wx, lru_bx, lru_lambda, gmlp_norm_g, gmlp_ws, gmlp_bs, w_out, final_g, loss_target, m_norm_g, m_w_in, m_conv_a_w, m_conv_r_w, m_conv_r_b, m_lru_wa, m_lru_ba, m_lru_wx, m_lru_bx, m_lru_lambda, m_gmlp_norm_g, m_gmlp_ws, m_gmlp_bs, m_w_out, m_final_g, v_norm_g, v_w_in, v_conv_a_w, v_conv_r_w, v_conv_r_b, v_lru_wa, v_lru_ba, v_lru_wx, v_lru_bx, v_lru_lambda, v_gmlp_norm_g, v_gmlp_ws, v_gmlp_bs, v_w_out, v_final_g):
    w = dict(norm_g=norm_g, w_in=w_in, conv_a_w=conv_a_w, conv_r_w=conv_r_w, conv_r_b=conv_r_b, lru_wa=lru_wa,
             lru_ba=lru_ba, lru_wx=lru_wx, lru_bx=lru_bx, lru_lambda=lru_lambda, gmlp_norm_g=gmlp_norm_g,
             gmlp_ws=gmlp_ws, gmlp_bs=gmlp_bs, w_out=w_out, final_g=final_g)
    m = dict(norm_g=m_norm_g, w_in=m_w_in, conv_a_w=m_conv_a_w, conv_r_w=m_conv_r_w, conv_r_b=m_conv_r_b,
             lru_wa=m_lru_wa, lru_ba=m_lru_ba, lru_wx=m_lru_wx, lru_bx=m_lru_bx, lru_lambda=m_lru_lambda,
             gmlp_norm_g=m_gmlp_norm_g, gmlp_ws=m_gmlp_ws, gmlp_bs=m_gmlp_bs, w_out=m_w_out, final_g=m_final_g)
    v = dict(norm_g=v_norm_g, w_in=v_w_in, conv_a_w=v_conv_a_w, conv_r_w=v_conv_r_w, conv_r_b=v_conv_r_b,
             lru_wa=v_lru_wa, lru_ba=v_lru_ba, lru_wx=v_lru_wx, lru_bx=v_lru_bx, lru_lambda=v_lru_lambda,
             gmlp_norm_g=v_gmlp_norm_g, gmlp_ws=v_gmlp_ws, gmlp_bs=v_gmlp_bs, w_out=v_w_out, final_g=v_final_g)
    depth, D, n_loc = w_in.shape
    e_loc = w_out.shape[1]
    cx, cy, cc = _my_place()
    me = 4 * cx + 2 * cy + cc

    w_in_w, w_out_w = w_in.astype(MXU_DTYPE), w_out.astype(MXU_DTYPE)
    c_loc = conv_a_w.shape[2]
    taps = (conv_a_w, conv_r_w)
    first, _ = _exchange_start([[(w_in_w[0], True), (_pack(taps), True)]], "gather_start_first")
    full_out = lambda g: g.reshape(N_DEV * e_loc, D)

    g_in0, g_taps = _exchange_wait(first[0], x, "gather_wait_in_0")
    groups = [[(w_out_w[0], True)]] + [[(w_in_w[l], True), (w_out_w[l], True)] for l in range(1, depth)]
    gathers, _ = _exchange_start(groups, "gather_start_rest", after=g_taps)
    g_taps = g_taps.reshape(N_DEV, -1)
    conv_full, off = [], 0
    for a in taps:
        part = g_taps[:, off:off + a.size].reshape((N_DEV,) + a.shape)
        conv_full.append(jnp.transpose(part, (1, 2, 0, 3)).reshape(a.shape[:2] + (N_DEV * c_loc,)))
        off += a.size
    conv_a_full, conv_r_full = conv_full
    later = {}

    def get_w_in(l, after):
        if l == 0:
            return g_in0
        g_in, later[l] = _exchange_wait(gathers[l], after, f"gather_wait_{l}")
        return g_in

    def get_w_out(l, after):
        if l == 0:
            return full_out(_exchange_wait(gathers[0], after, "gather_wait_out_0")[0])
        return full_out(later[l])

    early, late, last_token = {}, {}, [None]

    def emit_early(l, dw_out, mixer_grads):
        handles, token = _exchange_start(
            [[(dw_out.reshape(N_DEV, e_loc, D), False), (_pack(mixer_grads), True)]], f"early_start_{l}")
        early[l] = (handles[0], mixer_grads)
        return token

    def emit_late(l, dw_in, norm_grads):
        handles, token = _exchange_start([[(_pack(norm_grads), True)], [(dw_in, False)]], f"late_start_{l}")
        late[l] = (handles[0], handles[1], norm_grads)
        last_token[0] = token
        return token

    loss, grad_x = _local_step(
        x[0], loss_target[0], norm_g, get_w_in, get_w_out, emit_early, emit_late, conv_a_full, conv_r_full, conv_r_b,
        lru_wa, lru_ba, lru_wx, lru_bx, lru_lambda, gmlp_norm_g, gmlp_ws, gmlp_bs, final_g)
    loss = lax.psum(loss, ("x", "y", "c"))

    r_in, r_out, small_parts = {}, {}, []
    for l in reversed(range(depth)):
        r_out[l], r_mix = _exchange_wait(early[l][0], last_token[0], f"early_wait_{l}")
        (r_norm,) = _exchange_wait(late[l][0], last_token[0], f"late_wait_norm_{l}")
        small_parts += [r_mix, r_norm]
        if l > 0:
            (r_in[l],) = _exchange_wait(late[l][1], last_token[0], f"late_wait_{l}")
    big = {"w_out": _adamw_summed([r_out[l] for l in range(depth)], w_out, m_w_out, v_w_out, 128, "adamw_w_out")}

    sums = _sum_slots(small_parts, "sum_small_grads")
    by_layer = {}
    for i, l in enumerate(reversed(range(depth))):
        mix = _unpack(sums[2 * i], early[l][1])
        nrm = _unpack(sums[2 * i + 1], late[l][2])
        by_layer[l] = dict(zip(MIXER_NAMES, mix), norm_g=nrm[0])
        if l == depth - 1:
            g_final = nrm[1]
    g_small = {k: jnp.stack([by_layer[l][k] for l in range(depth)]) for k in ("norm_g",) + MIXER_NAMES}
    g_small["final_g"] = g_final
    for k in ("conv_a_w", "conv_r_w"):
        g_small[k] = lax.dynamic_slice_in_dim(g_small[k], me * c_loc, c_loc, axis=2)
    packs = [_pack([d[k] for k in SMALL_NAMES]) for d in (w, g_small, m, v)]
    res = _adamw_small(*packs, "adamw_small")
    like = [w[k] for k in SMALL_NAMES]
    d_s, m_s, v_s = (dict(zip(SMALL_NAMES, _unpack(r, like))) for r in res)

    (r_in[0],) = _exchange_wait(late[0][1], res[0], "late_wait_0")
    big["w_in"] = _adamw_summed([r_in[l] for l in range(depth)], w_in, m_w_in, v_w_in, 512, "adamw_w_in")

    grad, delta, new_m, new_v = {}, {}, {}, {}
    for k in WEIGHT_NAMES:
        if k in big:
            grad[k], delta[k], new_m[k], new_v[k] = big[k]
        else:
            grad[k], delta[k], new_m[k], new_v[k] = g_small[k], d_s[k], m_s[k], v_s[k]
    return (loss, grad_x[None], *[grad[k] for k in WEIGHT_NAMES], *[delta[k] for k in WEIGHT_NAMES],
            *[new_m[k] for k in WEIGHT_NAMES], *[new_v[k] for k in WEIGHT_NAMES])
```

```python
import functools
import math

import jax
import jax.numpy as jnp
from jax import lax
from jax.experimental import pallas as pl
from jax.experimental.pallas import tpu as pltpu

F32 = jnp.float32
MXU_DTYPE = jnp.bfloat16
WIRE_DTYPE = jnp.bfloat16
MESH = pl.DeviceIdType.MESH

N_DEV = 8
GROUP_W = 256
N_HEADS = 4
HEAD_DIM = 64
N_CHUNKS = 13
N_ABC = 9
GMLP_CHUNK = 128
ATTN_BLOCK = 128
ATTN_BLOCKS_PER_STEP = 4
ATTN_DILATIONS = (1, 4, 16)
NORM_EPS = 1e-6
RG_C = 8.0
SUBLANES = 8
LANES = 128
VMEM_LIMIT = 56 * 1024 * 1024

ADAM_LR = 0.001
ADAM_B1 = 0.9
ADAM_B2 = 0.999
ADAM_EPS = 1e-08
ADAM_WD = 0.01
ADAM_STEP = 10

TM_MIX = 512
TM_MM = 512
TM_WGRAD = 1024


def _params(sem, vmem=VMEM_LIMIT):
    return pltpu.CompilerParams(dimension_semantics=sem, vmem_limit_bytes=vmem)


def _mm(a, b):
    return jnp.dot(a.astype(MXU_DTYPE), b.astype(MXU_DTYPE), preferred_element_type=F32)


def _mm_tn(a, b):
    return lax.dot_general(a.astype(MXU_DTYPE), b.astype(MXU_DTYPE), (((0,), (0,)), ((), ())),
                           preferred_element_type=F32)


def _mm_nt(a, b):
    return lax.dot_general(a.astype(MXU_DTYPE), b.astype(MXU_DTYPE), (((1,), (1,)), ((), ())),
                           preferred_element_type=F32)


def _sigmoid(x):
    return 0.5 * jnp.tanh(0.5 * x) + 0.5


def _silu_and_grad(x):
    s = _sigmoid(x)
    return x * s, s * (1.0 + x * (1.0 - s))


_GELU_K = math.sqrt(2.0 / math.pi)
_GELU_C = 0.044715


def _gelu_and_grad(x):
    x2 = x * x
    t = jnp.tanh(_GELU_K * (x + _GELU_C * x * x2))
    val = 0.5 * x * (1.0 + t)
    grad = 0.5 * (1.0 + t) + 0.5 * x * (1.0 - t * t) * (_GELU_K * (1.0 + 3.0 * _GELU_C * x2))
    return val, grad


def _gelu(x):
    return 0.5 * x * (1.0 + jnp.tanh(_GELU_K * (x + _GELU_C * x * x * x)))


def _expm1_nonpos(u):
    poly = 1.0 / math.factorial(9)
    for k in range(8, 0, -1):
        poly = poly * u + 1.0 / math.factorial(k)
    return jnp.where(u > -0.25, poly * u, jnp.exp(u) - 1.0)


def _softplus(x):
    return jnp.maximum(x, 0.0) + jnp.log(1.0 + jnp.exp(-jnp.abs(x)))


def _shift_down(t, halo, k):
    rolled = pltpu.roll(t, k, 0)
    hr = pltpu.roll(halo, k, 0)
    row = lax.broadcasted_iota(jnp.int32, halo.shape, 0)
    first = jnp.where(row < k, hr, rolled[0:SUBLANES])
    return jnp.concatenate([first, rolled[SUBLANES:]], axis=0)


def _shift_up(t, nxt, k):
    tm = t.shape[0]
    rolled = pltpu.roll(t, tm - k, 0)
    nr = pltpu.roll(nxt, SUBLANES - k, 0)
    row = lax.broadcasted_iota(jnp.int32, nxt.shape, 0)
    last = jnp.where(row >= SUBLANES - k, nr, rolled[tm - SUBLANES:tm])
    return jnp.concatenate([rolled[:tm - SUBLANES], last], axis=0)


def _scan_fwd(a, b):
    tm = a.shape[0]
    row = lax.broadcasted_iota(jnp.int32, a.shape, 0)
    s = 1
    while s < tm:
        a_s = pltpu.roll(a, s, 0)
        b_s = pltpu.roll(b, s, 0)
        m = row >= s
        b = jnp.where(m, a * b_s + b, b)
        a = jnp.where(m, a * a_s, a)
        s *= 2
    return a, b


def _scan_rev(a, g):
    tm = a.shape[0]
    row = lax.broadcasted_iota(jnp.int32, a.shape, 0)
    s = 1
    while s < tm:
        a_s = pltpu.roll(a, tm - s, 0)
        g_s = pltpu.roll(g, tm - s, 0)
        m = row < tm - s
        g = jnp.where(m, g + a * g_s, g)
        a = jnp.where(m, a * a_s, a)
        s *= 2
    return g


def _group_rows(scr_ref, row, n_groups):
    return jnp.concatenate([scr_ref[pl.ds(c, 1), pl.ds(row, n_groups, stride=SUBLANES), :][0]
                            for c in range(scr_ref.shape[0])], axis=1)


def _spread_rows(rows_ref, n_groups, w):
    return jnp.concatenate([jnp.broadcast_to(rows_ref[g:g + 1, :], (SUBLANES, w)) for g in range(n_groups)], axis=0)


def _scan_groups(a, b, reverse):
    tm, w = a.shape
    shape3 = (tm // SUBLANES, SUBLANES, w)
    a3, b3 = a.reshape(shape3), b.reshape(shape3)
    sub = lax.broadcasted_iota(jnp.int32, shape3, 1)
    s = 1
    while s < SUBLANES:
        shift = SUBLANES - s if reverse else s
        a_s = pltpu.roll(a3, shift, 1)
        b_s = pltpu.roll(b3, shift, 1)
        m = (sub < SUBLANES - s) if reverse else (sub >= s)
        b3 = jnp.where(m, a3 * b_s + b3, b3)
        a3 = jnp.where(m, a3 * a_s, a3)
        s *= 2
    return a3.reshape(tm, w), b3.reshape(tm, w)


def _scan_fwd_tile(a, b, h_in, sa_ref, sb_ref, sc_ref):
    tm, w = a.shape
    n_groups = tm // SUBLANES
    a_loc, b_loc = _scan_groups(a, b, False)
    _put(sa_ref, a_loc)
    _put(sb_ref, b_loc)
    a_end, b_end = _scan_fwd(_group_rows(sa_ref, SUBLANES - 1, n_groups), _group_rows(sb_ref, SUBLANES - 1, n_groups))
    h_end = b_end + a_end * h_in
    sc_ref[...] = _shift_down(h_end, jnp.broadcast_to(h_in, (SUBLANES, w)), 1)
    return b_loc + a_loc * _spread_rows(sc_ref, n_groups, w), h_end


def _scan_rev_tile(a, g, sa_ref, sb_ref, sc_ref):
    tm, w = a.shape
    n_groups = tm // SUBLANES
    a_loc, g_loc = _scan_groups(a, g, True)
    _put(sa_ref, a_loc)
    _put(sb_ref, g_loc)
    d_first = _scan_rev(_group_rows(sa_ref, 0, n_groups), _group_rows(sb_ref, 0, n_groups))
    sc_ref[...] = _shift_up(d_first, jnp.zeros((SUBLANES, w), F32), 1)
    return g_loc + a_loc * _spread_rows(sc_ref, n_groups, w)


def _lane_scratch(tm, w):
    return pltpu.VMEM((w // LANES, tm, LANES), F32)


def _put(scr_ref, val):
    for c in range(scr_ref.shape[0]):
        scr_ref[c] = val[:, c * LANES:(c + 1) * LANES].astype(F32)


def _get(scr_ref):
    return jnp.concatenate([scr_ref[c] for c in range(scr_ref.shape[0])], axis=1)


def _deinterleave(src_ref, dst_ref, dil):
    nc, tm, _ = src_ref.shape
    w = nc * LANES
    for r in range(dil):
        for c in range(nc):
            piece = src_ref[pl.ds(c, 1), pl.ds(r, tm // dil, stride=dil), :][0] if dil > 1 else src_ref[c]
            dst_ref[:, r * w + c * LANES:r * w + (c + 1) * LANES] = piece.astype(dst_ref.dtype)


def _interleave(src_ref, dst_ref, dil):
    nc, tm, _ = dst_ref.shape
    w = nc * LANES
    for r in range(dil):
        for c in range(nc):
            dst_ref[pl.ds(c, 1), pl.ds(r, tm // dil, stride=dil), :] = (
                src_ref[:, r * w + c * LANES:r * w + (c + 1) * LANES].astype(F32)[None])


def _dilated_spec(tm, w, dil, index=lambda i: i):
    return pl.BlockSpec((tm // dil, dil * w), lambda i: (index(i), 0))


def _dilated_shape(S, w, dil, dtype):
    return jax.ShapeDtypeStruct((S // dil, dil * w), dtype)


def _head_masks(shape):
    lane = lax.broadcasted_iota(jnp.int32, shape, 1)
    return [(lane >= h * HEAD_DIM) & (lane < (h + 1) * HEAD_DIM) for h in range(N_HEADS)]


def _colsum(v):
    return jnp.sum(v, axis=0, keepdims=True)


def _conv_a(z_of, halo_of, w_ref):
    p = z_of(2) * z_of(0)
    p_h = halo_of(2) * halo_of(0)
    cv = w_ref[2:3, :] * p + w_ref[1:2, :] * _shift_down(p, p_h, 1) + w_ref[0:1, :] * _shift_down(p, p_h, 2)
    return p, p_h, cv


def _lru_gates(z_of, halo_of, wr_ref, vec_ref, wa_ref, wx_ref):
    rx = z_of(4)
    rx_h = halo_of(4)
    sh = [rx, _shift_down(rx, rx_h, 1), _shift_down(rx, rx_h, 2), _shift_down(rx, rx_h, 3)]
    xc = (wr_ref[3:4, :] * sh[0] + wr_ref[2:3, :] * sh[1] + wr_ref[1:2, :] * sh[2]
          + wr_ref[0:1, :] * sh[3] + vec_ref[0:1, :])
    ga = _sigmoid(jnp.dot(xc.astype(MXU_DTYPE), wa_ref[...], preferred_element_type=F32) + vec_ref[1:2, :])
    gi = _sigmoid(jnp.dot(xc.astype(MXU_DTYPE), wx_ref[...], preferred_element_type=F32) + vec_ref[2:3, :])
    sp = _softplus(-vec_ref[3:4, :])
    log_a = (-RG_C * ga) * sp
    a = jnp.exp(log_a)
    mult = jnp.sqrt(-_expm1_nonpos(2.0 * log_a))
    return xc, sh, ga, gi, a, mult, sp


def _gmlp_fwd(z_of, vec_ref, ws_ref, bs_ref, tm):
    u = _gelu(z_of(6))
    gv = _gelu(z_of(7))
    rr = lax.rsqrt(jnp.mean(gv * gv, axis=-1, keepdims=True) + NORM_EPS)
    vn = (gv * rr) * vec_ref[4:5, :]
    masks = _head_masks((GMLP_CHUNK, GROUP_W))
    parts = []
    for c in range(tm // GMLP_CHUNK):
        vc = vn[c * GMLP_CHUNK:(c + 1) * GMLP_CHUNK].astype(MXU_DTYPE)
        acc = bs_ref[...]
        for h in range(N_HEADS):
            acc = acc + jnp.where(masks[h], jnp.dot(ws_ref[h], vc, preferred_element_type=F32), 0.0)
        parts.append(acc)
    return u, gv, rr, vn, jnp.concatenate(parts, axis=0)


def _mix_specs(tm, S, order):
    const2 = lambda shape: pl.BlockSpec(shape, lambda i: (0, 0))
    return [const2((SUBLANES, GROUP_W)), const2((SUBLANES, GROUP_W)), const2((SUBLANES, GROUP_W)),
            const2((GROUP_W, GROUP_W)), const2((GROUP_W, GROUP_W)),
            pl.BlockSpec((N_HEADS, GMLP_CHUNK, GMLP_CHUNK), lambda i: (0, 0, 0)),
            const2((GMLP_CHUNK, GROUP_W))]


def _inproj_mix_fwd(x, g, w_t, mp, name):
    S, D = x.shape
    N = w_t.shape[0]
    tm = TM_MIX
    hb = tm // SUBLANES
    n_abc = N_ABC * GROUP_W
    n_qkv = 3 * GROUP_W

    def body(x_ref, g_ref, w_ref, wA_ref, wR_ref, vec_ref, wa_ref, wx_ref, ws_ref, bs_ref,
             z_ref, zg_ref, q1_ref, q4_ref, q16_ref, y_ref, h_ref,
             qkv_ref, halo_ref, carry_ref, sa_ref, sb_ref, sc_ref):
        @pl.when(pl.program_id(0) == 0)
        def _():
            halo_ref[...] = jnp.zeros_like(halo_ref)
            carry_ref[...] = jnp.zeros_like(carry_ref)

        xv = x_ref[...]
        r = lax.rsqrt(jnp.mean(xv * xv, axis=-1, keepdims=True) + NORM_EPS)
        hn = ((xv * r) * g_ref[...]).astype(MXU_DTYPE)
        z_ref[...] = _mm_nt(hn, w_ref[0:n_abc, :])
        _put(qkv_ref, _mm_nt(hn, w_ref[n_abc:n_abc + n_qkv, :]))
        zg_ref[...] = _mm_nt(hn, w_ref[n_abc + n_qkv:, :])
        for dil, ref in zip(ATTN_DILATIONS, (q1_ref, q4_ref, q16_ref)):
            _deinterleave(qkv_ref, ref, dil)

        z_of = lambda c: z_ref[:, c * GROUP_W:(c + 1) * GROUP_W]
        halo_of = lambda c: halo_ref[:, c * GROUP_W:(c + 1) * GROUP_W]

        _, _, cv = _conv_a(z_of, halo_of, wA_ref)
        y_ref[:, 0:GROUP_W] = (z_of(1) * cv * _silu_and_grad(z_of(3))[0]).astype(y_ref.dtype)

        xc, _, _, gi, a, mult, _ = _lru_gates(z_of, halo_of, wR_ref, vec_ref, wa_ref, wx_ref)
        b = mult * (gi * xc)
        h, h_end = _scan_fwd_tile(a, b, carry_ref[SUBLANES - 1:SUBLANES, :], sa_ref, sb_ref, sc_ref)
        h_ref[...] = h
        carry_ref[...] = h_end[hb - SUBLANES:hb]
        y_ref[:, GROUP_W:2 * GROUP_W] = (h * _silu_and_grad(z_of(5))[0]).astype(y_ref.dtype)

        u, _, _, _, sp = _gmlp_fwd(z_of, vec_ref, ws_ref, bs_ref, tm)
        y_ref[:, 2 * GROUP_W:3 * GROUP_W] = (u * sp * _silu_and_grad(z_of(8))[0]).astype(y_ref.dtype)
        halo_ref[...] = z_ref[tm - SUBLANES:tm, :]

    row = lambda wd: pl.BlockSpec((tm, wd), lambda i: (i, 0))
    return pl.pallas_call(
        body, name=name, grid=(S // tm,),
        in_specs=[row(D), pl.BlockSpec((1, D), lambda i: (0, 0)),
                  pl.BlockSpec((N, D), lambda i: (0, 0), pipeline_mode=pl.Buffered(1))] + _mix_specs(tm, S, "fwd"),
        out_specs=[row(n_abc), row(GROUP_W)] + [_dilated_spec(tm, n_qkv, dil) for dil in ATTN_DILATIONS]
                  + [row(3 * GROUP_W), row(GROUP_W)],
        out_shape=[jax.ShapeDtypeStruct((S, n_abc), F32), jax.ShapeDtypeStruct((S, GROUP_W), F32)]
                  + [_dilated_shape(S, n_qkv, dil, MXU_DTYPE) for dil in ATTN_DILATIONS]
                  + [jax.ShapeDtypeStruct((S, 3 * GROUP_W), MXU_DTYPE), jax.ShapeDtypeStruct((S, GROUP_W), F32)],
        scratch_shapes=[_lane_scratch(tm, n_qkv), pltpu.VMEM((SUBLANES, n_abc), F32),
                        pltpu.VMEM((SUBLANES, GROUP_W), F32), _lane_scratch(tm, GROUP_W), _lane_scratch(tm, GROUP_W),
                        pltpu.VMEM((hb, GROUP_W), F32)],
        compiler_params=_params(("arbitrary",)),
    )(x, g, w_t, mp["wA"], mp["wR"], mp["vec"], mp["wa"], mp["wx"], mp["ws"], mp["bs"])


_NEG = -1e30


def _slope(h):
    return 2.0 ** (-8.0 * (h + 1) / N_HEADS)


def _attn_bias(dil, offsets, n_keys):
    shape = (ATTN_BLOCK, n_keys)
    qi = lax.broadcasted_iota(jnp.int32, shape, 0)
    ki = lax.broadcasted_iota(jnp.int32, shape, 1)
    blocks = []
    for f in offsets:
        delta = qi + f - ki
        valid = (delta >= 0) & (delta <= ATTN_BLOCK)
        dist = (delta * dil).astype(F32)
        for h in range(N_HEADS):
            blocks.append(jnp.where(valid, -_slope(h) * dist, _NEG))
    return jnp.concatenate(blocks, axis=0)


def _stack_heads(t, masks):
    return jnp.concatenate([jnp.where(m, t, jnp.zeros_like(t)) for m in masks], axis=0)


def _unstack_heads(t4, masks, base=0):
    out = t4[base * ATTN_BLOCK:(base + 1) * ATTN_BLOCK]
    for h in range(1, N_HEADS):
        out = jnp.where(masks[h], t4[(base + h) * ATTN_BLOCK:(base + h + 1) * ATTN_BLOCK], out)
    return out


def _attn_fwd(qkv, dil, name):
    rows = qkv.shape[0]
    nb = rows // ATTN_BLOCK
    scale = 1.0 / math.sqrt(HEAD_DIM)
    B = ATTN_BLOCK
    per_step = ATTN_BLOCKS_PER_STEP

    def body(q_ref, kc_ref, kp_ref, vc_ref, vp_ref, o_ref, l_ref, bias_ref):
        n = pl.program_id(1)

        @pl.when(n == 0)
        def _():
            bias_ref[...] = _attn_bias(dil, (B,), 2 * B)

        masks = _head_masks((B, GROUP_W))
        for j in range(per_step):
            own = slice(j * B, (j + 1) * B)
            before = slice((j - 1) * B, j * B)
            qs = _stack_heads(q_ref[own], masks)
            keys = jnp.concatenate([kp_ref[...] if j == 0 else kc_ref[before], kc_ref[own]], axis=0)
            vals = jnp.concatenate([vp_ref[...] if j == 0 else vc_ref[before], vc_ref[own]], axis=0)
            s = _mm_nt(qs, keys) * scale + bias_ref[...]
            if j == 0:
                key_col = lax.broadcasted_iota(jnp.int32, s.shape, 1)
                s = jnp.where((n == 0) & (key_col < B), _NEG, s)
            m = jnp.max(s, axis=-1, keepdims=True)
            p = jnp.exp(s - m)
            l = jnp.sum(p, axis=-1, keepdims=True)
            o4 = jnp.dot(p.astype(MXU_DTYPE), vals, preferred_element_type=F32)
            o_ref[own] = _unstack_heads(o4, masks) / _unstack_heads(jnp.broadcast_to(l, o4.shape), masks)
            l_ref[own] = _unstack_heads(jnp.broadcast_to(m + jnp.log(l), o4.shape), masks)

    blk = (per_step * B, GROUP_W)
    cur = lambda c: pl.BlockSpec(blk, lambda r, n: (n, r * 3 + c))
    prev = lambda c: pl.BlockSpec((B, GROUP_W), lambda r, n: (jnp.maximum(n * per_step - 1, 0), r * 3 + c))
    out = pl.BlockSpec(blk, lambda r, n: (n, r))
    return pl.pallas_call(
        body, name=name, grid=(dil, nb // per_step),
        in_specs=[cur(0), cur(1), prev(1), cur(2), prev(2)],
        out_specs=[out, out],
        out_shape=[jax.ShapeDtypeStruct((rows, dil * GROUP_W), F32)] * 2,
        scratch_shapes=[pltpu.VMEM((N_HEADS * ATTN_BLOCK, 2 * ATTN_BLOCK), F32)],
        compiler_params=_params(("parallel", "arbitrary")),
    )(qkv, qkv, qkv, qkv, qkv)


def _outproj(x, z_g, y_abc, attn, w_out, name):
    S, D = x.shape
    tm = TM_MM
    n_abc = 3 * GROUP_W

    def body(x_ref, g_ref, yabc_ref, o1, l1, o2, l2, o3, l3, w_ref,
             xn_ref, y_ref, o_ref, lse1_ref, lse4_ref, lse16_ref, so2, sl2, so3, sl3, slse):
        for src, dst, dil in ((o2, so2, ATTN_DILATIONS[1]), (l2, sl2, ATTN_DILATIONS[1]),
                              (o3, so3, ATTN_DILATIONS[2]), (l3, sl3, ATTN_DILATIONS[2])):
            _interleave(src, dst, dil)
        la, lb, lc = l1[...], _get(sl2), _get(sl3)
        mx = jnp.maximum(jnp.maximum(la, lb), lc)
        ea, eb, ec = jnp.exp(la - mx), jnp.exp(lb - mx), jnp.exp(lc - mx)
        den = ea + eb + ec
        o = (ea * o1[...] + eb * _get(so2) + ec * _get(so3)) / den
        o_ref[...] = o
        _put(slse, mx + jnp.log(den))
        for dil, ref in zip(ATTN_DILATIONS, (lse1_ref, lse4_ref, lse16_ref)):
            _deinterleave(slse, ref, dil)
        y_d = o * _silu_and_grad(g_ref[...])[0]
        y_ref[:, 0:n_abc] = yabc_ref[...].astype(MXU_DTYPE)
        y_ref[:, n_abc:] = y_d.astype(MXU_DTYPE)
        xn_ref[...] = x_ref[...] + jnp.dot(y_ref[...], w_ref[...], preferred_element_type=F32)

    row = lambda w: pl.BlockSpec((tm, w), lambda i: (i, 0))
    dil_specs = [_dilated_spec(tm, GROUP_W, dil) for dil in ATTN_DILATIONS]
    (o1, l1), (o2, l2), (o3, l3) = attn
    return pl.pallas_call(
        body, name=name, grid=(S // tm,),
        in_specs=[row(D), row(GROUP_W), row(n_abc)] + [sp for sp in dil_specs for _ in range(2)]
                 + [pl.BlockSpec(w_out.shape, lambda i: (0, 0))],
        out_specs=[row(D), row(4 * GROUP_W), row(GROUP_W)] + dil_specs,
        out_shape=[jax.ShapeDtypeStruct((S, D), F32), jax.ShapeDtypeStruct((S, 4 * GROUP_W), MXU_DTYPE),
                   jax.ShapeDtypeStruct((S, GROUP_W), F32)]
                  + [_dilated_shape(S, GROUP_W, dil, F32) for dil in ATTN_DILATIONS],
        scratch_shapes=[_lane_scratch(tm, GROUP_W)] * 5,
        compiler_params=_params(("parallel",)),
    )(x, z_g, y_abc, o1, l1, o2, l2, o3, l3, w_out)


def _loss_head(x, g, target, name):
    S, D = x.shape
    tm = TM_MM

    def body(x_ref, g_ref, t_ref, dx_ref, loss_ref, dg_ref):
        i = pl.program_id(0)

        @pl.when(i == 0)
        def _():
            loss_ref[...] = jnp.zeros_like(loss_ref)
            dg_ref[...] = jnp.zeros_like(dg_ref)

        xv = x_ref[...]
        r = lax.rsqrt(jnp.mean(xv * xv, axis=-1, keepdims=True) + NORM_EPS)
        xn = xv * r
        err = xn * g_ref[...] - t_ref[...]
        per_tok = jnp.mean(err * err, axis=-1, keepdims=True)
        loss_ref[...] += 0.5 * jnp.sum(per_tok, axis=0, keepdims=True)
        dout = err * (1.0 / D)
        dg_ref[...] += _colsum(dout * xn)
        dxn = dout * g_ref[...]
        dx_ref[...] = r * (dxn - xn * jnp.mean(dxn * xn, axis=-1, keepdims=True))

    row = pl.BlockSpec((tm, D), lambda i: (i, 0))
    return pl.pallas_call(
        body, name=name, grid=(S // tm,),
        in_specs=[row, pl.BlockSpec((1, D), lambda i: (0, 0)), row],
        out_specs=[row, pl.BlockSpec((1, LANES), lambda i: (0, 0)), pl.BlockSpec((1, D), lambda i: (0, 0))],
        out_shape=[jax.ShapeDtypeStruct((S, D), F32), jax.ShapeDtypeStruct((1, LANES), F32),
                   jax.ShapeDtypeStruct((1, D), F32)],
        compiler_params=_params(("arbitrary",)),
    )(x, g, target)


def _outproj_mix_bwd(dx, y, w_out, z, z_g, hs, o, mp, name):
    S, D = dx.shape
    E = y.shape[1]
    tm = TM_MIX
    hb = tm // SUBLANES
    nT = S // tm
    last_blk = S // SUBLANES - 1
    wcols = N_ABC * GROUP_W

    def body(dx_ref, y_ref, w_ref, z_ref, zh_ref, zn_ref, zg_ref, h_ref, hh_ref, o_ref,
             wA_ref, wR_ref, vec_ref, wa_ref, wx_ref, ws_ref, bs_ref,
             dw_ref, dz_ref, dzg_ref, do1_ref, do4_ref, do16_ref, dl1_ref, dl4_ref, dl16_ref,
             dwA_ref, dwR_ref, dvec_ref, dwa_ref, dwx_ref, dws_ref, dbs_ref,
             hcarry_ref, xcarry_ref, bsacc_ref, do_ref, dl_ref, sa_ref, sb_ref, sc_ref, dy_ref, dyn_ref, acc_ref):
        i = pl.program_id(0)
        ti = nT - 1 - i

        @pl.when(i == 0)
        def _():
            acc_ref[...] = jnp.zeros_like(acc_ref)
            dyn_ref[...] = jnp.zeros_like(dyn_ref)
            hcarry_ref[...] = jnp.zeros_like(hcarry_ref)
            xcarry_ref[...] = jnp.zeros_like(xcarry_ref)
            bsacc_ref[...] = jnp.zeros_like(bsacc_ref)
            dwA_ref[...] = jnp.zeros_like(dwA_ref)
            dwR_ref[...] = jnp.zeros_like(dwR_ref)
            dvec_ref[...] = jnp.zeros_like(dvec_ref)
            dwa_ref[...] = jnp.zeros_like(dwa_ref)
            dwx_ref[...] = jnp.zeros_like(dwx_ref)
            dws_ref[...] = jnp.zeros_like(dws_ref)
            dbs_ref[...] = jnp.zeros_like(dbs_ref)

        dxb = dx_ref[...].astype(MXU_DTYPE)
        dy_ref[...] = _mm_nt(dxb, w_ref[...])
        acc_ref[...] += _mm_tn(y_ref[...], dxb)

        @pl.when(i == nT - 1)
        def _():
            dw_ref[...] = acc_ref[...].astype(dw_ref.dtype)

        has_prev = ti > 0
        has_next = i > 0
        col = lambda c: slice(c * GROUP_W, (c + 1) * GROUP_W)
        z_of = lambda c: z_ref[:, col(c)]
        halo_of = lambda c: jnp.where(has_prev, zh_ref[:, col(c)], 0.0)
        next_of = lambda c: zn_ref[:, col(c)]

        p, p_h, cv = _conv_a(z_of, halo_of, wA_ref)
        sg, dsg = _silu_and_grad(z_of(3))
        a_b = z_of(1)
        dya = dy_ref[:, col(0)]
        dcv = dya * a_b * sg
        dcv_n = jnp.where(has_next, dyn_ref[...] * next_of(1) * _silu_and_grad(next_of(3))[0], 0.0)
        dp = (wA_ref[2:3, :] * dcv + wA_ref[1:2, :] * _shift_up(dcv, dcv_n, 1)
              + wA_ref[0:1, :] * _shift_up(dcv, dcv_n, 2))
        dwA_ref[2:3, :] += _colsum(dcv * p)
        dwA_ref[1:2, :] += _colsum(dcv * _shift_down(p, p_h, 1))
        dwA_ref[0:1, :] += _colsum(dcv * _shift_down(p, p_h, 2))
        def put_dz(c, val):
            dz_ref[:, col(c)] = val.astype(dz_ref.dtype)

        put_dz(0, dp * z_of(2))
        put_dz(1, dya * cv * sg)
        put_dz(2, dp * z_of(0))
        put_dz(3, dya * a_b * cv * dsg)

        xc, sh, ga, gi, a, mult, sp = _lru_gates(z_of, halo_of, wR_ref, vec_ref, wa_ref, wx_ref)
        h = h_ref[...]
        h_prev = _shift_down(h, jnp.where(has_prev, hh_ref[...], 0.0), 1)
        sgr, dsgr = _silu_and_grad(z_of(5))
        dyb = dy_ref[:, col(1)]
        put_dz(5, dyb * h * dsgr)
        row = lax.broadcasted_iota(jnp.int32, (tm, GROUP_W), 0)
        g_in = dyb * sgr + jnp.where(row == tm - 1, hcarry_ref[0:1, :], 0.0)
        a_up = _shift_up(a, jnp.zeros((SUBLANES, GROUP_W), F32), 1)
        dH = _scan_rev_tile(a_up, g_in, sa_ref, sb_ref, sc_ref)
        hcarry_ref[...] = (a * dH)[0:SUBLANES]
        da = dH * h_prev
        gx = gi * xc
        dmult = dH * gx
        dgi = dH * mult * xc
        dxc = dH * mult * gi
        dlog_a = da * a - dmult * (a * a) / mult
        dga = dlog_a * (-RG_C * sp)
        dlam_row = _colsum(dlog_a * (-RG_C * ga)) * (-_sigmoid(-vec_ref[3:4, :]))
        dpre_a = dga * ga * (1.0 - ga)
        dpre_i = dgi * gi * (1.0 - gi)
        dwa_ref[...] += _mm_tn(xc, dpre_a)
        dwx_ref[...] += _mm_tn(xc, dpre_i)
        dxc = dxc + _mm_nt(dpre_a, wa_ref[...]) + _mm_nt(dpre_i, wx_ref[...])
        dvec_ref[0:1, :] += _colsum(dxc)
        dvec_ref[1:2, :] += _colsum(dpre_a)
        dvec_ref[2:3, :] += _colsum(dpre_i)
        dvec_ref[3:4, :] += dlam_row
        for k in range(4):
            dwR_ref[k:k + 1, :] += _colsum(dxc * sh[3 - k])
        dxc_n = xcarry_ref[...]
        put_dz(4, wR_ref[3:4, :] * dxc + wR_ref[2:3, :] * _shift_up(dxc, dxc_n, 1)
               + wR_ref[1:2, :] * _shift_up(dxc, dxc_n, 2) + wR_ref[0:1, :] * _shift_up(dxc, dxc_n, 3))
        xcarry_ref[...] = dxc[0:SUBLANES]

        c_u, c_v = z_of(6), z_of(7)
        u, du_dx = _gelu_and_grad(c_u)
        gv, dgv_dx = _gelu_and_grad(c_v)
        rr = lax.rsqrt(jnp.mean(gv * gv, axis=-1, keepdims=True) + NORM_EPS)
        xhat = gv * rr
        g_c = vec_ref[4:5, :]
        vn = xhat * g_c
        masks = _head_masks((GMLP_CHUNK, GROUP_W))
        tri_r = lax.broadcasted_iota(jnp.int32, (GMLP_CHUNK, GMLP_CHUNK), 0)
        tri_c = lax.broadcasted_iota(jnp.int32, (GMLP_CHUNK, GMLP_CHUNK), 1)
        tril = tri_r >= tri_c
        sgc, dsgc = _silu_and_grad(z_of(8))
        dyc = dy_ref[:, col(2)]
        dsp_full = dyc * u * sgc
        sp_parts, dvn_parts = [], []
        for c in range(tm // GMLP_CHUNK):
            rs = slice(c * GMLP_CHUNK, (c + 1) * GMLP_CHUNK)
            vc = vn[rs].astype(MXU_DTYPE)
            dsp_c = dsp_full[rs]
            bsacc_ref[...] += dsp_c
            acc = bs_ref[...]
            dvn_c = jnp.zeros((GMLP_CHUNK, GROUP_W), F32)
            for h in range(N_HEADS):
                w_h = ws_ref[h]
                acc = acc + jnp.where(masks[h], jnp.dot(w_h, vc, preferred_element_type=F32), 0.0)
                dsp_h = jnp.where(masks[h], dsp_c, 0.0).astype(MXU_DTYPE)
                dvn_c = dvn_c + _mm_tn(w_h, dsp_h)
                dws_ref[h] += jnp.where(tril, _mm_nt(dsp_h, vc), 0.0)
            sp_parts.append(acc)
            dvn_parts.append(dvn_c)
        spv = jnp.concatenate(sp_parts, axis=0)
        dvn = jnp.concatenate(dvn_parts, axis=0)
        put_dz(6, dyc * spv * sgc * du_dx)
        put_dz(8, dyc * u * spv * dsgc)
        dvec_ref[4:5, :] += _colsum(dvn * xhat)
        dgvn = dvn * g_c
        dgv = rr * (dgvn - xhat * jnp.mean(dgvn * xhat, axis=-1, keepdims=True))
        put_dz(7, dgv * dgv_dx)

        sgd, dsgd = _silu_and_grad(zg_ref[...])
        dyd = dy_ref[:, col(3)]
        ov = o_ref[...]
        do = dyd * sgd
        _put(do_ref, do)
        dzg_ref[...] = (dyd * ov * dsgd).astype(dzg_ref.dtype)
        prod = do * ov
        tmasks = _head_masks((tm, GROUP_W))
        dl = jnp.zeros((tm, GROUP_W), F32)
        for h in range(N_HEADS):
            dl = jnp.where(tmasks[h], jnp.sum(jnp.where(tmasks[h], prod, 0.0), axis=-1, keepdims=True), dl)
        _put(dl_ref, dl)
        for dil, d_out, l_out in zip(ATTN_DILATIONS, (do1_ref, do4_ref, do16_ref), (dl1_ref, dl4_ref, dl16_ref)):
            _deinterleave(do_ref, d_out, dil)
            _deinterleave(dl_ref, l_out, dil)

        @pl.when(i == nT - 1)
        def _():
            acc = bsacc_ref[...]
            lane = lax.broadcasted_iota(jnp.int32, (GMLP_CHUNK, LANES), 1)
            out = jnp.zeros((GMLP_CHUNK, LANES), F32)
            for h in range(N_HEADS):
                out = jnp.where(lane == h, jnp.sum(jnp.where(masks[h], acc, 0.0), axis=-1, keepdims=True), out)
            dbs_ref[...] = out

        dyn_ref[...] = dy_ref[0:SUBLANES, 0:GROUP_W]

    rev = lambda w: pl.BlockSpec((tm, w), lambda i: (nT - 1 - i, 0))
    prev8 = lambda w: pl.BlockSpec((SUBLANES, w), lambda i: (jnp.maximum((nT - 1 - i) * hb - 1, 0), 0))
    next8 = lambda w: pl.BlockSpec((SUBLANES, w), lambda i: (jnp.minimum((nT - i) * hb, last_blk), 0))
    const2 = lambda shape: pl.BlockSpec(shape, lambda i: (0, 0))
    dil_specs = [_dilated_spec(tm, GROUP_W, dil, lambda i: nT - 1 - i) for dil in ATTN_DILATIONS]
    dil_shapes = [_dilated_shape(S, GROUP_W, dil, F32) for dil in ATTN_DILATIONS]
    small = (SUBLANES, GROUP_W)
    sq = (GROUP_W, GROUP_W)
    ws_shape = (N_HEADS, GMLP_CHUNK, GMLP_CHUNK)
    return pl.pallas_call(
        body, name=name, grid=(nT,),
        in_specs=[rev(D), rev(E), pl.BlockSpec((E, D), lambda i: (0, 0), pipeline_mode=pl.Buffered(1)),
                  rev(wcols), prev8(wcols), next8(wcols), rev(GROUP_W), rev(GROUP_W), prev8(GROUP_W), rev(GROUP_W)]
                 + _mix_specs(tm, S, "bwd"),
        out_specs=[const2((E, D)), rev(wcols), rev(GROUP_W)] + dil_specs + dil_specs
                  + [const2(small), const2(small), const2(small), const2(sq), const2(sq),
                     pl.BlockSpec(ws_shape, lambda i: (0, 0, 0)), const2((GMLP_CHUNK, LANES))],
        out_shape=[jax.ShapeDtypeStruct((E, D), WIRE_DTYPE),
                   jax.ShapeDtypeStruct((S, wcols), MXU_DTYPE), jax.ShapeDtypeStruct((S, GROUP_W), MXU_DTYPE)]
                  + [_dilated_shape(S, GROUP_W, dil, MXU_DTYPE) for dil in ATTN_DILATIONS] + dil_shapes
                  + [jax.ShapeDtypeStruct(small, F32)] * 3 + [jax.ShapeDtypeStruct(sq, F32)] * 2
                  + [jax.ShapeDtypeStruct(ws_shape, F32), jax.ShapeDtypeStruct((GMLP_CHUNK, LANES), F32)],
        scratch_shapes=[pltpu.VMEM(small, F32), pltpu.VMEM(small, F32), pltpu.VMEM((GMLP_CHUNK, GROUP_W), F32),
                        _lane_scratch(tm, GROUP_W), _lane_scratch(tm, GROUP_W),
                        _lane_scratch(tm, GROUP_W), _lane_scratch(tm, GROUP_W), pltpu.VMEM((hb, GROUP_W), F32),
                        pltpu.VMEM((tm, E), F32), pltpu.VMEM(small, F32), pltpu.VMEM((E, D), F32)],
        compiler_params=_params(("arbitrary",)),
    )(dx, y, w_out, z, z, z, z_g, hs, hs, o, mp["wA"], mp["wR"], mp["vec"], mp["wa"], mp["wx"], mp["ws"], mp["bs"])


def _attn_bwd(qkv, do, lse, delta, dil, name):
    rows = qkv.shape[0]
    nb = rows // ATTN_BLOCK
    scale = 1.0 / math.sqrt(HEAD_DIM)
    B = ATTN_BLOCK
    per_step = ATTN_BLOCKS_PER_STEP
    n_steps = nb // per_step

    def body(qc_ref, qn_ref, kc_ref, kp_ref, vc_ref, vp_ref, doc_ref, don_ref, lc_ref, ln_ref, dc_ref, dn_ref,
             dq_ref, dk_ref, dv_ref, bias_ref, bias_next_ref):
        n = pl.program_id(1)

        @pl.when(n == 0)
        def _():
            bias_ref[...] = _attn_bias(dil, (B,), 2 * B)
            bias_next_ref[...] = _attn_bias(dil, (B,), B)

        masks = _head_masks((B, GROUP_W))

        def per_row(tile):
            return jnp.concatenate([jnp.max(jnp.where(masks[h], tile, _NEG), axis=-1, keepdims=True)
                                    for h in range(N_HEADS)], axis=0)

        def grads(q, dov, lse_tile, dl_tile, keys, vals, bias, dead):
            qs = _stack_heads(q, masks)
            dos = _stack_heads(dov.astype(MXU_DTYPE), masks)
            s = _mm_nt(qs, keys) * scale + bias
            if dead is not None:
                s = jnp.where(dead(s.shape), _NEG, s)
            p = jnp.exp(s - per_row(lse_tile))
            ds = (p * (_mm_nt(dos, vals) - per_row(dl_tile)) * scale).astype(MXU_DTYPE)
            return ds, _mm_tn(ds, qs), _mm_tn(p.astype(MXU_DTYPE), dos)

        for j in range(per_step):
            own = slice(j * B, (j + 1) * B)
            before = slice((j - 1) * B, j * B)
            keys = jnp.concatenate([kp_ref[...] if j == 0 else kc_ref[before], kc_ref[own]], axis=0)
            vals = jnp.concatenate([vp_ref[...] if j == 0 else vc_ref[before], vc_ref[own]], axis=0)
            dead = (lambda shape: (n == 0) & (lax.broadcasted_iota(jnp.int32, shape, 1) < B)) if j == 0 else None
            ds, dk2, dv2 = grads(qc_ref[own], doc_ref[own], lc_ref[own], dc_ref[own], keys, vals, bias_ref[...], dead)
            dq_ref[own] = _unstack_heads(jnp.dot(ds, keys, preferred_element_type=F32), masks).astype(dq_ref.dtype)
            if j > 0:
                dk_ref[before] = (dk_own + dk2[:B]).astype(dk_ref.dtype)
                dv_ref[before] = (dv_own + dv2[:B]).astype(dv_ref.dtype)
            dk_own, dv_own = dk2[B:], dv2[B:]
        last = slice((per_step - 1) * B, per_step * B)
        _, dk1, dv1 = grads(qn_ref[...], don_ref[...], ln_ref[...], dn_ref[...], kc_ref[last], vc_ref[last],
                            bias_next_ref[...], lambda shape: n == n_steps - 1)
        dk_ref[last] = (dk_own + dk1).astype(dk_ref.dtype)
        dv_ref[last] = (dv_own + dv1).astype(dv_ref.dtype)

    blk = (per_step * B, GROUP_W)
    one = (B, GROUP_W)
    nxt_idx = lambda n: jnp.minimum((n + 1) * per_step, nb - 1)
    prv_idx = lambda n: jnp.maximum(n * per_step - 1, 0)
    zcur = lambda c: pl.BlockSpec(blk, lambda r, n: (n, r * 3 + c))
    znext = lambda c: pl.BlockSpec(one, lambda r, n: (nxt_idx(n), r * 3 + c))
    zprev = lambda c: pl.BlockSpec(one, lambda r, n: (prv_idx(n), r * 3 + c))
    cur = pl.BlockSpec(blk, lambda r, n: (n, r))
    nxt = pl.BlockSpec(one, lambda r, n: (nxt_idx(n), r))
    return pl.pallas_call(
        body, name=name, grid=(dil, n_steps),
        in_specs=[zcur(0), znext(0), zcur(1), zprev(1), zcur(2), zprev(2), cur, nxt, cur, nxt, cur, nxt],
        out_specs=[cur, cur, cur],
        out_shape=[jax.ShapeDtypeStruct((rows, dil * GROUP_W), WIRE_DTYPE)] * 3,
        scratch_shapes=[pltpu.VMEM((N_HEADS * B, 2 * B), F32), pltpu.VMEM((N_HEADS * B, B), F32)],
        compiler_params=_params(("parallel", "arbitrary")),
    )(qkv, qkv, qkv, qkv, qkv, qkv, do, do, lse, lse, delta, delta)


def _inproj_bwd(x, g, dxn, dz_abc, dqkv, dz_g, w_t, name):
    S, D = x.shape
    N = w_t.shape[0]
    tm = TM_MM
    n_abc = N_ABC * GROUP_W

    def body(x_ref, g_ref, dxn_ref, dabc_ref, q1, k1, v1, q2, k2, v2, q3, k3, v3, dg_ref, w_ref,
             dx_ref, dz_ref, h_ref, dgn_ref, s4_ref, s16_ref):
        i = pl.program_id(0)

        @pl.when(i == 0)
        def _():
            dgn_ref[...] = jnp.zeros_like(dgn_ref)

        dz_ref[:, 0:n_abc] = dabc_ref[...].astype(MXU_DTYPE)
        for j, parts in enumerate(((q1, q2, q3), (k1, k2, k3), (v1, v2, v3))):
            c0 = n_abc + j * GROUP_W
            _interleave(parts[1], s4_ref, ATTN_DILATIONS[1])
            _interleave(parts[2], s16_ref, ATTN_DILATIONS[2])
            dz_ref[:, c0:c0 + GROUP_W] = (parts[0][...] + _get(s4_ref) + _get(s16_ref)).astype(MXU_DTYPE)
        dz_ref[:, n_abc + 3 * GROUP_W:] = dg_ref[...].astype(MXU_DTYPE)
        dh = jnp.dot(dz_ref[...], w_ref[...], preferred_element_type=F32)
        xv = x_ref[...]
        r = lax.rsqrt(jnp.mean(xv * xv, axis=-1, keepdims=True) + NORM_EPS)
        xn = xv * r
        gv = g_ref[...]
        h_ref[...] = (xn * gv).astype(MXU_DTYPE)
        dgn_ref[...] += _colsum(dh * xn)
        dn = dh * gv
        dx_ref[...] = dxn_ref[...] + r * (dn - xn * jnp.mean(dn * xn, axis=-1, keepdims=True))

    row = lambda w: pl.BlockSpec((tm, w), lambda i: (i, 0))
    flat = [t for p in dqkv for t in p]
    dil_specs = [_dilated_spec(tm, GROUP_W, dil) for dil in ATTN_DILATIONS for _ in range(3)]
    return pl.pallas_call(
        body, name=name, grid=(S // tm,),
        in_specs=[row(D), pl.BlockSpec((1, D), lambda i: (0, 0)), row(D), row(n_abc)] + dil_specs
                 + [row(GROUP_W), pl.BlockSpec((N, D), lambda i: (0, 0), pipeline_mode=pl.Buffered(1))],
        out_specs=[row(D), row(N), row(D), pl.BlockSpec((1, D), lambda i: (0, 0))],
        out_shape=[jax.ShapeDtypeStruct((S, D), F32), jax.ShapeDtypeStruct((S, N), MXU_DTYPE),
                   jax.ShapeDtypeStruct((S, D), MXU_DTYPE), jax.ShapeDtypeStruct((1, D), F32)],
        scratch_shapes=[_lane_scratch(tm, GROUP_W)] * 2,
        compiler_params=_params(("arbitrary",)),
    )(x, g, dxn, dz_abc, *flat, dz_g, w_t)


def _inproj_wgrad(h, dz, name):
    S, D = h.shape
    N = dz.shape[1]
    tm = TM_WGRAD
    nj = 2
    cw = N // nj
    per = N_DEV // nj
    n_loc = N // N_DEV

    def body(h_ref, dz_ref, dw_ref, acc_ref):
        i = pl.program_id(1)

        @pl.when(i == 0)
        def _():
            acc_ref[...] = jnp.zeros_like(acc_ref)

        acc_ref[...] += _mm_tn(dz_ref[...], h_ref[...])

        @pl.when(i == S // tm - 1)
        def _():
            for b in range(per):
                dw_ref[b] = acc_ref[b * n_loc:(b + 1) * n_loc, :].astype(dw_ref.dtype)

    return pl.pallas_call(
        body, name=name, grid=(nj, S // tm),
        in_specs=[pl.BlockSpec((tm, D), lambda j, i: (i, 0)), pl.BlockSpec((tm, cw), lambda j, i: (i, j))],
        out_specs=pl.BlockSpec((per, n_loc, D), lambda j, i: (j, 0, 0)),
        out_shape=jax.ShapeDtypeStruct((N_DEV, n_loc, D), WIRE_DTYPE),
        scratch_shapes=[pltpu.VMEM((cw, D), F32)],
        compiler_params=_params(("parallel", "arbitrary")),
    )(h, dz)


def _my_place():
    return lax.axis_index("x"), lax.axis_index("y"), lax.axis_index("c")


def _peer(x, y, c, k):
    px = 1 - x if k & 4 else x
    py = 1 - y if k & 2 else y
    pc = 1 - c if k & 1 else c
    return (px, py, pc), 4 * px + 2 * py + pc


HBM_SPEC = pl.BlockSpec(memory_space=pltpu.HBM)
SEM_SPEC = pl.BlockSpec(memory_space=pltpu.SEMAPHORE)
SPLIT_EFFECT = pltpu.SideEffectType.DATAFLOW_SIDE_EFFECTING
N_PEERS = N_DEV - 1


def _exchange_copies(srcs, lands, send_sems, recv_sems, whole, arrival):
    x, y, c = _my_place()
    me = 4 * x + 2 * y + c
    copies = []
    for t in range(len(srcs)):
        for k in range(1, N_DEV):
            peer, pidx = _peer(x, y, c, k)
            copies.append(pltpu.make_async_remote_copy(
                src_ref=srcs[t] if whole[t] else srcs[t].at[pidx],
                dst_ref=lands[t].at[pidx if arrival else me], send_sem=send_sems.at[t * N_PEERS + k - 1],
                recv_sem=recv_sems.at[t * N_PEERS + k - 1], device_id=peer, device_id_type=MESH))
    return copies


def _exchange_start(groups, name, after=None):
    sizes = [len(g) for g in groups]
    whole = [w for g in groups for _, w in g]
    srcs = [pltpu.with_memory_space_constraint(a, pltpu.HBM) for g in groups for a, _ in g]
    lands = [pltpu.with_memory_space_constraint(lax.empty(((N_DEV,) + a.shape) if w else a.shape, a.dtype), pltpu.HBM)
             for a, w in zip(srcs, whole)]
    n = len(srcs)
    n_g = len(groups)
    extra = [] if after is None else [after]
    n_in = 2 * n + len(extra)

    def body(*refs):
        src_refs, land_refs = refs[:n], refs[n:2 * n]
        sem_refs = refs[n_in + 2 * n:n_in + 2 * n + 2 * n_g]
        token = refs[-1]
        off = 0
        for gi, sz in enumerate(sizes):
            for send in _exchange_copies(src_refs[off:off + sz], land_refs[off:off + sz],
                                         sem_refs[2 * gi], sem_refs[2 * gi + 1], whole[off:off + sz], False):
                send.start()
            off += sz
        token[...] = jnp.zeros_like(token)

    sem_shapes = [pltpu.SemaphoreType.DMA((sz * N_PEERS,)) for sz in sizes for _ in range(2)]
    outs = pl.pallas_call(
        body, name=name,
        in_specs=[HBM_SPEC] * (2 * n) + [pl.BlockSpec(memory_space=pl.ANY)] * len(extra),
        out_specs=[HBM_SPEC] * (2 * n) + [SEM_SPEC] * (2 * n_g) + [pl.BlockSpec(memory_space=pltpu.VMEM)],
        out_shape=[pltpu.HBM(a.shape, a.dtype) for a in srcs + lands] + sem_shapes
                  + [jax.ShapeDtypeStruct((SUBLANES, LANES), F32)],
        input_output_aliases={i: i for i in range(2 * n)},
        compiler_params=pltpu.CompilerParams(has_side_effects=SPLIT_EFFECT),
    )(*srcs, *lands, *extra)
    handles, off = [], 0
    for gi, sz in enumerate(sizes):
        handles.append((outs[2 * n + 2 * gi], outs[2 * n + 2 * gi + 1], outs[off:off + sz], outs[n + off:n + off + sz],
                        whole[off:off + sz]))
        off += sz
    return handles, outs[-1]


def _exchange_wait(handle, after, name):
    send_sems, recv_sems, srcs, lands, whole = handle
    n = len(srcs)

    def body(*refs):
        src_refs, land_refs = refs[:n], refs[n:2 * n]
        for send in _exchange_copies(src_refs, land_refs, refs[2 * n], refs[2 * n + 1], whole, False):
            send.wait_send()
        for arrival in _exchange_copies(src_refs, land_refs, refs[2 * n], refs[2 * n + 1], whole, True):
            arrival.wait_recv()

    outs = pl.pallas_call(
        body, name=name,
        in_specs=[HBM_SPEC] * (2 * n) + [SEM_SPEC, SEM_SPEC, pl.BlockSpec(memory_space=pl.ANY)],
        out_specs=[HBM_SPEC] * (2 * n),
        out_shape=[pltpu.HBM(a.shape, a.dtype) for a in list(srcs) + list(lands)],
        input_output_aliases={i: i for i in range(2 * n)},
        compiler_params=pltpu.CompilerParams(has_side_effects=SPLIT_EFFECT),
    )(*srcs, *lands, send_sems, recv_sems, after)
    x, y, c = _my_place()
    me = 4 * x + 2 * y + c
    own = [s[None] if w else lax.dynamic_slice_in_dim(s, me, 1, axis=0) for s, w in zip(outs[:n], whole)]
    return [lax.dynamic_update_slice_in_dim(ld, o, me, axis=0) for ld, o in zip(outs[n:], own)]


def _sum_slots(parts, name):
    n = len(parts)

    def body(*refs):
        for p_ref, o_ref in zip(refs[:n], refs[n:]):
            acc = p_ref[0]
            for j in range(1, N_DEV):
                acc = acc + p_ref[j]
            o_ref[...] = acc

    vm = pl.BlockSpec(memory_space=pltpu.VMEM)
    return pl.pallas_call(
        body, name=name, in_specs=[vm] * n, out_specs=[vm] * n,
        out_shape=[jax.ShapeDtypeStruct(p.shape[1:], F32) for p in parts],
        compiler_params=pltpu.CompilerParams(vmem_limit_bytes=VMEM_LIMIT),
    )(*parts)


def _adamw_math(w, g, m, v):
    m = ADAM_B1 * m + (1.0 - ADAM_B1) * g
    v = ADAM_B2 * v + (1.0 - ADAM_B2) * (g * g)
    m_hat = m / (1.0 - ADAM_B1 ** ADAM_STEP)
    v_hat = v / (1.0 - ADAM_B2 ** ADAM_STEP)
    delta = -ADAM_LR * (m_hat / (jnp.sqrt(v_hat) + ADAM_EPS) + ADAM_WD * w)
    return delta, m, v


def _adamw_summed(parts, w, m, v, tr, name):
    depth, R, C = w.shape

    def body(*refs):
        p_refs = refs[:depth]
        w_ref, m_ref, v_ref, g_ref, d_ref, nm_ref, nv_ref = refs[depth:]
        lay = pl.program_id(0)
        for l in range(depth):
            @pl.when(lay == l)
            def _(p_ref=p_refs[l]):
                g = p_ref[0].astype(F32)
                for j in range(1, N_DEV):
                    g = g + p_ref[j].astype(F32)
                g_ref[0] = g
        d_ref[0], nm_ref[0], nv_ref[0] = _adamw_math(w_ref[0], g_ref[0], m_ref[0], v_ref[0])

    part_spec = lambda l: pl.BlockSpec((N_DEV, tr, C), lambda lay, i: (0, jnp.where(lay == l, i, 0), 0))
    row = pl.BlockSpec((1, tr, C), lambda lay, i: (lay, i, 0))
    return pl.pallas_call(
        body, name=name, grid=(depth, R // tr),
        in_specs=[part_spec(l) for l in range(depth)] + [row, row, row],
        out_specs=[row] * 4, out_shape=[jax.ShapeDtypeStruct((depth, R, C), F32)] * 4,
        compiler_params=_params(("arbitrary", "arbitrary")),
    )(*parts, w, m, v)


def _adamw_small(w, g, m, v, name):
    def body(w_ref, g_ref, m_ref, v_ref, d_ref, nm_ref, nv_ref):
        d_ref[...], nm_ref[...], nv_ref[...] = _adamw_math(w_ref[...], g_ref[...], m_ref[...], v_ref[...])

    vm = pl.BlockSpec(memory_space=pltpu.VMEM)
    return pl.pallas_call(
        body, name=name, in_specs=[vm] * 4, out_specs=[vm] * 3,
        out_shape=[jax.ShapeDtypeStruct(w.shape, F32)] * 3,
        compiler_params=pltpu.CompilerParams(vmem_limit_bytes=VMEM_LIMIT),
    )(w, g, m, v)


def _pack(arrays):
    flat = jnp.concatenate([a.reshape(-1) for a in arrays])
    pad = (-flat.shape[0]) % (SUBLANES * LANES)
    return jnp.pad(flat, (0, pad)).reshape(-1, LANES)


def _unpack(buf, like):
    flat = buf.reshape(-1)
    out, off = [], 0
    for a in like:
        out.append(flat[off:off + a.size].reshape(a.shape))
        off += a.size
    return out


def _block_diag(w):
    eye = jnp.eye(N_HEADS, dtype=w.dtype)
    return jnp.einsum('hij,hk->hikj', w, eye).reshape(GROUP_W, GROUP_W)


def _diag_blocks(w):
    return jnp.einsum('hihj->hij', w.reshape(N_HEADS, HEAD_DIM, N_HEADS, HEAD_DIM))


def _pad_rows(a):
    return jnp.pad(a, ((0, SUBLANES - a.shape[0]), (0, 0)))


def _mixer_params(l, conv_a_w, conv_r_w, conv_r_b, lru_wa, lru_ba, lru_wx, lru_bx, lru_lambda, gmlp_norm_g,
                  gmlp_ws, gmlp_bs):
    tril = jnp.tril(jnp.ones((GMLP_CHUNK, GMLP_CHUNK), dtype=bool))
    vec = jnp.stack([conv_r_b[l], lru_ba[l], lru_bx[l], lru_lambda[l], gmlp_norm_g[l]])
    return {
        "wA": _pad_rows(conv_a_w[l]), "wR": _pad_rows(conv_r_w[l]), "vec": _pad_rows(vec),
        "wa": _block_diag(lru_wa[l]).astype(MXU_DTYPE), "wx": _block_diag(lru_wx[l]).astype(MXU_DTYPE),
        "ws": jnp.where(tril[None], gmlp_ws[l], 0.0).astype(MXU_DTYPE),
        "bs": jnp.repeat(jnp.transpose(gmlp_bs[l]), HEAD_DIM, axis=1),
    }


MIXER_NAMES = ("conv_a_w", "conv_r_w", "conv_r_b", "lru_wa", "lru_ba", "lru_wx", "lru_bx", "lru_lambda",
               "gmlp_norm_g", "gmlp_ws", "gmlp_bs")
SMALL_NAMES = ("norm_g",) + MIXER_NAMES + ("final_g",)


def _local_step(x, loss_target, norm_g, get_w_in, get_w_out, emit_early, emit_late, conv_a_w, conv_r_w, conv_r_b,
                lru_wa, lru_ba, lru_wx, lru_bx, lru_lambda, gmlp_norm_g, gmlp_ws, gmlp_bs, final_g):
    depth = norm_g.shape[0]
    D = x.shape[1]
    small = (conv_a_w, conv_r_w, conv_r_b, lru_wa, lru_ba, lru_wx, lru_bx, lru_lambda, gmlp_norm_g, gmlp_ws, gmlp_bs)
    saved = []
    for l in range(depth):
        mp = _mixer_params(l, *small)
        w_in_l = get_w_in(l, x)
        z, z_g, *qkv, y_abc, hs = _inproj_mix_fwd(x, norm_g[l].reshape(1, D), w_in_l, mp, f"inproj_mix_fwd_{l}")
        attn =[_attn_fwd(qkv[p], dil, f"attn_fwd_d{dil}_{l}") for p, dil in enumerate(ATTN_DILATIONS)]
        w_out_l = get_w_out(l, y_abc)
        x_new, y, o, *lse = _outproj(x, z_g, y_abc, attn, w_out_l, f"outproj_{l}")
        saved.append((x, z, z_g, qkv, hs, y, o, lse, mp, w_in_l, w_out_l))
        x = x_new
    dx, loss, d_final_g = _loss_head(x, final_g.reshape(1, D), loss_target, "loss_head")
    token = None
    for l in reversed(range(depth)):
        x_l, z, z_g, qkv, hs, y, o, lse, mp, w_in_l, w_out_l = saved[l]
        if token is not None:
            mp = dict(mp, vec=mp["vec"] + token[0, 0])
        (dw_out, dz_abc, dz_g, do1, do4, do16, dl1, dl4, dl16, dwA, dwR, dvec, dwa, dwx, dws, dbs) = _outproj_mix_bwd(
            dx, y, w_out_l, z, z_g, hs, o, mp, f"outproj_mix_bwd_{l}")
        token = emit_early(l, dw_out, [
            dwA[:conv_a_w.shape[1]], dwR[:conv_r_w.shape[1]], dvec[0], _diag_blocks(dwa), dvec[1], _diag_blocks(dwx),
            dvec[2], dvec[3], dvec[4], dws, jnp.transpose(dbs[:, :N_HEADS])])
        g_row = norm_g[l].reshape(1, D)
        if token is not None:
            g_row = g_row + token[0, 0]
        dqkv = [_attn_bwd(qkv[p], do, lse[p], dl, dil, f"attn_bwd_d{dil}_{l}")
                for p, (dil, do, dl) in enumerate(zip(ATTN_DILATIONS, (do1, do4, do16), (dl1, dl4, dl16)))]
        dx, dz, h, dng = _inproj_bwd(x_l, g_row, dx, dz_abc, dqkv, dz_g, w_in_l, f"inproj_bwd_{l}")
        dw_in = _inproj_wgrad(h, dz, f"inproj_wgrad_{l}")
        token = emit_late(l, dw_in, [dng[0]] + ([d_final_g[0]] if l == depth - 1 else []))
    return loss[0, 0], dx
WEIGHT_NAMES = ("norm_g", "w_in", "conv_a_w", "conv_r_w", "conv_r_b", "lru_wa", "lru_ba", "lru_wx", "lru_bx",
                "lru_lambda", "gmlp_norm_g", "gmlp_ws", "gmlp_bs", "w_out", "final_g")


def kernel(x, norm_g, w_in, conv_a_w, conv_r_w, conv_r_b, lru_wa, lru_ba, lru_wx, lru_bx, lru_lambda, gmlp_norm_g, gmlp_ws, gmlp_bs, w_out, final_g, loss_target, m_norm_g, m_w_in, m_conv_a_w, m_conv_r_w, m_conv_r_b, m_lru_wa, m_lru_ba, m_lru_wx, m_lru_bx, m_lru_lambda, m_gmlp_norm_g, m_gmlp_ws, m_gmlp_bs, m_w_out, m_final_g, v_norm_g, v_w_in, v_conv_a_w, v_conv_r_w, v_conv_r_b, v_lru_wa, v_lru_ba, v_lru_wx, v_lru_bx, v_lru_lambda, v_gmlp_norm_g, v_gmlp_ws, v_gmlp_bs, v_w_out, v_final_g):
    w = dict(norm_g=norm_g, w_in=w_in, conv_a_w=conv_a_w, conv_r_w=conv_r_w, conv_r_b=conv_r_b, lru_wa=lru_wa,
             lru_ba=lru_ba, lru_wx=lru_wx, lru_bx=lru_bx, lru_lambda=lru_lambda, gmlp_norm_g=gmlp_norm_g,
             gmlp_ws=gmlp_ws, gmlp_bs=gmlp_bs, w_out=w_out, final_g=final_g)
    m = dict(norm_g=m_norm_g, w_in=m_w_in, conv_a_w=m_conv_a_w, conv_r_w=m_conv_r_w, conv_r_b=m_conv_r_b,
             lru_wa=m_lru_wa, lru_ba=m_lru_ba, lru_wx=m_lru_wx, lru_bx=m_lru_bx, lru_lambda=m_lru_lambda,
             gmlp_norm_g=m_gmlp_norm_g, gmlp_ws=m_gmlp_ws, gmlp_bs=m_gmlp_bs, w_out=m_w_out, final_g=m_final_g)
    v = dict(norm_g=v_norm_g, w_in=v_w_in, conv_a_w=v_conv_a_w, conv_r_w=v_conv_r_w, conv_r_b=v_conv_r_b,
             lru_wa=v_lru_wa, lru_ba=v_lru_ba, lru_wx=v_lru_wx, lru_bx=v_lru_bx, lru_lambda=v_lru_lambda,
             gmlp_norm_g=v_gmlp_norm_g, gmlp_ws=v_gmlp_ws, gmlp_bs=v_gmlp_bs, w_out=v_w_out, final_g=v_final_g)
    depth, D, n_loc = w_in.shape
    e_loc = w_out.shape[1]
    cx, cy, cc = _my_place()
    me = 4 * cx + 2 * cy + cc

    transposed = lambda a: jnp.transpose(a, (0, 2, 1))
    w_in_t, m_w_in_t, v_w_in_t = transposed(w_in), transposed(m_w_in), transposed(v_w_in)
    w_in_w, w_out_w = w_in_t.astype(MXU_DTYPE), w_out.astype(MXU_DTYPE)
    c_loc = conv_a_w.shape[2]
    taps = (conv_a_w, conv_r_w)
    first, _ = _exchange_start([[(w_in_w[0], True), (_pack(taps), True)], [(w_out_w[0], True)]], "gather_start_first")
    full_in = lambda g: g.reshape(N_DEV * n_loc, D)
    full_out = lambda g: g.reshape(N_DEV * e_loc, D)

    g_in0, g_taps = _exchange_wait(first[0], x, "gather_wait_in_0")
    groups = [[(w_in_w[l], True), (w_out_w[l], True)] for l in range(1, depth)]
    gathers, _ = _exchange_start(groups, "gather_start_rest", after=g_taps)
    g_taps = g_taps.reshape(N_DEV, -1)
    conv_full, off = [], 0
    for a in taps:
        part = g_taps[:, off:off + a.size].reshape((N_DEV,) + a.shape)
        conv_full.append(jnp.transpose(part, (1, 2, 0, 3)).reshape(a.shape[:2] + (N_DEV * c_loc,)))
        off += a.size
    conv_a_full, conv_r_full = conv_full
    later = {}

    def get_w_in(l, after):
        if l == 0:
            return full_in(g_in0)
        g_in, later[l] = _exchange_wait(gathers[l - 1], after, f"gather_wait_{l}")
        return full_in(g_in)

    def get_w_out(l, after):
        if l == 0:
            return full_out(_exchange_wait(first[1], after, "gather_wait_out_0")[0])
        return full_out(later[l])

    early, late, last_token = {}, {}, [None]

    def emit_early(l, dw_out, mixer_grads):
        handles, token = _exchange_start(
            [[(dw_out.reshape(N_DEV, e_loc, D), False), (_pack(mixer_grads), True)]], f"early_start_{l}")
        early[l] = (handles[0], mixer_grads)
        return token

    def emit_late(l, dw_in, norm_grads):
        handles, token = _exchange_start([[(_pack(norm_grads), True)], [(dw_in, False)]], f"late_start_{l}")
        late[l] = (handles[0], handles[1], norm_grads)
        last_token[0] = token
        return token

    loss, grad_x = _local_step(
        x[0], loss_target[0], norm_g, get_w_in, get_w_out, emit_early, emit_late, conv_a_full, conv_r_full, conv_r_b,
        lru_wa, lru_ba, lru_wx, lru_bx, lru_lambda, gmlp_norm_g, gmlp_ws, gmlp_bs, final_g)
    loss = lax.psum(loss, ("x", "y", "c"))

    r_in, r_out, small_parts = {}, {}, []
    for l in reversed(range(depth)):
        r_out[l], r_mix = _exchange_wait(early[l][0], last_token[0], f"early_wait_{l}")
        (r_norm,) = _exchange_wait(late[l][0], last_token[0], f"late_wait_norm_{l}")
        small_parts += [r_mix, r_norm]
        if l > 0:
            (r_in[l],) = _exchange_wait(late[l][1], last_token[0], f"late_wait_{l}")
    big = {"w_out": _adamw_summed([r_out[l] for l in range(depth)], w_out, m_w_out, v_w_out, 128, "adamw_w_out")}

    sums = _sum_slots(small_parts, "sum_small_grads")
    by_layer = {}
    for i, l in enumerate(reversed(range(depth))):
        mix = _unpack(sums[2 * i], early[l][1])
        nrm = _unpack(sums[2 * i + 1], late[l][2])
        by_layer[l] = dict(zip(MIXER_NAMES, mix), norm_g=nrm[0])
        if l == depth - 1:
            g_final = nrm[1]
    g_small = {k: jnp.stack([by_layer[l][k] for l in range(depth)]) for k in ("norm_g",) + MIXER_NAMES}
    g_small["final_g"] = g_final
    for k in ("conv_a_w", "conv_r_w"):
        g_small[k] = lax.dynamic_slice_in_dim(g_small[k], me * c_loc, c_loc, axis=2)
    packs = [_pack([d[k] for k in SMALL_NAMES]) for d in (w, g_small, m, v)]
    res = _adamw_small(*packs, "adamw_small")
    like = [w[k] for k in SMALL_NAMES]
    d_s, m_s, v_s = (dict(zip(SMALL_NAMES, _unpack(r, like))) for r in res)

    (r_in[0],) = _exchange_wait(late[0][1], res[0], "late_wait_0")
    big["w_in"] = [transposed(a) for a in _adamw_summed(
        [r_in[l] for l in range(depth)], w_in_t, m_w_in_t, v_w_in_t, n_loc // 2, "adamw_w_in")]

    grad, delta, new_m, new_v = {}, {}, {}, {}
    for k in WEIGHT_NAMES:
        if k in big:
            grad[k], delta[k], new_m[k], new_v[k] = big[k]
        else:
            grad[k], delta[k], new_m[k], new_v[k] = g_small[k], d_s[k], m_s[k], v_s[k]
    return (loss, grad_x[None], *[grad[k] for k in WEIGHT_NAMES], *[delta[k] for k in WEIGHT_NAMES],
            *[new_m[k] for k in WEIGHT_NAMES], *[new_v[k] for k in WEIGHT_NAMES])
```

```python
import functools
import math

import jax
import jax.numpy as jnp
from jax import lax
from jax.experimental import pallas as pl
from jax.experimental.pallas import tpu as pltpu

F32 = jnp.float32
MXU_DTYPE = jnp.bfloat16
WIRE_DTYPE = jnp.bfloat16
MESH = pl.DeviceIdType.MESH

N_DEV = 8
GROUP_W = 256
N_HEADS = 4
HEAD_DIM = 64
N_CHUNKS = 13
N_ABC = 9
GMLP_CHUNK = 128
ATTN_BLOCK = 128
ATTN_BLOCKS_PER_STEP = 4
ATTN_DILATIONS = (1, 4, 16)
NORM_EPS = 1e-6
RG_C = 8.0
SUBLANES = 8
LANES = 128
VMEM_LIMIT = 56 * 1024 * 1024

ADAM_LR = 0.001
ADAM_B1 = 0.9
ADAM_B2 = 0.999
ADAM_EPS = 1e-08
ADAM_WD = 0.01
ADAM_STEP = 10

TM_MIX = 512
TM_MM = 512
TM_WGRAD = 1024


def _params(sem, vmem=VMEM_LIMIT):
    return pltpu.CompilerParams(dimension_semantics=sem, vmem_limit_bytes=vmem)


def _mm(a, b):
    return jnp.dot(a.astype(MXU_DTYPE), b.astype(MXU_DTYPE), preferred_element_type=F32)


def _mm_tn(a, b):
    return lax.dot_general(a.astype(MXU_DTYPE), b.astype(MXU_DTYPE), (((0,), (0,)), ((), ())),
                           preferred_element_type=F32)


def _mm_nt(a, b):
    return lax.dot_general(a.astype(MXU_DTYPE), b.astype(MXU_DTYPE), (((1,), (1,)), ((), ())),
                           preferred_element_type=F32)


def _sigmoid(x):
    return 0.5 * jnp.tanh(0.5 * x) + 0.5


def _silu_and_grad(x):
    s = _sigmoid(x)
    return x * s, s * (1.0 + x * (1.0 - s))


_GELU_K = math.sqrt(2.0 / math.pi)
_GELU_C = 0.044715


def _gelu_and_grad(x):
    x2 = x * x
    t = jnp.tanh(_GELU_K * (x + _GELU_C * x * x2))
    val = 0.5 * x * (1.0 + t)
    grad = 0.5 * (1.0 + t) + 0.5 * x * (1.0 - t * t) * (_GELU_K * (1.0 + 3.0 * _GELU_C * x2))
    return val, grad


def _gelu(x):
    return 0.5 * x * (1.0 + jnp.tanh(_GELU_K * (x + _GELU_C * x * x * x)))


def _expm1_nonpos(u):
    poly = 1.0 / math.factorial(9)
    for k in range(8, 0, -1):
        poly = poly * u + 1.0 / math.factorial(k)
    return jnp.where(u > -0.25, poly * u, jnp.exp(u) - 1.0)


def _softplus(x):
    return jnp.maximum(x, 0.0) + jnp.log(1.0 + jnp.exp(-jnp.abs(x)))


def _shift_down(t, halo, k):
    rolled = pltpu.roll(t, k, 0)
    hr = pltpu.roll(halo, k, 0)
    row = lax.broadcasted_iota(jnp.int32, halo.shape, 0)
    first = jnp.where(row < k, hr, rolled[0:SUBLANES])
    return jnp.concatenate([first, rolled[SUBLANES:]], axis=0)


def _shift_up(t, nxt, k):
    tm = t.shape[0]
    rolled = pltpu.roll(t, tm - k, 0)
    nr = pltpu.roll(nxt, SUBLANES - k, 0)
    row = lax.broadcasted_iota(jnp.int32, nxt.shape, 0)
    last = jnp.where(row >= SUBLANES - k, nr, rolled[tm - SUBLANES:tm])
    return jnp.concatenate([rolled[:tm - SUBLANES], last], axis=0)


def _scan_fwd(a, b):
    tm = a.shape[0]
    row = lax.broadcasted_iota(jnp.int32, a.shape, 0)
    s = 1
    while s < tm:
        a_s = pltpu.roll(a, s, 0)
        b_s = pltpu.roll(b, s, 0)
        m = row >= s
        b = jnp.where(m, a * b_s + b, b)
        a = jnp.where(m, a * a_s, a)
        s *= 2
    return a, b


def _scan_rev(a, g):
    tm = a.shape[0]
    row = lax.broadcasted_iota(jnp.int32, a.shape, 0)
    s = 1
    while s < tm:
        a_s = pltpu.roll(a, tm - s, 0)
        g_s = pltpu.roll(g, tm - s, 0)
        m = row < tm - s
        g = jnp.where(m, g + a * g_s, g)
        a = jnp.where(m, a * a_s, a)
        s *= 2
    return g


def _group_rows(scr_ref, row, n_groups):
    return jnp.concatenate([scr_ref[pl.ds(c, 1), pl.ds(row, n_groups, stride=SUBLANES), :][0]
                            for c in range(scr_ref.shape[0])], axis=1)


def _spread_rows(rows_ref, n_groups, w):
    return jnp.concatenate([jnp.broadcast_to(rows_ref[g:g + 1, :], (SUBLANES, w)) for g in range(n_groups)], axis=0)


def _scan_groups(a, b, reverse):
    tm, w = a.shape
    shape3 = (tm // SUBLANES, SUBLANES, w)
    a3, b3 = a.reshape(shape3), b.reshape(shape3)
    sub = lax.broadcasted_iota(jnp.int32, shape3, 1)
    s = 1
    while s < SUBLANES:
        shift = SUBLANES - s if reverse else s
        a_s = pltpu.roll(a3, shift, 1)
        b_s = pltpu.roll(b3, shift, 1)
        m = (sub < SUBLANES - s) if reverse else (sub >= s)
        b3 = jnp.where(m, a3 * b_s + b3, b3)
        a3 = jnp.where(m, a3 * a_s, a3)
        s *= 2
    return a3.reshape(tm, w), b3.reshape(tm, w)


def _scan_fwd_tile(a, b, h_in, sa_ref, sb_ref, sc_ref):
    tm, w = a.shape
    n_groups = tm // SUBLANES
    a_loc, b_loc = _scan_groups(a, b, False)
    _put(sa_ref, a_loc)
    _put(sb_ref, b_loc)
    a_end, b_end = _scan_fwd(_group_rows(sa_ref, SUBLANES - 1, n_groups), _group_rows(sb_ref, SUBLANES - 1, n_groups))
    h_end = b_end + a_end * h_in
    sc_ref[...] = _shift_down(h_end, jnp.broadcast_to(h_in, (SUBLANES, w)), 1)
    return b_loc + a_loc * _spread_rows(sc_ref, n_groups, w), h_end


def _scan_rev_tile(a, g, sa_ref, sb_ref, sc_ref):
    tm, w = a.shape
    n_groups = tm // SUBLANES
    a_loc, g_loc = _scan_groups(a, g, True)
    _put(sa_ref, a_loc)
    _put(sb_ref, g_loc)
    d_first = _scan_rev(_group_rows(sa_ref, 0, n_groups), _group_rows(sb_ref, 0, n_groups))
    sc_ref[...] = _shift_up(d_first, jnp.zeros((SUBLANES, w), F32), 1)
    return g_loc + a_loc * _spread_rows(sc_ref, n_groups, w)


def _lane_scratch(tm, w):
    return pltpu.VMEM((w // LANES, tm, LANES), F32)


def _put(scr_ref, val):
    for c in range(scr_ref.shape[0]):
        scr_ref[c] = val[:, c * LANES:(c + 1) * LANES].astype(F32)


def _get(scr_ref):
    return jnp.concatenate([scr_ref[c] for c in range(scr_ref.shape[0])], axis=1)


def _deinterleave(src_ref, dst_ref, dil):
    nc, tm, _ = src_ref.shape
    w = nc * LANES
    for r in range(dil):
        for c in range(nc):
            piece = src_ref[pl.ds(c, 1), pl.ds(r, tm // dil, stride=dil), :][0] if dil > 1 else src_ref[c]
            dst_ref[:, r * w + c * LANES:r * w + (c + 1) * LANES] = piece.astype(dst_ref.dtype)


def _interleave(src_ref, dst_ref, dil):
    nc, tm, _ = dst_ref.shape
    w = nc * LANES
    for r in range(dil):
        for c in range(nc):
            dst_ref[pl.ds(c, 1), pl.ds(r, tm // dil, stride=dil), :] = (
                src_ref[:, r * w + c * LANES:r * w + (c + 1) * LANES].astype(F32)[None])


def _dilated_spec(tm, w, dil, index=lambda i: i):
    return pl.BlockSpec((tm // dil, dil * w), lambda i: (index(i), 0))


def _dilated_shape(S, w, dil, dtype):
    return jax.ShapeDtypeStruct((S // dil, dil * w), dtype)


def _head_masks(shape):
    lane = lax.broadcasted_iota(jnp.int32, shape, 1)
    return [(lane >= h * HEAD_DIM) & (lane < (h + 1) * HEAD_DIM) for h in range(N_HEADS)]


def _colsum(v):
    return jnp.sum(v, axis=0, keepdims=True)


def _conv_a(z_of, halo_of, w_ref):
    p = z_of(2) * z_of(0)
    p_h = halo_of(2) * halo_of(0)
    cv = w_ref[2:3, :] * p + w_ref[1:2, :] * _shift_down(p, p_h, 1) + w_ref[0:1, :] * _shift_down(p, p_h, 2)
    return p, p_h, cv


def _lru_gates(z_of, halo_of, wr_ref, vec_ref, wa_ref, wx_ref):
    rx = z_of(4)
    rx_h = halo_of(4)
    sh = [rx, _shift_down(rx, rx_h, 1), _shift_down(rx, rx_h, 2), _shift_down(rx, rx_h, 3)]
    xc = (wr_ref[3:4, :] * sh[0] + wr_ref[2:3, :] * sh[1] + wr_ref[1:2, :] * sh[2]
          + wr_ref[0:1, :] * sh[3] + vec_ref[0:1, :])
    ga = _sigmoid(jnp.dot(xc.astype(MXU_DTYPE), wa_ref[...], preferred_element_type=F32) + vec_ref[1:2, :])
    gi = _sigmoid(jnp.dot(xc.astype(MXU_DTYPE), wx_ref[...], preferred_element_type=F32) + vec_ref[2:3, :])
    sp = _softplus(-vec_ref[3:4, :])
    log_a = (-RG_C * ga) * sp
    a = jnp.exp(log_a)
    mult = jnp.sqrt(-_expm1_nonpos(2.0 * log_a))
    return xc, sh, ga, gi, a, mult, sp


def _gmlp_fwd(z_of, vec_ref, ws_ref, bs_ref, tm):
    u = _gelu(z_of(6))
    gv = _gelu(z_of(7))
    rr = lax.rsqrt(jnp.mean(gv * gv, axis=-1, keepdims=True) + NORM_EPS)
    vn = (gv * rr) * vec_ref[4:5, :]
    masks = _head_masks((GMLP_CHUNK, GROUP_W))
    parts = []
    for c in range(tm // GMLP_CHUNK):
        vc = vn[c * GMLP_CHUNK:(c + 1) * GMLP_CHUNK].astype(MXU_DTYPE)
        acc = bs_ref[...]
        for h in range(N_HEADS):
            acc = acc + jnp.where(masks[h], jnp.dot(ws_ref[h], vc, preferred_element_type=F32), 0.0)
        parts.append(acc)
    return u, gv, rr, vn, jnp.concatenate(parts, axis=0)


def _mix_specs(tm, S, order):
    const2 = lambda shape: pl.BlockSpec(shape, lambda i: (0, 0))
    return [const2((SUBLANES, GROUP_W)), const2((SUBLANES, GROUP_W)), const2((SUBLANES, GROUP_W)),
            const2((GROUP_W, GROUP_W)), const2((GROUP_W, GROUP_W)),
            pl.BlockSpec((N_HEADS, GMLP_CHUNK, GMLP_CHUNK), lambda i: (0, 0, 0)),
            const2((GMLP_CHUNK, GROUP_W))]


def _inproj_mix_fwd(x, g, w_t, mp, name):
    S, D = x.shape
    N = w_t.shape[0]
    tm = TM_MIX
    hb = tm // SUBLANES
    n_abc = N_ABC * GROUP_W
    n_qkv = 3 * GROUP_W

    def body(x_ref, g_ref, w_ref, wA_ref, wR_ref, vec_ref, wa_ref, wx_ref, ws_ref, bs_ref,
             z_ref, zg_ref, q1_ref, q4_ref, q16_ref, y_ref, h_ref,
             qkv_ref, halo_ref, carry_ref, sa_ref, sb_ref, sc_ref):
        @pl.when(pl.program_id(0) == 0)
        def _():
            halo_ref[...] = jnp.zeros_like(halo_ref)
            carry_ref[...] = jnp.zeros_like(carry_ref)

        xv = x_ref[...]
        r = lax.rsqrt(jnp.mean(xv * xv, axis=-1, keepdims=True) + NORM_EPS)
        hn = ((xv * r) * g_ref[...]).astype(MXU_DTYPE)
        z_ref[...] = _mm_nt(hn, w_ref[0:n_abc, :])
        _put(qkv_ref, _mm_nt(hn, w_ref[n_abc:n_abc + n_qkv, :]))
        zg_ref[...] = _mm_nt(hn, w_ref[n_abc + n_qkv:, :])
        for dil, ref in zip(ATTN_DILATIONS, (q1_ref, q4_ref, q16_ref)):
            _deinterleave(qkv_ref, ref, dil)

        z_of = lambda c: z_ref[:, c * GROUP_W:(c + 1) * GROUP_W]
        halo_of = lambda c: halo_ref[:, c * GROUP_W:(c + 1) * GROUP_W]

        _, _, cv = _conv_a(z_of, halo_of, wA_ref)
        y_ref[:, 0:GROUP_W] = (z_of(1) * cv * _silu_and_grad(z_of(3))[0]).astype(y_ref.dtype)

        xc, _, _, gi, a, mult, _ = _lru_gates(z_of, halo_of, wR_ref, vec_ref, wa_ref, wx_ref)
        b = mult * (gi * xc)
        h, h_end = _scan_fwd_tile(a, b, carry_ref[SUBLANES - 1:SUBLANES, :], sa_ref, sb_ref, sc_ref)
        h_ref[...] = h
        carry_ref[...] = h_end[hb - SUBLANES:hb]
        y_ref[:, GROUP_W:2 * GROUP_W] = (h * _silu_and_grad(z_of(5))[0]).astype(y_ref.dtype)

        u, _, _, _, sp = _gmlp_fwd(z_of, vec_ref, ws_ref, bs_ref, tm)
        y_ref[:, 2 * GROUP_W:3 * GROUP_W] = (u * sp * _silu_and_grad(z_of(8))[0]).astype(y_ref.dtype)
        halo_ref[...] = z_ref[tm - SUBLANES:tm, :]

    row = lambda wd: pl.BlockSpec((tm, wd), lambda i: (i, 0))
    return pl.pallas_call(
        body, name=name, grid=(S // tm,),
        in_specs=[row(D), pl.BlockSpec((1, D), lambda i: (0, 0)),
                  pl.BlockSpec((N, D), lambda i: (0, 0), pipeline_mode=pl.Buffered(1))] + _mix_specs(tm, S, "fwd"),
        out_specs=[row(n_abc), row(GROUP_W)] + [_dilated_spec(tm, n_qkv, dil) for dil in ATTN_DILATIONS]
                  + [row(3 * GROUP_W), row(GROUP_W)],
        out_shape=[jax.ShapeDtypeStruct((S, n_abc), F32), jax.ShapeDtypeStruct((S, GROUP_W), F32)]
                  + [_dilated_shape(S, n_qkv, dil, MXU_DTYPE) for dil in ATTN_DILATIONS]
                  + [jax.ShapeDtypeStruct((S, 3 * GROUP_W), MXU_DTYPE), jax.ShapeDtypeStruct((S, GROUP_W), F32)],
        scratch_shapes=[_lane_scratch(tm, n_qkv), pltpu.VMEM((SUBLANES, n_abc), F32),
                        pltpu.VMEM((SUBLANES, GROUP_W), F32), _lane_scratch(tm, GROUP_W), _lane_scratch(tm, GROUP_W),
                        pltpu.VMEM((hb, GROUP_W), F32)],
        compiler_params=_params(("arbitrary",)),
    )(x, g, w_t, mp["wA"], mp["wR"], mp["vec"], mp["wa"], mp["wx"], mp["ws"], mp["bs"])


_NEG = -1e30


def _slope(h):
    return 2.0 ** (-8.0 * (h + 1) / N_HEADS)


def _attn_bias(dil, offsets, n_keys):
    shape = (ATTN_BLOCK, n_keys)
    qi = lax.broadcasted_iota(jnp.int32, shape, 0)
    ki = lax.broadcasted_iota(jnp.int32, shape, 1)
    blocks = []
    for f in offsets:
        delta = qi + f - ki
        valid = (delta >= 0) & (delta <= ATTN_BLOCK)
        dist = (delta * dil).astype(F32)
        for h in range(N_HEADS):
            blocks.append(jnp.where(valid, -_slope(h) * dist, _NEG))
    return jnp.concatenate(blocks, axis=0)


def _stack_heads(t, masks):
    return jnp.concatenate([jnp.where(m, t, jnp.zeros_like(t)) for m in masks], axis=0)


def _unstack_heads(t4, masks, base=0):
    out = t4[base * ATTN_BLOCK:(base + 1) * ATTN_BLOCK]
    for h in range(1, N_HEADS):
        out = jnp.where(masks[h], t4[(base + h) * ATTN_BLOCK:(base + h + 1) * ATTN_BLOCK], out)
    return out


def _attn_fwd(qkv, dil, name):
    rows = qkv.shape[0]
    nb = rows // ATTN_BLOCK
    scale = 1.0 / math.sqrt(HEAD_DIM)
    B = ATTN_BLOCK
    per_step = ATTN_BLOCKS_PER_STEP

    def body(q_ref, kc_ref, kp_ref, vc_ref, vp_ref, o_ref, l_ref, bias_ref):
        n = pl.program_id(1)

        @pl.when(n == 0)
        def _():
            bias_ref[...] = _attn_bias(dil, (B,), 2 * B)

        masks = _head_masks((B, GROUP_W))
        for j in range(per_step):
            own = slice(j * B, (j + 1) * B)
            before = slice((j - 1) * B, j * B)
            qs = _stack_heads(q_ref[own], masks)
            keys = jnp.concatenate([kp_ref[...] if j == 0 else kc_ref[before], kc_ref[own]], axis=0)
            vals = jnp.concatenate([vp_ref[...] if j == 0 else vc_ref[before], vc_ref[own]], axis=0)
            s = _mm_nt(qs, keys) * scale + bias_ref[...]
            if j == 0:
                key_col = lax.broadcasted_iota(jnp.int32, s.shape, 1)
                s = jnp.where((n == 0) & (key_col < B), _NEG, s)
            m = jnp.max(s, axis=-1, keepdims=True)
            p = jnp.exp(s - m)
            l = jnp.sum(p, axis=-1, keepdims=True)
            o4 = jnp.dot(p.astype(MXU_DTYPE), vals, preferred_element_type=F32)
            o_ref[own] = _unstack_heads(o4, masks) / _unstack_heads(jnp.broadcast_to(l, o4.shape), masks)
            l_ref[own] = _unstack_heads(jnp.broadcast_to(m + jnp.log(l), o4.shape), masks)

    blk = (per_step * B, GROUP_W)
    cur = lambda c: pl.BlockSpec(blk, lambda r, n: (n, r * 3 + c))
    prev = lambda c: pl.BlockSpec((B, GROUP_W), lambda r, n: (jnp.maximum(n * per_step - 1, 0), r * 3 + c))
    out = pl.BlockSpec(blk, lambda r, n: (n, r))
    return pl.pallas_call(
        body, name=name, grid=(dil, nb // per_step),
        in_specs=[cur(0), cur(1), prev(1), cur(2), prev(2)],
        out_specs=[out, out],
        out_shape=[jax.ShapeDtypeStruct((rows, dil * GROUP_W), F32)] * 2,
        scratch_shapes=[pltpu.VMEM((N_HEADS * ATTN_BLOCK, 2 * ATTN_BLOCK), F32)],
        compiler_params=_params(("parallel", "arbitrary")),
    )(qkv, qkv, qkv, qkv, qkv)


def _outproj(x, z_g, y_abc, attn, w_out, name):
    S, D = x.shape
    tm = TM_MM
    n_abc = 3 * GROUP_W

    def body(x_ref, g_ref, yabc_ref, o1, l1, o2, l2, o3, l3, w_ref,
             xn_ref, y_ref, o_ref, lse1_ref, lse4_ref, lse16_ref, so2, sl2, so3, sl3, slse):
        for src, dst, dil in ((o2, so2, ATTN_DILATIONS[1]), (l2, sl2, ATTN_DILATIONS[1]),
                              (o3, so3, ATTN_DILATIONS[2]), (l3, sl3, ATTN_DILATIONS[2])):
            _interleave(src, dst, dil)
        la, lb, lc = l1[...], _get(sl2), _get(sl3)
        mx = jnp.maximum(jnp.maximum(la, lb), lc)
        ea, eb, ec = jnp.exp(la - mx), jnp.exp(lb - mx), jnp.exp(lc - mx)
        den = ea + eb + ec
        o = (ea * o1[...] + eb * _get(so2) + ec * _get(so3)) / den
        o_ref[...] = o
        _put(slse, mx + jnp.log(den))
        for dil, ref in zip(ATTN_DILATIONS, (lse1_ref, lse4_ref, lse16_ref)):
            _deinterleave(slse, ref, dil)
        y_d = o * _silu_and_grad(g_ref[...])[0]
        y_ref[:, 0:n_abc] = yabc_ref[...].astype(MXU_DTYPE)
        y_ref[:, n_abc:] = y_d.astype(MXU_DTYPE)
        xn_ref[...] = x_ref[...] + jnp.dot(y_ref[...], w_ref[...], preferred_element_type=F32)

    row = lambda w: pl.BlockSpec((tm, w), lambda i: (i, 0))
    dil_specs = [_dilated_spec(tm, GROUP_W, dil) for dil in ATTN_DILATIONS]
    (o1, l1), (o2, l2), (o3, l3) = attn
    return pl.pallas_call(
        body, name=name, grid=(S // tm,),
        in_specs=[row(D), row(GROUP_W), row(n_abc)] + [sp for sp in dil_specs for _ in range(2)]
                 + [pl.BlockSpec(w_out.shape, lambda i: (0, 0))],
        out_specs=[row(D), row(4 * GROUP_W), row(GROUP_W)] + dil_specs,
        out_shape=[jax.ShapeDtypeStruct((S, D), F32), jax.ShapeDtypeStruct((S, 4 * GROUP_W), MXU_DTYPE),
                   jax.ShapeDtypeStruct((S, GROUP_W), F32)]
                  + [_dilated_shape(S, GROUP_W, dil, F32) for dil in ATTN_DILATIONS],
        scratch_shapes=[_lane_scratch(tm, GROUP_W)] * 5,
        compiler_params=_params(("parallel",)),
    )(x, z_g, y_abc, o1, l1, o2, l2, o3, l3, w_out)


def _loss_head(x, g, target, name):
    S, D = x.shape
    tm = TM_MM

    def body(x_ref, g_ref, t_ref, dx_ref, loss_ref, dg_ref):
        i = pl.program_id(0)

        @pl.when(i == 0)
        def _():
            loss_ref[...] = jnp.zeros_like(loss_ref)
            dg_ref[...] = jnp.zeros_like(dg_ref)

        xv = x_ref[...]
        r = lax.rsqrt(jnp.mean(xv * xv, axis=-1, keepdims=True) + NORM_EPS)
        xn = xv * r
        err = xn * g_ref[...] - t_ref[...]
        per_tok = jnp.mean(err * err, axis=-1, keepdims=True)
        loss_ref[...] += 0.5 * jnp.sum(per_tok, axis=0, keepdims=True)
        dout = err * (1.0 / D)
        dg_ref[...] += _colsum(dout * xn)
        dxn = dout * g_ref[...]
        dx_ref[...] = r * (dxn - xn * jnp.mean(dxn * xn, axis=-1, keepdims=True))

    row = pl.BlockSpec((tm, D), lambda i: (i, 0))
    return pl.pallas_call(
        body, name=name, grid=(S // tm,),
        in_specs=[row, pl.BlockSpec((1, D), lambda i: (0, 0)), row],
        out_specs=[row, pl.BlockSpec((1, LANES), lambda i: (0, 0)), pl.BlockSpec((1, D), lambda i: (0, 0))],
        out_shape=[jax.ShapeDtypeStruct((S, D), F32), jax.ShapeDtypeStruct((1, LANES), F32),
                   jax.ShapeDtypeStruct((1, D), F32)],
        compiler_params=_params(("arbitrary",)),
    )(x, g, target)


def _outproj_mix_bwd(dx, y, w_out, z, z_g, hs, o, mp, name):
    S, D = dx.shape
    E = y.shape[1]
    tm = TM_MIX
    hb = tm // SUBLANES
    nT = S // tm
    last_blk = S // SUBLANES - 1
    wcols = N_ABC * GROUP_W

    def body(dx_ref, y_ref, w_ref, z_ref, zh_ref, zn_ref, zg_ref, h_ref, hh_ref, o_ref,
             wA_ref, wR_ref, vec_ref, wa_ref, wx_ref, ws_ref, bs_ref,
             dw_ref, dz_ref, dzg_ref, do1_ref, do4_ref, do16_ref, dl1_ref, dl4_ref, dl16_ref,
             dwA_ref, dwR_ref, dvec_ref, dwa_ref, dwx_ref, dws_ref, dbs_ref,
             hcarry_ref, xcarry_ref, bsacc_ref, do_ref, dl_ref, sa_ref, sb_ref, sc_ref, dy_ref, dyn_ref, acc_ref):
        i = pl.program_id(0)
        ti = nT - 1 - i

        @pl.when(i == 0)
        def _():
            acc_ref[...] = jnp.zeros_like(acc_ref)
            dyn_ref[...] = jnp.zeros_like(dyn_ref)
            hcarry_ref[...] = jnp.zeros_like(hcarry_ref)
            xcarry_ref[...] = jnp.zeros_like(xcarry_ref)
            bsacc_ref[...] = jnp.zeros_like(bsacc_ref)
            dwA_ref[...] = jnp.zeros_like(dwA_ref)
            dwR_ref[...] = jnp.zeros_like(dwR_ref)
            dvec_ref[...] = jnp.zeros_like(dvec_ref)
            dwa_ref[...] = jnp.zeros_like(dwa_ref)
            dwx_ref[...] = jnp.zeros_like(dwx_ref)
            dws_ref[...] = jnp.zeros_like(dws_ref)
            dbs_ref[...] = jnp.zeros_like(dbs_ref)

        dxb = dx_ref[...].astype(MXU_DTYPE)
        dy_ref[...] = _mm_nt(dxb, w_ref[...])
        acc_ref[...] += _mm_tn(y_ref[...], dxb)

        @pl.when(i == nT - 1)
        def _():
            dw_ref[...] = acc_ref[...].astype(dw_ref.dtype)

        has_prev = ti > 0
        has_next = i > 0
        col = lambda c: slice(c * GROUP_W, (c + 1) * GROUP_W)
        z_of = lambda c: z_ref[:, col(c)]
        halo_of = lambda c: jnp.where(has_prev, zh_ref[:, col(c)], 0.0)
        next_of = lambda c: zn_ref[:, col(c)]

        p, p_h, cv = _conv_a(z_of, halo_of, wA_ref)
        sg, dsg = _silu_and_grad(z_of(3))
        a_b = z_of(1)
        dya = dy_ref[:, col(0)]
        dcv = dya * a_b * sg
        dcv_n = jnp.where(has_next, dyn_ref[...] * next_of(1) * _silu_and_grad(next_of(3))[0], 0.0)
        dp = (wA_ref[2:3, :] * dcv + wA_ref[1:2, :] * _shift_up(dcv, dcv_n, 1)
              + wA_ref[0:1, :] * _shift_up(dcv, dcv_n, 2))
        dwA_ref[2:3, :] += _colsum(dcv * p)
        dwA_ref[1:2, :] += _colsum(dcv * _shift_down(p, p_h, 1))
        dwA_ref[0:1, :] += _colsum(dcv * _shift_down(p, p_h, 2))
        def put_dz(c, val):
            dz_ref[:, col(c)] = val.astype(dz_ref.dtype)

        put_dz(0, dp * z_of(2))
        put_dz(1, dya * cv * sg)
        put_dz(2, dp * z_of(0))
        put_dz(3, dya * a_b * cv * dsg)

        xc, sh, ga, gi, a, mult, sp = _lru_gates(z_of, halo_of, wR_ref, vec_ref, wa_ref, wx_ref)
        h = h_ref[...]
        h_prev = _shift_down(h, jnp.where(has_prev, hh_ref[...], 0.0), 1)
        sgr, dsgr = _silu_and_grad(z_of(5))
        dyb = dy_ref[:, col(1)]
        put_dz(5, dyb * h * dsgr)
        row = lax.broadcasted_iota(jnp.int32, (tm, GROUP_W), 0)
        g_in = dyb * sgr + jnp.where(row == tm - 1, hcarry_ref[0:1, :], 0.0)
        a_up = _shift_up(a, jnp.zeros((SUBLANES, GROUP_W), F32), 1)
        dH = _scan_rev_tile(a_up, g_in, sa_ref, sb_ref, sc_ref)
        hcarry_ref[...] = (a * dH)[0:SUBLANES]
        da = dH * h_prev
        gx = gi * xc
        dmult = dH * gx
        dgi = dH * mult * xc
        dxc = dH * mult * gi
        dlog_a = da * a - dmult * (a * a) / mult
        dga = dlog_a * (-RG_C * sp)
        dlam_row = _colsum(dlog_a * (-RG_C * ga)) * (-_sigmoid(-vec_ref[3:4, :]))
        dpre_a = dga * ga * (1.0 - ga)
        dpre_i = dgi * gi * (1.0 - gi)
        dwa_ref[...] += _mm_tn(xc, dpre_a)
        dwx_ref[...] += _mm_tn(xc, dpre_i)
        dxc = dxc + _mm_nt(dpre_a, wa_ref[...]) + _mm_nt(dpre_i, wx_ref[...])
        dvec_ref[0:1, :] += _colsum(dxc)
        dvec_ref[1:2, :] += _colsum(dpre_a)
        dvec_ref[2:3, :] += _colsum(dpre_i)
        dvec_ref[3:4, :] += dlam_row
        for k in range(4):
            dwR_ref[k:k + 1, :] += _colsum(dxc * sh[3 - k])
        dxc_n = xcarry_ref[...]
        put_dz(4, wR_ref[3:4, :] * dxc + wR_ref[2:3, :] * _shift_up(dxc, dxc_n, 1)
               + wR_ref[1:2, :] * _shift_up(dxc, dxc_n, 2) + wR_ref[0:1, :] * _shift_up(dxc, dxc_n, 3))
        xcarry_ref[...] = dxc[0:SUBLANES]

        c_u, c_v = z_of(6), z_of(7)
        u, du_dx = _gelu_and_grad(c_u)
        gv, dgv_dx = _gelu_and_grad(c_v)
        rr = lax.rsqrt(jnp.mean(gv * gv, axis=-1, keepdims=True) + NORM_EPS)
        xhat = gv * rr
        g_c = vec_ref[4:5, :]
        vn = xhat * g_c
        masks = _head_masks((GMLP_CHUNK, GROUP_W))
        tri_r = lax.broadcasted_iota(jnp.int32, (GMLP_CHUNK, GMLP_CHUNK), 0)
        tri_c = lax.broadcasted_iota(jnp.int32, (GMLP_CHUNK, GMLP_CHUNK), 1)
        tril = tri_r >= tri_c
        sgc, dsgc = _silu_and_grad(z_of(8))
        dyc = dy_ref[:, col(2)]
        dsp_full = dyc * u * sgc
        sp_parts, dvn_parts = [], []
        for c in range(tm // GMLP_CHUNK):
            rs = slice(c * GMLP_CHUNK, (c + 1) * GMLP_CHUNK)
            vc = vn[rs].astype(MXU_DTYPE)
            dsp_c = dsp_full[rs]
            bsacc_ref[...] += dsp_c
            acc = bs_ref[...]
            dvn_c = jnp.zeros((GMLP_CHUNK, GROUP_W), F32)
            for h in range(N_HEADS):
                w_h = ws_ref[h]
                acc = acc + jnp.where(masks[h], jnp.dot(w_h, vc, preferred_element_type=F32), 0.0)
                dsp_h = jnp.where(masks[h], dsp_c, 0.0).astype(MXU_DTYPE)
                dvn_c = dvn_c + _mm_tn(w_h, dsp_h)
                dws_ref[h] += jnp.where(tril, _mm_nt(dsp_h, vc), 0.0)
            sp_parts.append(acc)
            dvn_parts.append(dvn_c)
        spv = jnp.concatenate(sp_parts, axis=0)
        dvn = jnp.concatenate(dvn_parts, axis=0)
        put_dz(6, dyc * spv * sgc * du_dx)
        put_dz(8, dyc * u * spv * dsgc)
        dvec_ref[4:5, :] += _colsum(dvn * xhat)
        dgvn = dvn * g_c
        dgv = rr * (dgvn - xhat * jnp.mean(dgvn * xhat, axis=-1, keepdims=True))
        put_dz(7, dgv * dgv_dx)

        sgd, dsgd = _silu_and_grad(zg_ref[...])
        dyd = dy_ref[:, col(3)]
        ov = o_ref[...]
        do = dyd * sgd
        _put(do_ref, do)
        dzg_ref[...] = (dyd * ov * dsgd).astype(dzg_ref.dtype)
        prod = do * ov
        tmasks = _head_masks((tm, GROUP_W))
        dl = jnp.zeros((tm, GROUP_W), F32)
        for h in range(N_HEADS):
            dl = jnp.where(tmasks[h], jnp.sum(jnp.where(tmasks[h], prod, 0.0), axis=-1, keepdims=True), dl)
        _put(dl_ref, dl)
        for dil, d_out, l_out in zip(ATTN_DILATIONS, (do1_ref, do4_ref, do16_ref), (dl1_ref, dl4_ref, dl16_ref)):
            _deinterleave(do_ref, d_out, dil)
            _deinterleave(dl_ref, l_out, dil)

        @pl.when(i == nT - 1)
        def _():
            acc = bsacc_ref[...]
            lane = lax.broadcasted_iota(jnp.int32, (GMLP_CHUNK, LANES), 1)
            out = jnp.zeros((GMLP_CHUNK, LANES), F32)
            for h in range(N_HEADS):
                out = jnp.where(lane == h, jnp.sum(jnp.where(masks[h], acc, 0.0), axis=-1, keepdims=True), out)
            dbs_ref[...] = out

        dyn_ref[...] = dy_ref[0:SUBLANES, 0:GROUP_W]

    rev = lambda w: pl.BlockSpec((tm, w), lambda i: (nT - 1 - i, 0))
    prev8 = lambda w: pl.BlockSpec((SUBLANES, w), lambda i: (jnp.maximum((nT - 1 - i) * hb - 1, 0), 0))
    next8 = lambda w: pl.BlockSpec((SUBLANES, w), lambda i: (jnp.minimum((nT - i) * hb, last_blk), 0))
    const2 = lambda shape: pl.BlockSpec(shape, lambda i: (0, 0))
    dil_specs = [_dilated_spec(tm, GROUP_W, dil, lambda i: nT - 1 - i) for dil in ATTN_DILATIONS]
    dil_shapes = [_dilated_shape(S, GROUP_W, dil, F32) for dil in ATTN_DILATIONS]
    small = (SUBLANES, GROUP_W)
    sq = (GROUP_W, GROUP_W)
    ws_shape = (N_HEADS, GMLP_CHUNK, GMLP_CHUNK)
    return pl.pallas_call(
        body, name=name, grid=(nT,),
        in_specs=[rev(D), rev(E), pl.BlockSpec((E, D), lambda i: (0, 0), pipeline_mode=pl.Buffered(1)),
                  rev(wcols), prev8(wcols), next8(wcols), rev(GROUP_W), rev(GROUP_W), prev8(GROUP_W), rev(GROUP_W)]
                 + _mix_specs(tm, S, "bwd"),
        out_specs=[const2((E, D)), rev(wcols), rev(GROUP_W)] + dil_specs + dil_specs
                  + [const2(small), const2(small), const2(small), const2(sq), const2(sq),
                     pl.BlockSpec(ws_shape, lambda i: (0, 0, 0)), const2((GMLP_CHUNK, LANES))],
        out_shape=[jax.ShapeDtypeStruct((E, D), WIRE_DTYPE),
                   jax.ShapeDtypeStruct((S, wcols), MXU_DTYPE), jax.ShapeDtypeStruct((S, GROUP_W), MXU_DTYPE)]
                  + [_dilated_shape(S, GROUP_W, dil, MXU_DTYPE) for dil in ATTN_DILATIONS] + dil_shapes
                  + [jax.ShapeDtypeStruct(small, F32)] * 3 + [jax.ShapeDtypeStruct(sq, F32)] * 2
                  + [jax.ShapeDtypeStruct(ws_shape, F32), jax.ShapeDtypeStruct((GMLP_CHUNK, LANES), F32)],
        scratch_shapes=[pltpu.VMEM(small, F32), pltpu.VMEM(small, F32), pltpu.VMEM((GMLP_CHUNK, GROUP_W), F32),
                        _lane_scratch(tm, GROUP_W), _lane_scratch(tm, GROUP_W),
                        _lane_scratch(tm, GROUP_W), _lane_scratch(tm, GROUP_W), pltpu.VMEM((hb, GROUP_W), F32),
                        pltpu.VMEM((tm, E), F32), pltpu.VMEM(small, F32), pltpu.VMEM((E, D), F32)],
        compiler_params=_params(("arbitrary",)),
    )(dx, y, w_out, z, z, z, z_g, hs, hs, o, mp["wA"], mp["wR"], mp["vec"], mp["wa"], mp["wx"], mp["ws"], mp["bs"])


def _attn_bwd(qkv, do, lse, delta, dil, name):
    rows = qkv.shape[0]
    nb = rows // ATTN_BLOCK
    scale = 1.0 / math.sqrt(HEAD_DIM)
    B = ATTN_BLOCK
    per_step = ATTN_BLOCKS_PER_STEP
    n_steps = nb // per_step

    def body(qc_ref, qn_ref, kc_ref, kp_ref, vc_ref, vp_ref, doc_ref, don_ref, lc_ref, ln_ref, dc_ref, dn_ref,
             dq_ref, dk_ref, dv_ref, bias_ref, bias_next_ref):
        n = pl.program_id(1)

        @pl.when(n == 0)
        def _():
            bias_ref[...] = _attn_bias(dil, (B,), 2 * B)
            bias_next_ref[...] = _attn_bias(dil, (B,), B)

        masks = _head_masks((B, GROUP_W))

        def per_row(tile):
            return jnp.concatenate([jnp.max(jnp.where(masks[h], tile, _NEG), axis=-1, keepdims=True)
                                    for h in range(N_HEADS)], axis=0)

        def grads(q, dov, lse_tile, dl_tile, keys, vals, bias, dead):
            qs = _stack_heads(q, masks)
            dos = _stack_heads(dov.astype(MXU_DTYPE), masks)
            s = _mm_nt(qs, keys) * scale + bias
            if dead is not None:
                s = jnp.where(dead(s.shape), _NEG, s)
            p = jnp.exp(s - per_row(lse_tile))
            ds = (p * (_mm_nt(dos, vals) - per_row(dl_tile)) * scale).astype(MXU_DTYPE)
            return ds, _mm_tn(ds, qs), _mm_tn(p.astype(MXU_DTYPE), dos)

        for j in range(per_step):
            own = slice(j * B, (j + 1) * B)
            before = slice((j - 1) * B, j * B)
            keys = jnp.concatenate([kp_ref[...] if j == 0 else kc_ref[before], kc_ref[own]], axis=0)
            vals = jnp.concatenate([vp_ref[...] if j == 0 else vc_ref[before], vc_ref[own]], axis=0)
            dead = (lambda shape: (n == 0) & (lax.broadcasted_iota(jnp.int32, shape, 1) < B)) if j == 0 else None
            ds, dk2, dv2 = grads(qc_ref[own], doc_ref[own], lc_ref[own], dc_ref[own], keys, vals, bias_ref[...], dead)
            dq_ref[own] = _unstack_heads(jnp.dot(ds, keys, preferred_element_type=F32), masks).astype(dq_ref.dtype)
            if j > 0:
                dk_ref[before] = (dk_own + dk2[:B]).astype(dk_ref.dtype)
                dv_ref[before] = (dv_own + dv2[:B]).astype(dv_ref.dtype)
            dk_own, dv_own = dk2[B:], dv2[B:]
        last = slice((per_step - 1) * B, per_step * B)
        _, dk1, dv1 = grads(qn_ref[...], don_ref[...], ln_ref[...], dn_ref[...], kc_ref[last], vc_ref[last],
                            bias_next_ref[...], lambda shape: n == n_steps - 1)
        dk_ref[last] = (dk_own + dk1).astype(dk_ref.dtype)
        dv_ref[last] = (dv_own + dv1).astype(dv_ref.dtype)

    blk = (per_step * B, GROUP_W)
    one = (B, GROUP_W)
    nxt_idx = lambda n: jnp.minimum((n + 1) * per_step, nb - 1)
    prv_idx = lambda n: jnp.maximum(n * per_step - 1, 0)
    zcur = lambda c: pl.BlockSpec(blk, lambda r, n: (n, r * 3 + c))
    znext = lambda c: pl.BlockSpec(one, lambda r, n: (nxt_idx(n), r * 3 + c))
    zprev = lambda c: pl.BlockSpec(one, lambda r, n: (prv_idx(n), r * 3 + c))
    cur = pl.BlockSpec(blk, lambda r, n: (n, r))
    nxt = pl.BlockSpec(one, lambda r, n: (nxt_idx(n), r))
    return pl.pallas_call(
        body, name=name, grid=(dil, n_steps),
        in_specs=[zcur(0), znext(0), zcur(1), zprev(1), zcur(2), zprev(2), cur, nxt, cur, nxt, cur, nxt],
        out_specs=[cur, cur, cur],
        out_shape=[jax.ShapeDtypeStruct((rows, dil * GROUP_W), WIRE_DTYPE)] * 3,
        scratch_shapes=[pltpu.VMEM((N_HEADS * B, 2 * B), F32), pltpu.VMEM((N_HEADS * B, B), F32)],
        compiler_params=_params(("parallel", "arbitrary")),
    )(qkv, qkv, qkv, qkv, qkv, qkv, do, do, lse, lse, delta, delta)


def _inproj_bwd(x, g, dxn, dz_abc, dqkv, dz_g, w_t, name):
    S, D = x.shape
    N = w_t.shape[0]
    tm = TM_MM
    n_abc = N_ABC * GROUP_W

    def body(x_ref, g_ref, dxn_ref, dabc_ref, q1, k1, v1, q2, k2, v2, q3, k3, v3, dg_ref, w_ref,
             dx_ref, dz_ref, h_ref, dgn_ref, s4_ref, s16_ref):
        i = pl.program_id(0)

        @pl.when(i == 0)
        def _():
            dgn_ref[...] = jnp.zeros_like(dgn_ref)

        dz_ref[:, 0:n_abc] = dabc_ref[...].astype(MXU_DTYPE)
        for j, parts in enumerate(((q1, q2, q3), (k1, k2, k3), (v1, v2, v3))):
            c0 = n_abc + j * GROUP_W
            _interleave(parts[1], s4_ref, ATTN_DILATIONS[1])
            _interleave(parts[2], s16_ref, ATTN_DILATIONS[2])
            dz_ref[:, c0:c0 + GROUP_W] = (parts[0][...] + _get(s4_ref) + _get(s16_ref)).astype(MXU_DTYPE)
        dz_ref[:, n_abc + 3 * GROUP_W:] = dg_ref[...].astype(MXU_DTYPE)
        dh = jnp.dot(dz_ref[...], w_ref[...], preferred_element_type=F32)
        xv = x_ref[...]
        r = lax.rsqrt(jnp.mean(xv * xv, axis=-1, keepdims=True) + NORM_EPS)
        xn = xv * r
        gv = g_ref[...]
        h_ref[...] = (xn * gv).astype(MXU_DTYPE)
        dgn_ref[...] += _colsum(dh * xn)
        dn = dh * gv
        dx_ref[...] = dxn_ref[...] + r * (dn - xn * jnp.mean(dn * xn, axis=-1, keepdims=True))

    row = lambda w: pl.BlockSpec((tm, w), lambda i: (i, 0))
    flat = [t for p in dqkv for t in p]
    dil_specs = [_dilated_spec(tm, GROUP_W, dil) for dil in ATTN_DILATIONS for _ in range(3)]
    return pl.pallas_call(
        body, name=name, grid=(S // tm,),
        in_specs=[row(D), pl.BlockSpec((1, D), lambda i: (0, 0)), row(D), row(n_abc)] + dil_specs
                 + [row(GROUP_W), pl.BlockSpec((N, D), lambda i: (0, 0), pipeline_mode=pl.Buffered(1))],
        out_specs=[row(D), row(N), row(D), pl.BlockSpec((1, D), lambda i: (0, 0))],
        out_shape=[jax.ShapeDtypeStruct((S, D), F32), jax.ShapeDtypeStruct((S, N), MXU_DTYPE),
                   jax.ShapeDtypeStruct((S, D), MXU_DTYPE), jax.ShapeDtypeStruct((1, D), F32)],
        scratch_shapes=[_lane_scratch(tm, GROUP_W)] * 2,
        compiler_params=_params(("arbitrary",)),
    )(x, g, dxn, dz_abc, *flat, dz_g, w_t)


def _inproj_wgrad(h, dz, name):
    S, D = h.shape
    N = dz.shape[1]
    tm = TM_WGRAD
    nj = 2
    cw = N // nj
    per = N_DEV // nj
    n_loc = N // N_DEV

    def body(h_ref, dz_ref, dw_ref, acc_ref):
        i = pl.program_id(1)

        @pl.when(i == 0)
        def _():
            acc_ref[...] = jnp.zeros_like(acc_ref)

        acc_ref[...] += _mm_tn(dz_ref[...], h_ref[...])

        @pl.when(i == S // tm - 1)
        def _():
            for b in range(per):
                dw_ref[b] = acc_ref[b * n_loc:(b + 1) * n_loc, :].astype(dw_ref.dtype)

    return pl.pallas_call(
        body, name=name, grid=(nj, S // tm),
        in_specs=[pl.BlockSpec((tm, D), lambda j, i: (i, 0)), pl.BlockSpec((tm, cw), lambda j, i: (i, j))],
        out_specs=pl.BlockSpec((per, n_loc, D), lambda j, i: (j, 0, 0)),
        out_shape=jax.ShapeDtypeStruct((N_DEV, n_loc, D), WIRE_DTYPE),
        scratch_shapes=[pltpu.VMEM((cw, D), F32)],
        compiler_params=_params(("parallel", "arbitrary")),
    )(h, dz)


def _my_place():
    return lax.axis_index("x"), lax.axis_index("y"), lax.axis_index("c")


def _peer(x, y, c, k):
    px = 1 - x if k & 4 else x
    py = 1 - y if k & 2 else y
    pc = 1 - c if k & 1 else c
    return (px, py, pc), 4 * px + 2 * py + pc


HBM_SPEC = pl.BlockSpec(memory_space=pltpu.HBM)
SEM_SPEC = pl.BlockSpec(memory_space=pltpu.SEMAPHORE)
SPLIT_EFFECT = pltpu.SideEffectType.DATAFLOW_SIDE_EFFECTING
N_PEERS = N_DEV - 1


def _exchange_copies(srcs, lands, send_sems, recv_sems, whole, arrival):
    x, y, c = _my_place()
    me = 4 * x + 2 * y + c
    copies = []
    for t in range(len(srcs)):
        for k in range(1, N_DEV):
            peer, pidx = _peer(x, y, c, k)
            copies.append(pltpu.make_async_remote_copy(
                src_ref=srcs[t] if whole[t] else srcs[t].at[pidx],
                dst_ref=lands[t].at[pidx if arrival else me], send_sem=send_sems.at[t * N_PEERS + k - 1],
                recv_sem=recv_sems.at[t * N_PEERS + k - 1], device_id=peer, device_id_type=MESH))
    return copies


def _exchange_start(groups, name, after=None):
    sizes = [len(g) for g in groups]
    whole = [w for g in groups for _, w in g]
    srcs = [pltpu.with_memory_space_constraint(a, pltpu.HBM) for g in groups for a, _ in g]
    lands = [pltpu.with_memory_space_constraint(lax.empty(((N_DEV,) + a.shape) if w else a.shape, a.dtype), pltpu.HBM)
             for a, w in zip(srcs, whole)]
    n = len(srcs)
    n_g = len(groups)
    extra = [] if after is None else [after]
    n_in = 2 * n + len(extra)

    def body(*refs):
        src_refs, land_refs = refs[:n], refs[n:2 * n]
        sem_refs = refs[n_in + 2 * n:n_in + 2 * n + 2 * n_g]
        token = refs[-1]
        off = 0
        for gi, sz in enumerate(sizes):
            for send in _exchange_copies(src_refs[off:off + sz], land_refs[off:off + sz],
                                         sem_refs[2 * gi], sem_refs[2 * gi + 1], whole[off:off + sz], False):
                send.start()
            off += sz
        token[...] = jnp.zeros_like(token)

    sem_shapes = [pltpu.SemaphoreType.DMA((sz * N_PEERS,)) for sz in sizes for _ in range(2)]
    outs = pl.pallas_call(
        body, name=name,
        in_specs=[HBM_SPEC] * (2 * n) + [pl.BlockSpec(memory_space=pl.ANY)] * len(extra),
        out_specs=[HBM_SPEC] * (2 * n) + [SEM_SPEC] * (2 * n_g) + [pl.BlockSpec(memory_space=pltpu.VMEM)],
        out_shape=[pltpu.HBM(a.shape, a.dtype) for a in srcs + lands] + sem_shapes
                  + [jax.ShapeDtypeStruct((SUBLANES, LANES), F32)],
        input_output_aliases={i: i for i in range(2 * n)},
        compiler_params=pltpu.CompilerParams(has_side_effects=SPLIT_EFFECT),
    )(*srcs, *lands, *extra)
    handles, off = [], 0
    for gi, sz in enumerate(sizes):
        handles.append((outs[2 * n + 2 * gi], outs[2 * n + 2 * gi + 1], outs[off:off + sz], outs[n + off:n + off + sz],
                        whole[off:off + sz]))
        off += sz
    return handles, outs[-1]


def _exchange_wait(handle, after, name):
    send_sems, recv_sems, srcs, lands, whole = handle
    n = len(srcs)

    def body(*refs):
        src_refs, land_refs = refs[:n], refs[n:2 * n]
        for send in _exchange_copies(src_refs, land_refs, refs[2 * n], refs[2 * n + 1], whole, False):
            send.wait_send()
        for arrival in _exchange_copies(src_refs, land_refs, refs[2 * n], refs[2 * n + 1], whole, True):
            arrival.wait_recv()

    outs = pl.pallas_call(
        body, name=name,
        in_specs=[HBM_SPEC] * (2 * n) + [SEM_SPEC, SEM_SPEC, pl.BlockSpec(memory_space=pl.ANY)],
        out_specs=[HBM_SPEC] * (2 * n),
        out_shape=[pltpu.HBM(a.shape, a.dtype) for a in list(srcs) + list(lands)],
        input_output_aliases={i: i for i in range(2 * n)},
        compiler_params=pltpu.CompilerParams(has_side_effects=SPLIT_EFFECT),
    )(*srcs, *lands, send_sems, recv_sems, after)
    x, y, c = _my_place()
    me = 4 * x + 2 * y + c
    own = [s[None] if w else lax.dynamic_slice_in_dim(s, me, 1, axis=0) for s, w in zip(outs[:n], whole)]
    return [lax.dynamic_update_slice_in_dim(ld, o, me, axis=0) for ld, o in zip(outs[n:], own)]


def _sum_slots(parts, name):
    n = len(parts)

    def body(*refs):
        for p_ref, o_ref in zip(refs[:n], refs[n:]):
            acc = p_ref[0]
            for j in range(1, N_DEV):
                acc = acc + p_ref[j]
            o_ref[...] = acc

    vm = pl.BlockSpec(memory_space=pltpu.VMEM)
    return pl.pallas_call(
        body, name=name, in_specs=[vm] * n, out_specs=[vm] * n,
        out_shape=[jax.ShapeDtypeStruct(p.shape[1:], F32) for p in parts],
        compiler_params=pltpu.CompilerParams(vmem_limit_bytes=VMEM_LIMIT),
    )(*parts)


def _adamw_math(w, g, m, v):
    m = ADAM_B1 * m + (1.0 - ADAM_B1) * g
    v = ADAM_B2 * v + (1.0 - ADAM_B2) * (g * g)
    m_hat = m / (1.0 - ADAM_B1 ** ADAM_STEP)
    v_hat = v / (1.0 - ADAM_B2 ** ADAM_STEP)
    delta = -ADAM_LR * (m_hat / (jnp.sqrt(v_hat) + ADAM_EPS) + ADAM_WD * w)
    return delta, m, v


def _adamw_summed(parts, w, m, v, tr, name):
    depth, R, C = w.shape

    def body(*refs):
        p_refs = refs[:depth]
        w_ref, m_ref, v_ref, g_ref, d_ref, nm_ref, nv_ref = refs[depth:]
        lay = pl.program_id(0)
        for l in range(depth):
            @pl.when(lay == l)
            def _(p_ref=p_refs[l]):
                g = p_ref[0].astype(F32)
                for j in range(1, N_DEV):
                    g = g + p_ref[j].astype(F32)
                g_ref[0] = g
        d_ref[0], nm_ref[0], nv_ref[0] = _adamw_math(w_ref[0], g_ref[0], m_ref[0], v_ref[0])

    part_spec = lambda l: pl.BlockSpec((N_DEV, tr, C), lambda lay, i: (0, jnp.where(lay == l, i, 0), 0))
    row = pl.BlockSpec((1, tr, C), lambda lay, i: (lay, i, 0))
    return pl.pallas_call(
        body, name=name, grid=(depth, R // tr),
        in_specs=[part_spec(l) for l in range(depth)] + [row, row, row],
        out_specs=[row] * 4, out_shape=[jax.ShapeDtypeStruct((depth, R, C), F32)] * 4,
        compiler_params=_params(("arbitrary", "arbitrary")),
    )(*parts, w, m, v)


def _adamw_small(w, g, m, v, name):
    def body(w_ref, g_ref, m_ref, v_ref, d_ref, nm_ref, nv_ref):
        d_ref[...], nm_ref[...], nv_ref[...] = _adamw_math(w_ref[...], g_ref[...], m_ref[...], v_ref[...])

    vm = pl.BlockSpec(memory_space=pltpu.VMEM)
    return pl.pallas_call(
        body, name=name, in_specs=[vm] * 4, out_specs=[vm] * 3,
        out_shape=[jax.ShapeDtypeStruct(w.shape, F32)] * 3,
        compiler_params=pltpu.CompilerParams(vmem_limit_bytes=VMEM_LIMIT),
    )(w, g, m, v)


def _pack(arrays):
    flat = jnp.concatenate([a.reshape(-1) for a in arrays])
    pad = (-flat.shape[0]) % (SUBLANES * LANES)
    return jnp.pad(flat, (0, pad)).reshape(-1, LANES)


def _unpack(buf, like):
    flat = buf.reshape(-1)
    out, off = [], 0
    for a in like:
        out.append(flat[off:off + a.size].reshape(a.shape))
        off += a.size
    return out


def _block_diag(w):
    eye = jnp.eye(N_HEADS, dtype=w.dtype)
    return jnp.einsum('hij,hk->hikj', w, eye).reshape(GROUP_W, GROUP_W)


def _diag_blocks(w):
    return jnp.einsum('hihj->hij', w.reshape(N_HEADS, HEAD_DIM, N_HEADS, HEAD_DIM))


def _pad_rows(a):
    return jnp.pad(a, ((0, SUBLANES - a.shape[0]), (0, 0)))


def _mixer_params(l, conv_a_w, conv_r_w, conv_r_b, lru_wa, lru_ba, lru_wx, lru_bx, lru_lambda, gmlp_norm_g,
                  gmlp_ws, gmlp_bs):
    tril = jnp.tril(jnp.ones((GMLP_CHUNK, GMLP_CHUNK), dtype=bool))
    vec = jnp.stack([conv_r_b[l], lru_ba[l], lru_bx[l], lru_lambda[l], gmlp_norm_g[l]])
    return {
        "wA": _pad_rows(conv_a_w[l]), "wR": _pad_rows(conv_r_w[l]), "vec": _pad_rows(vec),
        "wa": _block_diag(lru_wa[l]).astype(MXU_DTYPE), "wx": _block_diag(lru_wx[l]).astype(MXU_DTYPE),
        "ws": jnp.where(tril[None], gmlp_ws[l], 0.0).astype(MXU_DTYPE),
        "bs": jnp.repeat(jnp.transpose(gmlp_bs[l]), HEAD_DIM, axis=1),
    }


MIXER_NAMES = ("conv_a_w", "conv_r_w", "conv_r_b", "lru_wa", "lru_ba", "lru_wx", "lru_bx", "lru_lambda",
               "gmlp_norm_g", "gmlp_ws", "gmlp_bs")
SMALL_NAMES = ("norm_g",) + MIXER_NAMES + ("final_g",)


def _local_step(x, loss_target, norm_g, get_w_in, get_w_out, emit_early, emit_late, conv_a_w, conv_r_w, conv_r_b,
                lru_wa, lru_ba, lru_wx, lru_bx, lru_lambda, gmlp_norm_g, gmlp_ws, gmlp_bs, final_g):
    depth = norm_g.shape[0]
    D = x.shape[1]
    small = (conv_a_w, conv_r_w, conv_r_b, lru_wa, lru_ba, lru_wx, lru_bx, lru_lambda, gmlp_norm_g, gmlp_ws, gmlp_bs)
    saved = []
    for l in range(depth):
        mp = _mixer_params(l, *small)
        w_in_l = get_w_in(l, x)
        z, z_g, *qkv, y_abc, hs = _inproj_mix_fwd(x, norm_g[l].reshape(1, D), w_in_l, mp, f"inproj_mix_fwd_{l}")
        attn =[_attn_fwd(qkv[p], dil, f"attn_fwd_d{dil}_{l}") for p, dil in enumerate(ATTN_DILATIONS)]
        w_out_l = get_w_out(l, y_abc)
        x_new, y, o, *lse = _outproj(x, z_g, y_abc, attn, w_out_l, f"outproj_{l}")
        saved.append((x, z, z_g, qkv, hs, y, o, lse, mp, w_in_l, w_out_l))
        x = x_new
    dx, loss, d_final_g = _loss_head(x, final_g.reshape(1, D), loss_target, "loss_head")
    token = None
    for l in reversed(range(depth)):
        x_l, z, z_g, qkv, hs, y, o, lse, mp, w_in_l, w_out_l = saved[l]
        if token is not None:
            mp = dict(mp, vec=mp["vec"] + token[0, 0])
        (dw_out, dz_abc, dz_g, do1, do4, do16, dl1, dl4, dl16, dwA, dwR, dvec, dwa, dwx, dws, dbs) = _outproj_mix_bwd(
            dx, y, w_out_l, z, z_g, hs, o, mp, f"outproj_mix_bwd_{l}")
        token = emit_early(l, dw_out, [
            dwA[:conv_a_w.shape[1]], dwR[:conv_r_w.shape[1]], dvec[0], _diag_blocks(dwa), dvec[1], _diag_blocks(dwx),
            dvec[2], dvec[3], dvec[4], dws, jnp.transpose(dbs[:, :N_HEADS])])
        g_row = norm_g[l].reshape(1, D)
        if token is not None:
            g_row = g_row + token[0, 0]
        dqkv = [_attn_bwd(qkv[p], do, lse[p], dl, dil, f"attn_bwd_d{dil}_{l}")
                for p, (dil, do, dl) in enumerate(zip(ATTN_DILATIONS, (do1, do4, do16), (dl1, dl4, dl16)))]
        dx, dz, h, dng = _inproj_bwd(x_l, g_row, dx, dz_abc, dqkv, dz_g, w_in_l, f"inproj_bwd_{l}")
        dw_in = _inproj_wgrad(h, dz, f"inproj_wgrad_{l}")
        token = emit_late(l, dw_in, [dng[0]] + ([d_final_g[0]] if l == depth - 1 else []))
    return loss[0, 0], dx
WEIGHT_NAMES = ("norm_g", "w_in", "conv_a_w", "conv_r_w", "conv_r_b", "lru_wa", "lru_ba", "lru_wx", "lru_bx",
                "lru_lambda", "gmlp_norm_g", "gmlp_ws", "gmlp_bs", "w_out", "final_g")


def kernel(x, norm_g, w_in, conv_a_w, conv_r_w, conv_r_b, lru_wa, lru_ba, lru_wx, lru_bx, lru_lambda, gmlp_norm_g, gmlp_ws, gmlp_bs, w_out, final_g, loss_target, m_norm_g, m_w_in, m_conv_a_w, m_conv_r_w, m_conv_r_b, m_lru_wa, m_lru_ba, m_lru_wx, m_lru_bx, m_lru_lambda, m_gmlp_norm_g, m_gmlp_ws, m_gmlp_bs, m_w_out, m_final_g, v_norm_g, v_w_in, v_conv_a_w, v_conv_r_w, v_conv_r_b, v_lru_wa, v_lru_ba, v_lru_wx, v_lru_bx, v_lru_lambda, v_gmlp_norm_g, v_gmlp_ws, v_gmlp_bs, v_w_out, v_final_g):
    w = dict(norm_g=norm_g, w_in=w_in, conv_a_w=conv_a_w, conv_r_w=conv_r_w, conv_r_b=conv_r_b, lru_wa=lru_wa,
             lru_ba=lru_ba, lru_wx=lru_wx, lru_bx=lru_bx, lru_lambda=lru_lambda, gmlp_norm_g=gmlp_norm_g,
             gmlp_ws=gmlp_ws, gmlp_bs=gmlp_bs, w_out=w_out, final_g=final_g)
    m = dict(norm_g=m_norm_g, w_in=m_w_in, conv_a_w=m_conv_a_w, conv_r_w=m_conv_r_w, conv_r_b=m_conv_r_b,
             lru_wa=m_lru_wa, lru_ba=m_lru_ba, lru_wx=m_lru_wx, lru_bx=m_lru_bx, lru_lambda=m_lru_lambda,
             gmlp_norm_g=m_gmlp_norm_g, gmlp_ws=m_gmlp_ws, gmlp_bs=m_gmlp_bs, w_out=m_w_out, final_g=m_final_g)
    v = dict(norm_g=v_norm_g, w_in=v_w_in, conv_a_w=v_conv_a_w, conv_r_w=v_conv_r_w, conv_r_b=v_conv_r_b,
             lru_wa=v_lru_wa, lru_ba=v_lru_ba, lru_wx=v_lru_wx, lru_bx=v_lru_bx, lru_lambda=v_lru_lambda,
             gmlp_norm_g=v_gmlp_norm_g, gmlp_ws=v_gmlp_ws, gmlp_bs=v_gmlp_bs, w_out=v_w_out, final_g=v_final_g)
    depth, D, n_loc = w_in.shape
    e_loc = w_out.shape[1]
    cx, cy, cc = _my_place()
    me = 4 * cx + 2 * cy + cc

    transposed = lambda a: jnp.transpose(a, (0, 2, 1))
    w_in_t, m_w_in_t, v_w_in_t = transposed(w_in), transposed(m_w_in), transposed(v_w_in)
    w_in_w, w_out_w = w_in_t.astype(MXU_DTYPE), w_out.astype(MXU_DTYPE)
    c_loc = conv_a_w.shape[2]
    taps = (conv_a_w, conv_r_w)
    first, _ = _exchange_start([[(w_in_w[0], True), (_pack(taps), True)], [(w_out_w[0], True)]], "gather_start_first")
    full_in = lambda g: g.reshape(N_DEV * n_loc, D)
    full_out = lambda g: g.reshape(N_DEV * e_loc, D)

    g_in0, g_taps = _exchange_wait(first[0], x, "gather_wait_in_0")
    groups = [[(w_in_w[l], True), (w_out_w[l], True)] for l in range(1, depth)]
    gathers, rest_token = _exchange_start(groups, "gather_start_rest", after=g_taps)
    g_taps = g_taps.reshape(N_DEV, -1) + rest_token[0, 0]
    conv_full, off = [], 0
    for a in taps:
        part = g_taps[:, off:off + a.size].reshape((N_DEV,) + a.shape)
        conv_full.append(jnp.transpose(part, (1, 2, 0, 3)).reshape(a.shape[:2] + (N_DEV * c_loc,)))
        off += a.size
    conv_a_full, conv_r_full = conv_full
    later = {}

    def get_w_in(l, after):
        if l == 0:
            return full_in(g_in0)
        g_in, later[l] = _exchange_wait(gathers[l - 1], after, f"gather_wait_{l}")
        return full_in(g_in)

    def get_w_out(l, after):
        if l == 0:
            return full_out(_exchange_wait(first[1], after, "gather_wait_out_0")[0])
        return full_out(later[l])

    early, late, last_token = {}, {}, [None]

    def emit_early(l, dw_out, mixer_grads):
        handles, token = _exchange_start(
            [[(dw_out.reshape(N_DEV, e_loc, D), False), (_pack(mixer_grads), True)]], f"early_start_{l}")
        early[l] = (handles[0], mixer_grads)
        return token

    def emit_late(l, dw_in, norm_grads):
        handles, token = _exchange_start([[(_pack(norm_grads), True)], [(dw_in, False)]], f"late_start_{l}")
        late[l] = (handles[0], handles[1], norm_grads)
        last_token[0] = token
        return token

    loss, grad_x = _local_step(
        x[0], loss_target[0], norm_g, get_w_in, get_w_out, emit_early, emit_late, conv_a_full, conv_r_full, conv_r_b,
        lru_wa, lru_ba, lru_wx, lru_bx, lru_lambda, gmlp_norm_g, gmlp_ws, gmlp_bs, final_g)
    loss = lax.psum(loss, ("x", "y", "c"))

    r_in, r_out, small_parts = {}, {}, []
    for l in reversed(range(depth)):
        r_out[l], r_mix = _exchange_wait(early[l][0], last_token[0], f"early_wait_{l}")
        (r_norm,) = _exchange_wait(late[l][0], last_token[0], f"late_wait_norm_{l}")
        small_parts += [r_mix, r_norm]
        if l > 0:
            (r_in[l],) = _exchange_wait(late[l][1], last_token[0], f"late_wait_{l}")
    big = {"w_out": _adamw_summed([r_out[l] for l in range(depth)], w_out, m_w_out, v_w_out, 128, "adamw_w_out")}

    sums = _sum_slots(small_parts, "sum_small_grads")
    by_layer = {}
    for i, l in enumerate(reversed(range(depth))):
        mix = _unpack(sums[2 * i], early[l][1])
        nrm = _unpack(sums[2 * i + 1], late[l][2])
        by_layer[l] = dict(zip(MIXER_NAMES, mix), norm_g=nrm[0])
        if l == depth - 1:
            g_final = nrm[1]
    g_small = {k: jnp.stack([by_layer[l][k] for l in range(depth)]) for k in ("norm_g",) + MIXER_NAMES}
    g_small["final_g"] = g_final
    for k in ("conv_a_w", "conv_r_w"):
        g_small[k] = lax.dynamic_slice_in_dim(g_small[k], me * c_loc, c_loc, axis=2)
    packs = [_pack([d[k] for k in SMALL_NAMES]) for d in (w, g_small, m, v)]
    res = _adamw_small(*packs, "adamw_small")
    like = [w[k] for k in SMALL_NAMES]
    d_s, m_s, v_s = (dict(zip(SMALL_NAMES, _unpack(r, like))) for r in res)

    (r_in[0],) = _exchange_wait(late[0][1], res[0], "late_wait_0")
    big["w_in"] = [transposed(a) for a in _adamw_summed(
        [r_in[l] for l in range(depth)], w_in_t, m_w_in_t, v_w_in_t, n_loc // 2, "adamw_w_in")]

    grad, delta, new_m, new_v = {}, {}, {}, {}
    for k in WEIGHT_NAMES:
        if k in big:
            grad[k], delta[k], new_m[k], new_v[k] = big[k]
        else:
            grad[k], delta[k], new_m[k], new_v[k] = g_small[k], d_s[k], m_s[k], v_s[k]
    return (loss, grad_x[None], *[grad[k] for k in WEIGHT_NAMES], *[delta[k] for k in WEIGHT_NAMES],
            *[new_m[k] for k in WEIGHT_NAMES], *[new_v[k] for k in WEIGHT_NAMES])
```

```python
import functools
import math

import jax
import jax.numpy as jnp
from jax import lax
from jax.experimental import pallas as pl
from jax.experimental.pallas import tpu as pltpu

F32 = jnp.float32
MXU_DTYPE = jnp.bfloat16
WIRE_DTYPE = jnp.bfloat16
MESH = pl.DeviceIdType.MESH

N_DEV = 8
GROUP_W = 256
N_HEADS = 4
HEAD_DIM = 64
N_CHUNKS = 13
N_ABC = 9
GMLP_CHUNK = 128
ATTN_BLOCK = 128
ATTN_FWD_BLOCKS_PER_STEP = 16
ATTN_BWD_BLOCKS_PER_STEP = 8
ATTN_DILATIONS = (1, 4, 16)
NORM_EPS = 1e-6
RG_C = 8.0
SUBLANES = 8
LANES = 128
VMEM_LIMIT = 56 * 1024 * 1024

ADAM_LR = 0.001
ADAM_B1 = 0.9
ADAM_B2 = 0.999
ADAM_EPS = 1e-08
ADAM_WD = 0.01
ADAM_STEP = 10

TM_MIX = 512
TM_MM = 512
TM_WGRAD = 1024


def _params(sem, vmem=VMEM_LIMIT):
    return pltpu.CompilerParams(dimension_semantics=sem, vmem_limit_bytes=vmem)


def _mm(a, b):
    return jnp.dot(a.astype(MXU_DTYPE), b.astype(MXU_DTYPE), preferred_element_type=F32)


def _mm_tn(a, b):
    return lax.dot_general(a.astype(MXU_DTYPE), b.astype(MXU_DTYPE), (((0,), (0,)), ((), ())),
                           preferred_element_type=F32)


def _mm_nt(a, b):
    return lax.dot_general(a.astype(MXU_DTYPE), b.astype(MXU_DTYPE), (((1,), (1,)), ((), ())),
                           preferred_element_type=F32)


def _sigmoid(x):
    return 0.5 * jnp.tanh(0.5 * x) + 0.5


def _silu_and_grad(x):
    s = _sigmoid(x)
    return x * s, s * (1.0 + x * (1.0 - s))


_GELU_K = math.sqrt(2.0 / math.pi)
_GELU_C = 0.044715


def _gelu_and_grad(x):
    x2 = x * x
    t = jnp.tanh(_GELU_K * (x + _GELU_C * x * x2))
    val = 0.5 * x * (1.0 + t)
    grad = 0.5 * (1.0 + t) + 0.5 * x * (1.0 - t * t) * (_GELU_K * (1.0 + 3.0 * _GELU_C * x2))
    return val, grad


def _gelu(x):
    return 0.5 * x * (1.0 + jnp.tanh(_GELU_K * (x + _GELU_C * x * x * x)))


def _expm1_nonpos(u):
    poly = 1.0 / math.factorial(9)
    for k in range(8, 0, -1):
        poly = poly * u + 1.0 / math.factorial(k)
    return jnp.where(u > -0.25, poly * u, jnp.exp(u) - 1.0)


def _softplus(x):
    return jnp.maximum(x, 0.0) + jnp.log(1.0 + jnp.exp(-jnp.abs(x)))


def _shift_down(t, halo, k):
    rolled = pltpu.roll(t, k, 0)
    hr = pltpu.roll(halo, k, 0)
    row = lax.broadcasted_iota(jnp.int32, halo.shape, 0)
    first = jnp.where(row < k, hr, rolled[0:SUBLANES])
    return jnp.concatenate([first, rolled[SUBLANES:]], axis=0)


def _shift_up(t, nxt, k):
    tm = t.shape[0]
    rolled = pltpu.roll(t, tm - k, 0)
    nr = pltpu.roll(nxt, SUBLANES - k, 0)
    row = lax.broadcasted_iota(jnp.int32, nxt.shape, 0)
    last = jnp.where(row >= SUBLANES - k, nr, rolled[tm - SUBLANES:tm])
    return jnp.concatenate([rolled[:tm - SUBLANES], last], axis=0)


def _scan_fwd(a, b):
    tm = a.shape[0]
    row = lax.broadcasted_iota(jnp.int32, a.shape, 0)
    s = 1
    while s < tm:
        a_s = pltpu.roll(a, s, 0)
        b_s = pltpu.roll(b, s, 0)
        m = row >= s
        b = jnp.where(m, a * b_s + b, b)
        a = jnp.where(m, a * a_s, a)
        s *= 2
    return a, b


def _scan_rev(a, g):
    tm = a.shape[0]
    row = lax.broadcasted_iota(jnp.int32, a.shape, 0)
    s = 1
    while s < tm:
        a_s = pltpu.roll(a, tm - s, 0)
        g_s = pltpu.roll(g, tm - s, 0)
        m = row < tm - s
        g = jnp.where(m, g + a * g_s, g)
        a = jnp.where(m, a * a_s, a)
        s *= 2
    return g


def _group_rows(scr_ref, row, n_groups):
    return jnp.concatenate([scr_ref[pl.ds(c, 1), pl.ds(row, n_groups, stride=SUBLANES), :][0]
                            for c in range(scr_ref.shape[0])], axis=1)


def _spread_rows(rows_ref, n_groups, w):
    return jnp.concatenate([jnp.broadcast_to(rows_ref[g:g + 1, :], (SUBLANES, w)) for g in range(n_groups)], axis=0)


def _scan_groups(a, b, reverse):
    tm, w = a.shape
    shape3 = (tm // SUBLANES, SUBLANES, w)
    a3, b3 = a.reshape(shape3), b.reshape(shape3)
    sub = lax.broadcasted_iota(jnp.int32, shape3, 1)
    s = 1
    while s < SUBLANES:
        shift = SUBLANES - s if reverse else s
        a_s = pltpu.roll(a3, shift, 1)
        b_s = pltpu.roll(b3, shift, 1)
        m = (sub < SUBLANES - s) if reverse else (sub >= s)
        b3 = jnp.where(m, a3 * b_s + b3, b3)
        a3 = jnp.where(m, a3 * a_s, a3)
        s *= 2
    return a3.reshape(tm, w), b3.reshape(tm, w)


def _scan_fwd_tile(a, b, h_in, sa_ref, sb_ref, sc_ref):
    tm, w = a.shape
    n_groups = tm // SUBLANES
    a_loc, b_loc = _scan_groups(a, b, False)
    _put(sa_ref, a_loc)
    _put(sb_ref, b_loc)
    a_end, b_end = _scan_fwd(_group_rows(sa_ref, SUBLANES - 1, n_groups), _group_rows(sb_ref, SUBLANES - 1, n_groups))
    h_end = b_end + a_end * h_in
    sc_ref[...] = _shift_down(h_end, jnp.broadcast_to(h_in, (SUBLANES, w)), 1)
    return b_loc + a_loc * _spread_rows(sc_ref, n_groups, w), h_end


def _scan_rev_tile(a, g, sa_ref, sb_ref, sc_ref):
    tm, w = a.shape
    n_groups = tm // SUBLANES
    a_loc, g_loc = _scan_groups(a, g, True)
    _put(sa_ref, a_loc)
    _put(sb_ref, g_loc)
    d_first = _scan_rev(_group_rows(sa_ref, 0, n_groups), _group_rows(sb_ref, 0, n_groups))
    sc_ref[...] = _shift_up(d_first, jnp.zeros((SUBLANES, w), F32), 1)
    return g_loc + a_loc * _spread_rows(sc_ref, n_groups, w)


def _lane_scratch(tm, w):
    return pltpu.VMEM((w // LANES, tm, LANES), F32)


def _put(scr_ref, val):
    for c in range(scr_ref.shape[0]):
        scr_ref[c] = val[:, c * LANES:(c + 1) * LANES].astype(F32)


def _get(scr_ref):
    return jnp.concatenate([scr_ref[c] for c in range(scr_ref.shape[0])], axis=1)


def _deinterleave(src_ref, dst_ref, dil):
    nc, tm, _ = src_ref.shape
    w = nc * LANES
    for r in range(dil):
        for c in range(nc):
            piece = src_ref[pl.ds(c, 1), pl.ds(r, tm // dil, stride=dil), :][0] if dil > 1 else src_ref[c]
            dst_ref[:, r * w + c * LANES:r * w + (c + 1) * LANES] = piece.astype(dst_ref.dtype)


def _interleave(src_ref, dst_ref, dil):
    nc, tm, _ = dst_ref.shape
    w = nc * LANES
    for r in range(dil):
        for c in range(nc):
            dst_ref[pl.ds(c, 1), pl.ds(r, tm // dil, stride=dil), :] = (
                src_ref[:, r * w + c * LANES:r * w + (c + 1) * LANES].astype(F32)[None])


def _dilated_spec(tm, w, dil, index=lambda i: i):
    return pl.BlockSpec((tm // dil, dil * w), lambda i: (index(i), 0))


def _dilated_shape(S, w, dil, dtype):
    return jax.ShapeDtypeStruct((S // dil, dil * w), dtype)


def _head_masks(shape):
    lane = lax.broadcasted_iota(jnp.int32, shape, 1)
    return [(lane >= h * HEAD_DIM) & (lane < (h + 1) * HEAD_DIM) for h in range(N_HEADS)]


def _colsum(v):
    return jnp.sum(v, axis=0, keepdims=True)


def _conv_a(z_of, halo_of, w_ref):
    p = z_of(2) * z_of(0)
    p_h = halo_of(2) * halo_of(0)
    cv = w_ref[2:3, :] * p + w_ref[1:2, :] * _shift_down(p, p_h, 1) + w_ref[0:1, :] * _shift_down(p, p_h, 2)
    return p, p_h, cv


def _lru_gates(z_of, halo_of, wr_ref, vec_ref, wa_ref, wx_ref):
    rx = z_of(4)
    rx_h = halo_of(4)
    sh = [rx, _shift_down(rx, rx_h, 1), _shift_down(rx, rx_h, 2), _shift_down(rx, rx_h, 3)]
    xc = (wr_ref[3:4, :] * sh[0] + wr_ref[2:3, :] * sh[1] + wr_ref[1:2, :] * sh[2]
          + wr_ref[0:1, :] * sh[3] + vec_ref[0:1, :])
    ga = _sigmoid(jnp.dot(xc.astype(MXU_DTYPE), wa_ref[...], preferred_element_type=F32) + vec_ref[1:2, :])
    gi = _sigmoid(jnp.dot(xc.astype(MXU_DTYPE), wx_ref[...], preferred_element_type=F32) + vec_ref[2:3, :])
    sp = _softplus(-vec_ref[3:4, :])
    log_a = (-RG_C * ga) * sp
    a = jnp.exp(log_a)
    mult = jnp.sqrt(-_expm1_nonpos(2.0 * log_a))
    return xc, sh, ga, gi, a, mult, sp


def _gmlp_fwd(z_of, vec_ref, ws_ref, bs_ref, tm):
    u = _gelu(z_of(6))
    gv = _gelu(z_of(7))
    rr = lax.rsqrt(jnp.mean(gv * gv, axis=-1, keepdims=True) + NORM_EPS)
    vn = (gv * rr) * vec_ref[4:5, :]
    masks = _head_masks((GMLP_CHUNK, GROUP_W))
    parts = []
    for c in range(tm // GMLP_CHUNK):
        vc = vn[c * GMLP_CHUNK:(c + 1) * GMLP_CHUNK].astype(MXU_DTYPE)
        acc = bs_ref[...]
        for h in range(N_HEADS):
            acc = acc + jnp.where(masks[h], jnp.dot(ws_ref[h], vc, preferred_element_type=F32), 0.0)
        parts.append(acc)
    return u, gv, rr, vn, jnp.concatenate(parts, axis=0)


def _mix_specs(tm, S, order):
    const2 = lambda shape: pl.BlockSpec(shape, lambda i: (0, 0))
    return [const2((SUBLANES, GROUP_W)), const2((SUBLANES, GROUP_W)), const2((SUBLANES, GROUP_W)),
            const2((GROUP_W, GROUP_W)), const2((GROUP_W, GROUP_W)),
            pl.BlockSpec((N_HEADS, GMLP_CHUNK, GMLP_CHUNK), lambda i: (0, 0, 0)),
            const2((GMLP_CHUNK, GROUP_W))]


def _inproj_mix_fwd(x, g, w_t, mp, name):
    S, D = x.shape
    N = w_t.shape[0]
    tm = TM_MIX
    hb = tm // SUBLANES
    n_abc = N_ABC * GROUP_W
    n_qkv = 3 * GROUP_W

    def body(x_ref, g_ref, w_ref, wA_ref, wR_ref, vec_ref, wa_ref, wx_ref, ws_ref, bs_ref,
             z_ref, zg_ref, q1_ref, q4_ref, q16_ref, y_ref, h_ref,
             qkv_ref, halo_ref, carry_ref, sa_ref, sb_ref, sc_ref):
        @pl.when(pl.program_id(0) == 0)
        def _():
            halo_ref[...] = jnp.zeros_like(halo_ref)
            carry_ref[...] = jnp.zeros_like(carry_ref)

        xv = x_ref[...]
        r = lax.rsqrt(jnp.mean(xv * xv, axis=-1, keepdims=True) + NORM_EPS)
        hn = ((xv * r) * g_ref[...]).astype(MXU_DTYPE)
        z_ref[...] = _mm_nt(hn, w_ref[0:n_abc, :])
        _put(qkv_ref, _mm_nt(hn, w_ref[n_abc:n_abc + n_qkv, :]))
        zg_ref[...] = _mm_nt(hn, w_ref[n_abc + n_qkv:, :])
        for dil, ref in zip(ATTN_DILATIONS, (q1_ref, q4_ref, q16_ref)):
            _deinterleave(qkv_ref, ref, dil)

        z_of = lambda c: z_ref[:, c * GROUP_W:(c + 1) * GROUP_W]
        halo_of = lambda c: halo_ref[:, c * GROUP_W:(c + 1) * GROUP_W]

        _, _, cv = _conv_a(z_of, halo_of, wA_ref)
        y_ref[:, 0:GROUP_W] = (z_of(1) * cv * _silu_and_grad(z_of(3))[0]).astype(y_ref.dtype)

        xc, _, _, gi, a, mult, _ = _lru_gates(z_of, halo_of, wR_ref, vec_ref, wa_ref, wx_ref)
        b = mult * (gi * xc)
        h, h_end = _scan_fwd_tile(a, b, carry_ref[SUBLANES - 1:SUBLANES, :], sa_ref, sb_ref, sc_ref)
        h_ref[...] = h
        carry_ref[...] = h_end[hb - SUBLANES:hb]
        y_ref[:, GROUP_W:2 * GROUP_W] = (h * _silu_and_grad(z_of(5))[0]).astype(y_ref.dtype)

        u, _, _, _, sp = _gmlp_fwd(z_of, vec_ref, ws_ref, bs_ref, tm)
        y_ref[:, 2 * GROUP_W:3 * GROUP_W] = (u * sp * _silu_and_grad(z_of(8))[0]).astype(y_ref.dtype)
        halo_ref[...] = z_ref[tm - SUBLANES:tm, :]

    row = lambda wd: pl.BlockSpec((tm, wd), lambda i: (i, 0))
    return pl.pallas_call(
        body, name=name, grid=(S // tm,),
        in_specs=[row(D), pl.BlockSpec((1, D), lambda i: (0, 0)),
                  pl.BlockSpec((N, D), lambda i: (0, 0), pipeline_mode=pl.Buffered(1))] + _mix_specs(tm, S, "fwd"),
        out_specs=[row(n_abc), row(GROUP_W)] + [_dilated_spec(tm, n_qkv, dil) for dil in ATTN_DILATIONS]
                  + [row(3 * GROUP_W), row(GROUP_W)],
        out_shape=[jax.ShapeDtypeStruct((S, n_abc), F32), jax.ShapeDtypeStruct((S, GROUP_W), F32)]
                  + [_dilated_shape(S, n_qkv, dil, MXU_DTYPE) for dil in ATTN_DILATIONS]
                  + [jax.ShapeDtypeStruct((S, 3 * GROUP_W), MXU_DTYPE), jax.ShapeDtypeStruct((S, GROUP_W), F32)],
        scratch_shapes=[_lane_scratch(tm, n_qkv), pltpu.VMEM((SUBLANES, n_abc), F32),
                        pltpu.VMEM((SUBLANES, GROUP_W), F32), _lane_scratch(tm, GROUP_W), _lane_scratch(tm, GROUP_W),
                        pltpu.VMEM((hb, GROUP_W), F32)],
        compiler_params=_params(("arbitrary",)),
    )(x, g, w_t, mp["wA"], mp["wR"], mp["vec"], mp["wa"], mp["wx"], mp["ws"], mp["bs"])


_NEG = -1e30


def _slope(h):
    return 2.0 ** (-8.0 * (h + 1) / N_HEADS)


def _attn_bias(dil, offsets, n_keys):
    shape = (ATTN_BLOCK, n_keys)
    qi = lax.broadcasted_iota(jnp.int32, shape, 0)
    ki = lax.broadcasted_iota(jnp.int32, shape, 1)
    blocks = []
    for f in offsets:
        delta = qi + f - ki
        valid = (delta >= 0) & (delta <= ATTN_BLOCK)
        dist = (delta * dil).astype(F32)
        for h in range(N_HEADS):
            blocks.append(jnp.where(valid, -_slope(h) * dist, _NEG))
    return jnp.concatenate(blocks, axis=0)


def _stack_heads(t, masks):
    return jnp.concatenate([jnp.where(m, t, jnp.zeros_like(t)) for m in masks], axis=0)


def _unstack_heads(t4, masks, base=0):
    out = t4[base * ATTN_BLOCK:(base + 1) * ATTN_BLOCK]
    for h in range(1, N_HEADS):
        out = jnp.where(masks[h], t4[(base + h) * ATTN_BLOCK:(base + h + 1) * ATTN_BLOCK], out)
    return out


def _attn_fwd(qkv, dil, name):
    rows = qkv.shape[0]
    nb = rows // ATTN_BLOCK
    scale = 1.0 / math.sqrt(HEAD_DIM)
    B = ATTN_BLOCK
    per_step = min(ATTN_FWD_BLOCKS_PER_STEP, nb)

    def body(q_ref, kc_ref, kp_ref, vc_ref, vp_ref, o_ref, l_ref, bias_ref):
        n = pl.program_id(1)

        @pl.when(n == 0)
        def _():
            bias_ref[...] = _attn_bias(dil, (B,), 2 * B)

        masks = _head_masks((B, GROUP_W))
        for j in range(per_step):
            own = slice(j * B, (j + 1) * B)
            before = slice((j - 1) * B, j * B)
            qs = _stack_heads(q_ref[own], masks)
            keys = jnp.concatenate([kp_ref[...] if j == 0 else kc_ref[before], kc_ref[own]], axis=0)
            vals = jnp.concatenate([vp_ref[...] if j == 0 else vc_ref[before], vc_ref[own]], axis=0)
            s = _mm_nt(qs, keys) * scale + bias_ref[...]
            if j == 0:
                key_col = lax.broadcasted_iota(jnp.int32, s.shape, 1)
                s = jnp.where((n == 0) & (key_col < B), _NEG, s)
            m = jnp.max(s, axis=-1, keepdims=True)
            p = jnp.exp(s - m)
            l = jnp.sum(p, axis=-1, keepdims=True)
            o4 = jnp.dot(p.astype(MXU_DTYPE), vals, preferred_element_type=F32)
            o_ref[own] = _unstack_heads(o4, masks) / _unstack_heads(jnp.broadcast_to(l, o4.shape), masks)
            l_ref[own] = _unstack_heads(jnp.broadcast_to(m + jnp.log(l), o4.shape), masks)

    blk = (per_step * B, GROUP_W)
    cur = lambda c: pl.BlockSpec(blk, lambda r, n: (n, r * 3 + c))
    prev = lambda c: pl.BlockSpec((B, GROUP_W), lambda r, n: (jnp.maximum(n * per_step - 1, 0), r * 3 + c))
    out = pl.BlockSpec(blk, lambda r, n: (n, r))
    return pl.pallas_call(
        body, name=name, grid=(dil, nb // per_step),
        in_specs=[cur(0), cur(1), prev(1), cur(2), prev(2)],
        out_specs=[out, out],
        out_shape=[jax.ShapeDtypeStruct((rows, dil * GROUP_W), F32)] * 2,
        scratch_shapes=[pltpu.VMEM((N_HEADS * ATTN_BLOCK, 2 * ATTN_BLOCK), F32)],
        compiler_params=_params(("parallel", "arbitrary")),
    )(qkv, qkv, qkv, qkv, qkv)


def _outproj(x, z_g, y_abc, attn, w_out, name):
    S, D = x.shape
    tm = TM_MM
    n_abc = 3 * GROUP_W

    def body(x_ref, g_ref, yabc_ref, o1, l1, o2, l2, o3, l3, w_ref,
             xn_ref, y_ref, o_ref, lse1_ref, lse4_ref, lse16_ref, so2, sl2, so3, sl3, slse):
        for src, dst, dil in ((o2, so2, ATTN_DILATIONS[1]), (l2, sl2, ATTN_DILATIONS[1]),
                              (o3, so3, ATTN_DILATIONS[2]), (l3, sl3, ATTN_DILATIONS[2])):
            _interleave(src, dst, dil)
        la, lb, lc = l1[...], _get(sl2), _get(sl3)
        mx = jnp.maximum(jnp.maximum(la, lb), lc)
        ea, eb, ec = jnp.exp(la - mx), jnp.exp(lb - mx), jnp.exp(lc - mx)
        den = ea + eb + ec
        o = (ea * o1[...] + eb * _get(so2) + ec * _get(so3)) / den
        o_ref[...] = o
        _put(slse, mx + jnp.log(den))
        for dil, ref in zip(ATTN_DILATIONS, (lse1_ref, lse4_ref, lse16_ref)):
            _deinterleave(slse, ref, dil)
        y_d = o * _silu_and_grad(g_ref[...])[0]
        y_ref[:, 0:n_abc] = yabc_ref[...].astype(MXU_DTYPE)
        y_ref[:, n_abc:] = y_d.astype(MXU_DTYPE)
        xn_ref[...] = x_ref[...] + jnp.dot(y_ref[...], w_ref[...], preferred_element_type=F32)

    row = lambda w: pl.BlockSpec((tm, w), lambda i: (i, 0))
    dil_specs = [_dilated_spec(tm, GROUP_W, dil) for dil in ATTN_DILATIONS]
    (o1, l1), (o2, l2), (o3, l3) = attn
    return pl.pallas_call(
        body, name=name, grid=(S // tm,),
        in_specs=[row(D), row(GROUP_W), row(n_abc)] + [sp for sp in dil_specs for _ in range(2)]
                 + [pl.BlockSpec(w_out.shape, lambda i: (0, 0))],
        out_specs=[row(D), row(4 * GROUP_W), row(GROUP_W)] + dil_specs,
        out_shape=[jax.ShapeDtypeStruct((S, D), F32), jax.ShapeDtypeStruct((S, 4 * GROUP_W), MXU_DTYPE),
                   jax.ShapeDtypeStruct((S, GROUP_W), F32)]
                  + [_dilated_shape(S, GROUP_W, dil, F32) for dil in ATTN_DILATIONS],
        scratch_shapes=[_lane_scratch(tm, GROUP_W)] * 5,
        compiler_params=_params(("parallel",)),
    )(x, z_g, y_abc, o1, l1, o2, l2, o3, l3, w_out)


def _loss_head(x, g, target, name):
    S, D = x.shape
    tm = TM_MM

    def body(x_ref, g_ref, t_ref, dx_ref, loss_ref, dg_ref):
        i = pl.program_id(0)

        @pl.when(i == 0)
        def _():
            loss_ref[...] = jnp.zeros_like(loss_ref)
            dg_ref[...] = jnp.zeros_like(dg_ref)

        xv = x_ref[...]
        r = lax.rsqrt(jnp.mean(xv * xv, axis=-1, keepdims=True) + NORM_EPS)
        xn = xv * r
        err = xn * g_ref[...] - t_ref[...]
        per_tok = jnp.mean(err * err, axis=-1, keepdims=True)
        loss_ref[...] += 0.5 * jnp.sum(per_tok, axis=0, keepdims=True)
        dout = err * (1.0 / D)
        dg_ref[...] += _colsum(dout * xn)
        dxn = dout * g_ref[...]
        dx_ref[...] = r * (dxn - xn * jnp.mean(dxn * xn, axis=-1, keepdims=True))

    row = pl.BlockSpec((tm, D), lambda i: (i, 0))
    return pl.pallas_call(
        body, name=name, grid=(S // tm,),
        in_specs=[row, pl.BlockSpec((1, D), lambda i: (0, 0)), row],
        out_specs=[row, pl.BlockSpec((1, LANES), lambda i: (0, 0)), pl.BlockSpec((1, D), lambda i: (0, 0))],
        out_shape=[jax.ShapeDtypeStruct((S, D), F32), jax.ShapeDtypeStruct((1, LANES), F32),
                   jax.ShapeDtypeStruct((1, D), F32)],
        compiler_params=_params(("arbitrary",)),
    )(x, g, target)


def _outproj_mix_bwd(dx, y, w_out, z, z_g, hs, o, mp, name):
    S, D = dx.shape
    E = y.shape[1]
    tm = TM_MIX
    hb = tm // SUBLANES
    nT = S // tm
    last_blk = S // SUBLANES - 1
    wcols = N_ABC * GROUP_W

    def body(dx_ref, y_ref, w_ref, z_ref, zh_ref, zn_ref, zg_ref, h_ref, hh_ref, o_ref,
             wA_ref, wR_ref, vec_ref, wa_ref, wx_ref, ws_ref, bs_ref,
             dw_ref, dz_ref, dzg_ref, do1_ref, do4_ref, do16_ref, dl1_ref, dl4_ref, dl16_ref,
             dwA_ref, dwR_ref, dvec_ref, dwa_ref, dwx_ref, dws_ref, dbs_ref,
             hcarry_ref, xcarry_ref, bsacc_ref, do_ref, dl_ref, sa_ref, sb_ref, sc_ref, dy_ref, dyn_ref, acc_ref):
        i = pl.program_id(0)
        ti = nT - 1 - i

        @pl.when(i == 0)
        def _():
            acc_ref[...] = jnp.zeros_like(acc_ref)
            dyn_ref[...] = jnp.zeros_like(dyn_ref)
            hcarry_ref[...] = jnp.zeros_like(hcarry_ref)
            xcarry_ref[...] = jnp.zeros_like(xcarry_ref)
            bsacc_ref[...] = jnp.zeros_like(bsacc_ref)
            dwA_ref[...] = jnp.zeros_like(dwA_ref)
            dwR_ref[...] = jnp.zeros_like(dwR_ref)
            dvec_ref[...] = jnp.zeros_like(dvec_ref)
            dwa_ref[...] = jnp.zeros_like(dwa_ref)
            dwx_ref[...] = jnp.zeros_like(dwx_ref)
            dws_ref[...] = jnp.zeros_like(dws_ref)
            dbs_ref[...] = jnp.zeros_like(dbs_ref)

        dxb = dx_ref[...].astype(MXU_DTYPE)
        dy_ref[...] = _mm_nt(dxb, w_ref[...])
        acc_ref[...] += _mm_tn(y_ref[...], dxb)

        @pl.when(i == nT - 1)
        def _():
            dw_ref[...] = acc_ref[...].astype(dw_ref.dtype)

        has_prev = ti > 0
        has_next = i > 0
        col = lambda c: slice(c * GROUP_W, (c + 1) * GROUP_W)
        z_of = lambda c: z_ref[:, col(c)]
        halo_of = lambda c: jnp.where(has_prev, zh_ref[:, col(c)], 0.0)
        next_of = lambda c: zn_ref[:, col(c)]

        p, p_h, cv = _conv_a(z_of, halo_of, wA_ref)
        sg, dsg = _silu_and_grad(z_of(3))
        a_b = z_of(1)
        dya = dy_ref[:, col(0)]
        dcv = dya * a_b * sg
        dcv_n = jnp.where(has_next, dyn_ref[...] * next_of(1) * _silu_and_grad(next_of(3))[0], 0.0)
        dp = (wA_ref[2:3, :] * dcv + wA_ref[1:2, :] * _shift_up(dcv, dcv_n, 1)
              + wA_ref[0:1, :] * _shift_up(dcv, dcv_n, 2))
        dwA_ref[2:3, :] += _colsum(dcv * p)
        dwA_ref[1:2, :] += _colsum(dcv * _shift_down(p, p_h, 1))
        dwA_ref[0:1, :] += _colsum(dcv * _shift_down(p, p_h, 2))
        def put_dz(c, val):
            dz_ref[:, col(c)] = val.astype(dz_ref.dtype)

        put_dz(0, dp * z_of(2))
        put_dz(1, dya * cv * sg)
        put_dz(2, dp * z_of(0))
        put_dz(3, dya * a_b * cv * dsg)

        xc, sh, ga, gi, a, mult, sp = _lru_gates(z_of, halo_of, wR_ref, vec_ref, wa_ref, wx_ref)
        h = h_ref[...]
        h_prev = _shift_down(h, jnp.where(has_prev, hh_ref[...], 0.0), 1)
        sgr, dsgr = _silu_and_grad(z_of(5))
        dyb = dy_ref[:, col(1)]
        put_dz(5, dyb * h * dsgr)
        row = lax.broadcasted_iota(jnp.int32, (tm, GROUP_W), 0)
        g_in = dyb * sgr + jnp.where(row == tm - 1, hcarry_ref[0:1, :], 0.0)
        a_up = _shift_up(a, jnp.zeros((SUBLANES, GROUP_W), F32), 1)
        dH = _scan_rev_tile(a_up, g_in, sa_ref, sb_ref, sc_ref)
        hcarry_ref[...] = (a * dH)[0:SUBLANES]
        da = dH * h_prev
        gx = gi * xc
        dmult = dH * gx
        dgi = dH * mult * xc
        dxc = dH * mult * gi
        dlog_a = da * a - dmult * (a * a) / mult
        dga = dlog_a * (-RG_C * sp)
        dlam_row = _colsum(dlog_a * (-RG_C * ga)) * (-_sigmoid(-vec_ref[3:4, :]))
        dpre_a = dga * ga * (1.0 - ga)
        dpre_i = dgi * gi * (1.0 - gi)
        dwa_ref[...] += _mm_tn(xc, dpre_a)
        dwx_ref[...] += _mm_tn(xc, dpre_i)
        dxc = dxc + _mm_nt(dpre_a, wa_ref[...]) + _mm_nt(dpre_i, wx_ref[...])
        dvec_ref[0:1, :] += _colsum(dxc)
        dvec_ref[1:2, :] += _colsum(dpre_a)
        dvec_ref[2:3, :] += _colsum(dpre_i)
        dvec_ref[3:4, :] += dlam_row
        for k in range(4):
            dwR_ref[k:k + 1, :] += _colsum(dxc * sh[3 - k])
        dxc_n = xcarry_ref[...]
        put_dz(4, wR_ref[3:4, :] * dxc + wR_ref[2:3, :] * _shift_up(dxc, dxc_n, 1)
               + wR_ref[1:2, :] * _shift_up(dxc, dxc_n, 2) + wR_ref[0:1, :] * _shift_up(dxc, dxc_n, 3))
        xcarry_ref[...] = dxc[0:SUBLANES]

        c_u, c_v = z_of(6), z_of(7)
        u, du_dx = _gelu_and_grad(c_u)
        gv, dgv_dx = _gelu_and_grad(c_v)
        rr = lax.rsqrt(jnp.mean(gv * gv, axis=-1, keepdims=True) + NORM_EPS)
        xhat = gv * rr
        g_c = vec_ref[4:5, :]
        vn = xhat * g_c
        masks = _head_masks((GMLP_CHUNK, GROUP_W))
        tri_r = lax.broadcasted_iota(jnp.int32, (GMLP_CHUNK, GMLP_CHUNK), 0)
        tri_c = lax.broadcasted_iota(jnp.int32, (GMLP_CHUNK, GMLP_CHUNK), 1)
        tril = tri_r >= tri_c
        sgc, dsgc = _silu_and_grad(z_of(8))
        dyc = dy_ref[:, col(2)]
        dsp_full = dyc * u * sgc
        sp_parts, dvn_parts = [], []
        for c in range(tm // GMLP_CHUNK):
            rs = slice(c * GMLP_CHUNK, (c + 1) * GMLP_CHUNK)
            vc = vn[rs].astype(MXU_DTYPE)
            dsp_c = dsp_full[rs]
            bsacc_ref[...] += dsp_c
            acc = bs_ref[...]
            dvn_c = jnp.zeros((GMLP_CHUNK, GROUP_W), F32)
            for h in range(N_HEADS):
                w_h = ws_ref[h]
                acc = acc + jnp.where(masks[h], jnp.dot(w_h, vc, preferred_element_type=F32), 0.0)
                dsp_h = jnp.where(masks[h], dsp_c, 0.0).astype(MXU_DTYPE)
                dvn_c = dvn_c + _mm_tn(w_h, dsp_h)
                dws_ref[h] += jnp.where(tril, _mm_nt(dsp_h, vc), 0.0)
            sp_parts.append(acc)
            dvn_parts.append(dvn_c)
        spv = jnp.concatenate(sp_parts, axis=0)
        dvn = jnp.concatenate(dvn_parts, axis=0)
        put_dz(6, dyc * spv * sgc * du_dx)
        put_dz(8, dyc * u * spv * dsgc)
        dvec_ref[4:5, :] += _colsum(dvn * xhat)
        dgvn = dvn * g_c
        dgv = rr * (dgvn - xhat * jnp.mean(dgvn * xhat, axis=-1, keepdims=True))
        put_dz(7, dgv * dgv_dx)

        sgd, dsgd = _silu_and_grad(zg_ref[...])
        dyd = dy_ref[:, col(3)]
        ov = o_ref[...]
        do = dyd * sgd
        _put(do_ref, do)
        dzg_ref[...] = (dyd * ov * dsgd).astype(dzg_ref.dtype)
        prod = do * ov
        tmasks = _head_masks((tm, GROUP_W))
        dl = jnp.zeros((tm, GROUP_W), F32)
        for h in range(N_HEADS):
            dl = jnp.where(tmasks[h], jnp.sum(jnp.where(tmasks[h], prod, 0.0), axis=-1, keepdims=True), dl)
        _put(dl_ref, dl)
        for dil, d_out, l_out in zip(ATTN_DILATIONS, (do1_ref, do4_ref, do16_ref), (dl1_ref, dl4_ref, dl16_ref)):
            _deinterleave(do_ref, d_out, dil)
            _deinterleave(dl_ref, l_out, dil)

        @pl.when(i == nT - 1)
        def _():
            acc = bsacc_ref[...]
            lane = lax.broadcasted_iota(jnp.int32, (GMLP_CHUNK, LANES), 1)
            out = jnp.zeros((GMLP_CHUNK, LANES), F32)
            for h in range(N_HEADS):
                out = jnp.where(lane == h, jnp.sum(jnp.where(masks[h], acc, 0.0), axis=-1, keepdims=True), out)
            dbs_ref[...] = out

        dyn_ref[...] = dy_ref[0:SUBLANES, 0:GROUP_W]

    rev = lambda w: pl.BlockSpec((tm, w), lambda i: (nT - 1 - i, 0))
    prev8 = lambda w: pl.BlockSpec((SUBLANES, w), lambda i: (jnp.maximum((nT - 1 - i) * hb - 1, 0), 0))
    next8 = lambda w: pl.BlockSpec((SUBLANES, w), lambda i: (jnp.minimum((nT - i) * hb, last_blk), 0))
    const2 = lambda shape: pl.BlockSpec(shape, lambda i: (0, 0))
    dil_specs = [_dilated_spec(tm, GROUP_W, dil, lambda i: nT - 1 - i) for dil in ATTN_DILATIONS]
    dil_shapes = [_dilated_shape(S, GROUP_W, dil, F32) for dil in ATTN_DILATIONS]
    small = (SUBLANES, GROUP_W)
    sq = (GROUP_W, GROUP_W)
    ws_shape = (N_HEADS, GMLP_CHUNK, GMLP_CHUNK)
    return pl.pallas_call(
        body, name=name, grid=(nT,),
        in_specs=[rev(D), rev(E), pl.BlockSpec((E, D), lambda i: (0, 0), pipeline_mode=pl.Buffered(1)),
                  rev(wcols), prev8(wcols), next8(wcols), rev(GROUP_W), rev(GROUP_W), prev8(GROUP_W), rev(GROUP_W)]
                 + _mix_specs(tm, S, "bwd"),
        out_specs=[const2((E, D)), rev(wcols), rev(GROUP_W)] + dil_specs + dil_specs
                  + [const2(small), const2(small), const2(small), const2(sq), const2(sq),
                     pl.BlockSpec(ws_shape, lambda i: (0, 0, 0)), const2((GMLP_CHUNK, LANES))],
        out_shape=[jax.ShapeDtypeStruct((E, D), WIRE_DTYPE),
                   jax.ShapeDtypeStruct((S, wcols), MXU_DTYPE), jax.ShapeDtypeStruct((S, GROUP_W), MXU_DTYPE)]
                  + [_dilated_shape(S, GROUP_W, dil, MXU_DTYPE) for dil in ATTN_DILATIONS] + dil_shapes
                  + [jax.ShapeDtypeStruct(small, F32)] * 3 + [jax.ShapeDtypeStruct(sq, F32)] * 2
                  + [jax.ShapeDtypeStruct(ws_shape, F32), jax.ShapeDtypeStruct((GMLP_CHUNK, LANES), F32)],
        scratch_shapes=[pltpu.VMEM(small, F32), pltpu.VMEM(small, F32), pltpu.VMEM((GMLP_CHUNK, GROUP_W), F32),
                        _lane_scratch(tm, GROUP_W), _lane_scratch(tm, GROUP_W),
                        _lane_scratch(tm, GROUP_W), _lane_scratch(tm, GROUP_W), pltpu.VMEM((hb, GROUP_W), F32),
                        pltpu.VMEM((tm, E), F32), pltpu.VMEM(small, F32), pltpu.VMEM((E, D), F32)],
        compiler_params=_params(("arbitrary",)),
    )(dx, y, w_out, z, z, z, z_g, hs, hs, o, mp["wA"], mp["wR"], mp["vec"], mp["wa"], mp["wx"], mp["ws"], mp["bs"])


def _attn_bwd(qkv, do, lse, delta, dil, name):
    rows = qkv.shape[0]
    nb = rows // ATTN_BLOCK
    scale = 1.0 / math.sqrt(HEAD_DIM)
    B = ATTN_BLOCK
    per_step = min(ATTN_BWD_BLOCKS_PER_STEP, nb)
    n_steps = nb // per_step

    def body(qc_ref, qn_ref, kc_ref, kp_ref, vc_ref, vp_ref, doc_ref, don_ref, lc_ref, ln_ref, dc_ref, dn_ref,
             dq_ref, dk_ref, dv_ref, bias_ref, bias_next_ref):
        n = pl.program_id(1)

        @pl.when(n == 0)
        def _():
            bias_ref[...] = _attn_bias(dil, (B,), 2 * B)
            bias_next_ref[...] = _attn_bias(dil, (B,), B)

        masks = _head_masks((B, GROUP_W))

        def per_row(tile):
            return jnp.concatenate([jnp.max(jnp.where(masks[h], tile, _NEG), axis=-1, keepdims=True)
                                    for h in range(N_HEADS)], axis=0)

        def grads(q, dov, lse_tile, dl_tile, keys, vals, bias, dead):
            qs = _stack_heads(q, masks)
            dos = _stack_heads(dov.astype(MXU_DTYPE), masks)
            s = _mm_nt(qs, keys) * scale + bias
            if dead is not None:
                s = jnp.where(dead(s.shape), _NEG, s)
            p = jnp.exp(s - per_row(lse_tile))
            ds = (p * (_mm_nt(dos, vals) - per_row(dl_tile)) * scale).astype(MXU_DTYPE)
            return ds, _mm_tn(ds, qs), _mm_tn(p.astype(MXU_DTYPE), dos)

        for j in range(per_step):
            own = slice(j * B, (j + 1) * B)
            before = slice((j - 1) * B, j * B)
            keys = jnp.concatenate([kp_ref[...] if j == 0 else kc_ref[before], kc_ref[own]], axis=0)
            vals = jnp.concatenate([vp_ref[...] if j == 0 else vc_ref[before], vc_ref[own]], axis=0)
            dead = (lambda shape: (n == 0) & (lax.broadcasted_iota(jnp.int32, shape, 1) < B)) if j == 0 else None
            ds, dk2, dv2 = grads(qc_ref[own], doc_ref[own], lc_ref[own], dc_ref[own], keys, vals, bias_ref[...], dead)
            dq_ref[own] = _unstack_heads(jnp.dot(ds, keys, preferred_element_type=F32), masks).astype(dq_ref.dtype)
            if j > 0:
                dk_ref[before] = (dk_own + dk2[:B]).astype(dk_ref.dtype)
                dv_ref[before] = (dv_own + dv2[:B]).astype(dv_ref.dtype)
            dk_own, dv_own = dk2[B:], dv2[B:]
        last = slice((per_step - 1) * B, per_step * B)
        _, dk1, dv1 = grads(qn_ref[...], don_ref[...], ln_ref[...], dn_ref[...], kc_ref[last], vc_ref[last],
                            bias_next_ref[...], lambda shape: n == n_steps - 1)
        dk_ref[last] = (dk_own + dk1).astype(dk_ref.dtype)
        dv_ref[last] = (dv_own + dv1).astype(dv_ref.dtype)

    blk = (per_step * B, GROUP_W)
    one = (B, GROUP_W)
    nxt_idx = lambda n: jnp.minimum((n + 1) * per_step, nb - 1)
    prv_idx = lambda n: jnp.maximum(n * per_step - 1, 0)
    zcur = lambda c: pl.BlockSpec(blk, lambda r, n: (n, r * 3 + c))
    znext = lambda c: pl.BlockSpec(one, lambda r, n: (nxt_idx(n), r * 3 + c))
    zprev = lambda c: pl.BlockSpec(one, lambda r, n: (prv_idx(n), r * 3 + c))
    cur = pl.BlockSpec(blk, lambda r, n: (n, r))
    nxt = pl.BlockSpec(one, lambda r, n: (nxt_idx(n), r))
    return pl.pallas_call(
        body, name=name, grid=(dil, n_steps),
        in_specs=[zcur(0), znext(0), zcur(1), zprev(1), zcur(2), zprev(2), cur, nxt, cur, nxt, cur, nxt],
        out_specs=[cur, cur, cur],
        out_shape=[jax.ShapeDtypeStruct((rows, dil * GROUP_W), WIRE_DTYPE)] * 3,
        scratch_shapes=[pltpu.VMEM((N_HEADS * B, 2 * B), F32), pltpu.VMEM((N_HEADS * B, B), F32)],
        compiler_params=_params(("parallel", "arbitrary")),
    )(qkv, qkv, qkv, qkv, qkv, qkv, do, do, lse, lse, delta, delta)


def _inproj_bwd(x, g, dxn, dz_abc, dqkv, dz_g, w_t, name):
    S, D = x.shape
    N = w_t.shape[0]
    tm = TM_MM
    n_abc = N_ABC * GROUP_W

    def body(x_ref, g_ref, dxn_ref, dabc_ref, q1, k1, v1, q2, k2, v2, q3, k3, v3, dg_ref, w_ref,
             dx_ref, dz_ref, h_ref, dgn_ref, s4_ref, s16_ref):
        i = pl.program_id(0)

        @pl.when(i == 0)
        def _():
            dgn_ref[...] = jnp.zeros_like(dgn_ref)

        dz_ref[:, 0:n_abc] = dabc_ref[...].astype(MXU_DTYPE)
        for j, parts in enumerate(((q1, q2, q3), (k1, k2, k3), (v1, v2, v3))):
            c0 = n_abc + j * GROUP_W
            _interleave(parts[1], s4_ref, ATTN_DILATIONS[1])
            _interleave(parts[2], s16_ref, ATTN_DILATIONS[2])
            dz_ref[:, c0:c0 + GROUP_W] = (parts[0][...] + _get(s4_ref) + _get(s16_ref)).astype(MXU_DTYPE)
        dz_ref[:, n_abc + 3 * GROUP_W:] = dg_ref[...].astype(MXU_DTYPE)
        dh = jnp.dot(dz_ref[...], w_ref[...], preferred_element_type=F32)
        xv = x_ref[...]
        r = lax.rsqrt(jnp.mean(xv * xv, axis=-1, keepdims=True) + NORM_EPS)
        xn = xv * r
        gv = g_ref[...]
        h_ref[...] = (xn * gv).astype(MXU_DTYPE)
        dgn_ref[...] += _colsum(dh * xn)
        dn = dh * gv
        dx_ref[...] = dxn_ref[...] + r * (dn - xn * jnp.mean(dn * xn, axis=-1, keepdims=True))

    row = lambda w: pl.BlockSpec((tm, w), lambda i: (i, 0))
    flat = [t for p in dqkv for t in p]
    dil_specs = [_dilated_spec(tm, GROUP_W, dil) for dil in ATTN_DILATIONS for _ in range(3)]
    return pl.pallas_call(
        body, name=name, grid=(S // tm,),
        in_specs=[row(D), pl.BlockSpec((1, D), lambda i: (0, 0)), row(D), row(n_abc)] + dil_specs
                 + [row(GROUP_W), pl.BlockSpec((N, D), lambda i: (0, 0), pipeline_mode=pl.Buffered(1))],
        out_specs=[row(D), row(N), row(D), pl.BlockSpec((1, D), lambda i: (0, 0))],
        out_shape=[jax.ShapeDtypeStruct((S, D), F32), jax.ShapeDtypeStruct((S, N), MXU_DTYPE),
                   jax.ShapeDtypeStruct((S, D), MXU_DTYPE), jax.ShapeDtypeStruct((1, D), F32)],
        scratch_shapes=[_lane_scratch(tm, GROUP_W)] * 2,
        compiler_params=_params(("arbitrary",)),
    )(x, g, dxn, dz_abc, *flat, dz_g, w_t)


def _inproj_wgrad(h, dz, name):
    S, D = h.shape
    N = dz.shape[1]
    tm = TM_WGRAD
    nj = 2
    cw = N // nj
    per = N_DEV // nj
    n_loc = N // N_DEV

    def body(h_ref, dz_ref, dw_ref, acc_ref):
        i = pl.program_id(1)

        @pl.when(i == 0)
        def _():
            acc_ref[...] = jnp.zeros_like(acc_ref)

        acc_ref[...] += _mm_tn(dz_ref[...], h_ref[...])

        @pl.when(i == S // tm - 1)
        def _():
            for b in range(per):
                dw_ref[b] = acc_ref[b * n_loc:(b + 1) * n_loc, :].astype(dw_ref.dtype)

    return pl.pallas_call(
        body, name=name, grid=(nj, S // tm),
        in_specs=[pl.BlockSpec((tm, D), lambda j, i: (i, 0)), pl.BlockSpec((tm, cw), lambda j, i: (i, j))],
        out_specs=pl.BlockSpec((per, n_loc, D), lambda j, i: (j, 0, 0)),
        out_shape=jax.ShapeDtypeStruct((N_DEV, n_loc, D), WIRE_DTYPE),
        scratch_shapes=[pltpu.VMEM((cw, D), F32)],
        compiler_params=_params(("parallel", "arbitrary")),
    )(h, dz)


def _my_place():
    return lax.axis_index("x"), lax.axis_index("y"), lax.axis_index("c")


def _peer(x, y, c, k):
    px = 1 - x if k & 4 else x
    py = 1 - y if k & 2 else y
    pc = 1 - c if k & 1 else c
    return (px, py, pc), 4 * px + 2 * py + pc


HBM_SPEC = pl.BlockSpec(memory_space=pltpu.HBM)
SEM_SPEC = pl.BlockSpec(memory_space=pltpu.SEMAPHORE)
SPLIT_EFFECT = pltpu.SideEffectType.DATAFLOW_SIDE_EFFECTING
N_PEERS = N_DEV - 1


def _exchange_copies(srcs, lands, send_sems, recv_sems, whole, arrival):
    x, y, c = _my_place()
    me = 4 * x + 2 * y + c
    copies = []
    for t in range(len(srcs)):
        for k in range(1, N_DEV):
            peer, pidx = _peer(x, y, c, k)
            copies.append(pltpu.make_async_remote_copy(
                src_ref=srcs[t] if whole[t] else srcs[t].at[pidx],
                dst_ref=lands[t].at[pidx if arrival else me], send_sem=send_sems.at[t * N_PEERS + k - 1],
                recv_sem=recv_sems.at[t * N_PEERS + k - 1], device_id=peer, device_id_type=MESH))
    return copies


def _exchange_start(groups, name, after=None):
    sizes = [len(g) for g in groups]
    whole = [w for g in groups for _, w in g]
    srcs = [pltpu.with_memory_space_constraint(a, pltpu.HBM) for g in groups for a, _ in g]
    lands = [pltpu.with_memory_space_constraint(lax.empty(((N_DEV,) + a.shape) if w else a.shape, a.dtype), pltpu.HBM)
             for a, w in zip(srcs, whole)]
    n = len(srcs)
    n_g = len(groups)
    extra = [] if after is None else [after]
    n_in = 2 * n + len(extra)

    def body(*refs):
        src_refs, land_refs = refs[:n], refs[n:2 * n]
        sem_refs = refs[n_in + 2 * n:n_in + 2 * n + 2 * n_g]
        token = refs[-1]
        off = 0
        for gi, sz in enumerate(sizes):
            for send in _exchange_copies(src_refs[off:off + sz], land_refs[off:off + sz],
                                         sem_refs[2 * gi], sem_refs[2 * gi + 1], whole[off:off + sz], False):
                send.start()
            off += sz
        token[...] = jnp.zeros_like(token)

    sem_shapes = [pltpu.SemaphoreType.DMA((sz * N_PEERS,)) for sz in sizes for _ in range(2)]
    outs = pl.pallas_call(
        body, name=name,
        in_specs=[HBM_SPEC] * (2 * n) + [pl.BlockSpec(memory_space=pl.ANY)] * len(extra),
        out_specs=[HBM_SPEC] * (2 * n) + [SEM_SPEC] * (2 * n_g) + [pl.BlockSpec(memory_space=pltpu.VMEM)],
        out_shape=[pltpu.HBM(a.shape, a.dtype) for a in srcs + lands] + sem_shapes
                  + [jax.ShapeDtypeStruct((SUBLANES, LANES), F32)],
        input_output_aliases={i: i for i in range(2 * n)},
        compiler_params=pltpu.CompilerParams(has_side_effects=SPLIT_EFFECT),
    )(*srcs, *lands, *extra)
    handles, off = [], 0
    for gi, sz in enumerate(sizes):
        handles.append((outs[2 * n + 2 * gi], outs[2 * n + 2 * gi + 1], outs[off:off + sz], outs[n + off:n + off + sz],
                        whole[off:off + sz]))
        off += sz
    return handles, outs[-1]


def _exchange_wait(handle, after, name):
    send_sems, recv_sems, srcs, lands, whole = handle
    n = len(srcs)

    def body(*refs):
        src_refs, land_refs = refs[:n], refs[n:2 * n]
        for send in _exchange_copies(src_refs, land_refs, refs[2 * n], refs[2 * n + 1], whole, False):
            send.wait_send()
        for arrival in _exchange_copies(src_refs, land_refs, refs[2 * n], refs[2 * n + 1], whole, True):
            arrival.wait_recv()

    outs = pl.pallas_call(
        body, name=name,
        in_specs=[HBM_SPEC] * (2 * n) + [SEM_SPEC, SEM_SPEC, pl.BlockSpec(memory_space=pl.ANY)],
        out_specs=[HBM_SPEC] * (2 * n),
        out_shape=[pltpu.HBM(a.shape, a.dtype) for a in list(srcs) + list(lands)],
        input_output_aliases={i: i for i in range(2 * n)},
        compiler_params=pltpu.CompilerParams(has_side_effects=SPLIT_EFFECT),
    )(*srcs, *lands, send_sems, recv_sems, after)
    x, y, c = _my_place()
    me = 4 * x + 2 * y + c
    own = [s[None] if w else lax.dynamic_slice_in_dim(s, me, 1, axis=0) for s, w in zip(outs[:n], whole)]
    return [lax.dynamic_update_slice_in_dim(ld, o, me, axis=0) for ld, o in zip(outs[n:], own)]


def _sum_slots(parts, name):
    n = len(parts)

    def body(*refs):
        for p_ref, o_ref in zip(refs[:n], refs[n:]):
            acc = p_ref[0]
            for j in range(1, N_DEV):
                acc = acc + p_ref[j]
            o_ref[...] = acc

    vm = pl.BlockSpec(memory_space=pltpu.VMEM)
    return pl.pallas_call(
        body, name=name, in_specs=[vm] * n, out_specs=[vm] * n,
        out_shape=[jax.ShapeDtypeStruct(p.shape[1:], F32) for p in parts],
        compiler_params=pltpu.CompilerParams(vmem_limit_bytes=VMEM_LIMIT),
    )(*parts)


def _adamw_math(w, g, m, v):
    m = ADAM_B1 * m + (1.0 - ADAM_B1) * g
    v = ADAM_B2 * v + (1.0 - ADAM_B2) * (g * g)
    m_hat = m / (1.0 - ADAM_B1 ** ADAM_STEP)
    v_hat = v / (1.0 - ADAM_B2 ** ADAM_STEP)
    delta = -ADAM_LR * (m_hat / (jnp.sqrt(v_hat) + ADAM_EPS) + ADAM_WD * w)
    return delta, m, v


def _adamw_summed(parts, w, m, v, tr, name):
    depth, R, C = w.shape

    def body(*refs):
        p_refs = refs[:depth]
        w_ref, m_ref, v_ref, g_ref, d_ref, nm_ref, nv_ref = refs[depth:]
        lay = pl.program_id(0)
        for l in range(depth):
            @pl.when(lay == l)
            def _(p_ref=p_refs[l]):
                g = p_ref[0].astype(F32)
                for j in range(1, N_DEV):
                    g = g + p_ref[j].astype(F32)
                g_ref[0] = g
        d_ref[0], nm_ref[0], nv_ref[0] = _adamw_math(w_ref[0], g_ref[0], m_ref[0], v_ref[0])

    part_spec = lambda l: pl.BlockSpec((N_DEV, tr, C), lambda lay, i: (0, jnp.where(lay == l, i, 0), 0))
    row = pl.BlockSpec((1, tr, C), lambda lay, i: (lay, i, 0))
    return pl.pallas_call(
        body, name=name, grid=(depth, R // tr),
        in_specs=[part_spec(l) for l in range(depth)] + [row, row, row],
        out_specs=[row] * 4, out_shape=[jax.ShapeDtypeStruct((depth, R, C), F32)] * 4,
        compiler_params=_params(("arbitrary", "arbitrary")),
    )(*parts, w, m, v)


def _adamw_small(w, g, m, v, name):
    def body(w_ref, g_ref, m_ref, v_ref, d_ref, nm_ref, nv_ref):
        d_ref[...], nm_ref[...], nv_ref[...] = _adamw_math(w_ref[...], g_ref[...], m_ref[...], v_ref[...])

    vm = pl.BlockSpec(memory_space=pltpu.VMEM)
    return pl.pallas_call(
        body, name=name, in_specs=[vm] * 4, out_specs=[vm] * 3,
        out_shape=[jax.ShapeDtypeStruct(w.shape, F32)] * 3,
        compiler_params=pltpu.CompilerParams(vmem_limit_bytes=VMEM_LIMIT),
    )(w, g, m, v)


def _pack(arrays):
    flat = jnp.concatenate([a.reshape(-1) for a in arrays])
    pad = (-flat.shape[0]) % (SUBLANES * LANES)
    return jnp.pad(flat, (0, pad)).reshape(-1, LANES)


def _unpack(buf, like):
    flat = buf.reshape(-1)
    out, off = [], 0
    for a in like:
        out.append(flat[off:off + a.size].reshape(a.shape))
        off += a.size
    return out


def _block_diag(w):
    eye = jnp.eye(N_HEADS, dtype=w.dtype)
    return jnp.einsum('hij,hk->hikj', w, eye).reshape(GROUP_W, GROUP_W)


def _diag_blocks(w):
    return jnp.einsum('hihj->hij', w.reshape(N_HEADS, HEAD_DIM, N_HEADS, HEAD_DIM))


def _pad_rows(a):
    return jnp.pad(a, ((0, SUBLANES - a.shape[0]), (0, 0)))


def _mixer_params(l, conv_a_w, conv_r_w, conv_r_b, lru_wa, lru_ba, lru_wx, lru_bx, lru_lambda, gmlp_norm_g,
                  gmlp_ws, gmlp_bs):
    tril = jnp.tril(jnp.ones((GMLP_CHUNK, GMLP_CHUNK), dtype=bool))
    vec = jnp.stack([conv_r_b[l], lru_ba[l], lru_bx[l], lru_lambda[l], gmlp_norm_g[l]])
    return {
        "wA": _pad_rows(conv_a_w[l]), "wR": _pad_rows(conv_r_w[l]), "vec": _pad_rows(vec),
        "wa": _block_diag(lru_wa[l]).astype(MXU_DTYPE), "wx": _block_diag(lru_wx[l]).astype(MXU_DTYPE),
        "ws": jnp.where(tril[None], gmlp_ws[l], 0.0).astype(MXU_DTYPE),
        "bs": jnp.repeat(jnp.transpose(gmlp_bs[l]), HEAD_DIM, axis=1),
    }


MIXER_NAMES = ("conv_a_w", "conv_r_w", "conv_r_b", "lru_wa", "lru_ba", "lru_wx", "lru_bx", "lru_lambda",
               "gmlp_norm_g", "gmlp_ws", "gmlp_bs")
SMALL_NAMES = ("norm_g",) + MIXER_NAMES + ("final_g",)


def _local_step(x, loss_target, norm_g, get_w_in, get_w_out, emit_early, emit_late, conv_a_w, conv_r_w, conv_r_b,
                lru_wa, lru_ba, lru_wx, lru_bx, lru_lambda, gmlp_norm_g, gmlp_ws, gmlp_bs, final_g):
    depth = norm_g.shape[0]
    D = x.shape[1]
    small = (conv_a_w, conv_r_w, conv_r_b, lru_wa, lru_ba, lru_wx, lru_bx, lru_lambda, gmlp_norm_g, gmlp_ws, gmlp_bs)
    saved = []
    for l in range(depth):
        mp = _mixer_params(l, *small)
        w_in_l = get_w_in(l, x)
        z, z_g, *qkv, y_abc, hs = _inproj_mix_fwd(x, norm_g[l].reshape(1, D), w_in_l, mp, f"inproj_mix_fwd_{l}")
        attn =[_attn_fwd(qkv[p], dil, f"attn_fwd_d{dil}_{l}") for p, dil in enumerate(ATTN_DILATIONS)]
        w_out_l = get_w_out(l, y_abc)
        x_new, y, o, *lse = _outproj(x, z_g, y_abc, attn, w_out_l, f"outproj_{l}")
        saved.append((x, z, z_g, qkv, hs, y, o, lse, mp, w_in_l, w_out_l))
        x = x_new
    dx, loss, d_final_g = _loss_head(x, final_g.reshape(1, D), loss_target, "loss_head")
    token = None
    for l in reversed(range(depth)):
        x_l, z, z_g, qkv, hs, y, o, lse, mp, w_in_l, w_out_l = saved[l]
        if token is not None:
            mp = dict(mp, vec=mp["vec"] + token[0, 0])
        (dw_out, dz_abc, dz_g, do1, do4, do16, dl1, dl4, dl16, dwA, dwR, dvec, dwa, dwx, dws, dbs) = _outproj_mix_bwd(
            dx, y, w_out_l, z, z_g, hs, o, mp, f"outproj_mix_bwd_{l}")
        token = emit_early(l, dw_out, [
            dwA[:conv_a_w.shape[1]], dwR[:conv_r_w.shape[1]], dvec[0], _diag_blocks(dwa), dvec[1], _diag_blocks(dwx),
            dvec[2], dvec[3], dvec[4], dws, jnp.transpose(dbs[:, :N_HEADS])])
        g_row = norm_g[l].reshape(1, D)
        if token is not None:
            g_row = g_row + token[0, 0]
        dqkv = [_attn_bwd(qkv[p], do, lse[p], dl, dil, f"attn_bwd_d{dil}_{l}")
                for p, (dil, do, dl) in enumerate(zip(ATTN_DILATIONS, (do1, do4, do16), (dl1, dl4, dl16)))]
        dx, dz, h, dng = _inproj_bwd(x_l, g_row, dx, dz_abc, dqkv, dz_g, w_in_l, f"inproj_bwd_{l}")
        dw_in = _inproj_wgrad(h, dz, f"inproj_wgrad_{l}")
        token = emit_late(l, dw_in, [dng[0]] + ([d_final_g[0]] if l == depth - 1 else []))
    return loss[0, 0], dx
WEIGHT_NAMES = ("norm_g", "w_in", "conv_a_w", "conv_r_w", "conv_r_b", "lru_wa", "lru_ba", "lru_wx", "lru_bx",
                "lru_lambda", "gmlp_norm_g", "gmlp_ws", "gmlp_bs", "w_out", "final_g")


def kernel(x, norm_g, w_in, conv_a_w, conv_r_w, conv_r_b, lru_wa, lru_ba, lru_wx, lru_bx, lru_lambda, gmlp_norm_g, gmlp_ws, gmlp_bs, w_out, final_g, loss_target, m_norm_g, m_w_in, m_conv_a_w, m_conv_r_w, m_conv_r_b, m_lru_wa, m_lru_ba, m_lru_wx, m_lru_bx, m_lru_lambda, m_gmlp_norm_g, m_gmlp_ws, m_gmlp_bs, m_w_out, m_final_g, v_norm_g, v_w_in, v_conv_a_w, v_conv_r_w, v_conv_r_b, v_lru_wa, v_lru_ba, v_lru_wx, v_lru_bx, v_lru_lambda, v_gmlp_norm_g, v_gmlp_ws, v_gmlp_bs, v_w_out, v_final_g):
    w = dict(norm_g=norm_g, w_in=w_in, conv_a_w=conv_a_w, conv_r_w=conv_r_w, conv_r_b=conv_r_b, lru_wa=lru_wa,
             lru_ba=lru_ba, lru_wx=lru_wx, lru_bx=lru_bx, lru_lambda=lru_lambda, gmlp_norm_g=gmlp_norm_g,
             gmlp_ws=gmlp_ws, gmlp_bs=gmlp_bs, w_out=w_out, final_g=final_g)
    m = dict(norm_g=m_norm_g, w_in=m_w_in, conv_a_w=m_conv_a_w, conv_r_w=m_conv_r_w, conv_r_b=m_conv_r_b,
             lru_wa=m_lru_wa, lru_ba=m_lru_ba, lru_wx=m_lru_wx, lru_bx=m_lru_bx, lru_lambda=m_lru_lambda,
             gmlp_norm_g=m_gmlp_norm_g, gmlp_ws=m_gmlp_ws, gmlp_bs=m_gmlp_bs, w_out=m_w_out, final_g=m_final_g)
    v = dict(norm_g=v_norm_g, w_in=v_w_in, conv_a_w=v_conv_a_w, conv_r_w=v_conv_r_w, conv_r_b=v_conv_r_b,
             lru_wa=v_lru_wa, lru_ba=v_lru_ba, lru_wx=v_lru_wx, lru_bx=v_lru_bx, lru_lambda=v_lru_lambda,
             gmlp_norm_g=v_gmlp_norm_g, gmlp_ws=v_gmlp_ws, gmlp_bs=v_gmlp_bs, w_out=v_w_out, final_g=v_final_g)
    depth, D, n_loc = w_in.shape
    e_loc = w_out.shape[1]
    cx, cy, cc = _my_place()
    me = 4 * cx + 2 * cy + cc

    transposed = lambda a: jnp.transpose(a, (0, 2, 1))
    w_in_t, m_w_in_t, v_w_in_t = transposed(w_in), transposed(m_w_in), transposed(v_w_in)
    w_in_w, w_out_w = w_in_t.astype(MXU_DTYPE), w_out.astype(MXU_DTYPE)
    c_loc = conv_a_w.shape[2]
    taps = (conv_a_w, conv_r_w)
    first, _ = _exchange_start([[(w_in_w[0], True), (_pack(taps), True)], [(w_out_w[0], True)]], "gather_start_first")
    full_in = lambda g: g.reshape(N_DEV * n_loc, D)
    full_out = lambda g: g.reshape(N_DEV * e_loc, D)

    g_in0, g_taps = _exchange_wait(first[0], x, "gather_wait_in_0")
    groups = [[(w_in_w[l], True), (w_out_w[l], True)] for l in range(1, depth)]
    gathers, rest_token = _exchange_start(groups, "gather_start_rest", after=g_taps)
    g_taps = g_taps.reshape(N_DEV, -1) + rest_token[0, 0]
    conv_full, off = [], 0
    for a in taps:
        part = g_taps[:, off:off + a.size].reshape((N_DEV,) + a.shape)
        conv_full.append(jnp.transpose(part, (1, 2, 0, 3)).reshape(a.shape[:2] + (N_DEV * c_loc,)))
        off += a.size
    conv_a_full, conv_r_full = conv_full
    later = {}

    def get_w_in(l, after):
        if l == 0:
            return full_in(g_in0)
        g_in, later[l] = _exchange_wait(gathers[l - 1], after, f"gather_wait_{l}")
        return full_in(g_in)

    def get_w_out(l, after):
        if l == 0:
            return full_out(_exchange_wait(first[1], after, "gather_wait_out_0")[0])
        return full_out(later[l])

    early, late, last_token = {}, {}, [None]

    def emit_early(l, dw_out, mixer_grads):
        handles, token = _exchange_start(
            [[(dw_out.reshape(N_DEV, e_loc, D), False), (_pack(mixer_grads), True)]], f"early_start_{l}")
        early[l] = (handles[0], mixer_grads)
        return token

    def emit_late(l, dw_in, norm_grads):
        handles, token = _exchange_start([[(_pack(norm_grads), True)], [(dw_in, False)]], f"late_start_{l}")
        late[l] = (handles[0], handles[1], norm_grads)
        last_token[0] = token
        return token

    loss, grad_x = _local_step(
        x[0], loss_target[0], norm_g, get_w_in, get_w_out, emit_early, emit_late, conv_a_full, conv_r_full, conv_r_b,
        lru_wa, lru_ba, lru_wx, lru_bx, lru_lambda, gmlp_norm_g, gmlp_ws, gmlp_bs, final_g)
    loss = lax.psum(loss, ("x", "y", "c"))

    r_in, r_out, small_parts = {}, {}, []
    for l in reversed(range(depth)):
        r_out[l], r_mix = _exchange_wait(early[l][0], last_token[0], f"early_wait_{l}")
        (r_norm,) = _exchange_wait(late[l][0], last_token[0], f"late_wait_norm_{l}")
        small_parts += [r_mix, r_norm]
        if l > 0:
            (r_in[l],) = _exchange_wait(late[l][1], last_token[0], f"late_wait_{l}")
    big = {"w_out": _adamw_summed([r_out[l] for l in range(depth)], w_out, m_w_out, v_w_out, 128, "adamw_w_out")}

    sums = _sum_slots(small_parts, "sum_small_grads")
    by_layer = {}
    for i, l in enumerate(reversed(range(depth))):
        mix = _unpack(sums[2 * i], early[l][1])
        nrm = _unpack(sums[2 * i + 1], late[l][2])
        by_layer[l] = dict(zip(MIXER_NAMES, mix), norm_g=nrm[0])
        if l == depth - 1:
            g_final = nrm[1]
    g_small = {k: jnp.stack([by_layer[l][k] for l in range(depth)]) for k in ("norm_g",) + MIXER_NAMES}
    g_small["final_g"] = g_final
    for k in ("conv_a_w", "conv_r_w"):
        g_small[k] = lax.dynamic_slice_in_dim(g_small[k], me * c_loc, c_loc, axis=2)
    packs = [_pack([d[k] for k in SMALL_NAMES]) for d in (w, g_small, m, v)]
    res = _adamw_small(*packs, "adamw_small")
    like = [w[k] for k in SMALL_NAMES]
    d_s, m_s, v_s = (dict(zip(SMALL_NAMES, _unpack(r, like))) for r in res)

    (r_in[0],) = _exchange_wait(late[0][1], res[0], "late_wait_0")
    big["w_in"] = [transposed(a) for a in _adamw_summed(
        [r_in[l] for l in range(depth)], w_in_t, m_w_in_t, v_w_in_t, n_loc // 2, "adamw_w_in")]

    grad, delta, new_m, new_v = {}, {}, {}, {}
    for k in WEIGHT_NAMES:
        if k in big:
            grad[k], delta[k], new_m[k], new_v[k] = big[k]
        else:
            grad[k], delta[k], new_m[k], new_v[k] = g_small[k], d_s[k], m_s[k], v_s[k]
    return (loss, grad_x[None], *[grad[k] for k in WEIGHT_NAMES], *[delta[k] for k in WEIGHT_NAMES],
            *[new_m[k] for k in WEIGHT_NAMES], *[new_v[k] for k in WEIGHT_NAMES])
```

```python
import functools
import math

import jax
import jax.numpy as jnp
from jax import lax
from jax.experimental import pallas as pl
from jax.experimental.pallas import tpu as pltpu

F32 = jnp.float32
MXU_DTYPE = jnp.bfloat16
WIRE_DTYPE = jnp.bfloat16
MESH = pl.DeviceIdType.MESH

N_DEV = 8
GROUP_W = 256
N_HEADS = 4
HEAD_DIM = 64
N_CHUNKS = 13
N_ABC = 9
GMLP_CHUNK = 128
ATTN_BLOCK = 128
ATTN_FWD_BLOCKS_PER_STEP = 16
ATTN_BWD_BLOCKS_PER_STEP = 8
ATTN_DILATIONS = (1, 4, 16)
NORM_EPS = 1e-6
RG_C = 8.0
SUBLANES = 8
LANES = 128
VMEM_LIMIT = 56 * 1024 * 1024

ADAM_LR = 0.001
ADAM_B1 = 0.9
ADAM_B2 = 0.999
ADAM_EPS = 1e-08
ADAM_WD = 0.01
ADAM_STEP = 10

TM_MIX = 512
TM_MM = 512
TM_WGRAD = 1024


def _params(sem, vmem=VMEM_LIMIT):
    return pltpu.CompilerParams(dimension_semantics=sem, vmem_limit_bytes=vmem)


def _mm(a, b):
    return jnp.dot(a.astype(MXU_DTYPE), b.astype(MXU_DTYPE), preferred_element_type=F32)


def _mm_tn(a, b):
    return lax.dot_general(a.astype(MXU_DTYPE), b.astype(MXU_DTYPE), (((0,), (0,)), ((), ())),
                           preferred_element_type=F32)


def _mm_nt(a, b):
    return lax.dot_general(a.astype(MXU_DTYPE), b.astype(MXU_DTYPE), (((1,), (1,)), ((), ())),
                           preferred_element_type=F32)


def _sigmoid(x):
    return 0.5 * jnp.tanh(0.5 * x) + 0.5


def _silu_and_grad(x):
    s = _sigmoid(x)
    return x * s, s * (1.0 + x * (1.0 - s))


_GELU_K = math.sqrt(2.0 / math.pi)
_GELU_C = 0.044715


def _gelu_and_grad(x):
    x2 = x * x
    t = jnp.tanh(_GELU_K * (x + _GELU_C * x * x2))
    val = 0.5 * x * (1.0 + t)
    grad = 0.5 * (1.0 + t) + 0.5 * x * (1.0 - t * t) * (_GELU_K * (1.0 + 3.0 * _GELU_C * x2))
    return val, grad


def _gelu(x):
    return 0.5 * x * (1.0 + jnp.tanh(_GELU_K * (x + _GELU_C * x * x * x)))


def _expm1_nonpos(u):
    poly = 1.0 / math.factorial(9)
    for k in range(8, 0, -1):
        poly = poly * u + 1.0 / math.factorial(k)
    return jnp.where(u > -0.25, poly * u, jnp.exp(u) - 1.0)


def _softplus(x):
    return jnp.maximum(x, 0.0) + jnp.log(1.0 + jnp.exp(-jnp.abs(x)))


def _shift_down(t, halo, k):
    rolled = pltpu.roll(t, k, 0)
    hr = pltpu.roll(halo, k, 0)
    row = lax.broadcasted_iota(jnp.int32, halo.shape, 0)
    first = jnp.where(row < k, hr, rolled[0:SUBLANES])
    return jnp.concatenate([first, rolled[SUBLANES:]], axis=0)


def _shift_up(t, nxt, k):
    tm = t.shape[0]
    rolled = pltpu.roll(t, tm - k, 0)
    nr = pltpu.roll(nxt, SUBLANES - k, 0)
    row = lax.broadcasted_iota(jnp.int32, nxt.shape, 0)
    last = jnp.where(row >= SUBLANES - k, nr, rolled[tm - SUBLANES:tm])
    return jnp.concatenate([rolled[:tm - SUBLANES], last], axis=0)


def _scan_fwd(a, b):
    tm = a.shape[0]
    row = lax.broadcasted_iota(jnp.int32, a.shape, 0)
    s = 1
    while s < tm:
        a_s = pltpu.roll(a, s, 0)
        b_s = pltpu.roll(b, s, 0)
        m = row >= s
        b = jnp.where(m, a * b_s + b, b)
        a = jnp.where(m, a * a_s, a)
        s *= 2
    return a, b


def _scan_rev(a, g):
    tm = a.shape[0]
    row = lax.broadcasted_iota(jnp.int32, a.shape, 0)
    s = 1
    while s < tm:
        a_s = pltpu.roll(a, tm - s, 0)
        g_s = pltpu.roll(g, tm - s, 0)
        m = row < tm - s
        g = jnp.where(m, g + a * g_s, g)
        a = jnp.where(m, a * a_s, a)
        s *= 2
    return g


def _group_rows(scr_ref, row, n_groups):
    return jnp.concatenate([scr_ref[pl.ds(c, 1), pl.ds(row, n_groups, stride=SUBLANES), :][0]
                            for c in range(scr_ref.shape[0])], axis=1)


def _spread_rows(rows_ref, n_groups, w):
    return jnp.concatenate([jnp.broadcast_to(rows_ref[g:g + 1, :], (SUBLANES, w)) for g in range(n_groups)], axis=0)


def _scan_groups(a, b, reverse):
    tm, w = a.shape
    shape3 = (tm // SUBLANES, SUBLANES, w)
    a3, b3 = a.reshape(shape3), b.reshape(shape3)
    sub = lax.broadcasted_iota(jnp.int32, shape3, 1)
    s = 1
    while s < SUBLANES:
        shift = SUBLANES - s if reverse else s
        a_s = pltpu.roll(a3, shift, 1)
        b_s = pltpu.roll(b3, shift, 1)
        m = (sub < SUBLANES - s) if reverse else (sub >= s)
        b3 = jnp.where(m, a3 * b_s + b3, b3)
        a3 = jnp.where(m, a3 * a_s, a3)
        s *= 2
    return a3.reshape(tm, w), b3.reshape(tm, w)


def _scan_fwd_tile(a, b, h_in, sa_ref, sb_ref, sc_ref):
    tm, w = a.shape
    n_groups = tm // SUBLANES
    a_loc, b_loc = _scan_groups(a, b, False)
    _put(sa_ref, a_loc)
    _put(sb_ref, b_loc)
    a_end, b_end = _scan_fwd(_group_rows(sa_ref, SUBLANES - 1, n_groups), _group_rows(sb_ref, SUBLANES - 1, n_groups))
    h_end = b_end + a_end * h_in
    sc_ref[...] = _shift_down(h_end, jnp.broadcast_to(h_in, (SUBLANES, w)), 1)
    return b_loc + a_loc * _spread_rows(sc_ref, n_groups, w), h_end


def _scan_rev_tile(a, g, sa_ref, sb_ref, sc_ref):
    tm, w = a.shape
    n_groups = tm // SUBLANES
    a_loc, g_loc = _scan_groups(a, g, True)
    _put(sa_ref, a_loc)
    _put(sb_ref, g_loc)
    d_first = _scan_rev(_group_rows(sa_ref, 0, n_groups), _group_rows(sb_ref, 0, n_groups))
    sc_ref[...] = _shift_up(d_first, jnp.zeros((SUBLANES, w), F32), 1)
    return g_loc + a_loc * _spread_rows(sc_ref, n_groups, w)


def _lane_scratch(tm, w):
    return pltpu.VMEM((w // LANES, tm, LANES), F32)


def _put(scr_ref, val):
    for c in range(scr_ref.shape[0]):
        scr_ref[c] = val[:, c * LANES:(c + 1) * LANES].astype(F32)


def _get(scr_ref):
    return jnp.concatenate([scr_ref[c] for c in range(scr_ref.shape[0])], axis=1)


MAX_ROW_STRIDE = 4


def _strided_rows(c, start, n, stride):
    return (pl.ds(c, 1), pl.ds(start, n, stride=stride), slice(None))


def _deinterleave(src_ref, dst_ref, dil, tmp_ref=None):
    nc, tm, _ = src_ref.shape
    w = nc * LANES
    s1 = min(dil, MAX_ROW_STRIDE)
    s2 = dil // s1
    if s2 > 1:
        for r0 in range(s1):
            for c in range(nc):
                tmp_ref[c, r0 * (tm // s1):(r0 + 1) * (tm // s1), :] = src_ref[_strided_rows(c, r0, tm // s1, s1)][0]
    for r in range(dil):
        r1, r0 = divmod(r, s1)
        for c in range(nc):
            if dil == 1:
                piece = src_ref[c]
            elif s2 == 1:
                piece = src_ref[_strided_rows(c, r, tm // dil, dil)][0]
            else:
                piece = tmp_ref[_strided_rows(c, r0 * (tm // s1) + r1, tm // dil, s2)][0]
            dst_ref[:, r * w + c * LANES:r * w + (c + 1) * LANES] = piece.astype(dst_ref.dtype)


def _interleave(src_ref, dst_ref, dil, tmp_ref=None):
    nc, tm, _ = dst_ref.shape
    w = nc * LANES
    s1 = min(dil, MAX_ROW_STRIDE)
    s2 = dil // s1
    for r in range(dil):
        r1, r0 = divmod(r, s1)
        for c in range(nc):
            piece = src_ref[:, r * w + c * LANES:r * w + (c + 1) * LANES].astype(F32)[None]
            if s2 == 1:
                dst_ref[_strided_rows(c, r, tm // dil, dil)] = piece
            else:
                tmp_ref[_strided_rows(c, r0 * (tm // s1) + r1, tm // dil, s2)] = piece
    if s2 > 1:
        for r0 in range(s1):
            for c in range(nc):
                dst_ref[_strided_rows(c, r0, tm // s1, s1)] = (
                    tmp_ref[c, r0 * (tm // s1):(r0 + 1) * (tm // s1), :][None])


def _dilated_spec(tm, w, dil, index=lambda i: i):
    return pl.BlockSpec((tm // dil, dil * w), lambda i: (index(i), 0))


def _dilated_shape(S, w, dil, dtype):
    return jax.ShapeDtypeStruct((S // dil, dil * w), dtype)


def _head_masks(shape):
    lane = lax.broadcasted_iota(jnp.int32, shape, 1)
    return [(lane >= h * HEAD_DIM) & (lane < (h + 1) * HEAD_DIM) for h in range(N_HEADS)]


def _colsum(v):
    return jnp.sum(v, axis=0, keepdims=True)


def _conv_a(z_of, halo_of, w_ref):
    p = z_of(2) * z_of(0)
    p_h = halo_of(2) * halo_of(0)
    cv = w_ref[2:3, :] * p + w_ref[1:2, :] * _shift_down(p, p_h, 1) + w_ref[0:1, :] * _shift_down(p, p_h, 2)
    return p, p_h, cv


def _lru_gates(z_of, halo_of, wr_ref, vec_ref, wa_ref, wx_ref):
    rx = z_of(4)
    rx_h = halo_of(4)
    sh = [rx, _shift_down(rx, rx_h, 1), _shift_down(rx, rx_h, 2), _shift_down(rx, rx_h, 3)]
    xc = (wr_ref[3:4, :] * sh[0] + wr_ref[2:3, :] * sh[1] + wr_ref[1:2, :] * sh[2]
          + wr_ref[0:1, :] * sh[3] + vec_ref[0:1, :])
    ga = _sigmoid(jnp.dot(xc.astype(MXU_DTYPE), wa_ref[...], preferred_element_type=F32) + vec_ref[1:2, :])
    gi = _sigmoid(jnp.dot(xc.astype(MXU_DTYPE), wx_ref[...], preferred_element_type=F32) + vec_ref[2:3, :])
    sp = _softplus(-vec_ref[3:4, :])
    log_a = (-RG_C * ga) * sp
    a = jnp.exp(log_a)
    mult = jnp.sqrt(-_expm1_nonpos(2.0 * log_a))
    return xc, sh, ga, gi, a, mult, sp


def _gmlp_fwd(z_of, vec_ref, ws_ref, bs_ref, tm):
    u = _gelu(z_of(6))
    gv = _gelu(z_of(7))
    rr = lax.rsqrt(jnp.mean(gv * gv, axis=-1, keepdims=True) + NORM_EPS)
    vn = (gv * rr) * vec_ref[4:5, :]
    masks = _head_masks((GMLP_CHUNK, GROUP_W))
    parts = []
    for c in range(tm // GMLP_CHUNK):
        vc = vn[c * GMLP_CHUNK:(c + 1) * GMLP_CHUNK].astype(MXU_DTYPE)
        acc = bs_ref[...]
        for h in range(N_HEADS):
            acc = acc + jnp.where(masks[h], jnp.dot(ws_ref[h], vc, preferred_element_type=F32), 0.0)
        parts.append(acc)
    return u, gv, rr, vn, jnp.concatenate(parts, axis=0)


def _mix_specs(tm, S, order):
    const2 = lambda shape: pl.BlockSpec(shape, lambda i: (0, 0))
    return [const2((SUBLANES, GROUP_W)), const2((SUBLANES, GROUP_W)), const2((SUBLANES, GROUP_W)),
            const2((GROUP_W, GROUP_W)), const2((GROUP_W, GROUP_W)),
            pl.BlockSpec((N_HEADS, GMLP_CHUNK, GMLP_CHUNK), lambda i: (0, 0, 0)),
            const2((GMLP_CHUNK, GROUP_W))]


def _inproj_mix_fwd(x, g, w_t, mp, name):
    S, D = x.shape
    N = w_t.shape[0]
    tm = TM_MIX
    hb = tm // SUBLANES
    n_abc = N_ABC * GROUP_W
    n_qkv = 3 * GROUP_W

    def body(x_ref, g_ref, w_ref, wA_ref, wR_ref, vec_ref, wa_ref, wx_ref, ws_ref, bs_ref,
             z_ref, zg_ref, q1_ref, q4_ref, q16_ref, y_ref, h_ref,
             qkv_ref, halo_ref, carry_ref, sa_ref, sb_ref, sc_ref, tmp_ref):
        @pl.when(pl.program_id(0) == 0)
        def _():
            halo_ref[...] = jnp.zeros_like(halo_ref)
            carry_ref[...] = jnp.zeros_like(carry_ref)

        xv = x_ref[...]
        r = lax.rsqrt(jnp.mean(xv * xv, axis=-1, keepdims=True) + NORM_EPS)
        hn = ((xv * r) * g_ref[...]).astype(MXU_DTYPE)
        z_ref[...] = _mm_nt(hn, w_ref[0:n_abc, :])
        _put(qkv_ref, _mm_nt(hn, w_ref[n_abc:n_abc + n_qkv, :]))
        zg_ref[...] = _mm_nt(hn, w_ref[n_abc + n_qkv:, :])
        for dil, ref in zip(ATTN_DILATIONS, (q1_ref, q4_ref, q16_ref)):
            _deinterleave(qkv_ref, ref, dil, tmp_ref)

        z_of = lambda c: z_ref[:, c * GROUP_W:(c + 1) * GROUP_W]
        halo_of = lambda c: halo_ref[:, c * GROUP_W:(c + 1) * GROUP_W]

        _, _, cv = _conv_a(z_of, halo_of, wA_ref)
        y_ref[:, 0:GROUP_W] = (z_of(1) * cv * _silu_and_grad(z_of(3))[0]).astype(y_ref.dtype)

        xc, _, _, gi, a, mult, _ = _lru_gates(z_of, halo_of, wR_ref, vec_ref, wa_ref, wx_ref)
        b = mult * (gi * xc)
        h, h_end = _scan_fwd_tile(a, b, carry_ref[SUBLANES - 1:SUBLANES, :], sa_ref, sb_ref, sc_ref)
        h_ref[...] = h
        carry_ref[...] = h_end[hb - SUBLANES:hb]
        y_ref[:, GROUP_W:2 * GROUP_W] = (h * _silu_and_grad(z_of(5))[0]).astype(y_ref.dtype)

        u, _, _, _, sp = _gmlp_fwd(z_of, vec_ref, ws_ref, bs_ref, tm)
        y_ref[:, 2 * GROUP_W:3 * GROUP_W] = (u * sp * _silu_and_grad(z_of(8))[0]).astype(y_ref.dtype)
        halo_ref[...] = z_ref[tm - SUBLANES:tm, :]

    row = lambda wd: pl.BlockSpec((tm, wd), lambda i: (i, 0))
    return pl.pallas_call(
        body, name=name, grid=(S // tm,),
        in_specs=[row(D), pl.BlockSpec((1, D), lambda i: (0, 0)),
                  pl.BlockSpec((N, D), lambda i: (0, 0), pipeline_mode=pl.Buffered(1))] + _mix_specs(tm, S, "fwd"),
        out_specs=[row(n_abc), row(GROUP_W)] + [_dilated_spec(tm, n_qkv, dil) for dil in ATTN_DILATIONS]
                  + [row(3 * GROUP_W), row(GROUP_W)],
        out_shape=[jax.ShapeDtypeStruct((S, n_abc), F32), jax.ShapeDtypeStruct((S, GROUP_W), F32)]
                  + [_dilated_shape(S, n_qkv, dil, MXU_DTYPE) for dil in ATTN_DILATIONS]
                  + [jax.ShapeDtypeStruct((S, 3 * GROUP_W), MXU_DTYPE), jax.ShapeDtypeStruct((S, GROUP_W), F32)],
        scratch_shapes=[_lane_scratch(tm, n_qkv), pltpu.VMEM((SUBLANES, n_abc), F32),
                        pltpu.VMEM((SUBLANES, GROUP_W), F32), _lane_scratch(tm, GROUP_W), _lane_scratch(tm, GROUP_W),
                        pltpu.VMEM((hb, GROUP_W), F32), _lane_scratch(tm, n_qkv)],
        compiler_params=_params(("arbitrary",)),
    )(x, g, w_t, mp["wA"], mp["wR"], mp["vec"], mp["wa"], mp["wx"], mp["ws"], mp["bs"])


_NEG = -1e30


def _slope(h):
    return 2.0 ** (-8.0 * (h + 1) / N_HEADS)


def _attn_bias(dil, offsets, n_keys):
    shape = (ATTN_BLOCK, n_keys)
    qi = lax.broadcasted_iota(jnp.int32, shape, 0)
    ki = lax.broadcasted_iota(jnp.int32, shape, 1)
    blocks = []
    for f in offsets:
        delta = qi + f - ki
        valid = (delta >= 0) & (delta <= ATTN_BLOCK)
        dist = (delta * dil).astype(F32)
        for h in range(N_HEADS):
            blocks.append(jnp.where(valid, -_slope(h) * dist, _NEG))
    return jnp.concatenate(blocks, axis=0)


def _stack_heads(t, masks):
    return jnp.concatenate([jnp.where(m, t, jnp.zeros_like(t)) for m in masks], axis=0)


def _unstack_heads(t4, masks, base=0):
    out = t4[base * ATTN_BLOCK:(base + 1) * ATTN_BLOCK]
    for h in range(1, N_HEADS):
        out = jnp.where(masks[h], t4[(base + h) * ATTN_BLOCK:(base + h + 1) * ATTN_BLOCK], out)
    return out


def _attn_fwd(qkv, dil, name):
    rows = qkv.shape[0]
    nb = rows // ATTN_BLOCK
    scale = 1.0 / math.sqrt(HEAD_DIM)
    B = ATTN_BLOCK
    per_step = min(ATTN_FWD_BLOCKS_PER_STEP, nb)

    def body(q_ref, kc_ref, kp_ref, vc_ref, vp_ref, o_ref, l_ref, bias_ref):
        n = pl.program_id(1)

        @pl.when(n == 0)
        def _():
            bias_ref[...] = _attn_bias(dil, (B,), 2 * B)

        masks = _head_masks((B, GROUP_W))
        for j in range(per_step):
            own = slice(j * B, (j + 1) * B)
            before = slice((j - 1) * B, j * B)
            qs = _stack_heads(q_ref[own], masks)
            keys = jnp.concatenate([kp_ref[...] if j == 0 else kc_ref[before], kc_ref[own]], axis=0)
            vals = jnp.concatenate([vp_ref[...] if j == 0 else vc_ref[before], vc_ref[own]], axis=0)
            s = _mm_nt(qs, keys) * scale + bias_ref[...]
            if j == 0:
                key_col = lax.broadcasted_iota(jnp.int32, s.shape, 1)
                s = jnp.where((n == 0) & (key_col < B), _NEG, s)
            m = jnp.max(s, axis=-1, keepdims=True)
            p = jnp.exp(s - m)
            l = jnp.sum(p, axis=-1, keepdims=True)
            o4 = jnp.dot(p.astype(MXU_DTYPE), vals, preferred_element_type=F32)
            o_ref[own] = _unstack_heads(o4, masks) / _unstack_heads(jnp.broadcast_to(l, o4.shape), masks)
            l_ref[own] = _unstack_heads(jnp.broadcast_to(m + jnp.log(l), o4.shape), masks)

    blk = (per_step * B, GROUP_W)
    cur = lambda c: pl.BlockSpec(blk, lambda r, n: (n, r * 3 + c))
    prev = lambda c: pl.BlockSpec((B, GROUP_W), lambda r, n: (jnp.maximum(n * per_step - 1, 0), r * 3 + c))
    out = pl.BlockSpec(blk, lambda r, n: (n, r))
    return pl.pallas_call(
        body, name=name, grid=(dil, nb // per_step),
        in_specs=[cur(0), cur(1), prev(1), cur(2), prev(2)],
        out_specs=[out, out],
        out_shape=[jax.ShapeDtypeStruct((rows, dil * GROUP_W), F32)] * 2,
        scratch_shapes=[pltpu.VMEM((N_HEADS * ATTN_BLOCK, 2 * ATTN_BLOCK), F32)],
        compiler_params=_params(("parallel", "arbitrary")),
    )(qkv, qkv, qkv, qkv, qkv)


def _outproj(x, z_g, y_abc, attn, w_out, name):
    S, D = x.shape
    tm = TM_MM
    n_abc = 3 * GROUP_W

    def body(x_ref, g_ref, yabc_ref, o1, l1, o2, l2, o3, l3, w_ref,
             xn_ref, y_ref, o_ref, lse1_ref, lse4_ref, lse16_ref, so2, sl2, so3, sl3, slse, tmp_ref):
        for src, dst, dil in ((o2, so2, ATTN_DILATIONS[1]), (l2, sl2, ATTN_DILATIONS[1]),
                              (o3, so3, ATTN_DILATIONS[2]), (l3, sl3, ATTN_DILATIONS[2])):
            _interleave(src, dst, dil, tmp_ref)
        la, lb, lc = l1[...], _get(sl2), _get(sl3)
        mx = jnp.maximum(jnp.maximum(la, lb), lc)
        ea, eb, ec = jnp.exp(la - mx), jnp.exp(lb - mx), jnp.exp(lc - mx)
        den = ea + eb + ec
        o = (ea * o1[...] + eb * _get(so2) + ec * _get(so3)) / den
        o_ref[...] = o
        _put(slse, mx + jnp.log(den))
        for dil, ref in zip(ATTN_DILATIONS, (lse1_ref, lse4_ref, lse16_ref)):
            _deinterleave(slse, ref, dil, tmp_ref)
        y_d = o * _silu_and_grad(g_ref[...])[0]
        y_ref[:, 0:n_abc] = yabc_ref[...].astype(MXU_DTYPE)
        y_ref[:, n_abc:] = y_d.astype(MXU_DTYPE)
        xn_ref[...] = x_ref[...] + jnp.dot(y_ref[...], w_ref[...], preferred_element_type=F32)

    row = lambda w: pl.BlockSpec((tm, w), lambda i: (i, 0))
    dil_specs = [_dilated_spec(tm, GROUP_W, dil) for dil in ATTN_DILATIONS]
    (o1, l1), (o2, l2), (o3, l3) = attn
    return pl.pallas_call(
        body, name=name, grid=(S // tm,),
        in_specs=[row(D), row(GROUP_W), row(n_abc)] + [sp for sp in dil_specs for _ in range(2)]
                 + [pl.BlockSpec(w_out.shape, lambda i: (0, 0))],
        out_specs=[row(D), row(4 * GROUP_W), row(GROUP_W)] + dil_specs,
        out_shape=[jax.ShapeDtypeStruct((S, D), F32), jax.ShapeDtypeStruct((S, 4 * GROUP_W), MXU_DTYPE),
                   jax.ShapeDtypeStruct((S, GROUP_W), F32)]
                  + [_dilated_shape(S, GROUP_W, dil, F32) for dil in ATTN_DILATIONS],
        scratch_shapes=[_lane_scratch(tm, GROUP_W)] * 6,
        compiler_params=_params(("parallel",)),
    )(x, z_g, y_abc, o1, l1, o2, l2, o3, l3, w_out)


def _loss_head(x, g, target, name):
    S, D = x.shape
    tm = TM_MM

    def body(x_ref, g_ref, t_ref, dx_ref, loss_ref, dg_ref):
        i = pl.program_id(0)

        @pl.when(i == 0)
        def _():
            loss_ref[...] = jnp.zeros_like(loss_ref)
            dg_ref[...] = jnp.zeros_like(dg_ref)

        xv = x_ref[...]
        r = lax.rsqrt(jnp.mean(xv * xv, axis=-1, keepdims=True) + NORM_EPS)
        xn = xv * r
        err = xn * g_ref[...] - t_ref[...]
        per_tok = jnp.mean(err * err, axis=-1, keepdims=True)
        loss_ref[...] += 0.5 * jnp.sum(per_tok, axis=0, keepdims=True)
        dout = err * (1.0 / D)
        dg_ref[...] += _colsum(dout * xn)
        dxn = dout * g_ref[...]
        dx_ref[...] = r * (dxn - xn * jnp.mean(dxn * xn, axis=-1, keepdims=True))

    row = pl.BlockSpec((tm, D), lambda i: (i, 0))
    return pl.pallas_call(
        body, name=name, grid=(S // tm,),
        in_specs=[row, pl.BlockSpec((1, D), lambda i: (0, 0)), row],
        out_specs=[row, pl.BlockSpec((1, LANES), lambda i: (0, 0)), pl.BlockSpec((1, D), lambda i: (0, 0))],
        out_shape=[jax.ShapeDtypeStruct((S, D), F32), jax.ShapeDtypeStruct((1, LANES), F32),
                   jax.ShapeDtypeStruct((1, D), F32)],
        compiler_params=_params(("arbitrary",)),
    )(x, g, target)


def _outproj_mix_bwd(dx, y, w_out, z, z_g, hs, o, mp, name):
    S, D = dx.shape
    E = y.shape[1]
    tm = TM_MIX
    hb = tm // SUBLANES
    nT = S // tm
    last_blk = S // SUBLANES - 1
    wcols = N_ABC * GROUP_W

    def body(dx_ref, y_ref, w_ref, z_ref, zh_ref, zn_ref, zg_ref, h_ref, hh_ref, o_ref,
             wA_ref, wR_ref, vec_ref, wa_ref, wx_ref, ws_ref, bs_ref,
             dw_ref, dz_ref, dzg_ref, do1_ref, do4_ref, do16_ref, dl1_ref, dl4_ref, dl16_ref,
             dwA_ref, dwR_ref, dvec_ref, dwa_ref, dwx_ref, dws_ref, dbs_ref,
             hcarry_ref, xcarry_ref, bsacc_ref, do_ref, dl_ref, sa_ref, sb_ref, sc_ref, dy_ref, dyn_ref, acc_ref,
             tmp_ref):
        i = pl.program_id(0)
        ti = nT - 1 - i

        @pl.when(i == 0)
        def _():
            acc_ref[...] = jnp.zeros_like(acc_ref)
            dyn_ref[...] = jnp.zeros_like(dyn_ref)
            hcarry_ref[...] = jnp.zeros_like(hcarry_ref)
            xcarry_ref[...] = jnp.zeros_like(xcarry_ref)
            bsacc_ref[...] = jnp.zeros_like(bsacc_ref)
            dwA_ref[...] = jnp.zeros_like(dwA_ref)
            dwR_ref[...] = jnp.zeros_like(dwR_ref)
            dvec_ref[...] = jnp.zeros_like(dvec_ref)
            dwa_ref[...] = jnp.zeros_like(dwa_ref)
            dwx_ref[...] = jnp.zeros_like(dwx_ref)
            dws_ref[...] = jnp.zeros_like(dws_ref)
            dbs_ref[...] = jnp.zeros_like(dbs_ref)

        dxb = dx_ref[...].astype(MXU_DTYPE)
        dy_ref[...] = _mm_nt(dxb, w_ref[...])
        acc_ref[...] += _mm_tn(y_ref[...], dxb)

        @pl.when(i == nT - 1)
        def _():
            dw_ref[...] = acc_ref[...].astype(dw_ref.dtype)

        has_prev = ti > 0
        has_next = i > 0
        col = lambda c: slice(c * GROUP_W, (c + 1) * GROUP_W)
        z_of = lambda c: z_ref[:, col(c)]
        halo_of = lambda c: jnp.where(has_prev, zh_ref[:, col(c)], 0.0)
        next_of = lambda c: zn_ref[:, col(c)]

        p, p_h, cv = _conv_a(z_of, halo_of, wA_ref)
        sg, dsg = _silu_and_grad(z_of(3))
        a_b = z_of(1)
        dya = dy_ref[:, col(0)]
        dcv = dya * a_b * sg
        dcv_n = jnp.where(has_next, dyn_ref[...] * next_of(1) * _silu_and_grad(next_of(3))[0], 0.0)
        dp = (wA_ref[2:3, :] * dcv + wA_ref[1:2, :] * _shift_up(dcv, dcv_n, 1)
              + wA_ref[0:1, :] * _shift_up(dcv, dcv_n, 2))
        dwA_ref[2:3, :] += _colsum(dcv * p)
        dwA_ref[1:2, :] += _colsum(dcv * _shift_down(p, p_h, 1))
        dwA_ref[0:1, :] += _colsum(dcv * _shift_down(p, p_h, 2))
        def put_dz(c, val):
            dz_ref[:, col(c)] = val.astype(dz_ref.dtype)

        put_dz(0, dp * z_of(2))
        put_dz(1, dya * cv * sg)
        put_dz(2, dp * z_of(0))
        put_dz(3, dya * a_b * cv * dsg)

        xc, sh, ga, gi, a, mult, sp = _lru_gates(z_of, halo_of, wR_ref, vec_ref, wa_ref, wx_ref)
        h = h_ref[...]
        h_prev = _shift_down(h, jnp.where(has_prev, hh_ref[...], 0.0), 1)
        sgr, dsgr = _silu_and_grad(z_of(5))
        dyb = dy_ref[:, col(1)]
        put_dz(5, dyb * h * dsgr)
        row = lax.broadcasted_iota(jnp.int32, (tm, GROUP_W), 0)
        g_in = dyb * sgr + jnp.where(row == tm - 1, hcarry_ref[0:1, :], 0.0)
        a_up = _shift_up(a, jnp.zeros((SUBLANES, GROUP_W), F32), 1)
        dH = _scan_rev_tile(a_up, g_in, sa_ref, sb_ref, sc_ref)
        hcarry_ref[...] = (a * dH)[0:SUBLANES]
        da = dH * h_prev
        gx = gi * xc
        dmult = dH * gx
        dgi = dH * mult * xc
        dxc = dH * mult * gi
        dlog_a = da * a - dmult * (a * a) / mult
        dga = dlog_a * (-RG_C * sp)
        dlam_row = _colsum(dlog_a * (-RG_C * ga)) * (-_sigmoid(-vec_ref[3:4, :]))
        dpre_a = dga * ga * (1.0 - ga)
        dpre_i = dgi * gi * (1.0 - gi)
        dwa_ref[...] += _mm_tn(xc, dpre_a)
        dwx_ref[...] += _mm_tn(xc, dpre_i)
        dxc = dxc + _mm_nt(dpre_a, wa_ref[...]) + _mm_nt(dpre_i, wx_ref[...])
        dvec_ref[0:1, :] += _colsum(dxc)
        dvec_ref[1:2, :] += _colsum(dpre_a)
        dvec_ref[2:3, :] += _colsum(dpre_i)
        dvec_ref[3:4, :] += dlam_row
        for k in range(4):
            dwR_ref[k:k + 1, :] += _colsum(dxc * sh[3 - k])
        dxc_n = xcarry_ref[...]
        put_dz(4, wR_ref[3:4, :] * dxc + wR_ref[2:3, :] * _shift_up(dxc, dxc_n, 1)
               + wR_ref[1:2, :] * _shift_up(dxc, dxc_n, 2) + wR_ref[0:1, :] * _shift_up(dxc, dxc_n, 3))
        xcarry_ref[...] = dxc[0:SUBLANES]

        c_u, c_v = z_of(6), z_of(7)
        u, du_dx = _gelu_and_grad(c_u)
        gv, dgv_dx = _gelu_and_grad(c_v)
        rr = lax.rsqrt(jnp.mean(gv * gv, axis=-1, keepdims=True) + NORM_EPS)
        xhat = gv * rr
        g_c = vec_ref[4:5, :]
        vn = xhat * g_c
        masks = _head_masks((GMLP_CHUNK, GROUP_W))
        tri_r = lax.broadcasted_iota(jnp.int32, (GMLP_CHUNK, GMLP_CHUNK), 0)
        tri_c = lax.broadcasted_iota(jnp.int32, (GMLP_CHUNK, GMLP_CHUNK), 1)
        tril = tri_r >= tri_c
        sgc, dsgc = _silu_and_grad(z_of(8))
        dyc = dy_ref[:, col(2)]
        dsp_full = dyc * u * sgc
        sp_parts, dvn_parts = [], []
        for c in range(tm // GMLP_CHUNK):
            rs = slice(c * GMLP_CHUNK, (c + 1) * GMLP_CHUNK)
            vc = vn[rs].astype(MXU_DTYPE)
            dsp_c = dsp_full[rs]
            bsacc_ref[...] += dsp_c
            acc = bs_ref[...]
            dvn_c = jnp.zeros((GMLP_CHUNK, GROUP_W), F32)
            for h in range(N_HEADS):
                w_h = ws_ref[h]
                acc = acc + jnp.where(masks[h], jnp.dot(w_h, vc, preferred_element_type=F32), 0.0)
                dsp_h = jnp.where(masks[h], dsp_c, 0.0).astype(MXU_DTYPE)
                dvn_c = dvn_c + _mm_tn(w_h, dsp_h)
                dws_ref[h] += jnp.where(tril, _mm_nt(dsp_h, vc), 0.0)
            sp_parts.append(acc)
            dvn_parts.append(dvn_c)
        spv = jnp.concatenate(sp_parts, axis=0)
        dvn = jnp.concatenate(dvn_parts, axis=0)
        put_dz(6, dyc * spv * sgc * du_dx)
        put_dz(8, dyc * u * spv * dsgc)
        dvec_ref[4:5, :] += _colsum(dvn * xhat)
        dgvn = dvn * g_c
        dgv = rr * (dgvn - xhat * jnp.mean(dgvn * xhat, axis=-1, keepdims=True))
        put_dz(7, dgv * dgv_dx)

        sgd, dsgd = _silu_and_grad(zg_ref[...])
        dyd = dy_ref[:, col(3)]
        ov = o_ref[...]
        do = dyd * sgd
        _put(do_ref, do)
        dzg_ref[...] = (dyd * ov * dsgd).astype(dzg_ref.dtype)
        prod = do * ov
        tmasks = _head_masks((tm, GROUP_W))
        dl = jnp.zeros((tm, GROUP_W), F32)
        for h in range(N_HEADS):
            dl = jnp.where(tmasks[h], jnp.sum(jnp.where(tmasks[h], prod, 0.0), axis=-1, keepdims=True), dl)
        _put(dl_ref, dl)
        for dil, d_out, l_out in zip(ATTN_DILATIONS, (do1_ref, do4_ref, do16_ref), (dl1_ref, dl4_ref, dl16_ref)):
            _deinterleave(do_ref, d_out, dil, tmp_ref)
            _deinterleave(dl_ref, l_out, dil, tmp_ref)

        @pl.when(i == nT - 1)
        def _():
            acc = bsacc_ref[...]
            lane = lax.broadcasted_iota(jnp.int32, (GMLP_CHUNK, LANES), 1)
            out = jnp.zeros((GMLP_CHUNK, LANES), F32)
            for h in range(N_HEADS):
                out = jnp.where(lane == h, jnp.sum(jnp.where(masks[h], acc, 0.0), axis=-1, keepdims=True), out)
            dbs_ref[...] = out

        dyn_ref[...] = dy_ref[0:SUBLANES, 0:GROUP_W]

    rev = lambda w: pl.BlockSpec((tm, w), lambda i: (nT - 1 - i, 0))
    prev8 = lambda w: pl.BlockSpec((SUBLANES, w), lambda i: (jnp.maximum((nT - 1 - i) * hb - 1, 0), 0))
    next8 = lambda w: pl.BlockSpec((SUBLANES, w), lambda i: (jnp.minimum((nT - i) * hb, last_blk), 0))
    const2 = lambda shape: pl.BlockSpec(shape, lambda i: (0, 0))
    dil_specs = [_dilated_spec(tm, GROUP_W, dil, lambda i: nT - 1 - i) for dil in ATTN_DILATIONS]
    dil_shapes = [_dilated_shape(S, GROUP_W, dil, F32) for dil in ATTN_DILATIONS]
    small = (SUBLANES, GROUP_W)
    sq = (GROUP_W, GROUP_W)
    ws_shape = (N_HEADS, GMLP_CHUNK, GMLP_CHUNK)
    return pl.pallas_call(
        body, name=name, grid=(nT,),
        in_specs=[rev(D), rev(E), pl.BlockSpec((E, D), lambda i: (0, 0), pipeline_mode=pl.Buffered(1)),
                  rev(wcols), prev8(wcols), next8(wcols), rev(GROUP_W), rev(GROUP_W), prev8(GROUP_W), rev(GROUP_W)]
                 + _mix_specs(tm, S, "bwd"),
        out_specs=[const2((E, D)), rev(wcols), rev(GROUP_W)] + dil_specs + dil_specs
                  + [const2(small), const2(small), const2(small), const2(sq), const2(sq),
                     pl.BlockSpec(ws_shape, lambda i: (0, 0, 0)), const2((GMLP_CHUNK, LANES))],
        out_shape=[jax.ShapeDtypeStruct((E, D), WIRE_DTYPE),
                   jax.ShapeDtypeStruct((S, wcols), MXU_DTYPE), jax.ShapeDtypeStruct((S, GROUP_W), MXU_DTYPE)]
                  + [_dilated_shape(S, GROUP_W, dil, MXU_DTYPE) for dil in ATTN_DILATIONS] + dil_shapes
                  + [jax.ShapeDtypeStruct(small, F32)] * 3 + [jax.ShapeDtypeStruct(sq, F32)] * 2
                  + [jax.ShapeDtypeStruct(ws_shape, F32), jax.ShapeDtypeStruct((GMLP_CHUNK, LANES), F32)],
        scratch_shapes=[pltpu.VMEM(small, F32), pltpu.VMEM(small, F32), pltpu.VMEM((GMLP_CHUNK, GROUP_W), F32),
                        _lane_scratch(tm, GROUP_W), _lane_scratch(tm, GROUP_W),
                        _lane_scratch(tm, GROUP_W), _lane_scratch(tm, GROUP_W), pltpu.VMEM((hb, GROUP_W), F32),
                        pltpu.VMEM((tm, E), F32), pltpu.VMEM(small, F32), pltpu.VMEM((E, D), F32),
                        _lane_scratch(tm, GROUP_W)],
        compiler_params=_params(("arbitrary",)),
    )(dx, y, w_out, z, z, z, z_g, hs, hs, o, mp["wA"], mp["wR"], mp["vec"], mp["wa"], mp["wx"], mp["ws"], mp["bs"])


def _attn_bwd(qkv, do, lse, delta, dil, name):
    rows = qkv.shape[0]
    nb = rows // ATTN_BLOCK
    scale = 1.0 / math.sqrt(HEAD_DIM)
    B = ATTN_BLOCK
    per_step = min(ATTN_BWD_BLOCKS_PER_STEP, nb)
    n_steps = nb // per_step

    def body(qc_ref, qn_ref, kc_ref, kp_ref, vc_ref, vp_ref, doc_ref, don_ref, lc_ref, ln_ref, dc_ref, dn_ref,
             dq_ref, dk_ref, dv_ref, bias_ref, bias_next_ref):
        n = pl.program_id(1)

        @pl.when(n == 0)
        def _():
            bias_ref[...] = _attn_bias(dil, (B,), 2 * B)
            bias_next_ref[...] = _attn_bias(dil, (B,), B)

        masks = _head_masks((B, GROUP_W))

        def per_row(tile):
            return jnp.concatenate([jnp.max(jnp.where(masks[h], tile, _NEG), axis=-1, keepdims=True)
                                    for h in range(N_HEADS)], axis=0)

        def grads(q, dov, lse_tile, dl_tile, keys, vals, bias, dead):
            qs = _stack_heads(q, masks)
            dos = _stack_heads(dov.astype(MXU_DTYPE), masks)
            s = _mm_nt(qs, keys) * scale + bias
            if dead is not None:
                s = jnp.where(dead(s.shape), _NEG, s)
            p = jnp.exp(s - per_row(lse_tile))
            ds = (p * (_mm_nt(dos, vals) - per_row(dl_tile)) * scale).astype(MXU_DTYPE)
            return ds, _mm_tn(ds, qs), _mm_tn(p.astype(MXU_DTYPE), dos)

        for j in range(per_step):
            own = slice(j * B, (j + 1) * B)
            before = slice((j - 1) * B, j * B)
            keys = jnp.concatenate([kp_ref[...] if j == 0 else kc_ref[before], kc_ref[own]], axis=0)
            vals = jnp.concatenate([vp_ref[...] if j == 0 else vc_ref[before], vc_ref[own]], axis=0)
            dead = (lambda shape: (n == 0) & (lax.broadcasted_iota(jnp.int32, shape, 1) < B)) if j == 0 else None
            ds, dk2, dv2 = grads(qc_ref[own], doc_ref[own], lc_ref[own], dc_ref[own], keys, vals, bias_ref[...], dead)
            dq_ref[own] = _unstack_heads(jnp.dot(ds, keys, preferred_element_type=F32), masks).astype(dq_ref.dtype)
            if j > 0:
                dk_ref[before] = (dk_own + dk2[:B]).astype(dk_ref.dtype)
                dv_ref[before] = (dv_own + dv2[:B]).astype(dv_ref.dtype)
            dk_own, dv_own = dk2[B:], dv2[B:]
        last = slice((per_step - 1) * B, per_step * B)
        _, dk1, dv1 = grads(qn_ref[...], don_ref[...], ln_ref[...], dn_ref[...], kc_ref[last], vc_ref[last],
                            bias_next_ref[...], lambda shape: n == n_steps - 1)
        dk_ref[last] = (dk_own + dk1).astype(dk_ref.dtype)
        dv_ref[last] = (dv_own + dv1).astype(dv_ref.dtype)

    blk = (per_step * B, GROUP_W)
    one = (B, GROUP_W)
    nxt_idx = lambda n: jnp.minimum((n + 1) * per_step, nb - 1)
    prv_idx = lambda n: jnp.maximum(n * per_step - 1, 0)
    zcur = lambda c: pl.BlockSpec(blk, lambda r, n: (n, r * 3 + c))
    znext = lambda c: pl.BlockSpec(one, lambda r, n: (nxt_idx(n), r * 3 + c))
    zprev = lambda c: pl.BlockSpec(one, lambda r, n: (prv_idx(n), r * 3 + c))
    cur = pl.BlockSpec(blk, lambda r, n: (n, r))
    nxt = pl.BlockSpec(one, lambda r, n: (nxt_idx(n), r))
    return pl.pallas_call(
        body, name=name, grid=(dil, n_steps),
        in_specs=[zcur(0), znext(0), zcur(1), zprev(1), zcur(2), zprev(2), cur, nxt, cur, nxt, cur, nxt],
        out_specs=[cur, cur, cur],
        out_shape=[jax.ShapeDtypeStruct((rows, dil * GROUP_W), WIRE_DTYPE)] * 3,
        scratch_shapes=[pltpu.VMEM((N_HEADS * B, 2 * B), F32), pltpu.VMEM((N_HEADS * B, B), F32)],
        compiler_params=_params(("parallel", "arbitrary")),
    )(qkv, qkv, qkv, qkv, qkv, qkv, do, do, lse, lse, delta, delta)


def _inproj_bwd(x, g, dxn, dz_abc, dqkv, dz_g, w_t, name):
    S, D = x.shape
    N = w_t.shape[0]
    tm = TM_MM
    n_abc = N_ABC * GROUP_W

    def body(x_ref, g_ref, dxn_ref, dabc_ref, q1, k1, v1, q2, k2, v2, q3, k3, v3, dg_ref, w_ref,
             dx_ref, dz_ref, h_ref, dgn_ref, s4_ref, s16_ref, tmp_ref):
        i = pl.program_id(0)

        @pl.when(i == 0)
        def _():
            dgn_ref[...] = jnp.zeros_like(dgn_ref)

        dz_ref[:, 0:n_abc] = dabc_ref[...].astype(MXU_DTYPE)
        for j, parts in enumerate(((q1, q2, q3), (k1, k2, k3), (v1, v2, v3))):
            c0 = n_abc + j * GROUP_W
            _interleave(parts[1], s4_ref, ATTN_DILATIONS[1])
            _interleave(parts[2], s16_ref, ATTN_DILATIONS[2], tmp_ref)
            dz_ref[:, c0:c0 + GROUP_W] = (parts[0][...] + _get(s4_ref) + _get(s16_ref)).astype(MXU_DTYPE)
        dz_ref[:, n_abc + 3 * GROUP_W:] = dg_ref[...].astype(MXU_DTYPE)
        dh = jnp.dot(dz_ref[...], w_ref[...], preferred_element_type=F32)
        xv = x_ref[...]
        r = lax.rsqrt(jnp.mean(xv * xv, axis=-1, keepdims=True) + NORM_EPS)
        xn = xv * r
        gv = g_ref[...]
        h_ref[...] = (xn * gv).astype(MXU_DTYPE)
        dgn_ref[...] += _colsum(dh * xn)
        dn = dh * gv
        dx_ref[...] = dxn_ref[...] + r * (dn - xn * jnp.mean(dn * xn, axis=-1, keepdims=True))

    row = lambda w: pl.BlockSpec((tm, w), lambda i: (i, 0))
    flat = [t for p in dqkv for t in p]
    dil_specs = [_dilated_spec(tm, GROUP_W, dil) for dil in ATTN_DILATIONS for _ in range(3)]
    return pl.pallas_call(
        body, name=name, grid=(S // tm,),
        in_specs=[row(D), pl.BlockSpec((1, D), lambda i: (0, 0)), row(D), row(n_abc)] + dil_specs
                 + [row(GROUP_W), pl.BlockSpec((N, D), lambda i: (0, 0), pipeline_mode=pl.Buffered(1))],
        out_specs=[row(D), row(N), row(D), pl.BlockSpec((1, D), lambda i: (0, 0))],
        out_shape=[jax.ShapeDtypeStruct((S, D), F32), jax.ShapeDtypeStruct((S, N), MXU_DTYPE),
                   jax.ShapeDtypeStruct((S, D), MXU_DTYPE), jax.ShapeDtypeStruct((1, D), F32)],
        scratch_shapes=[_lane_scratch(tm, GROUP_W)] * 3,
        compiler_params=_params(("arbitrary",)),
    )(x, g, dxn, dz_abc, *flat, dz_g, w_t)


def _inproj_wgrad(h, dz, name):
    S, D = h.shape
    N = dz.shape[1]
    tm = TM_WGRAD
    nj = 2
    cw = N // nj
    per = N_DEV // nj
    n_loc = N // N_DEV

    def body(h_ref, dz_ref, dw_ref, acc_ref):
        i = pl.program_id(1)

        @pl.when(i == 0)
        def _():
            acc_ref[...] = jnp.zeros_like(acc_ref)

        acc_ref[...] += _mm_tn(dz_ref[...], h_ref[...])

        @pl.when(i == S // tm - 1)
        def _():
            for b in range(per):
                dw_ref[b] = acc_ref[b * n_loc:(b + 1) * n_loc, :].astype(dw_ref.dtype)

    return pl.pallas_call(
        body, name=name, grid=(nj, S // tm),
        in_specs=[pl.BlockSpec((tm, D), lambda j, i: (i, 0)), pl.BlockSpec((tm, cw), lambda j, i: (i, j))],
        out_specs=pl.BlockSpec((per, n_loc, D), lambda j, i: (j, 0, 0)),
        out_shape=jax.ShapeDtypeStruct((N_DEV, n_loc, D), WIRE_DTYPE),
        scratch_shapes=[pltpu.VMEM((cw, D), F32)],
        compiler_params=_params(("parallel", "arbitrary")),
    )(h, dz)


def _my_place():
    return lax.axis_index("x"), lax.axis_index("y"), lax.axis_index("c")


def _peer(x, y, c, k):
    px = 1 - x if k & 4 else x
    py = 1 - y if k & 2 else y
    pc = 1 - c if k & 1 else c
    return (px, py, pc), 4 * px + 2 * py + pc


HBM_SPEC = pl.BlockSpec(memory_space=pltpu.HBM)
SEM_SPEC = pl.BlockSpec(memory_space=pltpu.SEMAPHORE)
SPLIT_EFFECT = pltpu.SideEffectType.DATAFLOW_SIDE_EFFECTING
N_PEERS = N_DEV - 1


def _exchange_copies(srcs, lands, send_sems, recv_sems, whole, arrival):
    x, y, c = _my_place()
    me = 4 * x + 2 * y + c
    copies = []
    for t in range(len(srcs)):
        for k in range(1, N_DEV):
            peer, pidx = _peer(x, y, c, k)
            copies.append(pltpu.make_async_remote_copy(
                src_ref=srcs[t] if whole[t] else srcs[t].at[pidx],
                dst_ref=lands[t].at[pidx if arrival else me], send_sem=send_sems.at[t * N_PEERS + k - 1],
                recv_sem=recv_sems.at[t * N_PEERS + k - 1], device_id=peer, device_id_type=MESH))
    return copies


def _exchange_start(groups, name, after=None):
    sizes = [len(g) for g in groups]
    whole = [w for g in groups for _, w in g]
    srcs = [pltpu.with_memory_space_constraint(a, pltpu.HBM) for g in groups for a, _ in g]
    lands = [pltpu.with_memory_space_constraint(lax.empty(((N_DEV,) + a.shape) if w else a.shape, a.dtype), pltpu.HBM)
             for a, w in zip(srcs, whole)]
    n = len(srcs)
    n_g = len(groups)
    extra = [] if after is None else [after]
    n_in = 2 * n + len(extra)

    def body(*refs):
        src_refs, land_refs = refs[:n], refs[n:2 * n]
        sem_refs = refs[n_in + 2 * n:n_in + 2 * n + 2 * n_g]
        token = refs[-1]
        off = 0
        for gi, sz in enumerate(sizes):
            for send in _exchange_copies(src_refs[off:off + sz], land_refs[off:off + sz],
                                         sem_refs[2 * gi], sem_refs[2 * gi + 1], whole[off:off + sz], False):
                send.start()
            off += sz
        token[...] = jnp.zeros_like(token)

    sem_shapes = [pltpu.SemaphoreType.DMA((sz * N_PEERS,)) for sz in sizes for _ in range(2)]
    outs = pl.pallas_call(
        body, name=name,
        in_specs=[HBM_SPEC] * (2 * n) + [pl.BlockSpec(memory_space=pl.ANY)] * len(extra),
        out_specs=[HBM_SPEC] * (2 * n) + [SEM_SPEC] * (2 * n_g) + [pl.BlockSpec(memory_space=pltpu.VMEM)],
        out_shape=[pltpu.HBM(a.shape, a.dtype) for a in srcs + lands] + sem_shapes
                  + [jax.ShapeDtypeStruct((SUBLANES, LANES), F32)],
        input_output_aliases={i: i for i in range(2 * n)},
        compiler_params=pltpu.CompilerParams(has_side_effects=SPLIT_EFFECT),
    )(*srcs, *lands, *extra)
    handles, off = [], 0
    for gi, sz in enumerate(sizes):
        handles.append((outs[2 * n + 2 * gi], outs[2 * n + 2 * gi + 1], outs[off:off + sz], outs[n + off:n + off + sz],
                        whole[off:off + sz]))
        off += sz
    return handles, outs[-1]


def _exchange_wait(handle, after, name):
    send_sems, recv_sems, srcs, lands, whole = handle
    n = len(srcs)

    def body(*refs):
        src_refs, land_refs = refs[:n], refs[n:2 * n]
        for send in _exchange_copies(src_refs, land_refs, refs[2 * n], refs[2 * n + 1], whole, False):
            send.wait_send()
        for arrival in _exchange_copies(src_refs, land_refs, refs[2 * n], refs[2 * n + 1], whole, True):
            arrival.wait_recv()

    outs = pl.pallas_call(
        body, name=name,
        in_specs=[HBM_SPEC] * (2 * n) + [SEM_SPEC, SEM_SPEC, pl.BlockSpec(memory_space=pl.ANY)],
        out_specs=[HBM_SPEC] * (2 * n),
        out_shape=[pltpu.HBM(a.shape, a.dtype) for a in list(srcs) + list(lands)],
        input_output_aliases={i: i for i in range(2 * n)},
        compiler_params=pltpu.CompilerParams(has_side_effects=SPLIT_EFFECT),
    )(*srcs, *lands, send_sems, recv_sems, after)
    x, y, c = _my_place()
    me = 4 * x + 2 * y + c
    own = [s[None] if w else lax.dynamic_slice_in_dim(s, me, 1, axis=0) for s, w in zip(outs[:n], whole)]
    return [lax.dynamic_update_slice_in_dim(ld, o, me, axis=0) for ld, o in zip(outs[n:], own)]


def _sum_slots(parts, name):
    n = len(parts)

    def body(*refs):
        for p_ref, o_ref in zip(refs[:n], refs[n:]):
            acc = p_ref[0]
            for j in range(1, N_DEV):
                acc = acc + p_ref[j]
            o_ref[...] = acc

    vm = pl.BlockSpec(memory_space=pltpu.VMEM)
    return pl.pallas_call(
        body, name=name, in_specs=[vm] * n, out_specs=[vm] * n,
        out_shape=[jax.ShapeDtypeStruct(p.shape[1:], F32) for p in parts],
        compiler_params=pltpu.CompilerParams(vmem_limit_bytes=VMEM_LIMIT),
    )(*parts)


def _adamw_math(w, g, m, v):
    m = ADAM_B1 * m + (1.0 - ADAM_B1) * g
    v = ADAM_B2 * v + (1.0 - ADAM_B2) * (g * g)
    m_hat = m / (1.0 - ADAM_B1 ** ADAM_STEP)
    v_hat = v / (1.0 - ADAM_B2 ** ADAM_STEP)
    delta = -ADAM_LR * (m_hat / (jnp.sqrt(v_hat) + ADAM_EPS) + ADAM_WD * w)
    return delta, m, v


def _adamw_summed(parts, w, m, v, tr, name):
    depth, R, C = w.shape

    def body(*refs):
        p_refs = refs[:depth]
        w_ref, m_ref, v_ref, g_ref, d_ref, nm_ref, nv_ref = refs[depth:]
        lay = pl.program_id(0)
        for l in range(depth):
            @pl.when(lay == l)
            def _(p_ref=p_refs[l]):
                g = p_ref[0].astype(F32)
                for j in range(1, N_DEV):
                    g = g + p_ref[j].astype(F32)
                g_ref[0] = g
        d_ref[0], nm_ref[0], nv_ref[0] = _adamw_math(w_ref[0], g_ref[0], m_ref[0], v_ref[0])

    part_spec = lambda l: pl.BlockSpec((N_DEV, tr, C), lambda lay, i: (0, jnp.where(lay == l, i, 0), 0))
    row = pl.BlockSpec((1, tr, C), lambda lay, i: (lay, i, 0))
    return pl.pallas_call(
        body, name=name, grid=(depth, R // tr),
        in_specs=[part_spec(l) for l in range(depth)] + [row, row, row],
        out_specs=[row] * 4, out_shape=[jax.ShapeDtypeStruct((depth, R, C), F32)] * 4,
        compiler_params=_params(("arbitrary", "arbitrary")),
    )(*parts, w, m, v)


def _adamw_small(w, g, m, v, name):
    def body(w_ref, g_ref, m_ref, v_ref, d_ref, nm_ref, nv_ref):
        d_ref[...], nm_ref[...], nv_ref[...] = _adamw_math(w_ref[...], g_ref[...], m_ref[...], v_ref[...])

    vm = pl.BlockSpec(memory_space=pltpu.VMEM)
    return pl.pallas_call(
        body, name=name, in_specs=[vm] * 4, out_specs=[vm] * 3,
        out_shape=[jax.ShapeDtypeStruct(w.shape, F32)] * 3,
        compiler_params=pltpu.CompilerParams(vmem_limit_bytes=VMEM_LIMIT),
    )(w, g, m, v)


def _pack(arrays):
    flat = jnp.concatenate([a.reshape(-1) for a in arrays])
    pad = (-flat.shape[0]) % (SUBLANES * LANES)
    return jnp.pad(flat, (0, pad)).reshape(-1, LANES)


def _unpack(buf, like):
    flat = buf.reshape(-1)
    out, off = [], 0
    for a in like:
        out.append(flat[off:off + a.size].reshape(a.shape))
        off += a.size
    return out


def _block_diag(w):
    eye = jnp.eye(N_HEADS, dtype=w.dtype)
    return jnp.einsum('hij,hk->hikj', w, eye).reshape(GROUP_W, GROUP_W)


def _diag_blocks(w):
    return jnp.einsum('hihj->hij', w.reshape(N_HEADS, HEAD_DIM, N_HEADS, HEAD_DIM))


def _pad_rows(a):
    return jnp.pad(a, ((0, SUBLANES - a.shape[0]), (0, 0)))


def _mixer_params(l, conv_a_w, conv_r_w, conv_r_b, lru_wa, lru_ba, lru_wx, lru_bx, lru_lambda, gmlp_norm_g,
                  gmlp_ws, gmlp_bs):
    tril = jnp.tril(jnp.ones((GMLP_CHUNK, GMLP_CHUNK), dtype=bool))
    vec = jnp.stack([conv_r_b[l], lru_ba[l], lru_bx[l], lru_lambda[l], gmlp_norm_g[l]])
    return {
        "wA": _pad_rows(conv_a_w[l]), "wR": _pad_rows(conv_r_w[l]), "vec": _pad_rows(vec),
        "wa": _block_diag(lru_wa[l]).astype(MXU_DTYPE), "wx": _block_diag(lru_wx[l]).astype(MXU_DTYPE),
        "ws": jnp.where(tril[None], gmlp_ws[l], 0.0).astype(MXU_DTYPE),
        "bs": jnp.repeat(jnp.transpose(gmlp_bs[l]), HEAD_DIM, axis=1),
    }


MIXER_NAMES = ("conv_a_w", "conv_r_w", "conv_r_b", "lru_wa", "lru_ba", "lru_wx", "lru_bx", "lru_lambda",
               "gmlp_norm_g", "gmlp_ws", "gmlp_bs")
SMALL_NAMES = ("norm_g",) + MIXER_NAMES + ("final_g",)


def _local_step(x, loss_target, norm_g, get_w_in, get_w_out, emit_early, emit_late, conv_a_w, conv_r_w, conv_r_b,
                lru_wa, lru_ba, lru_wx, lru_bx, lru_lambda, gmlp_norm_g, gmlp_ws, gmlp_bs, final_g):
    depth = norm_g.shape[0]
    D = x.shape[1]
    small = (conv_a_w, conv_r_w, conv_r_b, lru_wa, lru_ba, lru_wx, lru_bx, lru_lambda, gmlp_norm_g, gmlp_ws, gmlp_bs)
    saved = []
    for l in range(depth):
        mp = _mixer_params(l, *small)
        w_in_l = get_w_in(l, x)
        z, z_g, *qkv, y_abc, hs = _inproj_mix_fwd(x, norm_g[l].reshape(1, D), w_in_l, mp, f"inproj_mix_fwd_{l}")
        attn =[_attn_fwd(qkv[p], dil, f"attn_fwd_d{dil}_{l}") for p, dil in enumerate(ATTN_DILATIONS)]
        w_out_l = get_w_out(l, y_abc)
        x_new, y, o, *lse = _outproj(x, z_g, y_abc, attn, w_out_l, f"outproj_{l}")
        saved.append((x, z, z_g, qkv, hs, y, o, lse, mp, w_in_l, w_out_l))
        x = x_new
    dx, loss, d_final_g = _loss_head(x, final_g.reshape(1, D), loss_target, "loss_head")
    token = None
    for l in reversed(range(depth)):
        x_l, z, z_g, qkv, hs, y, o, lse, mp, w_in_l, w_out_l = saved[l]
        if token is not None:
            mp = dict(mp, vec=mp["vec"] + token[0, 0])
        (dw_out, dz_abc, dz_g, do1, do4, do16, dl1, dl4, dl16, dwA, dwR, dvec, dwa, dwx, dws, dbs) = _outproj_mix_bwd(
            dx, y, w_out_l, z, z_g, hs, o, mp, f"outproj_mix_bwd_{l}")
        token = emit_early(l, dw_out, [
            dwA[:conv_a_w.shape[1]], dwR[:conv_r_w.shape[1]], dvec[0], _diag_blocks(dwa), dvec[1], _diag_blocks(dwx),
            dvec[2], dvec[3], dvec[4], dws, jnp.transpose(dbs[:, :N_HEADS])])
        g_row = norm_g[l].reshape(1, D)
        if token is not None:
            g_row = g_row + token[0, 0]
        dqkv = [_attn_bwd(qkv[p], do, lse[p], dl, dil, f"attn_bwd_d{dil}_{l}")
                for p, (dil, do, dl) in enumerate(zip(ATTN_DILATIONS, (do1, do4, do16), (dl1, dl4, dl16)))]
        dx, dz, h, dng = _inproj_bwd(x_l, g_row, dx, dz_abc, dqkv, dz_g, w_in_l, f"inproj_bwd_{l}")
        dw_in = _inproj_wgrad(h, dz, f"inproj_wgrad_{l}")
        token = emit_late(l, dw_in, [dng[0]] + ([d_final_g[0]] if l == depth - 1 else []))
    return loss[0, 0], dx
WEIGHT_NAMES = ("norm_g", "w_in", "conv_a_w", "conv_r_w", "conv_r_b", "lru_wa", "lru_ba", "lru_wx", "lru_bx",
                "lru_lambda", "gmlp_norm_g", "gmlp_ws", "gmlp_bs", "w_out", "final_g")


def kernel(x, norm_g, w_in, conv_a_w, conv_r_w, conv_r_b, lru_wa, lru_ba, lru_wx, lru_bx, lru_lambda, gmlp_norm_g, gmlp_ws, gmlp_bs, w_out, final_g, loss_target, m_norm_g, m_w_in, m_conv_a_w, m_conv_r_w, m_conv_r_b, m_lru_wa, m_lru_ba, m_lru_wx, m_lru_bx, m_lru_lambda, m_gmlp_norm_g, m_gmlp_ws, m_gmlp_bs, m_w_out, m_final_g, v_norm_g, v_w_in, v_conv_a_w, v_conv_r_w, v_conv_r_b, v_lru_wa, v_lru_ba, v_lru_wx, v_lru_bx, v_lru_lambda, v_gmlp_norm_g, v_gmlp_ws, v_gmlp_bs, v_w_out, v_final_g):
    w = dict(norm_g=norm_g, w_in=w_in, conv_a_w=conv_a_w, conv_r_w=conv_r_w, conv_r_b=conv_r_b, lru_wa=lru_wa,
             lru_ba=lru_ba, lru_wx=lru_wx, lru_bx=lru_bx, lru_lambda=lru_lambda, gmlp_norm_g=gmlp_norm_g,
             gmlp_ws=gmlp_ws, gmlp_bs=gmlp_bs, w_out=w_out, final_g=final_g)
    m = dict(norm_g=m_norm_g, w_in=m_w_in, conv_a_w=m_conv_a_w, conv_r_w=m_conv_r_w, conv_r_b=m_conv_r_b,
             lru_wa=m_lru_wa, lru_ba=m_lru_ba, lru_wx=m_lru_wx, lru_bx=m_lru_bx, lru_lambda=m_lru_lambda,
             gmlp_norm_g=m_gmlp_norm_g, gmlp_ws=m_gmlp_ws, gmlp_bs=m_gmlp_bs, w_out=m_w_out, final_g=m_final_g)
    v = dict(norm_g=v_norm_g, w_in=v_w_in, conv_a_w=v_conv_a_w, conv_r_w=v_conv_r_w, conv_r_b=v_conv_r_b,
             lru_wa=v_lru_wa, lru_ba=v_lru_ba, lru_wx=v_lru_wx, lru_bx=v_lru_bx, lru_lambda=v_lru_lambda,
             gmlp_norm_g=v_gmlp_norm_g, gmlp_ws=v_gmlp_ws, gmlp_bs=v_gmlp_bs, w_out=v_w_out, final_g=v_final_g)
    depth, D, n_loc = w_in.shape
    e_loc = w_out.shape[1]
    cx, cy, cc = _my_place()
    me = 4 * cx + 2 * cy + cc

    transposed = lambda a: jnp.transpose(a, (0, 2, 1))
    w_in_t, m_w_in_t, v_w_in_t = transposed(w_in), transposed(m_w_in), transposed(v_w_in)
    w_in_w, w_out_w = w_in_t.astype(MXU_DTYPE), w_out.astype(MXU_DTYPE)
    c_loc = conv_a_w.shape[2]
    taps = (conv_a_w, conv_r_w)
    first, _ = _exchange_start([[(w_in_w[0], True), (_pack(taps), True)], [(w_out_w[0], True)]], "gather_start_first")
    full_in = lambda g: g.reshape(N_DEV * n_loc, D)
    full_out = lambda g: g.reshape(N_DEV * e_loc, D)

    g_in0, g_taps = _exchange_wait(first[0], x, "gather_wait_in_0")
    groups = [[(w_in_w[l], True), (w_out_w[l], True)] for l in range(1, depth)]
    gathers, rest_token = _exchange_start(groups, "gather_start_rest", after=g_taps)
    g_taps = g_taps.reshape(N_DEV, -1) + rest_token[0, 0]
    conv_full, off = [], 0
    for a in taps:
        part = g_taps[:, off:off + a.size].reshape((N_DEV,) + a.shape)
        conv_full.append(jnp.transpose(part, (1, 2, 0, 3)).reshape(a.shape[:2] + (N_DEV * c_loc,)))
        off += a.size
    conv_a_full, conv_r_full = conv_full
    later = {}

    def get_w_in(l, after):
        if l == 0:
            return full_in(g_in0)
        g_in, later[l] = _exchange_wait(gathers[l - 1], after, f"gather_wait_{l}")
        return full_in(g_in)

    def get_w_out(l, after):
        if l == 0:
            return full_out(_exchange_wait(first[1], after, "gather_wait_out_0")[0])
        return full_out(later[l])

    early, late, last_token = {}, {}, [None]

    def emit_early(l, dw_out, mixer_grads):
        handles, token = _exchange_start(
            [[(dw_out.reshape(N_DEV, e_loc, D), False), (_pack(mixer_grads), True)]], f"early_start_{l}")
        early[l] = (handles[0], mixer_grads)
        return token

    def emit_late(l, dw_in, norm_grads):
        handles, token = _exchange_start([[(_pack(norm_grads), True)], [(dw_in, False)]], f"late_start_{l}")
        late[l] = (handles[0], handles[1], norm_grads)
        last_token[0] = token
        return token

    loss, grad_x = _local_step(
        x[0], loss_target[0], norm_g, get_w_in, get_w_out, emit_early, emit_late, conv_a_full, conv_r_full, conv_r_b,
        lru_wa, lru_ba, lru_wx, lru_bx, lru_lambda, gmlp_norm_g, gmlp_ws, gmlp_bs, final_g)
    loss = lax.psum(loss, ("x", "y", "c"))

    r_in, r_out, small_parts = {}, {}, []
    for l in reversed(range(depth)):
        r_out[l], r_mix = _exchange_wait(early[l][0], last_token[0], f"early_wait_{l}")
        (r_norm,) = _exchange_wait(late[l][0], last_token[0], f"late_wait_norm_{l}")
        small_parts += [r_mix, r_norm]
        if l > 0:
            (r_in[l],) = _exchange_wait(late[l][1], last_token[0], f"late_wait_{l}")
    big = {"w_out": _adamw_summed([r_out[l] for l in range(depth)], w_out, m_w_out, v_w_out, 128, "adamw_w_out")}

    sums = _sum_slots(small_parts, "sum_small_grads")
    by_layer = {}
    for i, l in enumerate(reversed(range(depth))):
        mix = _unpack(sums[2 * i], early[l][1])
        nrm = _unpack(sums[2 * i + 1], late[l][2])
        by_layer[l] = dict(zip(MIXER_NAMES, mix), norm_g=nrm[0])
        if l == depth - 1:
            g_final = nrm[1]
    g_small = {k: jnp.stack([by_layer[l][k] for l in range(depth)]) for k in ("norm_g",) + MIXER_NAMES}
    g_small["final_g"] = g_final
    for k in ("conv_a_w", "conv_r_w"):
        g_small[k] = lax.dynamic_slice_in_dim(g_small[k], me * c_loc, c_loc, axis=2)
    packs = [_pack([d[k] for k in SMALL_NAMES]) for d in (w, g_small, m, v)]
    res = _adamw_small(*packs, "adamw_small")
    like = [w[k] for k in SMALL_NAMES]
    d_s, m_s, v_s = (dict(zip(SMALL_NAMES, _unpack(r, like))) for r in res)

    (r_in[0],) = _exchange_wait(late[0][1], res[0], "late_wait_0")
    big["w_in"] = [transposed(a) for a in _adamw_summed(
        [r_in[l] for l in range(depth)], w_in_t, m_w_in_t, v_w_in_t, n_loc // 2, "adamw_w_in")]

    grad, delta, new_m, new_v = {}, {}, {}, {}
    for k in WEIGHT_NAMES:
        if k in big:
            grad[k], delta[k], new_m[k], new_v[k] = big[k]
        else:
            grad[k], delta[k], new_m[k], new_v[k] = g_small[k], d_s[k], m_s[k], v_s[k]
    return (loss, grad_x[None], *[grad[k] for k in WEIGHT_NAMES], *[delta[k] for k in WEIGHT_NAMES],
            *[new_m[k] for k in WEIGHT_NAMES], *[new_v[k] for k in WEIGHT_NAMES])
```

```python
import functools
import math

import jax
import jax.numpy as jnp
from jax import lax
from jax.experimental import pallas as pl
from jax.experimental.pallas import tpu as pltpu

F32 = jnp.float32
MXU_DTYPE = jnp.bfloat16
WIRE_DTYPE = jnp.bfloat16
MESH = pl.DeviceIdType.MESH

N_DEV = 8
GROUP_W = 256
N_HEADS = 4
HEAD_DIM = 64
N_CHUNKS = 13
N_ABC = 9
GMLP_CHUNK = 128
ATTN_BLOCK = 128
ATTN_FWD_BLOCKS_PER_STEP = 16
ATTN_BWD_BLOCKS_PER_STEP = 8
ATTN_DILATIONS = (1, 4, 16)
NORM_EPS = 1e-6
RG_C = 8.0
SUBLANES = 8
LANES = 128
VMEM_LIMIT = 56 * 1024 * 1024

ADAM_LR = 0.001
ADAM_B1 = 0.9
ADAM_B2 = 0.999
ADAM_EPS = 1e-08
ADAM_WD = 0.01
ADAM_STEP = 10

TM_MIX = 512
TM_MM = 512
TM_WGRAD = 1024


def _params(sem, vmem=VMEM_LIMIT):
    return pltpu.CompilerParams(dimension_semantics=sem, vmem_limit_bytes=vmem)


def _mm(a, b):
    return jnp.dot(a.astype(MXU_DTYPE), b.astype(MXU_DTYPE), preferred_element_type=F32)


def _mm_tn(a, b):
    return lax.dot_general(a.astype(MXU_DTYPE), b.astype(MXU_DTYPE), (((0,), (0,)), ((), ())),
                           preferred_element_type=F32)


def _mm_nt(a, b):
    return lax.dot_general(a.astype(MXU_DTYPE), b.astype(MXU_DTYPE), (((1,), (1,)), ((), ())),
                           preferred_element_type=F32)


def _sigmoid(x):
    return 0.5 * jnp.tanh(0.5 * x) + 0.5


def _silu_and_grad(x):
    s = _sigmoid(x)
    return x * s, s * (1.0 + x * (1.0 - s))


_GELU_K = math.sqrt(2.0 / math.pi)
_GELU_C = 0.044715


def _gelu_and_grad(x):
    x2 = x * x
    t = jnp.tanh(_GELU_K * (x + _GELU_C * x * x2))
    val = 0.5 * x * (1.0 + t)
    grad = 0.5 * (1.0 + t) + 0.5 * x * (1.0 - t * t) * (_GELU_K * (1.0 + 3.0 * _GELU_C * x2))
    return val, grad


def _gelu(x):
    return 0.5 * x * (1.0 + jnp.tanh(_GELU_K * (x + _GELU_C * x * x * x)))


def _expm1_nonpos(u):
    poly = 1.0 / math.factorial(9)
    for k in range(8, 0, -1):
        poly = poly * u + 1.0 / math.factorial(k)
    return jnp.where(u > -0.25, poly * u, jnp.exp(u) - 1.0)


def _softplus(x):
    return jnp.maximum(x, 0.0) + jnp.log(1.0 + jnp.exp(-jnp.abs(x)))


def _shift_down(t, halo, k):
    rolled = pltpu.roll(t, k, 0)
    hr = pltpu.roll(halo, k, 0)
    row = lax.broadcasted_iota(jnp.int32, halo.shape, 0)
    first = jnp.where(row < k, hr, rolled[0:SUBLANES])
    return jnp.concatenate([first, rolled[SUBLANES:]], axis=0)


def _shift_up(t, nxt, k):
    tm = t.shape[0]
    rolled = pltpu.roll(t, tm - k, 0)
    nr = pltpu.roll(nxt, SUBLANES - k, 0)
    row = lax.broadcasted_iota(jnp.int32, nxt.shape, 0)
    last = jnp.where(row >= SUBLANES - k, nr, rolled[tm - SUBLANES:tm])
    return jnp.concatenate([rolled[:tm - SUBLANES], last], axis=0)


def _scan_fwd(a, b):
    tm = a.shape[0]
    row = lax.broadcasted_iota(jnp.int32, a.shape, 0)
    s = 1
    while s < tm:
        a_s = pltpu.roll(a, s, 0)
        b_s = pltpu.roll(b, s, 0)
        m = row >= s
        b = jnp.where(m, a * b_s + b, b)
        a = jnp.where(m, a * a_s, a)
        s *= 2
    return a, b


def _scan_rev(a, g):
    tm = a.shape[0]
    row = lax.broadcasted_iota(jnp.int32, a.shape, 0)
    s = 1
    while s < tm:
        a_s = pltpu.roll(a, tm - s, 0)
        g_s = pltpu.roll(g, tm - s, 0)
        m = row < tm - s
        g = jnp.where(m, g + a * g_s, g)
        a = jnp.where(m, a * a_s, a)
        s *= 2
    return g


def _group_rows(scr_ref, row, n_groups):
    return jnp.concatenate([scr_ref[pl.ds(c, 1), pl.ds(row, n_groups, stride=SUBLANES), :][0]
                            for c in range(scr_ref.shape[0])], axis=1)


def _spread_rows(rows_ref, n_groups, w):
    return jnp.concatenate([jnp.broadcast_to(rows_ref[g:g + 1, :], (SUBLANES, w)) for g in range(n_groups)], axis=0)


def _scan_groups(a, b, reverse):
    tm, w = a.shape
    shape3 = (tm // SUBLANES, SUBLANES, w)
    a3, b3 = a.reshape(shape3), b.reshape(shape3)
    sub = lax.broadcasted_iota(jnp.int32, shape3, 1)
    s = 1
    while s < SUBLANES:
        shift = SUBLANES - s if reverse else s
        a_s = pltpu.roll(a3, shift, 1)
        b_s = pltpu.roll(b3, shift, 1)
        m = (sub < SUBLANES - s) if reverse else (sub >= s)
        b3 = jnp.where(m, a3 * b_s + b3, b3)
        a3 = jnp.where(m, a3 * a_s, a3)
        s *= 2
    return a3.reshape(tm, w), b3.reshape(tm, w)


def _scan_fwd_tile(a, b, h_in, sa_ref, sb_ref, sc_ref):
    tm, w = a.shape
    n_groups = tm // SUBLANES
    a_loc, b_loc = _scan_groups(a, b, False)
    _put(sa_ref, a_loc)
    _put(sb_ref, b_loc)
    a_end, b_end = _scan_fwd(_group_rows(sa_ref, SUBLANES - 1, n_groups), _group_rows(sb_ref, SUBLANES - 1, n_groups))
    h_end = b_end + a_end * h_in
    sc_ref[...] = _shift_down(h_end, jnp.broadcast_to(h_in, (SUBLANES, w)), 1)
    return b_loc + a_loc * _spread_rows(sc_ref, n_groups, w), h_end


def _scan_rev_tile(a, g, sa_ref, sb_ref, sc_ref):
    tm, w = a.shape
    n_groups = tm // SUBLANES
    a_loc, g_loc = _scan_groups(a, g, True)
    _put(sa_ref, a_loc)
    _put(sb_ref, g_loc)
    d_first = _scan_rev(_group_rows(sa_ref, 0, n_groups), _group_rows(sb_ref, 0, n_groups))
    sc_ref[...] = _shift_up(d_first, jnp.zeros((SUBLANES, w), F32), 1)
    return g_loc + a_loc * _spread_rows(sc_ref, n_groups, w)


def _lane_scratch(tm, w):
    return pltpu.VMEM((w // LANES, tm, LANES), F32)


def _put(scr_ref, val):
    for c in range(scr_ref.shape[0]):
        scr_ref[c] = val[:, c * LANES:(c + 1) * LANES].astype(F32)


def _get(scr_ref):
    return jnp.concatenate([scr_ref[c] for c in range(scr_ref.shape[0])], axis=1)


MAX_ROW_STRIDE = 4


def _strided_rows(c, start, n, stride):
    return (pl.ds(c, 1), pl.ds(start, n, stride=stride), slice(None))


def _deinterleave(src_ref, dst_ref, dil, tmp_ref=None):
    nc, tm, _ = src_ref.shape
    w = nc * LANES
    s1 = min(dil, MAX_ROW_STRIDE)
    s2 = dil // s1
    if s2 > 1:
        for r0 in range(s1):
            for c in range(nc):
                tmp_ref[c, r0 * (tm // s1):(r0 + 1) * (tm // s1), :] = src_ref[_strided_rows(c, r0, tm // s1, s1)][0]
    for r in range(dil):
        r1, r0 = divmod(r, s1)
        for c in range(nc):
            if dil == 1:
                piece = src_ref[c]
            elif s2 == 1:
                piece = src_ref[_strided_rows(c, r, tm // dil, dil)][0]
            else:
                piece = tmp_ref[_strided_rows(c, r0 * (tm // s1) + r1, tm // dil, s2)][0]
            dst_ref[:, r * w + c * LANES:r * w + (c + 1) * LANES] = piece.astype(dst_ref.dtype)


def _interleave(src_ref, dst_ref, dil, tmp_ref=None):
    nc, tm, _ = dst_ref.shape
    w = nc * LANES
    s1 = min(dil, MAX_ROW_STRIDE)
    s2 = dil // s1
    for r in range(dil):
        r1, r0 = divmod(r, s1)
        for c in range(nc):
            piece = src_ref[:, r * w + c * LANES:r * w + (c + 1) * LANES].astype(F32)[None]
            if s2 == 1:
                dst_ref[_strided_rows(c, r, tm // dil, dil)] = piece
            else:
                tmp_ref[_strided_rows(c, r0 * (tm // s1) + r1, tm // dil, s2)] = piece
    if s2 > 1:
        for r0 in range(s1):
            for c in range(nc):
                dst_ref[_strided_rows(c, r0, tm // s1, s1)] = (
                    tmp_ref[c, r0 * (tm // s1):(r0 + 1) * (tm // s1), :][None])


def _dilated_spec(tm, w, dil, index=lambda i: i):
    return pl.BlockSpec((tm // dil, dil * w), lambda i: (index(i), 0))


def _dilated_shape(S, w, dil, dtype):
    return jax.ShapeDtypeStruct((S // dil, dil * w), dtype)


def _head_masks(shape):
    lane = lax.broadcasted_iota(jnp.int32, shape, 1)
    return [(lane >= h * HEAD_DIM) & (lane < (h + 1) * HEAD_DIM) for h in range(N_HEADS)]


def _colsum(v):
    return jnp.sum(v, axis=0, keepdims=True)


def _conv_a(z_of, halo_of, w_ref):
    p = z_of(2) * z_of(0)
    p_h = halo_of(2) * halo_of(0)
    cv = w_ref[2:3, :] * p + w_ref[1:2, :] * _shift_down(p, p_h, 1) + w_ref[0:1, :] * _shift_down(p, p_h, 2)
    return p, p_h, cv


def _lru_gates(z_of, halo_of, wr_ref, vec_ref, wa_ref, wx_ref):
    rx = z_of(4)
    rx_h = halo_of(4)
    sh = [rx, _shift_down(rx, rx_h, 1), _shift_down(rx, rx_h, 2), _shift_down(rx, rx_h, 3)]
    xc = (wr_ref[3:4, :] * sh[0] + wr_ref[2:3, :] * sh[1] + wr_ref[1:2, :] * sh[2]
          + wr_ref[0:1, :] * sh[3] + vec_ref[0:1, :])
    ga = _sigmoid(jnp.dot(xc.astype(MXU_DTYPE), wa_ref[...], preferred_element_type=F32) + vec_ref[1:2, :])
    gi = _sigmoid(jnp.dot(xc.astype(MXU_DTYPE), wx_ref[...], preferred_element_type=F32) + vec_ref[2:3, :])
    sp = _softplus(-vec_ref[3:4, :])
    log_a = (-RG_C * ga) * sp
    a = jnp.exp(log_a)
    mult = jnp.sqrt(-_expm1_nonpos(2.0 * log_a))
    return xc, sh, ga, gi, a, mult, sp


def _gmlp_fwd(z_of, vec_ref, ws_ref, bs_ref, tm):
    u = _gelu(z_of(6))
    gv = _gelu(z_of(7))
    rr = lax.rsqrt(jnp.mean(gv * gv, axis=-1, keepdims=True) + NORM_EPS)
    vn = (gv * rr) * vec_ref[4:5, :]
    masks = _head_masks((GMLP_CHUNK, GROUP_W))
    parts = []
    for c in range(tm // GMLP_CHUNK):
        vc = vn[c * GMLP_CHUNK:(c + 1) * GMLP_CHUNK].astype(MXU_DTYPE)
        acc = bs_ref[...]
        for h in range(N_HEADS):
            acc = acc + jnp.where(masks[h], jnp.dot(ws_ref[h], vc, preferred_element_type=F32), 0.0)
        parts.append(acc)
    return u, gv, rr, vn, jnp.concatenate(parts, axis=0)


def _mix_specs(tm, S, order):
    const2 = lambda shape: pl.BlockSpec(shape, lambda i: (0, 0))
    return [const2((SUBLANES, GROUP_W)), const2((SUBLANES, GROUP_W)), const2((SUBLANES, GROUP_W)),
            const2((GROUP_W, GROUP_W)), const2((GROUP_W, GROUP_W)),
            pl.BlockSpec((N_HEADS, GMLP_CHUNK, GMLP_CHUNK), lambda i: (0, 0, 0)),
            const2((GMLP_CHUNK, GROUP_W))]


def _inproj_mix_fwd(x, g, w_t, mp, name):
    S, D = x.shape
    N = w_t.shape[0]
    tm = TM_MIX
    hb = tm // SUBLANES
    n_abc = N_ABC * GROUP_W
    n_qkv = 3 * GROUP_W

    def body(x_ref, g_ref, w_ref, wA_ref, wR_ref, vec_ref, wa_ref, wx_ref, ws_ref, bs_ref,
             z_ref, zg_ref, q1_ref, q4_ref, q16_ref, y_ref, h_ref,
             qkv_ref, halo_ref, carry_ref, sa_ref, sb_ref, sc_ref, tmp_ref):
        @pl.when(pl.program_id(0) == 0)
        def _():
            halo_ref[...] = jnp.zeros_like(halo_ref)
            carry_ref[...] = jnp.zeros_like(carry_ref)

        xv = x_ref[...]
        r = lax.rsqrt(jnp.mean(xv * xv, axis=-1, keepdims=True) + NORM_EPS)
        hn = ((xv * r) * g_ref[...]).astype(MXU_DTYPE)
        z_ref[...] = _mm_nt(hn, w_ref[0:n_abc, :])
        _put(qkv_ref, _mm_nt(hn, w_ref[n_abc:n_abc + n_qkv, :]))
        zg_ref[...] = _mm_nt(hn, w_ref[n_abc + n_qkv:, :])
        for dil, ref in zip(ATTN_DILATIONS, (q1_ref, q4_ref, q16_ref)):
            _deinterleave(qkv_ref, ref, dil, tmp_ref)

        z_of = lambda c: z_ref[:, c * GROUP_W:(c + 1) * GROUP_W]
        halo_of = lambda c: halo_ref[:, c * GROUP_W:(c + 1) * GROUP_W]

        _, _, cv = _conv_a(z_of, halo_of, wA_ref)
        y_ref[:, 0:GROUP_W] = (z_of(1) * cv * _silu_and_grad(z_of(3))[0]).astype(y_ref.dtype)

        xc, _, _, gi, a, mult, _ = _lru_gates(z_of, halo_of, wR_ref, vec_ref, wa_ref, wx_ref)
        b = mult * (gi * xc)
        h, h_end = _scan_fwd_tile(a, b, carry_ref[SUBLANES - 1:SUBLANES, :], sa_ref, sb_ref, sc_ref)
        h_ref[...] = h
        carry_ref[...] = h_end[hb - SUBLANES:hb]
        y_ref[:, GROUP_W:2 * GROUP_W] = (h * _silu_and_grad(z_of(5))[0]).astype(y_ref.dtype)

        u, _, _, _, sp = _gmlp_fwd(z_of, vec_ref, ws_ref, bs_ref, tm)
        y_ref[:, 2 * GROUP_W:3 * GROUP_W] = (u * sp * _silu_and_grad(z_of(8))[0]).astype(y_ref.dtype)
        halo_ref[...] = z_ref[tm - SUBLANES:tm, :]

    row = lambda wd: pl.BlockSpec((tm, wd), lambda i: (i, 0))
    return pl.pallas_call(
        body, name=name, grid=(S // tm,),
        in_specs=[row(D), pl.BlockSpec((1, D), lambda i: (0, 0)),
                  pl.BlockSpec((N, D), lambda i: (0, 0), pipeline_mode=pl.Buffered(1))] + _mix_specs(tm, S, "fwd"),
        out_specs=[row(n_abc), row(GROUP_W)] + [_dilated_spec(tm, n_qkv, dil) for dil in ATTN_DILATIONS]
                  + [row(3 * GROUP_W), row(GROUP_W)],
        out_shape=[jax.ShapeDtypeStruct((S, n_abc), F32), jax.ShapeDtypeStruct((S, GROUP_W), F32)]
                  + [_dilated_shape(S, n_qkv, dil, MXU_DTYPE) for dil in ATTN_DILATIONS]
                  + [jax.ShapeDtypeStruct((S, 3 * GROUP_W), MXU_DTYPE), jax.ShapeDtypeStruct((S, GROUP_W), F32)],
        scratch_shapes=[_lane_scratch(tm, n_qkv), pltpu.VMEM((SUBLANES, n_abc), F32),
                        pltpu.VMEM((SUBLANES, GROUP_W), F32), _lane_scratch(tm, GROUP_W), _lane_scratch(tm, GROUP_W),
                        pltpu.VMEM((hb, GROUP_W), F32), _lane_scratch(tm, n_qkv)],
        compiler_params=_params(("arbitrary",)),
    )(x, g, w_t, mp["wA"], mp["wR"], mp["vec"], mp["wa"], mp["wx"], mp["ws"], mp["bs"])


_NEG = -1e30


def _slope(h):
    return 2.0 ** (-8.0 * (h + 1) / N_HEADS)


def _attn_bias(dil, offsets, n_keys):
    shape = (ATTN_BLOCK, n_keys)
    qi = lax.broadcasted_iota(jnp.int32, shape, 0)
    ki = lax.broadcasted_iota(jnp.int32, shape, 1)
    blocks = []
    for f in offsets:
        delta = qi + f - ki
        valid = (delta >= 0) & (delta <= ATTN_BLOCK)
        dist = (delta * dil).astype(F32)
        for h in range(N_HEADS):
            blocks.append(jnp.where(valid, -_slope(h) * dist, _NEG))
    return jnp.concatenate(blocks, axis=0)


def _stack_heads(t, masks):
    return jnp.concatenate([jnp.where(m, t, jnp.zeros_like(t)) for m in masks], axis=0)


def _unstack_heads(t4, masks, base=0):
    out = t4[base * ATTN_BLOCK:(base + 1) * ATTN_BLOCK]
    for h in range(1, N_HEADS):
        out = jnp.where(masks[h], t4[(base + h) * ATTN_BLOCK:(base + h + 1) * ATTN_BLOCK], out)
    return out


def _attn_fwd(qkv, dil, name):
    rows = qkv.shape[0]
    nb = rows // ATTN_BLOCK
    scale = 1.0 / math.sqrt(HEAD_DIM)
    B = ATTN_BLOCK
    per_step = min(ATTN_FWD_BLOCKS_PER_STEP, nb)

    def body(q_ref, kc_ref, kp_ref, vc_ref, vp_ref, o_ref, l_ref, bias_ref):
        n = pl.program_id(1)

        @pl.when(n == 0)
        def _():
            bias_ref[...] = _attn_bias(dil, (B,), 2 * B)

        masks = _head_masks((B, GROUP_W))
        for j in range(per_step):
            own = slice(j * B, (j + 1) * B)
            before = slice((j - 1) * B, j * B)
            qs = _stack_heads(q_ref[own], masks)
            keys = jnp.concatenate([kp_ref[...] if j == 0 else kc_ref[before], kc_ref[own]], axis=0)
            vals = jnp.concatenate([vp_ref[...] if j == 0 else vc_ref[before], vc_ref[own]], axis=0)
            s = _mm_nt(qs, keys) * scale + bias_ref[...]
            if j == 0:
                key_col = lax.broadcasted_iota(jnp.int32, s.shape, 1)
                s = jnp.where((n == 0) & (key_col < B), _NEG, s)
            m = jnp.max(s, axis=-1, keepdims=True)
            p = jnp.exp(s - m)
            l = jnp.sum(p, axis=-1, keepdims=True)
            o4 = jnp.dot(p.astype(MXU_DTYPE), vals, preferred_element_type=F32)
            o_ref[own] = (_unstack_heads(o4, masks)
                          / _unstack_heads(jnp.broadcast_to(l, o4.shape), masks)).astype(o_ref.dtype)
            l_ref[own] = _unstack_heads(jnp.broadcast_to(m + jnp.log(l), o4.shape), masks)

    blk = (per_step * B, GROUP_W)
    cur = lambda c: pl.BlockSpec(blk, lambda r, n: (n, r * 3 + c))
    prev = lambda c: pl.BlockSpec((B, GROUP_W), lambda r, n: (jnp.maximum(n * per_step - 1, 0), r * 3 + c))
    out = pl.BlockSpec(blk, lambda r, n: (n, r))
    return pl.pallas_call(
        body, name=name, grid=(dil, nb // per_step),
        in_specs=[cur(0), cur(1), prev(1), cur(2), prev(2)],
        out_specs=[out, out],
        out_shape=[jax.ShapeDtypeStruct((rows, dil * GROUP_W), MXU_DTYPE),
                   jax.ShapeDtypeStruct((rows, dil * GROUP_W), F32)],
        scratch_shapes=[pltpu.VMEM((N_HEADS * ATTN_BLOCK, 2 * ATTN_BLOCK), F32)],
        compiler_params=_params(("parallel", "arbitrary")),
    )(qkv, qkv, qkv, qkv, qkv)


def _outproj(x, z_g, y_abc, attn, w_out, name):
    S, D = x.shape
    tm = TM_MM
    n_abc = 3 * GROUP_W

    def body(x_ref, g_ref, yabc_ref, o1, l1, o2, l2, o3, l3, w_ref,
             xn_ref, y_ref, o_ref, lse1_ref, lse4_ref, lse16_ref, so2, sl2, so3, sl3, slse, tmp_ref):
        for src, dst, dil in ((o2, so2, ATTN_DILATIONS[1]), (l2, sl2, ATTN_DILATIONS[1]),
                              (o3, so3, ATTN_DILATIONS[2]), (l3, sl3, ATTN_DILATIONS[2])):
            _interleave(src, dst, dil, tmp_ref)
        la, lb, lc = l1[...], _get(sl2), _get(sl3)
        mx = jnp.maximum(jnp.maximum(la, lb), lc)
        ea, eb, ec = jnp.exp(la - mx), jnp.exp(lb - mx), jnp.exp(lc - mx)
        den = ea + eb + ec
        o = (ea * o1[...].astype(F32) + eb * _get(so2) + ec * _get(so3)) / den
        o_ref[...] = o
        _put(slse, mx + jnp.log(den))
        for dil, ref in zip(ATTN_DILATIONS, (lse1_ref, lse4_ref, lse16_ref)):
            _deinterleave(slse, ref, dil, tmp_ref)
        y_d = o * _silu_and_grad(g_ref[...])[0]
        y_ref[:, 0:n_abc] = yabc_ref[...].astype(MXU_DTYPE)
        y_ref[:, n_abc:] = y_d.astype(MXU_DTYPE)
        xn_ref[...] = x_ref[...] + jnp.dot(y_ref[...], w_ref[...], preferred_element_type=F32)

    row = lambda w: pl.BlockSpec((tm, w), lambda i: (i, 0))
    dil_specs = [_dilated_spec(tm, GROUP_W, dil) for dil in ATTN_DILATIONS]
    (o1, l1), (o2, l2), (o3, l3) = attn
    return pl.pallas_call(
        body, name=name, grid=(S // tm,),
        in_specs=[row(D), row(GROUP_W), row(n_abc)] + [sp for sp in dil_specs for _ in range(2)]
                 + [pl.BlockSpec(w_out.shape, lambda i: (0, 0))],
        out_specs=[row(D), row(4 * GROUP_W), row(GROUP_W)] + dil_specs,
        out_shape=[jax.ShapeDtypeStruct((S, D), F32), jax.ShapeDtypeStruct((S, 4 * GROUP_W), MXU_DTYPE),
                   jax.ShapeDtypeStruct((S, GROUP_W), F32)]
                  + [_dilated_shape(S, GROUP_W, dil, F32) for dil in ATTN_DILATIONS],
        scratch_shapes=[_lane_scratch(tm, GROUP_W)] * 6,
        compiler_params=_params(("parallel",)),
    )(x, z_g, y_abc, o1, l1, o2, l2, o3, l3, w_out)


def _loss_head(x, g, target, name):
    S, D = x.shape
    tm = TM_MM

    def body(x_ref, g_ref, t_ref, dx_ref, loss_ref, dg_ref):
        i = pl.program_id(0)

        @pl.when(i == 0)
        def _():
            loss_ref[...] = jnp.zeros_like(loss_ref)
            dg_ref[...] = jnp.zeros_like(dg_ref)

        xv = x_ref[...]
        r = lax.rsqrt(jnp.mean(xv * xv, axis=-1, keepdims=True) + NORM_EPS)
        xn = xv * r
        err = xn * g_ref[...] - t_ref[...]
        per_tok = jnp.mean(err * err, axis=-1, keepdims=True)
        loss_ref[...] += 0.5 * jnp.sum(per_tok, axis=0, keepdims=True)
        dout = err * (1.0 / D)
        dg_ref[...] += _colsum(dout * xn)
        dxn = dout * g_ref[...]
        dx_ref[...] = r * (dxn - xn * jnp.mean(dxn * xn, axis=-1, keepdims=True))

    row = pl.BlockSpec((tm, D), lambda i: (i, 0))
    return pl.pallas_call(
        body, name=name, grid=(S // tm,),
        in_specs=[row, pl.BlockSpec((1, D), lambda i: (0, 0)), row],
        out_specs=[row, pl.BlockSpec((1, LANES), lambda i: (0, 0)), pl.BlockSpec((1, D), lambda i: (0, 0))],
        out_shape=[jax.ShapeDtypeStruct((S, D), F32), jax.ShapeDtypeStruct((1, LANES), F32),
                   jax.ShapeDtypeStruct((1, D), F32)],
        compiler_params=_params(("arbitrary",)),
    )(x, g, target)


def _outproj_mix_bwd(dx, y, w_out, z, z_g, hs, o, mp, name):
    S, D = dx.shape
    E = y.shape[1]
    tm = TM_MIX
    hb = tm // SUBLANES
    nT = S // tm
    last_blk = S // SUBLANES - 1
    wcols = N_ABC * GROUP_W

    def body(dx_ref, y_ref, w_ref, z_ref, zh_ref, zn_ref, zg_ref, h_ref, hh_ref, o_ref,
             wA_ref, wR_ref, vec_ref, wa_ref, wx_ref, ws_ref, bs_ref,
             dw_ref, dz_ref, dzg_ref, do1_ref, do4_ref, do16_ref, dl1_ref, dl4_ref, dl16_ref,
             dwA_ref, dwR_ref, dvec_ref, dwa_ref, dwx_ref, dws_ref, dbs_ref,
             hcarry_ref, xcarry_ref, bsacc_ref, do_ref, dl_ref, sa_ref, sb_ref, sc_ref, dy_ref, dyn_ref, acc_ref,
             tmp_ref):
        i = pl.program_id(0)
        ti = nT - 1 - i

        @pl.when(i == 0)
        def _():
            acc_ref[...] = jnp.zeros_like(acc_ref)
            dyn_ref[...] = jnp.zeros_like(dyn_ref)
            hcarry_ref[...] = jnp.zeros_like(hcarry_ref)
            xcarry_ref[...] = jnp.zeros_like(xcarry_ref)
            bsacc_ref[...] = jnp.zeros_like(bsacc_ref)
            dwA_ref[...] = jnp.zeros_like(dwA_ref)
            dwR_ref[...] = jnp.zeros_like(dwR_ref)
            dvec_ref[...] = jnp.zeros_like(dvec_ref)
            dwa_ref[...] = jnp.zeros_like(dwa_ref)
            dwx_ref[...] = jnp.zeros_like(dwx_ref)
            dws_ref[...] = jnp.zeros_like(dws_ref)
            dbs_ref[...] = jnp.zeros_like(dbs_ref)

        dxb = dx_ref[...].astype(MXU_DTYPE)
        dy_ref[...] = _mm_nt(dxb, w_ref[...])
        acc_ref[...] += _mm_tn(y_ref[...], dxb)

        @pl.when(i == nT - 1)
        def _():
            dw_ref[...] = acc_ref[...].astype(dw_ref.dtype)

        has_prev = ti > 0
        has_next = i > 0
        col = lambda c: slice(c * GROUP_W, (c + 1) * GROUP_W)
        z_of = lambda c: z_ref[:, col(c)]
        halo_of = lambda c: jnp.where(has_prev, zh_ref[:, col(c)], 0.0)
        next_of = lambda c: zn_ref[:, col(c)]

        p, p_h, cv = _conv_a(z_of, halo_of, wA_ref)
        sg, dsg = _silu_and_grad(z_of(3))
        a_b = z_of(1)
        dya = dy_ref[:, col(0)]
        dcv = dya * a_b * sg
        dcv_n = jnp.where(has_next, dyn_ref[...] * next_of(1) * _silu_and_grad(next_of(3))[0], 0.0)
        dp = (wA_ref[2:3, :] * dcv + wA_ref[1:2, :] * _shift_up(dcv, dcv_n, 1)
              + wA_ref[0:1, :] * _shift_up(dcv, dcv_n, 2))
        dwA_ref[2:3, :] += _colsum(dcv * p)
        dwA_ref[1:2, :] += _colsum(dcv * _shift_down(p, p_h, 1))
        dwA_ref[0:1, :] += _colsum(dcv * _shift_down(p, p_h, 2))
        def put_dz(c, val):
            dz_ref[:, col(c)] = val.astype(dz_ref.dtype)

        put_dz(0, dp * z_of(2))
        put_dz(1, dya * cv * sg)
        put_dz(2, dp * z_of(0))
        put_dz(3, dya * a_b * cv * dsg)

        xc, sh, ga, gi, a, mult, sp = _lru_gates(z_of, halo_of, wR_ref, vec_ref, wa_ref, wx_ref)
        h = h_ref[...]
        h_prev = _shift_down(h, jnp.where(has_prev, hh_ref[...], 0.0), 1)
        sgr, dsgr = _silu_and_grad(z_of(5))
        dyb = dy_ref[:, col(1)]
        put_dz(5, dyb * h * dsgr)
        row = lax.broadcasted_iota(jnp.int32, (tm, GROUP_W), 0)
        g_in = dyb * sgr + jnp.where(row == tm - 1, hcarry_ref[0:1, :], 0.0)
        a_up = _shift_up(a, jnp.zeros((SUBLANES, GROUP_W), F32), 1)
        dH = _scan_rev_tile(a_up, g_in, sa_ref, sb_ref, sc_ref)
        hcarry_ref[...] = (a * dH)[0:SUBLANES]
        da = dH * h_prev
        gx = gi * xc
        dmult = dH * gx
        dgi = dH * mult * xc
        dxc = dH * mult * gi
        dlog_a = da * a - dmult * (a * a) / mult
        dga = dlog_a * (-RG_C * sp)
        dlam_row = _colsum(dlog_a * (-RG_C * ga)) * (-_sigmoid(-vec_ref[3:4, :]))
        dpre_a = dga * ga * (1.0 - ga)
        dpre_i = dgi * gi * (1.0 - gi)
        dwa_ref[...] += _mm_tn(xc, dpre_a)
        dwx_ref[...] += _mm_tn(xc, dpre_i)
        dxc = dxc + _mm_nt(dpre_a, wa_ref[...]) + _mm_nt(dpre_i, wx_ref[...])
        dvec_ref[0:1, :] += _colsum(dxc)
        dvec_ref[1:2, :] += _colsum(dpre_a)
        dvec_ref[2:3, :] += _colsum(dpre_i)
        dvec_ref[3:4, :] += dlam_row
        for k in range(4):
            dwR_ref[k:k + 1, :] += _colsum(dxc * sh[3 - k])
        dxc_n = xcarry_ref[...]
        put_dz(4, wR_ref[3:4, :] * dxc + wR_ref[2:3, :] * _shift_up(dxc, dxc_n, 1)
               + wR_ref[1:2, :] * _shift_up(dxc, dxc_n, 2) + wR_ref[0:1, :] * _shift_up(dxc, dxc_n, 3))
        xcarry_ref[...] = dxc[0:SUBLANES]

        c_u, c_v = z_of(6), z_of(7)
        u, du_dx = _gelu_and_grad(c_u)
        gv, dgv_dx = _gelu_and_grad(c_v)
        rr = lax.rsqrt(jnp.mean(gv * gv, axis=-1, keepdims=True) + NORM_EPS)
        xhat = gv * rr
        g_c = vec_ref[4:5, :]
        vn = xhat * g_c
        masks = _head_masks((GMLP_CHUNK, GROUP_W))
        tri_r = lax.broadcasted_iota(jnp.int32, (GMLP_CHUNK, GMLP_CHUNK), 0)
        tri_c = lax.broadcasted_iota(jnp.int32, (GMLP_CHUNK, GMLP_CHUNK), 1)
        tril = tri_r >= tri_c
        sgc, dsgc = _silu_and_grad(z_of(8))
        dyc = dy_ref[:, col(2)]
        dsp_full = dyc * u * sgc
        sp_parts, dvn_parts = [], []
        for c in range(tm // GMLP_CHUNK):
            rs = slice(c * GMLP_CHUNK, (c + 1) * GMLP_CHUNK)
            vc = vn[rs].astype(MXU_DTYPE)
            dsp_c = dsp_full[rs]
            bsacc_ref[...] += dsp_c
            acc = bs_ref[...]
            dvn_c = jnp.zeros((GMLP_CHUNK, GROUP_W), F32)
            for h in range(N_HEADS):
                w_h = ws_ref[h]
                acc = acc + jnp.where(masks[h], jnp.dot(w_h, vc, preferred_element_type=F32), 0.0)
                dsp_h = jnp.where(masks[h], dsp_c, 0.0).astype(MXU_DTYPE)
                dvn_c = dvn_c + _mm_tn(w_h, dsp_h)
                dws_ref[h] += jnp.where(tril, _mm_nt(dsp_h, vc), 0.0)
            sp_parts.append(acc)
            dvn_parts.append(dvn_c)
        spv = jnp.concatenate(sp_parts, axis=0)
        dvn = jnp.concatenate(dvn_parts, axis=0)
        put_dz(6, dyc * spv * sgc * du_dx)
        put_dz(8, dyc * u * spv * dsgc)
        dvec_ref[4:5, :] += _colsum(dvn * xhat)
        dgvn = dvn * g_c
        dgv = rr * (dgvn - xhat * jnp.mean(dgvn * xhat, axis=-1, keepdims=True))
        put_dz(7, dgv * dgv_dx)

        sgd, dsgd = _silu_and_grad(zg_ref[...])
        dyd = dy_ref[:, col(3)]
        ov = o_ref[...]
        do = dyd * sgd
        _put(do_ref, do)
        dzg_ref[...] = (dyd * ov * dsgd).astype(dzg_ref.dtype)
        prod = do * ov
        tmasks = _head_masks((tm, GROUP_W))
        dl = jnp.zeros((tm, GROUP_W), F32)
        for h in range(N_HEADS):
            dl = jnp.where(tmasks[h], jnp.sum(jnp.where(tmasks[h], prod, 0.0), axis=-1, keepdims=True), dl)
        _put(dl_ref, dl)
        for dil, d_out, l_out in zip(ATTN_DILATIONS, (do1_ref, do4_ref, do16_ref), (dl1_ref, dl4_ref, dl16_ref)):
            _deinterleave(do_ref, d_out, dil, tmp_ref)
            _deinterleave(dl_ref, l_out, dil, tmp_ref)

        @pl.when(i == nT - 1)
        def _():
            acc = bsacc_ref[...]
            lane = lax.broadcasted_iota(jnp.int32, (GMLP_CHUNK, LANES), 1)
            out = jnp.zeros((GMLP_CHUNK, LANES), F32)
            for h in range(N_HEADS):
                out = jnp.where(lane == h, jnp.sum(jnp.where(masks[h], acc, 0.0), axis=-1, keepdims=True), out)
            dbs_ref[...] = out

        dyn_ref[...] = dy_ref[0:SUBLANES, 0:GROUP_W]

    rev = lambda w: pl.BlockSpec((tm, w), lambda i: (nT - 1 - i, 0))
    prev8 = lambda w: pl.BlockSpec((SUBLANES, w), lambda i: (jnp.maximum((nT - 1 - i) * hb - 1, 0), 0))
    next8 = lambda w: pl.BlockSpec((SUBLANES, w), lambda i: (jnp.minimum((nT - i) * hb, last_blk), 0))
    const2 = lambda shape: pl.BlockSpec(shape, lambda i: (0, 0))
    dil_specs = [_dilated_spec(tm, GROUP_W, dil, lambda i: nT - 1 - i) for dil in ATTN_DILATIONS]
    dil_shapes = [_dilated_shape(S, GROUP_W, dil, F32) for dil in ATTN_DILATIONS]
    small = (SUBLANES, GROUP_W)
    sq = (GROUP_W, GROUP_W)
    ws_shape = (N_HEADS, GMLP_CHUNK, GMLP_CHUNK)
    return pl.pallas_call(
        body, name=name, grid=(nT,),
        in_specs=[rev(D), rev(E), pl.BlockSpec((E, D), lambda i: (0, 0), pipeline_mode=pl.Buffered(1)),
                  rev(wcols), prev8(wcols), next8(wcols), rev(GROUP_W), rev(GROUP_W), prev8(GROUP_W), rev(GROUP_W)]
                 + _mix_specs(tm, S, "bwd"),
        out_specs=[const2((E, D)), rev(wcols), rev(GROUP_W)] + dil_specs + dil_specs
                  + [const2(small), const2(small), const2(small), const2(sq), const2(sq),
                     pl.BlockSpec(ws_shape, lambda i: (0, 0, 0)), const2((GMLP_CHUNK, LANES))],
        out_shape=[jax.ShapeDtypeStruct((E, D), WIRE_DTYPE),
                   jax.ShapeDtypeStruct((S, wcols), MXU_DTYPE), jax.ShapeDtypeStruct((S, GROUP_W), MXU_DTYPE)]
                  + [_dilated_shape(S, GROUP_W, dil, MXU_DTYPE) for dil in ATTN_DILATIONS] + dil_shapes
                  + [jax.ShapeDtypeStruct(small, F32)] * 3 + [jax.ShapeDtypeStruct(sq, F32)] * 2
                  + [jax.ShapeDtypeStruct(ws_shape, F32), jax.ShapeDtypeStruct((GMLP_CHUNK, LANES), F32)],
        scratch_shapes=[pltpu.VMEM(small, F32), pltpu.VMEM(small, F32), pltpu.VMEM((GMLP_CHUNK, GROUP_W), F32),
                        _lane_scratch(tm, GROUP_W), _lane_scratch(tm, GROUP_W),
                        _lane_scratch(tm, GROUP_W), _lane_scratch(tm, GROUP_W), pltpu.VMEM((hb, GROUP_W), F32),
                        pltpu.VMEM((tm, E), F32), pltpu.VMEM(small, F32), pltpu.VMEM((E, D), F32),
                        _lane_scratch(tm, GROUP_W)],
        compiler_params=_params(("arbitrary",)),
    )(dx, y, w_out, z, z, z, z_g, hs, hs, o, mp["wA"], mp["wR"], mp["vec"], mp["wa"], mp["wx"], mp["ws"], mp["bs"])


def _attn_bwd(qkv, do, lse, delta, dil, name):
    rows = qkv.shape[0]
    nb = rows // ATTN_BLOCK
    scale = 1.0 / math.sqrt(HEAD_DIM)
    B = ATTN_BLOCK
    per_step = min(ATTN_BWD_BLOCKS_PER_STEP, nb)
    n_steps = nb // per_step

    def body(qc_ref, qn_ref, kc_ref, kp_ref, vc_ref, vp_ref, doc_ref, don_ref, lc_ref, ln_ref, dc_ref, dn_ref,
             dq_ref, dk_ref, dv_ref, bias_ref, bias_next_ref):
        n = pl.program_id(1)

        @pl.when(n == 0)
        def _():
            bias_ref[...] = _attn_bias(dil, (B,), 2 * B)
            bias_next_ref[...] = _attn_bias(dil, (B,), B)

        masks = _head_masks((B, GROUP_W))

        def per_row(tile):
            return jnp.concatenate([jnp.max(jnp.where(masks[h], tile, _NEG), axis=-1, keepdims=True)
                                    for h in range(N_HEADS)], axis=0)

        def grads(q, dov, lse_tile, dl_tile, keys, vals, bias, dead):
            qs = _stack_heads(q, masks)
            dos = _stack_heads(dov.astype(MXU_DTYPE), masks)
            s = _mm_nt(qs, keys) * scale + bias
            if dead is not None:
                s = jnp.where(dead(s.shape), _NEG, s)
            p = jnp.exp(s - per_row(lse_tile))
            ds = (p * (_mm_nt(dos, vals) - per_row(dl_tile)) * scale).astype(MXU_DTYPE)
            return ds, _mm_tn(ds, qs), _mm_tn(p.astype(MXU_DTYPE), dos)

        for j in range(per_step):
            own = slice(j * B, (j + 1) * B)
            before = slice((j - 1) * B, j * B)
            keys = jnp.concatenate([kp_ref[...] if j == 0 else kc_ref[before], kc_ref[own]], axis=0)
            vals = jnp.concatenate([vp_ref[...] if j == 0 else vc_ref[before], vc_ref[own]], axis=0)
            dead = (lambda shape: (n == 0) & (lax.broadcasted_iota(jnp.int32, shape, 1) < B)) if j == 0 else None
            ds, dk2, dv2 = grads(qc_ref[own], doc_ref[own], lc_ref[own], dc_ref[own], keys, vals, bias_ref[...], dead)
            dq_ref[own] = _unstack_heads(jnp.dot(ds, keys, preferred_element_type=F32), masks).astype(dq_ref.dtype)
            if j > 0:
                dk_ref[before] = (dk_own + dk2[:B]).astype(dk_ref.dtype)
                dv_ref[before] = (dv_own + dv2[:B]).astype(dv_ref.dtype)
            dk_own, dv_own = dk2[B:], dv2[B:]
        last = slice((per_step - 1) * B, per_step * B)
        _, dk1, dv1 = grads(qn_ref[...], don_ref[...], ln_ref[...], dn_ref[...], kc_ref[last], vc_ref[last],
                            bias_next_ref[...], lambda shape: n == n_steps - 1)
        dk_ref[last] = (dk_own + dk1).astype(dk_ref.dtype)
        dv_ref[last] = (dv_own + dv1).astype(dv_ref.dtype)

    blk = (per_step * B, GROUP_W)
    one = (B, GROUP_W)
    nxt_idx = lambda n: jnp.minimum((n + 1) * per_step, nb - 1)
    prv_idx = lambda n: jnp.maximum(n * per_step - 1, 0)
    zcur = lambda c: pl.BlockSpec(blk, lambda r, n: (n, r * 3 + c))
    znext = lambda c: pl.BlockSpec(one, lambda r, n: (nxt_idx(n), r * 3 + c))
    zprev = lambda c: pl.BlockSpec(one, lambda r, n: (prv_idx(n), r * 3 + c))
    cur = pl.BlockSpec(blk, lambda r, n: (n, r))
    nxt = pl.BlockSpec(one, lambda r, n: (nxt_idx(n), r))
    return pl.pallas_call(
        body, name=name, grid=(dil, n_steps),
        in_specs=[zcur(0), znext(0), zcur(1), zprev(1), zcur(2), zprev(2), cur, nxt, cur, nxt, cur, nxt],
        out_specs=[cur, cur, cur],
        out_shape=[jax.ShapeDtypeStruct((rows, dil * GROUP_W), WIRE_DTYPE)] * 3,
        scratch_shapes=[pltpu.VMEM((N_HEADS * B, 2 * B), F32), pltpu.VMEM((N_HEADS * B, B), F32)],
        compiler_params=_params(("parallel", "arbitrary")),
    )(qkv, qkv, qkv, qkv, qkv, qkv, do, do, lse, lse, delta, delta)


def _inproj_bwd(x, g, dxn, dz_abc, dqkv, dz_g, w_t, name):
    S, D = x.shape
    N = w_t.shape[0]
    tm = TM_MM
    n_abc = N_ABC * GROUP_W

    def body(x_ref, g_ref, dxn_ref, dabc_ref, q1, k1, v1, q2, k2, v2, q3, k3, v3, dg_ref, w_ref,
             dx_ref, dz_ref, h_ref, dgn_ref, s4_ref, s16_ref, tmp_ref):
        i = pl.program_id(0)

        @pl.when(i == 0)
        def _():
            dgn_ref[...] = jnp.zeros_like(dgn_ref)

        dz_ref[:, 0:n_abc] = dabc_ref[...].astype(MXU_DTYPE)
        for j, parts in enumerate(((q1, q2, q3), (k1, k2, k3), (v1, v2, v3))):
            c0 = n_abc + j * GROUP_W
            _interleave(parts[1], s4_ref, ATTN_DILATIONS[1])
            _interleave(parts[2], s16_ref, ATTN_DILATIONS[2], tmp_ref)
            dz_ref[:, c0:c0 + GROUP_W] = (parts[0][...] + _get(s4_ref) + _get(s16_ref)).astype(MXU_DTYPE)
        dz_ref[:, n_abc + 3 * GROUP_W:] = dg_ref[...].astype(MXU_DTYPE)
        dh = jnp.dot(dz_ref[...], w_ref[...], preferred_element_type=F32)
        xv = x_ref[...]
        r = lax.rsqrt(jnp.mean(xv * xv, axis=-1, keepdims=True) + NORM_EPS)
        xn = xv * r
        gv = g_ref[...]
        h_ref[...] = (xn * gv).astype(MXU_DTYPE)
        dgn_ref[...] += _colsum(dh * xn)
        dn = dh * gv
        dx_ref[...] = dxn_ref[...] + r * (dn - xn * jnp.mean(dn * xn, axis=-1, keepdims=True))

    row = lambda w: pl.BlockSpec((tm, w), lambda i: (i, 0))
    flat = [t for p in dqkv for t in p]
    dil_specs = [_dilated_spec(tm, GROUP_W, dil) for dil in ATTN_DILATIONS for _ in range(3)]
    return pl.pallas_call(
        body, name=name, grid=(S // tm,),
        in_specs=[row(D), pl.BlockSpec((1, D), lambda i: (0, 0)), row(D), row(n_abc)] + dil_specs
                 + [row(GROUP_W), pl.BlockSpec((N, D), lambda i: (0, 0), pipeline_mode=pl.Buffered(1))],
        out_specs=[row(D), row(N), row(D), pl.BlockSpec((1, D), lambda i: (0, 0))],
        out_shape=[jax.ShapeDtypeStruct((S, D), F32), jax.ShapeDtypeStruct((S, N), MXU_DTYPE),
                   jax.ShapeDtypeStruct((S, D), MXU_DTYPE), jax.ShapeDtypeStruct((1, D), F32)],
        scratch_shapes=[_lane_scratch(tm, GROUP_W)] * 3,
        compiler_params=_params(("arbitrary",)),
    )(x, g, dxn, dz_abc, *flat, dz_g, w_t)


def _inproj_wgrad(h, dz, name):
    S, D = h.shape
    N = dz.shape[1]
    tm = TM_WGRAD
    nj = 2
    cw = N // nj
    per = N_DEV // nj
    n_loc = N // N_DEV

    def body(h_ref, dz_ref, dw_ref, acc_ref):
        i = pl.program_id(1)

        @pl.when(i == 0)
        def _():
            acc_ref[...] = jnp.zeros_like(acc_ref)

        acc_ref[...] += _mm_tn(dz_ref[...], h_ref[...])

        @pl.when(i == S // tm - 1)
        def _():
            for b in range(per):
                dw_ref[b] = acc_ref[b * n_loc:(b + 1) * n_loc, :].astype(dw_ref.dtype)

    return pl.pallas_call(
        body, name=name, grid=(nj, S // tm),
        in_specs=[pl.BlockSpec((tm, D), lambda j, i: (i, 0)), pl.BlockSpec((tm, cw), lambda j, i: (i, j))],
        out_specs=pl.BlockSpec((per, n_loc, D), lambda j, i: (j, 0, 0)),
        out_shape=jax.ShapeDtypeStruct((N_DEV, n_loc, D), WIRE_DTYPE),
        scratch_shapes=[pltpu.VMEM((cw, D), F32)],
        compiler_params=_params(("parallel", "arbitrary")),
    )(h, dz)


def _my_place():
    return lax.axis_index("x"), lax.axis_index("y"), lax.axis_index("c")


def _peer(x, y, c, k):
    px = 1 - x if k & 4 else x
    py = 1 - y if k & 2 else y
    pc = 1 - c if k & 1 else c
    return (px, py, pc), 4 * px + 2 * py + pc


HBM_SPEC = pl.BlockSpec(memory_space=pltpu.HBM)
SEM_SPEC = pl.BlockSpec(memory_space=pltpu.SEMAPHORE)
SPLIT_EFFECT = pltpu.SideEffectType.DATAFLOW_SIDE_EFFECTING
N_PEERS = N_DEV - 1


def _exchange_copies(srcs, lands, send_sems, recv_sems, whole, arrival):
    x, y, c = _my_place()
    me = 4 * x + 2 * y + c
    copies = []
    for t in range(len(srcs)):
        for k in range(1, N_DEV):
            peer, pidx = _peer(x, y, c, k)
            copies.append(pltpu.make_async_remote_copy(
                src_ref=srcs[t] if whole[t] else srcs[t].at[pidx],
                dst_ref=lands[t].at[pidx if arrival else me], send_sem=send_sems.at[t * N_PEERS + k - 1],
                recv_sem=recv_sems.at[t * N_PEERS + k - 1], device_id=peer, device_id_type=MESH))
    return copies


def _exchange_start(groups, name, after=None):
    sizes = [len(g) for g in groups]
    whole = [w for g in groups for _, w in g]
    srcs = [pltpu.with_memory_space_constraint(a, pltpu.HBM) for g in groups for a, _ in g]
    lands = [pltpu.with_memory_space_constraint(lax.empty(((N_DEV,) + a.shape) if w else a.shape, a.dtype), pltpu.HBM)
             for a, w in zip(srcs, whole)]
    n = len(srcs)
    n_g = len(groups)
    extra = [] if after is None else [after]
    n_in = 2 * n + len(extra)

    def body(*refs):
        src_refs, land_refs = refs[:n], refs[n:2 * n]
        sem_refs = refs[n_in + 2 * n:n_in + 2 * n + 2 * n_g]
        token = refs[-1]
        off = 0
        for gi, sz in enumerate(sizes):
            for send in _exchange_copies(src_refs[off:off + sz], land_refs[off:off + sz],
                                         sem_refs[2 * gi], sem_refs[2 * gi + 1], whole[off:off + sz], False):
                send.start()
            off += sz
        token[...] = jnp.zeros_like(token)

    sem_shapes = [pltpu.SemaphoreType.DMA((sz * N_PEERS,)) for sz in sizes for _ in range(2)]
    outs = pl.pallas_call(
        body, name=name,
        in_specs=[HBM_SPEC] * (2 * n) + [pl.BlockSpec(memory_space=pl.ANY)] * len(extra),
        out_specs=[HBM_SPEC] * (2 * n) + [SEM_SPEC] * (2 * n_g) + [pl.BlockSpec(memory_space=pltpu.VMEM)],
        out_shape=[pltpu.HBM(a.shape, a.dtype) for a in srcs + lands] + sem_shapes
                  + [jax.ShapeDtypeStruct((SUBLANES, LANES), F32)],
        input_output_aliases={i: i for i in range(2 * n)},
        compiler_params=pltpu.CompilerParams(has_side_effects=SPLIT_EFFECT),
    )(*srcs, *lands, *extra)
    handles, off = [], 0
    for gi, sz in enumerate(sizes):
        handles.append((outs[2 * n + 2 * gi], outs[2 * n + 2 * gi + 1], outs[off:off + sz], outs[n + off:n + off + sz],
                        whole[off:off + sz]))
        off += sz
    return handles, outs[-1]


def _exchange_wait(handle, after, name):
    send_sems, recv_sems, srcs, lands, whole = handle
    n = len(srcs)

    def body(*refs):
        src_refs, land_refs = refs[:n], refs[n:2 * n]
        for send in _exchange_copies(src_refs, land_refs, refs[2 * n], refs[2 * n + 1], whole, False):
            send.wait_send()
        for arrival in _exchange_copies(src_refs, land_refs, refs[2 * n], refs[2 * n + 1], whole, True):
            arrival.wait_recv()

    outs = pl.pallas_call(
        body, name=name,
        in_specs=[HBM_SPEC] * (2 * n) + [SEM_SPEC, SEM_SPEC, pl.BlockSpec(memory_space=pl.ANY)],
        out_specs=[HBM_SPEC] * (2 * n),
        out_shape=[pltpu.HBM(a.shape, a.dtype) for a in list(srcs) + list(lands)],
        input_output_aliases={i: i for i in range(2 * n)},
        compiler_params=pltpu.CompilerParams(has_side_effects=SPLIT_EFFECT),
    )(*srcs, *lands, send_sems, recv_sems, after)
    x, y, c = _my_place()
    me = 4 * x + 2 * y + c
    own = [s[None] if w else lax.dynamic_slice_in_dim(s, me, 1, axis=0) for s, w in zip(outs[:n], whole)]
    return [lax.dynamic_update_slice_in_dim(ld, o, me, axis=0) for ld, o in zip(outs[n:], own)]


def _sum_slots(parts, name):
    n = len(parts)

    def body(*refs):
        for p_ref, o_ref in zip(refs[:n], refs[n:]):
            acc = p_ref[0]
            for j in range(1, N_DEV):
                acc = acc + p_ref[j]
            o_ref[...] = acc

    vm = pl.BlockSpec(memory_space=pltpu.VMEM)
    return pl.pallas_call(
        body, name=name, in_specs=[vm] * n, out_specs=[vm] * n,
        out_shape=[jax.ShapeDtypeStruct(p.shape[1:], F32) for p in parts],
        compiler_params=pltpu.CompilerParams(vmem_limit_bytes=VMEM_LIMIT),
    )(*parts)


def _adamw_math(w, g, m, v):
    m = ADAM_B1 * m + (1.0 - ADAM_B1) * g
    v = ADAM_B2 * v + (1.0 - ADAM_B2) * (g * g)
    m_hat = m / (1.0 - ADAM_B1 ** ADAM_STEP)
    v_hat = v / (1.0 - ADAM_B2 ** ADAM_STEP)
    delta = -ADAM_LR * (m_hat / (jnp.sqrt(v_hat) + ADAM_EPS) + ADAM_WD * w)
    return delta, m, v


def _adamw_summed(parts, w, m, v, tr, name):
    depth, R, C = w.shape

    def body(*refs):
        p_refs = refs[:depth]
        w_ref, m_ref, v_ref, g_ref, d_ref, nm_ref, nv_ref = refs[depth:]
        lay = pl.program_id(0)
        for l in range(depth):
            @pl.when(lay == l)
            def _(p_ref=p_refs[l]):
                g = p_ref[0].astype(F32)
                for j in range(1, N_DEV):
                    g = g + p_ref[j].astype(F32)
                g_ref[0] = g
        d_ref[0], nm_ref[0], nv_ref[0] = _adamw_math(w_ref[0], g_ref[0], m_ref[0], v_ref[0])

    part_spec = lambda l: pl.BlockSpec((N_DEV, tr, C), lambda lay, i: (0, jnp.where(lay == l, i, 0), 0))
    row = pl.BlockSpec((1, tr, C), lambda lay, i: (lay, i, 0))
    return pl.pallas_call(
        body, name=name, grid=(depth, R // tr),
        in_specs=[part_spec(l) for l in range(depth)] + [row, row, row],
        out_specs=[row] * 4, out_shape=[jax.ShapeDtypeStruct((depth, R, C), F32)] * 4,
        compiler_params=_params(("arbitrary", "arbitrary")),
    )(*parts, w, m, v)


def _adamw_small(w, g, m, v, name):
    def body(w_ref, g_ref, m_ref, v_ref, d_ref, nm_ref, nv_ref):
        d_ref[...], nm_ref[...], nv_ref[...] = _adamw_math(w_ref[...], g_ref[...], m_ref[...], v_ref[...])

    vm = pl.BlockSpec(memory_space=pltpu.VMEM)
    return pl.pallas_call(
        body, name=name, in_specs=[vm] * 4, out_specs=[vm] * 3,
        out_shape=[jax.ShapeDtypeStruct(w.shape, F32)] * 3,
        compiler_params=pltpu.CompilerParams(vmem_limit_bytes=VMEM_LIMIT),
    )(w, g, m, v)


def _pack(arrays):
    flat = jnp.concatenate([a.reshape(-1) for a in arrays])
    pad = (-flat.shape[0]) % (SUBLANES * LANES)
    return jnp.pad(flat, (0, pad)).reshape(-1, LANES)


def _unpack(buf, like):
    flat = buf.reshape(-1)
    out, off = [], 0
    for a in like:
        out.append(flat[off:off + a.size].reshape(a.shape))
        off += a.size
    return out


def _block_diag(w):
    eye = jnp.eye(N_HEADS, dtype=w.dtype)
    return jnp.einsum('hij,hk->hikj', w, eye).reshape(GROUP_W, GROUP_W)


def _diag_blocks(w):
    return jnp.einsum('hihj->hij', w.reshape(N_HEADS, HEAD_DIM, N_HEADS, HEAD_DIM))


def _pad_rows(a):
    return jnp.pad(a, ((0, SUBLANES - a.shape[0]), (0, 0)))


def _mixer_params(l, conv_a_w, conv_r_w, conv_r_b, lru_wa, lru_ba, lru_wx, lru_bx, lru_lambda, gmlp_norm_g,
                  gmlp_ws, gmlp_bs):
    tril = jnp.tril(jnp.ones((GMLP_CHUNK, GMLP_CHUNK), dtype=bool))
    vec = jnp.stack([conv_r_b[l], lru_ba[l], lru_bx[l], lru_lambda[l], gmlp_norm_g[l]])
    return {
        "wA": _pad_rows(conv_a_w[l]), "wR": _pad_rows(conv_r_w[l]), "vec": _pad_rows(vec),
        "wa": _block_diag(lru_wa[l]).astype(MXU_DTYPE), "wx": _block_diag(lru_wx[l]).astype(MXU_DTYPE),
        "ws": jnp.where(tril[None], gmlp_ws[l], 0.0).astype(MXU_DTYPE),
        "bs": jnp.repeat(jnp.transpose(gmlp_bs[l]), HEAD_DIM, axis=1),
    }


MIXER_NAMES = ("conv_a_w", "conv_r_w", "conv_r_b", "lru_wa", "lru_ba", "lru_wx", "lru_bx", "lru_lambda",
               "gmlp_norm_g", "gmlp_ws", "gmlp_bs")
SMALL_NAMES = ("norm_g",) + MIXER_NAMES + ("final_g",)


def _local_step(x, loss_target, norm_g, get_w_in, get_w_out, emit_early, emit_late, conv_a_w, conv_r_w, conv_r_b,
                lru_wa, lru_ba, lru_wx, lru_bx, lru_lambda, gmlp_norm_g, gmlp_ws, gmlp_bs, final_g):
    depth = norm_g.shape[0]
    D = x.shape[1]
    small = (conv_a_w, conv_r_w, conv_r_b, lru_wa, lru_ba, lru_wx, lru_bx, lru_lambda, gmlp_norm_g, gmlp_ws, gmlp_bs)
    saved = []
    for l in range(depth):
        mp = _mixer_params(l, *small)
        w_in_l = get_w_in(l, x)
        z, z_g, *qkv, y_abc, hs = _inproj_mix_fwd(x, norm_g[l].reshape(1, D), w_in_l, mp, f"inproj_mix_fwd_{l}")
        attn =[_attn_fwd(qkv[p], dil, f"attn_fwd_d{dil}_{l}") for p, dil in enumerate(ATTN_DILATIONS)]
        w_out_l = get_w_out(l, y_abc)
        x_new, y, o, *lse = _outproj(x, z_g, y_abc, attn, w_out_l, f"outproj_{l}")
        saved.append((x, z, z_g, qkv, hs, y, o, lse, mp, w_in_l, w_out_l))
        x = x_new
    dx, loss, d_final_g = _loss_head(x, final_g.reshape(1, D), loss_target, "loss_head")
    token = None
    for l in reversed(range(depth)):
        x_l, z, z_g, qkv, hs, y, o, lse, mp, w_in_l, w_out_l = saved[l]
        if token is not None:
            mp = dict(mp, vec=mp["vec"] + token[0, 0])
        (dw_out, dz_abc, dz_g, do1, do4, do16, dl1, dl4, dl16, dwA, dwR, dvec, dwa, dwx, dws, dbs) = _outproj_mix_bwd(
            dx, y, w_out_l, z, z_g, hs, o, mp, f"outproj_mix_bwd_{l}")
        token = emit_early(l, dw_out, [
            dwA[:conv_a_w.shape[1]], dwR[:conv_r_w.shape[1]], dvec[0], _diag_blocks(dwa), dvec[1], _diag_blocks(dwx),
            dvec[2], dvec[3], dvec[4], dws, jnp.transpose(dbs[:, :N_HEADS])])
        g_row = norm_g[l].reshape(1, D)
        if token is not None:
            g_row = g_row + token[0, 0]
        dqkv = [_attn_bwd(qkv[p], do, lse[p], dl, dil, f"attn_bwd_d{dil}_{l}")
                for p, (dil, do, dl) in enumerate(zip(ATTN_DILATIONS, (do1, do4, do16), (dl1, dl4, dl16)))]
        dx, dz, h, dng = _inproj_bwd(x_l, g_row, dx, dz_abc, dqkv, dz_g, w_in_l, f"inproj_bwd_{l}")
        dw_in = _inproj_wgrad(h, dz, f"inproj_wgrad_{l}")
        token = emit_late(l, dw_in, [dng[0]] + ([d_final_g[0]] if l == depth - 1 else []))
    return loss[0, 0], dx
WEIGHT_NAMES = ("norm_g", "w_in", "conv_a_w", "conv_r_w", "conv_r_b", "lru_wa", "lru_ba", "lru_wx", "lru_bx",
                "lru_lambda", "gmlp_norm_g", "gmlp_ws", "gmlp_bs", "w_out", "final_g")


def kernel(x, norm_g, w_in, conv_a_w, conv_r_w, conv_r_b, lru_wa, lru_ba, lru_wx, lru_bx, lru_lambda, gmlp_norm_g, gmlp_ws, gmlp_bs, w_out, final_g, loss_target, m_norm_g, m_w_in, m_conv_a_w, m_conv_r_w, m_conv_r_b, m_lru_wa, m_lru_ba, m_lru_wx, m_lru_bx, m_lru_lambda, m_gmlp_norm_g, m_gmlp_ws, m_gmlp_bs, m_w_out, m_final_g, v_norm_g, v_w_in, v_conv_a_w, v_conv_r_w, v_conv_r_b, v_lru_wa, v_lru_ba, v_lru_wx, v_lru_bx, v_lru_lambda, v_gmlp_norm_g, v_gmlp_ws, v_gmlp_bs, v_w_out, v_final_g):
    w = dict(norm_g=norm_g, w_in=w_in, conv_a_w=conv_a_w, conv_r_w=conv_r_w, conv_r_b=conv_r_b, lru_wa=lru_wa,
             lru_ba=lru_ba, lru_wx=lru_wx, lru_bx=lru_bx, lru_lambda=lru_lambda, gmlp_norm_g=gmlp_norm_g,
             gmlp_ws=gmlp_ws, gmlp_bs=gmlp_bs, w_out=w_out, final_g=final_g)
    m = dict(norm_g=m_norm_g, w_in=m_w_in, conv_a_w=m_conv_a_w, conv_r_w=m_conv_r_w, conv_r_b=m_conv_r_b,
             lru_wa=m_lru_wa, lru_ba=m_lru_ba, lru_wx=m_lru_wx, lru_bx=m_lru_bx, lru_lambda=m_lru_lambda,
             gmlp_norm_g=m_gmlp_norm_g, gmlp_ws=m_gmlp_ws, gmlp_bs=m_gmlp_bs, w_out=m_w_out, final_g=m_final_g)
    v = dict(norm_g=v_norm_g, w_in=v_w_in, conv_a_w=v_conv_a_w, conv_r_w=v_conv_r_w, conv_r_b=v_conv_r_b,
             lru_wa=v_lru_wa, lru_ba=v_lru_ba, lru_wx=v_lru_wx, lru_bx=v_lru_bx, lru_lambda=v_lru_lambda,
             gmlp_norm_g=v_gmlp_norm_g, gmlp_ws=v_gmlp_ws, gmlp_bs=v_gmlp_bs, w_out=v_w_out, final_g=v_final_g)
    depth, D, n_loc = w_in.shape
    e_loc = w_out.shape[1]
    cx, cy, cc = _my_place()
    me = 4 * cx + 2 * cy + cc

    transposed = lambda a: jnp.transpose(a, (0, 2, 1))
    w_in_t, m_w_in_t, v_w_in_t = transposed(w_in), transposed(m_w_in), transposed(v_w_in)
    w_in_w, w_out_w = w_in_t.astype(MXU_DTYPE), w_out.astype(MXU_DTYPE)
    c_loc = conv_a_w.shape[2]
    taps = (conv_a_w, conv_r_w)
    first, _ = _exchange_start([[(w_in_w[0], True), (_pack(taps), True)], [(w_out_w[0], True)]], "gather_start_first")
    full_in = lambda g: g.reshape(N_DEV * n_loc, D)
    full_out = lambda g: g.reshape(N_DEV * e_loc, D)

    g_in0, g_taps = _exchange_wait(first[0], x, "gather_wait_in_0")
    groups = [[(w_in_w[l], True), (w_out_w[l], True)] for l in range(1, depth)]
    gathers, rest_token = _exchange_start(groups, "gather_start_rest", after=g_taps)
    g_taps = g_taps.reshape(N_DEV, -1) + rest_token[0, 0]
    conv_full, off = [], 0
    for a in taps:
        part = g_taps[:, off:off + a.size].reshape((N_DEV,) + a.shape)
        conv_full.append(jnp.transpose(part, (1, 2, 0, 3)).reshape(a.shape[:2] + (N_DEV * c_loc,)))
        off += a.size
    conv_a_full, conv_r_full = conv_full
    later = {}

    def get_w_in(l, after):
        if l == 0:
            return full_in(g_in0)
        g_in, later[l] = _exchange_wait(gathers[l - 1], after, f"gather_wait_{l}")
        return full_in(g_in)

    def get_w_out(l, after):
        if l == 0:
            return full_out(_exchange_wait(first[1], after, "gather_wait_out_0")[0])
        return full_out(later[l])

    early, late, last_token = {}, {}, [None]

    def emit_early(l, dw_out, mixer_grads):
        handles, token = _exchange_start(
            [[(dw_out.reshape(N_DEV, e_loc, D), False), (_pack(mixer_grads), True)]], f"early_start_{l}")
        early[l] = (handles[0], mixer_grads)
        return token

    def emit_late(l, dw_in, norm_grads):
        handles, token = _exchange_start([[(_pack(norm_grads), True)], [(dw_in, False)]], f"late_start_{l}")
        late[l] = (handles[0], handles[1], norm_grads)
        last_token[0] = token
        return token

    loss, grad_x = _local_step(
        x[0], loss_target[0], norm_g, get_w_in, get_w_out, emit_early, emit_late, conv_a_full, conv_r_full, conv_r_b,
        lru_wa, lru_ba, lru_wx, lru_bx, lru_lambda, gmlp_norm_g, gmlp_ws, gmlp_bs, final_g)
    loss = lax.psum(loss, ("x", "y", "c"))

    r_in, r_out, small_parts = {}, {}, []
    for l in reversed(range(depth)):
        r_out[l], r_mix = _exchange_wait(early[l][0], last_token[0], f"early_wait_{l}")
        (r_norm,) = _exchange_wait(late[l][0], last_token[0], f"late_wait_norm_{l}")
        small_parts += [r_mix, r_norm]
        if l > 0:
            (r_in[l],) = _exchange_wait(late[l][1], last_token[0], f"late_wait_{l}")
    big = {"w_out": _adamw_summed([r_out[l] for l in range(depth)], w_out, m_w_out, v_w_out, 128, "adamw_w_out")}

    sums = _sum_slots(small_parts, "sum_small_grads")
    by_layer = {}
    for i, l in enumerate(reversed(range(depth))):
        mix = _unpack(sums[2 * i], early[l][1])
        nrm = _unpack(sums[2 * i + 1], late[l][2])
        by_layer[l] = dict(zip(MIXER_NAMES, mix), norm_g=nrm[0])
        if l == depth - 1:
            g_final = nrm[1]
    g_small = {k: jnp.stack([by_layer[l][k] for l in range(depth)]) for k in ("norm_g",) + MIXER_NAMES}
    g_small["final_g"] = g_final
    for k in ("conv_a_w", "conv_r_w"):
        g_small[k] = lax.dynamic_slice_in_dim(g_small[k], me * c_loc, c_loc, axis=2)
    packs = [_pack([d[k] for k in SMALL_NAMES]) for d in (w, g_small, m, v)]
    res = _adamw_small(*packs, "adamw_small")
    like = [w[k] for k in SMALL_NAMES]
    d_s, m_s, v_s = (dict(zip(SMALL_NAMES, _unpack(r, like))) for r in res)

    (r_in[0],) = _exchange_wait(late[0][1], res[0], "late_wait_0")
    big["w_in"] = [transposed(a) for a in _adamw_summed(
        [r_in[l] for l in range(depth)], w_in_t, m_w_in_t, v_w_in_t, n_loc // 2, "adamw_w_in")]

    grad, delta, new_m, new_v = {}, {}, {}, {}
    for k in WEIGHT_NAMES:
        if k in big:
            grad[k], delta[k], new_m[k], new_v[k] = big[k]
        else:
            grad[k], delta[k], new_m[k], new_v[k] = g_small[k], d_s[k], m_s[k], v_s[k]
    return (loss, grad_x[None], *[grad[k] for k in WEIGHT_NAMES], *[delta[k] for k in WEIGHT_NAMES],
            *[new_m[k] for k in WEIGHT_NAMES], *[new_v[k] for k in WEIGHT_NAMES])
```

```python
import functools
import math

import jax
import jax.numpy as jnp
from jax import lax
from jax.experimental import pallas as pl
from jax.experimental.pallas import tpu as pltpu

F32 = jnp.float32
MXU_DTYPE = jnp.bfloat16
WIRE_DTYPE = jnp.bfloat16
MESH = pl.DeviceIdType.MESH

N_DEV = 8
GROUP_W = 256
N_HEADS = 4
HEAD_DIM = 64
N_CHUNKS = 13
N_ABC = 9
GMLP_CHUNK = 128
ATTN_BLOCK = 128
ATTN_FWD_BLOCKS_PER_STEP = 16
ATTN_BWD_BLOCKS_PER_STEP = 8
ATTN_DILATIONS = (1, 4, 16)
NORM_EPS = 1e-6
RG_C = 8.0
SUBLANES = 8
LANES = 128
VMEM_LIMIT = 56 * 1024 * 1024

ADAM_LR = 0.001
ADAM_B1 = 0.9
ADAM_B2 = 0.999
ADAM_EPS = 1e-08
ADAM_WD = 0.01
ADAM_STEP = 10

TM_MIX = 512
TM_MM = 512
TM_WGRAD = 1024


def _params(sem, vmem=VMEM_LIMIT):
    return pltpu.CompilerParams(dimension_semantics=sem, vmem_limit_bytes=vmem)


def _mm(a, b):
    return jnp.dot(a.astype(MXU_DTYPE), b.astype(MXU_DTYPE), preferred_element_type=F32)


def _mm_tn(a, b):
    return lax.dot_general(a.astype(MXU_DTYPE), b.astype(MXU_DTYPE), (((0,), (0,)), ((), ())),
                           preferred_element_type=F32)


def _mm_nt(a, b):
    return lax.dot_general(a.astype(MXU_DTYPE), b.astype(MXU_DTYPE), (((1,), (1,)), ((), ())),
                           preferred_element_type=F32)


def _sigmoid(x):
    return 0.5 * jnp.tanh(0.5 * x) + 0.5


def _sigmoid_small_exact(x):
    return 1.0 / (1.0 + jnp.exp(-x))


def _silu_and_grad(x):
    s = _sigmoid(x)
    return x * s, s * (1.0 + x * (1.0 - s))


_GELU_K = math.sqrt(2.0 / math.pi)
_GELU_C = 0.044715


def _gelu_and_grad(x):
    x2 = x * x
    t = jnp.tanh(_GELU_K * (x + _GELU_C * x * x2))
    val = 0.5 * x * (1.0 + t)
    grad = 0.5 * (1.0 + t) + 0.5 * x * (1.0 - t * t) * (_GELU_K * (1.0 + 3.0 * _GELU_C * x2))
    return val, grad


def _gelu(x):
    return 0.5 * x * (1.0 + jnp.tanh(_GELU_K * (x + _GELU_C * x * x * x)))


def _expm1_nonpos(u):
    poly = 1.0 / math.factorial(9)
    for k in range(8, 0, -1):
        poly = poly * u + 1.0 / math.factorial(k)
    return jnp.where(u > -0.25, poly * u, jnp.exp(u) - 1.0)


def _softplus(x):
    return jnp.maximum(x, 0.0) + jnp.log(1.0 + jnp.exp(-jnp.abs(x)))


def _shift_down(t, halo, k):
    rolled = pltpu.roll(t, k, 0)
    hr = pltpu.roll(halo, k, 0)
    row = lax.broadcasted_iota(jnp.int32, halo.shape, 0)
    first = jnp.where(row < k, hr, rolled[0:SUBLANES])
    return jnp.concatenate([first, rolled[SUBLANES:]], axis=0)


def _shift_up(t, nxt, k):
    tm = t.shape[0]
    rolled = pltpu.roll(t, tm - k, 0)
    nr = pltpu.roll(nxt, SUBLANES - k, 0)
    row = lax.broadcasted_iota(jnp.int32, nxt.shape, 0)
    last = jnp.where(row >= SUBLANES - k, nr, rolled[tm - SUBLANES:tm])
    return jnp.concatenate([rolled[:tm - SUBLANES], last], axis=0)


def _scan_fwd(a, b):
    tm = a.shape[0]
    row = lax.broadcasted_iota(jnp.int32, a.shape, 0)
    s = 1
    while s < tm:
        a_s = pltpu.roll(a, s, 0)
        b_s = pltpu.roll(b, s, 0)
        m = row >= s
        b = jnp.where(m, a * b_s + b, b)
        a = jnp.where(m, a * a_s, a)
        s *= 2
    return a, b


def _scan_rev(a, g):
    tm = a.shape[0]
    row = lax.broadcasted_iota(jnp.int32, a.shape, 0)
    s = 1
    while s < tm:
        a_s = pltpu.roll(a, tm - s, 0)
        g_s = pltpu.roll(g, tm - s, 0)
        m = row < tm - s
        g = jnp.where(m, g + a * g_s, g)
        a = jnp.where(m, a * a_s, a)
        s *= 2
    return g


def _group_rows(scr_ref, row, n_groups):
    return jnp.concatenate([scr_ref[pl.ds(c, 1), pl.ds(row, n_groups, stride=SUBLANES), :][0]
                            for c in range(scr_ref.shape[0])], axis=1)


def _spread_rows(rows_ref, n_groups, w):
    return jnp.concatenate([jnp.broadcast_to(rows_ref[g:g + 1, :], (SUBLANES, w)) for g in range(n_groups)], axis=0)


def _scan_groups(a, b, reverse):
    tm, w = a.shape
    shape3 = (tm // SUBLANES, SUBLANES, w)
    a3, b3 = a.reshape(shape3), b.reshape(shape3)
    sub = lax.broadcasted_iota(jnp.int32, shape3, 1)
    s = 1
    while s < SUBLANES:
        shift = SUBLANES - s if reverse else s
        a_s = pltpu.roll(a3, shift, 1)
        b_s = pltpu.roll(b3, shift, 1)
        m = (sub < SUBLANES - s) if reverse else (sub >= s)
        b3 = jnp.where(m, a3 * b_s + b3, b3)
        a3 = jnp.where(m, a3 * a_s, a3)
        s *= 2
    return a3.reshape(tm, w), b3.reshape(tm, w)


def _scan_fwd_tile(a, b, h_in, sa_ref, sb_ref, sc_ref):
    tm, w = a.shape
    n_groups = tm // SUBLANES
    a_loc, b_loc = _scan_groups(a, b, False)
    _put(sa_ref, a_loc)
    _put(sb_ref, b_loc)
    a_end, b_end = _scan_fwd(_group_rows(sa_ref, SUBLANES - 1, n_groups), _group_rows(sb_ref, SUBLANES - 1, n_groups))
    h_end = b_end + a_end * h_in
    sc_ref[...] = _shift_down(h_end, jnp.broadcast_to(h_in, (SUBLANES, w)), 1)
    return b_loc + a_loc * _spread_rows(sc_ref, n_groups, w), h_end


def _scan_rev_tile(a, g, sa_ref, sb_ref, sc_ref):
    tm, w = a.shape
    n_groups = tm // SUBLANES
    a_loc, g_loc = _scan_groups(a, g, True)
    _put(sa_ref, a_loc)
    _put(sb_ref, g_loc)
    d_first = _scan_rev(_group_rows(sa_ref, 0, n_groups), _group_rows(sb_ref, 0, n_groups))
    sc_ref[...] = _shift_up(d_first, jnp.zeros((SUBLANES, w), F32), 1)
    return g_loc + a_loc * _spread_rows(sc_ref, n_groups, w)


def _lane_scratch(tm, w):
    return pltpu.VMEM((w // LANES, tm, LANES), F32)


def _put(scr_ref, val):
    for c in range(scr_ref.shape[0]):
        scr_ref[c] = val[:, c * LANES:(c + 1) * LANES].astype(F32)


def _get(scr_ref):
    return jnp.concatenate([scr_ref[c] for c in range(scr_ref.shape[0])], axis=1)


MAX_ROW_STRIDE = 4


def _strided_rows(c, start, n, stride):
    return (pl.ds(c, 1), pl.ds(start, n, stride=stride), slice(None))


def _deinterleave(src_ref, dst_ref, dil, tmp_ref=None):
    nc, tm, _ = src_ref.shape
    w = nc * LANES
    s1 = min(dil, MAX_ROW_STRIDE)
    s2 = dil // s1
    if s2 > 1:
        for r0 in range(s1):
            for c in range(nc):
                tmp_ref[c, r0 * (tm // s1):(r0 + 1) * (tm // s1), :] = src_ref[_strided_rows(c, r0, tm // s1, s1)][0]
    for r in range(dil):
        r1, r0 = divmod(r, s1)
        for c in range(nc):
            if dil == 1:
                piece = src_ref[c]
            elif s2 == 1:
                piece = src_ref[_strided_rows(c, r, tm // dil, dil)][0]
            else:
                piece = tmp_ref[_strided_rows(c, r0 * (tm // s1) + r1, tm // dil, s2)][0]
            dst_ref[:, r * w + c * LANES:r * w + (c + 1) * LANES] = piece.astype(dst_ref.dtype)


def _interleave(src_ref, dst_ref, dil, tmp_ref=None):
    nc, tm, _ = dst_ref.shape
    w = nc * LANES
    s1 = min(dil, MAX_ROW_STRIDE)
    s2 = dil // s1
    for r in range(dil):
        r1, r0 = divmod(r, s1)
        for c in range(nc):
            piece = src_ref[:, r * w + c * LANES:r * w + (c + 1) * LANES].astype(F32)[None]
            if s2 == 1:
                dst_ref[_strided_rows(c, r, tm // dil, dil)] = piece
            else:
                tmp_ref[_strided_rows(c, r0 * (tm // s1) + r1, tm // dil, s2)] = piece
    if s2 > 1:
        for r0 in range(s1):
            for c in range(nc):
                dst_ref[_strided_rows(c, r0, tm // s1, s1)] = (
                    tmp_ref[c, r0 * (tm // s1):(r0 + 1) * (tm // s1), :][None])


def _dilated_spec(tm, w, dil, index=lambda i: i):
    return pl.BlockSpec((tm // dil, dil * w), lambda i: (index(i), 0))


def _dilated_shape(S, w, dil, dtype):
    return jax.ShapeDtypeStruct((S // dil, dil * w), dtype)


def _head_masks(shape):
    lane = lax.broadcasted_iota(jnp.int32, shape, 1)
    return [(lane >= h * HEAD_DIM) & (lane < (h + 1) * HEAD_DIM) for h in range(N_HEADS)]


def _colsum(v):
    return jnp.sum(v, axis=0, keepdims=True)


def _conv_a(z_of, halo_of, w_ref):
    p = z_of(2) * z_of(0)
    p_h = halo_of(2) * halo_of(0)
    cv = w_ref[2:3, :] * p + w_ref[1:2, :] * _shift_down(p, p_h, 1) + w_ref[0:1, :] * _shift_down(p, p_h, 2)
    return p, p_h, cv


def _lru_gates(z_of, halo_of, wr_ref, vec_ref, wa_ref, wx_ref):
    rx = z_of(4)
    rx_h = halo_of(4)
    sh = [rx, _shift_down(rx, rx_h, 1), _shift_down(rx, rx_h, 2), _shift_down(rx, rx_h, 3)]
    xc = (wr_ref[3:4, :] * sh[0] + wr_ref[2:3, :] * sh[1] + wr_ref[1:2, :] * sh[2]
          + wr_ref[0:1, :] * sh[3] + vec_ref[0:1, :])
    ga = _sigmoid_small_exact(jnp.dot(xc.astype(MXU_DTYPE), wa_ref[...], preferred_element_type=F32) + vec_ref[1:2, :])
    gi = _sigmoid(jnp.dot(xc.astype(MXU_DTYPE), wx_ref[...], preferred_element_type=F32) + vec_ref[2:3, :])
    sp = _softplus(-vec_ref[3:4, :])
    log_a = (-RG_C * ga) * sp
    a = jnp.exp(log_a)
    mult = jnp.sqrt(-_expm1_nonpos(2.0 * log_a))
    return xc, sh, ga, gi, a, mult, sp


def _gmlp_fwd(z_of, vec_ref, ws_ref, bs_ref, tm):
    u = _gelu(z_of(6))
    gv = _gelu(z_of(7))
    rr = lax.rsqrt(jnp.mean(gv * gv, axis=-1, keepdims=True) + NORM_EPS)
    vn = (gv * rr) * vec_ref[4:5, :]
    masks = _head_masks((GMLP_CHUNK, GROUP_W))
    parts = []
    for c in range(tm // GMLP_CHUNK):
        vc = vn[c * GMLP_CHUNK:(c + 1) * GMLP_CHUNK].astype(MXU_DTYPE)
        acc = bs_ref[...]
        for h in range(N_HEADS):
            acc = acc + jnp.where(masks[h], jnp.dot(ws_ref[h], vc, preferred_element_type=F32), 0.0)
        parts.append(acc)
    return u, gv, rr, vn, jnp.concatenate(parts, axis=0)


def _mix_specs(tm, S, order):
    const2 = lambda shape: pl.BlockSpec(shape, lambda i: (0, 0))
    return [const2((SUBLANES, GROUP_W)), const2((SUBLANES, GROUP_W)), const2((SUBLANES, GROUP_W)),
            const2((GROUP_W, GROUP_W)), const2((GROUP_W, GROUP_W)),
            pl.BlockSpec((N_HEADS, GMLP_CHUNK, GMLP_CHUNK), lambda i: (0, 0, 0)),
            const2((GMLP_CHUNK, GROUP_W))]


def _inproj_mix_fwd(x, g, w_t, mp, name):
    S, D = x.shape
    N = w_t.shape[0]
    tm = TM_MIX
    hb = tm // SUBLANES
    n_abc = N_ABC * GROUP_W
    n_qkv = 3 * GROUP_W

    def body(x_ref, g_ref, w_ref, wA_ref, wR_ref, vec_ref, wa_ref, wx_ref, ws_ref, bs_ref,
             z_ref, zg_ref, q1_ref, q4_ref, q16_ref, y_ref, h_ref,
             qkv_ref, halo_ref, carry_ref, sa_ref, sb_ref, sc_ref, tmp_ref):
        @pl.when(pl.program_id(0) == 0)
        def _():
            halo_ref[...] = jnp.zeros_like(halo_ref)
            carry_ref[...] = jnp.zeros_like(carry_ref)

        xv = x_ref[...]
        r = lax.rsqrt(jnp.mean(xv * xv, axis=-1, keepdims=True) + NORM_EPS)
        hn = ((xv * r) * g_ref[...]).astype(MXU_DTYPE)
        z_ref[...] = _mm_nt(hn, w_ref[0:n_abc, :])
        _put(qkv_ref, _mm_nt(hn, w_ref[n_abc:n_abc + n_qkv, :]))
        zg_ref[...] = _mm_nt(hn, w_ref[n_abc + n_qkv:, :])
        for dil, ref in zip(ATTN_DILATIONS, (q1_ref, q4_ref, q16_ref)):
            _deinterleave(qkv_ref, ref, dil, tmp_ref)

        z_of = lambda c: z_ref[:, c * GROUP_W:(c + 1) * GROUP_W]
        halo_of = lambda c: halo_ref[:, c * GROUP_W:(c + 1) * GROUP_W]

        _, _, cv = _conv_a(z_of, halo_of, wA_ref)
        y_ref[:, 0:GROUP_W] = (z_of(1) * cv * _silu_and_grad(z_of(3))[0]).astype(y_ref.dtype)

        xc, _, _, gi, a, mult, _ = _lru_gates(z_of, halo_of, wR_ref, vec_ref, wa_ref, wx_ref)
        b = mult * (gi * xc)
        h, h_end = _scan_fwd_tile(a, b, carry_ref[SUBLANES - 1:SUBLANES, :], sa_ref, sb_ref, sc_ref)
        h_ref[...] = h
        carry_ref[...] = h_end[hb - SUBLANES:hb]
        y_ref[:, GROUP_W:2 * GROUP_W] = (h * _silu_and_grad(z_of(5))[0]).astype(y_ref.dtype)

        u, _, _, _, sp = _gmlp_fwd(z_of, vec_ref, ws_ref, bs_ref, tm)
        y_ref[:, 2 * GROUP_W:3 * GROUP_W] = (u * sp * _silu_and_grad(z_of(8))[0]).astype(y_ref.dtype)
        halo_ref[...] = z_ref[tm - SUBLANES:tm, :]

    row = lambda wd: pl.BlockSpec((tm, wd), lambda i: (i, 0))
    return pl.pallas_call(
        body, name=name, grid=(S // tm,),
        in_specs=[row(D), pl.BlockSpec((1, D), lambda i: (0, 0)),
                  pl.BlockSpec((N, D), lambda i: (0, 0), pipeline_mode=pl.Buffered(1))] + _mix_specs(tm, S, "fwd"),
        out_specs=[row(n_abc), row(GROUP_W)] + [_dilated_spec(tm, n_qkv, dil) for dil in ATTN_DILATIONS]
                  + [row(3 * GROUP_W), row(GROUP_W)],
        out_shape=[jax.ShapeDtypeStruct((S, n_abc), F32), jax.ShapeDtypeStruct((S, GROUP_W), F32)]
                  + [_dilated_shape(S, n_qkv, dil, MXU_DTYPE) for dil in ATTN_DILATIONS]
                  + [jax.ShapeDtypeStruct((S, 3 * GROUP_W), MXU_DTYPE), jax.ShapeDtypeStruct((S, GROUP_W), F32)],
        scratch_shapes=[_lane_scratch(tm, n_qkv), pltpu.VMEM((SUBLANES, n_abc), F32),
                        pltpu.VMEM((SUBLANES, GROUP_W), F32), _lane_scratch(tm, GROUP_W), _lane_scratch(tm, GROUP_W),
                        pltpu.VMEM((hb, GROUP_W), F32), _lane_scratch(tm, n_qkv)],
        compiler_params=_params(("arbitrary",)),
    )(x, g, w_t, mp["wA"], mp["wR"], mp["vec"], mp["wa"], mp["wx"], mp["ws"], mp["bs"])


_NEG = -1e30


def _slope(h):
    return 2.0 ** (-8.0 * (h + 1) / N_HEADS)


def _attn_bias(dil, offsets, n_keys):
    shape = (ATTN_BLOCK, n_keys)
    qi = lax.broadcasted_iota(jnp.int32, shape, 0)
    ki = lax.broadcasted_iota(jnp.int32, shape, 1)
    blocks = []
    for f in offsets:
        delta = qi + f - ki
        valid = (delta >= 0) & (delta <= ATTN_BLOCK)
        dist = (delta * dil).astype(F32)
        for h in range(N_HEADS):
            blocks.append(jnp.where(valid, -_slope(h) * dist, _NEG))
    return jnp.concatenate(blocks, axis=0)


def _stack_heads(t, masks):
    return jnp.concatenate([jnp.where(m, t, jnp.zeros_like(t)) for m in masks], axis=0)


def _unstack_heads(t4, masks, base=0):
    out = t4[base * ATTN_BLOCK:(base + 1) * ATTN_BLOCK]
    for h in range(1, N_HEADS):
        out = jnp.where(masks[h], t4[(base + h) * ATTN_BLOCK:(base + h + 1) * ATTN_BLOCK], out)
    return out


def _attn_fwd(qkv, dil, name):
    rows = qkv.shape[0]
    nb = rows // ATTN_BLOCK
    scale = 1.0 / math.sqrt(HEAD_DIM)
    B = ATTN_BLOCK
    per_step = min(ATTN_FWD_BLOCKS_PER_STEP, nb)

    def body(q_ref, kc_ref, kp_ref, vc_ref, vp_ref, o_ref, l_ref, bias_ref):
        n = pl.program_id(1)

        @pl.when(n == 0)
        def _():
            bias_ref[...] = _attn_bias(dil, (B,), 2 * B)

        masks = _head_masks((B, GROUP_W))
        for j in range(per_step):
            own = slice(j * B, (j + 1) * B)
            before = slice((j - 1) * B, j * B)
            qs = _stack_heads(q_ref[own], masks)
            keys = jnp.concatenate([kp_ref[...] if j == 0 else kc_ref[before], kc_ref[own]], axis=0)
            vals = jnp.concatenate([vp_ref[...] if j == 0 else vc_ref[before], vc_ref[own]], axis=0)
            s = _mm_nt(qs, keys) * scale + bias_ref[...]
            if j == 0:
                key_col = lax.broadcasted_iota(jnp.int32, s.shape, 1)
                s = jnp.where((n == 0) & (key_col < B), _NEG, s)
            m = jnp.max(s, axis=-1, keepdims=True)
            p = jnp.exp(s - m)
            l = jnp.sum(p, axis=-1, keepdims=True)
            o4 = jnp.dot(p.astype(MXU_DTYPE), vals, preferred_element_type=F32)
            o_ref[own] = (_unstack_heads(o4, masks)
                          / _unstack_heads(jnp.broadcast_to(l, o4.shape), masks)).astype(o_ref.dtype)
            l_ref[own] = _unstack_heads(jnp.broadcast_to(m + jnp.log(l), o4.shape), masks)

    blk = (per_step * B, GROUP_W)
    cur = lambda c: pl.BlockSpec(blk, lambda r, n: (n, r * 3 + c))
    prev = lambda c: pl.BlockSpec((B, GROUP_W), lambda r, n: (jnp.maximum(n * per_step - 1, 0), r * 3 + c))
    out = pl.BlockSpec(blk, lambda r, n: (n, r))
    return pl.pallas_call(
        body, name=name, grid=(dil, nb // per_step),
        in_specs=[cur(0), cur(1), prev(1), cur(2), prev(2)],
        out_specs=[out, out],
        out_shape=[jax.ShapeDtypeStruct((rows, dil * GROUP_W), MXU_DTYPE),
                   jax.ShapeDtypeStruct((rows, dil * GROUP_W), F32)],
        scratch_shapes=[pltpu.VMEM((N_HEADS * ATTN_BLOCK, 2 * ATTN_BLOCK), F32)],
        compiler_params=_params(("parallel", "arbitrary")),
    )(qkv, qkv, qkv, qkv, qkv)


def _outproj(x, z_g, y_abc, attn, w_out, name):
    S, D = x.shape
    tm = TM_MM
    n_abc = 3 * GROUP_W

    def body(x_ref, g_ref, yabc_ref, o1, l1, o2, l2, o3, l3, w_ref,
             xn_ref, y_ref, o_ref, lse1_ref, lse4_ref, lse16_ref, so2, sl2, so3, sl3, slse, tmp_ref):
        for src, dst, dil in ((o2, so2, ATTN_DILATIONS[1]), (l2, sl2, ATTN_DILATIONS[1]),
                              (o3, so3, ATTN_DILATIONS[2]), (l3, sl3, ATTN_DILATIONS[2])):
            _interleave(src, dst, dil, tmp_ref)
        la, lb, lc = l1[...], _get(sl2), _get(sl3)
        mx = jnp.maximum(jnp.maximum(la, lb), lc)
        ea, eb, ec = jnp.exp(la - mx), jnp.exp(lb - mx), jnp.exp(lc - mx)
        den = ea + eb + ec
        o = (ea * o1[...].astype(F32) + eb * _get(so2) + ec * _get(so3)) / den
        o_ref[...] = o
        _put(slse, mx + jnp.log(den))
        for dil, ref in zip(ATTN_DILATIONS, (lse1_ref, lse4_ref, lse16_ref)):
            _deinterleave(slse, ref, dil, tmp_ref)
        y_d = o * _silu_and_grad(g_ref[...])[0]
        y_ref[:, 0:n_abc] = yabc_ref[...].astype(MXU_DTYPE)
        y_ref[:, n_abc:] = y_d.astype(MXU_DTYPE)
        xn_ref[...] = x_ref[...] + jnp.dot(y_ref[...], w_ref[...], preferred_element_type=F32)

    row = lambda w: pl.BlockSpec((tm, w), lambda i: (i, 0))
    dil_specs = [_dilated_spec(tm, GROUP_W, dil) for dil in ATTN_DILATIONS]
    (o1, l1), (o2, l2), (o3, l3) = attn
    return pl.pallas_call(
        body, name=name, grid=(S // tm,),
        in_specs=[row(D), row(GROUP_W), row(n_abc)] + [sp for sp in dil_specs for _ in range(2)]
                 + [pl.BlockSpec(w_out.shape, lambda i: (0, 0))],
        out_specs=[row(D), row(4 * GROUP_W), row(GROUP_W)] + dil_specs,
        out_shape=[jax.ShapeDtypeStruct((S, D), F32), jax.ShapeDtypeStruct((S, 4 * GROUP_W), MXU_DTYPE),
                   jax.ShapeDtypeStruct((S, GROUP_W), F32)]
                  + [_dilated_shape(S, GROUP_W, dil, F32) for dil in ATTN_DILATIONS],
        scratch_shapes=[_lane_scratch(tm, GROUP_W)] * 6,
        compiler_params=_params(("parallel",)),
    )(x, z_g, y_abc, o1, l1, o2, l2, o3, l3, w_out)


def _loss_head(x, g, target, name):
    S, D = x.shape
    tm = TM_MM

    def body(x_ref, g_ref, t_ref, dx_ref, loss_ref, dg_ref):
        i = pl.program_id(0)

        @pl.when(i == 0)
        def _():
            loss_ref[...] = jnp.zeros_like(loss_ref)
            dg_ref[...] = jnp.zeros_like(dg_ref)

        xv = x_ref[...]
        r = lax.rsqrt(jnp.mean(xv * xv, axis=-1, keepdims=True) + NORM_EPS)
        xn = xv * r
        err = xn * g_ref[...] - t_ref[...]
        per_tok = jnp.mean(err * err, axis=-1, keepdims=True)
        loss_ref[...] += 0.5 * jnp.sum(per_tok, axis=0, keepdims=True)
        dout = err * (1.0 / D)
        dg_ref[...] += _colsum(dout * xn)
        dxn = dout * g_ref[...]
        dx_ref[...] = r * (dxn - xn * jnp.mean(dxn * xn, axis=-1, keepdims=True))

    row = pl.BlockSpec((tm, D), lambda i: (i, 0))
    return pl.pallas_call(
        body, name=name, grid=(S // tm,),
        in_specs=[row, pl.BlockSpec((1, D), lambda i: (0, 0)), row],
        out_specs=[row, pl.BlockSpec((1, LANES), lambda i: (0, 0)), pl.BlockSpec((1, D), lambda i: (0, 0))],
        out_shape=[jax.ShapeDtypeStruct((S, D), F32), jax.ShapeDtypeStruct((1, LANES), F32),
                   jax.ShapeDtypeStruct((1, D), F32)],
        compiler_params=_params(("arbitrary",)),
    )(x, g, target)


def _outproj_mix_bwd(dx, y, w_out, z, z_g, hs, o, mp, name):
    S, D = dx.shape
    E = y.shape[1]
    tm = TM_MIX
    hb = tm // SUBLANES
    nT = S // tm
    last_blk = S // SUBLANES - 1
    wcols = N_ABC * GROUP_W

    def body(dx_ref, y_ref, w_ref, z_ref, zh_ref, zn_ref, zg_ref, h_ref, hh_ref, o_ref,
             wA_ref, wR_ref, vec_ref, wa_ref, wx_ref, ws_ref, bs_ref,
             dw_ref, dz_ref, dzg_ref, do1_ref, do4_ref, do16_ref, dl1_ref, dl4_ref, dl16_ref,
             dwA_ref, dwR_ref, dvec_ref, dwa_ref, dwx_ref, dws_ref, dbs_ref,
             hcarry_ref, xcarry_ref, bsacc_ref, do_ref, dl_ref, sa_ref, sb_ref, sc_ref, dy_ref, dyn_ref, acc_ref,
             tmp_ref):
        i = pl.program_id(0)
        ti = nT - 1 - i

        @pl.when(i == 0)
        def _():
            acc_ref[...] = jnp.zeros_like(acc_ref)
            dyn_ref[...] = jnp.zeros_like(dyn_ref)
            hcarry_ref[...] = jnp.zeros_like(hcarry_ref)
            xcarry_ref[...] = jnp.zeros_like(xcarry_ref)
            bsacc_ref[...] = jnp.zeros_like(bsacc_ref)
            dwA_ref[...] = jnp.zeros_like(dwA_ref)
            dwR_ref[...] = jnp.zeros_like(dwR_ref)
            dvec_ref[...] = jnp.zeros_like(dvec_ref)
            dwa_ref[...] = jnp.zeros_like(dwa_ref)
            dwx_ref[...] = jnp.zeros_like(dwx_ref)
            dws_ref[...] = jnp.zeros_like(dws_ref)
            dbs_ref[...] = jnp.zeros_like(dbs_ref)

        dxb = dx_ref[...].astype(MXU_DTYPE)
        dy_ref[...] = _mm_nt(dxb, w_ref[...])
        acc_ref[...] += _mm_tn(y_ref[...], dxb)

        @pl.when(i == nT - 1)
        def _():
            dw_ref[...] = acc_ref[...].astype(dw_ref.dtype)

        has_prev = ti > 0
        has_next = i > 0
        col = lambda c: slice(c * GROUP_W, (c + 1) * GROUP_W)
        z_of = lambda c: z_ref[:, col(c)]
        halo_of = lambda c: jnp.where(has_prev, zh_ref[:, col(c)], 0.0)
        next_of = lambda c: zn_ref[:, col(c)]

        p, p_h, cv = _conv_a(z_of, halo_of, wA_ref)
        sg, dsg = _silu_and_grad(z_of(3))
        a_b = z_of(1)
        dya = dy_ref[:, col(0)]
        dcv = dya * a_b * sg
        dcv_n = jnp.where(has_next, dyn_ref[...] * next_of(1) * _silu_and_grad(next_of(3))[0], 0.0)
        dp = (wA_ref[2:3, :] * dcv + wA_ref[1:2, :] * _shift_up(dcv, dcv_n, 1)
              + wA_ref[0:1, :] * _shift_up(dcv, dcv_n, 2))
        dwA_ref[2:3, :] += _colsum(dcv * p)
        dwA_ref[1:2, :] += _colsum(dcv * _shift_down(p, p_h, 1))
        dwA_ref[0:1, :] += _colsum(dcv * _shift_down(p, p_h, 2))
        def put_dz(c, val):
            dz_ref[:, col(c)] = val.astype(dz_ref.dtype)

        put_dz(0, dp * z_of(2))
        put_dz(1, dya * cv * sg)
        put_dz(2, dp * z_of(0))
        put_dz(3, dya * a_b * cv * dsg)

        xc, sh, ga, gi, a, mult, sp = _lru_gates(z_of, halo_of, wR_ref, vec_ref, wa_ref, wx_ref)
        h = h_ref[...]
        h_prev = _shift_down(h, jnp.where(has_prev, hh_ref[...], 0.0), 1)
        sgr, dsgr = _silu_and_grad(z_of(5))
        dyb = dy_ref[:, col(1)]
        put_dz(5, dyb * h * dsgr)
        row = lax.broadcasted_iota(jnp.int32, (tm, GROUP_W), 0)
        g_in = dyb * sgr + jnp.where(row == tm - 1, hcarry_ref[0:1, :], 0.0)
        a_up = _shift_up(a, jnp.zeros((SUBLANES, GROUP_W), F32), 1)
        dH = _scan_rev_tile(a_up, g_in, sa_ref, sb_ref, sc_ref)
        hcarry_ref[...] = (a * dH)[0:SUBLANES]
        da = dH * h_prev
        gx = gi * xc
        dmult = dH * gx
        dgi = dH * mult * xc
        dxc = dH * mult * gi
        dlog_a = da * a - dmult * (a * a) / mult
        dga = dlog_a * (-RG_C * sp)
        dlam_row = _colsum(dlog_a * (-RG_C * ga)) * (-_sigmoid(-vec_ref[3:4, :]))
        dpre_a = dga * ga * (1.0 - ga)
        dpre_i = dgi * gi * (1.0 - gi)
        dwa_ref[...] += _mm_tn(xc, dpre_a)
        dwx_ref[...] += _mm_tn(xc, dpre_i)
        dxc = dxc + _mm_nt(dpre_a, wa_ref[...]) + _mm_nt(dpre_i, wx_ref[...])
        dvec_ref[0:1, :] += _colsum(dxc)
        dvec_ref[1:2, :] += _colsum(dpre_a)
        dvec_ref[2:3, :] += _colsum(dpre_i)
        dvec_ref[3:4, :] += dlam_row
        for k in range(4):
            dwR_ref[k:k + 1, :] += _colsum(dxc * sh[3 - k])
        dxc_n = xcarry_ref[...]
        put_dz(4, wR_ref[3:4, :] * dxc + wR_ref[2:3, :] * _shift_up(dxc, dxc_n, 1)
               + wR_ref[1:2, :] * _shift_up(dxc, dxc_n, 2) + wR_ref[0:1, :] * _shift_up(dxc, dxc_n, 3))
        xcarry_ref[...] = dxc[0:SUBLANES]

        c_u, c_v = z_of(6), z_of(7)
        u, du_dx = _gelu_and_grad(c_u)
        gv, dgv_dx = _gelu_and_grad(c_v)
        rr = lax.rsqrt(jnp.mean(gv * gv, axis=-1, keepdims=True) + NORM_EPS)
        xhat = gv * rr
        g_c = vec_ref[4:5, :]
        vn = xhat * g_c
        masks = _head_masks((GMLP_CHUNK, GROUP_W))
        tri_r = lax.broadcasted_iota(jnp.int32, (GMLP_CHUNK, GMLP_CHUNK), 0)
        tri_c = lax.broadcasted_iota(jnp.int32, (GMLP_CHUNK, GMLP_CHUNK), 1)
        tril = tri_r >= tri_c
        sgc, dsgc = _silu_and_grad(z_of(8))
        dyc = dy_ref[:, col(2)]
        dsp_full = dyc * u * sgc
        sp_parts, dvn_parts = [], []
        for c in range(tm // GMLP_CHUNK):
            rs = slice(c * GMLP_CHUNK, (c + 1) * GMLP_CHUNK)
            vc = vn[rs].astype(MXU_DTYPE)
            dsp_c = dsp_full[rs]
            bsacc_ref[...] += dsp_c
            acc = bs_ref[...]
            dvn_c = jnp.zeros((GMLP_CHUNK, GROUP_W), F32)
            for h in range(N_HEADS):
                w_h = ws_ref[h]
                acc = acc + jnp.where(masks[h], jnp.dot(w_h, vc, preferred_element_type=F32), 0.0)
                dsp_h = jnp.where(masks[h], dsp_c, 0.0).astype(MXU_DTYPE)
                dvn_c = dvn_c + _mm_tn(w_h, dsp_h)
                dws_ref[h] += jnp.where(tril, _mm_nt(dsp_h, vc), 0.0)
            sp_parts.append(acc)
            dvn_parts.append(dvn_c)
        spv = jnp.concatenate(sp_parts, axis=0)
        dvn = jnp.concatenate(dvn_parts, axis=0)
        put_dz(6, dyc * spv * sgc * du_dx)
        put_dz(8, dyc * u * spv * dsgc)
        dvec_ref[4:5, :] += _colsum(dvn * xhat)
        dgvn = dvn * g_c
        dgv = rr * (dgvn - xhat * jnp.mean(dgvn * xhat, axis=-1, keepdims=True))
        put_dz(7, dgv * dgv_dx)

        sgd, dsgd = _silu_and_grad(zg_ref[...])
        dyd = dy_ref[:, col(3)]
        ov = o_ref[...]
        do = dyd * sgd
        _put(do_ref, do)
        dzg_ref[...] = (dyd * ov * dsgd).astype(dzg_ref.dtype)
        prod = do * ov
        tmasks = _head_masks((tm, GROUP_W))
        dl = jnp.zeros((tm, GROUP_W), F32)
        for h in range(N_HEADS):
            dl = jnp.where(tmasks[h], jnp.sum(jnp.where(tmasks[h], prod, 0.0), axis=-1, keepdims=True), dl)
        _put(dl_ref, dl)
        for dil, d_out, l_out in zip(ATTN_DILATIONS, (do1_ref, do4_ref, do16_ref), (dl1_ref, dl4_ref, dl16_ref)):
            _deinterleave(do_ref, d_out, dil, tmp_ref)
            _deinterleave(dl_ref, l_out, dil, tmp_ref)

        @pl.when(i == nT - 1)
        def _():
            acc = bsacc_ref[...]
            lane = lax.broadcasted_iota(jnp.int32, (GMLP_CHUNK, LANES), 1)
            out = jnp.zeros((GMLP_CHUNK, LANES), F32)
            for h in range(N_HEADS):
                out = jnp.where(lane == h, jnp.sum(jnp.where(masks[h], acc, 0.0), axis=-1, keepdims=True), out)
            dbs_ref[...] = out

        dyn_ref[...] = dy_ref[0:SUBLANES, 0:GROUP_W]

    rev = lambda w: pl.BlockSpec((tm, w), lambda i: (nT - 1 - i, 0))
    prev8 = lambda w: pl.BlockSpec((SUBLANES, w), lambda i: (jnp.maximum((nT - 1 - i) * hb - 1, 0), 0))
    next8 = lambda w: pl.BlockSpec((SUBLANES, w), lambda i: (jnp.minimum((nT - i) * hb, last_blk), 0))
    const2 = lambda shape: pl.BlockSpec(shape, lambda i: (0, 0))
    dil_specs = [_dilated_spec(tm, GROUP_W, dil, lambda i: nT - 1 - i) for dil in ATTN_DILATIONS]
    dil_shapes = [_dilated_shape(S, GROUP_W, dil, F32) for dil in ATTN_DILATIONS]
    small = (SUBLANES, GROUP_W)
    sq = (GROUP_W, GROUP_W)
    ws_shape = (N_HEADS, GMLP_CHUNK, GMLP_CHUNK)
    return pl.pallas_call(
        body, name=name, grid=(nT,),
        in_specs=[rev(D), rev(E), pl.BlockSpec((E, D), lambda i: (0, 0), pipeline_mode=pl.Buffered(1)),
                  rev(wcols), prev8(wcols), next8(wcols), rev(GROUP_W), rev(GROUP_W), prev8(GROUP_W), rev(GROUP_W)]
                 + _mix_specs(tm, S, "bwd"),
        out_specs=[const2((E, D)), rev(wcols), rev(GROUP_W)] + dil_specs + dil_specs
                  + [const2(small), const2(small), const2(small), const2(sq), const2(sq),
                     pl.BlockSpec(ws_shape, lambda i: (0, 0, 0)), const2((GMLP_CHUNK, LANES))],
        out_shape=[jax.ShapeDtypeStruct((E, D), WIRE_DTYPE),
                   jax.ShapeDtypeStruct((S, wcols), MXU_DTYPE), jax.ShapeDtypeStruct((S, GROUP_W), MXU_DTYPE)]
                  + [_dilated_shape(S, GROUP_W, dil, MXU_DTYPE) for dil in ATTN_DILATIONS] + dil_shapes
                  + [jax.ShapeDtypeStruct(small, F32)] * 3 + [jax.ShapeDtypeStruct(sq, F32)] * 2
                  + [jax.ShapeDtypeStruct(ws_shape, F32), jax.ShapeDtypeStruct((GMLP_CHUNK, LANES), F32)],
        scratch_shapes=[pltpu.VMEM(small, F32), pltpu.VMEM(small, F32), pltpu.VMEM((GMLP_CHUNK, GROUP_W), F32),
                        _lane_scratch(tm, GROUP_W), _lane_scratch(tm, GROUP_W),
                        _lane_scratch(tm, GROUP_W), _lane_scratch(tm, GROUP_W), pltpu.VMEM((hb, GROUP_W), F32),
                        pltpu.VMEM((tm, E), F32), pltpu.VMEM(small, F32), pltpu.VMEM((E, D), F32),
                        _lane_scratch(tm, GROUP_W)],
        compiler_params=_params(("arbitrary",)),
    )(dx, y, w_out, z, z, z, z_g, hs, hs, o, mp["wA"], mp["wR"], mp["vec"], mp["wa"], mp["wx"], mp["ws"], mp["bs"])


def _attn_bwd(qkv, do, lse, delta, dil, name):
    rows = qkv.shape[0]
    nb = rows // ATTN_BLOCK
    scale = 1.0 / math.sqrt(HEAD_DIM)
    B = ATTN_BLOCK
    per_step = min(ATTN_BWD_BLOCKS_PER_STEP, nb)
    n_steps = nb // per_step

    def body(qc_ref, qn_ref, kc_ref, kp_ref, vc_ref, vp_ref, doc_ref, don_ref, lc_ref, ln_ref, dc_ref, dn_ref,
             dq_ref, dk_ref, dv_ref, bias_ref, bias_next_ref):
        n = pl.program_id(1)

        @pl.when(n == 0)
        def _():
            bias_ref[...] = _attn_bias(dil, (B,), 2 * B)
            bias_next_ref[...] = _attn_bias(dil, (B,), B)

        masks = _head_masks((B, GROUP_W))

        def per_row(tile):
            return jnp.concatenate([jnp.max(jnp.where(masks[h], tile, _NEG), axis=-1, keepdims=True)
                                    for h in range(N_HEADS)], axis=0)

        def grads(q, dov, lse_tile, dl_tile, keys, vals, bias, dead):
            qs = _stack_heads(q, masks)
            dos = _stack_heads(dov.astype(MXU_DTYPE), masks)
            s = _mm_nt(qs, keys) * scale + bias
            if dead is not None:
                s = jnp.where(dead(s.shape), _NEG, s)
            p = jnp.exp(s - per_row(lse_tile))
            ds = (p * (_mm_nt(dos, vals) - per_row(dl_tile)) * scale).astype(MXU_DTYPE)
            return ds, _mm_tn(ds, qs), _mm_tn(p.astype(MXU_DTYPE), dos)

        for j in range(per_step):
            own = slice(j * B, (j + 1) * B)
            before = slice((j - 1) * B, j * B)
            keys = jnp.concatenate([kp_ref[...] if j == 0 else kc_ref[before], kc_ref[own]], axis=0)
            vals = jnp.concatenate([vp_ref[...] if j == 0 else vc_ref[before], vc_ref[own]], axis=0)
            dead = (lambda shape: (n == 0) & (lax.broadcasted_iota(jnp.int32, shape, 1) < B)) if j == 0 else None
            ds, dk2, dv2 = grads(qc_ref[own], doc_ref[own], lc_ref[own], dc_ref[own], keys, vals, bias_ref[...], dead)
            dq_ref[own] = _unstack_heads(jnp.dot(ds, keys, preferred_element_type=F32), masks).astype(dq_ref.dtype)
            if j > 0:
                dk_ref[before] = (dk_own + dk2[:B]).astype(dk_ref.dtype)
                dv_ref[before] = (dv_own + dv2[:B]).astype(dv_ref.dtype)
            dk_own, dv_own = dk2[B:], dv2[B:]
        last = slice((per_step - 1) * B, per_step * B)
        _, dk1, dv1 = grads(qn_ref[...], don_ref[...], ln_ref[...], dn_ref[...], kc_ref[last], vc_ref[last],
                            bias_next_ref[...], lambda shape: n == n_steps - 1)
        dk_ref[last] = (dk_own + dk1).astype(dk_ref.dtype)
        dv_ref[last] = (dv_own + dv1).astype(dv_ref.dtype)

    blk = (per_step * B, GROUP_W)
    one = (B, GROUP_W)
    nxt_idx = lambda n: jnp.minimum((n + 1) * per_step, nb - 1)
    prv_idx = lambda n: jnp.maximum(n * per_step - 1, 0)
    zcur = lambda c: pl.BlockSpec(blk, lambda r, n: (n, r * 3 + c))
    znext = lambda c: pl.BlockSpec(one, lambda r, n: (nxt_idx(n), r * 3 + c))
    zprev = lambda c: pl.BlockSpec(one, lambda r, n: (prv_idx(n), r * 3 + c))
    cur = pl.BlockSpec(blk, lambda r, n: (n, r))
    nxt = pl.BlockSpec(one, lambda r, n: (nxt_idx(n), r))
    return pl.pallas_call(
        body, name=name, grid=(dil, n_steps),
        in_specs=[zcur(0), znext(0), zcur(1), zprev(1), zcur(2), zprev(2), cur, nxt, cur, nxt, cur, nxt],
        out_specs=[cur, cur, cur],
        out_shape=[jax.ShapeDtypeStruct((rows, dil * GROUP_W), WIRE_DTYPE)] * 3,
        scratch_shapes=[pltpu.VMEM((N_HEADS * B, 2 * B), F32), pltpu.VMEM((N_HEADS * B, B), F32)],
        compiler_params=_params(("parallel", "arbitrary")),
    )(qkv, qkv, qkv, qkv, qkv, qkv, do, do, lse, lse, delta, delta)


def _inproj_bwd(x, g, dxn, dz_abc, dqkv, dz_g, w_t, name):
    S, D = x.shape
    N = w_t.shape[0]
    tm = TM_MM
    n_abc = N_ABC * GROUP_W

    def body(x_ref, g_ref, dxn_ref, dabc_ref, q1, k1, v1, q2, k2, v2, q3, k3, v3, dg_ref, w_ref,
             dx_ref, dz_ref, h_ref, dgn_ref, s4_ref, s16_ref, tmp_ref):
        i = pl.program_id(0)

        @pl.when(i == 0)
        def _():
            dgn_ref[...] = jnp.zeros_like(dgn_ref)

        dz_ref[:, 0:n_abc] = dabc_ref[...].astype(MXU_DTYPE)
        for j, parts in enumerate(((q1, q2, q3), (k1, k2, k3), (v1, v2, v3))):
            c0 = n_abc + j * GROUP_W
            _interleave(parts[1], s4_ref, ATTN_DILATIONS[1])
            _interleave(parts[2], s16_ref, ATTN_DILATIONS[2], tmp_ref)
            dz_ref[:, c0:c0 + GROUP_W] = (parts[0][...] + _get(s4_ref) + _get(s16_ref)).astype(MXU_DTYPE)
        dz_ref[:, n_abc + 3 * GROUP_W:] = dg_ref[...].astype(MXU_DTYPE)
        dh = jnp.dot(dz_ref[...], w_ref[...], preferred_element_type=F32)
        xv = x_ref[...]
        r = lax.rsqrt(jnp.mean(xv * xv, axis=-1, keepdims=True) + NORM_EPS)
        xn = xv * r
        gv = g_ref[...]
        h_ref[...] = (xn * gv).astype(MXU_DTYPE)
        dgn_ref[...] += _colsum(dh * xn)
        dn = dh * gv
        dx_ref[...] = dxn_ref[...] + r * (dn - xn * jnp.mean(dn * xn, axis=-1, keepdims=True))

    row = lambda w: pl.BlockSpec((tm, w), lambda i: (i, 0))
    flat = [t for p in dqkv for t in p]
    dil_specs = [_dilated_spec(tm, GROUP_W, dil) for dil in ATTN_DILATIONS for _ in range(3)]
    return pl.pallas_call(
        body, name=name, grid=(S // tm,),
        in_specs=[row(D), pl.BlockSpec((1, D), lambda i: (0, 0)), row(D), row(n_abc)] + dil_specs
                 + [row(GROUP_W), pl.BlockSpec((N, D), lambda i: (0, 0), pipeline_mode=pl.Buffered(1))],
        out_specs=[row(D), row(N), row(D), pl.BlockSpec((1, D), lambda i: (0, 0))],
        out_shape=[jax.ShapeDtypeStruct((S, D), F32), jax.ShapeDtypeStruct((S, N), MXU_DTYPE),
                   jax.ShapeDtypeStruct((S, D), MXU_DTYPE), jax.ShapeDtypeStruct((1, D), F32)],
        scratch_shapes=[_lane_scratch(tm, GROUP_W)] * 3,
        compiler_params=_params(("arbitrary",)),
    )(x, g, dxn, dz_abc, *flat, dz_g, w_t)


def _inproj_wgrad(h, dz, name):
    S, D = h.shape
    N = dz.shape[1]
    tm = TM_WGRAD
    nj = 2
    cw = N // nj
    per = N_DEV // nj
    n_loc = N // N_DEV

    def body(h_ref, dz_ref, dw_ref, acc_ref):
        i = pl.program_id(1)

        @pl.when(i == 0)
        def _():
            acc_ref[...] = jnp.zeros_like(acc_ref)

        acc_ref[...] += _mm_tn(dz_ref[...], h_ref[...])

        @pl.when(i == S // tm - 1)
        def _():
            for b in range(per):
                dw_ref[b] = acc_ref[b * n_loc:(b + 1) * n_loc, :].astype(dw_ref.dtype)

    return pl.pallas_call(
        body, name=name, grid=(nj, S // tm),
        in_specs=[pl.BlockSpec((tm, D), lambda j, i: (i, 0)), pl.BlockSpec((tm, cw), lambda j, i: (i, j))],
        out_specs=pl.BlockSpec((per, n_loc, D), lambda j, i: (j, 0, 0)),
        out_shape=jax.ShapeDtypeStruct((N_DEV, n_loc, D), WIRE_DTYPE),
        scratch_shapes=[pltpu.VMEM((cw, D), F32)],
        compiler_params=_params(("parallel", "arbitrary")),
    )(h, dz)


def _my_place():
    return lax.axis_index("x"), lax.axis_index("y"), lax.axis_index("c")


def _peer(x, y, c, k):
    px = 1 - x if k & 4 else x
    py = 1 - y if k & 2 else y
    pc = 1 - c if k & 1 else c
    return (px, py, pc), 4 * px + 2 * py + pc


HBM_SPEC = pl.BlockSpec(memory_space=pltpu.HBM)
SEM_SPEC = pl.BlockSpec(memory_space=pltpu.SEMAPHORE)
SPLIT_EFFECT = pltpu.SideEffectType.DATAFLOW_SIDE_EFFECTING
N_PEERS = N_DEV - 1


def _exchange_copies(srcs, lands, send_sems, recv_sems, whole, arrival):
    x, y, c = _my_place()
    me = 4 * x + 2 * y + c
    copies = []
    for t in range(len(srcs)):
        for k in range(1, N_DEV):
            peer, pidx = _peer(x, y, c, k)
            copies.append(pltpu.make_async_remote_copy(
                src_ref=srcs[t] if whole[t] else srcs[t].at[pidx],
                dst_ref=lands[t].at[pidx if arrival else me], send_sem=send_sems.at[t * N_PEERS + k - 1],
                recv_sem=recv_sems.at[t * N_PEERS + k - 1], device_id=peer, device_id_type=MESH))
    return copies


def _exchange_start(groups, name, after=None):
    sizes = [len(g) for g in groups]
    whole = [w for g in groups for _, w in g]
    srcs = [pltpu.with_memory_space_constraint(a, pltpu.HBM) for g in groups for a, _ in g]
    lands = [pltpu.with_memory_space_constraint(lax.empty(((N_DEV,) + a.shape) if w else a.shape, a.dtype), pltpu.HBM)
             for a, w in zip(srcs, whole)]
    n = len(srcs)
    n_g = len(groups)
    extra = [] if after is None else [after]
    n_in = 2 * n + len(extra)

    def body(*refs):
        src_refs, land_refs = refs[:n], refs[n:2 * n]
        sem_refs = refs[n_in + 2 * n:n_in + 2 * n + 2 * n_g]
        token = refs[-1]
        off = 0
        for gi, sz in enumerate(sizes):
            for send in _exchange_copies(src_refs[off:off + sz], land_refs[off:off + sz],
                                         sem_refs[2 * gi], sem_refs[2 * gi + 1], whole[off:off + sz], False):
                send.start()
            off += sz
        token[...] = jnp.zeros_like(token)

    sem_shapes = [pltpu.SemaphoreType.DMA((sz * N_PEERS,)) for sz in sizes for _ in range(2)]
    outs = pl.pallas_call(
        body, name=name,
        in_specs=[HBM_SPEC] * (2 * n) + [pl.BlockSpec(memory_space=pl.ANY)] * len(extra),
        out_specs=[HBM_SPEC] * (2 * n) + [SEM_SPEC] * (2 * n_g) + [pl.BlockSpec(memory_space=pltpu.VMEM)],
        out_shape=[pltpu.HBM(a.shape, a.dtype) for a in srcs + lands] + sem_shapes
                  + [jax.ShapeDtypeStruct((SUBLANES, LANES), F32)],
        input_output_aliases={i: i for i in range(2 * n)},
        compiler_params=pltpu.CompilerParams(has_side_effects=SPLIT_EFFECT),
    )(*srcs, *lands, *extra)
    handles, off = [], 0
    for gi, sz in enumerate(sizes):
        handles.append((outs[2 * n + 2 * gi], outs[2 * n + 2 * gi + 1], outs[off:off + sz], outs[n + off:n + off + sz],
                        whole[off:off + sz]))
        off += sz
    return handles, outs[-1]


def _exchange_wait(handle, after, name):
    send_sems, recv_sems, srcs, lands, whole = handle
    n = len(srcs)

    def body(*refs):
        src_refs, land_refs = refs[:n], refs[n:2 * n]
        for send in _exchange_copies(src_refs, land_refs, refs[2 * n], refs[2 * n + 1], whole, False):
            send.wait_send()
        for arrival in _exchange_copies(src_refs, land_refs, refs[2 * n], refs[2 * n + 1], whole, True):
            arrival.wait_recv()

    outs = pl.pallas_call(
        body, name=name,
        in_specs=[HBM_SPEC] * (2 * n) + [SEM_SPEC, SEM_SPEC, pl.BlockSpec(memory_space=pl.ANY)],
        out_specs=[HBM_SPEC] * (2 * n),
        out_shape=[pltpu.HBM(a.shape, a.dtype) for a in list(srcs) + list(lands)],
        input_output_aliases={i: i for i in range(2 * n)},
        compiler_params=pltpu.CompilerParams(has_side_effects=SPLIT_EFFECT),
    )(*srcs, *lands, send_sems, recv_sems, after)
    x, y, c = _my_place()
    me = 4 * x + 2 * y + c
    own = [s[None] if w else lax.dynamic_slice_in_dim(s, me, 1, axis=0) for s, w in zip(outs[:n], whole)]
    return [lax.dynamic_update_slice_in_dim(ld, o, me, axis=0) for ld, o in zip(outs[n:], own)]


def _sum_slots(parts, name):
    n = len(parts)

    def body(*refs):
        for p_ref, o_ref in zip(refs[:n], refs[n:]):
            acc = p_ref[0]
            for j in range(1, N_DEV):
                acc = acc + p_ref[j]
            o_ref[...] = acc

    vm = pl.BlockSpec(memory_space=pltpu.VMEM)
    return pl.pallas_call(
        body, name=name, in_specs=[vm] * n, out_specs=[vm] * n,
        out_shape=[jax.ShapeDtypeStruct(p.shape[1:], F32) for p in parts],
        compiler_params=pltpu.CompilerParams(vmem_limit_bytes=VMEM_LIMIT),
    )(*parts)


def _adamw_math(w, g, m, v):
    m = ADAM_B1 * m + (1.0 - ADAM_B1) * g
    v = ADAM_B2 * v + (1.0 - ADAM_B2) * (g * g)
    m_hat = m / (1.0 - ADAM_B1 ** ADAM_STEP)
    v_hat = v / (1.0 - ADAM_B2 ** ADAM_STEP)
    delta = -ADAM_LR * (m_hat / (jnp.sqrt(v_hat) + ADAM_EPS) + ADAM_WD * w)
    return delta, m, v


def _adamw_summed(parts, w, m, v, tr, name):
    depth, R, C = w.shape

    def body(*refs):
        p_refs = refs[:depth]
        w_ref, m_ref, v_ref, g_ref, d_ref, nm_ref, nv_ref = refs[depth:]
        lay = pl.program_id(0)
        for l in range(depth):
            @pl.when(lay == l)
            def _(p_ref=p_refs[l]):
                g = p_ref[0].astype(F32)
                for j in range(1, N_DEV):
                    g = g + p_ref[j].astype(F32)
                g_ref[0] = g
        d_ref[0], nm_ref[0], nv_ref[0] = _adamw_math(w_ref[0], g_ref[0], m_ref[0], v_ref[0])

    part_spec = lambda l: pl.BlockSpec((N_DEV, tr, C), lambda lay, i: (0, jnp.where(lay == l, i, 0), 0))
    row = pl.BlockSpec((1, tr, C), lambda lay, i: (lay, i, 0))
    return pl.pallas_call(
        body, name=name, grid=(depth, R // tr),
        in_specs=[part_spec(l) for l in range(depth)] + [row, row, row],
        out_specs=[row] * 4, out_shape=[jax.ShapeDtypeStruct((depth, R, C), F32)] * 4,
        compiler_params=_params(("arbitrary", "arbitrary")),
    )(*parts, w, m, v)


def _adamw_small(w, g, m, v, name):
    def body(w_ref, g_ref, m_ref, v_ref, d_ref, nm_ref, nv_ref):
        d_ref[...], nm_ref[...], nv_ref[...] = _adamw_math(w_ref[...], g_ref[...], m_ref[...], v_ref[...])

    vm = pl.BlockSpec(memory_space=pltpu.VMEM)
    return pl.pallas_call(
        body, name=name, in_specs=[vm] * 4, out_specs=[vm] * 3,
        out_shape=[jax.ShapeDtypeStruct(w.shape, F32)] * 3,
        compiler_params=pltpu.CompilerParams(vmem_limit_bytes=VMEM_LIMIT),
    )(w, g, m, v)


def _pack(arrays):
    flat = jnp.concatenate([a.reshape(-1) for a in arrays])
    pad = (-flat.shape[0]) % (SUBLANES * LANES)
    return jnp.pad(flat, (0, pad)).reshape(-1, LANES)


def _unpack(buf, like):
    flat = buf.reshape(-1)
    out, off = [], 0
    for a in like:
        out.append(flat[off:off + a.size].reshape(a.shape))
        off += a.size
    return out


def _block_diag(w):
    eye = jnp.eye(N_HEADS, dtype=w.dtype)
    return jnp.einsum('hij,hk->hikj', w, eye).reshape(GROUP_W, GROUP_W)


def _diag_blocks(w):
    return jnp.einsum('hihj->hij', w.reshape(N_HEADS, HEAD_DIM, N_HEADS, HEAD_DIM))


def _pad_rows(a):
    return jnp.pad(a, ((0, SUBLANES - a.shape[0]), (0, 0)))


def _mixer_params(l, conv_a_w, conv_r_w, conv_r_b, lru_wa, lru_ba, lru_wx, lru_bx, lru_lambda, gmlp_norm_g,
                  gmlp_ws, gmlp_bs):
    tril = jnp.tril(jnp.ones((GMLP_CHUNK, GMLP_CHUNK), dtype=bool))
    vec = jnp.stack([conv_r_b[l], lru_ba[l], lru_bx[l], lru_lambda[l], gmlp_norm_g[l]])
    return {
        "wA": _pad_rows(conv_a_w[l]), "wR": _pad_rows(conv_r_w[l]), "vec": _pad_rows(vec),
        "wa": _block_diag(lru_wa[l]).astype(MXU_DTYPE), "wx": _block_diag(lru_wx[l]).astype(MXU_DTYPE),
        "ws": jnp.where(tril[None], gmlp_ws[l], 0.0).astype(MXU_DTYPE),
        "bs": jnp.repeat(jnp.transpose(gmlp_bs[l]), HEAD_DIM, axis=1),
    }


MIXER_NAMES = ("conv_a_w", "conv_r_w", "conv_r_b", "lru_wa", "lru_ba", "lru_wx", "lru_bx", "lru_lambda",
               "gmlp_norm_g", "gmlp_ws", "gmlp_bs")
SMALL_NAMES = ("norm_g",) + MIXER_NAMES + ("final_g",)


def _local_step(x, loss_target, norm_g, get_w_in, get_w_out, emit_early, emit_late, conv_a_w, conv_r_w, conv_r_b,
                lru_wa, lru_ba, lru_wx, lru_bx, lru_lambda, gmlp_norm_g, gmlp_ws, gmlp_bs, final_g):
    depth = norm_g.shape[0]
    D = x.shape[1]
    small = (conv_a_w, conv_r_w, conv_r_b, lru_wa, lru_ba, lru_wx, lru_bx, lru_lambda, gmlp_norm_g, gmlp_ws, gmlp_bs)
    saved = []
    for l in range(depth):
        mp = _mixer_params(l, *small)
        w_in_l = get_w_in(l, x)
        z, z_g, *qkv, y_abc, hs = _inproj_mix_fwd(x, norm_g[l].reshape(1, D), w_in_l, mp, f"inproj_mix_fwd_{l}")
        attn =[_attn_fwd(qkv[p], dil, f"attn_fwd_d{dil}_{l}") for p, dil in enumerate(ATTN_DILATIONS)]
        w_out_l = get_w_out(l, y_abc)
        x_new, y, o, *lse = _outproj(x, z_g, y_abc, attn, w_out_l, f"outproj_{l}")
        saved.append((x, z, z_g, qkv, hs, y, o, lse, mp, w_in_l, w_out_l))
        x = x_new
    dx, loss, d_final_g = _loss_head(x, final_g.reshape(1, D), loss_target, "loss_head")
    token = None
    for l in reversed(range(depth)):
        x_l, z, z_g, qkv, hs, y, o, lse, mp, w_in_l, w_out_l = saved[l]
        if token is not None:
            mp = dict(mp, vec=mp["vec"] + token[0, 0])
        (dw_out, dz_abc, dz_g, do1, do4, do16, dl1, dl4, dl16, dwA, dwR, dvec, dwa, dwx, dws, dbs) = _outproj_mix_bwd(
            dx, y, w_out_l, z, z_g, hs, o, mp, f"outproj_mix_bwd_{l}")
        token = emit_early(l, dw_out, [
            dwA[:conv_a_w.shape[1]], dwR[:conv_r_w.shape[1]], dvec[0], _diag_blocks(dwa), dvec[1], _diag_blocks(dwx),
            dvec[2], dvec[3], dvec[4], dws, jnp.transpose(dbs[:, :N_HEADS])])
        g_row = norm_g[l].reshape(1, D)
        if token is not None:
            g_row = g_row + token[0, 0]
        dqkv = [_attn_bwd(qkv[p], do, lse[p], dl, dil, f"attn_bwd_d{dil}_{l}")
                for p, (dil, do, dl) in enumerate(zip(ATTN_DILATIONS, (do1, do4, do16), (dl1, dl4, dl16)))]
        dx, dz, h, dng = _inproj_bwd(x_l, g_row, dx, dz_abc, dqkv, dz_g, w_in_l, f"inproj_bwd_{l}")
        dw_in = _inproj_wgrad(h, dz, f"inproj_wgrad_{l}")
        token = emit_late(l, dw_in, [dng[0]] + ([d_final_g[0]] if l == depth - 1 else []))
    return loss[0, 0], dx
WEIGHT_NAMES = ("norm_g", "w_in", "conv_a_w", "conv_r_w", "conv_r_b", "lru_wa", "lru_ba", "lru_wx", "lru_bx",
                "lru_lambda", "gmlp_norm_g", "gmlp_ws", "gmlp_bs", "w_out", "final_g")


def kernel(x, norm_g, w_in, conv_a_w, conv_r_w, conv_r_b, lru_wa, lru_ba, lru_wx, lru_bx, lru_lambda, gmlp_norm_g, gmlp_ws, gmlp_bs, w_out, final_g, loss_target, m_norm_g, m_w_in, m_conv_a_w, m_conv_r_w, m_conv_r_b, m_lru_wa, m_lru_ba, m_lru_wx, m_lru_bx, m_lru_lambda, m_gmlp_norm_g, m_gmlp_ws, m_gmlp_bs, m_w_out, m_final_g, v_norm_g, v_w_in, v_conv_a_w, v_conv_r_w, v_conv_r_b, v_lru_wa, v_lru_ba, v_lru_wx, v_lru_bx, v_lru_lambda, v_gmlp_norm_g, v_gmlp_ws, v_gmlp_bs, v_w_out, v_final_g):
    w = dict(norm_g=norm_g, w_in=w_in, conv_a_w=conv_a_w, conv_r_w=conv_r_w, conv_r_b=conv_r_b, lru_wa=lru_wa,
             lru_ba=lru_ba, lru_wx=lru_wx, lru_bx=lru_bx, lru_lambda=lru_lambda, gmlp_norm_g=gmlp_norm_g,
             gmlp_ws=gmlp_ws, gmlp_bs=gmlp_bs, w_out=w_out, final_g=final_g)
    m = dict(norm_g=m_norm_g, w_in=m_w_in, conv_a_w=m_conv_a_w, conv_r_w=m_conv_r_w, conv_r_b=m_conv_r_b,
             lru_wa=m_lru_wa, lru_ba=m_lru_ba, lru_wx=m_lru_wx, lru_bx=m_lru_bx, lru_lambda=m_lru_lambda,
             gmlp_norm_g=m_gmlp_norm_g, gmlp_ws=m_gmlp_ws, gmlp_bs=m_gmlp_bs, w_out=m_w_out, final_g=m_final_g)
    v = dict(norm_g=v_norm_g, w_in=v_w_in, conv_a_w=v_conv_a_w, conv_r_w=v_conv_r_w, conv_r_b=v_conv_r_b,
             lru_wa=v_lru_wa, lru_ba=v_lru_ba, lru_wx=v_lru_wx, lru_bx=v_lru_bx, lru_lambda=v_lru_lambda,
             gmlp_norm_g=v_gmlp_norm_g, gmlp_ws=v_gmlp_ws, gmlp_bs=v_gmlp_bs, w_out=v_w_out, final_g=v_final_g)
    depth, D, n_loc = w_in.shape
    e_loc = w_out.shape[1]
    cx, cy, cc = _my_place()
    me = 4 * cx + 2 * cy + cc

    transposed = lambda a: jnp.transpose(a, (0, 2, 1))
    w_in_t, m_w_in_t, v_w_in_t = transposed(w_in), transposed(m_w_in), transposed(v_w_in)
    w_in_w, w_out_w = w_in_t.astype(MXU_DTYPE), w_out.astype(MXU_DTYPE)
    c_loc = conv_a_w.shape[2]
    taps = (conv_a_w, conv_r_w)
    first, _ = _exchange_start([[(w_in_w[0], True), (_pack(taps), True)], [(w_out_w[0], True)]], "gather_start_first")
    full_in = lambda g: g.reshape(N_DEV * n_loc, D)
    full_out = lambda g: g.reshape(N_DEV * e_loc, D)

    g_in0, g_taps = _exchange_wait(first[0], x, "gather_wait_in_0")
    groups = [[(w_in_w[l], True), (w_out_w[l], True)] for l in range(1, depth)]
    gathers, rest_token = _exchange_start(groups, "gather_start_rest", after=g_taps)
    g_taps = g_taps.reshape(N_DEV, -1) + rest_token[0, 0]
    conv_full, off = [], 0
    for a in taps:
        part = g_taps[:, off:off + a.size].reshape((N_DEV,) + a.shape)
        conv_full.append(jnp.transpose(part, (1, 2, 0, 3)).reshape(a.shape[:2] + (N_DEV * c_loc,)))
        off += a.size
    conv_a_full, conv_r_full = conv_full
    later = {}

    def get_w_in(l, after):
        if l == 0:
            return full_in(g_in0)
        g_in, later[l] = _exchange_wait(gathers[l - 1], after, f"gather_wait_{l}")
        return full_in(g_in)

    def get_w_out(l, after):
        if l == 0:
            return full_out(_exchange_wait(first[1], after, "gather_wait_out_0")[0])
        return full_out(later[l])

    early, late, last_token = {}, {}, [None]

    def emit_early(l, dw_out, mixer_grads):
        handles, token = _exchange_start(
            [[(dw_out.reshape(N_DEV, e_loc, D), False), (_pack(mixer_grads), True)]], f"early_start_{l}")
        early[l] = (handles[0], mixer_grads)
        return token

    def emit_late(l, dw_in, norm_grads):
        handles, token = _exchange_start([[(_pack(norm_grads), True)], [(dw_in, False)]], f"late_start_{l}")
        late[l] = (handles[0], handles[1], norm_grads)
        last_token[0] = token
        return token

    loss, grad_x = _local_step(
        x[0], loss_target[0], norm_g, get_w_in, get_w_out, emit_early, emit_late, conv_a_full, conv_r_full, conv_r_b,
        lru_wa, lru_ba, lru_wx, lru_bx, lru_lambda, gmlp_norm_g, gmlp_ws, gmlp_bs, final_g)
    loss = lax.psum(loss, ("x", "y", "c"))

    r_in, r_out, small_parts = {}, {}, []
    for l in reversed(range(depth)):
        r_out[l], r_mix = _exchange_wait(early[l][0], last_token[0], f"early_wait_{l}")
        (r_norm,) = _exchange_wait(late[l][0], last_token[0], f"late_wait_norm_{l}")
        small_parts += [r_mix, r_norm]
        if l > 0:
            (r_in[l],) = _exchange_wait(late[l][1], last_token[0], f"late_wait_{l}")
    big = {"w_out": _adamw_summed([r_out[l] for l in range(depth)], w_out, m_w_out, v_w_out, 128, "adamw_w_out")}

    sums = _sum_slots(small_parts, "sum_small_grads")
    by_layer = {}
    for i, l in enumerate(reversed(range(depth))):
        mix = _unpack(sums[2 * i], early[l][1])
        nrm = _unpack(sums[2 * i + 1], late[l][2])
        by_layer[l] = dict(zip(MIXER_NAMES, mix), norm_g=nrm[0])
        if l == depth - 1:
            g_final = nrm[1]
    g_small = {k: jnp.stack([by_layer[l][k] for l in range(depth)]) for k in ("norm_g",) + MIXER_NAMES}
    g_small["final_g"] = g_final
    for k in ("conv_a_w", "conv_r_w"):
        g_small[k] = lax.dynamic_slice_in_dim(g_small[k], me * c_loc, c_loc, axis=2)
    packs = [_pack([d[k] for k in SMALL_NAMES]) for d in (w, g_small, m, v)]
    res = _adamw_small(*packs, "adamw_small")
    like = [w[k] for k in SMALL_NAMES]
    d_s, m_s, v_s = (dict(zip(SMALL_NAMES, _unpack(r, like))) for r in res)

    (r_in[0],) = _exchange_wait(late[0][1], res[0], "late_wait_0")
    big["w_in"] = [transposed(a) for a in _adamw_summed(
        [r_in[l] for l in range(depth)], w_in_t, m_w_in_t, v_w_in_t, n_loc // 2, "adamw_w_in")]

    grad, delta, new_m, new_v = {}, {}, {}, {}
    for k in WEIGHT_NAMES:
        if k in big:
            grad[k], delta[k], new_m[k], new_v[k] = big[k]
        else:
            grad[k], delta[k], new_m[k], new_v[k] = g_small[k], d_s[k], m_s[k], v_s[k]
    return (loss, grad_x[None], *[grad[k] for k in WEIGHT_NAMES], *[delta[k] for k in WEIGHT_NAMES],
            *[new_m[k] for k in WEIGHT_NAMES], *[new_v[k] for k in WEIGHT_NAMES])
```

```python
import functools
import math

import jax
import jax.numpy as jnp
from jax import lax
from jax.experimental import pallas as pl
from jax.experimental.pallas import tpu as pltpu

F32 = jnp.float32
MXU_DTYPE = jnp.bfloat16
WIRE_DTYPE = jnp.bfloat16
MESH = pl.DeviceIdType.MESH

N_DEV = 8
GROUP_W = 256
N_HEADS = 4
HEAD_DIM = 64
N_CHUNKS = 13
N_ABC = 9
GMLP_CHUNK = 128
ATTN_BLOCK = 128
ATTN_FWD_BLOCKS_PER_STEP = 16
ATTN_BWD_BLOCKS_PER_STEP = 8
ATTN_DILATIONS = (1, 4, 16)
NORM_EPS = 1e-6
RG_C = 8.0
SUBLANES = 8
LANES = 128
VMEM_LIMIT = 56 * 1024 * 1024

ADAM_LR = 0.001
ADAM_B1 = 0.9
ADAM_B2 = 0.999
ADAM_EPS = 1e-08
ADAM_WD = 0.01
ADAM_STEP = 10

TM_MIX = 512
TM_MM = 512
TM_WGRAD = 1024


def _params(sem, vmem=VMEM_LIMIT):
    return pltpu.CompilerParams(dimension_semantics=sem, vmem_limit_bytes=vmem)


def _mm(a, b):
    return jnp.dot(a.astype(MXU_DTYPE), b.astype(MXU_DTYPE), preferred_element_type=F32)


def _mm_tn(a, b):
    return lax.dot_general(a.astype(MXU_DTYPE), b.astype(MXU_DTYPE), (((0,), (0,)), ((), ())),
                           preferred_element_type=F32)


def _mm_nt(a, b):
    return lax.dot_general(a.astype(MXU_DTYPE), b.astype(MXU_DTYPE), (((1,), (1,)), ((), ())),
                           preferred_element_type=F32)


def _sigmoid(x):
    return 0.5 * jnp.tanh(0.5 * x) + 0.5


def _sigmoid_small_exact(x):
    return 1.0 / (1.0 + jnp.exp(-x))


def _silu_and_grad(x):
    s = _sigmoid(x)
    return x * s, s * (1.0 + x * (1.0 - s))


_GELU_K = math.sqrt(2.0 / math.pi)
_GELU_C = 0.044715


def _gelu_and_grad(x):
    x2 = x * x
    t = jnp.tanh(_GELU_K * (x + _GELU_C * x * x2))
    val = 0.5 * x * (1.0 + t)
    grad = 0.5 * (1.0 + t) + 0.5 * x * (1.0 - t * t) * (_GELU_K * (1.0 + 3.0 * _GELU_C * x2))
    return val, grad


def _gelu(x):
    return 0.5 * x * (1.0 + jnp.tanh(_GELU_K * (x + _GELU_C * x * x * x)))


def _expm1_nonpos(u):
    poly = 1.0 / math.factorial(9)
    for k in range(8, 0, -1):
        poly = poly * u + 1.0 / math.factorial(k)
    return jnp.where(u > -0.25, poly * u, jnp.exp(u) - 1.0)


def _softplus(x):
    return jnp.maximum(x, 0.0) + jnp.log(1.0 + jnp.exp(-jnp.abs(x)))


def _shift_down(t, halo, k):
    rolled = pltpu.roll(t, k, 0)
    hr = pltpu.roll(halo, k, 0)
    row = lax.broadcasted_iota(jnp.int32, halo.shape, 0)
    first = jnp.where(row < k, hr, rolled[0:SUBLANES])
    return jnp.concatenate([first, rolled[SUBLANES:]], axis=0)


def _shift_up(t, nxt, k):
    tm = t.shape[0]
    rolled = pltpu.roll(t, tm - k, 0)
    nr = pltpu.roll(nxt, SUBLANES - k, 0)
    row = lax.broadcasted_iota(jnp.int32, nxt.shape, 0)
    last = jnp.where(row >= SUBLANES - k, nr, rolled[tm - SUBLANES:tm])
    return jnp.concatenate([rolled[:tm - SUBLANES], last], axis=0)


def _scan_fwd(a, b):
    tm = a.shape[0]
    row = lax.broadcasted_iota(jnp.int32, a.shape, 0)
    s = 1
    while s < tm:
        a_s = pltpu.roll(a, s, 0)
        b_s = pltpu.roll(b, s, 0)
        m = row >= s
        b = jnp.where(m, a * b_s + b, b)
        a = jnp.where(m, a * a_s, a)
        s *= 2
    return a, b


def _scan_rev(a, g):
    tm = a.shape[0]
    row = lax.broadcasted_iota(jnp.int32, a.shape, 0)
    s = 1
    while s < tm:
        a_s = pltpu.roll(a, tm - s, 0)
        g_s = pltpu.roll(g, tm - s, 0)
        m = row < tm - s
        g = jnp.where(m, g + a * g_s, g)
        a = jnp.where(m, a * a_s, a)
        s *= 2
    return g


def _group_rows(scr_ref, row, n_groups):
    return jnp.concatenate([scr_ref[pl.ds(c, 1), pl.ds(row, n_groups, stride=SUBLANES), :][0]
                            for c in range(scr_ref.shape[0])], axis=1)


def _spread_rows(rows_ref, n_groups, w):
    return jnp.concatenate([jnp.broadcast_to(rows_ref[g:g + 1, :], (SUBLANES, w)) for g in range(n_groups)], axis=0)


def _scan_groups(a, b, reverse):
    tm, w = a.shape
    shape3 = (tm // SUBLANES, SUBLANES, w)
    a3, b3 = a.reshape(shape3), b.reshape(shape3)
    sub = lax.broadcasted_iota(jnp.int32, shape3, 1)
    s = 1
    while s < SUBLANES:
        shift = SUBLANES - s if reverse else s
        a_s = pltpu.roll(a3, shift, 1)
        b_s = pltpu.roll(b3, shift, 1)
        m = (sub < SUBLANES - s) if reverse else (sub >= s)
        b3 = jnp.where(m, a3 * b_s + b3, b3)
        a3 = jnp.where(m, a3 * a_s, a3)
        s *= 2
    return a3.reshape(tm, w), b3.reshape(tm, w)


def _scan_fwd_tile(a, b, h_in, sa_ref, sb_ref, sc_ref):
    tm, w = a.shape
    n_groups = tm // SUBLANES
    a_loc, b_loc = _scan_groups(a, b, False)
    _put(sa_ref, a_loc)
    _put(sb_ref, b_loc)
    a_end, b_end = _scan_fwd(_group_rows(sa_ref, SUBLANES - 1, n_groups), _group_rows(sb_ref, SUBLANES - 1, n_groups))
    h_end = b_end + a_end * h_in
    sc_ref[...] = _shift_down(h_end, jnp.broadcast_to(h_in, (SUBLANES, w)), 1)
    return b_loc + a_loc * _spread_rows(sc_ref, n_groups, w), h_end


def _scan_rev_tile(a, g, sa_ref, sb_ref, sc_ref):
    tm, w = a.shape
    n_groups = tm // SUBLANES
    a_loc, g_loc = _scan_groups(a, g, True)
    _put(sa_ref, a_loc)
    _put(sb_ref, g_loc)
    d_first = _scan_rev(_group_rows(sa_ref, 0, n_groups), _group_rows(sb_ref, 0, n_groups))
    sc_ref[...] = _shift_up(d_first, jnp.zeros((SUBLANES, w), F32), 1)
    return g_loc + a_loc * _spread_rows(sc_ref, n_groups, w)


def _lane_scratch(tm, w):
    return pltpu.VMEM((w // LANES, tm, LANES), F32)


def _put(scr_ref, val):
    for c in range(scr_ref.shape[0]):
        scr_ref[c] = val[:, c * LANES:(c + 1) * LANES].astype(F32)


def _get(scr_ref):
    return jnp.concatenate([scr_ref[c] for c in range(scr_ref.shape[0])], axis=1)


MAX_ROW_STRIDE = 4


def _strided_rows(c, start, n, stride):
    return (pl.ds(c, 1), pl.ds(start, n, stride=stride), slice(None))


def _deinterleave(src_ref, dst_ref, dil, tmp_ref=None):
    nc, tm, _ = src_ref.shape
    w = nc * LANES
    s1 = min(dil, MAX_ROW_STRIDE)
    s2 = dil // s1
    if s2 > 1:
        for r0 in range(s1):
            for c in range(nc):
                tmp_ref[c, r0 * (tm // s1):(r0 + 1) * (tm // s1), :] = src_ref[_strided_rows(c, r0, tm // s1, s1)][0]
    for r in range(dil):
        r1, r0 = divmod(r, s1)
        for c in range(nc):
            if dil == 1:
                piece = src_ref[c]
            elif s2 == 1:
                piece = src_ref[_strided_rows(c, r, tm // dil, dil)][0]
            else:
                piece = tmp_ref[_strided_rows(c, r0 * (tm // s1) + r1, tm // dil, s2)][0]
            dst_ref[:, r * w + c * LANES:r * w + (c + 1) * LANES] = piece.astype(dst_ref.dtype)


def _interleave(src_ref, dst_ref, dil, tmp_ref=None):
    nc, tm, _ = dst_ref.shape
    w = nc * LANES
    s1 = min(dil, MAX_ROW_STRIDE)
    s2 = dil // s1
    for r in range(dil):
        r1, r0 = divmod(r, s1)
        for c in range(nc):
            piece = src_ref[:, r * w + c * LANES:r * w + (c + 1) * LANES].astype(F32)[None]
            if s2 == 1:
                dst_ref[_strided_rows(c, r, tm // dil, dil)] = piece
            else:
                tmp_ref[_strided_rows(c, r0 * (tm // s1) + r1, tm // dil, s2)] = piece
    if s2 > 1:
        for r0 in range(s1):
            for c in range(nc):
                dst_ref[_strided_rows(c, r0, tm // s1, s1)] = (
                    tmp_ref[c, r0 * (tm // s1):(r0 + 1) * (tm // s1), :][None])


def _dilated_spec(tm, w, dil, index=lambda i: i):
    return pl.BlockSpec((tm // dil, dil * w), lambda i: (index(i), 0))


def _dilated_shape(S, w, dil, dtype):
    return jax.ShapeDtypeStruct((S // dil, dil * w), dtype)


def _head_masks(shape):
    lane = lax.broadcasted_iota(jnp.int32, shape, 1)
    return [(lane >= h * HEAD_DIM) & (lane < (h + 1) * HEAD_DIM) for h in range(N_HEADS)]


def _colsum(v):
    return jnp.sum(v, axis=0, keepdims=True)


def _conv_a(z_of, halo_of, w_ref):
    p = z_of(2) * z_of(0)
    p_h = halo_of(2) * halo_of(0)
    cv = w_ref[2:3, :] * p + w_ref[1:2, :] * _shift_down(p, p_h, 1) + w_ref[0:1, :] * _shift_down(p, p_h, 2)
    return p, p_h, cv


def _lru_gates(z_of, halo_of, wr_ref, vec_ref, wa_ref, wx_ref, saved=None):
    rx = z_of(4)
    rx_h = halo_of(4)
    sh = [rx, _shift_down(rx, rx_h, 1), _shift_down(rx, rx_h, 2), _shift_down(rx, rx_h, 3)]
    xc = (wr_ref[3:4, :] * sh[0] + wr_ref[2:3, :] * sh[1] + wr_ref[1:2, :] * sh[2]
          + wr_ref[0:1, :] * sh[3] + vec_ref[0:1, :])
    ga = _sigmoid_small_exact(jnp.dot(xc.astype(MXU_DTYPE), wa_ref[...], preferred_element_type=F32) + vec_ref[1:2, :])
    gi = _sigmoid(jnp.dot(xc.astype(MXU_DTYPE), wx_ref[...], preferred_element_type=F32) + vec_ref[2:3, :])
    sp = _softplus(-vec_ref[3:4, :])
    if saved is not None:
        return (xc, sh, ga, gi) + tuple(saved) + (sp,)
    log_a = (-RG_C * ga) * sp
    a = jnp.exp(log_a)
    mult = jnp.sqrt(-_expm1_nonpos(2.0 * log_a))
    return xc, sh, ga, gi, a, mult, sp


def _gmlp_fwd(z_of, vec_ref, ws_ref, bs_ref, tm):
    u = _gelu(z_of(6))
    gv = _gelu(z_of(7))
    rr = lax.rsqrt(jnp.mean(gv * gv, axis=-1, keepdims=True) + NORM_EPS)
    vn = (gv * rr) * vec_ref[4:5, :]
    masks = _head_masks((GMLP_CHUNK, GROUP_W))
    parts = []
    for c in range(tm // GMLP_CHUNK):
        vc = vn[c * GMLP_CHUNK:(c + 1) * GMLP_CHUNK].astype(MXU_DTYPE)
        acc = bs_ref[...]
        for h in range(N_HEADS):
            acc = acc + jnp.where(masks[h], jnp.dot(ws_ref[h], vc, preferred_element_type=F32), 0.0)
        parts.append(acc)
    return u, gv, rr, vn, jnp.concatenate(parts, axis=0)


def _mix_specs():
    const2 = lambda shape: pl.BlockSpec(shape, lambda i: (0, 0))
    return [const2((SUBLANES, GROUP_W)), const2((SUBLANES, GROUP_W)), const2((SUBLANES, GROUP_W)),
            const2((GROUP_W, GROUP_W)), const2((GROUP_W, GROUP_W)),
            pl.BlockSpec((N_HEADS, GMLP_CHUNK, GMLP_CHUNK), lambda i: (0, 0, 0)),
            const2((GMLP_CHUNK, GROUP_W))]


def _inproj_mix_fwd(x, g, w_t, mp, name):
    S, D = x.shape
    N = w_t.shape[0]
    tm = TM_MIX
    hb = tm // SUBLANES
    n_abc = N_ABC * GROUP_W
    n_qkv = 3 * GROUP_W

    def body(x_ref, g_ref, w_ref, wA_ref, wR_ref, vec_ref, wa_ref, wx_ref, ws_ref, bs_ref,
             z_ref, zg_ref, q1_ref, q4_ref, q16_ref, y_ref, h_ref, a_ref, mult_ref,
             qkv_ref, halo_ref, carry_ref, sa_ref, sb_ref, sc_ref, tmp_ref):
        @pl.when(pl.program_id(0) == 0)
        def _():
            halo_ref[...] = jnp.zeros_like(halo_ref)
            carry_ref[...] = jnp.zeros_like(carry_ref)

        xv = x_ref[...]
        r = lax.rsqrt(jnp.mean(xv * xv, axis=-1, keepdims=True) + NORM_EPS)
        hn = ((xv * r) * g_ref[...]).astype(MXU_DTYPE)
        z_ref[...] = _mm_nt(hn, w_ref[0:n_abc, :])
        _put(qkv_ref, _mm_nt(hn, w_ref[n_abc:n_abc + n_qkv, :]))
        zg_ref[...] = _mm_nt(hn, w_ref[n_abc + n_qkv:, :])
        for dil, ref in zip(ATTN_DILATIONS, (q1_ref, q4_ref, q16_ref)):
            _deinterleave(qkv_ref, ref, dil, tmp_ref)

        z_of = lambda c: z_ref[:, c * GROUP_W:(c + 1) * GROUP_W]
        halo_of = lambda c: halo_ref[:, c * GROUP_W:(c + 1) * GROUP_W]

        _, _, cv = _conv_a(z_of, halo_of, wA_ref)
        y_ref[:, 0:GROUP_W] = (z_of(1) * cv * _silu_and_grad(z_of(3))[0]).astype(y_ref.dtype)

        xc, _, _, gi, a, mult, _ = _lru_gates(z_of, halo_of, wR_ref, vec_ref, wa_ref, wx_ref)
        a_ref[...] = a
        mult_ref[...] = mult
        b = mult * (gi * xc)
        h, h_end = _scan_fwd_tile(a, b, carry_ref[SUBLANES - 1:SUBLANES, :], sa_ref, sb_ref, sc_ref)
        h_ref[...] = h
        carry_ref[...] = h_end[hb - SUBLANES:hb]
        y_ref[:, GROUP_W:2 * GROUP_W] = (h * _silu_and_grad(z_of(5))[0]).astype(y_ref.dtype)

        u, _, _, _, sp = _gmlp_fwd(z_of, vec_ref, ws_ref, bs_ref, tm)
        y_ref[:, 2 * GROUP_W:3 * GROUP_W] = (u * sp * _silu_and_grad(z_of(8))[0]).astype(y_ref.dtype)
        halo_ref[...] = z_ref[tm - SUBLANES:tm, :]

    row = lambda wd: pl.BlockSpec((tm, wd), lambda i: (i, 0))
    return pl.pallas_call(
        body, name=name, grid=(S // tm,),
        in_specs=[row(D), pl.BlockSpec((1, D), lambda i: (0, 0)),
                  pl.BlockSpec((N, D), lambda i: (0, 0), pipeline_mode=pl.Buffered(1))] + _mix_specs(),
        out_specs=[row(n_abc), row(GROUP_W)] + [_dilated_spec(tm, n_qkv, dil) for dil in ATTN_DILATIONS]
                  + [row(3 * GROUP_W)] + [row(GROUP_W)] * 3,
        out_shape=[jax.ShapeDtypeStruct((S, n_abc), F32), jax.ShapeDtypeStruct((S, GROUP_W), F32)]
                  + [_dilated_shape(S, n_qkv, dil, MXU_DTYPE) for dil in ATTN_DILATIONS]
                  + [jax.ShapeDtypeStruct((S, 3 * GROUP_W), MXU_DTYPE)] + [jax.ShapeDtypeStruct((S, GROUP_W), F32)] * 3,
        scratch_shapes=[_lane_scratch(tm, n_qkv), pltpu.VMEM((SUBLANES, n_abc), F32),
                        pltpu.VMEM((SUBLANES, GROUP_W), F32), _lane_scratch(tm, GROUP_W), _lane_scratch(tm, GROUP_W),
                        pltpu.VMEM((hb, GROUP_W), F32), _lane_scratch(tm, n_qkv)],
        compiler_params=_params(("arbitrary",)),
    )(x, g, w_t, mp["wA"], mp["wR"], mp["vec"], mp["wa"], mp["wx"], mp["ws"], mp["bs"])


_NEG = -1e30


def _slope(h):
    return 2.0 ** (-8.0 * (h + 1) / N_HEADS)


def _attn_bias(dil, offsets, n_keys):
    shape = (ATTN_BLOCK, n_keys)
    qi = lax.broadcasted_iota(jnp.int32, shape, 0)
    ki = lax.broadcasted_iota(jnp.int32, shape, 1)
    blocks = []
    for f in offsets:
        delta = qi + f - ki
        valid = (delta >= 0) & (delta <= ATTN_BLOCK)
        dist = (delta * dil).astype(F32)
        for h in range(N_HEADS):
            blocks.append(jnp.where(valid, -_slope(h) * dist, _NEG))
    return jnp.concatenate(blocks, axis=0)


def _stack_heads(t, masks):
    return jnp.concatenate([jnp.where(m, t, jnp.zeros_like(t)) for m in masks], axis=0)


def _unstack_heads(t4, masks, base=0):
    out = t4[base * ATTN_BLOCK:(base + 1) * ATTN_BLOCK]
    for h in range(1, N_HEADS):
        out = jnp.where(masks[h], t4[(base + h) * ATTN_BLOCK:(base + h + 1) * ATTN_BLOCK], out)
    return out


def _attn_fwd(qkv, dil, name):
    rows = qkv.shape[0]
    nb = rows // ATTN_BLOCK
    scale = 1.0 / math.sqrt(HEAD_DIM)
    B = ATTN_BLOCK
    per_step = min(ATTN_FWD_BLOCKS_PER_STEP, nb)

    def body(q_ref, kc_ref, kp_ref, vc_ref, vp_ref, o_ref, l_ref, bias_ref):
        n = pl.program_id(1)

        @pl.when(n == 0)
        def _():
            bias_ref[...] = _attn_bias(dil, (B,), 2 * B)

        masks = _head_masks((B, GROUP_W))
        for j in range(per_step):
            own = slice(j * B, (j + 1) * B)
            before = slice((j - 1) * B, j * B)
            qs = _stack_heads(q_ref[own], masks)
            keys = jnp.concatenate([kp_ref[...] if j == 0 else kc_ref[before], kc_ref[own]], axis=0)
            vals = jnp.concatenate([vp_ref[...] if j == 0 else vc_ref[before], vc_ref[own]], axis=0)
            s = _mm_nt(qs, keys) * scale + bias_ref[...]
            if j == 0:
                key_col = lax.broadcasted_iota(jnp.int32, s.shape, 1)
                s = jnp.where((n == 0) & (key_col < B), _NEG, s)
            m = jnp.max(s, axis=-1, keepdims=True)
            p = jnp.exp(s - m)
            l = jnp.sum(p, axis=-1, keepdims=True)
            o4 = jnp.dot(p.astype(MXU_DTYPE), vals, preferred_element_type=F32)
            o_ref[own] = (_unstack_heads(o4, masks)
                          / _unstack_heads(jnp.broadcast_to(l, o4.shape), masks)).astype(o_ref.dtype)
            l_ref[own] = _unstack_heads(jnp.broadcast_to(m + jnp.log(l), o4.shape), masks)

    blk = (per_step * B, GROUP_W)
    cur = lambda c: pl.BlockSpec(blk, lambda r, n: (n, r * 3 + c))
    prev = lambda c: pl.BlockSpec((B, GROUP_W), lambda r, n: (jnp.maximum(n * per_step - 1, 0), r * 3 + c))
    out = pl.BlockSpec(blk, lambda r, n: (n, r))
    return pl.pallas_call(
        body, name=name, grid=(dil, nb // per_step),
        in_specs=[cur(0), cur(1), prev(1), cur(2), prev(2)],
        out_specs=[out, out],
        out_shape=[jax.ShapeDtypeStruct((rows, dil * GROUP_W), MXU_DTYPE),
                   jax.ShapeDtypeStruct((rows, dil * GROUP_W), F32)],
        scratch_shapes=[pltpu.VMEM((N_HEADS * ATTN_BLOCK, 2 * ATTN_BLOCK), F32)],
        compiler_params=_params(("parallel", "arbitrary")),
    )(qkv, qkv, qkv, qkv, qkv)


def _outproj(x, z_g, y_abc, attn, w_out, name):
    S, D = x.shape
    tm = TM_MM
    n_abc = 3 * GROUP_W

    def body(x_ref, g_ref, yabc_ref, o1, l1, o2, l2, o3, l3, w_ref,
             xn_ref, y_ref, o_ref, lse1_ref, lse4_ref, lse16_ref, so2, sl2, so3, sl3, slse, tmp_ref):
        for src, dst, dil in ((o2, so2, ATTN_DILATIONS[1]), (l2, sl2, ATTN_DILATIONS[1]),
                              (o3, so3, ATTN_DILATIONS[2]), (l3, sl3, ATTN_DILATIONS[2])):
            _interleave(src, dst, dil, tmp_ref)
        la, lb, lc = l1[...], _get(sl2), _get(sl3)
        mx = jnp.maximum(jnp.maximum(la, lb), lc)
        ea, eb, ec = jnp.exp(la - mx), jnp.exp(lb - mx), jnp.exp(lc - mx)
        den = ea + eb + ec
        o = (ea * o1[...].astype(F32) + eb * _get(so2) + ec * _get(so3)) / den
        o_ref[...] = o
        _put(slse, mx + jnp.log(den))
        for dil, ref in zip(ATTN_DILATIONS, (lse1_ref, lse4_ref, lse16_ref)):
            _deinterleave(slse, ref, dil, tmp_ref)
        y_d = o * _silu_and_grad(g_ref[...])[0]
        y_ref[:, 0:n_abc] = yabc_ref[...].astype(MXU_DTYPE)
        y_ref[:, n_abc:] = y_d.astype(MXU_DTYPE)
        xn_ref[...] = x_ref[...] + jnp.dot(y_ref[...], w_ref[...], preferred_element_type=F32)

    row = lambda w: pl.BlockSpec((tm, w), lambda i: (i, 0))
    dil_specs = [_dilated_spec(tm, GROUP_W, dil) for dil in ATTN_DILATIONS]
    (o1, l1), (o2, l2), (o3, l3) = attn
    return pl.pallas_call(
        body, name=name, grid=(S // tm,),
        in_specs=[row(D), row(GROUP_W), row(n_abc)] + [sp for sp in dil_specs for _ in range(2)]
                 + [pl.BlockSpec(w_out.shape, lambda i: (0, 0))],
        out_specs=[row(D), row(4 * GROUP_W), row(GROUP_W)] + dil_specs,
        out_shape=[jax.ShapeDtypeStruct((S, D), F32), jax.ShapeDtypeStruct((S, 4 * GROUP_W), MXU_DTYPE),
                   jax.ShapeDtypeStruct((S, GROUP_W), F32)]
                  + [_dilated_shape(S, GROUP_W, dil, F32) for dil in ATTN_DILATIONS],
        scratch_shapes=[_lane_scratch(tm, GROUP_W)] * 6,
        compiler_params=_params(("parallel",)),
    )(x, z_g, y_abc, o1, l1, o2, l2, o3, l3, w_out)


def _loss_head(x, g, target, name):
    S, D = x.shape
    tm = TM_MM

    def body(x_ref, g_ref, t_ref, dx_ref, loss_ref, dg_ref):
        i = pl.program_id(0)

        @pl.when(i == 0)
        def _():
            loss_ref[...] = jnp.zeros_like(loss_ref)
            dg_ref[...] = jnp.zeros_like(dg_ref)

        xv = x_ref[...]
        r = lax.rsqrt(jnp.mean(xv * xv, axis=-1, keepdims=True) + NORM_EPS)
        xn = xv * r
        err = xn * g_ref[...] - t_ref[...]
        per_tok = jnp.mean(err * err, axis=-1, keepdims=True)
        loss_ref[...] += 0.5 * jnp.sum(per_tok, axis=0, keepdims=True)
        dout = err * (1.0 / D)
        dg_ref[...] += _colsum(dout * xn)
        dxn = dout * g_ref[...]
        dx_ref[...] = r * (dxn - xn * jnp.mean(dxn * xn, axis=-1, keepdims=True))

    row = pl.BlockSpec((tm, D), lambda i: (i, 0))
    return pl.pallas_call(
        body, name=name, grid=(S // tm,),
        in_specs=[row, pl.BlockSpec((1, D), lambda i: (0, 0)), row],
        out_specs=[row, pl.BlockSpec((1, LANES), lambda i: (0, 0)), pl.BlockSpec((1, D), lambda i: (0, 0))],
        out_shape=[jax.ShapeDtypeStruct((S, D), F32), jax.ShapeDtypeStruct((1, LANES), F32),
                   jax.ShapeDtypeStruct((1, D), F32)],
        compiler_params=_params(("arbitrary",)),
    )(x, g, target)


def _outproj_mix_bwd(dx, y, w_out, z, z_g, hs, lru_a, lru_mult, o, mp, name):
    S, D = dx.shape
    E = y.shape[1]
    tm = TM_MIX
    hb = tm // SUBLANES
    nT = S // tm
    last_blk = S // SUBLANES - 1
    wcols = N_ABC * GROUP_W

    def body(dx_ref, y_ref, w_ref, z_ref, zh_ref, zn_ref, zg_ref, h_ref, hh_ref, a_ref, mult_ref, o_ref,
             wA_ref, wR_ref, vec_ref, wa_ref, wx_ref, ws_ref, bs_ref,
             dw_ref, dz_ref, dzg_ref, do1_ref, do4_ref, do16_ref, dl1_ref, dl4_ref, dl16_ref,
             dwA_ref, dwR_ref, dvec_ref, dwa_ref, dwx_ref, dws_ref, dbs_ref,
             hcarry_ref, xcarry_ref, bsacc_ref, do_ref, dl_ref, sa_ref, sb_ref, sc_ref, dy_ref, dyn_ref, acc_ref,
             tmp_ref):
        i = pl.program_id(0)
        ti = nT - 1 - i

        @pl.when(i == 0)
        def _():
            acc_ref[...] = jnp.zeros_like(acc_ref)
            dyn_ref[...] = jnp.zeros_like(dyn_ref)
            hcarry_ref[...] = jnp.zeros_like(hcarry_ref)
            xcarry_ref[...] = jnp.zeros_like(xcarry_ref)
            bsacc_ref[...] = jnp.zeros_like(bsacc_ref)
            dwA_ref[...] = jnp.zeros_like(dwA_ref)
            dwR_ref[...] = jnp.zeros_like(dwR_ref)
            dvec_ref[...] = jnp.zeros_like(dvec_ref)
            dwa_ref[...] = jnp.zeros_like(dwa_ref)
            dwx_ref[...] = jnp.zeros_like(dwx_ref)
            dws_ref[...] = jnp.zeros_like(dws_ref)
            dbs_ref[...] = jnp.zeros_like(dbs_ref)

        dxb = dx_ref[...].astype(MXU_DTYPE)
        dy_ref[...] = _mm_nt(dxb, w_ref[...])
        acc_ref[...] += _mm_tn(y_ref[...], dxb)

        @pl.when(i == nT - 1)
        def _():
            dw_ref[...] = acc_ref[...].astype(dw_ref.dtype)

        has_prev = ti > 0
        has_next = i > 0
        col = lambda c: slice(c * GROUP_W, (c + 1) * GROUP_W)
        z_of = lambda c: z_ref[:, col(c)]
        halo_of = lambda c: jnp.where(has_prev, zh_ref[:, col(c)], 0.0)
        next_of = lambda c: zn_ref[:, col(c)]

        p, p_h, cv = _conv_a(z_of, halo_of, wA_ref)
        sg, dsg = _silu_and_grad(z_of(3))
        a_b = z_of(1)
        dya = dy_ref[:, col(0)]
        dcv = dya * a_b * sg
        dcv_n = jnp.where(has_next, dyn_ref[...] * next_of(1) * _silu_and_grad(next_of(3))[0], 0.0)
        dp = (wA_ref[2:3, :] * dcv + wA_ref[1:2, :] * _shift_up(dcv, dcv_n, 1)
              + wA_ref[0:1, :] * _shift_up(dcv, dcv_n, 2))
        dwA_ref[2:3, :] += _colsum(dcv * p)
        dwA_ref[1:2, :] += _colsum(dcv * _shift_down(p, p_h, 1))
        dwA_ref[0:1, :] += _colsum(dcv * _shift_down(p, p_h, 2))
        def put_dz(c, val):
            dz_ref[:, col(c)] = val.astype(dz_ref.dtype)

        put_dz(0, dp * z_of(2))
        put_dz(1, dya * cv * sg)
        put_dz(2, dp * z_of(0))
        put_dz(3, dya * a_b * cv * dsg)

        xc, sh, ga, gi, a, mult, sp = _lru_gates(z_of, halo_of, wR_ref, vec_ref, wa_ref, wx_ref,
                                                 saved=(a_ref[...], mult_ref[...]))
        h = h_ref[...]
        h_prev = _shift_down(h, jnp.where(has_prev, hh_ref[...], 0.0), 1)
        sgr, dsgr = _silu_and_grad(z_of(5))
        dyb = dy_ref[:, col(1)]
        put_dz(5, dyb * h * dsgr)
        row = lax.broadcasted_iota(jnp.int32, (tm, GROUP_W), 0)
        g_in = dyb * sgr + jnp.where(row == tm - 1, hcarry_ref[0:1, :], 0.0)
        a_up = _shift_up(a, jnp.zeros((SUBLANES, GROUP_W), F32), 1)
        dH = _scan_rev_tile(a_up, g_in, sa_ref, sb_ref, sc_ref)
        hcarry_ref[...] = (a * dH)[0:SUBLANES]
        da = dH * h_prev
        gx = gi * xc
        dmult = dH * gx
        dgi = dH * mult * xc
        dxc = dH * mult * gi
        dlog_a = da * a - dmult * (a * a) / mult
        dga = dlog_a * (-RG_C * sp)
        dlam_row = _colsum(dlog_a * (-RG_C * ga)) * (-_sigmoid(-vec_ref[3:4, :]))
        dpre_a = dga * ga * (1.0 - ga)
        dpre_i = dgi * gi * (1.0 - gi)
        dwa_ref[...] += _mm_tn(xc, dpre_a)
        dwx_ref[...] += _mm_tn(xc, dpre_i)
        dxc = dxc + _mm_nt(dpre_a, wa_ref[...]) + _mm_nt(dpre_i, wx_ref[...])
        dvec_ref[0:1, :] += _colsum(dxc)
        dvec_ref[1:2, :] += _colsum(dpre_a)
        dvec_ref[2:3, :] += _colsum(dpre_i)
        dvec_ref[3:4, :] += dlam_row
        for k in range(4):
            dwR_ref[k:k + 1, :] += _colsum(dxc * sh[3 - k])
        dxc_n = xcarry_ref[...]
        put_dz(4, wR_ref[3:4, :] * dxc + wR_ref[2:3, :] * _shift_up(dxc, dxc_n, 1)
               + wR_ref[1:2, :] * _shift_up(dxc, dxc_n, 2) + wR_ref[0:1, :] * _shift_up(dxc, dxc_n, 3))
        xcarry_ref[...] = dxc[0:SUBLANES]

        c_u, c_v = z_of(6), z_of(7)
        u, du_dx = _gelu_and_grad(c_u)
        gv, dgv_dx = _gelu_and_grad(c_v)
        rr = lax.rsqrt(jnp.mean(gv * gv, axis=-1, keepdims=True) + NORM_EPS)
        xhat = gv * rr
        g_c = vec_ref[4:5, :]
        vn = xhat * g_c
        masks = _head_masks((GMLP_CHUNK, GROUP_W))
        tri_r = lax.broadcasted_iota(jnp.int32, (GMLP_CHUNK, GMLP_CHUNK), 0)
        tri_c = lax.broadcasted_iota(jnp.int32, (GMLP_CHUNK, GMLP_CHUNK), 1)
        tril = tri_r >= tri_c
        sgc, dsgc = _silu_and_grad(z_of(8))
        dyc = dy_ref[:, col(2)]
        dsp_full = dyc * u * sgc
        sp_parts, dvn_parts = [], []
        for c in range(tm // GMLP_CHUNK):
            rs = slice(c * GMLP_CHUNK, (c + 1) * GMLP_CHUNK)
            vc = vn[rs].astype(MXU_DTYPE)
            dsp_c = dsp_full[rs]
            bsacc_ref[...] += dsp_c
            acc = bs_ref[...]
            dvn_c = jnp.zeros((GMLP_CHUNK, GROUP_W), F32)
            for h in range(N_HEADS):
                w_h = ws_ref[h]
                acc = acc + jnp.where(masks[h], jnp.dot(w_h, vc, preferred_element_type=F32), 0.0)
                dsp_h = jnp.where(masks[h], dsp_c, 0.0).astype(MXU_DTYPE)
                dvn_c = dvn_c + _mm_tn(w_h, dsp_h)
                dws_ref[h] += jnp.where(tril, _mm_nt(dsp_h, vc), 0.0)
            sp_parts.append(acc)
            dvn_parts.append(dvn_c)
        spv = jnp.concatenate(sp_parts, axis=0)
        dvn = jnp.concatenate(dvn_parts, axis=0)
        put_dz(6, dyc * spv * sgc * du_dx)
        put_dz(8, dyc * u * spv * dsgc)
        dvec_ref[4:5, :] += _colsum(dvn * xhat)
        dgvn = dvn * g_c
        dgv = rr * (dgvn - xhat * jnp.mean(dgvn * xhat, axis=-1, keepdims=True))
        put_dz(7, dgv * dgv_dx)

        sgd, dsgd = _silu_and_grad(zg_ref[...])
        dyd = dy_ref[:, col(3)]
        ov = o_ref[...]
        do = dyd * sgd
        _put(do_ref, do)
        dzg_ref[...] = (dyd * ov * dsgd).astype(dzg_ref.dtype)
        prod = do * ov
        tmasks = _head_masks((tm, GROUP_W))
        dl = jnp.zeros((tm, GROUP_W), F32)
        for h in range(N_HEADS):
            dl = jnp.where(tmasks[h], jnp.sum(jnp.where(tmasks[h], prod, 0.0), axis=-1, keepdims=True), dl)
        _put(dl_ref, dl)
        for dil, d_out, l_out in zip(ATTN_DILATIONS, (do1_ref, do4_ref, do16_ref), (dl1_ref, dl4_ref, dl16_ref)):
            _deinterleave(do_ref, d_out, dil, tmp_ref)
            _deinterleave(dl_ref, l_out, dil, tmp_ref)

        @pl.when(i == nT - 1)
        def _():
            acc = bsacc_ref[...]
            lane = lax.broadcasted_iota(jnp.int32, (GMLP_CHUNK, LANES), 1)
            out = jnp.zeros((GMLP_CHUNK, LANES), F32)
            for h in range(N_HEADS):
                out = jnp.where(lane == h, jnp.sum(jnp.where(masks[h], acc, 0.0), axis=-1, keepdims=True), out)
            dbs_ref[...] = out

        dyn_ref[...] = dy_ref[0:SUBLANES, 0:GROUP_W]

    rev = lambda w: pl.BlockSpec((tm, w), lambda i: (nT - 1 - i, 0))
    prev8 = lambda w: pl.BlockSpec((SUBLANES, w), lambda i: (jnp.maximum((nT - 1 - i) * hb - 1, 0), 0))
    next8 = lambda w: pl.BlockSpec((SUBLANES, w), lambda i: (jnp.minimum((nT - i) * hb, last_blk), 0))
    const2 = lambda shape: pl.BlockSpec(shape, lambda i: (0, 0))
    dil_specs = [_dilated_spec(tm, GROUP_W, dil, lambda i: nT - 1 - i) for dil in ATTN_DILATIONS]
    dil_shapes = [_dilated_shape(S, GROUP_W, dil, F32) for dil in ATTN_DILATIONS]
    small = (SUBLANES, GROUP_W)
    sq = (GROUP_W, GROUP_W)
    ws_shape = (N_HEADS, GMLP_CHUNK, GMLP_CHUNK)
    return pl.pallas_call(
        body, name=name, grid=(nT,),
        in_specs=[rev(D), rev(E), pl.BlockSpec((E, D), lambda i: (0, 0), pipeline_mode=pl.Buffered(1)),
                  rev(wcols), prev8(wcols), next8(wcols), rev(GROUP_W), rev(GROUP_W), prev8(GROUP_W),
                  rev(GROUP_W), rev(GROUP_W), rev(GROUP_W)]
                 + _mix_specs(),
        out_specs=[const2((E, D)), rev(wcols), rev(GROUP_W)] + dil_specs + dil_specs
                  + [const2(small), const2(small), const2(small), const2(sq), const2(sq),
                     pl.BlockSpec(ws_shape, lambda i: (0, 0, 0)), const2((GMLP_CHUNK, LANES))],
        out_shape=[jax.ShapeDtypeStruct((E, D), WIRE_DTYPE),
                   jax.ShapeDtypeStruct((S, wcols), MXU_DTYPE), jax.ShapeDtypeStruct((S, GROUP_W), MXU_DTYPE)]
                  + [_dilated_shape(S, GROUP_W, dil, MXU_DTYPE) for dil in ATTN_DILATIONS] + dil_shapes
                  + [jax.ShapeDtypeStruct(small, F32)] * 3 + [jax.ShapeDtypeStruct(sq, F32)] * 2
                  + [jax.ShapeDtypeStruct(ws_shape, F32), jax.ShapeDtypeStruct((GMLP_CHUNK, LANES), F32)],
        scratch_shapes=[pltpu.VMEM(small, F32), pltpu.VMEM(small, F32), pltpu.VMEM((GMLP_CHUNK, GROUP_W), F32),
                        _lane_scratch(tm, GROUP_W), _lane_scratch(tm, GROUP_W),
                        _lane_scratch(tm, GROUP_W), _lane_scratch(tm, GROUP_W), pltpu.VMEM((hb, GROUP_W), F32),
                        pltpu.VMEM((tm, E), F32), pltpu.VMEM(small, F32), pltpu.VMEM((E, D), F32),
                        _lane_scratch(tm, GROUP_W)],
        compiler_params=_params(("arbitrary",)),
    )(dx, y, w_out, z, z, z, z_g, hs, hs, lru_a, lru_mult, o, mp["wA"], mp["wR"], mp["vec"], mp["wa"], mp["wx"], mp["ws"], mp["bs"])


def _attn_bwd(qkv, do, lse, delta, dil, name):
    rows = qkv.shape[0]
    nb = rows // ATTN_BLOCK
    scale = 1.0 / math.sqrt(HEAD_DIM)
    B = ATTN_BLOCK
    per_step = min(ATTN_BWD_BLOCKS_PER_STEP, nb)
    n_steps = nb // per_step

    def body(qc_ref, qn_ref, kc_ref, kp_ref, vc_ref, vp_ref, doc_ref, don_ref, lc_ref, ln_ref, dc_ref, dn_ref,
             dq_ref, dk_ref, dv_ref, bias_ref, bias_next_ref):
        n = pl.program_id(1)

        @pl.when(n == 0)
        def _():
            bias_ref[...] = _attn_bias(dil, (B,), 2 * B)
            bias_next_ref[...] = _attn_bias(dil, (B,), B)

        masks = _head_masks((B, GROUP_W))

        def per_row(tile):
            return jnp.concatenate([jnp.max(jnp.where(masks[h], tile, _NEG), axis=-1, keepdims=True)
                                    for h in range(N_HEADS)], axis=0)

        def grads(q, dov, lse_tile, dl_tile, keys, vals, bias, dead):
            qs = _stack_heads(q, masks)
            dos = _stack_heads(dov.astype(MXU_DTYPE), masks)
            s = _mm_nt(qs, keys) * scale + bias
            if dead is not None:
                s = jnp.where(dead(s.shape), _NEG, s)
            p = jnp.exp(s - per_row(lse_tile))
            ds = (p * (_mm_nt(dos, vals) - per_row(dl_tile)) * scale).astype(MXU_DTYPE)
            return ds, _mm_tn(ds, qs), _mm_tn(p.astype(MXU_DTYPE), dos)

        for j in range(per_step):
            own = slice(j * B, (j + 1) * B)
            before = slice((j - 1) * B, j * B)
            keys = jnp.concatenate([kp_ref[...] if j == 0 else kc_ref[before], kc_ref[own]], axis=0)
            vals = jnp.concatenate([vp_ref[...] if j == 0 else vc_ref[before], vc_ref[own]], axis=0)
            dead = (lambda shape: (n == 0) & (lax.broadcasted_iota(jnp.int32, shape, 1) < B)) if j == 0 else None
            ds, dk2, dv2 = grads(qc_ref[own], doc_ref[own], lc_ref[own], dc_ref[own], keys, vals, bias_ref[...], dead)
            dq_ref[own] = _unstack_heads(jnp.dot(ds, keys, preferred_element_type=F32), masks).astype(dq_ref.dtype)
            if j > 0:
                dk_ref[before] = (dk_own + dk2[:B]).astype(dk_ref.dtype)
                dv_ref[before] = (dv_own + dv2[:B]).astype(dv_ref.dtype)
            dk_own, dv_own = dk2[B:], dv2[B:]
        last = slice((per_step - 1) * B, per_step * B)
        _, dk1, dv1 = grads(qn_ref[...], don_ref[...], ln_ref[...], dn_ref[...], kc_ref[last], vc_ref[last],
                            bias_next_ref[...], lambda shape: n == n_steps - 1)
        dk_ref[last] = (dk_own + dk1).astype(dk_ref.dtype)
        dv_ref[last] = (dv_own + dv1).astype(dv_ref.dtype)

    blk = (per_step * B, GROUP_W)
    one = (B, GROUP_W)
    nxt_idx = lambda n: jnp.minimum((n + 1) * per_step, nb - 1)
    prv_idx = lambda n: jnp.maximum(n * per_step - 1, 0)
    zcur = lambda c: pl.BlockSpec(blk, lambda r, n: (n, r * 3 + c))
    znext = lambda c: pl.BlockSpec(one, lambda r, n: (nxt_idx(n), r * 3 + c))
    zprev = lambda c: pl.BlockSpec(one, lambda r, n: (prv_idx(n), r * 3 + c))
    cur = pl.BlockSpec(blk, lambda r, n: (n, r))
    nxt = pl.BlockSpec(one, lambda r, n: (nxt_idx(n), r))
    return pl.pallas_call(
        body, name=name, grid=(dil, n_steps),
        in_specs=[zcur(0), znext(0), zcur(1), zprev(1), zcur(2), zprev(2), cur, nxt, cur, nxt, cur, nxt],
        out_specs=[cur, cur, cur],
        out_shape=[jax.ShapeDtypeStruct((rows, dil * GROUP_W), WIRE_DTYPE)] * 3,
        scratch_shapes=[pltpu.VMEM((N_HEADS * B, 2 * B), F32), pltpu.VMEM((N_HEADS * B, B), F32)],
        compiler_params=_params(("parallel", "arbitrary")),
    )(qkv, qkv, qkv, qkv, qkv, qkv, do, do, lse, lse, delta, delta)


def _inproj_bwd(x, g, dxn, dz_abc, dqkv, dz_g, w_t, name):
    S, D = x.shape
    N = w_t.shape[0]
    tm = TM_MM
    n_abc = N_ABC * GROUP_W

    def body(x_ref, g_ref, dxn_ref, dabc_ref, q1, k1, v1, q2, k2, v2, q3, k3, v3, dg_ref, w_ref,
             dx_ref, dz_ref, h_ref, dgn_ref, s4_ref, s16_ref, tmp_ref):
        i = pl.program_id(0)

        @pl.when(i == 0)
        def _():
            dgn_ref[...] = jnp.zeros_like(dgn_ref)

        dz_ref[:, 0:n_abc] = dabc_ref[...].astype(MXU_DTYPE)
        for j, parts in enumerate(((q1, q2, q3), (k1, k2, k3), (v1, v2, v3))):
            c0 = n_abc + j * GROUP_W
            _interleave(parts[1], s4_ref, ATTN_DILATIONS[1])
            _interleave(parts[2], s16_ref, ATTN_DILATIONS[2], tmp_ref)
            dz_ref[:, c0:c0 + GROUP_W] = (parts[0][...] + _get(s4_ref) + _get(s16_ref)).astype(MXU_DTYPE)
        dz_ref[:, n_abc + 3 * GROUP_W:] = dg_ref[...].astype(MXU_DTYPE)
        dh = jnp.dot(dz_ref[...], w_ref[...], preferred_element_type=F32)
        xv = x_ref[...]
        r = lax.rsqrt(jnp.mean(xv * xv, axis=-1, keepdims=True) + NORM_EPS)
        xn = xv * r
        gv = g_ref[...]
        h_ref[...] = (xn * gv).astype(MXU_DTYPE)
        dgn_ref[...] += _colsum(dh * xn)
        dn = dh * gv
        dx_ref[...] = dxn_ref[...] + r * (dn - xn * jnp.mean(dn * xn, axis=-1, keepdims=True))

    row = lambda w: pl.BlockSpec((tm, w), lambda i: (i, 0))
    flat = [t for p in dqkv for t in p]
    dil_specs = [_dilated_spec(tm, GROUP_W, dil) for dil in ATTN_DILATIONS for _ in range(3)]
    return pl.pallas_call(
        body, name=name, grid=(S // tm,),
        in_specs=[row(D), pl.BlockSpec((1, D), lambda i: (0, 0)), row(D), row(n_abc)] + dil_specs
                 + [row(GROUP_W), pl.BlockSpec((N, D), lambda i: (0, 0), pipeline_mode=pl.Buffered(1))],
        out_specs=[row(D), row(N), row(D), pl.BlockSpec((1, D), lambda i: (0, 0))],
        out_shape=[jax.ShapeDtypeStruct((S, D), F32), jax.ShapeDtypeStruct((S, N), MXU_DTYPE),
                   jax.ShapeDtypeStruct((S, D), MXU_DTYPE), jax.ShapeDtypeStruct((1, D), F32)],
        scratch_shapes=[_lane_scratch(tm, GROUP_W)] * 3,
        compiler_params=_params(("arbitrary",)),
    )(x, g, dxn, dz_abc, *flat, dz_g, w_t)


def _inproj_wgrad(h, dz, name):
    S, D = h.shape
    N = dz.shape[1]
    tm = TM_WGRAD
    nj = 2
    cw = N // nj
    per = N_DEV // nj
    n_loc = N // N_DEV

    def body(h_ref, dz_ref, dw_ref, acc_ref):
        i = pl.program_id(1)

        @pl.when(i == 0)
        def _():
            acc_ref[...] = jnp.zeros_like(acc_ref)

        acc_ref[...] += _mm_tn(dz_ref[...], h_ref[...])

        @pl.when(i == S // tm - 1)
        def _():
            for b in range(per):
                dw_ref[b] = acc_ref[b * n_loc:(b + 1) * n_loc, :].astype(dw_ref.dtype)

    return pl.pallas_call(
        body, name=name, grid=(nj, S // tm),
        in_specs=[pl.BlockSpec((tm, D), lambda j, i: (i, 0)), pl.BlockSpec((tm, cw), lambda j, i: (i, j))],
        out_specs=pl.BlockSpec((per, n_loc, D), lambda j, i: (j, 0, 0)),
        out_shape=jax.ShapeDtypeStruct((N_DEV, n_loc, D), WIRE_DTYPE),
        scratch_shapes=[pltpu.VMEM((cw, D), F32)],
        compiler_params=_params(("parallel", "arbitrary")),
    )(h, dz)


def _my_place():
    return lax.axis_index("x"), lax.axis_index("y"), lax.axis_index("c")


def _peer(x, y, c, k):
    px = 1 - x if k & 4 else x
    py = 1 - y if k & 2 else y
    pc = 1 - c if k & 1 else c
    return (px, py, pc), 4 * px + 2 * py + pc


HBM_SPEC = pl.BlockSpec(memory_space=pltpu.HBM)
SEM_SPEC = pl.BlockSpec(memory_space=pltpu.SEMAPHORE)
SPLIT_EFFECT = pltpu.SideEffectType.DATAFLOW_SIDE_EFFECTING
N_PEERS = N_DEV - 1


def _exchange_copies(srcs, lands, send_sems, recv_sems, whole, arrival):
    x, y, c = _my_place()
    me = 4 * x + 2 * y + c
    copies = []
    for t in range(len(srcs)):
        for k in range(1, N_DEV):
            peer, pidx = _peer(x, y, c, k)
            copies.append(pltpu.make_async_remote_copy(
                src_ref=srcs[t] if whole[t] else srcs[t].at[pidx],
                dst_ref=lands[t].at[pidx if arrival else me], send_sem=send_sems.at[t * N_PEERS + k - 1],
                recv_sem=recv_sems.at[t * N_PEERS + k - 1], device_id=peer, device_id_type=MESH))
    return copies


def _exchange_start(groups, name, after=None):
    sizes = [len(g) for g in groups]
    whole = [w for g in groups for _, w in g]
    srcs = [pltpu.with_memory_space_constraint(a, pltpu.HBM) for g in groups for a, _ in g]
    lands = [pltpu.with_memory_space_constraint(lax.empty(((N_DEV,) + a.shape) if w else a.shape, a.dtype), pltpu.HBM)
             for a, w in zip(srcs, whole)]
    n = len(srcs)
    n_g = len(groups)
    extra = [] if after is None else [after]
    n_in = 2 * n + len(extra)

    def body(*refs):
        src_refs, land_refs = refs[:n], refs[n:2 * n]
        sem_refs = refs[n_in + 2 * n:n_in + 2 * n + 2 * n_g]
        token = refs[-1]
        off = 0
        for gi, sz in enumerate(sizes):
            for send in _exchange_copies(src_refs[off:off + sz], land_refs[off:off + sz],
                                         sem_refs[2 * gi], sem_refs[2 * gi + 1], whole[off:off + sz], False):
                send.start()
            off += sz
        token[...] = jnp.zeros_like(token)

    sem_shapes = [pltpu.SemaphoreType.DMA((sz * N_PEERS,)) for sz in sizes for _ in range(2)]
    outs = pl.pallas_call(
        body, name=name,
        in_specs=[HBM_SPEC] * (2 * n) + [pl.BlockSpec(memory_space=pl.ANY)] * len(extra),
        out_specs=[HBM_SPEC] * (2 * n) + [SEM_SPEC] * (2 * n_g) + [pl.BlockSpec(memory_space=pltpu.VMEM)],
        out_shape=[pltpu.HBM(a.shape, a.dtype) for a in srcs + lands] + sem_shapes
                  + [jax.ShapeDtypeStruct((SUBLANES, LANES), F32)],
        input_output_aliases={i: i for i in range(2 * n)},
        compiler_params=pltpu.CompilerParams(has_side_effects=SPLIT_EFFECT),
    )(*srcs, *lands, *extra)
    handles, off = [], 0
    for gi, sz in enumerate(sizes):
        handles.append((outs[2 * n + 2 * gi], outs[2 * n + 2 * gi + 1], outs[off:off + sz], outs[n + off:n + off + sz],
                        whole[off:off + sz]))
        off += sz
    return handles, outs[-1]


def _exchange_wait(handle, after, name):
    send_sems, recv_sems, srcs, lands, whole = handle
    n = len(srcs)

    def body(*refs):
        src_refs, land_refs = refs[:n], refs[n:2 * n]
        for send in _exchange_copies(src_refs, land_refs, refs[2 * n], refs[2 * n + 1], whole, False):
            send.wait_send()
        for arrival in _exchange_copies(src_refs, land_refs, refs[2 * n], refs[2 * n + 1], whole, True):
            arrival.wait_recv()

    outs = pl.pallas_call(
        body, name=name,
        in_specs=[HBM_SPEC] * (2 * n) + [SEM_SPEC, SEM_SPEC, pl.BlockSpec(memory_space=pl.ANY)],
        out_specs=[HBM_SPEC] * (2 * n),
        out_shape=[pltpu.HBM(a.shape, a.dtype) for a in list(srcs) + list(lands)],
        input_output_aliases={i: i for i in range(2 * n)},
        compiler_params=pltpu.CompilerParams(has_side_effects=SPLIT_EFFECT),
    )(*srcs, *lands, send_sems, recv_sems, after)
    x, y, c = _my_place()
    me = 4 * x + 2 * y + c
    own = [s[None] if w else lax.dynamic_slice_in_dim(s, me, 1, axis=0) for s, w in zip(outs[:n], whole)]
    return [lax.dynamic_update_slice_in_dim(ld, o, me, axis=0) for ld, o in zip(outs[n:], own)]


def _sum_slots(parts, name):
    n = len(parts)

    def body(*refs):
        for p_ref, o_ref in zip(refs[:n], refs[n:]):
            acc = p_ref[0]
            for j in range(1, N_DEV):
                acc = acc + p_ref[j]
            o_ref[...] = acc

    vm = pl.BlockSpec(memory_space=pltpu.VMEM)
    return pl.pallas_call(
        body, name=name, in_specs=[vm] * n, out_specs=[vm] * n,
        out_shape=[jax.ShapeDtypeStruct(p.shape[1:], F32) for p in parts],
        compiler_params=pltpu.CompilerParams(vmem_limit_bytes=VMEM_LIMIT),
    )(*parts)


def _adamw_math(w, g, m, v):
    m = ADAM_B1 * m + (1.0 - ADAM_B1) * g
    v = ADAM_B2 * v + (1.0 - ADAM_B2) * (g * g)
    m_hat = m / (1.0 - ADAM_B1 ** ADAM_STEP)
    v_hat = v / (1.0 - ADAM_B2 ** ADAM_STEP)
    delta = -ADAM_LR * (m_hat / (jnp.sqrt(v_hat) + ADAM_EPS) + ADAM_WD * w)
    return delta, m, v


def _adamw_summed(parts, w, m, v, tr, name):
    depth, R, C = w.shape

    def body(*refs):
        p_refs = refs[:depth]
        w_ref, m_ref, v_ref, g_ref, d_ref, nm_ref, nv_ref = refs[depth:]
        lay = pl.program_id(0)
        for l in range(depth):
            @pl.when(lay == l)
            def _(p_ref=p_refs[l]):
                g = p_ref[0].astype(F32)
                for j in range(1, N_DEV):
                    g = g + p_ref[j].astype(F32)
                g_ref[0] = g
        d_ref[0], nm_ref[0], nv_ref[0] = _adamw_math(w_ref[0], g_ref[0], m_ref[0], v_ref[0])

    part_spec = lambda l: pl.BlockSpec((N_DEV, tr, C), lambda lay, i: (0, jnp.where(lay == l, i, 0), 0))
    row = pl.BlockSpec((1, tr, C), lambda lay, i: (lay, i, 0))
    return pl.pallas_call(
        body, name=name, grid=(depth, R // tr),
        in_specs=[part_spec(l) for l in range(depth)] + [row, row, row],
        out_specs=[row] * 4, out_shape=[jax.ShapeDtypeStruct((depth, R, C), F32)] * 4,
        compiler_params=_params(("arbitrary", "arbitrary")),
    )(*parts, w, m, v)


def _adamw_small(w, g, m, v, name):
    def body(w_ref, g_ref, m_ref, v_ref, d_ref, nm_ref, nv_ref):
        d_ref[...], nm_ref[...], nv_ref[...] = _adamw_math(w_ref[...], g_ref[...], m_ref[...], v_ref[...])

    vm = pl.BlockSpec(memory_space=pltpu.VMEM)
    return pl.pallas_call(
        body, name=name, in_specs=[vm] * 4, out_specs=[vm] * 3,
        out_shape=[jax.ShapeDtypeStruct(w.shape, F32)] * 3,
        compiler_params=pltpu.CompilerParams(vmem_limit_bytes=VMEM_LIMIT),
    )(w, g, m, v)


def _pack(arrays):
    flat = jnp.concatenate([a.reshape(-1) for a in arrays])
    pad = (-flat.shape[0]) % (SUBLANES * LANES)
    return jnp.pad(flat, (0, pad)).reshape(-1, LANES)


def _unpack(buf, like):
    flat = buf.reshape(-1)
    out, off = [], 0
    for a in like:
        out.append(flat[off:off + a.size].reshape(a.shape))
        off += a.size
    return out


def _block_diag(w):
    eye = jnp.eye(N_HEADS, dtype=w.dtype)
    return jnp.einsum('hij,hk->hikj', w, eye).reshape(GROUP_W, GROUP_W)


def _diag_blocks(w):
    return jnp.einsum('hihj->hij', w.reshape(N_HEADS, HEAD_DIM, N_HEADS, HEAD_DIM))


def _pad_rows(a):
    return jnp.pad(a, ((0, SUBLANES - a.shape[0]), (0, 0)))


def _mixer_params(l, conv_a_w, conv_r_w, conv_r_b, lru_wa, lru_ba, lru_wx, lru_bx, lru_lambda, gmlp_norm_g,
                  gmlp_ws, gmlp_bs):
    tril = jnp.tril(jnp.ones((GMLP_CHUNK, GMLP_CHUNK), dtype=bool))
    vec = jnp.stack([conv_r_b[l], lru_ba[l], lru_bx[l], lru_lambda[l], gmlp_norm_g[l]])
    return {
        "wA": _pad_rows(conv_a_w[l]), "wR": _pad_rows(conv_r_w[l]), "vec": _pad_rows(vec),
        "wa": _block_diag(lru_wa[l]).astype(MXU_DTYPE), "wx": _block_diag(lru_wx[l]).astype(MXU_DTYPE),
        "ws": jnp.where(tril[None], gmlp_ws[l], 0.0).astype(MXU_DTYPE),
        "bs": jnp.repeat(jnp.transpose(gmlp_bs[l]), HEAD_DIM, axis=1),
    }


MIXER_NAMES = ("conv_a_w", "conv_r_w", "conv_r_b", "lru_wa", "lru_ba", "lru_wx", "lru_bx", "lru_lambda",
               "gmlp_norm_g", "gmlp_ws", "gmlp_bs")
SMALL_NAMES = ("norm_g",) + MIXER_NAMES + ("final_g",)


def _local_step(x, loss_target, norm_g, get_w_in, get_w_out, emit_early, emit_late, conv_a_w, conv_r_w, conv_r_b,
                lru_wa, lru_ba, lru_wx, lru_bx, lru_lambda, gmlp_norm_g, gmlp_ws, gmlp_bs, final_g):
    depth = norm_g.shape[0]
    D = x.shape[1]
    small = (conv_a_w, conv_r_w, conv_r_b, lru_wa, lru_ba, lru_wx, lru_bx, lru_lambda, gmlp_norm_g, gmlp_ws, gmlp_bs)
    saved = []
    for l in range(depth):
        mp = _mixer_params(l, *small)
        w_in_l = get_w_in(l, x)
        z, z_g, *qkv, y_abc, hs, lru_a, lru_mult = _inproj_mix_fwd(
            x, norm_g[l].reshape(1, D), w_in_l, mp, f"inproj_mix_fwd_{l}")
        attn =[_attn_fwd(qkv[p], dil, f"attn_fwd_d{dil}_{l}") for p, dil in enumerate(ATTN_DILATIONS)]
        w_out_l = get_w_out(l, y_abc)
        x_new, y, o, *lse = _outproj(x, z_g, y_abc, attn, w_out_l, f"outproj_{l}")
        saved.append((x, z, z_g, qkv, (hs, lru_a, lru_mult), y, o, lse, mp, w_in_l, w_out_l))
        x = x_new
    dx, loss, d_final_g = _loss_head(x, final_g.reshape(1, D), loss_target, "loss_head")
    token = None
    for l in reversed(range(depth)):
        x_l, z, z_g, qkv, lru, y, o, lse, mp, w_in_l, w_out_l = saved[l]
        if token is not None:
            mp = dict(mp, vec=mp["vec"] + token[0, 0])
        (dw_out, dz_abc, dz_g, do1, do4, do16, dl1, dl4, dl16, dwA, dwR, dvec, dwa, dwx, dws, dbs) = _outproj_mix_bwd(
            dx, y, w_out_l, z, z_g, *lru, o, mp, f"outproj_mix_bwd_{l}")
        token = emit_early(l, dw_out, [
            dwA[:conv_a_w.shape[1]], dwR[:conv_r_w.shape[1]], dvec[0], _diag_blocks(dwa), dvec[1], _diag_blocks(dwx),
            dvec[2], dvec[3], dvec[4], dws, jnp.transpose(dbs[:, :N_HEADS])])
        g_row = norm_g[l].reshape(1, D)
        if token is not None:
            g_row = g_row + token[0, 0]
        dqkv = [_attn_bwd(qkv[p], do, lse[p], dl, dil, f"attn_bwd_d{dil}_{l}")
                for p, (dil, do, dl) in enumerate(zip(ATTN_DILATIONS, (do1, do4, do16), (dl1, dl4, dl16)))]
        dx, dz, h, dng = _inproj_bwd(x_l, g_row, dx, dz_abc, dqkv, dz_g, w_in_l, f"inproj_bwd_{l}")
        dw_in = _inproj_wgrad(h, dz, f"inproj_wgrad_{l}")
        token = emit_late(l, dw_in, [dng[0]] + ([d_final_g[0]] if l == depth - 1 else []))
    return loss[0, 0], dx
WEIGHT_NAMES = ("norm_g", "w_in", "conv_a_w", "conv_r_w", "conv_r_b", "lru_wa", "lru_ba", "lru_wx", "lru_bx",
                "lru_lambda", "gmlp_norm_g", "gmlp_ws", "gmlp_bs", "w_out", "final_g")


def kernel(x, norm_g, w_in, conv_a_w, conv_r_w, conv_r_b, lru_wa, lru_ba, lru_wx, lru_bx, lru_lambda, gmlp_norm_g, gmlp_ws, gmlp_bs, w_out, final_g, loss_target, m_norm_g, m_w_in, m_conv_a_w, m_conv_r_w, m_conv_r_b, m_lru_wa, m_lru_ba, m_lru_wx, m_lru_bx, m_lru_lambda, m_gmlp_norm_g, m_gmlp_ws, m_gmlp_bs, m_w_out, m_final_g, v_norm_g, v_w_in, v_conv_a_w, v_conv_r_w, v_conv_r_b, v_lru_wa, v_lru_ba, v_lru_wx, v_lru_bx, v_lru_lambda, v_gmlp_norm_g, v_gmlp_ws, v_gmlp_bs, v_w_out, v_final_g):
    w = dict(norm_g=norm_g, w_in=w_in, conv_a_w=conv_a_w, conv_r_w=conv_r_w, conv_r_b=conv_r_b, lru_wa=lru_wa,
             lru_ba=lru_ba, lru_wx=lru_wx, lru_bx=lru_bx, lru_lambda=lru_lambda, gmlp_norm_g=gmlp_norm_g,
             gmlp_ws=gmlp_ws, gmlp_bs=gmlp_bs, w_out=w_out, final_g=final_g)
    m = dict(norm_g=m_norm_g, w_in=m_w_in, conv_a_w=m_conv_a_w, conv_r_w=m_conv_r_w, conv_r_b=m_conv_r_b,
             lru_wa=m_lru_wa, lru_ba=m_lru_ba, lru_wx=m_lru_wx, lru_bx=m_lru_bx, lru_lambda=m_lru_lambda,
             gmlp_norm_g=m_gmlp_norm_g, gmlp_ws=m_gmlp_ws, gmlp_bs=m_gmlp_bs, w_out=m_w_out, final_g=m_final_g)
    v = dict(norm_g=v_norm_g, w_in=v_w_in, conv_a_w=v_conv_a_w, conv_r_w=v_conv_r_w, conv_r_b=v_conv_r_b,
             lru_wa=v_lru_wa, lru_ba=v_lru_ba, lru_wx=v_lru_wx, lru_bx=v_lru_bx, lru_lambda=v_lru_lambda,
             gmlp_norm_g=v_gmlp_norm_g, gmlp_ws=v_gmlp_ws, gmlp_bs=v_gmlp_bs, w_out=v_w_out, final_g=v_final_g)
    depth, D, n_loc = w_in.shape
    e_loc = w_out.shape[1]
    cx, cy, cc = _my_place()
    me = 4 * cx + 2 * cy + cc

    transposed = lambda a: jnp.transpose(a, (0, 2, 1))
    w_in_t, m_w_in_t, v_w_in_t = transposed(w_in), transposed(m_w_in), transposed(v_w_in)
    w_in_w, w_out_w = w_in_t.astype(MXU_DTYPE), w_out.astype(MXU_DTYPE)
    c_loc = conv_a_w.shape[2]
    taps = (conv_a_w, conv_r_w)
    first, _ = _exchange_start([[(w_in_w[0], True), (_pack(taps), True)], [(w_out_w[0], True)]], "gather_start_first")
    full_in = lambda g: g.reshape(N_DEV * n_loc, D)
    full_out = lambda g: g.reshape(N_DEV * e_loc, D)

    g_in0, g_taps = _exchange_wait(first[0], x, "gather_wait_in_0")
    groups = [[(w_in_w[l], True), (w_out_w[l], True)] for l in range(1, depth)]
    gathers, rest_token = _exchange_start(groups, "gather_start_rest", after=g_taps)
    g_taps = g_taps.reshape(N_DEV, -1) + rest_token[0, 0]
    conv_full, off = [], 0
    for a in taps:
        part = g_taps[:, off:off + a.size].reshape((N_DEV,) + a.shape)
        conv_full.append(jnp.transpose(part, (1, 2, 0, 3)).reshape(a.shape[:2] + (N_DEV * c_loc,)))
        off += a.size
    conv_a_full, conv_r_full = conv_full
    later = {}

    def get_w_in(l, after):
        if l == 0:
            return full_in(g_in0)
        g_in, later[l] = _exchange_wait(gathers[l - 1], after, f"gather_wait_{l}")
        return full_in(g_in)

    def get_w_out(l, after):
        if l == 0:
            return full_out(_exchange_wait(first[1], after, "gather_wait_out_0")[0])
        return full_out(later[l])

    early, late, last_token = {}, {}, [None]

    def emit_early(l, dw_out, mixer_grads):
        handles, token = _exchange_start(
            [[(dw_out.reshape(N_DEV, e_loc, D), False), (_pack(mixer_grads), True)]], f"early_start_{l}")
        early[l] = (handles[0], mixer_grads)
        return token

    def emit_late(l, dw_in, norm_grads):
        handles, token = _exchange_start([[(_pack(norm_grads), True)], [(dw_in, False)]], f"late_start_{l}")
        late[l] = (handles[0], handles[1], norm_grads)
        last_token[0] = token
        return token

    loss, grad_x = _local_step(
        x[0], loss_target[0], norm_g, get_w_in, get_w_out, emit_early, emit_late, conv_a_full, conv_r_full, conv_r_b,
        lru_wa, lru_ba, lru_wx, lru_bx, lru_lambda, gmlp_norm_g, gmlp_ws, gmlp_bs, final_g)
    loss = lax.psum(loss, ("x", "y", "c"))

    r_in, r_out, small_parts = {}, {}, []
    for l in reversed(range(depth)):
        r_out[l], r_mix = _exchange_wait(early[l][0], last_token[0], f"early_wait_{l}")
        (r_norm,) = _exchange_wait(late[l][0], last_token[0], f"late_wait_norm_{l}")
        small_parts += [r_mix, r_norm]
        if l > 0:
            (r_in[l],) = _exchange_wait(late[l][1], last_token[0], f"late_wait_{l}")
    big = {"w_out": _adamw_summed([r_out[l] for l in range(depth)], w_out, m_w_out, v_w_out, e_loc, "adamw_w_out")}

    sums = _sum_slots(small_parts, "sum_small_grads")
    by_layer = {}
    for i, l in enumerate(reversed(range(depth))):
        mix = _unpack(sums[2 * i], early[l][1])
        nrm = _unpack(sums[2 * i + 1], late[l][2])
        by_layer[l] = dict(zip(MIXER_NAMES, mix), norm_g=nrm[0])
        if l == depth - 1:
            g_final = nrm[1]
    g_small = {k: jnp.stack([by_layer[l][k] for l in range(depth)]) for k in ("norm_g",) + MIXER_NAMES}
    g_small["final_g"] = g_final
    for k in ("conv_a_w", "conv_r_w"):
        g_small[k] = lax.dynamic_slice_in_dim(g_small[k], me * c_loc, c_loc, axis=2)
    packs = [_pack([d[k] for k in SMALL_NAMES]) for d in (w, g_small, m, v)]
    res = _adamw_small(*packs, "adamw_small")
    like = [w[k] for k in SMALL_NAMES]
    d_s, m_s, v_s = (dict(zip(SMALL_NAMES, _unpack(r, like))) for r in res)

    (r_in[0],) = _exchange_wait(late[0][1], res[0], "late_wait_0")
    big["w_in"] = [transposed(a) for a in _adamw_summed(
        [r_in[l] for l in range(depth)], w_in_t, m_w_in_t, v_w_in_t, n_loc // 2, "adamw_w_in")]

    grad, delta, new_m, new_v = {}, {}, {}, {}
    for k in WEIGHT_NAMES:
        if k in big:
            grad[k], delta[k], new_m[k], new_v[k] = big[k]
        else:
            grad[k], delta[k], new_m[k], new_v[k] = g_small[k], d_s[k], m_s[k], v_s[k]
    return (loss, grad_x[None], *[grad[k] for k in WEIGHT_NAMES], *[delta[k] for k in WEIGHT_NAMES],
            *[new_m[k] for k in WEIGHT_NAMES], *[new_v[k] for k in WEIGHT_NAMES])
```

```python
import functools
import math

import jax
import jax.numpy as jnp
from jax import lax
from jax.experimental import pallas as pl
from jax.experimental.pallas import tpu as pltpu

F32 = jnp.float32
MXU_DTYPE = jnp.bfloat16
WIRE_DTYPE = jnp.bfloat16
MESH = pl.DeviceIdType.MESH

N_DEV = 8
GROUP_W = 256
N_HEADS = 4
HEAD_DIM = 64
N_CHUNKS = 13
N_ABC = 9
GMLP_CHUNK = 128
ATTN_BLOCK = 128
ATTN_FWD_BLOCKS_PER_STEP = 16
ATTN_BWD_BLOCKS_PER_STEP = 8
ATTN_DILATIONS = (1, 4, 16)
NORM_EPS = 1e-6
RG_C = 8.0
SUBLANES = 8
LANES = 128
VMEM_LIMIT = 56 * 1024 * 1024

ADAM_LR = 0.001
ADAM_B1 = 0.9
ADAM_B2 = 0.999
ADAM_EPS = 1e-08
ADAM_WD = 0.01
ADAM_STEP = 10

TM_MIX = 512
TM_MM = 512
TM_WGRAD = 1024


def _params(sem, vmem=VMEM_LIMIT):
    return pltpu.CompilerParams(dimension_semantics=sem, vmem_limit_bytes=vmem)


def _mm(a, b):
    return jnp.dot(a.astype(MXU_DTYPE), b.astype(MXU_DTYPE), preferred_element_type=F32)


def _mm_tn(a, b):
    return lax.dot_general(a.astype(MXU_DTYPE), b.astype(MXU_DTYPE), (((0,), (0,)), ((), ())),
                           preferred_element_type=F32)


def _mm_nt(a, b):
    return lax.dot_general(a.astype(MXU_DTYPE), b.astype(MXU_DTYPE), (((1,), (1,)), ((), ())),
                           preferred_element_type=F32)


def _sigmoid(x):
    return 0.5 * jnp.tanh(0.5 * x) + 0.5


def _sigmoid_small_exact(x):
    return 1.0 / (1.0 + jnp.exp(-x))


def _silu_and_grad(x):
    s = _sigmoid(x)
    return x * s, s * (1.0 + x * (1.0 - s))


_GELU_K = math.sqrt(2.0 / math.pi)
_GELU_C = 0.044715


def _gelu_and_grad(x):
    x2 = x * x
    t = jnp.tanh(_GELU_K * (x + _GELU_C * x * x2))
    val = 0.5 * x * (1.0 + t)
    grad = 0.5 * (1.0 + t) + 0.5 * x * (1.0 - t * t) * (_GELU_K * (1.0 + 3.0 * _GELU_C * x2))
    return val, grad


def _gelu(x):
    return 0.5 * x * (1.0 + jnp.tanh(_GELU_K * (x + _GELU_C * x * x * x)))


def _expm1_nonpos(u):
    poly = 1.0 / math.factorial(9)
    for k in range(8, 0, -1):
        poly = poly * u + 1.0 / math.factorial(k)
    return jnp.where(u > -0.25, poly * u, jnp.exp(u) - 1.0)


def _softplus(x):
    return jnp.maximum(x, 0.0) + jnp.log(1.0 + jnp.exp(-jnp.abs(x)))


def _shift_down(t, halo, k):
    rolled = pltpu.roll(t, k, 0)
    hr = pltpu.roll(halo, k, 0)
    row = lax.broadcasted_iota(jnp.int32, halo.shape, 0)
    first = jnp.where(row < k, hr, rolled[0:SUBLANES])
    return jnp.concatenate([first, rolled[SUBLANES:]], axis=0)


def _shift_up(t, nxt, k):
    tm = t.shape[0]
    rolled = pltpu.roll(t, tm - k, 0)
    nr = pltpu.roll(nxt, SUBLANES - k, 0)
    row = lax.broadcasted_iota(jnp.int32, nxt.shape, 0)
    last = jnp.where(row >= SUBLANES - k, nr, rolled[tm - SUBLANES:tm])
    return jnp.concatenate([rolled[:tm - SUBLANES], last], axis=0)


def _scan_fwd(a, b):
    tm = a.shape[0]
    row = lax.broadcasted_iota(jnp.int32, a.shape, 0)
    s = 1
    while s < tm:
        a_s = pltpu.roll(a, s, 0)
        b_s = pltpu.roll(b, s, 0)
        m = row >= s
        b = jnp.where(m, a * b_s + b, b)
        a = jnp.where(m, a * a_s, a)
        s *= 2
    return a, b


def _scan_rev(a, g):
    tm = a.shape[0]
    row = lax.broadcasted_iota(jnp.int32, a.shape, 0)
    s = 1
    while s < tm:
        a_s = pltpu.roll(a, tm - s, 0)
        g_s = pltpu.roll(g, tm - s, 0)
        m = row < tm - s
        g = jnp.where(m, g + a * g_s, g)
        a = jnp.where(m, a * a_s, a)
        s *= 2
    return g


def _group_rows(scr_ref, row, n_groups):
    return jnp.concatenate([scr_ref[pl.ds(c, 1), pl.ds(row, n_groups, stride=SUBLANES), :][0]
                            for c in range(scr_ref.shape[0])], axis=1)


def _spread_rows(rows_ref, n_groups, w):
    return jnp.concatenate([jnp.broadcast_to(rows_ref[g:g + 1, :], (SUBLANES, w)) for g in range(n_groups)], axis=0)


def _scan_groups(a, b, reverse):
    tm, w = a.shape
    shape3 = (tm // SUBLANES, SUBLANES, w)
    a3, b3 = a.reshape(shape3), b.reshape(shape3)
    sub = lax.broadcasted_iota(jnp.int32, shape3, 1)
    s = 1
    while s < SUBLANES:
        shift = SUBLANES - s if reverse else s
        a_s = pltpu.roll(a3, shift, 1)
        b_s = pltpu.roll(b3, shift, 1)
        m = (sub < SUBLANES - s) if reverse else (sub >= s)
        b3 = jnp.where(m, a3 * b_s + b3, b3)
        a3 = jnp.where(m, a3 * a_s, a3)
        s *= 2
    return a3.reshape(tm, w), b3.reshape(tm, w)


def _scan_fwd_tile(a, b, h_in, sa_ref, sb_ref, sc_ref):
    tm, w = a.shape
    n_groups = tm // SUBLANES
    a_loc, b_loc = _scan_groups(a, b, False)
    _put(sa_ref, a_loc)
    _put(sb_ref, b_loc)
    a_end, b_end = _scan_fwd(_group_rows(sa_ref, SUBLANES - 1, n_groups), _group_rows(sb_ref, SUBLANES - 1, n_groups))
    h_end = b_end + a_end * h_in
    sc_ref[...] = _shift_down(h_end, jnp.broadcast_to(h_in, (SUBLANES, w)), 1)
    return b_loc + a_loc * _spread_rows(sc_ref, n_groups, w), h_end


def _scan_rev_tile(a, g, sa_ref, sb_ref, sc_ref):
    tm, w = a.shape
    n_groups = tm // SUBLANES
    a_loc, g_loc = _scan_groups(a, g, True)
    _put(sa_ref, a_loc)
    _put(sb_ref, g_loc)
    d_first = _scan_rev(_group_rows(sa_ref, 0, n_groups), _group_rows(sb_ref, 0, n_groups))
    sc_ref[...] = _shift_up(d_first, jnp.zeros((SUBLANES, w), F32), 1)
    return g_loc + a_loc * _spread_rows(sc_ref, n_groups, w)


def _lane_scratch(tm, w):
    return pltpu.VMEM((w // LANES, tm, LANES), F32)


def _put(scr_ref, val):
    for c in range(scr_ref.shape[0]):
        scr_ref[c] = val[:, c * LANES:(c + 1) * LANES].astype(F32)


def _get(scr_ref):
    return jnp.concatenate([scr_ref[c] for c in range(scr_ref.shape[0])], axis=1)


MAX_ROW_STRIDE = 4


def _strided_rows(c, start, n, stride):
    return (pl.ds(c, 1), pl.ds(start, n, stride=stride), slice(None))


def _deinterleave(src_ref, dst_ref, dil, tmp_ref=None):
    nc, tm, _ = src_ref.shape
    s1 = min(dil, MAX_ROW_STRIDE)
    s2 = dil // s1
    if s2 > 1:
        for r0 in range(s1):
            for c in range(nc):
                tmp_ref[c, r0 * (tm // s1):(r0 + 1) * (tm // s1), :] = src_ref[_strided_rows(c, r0, tm // s1, s1)][0]
    for r in range(dil):
        r1, r0 = divmod(r, s1)
        for c in range(nc):
            if dil == 1:
                piece = src_ref[c]
            elif s2 == 1:
                piece = src_ref[_strided_rows(c, r, tm // dil, dil)][0]
            else:
                piece = tmp_ref[_strided_rows(c, r0 * (tm // s1) + r1, tm // dil, s2)][0]
            dst_ref[r, :, c * LANES:(c + 1) * LANES] = piece.astype(dst_ref.dtype)


def _interleave(src_ref, dst_ref, dil, tmp_ref=None):
    nc, tm, _ = dst_ref.shape
    s1 = min(dil, MAX_ROW_STRIDE)
    s2 = dil // s1
    for r in range(dil):
        r1, r0 = divmod(r, s1)
        for c in range(nc):
            piece = src_ref[r, :, c * LANES:(c + 1) * LANES].astype(F32)[None]
            if s2 == 1:
                dst_ref[_strided_rows(c, r, tm // dil, dil)] = piece
            else:
                tmp_ref[_strided_rows(c, r0 * (tm // s1) + r1, tm // dil, s2)] = piece
    if s2 > 1:
        for r0 in range(s1):
            for c in range(nc):
                dst_ref[_strided_rows(c, r0, tm // s1, s1)] = (
                    tmp_ref[c, r0 * (tm // s1):(r0 + 1) * (tm // s1), :][None])


def _dilated_spec(tm, w, dil, index=lambda i: i):
    return pl.BlockSpec((dil, tm // dil, w), lambda i: (0, index(i), 0))


def _dilated_shape(S, w, dil, dtype):
    return jax.ShapeDtypeStruct((dil, S // dil, w), dtype)


def _head_masks(shape):
    lane = lax.broadcasted_iota(jnp.int32, shape, 1)
    return [(lane >= h * HEAD_DIM) & (lane < (h + 1) * HEAD_DIM) for h in range(N_HEADS)]


def _colsum(v):
    return jnp.sum(v, axis=0, keepdims=True)


def _conv_a(z_of, halo_of, w_ref):
    p = z_of(2) * z_of(0)
    p_h = halo_of(2) * halo_of(0)
    cv = w_ref[2:3, :] * p + w_ref[1:2, :] * _shift_down(p, p_h, 1) + w_ref[0:1, :] * _shift_down(p, p_h, 2)
    return p, p_h, cv


def _lru_gates(z_of, halo_of, wr_ref, vec_ref, wa_ref, wx_ref, saved=None):
    rx = z_of(4)
    rx_h = halo_of(4)
    sh = [rx, _shift_down(rx, rx_h, 1), _shift_down(rx, rx_h, 2), _shift_down(rx, rx_h, 3)]
    xc = (wr_ref[3:4, :] * sh[0] + wr_ref[2:3, :] * sh[1] + wr_ref[1:2, :] * sh[2]
          + wr_ref[0:1, :] * sh[3] + vec_ref[0:1, :])
    ga = _sigmoid_small_exact(jnp.dot(xc.astype(MXU_DTYPE), wa_ref[...], preferred_element_type=F32) + vec_ref[1:2, :])
    gi = _sigmoid(jnp.dot(xc.astype(MXU_DTYPE), wx_ref[...], preferred_element_type=F32) + vec_ref[2:3, :])
    sp = _softplus(-vec_ref[3:4, :])
    if saved is not None:
        return (xc, sh, ga, gi) + tuple(saved) + (sp,)
    log_a = (-RG_C * ga) * sp
    a = jnp.exp(log_a)
    mult = jnp.sqrt(-_expm1_nonpos(2.0 * log_a))
    return xc, sh, ga, gi, a, mult, sp


def _gmlp_fwd(z_of, vec_ref, ws_ref, bs_ref, tm):
    u = _gelu(z_of(6))
    gv = _gelu(z_of(7))
    rr = lax.rsqrt(jnp.mean(gv * gv, axis=-1, keepdims=True) + NORM_EPS)
    vn = (gv * rr) * vec_ref[4:5, :]
    masks = _head_masks((GMLP_CHUNK, GROUP_W))
    parts = []
    for c in range(tm // GMLP_CHUNK):
        vc = vn[c * GMLP_CHUNK:(c + 1) * GMLP_CHUNK].astype(MXU_DTYPE)
        acc = bs_ref[...]
        for h in range(N_HEADS):
            acc = acc + jnp.where(masks[h], jnp.dot(ws_ref[h], vc, preferred_element_type=F32), 0.0)
        parts.append(acc)
    return u, gv, rr, vn, jnp.concatenate(parts, axis=0)


def _mix_specs():
    const2 = lambda shape: pl.BlockSpec(shape, lambda i: (0, 0))
    return [const2((SUBLANES, GROUP_W)), const2((SUBLANES, GROUP_W)), const2((SUBLANES, GROUP_W)),
            const2((GROUP_W, GROUP_W)), const2((GROUP_W, GROUP_W)),
            pl.BlockSpec((N_HEADS, GMLP_CHUNK, GMLP_CHUNK), lambda i: (0, 0, 0)),
            const2((GMLP_CHUNK, GROUP_W))]


def _inproj_mix_fwd(x, g, w_t, mp, name):
    S, D = x.shape
    N = w_t.shape[0]
    tm = TM_MIX
    hb = tm // SUBLANES
    n_abc = N_ABC * GROUP_W
    n_qkv = 3 * GROUP_W

    def body(x_ref, g_ref, w_ref, wA_ref, wR_ref, vec_ref, wa_ref, wx_ref, ws_ref, bs_ref,
             z_ref, zg_ref, q1_ref, q4_ref, q16_ref, y_ref, h_ref, a_ref, mult_ref,
             qkv_ref, halo_ref, carry_ref, sa_ref, sb_ref, sc_ref, tmp_ref):
        @pl.when(pl.program_id(0) == 0)
        def _():
            halo_ref[...] = jnp.zeros_like(halo_ref)
            carry_ref[...] = jnp.zeros_like(carry_ref)

        xv = x_ref[...]
        r = lax.rsqrt(jnp.mean(xv * xv, axis=-1, keepdims=True) + NORM_EPS)
        hn = ((xv * r) * g_ref[...]).astype(MXU_DTYPE)
        z_ref[...] = _mm_nt(hn, w_ref[0:n_abc, :])
        _put(qkv_ref, _mm_nt(hn, w_ref[n_abc:n_abc + n_qkv, :]))
        zg_ref[...] = _mm_nt(hn, w_ref[n_abc + n_qkv:, :])
        for dil, ref in zip(ATTN_DILATIONS, (q1_ref, q4_ref, q16_ref)):
            _deinterleave(qkv_ref, ref, dil, tmp_ref)

        z_of = lambda c: z_ref[:, c * GROUP_W:(c + 1) * GROUP_W]
        halo_of = lambda c: halo_ref[:, c * GROUP_W:(c + 1) * GROUP_W]

        _, _, cv = _conv_a(z_of, halo_of, wA_ref)
        y_ref[:, 0:GROUP_W] = (z_of(1) * cv * _silu_and_grad(z_of(3))[0]).astype(y_ref.dtype)

        xc, _, _, gi, a, mult, _ = _lru_gates(z_of, halo_of, wR_ref, vec_ref, wa_ref, wx_ref)
        a_ref[...] = a
        mult_ref[...] = mult
        b = mult * (gi * xc)
        h, h_end = _scan_fwd_tile(a, b, carry_ref[SUBLANES - 1:SUBLANES, :], sa_ref, sb_ref, sc_ref)
        h_ref[...] = h
        carry_ref[...] = h_end[hb - SUBLANES:hb]
        y_ref[:, GROUP_W:2 * GROUP_W] = (h * _silu_and_grad(z_of(5))[0]).astype(y_ref.dtype)

        u, _, _, _, sp = _gmlp_fwd(z_of, vec_ref, ws_ref, bs_ref, tm)
        y_ref[:, 2 * GROUP_W:3 * GROUP_W] = (u * sp * _silu_and_grad(z_of(8))[0]).astype(y_ref.dtype)
        halo_ref[...] = z_ref[tm - SUBLANES:tm, :]

    row = lambda wd: pl.BlockSpec((tm, wd), lambda i: (i, 0))
    return pl.pallas_call(
        body, name=name, grid=(S // tm,),
        in_specs=[row(D), pl.BlockSpec((1, D), lambda i: (0, 0)),
                  pl.BlockSpec((N, D), lambda i: (0, 0), pipeline_mode=pl.Buffered(1))] + _mix_specs(),
        out_specs=[row(n_abc), row(GROUP_W)] + [_dilated_spec(tm, n_qkv, dil) for dil in ATTN_DILATIONS]
                  + [row(3 * GROUP_W)] + [row(GROUP_W)] * 3,
        out_shape=[jax.ShapeDtypeStruct((S, n_abc), F32), jax.ShapeDtypeStruct((S, GROUP_W), F32)]
                  + [_dilated_shape(S, n_qkv, dil, MXU_DTYPE) for dil in ATTN_DILATIONS]
                  + [jax.ShapeDtypeStruct((S, 3 * GROUP_W), MXU_DTYPE)] + [jax.ShapeDtypeStruct((S, GROUP_W), F32)] * 3,
        scratch_shapes=[_lane_scratch(tm, n_qkv), pltpu.VMEM((SUBLANES, n_abc), F32),
                        pltpu.VMEM((SUBLANES, GROUP_W), F32), _lane_scratch(tm, GROUP_W), _lane_scratch(tm, GROUP_W),
                        pltpu.VMEM((hb, GROUP_W), F32), _lane_scratch(tm, n_qkv)],
        compiler_params=_params(("arbitrary",)),
    )(x, g, w_t, mp["wA"], mp["wR"], mp["vec"], mp["wa"], mp["wx"], mp["ws"], mp["bs"])


_NEG = -1e30


def _slope(h):
    return 2.0 ** (-8.0 * (h + 1) / N_HEADS)


def _attn_bias(dil, offsets, n_keys):
    shape = (ATTN_BLOCK, n_keys)
    qi = lax.broadcasted_iota(jnp.int32, shape, 0)
    ki = lax.broadcasted_iota(jnp.int32, shape, 1)
    blocks = []
    for f in offsets:
        delta = qi + f - ki
        valid = (delta >= 0) & (delta <= ATTN_BLOCK)
        dist = (delta * dil).astype(F32)
        for h in range(N_HEADS):
            blocks.append(jnp.where(valid, -_slope(h) * dist, _NEG))
    return jnp.concatenate(blocks, axis=0)


def _stack_heads(t, masks):
    return jnp.concatenate([jnp.where(m, t, jnp.zeros_like(t)) for m in masks], axis=0)


def _unstack_heads(t4, masks, base=0):
    out = t4[base * ATTN_BLOCK:(base + 1) * ATTN_BLOCK]
    for h in range(1, N_HEADS):
        out = jnp.where(masks[h], t4[(base + h) * ATTN_BLOCK:(base + h + 1) * ATTN_BLOCK], out)
    return out


def _group_starts(n, per_step, group):
    if per_step % group == 0:
        return (lambda j: j % group == 0), True
    steps = group // per_step
    return (lambda j: (n % steps == 0) if j == 0 else False), (n + 1) % steps == 0


def _attn_fwd(qkv, dil, name):
    S = qkv.shape[0] * qkv.shape[1]
    qkv = qkv.reshape(S, qkv.shape[2])
    nb = S // ATTN_BLOCK
    group = nb // dil
    scale = 1.0 / math.sqrt(HEAD_DIM)
    B = ATTN_BLOCK
    per_step = ATTN_FWD_BLOCKS_PER_STEP

    def body(q_ref, kc_ref, kp_ref, vc_ref, vp_ref, o_ref, l_ref, bias_ref):
        n = pl.program_id(0)

        @pl.when(n == 0)
        def _():
            bias_ref[...] = _attn_bias(dil, (B,), 2 * B)

        masks = _head_masks((B, GROUP_W))
        starts, _ = _group_starts(n, per_step, group)
        for j in range(per_step):
            own = slice(j * B, (j + 1) * B)
            before = slice((j - 1) * B, j * B)
            qs = _stack_heads(q_ref[own], masks)
            keys = jnp.concatenate([kp_ref[...] if j == 0 else kc_ref[before], kc_ref[own]], axis=0)
            vals = jnp.concatenate([vp_ref[...] if j == 0 else vc_ref[before], vc_ref[own]], axis=0)
            s = _mm_nt(qs, keys) * scale + bias_ref[...]
            if starts(j) is not False:
                key_col = lax.broadcasted_iota(jnp.int32, s.shape, 1)
                s = jnp.where(starts(j) & (key_col < B), _NEG, s)
            m = jnp.max(s, axis=-1, keepdims=True)
            p = jnp.exp(s - m)
            l = jnp.sum(p, axis=-1, keepdims=True)
            o4 = jnp.dot(p.astype(MXU_DTYPE), vals, preferred_element_type=F32)
            o_ref[own] = (_unstack_heads(o4, masks)
                          / _unstack_heads(jnp.broadcast_to(l, o4.shape), masks)).astype(o_ref.dtype)
            l_ref[own] = _unstack_heads(jnp.broadcast_to(m + jnp.log(l), o4.shape), masks)

    blk = (per_step * B, GROUP_W)
    cur = lambda c: pl.BlockSpec(blk, lambda n: (n, c))
    prev = lambda c: pl.BlockSpec((B, GROUP_W), lambda n: (jnp.maximum(n * per_step - 1, 0), c))
    out = pl.BlockSpec(blk, lambda n: (n, 0))
    o, l = pl.pallas_call(
        body, name=name, grid=(nb // per_step,),
        in_specs=[cur(0), cur(1), prev(1), cur(2), prev(2)],
        out_specs=[out, out],
        out_shape=[jax.ShapeDtypeStruct((S, GROUP_W), MXU_DTYPE), jax.ShapeDtypeStruct((S, GROUP_W), F32)],
        scratch_shapes=[pltpu.VMEM((N_HEADS * ATTN_BLOCK, 2 * ATTN_BLOCK), F32)],
        compiler_params=_params(("arbitrary",)),
    )(qkv, qkv, qkv, qkv, qkv)
    return o.reshape(dil, S // dil, GROUP_W), l.reshape(dil, S // dil, GROUP_W)


def _outproj(x, z_g, y_abc, attn, w_out, name):
    S, D = x.shape
    tm = TM_MM
    n_abc = 3 * GROUP_W

    def body(x_ref, g_ref, yabc_ref, o1, l1, o2, l2, o3, l3, w_ref,
             xn_ref, y_ref, o_ref, lse1_ref, lse4_ref, lse16_ref, so2, sl2, so3, sl3, slse, tmp_ref):
        for src, dst, dil in ((o2, so2, ATTN_DILATIONS[1]), (l2, sl2, ATTN_DILATIONS[1]),
                              (o3, so3, ATTN_DILATIONS[2]), (l3, sl3, ATTN_DILATIONS[2])):
            _interleave(src, dst, dil, tmp_ref)
        la, lb, lc = l1[0], _get(sl2), _get(sl3)
        mx = jnp.maximum(jnp.maximum(la, lb), lc)
        ea, eb, ec = jnp.exp(la - mx), jnp.exp(lb - mx), jnp.exp(lc - mx)
        den = ea + eb + ec
        o = (ea * o1[0].astype(F32) + eb * _get(so2) + ec * _get(so3)) / den
        o_ref[...] = o
        _put(slse, mx + jnp.log(den))
        for dil, ref in zip(ATTN_DILATIONS, (lse1_ref, lse4_ref, lse16_ref)):
            _deinterleave(slse, ref, dil, tmp_ref)
        y_d = o * _silu_and_grad(g_ref[...])[0]
        y_ref[:, 0:n_abc] = yabc_ref[...].astype(MXU_DTYPE)
        y_ref[:, n_abc:] = y_d.astype(MXU_DTYPE)
        xn_ref[...] = x_ref[...] + jnp.dot(y_ref[...], w_ref[...], preferred_element_type=F32)

    row = lambda w: pl.BlockSpec((tm, w), lambda i: (i, 0))
    dil_specs = [_dilated_spec(tm, GROUP_W, dil) for dil in ATTN_DILATIONS]
    (o1, l1), (o2, l2), (o3, l3) = attn
    return pl.pallas_call(
        body, name=name, grid=(S // tm,),
        in_specs=[row(D), row(GROUP_W), row(n_abc)] + [sp for sp in dil_specs for _ in range(2)]
                 + [pl.BlockSpec(w_out.shape, lambda i: (0, 0))],
        out_specs=[row(D), row(4 * GROUP_W), row(GROUP_W)] + dil_specs,
        out_shape=[jax.ShapeDtypeStruct((S, D), F32), jax.ShapeDtypeStruct((S, 4 * GROUP_W), MXU_DTYPE),
                   jax.ShapeDtypeStruct((S, GROUP_W), F32)]
                  + [_dilated_shape(S, GROUP_W, dil, F32) for dil in ATTN_DILATIONS],
        scratch_shapes=[_lane_scratch(tm, GROUP_W)] * 6,
        compiler_params=_params(("parallel",)),
    )(x, z_g, y_abc, o1, l1, o2, l2, o3, l3, w_out)


def _loss_head(x, g, target, name):
    S, D = x.shape
    tm = TM_MM

    def body(x_ref, g_ref, t_ref, dx_ref, loss_ref, dg_ref):
        i = pl.program_id(0)

        @pl.when(i == 0)
        def _():
            loss_ref[...] = jnp.zeros_like(loss_ref)
            dg_ref[...] = jnp.zeros_like(dg_ref)

        xv = x_ref[...]
        r = lax.rsqrt(jnp.mean(xv * xv, axis=-1, keepdims=True) + NORM_EPS)
        xn = xv * r
        err = xn * g_ref[...] - t_ref[...]
        per_tok = jnp.mean(err * err, axis=-1, keepdims=True)
        loss_ref[...] += 0.5 * jnp.sum(per_tok, axis=0, keepdims=True)
        dout = err * (1.0 / D)
        dg_ref[...] += _colsum(dout * xn)
        dxn = dout * g_ref[...]
        dx_ref[...] = r * (dxn - xn * jnp.mean(dxn * xn, axis=-1, keepdims=True))

    row = pl.BlockSpec((tm, D), lambda i: (i, 0))
    return pl.pallas_call(
        body, name=name, grid=(S // tm,),
        in_specs=[row, pl.BlockSpec((1, D), lambda i: (0, 0)), row],
        out_specs=[row, pl.BlockSpec((1, LANES), lambda i: (0, 0)), pl.BlockSpec((1, D), lambda i: (0, 0))],
        out_shape=[jax.ShapeDtypeStruct((S, D), F32), jax.ShapeDtypeStruct((1, LANES), F32),
                   jax.ShapeDtypeStruct((1, D), F32)],
        compiler_params=_params(("arbitrary",)),
    )(x, g, target)


def _outproj_mix_bwd(dx, y, w_out, z, z_g, hs, lru_a, lru_mult, o, mp, name):
    S, D = dx.shape
    E = y.shape[1]
    tm = TM_MIX
    hb = tm // SUBLANES
    nT = S // tm
    last_blk = S // SUBLANES - 1
    wcols = N_ABC * GROUP_W

    def body(dx_ref, y_ref, w_ref, z_ref, zh_ref, zn_ref, zg_ref, h_ref, hh_ref, a_ref, mult_ref, o_ref,
             wA_ref, wR_ref, vec_ref, wa_ref, wx_ref, ws_ref, bs_ref,
             dw_ref, dz_ref, dzg_ref, do1_ref, do4_ref, do16_ref, dl1_ref, dl4_ref, dl16_ref,
             dwA_ref, dwR_ref, dvec_ref, dwa_ref, dwx_ref, dws_ref, dbs_ref,
             hcarry_ref, xcarry_ref, bsacc_ref, do_ref, dl_ref, sa_ref, sb_ref, sc_ref, dy_ref, dyn_ref, acc_ref,
             tmp_ref):
        i = pl.program_id(0)
        ti = nT - 1 - i

        @pl.when(i == 0)
        def _():
            acc_ref[...] = jnp.zeros_like(acc_ref)
            dyn_ref[...] = jnp.zeros_like(dyn_ref)
            hcarry_ref[...] = jnp.zeros_like(hcarry_ref)
            xcarry_ref[...] = jnp.zeros_like(xcarry_ref)
            bsacc_ref[...] = jnp.zeros_like(bsacc_ref)
            dwA_ref[...] = jnp.zeros_like(dwA_ref)
            dwR_ref[...] = jnp.zeros_like(dwR_ref)
            dvec_ref[...] = jnp.zeros_like(dvec_ref)
            dwa_ref[...] = jnp.zeros_like(dwa_ref)
            dwx_ref[...] = jnp.zeros_like(dwx_ref)
            dws_ref[...] = jnp.zeros_like(dws_ref)
            dbs_ref[...] = jnp.zeros_like(dbs_ref)

        dxb = dx_ref[...].astype(MXU_DTYPE)
        dy_ref[...] = _mm_nt(dxb, w_ref[...])
        acc_ref[...] += _mm_tn(y_ref[...], dxb)

        @pl.when(i == nT - 1)
        def _():
            dw_ref[...] = acc_ref[...].astype(dw_ref.dtype)

        has_prev = ti > 0
        has_next = i > 0
        col = lambda c: slice(c * GROUP_W, (c + 1) * GROUP_W)
        z_of = lambda c: z_ref[:, col(c)]
        halo_of = lambda c: jnp.where(has_prev, zh_ref[:, col(c)], 0.0)
        next_of = lambda c: zn_ref[:, col(c)]

        p, p_h, cv = _conv_a(z_of, halo_of, wA_ref)
        sg, dsg = _silu_and_grad(z_of(3))
        a_b = z_of(1)
        dya = dy_ref[:, col(0)]
        dcv = dya * a_b * sg
        dcv_n = jnp.where(has_next, dyn_ref[...] * next_of(1) * _silu_and_grad(next_of(3))[0], 0.0)
        dp = (wA_ref[2:3, :] * dcv + wA_ref[1:2, :] * _shift_up(dcv, dcv_n, 1)
              + wA_ref[0:1, :] * _shift_up(dcv, dcv_n, 2))
        dwA_ref[2:3, :] += _colsum(dcv * p)
        dwA_ref[1:2, :] += _colsum(dcv * _shift_down(p, p_h, 1))
        dwA_ref[0:1, :] += _colsum(dcv * _shift_down(p, p_h, 2))
        def put_dz(c, val):
            dz_ref[:, col(c)] = val.astype(dz_ref.dtype)

        put_dz(0, dp * z_of(2))
        put_dz(1, dya * cv * sg)
        put_dz(2, dp * z_of(0))
        put_dz(3, dya * a_b * cv * dsg)

        xc, sh, ga, gi, a, mult, sp = _lru_gates(z_of, halo_of, wR_ref, vec_ref, wa_ref, wx_ref,
                                                 saved=(a_ref[...], mult_ref[...]))
        h = h_ref[...]
        h_prev = _shift_down(h, jnp.where(has_prev, hh_ref[...], 0.0), 1)
        sgr, dsgr = _silu_and_grad(z_of(5))
        dyb = dy_ref[:, col(1)]
        put_dz(5, dyb * h * dsgr)
        row = lax.broadcasted_iota(jnp.int32, (tm, GROUP_W), 0)
        g_in = dyb * sgr + jnp.where(row == tm - 1, hcarry_ref[0:1, :], 0.0)
        a_up = _shift_up(a, jnp.zeros((SUBLANES, GROUP_W), F32), 1)
        dH = _scan_rev_tile(a_up, g_in, sa_ref, sb_ref, sc_ref)
        hcarry_ref[...] = (a * dH)[0:SUBLANES]
        da = dH * h_prev
        gx = gi * xc
        dmult = dH * gx
        dgi = dH * mult * xc
        dxc = dH * mult * gi
        dlog_a = da * a - dmult * (a * a) / mult
        dga = dlog_a * (-RG_C * sp)
        dlam_row = _colsum(dlog_a * (-RG_C * ga)) * (-_sigmoid(-vec_ref[3:4, :]))
        dpre_a = dga * ga * (1.0 - ga)
        dpre_i = dgi * gi * (1.0 - gi)
        dwa_ref[...] += _mm_tn(xc, dpre_a)
        dwx_ref[...] += _mm_tn(xc, dpre_i)
        dxc = dxc + _mm_nt(dpre_a, wa_ref[...]) + _mm_nt(dpre_i, wx_ref[...])
        dvec_ref[0:1, :] += _colsum(dxc)
        dvec_ref[1:2, :] += _colsum(dpre_a)
        dvec_ref[2:3, :] += _colsum(dpre_i)
        dvec_ref[3:4, :] += dlam_row
        for k in range(4):
            dwR_ref[k:k + 1, :] += _colsum(dxc * sh[3 - k])
        dxc_n = xcarry_ref[...]
        put_dz(4, wR_ref[3:4, :] * dxc + wR_ref[2:3, :] * _shift_up(dxc, dxc_n, 1)
               + wR_ref[1:2, :] * _shift_up(dxc, dxc_n, 2) + wR_ref[0:1, :] * _shift_up(dxc, dxc_n, 3))
        xcarry_ref[...] = dxc[0:SUBLANES]

        c_u, c_v = z_of(6), z_of(7)
        u, du_dx = _gelu_and_grad(c_u)
        gv, dgv_dx = _gelu_and_grad(c_v)
        rr = lax.rsqrt(jnp.mean(gv * gv, axis=-1, keepdims=True) + NORM_EPS)
        xhat = gv * rr
        g_c = vec_ref[4:5, :]
        vn = xhat * g_c
        masks = _head_masks((GMLP_CHUNK, GROUP_W))
        tri_r = lax.broadcasted_iota(jnp.int32, (GMLP_CHUNK, GMLP_CHUNK), 0)
        tri_c = lax.broadcasted_iota(jnp.int32, (GMLP_CHUNK, GMLP_CHUNK), 1)
        tril = tri_r >= tri_c
        sgc, dsgc = _silu_and_grad(z_of(8))
        dyc = dy_ref[:, col(2)]
        dsp_full = dyc * u * sgc
        sp_parts, dvn_parts = [], []
        for c in range(tm // GMLP_CHUNK):
            rs = slice(c * GMLP_CHUNK, (c + 1) * GMLP_CHUNK)
            vc = vn[rs].astype(MXU_DTYPE)
            dsp_c = dsp_full[rs]
            bsacc_ref[...] += dsp_c
            acc = bs_ref[...]
            dvn_c = jnp.zeros((GMLP_CHUNK, GROUP_W), F32)
            for h in range(N_HEADS):
                w_h = ws_ref[h]
                acc = acc + jnp.where(masks[h], jnp.dot(w_h, vc, preferred_element_type=F32), 0.0)
                dsp_h = jnp.where(masks[h], dsp_c, 0.0).astype(MXU_DTYPE)
                dvn_c = dvn_c + _mm_tn(w_h, dsp_h)
                dws_ref[h] += jnp.where(tril, _mm_nt(dsp_h, vc), 0.0)
            sp_parts.append(acc)
            dvn_parts.append(dvn_c)
        spv = jnp.concatenate(sp_parts, axis=0)
        dvn = jnp.concatenate(dvn_parts, axis=0)
        put_dz(6, dyc * spv * sgc * du_dx)
        put_dz(8, dyc * u * spv * dsgc)
        dvec_ref[4:5, :] += _colsum(dvn * xhat)
        dgvn = dvn * g_c
        dgv = rr * (dgvn - xhat * jnp.mean(dgvn * xhat, axis=-1, keepdims=True))
        put_dz(7, dgv * dgv_dx)

        sgd, dsgd = _silu_and_grad(zg_ref[...])
        dyd = dy_ref[:, col(3)]
        ov = o_ref[...]
        do = dyd * sgd
        _put(do_ref, do)
        dzg_ref[...] = (dyd * ov * dsgd).astype(dzg_ref.dtype)
        prod = do * ov
        tmasks = _head_masks((tm, GROUP_W))
        dl = jnp.zeros((tm, GROUP_W), F32)
        for h in range(N_HEADS):
            dl = jnp.where(tmasks[h], jnp.sum(jnp.where(tmasks[h], prod, 0.0), axis=-1, keepdims=True), dl)
        _put(dl_ref, dl)
        for dil, d_out, l_out in zip(ATTN_DILATIONS, (do1_ref, do4_ref, do16_ref), (dl1_ref, dl4_ref, dl16_ref)):
            _deinterleave(do_ref, d_out, dil, tmp_ref)
            _deinterleave(dl_ref, l_out, dil, tmp_ref)

        @pl.when(i == nT - 1)
        def _():
            acc = bsacc_ref[...]
            lane = lax.broadcasted_iota(jnp.int32, (GMLP_CHUNK, LANES), 1)
            out = jnp.zeros((GMLP_CHUNK, LANES), F32)
            for h in range(N_HEADS):
                out = jnp.where(lane == h, jnp.sum(jnp.where(masks[h], acc, 0.0), axis=-1, keepdims=True), out)
            dbs_ref[...] = out

        dyn_ref[...] = dy_ref[0:SUBLANES, 0:GROUP_W]

    rev = lambda w: pl.BlockSpec((tm, w), lambda i: (nT - 1 - i, 0))
    prev8 = lambda w: pl.BlockSpec((SUBLANES, w), lambda i: (jnp.maximum((nT - 1 - i) * hb - 1, 0), 0))
    next8 = lambda w: pl.BlockSpec((SUBLANES, w), lambda i: (jnp.minimum((nT - i) * hb, last_blk), 0))
    const2 = lambda shape: pl.BlockSpec(shape, lambda i: (0, 0))
    dil_specs = [_dilated_spec(tm, GROUP_W, dil, lambda i: nT - 1 - i) for dil in ATTN_DILATIONS]
    dil_shapes = [_dilated_shape(S, GROUP_W, dil, F32) for dil in ATTN_DILATIONS]
    small = (SUBLANES, GROUP_W)
    sq = (GROUP_W, GROUP_W)
    ws_shape = (N_HEADS, GMLP_CHUNK, GMLP_CHUNK)
    return pl.pallas_call(
        body, name=name, grid=(nT,),
        in_specs=[rev(D), rev(E), pl.BlockSpec((E, D), lambda i: (0, 0), pipeline_mode=pl.Buffered(1)),
                  rev(wcols), prev8(wcols), next8(wcols), rev(GROUP_W), rev(GROUP_W), prev8(GROUP_W),
                  rev(GROUP_W), rev(GROUP_W), rev(GROUP_W)]
                 + _mix_specs(),
        out_specs=[const2((E, D)), rev(wcols), rev(GROUP_W)] + dil_specs + dil_specs
                  + [const2(small), const2(small), const2(small), const2(sq), const2(sq),
                     pl.BlockSpec(ws_shape, lambda i: (0, 0, 0)), const2((GMLP_CHUNK, LANES))],
        out_shape=[jax.ShapeDtypeStruct((E, D), WIRE_DTYPE),
                   jax.ShapeDtypeStruct((S, wcols), MXU_DTYPE), jax.ShapeDtypeStruct((S, GROUP_W), MXU_DTYPE)]
                  + [_dilated_shape(S, GROUP_W, dil, MXU_DTYPE) for dil in ATTN_DILATIONS] + dil_shapes
                  + [jax.ShapeDtypeStruct(small, F32)] * 3 + [jax.ShapeDtypeStruct(sq, F32)] * 2
                  + [jax.ShapeDtypeStruct(ws_shape, F32), jax.ShapeDtypeStruct((GMLP_CHUNK, LANES), F32)],
        scratch_shapes=[pltpu.VMEM(small, F32), pltpu.VMEM(small, F32), pltpu.VMEM((GMLP_CHUNK, GROUP_W), F32),
                        _lane_scratch(tm, GROUP_W), _lane_scratch(tm, GROUP_W),
                        _lane_scratch(tm, GROUP_W), _lane_scratch(tm, GROUP_W), pltpu.VMEM((hb, GROUP_W), F32),
                        pltpu.VMEM((tm, E), F32), pltpu.VMEM(small, F32), pltpu.VMEM((E, D), F32),
                        _lane_scratch(tm, GROUP_W)],
        compiler_params=_params(("arbitrary",)),
    )(dx, y, w_out, z, z, z, z_g, hs, hs, lru_a, lru_mult, o, mp["wA"], mp["wR"], mp["vec"], mp["wa"], mp["wx"], mp["ws"], mp["bs"])


def _attn_bwd(qkv, do, lse, delta, dil, name):
    S = qkv.shape[0] * qkv.shape[1]
    flat = lambda t: t.reshape(S, t.shape[2])
    qkv, do, lse, delta = flat(qkv), flat(do), flat(lse), flat(delta)
    nb = S // ATTN_BLOCK
    group = nb // dil
    scale = 1.0 / math.sqrt(HEAD_DIM)
    B = ATTN_BLOCK
    per_step = ATTN_BWD_BLOCKS_PER_STEP
    n_steps = nb // per_step

    def body(qc_ref, qn_ref, kc_ref, kp_ref, vc_ref, vp_ref, doc_ref, don_ref, lc_ref, ln_ref, dc_ref, dn_ref,
             dq_ref, dk_ref, dv_ref, bias_ref, bias_next_ref):
        n = pl.program_id(0)
        starts, next_starts = _group_starts(n, per_step, group)

        @pl.when(n == 0)
        def _():
            bias_ref[...] = _attn_bias(dil, (B,), 2 * B)
            bias_next_ref[...] = _attn_bias(dil, (B,), B)

        masks = _head_masks((B, GROUP_W))

        def per_row(tile):
            return jnp.concatenate([jnp.max(jnp.where(masks[h], tile, _NEG), axis=-1, keepdims=True)
                                    for h in range(N_HEADS)], axis=0)

        def grads(q, dov, lse_tile, dl_tile, keys, vals, bias, dead):
            qs = _stack_heads(q, masks)
            dos = _stack_heads(dov.astype(MXU_DTYPE), masks)
            s = _mm_nt(qs, keys) * scale + bias
            if dead is not None:
                s = jnp.where(dead(s.shape), _NEG, s)
            p = jnp.exp(s - per_row(lse_tile))
            ds = (p * (_mm_nt(dos, vals) - per_row(dl_tile)) * scale).astype(MXU_DTYPE)
            return ds, _mm_tn(ds, qs), _mm_tn(p.astype(MXU_DTYPE), dos)

        for j in range(per_step):
            own = slice(j * B, (j + 1) * B)
            before = slice((j - 1) * B, j * B)
            keys = jnp.concatenate([kp_ref[...] if j == 0 else kc_ref[before], kc_ref[own]], axis=0)
            vals = jnp.concatenate([vp_ref[...] if j == 0 else vc_ref[before], vc_ref[own]], axis=0)
            dead = None
            if starts(j) is not False:
                dead = lambda shape, j=j: starts(j) & (lax.broadcasted_iota(jnp.int32, shape, 1) < B)
            ds, dk2, dv2 = grads(qc_ref[own], doc_ref[own], lc_ref[own], dc_ref[own], keys, vals, bias_ref[...], dead)
            dq_ref[own] = _unstack_heads(jnp.dot(ds, keys, preferred_element_type=F32), masks).astype(dq_ref.dtype)
            if j > 0:
                dk_ref[before] = (dk_own + dk2[:B]).astype(dk_ref.dtype)
                dv_ref[before] = (dv_own + dv2[:B]).astype(dv_ref.dtype)
            dk_own, dv_own = dk2[B:], dv2[B:]
        last = slice((per_step - 1) * B, per_step * B)
        if next_starts is not True:
            _, dk1, dv1 = grads(qn_ref[...], don_ref[...], ln_ref[...], dn_ref[...], kc_ref[last], vc_ref[last],
                                bias_next_ref[...], lambda shape: next_starts)
            dk_own, dv_own = dk_own + dk1, dv_own + dv1
        dk_ref[last] = dk_own.astype(dk_ref.dtype)
        dv_ref[last] = dv_own.astype(dv_ref.dtype)

    blk = (per_step * B, GROUP_W)
    one = (B, GROUP_W)
    nxt_idx = lambda n: jnp.minimum((n + 1) * per_step, nb - 1)
    prv_idx = lambda n: jnp.maximum(n * per_step - 1, 0)
    zcur = lambda c: pl.BlockSpec(blk, lambda n: (n, c))
    znext = lambda c: pl.BlockSpec(one, lambda n: (nxt_idx(n), c))
    zprev = lambda c: pl.BlockSpec(one, lambda n: (prv_idx(n), c))
    cur = pl.BlockSpec(blk, lambda n: (n, 0))
    nxt = pl.BlockSpec(one, lambda n: (nxt_idx(n), 0))
    grads_out = pl.pallas_call(
        body, name=name, grid=(n_steps,),
        in_specs=[zcur(0), znext(0), zcur(1), zprev(1), zcur(2), zprev(2), cur, nxt, cur, nxt, cur, nxt],
        out_specs=[cur, cur, cur],
        out_shape=[jax.ShapeDtypeStruct((S, GROUP_W), WIRE_DTYPE)] * 3,
        scratch_shapes=[pltpu.VMEM((N_HEADS * B, 2 * B), F32), pltpu.VMEM((N_HEADS * B, B), F32)],
        compiler_params=_params(("arbitrary",)),
    )(qkv, qkv, qkv, qkv, qkv, qkv, do, do, lse, lse, delta, delta)
    return [t.reshape(dil, S // dil, GROUP_W) for t in grads_out]


def _inproj_bwd(x, g, dxn, dz_abc, dqkv, dz_g, w_t, name):
    S, D = x.shape
    N = w_t.shape[0]
    tm = TM_MM
    n_abc = N_ABC * GROUP_W

    def body(x_ref, g_ref, dxn_ref, dabc_ref, q1, k1, v1, q2, k2, v2, q3, k3, v3, dg_ref, w_ref,
             dx_ref, dz_ref, h_ref, dgn_ref, s4_ref, s16_ref, tmp_ref):
        i = pl.program_id(0)

        @pl.when(i == 0)
        def _():
            dgn_ref[...] = jnp.zeros_like(dgn_ref)

        dz_ref[:, 0:n_abc] = dabc_ref[...].astype(MXU_DTYPE)
        for j, parts in enumerate(((q1, q2, q3), (k1, k2, k3), (v1, v2, v3))):
            c0 = n_abc + j * GROUP_W
            _interleave(parts[1], s4_ref, ATTN_DILATIONS[1])
            _interleave(parts[2], s16_ref, ATTN_DILATIONS[2], tmp_ref)
            dz_ref[:, c0:c0 + GROUP_W] = (parts[0][0] + _get(s4_ref) + _get(s16_ref)).astype(MXU_DTYPE)
        dz_ref[:, n_abc + 3 * GROUP_W:] = dg_ref[...].astype(MXU_DTYPE)
        dh = jnp.dot(dz_ref[...], w_ref[...], preferred_element_type=F32)
        xv = x_ref[...]
        r = lax.rsqrt(jnp.mean(xv * xv, axis=-1, keepdims=True) + NORM_EPS)
        xn = xv * r
        gv = g_ref[...]
        h_ref[...] = (xn * gv).astype(MXU_DTYPE)
        dgn_ref[...] += _colsum(dh * xn)
        dn = dh * gv
        dx_ref[...] = dxn_ref[...] + r * (dn - xn * jnp.mean(dn * xn, axis=-1, keepdims=True))

    row = lambda w: pl.BlockSpec((tm, w), lambda i: (i, 0))
    flat = [t for p in dqkv for t in p]
    dil_specs = [_dilated_spec(tm, GROUP_W, dil) for dil in ATTN_DILATIONS for _ in range(3)]
    return pl.pallas_call(
        body, name=name, grid=(S // tm,),
        in_specs=[row(D), pl.BlockSpec((1, D), lambda i: (0, 0)), row(D), row(n_abc)] + dil_specs
                 + [row(GROUP_W), pl.BlockSpec((N, D), lambda i: (0, 0), pipeline_mode=pl.Buffered(1))],
        out_specs=[row(D), row(N), row(D), pl.BlockSpec((1, D), lambda i: (0, 0))],
        out_shape=[jax.ShapeDtypeStruct((S, D), F32), jax.ShapeDtypeStruct((S, N), MXU_DTYPE),
                   jax.ShapeDtypeStruct((S, D), MXU_DTYPE), jax.ShapeDtypeStruct((1, D), F32)],
        scratch_shapes=[_lane_scratch(tm, GROUP_W)] * 3,
        compiler_params=_params(("arbitrary",)),
    )(x, g, dxn, dz_abc, *flat, dz_g, w_t)


def _inproj_wgrad(h, dz, name):
    S, D = h.shape
    N = dz.shape[1]
    tm = TM_WGRAD
    nj = 2
    cw = N // nj
    per = N_DEV // nj
    n_loc = N // N_DEV

    def body(h_ref, dz_ref, dw_ref, acc_ref):
        i = pl.program_id(1)

        @pl.when(i == 0)
        def _():
            acc_ref[...] = jnp.zeros_like(acc_ref)

        acc_ref[...] += _mm_tn(dz_ref[...], h_ref[...])

        @pl.when(i == S // tm - 1)
        def _():
            for b in range(per):
                dw_ref[b] = acc_ref[b * n_loc:(b + 1) * n_loc, :].astype(dw_ref.dtype)

    return pl.pallas_call(
        body, name=name, grid=(nj, S // tm),
        in_specs=[pl.BlockSpec((tm, D), lambda j, i: (i, 0)), pl.BlockSpec((tm, cw), lambda j, i: (i, j))],
        out_specs=pl.BlockSpec((per, n_loc, D), lambda j, i: (j, 0, 0)),
        out_shape=jax.ShapeDtypeStruct((N_DEV, n_loc, D), WIRE_DTYPE),
        scratch_shapes=[pltpu.VMEM((cw, D), F32)],
        compiler_params=_params(("parallel", "arbitrary")),
    )(h, dz)


def _my_place():
    return lax.axis_index("x"), lax.axis_index("y"), lax.axis_index("c")


def _peer(x, y, c, k):
    px = 1 - x if k & 4 else x
    py = 1 - y if k & 2 else y
    pc = 1 - c if k & 1 else c
    return (px, py, pc), 4 * px + 2 * py + pc


HBM_SPEC = pl.BlockSpec(memory_space=pltpu.HBM)
SEM_SPEC = pl.BlockSpec(memory_space=pltpu.SEMAPHORE)
SPLIT_EFFECT = pltpu.SideEffectType.DATAFLOW_SIDE_EFFECTING
N_PEERS = N_DEV - 1


def _exchange_copies(srcs, lands, send_sems, recv_sems, whole, arrival):
    x, y, c = _my_place()
    me = 4 * x + 2 * y + c
    copies = []
    for t in range(len(srcs)):
        for k in range(1, N_DEV):
            peer, pidx = _peer(x, y, c, k)
            copies.append(pltpu.make_async_remote_copy(
                src_ref=srcs[t] if whole[t] else srcs[t].at[pidx],
                dst_ref=lands[t].at[pidx if arrival else me], send_sem=send_sems.at[t * N_PEERS + k - 1],
                recv_sem=recv_sems.at[t * N_PEERS + k - 1], device_id=peer, device_id_type=MESH))
    return copies


def _exchange_start(groups, name, after=None):
    sizes = [len(g) for g in groups]
    whole = [w for g in groups for _, w in g]
    srcs = [pltpu.with_memory_space_constraint(a, pltpu.HBM) for g in groups for a, _ in g]
    lands = [pltpu.with_memory_space_constraint(lax.empty(((N_DEV,) + a.shape) if w else a.shape, a.dtype), pltpu.HBM)
             for a, w in zip(srcs, whole)]
    n = len(srcs)
    n_g = len(groups)
    extra = [] if after is None else [after]
    n_in = 2 * n + len(extra)

    def body(*refs):
        src_refs, land_refs = refs[:n], refs[n:2 * n]
        sem_refs = refs[n_in + 2 * n:n_in + 2 * n + 2 * n_g]
        token = refs[-1]
        off = 0
        for gi, sz in enumerate(sizes):
            for send in _exchange_copies(src_refs[off:off + sz], land_refs[off:off + sz],
                                         sem_refs[2 * gi], sem_refs[2 * gi + 1], whole[off:off + sz], False):
                send.start()
            off += sz
        token[...] = jnp.zeros_like(token)

    sem_shapes = [pltpu.SemaphoreType.DMA((sz * N_PEERS,)) for sz in sizes for _ in range(2)]
    outs = pl.pallas_call(
        body, name=name,
        in_specs=[HBM_SPEC] * (2 * n) + [pl.BlockSpec(memory_space=pl.ANY)] * len(extra),
        out_specs=[HBM_SPEC] * (2 * n) + [SEM_SPEC] * (2 * n_g) + [pl.BlockSpec(memory_space=pltpu.VMEM)],
        out_shape=[pltpu.HBM(a.shape, a.dtype) for a in srcs + lands] + sem_shapes
                  + [jax.ShapeDtypeStruct((SUBLANES, LANES), F32)],
        input_output_aliases={i: i for i in range(2 * n)},
        compiler_params=pltpu.CompilerParams(has_side_effects=SPLIT_EFFECT),
    )(*srcs, *lands, *extra)
    handles, off = [], 0
    for gi, sz in enumerate(sizes):
        handles.append((outs[2 * n + 2 * gi], outs[2 * n + 2 * gi + 1], outs[off:off + sz], outs[n + off:n + off + sz],
                        whole[off:off + sz]))
        off += sz
    return handles, outs[-1]


def _exchange_wait(handle, after, name):
    send_sems, recv_sems, srcs, lands, whole = handle
    n = len(srcs)

    def body(*refs):
        src_refs, land_refs = refs[:n], refs[n:2 * n]
        for send in _exchange_copies(src_refs, land_refs, refs[2 * n], refs[2 * n + 1], whole, False):
            send.wait_send()
        for arrival in _exchange_copies(src_refs, land_refs, refs[2 * n], refs[2 * n + 1], whole, True):
            arrival.wait_recv()

    outs = pl.pallas_call(
        body, name=name,
        in_specs=[HBM_SPEC] * (2 * n) + [SEM_SPEC, SEM_SPEC, pl.BlockSpec(memory_space=pl.ANY)],
        out_specs=[HBM_SPEC] * (2 * n),
        out_shape=[pltpu.HBM(a.shape, a.dtype) for a in list(srcs) + list(lands)],
        input_output_aliases={i: i for i in range(2 * n)},
        compiler_params=pltpu.CompilerParams(has_side_effects=SPLIT_EFFECT),
    )(*srcs, *lands, send_sems, recv_sems, after)
    x, y, c = _my_place()
    me = 4 * x + 2 * y + c
    own = [s[None] if w else lax.dynamic_slice_in_dim(s, me, 1, axis=0) for s, w in zip(outs[:n], whole)]
    return [lax.dynamic_update_slice_in_dim(ld, o, me, axis=0) for ld, o in zip(outs[n:], own)]


def _sum_slots(parts, name):
    n = len(parts)

    def body(*refs):
        for p_ref, o_ref in zip(refs[:n], refs[n:]):
            acc = p_ref[0]
            for j in range(1, N_DEV):
                acc = acc + p_ref[j]
            o_ref[...] = acc

    vm = pl.BlockSpec(memory_space=pltpu.VMEM)
    return pl.pallas_call(
        body, name=name, in_specs=[vm] * n, out_specs=[vm] * n,
        out_shape=[jax.ShapeDtypeStruct(p.shape[1:], F32) for p in parts],
        compiler_params=pltpu.CompilerParams(vmem_limit_bytes=VMEM_LIMIT),
    )(*parts)


def _adamw_math(w, g, m, v):
    m = ADAM_B1 * m + (1.0 - ADAM_B1) * g
    v = ADAM_B2 * v + (1.0 - ADAM_B2) * (g * g)
    m_hat = m / (1.0 - ADAM_B1 ** ADAM_STEP)
    v_hat = v / (1.0 - ADAM_B2 ** ADAM_STEP)
    delta = -ADAM_LR * (m_hat / (jnp.sqrt(v_hat) + ADAM_EPS) + ADAM_WD * w)
    return delta, m, v


def _adamw_summed(parts, w, m, v, tr, name):
    depth, R, C = w.shape

    def body(*refs):
        p_refs = refs[:depth]
        w_ref, m_ref, v_ref, g_ref, d_ref, nm_ref, nv_ref = refs[depth:]
        lay = pl.program_id(0)
        for l in range(depth):
            @pl.when(lay == l)
            def _(p_ref=p_refs[l]):
                g = p_ref[0].astype(F32)
                for j in range(1, N_DEV):
                    g = g + p_ref[j].astype(F32)
                g_ref[0] = g
        d_ref[0], nm_ref[0], nv_ref[0] = _adamw_math(w_ref[0], g_ref[0], m_ref[0], v_ref[0])

    part_spec = lambda l: pl.BlockSpec((N_DEV, tr, C), lambda lay, i: (0, jnp.where(lay == l, i, 0), 0))
    row = pl.BlockSpec((1, tr, C), lambda lay, i: (lay, i, 0))
    return pl.pallas_call(
        body, name=name, grid=(depth, R // tr),
        in_specs=[part_spec(l) for l in range(depth)] + [row, row, row],
        out_specs=[row] * 4, out_shape=[jax.ShapeDtypeStruct((depth, R, C), F32)] * 4,
        compiler_params=_params(("arbitrary", "arbitrary")),
    )(*parts, w, m, v)


def _adamw_small(w, g, m, v, name):
    def body(w_ref, g_ref, m_ref, v_ref, d_ref, nm_ref, nv_ref):
        d_ref[...], nm_ref[...], nv_ref[...] = _adamw_math(w_ref[...], g_ref[...], m_ref[...], v_ref[...])

    vm = pl.BlockSpec(memory_space=pltpu.VMEM)
    return pl.pallas_call(
        body, name=name, in_specs=[vm] * 4, out_specs=[vm] * 3,
        out_shape=[jax.ShapeDtypeStruct(w.shape, F32)] * 3,
        compiler_params=pltpu.CompilerParams(vmem_limit_bytes=VMEM_LIMIT),
    )(w, g, m, v)


def _pack(arrays):
    flat = jnp.concatenate([a.reshape(-1) for a in arrays])
    pad = (-flat.shape[0]) % (SUBLANES * LANES)
    return jnp.pad(flat, (0, pad)).reshape(-1, LANES)


def _unpack(buf, like):
    flat = buf.reshape(-1)
    out, off = [], 0
    for a in like:
        out.append(flat[off:off + a.size].reshape(a.shape))
        off += a.size
    return out


def _block_diag(w):
    eye = jnp.eye(N_HEADS, dtype=w.dtype)
    return jnp.einsum('hij,hk->hikj', w, eye).reshape(GROUP_W, GROUP_W)


def _diag_blocks(w):
    return jnp.einsum('hihj->hij', w.reshape(N_HEADS, HEAD_DIM, N_HEADS, HEAD_DIM))


def _pad_rows(a):
    return jnp.pad(a, ((0, SUBLANES - a.shape[0]), (0, 0)))


def _mixer_params(l, conv_a_w, conv_r_w, conv_r_b, lru_wa, lru_ba, lru_wx, lru_bx, lru_lambda, gmlp_norm_g,
                  gmlp_ws, gmlp_bs):
    tril = jnp.tril(jnp.ones((GMLP_CHUNK, GMLP_CHUNK), dtype=bool))
    vec = jnp.stack([conv_r_b[l], lru_ba[l], lru_bx[l], lru_lambda[l], gmlp_norm_g[l]])
    return {
        "wA": _pad_rows(conv_a_w[l]), "wR": _pad_rows(conv_r_w[l]), "vec": _pad_rows(vec),
        "wa": _block_diag(lru_wa[l]).astype(MXU_DTYPE), "wx": _block_diag(lru_wx[l]).astype(MXU_DTYPE),
        "ws": jnp.where(tril[None], gmlp_ws[l], 0.0).astype(MXU_DTYPE),
        "bs": jnp.repeat(jnp.transpose(gmlp_bs[l]), HEAD_DIM, axis=1),
    }


MIXER_NAMES = ("conv_a_w", "conv_r_w", "conv_r_b", "lru_wa", "lru_ba", "lru_wx", "lru_bx", "lru_lambda",
               "gmlp_norm_g", "gmlp_ws", "gmlp_bs")
SMALL_NAMES = ("norm_g",) + MIXER_NAMES + ("final_g",)


def _local_step(x, loss_target, norm_g, get_w_in, get_w_out, emit_early, emit_late, conv_a_w, conv_r_w, conv_r_b,
                lru_wa, lru_ba, lru_wx, lru_bx, lru_lambda, gmlp_norm_g, gmlp_ws, gmlp_bs, final_g):
    depth = norm_g.shape[0]
    D = x.shape[1]
    small = (conv_a_w, conv_r_w, conv_r_b, lru_wa, lru_ba, lru_wx, lru_bx, lru_lambda, gmlp_norm_g, gmlp_ws, gmlp_bs)
    saved = []
    for l in range(depth):
        mp = _mixer_params(l, *small)
        w_in_l = get_w_in(l, x)
        z, z_g, *qkv, y_abc, hs, lru_a, lru_mult = _inproj_mix_fwd(
            x, norm_g[l].reshape(1, D), w_in_l, mp, f"inproj_mix_fwd_{l}")
        attn =[_attn_fwd(qkv[p], dil, f"attn_fwd_d{dil}_{l}") for p, dil in enumerate(ATTN_DILATIONS)]
        w_out_l = get_w_out(l, y_abc)
        x_new, y, o, *lse = _outproj(x, z_g, y_abc, attn, w_out_l, f"outproj_{l}")
        saved.append((x, z, z_g, qkv, (hs, lru_a, lru_mult), y, o, lse, mp, w_in_l, w_out_l))
        x = x_new
    dx, loss, d_final_g = _loss_head(x, final_g.reshape(1, D), loss_target, "loss_head")
    token = None
    for l in reversed(range(depth)):
        x_l, z, z_g, qkv, lru, y, o, lse, mp, w_in_l, w_out_l = saved[l]
        if token is not None:
            mp = dict(mp, vec=mp["vec"] + token[0, 0])
        (dw_out, dz_abc, dz_g, do1, do4, do16, dl1, dl4, dl16, dwA, dwR, dvec, dwa, dwx, dws, dbs) = _outproj_mix_bwd(
            dx, y, w_out_l, z, z_g, *lru, o, mp, f"outproj_mix_bwd_{l}")
        token = emit_early(l, dw_out, [
            dwA[:conv_a_w.shape[1]], dwR[:conv_r_w.shape[1]], dvec[0], _diag_blocks(dwa), dvec[1], _diag_blocks(dwx),
            dvec[2], dvec[3], dvec[4], dws, jnp.transpose(dbs[:, :N_HEADS])])
        g_row = norm_g[l].reshape(1, D)
        if token is not None:
            g_row = g_row + token[0, 0]
        dqkv = [_attn_bwd(qkv[p], do, lse[p], dl, dil, f"attn_bwd_d{dil}_{l}")
                for p, (dil, do, dl) in enumerate(zip(ATTN_DILATIONS, (do1, do4, do16), (dl1, dl4, dl16)))]
        dx, dz, h, dng = _inproj_bwd(x_l, g_row, dx, dz_abc, dqkv, dz_g, w_in_l, f"inproj_bwd_{l}")
        dw_in = _inproj_wgrad(h, dz, f"inproj_wgrad_{l}")
        token = emit_late(l, dw_in, [dng[0]] + ([d_final_g[0]] if l == depth - 1 else []))
    return loss[0, 0], dx
WEIGHT_NAMES = ("norm_g", "w_in", "conv_a_w", "conv_r_w", "conv_r_b", "lru_wa", "lru_ba", "lru_wx", "lru_bx",
                "lru_lambda", "gmlp_norm_g", "gmlp_ws", "gmlp_bs", "w_out", "final_g")


def kernel(x, norm_g, w_in, conv_a_w, conv_r_w, conv_r_b, lru_wa, lru_ba, lru_wx, lru_bx, lru_lambda, gmlp_norm_g, gmlp_ws, gmlp_bs, w_out, final_g, loss_target, m_norm_g, m_w_in, m_conv_a_w, m_conv_r_w, m_conv_r_b, m_lru_wa, m_lru_ba, m_lru_wx, m_lru_bx, m_lru_lambda, m_gmlp_norm_g, m_gmlp_ws, m_gmlp_bs, m_w_out, m_final_g, v_norm_g, v_w_in, v_conv_a_w, v_conv_r_w, v_conv_r_b, v_lru_wa, v_lru_ba, v_lru_wx, v_lru_bx, v_lru_lambda, v_gmlp_norm_g, v_gmlp_ws, v_gmlp_bs, v_w_out, v_final_g):
    w = dict(norm_g=norm_g, w_in=w_in, conv_a_w=conv_a_w, conv_r_w=conv_r_w, conv_r_b=conv_r_b, lru_wa=lru_wa,
             lru_ba=lru_ba, lru_wx=lru_wx, lru_bx=lru_bx, lru_lambda=lru_lambda, gmlp_norm_g=gmlp_norm_g,
             gmlp_ws=gmlp_ws, gmlp_bs=gmlp_bs, w_out=w_out, final_g=final_g)
    m = dict(norm_g=m_norm_g, w_in=m_w_in, conv_a_w=m_conv_a_w, conv_r_w=m_conv_r_w, conv_r_b=m_conv_r_b,
             lru_wa=m_lru_wa, lru_ba=m_lru_ba, lru_wx=m_lru_wx, lru_bx=m_lru_bx, lru_lambda=m_lru_lambda,
             gmlp_norm_g=m_gmlp_norm_g, gmlp_ws=m_gmlp_ws, gmlp_bs=m_gmlp_bs, w_out=m_w_out, final_g=m_final_g)
    v = dict(norm_g=v_norm_g, w_in=v_w_in, conv_a_w=v_conv_a_w, conv_r_w=v_conv_r_w, conv_r_b=v_conv_r_b,
             lru_wa=v_lru_wa, lru_ba=v_lru_ba, lru_wx=v_lru_wx, lru_bx=v_lru_bx, lru_lambda=v_lru_lambda,
             gmlp_norm_g=v_gmlp_norm_g, gmlp_ws=v_gmlp_ws, gmlp_bs=v_gmlp_bs, w_out=v_w_out, final_g=v_final_g)
    depth, D, n_loc = w_in.shape
    e_loc = w_out.shape[1]
    cx, cy, cc = _my_place()
    me = 4 * cx + 2 * cy + cc

    transposed = lambda a: jnp.transpose(a, (0, 2, 1))
    w_in_t, m_w_in_t, v_w_in_t = transposed(w_in), transposed(m_w_in), transposed(v_w_in)
    w_in_w, w_out_w = w_in_t.astype(MXU_DTYPE), w_out.astype(MXU_DTYPE)
    c_loc = conv_a_w.shape[2]
    taps = (conv_a_w, conv_r_w)
    first, _ = _exchange_start([[(w_in_w[0], True), (_pack(taps), True)], [(w_out_w[0], True)]], "gather_start_first")
    full_in = lambda g: g.reshape(N_DEV * n_loc, D)
    full_out = lambda g: g.reshape(N_DEV * e_loc, D)

    g_in0, g_taps = _exchange_wait(first[0], x, "gather_wait_in_0")
    groups = [[(w_in_w[l], True), (w_out_w[l], True)] for l in range(1, depth)]
    gathers, rest_token = _exchange_start(groups, "gather_start_rest", after=g_taps)
    g_taps = g_taps.reshape(N_DEV, -1) + rest_token[0, 0]
    conv_full, off = [], 0
    for a in taps:
        part = g_taps[:, off:off + a.size].reshape((N_DEV,) + a.shape)
        conv_full.append(jnp.transpose(part, (1, 2, 0, 3)).reshape(a.shape[:2] + (N_DEV * c_loc,)))
        off += a.size
    conv_a_full, conv_r_full = conv_full
    later = {}

    def get_w_in(l, after):
        if l == 0:
            return full_in(g_in0)
        g_in, later[l] = _exchange_wait(gathers[l - 1], after, f"gather_wait_{l}")
        return full_in(g_in)

    def get_w_out(l, after):
        if l == 0:
            return full_out(_exchange_wait(first[1], after, "gather_wait_out_0")[0])
        return full_out(later[l])

    early, late, last_token = {}, {}, [None]

    def emit_early(l, dw_out, mixer_grads):
        handles, token = _exchange_start(
            [[(dw_out.reshape(N_DEV, e_loc, D), False), (_pack(mixer_grads), True)]], f"early_start_{l}")
        early[l] = (handles[0], mixer_grads)
        return token

    def emit_late(l, dw_in, norm_grads):
        handles, token = _exchange_start([[(_pack(norm_grads), True)], [(dw_in, False)]], f"late_start_{l}")
        late[l] = (handles[0], handles[1], norm_grads)
        last_token[0] = token
        return token

    loss, grad_x = _local_step(
        x[0], loss_target[0], norm_g, get_w_in, get_w_out, emit_early, emit_late, conv_a_full, conv_r_full, conv_r_b,
        lru_wa, lru_ba, lru_wx, lru_bx, lru_lambda, gmlp_norm_g, gmlp_ws, gmlp_bs, final_g)
    loss = lax.psum(loss, ("x", "y", "c"))

    r_in, r_out, small_parts = {}, {}, []
    for l in reversed(range(depth)):
        r_out[l], r_mix = _exchange_wait(early[l][0], last_token[0], f"early_wait_{l}")
        (r_norm,) = _exchange_wait(late[l][0], last_token[0], f"late_wait_norm_{l}")
        small_parts += [r_mix, r_norm]
        if l > 0:
            (r_in[l],) = _exchange_wait(late[l][1], last_token[0], f"late_wait_{l}")
    big = {"w_out": _adamw_summed([r_out[l] for l in range(depth)], w_out, m_w_out, v_w_out, e_loc, "adamw_w_out")}

    sums = _sum_slots(small_parts, "sum_small_grads")
    by_layer = {}
    for i, l in enumerate(reversed(range(depth))):
        mix = _unpack(sums[2 * i], early[l][1])
        nrm = _unpack(sums[2 * i + 1], late[l][2])
        by_layer[l] = dict(zip(MIXER_NAMES, mix), norm_g=nrm[0])
        if l == depth - 1:
            g_final = nrm[1]
    g_small = {k: jnp.stack([by_layer[l][k] for l in range(depth)]) for k in ("norm_g",) + MIXER_NAMES}
    g_small["final_g"] = g_final
    for k in ("conv_a_w", "conv_r_w"):
        g_small[k] = lax.dynamic_slice_in_dim(g_small[k], me * c_loc, c_loc, axis=2)
    packs = [_pack([d[k] for k in SMALL_NAMES]) for d in (w, g_small, m, v)]
    res = _adamw_small(*packs, "adamw_small")
    like = [w[k] for k in SMALL_NAMES]
    d_s, m_s, v_s = (dict(zip(SMALL_NAMES, _unpack(r, like))) for r in res)

    (r_in[0],) = _exchange_wait(late[0][1], res[0], "late_wait_0")
    big["w_in"] = [transposed(a) for a in _adamw_summed(
        [r_in[l] for l in range(depth)], w_in_t, m_w_in_t, v_w_in_t, n_loc // 2, "adamw_w_in")]

    grad, delta, new_m, new_v = {}, {}, {}, {}
    for k in WEIGHT_NAMES:
        if k in big:
            grad[k], delta[k], new_m[k], new_v[k] = big[k]
        else:
            grad[k], delta[k], new_m[k], new_v[k] = g_small[k], d_s[k], m_s[k], v_s[k]
    return (loss, grad_x[None], *[grad[k] for k in WEIGHT_NAMES], *[delta[k] for k in WEIGHT_NAMES],
            *[new_m[k] for k in WEIGHT_NAMES], *[new_v[k] for k in WEIGHT_NAMES])
```

```python
import functools
import math

import jax
import jax.numpy as jnp
from jax import lax
from jax.experimental import pallas as pl
from jax.experimental.pallas import tpu as pltpu

F32 = jnp.float32
MXU_DTYPE = jnp.bfloat16
WIRE_DTYPE = jnp.bfloat16
MESH = pl.DeviceIdType.MESH

N_DEV = 8
GROUP_W = 256
N_HEADS = 4
HEAD_DIM = 64
N_CHUNKS = 13
N_ABC = 9
GMLP_CHUNK = 128
ATTN_BLOCK = 128
ATTN_FWD_BLOCKS_PER_STEP = 16
ATTN_BWD_BLOCKS_PER_STEP = 8
ATTN_DILATIONS = (1, 4, 16)
NORM_EPS = 1e-6
RG_C = 8.0
SUBLANES = 8
LANES = 128
VMEM_LIMIT = 56 * 1024 * 1024

ADAM_LR = 0.001
ADAM_B1 = 0.9
ADAM_B2 = 0.999
ADAM_EPS = 1e-08
ADAM_WD = 0.01
ADAM_STEP = 10

TM_MIX = 512
TM_MM = 512
TM_WGRAD = 1024


def _params(sem, vmem=VMEM_LIMIT):
    return pltpu.CompilerParams(dimension_semantics=sem, vmem_limit_bytes=vmem)


def _mm(a, b):
    return jnp.dot(a.astype(MXU_DTYPE), b.astype(MXU_DTYPE), preferred_element_type=F32)


def _mm_tn(a, b):
    return lax.dot_general(a.astype(MXU_DTYPE), b.astype(MXU_DTYPE), (((0,), (0,)), ((), ())),
                           preferred_element_type=F32)


def _mm_nt(a, b):
    return lax.dot_general(a.astype(MXU_DTYPE), b.astype(MXU_DTYPE), (((1,), (1,)), ((), ())),
                           preferred_element_type=F32)


def _sigmoid(x):
    return 0.5 * jnp.tanh(0.5 * x) + 0.5


def _sigmoid_small_exact(x):
    return 1.0 / (1.0 + jnp.exp(-x))


def _silu_and_grad(x):
    s = _sigmoid(x)
    return x * s, s * (1.0 + x * (1.0 - s))


_GELU_K = math.sqrt(2.0 / math.pi)
_GELU_C = 0.044715


def _gelu_and_grad(x):
    x2 = x * x
    t = jnp.tanh(_GELU_K * (x + _GELU_C * x * x2))
    val = 0.5 * x * (1.0 + t)
    grad = 0.5 * (1.0 + t) + 0.5 * x * (1.0 - t * t) * (_GELU_K * (1.0 + 3.0 * _GELU_C * x2))
    return val, grad


def _gelu(x):
    return 0.5 * x * (1.0 + jnp.tanh(_GELU_K * (x + _GELU_C * x * x * x)))


def _expm1_nonpos(u):
    poly = 1.0 / math.factorial(9)
    for k in range(8, 0, -1):
        poly = poly * u + 1.0 / math.factorial(k)
    return jnp.where(u > -0.25, poly * u, jnp.exp(u) - 1.0)


def _softplus(x):
    return jnp.maximum(x, 0.0) + jnp.log(1.0 + jnp.exp(-jnp.abs(x)))


def _shift_down(t, halo, k):
    rolled = pltpu.roll(t, k, 0)
    hr = pltpu.roll(halo, k, 0)
    row = lax.broadcasted_iota(jnp.int32, halo.shape, 0)
    first = jnp.where(row < k, hr, rolled[0:SUBLANES])
    return jnp.concatenate([first, rolled[SUBLANES:]], axis=0)


def _shift_up(t, nxt, k):
    tm = t.shape[0]
    rolled = pltpu.roll(t, tm - k, 0)
    nr = pltpu.roll(nxt, SUBLANES - k, 0)
    row = lax.broadcasted_iota(jnp.int32, nxt.shape, 0)
    last = jnp.where(row >= SUBLANES - k, nr, rolled[tm - SUBLANES:tm])
    return jnp.concatenate([rolled[:tm - SUBLANES], last], axis=0)


def _scan_fwd(a, b):
    tm = a.shape[0]
    row = lax.broadcasted_iota(jnp.int32, a.shape, 0)
    s = 1
    while s < tm:
        a_s = pltpu.roll(a, s, 0)
        b_s = pltpu.roll(b, s, 0)
        m = row >= s
        b = jnp.where(m, a * b_s + b, b)
        a = jnp.where(m, a * a_s, a)
        s *= 2
    return a, b


def _scan_rev(a, g):
    tm = a.shape[0]
    row = lax.broadcasted_iota(jnp.int32, a.shape, 0)
    s = 1
    while s < tm:
        a_s = pltpu.roll(a, tm - s, 0)
        g_s = pltpu.roll(g, tm - s, 0)
        m = row < tm - s
        g = jnp.where(m, g + a * g_s, g)
        a = jnp.where(m, a * a_s, a)
        s *= 2
    return g


def _group_rows(scr_ref, row, n_groups):
    return jnp.concatenate([scr_ref[pl.ds(c, 1), pl.ds(row, n_groups, stride=SUBLANES), :][0]
                            for c in range(scr_ref.shape[0])], axis=1)


def _spread_rows(rows_ref, n_groups, w):
    return jnp.concatenate([jnp.broadcast_to(rows_ref[g:g + 1, :], (SUBLANES, w)) for g in range(n_groups)], axis=0)


def _scan_groups(a, b, reverse):
    tm, w = a.shape
    shape3 = (tm // SUBLANES, SUBLANES, w)
    a3, b3 = a.reshape(shape3), b.reshape(shape3)
    sub = lax.broadcasted_iota(jnp.int32, shape3, 1)
    s = 1
    while s < SUBLANES:
        shift = SUBLANES - s if reverse else s
        a_s = pltpu.roll(a3, shift, 1)
        b_s = pltpu.roll(b3, shift, 1)
        m = (sub < SUBLANES - s) if reverse else (sub >= s)
        b3 = jnp.where(m, a3 * b_s + b3, b3)
        a3 = jnp.where(m, a3 * a_s, a3)
        s *= 2
    return a3.reshape(tm, w), b3.reshape(tm, w)


def _scan_fwd_tile(a, b, h_in, sa_ref, sb_ref, sc_ref):
    tm, w = a.shape
    n_groups = tm // SUBLANES
    a_loc, b_loc = _scan_groups(a, b, False)
    _put(sa_ref, a_loc)
    _put(sb_ref, b_loc)
    a_end, b_end = _scan_fwd(_group_rows(sa_ref, SUBLANES - 1, n_groups), _group_rows(sb_ref, SUBLANES - 1, n_groups))
    h_end = b_end + a_end * h_in
    sc_ref[...] = _shift_down(h_end, jnp.broadcast_to(h_in, (SUBLANES, w)), 1)
    return b_loc + a_loc * _spread_rows(sc_ref, n_groups, w), h_end


def _scan_rev_tile(a, g, sa_ref, sb_ref, sc_ref):
    tm, w = a.shape
    n_groups = tm // SUBLANES
    a_loc, g_loc = _scan_groups(a, g, True)
    _put(sa_ref, a_loc)
    _put(sb_ref, g_loc)
    d_first = _scan_rev(_group_rows(sa_ref, 0, n_groups), _group_rows(sb_ref, 0, n_groups))
    sc_ref[...] = _shift_up(d_first, jnp.zeros((SUBLANES, w), F32), 1)
    return g_loc + a_loc * _spread_rows(sc_ref, n_groups, w)


def _lane_scratch(tm, w):
    return pltpu.VMEM((w // LANES, tm, LANES), F32)


def _put(scr_ref, val):
    for c in range(scr_ref.shape[0]):
        scr_ref[c] = val[:, c * LANES:(c + 1) * LANES].astype(F32)


def _get(scr_ref):
    return jnp.concatenate([scr_ref[c] for c in range(scr_ref.shape[0])], axis=1)


MAX_ROW_STRIDE = 4


def _strided_rows(c, start, n, stride):
    return (pl.ds(c, 1), pl.ds(start, n, stride=stride), slice(None))


def _deinterleave(src_ref, dst_ref, dil, tmp_ref=None):
    nc, tm, _ = src_ref.shape
    s1 = min(dil, MAX_ROW_STRIDE)
    s2 = dil // s1
    if s2 > 1:
        for r0 in range(s1):
            for c in range(nc):
                tmp_ref[c, r0 * (tm // s1):(r0 + 1) * (tm // s1), :] = src_ref[_strided_rows(c, r0, tm // s1, s1)][0]
    for r in range(dil):
        r1, r0 = divmod(r, s1)
        for c in range(nc):
            if dil == 1:
                piece = src_ref[c]
            elif s2 == 1:
                piece = src_ref[_strided_rows(c, r, tm // dil, dil)][0]
            else:
                piece = tmp_ref[_strided_rows(c, r0 * (tm // s1) + r1, tm // dil, s2)][0]
            dst_ref[r, :, c * LANES:(c + 1) * LANES] = piece.astype(dst_ref.dtype)


def _interleave(src_ref, dst_ref, dil, tmp_ref=None):
    nc, tm, _ = dst_ref.shape
    s1 = min(dil, MAX_ROW_STRIDE)
    s2 = dil // s1
    for r in range(dil):
        r1, r0 = divmod(r, s1)
        for c in range(nc):
            piece = src_ref[r, :, c * LANES:(c + 1) * LANES].astype(F32)[None]
            if s2 == 1:
                dst_ref[_strided_rows(c, r, tm // dil, dil)] = piece
            else:
                tmp_ref[_strided_rows(c, r0 * (tm // s1) + r1, tm // dil, s2)] = piece
    if s2 > 1:
        for r0 in range(s1):
            for c in range(nc):
                dst_ref[_strided_rows(c, r0, tm // s1, s1)] = (
                    tmp_ref[c, r0 * (tm // s1):(r0 + 1) * (tm // s1), :][None])


def _dilated_spec(tm, w, dil, index=lambda i: i):
    return pl.BlockSpec((dil, tm // dil, w), lambda i: (0, index(i), 0))


def _dilated_shape(S, w, dil, dtype):
    return jax.ShapeDtypeStruct((dil, S // dil, w), dtype)


def _head_masks(shape):
    lane = lax.broadcasted_iota(jnp.int32, shape, 1)
    return [(lane >= h * HEAD_DIM) & (lane < (h + 1) * HEAD_DIM) for h in range(N_HEADS)]


def _colsum(v):
    return jnp.sum(v, axis=0, keepdims=True)


def _conv_a(z_of, halo_of, w_ref):
    p = z_of(2) * z_of(0)
    p_h = halo_of(2) * halo_of(0)
    cv = w_ref[2:3, :] * p + w_ref[1:2, :] * _shift_down(p, p_h, 1) + w_ref[0:1, :] * _shift_down(p, p_h, 2)
    return p, p_h, cv


def _lru_gates(z_of, halo_of, wr_ref, vec_ref, wa_ref, wx_ref, saved=None):
    rx = z_of(4)
    rx_h = halo_of(4)
    sh = [rx, _shift_down(rx, rx_h, 1), _shift_down(rx, rx_h, 2), _shift_down(rx, rx_h, 3)]
    xc = (wr_ref[3:4, :] * sh[0] + wr_ref[2:3, :] * sh[1] + wr_ref[1:2, :] * sh[2]
          + wr_ref[0:1, :] * sh[3] + vec_ref[0:1, :])
    ga = _sigmoid_small_exact(jnp.dot(xc.astype(MXU_DTYPE), wa_ref[...], preferred_element_type=F32) + vec_ref[1:2, :])
    gi = _sigmoid(jnp.dot(xc.astype(MXU_DTYPE), wx_ref[...], preferred_element_type=F32) + vec_ref[2:3, :])
    sp = _softplus(-vec_ref[3:4, :])
    if saved is not None:
        return (xc, sh, ga, gi) + tuple(saved) + (sp,)
    log_a = (-RG_C * ga) * sp
    a = jnp.exp(log_a)
    mult = jnp.sqrt(-_expm1_nonpos(2.0 * log_a))
    return xc, sh, ga, gi, a, mult, sp


def _gmlp_fwd(z_of, vec_ref, ws_ref, bs_ref, tm):
    u = _gelu(z_of(6))
    gv = _gelu(z_of(7))
    rr = lax.rsqrt(jnp.mean(gv * gv, axis=-1, keepdims=True) + NORM_EPS)
    vn = (gv * rr) * vec_ref[4:5, :]
    masks = _head_masks((GMLP_CHUNK, GROUP_W))
    parts = []
    for c in range(tm // GMLP_CHUNK):
        vc = vn[c * GMLP_CHUNK:(c + 1) * GMLP_CHUNK].astype(MXU_DTYPE)
        acc = bs_ref[...]
        for h in range(N_HEADS):
            acc = acc + jnp.where(masks[h], jnp.dot(ws_ref[h], vc, preferred_element_type=F32), 0.0)
        parts.append(acc)
    return u, gv, rr, vn, jnp.concatenate(parts, axis=0)


def _mix_specs():
    const2 = lambda shape: pl.BlockSpec(shape, lambda i: (0, 0))
    return [const2((SUBLANES, GROUP_W)), const2((SUBLANES, GROUP_W)), const2((SUBLANES, GROUP_W)),
            const2((GROUP_W, GROUP_W)), const2((GROUP_W, GROUP_W)),
            pl.BlockSpec((N_HEADS, GMLP_CHUNK, GMLP_CHUNK), lambda i: (0, 0, 0)),
            const2((GMLP_CHUNK, GROUP_W))]


def _inproj_mix_fwd(x, g, w_t, mp, name):
    S, D = x.shape
    N = w_t.shape[0]
    tm = TM_MIX
    hb = tm // SUBLANES
    n_abc = N_ABC * GROUP_W
    n_qkv = 3 * GROUP_W

    def body(x_ref, g_ref, w_ref, wA_ref, wR_ref, vec_ref, wa_ref, wx_ref, ws_ref, bs_ref,
             z_ref, zg_ref, q1_ref, q4_ref, q16_ref, y_ref, h_ref, a_ref, mult_ref,
             qkv_ref, halo_ref, carry_ref, sa_ref, sb_ref, sc_ref, tmp_ref):
        @pl.when(pl.program_id(0) == 0)
        def _():
            halo_ref[...] = jnp.zeros_like(halo_ref)
            carry_ref[...] = jnp.zeros_like(carry_ref)

        xv = x_ref[...]
        r = lax.rsqrt(jnp.mean(xv * xv, axis=-1, keepdims=True) + NORM_EPS)
        hn = ((xv * r) * g_ref[...]).astype(MXU_DTYPE)
        z_ref[...] = _mm_nt(hn, w_ref[0:n_abc, :])
        _put(qkv_ref, _mm_nt(hn, w_ref[n_abc:n_abc + n_qkv, :]))
        zg_ref[...] = _mm_nt(hn, w_ref[n_abc + n_qkv:, :])
        for dil, ref in zip(ATTN_DILATIONS, (q1_ref, q4_ref, q16_ref)):
            _deinterleave(qkv_ref, ref, dil, tmp_ref)

        z_of = lambda c: z_ref[:, c * GROUP_W:(c + 1) * GROUP_W]
        halo_of = lambda c: halo_ref[:, c * GROUP_W:(c + 1) * GROUP_W]

        _, _, cv = _conv_a(z_of, halo_of, wA_ref)
        y_ref[:, 0:GROUP_W] = (z_of(1) * cv * _silu_and_grad(z_of(3))[0]).astype(y_ref.dtype)

        xc, _, _, gi, a, mult, _ = _lru_gates(z_of, halo_of, wR_ref, vec_ref, wa_ref, wx_ref)
        a_ref[...] = a
        mult_ref[...] = mult
        b = mult * (gi * xc)
        h, h_end = _scan_fwd_tile(a, b, carry_ref[SUBLANES - 1:SUBLANES, :], sa_ref, sb_ref, sc_ref)
        h_ref[...] = h
        carry_ref[...] = h_end[hb - SUBLANES:hb]
        y_ref[:, GROUP_W:2 * GROUP_W] = (h * _silu_and_grad(z_of(5))[0]).astype(y_ref.dtype)

        u, _, _, _, sp = _gmlp_fwd(z_of, vec_ref, ws_ref, bs_ref, tm)
        y_ref[:, 2 * GROUP_W:3 * GROUP_W] = (u * sp * _silu_and_grad(z_of(8))[0]).astype(y_ref.dtype)
        halo_ref[...] = z_ref[tm - SUBLANES:tm, :]

    row = lambda wd: pl.BlockSpec((tm, wd), lambda i: (i, 0))
    return pl.pallas_call(
        body, name=name, grid=(S // tm,),
        in_specs=[row(D), pl.BlockSpec((1, D), lambda i: (0, 0)),
                  pl.BlockSpec((N, D), lambda i: (0, 0), pipeline_mode=pl.Buffered(1))] + _mix_specs(),
        out_specs=[row(n_abc), row(GROUP_W)] + [_dilated_spec(tm, n_qkv, dil) for dil in ATTN_DILATIONS]
                  + [row(3 * GROUP_W)] + [row(GROUP_W)] * 3,
        out_shape=[jax.ShapeDtypeStruct((S, n_abc), F32), jax.ShapeDtypeStruct((S, GROUP_W), F32)]
                  + [_dilated_shape(S, n_qkv, dil, MXU_DTYPE) for dil in ATTN_DILATIONS]
                  + [jax.ShapeDtypeStruct((S, 3 * GROUP_W), MXU_DTYPE)] + [jax.ShapeDtypeStruct((S, GROUP_W), F32)] * 3,
        scratch_shapes=[_lane_scratch(tm, n_qkv), pltpu.VMEM((SUBLANES, n_abc), F32),
                        pltpu.VMEM((SUBLANES, GROUP_W), F32), _lane_scratch(tm, GROUP_W), _lane_scratch(tm, GROUP_W),
                        pltpu.VMEM((hb, GROUP_W), F32), _lane_scratch(tm, n_qkv)],
        compiler_params=_params(("arbitrary",)),
    )(x, g, w_t, mp["wA"], mp["wR"], mp["vec"], mp["wa"], mp["wx"], mp["ws"], mp["bs"])


_NEG = -1e30


def _slope(h):
    return 2.0 ** (-8.0 * (h + 1) / N_HEADS)


def _attn_bias(dil, offsets, n_keys):
    shape = (ATTN_BLOCK, n_keys)
    qi = lax.broadcasted_iota(jnp.int32, shape, 0)
    ki = lax.broadcasted_iota(jnp.int32, shape, 1)
    blocks = []
    for f in offsets:
        delta = qi + f - ki
        valid = (delta >= 0) & (delta <= ATTN_BLOCK)
        dist = (delta * dil).astype(F32)
        for h in range(N_HEADS):
            blocks.append(jnp.where(valid, -_slope(h) * dist, _NEG))
    return jnp.concatenate(blocks, axis=0)


def _stack_heads(t, masks):
    return jnp.concatenate([jnp.where(m, t, jnp.zeros_like(t)) for m in masks], axis=0)


def _unstack_heads(t4, masks, base=0):
    out = t4[base * ATTN_BLOCK:(base + 1) * ATTN_BLOCK]
    for h in range(1, N_HEADS):
        out = jnp.where(masks[h], t4[(base + h) * ATTN_BLOCK:(base + h + 1) * ATTN_BLOCK], out)
    return out


def _group_starts(n, per_step, group):
    if per_step % group == 0:
        return (lambda j: j % group == 0), True
    steps = group // per_step
    return (lambda j: (n % steps == 0) if j == 0 else False), (n + 1) % steps == 0


def _attn_fwd(qkv, dil, name):
    S = qkv.shape[0] * qkv.shape[1]
    qkv = qkv.reshape(S, qkv.shape[2])
    nb = S // ATTN_BLOCK
    group = nb // dil
    scale = 1.0 / math.sqrt(HEAD_DIM)
    B = ATTN_BLOCK
    per_step = ATTN_FWD_BLOCKS_PER_STEP

    def body(q_ref, kc_ref, kp_ref, vc_ref, vp_ref, o_ref, l_ref, bias_ref):
        n = pl.program_id(0)

        @pl.when(n == 0)
        def _():
            bias_ref[...] = _attn_bias(dil, (B,), 2 * B)

        masks = _head_masks((B, GROUP_W))
        starts, _ = _group_starts(n, per_step, group)
        for j in range(per_step):
            own = slice(j * B, (j + 1) * B)
            before = slice((j - 1) * B, j * B)
            qs = _stack_heads(q_ref[own], masks)
            keys = jnp.concatenate([kp_ref[...] if j == 0 else kc_ref[before], kc_ref[own]], axis=0)
            vals = jnp.concatenate([vp_ref[...] if j == 0 else vc_ref[before], vc_ref[own]], axis=0)
            s = _mm_nt(qs, keys) * scale + bias_ref[...]
            if starts(j) is not False:
                key_col = lax.broadcasted_iota(jnp.int32, s.shape, 1)
                s = jnp.where(starts(j) & (key_col < B), _NEG, s)
            m = jnp.max(s, axis=-1, keepdims=True)
            p = jnp.exp(s - m)
            l = jnp.sum(p, axis=-1, keepdims=True)
            o4 = jnp.dot(p.astype(MXU_DTYPE), vals, preferred_element_type=F32)
            o_ref[own] = (_unstack_heads(o4, masks)
                          / _unstack_heads(jnp.broadcast_to(l, o4.shape), masks)).astype(o_ref.dtype)
            l_ref[own] = _unstack_heads(jnp.broadcast_to(m + jnp.log(l), o4.shape), masks)

    blk = (per_step * B, GROUP_W)
    cur = lambda c: pl.BlockSpec(blk, lambda n: (n, c))
    prev = lambda c: pl.BlockSpec((B, GROUP_W), lambda n: (jnp.maximum(n * per_step - 1, 0), c))
    out = pl.BlockSpec(blk, lambda n: (n, 0))
    o, l = pl.pallas_call(
        body, name=name, grid=(nb // per_step,),
        in_specs=[cur(0), cur(1), prev(1), cur(2), prev(2)],
        out_specs=[out, out],
        out_shape=[jax.ShapeDtypeStruct((S, GROUP_W), MXU_DTYPE), jax.ShapeDtypeStruct((S, GROUP_W), F32)],
        scratch_shapes=[pltpu.VMEM((N_HEADS * ATTN_BLOCK, 2 * ATTN_BLOCK), F32)],
        compiler_params=_params(("arbitrary",)),
    )(qkv, qkv, qkv, qkv, qkv)
    return o.reshape(dil, S // dil, GROUP_W), l.reshape(dil, S // dil, GROUP_W)


def _outproj(x, z_g, y_abc, attn, w_out, name):
    S, D = x.shape
    tm = TM_MM
    n_abc = 3 * GROUP_W

    def body(x_ref, g_ref, yabc_ref, o1, l1, o2, l2, o3, l3, w_ref,
             xn_ref, y_ref, o_ref, lse1_ref, lse4_ref, lse16_ref, so2, sl2, so3, sl3, slse, tmp_ref):
        for src, dst, dil in ((o2, so2, ATTN_DILATIONS[1]), (l2, sl2, ATTN_DILATIONS[1]),
                              (o3, so3, ATTN_DILATIONS[2]), (l3, sl3, ATTN_DILATIONS[2])):
            _interleave(src, dst, dil, tmp_ref)
        la, lb, lc = l1[0], _get(sl2), _get(sl3)
        mx = jnp.maximum(jnp.maximum(la, lb), lc)
        ea, eb, ec = jnp.exp(la - mx), jnp.exp(lb - mx), jnp.exp(lc - mx)
        den = ea + eb + ec
        o = (ea * o1[0].astype(F32) + eb * _get(so2) + ec * _get(so3)) / den
        o_ref[...] = o
        _put(slse, mx + jnp.log(den))
        for dil, ref in zip(ATTN_DILATIONS, (lse1_ref, lse4_ref, lse16_ref)):
            _deinterleave(slse, ref, dil, tmp_ref)
        y_d = o * _silu_and_grad(g_ref[...])[0]
        y_ref[:, 0:n_abc] = yabc_ref[...].astype(MXU_DTYPE)
        y_ref[:, n_abc:] = y_d.astype(MXU_DTYPE)
        xn_ref[...] = x_ref[...] + jnp.dot(y_ref[...], w_ref[...], preferred_element_type=F32)

    row = lambda w: pl.BlockSpec((tm, w), lambda i: (i, 0))
    dil_specs = [_dilated_spec(tm, GROUP_W, dil) for dil in ATTN_DILATIONS]
    (o1, l1), (o2, l2), (o3, l3) = attn
    return pl.pallas_call(
        body, name=name, grid=(S // tm,),
        in_specs=[row(D), row(GROUP_W), row(n_abc)] + [sp for sp in dil_specs for _ in range(2)]
                 + [pl.BlockSpec(w_out.shape, lambda i: (0, 0))],
        out_specs=[row(D), row(4 * GROUP_W), row(GROUP_W)] + dil_specs,
        out_shape=[jax.ShapeDtypeStruct((S, D), F32), jax.ShapeDtypeStruct((S, 4 * GROUP_W), MXU_DTYPE),
                   jax.ShapeDtypeStruct((S, GROUP_W), F32)]
                  + [_dilated_shape(S, GROUP_W, dil, F32) for dil in ATTN_DILATIONS],
        scratch_shapes=[_lane_scratch(tm, GROUP_W)] * 6,
        compiler_params=_params(("parallel",)),
    )(x, z_g, y_abc, o1, l1, o2, l2, o3, l3, w_out)


def _loss_head(x, g, target, name):
    S, D = x.shape
    tm = TM_MM

    def body(x_ref, g_ref, t_ref, dx_ref, loss_ref, dg_ref):
        i = pl.program_id(0)

        @pl.when(i == 0)
        def _():
            loss_ref[...] = jnp.zeros_like(loss_ref)
            dg_ref[...] = jnp.zeros_like(dg_ref)

        xv = x_ref[...]
        r = lax.rsqrt(jnp.mean(xv * xv, axis=-1, keepdims=True) + NORM_EPS)
        xn = xv * r
        err = xn * g_ref[...] - t_ref[...]
        per_tok = jnp.mean(err * err, axis=-1, keepdims=True)
        loss_ref[...] += 0.5 * jnp.sum(per_tok, axis=0, keepdims=True)
        dout = err * (1.0 / D)
        dg_ref[...] += _colsum(dout * xn)
        dxn = dout * g_ref[...]
        dx_ref[...] = r * (dxn - xn * jnp.mean(dxn * xn, axis=-1, keepdims=True))

    row = pl.BlockSpec((tm, D), lambda i: (i, 0))
    return pl.pallas_call(
        body, name=name, grid=(S // tm,),
        in_specs=[row, pl.BlockSpec((1, D), lambda i: (0, 0)), row],
        out_specs=[row, pl.BlockSpec((1, LANES), lambda i: (0, 0)), pl.BlockSpec((1, D), lambda i: (0, 0))],
        out_shape=[jax.ShapeDtypeStruct((S, D), F32), jax.ShapeDtypeStruct((1, LANES), F32),
                   jax.ShapeDtypeStruct((1, D), F32)],
        compiler_params=_params(("arbitrary",)),
    )(x, g, target)


def _outproj_mix_bwd(dx, y, w_out, z, z_g, hs, lru_a, lru_mult, o, mp, name):
    S, D = dx.shape
    E = y.shape[1]
    tm = TM_MIX
    hb = tm // SUBLANES
    nT = S // tm
    last_blk = S // SUBLANES - 1
    wcols = N_ABC * GROUP_W

    def body(dx_ref, y_ref, w_ref, z_ref, zh_ref, zn_ref, zg_ref, h_ref, hh_ref, a_ref, mult_ref, o_ref,
             wA_ref, wR_ref, vec_ref, wa_ref, wx_ref, ws_ref, bs_ref,
             dw_ref, dz_ref, dzg_ref, do1_ref, do4_ref, do16_ref, dl1_ref, dl4_ref, dl16_ref,
             dwA_ref, dwR_ref, dvec_ref, dwa_ref, dwx_ref, dws_ref, dbs_ref,
             hcarry_ref, xcarry_ref, bsacc_ref, do_ref, dl_ref, sa_ref, sb_ref, sc_ref, dy_ref, dyn_ref, acc_ref,
             tmp_ref):
        i = pl.program_id(0)
        ti = nT - 1 - i

        @pl.when(i == 0)
        def _():
            acc_ref[...] = jnp.zeros_like(acc_ref)
            dyn_ref[...] = jnp.zeros_like(dyn_ref)
            hcarry_ref[...] = jnp.zeros_like(hcarry_ref)
            xcarry_ref[...] = jnp.zeros_like(xcarry_ref)
            bsacc_ref[...] = jnp.zeros_like(bsacc_ref)
            dwA_ref[...] = jnp.zeros_like(dwA_ref)
            dwR_ref[...] = jnp.zeros_like(dwR_ref)
            dvec_ref[...] = jnp.zeros_like(dvec_ref)
            dwa_ref[...] = jnp.zeros_like(dwa_ref)
            dwx_ref[...] = jnp.zeros_like(dwx_ref)
            dws_ref[...] = jnp.zeros_like(dws_ref)
            dbs_ref[...] = jnp.zeros_like(dbs_ref)

        dxb = dx_ref[...].astype(MXU_DTYPE)
        dy_ref[...] = _mm_nt(dxb, w_ref[...])
        acc_ref[...] += _mm_tn(y_ref[...], dxb)

        @pl.when(i == nT - 1)
        def _():
            dw_ref[...] = acc_ref[...].astype(dw_ref.dtype)

        has_prev = ti > 0
        has_next = i > 0
        col = lambda c: slice(c * GROUP_W, (c + 1) * GROUP_W)
        z_of = lambda c: z_ref[:, col(c)]
        halo_of = lambda c: jnp.where(has_prev, zh_ref[:, col(c)], 0.0)
        next_of = lambda c: zn_ref[:, col(c)]

        p, p_h, cv = _conv_a(z_of, halo_of, wA_ref)
        sg, dsg = _silu_and_grad(z_of(3))
        a_b = z_of(1)
        dya = dy_ref[:, col(0)]
        dcv = dya * a_b * sg
        dcv_n = jnp.where(has_next, dyn_ref[...] * next_of(1) * _silu_and_grad(next_of(3))[0], 0.0)
        dp = (wA_ref[2:3, :] * dcv + wA_ref[1:2, :] * _shift_up(dcv, dcv_n, 1)
              + wA_ref[0:1, :] * _shift_up(dcv, dcv_n, 2))
        dwA_ref[2:3, :] += _colsum(dcv * p)
        dwA_ref[1:2, :] += _colsum(dcv * _shift_down(p, p_h, 1))
        dwA_ref[0:1, :] += _colsum(dcv * _shift_down(p, p_h, 2))
        def put_dz(c, val):
            dz_ref[:, col(c)] = val.astype(dz_ref.dtype)

        put_dz(0, dp * z_of(2))
        put_dz(1, dya * cv * sg)
        put_dz(2, dp * z_of(0))
        put_dz(3, dya * a_b * cv * dsg)

        xc, sh, ga, gi, a, mult, sp = _lru_gates(z_of, halo_of, wR_ref, vec_ref, wa_ref, wx_ref,
                                                 saved=(a_ref[...], mult_ref[...]))
        h = h_ref[...]
        h_prev = _shift_down(h, jnp.where(has_prev, hh_ref[...], 0.0), 1)
        sgr, dsgr = _silu_and_grad(z_of(5))
        dyb = dy_ref[:, col(1)]
        put_dz(5, dyb * h * dsgr)
        row = lax.broadcasted_iota(jnp.int32, (tm, GROUP_W), 0)
        g_in = dyb * sgr + jnp.where(row == tm - 1, hcarry_ref[0:1, :], 0.0)
        a_up = _shift_up(a, jnp.zeros((SUBLANES, GROUP_W), F32), 1)
        dH = _scan_rev_tile(a_up, g_in, sa_ref, sb_ref, sc_ref)
        hcarry_ref[...] = (a * dH)[0:SUBLANES]
        da = dH * h_prev
        gx = gi * xc
        dmult = dH * gx
        dgi = dH * mult * xc
        dxc = dH * mult * gi
        dlog_a = da * a - dmult * (a * a) / mult
        dga = dlog_a * (-RG_C * sp)
        dlam_row = _colsum(dlog_a * (-RG_C * ga)) * (-_sigmoid(-vec_ref[3:4, :]))
        dpre_a = dga * ga * (1.0 - ga)
        dpre_i = dgi * gi * (1.0 - gi)
        dwa_ref[...] += _mm_tn(xc, dpre_a)
        dwx_ref[...] += _mm_tn(xc, dpre_i)
        dxc = dxc + _mm_nt(dpre_a, wa_ref[...]) + _mm_nt(dpre_i, wx_ref[...])
        dvec_ref[0:1, :] += _colsum(dxc)
        dvec_ref[1:2, :] += _colsum(dpre_a)
        dvec_ref[2:3, :] += _colsum(dpre_i)
        dvec_ref[3:4, :] += dlam_row
        for k in range(4):
            dwR_ref[k:k + 1, :] += _colsum(dxc * sh[3 - k])
        dxc_n = xcarry_ref[...]
        put_dz(4, wR_ref[3:4, :] * dxc + wR_ref[2:3, :] * _shift_up(dxc, dxc_n, 1)
               + wR_ref[1:2, :] * _shift_up(dxc, dxc_n, 2) + wR_ref[0:1, :] * _shift_up(dxc, dxc_n, 3))
        xcarry_ref[...] = dxc[0:SUBLANES]

        c_u, c_v = z_of(6), z_of(7)
        u, du_dx = _gelu_and_grad(c_u)
        gv, dgv_dx = _gelu_and_grad(c_v)
        rr = lax.rsqrt(jnp.mean(gv * gv, axis=-1, keepdims=True) + NORM_EPS)
        xhat = gv * rr
        g_c = vec_ref[4:5, :]
        vn = xhat * g_c
        masks = _head_masks((GMLP_CHUNK, GROUP_W))
        tri_r = lax.broadcasted_iota(jnp.int32, (GMLP_CHUNK, GMLP_CHUNK), 0)
        tri_c = lax.broadcasted_iota(jnp.int32, (GMLP_CHUNK, GMLP_CHUNK), 1)
        tril = tri_r >= tri_c
        sgc, dsgc = _silu_and_grad(z_of(8))
        dyc = dy_ref[:, col(2)]
        dsp_full = dyc * u * sgc
        sp_parts, dvn_parts = [], []
        for c in range(tm // GMLP_CHUNK):
            rs = slice(c * GMLP_CHUNK, (c + 1) * GMLP_CHUNK)
            vc = vn[rs].astype(MXU_DTYPE)
            dsp_c = dsp_full[rs]
            bsacc_ref[...] += dsp_c
            acc = bs_ref[...]
            dvn_c = jnp.zeros((GMLP_CHUNK, GROUP_W), F32)
            for h in range(N_HEADS):
                w_h = ws_ref[h]
                acc = acc + jnp.where(masks[h], jnp.dot(w_h, vc, preferred_element_type=F32), 0.0)
                dsp_h = jnp.where(masks[h], dsp_c, 0.0).astype(MXU_DTYPE)
                dvn_c = dvn_c + _mm_tn(w_h, dsp_h)
                dws_ref[h] += jnp.where(tril, _mm_nt(dsp_h, vc), 0.0)
            sp_parts.append(acc)
            dvn_parts.append(dvn_c)
        spv = jnp.concatenate(sp_parts, axis=0)
        dvn = jnp.concatenate(dvn_parts, axis=0)
        put_dz(6, dyc * spv * sgc * du_dx)
        put_dz(8, dyc * u * spv * dsgc)
        dvec_ref[4:5, :] += _colsum(dvn * xhat)
        dgvn = dvn * g_c
        dgv = rr * (dgvn - xhat * jnp.mean(dgvn * xhat, axis=-1, keepdims=True))
        put_dz(7, dgv * dgv_dx)

        sgd, dsgd = _silu_and_grad(zg_ref[...])
        dyd = dy_ref[:, col(3)]
        ov = o_ref[...]
        do = dyd * sgd
        _put(do_ref, do)
        dzg_ref[...] = (dyd * ov * dsgd).astype(dzg_ref.dtype)
        prod = do * ov
        tmasks = _head_masks((tm, GROUP_W))
        dl = jnp.zeros((tm, GROUP_W), F32)
        for h in range(N_HEADS):
            dl = jnp.where(tmasks[h], jnp.sum(jnp.where(tmasks[h], prod, 0.0), axis=-1, keepdims=True), dl)
        _put(dl_ref, dl)
        for dil, d_out, l_out in zip(ATTN_DILATIONS, (do1_ref, do4_ref, do16_ref), (dl1_ref, dl4_ref, dl16_ref)):
            _deinterleave(do_ref, d_out, dil, tmp_ref)
            _deinterleave(dl_ref, l_out, dil, tmp_ref)

        @pl.when(i == nT - 1)
        def _():
            acc = bsacc_ref[...]
            lane = lax.broadcasted_iota(jnp.int32, (GMLP_CHUNK, LANES), 1)
            out = jnp.zeros((GMLP_CHUNK, LANES), F32)
            for h in range(N_HEADS):
                out = jnp.where(lane == h, jnp.sum(jnp.where(masks[h], acc, 0.0), axis=-1, keepdims=True), out)
            dbs_ref[...] = out

        dyn_ref[...] = dy_ref[0:SUBLANES, 0:GROUP_W]

    rev = lambda w: pl.BlockSpec((tm, w), lambda i: (nT - 1 - i, 0))
    prev8 = lambda w: pl.BlockSpec((SUBLANES, w), lambda i: (jnp.maximum((nT - 1 - i) * hb - 1, 0), 0))
    next8 = lambda w: pl.BlockSpec((SUBLANES, w), lambda i: (jnp.minimum((nT - i) * hb, last_blk), 0))
    const2 = lambda shape: pl.BlockSpec(shape, lambda i: (0, 0))
    dil_specs = [_dilated_spec(tm, GROUP_W, dil, lambda i: nT - 1 - i) for dil in ATTN_DILATIONS]
    dil_shapes = [_dilated_shape(S, GROUP_W, dil, F32) for dil in ATTN_DILATIONS]
    small = (SUBLANES, GROUP_W)
    sq = (GROUP_W, GROUP_W)
    ws_shape = (N_HEADS, GMLP_CHUNK, GMLP_CHUNK)
    return pl.pallas_call(
        body, name=name, grid=(nT,),
        in_specs=[rev(D), rev(E), pl.BlockSpec((E, D), lambda i: (0, 0), pipeline_mode=pl.Buffered(1)),
                  rev(wcols), prev8(wcols), next8(wcols), rev(GROUP_W), rev(GROUP_W), prev8(GROUP_W),
                  rev(GROUP_W), rev(GROUP_W), rev(GROUP_W)]
                 + _mix_specs(),
        out_specs=[const2((E, D)), rev(wcols), rev(GROUP_W)] + dil_specs + dil_specs
                  + [const2(small), const2(small), const2(small), const2(sq), const2(sq),
                     pl.BlockSpec(ws_shape, lambda i: (0, 0, 0)), const2((GMLP_CHUNK, LANES))],
        out_shape=[jax.ShapeDtypeStruct((E, D), WIRE_DTYPE),
                   jax.ShapeDtypeStruct((S, wcols), MXU_DTYPE), jax.ShapeDtypeStruct((S, GROUP_W), MXU_DTYPE)]
                  + [_dilated_shape(S, GROUP_W, dil, MXU_DTYPE) for dil in ATTN_DILATIONS] + dil_shapes
                  + [jax.ShapeDtypeStruct(small, F32)] * 3 + [jax.ShapeDtypeStruct(sq, F32)] * 2
                  + [jax.ShapeDtypeStruct(ws_shape, F32), jax.ShapeDtypeStruct((GMLP_CHUNK, LANES), F32)],
        scratch_shapes=[pltpu.VMEM(small, F32), pltpu.VMEM(small, F32), pltpu.VMEM((GMLP_CHUNK, GROUP_W), F32),
                        _lane_scratch(tm, GROUP_W), _lane_scratch(tm, GROUP_W),
                        _lane_scratch(tm, GROUP_W), _lane_scratch(tm, GROUP_W), pltpu.VMEM((hb, GROUP_W), F32),
                        pltpu.VMEM((tm, E), F32), pltpu.VMEM(small, F32), pltpu.VMEM((E, D), F32),
                        _lane_scratch(tm, GROUP_W)],
        compiler_params=_params(("arbitrary",)),
    )(dx, y, w_out, z, z, z, z_g, hs, hs, lru_a, lru_mult, o, mp["wA"], mp["wR"], mp["vec"], mp["wa"], mp["wx"], mp["ws"], mp["bs"])


def _attn_bwd(qkv, do, lse, delta, dil, name):
    S = qkv.shape[0] * qkv.shape[1]
    flat = lambda t: t.reshape(S, t.shape[2])
    qkv, do, lse, delta = flat(qkv), flat(do), flat(lse), flat(delta)
    nb = S // ATTN_BLOCK
    group = nb // dil
    scale = 1.0 / math.sqrt(HEAD_DIM)
    B = ATTN_BLOCK
    per_step = ATTN_BWD_BLOCKS_PER_STEP
    n_steps = nb // per_step

    def body(qc_ref, qn_ref, kc_ref, kp_ref, vc_ref, vp_ref, doc_ref, don_ref, lc_ref, ln_ref, dc_ref, dn_ref,
             dq_ref, dk_ref, dv_ref, bias_ref, bias_next_ref):
        n = pl.program_id(0)
        starts, next_starts = _group_starts(n, per_step, group)

        @pl.when(n == 0)
        def _():
            bias_ref[...] = _attn_bias(dil, (B,), 2 * B)
            bias_next_ref[...] = _attn_bias(dil, (B,), B)

        masks = _head_masks((B, GROUP_W))

        def per_row(tile):
            return jnp.concatenate([jnp.max(jnp.where(masks[h], tile, _NEG), axis=-1, keepdims=True)
                                    for h in range(N_HEADS)], axis=0)

        def grads(q, dov, lse_tile, dl_tile, keys, vals, bias, dead):
            qs = _stack_heads(q, masks)
            dos = _stack_heads(dov.astype(MXU_DTYPE), masks)
            s = _mm_nt(qs, keys) * scale + bias
            if dead is not None:
                s = jnp.where(dead(s.shape), _NEG, s)
            p = jnp.exp(s - per_row(lse_tile))
            ds = (p * (_mm_nt(dos, vals) - per_row(dl_tile)) * scale).astype(MXU_DTYPE)
            return ds, _mm_tn(ds, qs), _mm_tn(p.astype(MXU_DTYPE), dos)

        for j in range(per_step):
            own = slice(j * B, (j + 1) * B)
            before = slice((j - 1) * B, j * B)
            keys = jnp.concatenate([kp_ref[...] if j == 0 else kc_ref[before], kc_ref[own]], axis=0)
            vals = jnp.concatenate([vp_ref[...] if j == 0 else vc_ref[before], vc_ref[own]], axis=0)
            dead = None
            if starts(j) is not False:
                dead = lambda shape, j=j: starts(j) & (lax.broadcasted_iota(jnp.int32, shape, 1) < B)
            ds, dk2, dv2 = grads(qc_ref[own], doc_ref[own], lc_ref[own], dc_ref[own], keys, vals, bias_ref[...], dead)
            dq_ref[own] = _unstack_heads(jnp.dot(ds, keys, preferred_element_type=F32), masks).astype(dq_ref.dtype)
            if j > 0:
                dk_ref[before] = (dk_own + dk2[:B]).astype(dk_ref.dtype)
                dv_ref[before] = (dv_own + dv2[:B]).astype(dv_ref.dtype)
            dk_own, dv_own = dk2[B:], dv2[B:]
        last = slice((per_step - 1) * B, per_step * B)
        if next_starts is not True:
            _, dk1, dv1 = grads(qn_ref[...], don_ref[...], ln_ref[...], dn_ref[...], kc_ref[last], vc_ref[last],
                                bias_next_ref[...], lambda shape: next_starts)
            dk_own, dv_own = dk_own + dk1, dv_own + dv1
        dk_ref[last] = dk_own.astype(dk_ref.dtype)
        dv_ref[last] = dv_own.astype(dv_ref.dtype)

    blk = (per_step * B, GROUP_W)
    one = (B, GROUP_W)
    nxt_idx = lambda n: jnp.minimum((n + 1) * per_step, nb - 1)
    prv_idx = lambda n: jnp.maximum(n * per_step - 1, 0)
    zcur = lambda c: pl.BlockSpec(blk, lambda n: (n, c))
    znext = lambda c: pl.BlockSpec(one, lambda n: (nxt_idx(n), c))
    zprev = lambda c: pl.BlockSpec(one, lambda n: (prv_idx(n), c))
    cur = pl.BlockSpec(blk, lambda n: (n, 0))
    nxt = pl.BlockSpec(one, lambda n: (nxt_idx(n), 0))
    grads_out = pl.pallas_call(
        body, name=name, grid=(n_steps,),
        in_specs=[zcur(0), znext(0), zcur(1), zprev(1), zcur(2), zprev(2), cur, nxt, cur, nxt, cur, nxt],
        out_specs=[cur, cur, cur],
        out_shape=[jax.ShapeDtypeStruct((S, GROUP_W), WIRE_DTYPE)] * 3,
        scratch_shapes=[pltpu.VMEM((N_HEADS * B, 2 * B), F32), pltpu.VMEM((N_HEADS * B, B), F32)],
        compiler_params=_params(("arbitrary",)),
    )(qkv, qkv, qkv, qkv, qkv, qkv, do, do, lse, lse, delta, delta)
    return [t.reshape(dil, S // dil, GROUP_W) for t in grads_out]


def _inproj_bwd(x, g, dxn, dz_abc, dqkv, dz_g, w_t, name):
    S, D = x.shape
    N = w_t.shape[0]
    tm = TM_MM
    n_abc = N_ABC * GROUP_W

    def body(x_ref, g_ref, dxn_ref, dabc_ref, q1, k1, v1, q2, k2, v2, q3, k3, v3, dg_ref, w_ref,
             dx_ref, dz_ref, h_ref, dgn_ref, s4_ref, s16_ref, tmp_ref):
        i = pl.program_id(0)

        @pl.when(i == 0)
        def _():
            dgn_ref[...] = jnp.zeros_like(dgn_ref)

        dz_ref[:, 0:n_abc] = dabc_ref[...].astype(MXU_DTYPE)
        for j, parts in enumerate(((q1, q2, q3), (k1, k2, k3), (v1, v2, v3))):
            c0 = n_abc + j * GROUP_W
            _interleave(parts[1], s4_ref, ATTN_DILATIONS[1])
            _interleave(parts[2], s16_ref, ATTN_DILATIONS[2], tmp_ref)
            dz_ref[:, c0:c0 + GROUP_W] = (parts[0][0] + _get(s4_ref) + _get(s16_ref)).astype(MXU_DTYPE)
        dz_ref[:, n_abc + 3 * GROUP_W:] = dg_ref[...].astype(MXU_DTYPE)
        dh = jnp.dot(dz_ref[...], w_ref[...], preferred_element_type=F32)
        xv = x_ref[...]
        r = lax.rsqrt(jnp.mean(xv * xv, axis=-1, keepdims=True) + NORM_EPS)
        xn = xv * r
        gv = g_ref[...]
        h_ref[...] = (xn * gv).astype(MXU_DTYPE)
        dgn_ref[...] += _colsum(dh * xn)
        dn = dh * gv
        dx_ref[...] = dxn_ref[...] + r * (dn - xn * jnp.mean(dn * xn, axis=-1, keepdims=True))

    row = lambda w: pl.BlockSpec((tm, w), lambda i: (i, 0))
    flat = [t for p in dqkv for t in p]
    dil_specs = [_dilated_spec(tm, GROUP_W, dil) for dil in ATTN_DILATIONS for _ in range(3)]
    return pl.pallas_call(
        body, name=name, grid=(S // tm,),
        in_specs=[row(D), pl.BlockSpec((1, D), lambda i: (0, 0)), row(D), row(n_abc)] + dil_specs
                 + [row(GROUP_W), pl.BlockSpec((N, D), lambda i: (0, 0), pipeline_mode=pl.Buffered(1))],
        out_specs=[row(D), row(N), row(D), pl.BlockSpec((1, D), lambda i: (0, 0))],
        out_shape=[jax.ShapeDtypeStruct((S, D), F32), jax.ShapeDtypeStruct((S, N), MXU_DTYPE),
                   jax.ShapeDtypeStruct((S, D), MXU_DTYPE), jax.ShapeDtypeStruct((1, D), F32)],
        scratch_shapes=[_lane_scratch(tm, GROUP_W)] * 3,
        compiler_params=_params(("arbitrary",)),
    )(x, g, dxn, dz_abc, *flat, dz_g, w_t)


def _inproj_wgrad(h, dz, name):
    S, D = h.shape
    N = dz.shape[1]
    tm = TM_WGRAD
    nj = 2
    cw = N // nj
    per = N_DEV // nj
    n_loc = N // N_DEV

    def body(h_ref, dz_ref, dw_ref, acc_ref):
        i = pl.program_id(1)

        @pl.when(i == 0)
        def _():
            acc_ref[...] = jnp.zeros_like(acc_ref)

        acc_ref[...] += _mm_tn(dz_ref[...], h_ref[...])

        @pl.when(i == S // tm - 1)
        def _():
            for b in range(per):
                dw_ref[b] = acc_ref[b * n_loc:(b + 1) * n_loc, :].astype(dw_ref.dtype)

    return pl.pallas_call(
        body, name=name, grid=(nj, S // tm),
        in_specs=[pl.BlockSpec((tm, D), lambda j, i: (i, 0)), pl.BlockSpec((tm, cw), lambda j, i: (i, j))],
        out_specs=pl.BlockSpec((per, n_loc, D), lambda j, i: (j, 0, 0)),
        out_shape=jax.ShapeDtypeStruct((N_DEV, n_loc, D), WIRE_DTYPE),
        scratch_shapes=[pltpu.VMEM((cw, D), F32)],
        compiler_params=_params(("parallel", "arbitrary")),
    )(h, dz)


def _my_place():
    return lax.axis_index("x"), lax.axis_index("y"), lax.axis_index("c")


def _peer(x, y, c, k):
    px = 1 - x if k & 4 else x
    py = 1 - y if k & 2 else y
    pc = 1 - c if k & 1 else c
    return (px, py, pc), 4 * px + 2 * py + pc


HBM_SPEC = pl.BlockSpec(memory_space=pltpu.HBM)
SEM_SPEC = pl.BlockSpec(memory_space=pltpu.SEMAPHORE)
SPLIT_EFFECT = pltpu.SideEffectType.DATAFLOW_SIDE_EFFECTING
N_PEERS = N_DEV - 1


def _exchange_copies(srcs, lands, send_sems, recv_sems, whole, arrival):
    x, y, c = _my_place()
    me = 4 * x + 2 * y + c
    copies = []
    for t in range(len(srcs)):
        for k in range(1, N_DEV):
            peer, pidx = _peer(x, y, c, k)
            copies.append(pltpu.make_async_remote_copy(
                src_ref=srcs[t] if whole[t] else srcs[t].at[pidx],
                dst_ref=lands[t].at[pidx if arrival else me], send_sem=send_sems.at[t * N_PEERS + k - 1],
                recv_sem=recv_sems.at[t * N_PEERS + k - 1], device_id=peer, device_id_type=MESH))
    return copies


def _exchange_start(groups, name, after=None):
    sizes = [len(g) for g in groups]
    whole = [w for g in groups for _, w in g]
    srcs = [pltpu.with_memory_space_constraint(a, pltpu.HBM) for g in groups for a, _ in g]
    lands = [pltpu.with_memory_space_constraint(lax.empty(((N_DEV,) + a.shape) if w else a.shape, a.dtype), pltpu.HBM)
             for a, w in zip(srcs, whole)]
    n = len(srcs)
    n_g = len(groups)
    extra = [] if after is None else [after]
    n_in = 2 * n + len(extra)

    def body(*refs):
        src_refs, land_refs = refs[:n], refs[n:2 * n]
        sem_refs = refs[n_in + 2 * n:n_in + 2 * n + 2 * n_g]
        token = refs[-1]
        off = 0
        for gi, sz in enumerate(sizes):
            for send in _exchange_copies(src_refs[off:off + sz], land_refs[off:off + sz],
                                         sem_refs[2 * gi], sem_refs[2 * gi + 1], whole[off:off + sz], False):
                send.start()
            off += sz
        token[...] = jnp.zeros_like(token)

    sem_shapes = [pltpu.SemaphoreType.DMA((sz * N_PEERS,)) for sz in sizes for _ in range(2)]
    outs = pl.pallas_call(
        body, name=name,
        in_specs=[HBM_SPEC] * (2 * n) + [pl.BlockSpec(memory_space=pl.ANY)] * len(extra),
        out_specs=[HBM_SPEC] * (2 * n) + [SEM_SPEC] * (2 * n_g) + [pl.BlockSpec(memory_space=pltpu.VMEM)],
        out_shape=[pltpu.HBM(a.shape, a.dtype) for a in srcs + lands] + sem_shapes
                  + [jax.ShapeDtypeStruct((SUBLANES, LANES), F32)],
        input_output_aliases={i: i for i in range(2 * n)},
        compiler_params=pltpu.CompilerParams(has_side_effects=SPLIT_EFFECT),
    )(*srcs, *lands, *extra)
    handles, off = [], 0
    for gi, sz in enumerate(sizes):
        handles.append((outs[2 * n + 2 * gi], outs[2 * n + 2 * gi + 1], outs[off:off + sz], outs[n + off:n + off + sz],
                        whole[off:off + sz]))
        off += sz
    return handles, outs[-1]


def _exchange_wait(handle, after, name):
    send_sems, recv_sems, srcs, lands, whole = handle
    n = len(srcs)

    def body(*refs):
        src_refs, land_refs = refs[:n], refs[n:2 * n]
        for send in _exchange_copies(src_refs, land_refs, refs[2 * n], refs[2 * n + 1], whole, False):
            send.wait_send()
        for arrival in _exchange_copies(src_refs, land_refs, refs[2 * n], refs[2 * n + 1], whole, True):
            arrival.wait_recv()

    outs = pl.pallas_call(
        body, name=name,
        in_specs=[HBM_SPEC] * (2 * n) + [SEM_SPEC, SEM_SPEC, pl.BlockSpec(memory_space=pl.ANY)],
        out_specs=[HBM_SPEC] * (2 * n),
        out_shape=[pltpu.HBM(a.shape, a.dtype) for a in list(srcs) + list(lands)],
        input_output_aliases={i: i for i in range(2 * n)},
        compiler_params=pltpu.CompilerParams(has_side_effects=SPLIT_EFFECT),
    )(*srcs, *lands, send_sems, recv_sems, after)
    x, y, c = _my_place()
    me = 4 * x + 2 * y + c
    own = [s[None] if w else lax.dynamic_slice_in_dim(s, me, 1, axis=0) for s, w in zip(outs[:n], whole)]
    return [lax.dynamic_update_slice_in_dim(ld, o, me, axis=0) for ld, o in zip(outs[n:], own)]


def _sum_slots(parts, name):
    n = len(parts)

    def body(*refs):
        for p_ref, o_ref in zip(refs[:n], refs[n:]):
            acc = p_ref[0]
            for j in range(1, N_DEV):
                acc = acc + p_ref[j]
            o_ref[...] = acc

    vm = pl.BlockSpec(memory_space=pltpu.VMEM)
    return pl.pallas_call(
        body, name=name, in_specs=[vm] * n, out_specs=[vm] * n,
        out_shape=[jax.ShapeDtypeStruct(p.shape[1:], F32) for p in parts],
        compiler_params=pltpu.CompilerParams(vmem_limit_bytes=VMEM_LIMIT),
    )(*parts)


def _adamw_math(w, g, m, v):
    m = ADAM_B1 * m + (1.0 - ADAM_B1) * g
    v = ADAM_B2 * v + (1.0 - ADAM_B2) * (g * g)
    m_hat = m / (1.0 - ADAM_B1 ** ADAM_STEP)
    v_hat = v / (1.0 - ADAM_B2 ** ADAM_STEP)
    delta = -ADAM_LR * (m_hat / (jnp.sqrt(v_hat) + ADAM_EPS) + ADAM_WD * w)
    return delta, m, v


def _adamw_summed(parts, w, m, v, tr, name):
    depth, R, C = w.shape

    def body(*refs):
        p_refs = refs[:depth]
        w_ref, m_ref, v_ref, g_ref, d_ref, nm_ref, nv_ref = refs[depth:]
        lay = pl.program_id(0)
        for l in range(depth):
            @pl.when(lay == l)
            def _(p_ref=p_refs[l]):
                g = p_ref[0].astype(F32)
                for j in range(1, N_DEV):
                    g = g + p_ref[j].astype(F32)
                g_ref[0] = g
        d_ref[0], nm_ref[0], nv_ref[0] = _adamw_math(w_ref[0], g_ref[0], m_ref[0], v_ref[0])

    part_spec = lambda l: pl.BlockSpec((N_DEV, tr, C), lambda lay, i: (0, jnp.where(lay == l, i, 0), 0))
    row = pl.BlockSpec((1, tr, C), lambda lay, i: (lay, i, 0))
    return pl.pallas_call(
        body, name=name, grid=(depth, R // tr),
        in_specs=[part_spec(l) for l in range(depth)] + [row, row, row],
        out_specs=[row] * 4, out_shape=[jax.ShapeDtypeStruct((depth, R, C), F32)] * 4,
        compiler_params=_params(("arbitrary", "arbitrary")),
    )(*parts, w, m, v)


def _adamw_small(w, g, m, v, name):
    def body(w_ref, g_ref, m_ref, v_ref, d_ref, nm_ref, nv_ref):
        d_ref[...], nm_ref[...], nv_ref[...] = _adamw_math(w_ref[...], g_ref[...], m_ref[...], v_ref[...])

    vm = pl.BlockSpec(memory_space=pltpu.VMEM)
    return pl.pallas_call(
        body, name=name, in_specs=[vm] * 4, out_specs=[vm] * 3,
        out_shape=[jax.ShapeDtypeStruct(w.shape, F32)] * 3,
        compiler_params=pltpu.CompilerParams(vmem_limit_bytes=VMEM_LIMIT),
    )(w, g, m, v)


def _pack(arrays):
    flat = jnp.concatenate([a.reshape(-1) for a in arrays])
    pad = (-flat.shape[0]) % (SUBLANES * LANES)
    return jnp.pad(flat, (0, pad)).reshape(-1, LANES)


def _unpack(buf, like):
    flat = buf.reshape(-1)
    out, off = [], 0
    for a in like:
        out.append(flat[off:off + a.size].reshape(a.shape))
        off += a.size
    return out


def _block_diag(w):
    eye = jnp.eye(N_HEADS, dtype=w.dtype)
    return jnp.einsum('hij,hk->hikj', w, eye).reshape(GROUP_W, GROUP_W)


def _diag_blocks(w):
    return jnp.einsum('hihj->hij', w.reshape(N_HEADS, HEAD_DIM, N_HEADS, HEAD_DIM))


def _pad_rows(a):
    return jnp.pad(a, ((0, SUBLANES - a.shape[0]), (0, 0)))


def _mixer_params(l, conv_a_w, conv_r_w, conv_r_b, lru_wa, lru_ba, lru_wx, lru_bx, lru_lambda, gmlp_norm_g,
                  gmlp_ws, gmlp_bs):
    tril = jnp.tril(jnp.ones((GMLP_CHUNK, GMLP_CHUNK), dtype=bool))
    vec = jnp.stack([conv_r_b[l], lru_ba[l], lru_bx[l], lru_lambda[l], gmlp_norm_g[l]])
    return {
        "wA": _pad_rows(conv_a_w[l]), "wR": _pad_rows(conv_r_w[l]), "vec": _pad_rows(vec),
        "wa": _block_diag(lru_wa[l]).astype(MXU_DTYPE), "wx": _block_diag(lru_wx[l]).astype(MXU_DTYPE),
        "ws": jnp.where(tril[None], gmlp_ws[l], 0.0).astype(MXU_DTYPE),
        "bs": jnp.repeat(jnp.transpose(gmlp_bs[l]), HEAD_DIM, axis=1),
    }


MIXER_NAMES = ("conv_a_w", "conv_r_w", "conv_r_b", "lru_wa", "lru_ba", "lru_wx", "lru_bx", "lru_lambda",
               "gmlp_norm_g", "gmlp_ws", "gmlp_bs")
SMALL_NAMES = ("norm_g",) + MIXER_NAMES + ("final_g",)


def _local_step(x, loss_target, norm_g, get_w_in, get_w_out, emit_early, emit_late, conv_a_w, conv_r_w, conv_r_b,
                lru_wa, lru_ba, lru_wx, lru_bx, lru_lambda, gmlp_norm_g, gmlp_ws, gmlp_bs, final_g):
    depth = norm_g.shape[0]
    D = x.shape[1]
    small = (conv_a_w, conv_r_w, conv_r_b, lru_wa, lru_ba, lru_wx, lru_bx, lru_lambda, gmlp_norm_g, gmlp_ws, gmlp_bs)
    saved = []
    for l in range(depth):
        mp = _mixer_params(l, *small)
        w_in_l = get_w_in(l, x)
        z, z_g, *qkv, y_abc, hs, lru_a, lru_mult = _inproj_mix_fwd(
            x, norm_g[l].reshape(1, D), w_in_l, mp, f"inproj_mix_fwd_{l}")
        attn =[_attn_fwd(qkv[p], dil, f"attn_fwd_d{dil}_{l}") for p, dil in enumerate(ATTN_DILATIONS)]
        w_out_l = get_w_out(l, y_abc)
        x_new, y, o, *lse = _outproj(x, z_g, y_abc, attn, w_out_l, f"outproj_{l}")
        saved.append((x, z, z_g, qkv, (hs, lru_a, lru_mult), y, o, lse, mp, w_in_l, w_out_l))
        x = x_new
    dx, loss, d_final_g = _loss_head(x, final_g.reshape(1, D), loss_target, "loss_head")
    token = None
    for l in reversed(range(depth)):
        x_l, z, z_g, qkv, lru, y, o, lse, mp, w_in_l, w_out_l = saved[l]
        if token is not None:
            mp = dict(mp, vec=mp["vec"] + token[0, 0])
        (dw_out, dz_abc, dz_g, do1, do4, do16, dl1, dl4, dl16, dwA, dwR, dvec, dwa, dwx, dws, dbs) = _outproj_mix_bwd(
            dx, y, w_out_l, z, z_g, *lru, o, mp, f"outproj_mix_bwd_{l}")
        token = emit_early(l, dw_out, [
            dwA[:conv_a_w.shape[1]], dwR[:conv_r_w.shape[1]], dvec[0], _diag_blocks(dwa), dvec[1], _diag_blocks(dwx),
            dvec[2], dvec[3], dvec[4], dws, jnp.transpose(dbs[:, :N_HEADS])])
        g_row = norm_g[l].reshape(1, D)
        if token is not None:
            g_row = g_row + token[0, 0]
        dqkv = [_attn_bwd(qkv[p], do, lse[p], dl, dil, f"attn_bwd_d{dil}_{l}")
                for p, (dil, do, dl) in enumerate(zip(ATTN_DILATIONS, (do1, do4, do16), (dl1, dl4, dl16)))]
        dx, dz, h, dng = _inproj_bwd(x_l, g_row, dx, dz_abc, dqkv, dz_g, w_in_l, f"inproj_bwd_{l}")
        dw_in = _inproj_wgrad(h, dz, f"inproj_wgrad_{l}")
        token = emit_late(l, dw_in, [dng[0]] + ([d_final_g[0]] if l == depth - 1 else []))
    return loss[0, 0], dx
WEIGHT_NAMES = ("norm_g", "w_in", "conv_a_w", "conv_r_w", "conv_r_b", "lru_wa", "lru_ba", "lru_wx", "lru_bx",
                "lru_lambda", "gmlp_norm_g", "gmlp_ws", "gmlp_bs", "w_out", "final_g")


def kernel(x, norm_g, w_in, conv_a_w, conv_r_w, conv_r_b, lru_wa, lru_ba, lru_wx, lru_bx, lru_lambda, gmlp_norm_g, gmlp_ws, gmlp_bs, w_out, final_g, loss_target, m_norm_g, m_w_in, m_conv_a_w, m_conv_r_w, m_conv_r_b, m_lru_wa, m_lru_ba, m_lru_wx, m_lru_bx, m_lru_lambda, m_gmlp_norm_g, m_gmlp_ws, m_gmlp_bs, m_w_out, m_final_g, v_norm_g, v_w_in, v_conv_a_w, v_conv_r_w, v_conv_r_b, v_lru_wa, v_lru_ba, v_lru_wx, v_lru_bx, v_lru_lambda, v_gmlp_norm_g, v_gmlp_ws, v_gmlp_bs, v_w_out, v_final_g):
    w = dict(norm_g=norm_g, w_in=w_in, conv_a_w=conv_a_w, conv_r_w=conv_r_w, conv_r_b=conv_r_b, lru_wa=lru_wa,
             lru_ba=lru_ba, lru_wx=lru_wx, lru_bx=lru_bx, lru_lambda=lru_lambda, gmlp_norm_g=gmlp_norm_g,
             gmlp_ws=gmlp_ws, gmlp_bs=gmlp_bs, w_out=w_out, final_g=final_g)
    m = dict(norm_g=m_norm_g, w_in=m_w_in, conv_a_w=m_conv_a_w, conv_r_w=m_conv_r_w, conv_r_b=m_conv_r_b,
             lru_wa=m_lru_wa, lru_ba=m_lru_ba, lru_wx=m_lru_wx, lru_bx=m_lru_bx, lru_lambda=m_lru_lambda,
             gmlp_norm_g=m_gmlp_norm_g, gmlp_ws=m_gmlp_ws, gmlp_bs=m_gmlp_bs, w_out=m_w_out, final_g=m_final_g)
    v = dict(norm_g=v_norm_g, w_in=v_w_in, conv_a_w=v_conv_a_w, conv_r_w=v_conv_r_w, conv_r_b=v_conv_r_b,
             lru_wa=v_lru_wa, lru_ba=v_lru_ba, lru_wx=v_lru_wx, lru_bx=v_lru_bx, lru_lambda=v_lru_lambda,
             gmlp_norm_g=v_gmlp_norm_g, gmlp_ws=v_gmlp_ws, gmlp_bs=v_gmlp_bs, w_out=v_w_out, final_g=v_final_g)
    depth, D, n_loc = w_in.shape
    e_loc = w_out.shape[1]
    cx, cy, cc = _my_place()
    me = 4 * cx + 2 * cy + cc

    transposed = lambda a: jnp.transpose(a, (0, 2, 1))
    w_in_t, m_w_in_t, v_w_in_t = transposed(w_in), transposed(m_w_in), transposed(v_w_in)
    w_in_w, w_out_w = w_in_t.astype(MXU_DTYPE), w_out.astype(MXU_DTYPE)
    c_loc = conv_a_w.shape[2]
    taps = (conv_a_w, conv_r_w)
    first, _ = _exchange_start([[(w_in_w[0], True), (_pack(taps), True)], [(w_out_w[0], True)]], "gather_start_first")
    full_in = lambda g: g.reshape(N_DEV * n_loc, D)
    full_out = lambda g: g.reshape(N_DEV * e_loc, D)

    g_in0, g_taps = _exchange_wait(first[0], x, "gather_wait_in_0")
    groups = [[(w_in_w[l], True), (w_out_w[l], True)] for l in range(1, depth)]
    gathers, rest_token = _exchange_start(groups, "gather_start_rest", after=g_taps)
    g_taps = g_taps.reshape(N_DEV, -1) + rest_token[0, 0]
    conv_full, off = [], 0
    for a in taps:
        part = g_taps[:, off:off + a.size].reshape((N_DEV,) + a.shape)
        conv_full.append(jnp.transpose(part, (1, 2, 0, 3)).reshape(a.shape[:2] + (N_DEV * c_loc,)))
        off += a.size
    conv_a_full, conv_r_full = conv_full
    later = {}

    def get_w_in(l, after):
        if l == 0:
            return full_in(g_in0)
        g_in, later[l] = _exchange_wait(gathers[l - 1], after, f"gather_wait_{l}")
        return full_in(g_in)

    def get_w_out(l, after):
        if l == 0:
            return full_out(_exchange_wait(first[1], after, "gather_wait_out_0")[0])
        return full_out(later[l])

    early, late, last_token, held = {}, {}, [None], {}

    def emit_early(l, dw_out, mixer_grads):
        group = [(dw_out.reshape(N_DEV, e_loc, D), False), (_pack(mixer_grads), True)]
        if l > 0:
            held[l] = (group, mixer_grads)
            return None
        handles, token = _exchange_start([group], f"early_start_{l}")
        early[l] = (handles[0], mixer_grads)
        return token

    def emit_late(l, dw_in, norm_grads):
        groups = [[(_pack(norm_grads), True)], [(dw_in, False)]] + ([held[l][0]] if l in held else [])
        handles, token = _exchange_start(groups, f"late_start_{l}")
        late[l] = (handles[0], handles[1], norm_grads)
        if l in held:
            early[l] = (handles[2], held[l][1])
        last_token[0] = token
        return token

    loss, grad_x = _local_step(
        x[0], loss_target[0], norm_g, get_w_in, get_w_out, emit_early, emit_late, conv_a_full, conv_r_full, conv_r_b,
        lru_wa, lru_ba, lru_wx, lru_bx, lru_lambda, gmlp_norm_g, gmlp_ws, gmlp_bs, final_g)
    loss = lax.psum(loss, ("x", "y", "c"))

    r_in, r_out, small_parts = {}, {}, []
    for l in reversed(range(depth)):
        r_out[l], r_mix = _exchange_wait(early[l][0], last_token[0], f"early_wait_{l}")
        (r_norm,) = _exchange_wait(late[l][0], last_token[0], f"late_wait_norm_{l}")
        small_parts += [r_mix, r_norm]
        if l > 0:
            (r_in[l],) = _exchange_wait(late[l][1], last_token[0], f"late_wait_{l}")
    big = {"w_out": _adamw_summed([r_out[l] for l in range(depth)], w_out, m_w_out, v_w_out, e_loc, "adamw_w_out")}

    sums = _sum_slots(small_parts, "sum_small_grads")
    by_layer = {}
    for i, l in enumerate(reversed(range(depth))):
        mix = _unpack(sums[2 * i], early[l][1])
        nrm = _unpack(sums[2 * i + 1], late[l][2])
        by_layer[l] = dict(zip(MIXER_NAMES, mix), norm_g=nrm[0])
        if l == depth - 1:
            g_final = nrm[1]
    g_small = {k: jnp.stack([by_layer[l][k] for l in range(depth)]) for k in ("norm_g",) + MIXER_NAMES}
    g_small["final_g"] = g_final
    for k in ("conv_a_w", "conv_r_w"):
        g_small[k] = lax.dynamic_slice_in_dim(g_small[k], me * c_loc, c_loc, axis=2)
    packs = [_pack([d[k] for k in SMALL_NAMES]) for d in (w, g_small, m, v)]
    res = _adamw_small(*packs, "adamw_small")
    like = [w[k] for k in SMALL_NAMES]
    d_s, m_s, v_s = (dict(zip(SMALL_NAMES, _unpack(r, like))) for r in res)

    (r_in[0],) = _exchange_wait(late[0][1], res[0], "late_wait_0")
    big["w_in"] = [transposed(a) for a in _adamw_summed(
        [r_in[l] for l in range(depth)], w_in_t, m_w_in_t, v_w_in_t, n_loc // 2, "adamw_w_in")]

    grad, delta, new_m, new_v = {}, {}, {}, {}
    for k in WEIGHT_NAMES:
        if k in big:
            grad[k], delta[k], new_m[k], new_v[k] = big[k]
        else:
            grad[k], delta[k], new_m[k], new_v[k] = g_small[k], d_s[k], m_s[k], v_s[k]
    return (loss, grad_x[None], *[grad[k] for k in WEIGHT_NAMES], *[delta[k] for k in WEIGHT_NAMES],
            *[new_m[k] for k in WEIGHT_NAMES], *[new_v[k] for k in WEIGHT_NAMES])
```

```python
import functools
import math

import jax
import jax.numpy as jnp
from jax import lax
from jax.experimental import pallas as pl
from jax.experimental.pallas import tpu as pltpu

F32 = jnp.float32
MXU_DTYPE = jnp.bfloat16
WIRE_DTYPE = jnp.bfloat16
MESH = pl.DeviceIdType.MESH

N_DEV = 8
GROUP_W = 256
N_HEADS = 4
HEAD_DIM = 64
N_CHUNKS = 13
N_ABC = 9
GMLP_CHUNK = 128
ATTN_BLOCK = 128
ATTN_FWD_BLOCKS_PER_STEP = 32
ATTN_BWD_BLOCKS_PER_STEP = 8
ATTN_DILATIONS = (1, 4, 16)
NORM_EPS = 1e-6
RG_C = 8.0
SUBLANES = 8
LANES = 128
VMEM_LIMIT = 56 * 1024 * 1024

ADAM_LR = 0.001
ADAM_B1 = 0.9
ADAM_B2 = 0.999
ADAM_EPS = 1e-08
ADAM_WD = 0.01
ADAM_STEP = 10

TM_MIX = 512
TM_MM = 512
TM_WGRAD = 1024


def _params(sem, vmem=VMEM_LIMIT):
    return pltpu.CompilerParams(dimension_semantics=sem, vmem_limit_bytes=vmem)


def _mm(a, b):
    return jnp.dot(a.astype(MXU_DTYPE), b.astype(MXU_DTYPE), preferred_element_type=F32)


def _mm_tn(a, b):
    return lax.dot_general(a.astype(MXU_DTYPE), b.astype(MXU_DTYPE), (((0,), (0,)), ((), ())),
                           preferred_element_type=F32)


def _mm_nt(a, b):
    return lax.dot_general(a.astype(MXU_DTYPE), b.astype(MXU_DTYPE), (((1,), (1,)), ((), ())),
                           preferred_element_type=F32)


def _sigmoid(x):
    return 0.5 * jnp.tanh(0.5 * x) + 0.5


def _sigmoid_small_exact(x):
    return 1.0 / (1.0 + jnp.exp(-x))


def _silu_and_grad(x):
    s = _sigmoid(x)
    return x * s, s * (1.0 + x * (1.0 - s))


_GELU_K = math.sqrt(2.0 / math.pi)
_GELU_C = 0.044715


def _gelu_and_grad(x):
    x2 = x * x
    t = jnp.tanh(_GELU_K * (x + _GELU_C * x * x2))
    val = 0.5 * x * (1.0 + t)
    grad = 0.5 * (1.0 + t) + 0.5 * x * (1.0 - t * t) * (_GELU_K * (1.0 + 3.0 * _GELU_C * x2))
    return val, grad


def _gelu(x):
    return 0.5 * x * (1.0 + jnp.tanh(_GELU_K * (x + _GELU_C * x * x * x)))


def _expm1_nonpos(u):
    poly = 1.0 / math.factorial(9)
    for k in range(8, 0, -1):
        poly = poly * u + 1.0 / math.factorial(k)
    return jnp.where(u > -0.25, poly * u, jnp.exp(u) - 1.0)


def _softplus(x):
    return jnp.maximum(x, 0.0) + jnp.log(1.0 + jnp.exp(-jnp.abs(x)))


def _shift_down(t, halo, k):
    rolled = pltpu.roll(t, k, 0)
    hr = pltpu.roll(halo, k, 0)
    row = lax.broadcasted_iota(jnp.int32, halo.shape, 0)
    first = jnp.where(row < k, hr, rolled[0:SUBLANES])
    return jnp.concatenate([first, rolled[SUBLANES:]], axis=0)


def _shift_up(t, nxt, k):
    tm = t.shape[0]
    rolled = pltpu.roll(t, tm - k, 0)
    nr = pltpu.roll(nxt, SUBLANES - k, 0)
    row = lax.broadcasted_iota(jnp.int32, nxt.shape, 0)
    last = jnp.where(row >= SUBLANES - k, nr, rolled[tm - SUBLANES:tm])
    return jnp.concatenate([rolled[:tm - SUBLANES], last], axis=0)


def _scan_fwd(a, b):
    tm = a.shape[0]
    row = lax.broadcasted_iota(jnp.int32, a.shape, 0)
    s = 1
    while s < tm:
        a_s = pltpu.roll(a, s, 0)
        b_s = pltpu.roll(b, s, 0)
        m = row >= s
        b = jnp.where(m, a * b_s + b, b)
        a = jnp.where(m, a * a_s, a)
        s *= 2
    return a, b


def _scan_rev(a, g):
    tm = a.shape[0]
    row = lax.broadcasted_iota(jnp.int32, a.shape, 0)
    s = 1
    while s < tm:
        a_s = pltpu.roll(a, tm - s, 0)
        g_s = pltpu.roll(g, tm - s, 0)
        m = row < tm - s
        g = jnp.where(m, g + a * g_s, g)
        a = jnp.where(m, a * a_s, a)
        s *= 2
    return g


def _group_rows(scr_ref, row, n_groups):
    return jnp.concatenate([scr_ref[pl.ds(c, 1), pl.ds(row, n_groups, stride=SUBLANES), :][0]
                            for c in range(scr_ref.shape[0])], axis=1)


def _spread_rows(rows_ref, n_groups, w):
    return jnp.concatenate([jnp.broadcast_to(rows_ref[g:g + 1, :], (SUBLANES, w)) for g in range(n_groups)], axis=0)


def _scan_groups(a, b, reverse):
    tm, w = a.shape
    shape3 = (tm // SUBLANES, SUBLANES, w)
    a3, b3 = a.reshape(shape3), b.reshape(shape3)
    sub = lax.broadcasted_iota(jnp.int32, shape3, 1)
    s = 1
    while s < SUBLANES:
        shift = SUBLANES - s if reverse else s
        a_s = pltpu.roll(a3, shift, 1)
        b_s = pltpu.roll(b3, shift, 1)
        m = (sub < SUBLANES - s) if reverse else (sub >= s)
        b3 = jnp.where(m, a3 * b_s + b3, b3)
        a3 = jnp.where(m, a3 * a_s, a3)
        s *= 2
    return a3.reshape(tm, w), b3.reshape(tm, w)


def _scan_fwd_tile(a, b, h_in, sa_ref, sb_ref, sc_ref):
    tm, w = a.shape
    n_groups = tm // SUBLANES
    a_loc, b_loc = _scan_groups(a, b, False)
    _put(sa_ref, a_loc)
    _put(sb_ref, b_loc)
    a_end, b_end = _scan_fwd(_group_rows(sa_ref, SUBLANES - 1, n_groups), _group_rows(sb_ref, SUBLANES - 1, n_groups))
    h_end = b_end + a_end * h_in
    sc_ref[...] = _shift_down(h_end, jnp.broadcast_to(h_in, (SUBLANES, w)), 1)
    return b_loc + a_loc * _spread_rows(sc_ref, n_groups, w), h_end


def _scan_rev_tile(a, g, sa_ref, sb_ref, sc_ref):
    tm, w = a.shape
    n_groups = tm // SUBLANES
    a_loc, g_loc = _scan_groups(a, g, True)
    _put(sa_ref, a_loc)
    _put(sb_ref, g_loc)
    d_first = _scan_rev(_group_rows(sa_ref, 0, n_groups), _group_rows(sb_ref, 0, n_groups))
    sc_ref[...] = _shift_up(d_first, jnp.zeros((SUBLANES, w), F32), 1)
    return g_loc + a_loc * _spread_rows(sc_ref, n_groups, w)


def _lane_scratch(tm, w):
    return pltpu.VMEM((w // LANES, tm, LANES), F32)


def _put(scr_ref, val):
    for c in range(scr_ref.shape[0]):
        scr_ref[c] = val[:, c * LANES:(c + 1) * LANES].astype(F32)


def _get(scr_ref):
    return jnp.concatenate([scr_ref[c] for c in range(scr_ref.shape[0])], axis=1)


MAX_ROW_STRIDE = 4


def _strided_rows(c, start, n, stride):
    return (pl.ds(c, 1), pl.ds(start, n, stride=stride), slice(None))


def _deinterleave(src_ref, dst_ref, dil, tmp_ref=None):
    nc, tm, _ = src_ref.shape
    s1 = min(dil, MAX_ROW_STRIDE)
    s2 = dil // s1
    if s2 > 1:
        for r0 in range(s1):
            for c in range(nc):
                tmp_ref[c, r0 * (tm // s1):(r0 + 1) * (tm // s1), :] = src_ref[_strided_rows(c, r0, tm // s1, s1)][0]
    for r in range(dil):
        r1, r0 = divmod(r, s1)
        for c in range(nc):
            if dil == 1:
                piece = src_ref[c]
            elif s2 == 1:
                piece = src_ref[_strided_rows(c, r, tm // dil, dil)][0]
            else:
                piece = tmp_ref[_strided_rows(c, r0 * (tm // s1) + r1, tm // dil, s2)][0]
            dst_ref[r, :, c * LANES:(c + 1) * LANES] = piece.astype(dst_ref.dtype)


def _interleave(src_ref, dst_ref, dil, tmp_ref=None):
    nc, tm, _ = dst_ref.shape
    s1 = min(dil, MAX_ROW_STRIDE)
    s2 = dil // s1
    for r in range(dil):
        r1, r0 = divmod(r, s1)
        for c in range(nc):
            piece = src_ref[r, :, c * LANES:(c + 1) * LANES].astype(F32)[None]
            if s2 == 1:
                dst_ref[_strided_rows(c, r, tm // dil, dil)] = piece
            else:
                tmp_ref[_strided_rows(c, r0 * (tm // s1) + r1, tm // dil, s2)] = piece
    if s2 > 1:
        for r0 in range(s1):
            for c in range(nc):
                dst_ref[_strided_rows(c, r0, tm // s1, s1)] = (
                    tmp_ref[c, r0 * (tm // s1):(r0 + 1) * (tm // s1), :][None])


def _dilated_spec(tm, w, dil, index=lambda i: i):
    return pl.BlockSpec((dil, tm // dil, w), lambda i: (0, index(i), 0))


def _dilated_shape(S, w, dil, dtype):
    return jax.ShapeDtypeStruct((dil, S // dil, w), dtype)


def _head_masks(shape):
    lane = lax.broadcasted_iota(jnp.int32, shape, 1)
    return [(lane >= h * HEAD_DIM) & (lane < (h + 1) * HEAD_DIM) for h in range(N_HEADS)]


def _colsum(v):
    return jnp.sum(v, axis=0, keepdims=True)


def _conv_a(z_of, halo_of, w_ref):
    p = z_of(2) * z_of(0)
    p_h = halo_of(2) * halo_of(0)
    cv = w_ref[2:3, :] * p + w_ref[1:2, :] * _shift_down(p, p_h, 1) + w_ref[0:1, :] * _shift_down(p, p_h, 2)
    return p, p_h, cv


def _lru_gates(z_of, halo_of, wr_ref, vec_ref, wa_ref, wx_ref, saved=None):
    rx = z_of(4)
    rx_h = halo_of(4)
    sh = [rx, _shift_down(rx, rx_h, 1), _shift_down(rx, rx_h, 2), _shift_down(rx, rx_h, 3)]
    xc = (wr_ref[3:4, :] * sh[0] + wr_ref[2:3, :] * sh[1] + wr_ref[1:2, :] * sh[2]
          + wr_ref[0:1, :] * sh[3] + vec_ref[0:1, :])
    ga = _sigmoid_small_exact(jnp.dot(xc.astype(MXU_DTYPE), wa_ref[...], preferred_element_type=F32) + vec_ref[1:2, :])
    gi = _sigmoid(jnp.dot(xc.astype(MXU_DTYPE), wx_ref[...], preferred_element_type=F32) + vec_ref[2:3, :])
    sp = _softplus(-vec_ref[3:4, :])
    if saved is not None:
        return (xc, sh, ga, gi) + tuple(saved) + (sp,)
    log_a = (-RG_C * ga) * sp
    a = jnp.exp(log_a)
    mult = jnp.sqrt(-_expm1_nonpos(2.0 * log_a))
    return xc, sh, ga, gi, a, mult, sp


def _gmlp_fwd(z_of, vec_ref, ws_ref, bs_ref, tm):
    u = _gelu(z_of(6))
    gv = _gelu(z_of(7))
    rr = lax.rsqrt(jnp.mean(gv * gv, axis=-1, keepdims=True) + NORM_EPS)
    vn = (gv * rr) * vec_ref[4:5, :]
    masks = _head_masks((GMLP_CHUNK, GROUP_W))
    parts = []
    for c in range(tm // GMLP_CHUNK):
        vc = vn[c * GMLP_CHUNK:(c + 1) * GMLP_CHUNK].astype(MXU_DTYPE)
        acc = bs_ref[...]
        for h in range(N_HEADS):
            acc = acc + jnp.where(masks[h], jnp.dot(ws_ref[h], vc, preferred_element_type=F32), 0.0)
        parts.append(acc)
    return u, gv, rr, vn, jnp.concatenate(parts, axis=0)


def _mix_specs():
    const2 = lambda shape: pl.BlockSpec(shape, lambda i: (0, 0))
    return [const2((SUBLANES, GROUP_W)), const2((SUBLANES, GROUP_W)), const2((SUBLANES, GROUP_W)),
            const2((GROUP_W, GROUP_W)), const2((GROUP_W, GROUP_W)),
            pl.BlockSpec((N_HEADS, GMLP_CHUNK, GMLP_CHUNK), lambda i: (0, 0, 0)),
            const2((GMLP_CHUNK, GROUP_W))]


def _inproj_mix_fwd(x, g, w_t, mp, name):
    S, D = x.shape
    N = w_t.shape[0]
    tm = TM_MIX
    hb = tm // SUBLANES
    n_abc = N_ABC * GROUP_W
    n_qkv = 3 * GROUP_W

    def body(x_ref, g_ref, w_ref, wA_ref, wR_ref, vec_ref, wa_ref, wx_ref, ws_ref, bs_ref,
             z_ref, zg_ref, q1_ref, q4_ref, q16_ref, y_ref, h_ref, a_ref, mult_ref,
             qkv_ref, halo_ref, carry_ref, sa_ref, sb_ref, sc_ref, tmp_ref):
        @pl.when(pl.program_id(0) == 0)
        def _():
            halo_ref[...] = jnp.zeros_like(halo_ref)
            carry_ref[...] = jnp.zeros_like(carry_ref)

        xv = x_ref[...]
        r = lax.rsqrt(jnp.mean(xv * xv, axis=-1, keepdims=True) + NORM_EPS)
        hn = ((xv * r) * g_ref[...]).astype(MXU_DTYPE)
        z_ref[...] = _mm_nt(hn, w_ref[0:n_abc, :])
        _put(qkv_ref, _mm_nt(hn, w_ref[n_abc:n_abc + n_qkv, :]))
        zg_ref[...] = _mm_nt(hn, w_ref[n_abc + n_qkv:, :])
        for dil, ref in zip(ATTN_DILATIONS, (q1_ref, q4_ref, q16_ref)):
            _deinterleave(qkv_ref, ref, dil, tmp_ref)

        z_of = lambda c: z_ref[:, c * GROUP_W:(c + 1) * GROUP_W]
        halo_of = lambda c: halo_ref[:, c * GROUP_W:(c + 1) * GROUP_W]

        _, _, cv = _conv_a(z_of, halo_of, wA_ref)
        y_ref[:, 0:GROUP_W] = (z_of(1) * cv * _silu_and_grad(z_of(3))[0]).astype(y_ref.dtype)

        xc, _, _, gi, a, mult, _ = _lru_gates(z_of, halo_of, wR_ref, vec_ref, wa_ref, wx_ref)
        a_ref[...] = a
        mult_ref[...] = mult
        b = mult * (gi * xc)
        h, h_end = _scan_fwd_tile(a, b, carry_ref[SUBLANES - 1:SUBLANES, :], sa_ref, sb_ref, sc_ref)
        h_ref[...] = h
        carry_ref[...] = h_end[hb - SUBLANES:hb]
        y_ref[:, GROUP_W:2 * GROUP_W] = (h * _silu_and_grad(z_of(5))[0]).astype(y_ref.dtype)

        u, _, _, _, sp = _gmlp_fwd(z_of, vec_ref, ws_ref, bs_ref, tm)
        y_ref[:, 2 * GROUP_W:3 * GROUP_W] = (u * sp * _silu_and_grad(z_of(8))[0]).astype(y_ref.dtype)
        halo_ref[...] = z_ref[tm - SUBLANES:tm, :]

    row = lambda wd: pl.BlockSpec((tm, wd), lambda i: (i, 0))
    return pl.pallas_call(
        body, name=name, grid=(S // tm,),
        in_specs=[row(D), pl.BlockSpec((1, D), lambda i: (0, 0)),
                  pl.BlockSpec((N, D), lambda i: (0, 0), pipeline_mode=pl.Buffered(1))] + _mix_specs(),
        out_specs=[row(n_abc), row(GROUP_W)] + [_dilated_spec(tm, n_qkv, dil) for dil in ATTN_DILATIONS]
                  + [row(3 * GROUP_W)] + [row(GROUP_W)] * 3,
        out_shape=[jax.ShapeDtypeStruct((S, n_abc), F32), jax.ShapeDtypeStruct((S, GROUP_W), F32)]
                  + [_dilated_shape(S, n_qkv, dil, MXU_DTYPE) for dil in ATTN_DILATIONS]
                  + [jax.ShapeDtypeStruct((S, 3 * GROUP_W), MXU_DTYPE)] + [jax.ShapeDtypeStruct((S, GROUP_W), F32)] * 3,
        scratch_shapes=[_lane_scratch(tm, n_qkv), pltpu.VMEM((SUBLANES, n_abc), F32),
                        pltpu.VMEM((SUBLANES, GROUP_W), F32), _lane_scratch(tm, GROUP_W), _lane_scratch(tm, GROUP_W),
                        pltpu.VMEM((hb, GROUP_W), F32), _lane_scratch(tm, n_qkv)],
        compiler_params=_params(("arbitrary",)),
    )(x, g, w_t, mp["wA"], mp["wR"], mp["vec"], mp["wa"], mp["wx"], mp["ws"], mp["bs"])


_NEG = -1e30


def _slope(h):
    return 2.0 ** (-8.0 * (h + 1) / N_HEADS)


def _attn_bias(dil, offsets, n_keys):
    shape = (ATTN_BLOCK, n_keys)
    qi = lax.broadcasted_iota(jnp.int32, shape, 0)
    ki = lax.broadcasted_iota(jnp.int32, shape, 1)
    blocks = []
    for f in offsets:
        delta = qi + f - ki
        valid = (delta >= 0) & (delta <= ATTN_BLOCK)
        dist = (delta * dil).astype(F32)
        for h in range(N_HEADS):
            blocks.append(jnp.where(valid, -_slope(h) * dist, _NEG))
    return jnp.concatenate(blocks, axis=0)


def _stack_heads(t, masks):
    return jnp.concatenate([jnp.where(m, t, jnp.zeros_like(t)) for m in masks], axis=0)


def _unstack_heads(t4, masks, base=0):
    out = t4[base * ATTN_BLOCK:(base + 1) * ATTN_BLOCK]
    for h in range(1, N_HEADS):
        out = jnp.where(masks[h], t4[(base + h) * ATTN_BLOCK:(base + h + 1) * ATTN_BLOCK], out)
    return out


def _group_starts(n, per_step, group):
    if per_step % group == 0:
        return (lambda j: j % group == 0), True
    steps = group // per_step
    return (lambda j: (n % steps == 0) if j == 0 else False), (n + 1) % steps == 0


def _attn_fwd(qkv, dil, name):
    S = qkv.shape[0] * qkv.shape[1]
    qkv = qkv.reshape(S, qkv.shape[2])
    nb = S // ATTN_BLOCK
    group = nb // dil
    scale = 1.0 / math.sqrt(HEAD_DIM)
    B = ATTN_BLOCK
    per_step = ATTN_FWD_BLOCKS_PER_STEP

    def body(q_ref, kc_ref, kp_ref, vc_ref, vp_ref, o_ref, l_ref, bias_ref):
        n = pl.program_id(0)

        @pl.when(n == 0)
        def _():
            bias_ref[...] = _attn_bias(dil, (B,), 2 * B)

        masks = _head_masks((B, GROUP_W))
        starts, _ = _group_starts(n, per_step, group)
        for j in range(per_step):
            own = slice(j * B, (j + 1) * B)
            before = slice((j - 1) * B, j * B)
            qs = _stack_heads(q_ref[own], masks)
            keys = jnp.concatenate([kp_ref[...] if j == 0 else kc_ref[before], kc_ref[own]], axis=0)
            vals = jnp.concatenate([vp_ref[...] if j == 0 else vc_ref[before], vc_ref[own]], axis=0)
            s = _mm_nt(qs, keys) * scale + bias_ref[...]
            if starts(j) is not False:
                key_col = lax.broadcasted_iota(jnp.int32, s.shape, 1)
                s = jnp.where(starts(j) & (key_col < B), _NEG, s)
            m = jnp.max(s, axis=-1, keepdims=True)
            p = jnp.exp(s - m)
            l = jnp.sum(p, axis=-1, keepdims=True)
            o4 = jnp.dot(p.astype(MXU_DTYPE), vals, preferred_element_type=F32)
            o_ref[own] = (_unstack_heads(o4, masks)
                          / _unstack_heads(jnp.broadcast_to(l, o4.shape), masks)).astype(o_ref.dtype)
            l_ref[own] = _unstack_heads(jnp.broadcast_to(m + jnp.log(l), o4.shape), masks)

    blk = (per_step * B, GROUP_W)
    cur = lambda c: pl.BlockSpec(blk, lambda n: (n, c))
    prev = lambda c: pl.BlockSpec((B, GROUP_W), lambda n: (jnp.maximum(n * per_step - 1, 0), c))
    out = pl.BlockSpec(blk, lambda n: (n, 0))
    o, l = pl.pallas_call(
        body, name=name, grid=(nb // per_step,),
        in_specs=[cur(0), cur(1), prev(1), cur(2), prev(2)],
        out_specs=[out, out],
        out_shape=[jax.ShapeDtypeStruct((S, GROUP_W), MXU_DTYPE), jax.ShapeDtypeStruct((S, GROUP_W), F32)],
        scratch_shapes=[pltpu.VMEM((N_HEADS * ATTN_BLOCK, 2 * ATTN_BLOCK), F32)],
        compiler_params=_params(("arbitrary",)),
    )(qkv, qkv, qkv, qkv, qkv)
    return o.reshape(dil, S // dil, GROUP_W), l.reshape(dil, S // dil, GROUP_W)


def _outproj(x, z_g, y_abc, attn, w_out, name):
    S, D = x.shape
    tm = TM_MM
    n_abc = 3 * GROUP_W

    def body(x_ref, g_ref, yabc_ref, o1, l1, o2, l2, o3, l3, w_ref,
             xn_ref, y_ref, o_ref, lse1_ref, lse4_ref, lse16_ref, so2, sl2, so3, sl3, slse, tmp_ref):
        for src, dst, dil in ((o2, so2, ATTN_DILATIONS[1]), (l2, sl2, ATTN_DILATIONS[1]),
                              (o3, so3, ATTN_DILATIONS[2]), (l3, sl3, ATTN_DILATIONS[2])):
            _interleave(src, dst, dil, tmp_ref)
        la, lb, lc = l1[0], _get(sl2), _get(sl3)
        mx = jnp.maximum(jnp.maximum(la, lb), lc)
        ea, eb, ec = jnp.exp(la - mx), jnp.exp(lb - mx), jnp.exp(lc - mx)
        den = ea + eb + ec
        o = (ea * o1[0].astype(F32) + eb * _get(so2) + ec * _get(so3)) / den
        o_ref[...] = o
        _put(slse, mx + jnp.log(den))
        for dil, ref in zip(ATTN_DILATIONS, (lse1_ref, lse4_ref, lse16_ref)):
            _deinterleave(slse, ref, dil, tmp_ref)
        y_d = o * _silu_and_grad(g_ref[...])[0]
        y_ref[:, 0:n_abc] = yabc_ref[...].astype(MXU_DTYPE)
        y_ref[:, n_abc:] = y_d.astype(MXU_DTYPE)
        xn_ref[...] = x_ref[...] + jnp.dot(y_ref[...], w_ref[...], preferred_element_type=F32)

    row = lambda w: pl.BlockSpec((tm, w), lambda i: (i, 0))
    dil_specs = [_dilated_spec(tm, GROUP_W, dil) for dil in ATTN_DILATIONS]
    (o1, l1), (o2, l2), (o3, l3) = attn
    return pl.pallas_call(
        body, name=name, grid=(S // tm,),
        in_specs=[row(D), row(GROUP_W), row(n_abc)] + [sp for sp in dil_specs for _ in range(2)]
                 + [pl.BlockSpec(w_out.shape, lambda i: (0, 0))],
        out_specs=[row(D), row(4 * GROUP_W), row(GROUP_W)] + dil_specs,
        out_shape=[jax.ShapeDtypeStruct((S, D), F32), jax.ShapeDtypeStruct((S, 4 * GROUP_W), MXU_DTYPE),
                   jax.ShapeDtypeStruct((S, GROUP_W), F32)]
                  + [_dilated_shape(S, GROUP_W, dil, F32) for dil in ATTN_DILATIONS],
        scratch_shapes=[_lane_scratch(tm, GROUP_W)] * 6,
        compiler_params=_params(("parallel",)),
    )(x, z_g, y_abc, o1, l1, o2, l2, o3, l3, w_out)


def _loss_head(x, g, target, name):
    S, D = x.shape
    tm = TM_MM

    def body(x_ref, g_ref, t_ref, dx_ref, loss_ref, dg_ref):
        i = pl.program_id(0)

        @pl.when(i == 0)
        def _():
            loss_ref[...] = jnp.zeros_like(loss_ref)
            dg_ref[...] = jnp.zeros_like(dg_ref)

        xv = x_ref[...]
        r = lax.rsqrt(jnp.mean(xv * xv, axis=-1, keepdims=True) + NORM_EPS)
        xn = xv * r
        err = xn * g_ref[...] - t_ref[...]
        per_tok = jnp.mean(err * err, axis=-1, keepdims=True)
        loss_ref[...] += 0.5 * jnp.sum(per_tok, axis=0, keepdims=True)
        dout = err * (1.0 / D)
        dg_ref[...] += _colsum(dout * xn)
        dxn = dout * g_ref[...]
        dx_ref[...] = r * (dxn - xn * jnp.mean(dxn * xn, axis=-1, keepdims=True))

    row = pl.BlockSpec((tm, D), lambda i: (i, 0))
    return pl.pallas_call(
        body, name=name, grid=(S // tm,),
        in_specs=[row, pl.BlockSpec((1, D), lambda i: (0, 0)), row],
        out_specs=[row, pl.BlockSpec((1, LANES), lambda i: (0, 0)), pl.BlockSpec((1, D), lambda i: (0, 0))],
        out_shape=[jax.ShapeDtypeStruct((S, D), F32), jax.ShapeDtypeStruct((1, LANES), F32),
                   jax.ShapeDtypeStruct((1, D), F32)],
        compiler_params=_params(("arbitrary",)),
    )(x, g, target)


def _outproj_mix_bwd(dx, y, w_out, z, z_g, hs, lru_a, lru_mult, o, mp, name):
    S, D = dx.shape
    E = y.shape[1]
    tm = TM_MIX
    hb = tm // SUBLANES
    nT = S // tm
    last_blk = S // SUBLANES - 1
    wcols = N_ABC * GROUP_W

    def body(dx_ref, y_ref, w_ref, z_ref, zh_ref, zn_ref, zg_ref, h_ref, hh_ref, a_ref, mult_ref, o_ref,
             wA_ref, wR_ref, vec_ref, wa_ref, wx_ref, ws_ref, bs_ref,
             dw_ref, dz_ref, dzg_ref, do1_ref, do4_ref, do16_ref, dl1_ref, dl4_ref, dl16_ref,
             dwA_ref, dwR_ref, dvec_ref, dwa_ref, dwx_ref, dws_ref, dbs_ref,
             hcarry_ref, xcarry_ref, bsacc_ref, do_ref, dl_ref, sa_ref, sb_ref, sc_ref, dy_ref, dyn_ref, acc_ref,
             tmp_ref):
        i = pl.program_id(0)
        ti = nT - 1 - i

        @pl.when(i == 0)
        def _():
            acc_ref[...] = jnp.zeros_like(acc_ref)
            dyn_ref[...] = jnp.zeros_like(dyn_ref)
            hcarry_ref[...] = jnp.zeros_like(hcarry_ref)
            xcarry_ref[...] = jnp.zeros_like(xcarry_ref)
            bsacc_ref[...] = jnp.zeros_like(bsacc_ref)
            dwA_ref[...] = jnp.zeros_like(dwA_ref)
            dwR_ref[...] = jnp.zeros_like(dwR_ref)
            dvec_ref[...] = jnp.zeros_like(dvec_ref)
            dwa_ref[...] = jnp.zeros_like(dwa_ref)
            dwx_ref[...] = jnp.zeros_like(dwx_ref)
            dws_ref[...] = jnp.zeros_like(dws_ref)
            dbs_ref[...] = jnp.zeros_like(dbs_ref)

        dxb = dx_ref[...].astype(MXU_DTYPE)
        dy_ref[...] = _mm_nt(dxb, w_ref[...])
        acc_ref[...] += _mm_tn(y_ref[...], dxb)

        @pl.when(i == nT - 1)
        def _():
            dw_ref[...] = acc_ref[...].astype(dw_ref.dtype)

        has_prev = ti > 0
        has_next = i > 0
        col = lambda c: slice(c * GROUP_W, (c + 1) * GROUP_W)
        z_of = lambda c: z_ref[:, col(c)]
        halo_of = lambda c: jnp.where(has_prev, zh_ref[:, col(c)], 0.0)
        next_of = lambda c: zn_ref[:, col(c)]

        p, p_h, cv = _conv_a(z_of, halo_of, wA_ref)
        sg, dsg = _silu_and_grad(z_of(3))
        a_b = z_of(1)
        dya = dy_ref[:, col(0)]
        dcv = dya * a_b * sg
        dcv_n = jnp.where(has_next, dyn_ref[...] * next_of(1) * _silu_and_grad(next_of(3))[0], 0.0)
        dp = (wA_ref[2:3, :] * dcv + wA_ref[1:2, :] * _shift_up(dcv, dcv_n, 1)
              + wA_ref[0:1, :] * _shift_up(dcv, dcv_n, 2))
        dwA_ref[2:3, :] += _colsum(dcv * p)
        dwA_ref[1:2, :] += _colsum(dcv * _shift_down(p, p_h, 1))
        dwA_ref[0:1, :] += _colsum(dcv * _shift_down(p, p_h, 2))
        def put_dz(c, val):
            dz_ref[:, col(c)] = val.astype(dz_ref.dtype)

        put_dz(0, dp * z_of(2))
        put_dz(1, dya * cv * sg)
        put_dz(2, dp * z_of(0))
        put_dz(3, dya * a_b * cv * dsg)

        xc, sh, ga, gi, a, mult, sp = _lru_gates(z_of, halo_of, wR_ref, vec_ref, wa_ref, wx_ref,
                                                 saved=(a_ref[...], mult_ref[...]))
        h = h_ref[...]
        h_prev = _shift_down(h, jnp.where(has_prev, hh_ref[...], 0.0), 1)
        sgr, dsgr = _silu_and_grad(z_of(5))
        dyb = dy_ref[:, col(1)]
        put_dz(5, dyb * h * dsgr)
        row = lax.broadcasted_iota(jnp.int32, (tm, GROUP_W), 0)
        g_in = dyb * sgr + jnp.where(row == tm - 1, hcarry_ref[0:1, :], 0.0)
        a_up = _shift_up(a, jnp.zeros((SUBLANES, GROUP_W), F32), 1)
        dH = _scan_rev_tile(a_up, g_in, sa_ref, sb_ref, sc_ref)
        hcarry_ref[...] = (a * dH)[0:SUBLANES]
        da = dH * h_prev
        gx = gi * xc
        dmult = dH * gx
        dgi = dH * mult * xc
        dxc = dH * mult * gi
        dlog_a = da * a - dmult * (a * a) / mult
        dga = dlog_a * (-RG_C * sp)
        dlam_row = _colsum(dlog_a * (-RG_C * ga)) * (-_sigmoid(-vec_ref[3:4, :]))
        dpre_a = dga * ga * (1.0 - ga)
        dpre_i = dgi * gi * (1.0 - gi)
        dwa_ref[...] += _mm_tn(xc, dpre_a)
        dwx_ref[...] += _mm_tn(xc, dpre_i)
        dxc = dxc + _mm_nt(dpre_a, wa_ref[...]) + _mm_nt(dpre_i, wx_ref[...])
        dvec_ref[0:1, :] += _colsum(dxc)
        dvec_ref[1:2, :] += _colsum(dpre_a)
        dvec_ref[2:3, :] += _colsum(dpre_i)
        dvec_ref[3:4, :] += dlam_row
        for k in range(4):
            dwR_ref[k:k + 1, :] += _colsum(dxc * sh[3 - k])
        dxc_n = xcarry_ref[...]
        put_dz(4, wR_ref[3:4, :] * dxc + wR_ref[2:3, :] * _shift_up(dxc, dxc_n, 1)
               + wR_ref[1:2, :] * _shift_up(dxc, dxc_n, 2) + wR_ref[0:1, :] * _shift_up(dxc, dxc_n, 3))
        xcarry_ref[...] = dxc[0:SUBLANES]

        c_u, c_v = z_of(6), z_of(7)
        u, du_dx = _gelu_and_grad(c_u)
        gv, dgv_dx = _gelu_and_grad(c_v)
        rr = lax.rsqrt(jnp.mean(gv * gv, axis=-1, keepdims=True) + NORM_EPS)
        xhat = gv * rr
        g_c = vec_ref[4:5, :]
        vn = xhat * g_c
        masks = _head_masks((GMLP_CHUNK, GROUP_W))
        tri_r = lax.broadcasted_iota(jnp.int32, (GMLP_CHUNK, GMLP_CHUNK), 0)
        tri_c = lax.broadcasted_iota(jnp.int32, (GMLP_CHUNK, GMLP_CHUNK), 1)
        tril = tri_r >= tri_c
        sgc, dsgc = _silu_and_grad(z_of(8))
        dyc = dy_ref[:, col(2)]
        dsp_full = dyc * u * sgc
        sp_parts, dvn_parts = [], []
        for c in range(tm // GMLP_CHUNK):
            rs = slice(c * GMLP_CHUNK, (c + 1) * GMLP_CHUNK)
            vc = vn[rs].astype(MXU_DTYPE)
            dsp_c = dsp_full[rs]
            bsacc_ref[...] += dsp_c
            acc = bs_ref[...]
            dvn_c = jnp.zeros((GMLP_CHUNK, GROUP_W), F32)
            for h in range(N_HEADS):
                w_h = ws_ref[h]
                acc = acc + jnp.where(masks[h], jnp.dot(w_h, vc, preferred_element_type=F32), 0.0)
                dsp_h = jnp.where(masks[h], dsp_c, 0.0).astype(MXU_DTYPE)
                dvn_c = dvn_c + _mm_tn(w_h, dsp_h)
                dws_ref[h] += jnp.where(tril, _mm_nt(dsp_h, vc), 0.0)
            sp_parts.append(acc)
            dvn_parts.append(dvn_c)
        spv = jnp.concatenate(sp_parts, axis=0)
        dvn = jnp.concatenate(dvn_parts, axis=0)
        put_dz(6, dyc * spv * sgc * du_dx)
        put_dz(8, dyc * u * spv * dsgc)
        dvec_ref[4:5, :] += _colsum(dvn * xhat)
        dgvn = dvn * g_c
        dgv = rr * (dgvn - xhat * jnp.mean(dgvn * xhat, axis=-1, keepdims=True))
        put_dz(7, dgv * dgv_dx)

        sgd, dsgd = _silu_and_grad(zg_ref[...])
        dyd = dy_ref[:, col(3)]
        ov = o_ref[...]
        do = dyd * sgd
        _put(do_ref, do)
        dzg_ref[...] = (dyd * ov * dsgd).astype(dzg_ref.dtype)
        prod = do * ov
        tmasks = _head_masks((tm, GROUP_W))
        dl = jnp.zeros((tm, GROUP_W), F32)
        for h in range(N_HEADS):
            dl = jnp.where(tmasks[h], jnp.sum(jnp.where(tmasks[h], prod, 0.0), axis=-1, keepdims=True), dl)
        _put(dl_ref, dl)
        for dil, d_out, l_out in zip(ATTN_DILATIONS, (do1_ref, do4_ref, do16_ref), (dl1_ref, dl4_ref, dl16_ref)):
            _deinterleave(do_ref, d_out, dil, tmp_ref)
            _deinterleave(dl_ref, l_out, dil, tmp_ref)

        @pl.when(i == nT - 1)
        def _():
            acc = bsacc_ref[...]
            lane = lax.broadcasted_iota(jnp.int32, (GMLP_CHUNK, LANES), 1)
            out = jnp.zeros((GMLP_CHUNK, LANES), F32)
            for h in range(N_HEADS):
                out = jnp.where(lane == h, jnp.sum(jnp.where(masks[h], acc, 0.0), axis=-1, keepdims=True), out)
            dbs_ref[...] = out

        dyn_ref[...] = dy_ref[0:SUBLANES, 0:GROUP_W]

    rev = lambda w: pl.BlockSpec((tm, w), lambda i: (nT - 1 - i, 0))
    prev8 = lambda w: pl.BlockSpec((SUBLANES, w), lambda i: (jnp.maximum((nT - 1 - i) * hb - 1, 0), 0))
    next8 = lambda w: pl.BlockSpec((SUBLANES, w), lambda i: (jnp.minimum((nT - i) * hb, last_blk), 0))
    const2 = lambda shape: pl.BlockSpec(shape, lambda i: (0, 0))
    dil_specs = [_dilated_spec(tm, GROUP_W, dil, lambda i: nT - 1 - i) for dil in ATTN_DILATIONS]
    dil_shapes = [_dilated_shape(S, GROUP_W, dil, F32) for dil in ATTN_DILATIONS]
    small = (SUBLANES, GROUP_W)
    sq = (GROUP_W, GROUP_W)
    ws_shape = (N_HEADS, GMLP_CHUNK, GMLP_CHUNK)
    return pl.pallas_call(
        body, name=name, grid=(nT,),
        in_specs=[rev(D), rev(E), pl.BlockSpec((E, D), lambda i: (0, 0), pipeline_mode=pl.Buffered(1)),
                  rev(wcols), prev8(wcols), next8(wcols), rev(GROUP_W), rev(GROUP_W), prev8(GROUP_W),
                  rev(GROUP_W), rev(GROUP_W), rev(GROUP_W)]
                 + _mix_specs(),
        out_specs=[const2((E, D)), rev(wcols), rev(GROUP_W)] + dil_specs + dil_specs
                  + [const2(small), const2(small), const2(small), const2(sq), const2(sq),
                     pl.BlockSpec(ws_shape, lambda i: (0, 0, 0)), const2((GMLP_CHUNK, LANES))],
        out_shape=[jax.ShapeDtypeStruct((E, D), WIRE_DTYPE),
                   jax.ShapeDtypeStruct((S, wcols), MXU_DTYPE), jax.ShapeDtypeStruct((S, GROUP_W), MXU_DTYPE)]
                  + [_dilated_shape(S, GROUP_W, dil, MXU_DTYPE) for dil in ATTN_DILATIONS] + dil_shapes
                  + [jax.ShapeDtypeStruct(small, F32)] * 3 + [jax.ShapeDtypeStruct(sq, F32)] * 2
                  + [jax.ShapeDtypeStruct(ws_shape, F32), jax.ShapeDtypeStruct((GMLP_CHUNK, LANES), F32)],
        scratch_shapes=[pltpu.VMEM(small, F32), pltpu.VMEM(small, F32), pltpu.VMEM((GMLP_CHUNK, GROUP_W), F32),
                        _lane_scratch(tm, GROUP_W), _lane_scratch(tm, GROUP_W),
                        _lane_scratch(tm, GROUP_W), _lane_scratch(tm, GROUP_W), pltpu.VMEM((hb, GROUP_W), F32),
                        pltpu.VMEM((tm, E), F32), pltpu.VMEM(small, F32), pltpu.VMEM((E, D), F32),
                        _lane_scratch(tm, GROUP_W)],
        compiler_params=_params(("arbitrary",)),
    )(dx, y, w_out, z, z, z, z_g, hs, hs, lru_a, lru_mult, o, mp["wA"], mp["wR"], mp["vec"], mp["wa"], mp["wx"], mp["ws"], mp["bs"])


def _attn_bwd(qkv, do, lse, delta, dil, name):
    S = qkv.shape[0] * qkv.shape[1]
    flat = lambda t: t.reshape(S, t.shape[2])
    qkv, do, lse, delta = flat(qkv), flat(do), flat(lse), flat(delta)
    nb = S // ATTN_BLOCK
    group = nb // dil
    scale = 1.0 / math.sqrt(HEAD_DIM)
    B = ATTN_BLOCK
    per_step = ATTN_BWD_BLOCKS_PER_STEP
    n_steps = nb // per_step

    def body(qc_ref, qn_ref, kc_ref, kp_ref, vc_ref, vp_ref, doc_ref, don_ref, lc_ref, ln_ref, dc_ref, dn_ref,
             dq_ref, dk_ref, dv_ref, bias_ref, bias_next_ref):
        n = pl.program_id(0)
        starts, next_starts = _group_starts(n, per_step, group)

        @pl.when(n == 0)
        def _():
            bias_ref[...] = _attn_bias(dil, (B,), 2 * B)
            bias_next_ref[...] = _attn_bias(dil, (B,), B)

        masks = _head_masks((B, GROUP_W))

        def per_row(tile):
            return jnp.concatenate([jnp.max(jnp.where(masks[h], tile, _NEG), axis=-1, keepdims=True)
                                    for h in range(N_HEADS)], axis=0)

        def grads(q, dov, lse_tile, dl_tile, keys, vals, bias, dead):
            qs = _stack_heads(q, masks)
            dos = _stack_heads(dov.astype(MXU_DTYPE), masks)
            s = _mm_nt(qs, keys) * scale + bias
            if dead is not None:
                s = jnp.where(dead(s.shape), _NEG, s)
            p = jnp.exp(s - per_row(lse_tile))
            ds = (p * (_mm_nt(dos, vals) - per_row(dl_tile)) * scale).astype(MXU_DTYPE)
            return ds, _mm_tn(ds, qs), _mm_tn(p.astype(MXU_DTYPE), dos)

        for j in range(per_step):
            own = slice(j * B, (j + 1) * B)
            before = slice((j - 1) * B, j * B)
            keys = jnp.concatenate([kp_ref[...] if j == 0 else kc_ref[before], kc_ref[own]], axis=0)
            vals = jnp.concatenate([vp_ref[...] if j == 0 else vc_ref[before], vc_ref[own]], axis=0)
            dead = None
            if starts(j) is not False:
                dead = lambda shape, j=j: starts(j) & (lax.broadcasted_iota(jnp.int32, shape, 1) < B)
            ds, dk2, dv2 = grads(qc_ref[own], doc_ref[own], lc_ref[own], dc_ref[own], keys, vals, bias_ref[...], dead)
            dq_ref[own] = _unstack_heads(jnp.dot(ds, keys, preferred_element_type=F32), masks).astype(dq_ref.dtype)
            if j > 0:
                dk_ref[before] = (dk_own + dk2[:B]).astype(dk_ref.dtype)
                dv_ref[before] = (dv_own + dv2[:B]).astype(dv_ref.dtype)
            dk_own, dv_own = dk2[B:], dv2[B:]
        last = slice((per_step - 1) * B, per_step * B)
        if next_starts is not True:
            _, dk1, dv1 = grads(qn_ref[...], don_ref[...], ln_ref[...], dn_ref[...], kc_ref[last], vc_ref[last],
                                bias_next_ref[...], lambda shape: next_starts)
            dk_own, dv_own = dk_own + dk1, dv_own + dv1
        dk_ref[last] = dk_own.astype(dk_ref.dtype)
        dv_ref[last] = dv_own.astype(dv_ref.dtype)

    blk = (per_step * B, GROUP_W)
    one = (B, GROUP_W)
    nxt_idx = lambda n: jnp.minimum((n + 1) * per_step, nb - 1)
    prv_idx = lambda n: jnp.maximum(n * per_step - 1, 0)
    zcur = lambda c: pl.BlockSpec(blk, lambda n: (n, c))
    znext = lambda c: pl.BlockSpec(one, lambda n: (nxt_idx(n), c))
    zprev = lambda c: pl.BlockSpec(one, lambda n: (prv_idx(n), c))
    cur = pl.BlockSpec(blk, lambda n: (n, 0))
    nxt = pl.BlockSpec(one, lambda n: (nxt_idx(n), 0))
    grads_out = pl.pallas_call(
        body, name=name, grid=(n_steps,),
        in_specs=[zcur(0), znext(0), zcur(1), zprev(1), zcur(2), zprev(2), cur, nxt, cur, nxt, cur, nxt],
        out_specs=[cur, cur, cur],
        out_shape=[jax.ShapeDtypeStruct((S, GROUP_W), WIRE_DTYPE)] * 3,
        scratch_shapes=[pltpu.VMEM((N_HEADS * B, 2 * B), F32), pltpu.VMEM((N_HEADS * B, B), F32)],
        compiler_params=_params(("arbitrary",)),
    )(qkv, qkv, qkv, qkv, qkv, qkv, do, do, lse, lse, delta, delta)
    return [t.reshape(dil, S // dil, GROUP_W) for t in grads_out]


def _inproj_bwd(x, g, dxn, dz_abc, dqkv, dz_g, w_t, name):
    S, D = x.shape
    N = w_t.shape[0]
    tm = TM_MM
    n_abc = N_ABC * GROUP_W

    def body(x_ref, g_ref, dxn_ref, dabc_ref, q1, k1, v1, q2, k2, v2, q3, k3, v3, dg_ref, w_ref,
             dx_ref, dz_ref, h_ref, dgn_ref, s4_ref, s16_ref, tmp_ref):
        i = pl.program_id(0)

        @pl.when(i == 0)
        def _():
            dgn_ref[...] = jnp.zeros_like(dgn_ref)

        dz_ref[:, 0:n_abc] = dabc_ref[...].astype(MXU_DTYPE)
        for j, parts in enumerate(((q1, q2, q3), (k1, k2, k3), (v1, v2, v3))):
            c0 = n_abc + j * GROUP_W
            _interleave(parts[1], s4_ref, ATTN_DILATIONS[1])
            _interleave(parts[2], s16_ref, ATTN_DILATIONS[2], tmp_ref)
            dz_ref[:, c0:c0 + GROUP_W] = (parts[0][0] + _get(s4_ref) + _get(s16_ref)).astype(MXU_DTYPE)
        dz_ref[:, n_abc + 3 * GROUP_W:] = dg_ref[...].astype(MXU_DTYPE)
        dh = jnp.dot(dz_ref[...], w_ref[...], preferred_element_type=F32)
        xv = x_ref[...]
        r = lax.rsqrt(jnp.mean(xv * xv, axis=-1, keepdims=True) + NORM_EPS)
        xn = xv * r
        gv = g_ref[...]
        h_ref[...] = (xn * gv).astype(MXU_DTYPE)
        dgn_ref[...] += _colsum(dh * xn)
        dn = dh * gv
        dx_ref[...] = dxn_ref[...] + r * (dn - xn * jnp.mean(dn * xn, axis=-1, keepdims=True))

    row = lambda w: pl.BlockSpec((tm, w), lambda i: (i, 0))
    flat = [t for p in dqkv for t in p]
    dil_specs = [_dilated_spec(tm, GROUP_W, dil) for dil in ATTN_DILATIONS for _ in range(3)]
    return pl.pallas_call(
        body, name=name, grid=(S // tm,),
        in_specs=[row(D), pl.BlockSpec((1, D), lambda i: (0, 0)), row(D), row(n_abc)] + dil_specs
                 + [row(GROUP_W), pl.BlockSpec((N, D), lambda i: (0, 0), pipeline_mode=pl.Buffered(1))],
        out_specs=[row(D), row(N), row(D), pl.BlockSpec((1, D), lambda i: (0, 0))],
        out_shape=[jax.ShapeDtypeStruct((S, D), F32), jax.ShapeDtypeStruct((S, N), MXU_DTYPE),
                   jax.ShapeDtypeStruct((S, D), MXU_DTYPE), jax.ShapeDtypeStruct((1, D), F32)],
        scratch_shapes=[_lane_scratch(tm, GROUP_W)] * 3,
        compiler_params=_params(("arbitrary",)),
    )(x, g, dxn, dz_abc, *flat, dz_g, w_t)


def _inproj_wgrad(h, dz, name):
    S, D = h.shape
    N = dz.shape[1]
    tm = TM_WGRAD
    nj = 2
    cw = N // nj
    per = N_DEV // nj
    n_loc = N // N_DEV

    def body(h_ref, dz_ref, dw_ref, acc_ref):
        i = pl.program_id(1)

        @pl.when(i == 0)
        def _():
            acc_ref[...] = jnp.zeros_like(acc_ref)

        acc_ref[...] += _mm_tn(dz_ref[...], h_ref[...])

        @pl.when(i == S // tm - 1)
        def _():
            for b in range(per):
                dw_ref[b] = acc_ref[b * n_loc:(b + 1) * n_loc, :].astype(dw_ref.dtype)

    return pl.pallas_call(
        body, name=name, grid=(nj, S // tm),
        in_specs=[pl.BlockSpec((tm, D), lambda j, i: (i, 0)), pl.BlockSpec((tm, cw), lambda j, i: (i, j))],
        out_specs=pl.BlockSpec((per, n_loc, D), lambda j, i: (j, 0, 0)),
        out_shape=jax.ShapeDtypeStruct((N_DEV, n_loc, D), WIRE_DTYPE),
        scratch_shapes=[pltpu.VMEM((cw, D), F32)],
        compiler_params=_params(("parallel", "arbitrary")),
    )(h, dz)


def _my_place():
    return lax.axis_index("x"), lax.axis_index("y"), lax.axis_index("c")


def _peer(x, y, c, k):
    px = 1 - x if k & 4 else x
    py = 1 - y if k & 2 else y
    pc = 1 - c if k & 1 else c
    return (px, py, pc), 4 * px + 2 * py + pc


HBM_SPEC = pl.BlockSpec(memory_space=pltpu.HBM)
SEM_SPEC = pl.BlockSpec(memory_space=pltpu.SEMAPHORE)
SPLIT_EFFECT = pltpu.SideEffectType.DATAFLOW_SIDE_EFFECTING
N_PEERS = N_DEV - 1


def _exchange_copies(srcs, lands, send_sems, recv_sems, whole, arrival):
    x, y, c = _my_place()
    me = 4 * x + 2 * y + c
    copies = []
    for t in range(len(srcs)):
        for k in range(1, N_DEV):
            peer, pidx = _peer(x, y, c, k)
            copies.append(pltpu.make_async_remote_copy(
                src_ref=srcs[t] if whole[t] else srcs[t].at[pidx],
                dst_ref=lands[t].at[pidx if arrival else me], send_sem=send_sems.at[t * N_PEERS + k - 1],
                recv_sem=recv_sems.at[t * N_PEERS + k - 1], device_id=peer, device_id_type=MESH))
    return copies


def _exchange_start(groups, name, after=None):
    sizes = [len(g) for g in groups]
    whole = [w for g in groups for _, w in g]
    srcs = [pltpu.with_memory_space_constraint(a, pltpu.HBM) for g in groups for a, _ in g]
    lands = [pltpu.with_memory_space_constraint(lax.empty(((N_DEV,) + a.shape) if w else a.shape, a.dtype), pltpu.HBM)
             for a, w in zip(srcs, whole)]
    n = len(srcs)
    n_g = len(groups)
    extra = [] if after is None else [after]
    n_in = 2 * n + len(extra)

    def body(*refs):
        src_refs, land_refs = refs[:n], refs[n:2 * n]
        sem_refs = refs[n_in + 2 * n:n_in + 2 * n + 2 * n_g]
        token = refs[-1]
        off = 0
        for gi, sz in enumerate(sizes):
            for send in _exchange_copies(src_refs[off:off + sz], land_refs[off:off + sz],
                                         sem_refs[2 * gi], sem_refs[2 * gi + 1], whole[off:off + sz], False):
                send.start()
            off += sz
        token[...] = jnp.zeros_like(token)

    sem_shapes = [pltpu.SemaphoreType.DMA((sz * N_PEERS,)) for sz in sizes for _ in range(2)]
    outs = pl.pallas_call(
        body, name=name,
        in_specs=[HBM_SPEC] * (2 * n) + [pl.BlockSpec(memory_space=pl.ANY)] * len(extra),
        out_specs=[HBM_SPEC] * (2 * n) + [SEM_SPEC] * (2 * n_g) + [pl.BlockSpec(memory_space=pltpu.VMEM)],
        out_shape=[pltpu.HBM(a.shape, a.dtype) for a in srcs + lands] + sem_shapes
                  + [jax.ShapeDtypeStruct((SUBLANES, LANES), F32)],
        input_output_aliases={i: i for i in range(2 * n)},
        compiler_params=pltpu.CompilerParams(has_side_effects=SPLIT_EFFECT),
    )(*srcs, *lands, *extra)
    handles, off = [], 0
    for gi, sz in enumerate(sizes):
        handles.append((outs[2 * n + 2 * gi], outs[2 * n + 2 * gi + 1], outs[off:off + sz], outs[n + off:n + off + sz],
                        whole[off:off + sz]))
        off += sz
    return handles, outs[-1]


def _exchange_wait(handle, after, name):
    send_sems, recv_sems, srcs, lands, whole = handle
    n = len(srcs)

    def body(*refs):
        src_refs, land_refs = refs[:n], refs[n:2 * n]
        for send in _exchange_copies(src_refs, land_refs, refs[2 * n], refs[2 * n + 1], whole, False):
            send.wait_send()
        for arrival in _exchange_copies(src_refs, land_refs, refs[2 * n], refs[2 * n + 1], whole, True):
            arrival.wait_recv()

    outs = pl.pallas_call(
        body, name=name,
        in_specs=[HBM_SPEC] * (2 * n) + [SEM_SPEC, SEM_SPEC, pl.BlockSpec(memory_space=pl.ANY)],
        out_specs=[HBM_SPEC] * (2 * n),
        out_shape=[pltpu.HBM(a.shape, a.dtype) for a in list(srcs) + list(lands)],
        input_output_aliases={i: i for i in range(2 * n)},
        compiler_params=pltpu.CompilerParams(has_side_effects=SPLIT_EFFECT),
    )(*srcs, *lands, send_sems, recv_sems, after)
    x, y, c = _my_place()
    me = 4 * x + 2 * y + c
    own = [s[None] if w else lax.dynamic_slice_in_dim(s, me, 1, axis=0) for s, w in zip(outs[:n], whole)]
    return [lax.dynamic_update_slice_in_dim(ld, o, me, axis=0) for ld, o in zip(outs[n:], own)]


def _sum_slots(parts, name):
    n = len(parts)

    def body(*refs):
        for p_ref, o_ref in zip(refs[:n], refs[n:]):
            acc = p_ref[0]
            for j in range(1, N_DEV):
                acc = acc + p_ref[j]
            o_ref[...] = acc

    vm = pl.BlockSpec(memory_space=pltpu.VMEM)
    return pl.pallas_call(
        body, name=name, in_specs=[vm] * n, out_specs=[vm] * n,
        out_shape=[jax.ShapeDtypeStruct(p.shape[1:], F32) for p in parts],
        compiler_params=pltpu.CompilerParams(vmem_limit_bytes=VMEM_LIMIT),
    )(*parts)


def _adamw_math(w, g, m, v):
    m = ADAM_B1 * m + (1.0 - ADAM_B1) * g
    v = ADAM_B2 * v + (1.0 - ADAM_B2) * (g * g)
    m_hat = m / (1.0 - ADAM_B1 ** ADAM_STEP)
    v_hat = v / (1.0 - ADAM_B2 ** ADAM_STEP)
    delta = -ADAM_LR * (m_hat / (jnp.sqrt(v_hat) + ADAM_EPS) + ADAM_WD * w)
    return delta, m, v


def _adamw_summed(parts, w, m, v, tr, name):
    depth, R, C = w.shape

    def body(*refs):
        p_refs = refs[:depth]
        w_ref, m_ref, v_ref, g_ref, d_ref, nm_ref, nv_ref = refs[depth:]
        lay = pl.program_id(0)
        for l in range(depth):
            @pl.when(lay == l)
            def _(p_ref=p_refs[l]):
                g = p_ref[0].astype(F32)
                for j in range(1, N_DEV):
                    g = g + p_ref[j].astype(F32)
                g_ref[0] = g
        d_ref[0], nm_ref[0], nv_ref[0] = _adamw_math(w_ref[0], g_ref[0], m_ref[0], v_ref[0])

    part_spec = lambda l: pl.BlockSpec((N_DEV, tr, C), lambda lay, i: (0, jnp.where(lay == l, i, 0), 0))
    row = pl.BlockSpec((1, tr, C), lambda lay, i: (lay, i, 0))
    return pl.pallas_call(
        body, name=name, grid=(depth, R // tr),
        in_specs=[part_spec(l) for l in range(depth)] + [row, row, row],
        out_specs=[row] * 4, out_shape=[jax.ShapeDtypeStruct((depth, R, C), F32)] * 4,
        compiler_params=_params(("arbitrary", "arbitrary")),
    )(*parts, w, m, v)


def _adamw_small(w, g, m, v, name):
    def body(w_ref, g_ref, m_ref, v_ref, d_ref, nm_ref, nv_ref):
        d_ref[...], nm_ref[...], nv_ref[...] = _adamw_math(w_ref[...], g_ref[...], m_ref[...], v_ref[...])

    vm = pl.BlockSpec(memory_space=pltpu.VMEM)
    return pl.pallas_call(
        body, name=name, in_specs=[vm] * 4, out_specs=[vm] * 3,
        out_shape=[jax.ShapeDtypeStruct(w.shape, F32)] * 3,
        compiler_params=pltpu.CompilerParams(vmem_limit_bytes=VMEM_LIMIT),
    )(w, g, m, v)


def _pack(arrays):
    flat = jnp.concatenate([a.reshape(-1) for a in arrays])
    pad = (-flat.shape[0]) % (SUBLANES * LANES)
    return jnp.pad(flat, (0, pad)).reshape(-1, LANES)


def _unpack(buf, like):
    flat = buf.reshape(-1)
    out, off = [], 0
    for a in like:
        out.append(flat[off:off + a.size].reshape(a.shape))
        off += a.size
    return out


def _block_diag(w):
    eye = jnp.eye(N_HEADS, dtype=w.dtype)
    return jnp.einsum('hij,hk->hikj', w, eye).reshape(GROUP_W, GROUP_W)


def _diag_blocks(w):
    return jnp.einsum('hihj->hij', w.reshape(N_HEADS, HEAD_DIM, N_HEADS, HEAD_DIM))


def _pad_rows(a):
    return jnp.pad(a, ((0, SUBLANES - a.shape[0]), (0, 0)))


def _mixer_params(l, conv_a_w, conv_r_w, conv_r_b, lru_wa, lru_ba, lru_wx, lru_bx, lru_lambda, gmlp_norm_g,
                  gmlp_ws, gmlp_bs):
    tril = jnp.tril(jnp.ones((GMLP_CHUNK, GMLP_CHUNK), dtype=bool))
    vec = jnp.stack([conv_r_b[l], lru_ba[l], lru_bx[l], lru_lambda[l], gmlp_norm_g[l]])
    return {
        "wA": _pad_rows(conv_a_w[l]), "wR": _pad_rows(conv_r_w[l]), "vec": _pad_rows(vec),
        "wa": _block_diag(lru_wa[l]).astype(MXU_DTYPE), "wx": _block_diag(lru_wx[l]).astype(MXU_DTYPE),
        "ws": jnp.where(tril[None], gmlp_ws[l], 0.0).astype(MXU_DTYPE),
        "bs": jnp.repeat(jnp.transpose(gmlp_bs[l]), HEAD_DIM, axis=1),
    }


MIXER_NAMES = ("conv_a_w", "conv_r_w", "conv_r_b", "lru_wa", "lru_ba", "lru_wx", "lru_bx", "lru_lambda",
               "gmlp_norm_g", "gmlp_ws", "gmlp_bs")
SMALL_NAMES = ("norm_g",) + MIXER_NAMES + ("final_g",)


def _local_step(x, loss_target, norm_g, get_w_in, get_w_out, emit_early, emit_late, conv_a_w, conv_r_w, conv_r_b,
                lru_wa, lru_ba, lru_wx, lru_bx, lru_lambda, gmlp_norm_g, gmlp_ws, gmlp_bs, final_g):
    depth = norm_g.shape[0]
    D = x.shape[1]
    small = (conv_a_w, conv_r_w, conv_r_b, lru_wa, lru_ba, lru_wx, lru_bx, lru_lambda, gmlp_norm_g, gmlp_ws, gmlp_bs)
    saved = []
    for l in range(depth):
        mp = _mixer_params(l, *small)
        w_in_l = get_w_in(l, x)
        z, z_g, *qkv, y_abc, hs, lru_a, lru_mult = _inproj_mix_fwd(
            x, norm_g[l].reshape(1, D), w_in_l, mp, f"inproj_mix_fwd_{l}")
        attn =[_attn_fwd(qkv[p], dil, f"attn_fwd_d{dil}_{l}") for p, dil in enumerate(ATTN_DILATIONS)]
        w_out_l = get_w_out(l, y_abc)
        x_new, y, o, *lse = _outproj(x, z_g, y_abc, attn, w_out_l, f"outproj_{l}")
        saved.append((x, z, z_g, qkv, (hs, lru_a, lru_mult), y, o, lse, mp, w_in_l, w_out_l))
        x = x_new
    dx, loss, d_final_g = _loss_head(x, final_g.reshape(1, D), loss_target, "loss_head")
    token = None
    for l in reversed(range(depth)):
        x_l, z, z_g, qkv, lru, y, o, lse, mp, w_in_l, w_out_l = saved[l]
        if token is not None:
            mp = dict(mp, vec=mp["vec"] + token[0, 0])
        (dw_out, dz_abc, dz_g, do1, do4, do16, dl1, dl4, dl16, dwA, dwR, dvec, dwa, dwx, dws, dbs) = _outproj_mix_bwd(
            dx, y, w_out_l, z, z_g, *lru, o, mp, f"outproj_mix_bwd_{l}")
        token = emit_early(l, dw_out, [
            dwA[:conv_a_w.shape[1]], dwR[:conv_r_w.shape[1]], dvec[0], _diag_blocks(dwa), dvec[1], _diag_blocks(dwx),
            dvec[2], dvec[3], dvec[4], dws, jnp.transpose(dbs[:, :N_HEADS])])
        g_row = norm_g[l].reshape(1, D)
        if token is not None:
            g_row = g_row + token[0, 0]
        dqkv = [_attn_bwd(qkv[p], do, lse[p], dl, dil, f"attn_bwd_d{dil}_{l}")
                for p, (dil, do, dl) in enumerate(zip(ATTN_DILATIONS, (do1, do4, do16), (dl1, dl4, dl16)))]
        dx, dz, h, dng = _inproj_bwd(x_l, g_row, dx, dz_abc, dqkv, dz_g, w_in_l, f"inproj_bwd_{l}")
        dw_in = _inproj_wgrad(h, dz, f"inproj_wgrad_{l}")
        token = emit_late(l, dw_in, [dng[0]] + ([d_final_g[0]] if l == depth - 1 else []))
    return loss[0, 0], dx
WEIGHT_NAMES = ("norm_g", "w_in", "conv_a_w", "conv_r_w", "conv_r_b", "lru_wa", "lru_ba", "lru_wx", "lru_bx",
                "lru_lambda", "gmlp_norm_g", "gmlp_ws", "gmlp_bs", "w_out", "final_g")


def kernel(x, norm_g, w_in, conv_a_w, conv_r_w, conv_r_b, lru_wa, lru_ba, lru_wx, lru_bx, lru_lambda, gmlp_norm_g, gmlp_ws, gmlp_bs, w_out, final_g, loss_target, m_norm_g, m_w_in, m_conv_a_w, m_conv_r_w, m_conv_r_b, m_lru_wa, m_lru_ba, m_lru_wx, m_lru_bx, m_lru_lambda, m_gmlp_norm_g, m_gmlp_ws, m_gmlp_bs, m_w_out, m_final_g, v_norm_g, v_w_in, v_conv_a_w, v_conv_r_w, v_conv_r_b, v_lru_wa, v_lru_ba, v_lru_wx, v_lru_bx, v_lru_lambda, v_gmlp_norm_g, v_gmlp_ws, v_gmlp_bs, v_w_out, v_final_g):
    w = dict(norm_g=norm_g, w_in=w_in, conv_a_w=conv_a_w, conv_r_w=conv_r_w, conv_r_b=conv_r_b, lru_wa=lru_wa,
             lru_ba=lru_ba, lru_wx=lru_wx, lru_bx=lru_bx, lru_lambda=lru_lambda, gmlp_norm_g=gmlp_norm_g,
             gmlp_ws=gmlp_ws, gmlp_bs=gmlp_bs, w_out=w_out, final_g=final_g)
    m = dict(norm_g=m_norm_g, w_in=m_w_in, conv_a_w=m_conv_a_w, conv_r_w=m_conv_r_w, conv_r_b=m_conv_r_b,
             lru_wa=m_lru_wa, lru_ba=m_lru_ba, lru_wx=m_lru_wx, lru_bx=m_lru_bx, lru_lambda=m_lru_lambda,
             gmlp_norm_g=m_gmlp_norm_g, gmlp_ws=m_gmlp_ws, gmlp_bs=m_gmlp_bs, w_out=m_w_out, final_g=m_final_g)
    v = dict(norm_g=v_norm_g, w_in=v_w_in, conv_a_w=v_conv_a_w, conv_r_w=v_conv_r_w, conv_r_b=v_conv_r_b,
             lru_wa=v_lru_wa, lru_ba=v_lru_ba, lru_wx=v_lru_wx, lru_bx=v_lru_bx, lru_lambda=v_lru_lambda,
             gmlp_norm_g=v_gmlp_norm_g, gmlp_ws=v_gmlp_ws, gmlp_bs=v_gmlp_bs, w_out=v_w_out, final_g=v_final_g)
    depth, D, n_loc = w_in.shape
    e_loc = w_out.shape[1]
    cx, cy, cc = _my_place()
    me = 4 * cx + 2 * cy + cc

    transposed = lambda a: jnp.transpose(a, (0, 2, 1))
    w_in_t, m_w_in_t, v_w_in_t = transposed(w_in), transposed(m_w_in), transposed(v_w_in)
    w_in_w, w_out_w = w_in_t.astype(MXU_DTYPE), w_out.astype(MXU_DTYPE)
    c_loc = conv_a_w.shape[2]
    taps = (conv_a_w, conv_r_w)
    first, _ = _exchange_start([[(w_in_w[0], True), (_pack(taps), True)], [(w_out_w[0], True)]], "gather_start_first")
    full_in = lambda g: g.reshape(N_DEV * n_loc, D)
    full_out = lambda g: g.reshape(N_DEV * e_loc, D)

    g_in0, g_taps = _exchange_wait(first[0], x, "gather_wait_in_0")
    groups = [[(w_in_w[l], True), (w_out_w[l], True)] for l in range(1, depth)]
    gathers, rest_token = _exchange_start(groups, "gather_start_rest", after=g_taps)
    g_taps = g_taps.reshape(N_DEV, -1) + rest_token[0, 0]
    conv_full, off = [], 0
    for a in taps:
        part = g_taps[:, off:off + a.size].reshape((N_DEV,) + a.shape)
        conv_full.append(jnp.transpose(part, (1, 2, 0, 3)).reshape(a.shape[:2] + (N_DEV * c_loc,)))
        off += a.size
    conv_a_full, conv_r_full = conv_full
    later = {}

    def get_w_in(l, after):
        if l == 0:
            return full_in(g_in0)
        g_in, later[l] = _exchange_wait(gathers[l - 1], after, f"gather_wait_{l}")
        return full_in(g_in)

    def get_w_out(l, after):
        if l == 0:
            return full_out(_exchange_wait(first[1], after, "gather_wait_out_0")[0])
        return full_out(later[l])

    early, late, last_token, held = {}, {}, [None], {}

    def emit_early(l, dw_out, mixer_grads):
        group = [(dw_out.reshape(N_DEV, e_loc, D), False), (_pack(mixer_grads), True)]
        if l > 0:
            held[l] = (group, mixer_grads)
            return None
        handles, token = _exchange_start([group], f"early_start_{l}")
        early[l] = (handles[0], mixer_grads)
        return token

    def emit_late(l, dw_in, norm_grads):
        groups = [[(_pack(norm_grads), True)], [(dw_in, False)]] + ([held[l][0]] if l in held else [])
        handles, token = _exchange_start(groups, f"late_start_{l}")
        late[l] = (handles[0], handles[1], norm_grads)
        if l in held:
            early[l] = (handles[2], held[l][1])
        last_token[0] = token
        return token

    loss, grad_x = _local_step(
        x[0], loss_target[0], norm_g, get_w_in, get_w_out, emit_early, emit_late, conv_a_full, conv_r_full, conv_r_b,
        lru_wa, lru_ba, lru_wx, lru_bx, lru_lambda, gmlp_norm_g, gmlp_ws, gmlp_bs, final_g)
    loss = lax.psum(loss, ("x", "y", "c"))

    r_in, r_out, small_parts = {}, {}, []
    for l in reversed(range(depth)):
        r_out[l], r_mix = _exchange_wait(early[l][0], last_token[0], f"early_wait_{l}")
        (r_norm,) = _exchange_wait(late[l][0], last_token[0], f"late_wait_norm_{l}")
        small_parts += [r_mix, r_norm]
        if l > 0:
            (r_in[l],) = _exchange_wait(late[l][1], last_token[0], f"late_wait_{l}")
    big = {"w_out": _adamw_summed([r_out[l] for l in range(depth)], w_out, m_w_out, v_w_out, e_loc, "adamw_w_out")}

    sums = _sum_slots(small_parts, "sum_small_grads")
    by_layer = {}
    for i, l in enumerate(reversed(range(depth))):
        mix = _unpack(sums[2 * i], early[l][1])
        nrm = _unpack(sums[2 * i + 1], late[l][2])
        by_layer[l] = dict(zip(MIXER_NAMES, mix), norm_g=nrm[0])
        if l == depth - 1:
            g_final = nrm[1]
    g_small = {k: jnp.stack([by_layer[l][k] for l in range(depth)]) for k in ("norm_g",) + MIXER_NAMES}
    g_small["final_g"] = g_final
    for k in ("conv_a_w", "conv_r_w"):
        g_small[k] = lax.dynamic_slice_in_dim(g_small[k], me * c_loc, c_loc, axis=2)
    packs = [_pack([d[k] for k in SMALL_NAMES]) for d in (w, g_small, m, v)]
    res = _adamw_small(*packs, "adamw_small")
    like = [w[k] for k in SMALL_NAMES]
    d_s, m_s, v_s = (dict(zip(SMALL_NAMES, _unpack(r, like))) for r in res)

    (r_in[0],) = _exchange_wait(late[0][1], res[0], "late_wait_0")
    big["w_in"] = [transposed(a) for a in _adamw_summed(
        [r_in[l] for l in range(depth)], w_in_t, m_w_in_t, v_w_in_t, n_loc // 2, "adamw_w_in")]

    grad, delta, new_m, new_v = {}, {}, {}, {}
    for k in WEIGHT_NAMES:
        if k in big:
            grad[k], delta[k], new_m[k], new_v[k] = big[k]
        else:
            grad[k], delta[k], new_m[k], new_v[k] = g_small[k], d_s[k], m_s[k], v_s[k]
    return (loss, grad_x[None], *[grad[k] for k in WEIGHT_NAMES], *[delta[k] for k in WEIGHT_NAMES],
            *[new_m[k] for k in WEIGHT_NAMES], *[new_v[k] for k in WEIGHT_NAMES])
```

```python
import math

import jax
import jax.numpy as jnp
from jax import lax
from jax.experimental import pallas as pl
from jax.experimental.pallas import tpu as pltpu

F32 = jnp.float32
MXU_DTYPE = jnp.bfloat16
WIRE_DTYPE = jnp.bfloat16
MESH = pl.DeviceIdType.MESH

N_DEV = 8
GROUP_W = 256
N_HEADS = 4
HEAD_DIM = 64
N_ABC = 9
GMLP_CHUNK = 128
ATTN_BLOCK = 128
ATTN_FWD_BLOCKS_PER_STEP = 16
ATTN_BWD_BLOCKS_PER_STEP = 8
ATTN_DILATIONS = (1, 4, 16)
NORM_EPS = 1e-6
RG_C = 8.0
SUBLANES = 8
LANES = 128
VMEM_LIMIT = 56 * 1024 * 1024

ADAM_LR = 0.001
ADAM_B1 = 0.9
ADAM_B2 = 0.999
ADAM_EPS = 1e-08
ADAM_WD = 0.01
ADAM_STEP = 10

TM_MIX = 512
TM_MM = 512
TM_WGRAD = 1024


def _params(sem, vmem=VMEM_LIMIT):
    return pltpu.CompilerParams(dimension_semantics=sem, vmem_limit_bytes=vmem)


def _mm_tn(a, b):
    return lax.dot_general(a.astype(MXU_DTYPE), b.astype(MXU_DTYPE), (((0,), (0,)), ((), ())),
                           preferred_element_type=F32)


def _mm_nt(a, b):
    return lax.dot_general(a.astype(MXU_DTYPE), b.astype(MXU_DTYPE), (((1,), (1,)), ((), ())),
                           preferred_element_type=F32)


def _sigmoid(x):
    return 0.5 * jnp.tanh(0.5 * x) + 0.5


def _sigmoid_small_exact(x):
    return 1.0 / (1.0 + jnp.exp(-x))


def _silu_and_grad(x):
    s = _sigmoid(x)
    return x * s, s * (1.0 + x * (1.0 - s))


_GELU_K = math.sqrt(2.0 / math.pi)
_GELU_C = 0.044715


def _gelu_and_grad(x):
    x2 = x * x
    t = jnp.tanh(_GELU_K * (x + _GELU_C * x * x2))
    val = 0.5 * x * (1.0 + t)
    grad = 0.5 * (1.0 + t) + 0.5 * x * (1.0 - t * t) * (_GELU_K * (1.0 + 3.0 * _GELU_C * x2))
    return val, grad


def _gelu(x):
    return 0.5 * x * (1.0 + jnp.tanh(_GELU_K * (x + _GELU_C * x * x * x)))


def _expm1_nonpos(u):
    poly = 1.0 / math.factorial(9)
    for k in range(8, 0, -1):
        poly = poly * u + 1.0 / math.factorial(k)
    return jnp.where(u > -0.25, poly * u, jnp.exp(u) - 1.0)


def _softplus(x):
    return jnp.maximum(x, 0.0) + jnp.log(1.0 + jnp.exp(-jnp.abs(x)))


def _shift_down(t, halo, k):
    rolled = pltpu.roll(t, k, 0)
    hr = pltpu.roll(halo, k, 0)
    row = lax.broadcasted_iota(jnp.int32, halo.shape, 0)
    first = jnp.where(row < k, hr, rolled[0:SUBLANES])
    return jnp.concatenate([first, rolled[SUBLANES:]], axis=0)


def _shift_up(t, nxt, k):
    tm = t.shape[0]
    rolled = pltpu.roll(t, tm - k, 0)
    nr = pltpu.roll(nxt, SUBLANES - k, 0)
    row = lax.broadcasted_iota(jnp.int32, nxt.shape, 0)
    last = jnp.where(row >= SUBLANES - k, nr, rolled[tm - SUBLANES:tm])
    return jnp.concatenate([rolled[:tm - SUBLANES], last], axis=0)


def _scan_fwd(a, b):
    tm = a.shape[0]
    row = lax.broadcasted_iota(jnp.int32, a.shape, 0)
    s = 1
    while s < tm:
        a_s = pltpu.roll(a, s, 0)
        b_s = pltpu.roll(b, s, 0)
        m = row >= s
        b = jnp.where(m, a * b_s + b, b)
        a = jnp.where(m, a * a_s, a)
        s *= 2
    return a, b


def _scan_rev(a, g):
    tm = a.shape[0]
    row = lax.broadcasted_iota(jnp.int32, a.shape, 0)
    s = 1
    while s < tm:
        a_s = pltpu.roll(a, tm - s, 0)
        g_s = pltpu.roll(g, tm - s, 0)
        m = row < tm - s
        g = jnp.where(m, g + a * g_s, g)
        a = jnp.where(m, a * a_s, a)
        s *= 2
    return g


def _group_rows(scr_ref, row, n_groups):
    return jnp.concatenate([scr_ref[pl.ds(c, 1), pl.ds(row, n_groups, stride=SUBLANES), :][0]
                            for c in range(scr_ref.shape[0])], axis=1)


def _spread_rows(rows_ref, n_groups, w):
    return jnp.concatenate([jnp.broadcast_to(rows_ref[g:g + 1, :], (SUBLANES, w)) for g in range(n_groups)], axis=0)


def _scan_groups(a, b, reverse):
    tm, w = a.shape
    shape3 = (tm // SUBLANES, SUBLANES, w)
    a3, b3 = a.reshape(shape3), b.reshape(shape3)
    sub = lax.broadcasted_iota(jnp.int32, shape3, 1)
    s = 1
    while s < SUBLANES:
        shift = SUBLANES - s if reverse else s
        a_s = pltpu.roll(a3, shift, 1)
        b_s = pltpu.roll(b3, shift, 1)
        m = (sub < SUBLANES - s) if reverse else (sub >= s)
        b3 = jnp.where(m, a3 * b_s + b3, b3)
        a3 = jnp.where(m, a3 * a_s, a3)
        s *= 2
    return a3.reshape(tm, w), b3.reshape(tm, w)


def _scan_fwd_tile(a, b, h_in, sa_ref, sb_ref, sc_ref):
    tm, w = a.shape
    n_groups = tm // SUBLANES
    a_loc, b_loc = _scan_groups(a, b, False)
    _put(sa_ref, a_loc)
    _put(sb_ref, b_loc)
    a_end, b_end = _scan_fwd(_group_rows(sa_ref, SUBLANES - 1, n_groups), _group_rows(sb_ref, SUBLANES - 1, n_groups))
    h_end = b_end + a_end * h_in
    sc_ref[...] = _shift_down(h_end, jnp.broadcast_to(h_in, (SUBLANES, w)), 1)
    return b_loc + a_loc * _spread_rows(sc_ref, n_groups, w), h_end


def _scan_rev_tile(a, g, sa_ref, sb_ref, sc_ref):
    tm, w = a.shape
    n_groups = tm // SUBLANES
    a_loc, g_loc = _scan_groups(a, g, True)
    _put(sa_ref, a_loc)
    _put(sb_ref, g_loc)
    d_first = _scan_rev(_group_rows(sa_ref, 0, n_groups), _group_rows(sb_ref, 0, n_groups))
    sc_ref[...] = _shift_up(d_first, jnp.zeros((SUBLANES, w), F32), 1)
    return g_loc + a_loc * _spread_rows(sc_ref, n_groups, w)


def _lane_scratch(tm, w):
    return pltpu.VMEM((w // LANES, tm, LANES), F32)


def _put(scr_ref, val):
    for c in range(scr_ref.shape[0]):
        scr_ref[c] = val[:, c * LANES:(c + 1) * LANES].astype(F32)


def _get(scr_ref):
    return jnp.concatenate([scr_ref[c] for c in range(scr_ref.shape[0])], axis=1)


MAX_ROW_STRIDE = 4


def _strided_rows(c, start, n, stride):
    return (pl.ds(c, 1), pl.ds(start, n, stride=stride), slice(None))


def _deinterleave(src_ref, dst_ref, dil, tmp_ref=None):
    nc, tm, _ = src_ref.shape
    s1 = min(dil, MAX_ROW_STRIDE)
    s2 = dil // s1
    if s2 > 1:
        for r0 in range(s1):
            for c in range(nc):
                tmp_ref[c, r0 * (tm // s1):(r0 + 1) * (tm // s1), :] = src_ref[_strided_rows(c, r0, tm // s1, s1)][0]
    for r in range(dil):
        r1, r0 = divmod(r, s1)
        for c in range(nc):
            if dil == 1:
                piece = src_ref[c]
            elif s2 == 1:
                piece = src_ref[_strided_rows(c, r, tm // dil, dil)][0]
            else:
                piece = tmp_ref[_strided_rows(c, r0 * (tm // s1) + r1, tm // dil, s2)][0]
            dst_ref[r, :, c * LANES:(c + 1) * LANES] = piece.astype(dst_ref.dtype)


def _interleave(src_ref, dst_ref, dil, tmp_ref=None):
    nc, tm, _ = dst_ref.shape
    s1 = min(dil, MAX_ROW_STRIDE)
    s2 = dil // s1
    for r in range(dil):
        r1, r0 = divmod(r, s1)
        for c in range(nc):
            piece = src_ref[r, :, c * LANES:(c + 1) * LANES].astype(F32)[None]
            if s2 == 1:
                dst_ref[_strided_rows(c, r, tm // dil, dil)] = piece
            else:
                tmp_ref[_strided_rows(c, r0 * (tm // s1) + r1, tm // dil, s2)] = piece
    if s2 > 1:
        for r0 in range(s1):
            for c in range(nc):
                dst_ref[_strided_rows(c, r0, tm // s1, s1)] = (
                    tmp_ref[c, r0 * (tm // s1):(r0 + 1) * (tm // s1), :][None])


def _dilated_spec(tm, w, dil, index=lambda i: i):
    return pl.BlockSpec((dil, tm // dil, w), lambda i: (0, index(i), 0))


def _dilated_shape(S, w, dil, dtype):
    return jax.ShapeDtypeStruct((dil, S // dil, w), dtype)


def _head_masks(shape):
    lane = lax.broadcasted_iota(jnp.int32, shape, 1)
    return [(lane >= h * HEAD_DIM) & (lane < (h + 1) * HEAD_DIM) for h in range(N_HEADS)]


def _colsum(v):
    return jnp.sum(v, axis=0, keepdims=True)


def _conv_a(z_of, halo_of, w_ref):
    p = z_of(2) * z_of(0)
    p_h = halo_of(2) * halo_of(0)
    cv = w_ref[2:3, :] * p + w_ref[1:2, :] * _shift_down(p, p_h, 1) + w_ref[0:1, :] * _shift_down(p, p_h, 2)
    return p, p_h, cv


def _lru_gates(z_of, halo_of, wr_ref, vec_ref, wa_ref, wx_ref, saved=None):
    rx = z_of(4)
    rx_h = halo_of(4)
    sh = [rx, _shift_down(rx, rx_h, 1), _shift_down(rx, rx_h, 2), _shift_down(rx, rx_h, 3)]
    xc = (wr_ref[3:4, :] * sh[0] + wr_ref[2:3, :] * sh[1] + wr_ref[1:2, :] * sh[2]
          + wr_ref[0:1, :] * sh[3] + vec_ref[0:1, :])
    ga = _sigmoid_small_exact(jnp.dot(xc.astype(MXU_DTYPE), wa_ref[...], preferred_element_type=F32) + vec_ref[1:2, :])
    gi = _sigmoid(jnp.dot(xc.astype(MXU_DTYPE), wx_ref[...], preferred_element_type=F32) + vec_ref[2:3, :])
    sp = _softplus(-vec_ref[3:4, :])
    if saved is not None:
        return (xc, sh, ga, gi) + tuple(saved) + (sp,)
    log_a = (-RG_C * ga) * sp
    a = jnp.exp(log_a)
    mult = jnp.sqrt(-_expm1_nonpos(2.0 * log_a))
    return xc, sh, ga, gi, a, mult, sp


def _gmlp_fwd(z_of, vec_ref, ws_ref, bs_ref, tm):
    u = _gelu(z_of(6))
    gv = _gelu(z_of(7))
    rr = lax.rsqrt(jnp.mean(gv * gv, axis=-1, keepdims=True) + NORM_EPS)
    vn = (gv * rr) * vec_ref[4:5, :]
    masks = _head_masks((GMLP_CHUNK, GROUP_W))
    parts = []
    for c in range(tm // GMLP_CHUNK):
        vc = vn[c * GMLP_CHUNK:(c + 1) * GMLP_CHUNK].astype(MXU_DTYPE)
        acc = bs_ref[...]
        for h in range(N_HEADS):
            acc = acc + jnp.where(masks[h], jnp.dot(ws_ref[h], vc, preferred_element_type=F32), 0.0)
        parts.append(acc)
    return u, gv, rr, vn, jnp.concatenate(parts, axis=0)


def _mix_specs():
    const2 = lambda shape: pl.BlockSpec(shape, lambda i: (0, 0))
    return [const2((SUBLANES, GROUP_W)), const2((SUBLANES, GROUP_W)), const2((SUBLANES, GROUP_W)),
            const2((GROUP_W, GROUP_W)), const2((GROUP_W, GROUP_W)),
            pl.BlockSpec((N_HEADS, GMLP_CHUNK, GMLP_CHUNK), lambda i: (0, 0, 0)),
            const2((GMLP_CHUNK, GROUP_W))]


def _inproj_mix_fwd(x, g, w_t, mp, name):
    S, D = x.shape
    N = w_t.shape[0]
    tm = TM_MIX
    hb = tm // SUBLANES
    n_abc = N_ABC * GROUP_W
    n_qkv = 3 * GROUP_W

    def body(x_ref, g_ref, w_ref, wA_ref, wR_ref, vec_ref, wa_ref, wx_ref, ws_ref, bs_ref,
             z_ref, zg_ref, q1_ref, q4_ref, q16_ref, y_ref, h_ref, a_ref, mult_ref,
             qkv_ref, halo_ref, carry_ref, sa_ref, sb_ref, sc_ref, tmp_ref):
        @pl.when(pl.program_id(0) == 0)
        def _():
            halo_ref[...] = jnp.zeros_like(halo_ref)
            carry_ref[...] = jnp.zeros_like(carry_ref)

        xv = x_ref[...]
        r = lax.rsqrt(jnp.mean(xv * xv, axis=-1, keepdims=True) + NORM_EPS)
        hn = ((xv * r) * g_ref[...]).astype(MXU_DTYPE)
        z_ref[...] = _mm_nt(hn, w_ref[0:n_abc, :])
        _put(qkv_ref, _mm_nt(hn, w_ref[n_abc:n_abc + n_qkv, :]))
        zg_ref[...] = _mm_nt(hn, w_ref[n_abc + n_qkv:, :])
        for dil, ref in zip(ATTN_DILATIONS, (q1_ref, q4_ref, q16_ref)):
            _deinterleave(qkv_ref, ref, dil, tmp_ref)

        z_of = lambda c: z_ref[:, c * GROUP_W:(c + 1) * GROUP_W]
        halo_of = lambda c: halo_ref[:, c * GROUP_W:(c + 1) * GROUP_W]

        _, _, cv = _conv_a(z_of, halo_of, wA_ref)
        y_ref[:, 0:GROUP_W] = (z_of(1) * cv * _silu_and_grad(z_of(3))[0]).astype(y_ref.dtype)

        xc, _, _, gi, a, mult, _ = _lru_gates(z_of, halo_of, wR_ref, vec_ref, wa_ref, wx_ref)
        a_ref[...] = a
        mult_ref[...] = mult
        b = mult * (gi * xc)
        h, h_end = _scan_fwd_tile(a, b, carry_ref[SUBLANES - 1:SUBLANES, :], sa_ref, sb_ref, sc_ref)
        h_ref[...] = h
        carry_ref[...] = h_end[hb - SUBLANES:hb]
        y_ref[:, GROUP_W:2 * GROUP_W] = (h * _silu_and_grad(z_of(5))[0]).astype(y_ref.dtype)

        u, _, _, _, sp = _gmlp_fwd(z_of, vec_ref, ws_ref, bs_ref, tm)
        y_ref[:, 2 * GROUP_W:3 * GROUP_W] = (u * sp * _silu_and_grad(z_of(8))[0]).astype(y_ref.dtype)
        halo_ref[...] = z_ref[tm - SUBLANES:tm, :]

    row = lambda wd: pl.BlockSpec((tm, wd), lambda i: (i, 0))
    return pl.pallas_call(
        body, name=name, grid=(S // tm,),
        in_specs=[row(D), pl.BlockSpec((1, D), lambda i: (0, 0)),
                  pl.BlockSpec((N, D), lambda i: (0, 0), pipeline_mode=pl.Buffered(1))] + _mix_specs(),
        out_specs=[row(n_abc), row(GROUP_W)] + [_dilated_spec(tm, n_qkv, dil) for dil in ATTN_DILATIONS]
                  + [row(3 * GROUP_W)] + [row(GROUP_W)] * 3,
        out_shape=[jax.ShapeDtypeStruct((S, n_abc), F32), jax.ShapeDtypeStruct((S, GROUP_W), F32)]
                  + [_dilated_shape(S, n_qkv, dil, MXU_DTYPE) for dil in ATTN_DILATIONS]
                  + [jax.ShapeDtypeStruct((S, 3 * GROUP_W), MXU_DTYPE)] + [jax.ShapeDtypeStruct((S, GROUP_W), F32)] * 3,
        scratch_shapes=[_lane_scratch(tm, n_qkv), pltpu.VMEM((SUBLANES, n_abc), F32),
                        pltpu.VMEM((SUBLANES, GROUP_W), F32), _lane_scratch(tm, GROUP_W), _lane_scratch(tm, GROUP_W),
                        pltpu.VMEM((hb, GROUP_W), F32), _lane_scratch(tm, n_qkv)],
        compiler_params=_params(("arbitrary",)),
    )(x, g, w_t, mp["wA"], mp["wR"], mp["vec"], mp["wa"], mp["wx"], mp["ws"], mp["bs"])


_NEG = -1e30


def _slope(h):
    return 2.0 ** (-8.0 * (h + 1) / N_HEADS)


def _attn_bias(dil, offsets, n_keys):
    shape = (ATTN_BLOCK, n_keys)
    qi = lax.broadcasted_iota(jnp.int32, shape, 0)
    ki = lax.broadcasted_iota(jnp.int32, shape, 1)
    blocks = []
    for f in offsets:
        delta = qi + f - ki
        valid = (delta >= 0) & (delta <= ATTN_BLOCK)
        dist = (delta * dil).astype(F32)
        for h in range(N_HEADS):
            blocks.append(jnp.where(valid, -_slope(h) * dist, _NEG))
    return jnp.concatenate(blocks, axis=0)


def _stack_heads(t, masks):
    return jnp.concatenate([jnp.where(m, t, jnp.zeros_like(t)) for m in masks], axis=0)


def _unstack_heads(t4, masks, base=0):
    out = t4[base * ATTN_BLOCK:(base + 1) * ATTN_BLOCK]
    for h in range(1, N_HEADS):
        out = jnp.where(masks[h], t4[(base + h) * ATTN_BLOCK:(base + h + 1) * ATTN_BLOCK], out)
    return out


def _group_starts(n, per_step, group):
    if per_step % group == 0:
        return (lambda j: j % group == 0), True
    steps = group // per_step
    return (lambda j: (n % steps == 0) if j == 0 else False), (n + 1) % steps == 0


def _attn_fwd(qkv, dil, name):
    S = qkv.shape[0] * qkv.shape[1]
    qkv = qkv.reshape(S, qkv.shape[2])
    nb = S // ATTN_BLOCK
    group = nb // dil
    scale = 1.0 / math.sqrt(HEAD_DIM)
    B = ATTN_BLOCK
    per_step = ATTN_FWD_BLOCKS_PER_STEP

    def body(q_ref, kc_ref, kp_ref, vc_ref, vp_ref, o_ref, l_ref, bias_ref):
        n = pl.program_id(0)

        @pl.when(n == 0)
        def _():
            bias_ref[...] = _attn_bias(dil, (B,), 2 * B)

        masks = _head_masks((B, GROUP_W))
        starts, _ = _group_starts(n, per_step, group)
        for j in range(per_step):
            own = slice(j * B, (j + 1) * B)
            before = slice((j - 1) * B, j * B)
            qs = _stack_heads(q_ref[own], masks)
            keys = jnp.concatenate([kp_ref[...] if j == 0 else kc_ref[before], kc_ref[own]], axis=0)
            vals = jnp.concatenate([vp_ref[...] if j == 0 else vc_ref[before], vc_ref[own]], axis=0)
            s = _mm_nt(qs, keys) * scale + bias_ref[...]
            if starts(j) is not False:
                key_col = lax.broadcasted_iota(jnp.int32, s.shape, 1)
                s = jnp.where(starts(j) & (key_col < B), _NEG, s)
            m = jnp.max(s, axis=-1, keepdims=True)
            p = jnp.exp(s - m)
            l = jnp.sum(p, axis=-1, keepdims=True)
            o4 = jnp.dot(p.astype(MXU_DTYPE), vals, preferred_element_type=F32)
            o_ref[own] = (_unstack_heads(o4, masks)
                          / _unstack_heads(jnp.broadcast_to(l, o4.shape), masks)).astype(o_ref.dtype)
            l_ref[own] = _unstack_heads(jnp.broadcast_to(m + jnp.log(l), o4.shape), masks)

    blk = (per_step * B, GROUP_W)
    cur = lambda c: pl.BlockSpec(blk, lambda n: (n, c))
    prev = lambda c: pl.BlockSpec((B, GROUP_W), lambda n: (jnp.maximum(n * per_step - 1, 0), c))
    out = pl.BlockSpec(blk, lambda n: (n, 0))
    o, l = pl.pallas_call(
        body, name=name, grid=(nb // per_step,),
        in_specs=[cur(0), cur(1), prev(1), cur(2), prev(2)],
        out_specs=[out, out],
        out_shape=[jax.ShapeDtypeStruct((S, GROUP_W), MXU_DTYPE), jax.ShapeDtypeStruct((S, GROUP_W), F32)],
        scratch_shapes=[pltpu.VMEM((N_HEADS * ATTN_BLOCK, 2 * ATTN_BLOCK), F32)],
        compiler_params=_params(("arbitrary",)),
    )(qkv, qkv, qkv, qkv, qkv)
    return o.reshape(dil, S // dil, GROUP_W), l.reshape(dil, S // dil, GROUP_W)


def _outproj(x, z_g, y_abc, attn, w_out, name):
    S, D = x.shape
    tm = TM_MM
    n_abc = 3 * GROUP_W

    def body(x_ref, g_ref, yabc_ref, o1, l1, o2, l2, o3, l3, w_ref,
             xn_ref, y_ref, o_ref, lse1_ref, lse4_ref, lse16_ref, so2, sl2, so3, sl3, slse, tmp_ref):
        for src, dst, dil in ((o2, so2, ATTN_DILATIONS[1]), (l2, sl2, ATTN_DILATIONS[1]),
                              (o3, so3, ATTN_DILATIONS[2]), (l3, sl3, ATTN_DILATIONS[2])):
            _interleave(src, dst, dil, tmp_ref)
        la, lb, lc = l1[0], _get(sl2), _get(sl3)
        mx = jnp.maximum(jnp.maximum(la, lb), lc)
        ea, eb, ec = jnp.exp(la - mx), jnp.exp(lb - mx), jnp.exp(lc - mx)
        den = ea + eb + ec
        o = (ea * o1[0].astype(F32) + eb * _get(so2) + ec * _get(so3)) / den
        o_ref[...] = o
        _put(slse, mx + jnp.log(den))
        for dil, ref in zip(ATTN_DILATIONS, (lse1_ref, lse4_ref, lse16_ref)):
            _deinterleave(slse, ref, dil, tmp_ref)
        y_d = o * _silu_and_grad(g_ref[...])[0]
        y_ref[:, 0:n_abc] = yabc_ref[...].astype(MXU_DTYPE)
        y_ref[:, n_abc:] = y_d.astype(MXU_DTYPE)
        xn_ref[...] = x_ref[...] + jnp.dot(y_ref[...], w_ref[...], preferred_element_type=F32)

    row = lambda w: pl.BlockSpec((tm, w), lambda i: (i, 0))
    dil_specs = [_dilated_spec(tm, GROUP_W, dil) for dil in ATTN_DILATIONS]
    (o1, l1), (o2, l2), (o3, l3) = attn
    return pl.pallas_call(
        body, name=name, grid=(S // tm,),
        in_specs=[row(D), row(GROUP_W), row(n_abc)] + [sp for sp in dil_specs for _ in range(2)]
                 + [pl.BlockSpec(w_out.shape, lambda i: (0, 0))],
        out_specs=[row(D), row(4 * GROUP_W), row(GROUP_W)] + dil_specs,
        out_shape=[jax.ShapeDtypeStruct((S, D), F32), jax.ShapeDtypeStruct((S, 4 * GROUP_W), MXU_DTYPE),
                   jax.ShapeDtypeStruct((S, GROUP_W), F32)]
                  + [_dilated_shape(S, GROUP_W, dil, F32) for dil in ATTN_DILATIONS],
        scratch_shapes=[_lane_scratch(tm, GROUP_W)] * 6,
        compiler_params=_params(("parallel",)),
    )(x, z_g, y_abc, o1, l1, o2, l2, o3, l3, w_out)


def _loss_head(x, g, target, name):
    S, D = x.shape
    tm = TM_MM

    def body(x_ref, g_ref, t_ref, dx_ref, loss_ref, dg_ref):
        i = pl.program_id(0)

        @pl.when(i == 0)
        def _():
            loss_ref[...] = jnp.zeros_like(loss_ref)
            dg_ref[...] = jnp.zeros_like(dg_ref)

        xv = x_ref[...]
        r = lax.rsqrt(jnp.mean(xv * xv, axis=-1, keepdims=True) + NORM_EPS)
        xn = xv * r
        err = xn * g_ref[...] - t_ref[...]
        per_tok = jnp.mean(err * err, axis=-1, keepdims=True)
        loss_ref[...] += 0.5 * jnp.sum(per_tok, axis=0, keepdims=True)
        dout = err * (1.0 / D)
        dg_ref[...] += _colsum(dout * xn)
        dxn = dout * g_ref[...]
        dx_ref[...] = r * (dxn - xn * jnp.mean(dxn * xn, axis=-1, keepdims=True))

    row = pl.BlockSpec((tm, D), lambda i: (i, 0))
    return pl.pallas_call(
        body, name=name, grid=(S // tm,),
        in_specs=[row, pl.BlockSpec((1, D), lambda i: (0, 0)), row],
        out_specs=[row, pl.BlockSpec((1, LANES), lambda i: (0, 0)), pl.BlockSpec((1, D), lambda i: (0, 0))],
        out_shape=[jax.ShapeDtypeStruct((S, D), F32), jax.ShapeDtypeStruct((1, LANES), F32),
                   jax.ShapeDtypeStruct((1, D), F32)],
        compiler_params=_params(("arbitrary",)),
    )(x, g, target)


def _outproj_mix_bwd(dx, y, w_out, z, z_g, hs, lru_a, lru_mult, o, mp, name):
    S, D = dx.shape
    E = y.shape[1]
    tm = TM_MIX
    hb = tm // SUBLANES
    nT = S // tm
    last_blk = S // SUBLANES - 1
    wcols = N_ABC * GROUP_W

    def body(dx_ref, y_ref, w_ref, z_ref, zh_ref, zn_ref, zg_ref, h_ref, hh_ref, a_ref, mult_ref, o_ref,
             wA_ref, wR_ref, vec_ref, wa_ref, wx_ref, ws_ref, bs_ref,
             dw_ref, dz_ref, dzg_ref, do1_ref, do4_ref, do16_ref, dl1_ref, dl4_ref, dl16_ref,
             dwA_ref, dwR_ref, dvec_ref, dwa_ref, dwx_ref, dws_ref, dbs_ref,
             hcarry_ref, xcarry_ref, bsacc_ref, do_ref, dl_ref, sa_ref, sb_ref, sc_ref, dy_ref, dyn_ref, acc_ref,
             tmp_ref):
        i = pl.program_id(0)
        ti = nT - 1 - i

        @pl.when(i == 0)
        def _():
            acc_ref[...] = jnp.zeros_like(acc_ref)
            dyn_ref[...] = jnp.zeros_like(dyn_ref)
            hcarry_ref[...] = jnp.zeros_like(hcarry_ref)
            xcarry_ref[...] = jnp.zeros_like(xcarry_ref)
            bsacc_ref[...] = jnp.zeros_like(bsacc_ref)
            dwA_ref[...] = jnp.zeros_like(dwA_ref)
            dwR_ref[...] = jnp.zeros_like(dwR_ref)
            dvec_ref[...] = jnp.zeros_like(dvec_ref)
            dwa_ref[...] = jnp.zeros_like(dwa_ref)
            dwx_ref[...] = jnp.zeros_like(dwx_ref)
            dws_ref[...] = jnp.zeros_like(dws_ref)
            dbs_ref[...] = jnp.zeros_like(dbs_ref)

        dxb = dx_ref[...].astype(MXU_DTYPE)
        dy_ref[...] = _mm_nt(dxb, w_ref[...])
        acc_ref[...] += _mm_tn(y_ref[...], dxb)

        @pl.when(i == nT - 1)
        def _():
            dw_ref[...] = acc_ref[...].astype(dw_ref.dtype)

        has_prev = ti > 0
        has_next = i > 0
        col = lambda c: slice(c * GROUP_W, (c + 1) * GROUP_W)
        z_of = lambda c: z_ref[:, col(c)]
        halo_of = lambda c: jnp.where(has_prev, zh_ref[:, col(c)], 0.0)
        next_of = lambda c: zn_ref[:, col(c)]

        p, p_h, cv = _conv_a(z_of, halo_of, wA_ref)
        sg, dsg = _silu_and_grad(z_of(3))
        a_b = z_of(1)
        dya = dy_ref[:, col(0)]
        dcv = dya * a_b * sg
        dcv_n = jnp.where(has_next, dyn_ref[...] * next_of(1) * _silu_and_grad(next_of(3))[0], 0.0)
        dp = (wA_ref[2:3, :] * dcv + wA_ref[1:2, :] * _shift_up(dcv, dcv_n, 1)
              + wA_ref[0:1, :] * _shift_up(dcv, dcv_n, 2))
        dwA_ref[2:3, :] += _colsum(dcv * p)
        dwA_ref[1:2, :] += _colsum(dcv * _shift_down(p, p_h, 1))
        dwA_ref[0:1, :] += _colsum(dcv * _shift_down(p, p_h, 2))
        def put_dz(c, val):
            dz_ref[:, col(c)] = val.astype(dz_ref.dtype)

        put_dz(0, dp * z_of(2))
        put_dz(1, dya * cv * sg)
        put_dz(2, dp * z_of(0))
        put_dz(3, dya * a_b * cv * dsg)

        xc, sh, ga, gi, a, mult, sp = _lru_gates(z_of, halo_of, wR_ref, vec_ref, wa_ref, wx_ref,
                                                 saved=(a_ref[...], mult_ref[...]))
        h = h_ref[...]
        h_prev = _shift_down(h, jnp.where(has_prev, hh_ref[...], 0.0), 1)
        sgr, dsgr = _silu_and_grad(z_of(5))
        dyb = dy_ref[:, col(1)]
        put_dz(5, dyb * h * dsgr)
        row = lax.broadcasted_iota(jnp.int32, (tm, GROUP_W), 0)
        g_in = dyb * sgr + jnp.where(row == tm - 1, hcarry_ref[0:1, :], 0.0)
        a_up = _shift_up(a, jnp.zeros((SUBLANES, GROUP_W), F32), 1)
        dH = _scan_rev_tile(a_up, g_in, sa_ref, sb_ref, sc_ref)
        hcarry_ref[...] = (a * dH)[0:SUBLANES]
        da = dH * h_prev
        gx = gi * xc
        dmult = dH * gx
        dgi = dH * mult * xc
        dxc = dH * mult * gi
        dlog_a = da * a - dmult * (a * a) / mult
        dga = dlog_a * (-RG_C * sp)
        dlam_row = _colsum(dlog_a * (-RG_C * ga)) * (-_sigmoid(-vec_ref[3:4, :]))
        dpre_a = dga * ga * (1.0 - ga)
        dpre_i = dgi * gi * (1.0 - gi)
        dwa_ref[...] += _mm_tn(xc, dpre_a)
        dwx_ref[...] += _mm_tn(xc, dpre_i)
        dxc = dxc + _mm_nt(dpre_a, wa_ref[...]) + _mm_nt(dpre_i, wx_ref[...])
        dvec_ref[0:1, :] += _colsum(dxc)
        dvec_ref[1:2, :] += _colsum(dpre_a)
        dvec_ref[2:3, :] += _colsum(dpre_i)
        dvec_ref[3:4, :] += dlam_row
        for k in range(4):
            dwR_ref[k:k + 1, :] += _colsum(dxc * sh[3 - k])
        dxc_n = xcarry_ref[...]
        put_dz(4, wR_ref[3:4, :] * dxc + wR_ref[2:3, :] * _shift_up(dxc, dxc_n, 1)
               + wR_ref[1:2, :] * _shift_up(dxc, dxc_n, 2) + wR_ref[0:1, :] * _shift_up(dxc, dxc_n, 3))
        xcarry_ref[...] = dxc[0:SUBLANES]

        c_u, c_v = z_of(6), z_of(7)
        u, du_dx = _gelu_and_grad(c_u)
        gv, dgv_dx = _gelu_and_grad(c_v)
        rr = lax.rsqrt(jnp.mean(gv * gv, axis=-1, keepdims=True) + NORM_EPS)
        xhat = gv * rr
        g_c = vec_ref[4:5, :]
        vn = xhat * g_c
        masks = _head_masks((GMLP_CHUNK, GROUP_W))
        tri_r = lax.broadcasted_iota(jnp.int32, (GMLP_CHUNK, GMLP_CHUNK), 0)
        tri_c = lax.broadcasted_iota(jnp.int32, (GMLP_CHUNK, GMLP_CHUNK), 1)
        tril = tri_r >= tri_c
        sgc, dsgc = _silu_and_grad(z_of(8))
        dyc = dy_ref[:, col(2)]
        dsp_full = dyc * u * sgc
        sp_parts, dvn_parts = [], []
        for c in range(tm // GMLP_CHUNK):
            rs = slice(c * GMLP_CHUNK, (c + 1) * GMLP_CHUNK)
            vc = vn[rs].astype(MXU_DTYPE)
            dsp_c = dsp_full[rs]
            bsacc_ref[...] += dsp_c
            acc = bs_ref[...]
            dvn_c = jnp.zeros((GMLP_CHUNK, GROUP_W), F32)
            for h in range(N_HEADS):
                w_h = ws_ref[h]
                acc = acc + jnp.where(masks[h], jnp.dot(w_h, vc, preferred_element_type=F32), 0.0)
                dsp_h = jnp.where(masks[h], dsp_c, 0.0).astype(MXU_DTYPE)
                dvn_c = dvn_c + _mm_tn(w_h, dsp_h)
                dws_ref[h] += jnp.where(tril, _mm_nt(dsp_h, vc), 0.0)
            sp_parts.append(acc)
            dvn_parts.append(dvn_c)
        spv = jnp.concatenate(sp_parts, axis=0)
        dvn = jnp.concatenate(dvn_parts, axis=0)
        put_dz(6, dyc * spv * sgc * du_dx)
        put_dz(8, dyc * u * spv * dsgc)
        dvec_ref[4:5, :] += _colsum(dvn * xhat)
        dgvn = dvn * g_c
        dgv = rr * (dgvn - xhat * jnp.mean(dgvn * xhat, axis=-1, keepdims=True))
        put_dz(7, dgv * dgv_dx)

        sgd, dsgd = _silu_and_grad(zg_ref[...])
        dyd = dy_ref[:, col(3)]
        ov = o_ref[...]
        do = dyd * sgd
        _put(do_ref, do)
        dzg_ref[...] = (dyd * ov * dsgd).astype(dzg_ref.dtype)
        prod = do * ov
        tmasks = _head_masks((tm, GROUP_W))
        dl = jnp.zeros((tm, GROUP_W), F32)
        for h in range(N_HEADS):
            dl = jnp.where(tmasks[h], jnp.sum(jnp.where(tmasks[h], prod, 0.0), axis=-1, keepdims=True), dl)
        _put(dl_ref, dl)
        for dil, d_out, l_out in zip(ATTN_DILATIONS, (do1_ref, do4_ref, do16_ref), (dl1_ref, dl4_ref, dl16_ref)):
            _deinterleave(do_ref, d_out, dil, tmp_ref)
            _deinterleave(dl_ref, l_out, dil, tmp_ref)

        @pl.when(i == nT - 1)
        def _():
            acc = bsacc_ref[...]
            lane = lax.broadcasted_iota(jnp.int32, (GMLP_CHUNK, LANES), 1)
            out = jnp.zeros((GMLP_CHUNK, LANES), F32)
            for h in range(N_HEADS):
                out = jnp.where(lane == h, jnp.sum(jnp.where(masks[h], acc, 0.0), axis=-1, keepdims=True), out)
            dbs_ref[...] = out

        dyn_ref[...] = dy_ref[0:SUBLANES, 0:GROUP_W]

    rev = lambda w: pl.BlockSpec((tm, w), lambda i: (nT - 1 - i, 0))
    prev8 = lambda w: pl.BlockSpec((SUBLANES, w), lambda i: (jnp.maximum((nT - 1 - i) * hb - 1, 0), 0))
    next8 = lambda w: pl.BlockSpec((SUBLANES, w), lambda i: (jnp.minimum((nT - i) * hb, last_blk), 0))
    const2 = lambda shape: pl.BlockSpec(shape, lambda i: (0, 0))
    dil_specs = [_dilated_spec(tm, GROUP_W, dil, lambda i: nT - 1 - i) for dil in ATTN_DILATIONS]
    dil_shapes = [_dilated_shape(S, GROUP_W, dil, F32) for dil in ATTN_DILATIONS]
    small = (SUBLANES, GROUP_W)
    sq = (GROUP_W, GROUP_W)
    ws_shape = (N_HEADS, GMLP_CHUNK, GMLP_CHUNK)
    return pl.pallas_call(
        body, name=name, grid=(nT,),
        in_specs=[rev(D), rev(E), pl.BlockSpec((E, D), lambda i: (0, 0), pipeline_mode=pl.Buffered(1)),
                  rev(wcols), prev8(wcols), next8(wcols), rev(GROUP_W), rev(GROUP_W), prev8(GROUP_W),
                  rev(GROUP_W), rev(GROUP_W), rev(GROUP_W)]
                 + _mix_specs(),
        out_specs=[const2((E, D)), rev(wcols), rev(GROUP_W)] + dil_specs + dil_specs
                  + [const2(small), const2(small), const2(small), const2(sq), const2(sq),
                     pl.BlockSpec(ws_shape, lambda i: (0, 0, 0)), const2((GMLP_CHUNK, LANES))],
        out_shape=[jax.ShapeDtypeStruct((E, D), WIRE_DTYPE),
                   jax.ShapeDtypeStruct((S, wcols), MXU_DTYPE), jax.ShapeDtypeStruct((S, GROUP_W), MXU_DTYPE)]
                  + [_dilated_shape(S, GROUP_W, dil, MXU_DTYPE) for dil in ATTN_DILATIONS] + dil_shapes
                  + [jax.ShapeDtypeStruct(small, F32)] * 3 + [jax.ShapeDtypeStruct(sq, F32)] * 2
                  + [jax.ShapeDtypeStruct(ws_shape, F32), jax.ShapeDtypeStruct((GMLP_CHUNK, LANES), F32)],
        scratch_shapes=[pltpu.VMEM(small, F32), pltpu.VMEM(small, F32), pltpu.VMEM((GMLP_CHUNK, GROUP_W), F32),
                        _lane_scratch(tm, GROUP_W), _lane_scratch(tm, GROUP_W),
                        _lane_scratch(tm, GROUP_W), _lane_scratch(tm, GROUP_W), pltpu.VMEM((hb, GROUP_W), F32),
                        pltpu.VMEM((tm, E), F32), pltpu.VMEM(small, F32), pltpu.VMEM((E, D), F32),
                        _lane_scratch(tm, GROUP_W)],
        compiler_params=_params(("arbitrary",)),
    )(dx, y, w_out, z, z, z, z_g, hs, hs, lru_a, lru_mult, o, mp["wA"], mp["wR"], mp["vec"], mp["wa"], mp["wx"], mp["ws"], mp["bs"])


def _attn_bwd(qkv, do, lse, delta, dil, name):
    S = qkv.shape[0] * qkv.shape[1]
    flat = lambda t: t.reshape(S, t.shape[2])
    qkv, do, lse, delta = flat(qkv), flat(do), flat(lse), flat(delta)
    nb = S // ATTN_BLOCK
    group = nb // dil
    scale = 1.0 / math.sqrt(HEAD_DIM)
    B = ATTN_BLOCK
    per_step = ATTN_BWD_BLOCKS_PER_STEP
    n_steps = nb // per_step

    def body(qc_ref, qn_ref, kc_ref, kp_ref, vc_ref, vp_ref, doc_ref, don_ref, lc_ref, ln_ref, dc_ref, dn_ref,
             dq_ref, dk_ref, dv_ref, bias_ref, bias_next_ref):
        n = pl.program_id(0)
        starts, next_starts = _group_starts(n, per_step, group)

        @pl.when(n == 0)
        def _():
            bias_ref[...] = _attn_bias(dil, (B,), 2 * B)
            bias_next_ref[...] = _attn_bias(dil, (B,), B)

        masks = _head_masks((B, GROUP_W))

        def per_row(tile):
            return jnp.concatenate([jnp.max(jnp.where(masks[h], tile, _NEG), axis=-1, keepdims=True)
                                    for h in range(N_HEADS)], axis=0)

        def grads(q, dov, lse_tile, dl_tile, keys, vals, bias, dead):
            qs = _stack_heads(q, masks)
            dos = _stack_heads(dov.astype(MXU_DTYPE), masks)
            s = _mm_nt(qs, keys) * scale + bias
            if dead is not None:
                s = jnp.where(dead(s.shape), _NEG, s)
            p = jnp.exp(s - per_row(lse_tile))
            ds = (p * (_mm_nt(dos, vals) - per_row(dl_tile)) * scale).astype(MXU_DTYPE)
            return ds, _mm_tn(ds, qs), _mm_tn(p.astype(MXU_DTYPE), dos)

        for j in range(per_step):
            own = slice(j * B, (j + 1) * B)
            before = slice((j - 1) * B, j * B)
            keys = jnp.concatenate([kp_ref[...] if j == 0 else kc_ref[before], kc_ref[own]], axis=0)
            vals = jnp.concatenate([vp_ref[...] if j == 0 else vc_ref[before], vc_ref[own]], axis=0)
            dead = None
            if starts(j) is not False:
                dead = lambda shape, j=j: starts(j) & (lax.broadcasted_iota(jnp.int32, shape, 1) < B)
            ds, dk2, dv2 = grads(qc_ref[own], doc_ref[own], lc_ref[own], dc_ref[own], keys, vals, bias_ref[...], dead)
            dq_ref[own] = _unstack_heads(jnp.dot(ds, keys, preferred_element_type=F32), masks).astype(dq_ref.dtype)
            if j > 0:
                dk_ref[before] = (dk_own + dk2[:B]).astype(dk_ref.dtype)
                dv_ref[before] = (dv_own + dv2[:B]).astype(dv_ref.dtype)
            dk_own, dv_own = dk2[B:], dv2[B:]
        last = slice((per_step - 1) * B, per_step * B)
        if next_starts is not True:
            _, dk1, dv1 = grads(qn_ref[...], don_ref[...], ln_ref[...], dn_ref[...], kc_ref[last], vc_ref[last],
                                bias_next_ref[...], lambda shape: next_starts)
            dk_own, dv_own = dk_own + dk1, dv_own + dv1
        dk_ref[last] = dk_own.astype(dk_ref.dtype)
        dv_ref[last] = dv_own.astype(dv_ref.dtype)

    blk = (per_step * B, GROUP_W)
    one = (B, GROUP_W)
    nxt_idx = lambda n: jnp.minimum((n + 1) * per_step, nb - 1)
    prv_idx = lambda n: jnp.maximum(n * per_step - 1, 0)
    zcur = lambda c: pl.BlockSpec(blk, lambda n: (n, c))
    znext = lambda c: pl.BlockSpec(one, lambda n: (nxt_idx(n), c))
    zprev = lambda c: pl.BlockSpec(one, lambda n: (prv_idx(n), c))
    cur = pl.BlockSpec(blk, lambda n: (n, 0))
    nxt = pl.BlockSpec(one, lambda n: (nxt_idx(n), 0))
    grads_out = pl.pallas_call(
        body, name=name, grid=(n_steps,),
        in_specs=[zcur(0), znext(0), zcur(1), zprev(1), zcur(2), zprev(2), cur, nxt, cur, nxt, cur, nxt],
        out_specs=[cur, cur, cur],
        out_shape=[jax.ShapeDtypeStruct((S, GROUP_W), WIRE_DTYPE)] * 3,
        scratch_shapes=[pltpu.VMEM((N_HEADS * B, 2 * B), F32), pltpu.VMEM((N_HEADS * B, B), F32)],
        compiler_params=_params(("arbitrary",)),
    )(qkv, qkv, qkv, qkv, qkv, qkv, do, do, lse, lse, delta, delta)
    return [t.reshape(dil, S // dil, GROUP_W) for t in grads_out]


def _inproj_bwd(x, g, dxn, dz_abc, dqkv, dz_g, w_t, name):
    S, D = x.shape
    N = w_t.shape[0]
    tm = TM_MM
    n_abc = N_ABC * GROUP_W

    def body(x_ref, g_ref, dxn_ref, dabc_ref, q1, k1, v1, q2, k2, v2, q3, k3, v3, dg_ref, w_ref,
             dx_ref, dz_ref, h_ref, dgn_ref, s4_ref, s16_ref, tmp_ref):
        i = pl.program_id(0)

        @pl.when(i == 0)
        def _():
            dgn_ref[...] = jnp.zeros_like(dgn_ref)

        dz_ref[:, 0:n_abc] = dabc_ref[...].astype(MXU_DTYPE)
        for j, parts in enumerate(((q1, q2, q3), (k1, k2, k3), (v1, v2, v3))):
            c0 = n_abc + j * GROUP_W
            _interleave(parts[1], s4_ref, ATTN_DILATIONS[1])
            _interleave(parts[2], s16_ref, ATTN_DILATIONS[2], tmp_ref)
            dz_ref[:, c0:c0 + GROUP_W] = (parts[0][0] + _get(s4_ref) + _get(s16_ref)).astype(MXU_DTYPE)
        dz_ref[:, n_abc + 3 * GROUP_W:] = dg_ref[...].astype(MXU_DTYPE)
        dh = jnp.dot(dz_ref[...], w_ref[...], preferred_element_type=F32)
        xv = x_ref[...]
        r = lax.rsqrt(jnp.mean(xv * xv, axis=-1, keepdims=True) + NORM_EPS)
        xn = xv * r
        gv = g_ref[...]
        h_ref[...] = (xn * gv).astype(MXU_DTYPE)
        dgn_ref[...] += _colsum(dh * xn)
        dn = dh * gv
        dx_ref[...] = dxn_ref[...] + r * (dn - xn * jnp.mean(dn * xn, axis=-1, keepdims=True))

    row = lambda w: pl.BlockSpec((tm, w), lambda i: (i, 0))
    flat = [t for p in dqkv for t in p]
    dil_specs = [_dilated_spec(tm, GROUP_W, dil) for dil in ATTN_DILATIONS for _ in range(3)]
    return pl.pallas_call(
        body, name=name, grid=(S // tm,),
        in_specs=[row(D), pl.BlockSpec((1, D), lambda i: (0, 0)), row(D), row(n_abc)] + dil_specs
                 + [row(GROUP_W), pl.BlockSpec((N, D), lambda i: (0, 0), pipeline_mode=pl.Buffered(1))],
        out_specs=[row(D), row(N), row(D), pl.BlockSpec((1, D), lambda i: (0, 0))],
        out_shape=[jax.ShapeDtypeStruct((S, D), F32), jax.ShapeDtypeStruct((S, N), MXU_DTYPE),
                   jax.ShapeDtypeStruct((S, D), MXU_DTYPE), jax.ShapeDtypeStruct((1, D), F32)],
        scratch_shapes=[_lane_scratch(tm, GROUP_W)] * 3,
        compiler_params=_params(("arbitrary",)),
    )(x, g, dxn, dz_abc, *flat, dz_g, w_t)


def _inproj_wgrad(h, dz, name):
    S, D = h.shape
    N = dz.shape[1]
    tm = TM_WGRAD
    nj = 1
    cw = N // nj
    per = N_DEV // nj
    n_loc = N // N_DEV

    def body(h_ref, dz_ref, dw_ref, acc_ref):
        i = pl.program_id(1)

        @pl.when(i == 0)
        def _():
            acc_ref[...] = jnp.zeros_like(acc_ref)

        acc_ref[...] += _mm_tn(dz_ref[...], h_ref[...])

        @pl.when(i == S // tm - 1)
        def _():
            for b in range(per):
                dw_ref[b] = acc_ref[b * n_loc:(b + 1) * n_loc, :].astype(dw_ref.dtype)

    return pl.pallas_call(
        body, name=name, grid=(nj, S // tm),
        in_specs=[pl.BlockSpec((tm, D), lambda j, i: (i, 0)), pl.BlockSpec((tm, cw), lambda j, i: (i, j))],
        out_specs=pl.BlockSpec((per, n_loc, D), lambda j, i: (j, 0, 0)),
        out_shape=jax.ShapeDtypeStruct((N_DEV, n_loc, D), WIRE_DTYPE),
        scratch_shapes=[pltpu.VMEM((cw, D), F32)],
        compiler_params=_params(("parallel", "arbitrary")),
    )(h, dz)


def _my_place():
    return lax.axis_index("x"), lax.axis_index("y"), lax.axis_index("c")


def _peer(x, y, c, k):
    px = 1 - x if k & 4 else x
    py = 1 - y if k & 2 else y
    pc = 1 - c if k & 1 else c
    return (px, py, pc), 4 * px + 2 * py + pc


HBM_SPEC = pl.BlockSpec(memory_space=pltpu.HBM)
SEM_SPEC = pl.BlockSpec(memory_space=pltpu.SEMAPHORE)
SPLIT_EFFECT = pltpu.SideEffectType.DATAFLOW_SIDE_EFFECTING
N_PEERS = N_DEV - 1


def _exchange_copies(srcs, lands, send_sems, recv_sems, whole, arrival):
    x, y, c = _my_place()
    me = 4 * x + 2 * y + c
    copies = []
    for t in range(len(srcs)):
        for k in range(1, N_DEV):
            peer, pidx = _peer(x, y, c, k)
            copies.append(pltpu.make_async_remote_copy(
                src_ref=srcs[t] if whole[t] else srcs[t].at[pidx],
                dst_ref=lands[t].at[pidx if arrival else me], send_sem=send_sems.at[t * N_PEERS + k - 1],
                recv_sem=recv_sems.at[t * N_PEERS + k - 1], device_id=peer, device_id_type=MESH))
    return copies


def _exchange_start(groups, name, after=None):
    sizes = [len(g) for g in groups]
    whole = [w for g in groups for _, w in g]
    srcs = [pltpu.with_memory_space_constraint(a, pltpu.HBM) for g in groups for a, _ in g]
    lands = [pltpu.with_memory_space_constraint(lax.empty(((N_DEV,) + a.shape) if w else a.shape, a.dtype), pltpu.HBM)
             for a, w in zip(srcs, whole)]
    n = len(srcs)
    n_g = len(groups)
    extra = [] if after is None else [after]
    n_in = 2 * n + len(extra)

    def body(*refs):
        src_refs, land_refs = refs[:n], refs[n:2 * n]
        sem_refs = refs[n_in + 2 * n:n_in + 2 * n + 2 * n_g]
        token = refs[-1]
        off = 0
        for gi, sz in enumerate(sizes):
            for send in _exchange_copies(src_refs[off:off + sz], land_refs[off:off + sz],
                                         sem_refs[2 * gi], sem_refs[2 * gi + 1], whole[off:off + sz], False):
                send.start()
            off += sz
        token[...] = jnp.zeros_like(token)

    sem_shapes = [pltpu.SemaphoreType.DMA((sz * N_PEERS,)) for sz in sizes for _ in range(2)]
    outs = pl.pallas_call(
        body, name=name,
        in_specs=[HBM_SPEC] * (2 * n) + [pl.BlockSpec(memory_space=pl.ANY)] * len(extra),
        out_specs=[HBM_SPEC] * (2 * n) + [SEM_SPEC] * (2 * n_g) + [pl.BlockSpec(memory_space=pltpu.VMEM)],
        out_shape=[pltpu.HBM(a.shape, a.dtype) for a in srcs + lands] + sem_shapes
                  + [jax.ShapeDtypeStruct((SUBLANES, LANES), F32)],
        input_output_aliases={i: i for i in range(2 * n)},
        compiler_params=pltpu.CompilerParams(has_side_effects=SPLIT_EFFECT),
    )(*srcs, *lands, *extra)
    handles, off = [], 0
    for gi, sz in enumerate(sizes):
        handles.append((outs[2 * n + 2 * gi], outs[2 * n + 2 * gi + 1], outs[off:off + sz], outs[n + off:n + off + sz],
                        whole[off:off + sz]))
        off += sz
    return handles, outs[-1]


def _exchange_wait(handle, after, name):
    send_sems, recv_sems, srcs, lands, whole = handle
    n = len(srcs)

    def body(*refs):
        src_refs, land_refs = refs[:n], refs[n:2 * n]
        for send in _exchange_copies(src_refs, land_refs, refs[2 * n], refs[2 * n + 1], whole, False):
            send.wait_send()
        for arrival in _exchange_copies(src_refs, land_refs, refs[2 * n], refs[2 * n + 1], whole, True):
            arrival.wait_recv()

    outs = pl.pallas_call(
        body, name=name,
        in_specs=[HBM_SPEC] * (2 * n) + [SEM_SPEC, SEM_SPEC, pl.BlockSpec(memory_space=pl.ANY)],
        out_specs=[HBM_SPEC] * (2 * n),
        out_shape=[pltpu.HBM(a.shape, a.dtype) for a in list(srcs) + list(lands)],
        input_output_aliases={i: i for i in range(2 * n)},
        compiler_params=pltpu.CompilerParams(has_side_effects=SPLIT_EFFECT),
    )(*srcs, *lands, send_sems, recv_sems, after)
    x, y, c = _my_place()
    me = 4 * x + 2 * y + c
    own = [s[None] if w else lax.dynamic_slice_in_dim(s, me, 1, axis=0) for s, w in zip(outs[:n], whole)]
    return [lax.dynamic_update_slice_in_dim(ld, o, me, axis=0) for ld, o in zip(outs[n:], own)]


def _sum_slots(parts, name):
    n = len(parts)

    def body(*refs):
        for p_ref, o_ref in zip(refs[:n], refs[n:]):
            acc = p_ref[0]
            for j in range(1, N_DEV):
                acc = acc + p_ref[j]
            o_ref[...] = acc

    vm = pl.BlockSpec(memory_space=pltpu.VMEM)
    return pl.pallas_call(
        body, name=name, in_specs=[vm] * n, out_specs=[vm] * n,
        out_shape=[jax.ShapeDtypeStruct(p.shape[1:], F32) for p in parts],
        compiler_params=pltpu.CompilerParams(vmem_limit_bytes=VMEM_LIMIT),
    )(*parts)


def _adamw_math(w, g, m, v):
    m = ADAM_B1 * m + (1.0 - ADAM_B1) * g
    v = ADAM_B2 * v + (1.0 - ADAM_B2) * (g * g)
    m_hat = m / (1.0 - ADAM_B1 ** ADAM_STEP)
    v_hat = v / (1.0 - ADAM_B2 ** ADAM_STEP)
    delta = -ADAM_LR * (m_hat / (jnp.sqrt(v_hat) + ADAM_EPS) + ADAM_WD * w)
    return delta, m, v


def _adamw_summed(parts, w, m, v, tr, name):
    depth, R, C = w.shape

    def body(*refs):
        p_refs = refs[:depth]
        w_ref, m_ref, v_ref, g_ref, d_ref, nm_ref, nv_ref = refs[depth:]
        lay = pl.program_id(0)
        for l in range(depth):
            @pl.when(lay == l)
            def _(p_ref=p_refs[l]):
                g = p_ref[0].astype(F32)
                for j in range(1, N_DEV):
                    g = g + p_ref[j].astype(F32)
                g_ref[0] = g
        d_ref[0], nm_ref[0], nv_ref[0] = _adamw_math(w_ref[0], g_ref[0], m_ref[0], v_ref[0])

    part_spec = lambda l: pl.BlockSpec((N_DEV, tr, C), lambda lay, i: (0, jnp.where(lay == l, i, 0), 0))
    row = pl.BlockSpec((1, tr, C), lambda lay, i: (lay, i, 0))
    return pl.pallas_call(
        body, name=name, grid=(depth, R // tr),
        in_specs=[part_spec(l) for l in range(depth)] + [row, row, row],
        out_specs=[row] * 4, out_shape=[jax.ShapeDtypeStruct((depth, R, C), F32)] * 4,
        compiler_params=_params(("arbitrary", "arbitrary")),
    )(*parts, w, m, v)


def _adamw_small(w, g, m, v, name):
    def body(w_ref, g_ref, m_ref, v_ref, d_ref, nm_ref, nv_ref):
        d_ref[...], nm_ref[...], nv_ref[...] = _adamw_math(w_ref[...], g_ref[...], m_ref[...], v_ref[...])

    vm = pl.BlockSpec(memory_space=pltpu.VMEM)
    return pl.pallas_call(
        body, name=name, in_specs=[vm] * 4, out_specs=[vm] * 3,
        out_shape=[jax.ShapeDtypeStruct(w.shape, F32)] * 3,
        compiler_params=pltpu.CompilerParams(vmem_limit_bytes=VMEM_LIMIT),
    )(w, g, m, v)


def _pack(arrays):
    flat = jnp.concatenate([a.reshape(-1) for a in arrays])
    pad = (-flat.shape[0]) % (SUBLANES * LANES)
    return jnp.pad(flat, (0, pad)).reshape(-1, LANES)


def _unpack(buf, like):
    flat = buf.reshape(-1)
    out, off = [], 0
    for a in like:
        out.append(flat[off:off + a.size].reshape(a.shape))
        off += a.size
    return out


def _block_diag(w):
    eye = jnp.eye(N_HEADS, dtype=w.dtype)
    return jnp.einsum('hij,hk->hikj', w, eye).reshape(GROUP_W, GROUP_W)


def _diag_blocks(w):
    return jnp.einsum('hihj->hij', w.reshape(N_HEADS, HEAD_DIM, N_HEADS, HEAD_DIM))


def _pad_rows(a):
    return jnp.pad(a, ((0, SUBLANES - a.shape[0]), (0, 0)))


def _mixer_params(l, conv_a_w, conv_r_w, conv_r_b, lru_wa, lru_ba, lru_wx, lru_bx, lru_lambda, gmlp_norm_g,
                  gmlp_ws, gmlp_bs):
    tril = jnp.tril(jnp.ones((GMLP_CHUNK, GMLP_CHUNK), dtype=bool))
    vec = jnp.stack([conv_r_b[l], lru_ba[l], lru_bx[l], lru_lambda[l], gmlp_norm_g[l]])
    return {
        "wA": _pad_rows(conv_a_w[l]), "wR": _pad_rows(conv_r_w[l]), "vec": _pad_rows(vec),
        "wa": _block_diag(lru_wa[l]).astype(MXU_DTYPE), "wx": _block_diag(lru_wx[l]).astype(MXU_DTYPE),
        "ws": jnp.where(tril[None], gmlp_ws[l], 0.0).astype(MXU_DTYPE),
        "bs": jnp.repeat(jnp.transpose(gmlp_bs[l]), HEAD_DIM, axis=1),
    }


MIXER_NAMES = ("conv_a_w", "conv_r_w", "conv_r_b", "lru_wa", "lru_ba", "lru_wx", "lru_bx", "lru_lambda",
               "gmlp_norm_g", "gmlp_ws", "gmlp_bs")
SMALL_NAMES = ("norm_g",) + MIXER_NAMES + ("final_g",)


def _local_step(x, loss_target, norm_g, get_w_in, get_w_out, emit_early, emit_late, conv_a_w, conv_r_w, conv_r_b,
                lru_wa, lru_ba, lru_wx, lru_bx, lru_lambda, gmlp_norm_g, gmlp_ws, gmlp_bs, final_g):
    depth = norm_g.shape[0]
    D = x.shape[1]
    small = (conv_a_w, conv_r_w, conv_r_b, lru_wa, lru_ba, lru_wx, lru_bx, lru_lambda, gmlp_norm_g, gmlp_ws, gmlp_bs)
    saved = []
    for l in range(depth):
        mp = _mixer_params(l, *small)
        w_in_l = get_w_in(l, x)
        z, z_g, *qkv, y_abc, hs, lru_a, lru_mult = _inproj_mix_fwd(
            x, norm_g[l].reshape(1, D), w_in_l, mp, f"inproj_mix_fwd_{l}")
        attn =[_attn_fwd(qkv[p], dil, f"attn_fwd_d{dil}_{l}") for p, dil in enumerate(ATTN_DILATIONS)]
        w_out_l = get_w_out(l, y_abc)
        x_new, y, o, *lse = _outproj(x, z_g, y_abc, attn, w_out_l, f"outproj_{l}")
        saved.append((x, z, z_g, qkv, (hs, lru_a, lru_mult), y, o, lse, mp, w_in_l, w_out_l))
        x = x_new
    dx, loss, d_final_g = _loss_head(x, final_g.reshape(1, D), loss_target, "loss_head")
    token = None
    for l in reversed(range(depth)):
        x_l, z, z_g, qkv, lru, y, o, lse, mp, w_in_l, w_out_l = saved[l]
        if token is not None:
            mp = dict(mp, vec=mp["vec"] + token[0, 0])
        (dw_out, dz_abc, dz_g, do1, do4, do16, dl1, dl4, dl16, dwA, dwR, dvec, dwa, dwx, dws, dbs) = _outproj_mix_bwd(
            dx, y, w_out_l, z, z_g, *lru, o, mp, f"outproj_mix_bwd_{l}")
        token = emit_early(l, dw_out, [
            dwA[:conv_a_w.shape[1]], dwR[:conv_r_w.shape[1]], dvec[0], _diag_blocks(dwa), dvec[1], _diag_blocks(dwx),
            dvec[2], dvec[3], dvec[4], dws, jnp.transpose(dbs[:, :N_HEADS])])
        g_row = norm_g[l].reshape(1, D)
        if token is not None:
            g_row = g_row + token[0, 0]
        dqkv = [_attn_bwd(qkv[p], do, lse[p], dl, dil, f"attn_bwd_d{dil}_{l}")
                for p, (dil, do, dl) in enumerate(zip(ATTN_DILATIONS, (do1, do4, do16), (dl1, dl4, dl16)))]
        dx, dz, h, dng = _inproj_bwd(x_l, g_row, dx, dz_abc, dqkv, dz_g, w_in_l, f"inproj_bwd_{l}")
        dw_in = _inproj_wgrad(h, dz, f"inproj_wgrad_{l}")
        token = emit_late(l, dw_in, [dng[0]] + ([d_final_g[0]] if l == depth - 1 else []))
    return loss[0, 0], dx
WEIGHT_NAMES = ("norm_g", "w_in", "conv_a_w", "conv_r_w", "conv_r_b", "lru_wa", "lru_ba", "lru_wx", "lru_bx",
                "lru_lambda", "gmlp_norm_g", "gmlp_ws", "gmlp_bs", "w_out", "final_g")


def kernel(x, norm_g, w_in, conv_a_w, conv_r_w, conv_r_b, lru_wa, lru_ba, lru_wx, lru_bx, lru_lambda, gmlp_norm_g, gmlp_ws, gmlp_bs, w_out, final_g, loss_target, m_norm_g, m_w_in, m_conv_a_w, m_conv_r_w, m_conv_r_b, m_lru_wa, m_lru_ba, m_lru_wx, m_lru_bx, m_lru_lambda, m_gmlp_norm_g, m_gmlp_ws, m_gmlp_bs, m_w_out, m_final_g, v_norm_g, v_w_in, v_conv_a_w, v_conv_r_w, v_conv_r_b, v_lru_wa, v_lru_ba, v_lru_wx, v_lru_bx, v_lru_lambda, v_gmlp_norm_g, v_gmlp_ws, v_gmlp_bs, v_w_out, v_final_g):
    w = dict(norm_g=norm_g, w_in=w_in, conv_a_w=conv_a_w, conv_r_w=conv_r_w, conv_r_b=conv_r_b, lru_wa=lru_wa,
             lru_ba=lru_ba, lru_wx=lru_wx, lru_bx=lru_bx, lru_lambda=lru_lambda, gmlp_norm_g=gmlp_norm_g,
             gmlp_ws=gmlp_ws, gmlp_bs=gmlp_bs, w_out=w_out, final_g=final_g)
    m = dict(norm_g=m_norm_g, w_in=m_w_in, conv_a_w=m_conv_a_w, conv_r_w=m_conv_r_w, conv_r_b=m_conv_r_b,
             lru_wa=m_lru_wa, lru_ba=m_lru_ba, lru_wx=m_lru_wx, lru_bx=m_lru_bx, lru_lambda=m_lru_lambda,
             gmlp_norm_g=m_gmlp_norm_g, gmlp_ws=m_gmlp_ws, gmlp_bs=m_gmlp_bs, w_out=m_w_out, final_g=m_final_g)
    v = dict(norm_g=v_norm_g, w_in=v_w_in, conv_a_w=v_conv_a_w, conv_r_w=v_conv_r_w, conv_r_b=v_conv_r_b,
             lru_wa=v_lru_wa, lru_ba=v_lru_ba, lru_wx=v_lru_wx, lru_bx=v_lru_bx, lru_lambda=v_lru_lambda,
             gmlp_norm_g=v_gmlp_norm_g, gmlp_ws=v_gmlp_ws, gmlp_bs=v_gmlp_bs, w_out=v_w_out, final_g=v_final_g)
    depth, D, n_loc = w_in.shape
    e_loc = w_out.shape[1]
    cx, cy, cc = _my_place()
    me = 4 * cx + 2 * cy + cc

    transposed = lambda a: jnp.transpose(a, (0, 2, 1))
    w_in_t, m_w_in_t, v_w_in_t = transposed(w_in), transposed(m_w_in), transposed(v_w_in)
    w_in_w, w_out_w = w_in_t.astype(MXU_DTYPE), w_out.astype(MXU_DTYPE)
    c_loc = conv_a_w.shape[2]
    taps = (conv_a_w, conv_r_w)
    first, _ = _exchange_start([[(w_in_w[0], True), (_pack(taps), True)], [(w_out_w[0], True)]], "gather_start_first")
    full_in = lambda g: g.reshape(N_DEV * n_loc, D)
    full_out = lambda g: g.reshape(N_DEV * e_loc, D)

    g_in0, g_taps = _exchange_wait(first[0], x, "gather_wait_in_0")
    groups = [[(w_in_w[l], True), (w_out_w[l], True)] for l in range(1, depth)]
    gathers, rest_token = _exchange_start(groups, "gather_start_rest", after=g_taps)
    g_taps = g_taps.reshape(N_DEV, -1) + rest_token[0, 0]
    conv_full, off = [], 0
    for a in taps:
        part = g_taps[:, off:off + a.size].reshape((N_DEV,) + a.shape)
        conv_full.append(jnp.transpose(part, (1, 2, 0, 3)).reshape(a.shape[:2] + (N_DEV * c_loc,)))
        off += a.size
    conv_a_full, conv_r_full = conv_full
    later = {}

    def get_w_in(l, after):
        if l == 0:
            return full_in(g_in0)
        g_in, later[l] = _exchange_wait(gathers[l - 1], after, f"gather_wait_{l}")
        return full_in(g_in)

    def get_w_out(l, after):
        if l == 0:
            return full_out(_exchange_wait(first[1], after, "gather_wait_out_0")[0])
        return full_out(later[l])

    early, late, last_token, held = {}, {}, [None], {}

    def emit_early(l, dw_out, mixer_grads):
        group = [(dw_out.reshape(N_DEV, e_loc, D), False), (_pack(mixer_grads), True)]
        if l > 0:
            held[l] = (group, mixer_grads)
            return None
        handles, token = _exchange_start([group], f"early_start_{l}")
        early[l] = (handles[0], mixer_grads)
        return token

    def emit_late(l, dw_in, norm_grads):
        groups = [[(_pack(norm_grads), True)], [(dw_in, False)]] + ([held[l][0]] if l in held else [])
        handles, token = _exchange_start(groups, f"late_start_{l}")
        late[l] = (handles[0], handles[1], norm_grads)
        if l in held:
            early[l] = (handles[2], held[l][1])
        last_token[0] = token
        return token

    loss, grad_x = _local_step(
        x[0], loss_target[0], norm_g, get_w_in, get_w_out, emit_early, emit_late, conv_a_full, conv_r_full, conv_r_b,
        lru_wa, lru_ba, lru_wx, lru_bx, lru_lambda, gmlp_norm_g, gmlp_ws, gmlp_bs, final_g)
    loss = lax.psum(loss, ("x", "y", "c"))

    r_in, r_out, small_parts = {}, {}, []
    for l in reversed(range(depth)):
        r_out[l], r_mix = _exchange_wait(early[l][0], last_token[0], f"early_wait_{l}")
        (r_norm,) = _exchange_wait(late[l][0], last_token[0], f"late_wait_norm_{l}")
        small_parts += [r_mix, r_norm]
        if l > 0:
            (r_in[l],) = _exchange_wait(late[l][1], last_token[0], f"late_wait_{l}")
    big = {"w_out": _adamw_summed([r_out[l] for l in range(depth)], w_out, m_w_out, v_w_out, e_loc, "adamw_w_out")}

    sums = _sum_slots(small_parts, "sum_small_grads")
    by_layer = {}
    for i, l in enumerate(reversed(range(depth))):
        mix = _unpack(sums[2 * i], early[l][1])
        nrm = _unpack(sums[2 * i + 1], late[l][2])
        by_layer[l] = dict(zip(MIXER_NAMES, mix), norm_g=nrm[0])
        if l == depth - 1:
            g_final = nrm[1]
    g_small = {k: jnp.stack([by_layer[l][k] for l in range(depth)]) for k in ("norm_g",) + MIXER_NAMES}
    g_small["final_g"] = g_final
    for k in ("conv_a_w", "conv_r_w"):
        g_small[k] = lax.dynamic_slice_in_dim(g_small[k], me * c_loc, c_loc, axis=2)
    packs = [_pack([d[k] for k in SMALL_NAMES]) for d in (w, g_small, m, v)]
    res = _adamw_small(*packs, "adamw_small")
    like = [w[k] for k in SMALL_NAMES]
    d_s, m_s, v_s = (dict(zip(SMALL_NAMES, _unpack(r, like))) for r in res)

    (r_in[0],) = _exchange_wait(late[0][1], res[0], "late_wait_0")
    big["w_in"] = [transposed(a) for a in _adamw_summed(
        [r_in[l] for l in range(depth)], w_in_t, m_w_in_t, v_w_in_t, n_loc // 2, "adamw_w_in")]

    grad, delta, new_m, new_v = {}, {}, {}, {}
    for k in WEIGHT_NAMES:
        if k in big:
            grad[k], delta[k], new_m[k], new_v[k] = big[k]
        else:
            grad[k], delta[k], new_m[k], new_v[k] = g_small[k], d_s[k], m_s[k], v_s[k]
    return (loss, grad_x[None], *[grad[k] for k in WEIGHT_NAMES], *[delta[k] for k in WEIGHT_NAMES],
            *[new_m[k] for k in WEIGHT_NAMES], *[new_v[k] for k in WEIGHT_NAMES])
```

```python
import math

import jax
import jax.numpy as jnp
from jax import lax
from jax.experimental import pallas as pl
from jax.experimental.pallas import tpu as pltpu

F32 = jnp.float32
MXU_DTYPE = jnp.bfloat16
WIRE_DTYPE = jnp.bfloat16
MESH = pl.DeviceIdType.MESH

N_DEV = 8
GROUP_W = 256
N_HEADS = 4
HEAD_DIM = 64
N_ABC = 9
GMLP_CHUNK = 128
ATTN_BLOCK = 128
ATTN_FWD_BLOCKS_PER_STEP = 8
ATTN_BWD_BLOCKS_PER_STEP = 8
ATTN_DILATIONS = (1, 4, 16)
NORM_EPS = 1e-6
RG_C = 8.0
SUBLANES = 8
LANES = 128
VMEM_LIMIT = 56 * 1024 * 1024

ADAM_LR = 0.001
ADAM_B1 = 0.9
ADAM_B2 = 0.999
ADAM_EPS = 1e-08
ADAM_WD = 0.01
ADAM_STEP = 10

TM_MIX = 512
TM_MM = 512
TM_WGRAD = 1024


def _params(sem, vmem=VMEM_LIMIT):
    return pltpu.CompilerParams(dimension_semantics=sem, vmem_limit_bytes=vmem)


def _mm_tn(a, b):
    return lax.dot_general(a.astype(MXU_DTYPE), b.astype(MXU_DTYPE), (((0,), (0,)), ((), ())),
                           preferred_element_type=F32)


def _mm_nt(a, b):
    return lax.dot_general(a.astype(MXU_DTYPE), b.astype(MXU_DTYPE), (((1,), (1,)), ((), ())),
                           preferred_element_type=F32)


def _sigmoid(x):
    return 0.5 * jnp.tanh(0.5 * x) + 0.5


def _sigmoid_small_exact(x):
    return 1.0 / (1.0 + jnp.exp(-x))


def _silu_and_grad(x):
    s = _sigmoid(x)
    return x * s, s * (1.0 + x * (1.0 - s))


_GELU_K = math.sqrt(2.0 / math.pi)
_GELU_C = 0.044715


def _gelu_and_grad(x):
    x2 = x * x
    t = jnp.tanh(_GELU_K * (x + _GELU_C * x * x2))
    val = 0.5 * x * (1.0 + t)
    grad = 0.5 * (1.0 + t) + 0.5 * x * (1.0 - t * t) * (_GELU_K * (1.0 + 3.0 * _GELU_C * x2))
    return val, grad


def _gelu(x):
    return 0.5 * x * (1.0 + jnp.tanh(_GELU_K * (x + _GELU_C * x * x * x)))


def _expm1_nonpos(u):
    poly = 1.0 / math.factorial(9)
    for k in range(8, 0, -1):
        poly = poly * u + 1.0 / math.factorial(k)
    return jnp.where(u > -0.25, poly * u, jnp.exp(u) - 1.0)


def _softplus(x):
    return jnp.maximum(x, 0.0) + jnp.log(1.0 + jnp.exp(-jnp.abs(x)))


def _shift_down(t, halo, k):
    rolled = pltpu.roll(t, k, 0)
    hr = pltpu.roll(halo, k, 0)
    row = lax.broadcasted_iota(jnp.int32, halo.shape, 0)
    first = jnp.where(row < k, hr, rolled[0:SUBLANES])
    return jnp.concatenate([first, rolled[SUBLANES:]], axis=0)


def _shift_up(t, nxt, k):
    tm = t.shape[0]
    rolled = pltpu.roll(t, tm - k, 0)
    nr = pltpu.roll(nxt, SUBLANES - k, 0)
    row = lax.broadcasted_iota(jnp.int32, nxt.shape, 0)
    last = jnp.where(row >= SUBLANES - k, nr, rolled[tm - SUBLANES:tm])
    return jnp.concatenate([rolled[:tm - SUBLANES], last], axis=0)


def _scan_fwd(a, b):
    tm = a.shape[0]
    row = lax.broadcasted_iota(jnp.int32, a.shape, 0)
    s = 1
    while s < tm:
        a_s = pltpu.roll(a, s, 0)
        b_s = pltpu.roll(b, s, 0)
        m = row >= s
        b = jnp.where(m, a * b_s + b, b)
        a = jnp.where(m, a * a_s, a)
        s *= 2
    return a, b


def _scan_rev(a, g):
    tm = a.shape[0]
    row = lax.broadcasted_iota(jnp.int32, a.shape, 0)
    s = 1
    while s < tm:
        a_s = pltpu.roll(a, tm - s, 0)
        g_s = pltpu.roll(g, tm - s, 0)
        m = row < tm - s
        g = jnp.where(m, g + a * g_s, g)
        a = jnp.where(m, a * a_s, a)
        s *= 2
    return g


def _group_rows(scr_ref, row, n_groups):
    return jnp.concatenate([scr_ref[pl.ds(c, 1), pl.ds(row, n_groups, stride=SUBLANES), :][0]
                            for c in range(scr_ref.shape[0])], axis=1)


def _spread_rows(rows_ref, n_groups, w):
    return jnp.concatenate([jnp.broadcast_to(rows_ref[g:g + 1, :], (SUBLANES, w)) for g in range(n_groups)], axis=0)


def _scan_groups(a, b, reverse):
    tm, w = a.shape
    shape3 = (tm // SUBLANES, SUBLANES, w)
    a3, b3 = a.reshape(shape3), b.reshape(shape3)
    sub = lax.broadcasted_iota(jnp.int32, shape3, 1)
    s = 1
    while s < SUBLANES:
        shift = SUBLANES - s if reverse else s
        a_s = pltpu.roll(a3, shift, 1)
        b_s = pltpu.roll(b3, shift, 1)
        m = (sub < SUBLANES - s) if reverse else (sub >= s)
        b3 = jnp.where(m, a3 * b_s + b3, b3)
        a3 = jnp.where(m, a3 * a_s, a3)
        s *= 2
    return a3.reshape(tm, w), b3.reshape(tm, w)


def _scan_fwd_tile(a, b, h_in, sa_ref, sb_ref, sc_ref):
    tm, w = a.shape
    n_groups = tm // SUBLANES
    a_loc, b_loc = _scan_groups(a, b, False)
    _put(sa_ref, a_loc)
    _put(sb_ref, b_loc)
    a_end, b_end = _scan_fwd(_group_rows(sa_ref, SUBLANES - 1, n_groups), _group_rows(sb_ref, SUBLANES - 1, n_groups))
    h_end = b_end + a_end * h_in
    sc_ref[...] = _shift_down(h_end, jnp.broadcast_to(h_in, (SUBLANES, w)), 1)
    return b_loc + a_loc * _spread_rows(sc_ref, n_groups, w), h_end


def _scan_rev_tile(a, g, sa_ref, sb_ref, sc_ref):
    tm, w = a.shape
    n_groups = tm // SUBLANES
    a_loc, g_loc = _scan_groups(a, g, True)
    _put(sa_ref, a_loc)
    _put(sb_ref, g_loc)
    d_first = _scan_rev(_group_rows(sa_ref, 0, n_groups), _group_rows(sb_ref, 0, n_groups))
    sc_ref[...] = _shift_up(d_first, jnp.zeros((SUBLANES, w), F32), 1)
    return g_loc + a_loc * _spread_rows(sc_ref, n_groups, w)


def _lane_scratch(tm, w):
    return pltpu.VMEM((w // LANES, tm, LANES), F32)


def _put(scr_ref, val):
    for c in range(scr_ref.shape[0]):
        scr_ref[c] = val[:, c * LANES:(c + 1) * LANES].astype(F32)


def _get(scr_ref):
    return jnp.concatenate([scr_ref[c] for c in range(scr_ref.shape[0])], axis=1)


MAX_ROW_STRIDE = 4


def _strided_rows(c, start, n, stride):
    return (pl.ds(c, 1), pl.ds(start, n, stride=stride), slice(None))


def _deinterleave(src_ref, dst_ref, dil, tmp_ref=None):
    nc, tm, _ = src_ref.shape
    s1 = min(dil, MAX_ROW_STRIDE)
    s2 = dil // s1
    if s2 > 1:
        for r0 in range(s1):
            for c in range(nc):
                tmp_ref[c, r0 * (tm // s1):(r0 + 1) * (tm // s1), :] = src_ref[_strided_rows(c, r0, tm // s1, s1)][0]
    for r in range(dil):
        r1, r0 = divmod(r, s1)
        for c in range(nc):
            if dil == 1:
                piece = src_ref[c]
            elif s2 == 1:
                piece = src_ref[_strided_rows(c, r, tm // dil, dil)][0]
            else:
                piece = tmp_ref[_strided_rows(c, r0 * (tm // s1) + r1, tm // dil, s2)][0]
            dst_ref[r, :, c * LANES:(c + 1) * LANES] = piece.astype(dst_ref.dtype)


def _interleave(src_ref, dst_ref, dil, tmp_ref=None):
    nc, tm, _ = dst_ref.shape
    s1 = min(dil, MAX_ROW_STRIDE)
    s2 = dil // s1
    for r in range(dil):
        r1, r0 = divmod(r, s1)
        for c in range(nc):
            piece = src_ref[r, :, c * LANES:(c + 1) * LANES].astype(F32)[None]
            if s2 == 1:
                dst_ref[_strided_rows(c, r, tm // dil, dil)] = piece
            else:
                tmp_ref[_strided_rows(c, r0 * (tm // s1) + r1, tm // dil, s2)] = piece
    if s2 > 1:
        for r0 in range(s1):
            for c in range(nc):
                dst_ref[_strided_rows(c, r0, tm // s1, s1)] = (
                    tmp_ref[c, r0 * (tm // s1):(r0 + 1) * (tm // s1), :][None])


def _dilated_spec(tm, w, dil, index=lambda i: i):
    return pl.BlockSpec((dil, tm // dil, w), lambda i: (0, index(i), 0))


def _dilated_shape(S, w, dil, dtype):
    return jax.ShapeDtypeStruct((dil, S // dil, w), dtype)


def _head_masks(shape):
    lane = lax.broadcasted_iota(jnp.int32, shape, 1)
    return [(lane >= h * HEAD_DIM) & (lane < (h + 1) * HEAD_DIM) for h in range(N_HEADS)]


def _colsum(v):
    return jnp.sum(v, axis=0, keepdims=True)


def _conv_a(z_of, halo_of, w_ref):
    p = z_of(2) * z_of(0)
    p_h = halo_of(2) * halo_of(0)
    cv = w_ref[2:3, :] * p + w_ref[1:2, :] * _shift_down(p, p_h, 1) + w_ref[0:1, :] * _shift_down(p, p_h, 2)
    return p, p_h, cv


def _lru_gates(z_of, halo_of, wr_ref, vec_ref, wa_ref, wx_ref, saved=None):
    rx = z_of(4)
    rx_h = halo_of(4)
    sh = [rx, _shift_down(rx, rx_h, 1), _shift_down(rx, rx_h, 2), _shift_down(rx, rx_h, 3)]
    xc = (wr_ref[3:4, :] * sh[0] + wr_ref[2:3, :] * sh[1] + wr_ref[1:2, :] * sh[2]
          + wr_ref[0:1, :] * sh[3] + vec_ref[0:1, :])
    ga = _sigmoid_small_exact(jnp.dot(xc.astype(MXU_DTYPE), wa_ref[...], preferred_element_type=F32) + vec_ref[1:2, :])
    gi = _sigmoid(jnp.dot(xc.astype(MXU_DTYPE), wx_ref[...], preferred_element_type=F32) + vec_ref[2:3, :])
    sp = _softplus(-vec_ref[3:4, :])
    if saved is not None:
        return (xc, sh, ga, gi) + tuple(saved) + (sp,)
    log_a = (-RG_C * ga) * sp
    a = jnp.exp(log_a)
    mult = jnp.sqrt(-_expm1_nonpos(2.0 * log_a))
    return xc, sh, ga, gi, a, mult, sp


def _gmlp_fwd(z_of, vec_ref, ws_ref, bs_ref, tm):
    u = _gelu(z_of(6))
    gv = _gelu(z_of(7))
    rr = lax.rsqrt(jnp.mean(gv * gv, axis=-1, keepdims=True) + NORM_EPS)
    vn = (gv * rr) * vec_ref[4:5, :]
    masks = _head_masks((GMLP_CHUNK, GROUP_W))
    parts = []
    for c in range(tm // GMLP_CHUNK):
        vc = vn[c * GMLP_CHUNK:(c + 1) * GMLP_CHUNK].astype(MXU_DTYPE)
        acc = bs_ref[...]
        for h in range(N_HEADS):
            acc = acc + jnp.where(masks[h], jnp.dot(ws_ref[h], vc, preferred_element_type=F32), 0.0)
        parts.append(acc)
    return u, gv, rr, vn, jnp.concatenate(parts, axis=0)


def _mix_specs():
    const2 = lambda shape: pl.BlockSpec(shape, lambda i: (0, 0))
    return [const2((SUBLANES, GROUP_W)), const2((SUBLANES, GROUP_W)), const2((SUBLANES, GROUP_W)),
            const2((GROUP_W, GROUP_W)), const2((GROUP_W, GROUP_W)),
            pl.BlockSpec((N_HEADS, GMLP_CHUNK, GMLP_CHUNK), lambda i: (0, 0, 0)),
            const2((GMLP_CHUNK, GROUP_W))]


def _inproj_mix_fwd(x, g, w_t, mp, name):
    S, D = x.shape
    N = w_t.shape[0]
    tm = TM_MIX
    hb = tm // SUBLANES
    n_abc = N_ABC * GROUP_W
    n_qkv = 3 * GROUP_W

    def body(x_ref, g_ref, w_ref, wA_ref, wR_ref, vec_ref, wa_ref, wx_ref, ws_ref, bs_ref,
             z_ref, zg_ref, q1_ref, q4_ref, q16_ref, y_ref, h_ref, a_ref, mult_ref,
             qkv_ref, halo_ref, carry_ref, sa_ref, sb_ref, sc_ref, tmp_ref):
        @pl.when(pl.program_id(0) == 0)
        def _():
            halo_ref[...] = jnp.zeros_like(halo_ref)
            carry_ref[...] = jnp.zeros_like(carry_ref)

        xv = x_ref[...]
        r = lax.rsqrt(jnp.mean(xv * xv, axis=-1, keepdims=True) + NORM_EPS)
        hn = ((xv * r) * g_ref[...]).astype(MXU_DTYPE)
        z_ref[...] = _mm_nt(hn, w_ref[0:n_abc, :])
        _put(qkv_ref, _mm_nt(hn, w_ref[n_abc:n_abc + n_qkv, :]))
        zg_ref[...] = _mm_nt(hn, w_ref[n_abc + n_qkv:, :])
        for dil, ref in zip(ATTN_DILATIONS, (q1_ref, q4_ref, q16_ref)):
            _deinterleave(qkv_ref, ref, dil, tmp_ref)

        z_of = lambda c: z_ref[:, c * GROUP_W:(c + 1) * GROUP_W]
        halo_of = lambda c: halo_ref[:, c * GROUP_W:(c + 1) * GROUP_W]

        _, _, cv = _conv_a(z_of, halo_of, wA_ref)
        y_ref[:, 0:GROUP_W] = (z_of(1) * cv * _silu_and_grad(z_of(3))[0]).astype(y_ref.dtype)

        xc, _, _, gi, a, mult, _ = _lru_gates(z_of, halo_of, wR_ref, vec_ref, wa_ref, wx_ref)
        a_ref[...] = a
        mult_ref[...] = mult
        b = mult * (gi * xc)
        h, h_end = _scan_fwd_tile(a, b, carry_ref[SUBLANES - 1:SUBLANES, :], sa_ref, sb_ref, sc_ref)
        h_ref[...] = h
        carry_ref[...] = h_end[hb - SUBLANES:hb]
        y_ref[:, GROUP_W:2 * GROUP_W] = (h * _silu_and_grad(z_of(5))[0]).astype(y_ref.dtype)

        u, _, _, _, sp = _gmlp_fwd(z_of, vec_ref, ws_ref, bs_ref, tm)
        y_ref[:, 2 * GROUP_W:3 * GROUP_W] = (u * sp * _silu_and_grad(z_of(8))[0]).astype(y_ref.dtype)
        halo_ref[...] = z_ref[tm - SUBLANES:tm, :]

    row = lambda wd: pl.BlockSpec((tm, wd), lambda i: (i, 0))
    return pl.pallas_call(
        body, name=name, grid=(S // tm,),
        in_specs=[row(D), pl.BlockSpec((1, D), lambda i: (0, 0)),
                  pl.BlockSpec((N, D), lambda i: (0, 0), pipeline_mode=pl.Buffered(1))] + _mix_specs(),
        out_specs=[row(n_abc), row(GROUP_W)] + [_dilated_spec(tm, n_qkv, dil) for dil in ATTN_DILATIONS]
                  + [row(3 * GROUP_W)] + [row(GROUP_W)] * 3,
        out_shape=[jax.ShapeDtypeStruct((S, n_abc), F32), jax.ShapeDtypeStruct((S, GROUP_W), F32)]
                  + [_dilated_shape(S, n_qkv, dil, MXU_DTYPE) for dil in ATTN_DILATIONS]
                  + [jax.ShapeDtypeStruct((S, 3 * GROUP_W), MXU_DTYPE)] + [jax.ShapeDtypeStruct((S, GROUP_W), F32)] * 3,
        scratch_shapes=[_lane_scratch(tm, n_qkv), pltpu.VMEM((SUBLANES, n_abc), F32),
                        pltpu.VMEM((SUBLANES, GROUP_W), F32), _lane_scratch(tm, GROUP_W), _lane_scratch(tm, GROUP_W),
                        pltpu.VMEM((hb, GROUP_W), F32), _lane_scratch(tm, n_qkv)],
        compiler_params=_params(("arbitrary",)),
    )(x, g, w_t, mp["wA"], mp["wR"], mp["vec"], mp["wa"], mp["wx"], mp["ws"], mp["bs"])


_NEG = -1e30


def _slope(h):
    return 2.0 ** (-8.0 * (h + 1) / N_HEADS)


def _attn_bias(dil, offsets, n_keys):
    shape = (ATTN_BLOCK, n_keys)
    qi = lax.broadcasted_iota(jnp.int32, shape, 0)
    ki = lax.broadcasted_iota(jnp.int32, shape, 1)
    blocks = []
    for f in offsets:
        delta = qi + f - ki
        valid = (delta >= 0) & (delta <= ATTN_BLOCK)
        dist = (delta * dil).astype(F32)
        for h in range(N_HEADS):
            blocks.append(jnp.where(valid, -_slope(h) * dist, _NEG))
    return jnp.concatenate(blocks, axis=0)


def _stack_heads(t, masks):
    return jnp.concatenate([jnp.where(m, t, jnp.zeros_like(t)) for m in masks], axis=0)


def _unstack_heads(t4, masks, base=0):
    out = t4[base * ATTN_BLOCK:(base + 1) * ATTN_BLOCK]
    for h in range(1, N_HEADS):
        out = jnp.where(masks[h], t4[(base + h) * ATTN_BLOCK:(base + h + 1) * ATTN_BLOCK], out)
    return out


def _group_starts(n, per_step, group):
    if per_step % group == 0:
        return (lambda j: j % group == 0), True
    steps = group // per_step
    return (lambda j: (n % steps == 0) if j == 0 else False), (n + 1) % steps == 0


def _attn_fwd(qkv, dil, name):
    S = qkv.shape[0] * qkv.shape[1]
    qkv = qkv.reshape(S, qkv.shape[2])
    nb = S // ATTN_BLOCK
    group = nb // dil
    scale = 1.0 / math.sqrt(HEAD_DIM)
    B = ATTN_BLOCK
    per_step = ATTN_FWD_BLOCKS_PER_STEP

    def body(q_ref, kc_ref, kp_ref, vc_ref, vp_ref, o_ref, l_ref, bias_ref):
        n = pl.program_id(0)

        @pl.when(n == 0)
        def _():
            bias_ref[...] = _attn_bias(dil, (B,), 2 * B)

        masks = _head_masks((B, GROUP_W))
        starts, _ = _group_starts(n, per_step, group)
        for j in range(per_step):
            own = slice(j * B, (j + 1) * B)
            before = slice((j - 1) * B, j * B)
            qs = _stack_heads(q_ref[own], masks)
            keys = jnp.concatenate([kp_ref[...] if j == 0 else kc_ref[before], kc_ref[own]], axis=0)
            vals = jnp.concatenate([vp_ref[...] if j == 0 else vc_ref[before], vc_ref[own]], axis=0)
            s = _mm_nt(qs, keys) * scale + bias_ref[...]
            if starts(j) is not False:
                key_col = lax.broadcasted_iota(jnp.int32, s.shape, 1)
                s = jnp.where(starts(j) & (key_col < B), _NEG, s)
            m = jnp.max(s, axis=-1, keepdims=True)
            p = jnp.exp(s - m)
            l = jnp.sum(p, axis=-1, keepdims=True)
            o4 = jnp.dot(p.astype(MXU_DTYPE), vals, preferred_element_type=F32)
            o_ref[own] = (_unstack_heads(o4, masks)
                          / _unstack_heads(jnp.broadcast_to(l, o4.shape), masks)).astype(o_ref.dtype)
            l_ref[own] = _unstack_heads(jnp.broadcast_to(m + jnp.log(l), o4.shape), masks)

    blk = (per_step * B, GROUP_W)
    cur = lambda c: pl.BlockSpec(blk, lambda n: (n, c))
    prev = lambda c: pl.BlockSpec((B, GROUP_W), lambda n: (jnp.maximum(n * per_step - 1, 0), c))
    out = pl.BlockSpec(blk, lambda n: (n, 0))
    o, l = pl.pallas_call(
        body, name=name, grid=(nb // per_step,),
        in_specs=[cur(0), cur(1), prev(1), cur(2), prev(2)],
        out_specs=[out, out],
        out_shape=[jax.ShapeDtypeStruct((S, GROUP_W), MXU_DTYPE), jax.ShapeDtypeStruct((S, GROUP_W), F32)],
        scratch_shapes=[pltpu.VMEM((N_HEADS * ATTN_BLOCK, 2 * ATTN_BLOCK), F32)],
        compiler_params=_params(("arbitrary",)),
    )(qkv, qkv, qkv, qkv, qkv)
    return o.reshape(dil, S // dil, GROUP_W), l.reshape(dil, S // dil, GROUP_W)


def _outproj(x, z_g, y_abc, attn, w_out, name):
    S, D = x.shape
    tm = TM_MM
    n_abc = 3 * GROUP_W

    def body(x_ref, g_ref, yabc_ref, o1, l1, o2, l2, o3, l3, w_ref,
             xn_ref, y_ref, o_ref, lse1_ref, lse4_ref, lse16_ref, so2, sl2, so3, sl3, slse, tmp_ref):
        for src, dst, dil in ((o2, so2, ATTN_DILATIONS[1]), (l2, sl2, ATTN_DILATIONS[1]),
                              (o3, so3, ATTN_DILATIONS[2]), (l3, sl3, ATTN_DILATIONS[2])):
            _interleave(src, dst, dil, tmp_ref)
        la, lb, lc = l1[0], _get(sl2), _get(sl3)
        mx = jnp.maximum(jnp.maximum(la, lb), lc)
        ea, eb, ec = jnp.exp(la - mx), jnp.exp(lb - mx), jnp.exp(lc - mx)
        den = ea + eb + ec
        o = (ea * o1[0].astype(F32) + eb * _get(so2) + ec * _get(so3)) / den
        o_ref[...] = o
        _put(slse, mx + jnp.log(den))
        for dil, ref in zip(ATTN_DILATIONS, (lse1_ref, lse4_ref, lse16_ref)):
            _deinterleave(slse, ref, dil, tmp_ref)
        y_d = o * _silu_and_grad(g_ref[...])[0]
        y_ref[:, 0:n_abc] = yabc_ref[...].astype(MXU_DTYPE)
        y_ref[:, n_abc:] = y_d.astype(MXU_DTYPE)
        xn_ref[...] = x_ref[...] + jnp.dot(y_ref[...], w_ref[...], preferred_element_type=F32)

    row = lambda w: pl.BlockSpec((tm, w), lambda i: (i, 0))
    dil_specs = [_dilated_spec(tm, GROUP_W, dil) for dil in ATTN_DILATIONS]
    (o1, l1), (o2, l2), (o3, l3) = attn
    return pl.pallas_call(
        body, name=name, grid=(S // tm,),
        in_specs=[row(D), row(GROUP_W), row(n_abc)] + [sp for sp in dil_specs for _ in range(2)]
                 + [pl.BlockSpec(w_out.shape, lambda i: (0, 0))],
        out_specs=[row(D), row(4 * GROUP_W), row(GROUP_W)] + dil_specs,
        out_shape=[jax.ShapeDtypeStruct((S, D), F32), jax.ShapeDtypeStruct((S, 4 * GROUP_W), MXU_DTYPE),
                   jax.ShapeDtypeStruct((S, GROUP_W), F32)]
                  + [_dilated_shape(S, GROUP_W, dil, F32) for dil in ATTN_DILATIONS],
        scratch_shapes=[_lane_scratch(tm, GROUP_W)] * 6,
        compiler_params=_params(("parallel",)),
    )(x, z_g, y_abc, o1, l1, o2, l2, o3, l3, w_out)


def _loss_head(x, g, target, name):
    S, D = x.shape
    tm = TM_MM

    def body(x_ref, g_ref, t_ref, dx_ref, loss_ref, dg_ref):
        i = pl.program_id(0)

        @pl.when(i == 0)
        def _():
            loss_ref[...] = jnp.zeros_like(loss_ref)
            dg_ref[...] = jnp.zeros_like(dg_ref)

        xv = x_ref[...]
        r = lax.rsqrt(jnp.mean(xv * xv, axis=-1, keepdims=True) + NORM_EPS)
        xn = xv * r
        err = xn * g_ref[...] - t_ref[...]
        per_tok = jnp.mean(err * err, axis=-1, keepdims=True)
        loss_ref[...] += 0.5 * jnp.sum(per_tok, axis=0, keepdims=True)
        dout = err * (1.0 / D)
        dg_ref[...] += _colsum(dout * xn)
        dxn = dout * g_ref[...]
        dx_ref[...] = r * (dxn - xn * jnp.mean(dxn * xn, axis=-1, keepdims=True))

    row = pl.BlockSpec((tm, D), lambda i: (i, 0))
    return pl.pallas_call(
        body, name=name, grid=(S // tm,),
        in_specs=[row, pl.BlockSpec((1, D), lambda i: (0, 0)), row],
        out_specs=[row, pl.BlockSpec((1, LANES), lambda i: (0, 0)), pl.BlockSpec((1, D), lambda i: (0, 0))],
        out_shape=[jax.ShapeDtypeStruct((S, D), F32), jax.ShapeDtypeStruct((1, LANES), F32),
                   jax.ShapeDtypeStruct((1, D), F32)],
        compiler_params=_params(("arbitrary",)),
    )(x, g, target)


def _outproj_mix_bwd(dx, y, w_out, z, z_g, hs, lru_a, lru_mult, o, mp, name):
    S, D = dx.shape
    E = y.shape[1]
    tm = TM_MIX
    hb = tm // SUBLANES
    nT = S // tm
    last_blk = S // SUBLANES - 1
    wcols = N_ABC * GROUP_W

    def body(dx_ref, y_ref, w_ref, z_ref, zh_ref, zn_ref, zg_ref, h_ref, hh_ref, a_ref, mult_ref, o_ref,
             wA_ref, wR_ref, vec_ref, wa_ref, wx_ref, ws_ref, bs_ref,
             dw_ref, dz_ref, dzg_ref, do1_ref, do4_ref, do16_ref, dl1_ref, dl4_ref, dl16_ref,
             dwA_ref, dwR_ref, dvec_ref, dwa_ref, dwx_ref, dws_ref, dbs_ref,
             hcarry_ref, xcarry_ref, bsacc_ref, do_ref, dl_ref, sa_ref, sb_ref, sc_ref, dy_ref, dyn_ref, acc_ref,
             tmp_ref):
        i = pl.program_id(0)
        ti = nT - 1 - i

        @pl.when(i == 0)
        def _():
            acc_ref[...] = jnp.zeros_like(acc_ref)
            dyn_ref[...] = jnp.zeros_like(dyn_ref)
            hcarry_ref[...] = jnp.zeros_like(hcarry_ref)
            xcarry_ref[...] = jnp.zeros_like(xcarry_ref)
            bsacc_ref[...] = jnp.zeros_like(bsacc_ref)
            dwA_ref[...] = jnp.zeros_like(dwA_ref)
            dwR_ref[...] = jnp.zeros_like(dwR_ref)
            dvec_ref[...] = jnp.zeros_like(dvec_ref)
            dwa_ref[...] = jnp.zeros_like(dwa_ref)
            dwx_ref[...] = jnp.zeros_like(dwx_ref)
            dws_ref[...] = jnp.zeros_like(dws_ref)
            dbs_ref[...] = jnp.zeros_like(dbs_ref)

        dxb = dx_ref[...].astype(MXU_DTYPE)
        dy_ref[...] = _mm_nt(dxb, w_ref[...])
        acc_ref[...] += _mm_tn(y_ref[...], dxb)

        @pl.when(i == nT - 1)
        def _():
            dw_ref[...] = acc_ref[...].astype(dw_ref.dtype)

        has_prev = ti > 0
        has_next = i > 0
        col = lambda c: slice(c * GROUP_W, (c + 1) * GROUP_W)
        z_of = lambda c: z_ref[:, col(c)]
        halo_of = lambda c: jnp.where(has_prev, zh_ref[:, col(c)], 0.0)
        next_of = lambda c: zn_ref[:, col(c)]

        p, p_h, cv = _conv_a(z_of, halo_of, wA_ref)
        sg, dsg = _silu_and_grad(z_of(3))
        a_b = z_of(1)
        dya = dy_ref[:, col(0)]
        dcv = dya * a_b * sg
        dcv_n = jnp.where(has_next, dyn_ref[...] * next_of(1) * _silu_and_grad(next_of(3))[0], 0.0)
        dp = (wA_ref[2:3, :] * dcv + wA_ref[1:2, :] * _shift_up(dcv, dcv_n, 1)
              + wA_ref[0:1, :] * _shift_up(dcv, dcv_n, 2))
        dwA_ref[2:3, :] += _colsum(dcv * p)
        dwA_ref[1:2, :] += _colsum(dcv * _shift_down(p, p_h, 1))
        dwA_ref[0:1, :] += _colsum(dcv * _shift_down(p, p_h, 2))
        def put_dz(c, val):
            dz_ref[:, col(c)] = val.astype(dz_ref.dtype)

        put_dz(0, dp * z_of(2))
        put_dz(1, dya * cv * sg)
        put_dz(2, dp * z_of(0))
        put_dz(3, dya * a_b * cv * dsg)

        xc, sh, ga, gi, a, mult, sp = _lru_gates(z_of, halo_of, wR_ref, vec_ref, wa_ref, wx_ref,
                                                 saved=(a_ref[...], mult_ref[...]))
        h = h_ref[...]
        h_prev = _shift_down(h, jnp.where(has_prev, hh_ref[...], 0.0), 1)
        sgr, dsgr = _silu_and_grad(z_of(5))
        dyb = dy_ref[:, col(1)]
        put_dz(5, dyb * h * dsgr)
        row = lax.broadcasted_iota(jnp.int32, (tm, GROUP_W), 0)
        g_in = dyb * sgr + jnp.where(row == tm - 1, hcarry_ref[0:1, :], 0.0)
        a_up = _shift_up(a, jnp.zeros((SUBLANES, GROUP_W), F32), 1)
        dH = _scan_rev_tile(a_up, g_in, sa_ref, sb_ref, sc_ref)
        hcarry_ref[...] = (a * dH)[0:SUBLANES]
        da = dH * h_prev
        gx = gi * xc
        dmult = dH * gx
        dgi = dH * mult * xc
        dxc = dH * mult * gi
        dlog_a = da * a - dmult * (a * a) / mult
        dga = dlog_a * (-RG_C * sp)
        dlam_row = _colsum(dlog_a * (-RG_C * ga)) * (-_sigmoid(-vec_ref[3:4, :]))
        dpre_a = dga * ga * (1.0 - ga)
        dpre_i = dgi * gi * (1.0 - gi)
        dwa_ref[...] += _mm_tn(xc, dpre_a)
        dwx_ref[...] += _mm_tn(xc, dpre_i)
        dxc = dxc + _mm_nt(dpre_a, wa_ref[...]) + _mm_nt(dpre_i, wx_ref[...])
        dvec_ref[0:1, :] += _colsum(dxc)
        dvec_ref[1:2, :] += _colsum(dpre_a)
        dvec_ref[2:3, :] += _colsum(dpre_i)
        dvec_ref[3:4, :] += dlam_row
        for k in range(4):
            dwR_ref[k:k + 1, :] += _colsum(dxc * sh[3 - k])
        dxc_n = xcarry_ref[...]
        put_dz(4, wR_ref[3:4, :] * dxc + wR_ref[2:3, :] * _shift_up(dxc, dxc_n, 1)
               + wR_ref[1:2, :] * _shift_up(dxc, dxc_n, 2) + wR_ref[0:1, :] * _shift_up(dxc, dxc_n, 3))
        xcarry_ref[...] = dxc[0:SUBLANES]

        c_u, c_v = z_of(6), z_of(7)
        u, du_dx = _gelu_and_grad(c_u)
        gv, dgv_dx = _gelu_and_grad(c_v)
        rr = lax.rsqrt(jnp.mean(gv * gv, axis=-1, keepdims=True) + NORM_EPS)
        xhat = gv * rr
        g_c = vec_ref[4:5, :]
        vn = xhat * g_c
        masks = _head_masks((GMLP_CHUNK, GROUP_W))
        tri_r = lax.broadcasted_iota(jnp.int32, (GMLP_CHUNK, GMLP_CHUNK), 0)
        tri_c = lax.broadcasted_iota(jnp.int32, (GMLP_CHUNK, GMLP_CHUNK), 1)
        tril = tri_r >= tri_c
        sgc, dsgc = _silu_and_grad(z_of(8))
        dyc = dy_ref[:, col(2)]
        dsp_full = dyc * u * sgc
        sp_parts, dvn_parts = [], []
        for c in range(tm // GMLP_CHUNK):
            rs = slice(c * GMLP_CHUNK, (c + 1) * GMLP_CHUNK)
            vc = vn[rs].astype(MXU_DTYPE)
            dsp_c = dsp_full[rs]
            bsacc_ref[...] += dsp_c
            acc = bs_ref[...]
            dvn_c = jnp.zeros((GMLP_CHUNK, GROUP_W), F32)
            for h in range(N_HEADS):
                w_h = ws_ref[h]
                acc = acc + jnp.where(masks[h], jnp.dot(w_h, vc, preferred_element_type=F32), 0.0)
                dsp_h = jnp.where(masks[h], dsp_c, 0.0).astype(MXU_DTYPE)
                dvn_c = dvn_c + _mm_tn(w_h, dsp_h)
                dws_ref[h] += jnp.where(tril, _mm_nt(dsp_h, vc), 0.0)
            sp_parts.append(acc)
            dvn_parts.append(dvn_c)
        spv = jnp.concatenate(sp_parts, axis=0)
        dvn = jnp.concatenate(dvn_parts, axis=0)
        put_dz(6, dyc * spv * sgc * du_dx)
        put_dz(8, dyc * u * spv * dsgc)
        dvec_ref[4:5, :] += _colsum(dvn * xhat)
        dgvn = dvn * g_c
        dgv = rr * (dgvn - xhat * jnp.mean(dgvn * xhat, axis=-1, keepdims=True))
        put_dz(7, dgv * dgv_dx)

        sgd, dsgd = _silu_and_grad(zg_ref[...])
        dyd = dy_ref[:, col(3)]
        ov = o_ref[...]
        do = dyd * sgd
        _put(do_ref, do)
        dzg_ref[...] = (dyd * ov * dsgd).astype(dzg_ref.dtype)
        prod = do * ov
        tmasks = _head_masks((tm, GROUP_W))
        dl = jnp.zeros((tm, GROUP_W), F32)
        for h in range(N_HEADS):
            dl = jnp.where(tmasks[h], jnp.sum(jnp.where(tmasks[h], prod, 0.0), axis=-1, keepdims=True), dl)
        _put(dl_ref, dl)
        for dil, d_out, l_out in zip(ATTN_DILATIONS, (do1_ref, do4_ref, do16_ref), (dl1_ref, dl4_ref, dl16_ref)):
            _deinterleave(do_ref, d_out, dil, tmp_ref)
            _deinterleave(dl_ref, l_out, dil, tmp_ref)

        @pl.when(i == nT - 1)
        def _():
            acc = bsacc_ref[...]
            lane = lax.broadcasted_iota(jnp.int32, (GMLP_CHUNK, LANES), 1)
            out = jnp.zeros((GMLP_CHUNK, LANES), F32)
            for h in range(N_HEADS):
                out = jnp.where(lane == h, jnp.sum(jnp.where(masks[h], acc, 0.0), axis=-1, keepdims=True), out)
            dbs_ref[...] = out

        dyn_ref[...] = dy_ref[0:SUBLANES, 0:GROUP_W]

    rev = lambda w: pl.BlockSpec((tm, w), lambda i: (nT - 1 - i, 0))
    prev8 = lambda w: pl.BlockSpec((SUBLANES, w), lambda i: (jnp.maximum((nT - 1 - i) * hb - 1, 0), 0))
    next8 = lambda w: pl.BlockSpec((SUBLANES, w), lambda i: (jnp.minimum((nT - i) * hb, last_blk), 0))
    const2 = lambda shape: pl.BlockSpec(shape, lambda i: (0, 0))
    dil_specs = [_dilated_spec(tm, GROUP_W, dil, lambda i: nT - 1 - i) for dil in ATTN_DILATIONS]
    dil_shapes = [_dilated_shape(S, GROUP_W, dil, F32) for dil in ATTN_DILATIONS]
    small = (SUBLANES, GROUP_W)
    sq = (GROUP_W, GROUP_W)
    ws_shape = (N_HEADS, GMLP_CHUNK, GMLP_CHUNK)
    return pl.pallas_call(
        body, name=name, grid=(nT,),
        in_specs=[rev(D), rev(E), pl.BlockSpec((E, D), lambda i: (0, 0), pipeline_mode=pl.Buffered(1)),
                  rev(wcols), prev8(wcols), next8(wcols), rev(GROUP_W), rev(GROUP_W), prev8(GROUP_W),
                  rev(GROUP_W), rev(GROUP_W), rev(GROUP_W)]
                 + _mix_specs(),
        out_specs=[const2((E, D)), rev(wcols), rev(GROUP_W)] + dil_specs + dil_specs
                  + [const2(small), const2(small), const2(small), const2(sq), const2(sq),
                     pl.BlockSpec(ws_shape, lambda i: (0, 0, 0)), const2((GMLP_CHUNK, LANES))],
        out_shape=[jax.ShapeDtypeStruct((E, D), WIRE_DTYPE),
                   jax.ShapeDtypeStruct((S, wcols), MXU_DTYPE), jax.ShapeDtypeStruct((S, GROUP_W), MXU_DTYPE)]
                  + [_dilated_shape(S, GROUP_W, dil, MXU_DTYPE) for dil in ATTN_DILATIONS] + dil_shapes
                  + [jax.ShapeDtypeStruct(small, F32)] * 3 + [jax.ShapeDtypeStruct(sq, F32)] * 2
                  + [jax.ShapeDtypeStruct(ws_shape, F32), jax.ShapeDtypeStruct((GMLP_CHUNK, LANES), F32)],
        scratch_shapes=[pltpu.VMEM(small, F32), pltpu.VMEM(small, F32), pltpu.VMEM((GMLP_CHUNK, GROUP_W), F32),
                        _lane_scratch(tm, GROUP_W), _lane_scratch(tm, GROUP_W),
                        _lane_scratch(tm, GROUP_W), _lane_scratch(tm, GROUP_W), pltpu.VMEM((hb, GROUP_W), F32),
                        pltpu.VMEM((tm, E), F32), pltpu.VMEM(small, F32), pltpu.VMEM((E, D), F32),
                        _lane_scratch(tm, GROUP_W)],
        compiler_params=_params(("arbitrary",)),
    )(dx, y, w_out, z, z, z, z_g, hs, hs, lru_a, lru_mult, o, mp["wA"], mp["wR"], mp["vec"], mp["wa"], mp["wx"], mp["ws"], mp["bs"])


def _attn_bwd(qkv, do, lse, delta, dil, name):
    S = qkv.shape[0] * qkv.shape[1]
    flat = lambda t: t.reshape(S, t.shape[2])
    qkv, do, lse, delta = flat(qkv), flat(do), flat(lse), flat(delta)
    nb = S // ATTN_BLOCK
    group = nb // dil
    scale = 1.0 / math.sqrt(HEAD_DIM)
    B = ATTN_BLOCK
    per_step = ATTN_BWD_BLOCKS_PER_STEP
    n_steps = nb // per_step

    def body(qc_ref, qn_ref, kc_ref, kp_ref, vc_ref, vp_ref, doc_ref, don_ref, lc_ref, ln_ref, dc_ref, dn_ref,
             dq_ref, dk_ref, dv_ref, bias_ref, bias_next_ref):
        n = pl.program_id(0)
        starts, next_starts = _group_starts(n, per_step, group)

        @pl.when(n == 0)
        def _():
            bias_ref[...] = _attn_bias(dil, (B,), 2 * B)
            bias_next_ref[...] = _attn_bias(dil, (B,), B)

        masks = _head_masks((B, GROUP_W))

        def per_row(tile):
            return jnp.concatenate([jnp.max(jnp.where(masks[h], tile, _NEG), axis=-1, keepdims=True)
                                    for h in range(N_HEADS)], axis=0)

        def grads(q, dov, lse_tile, dl_tile, keys, vals, bias, dead):
            qs = _stack_heads(q, masks)
            dos = _stack_heads(dov.astype(MXU_DTYPE), masks)
            s = _mm_nt(qs, keys) * scale + bias
            if dead is not None:
                s = jnp.where(dead(s.shape), _NEG, s)
            p = jnp.exp(s - per_row(lse_tile))
            ds = (p * (_mm_nt(dos, vals) - per_row(dl_tile)) * scale).astype(MXU_DTYPE)
            return ds, _mm_tn(ds, qs), _mm_tn(p.astype(MXU_DTYPE), dos)

        for j in range(per_step):
            own = slice(j * B, (j + 1) * B)
            before = slice((j - 1) * B, j * B)
            keys = jnp.concatenate([kp_ref[...] if j == 0 else kc_ref[before], kc_ref[own]], axis=0)
            vals = jnp.concatenate([vp_ref[...] if j == 0 else vc_ref[before], vc_ref[own]], axis=0)
            dead = None
            if starts(j) is not False:
                dead = lambda shape, j=j: starts(j) & (lax.broadcasted_iota(jnp.int32, shape, 1) < B)
            ds, dk2, dv2 = grads(qc_ref[own], doc_ref[own], lc_ref[own], dc_ref[own], keys, vals, bias_ref[...], dead)
            dq_ref[own] = _unstack_heads(jnp.dot(ds, keys, preferred_element_type=F32), masks).astype(dq_ref.dtype)
            if j > 0:
                dk_ref[before] = (dk_own + dk2[:B]).astype(dk_ref.dtype)
                dv_ref[before] = (dv_own + dv2[:B]).astype(dv_ref.dtype)
            dk_own, dv_own = dk2[B:], dv2[B:]
        last = slice((per_step - 1) * B, per_step * B)
        if next_starts is not True:
            _, dk1, dv1 = grads(qn_ref[...], don_ref[...], ln_ref[...], dn_ref[...], kc_ref[last], vc_ref[last],
                                bias_next_ref[...], lambda shape: next_starts)
            dk_own, dv_own = dk_own + dk1, dv_own + dv1
        dk_ref[last] = dk_own.astype(dk_ref.dtype)
        dv_ref[last] = dv_own.astype(dv_ref.dtype)

    blk = (per_step * B, GROUP_W)
    one = (B, GROUP_W)
    nxt_idx = lambda n: jnp.minimum((n + 1) * per_step, nb - 1)
    prv_idx = lambda n: jnp.maximum(n * per_step - 1, 0)
    zcur = lambda c: pl.BlockSpec(blk, lambda n: (n, c))
    znext = lambda c: pl.BlockSpec(one, lambda n: (nxt_idx(n), c))
    zprev = lambda c: pl.BlockSpec(one, lambda n: (prv_idx(n), c))
    cur = pl.BlockSpec(blk, lambda n: (n, 0))
    nxt = pl.BlockSpec(one, lambda n: (nxt_idx(n), 0))
    grads_out = pl.pallas_call(
        body, name=name, grid=(n_steps,),
        in_specs=[zcur(0), znext(0), zcur(1), zprev(1), zcur(2), zprev(2), cur, nxt, cur, nxt, cur, nxt],
        out_specs=[cur, cur, cur],
        out_shape=[jax.ShapeDtypeStruct((S, GROUP_W), WIRE_DTYPE)] * 3,
        scratch_shapes=[pltpu.VMEM((N_HEADS * B, 2 * B), F32), pltpu.VMEM((N_HEADS * B, B), F32)],
        compiler_params=_params(("arbitrary",)),
    )(qkv, qkv, qkv, qkv, qkv, qkv, do, do, lse, lse, delta, delta)
    return [t.reshape(dil, S // dil, GROUP_W) for t in grads_out]


def _inproj_bwd(x, g, dxn, dz_abc, dqkv, dz_g, w_t, name):
    S, D = x.shape
    N = w_t.shape[0]
    tm = TM_MM
    n_abc = N_ABC * GROUP_W

    def body(x_ref, g_ref, dxn_ref, dabc_ref, q1, k1, v1, q2, k2, v2, q3, k3, v3, dg_ref, w_ref,
             dx_ref, dz_ref, h_ref, dgn_ref, s4_ref, s16_ref, tmp_ref):
        i = pl.program_id(0)

        @pl.when(i == 0)
        def _():
            dgn_ref[...] = jnp.zeros_like(dgn_ref)

        dz_ref[:, 0:n_abc] = dabc_ref[...].astype(MXU_DTYPE)
        for j, parts in enumerate(((q1, q2, q3), (k1, k2, k3), (v1, v2, v3))):
            c0 = n_abc + j * GROUP_W
            _interleave(parts[1], s4_ref, ATTN_DILATIONS[1])
            _interleave(parts[2], s16_ref, ATTN_DILATIONS[2], tmp_ref)
            dz_ref[:, c0:c0 + GROUP_W] = (parts[0][0] + _get(s4_ref) + _get(s16_ref)).astype(MXU_DTYPE)
        dz_ref[:, n_abc + 3 * GROUP_W:] = dg_ref[...].astype(MXU_DTYPE)
        dh = jnp.dot(dz_ref[...], w_ref[...], preferred_element_type=F32)
        xv = x_ref[...]
        r = lax.rsqrt(jnp.mean(xv * xv, axis=-1, keepdims=True) + NORM_EPS)
        xn = xv * r
        gv = g_ref[...]
        h_ref[...] = (xn * gv).astype(MXU_DTYPE)
        dgn_ref[...] += _colsum(dh * xn)
        dn = dh * gv
        dx_ref[...] = dxn_ref[...] + r * (dn - xn * jnp.mean(dn * xn, axis=-1, keepdims=True))

    row = lambda w: pl.BlockSpec((tm, w), lambda i: (i, 0))
    flat = [t for p in dqkv for t in p]
    dil_specs = [_dilated_spec(tm, GROUP_W, dil) for dil in ATTN_DILATIONS for _ in range(3)]
    return pl.pallas_call(
        body, name=name, grid=(S // tm,),
        in_specs=[row(D), pl.BlockSpec((1, D), lambda i: (0, 0)), row(D), row(n_abc)] + dil_specs
                 + [row(GROUP_W), pl.BlockSpec((N, D), lambda i: (0, 0), pipeline_mode=pl.Buffered(1))],
        out_specs=[row(D), row(N), row(D), pl.BlockSpec((1, D), lambda i: (0, 0))],
        out_shape=[jax.ShapeDtypeStruct((S, D), F32), jax.ShapeDtypeStruct((S, N), MXU_DTYPE),
                   jax.ShapeDtypeStruct((S, D), MXU_DTYPE), jax.ShapeDtypeStruct((1, D), F32)],
        scratch_shapes=[_lane_scratch(tm, GROUP_W)] * 3,
        compiler_params=_params(("arbitrary",)),
    )(x, g, dxn, dz_abc, *flat, dz_g, w_t)


def _inproj_wgrad(h, dz, name):
    S, D = h.shape
    N = dz.shape[1]
    tm = TM_WGRAD
    nj = 2
    cw = N // nj
    per = N_DEV // nj
    n_loc = N // N_DEV

    def body(h_ref, dz_ref, dw_ref, acc_ref):
        i = pl.program_id(1)

        @pl.when(i == 0)
        def _():
            acc_ref[...] = jnp.zeros_like(acc_ref)

        acc_ref[...] += _mm_tn(dz_ref[...], h_ref[...])

        @pl.when(i == S // tm - 1)
        def _():
            for b in range(per):
                dw_ref[b] = acc_ref[b * n_loc:(b + 1) * n_loc, :].astype(dw_ref.dtype)

    return pl.pallas_call(
        body, name=name, grid=(nj, S // tm),
        in_specs=[pl.BlockSpec((tm, D), lambda j, i: (i, 0)), pl.BlockSpec((tm, cw), lambda j, i: (i, j))],
        out_specs=pl.BlockSpec((per, n_loc, D), lambda j, i: (j, 0, 0)),
        out_shape=jax.ShapeDtypeStruct((N_DEV, n_loc, D), WIRE_DTYPE),
        scratch_shapes=[pltpu.VMEM((cw, D), F32)],
        compiler_params=_params(("parallel", "arbitrary")),
    )(h, dz)


def _my_place():
    return lax.axis_index("x"), lax.axis_index("y"), lax.axis_index("c")


def _peer(x, y, c, k):
    px = 1 - x if k & 4 else x
    py = 1 - y if k & 2 else y
    pc = 1 - c if k & 1 else c
    return (px, py, pc), 4 * px + 2 * py + pc


HBM_SPEC = pl.BlockSpec(memory_space=pltpu.HBM)
SEM_SPEC = pl.BlockSpec(memory_space=pltpu.SEMAPHORE)
SPLIT_EFFECT = pltpu.SideEffectType.DATAFLOW_SIDE_EFFECTING
N_PEERS = N_DEV - 1


def _exchange_copies(srcs, lands, send_sems, recv_sems, whole, arrival):
    x, y, c = _my_place()
    me = 4 * x + 2 * y + c
    copies = []
    for t in range(len(srcs)):
        for k in range(1, N_DEV):
            peer, pidx = _peer(x, y, c, k)
            copies.append(pltpu.make_async_remote_copy(
                src_ref=srcs[t] if whole[t] else srcs[t].at[pidx],
                dst_ref=lands[t].at[pidx if arrival else me], send_sem=send_sems.at[t * N_PEERS + k - 1],
                recv_sem=recv_sems.at[t * N_PEERS + k - 1], device_id=peer, device_id_type=MESH))
    return copies


def _exchange_start(groups, name, after=None):
    sizes = [len(g) for g in groups]
    whole = [w for g in groups for _, w in g]
    srcs = [pltpu.with_memory_space_constraint(a, pltpu.HBM) for g in groups for a, _ in g]
    lands = [pltpu.with_memory_space_constraint(lax.empty(((N_DEV,) + a.shape) if w else a.shape, a.dtype), pltpu.HBM)
             for a, w in zip(srcs, whole)]
    n = len(srcs)
    n_g = len(groups)
    extra = [] if after is None else [after]
    n_in = 2 * n + len(extra)

    def body(*refs):
        src_refs, land_refs = refs[:n], refs[n:2 * n]
        sem_refs = refs[n_in + 2 * n:n_in + 2 * n + 2 * n_g]
        token = refs[-1]
        off = 0
        for gi, sz in enumerate(sizes):
            for send in _exchange_copies(src_refs[off:off + sz], land_refs[off:off + sz],
                                         sem_refs[2 * gi], sem_refs[2 * gi + 1], whole[off:off + sz], False):
                send.start()
            off += sz
        token[...] = jnp.zeros_like(token)

    sem_shapes = [pltpu.SemaphoreType.DMA((sz * N_PEERS,)) for sz in sizes for _ in range(2)]
    outs = pl.pallas_call(
        body, name=name,
        in_specs=[HBM_SPEC] * (2 * n) + [pl.BlockSpec(memory_space=pl.ANY)] * len(extra),
        out_specs=[HBM_SPEC] * (2 * n) + [SEM_SPEC] * (2 * n_g) + [pl.BlockSpec(memory_space=pltpu.VMEM)],
        out_shape=[pltpu.HBM(a.shape, a.dtype) for a in srcs + lands] + sem_shapes
                  + [jax.ShapeDtypeStruct((SUBLANES, LANES), F32)],
        input_output_aliases={i: i for i in range(2 * n)},
        compiler_params=pltpu.CompilerParams(has_side_effects=SPLIT_EFFECT),
    )(*srcs, *lands, *extra)
    handles, off = [], 0
    for gi, sz in enumerate(sizes):
        handles.append((outs[2 * n + 2 * gi], outs[2 * n + 2 * gi + 1], outs[off:off + sz], outs[n + off:n + off + sz],
                        whole[off:off + sz]))
        off += sz
    return handles, outs[-1]


def _exchange_wait(handle, after, name):
    send_sems, recv_sems, srcs, lands, whole = handle
    n = len(srcs)

    def body(*refs):
        src_refs, land_refs = refs[:n], refs[n:2 * n]
        for send in _exchange_copies(src_refs, land_refs, refs[2 * n], refs[2 * n + 1], whole, False):
            send.wait_send()
        for arrival in _exchange_copies(src_refs, land_refs, refs[2 * n], refs[2 * n + 1], whole, True):
            arrival.wait_recv()

    outs = pl.pallas_call(
        body, name=name,
        in_specs=[HBM_SPEC] * (2 * n) + [SEM_SPEC, SEM_SPEC, pl.BlockSpec(memory_space=pl.ANY)],
        out_specs=[HBM_SPEC] * (2 * n),
        out_shape=[pltpu.HBM(a.shape, a.dtype) for a in list(srcs) + list(lands)],
        input_output_aliases={i: i for i in range(2 * n)},
        compiler_params=pltpu.CompilerParams(has_side_effects=SPLIT_EFFECT),
    )(*srcs, *lands, send_sems, recv_sems, after)
    x, y, c = _my_place()
    me = 4 * x + 2 * y + c
    own = [s[None] if w else lax.dynamic_slice_in_dim(s, me, 1, axis=0) for s, w in zip(outs[:n], whole)]
    return [lax.dynamic_update_slice_in_dim(ld, o, me, axis=0) for ld, o in zip(outs[n:], own)]


def _sum_slots(parts, name):
    n = len(parts)

    def body(*refs):
        for p_ref, o_ref in zip(refs[:n], refs[n:]):
            acc = p_ref[0]
            for j in range(1, N_DEV):
                acc = acc + p_ref[j]
            o_ref[...] = acc

    vm = pl.BlockSpec(memory_space=pltpu.VMEM)
    return pl.pallas_call(
        body, name=name, in_specs=[vm] * n, out_specs=[vm] * n,
        out_shape=[jax.ShapeDtypeStruct(p.shape[1:], F32) for p in parts],
        compiler_params=pltpu.CompilerParams(vmem_limit_bytes=VMEM_LIMIT),
    )(*parts)


def _adamw_math(w, g, m, v):
    m = ADAM_B1 * m + (1.0 - ADAM_B1) * g
    v = ADAM_B2 * v + (1.0 - ADAM_B2) * (g * g)
    m_hat = m / (1.0 - ADAM_B1 ** ADAM_STEP)
    v_hat = v / (1.0 - ADAM_B2 ** ADAM_STEP)
    delta = -ADAM_LR * (m_hat / (jnp.sqrt(v_hat) + ADAM_EPS) + ADAM_WD * w)
    return delta, m, v


def _adamw_summed(parts, w, m, v, tr, name):
    depth, R, C = w.shape

    def body(*refs):
        p_refs = refs[:depth]
        w_ref, m_ref, v_ref, g_ref, d_ref, nm_ref, nv_ref = refs[depth:]
        lay = pl.program_id(0)
        for l in range(depth):
            @pl.when(lay == l)
            def _(p_ref=p_refs[l]):
                g = p_ref[0].astype(F32)
                for j in range(1, N_DEV):
                    g = g + p_ref[j].astype(F32)
                g_ref[0] = g
        d_ref[0], nm_ref[0], nv_ref[0] = _adamw_math(w_ref[0], g_ref[0], m_ref[0], v_ref[0])

    part_spec = lambda l: pl.BlockSpec((N_DEV, tr, C), lambda lay, i: (0, jnp.where(lay == l, i, 0), 0))
    row = pl.BlockSpec((1, tr, C), lambda lay, i: (lay, i, 0))
    return pl.pallas_call(
        body, name=name, grid=(depth, R // tr),
        in_specs=[part_spec(l) for l in range(depth)] + [row, row, row],
        out_specs=[row] * 4, out_shape=[jax.ShapeDtypeStruct((depth, R, C), F32)] * 4,
        compiler_params=_params(("arbitrary", "arbitrary")),
    )(*parts, w, m, v)


def _adamw_small(w, g, m, v, name):
    def body(w_ref, g_ref, m_ref, v_ref, d_ref, nm_ref, nv_ref):
        d_ref[...], nm_ref[...], nv_ref[...] = _adamw_math(w_ref[...], g_ref[...], m_ref[...], v_ref[...])

    vm = pl.BlockSpec(memory_space=pltpu.VMEM)
    return pl.pallas_call(
        body, name=name, in_specs=[vm] * 4, out_specs=[vm] * 3,
        out_shape=[jax.ShapeDtypeStruct(w.shape, F32)] * 3,
        compiler_params=pltpu.CompilerParams(vmem_limit_bytes=VMEM_LIMIT),
    )(w, g, m, v)


def _pack(arrays):
    flat = jnp.concatenate([a.reshape(-1) for a in arrays])
    pad = (-flat.shape[0]) % (SUBLANES * LANES)
    return jnp.pad(flat, (0, pad)).reshape(-1, LANES)


def _unpack(buf, like):
    flat = buf.reshape(-1)
    out, off = [], 0
    for a in like:
        out.append(flat[off:off + a.size].reshape(a.shape))
        off += a.size
    return out


def _block_diag(w):
    eye = jnp.eye(N_HEADS, dtype=w.dtype)
    return jnp.einsum('hij,hk->hikj', w, eye).reshape(GROUP_W, GROUP_W)


def _diag_blocks(w):
    return jnp.einsum('hihj->hij', w.reshape(N_HEADS, HEAD_DIM, N_HEADS, HEAD_DIM))


def _pad_rows(a):
    return jnp.pad(a, ((0, SUBLANES - a.shape[0]), (0, 0)))


def _mixer_params(l, conv_a_w, conv_r_w, conv_r_b, lru_wa, lru_ba, lru_wx, lru_bx, lru_lambda, gmlp_norm_g,
                  gmlp_ws, gmlp_bs):
    tril = jnp.tril(jnp.ones((GMLP_CHUNK, GMLP_CHUNK), dtype=bool))
    vec = jnp.stack([conv_r_b[l], lru_ba[l], lru_bx[l], lru_lambda[l], gmlp_norm_g[l]])
    return {
        "wA": _pad_rows(conv_a_w[l]), "wR": _pad_rows(conv_r_w[l]), "vec": _pad_rows(vec),
        "wa": _block_diag(lru_wa[l]).astype(MXU_DTYPE), "wx": _block_diag(lru_wx[l]).astype(MXU_DTYPE),
        "ws": jnp.where(tril[None], gmlp_ws[l], 0.0).astype(MXU_DTYPE),
        "bs": jnp.repeat(jnp.transpose(gmlp_bs[l]), HEAD_DIM, axis=1),
    }


MIXER_NAMES = ("conv_a_w", "conv_r_w", "conv_r_b", "lru_wa", "lru_ba", "lru_wx", "lru_bx", "lru_lambda",
               "gmlp_norm_g", "gmlp_ws", "gmlp_bs")
SMALL_NAMES = ("norm_g",) + MIXER_NAMES + ("final_g",)


def _local_step(x, loss_target, norm_g, get_w_in, get_w_out, emit_early, emit_late, conv_a_w, conv_r_w, conv_r_b,
                lru_wa, lru_ba, lru_wx, lru_bx, lru_lambda, gmlp_norm_g, gmlp_ws, gmlp_bs, final_g):
    depth = norm_g.shape[0]
    D = x.shape[1]
    small = (conv_a_w, conv_r_w, conv_r_b, lru_wa, lru_ba, lru_wx, lru_bx, lru_lambda, gmlp_norm_g, gmlp_ws, gmlp_bs)
    saved = []
    for l in range(depth):
        mp = _mixer_params(l, *small)
        w_in_l = get_w_in(l, x)
        z, z_g, *qkv, y_abc, hs, lru_a, lru_mult = _inproj_mix_fwd(
            x, norm_g[l].reshape(1, D), w_in_l, mp, f"inproj_mix_fwd_{l}")
        attn =[_attn_fwd(qkv[p], dil, f"attn_fwd_d{dil}_{l}") for p, dil in enumerate(ATTN_DILATIONS)]
        w_out_l = get_w_out(l, y_abc)
        x_new, y, o, *lse = _outproj(x, z_g, y_abc, attn, w_out_l, f"outproj_{l}")
        saved.append((x, z, z_g, qkv, (hs, lru_a, lru_mult), y, o, lse, mp, w_in_l, w_out_l))
        x = x_new
    dx, loss, d_final_g = _loss_head(x, final_g.reshape(1, D), loss_target, "loss_head")
    token = None
    for l in reversed(range(depth)):
        x_l, z, z_g, qkv, lru, y, o, lse, mp, w_in_l, w_out_l = saved[l]
        if token is not None:
            mp = dict(mp, vec=mp["vec"] + token[0, 0])
        (dw_out, dz_abc, dz_g, do1, do4, do16, dl1, dl4, dl16, dwA, dwR, dvec, dwa, dwx, dws, dbs) = _outproj_mix_bwd(
            dx, y, w_out_l, z, z_g, *lru, o, mp, f"outproj_mix_bwd_{l}")
        token = emit_early(l, dw_out, [
            dwA[:conv_a_w.shape[1]], dwR[:conv_r_w.shape[1]], dvec[0], _diag_blocks(dwa), dvec[1], _diag_blocks(dwx),
            dvec[2], dvec[3], dvec[4], dws, jnp.transpose(dbs[:, :N_HEADS])])
        g_row = norm_g[l].reshape(1, D)
        if token is not None:
            g_row = g_row + token[0, 0]
        dqkv = [_attn_bwd(qkv[p], do, lse[p], dl, dil, f"attn_bwd_d{dil}_{l}")
                for p, (dil, do, dl) in enumerate(zip(ATTN_DILATIONS, (do1, do4, do16), (dl1, dl4, dl16)))]
        dx, dz, h, dng = _inproj_bwd(x_l, g_row, dx, dz_abc, dqkv, dz_g, w_in_l, f"inproj_bwd_{l}")
        dw_in = _inproj_wgrad(h, dz, f"inproj_wgrad_{l}")
        token = emit_late(l, dw_in, [dng[0]] + ([d_final_g[0]] if l == depth - 1 else []))
    return loss[0, 0], dx
WEIGHT_NAMES = ("norm_g", "w_in", "conv_a_w", "conv_r_w", "conv_r_b", "lru_wa", "lru_ba", "lru_wx", "lru_bx",
                "lru_lambda", "gmlp_norm_g", "gmlp_ws", "gmlp_bs", "w_out", "final_g")


def kernel(x, norm_g, w_in, conv_a_w, conv_r_w, conv_r_b, lru_wa, lru_ba, lru_wx, lru_bx, lru_lambda, gmlp_norm_g, gmlp_ws, gmlp_bs, w_out, final_g, loss_target, m_norm_g, m_w_in, m_conv_a_w, m_conv_r_w, m_conv_r_b, m_lru_wa, m_lru_ba, m_lru_wx, m_lru_bx, m_lru_lambda, m_gmlp_norm_g, m_gmlp_ws, m_gmlp_bs, m_w_out, m_final_g, v_norm_g, v_w_in, v_conv_a_w, v_conv_r_w, v_conv_r_b, v_lru_wa, v_lru_ba, v_lru_wx, v_lru_bx, v_lru_lambda, v_gmlp_norm_g, v_gmlp_ws, v_gmlp_bs, v_w_out, v_final_g):
    w = dict(norm_g=norm_g, w_in=w_in, conv_a_w=conv_a_w, conv_r_w=conv_r_w, conv_r_b=conv_r_b, lru_wa=lru_wa,
             lru_ba=lru_ba, lru_wx=lru_wx, lru_bx=lru_bx, lru_lambda=lru_lambda, gmlp_norm_g=gmlp_norm_g,
             gmlp_ws=gmlp_ws, gmlp_bs=gmlp_bs, w_out=w_out, final_g=final_g)
    m = dict(norm_g=m_norm_g, w_in=m_w_in, conv_a_w=m_conv_a_w, conv_r_w=m_conv_r_w, conv_r_b=m_conv_r_b,
             lru_wa=m_lru_wa, lru_ba=m_lru_ba, lru_wx=m_lru_wx, lru_bx=m_lru_bx, lru_lambda=m_lru_lambda,
             gmlp_norm_g=m_gmlp_norm_g, gmlp_ws=m_gmlp_ws, gmlp_bs=m_gmlp_bs, w_out=m_w_out, final_g=m_final_g)
    v = dict(norm_g=v_norm_g, w_in=v_w_in, conv_a_w=v_conv_a_w, conv_r_w=v_conv_r_w, conv_r_b=v_conv_r_b,
             lru_wa=v_lru_wa, lru_ba=v_lru_ba, lru_wx=v_lru_wx, lru_bx=v_lru_bx, lru_lambda=v_lru_lambda,
             gmlp_norm_g=v_gmlp_norm_g, gmlp_ws=v_gmlp_ws, gmlp_bs=v_gmlp_bs, w_out=v_w_out, final_g=v_final_g)
    depth, D, n_loc = w_in.shape
    e_loc = w_out.shape[1]
    cx, cy, cc = _my_place()
    me = 4 * cx + 2 * cy + cc

    transposed = lambda a: jnp.transpose(a, (0, 2, 1))
    w_in_t, m_w_in_t, v_w_in_t = transposed(w_in), transposed(m_w_in), transposed(v_w_in)
    w_in_w, w_out_w = w_in_t.astype(MXU_DTYPE), w_out.astype(MXU_DTYPE)
    c_loc = conv_a_w.shape[2]
    taps = (conv_a_w, conv_r_w)
    first, _ = _exchange_start([[(w_in_w[0], True), (_pack(taps), True)], [(w_out_w[0], True)]], "gather_start_first")
    full_in = lambda g: g.reshape(N_DEV * n_loc, D)
    full_out = lambda g: g.reshape(N_DEV * e_loc, D)

    g_in0, g_taps = _exchange_wait(first[0], x, "gather_wait_in_0")
    groups = [[(w_in_w[l], True), (w_out_w[l], True)] for l in range(1, depth)]
    gathers, rest_token = _exchange_start(groups, "gather_start_rest", after=g_taps)
    g_taps = g_taps.reshape(N_DEV, -1) + rest_token[0, 0]
    conv_full, off = [], 0
    for a in taps:
        part = g_taps[:, off:off + a.size].reshape((N_DEV,) + a.shape)
        conv_full.append(jnp.transpose(part, (1, 2, 0, 3)).reshape(a.shape[:2] + (N_DEV * c_loc,)))
        off += a.size
    conv_a_full, conv_r_full = conv_full
    later = {}

    def get_w_in(l, after):
        if l == 0:
            return full_in(g_in0)
        g_in, later[l] = _exchange_wait(gathers[l - 1], after, f"gather_wait_{l}")
        return full_in(g_in)

    def get_w_out(l, after):
        if l == 0:
            return full_out(_exchange_wait(first[1], after, "gather_wait_out_0")[0])
        return full_out(later[l])

    early, late, last_token, held = {}, {}, [None], {}

    def emit_early(l, dw_out, mixer_grads):
        group = [(dw_out.reshape(N_DEV, e_loc, D), False), (_pack(mixer_grads), True)]
        if l > 0:
            held[l] = (group, mixer_grads)
            return None
        handles, token = _exchange_start([group], f"early_start_{l}")
        early[l] = (handles[0], mixer_grads)
        return token

    def emit_late(l, dw_in, norm_grads):
        groups = [[(_pack(norm_grads), True)], [(dw_in, False)]] + ([held[l][0]] if l in held else [])
        handles, token = _exchange_start(groups, f"late_start_{l}")
        late[l] = (handles[0], handles[1], norm_grads)
        if l in held:
            early[l] = (handles[2], held[l][1])
        last_token[0] = token
        return token

    loss, grad_x = _local_step(
        x[0], loss_target[0], norm_g, get_w_in, get_w_out, emit_early, emit_late, conv_a_full, conv_r_full, conv_r_b,
        lru_wa, lru_ba, lru_wx, lru_bx, lru_lambda, gmlp_norm_g, gmlp_ws, gmlp_bs, final_g)
    loss = lax.psum(loss, ("x", "y", "c"))

    r_in, r_out, small_parts = {}, {}, []
    for l in reversed(range(depth)):
        r_out[l], r_mix = _exchange_wait(early[l][0], last_token[0], f"early_wait_{l}")
        (r_norm,) = _exchange_wait(late[l][0], last_token[0], f"late_wait_norm_{l}")
        small_parts += [r_mix, r_norm]
        if l > 0:
            (r_in[l],) = _exchange_wait(late[l][1], last_token[0], f"late_wait_{l}")
    big = {"w_out": _adamw_summed([r_out[l] for l in range(depth)], w_out, m_w_out, v_w_out, e_loc, "adamw_w_out")}

    sums = _sum_slots(small_parts, "sum_small_grads")
    by_layer = {}
    for i, l in enumerate(reversed(range(depth))):
        mix = _unpack(sums[2 * i], early[l][1])
        nrm = _unpack(sums[2 * i + 1], late[l][2])
        by_layer[l] = dict(zip(MIXER_NAMES, mix), norm_g=nrm[0])
        if l == depth - 1:
            g_final = nrm[1]
    g_small = {k: jnp.stack([by_layer[l][k] for l in range(depth)]) for k in ("norm_g",) + MIXER_NAMES}
    g_small["final_g"] = g_final
    for k in ("conv_a_w", "conv_r_w"):
        g_small[k] = lax.dynamic_slice_in_dim(g_small[k], me * c_loc, c_loc, axis=2)
    packs = [_pack([d[k] for k in SMALL_NAMES]) for d in (w, g_small, m, v)]
    res = _adamw_small(*packs, "adamw_small")
    like = [w[k] for k in SMALL_NAMES]
    d_s, m_s, v_s = (dict(zip(SMALL_NAMES, _unpack(r, like))) for r in res)

    (r_in[0],) = _exchange_wait(late[0][1], res[0], "late_wait_0")
    big["w_in"] = [transposed(a) for a in _adamw_summed(
        [r_in[l] for l in range(depth)], w_in_t, m_w_in_t, v_w_in_t, n_loc // 2, "adamw_w_in")]

    grad, delta, new_m, new_v = {}, {}, {}, {}
    for k in WEIGHT_NAMES:
        if k in big:
            grad[k], delta[k], new_m[k], new_v[k] = big[k]
        else:
            grad[k], delta[k], new_m[k], new_v[k] = g_small[k], d_s[k], m_s[k], v_s[k]
    return (loss, grad_x[None], *[grad[k] for k in WEIGHT_NAMES], *[delta[k] for k in WEIGHT_NAMES],
            *[new_m[k] for k in WEIGHT_NAMES], *[new_v[k] for k in WEIGHT_NAMES])
```

```python
import math

import jax
import jax.numpy as jnp
from jax import lax
from jax.experimental import pallas as pl
from jax.experimental.pallas import tpu as pltpu

F32 = jnp.float32
MXU_DTYPE = jnp.bfloat16
WIRE_DTYPE = jnp.bfloat16
MESH = pl.DeviceIdType.MESH

N_DEV = 8
GROUP_W = 256
N_HEADS = 4
HEAD_DIM = 64
N_ABC = 9
GMLP_CHUNK = 128
ATTN_BLOCK = 128
ATTN_FWD_BLOCKS_PER_STEP = 8
ATTN_BWD_BLOCKS_PER_STEP = 8
ATTN_DILATIONS = (1, 4, 16)
NORM_EPS = 1e-6
RG_C = 8.0
SUBLANES = 8
LANES = 128
VMEM_LIMIT = 56 * 1024 * 1024

ADAM_LR = 0.001
ADAM_B1 = 0.9
ADAM_B2 = 0.999
ADAM_EPS = 1e-08
ADAM_WD = 0.01
ADAM_STEP = 10

TM_MIX = 512
TM_MM = 512
TM_WGRAD = 1024


def _params(sem, vmem=VMEM_LIMIT):
    return pltpu.CompilerParams(dimension_semantics=sem, vmem_limit_bytes=vmem)


def _mm_tn(a, b):
    return lax.dot_general(a.astype(MXU_DTYPE), b.astype(MXU_DTYPE), (((0,), (0,)), ((), ())),
                           preferred_element_type=F32)


def _mm_nt(a, b):
    return lax.dot_general(a.astype(MXU_DTYPE), b.astype(MXU_DTYPE), (((1,), (1,)), ((), ())),
                           preferred_element_type=F32)


def _sigmoid(x):
    return 0.5 * jnp.tanh(0.5 * x) + 0.5


def _sigmoid_small_exact(x):
    return 1.0 / (1.0 + jnp.exp(-x))


def _silu_and_grad(x):
    s = _sigmoid(x)
    return x * s, s * (1.0 + x * (1.0 - s))


_GELU_K = math.sqrt(2.0 / math.pi)
_GELU_C = 0.044715


def _gelu_and_grad(x):
    x2 = x * x
    t = jnp.tanh(_GELU_K * (x + _GELU_C * x * x2))
    val = 0.5 * x * (1.0 + t)
    grad = 0.5 * (1.0 + t) + 0.5 * x * (1.0 - t * t) * (_GELU_K * (1.0 + 3.0 * _GELU_C * x2))
    return val, grad


def _gelu(x):
    return 0.5 * x * (1.0 + jnp.tanh(_GELU_K * (x + _GELU_C * x * x * x)))


def _expm1_nonpos(u):
    poly = 1.0 / math.factorial(9)
    for k in range(8, 0, -1):
        poly = poly * u + 1.0 / math.factorial(k)
    return jnp.where(u > -0.25, poly * u, jnp.exp(u) - 1.0)


def _softplus(x):
    return jnp.maximum(x, 0.0) + jnp.log(1.0 + jnp.exp(-jnp.abs(x)))


def _shift_down(t, halo, k):
    rolled = pltpu.roll(t, k, 0)
    hr = pltpu.roll(halo, k, 0)
    row = lax.broadcasted_iota(jnp.int32, halo.shape, 0)
    first = jnp.where(row < k, hr, rolled[0:SUBLANES])
    return jnp.concatenate([first, rolled[SUBLANES:]], axis=0)


def _shift_up(t, nxt, k):
    tm = t.shape[0]
    rolled = pltpu.roll(t, tm - k, 0)
    nr = pltpu.roll(nxt, SUBLANES - k, 0)
    row = lax.broadcasted_iota(jnp.int32, nxt.shape, 0)
    last = jnp.where(row >= SUBLANES - k, nr, rolled[tm - SUBLANES:tm])
    return jnp.concatenate([rolled[:tm - SUBLANES], last], axis=0)


def _scan_fwd(a, b):
    tm = a.shape[0]
    row = lax.broadcasted_iota(jnp.int32, a.shape, 0)
    s = 1
    while s < tm:
        a_s = pltpu.roll(a, s, 0)
        b_s = pltpu.roll(b, s, 0)
        m = row >= s
        b = jnp.where(m, a * b_s + b, b)
        a = jnp.where(m, a * a_s, a)
        s *= 2
    return a, b


def _scan_rev(a, g):
    tm = a.shape[0]
    row = lax.broadcasted_iota(jnp.int32, a.shape, 0)
    s = 1
    while s < tm:
        a_s = pltpu.roll(a, tm - s, 0)
        g_s = pltpu.roll(g, tm - s, 0)
        m = row < tm - s
        g = jnp.where(m, g + a * g_s, g)
        a = jnp.where(m, a * a_s, a)
        s *= 2
    return g


def _group_rows(scr_ref, row, n_groups):
    return jnp.concatenate([scr_ref[pl.ds(c, 1), pl.ds(row, n_groups, stride=SUBLANES), :][0]
                            for c in range(scr_ref.shape[0])], axis=1)


def _spread_rows(rows_ref, n_groups, w):
    return jnp.concatenate([jnp.broadcast_to(rows_ref[g:g + 1, :], (SUBLANES, w)) for g in range(n_groups)], axis=0)


def _scan_groups(a, b, reverse):
    tm, w = a.shape
    shape3 = (tm // SUBLANES, SUBLANES, w)
    a3, b3 = a.reshape(shape3), b.reshape(shape3)
    sub = lax.broadcasted_iota(jnp.int32, shape3, 1)
    s = 1
    while s < SUBLANES:
        shift = SUBLANES - s if reverse else s
        a_s = pltpu.roll(a3, shift, 1)
        b_s = pltpu.roll(b3, shift, 1)
        m = (sub < SUBLANES - s) if reverse else (sub >= s)
        b3 = jnp.where(m, a3 * b_s + b3, b3)
        a3 = jnp.where(m, a3 * a_s, a3)
        s *= 2
    return a3.reshape(tm, w), b3.reshape(tm, w)


def _scan_fwd_tile(a, b, h_in, sa_ref, sb_ref, sc_ref):
    tm, w = a.shape
    n_groups = tm // SUBLANES
    a_loc, b_loc = _scan_groups(a, b, False)
    _put(sa_ref, a_loc)
    _put(sb_ref, b_loc)
    a_end, b_end = _scan_fwd(_group_rows(sa_ref, SUBLANES - 1, n_groups), _group_rows(sb_ref, SUBLANES - 1, n_groups))
    h_end = b_end + a_end * h_in
    sc_ref[...] = _shift_down(h_end, jnp.broadcast_to(h_in, (SUBLANES, w)), 1)
    return b_loc + a_loc * _spread_rows(sc_ref, n_groups, w), h_end


def _scan_rev_tile(a, g, sa_ref, sb_ref, sc_ref):
    tm, w = a.shape
    n_groups = tm // SUBLANES
    a_loc, g_loc = _scan_groups(a, g, True)
    _put(sa_ref, a_loc)
    _put(sb_ref, g_loc)
    d_first = _scan_rev(_group_rows(sa_ref, 0, n_groups), _group_rows(sb_ref, 0, n_groups))
    sc_ref[...] = _shift_up(d_first, jnp.zeros((SUBLANES, w), F32), 1)
    return g_loc + a_loc * _spread_rows(sc_ref, n_groups, w)


def _lane_scratch(tm, w):
    return pltpu.VMEM((w // LANES, tm, LANES), F32)


def _put(scr_ref, val):
    for c in range(scr_ref.shape[0]):
        scr_ref[c] = val[:, c * LANES:(c + 1) * LANES].astype(F32)


def _get(scr_ref):
    return jnp.concatenate([scr_ref[c] for c in range(scr_ref.shape[0])], axis=1)


MAX_ROW_STRIDE = 4


def _strided_rows(c, start, n, stride):
    return (pl.ds(c, 1), pl.ds(start, n, stride=stride), slice(None))


def _deinterleave(src_ref, dst_ref, dil, tmp_ref=None):
    nc, tm, _ = src_ref.shape
    s1 = min(dil, MAX_ROW_STRIDE)
    s2 = dil // s1
    if s2 > 1:
        for r0 in range(s1):
            for c in range(nc):
                tmp_ref[c, r0 * (tm // s1):(r0 + 1) * (tm // s1), :] = src_ref[_strided_rows(c, r0, tm // s1, s1)][0]
    for r in range(dil):
        r1, r0 = divmod(r, s1)
        for c in range(nc):
            if dil == 1:
                piece = src_ref[c]
            elif s2 == 1:
                piece = src_ref[_strided_rows(c, r, tm // dil, dil)][0]
            else:
                piece = tmp_ref[_strided_rows(c, r0 * (tm // s1) + r1, tm // dil, s2)][0]
            dst_ref[r, :, c * LANES:(c + 1) * LANES] = piece.astype(dst_ref.dtype)


def _interleave(src_ref, dst_ref, dil, tmp_ref=None):
    nc, tm, _ = dst_ref.shape
    s1 = min(dil, MAX_ROW_STRIDE)
    s2 = dil // s1
    for r in range(dil):
        r1, r0 = divmod(r, s1)
        for c in range(nc):
            piece = src_ref[r, :, c * LANES:(c + 1) * LANES].astype(F32)[None]
            if s2 == 1:
                dst_ref[_strided_rows(c, r, tm // dil, dil)] = piece
            else:
                tmp_ref[_strided_rows(c, r0 * (tm // s1) + r1, tm // dil, s2)] = piece
    if s2 > 1:
        for r0 in range(s1):
            for c in range(nc):
                dst_ref[_strided_rows(c, r0, tm // s1, s1)] = (
                    tmp_ref[c, r0 * (tm // s1):(r0 + 1) * (tm // s1), :][None])


def _dilated_spec(tm, w, dil, index=lambda i: i):
    return pl.BlockSpec((dil, tm // dil, w), lambda i: (0, index(i), 0))


def _dilated_shape(S, w, dil, dtype):
    return jax.ShapeDtypeStruct((dil, S // dil, w), dtype)


def _head_masks(shape):
    lane = lax.broadcasted_iota(jnp.int32, shape, 1)
    return [(lane >= h * HEAD_DIM) & (lane < (h + 1) * HEAD_DIM) for h in range(N_HEADS)]


def _colsum(v):
    return jnp.sum(v, axis=0, keepdims=True)


def _conv_a(z_of, halo_of, w_ref):
    p = z_of(2) * z_of(0)
    p_h = halo_of(2) * halo_of(0)
    cv = w_ref[2:3, :] * p + w_ref[1:2, :] * _shift_down(p, p_h, 1) + w_ref[0:1, :] * _shift_down(p, p_h, 2)
    return p, p_h, cv


def _lru_gates(z_of, halo_of, wr_ref, vec_ref, wa_ref, wx_ref, saved=None):
    rx = z_of(4)
    rx_h = halo_of(4)
    sh = [rx, _shift_down(rx, rx_h, 1), _shift_down(rx, rx_h, 2), _shift_down(rx, rx_h, 3)]
    xc = (wr_ref[3:4, :] * sh[0] + wr_ref[2:3, :] * sh[1] + wr_ref[1:2, :] * sh[2]
          + wr_ref[0:1, :] * sh[3] + vec_ref[0:1, :])
    ga = _sigmoid_small_exact(jnp.dot(xc.astype(MXU_DTYPE), wa_ref[...], preferred_element_type=F32) + vec_ref[1:2, :])
    gi = _sigmoid(jnp.dot(xc.astype(MXU_DTYPE), wx_ref[...], preferred_element_type=F32) + vec_ref[2:3, :])
    sp = _softplus(-vec_ref[3:4, :])
    if saved is not None:
        return (xc, sh, ga, gi) + tuple(saved) + (sp,)
    log_a = (-RG_C * ga) * sp
    a = jnp.exp(log_a)
    mult = jnp.sqrt(-_expm1_nonpos(2.0 * log_a))
    return xc, sh, ga, gi, a, mult, sp


def _gmlp_fwd(z_of, vec_ref, ws_ref, bs_ref, tm):
    u = _gelu(z_of(6))
    gv = _gelu(z_of(7))
    rr = lax.rsqrt(jnp.mean(gv * gv, axis=-1, keepdims=True) + NORM_EPS)
    vn = (gv * rr) * vec_ref[4:5, :]
    masks = _head_masks((GMLP_CHUNK, GROUP_W))
    parts = []
    for c in range(tm // GMLP_CHUNK):
        vc = vn[c * GMLP_CHUNK:(c + 1) * GMLP_CHUNK].astype(MXU_DTYPE)
        acc = bs_ref[...]
        for h in range(N_HEADS):
            acc = acc + jnp.where(masks[h], jnp.dot(ws_ref[h], vc, preferred_element_type=F32), 0.0)
        parts.append(acc)
    return u, gv, rr, vn, jnp.concatenate(parts, axis=0)


def _mix_specs():
    const2 = lambda shape: pl.BlockSpec(shape, lambda i: (0, 0))
    return [const2((SUBLANES, GROUP_W)), const2((SUBLANES, GROUP_W)), const2((SUBLANES, GROUP_W)),
            const2((GROUP_W, GROUP_W)), const2((GROUP_W, GROUP_W)),
            pl.BlockSpec((N_HEADS, GMLP_CHUNK, GMLP_CHUNK), lambda i: (0, 0, 0)),
            const2((GMLP_CHUNK, GROUP_W))]


def _inproj_mix_fwd(x, g, w_t, mp, name):
    S, D = x.shape
    N = w_t.shape[0]
    tm = TM_MIX
    hb = tm // SUBLANES
    n_abc = N_ABC * GROUP_W
    n_qkv = 3 * GROUP_W

    def body(x_ref, g_ref, w_ref, wA_ref, wR_ref, vec_ref, wa_ref, wx_ref, ws_ref, bs_ref,
             z_ref, zg_ref, q1_ref, q4_ref, q16_ref, y_ref, h_ref, a_ref, mult_ref,
             qkv_ref, halo_ref, carry_ref, sa_ref, sb_ref, sc_ref, tmp_ref):
        @pl.when(pl.program_id(0) == 0)
        def _():
            halo_ref[...] = jnp.zeros_like(halo_ref)
            carry_ref[...] = jnp.zeros_like(carry_ref)

        xv = x_ref[...]
        r = lax.rsqrt(jnp.mean(xv * xv, axis=-1, keepdims=True) + NORM_EPS)
        hn = ((xv * r) * g_ref[...]).astype(MXU_DTYPE)
        z_ref[...] = _mm_nt(hn, w_ref[0:n_abc, :])
        _put(qkv_ref, _mm_nt(hn, w_ref[n_abc:n_abc + n_qkv, :]))
        zg_ref[...] = _mm_nt(hn, w_ref[n_abc + n_qkv:, :])
        for dil, ref in zip(ATTN_DILATIONS, (q1_ref, q4_ref, q16_ref)):
            _deinterleave(qkv_ref, ref, dil, tmp_ref)

        z_of = lambda c: z_ref[:, c * GROUP_W:(c + 1) * GROUP_W]
        halo_of = lambda c: halo_ref[:, c * GROUP_W:(c + 1) * GROUP_W]

        _, _, cv = _conv_a(z_of, halo_of, wA_ref)
        y_ref[:, 0:GROUP_W] = (z_of(1) * cv * _silu_and_grad(z_of(3))[0]).astype(y_ref.dtype)

        xc, _, _, gi, a, mult, _ = _lru_gates(z_of, halo_of, wR_ref, vec_ref, wa_ref, wx_ref)
        a_ref[...] = a
        mult_ref[...] = mult
        b = mult * (gi * xc)
        h, h_end = _scan_fwd_tile(a, b, carry_ref[SUBLANES - 1:SUBLANES, :], sa_ref, sb_ref, sc_ref)
        h_ref[...] = h
        carry_ref[...] = h_end[hb - SUBLANES:hb]
        y_ref[:, GROUP_W:2 * GROUP_W] = (h * _silu_and_grad(z_of(5))[0]).astype(y_ref.dtype)

        u, _, _, _, sp = _gmlp_fwd(z_of, vec_ref, ws_ref, bs_ref, tm)
        y_ref[:, 2 * GROUP_W:3 * GROUP_W] = (u * sp * _silu_and_grad(z_of(8))[0]).astype(y_ref.dtype)
        halo_ref[...] = z_ref[tm - SUBLANES:tm, :]

    row = lambda wd: pl.BlockSpec((tm, wd), lambda i: (i, 0))
    return pl.pallas_call(
        body, name=name, grid=(S // tm,),
        in_specs=[row(D), pl.BlockSpec((1, D), lambda i: (0, 0)),
                  pl.BlockSpec((N, D), lambda i: (0, 0), pipeline_mode=pl.Buffered(1))] + _mix_specs(),
        out_specs=[row(n_abc), row(GROUP_W)] + [_dilated_spec(tm, n_qkv, dil) for dil in ATTN_DILATIONS]
                  + [row(3 * GROUP_W)] + [row(GROUP_W)] * 3,
        out_shape=[jax.ShapeDtypeStruct((S, n_abc), F32), jax.ShapeDtypeStruct((S, GROUP_W), F32)]
                  + [_dilated_shape(S, n_qkv, dil, MXU_DTYPE) for dil in ATTN_DILATIONS]
                  + [jax.ShapeDtypeStruct((S, 3 * GROUP_W), MXU_DTYPE)] + [jax.ShapeDtypeStruct((S, GROUP_W), F32)] * 3,
        scratch_shapes=[_lane_scratch(tm, n_qkv), pltpu.VMEM((SUBLANES, n_abc), F32),
                        pltpu.VMEM((SUBLANES, GROUP_W), F32), _lane_scratch(tm, GROUP_W), _lane_scratch(tm, GROUP_W),
                        pltpu.VMEM((hb, GROUP_W), F32), _lane_scratch(tm, n_qkv)],
        compiler_params=_params(("arbitrary",)),
    )(x, g, w_t, mp["wA"], mp["wR"], mp["vec"], mp["wa"], mp["wx"], mp["ws"], mp["bs"])


_NEG = -1e30


def _slope(h):
    return 2.0 ** (-8.0 * (h + 1) / N_HEADS)


def _attn_bias(dil, offsets, n_keys):
    shape = (ATTN_BLOCK, n_keys)
    qi = lax.broadcasted_iota(jnp.int32, shape, 0)
    ki = lax.broadcasted_iota(jnp.int32, shape, 1)
    blocks = []
    for f in offsets:
        delta = qi + f - ki
        valid = (delta >= 0) & (delta <= ATTN_BLOCK)
        dist = (delta * dil).astype(F32)
        for h in range(N_HEADS):
            blocks.append(jnp.where(valid, -_slope(h) * dist, _NEG))
    return jnp.concatenate(blocks, axis=0)


def _stack_heads(t, masks):
    return jnp.concatenate([jnp.where(m, t, jnp.zeros_like(t)) for m in masks], axis=0)


def _unstack_heads(t4, masks, base=0):
    out = t4[base * ATTN_BLOCK:(base + 1) * ATTN_BLOCK]
    for h in range(1, N_HEADS):
        out = jnp.where(masks[h], t4[(base + h) * ATTN_BLOCK:(base + h + 1) * ATTN_BLOCK], out)
    return out


def _group_starts(n, per_step, group):
    if per_step % group == 0:
        return (lambda j: j % group == 0), True
    steps = group // per_step
    return (lambda j: (n % steps == 0) if j == 0 else False), (n + 1) % steps == 0


def _attn_fwd(qkv, dil, name):
    S = qkv.shape[0] * qkv.shape[1]
    qkv = qkv.reshape(S, qkv.shape[2])
    nb = S // ATTN_BLOCK
    group = nb // dil
    scale = 1.0 / math.sqrt(HEAD_DIM)
    B = ATTN_BLOCK
    per_step = ATTN_FWD_BLOCKS_PER_STEP

    def body(q_ref, kc_ref, kp_ref, vc_ref, vp_ref, o_ref, l_ref, bias_ref):
        n = pl.program_id(0)

        @pl.when(n == 0)
        def _():
            bias_ref[...] = _attn_bias(dil, (B,), 2 * B)

        masks = _head_masks((B, GROUP_W))
        starts, _ = _group_starts(n, per_step, group)
        for j in range(per_step):
            own = slice(j * B, (j + 1) * B)
            before = slice((j - 1) * B, j * B)
            qs = _stack_heads(q_ref[own], masks)
            keys = jnp.concatenate([kp_ref[...] if j == 0 else kc_ref[before], kc_ref[own]], axis=0)
            vals = jnp.concatenate([vp_ref[...] if j == 0 else vc_ref[before], vc_ref[own]], axis=0)
            s = _mm_nt(qs, keys) * scale + bias_ref[...]
            if starts(j) is not False:
                key_col = lax.broadcasted_iota(jnp.int32, s.shape, 1)
                s = jnp.where(starts(j) & (key_col < B), _NEG, s)
            m = jnp.max(s, axis=-1, keepdims=True)
            p = jnp.exp(s - m)
            l = jnp.sum(p, axis=-1, keepdims=True)
            o4 = jnp.dot(p.astype(MXU_DTYPE), vals, preferred_element_type=F32)
            o_ref[own] = (_unstack_heads(o4, masks)
                          / _unstack_heads(jnp.broadcast_to(l, o4.shape), masks)).astype(o_ref.dtype)
            l_ref[own] = _unstack_heads(jnp.broadcast_to(m + jnp.log(l), o4.shape), masks)

    blk = (per_step * B, GROUP_W)
    cur = lambda c: pl.BlockSpec(blk, lambda n: (n, c))
    prev = lambda c: pl.BlockSpec((B, GROUP_W), lambda n: (jnp.maximum(n * per_step - 1, 0), c))
    out = pl.BlockSpec(blk, lambda n: (n, 0))
    o, l = pl.pallas_call(
        body, name=name, grid=(nb // per_step,),
        in_specs=[cur(0), cur(1), prev(1), cur(2), prev(2)],
        out_specs=[out, out],
        out_shape=[jax.ShapeDtypeStruct((S, GROUP_W), MXU_DTYPE), jax.ShapeDtypeStruct((S, GROUP_W), F32)],
        scratch_shapes=[pltpu.VMEM((N_HEADS * ATTN_BLOCK, 2 * ATTN_BLOCK), F32)],
        compiler_params=_params(("arbitrary",)),
    )(qkv, qkv, qkv, qkv, qkv)
    return o.reshape(dil, S // dil, GROUP_W), l.reshape(dil, S // dil, GROUP_W)


def _outproj(x, z_g, y_abc, attn, w_out, name, head=None):
    S, D = x.shape
    tm = TM_MM
    n_abc = 3 * GROUP_W

    def body(x_ref, g_ref, yabc_ref, o1, l1, o2, l2, o3, l3, w_ref, *rest):
        if head is not None:
            fg_ref, t_ref, *rest = rest
        xn_ref, y_ref, o_ref, lse1_ref, lse4_ref, lse16_ref, *rest = rest
        if head is not None:
            loss_ref, dfg_ref, *rest = rest
        so2, sl2, so3, sl3, slse, tmp_ref = rest
        for src, dst, dil in ((o2, so2, ATTN_DILATIONS[1]), (l2, sl2, ATTN_DILATIONS[1]),
                              (o3, so3, ATTN_DILATIONS[2]), (l3, sl3, ATTN_DILATIONS[2])):
            _interleave(src, dst, dil, tmp_ref)
        la, lb, lc = l1[0], _get(sl2), _get(sl3)
        mx = jnp.maximum(jnp.maximum(la, lb), lc)
        ea, eb, ec = jnp.exp(la - mx), jnp.exp(lb - mx), jnp.exp(lc - mx)
        den = ea + eb + ec
        o = (ea * o1[0].astype(F32) + eb * _get(so2) + ec * _get(so3)) / den
        o_ref[...] = o
        _put(slse, mx + jnp.log(den))
        for dil, ref in zip(ATTN_DILATIONS, (lse1_ref, lse4_ref, lse16_ref)):
            _deinterleave(slse, ref, dil, tmp_ref)
        y_d = o * _silu_and_grad(g_ref[...])[0]
        y_ref[:, 0:n_abc] = yabc_ref[...].astype(MXU_DTYPE)
        y_ref[:, n_abc:] = y_d.astype(MXU_DTYPE)
        xn = x_ref[...] + jnp.dot(y_ref[...], w_ref[...], preferred_element_type=F32)
        if head is None:
            xn_ref[...] = xn
            return

        @pl.when(pl.program_id(0) == 0)
        def _():
            loss_ref[...] = jnp.zeros_like(loss_ref)
            dfg_ref[...] = jnp.zeros_like(dfg_ref)

        r = lax.rsqrt(jnp.mean(xn * xn, axis=-1, keepdims=True) + NORM_EPS)
        xh = xn * r
        err = xh * fg_ref[...] - t_ref[...]
        loss_ref[...] += 0.5 * jnp.sum(jnp.mean(err * err, axis=-1, keepdims=True), axis=0, keepdims=True)
        dout = err * (1.0 / D)
        dfg_ref[...] += _colsum(dout * xh)
        dxh = dout * fg_ref[...]
        xn_ref[...] = r * (dxh - xh * jnp.mean(dxh * xh, axis=-1, keepdims=True))

    row = lambda w: pl.BlockSpec((tm, w), lambda i: (i, 0))
    const = lambda w: pl.BlockSpec((1, w), lambda i: (0, 0))
    dil_specs = [_dilated_spec(tm, GROUP_W, dil) for dil in ATTN_DILATIONS]
    (o1, l1), (o2, l2), (o3, l3) = attn
    with_head = head is not None
    return pl.pallas_call(
        body, name=name, grid=(S // tm,),
        in_specs=[row(D), row(GROUP_W), row(n_abc)] + [sp for sp in dil_specs for _ in range(2)]
                 + [pl.BlockSpec(w_out.shape, lambda i: (0, 0))] + ([const(D), row(D)] if with_head else []),
        out_specs=[row(D), row(4 * GROUP_W), row(GROUP_W)] + dil_specs + ([const(LANES), const(D)] if with_head else []),
        out_shape=[jax.ShapeDtypeStruct((S, D), F32), jax.ShapeDtypeStruct((S, 4 * GROUP_W), MXU_DTYPE),
                   jax.ShapeDtypeStruct((S, GROUP_W), F32)]
                  + [_dilated_shape(S, GROUP_W, dil, F32) for dil in ATTN_DILATIONS]
                  + ([jax.ShapeDtypeStruct((1, LANES), F32), jax.ShapeDtypeStruct((1, D), F32)] if with_head else []),
        scratch_shapes=[_lane_scratch(tm, GROUP_W)] * 6,
        compiler_params=_params(("arbitrary" if with_head else "parallel",)),
    )(x, z_g, y_abc, o1, l1, o2, l2, o3, l3, w_out, *(head or ()))


def _loss_head(x, g, target, name):
    S, D = x.shape
    tm = TM_MM

    def body(x_ref, g_ref, t_ref, dx_ref, loss_ref, dg_ref):
        i = pl.program_id(0)

        @pl.when(i == 0)
        def _():
            loss_ref[...] = jnp.zeros_like(loss_ref)
            dg_ref[...] = jnp.zeros_like(dg_ref)

        xv = x_ref[...]
        r = lax.rsqrt(jnp.mean(xv * xv, axis=-1, keepdims=True) + NORM_EPS)
        xn = xv * r
        err = xn * g_ref[...] - t_ref[...]
        per_tok = jnp.mean(err * err, axis=-1, keepdims=True)
        loss_ref[...] += 0.5 * jnp.sum(per_tok, axis=0, keepdims=True)
        dout = err * (1.0 / D)
        dg_ref[...] += _colsum(dout * xn)
        dxn = dout * g_ref[...]
        dx_ref[...] = r * (dxn - xn * jnp.mean(dxn * xn, axis=-1, keepdims=True))

    row = pl.BlockSpec((tm, D), lambda i: (i, 0))
    return pl.pallas_call(
        body, name=name, grid=(S // tm,),
        in_specs=[row, pl.BlockSpec((1, D), lambda i: (0, 0)), row],
        out_specs=[row, pl.BlockSpec((1, LANES), lambda i: (0, 0)), pl.BlockSpec((1, D), lambda i: (0, 0))],
        out_shape=[jax.ShapeDtypeStruct((S, D), F32), jax.ShapeDtypeStruct((1, LANES), F32),
                   jax.ShapeDtypeStruct((1, D), F32)],
        compiler_params=_params(("arbitrary",)),
    )(x, g, target)


def _outproj_mix_bwd(dx, y, w_out, z, z_g, hs, lru_a, lru_mult, o, mp, name):
    S, D = dx.shape
    E = y.shape[1]
    tm = TM_MIX
    hb = tm // SUBLANES
    nT = S // tm
    last_blk = S // SUBLANES - 1
    wcols = N_ABC * GROUP_W

    def body(dx_ref, y_ref, w_ref, z_ref, zh_ref, zn_ref, zg_ref, h_ref, hh_ref, a_ref, mult_ref, o_ref,
             wA_ref, wR_ref, vec_ref, wa_ref, wx_ref, ws_ref, bs_ref,
             dw_ref, dz_ref, dzg_ref, do1_ref, do4_ref, do16_ref, dl1_ref, dl4_ref, dl16_ref,
             dwA_ref, dwR_ref, dvec_ref, dwa_ref, dwx_ref, dws_ref, dbs_ref,
             hcarry_ref, xcarry_ref, bsacc_ref, do_ref, dl_ref, sa_ref, sb_ref, sc_ref, dy_ref, dyn_ref, acc_ref,
             tmp_ref):
        i = pl.program_id(0)
        ti = nT - 1 - i

        @pl.when(i == 0)
        def _():
            acc_ref[...] = jnp.zeros_like(acc_ref)
            dyn_ref[...] = jnp.zeros_like(dyn_ref)
            hcarry_ref[...] = jnp.zeros_like(hcarry_ref)
            xcarry_ref[...] = jnp.zeros_like(xcarry_ref)
            bsacc_ref[...] = jnp.zeros_like(bsacc_ref)
            dwA_ref[...] = jnp.zeros_like(dwA_ref)
            dwR_ref[...] = jnp.zeros_like(dwR_ref)
            dvec_ref[...] = jnp.zeros_like(dvec_ref)
            dwa_ref[...] = jnp.zeros_like(dwa_ref)
            dwx_ref[...] = jnp.zeros_like(dwx_ref)
            dws_ref[...] = jnp.zeros_like(dws_ref)
            dbs_ref[...] = jnp.zeros_like(dbs_ref)

        dxb = dx_ref[...].astype(MXU_DTYPE)
        dy_ref[...] = _mm_nt(dxb, w_ref[...])
        acc_ref[...] += _mm_tn(y_ref[...], dxb)

        @pl.when(i == nT - 1)
        def _():
            dw_ref[...] = acc_ref[...].astype(dw_ref.dtype)

        has_prev = ti > 0
        has_next = i > 0
        col = lambda c: slice(c * GROUP_W, (c + 1) * GROUP_W)
        z_of = lambda c: z_ref[:, col(c)]
        halo_of = lambda c: jnp.where(has_prev, zh_ref[:, col(c)], 0.0)
        next_of = lambda c: zn_ref[:, col(c)]

        p, p_h, cv = _conv_a(z_of, halo_of, wA_ref)
        sg, dsg = _silu_and_grad(z_of(3))
        a_b = z_of(1)
        dya = dy_ref[:, col(0)]
        dcv = dya * a_b * sg
        dcv_n = jnp.where(has_next, dyn_ref[...] * next_of(1) * _silu_and_grad(next_of(3))[0], 0.0)
        dp = (wA_ref[2:3, :] * dcv + wA_ref[1:2, :] * _shift_up(dcv, dcv_n, 1)
              + wA_ref[0:1, :] * _shift_up(dcv, dcv_n, 2))
        dwA_ref[2:3, :] += _colsum(dcv * p)
        dwA_ref[1:2, :] += _colsum(dcv * _shift_down(p, p_h, 1))
        dwA_ref[0:1, :] += _colsum(dcv * _shift_down(p, p_h, 2))
        def put_dz(c, val):
            dz_ref[:, col(c)] = val.astype(dz_ref.dtype)

        put_dz(0, dp * z_of(2))
        put_dz(1, dya * cv * sg)
        put_dz(2, dp * z_of(0))
        put_dz(3, dya * a_b * cv * dsg)

        xc, sh, ga, gi, a, mult, sp = _lru_gates(z_of, halo_of, wR_ref, vec_ref, wa_ref, wx_ref,
                                                 saved=(a_ref[...], mult_ref[...]))
        h = h_ref[...]
        h_prev = _shift_down(h, jnp.where(has_prev, hh_ref[...], 0.0), 1)
        sgr, dsgr = _silu_and_grad(z_of(5))
        dyb = dy_ref[:, col(1)]
        put_dz(5, dyb * h * dsgr)
        row = lax.broadcasted_iota(jnp.int32, (tm, GROUP_W), 0)
        g_in = dyb * sgr + jnp.where(row == tm - 1, hcarry_ref[0:1, :], 0.0)
        a_up = _shift_up(a, jnp.zeros((SUBLANES, GROUP_W), F32), 1)
        dH = _scan_rev_tile(a_up, g_in, sa_ref, sb_ref, sc_ref)
        hcarry_ref[...] = (a * dH)[0:SUBLANES]
        da = dH * h_prev
        gx = gi * xc
        dmult = dH * gx
        dgi = dH * mult * xc
        dxc = dH * mult * gi
        dlog_a = da * a - dmult * (a * a) / mult
        dga = dlog_a * (-RG_C * sp)
        dlam_row = _colsum(dlog_a * (-RG_C * ga)) * (-_sigmoid(-vec_ref[3:4, :]))
        dpre_a = dga * ga * (1.0 - ga)
        dpre_i = dgi * gi * (1.0 - gi)
        dwa_ref[...] += _mm_tn(xc, dpre_a)
        dwx_ref[...] += _mm_tn(xc, dpre_i)
        dxc = dxc + _mm_nt(dpre_a, wa_ref[...]) + _mm_nt(dpre_i, wx_ref[...])
        dvec_ref[0:1, :] += _colsum(dxc)
        dvec_ref[1:2, :] += _colsum(dpre_a)
        dvec_ref[2:3, :] += _colsum(dpre_i)
        dvec_ref[3:4, :] += dlam_row
        for k in range(4):
            dwR_ref[k:k + 1, :] += _colsum(dxc * sh[3 - k])
        dxc_n = xcarry_ref[...]
        put_dz(4, wR_ref[3:4, :] * dxc + wR_ref[2:3, :] * _shift_up(dxc, dxc_n, 1)
               + wR_ref[1:2, :] * _shift_up(dxc, dxc_n, 2) + wR_ref[0:1, :] * _shift_up(dxc, dxc_n, 3))
        xcarry_ref[...] = dxc[0:SUBLANES]

        c_u, c_v = z_of(6), z_of(7)
        u, du_dx = _gelu_and_grad(c_u)
        gv, dgv_dx = _gelu_and_grad(c_v)
        rr = lax.rsqrt(jnp.mean(gv * gv, axis=-1, keepdims=True) + NORM_EPS)
        xhat = gv * rr
        g_c = vec_ref[4:5, :]
        vn = xhat * g_c
        masks = _head_masks((GMLP_CHUNK, GROUP_W))
        tri_r = lax.broadcasted_iota(jnp.int32, (GMLP_CHUNK, GMLP_CHUNK), 0)
        tri_c = lax.broadcasted_iota(jnp.int32, (GMLP_CHUNK, GMLP_CHUNK), 1)
        tril = tri_r >= tri_c
        sgc, dsgc = _silu_and_grad(z_of(8))
        dyc = dy_ref[:, col(2)]
        dsp_full = dyc * u * sgc
        sp_parts, dvn_parts = [], []
        for c in range(tm // GMLP_CHUNK):
            rs = slice(c * GMLP_CHUNK, (c + 1) * GMLP_CHUNK)
            vc = vn[rs].astype(MXU_DTYPE)
            dsp_c = dsp_full[rs]
            bsacc_ref[...] += dsp_c
            acc = bs_ref[...]
            dvn_c = jnp.zeros((GMLP_CHUNK, GROUP_W), F32)
            for h in range(N_HEADS):
                w_h = ws_ref[h]
                acc = acc + jnp.where(masks[h], jnp.dot(w_h, vc, preferred_element_type=F32), 0.0)
                dsp_h = jnp.where(masks[h], dsp_c, 0.0).astype(MXU_DTYPE)
                dvn_c = dvn_c + _mm_tn(w_h, dsp_h)
                dws_ref[h] += jnp.where(tril, _mm_nt(dsp_h, vc), 0.0)
            sp_parts.append(acc)
            dvn_parts.append(dvn_c)
        spv = jnp.concatenate(sp_parts, axis=0)
        dvn = jnp.concatenate(dvn_parts, axis=0)
        put_dz(6, dyc * spv * sgc * du_dx)
        put_dz(8, dyc * u * spv * dsgc)
        dvec_ref[4:5, :] += _colsum(dvn * xhat)
        dgvn = dvn * g_c
        dgv = rr * (dgvn - xhat * jnp.mean(dgvn * xhat, axis=-1, keepdims=True))
        put_dz(7, dgv * dgv_dx)

        sgd, dsgd = _silu_and_grad(zg_ref[...])
        dyd = dy_ref[:, col(3)]
        ov = o_ref[...]
        do = dyd * sgd
        _put(do_ref, do)
        dzg_ref[...] = (dyd * ov * dsgd).astype(dzg_ref.dtype)
        prod = do * ov
        tmasks = _head_masks((tm, GROUP_W))
        dl = jnp.zeros((tm, GROUP_W), F32)
        for h in range(N_HEADS):
            dl = jnp.where(tmasks[h], jnp.sum(jnp.where(tmasks[h], prod, 0.0), axis=-1, keepdims=True), dl)
        _put(dl_ref, dl)
        for dil, d_out, l_out in zip(ATTN_DILATIONS, (do1_ref, do4_ref, do16_ref), (dl1_ref, dl4_ref, dl16_ref)):
            _deinterleave(do_ref, d_out, dil, tmp_ref)
            _deinterleave(dl_ref, l_out, dil, tmp_ref)

        @pl.when(i == nT - 1)
        def _():
            acc = bsacc_ref[...]
            lane = lax.broadcasted_iota(jnp.int32, (GMLP_CHUNK, LANES), 1)
            out = jnp.zeros((GMLP_CHUNK, LANES), F32)
            for h in range(N_HEADS):
                out = jnp.where(lane == h, jnp.sum(jnp.where(masks[h], acc, 0.0), axis=-1, keepdims=True), out)
            dbs_ref[...] = out

        dyn_ref[...] = dy_ref[0:SUBLANES, 0:GROUP_W]

    rev = lambda w: pl.BlockSpec((tm, w), lambda i: (nT - 1 - i, 0))
    prev8 = lambda w: pl.BlockSpec((SUBLANES, w), lambda i: (jnp.maximum((nT - 1 - i) * hb - 1, 0), 0))
    next8 = lambda w: pl.BlockSpec((SUBLANES, w), lambda i: (jnp.minimum((nT - i) * hb, last_blk), 0))
    const2 = lambda shape: pl.BlockSpec(shape, lambda i: (0, 0))
    dil_specs = [_dilated_spec(tm, GROUP_W, dil, lambda i: nT - 1 - i) for dil in ATTN_DILATIONS]
    dil_shapes = [_dilated_shape(S, GROUP_W, dil, F32) for dil in ATTN_DILATIONS]
    small = (SUBLANES, GROUP_W)
    sq = (GROUP_W, GROUP_W)
    ws_shape = (N_HEADS, GMLP_CHUNK, GMLP_CHUNK)
    return pl.pallas_call(
        body, name=name, grid=(nT,),
        in_specs=[rev(D), rev(E), pl.BlockSpec((E, D), lambda i: (0, 0), pipeline_mode=pl.Buffered(1)),
                  rev(wcols), prev8(wcols), next8(wcols), rev(GROUP_W), rev(GROUP_W), prev8(GROUP_W),
                  rev(GROUP_W), rev(GROUP_W), rev(GROUP_W)]
                 + _mix_specs(),
        out_specs=[const2((E, D)), rev(wcols), rev(GROUP_W)] + dil_specs + dil_specs
                  + [const2(small), const2(small), const2(small), const2(sq), const2(sq),
                     pl.BlockSpec(ws_shape, lambda i: (0, 0, 0)), const2((GMLP_CHUNK, LANES))],
        out_shape=[jax.ShapeDtypeStruct((E, D), WIRE_DTYPE),
                   jax.ShapeDtypeStruct((S, wcols), MXU_DTYPE), jax.ShapeDtypeStruct((S, GROUP_W), MXU_DTYPE)]
                  + [_dilated_shape(S, GROUP_W, dil, MXU_DTYPE) for dil in ATTN_DILATIONS] + dil_shapes
                  + [jax.ShapeDtypeStruct(small, F32)] * 3 + [jax.ShapeDtypeStruct(sq, F32)] * 2
                  + [jax.ShapeDtypeStruct(ws_shape, F32), jax.ShapeDtypeStruct((GMLP_CHUNK, LANES), F32)],
        scratch_shapes=[pltpu.VMEM(small, F32), pltpu.VMEM(small, F32), pltpu.VMEM((GMLP_CHUNK, GROUP_W), F32),
                        _lane_scratch(tm, GROUP_W), _lane_scratch(tm, GROUP_W),
                        _lane_scratch(tm, GROUP_W), _lane_scratch(tm, GROUP_W), pltpu.VMEM((hb, GROUP_W), F32),
                        pltpu.VMEM((tm, E), F32), pltpu.VMEM(small, F32), pltpu.VMEM((E, D), F32),
                        _lane_scratch(tm, GROUP_W)],
        compiler_params=_params(("arbitrary",)),
    )(dx, y, w_out, z, z, z, z_g, hs, hs, lru_a, lru_mult, o, mp["wA"], mp["wR"], mp["vec"], mp["wa"], mp["wx"], mp["ws"], mp["bs"])


def _attn_bwd(qkv, do, lse, delta, dil, name):
    S = qkv.shape[0] * qkv.shape[1]
    flat = lambda t: t.reshape(S, t.shape[2])
    qkv, do, lse, delta = flat(qkv), flat(do), flat(lse), flat(delta)
    nb = S // ATTN_BLOCK
    group = nb // dil
    scale = 1.0 / math.sqrt(HEAD_DIM)
    B = ATTN_BLOCK
    per_step = ATTN_BWD_BLOCKS_PER_STEP
    n_steps = nb // per_step

    def body(qc_ref, qn_ref, kc_ref, kp_ref, vc_ref, vp_ref, doc_ref, don_ref, lc_ref, ln_ref, dc_ref, dn_ref,
             dq_ref, dk_ref, dv_ref, bias_ref, bias_next_ref):
        n = pl.program_id(0)
        starts, next_starts = _group_starts(n, per_step, group)

        @pl.when(n == 0)
        def _():
            bias_ref[...] = _attn_bias(dil, (B,), 2 * B)
            bias_next_ref[...] = _attn_bias(dil, (B,), B)

        masks = _head_masks((B, GROUP_W))

        def per_row(tile):
            return jnp.concatenate([jnp.max(jnp.where(masks[h], tile, _NEG), axis=-1, keepdims=True)
                                    for h in range(N_HEADS)], axis=0)

        def grads(q, dov, lse_tile, dl_tile, keys, vals, bias, dead):
            qs = _stack_heads(q, masks)
            dos = _stack_heads(dov.astype(MXU_DTYPE), masks)
            s = _mm_nt(qs, keys) * scale + bias
            if dead is not None:
                s = jnp.where(dead(s.shape), _NEG, s)
            p = jnp.exp(s - per_row(lse_tile))
            ds = (p * (_mm_nt(dos, vals) - per_row(dl_tile)) * scale).astype(MXU_DTYPE)
            return ds, _mm_tn(ds, qs), _mm_tn(p.astype(MXU_DTYPE), dos)

        for j in range(per_step):
            own = slice(j * B, (j + 1) * B)
            before = slice((j - 1) * B, j * B)
            keys = jnp.concatenate([kp_ref[...] if j == 0 else kc_ref[before], kc_ref[own]], axis=0)
            vals = jnp.concatenate([vp_ref[...] if j == 0 else vc_ref[before], vc_ref[own]], axis=0)
            dead = None
            if starts(j) is not False:
                dead = lambda shape, j=j: starts(j) & (lax.broadcasted_iota(jnp.int32, shape, 1) < B)
            ds, dk2, dv2 = grads(qc_ref[own], doc_ref[own], lc_ref[own], dc_ref[own], keys, vals, bias_ref[...], dead)
            dq_ref[own] = _unstack_heads(jnp.dot(ds, keys, preferred_element_type=F32), masks).astype(dq_ref.dtype)
            if j > 0:
                dk_ref[before] = (dk_own + dk2[:B]).astype(dk_ref.dtype)
                dv_ref[before] = (dv_own + dv2[:B]).astype(dv_ref.dtype)
            dk_own, dv_own = dk2[B:], dv2[B:]
        last = slice((per_step - 1) * B, per_step * B)
        if next_starts is not True:
            _, dk1, dv1 = grads(qn_ref[...], don_ref[...], ln_ref[...], dn_ref[...], kc_ref[last], vc_ref[last],
                                bias_next_ref[...], lambda shape: next_starts)
            dk_own, dv_own = dk_own + dk1, dv_own + dv1
        dk_ref[last] = dk_own.astype(dk_ref.dtype)
        dv_ref[last] = dv_own.astype(dv_ref.dtype)

    blk = (per_step * B, GROUP_W)
    one = (B, GROUP_W)
    nxt_idx = lambda n: jnp.minimum((n + 1) * per_step, nb - 1)
    prv_idx = lambda n: jnp.maximum(n * per_step - 1, 0)
    zcur = lambda c: pl.BlockSpec(blk, lambda n: (n, c))
    znext = lambda c: pl.BlockSpec(one, lambda n: (nxt_idx(n), c))
    zprev = lambda c: pl.BlockSpec(one, lambda n: (prv_idx(n), c))
    cur = pl.BlockSpec(blk, lambda n: (n, 0))
    nxt = pl.BlockSpec(one, lambda n: (nxt_idx(n), 0))
    grads_out = pl.pallas_call(
        body, name=name, grid=(n_steps,),
        in_specs=[zcur(0), znext(0), zcur(1), zprev(1), zcur(2), zprev(2), cur, nxt, cur, nxt, cur, nxt],
        out_specs=[cur, cur, cur],
        out_shape=[jax.ShapeDtypeStruct((S, GROUP_W), WIRE_DTYPE)] * 3,
        scratch_shapes=[pltpu.VMEM((N_HEADS * B, 2 * B), F32), pltpu.VMEM((N_HEADS * B, B), F32)],
        compiler_params=_params(("arbitrary",)),
    )(qkv, qkv, qkv, qkv, qkv, qkv, do, do, lse, lse, delta, delta)
    return [t.reshape(dil, S // dil, GROUP_W) for t in grads_out]


def _inproj_bwd(x, g, dxn, dz_abc, dqkv, dz_g, w_t, name):
    S, D = x.shape
    N = w_t.shape[0]
    tm = TM_MM
    n_abc = N_ABC * GROUP_W

    def body(x_ref, g_ref, dxn_ref, dabc_ref, q1, k1, v1, q2, k2, v2, q3, k3, v3, dg_ref, w_ref,
             dx_ref, dz_ref, h_ref, dgn_ref, s4_ref, s16_ref, tmp_ref):
        i = pl.program_id(0)

        @pl.when(i == 0)
        def _():
            dgn_ref[...] = jnp.zeros_like(dgn_ref)

        dz_ref[:, 0:n_abc] = dabc_ref[...].astype(MXU_DTYPE)
        for j, parts in enumerate(((q1, q2, q3), (k1, k2, k3), (v1, v2, v3))):
            c0 = n_abc + j * GROUP_W
            _interleave(parts[1], s4_ref, ATTN_DILATIONS[1])
            _interleave(parts[2], s16_ref, ATTN_DILATIONS[2], tmp_ref)
            dz_ref[:, c0:c0 + GROUP_W] = (parts[0][0] + _get(s4_ref) + _get(s16_ref)).astype(MXU_DTYPE)
        dz_ref[:, n_abc + 3 * GROUP_W:] = dg_ref[...].astype(MXU_DTYPE)
        dh = jnp.dot(dz_ref[...], w_ref[...], preferred_element_type=F32)
        xv = x_ref[...]
        r = lax.rsqrt(jnp.mean(xv * xv, axis=-1, keepdims=True) + NORM_EPS)
        xn = xv * r
        gv = g_ref[...]
        h_ref[...] = (xn * gv).astype(MXU_DTYPE)
        dgn_ref[...] += _colsum(dh * xn)
        dn = dh * gv
        dx_ref[...] = dxn_ref[...] + r * (dn - xn * jnp.mean(dn * xn, axis=-1, keepdims=True))

    row = lambda w: pl.BlockSpec((tm, w), lambda i: (i, 0))
    flat = [t for p in dqkv for t in p]
    dil_specs = [_dilated_spec(tm, GROUP_W, dil) for dil in ATTN_DILATIONS for _ in range(3)]
    return pl.pallas_call(
        body, name=name, grid=(S // tm,),
        in_specs=[row(D), pl.BlockSpec((1, D), lambda i: (0, 0)), row(D), row(n_abc)] + dil_specs
                 + [row(GROUP_W), pl.BlockSpec((N, D), lambda i: (0, 0), pipeline_mode=pl.Buffered(1))],
        out_specs=[row(D), row(N), row(D), pl.BlockSpec((1, D), lambda i: (0, 0))],
        out_shape=[jax.ShapeDtypeStruct((S, D), F32), jax.ShapeDtypeStruct((S, N), MXU_DTYPE),
                   jax.ShapeDtypeStruct((S, D), MXU_DTYPE), jax.ShapeDtypeStruct((1, D), F32)],
        scratch_shapes=[_lane_scratch(tm, GROUP_W)] * 3,
        compiler_params=_params(("arbitrary",)),
    )(x, g, dxn, dz_abc, *flat, dz_g, w_t)


def _inproj_wgrad(h, dz, name):
    S, D = h.shape
    N = dz.shape[1]
    tm = TM_WGRAD
    nj = 2
    cw = N // nj
    per = N_DEV // nj
    n_loc = N // N_DEV

    def body(h_ref, dz_ref, dw_ref, acc_ref):
        i = pl.program_id(1)

        @pl.when(i == 0)
        def _():
            acc_ref[...] = jnp.zeros_like(acc_ref)

        acc_ref[...] += _mm_tn(dz_ref[...], h_ref[...])

        @pl.when(i == S // tm - 1)
        def _():
            for b in range(per):
                dw_ref[b] = acc_ref[b * n_loc:(b + 1) * n_loc, :].astype(dw_ref.dtype)

    return pl.pallas_call(
        body, name=name, grid=(nj, S // tm),
        in_specs=[pl.BlockSpec((tm, D), lambda j, i: (i, 0)), pl.BlockSpec((tm, cw), lambda j, i: (i, j))],
        out_specs=pl.BlockSpec((per, n_loc, D), lambda j, i: (j, 0, 0)),
        out_shape=jax.ShapeDtypeStruct((N_DEV, n_loc, D), WIRE_DTYPE),
        scratch_shapes=[pltpu.VMEM((cw, D), F32)],
        compiler_params=_params(("parallel", "arbitrary")),
    )(h, dz)


def _my_place():
    return lax.axis_index("x"), lax.axis_index("y"), lax.axis_index("c")


def _peer(x, y, c, k):
    px = 1 - x if k & 4 else x
    py = 1 - y if k & 2 else y
    pc = 1 - c if k & 1 else c
    return (px, py, pc), 4 * px + 2 * py + pc


HBM_SPEC = pl.BlockSpec(memory_space=pltpu.HBM)
SEM_SPEC = pl.BlockSpec(memory_space=pltpu.SEMAPHORE)
SPLIT_EFFECT = pltpu.SideEffectType.DATAFLOW_SIDE_EFFECTING
N_PEERS = N_DEV - 1


def _exchange_copies(srcs, lands, send_sems, recv_sems, whole, arrival):
    x, y, c = _my_place()
    me = 4 * x + 2 * y + c
    copies = []
    for t in range(len(srcs)):
        for k in range(1, N_DEV):
            peer, pidx = _peer(x, y, c, k)
            copies.append(pltpu.make_async_remote_copy(
                src_ref=srcs[t] if whole[t] else srcs[t].at[pidx],
                dst_ref=lands[t].at[pidx if arrival else me], send_sem=send_sems.at[t * N_PEERS + k - 1],
                recv_sem=recv_sems.at[t * N_PEERS + k - 1], device_id=peer, device_id_type=MESH))
    return copies


def _exchange_start(groups, name, after=None):
    sizes = [len(g) for g in groups]
    whole = [w for g in groups for _, w in g]
    srcs = [pltpu.with_memory_space_constraint(a, pltpu.HBM) for g in groups for a, _ in g]
    lands = [pltpu.with_memory_space_constraint(lax.empty(((N_DEV,) + a.shape) if w else a.shape, a.dtype), pltpu.HBM)
             for a, w in zip(srcs, whole)]
    n = len(srcs)
    n_g = len(groups)
    extra = [] if after is None else [after]
    n_in = 2 * n + len(extra)

    def body(*refs):
        src_refs, land_refs = refs[:n], refs[n:2 * n]
        sem_refs = refs[n_in + 2 * n:n_in + 2 * n + 2 * n_g]
        token = refs[-1]
        off = 0
        for gi, sz in enumerate(sizes):
            for send in _exchange_copies(src_refs[off:off + sz], land_refs[off:off + sz],
                                         sem_refs[2 * gi], sem_refs[2 * gi + 1], whole[off:off + sz], False):
                send.start()
            off += sz
        token[...] = jnp.zeros_like(token)

    sem_shapes = [pltpu.SemaphoreType.DMA((sz * N_PEERS,)) for sz in sizes for _ in range(2)]
    outs = pl.pallas_call(
        body, name=name,
        in_specs=[HBM_SPEC] * (2 * n) + [pl.BlockSpec(memory_space=pl.ANY)] * len(extra),
        out_specs=[HBM_SPEC] * (2 * n) + [SEM_SPEC] * (2 * n_g) + [pl.BlockSpec(memory_space=pltpu.VMEM)],
        out_shape=[pltpu.HBM(a.shape, a.dtype) for a in srcs + lands] + sem_shapes
                  + [jax.ShapeDtypeStruct((SUBLANES, LANES), F32)],
        input_output_aliases={i: i for i in range(2 * n)},
        compiler_params=pltpu.CompilerParams(has_side_effects=SPLIT_EFFECT),
    )(*srcs, *lands, *extra)
    handles, off = [], 0
    for gi, sz in enumerate(sizes):
        handles.append((outs[2 * n + 2 * gi], outs[2 * n + 2 * gi + 1], outs[off:off + sz], outs[n + off:n + off + sz],
                        whole[off:off + sz]))
        off += sz
    return handles, outs[-1]


def _exchange_wait(handle, after, name):
    send_sems, recv_sems, srcs, lands, whole = handle
    n = len(srcs)

    def body(*refs):
        src_refs, land_refs = refs[:n], refs[n:2 * n]
        for send in _exchange_copies(src_refs, land_refs, refs[2 * n], refs[2 * n + 1], whole, False):
            send.wait_send()
        for arrival in _exchange_copies(src_refs, land_refs, refs[2 * n], refs[2 * n + 1], whole, True):
            arrival.wait_recv()

    outs = pl.pallas_call(
        body, name=name,
        in_specs=[HBM_SPEC] * (2 * n) + [SEM_SPEC, SEM_SPEC, pl.BlockSpec(memory_space=pl.ANY)],
        out_specs=[HBM_SPEC] * (2 * n),
        out_shape=[pltpu.HBM(a.shape, a.dtype) for a in list(srcs) + list(lands)],
        input_output_aliases={i: i for i in range(2 * n)},
        compiler_params=pltpu.CompilerParams(has_side_effects=SPLIT_EFFECT),
    )(*srcs, *lands, send_sems, recv_sems, after)
    x, y, c = _my_place()
    me = 4 * x + 2 * y + c
    own = [s[None] if w else lax.dynamic_slice_in_dim(s, me, 1, axis=0) for s, w in zip(outs[:n], whole)]
    return [lax.dynamic_update_slice_in_dim(ld, o, me, axis=0) for ld, o in zip(outs[n:], own)]


def _sum_slots(parts, name):
    n = len(parts)

    def body(*refs):
        for p_ref, o_ref in zip(refs[:n], refs[n:]):
            acc = p_ref[0]
            for j in range(1, N_DEV):
                acc = acc + p_ref[j]
            o_ref[...] = acc

    vm = pl.BlockSpec(memory_space=pltpu.VMEM)
    return pl.pallas_call(
        body, name=name, in_specs=[vm] * n, out_specs=[vm] * n,
        out_shape=[jax.ShapeDtypeStruct(p.shape[1:], F32) for p in parts],
        compiler_params=pltpu.CompilerParams(vmem_limit_bytes=VMEM_LIMIT),
    )(*parts)


def _adamw_math(w, g, m, v):
    m = ADAM_B1 * m + (1.0 - ADAM_B1) * g
    v = ADAM_B2 * v + (1.0 - ADAM_B2) * (g * g)
    m_hat = m / (1.0 - ADAM_B1 ** ADAM_STEP)
    v_hat = v / (1.0 - ADAM_B2 ** ADAM_STEP)
    delta = -ADAM_LR * (m_hat / (jnp.sqrt(v_hat) + ADAM_EPS) + ADAM_WD * w)
    return delta, m, v


def _adamw_summed(parts, w, m, v, tr, name):
    depth, R, C = w.shape

    def body(*refs):
        p_refs = refs[:depth]
        w_ref, m_ref, v_ref, g_ref, d_ref, nm_ref, nv_ref = refs[depth:]
        lay = pl.program_id(0)
        for l in range(depth):
            @pl.when(lay == l)
            def _(p_ref=p_refs[l]):
                g = p_ref[0].astype(F32)
                for j in range(1, N_DEV):
                    g = g + p_ref[j].astype(F32)
                g_ref[0] = g
        d_ref[0], nm_ref[0], nv_ref[0] = _adamw_math(w_ref[0], g_ref[0], m_ref[0], v_ref[0])

    part_spec = lambda l: pl.BlockSpec((N_DEV, tr, C), lambda lay, i: (0, jnp.where(lay == l, i, 0), 0))
    row = pl.BlockSpec((1, tr, C), lambda lay, i: (lay, i, 0))
    return pl.pallas_call(
        body, name=name, grid=(depth, R // tr),
        in_specs=[part_spec(l) for l in range(depth)] + [row, row, row],
        out_specs=[row] * 4, out_shape=[jax.ShapeDtypeStruct((depth, R, C), F32)] * 4,
        compiler_params=_params(("arbitrary", "arbitrary")),
    )(*parts, w, m, v)


def _adamw_small(w, g, m, v, name):
    def body(w_ref, g_ref, m_ref, v_ref, d_ref, nm_ref, nv_ref):
        d_ref[...], nm_ref[...], nv_ref[...] = _adamw_math(w_ref[...], g_ref[...], m_ref[...], v_ref[...])

    vm = pl.BlockSpec(memory_space=pltpu.VMEM)
    return pl.pallas_call(
        body, name=name, in_specs=[vm] * 4, out_specs=[vm] * 3,
        out_shape=[jax.ShapeDtypeStruct(w.shape, F32)] * 3,
        compiler_params=pltpu.CompilerParams(vmem_limit_bytes=VMEM_LIMIT),
    )(w, g, m, v)


def _pack(arrays):
    flat = jnp.concatenate([a.reshape(-1) for a in arrays])
    pad = (-flat.shape[0]) % (SUBLANES * LANES)
    return jnp.pad(flat, (0, pad)).reshape(-1, LANES)


def _unpack(buf, like):
    flat = buf.reshape(-1)
    out, off = [], 0
    for a in like:
        out.append(flat[off:off + a.size].reshape(a.shape))
        off += a.size
    return out


def _block_diag(w):
    eye = jnp.eye(N_HEADS, dtype=w.dtype)
    return jnp.einsum('hij,hk->hikj', w, eye).reshape(GROUP_W, GROUP_W)


def _diag_blocks(w):
    return jnp.einsum('hihj->hij', w.reshape(N_HEADS, HEAD_DIM, N_HEADS, HEAD_DIM))


def _pad_rows(a):
    return jnp.pad(a, ((0, SUBLANES - a.shape[0]), (0, 0)))


def _mixer_params(l, conv_a_w, conv_r_w, conv_r_b, lru_wa, lru_ba, lru_wx, lru_bx, lru_lambda, gmlp_norm_g,
                  gmlp_ws, gmlp_bs):
    tril = jnp.tril(jnp.ones((GMLP_CHUNK, GMLP_CHUNK), dtype=bool))
    vec = jnp.stack([conv_r_b[l], lru_ba[l], lru_bx[l], lru_lambda[l], gmlp_norm_g[l]])
    return {
        "wA": _pad_rows(conv_a_w[l]), "wR": _pad_rows(conv_r_w[l]), "vec": _pad_rows(vec),
        "wa": _block_diag(lru_wa[l]).astype(MXU_DTYPE), "wx": _block_diag(lru_wx[l]).astype(MXU_DTYPE),
        "ws": jnp.where(tril[None], gmlp_ws[l], 0.0).astype(MXU_DTYPE),
        "bs": jnp.repeat(jnp.transpose(gmlp_bs[l]), HEAD_DIM, axis=1),
    }


MIXER_NAMES = ("conv_a_w", "conv_r_w", "conv_r_b", "lru_wa", "lru_ba", "lru_wx", "lru_bx", "lru_lambda",
               "gmlp_norm_g", "gmlp_ws", "gmlp_bs")
SMALL_NAMES = ("norm_g",) + MIXER_NAMES + ("final_g",)


def _local_step(x, loss_target, norm_g, get_w_in, get_w_out, emit_early, emit_late, conv_a_w, conv_r_w, conv_r_b,
                lru_wa, lru_ba, lru_wx, lru_bx, lru_lambda, gmlp_norm_g, gmlp_ws, gmlp_bs, final_g):
    depth = norm_g.shape[0]
    D = x.shape[1]
    small = (conv_a_w, conv_r_w, conv_r_b, lru_wa, lru_ba, lru_wx, lru_bx, lru_lambda, gmlp_norm_g, gmlp_ws, gmlp_bs)
    saved = []
    for l in range(depth):
        mp = _mixer_params(l, *small)
        w_in_l = get_w_in(l, x)
        z, z_g, *qkv, y_abc, hs, lru_a, lru_mult = _inproj_mix_fwd(
            x, norm_g[l].reshape(1, D), w_in_l, mp, f"inproj_mix_fwd_{l}")
        attn =[_attn_fwd(qkv[p], dil, f"attn_fwd_d{dil}_{l}") for p, dil in enumerate(ATTN_DILATIONS)]
        w_out_l = get_w_out(l, y_abc)
        head = (final_g.reshape(1, D), loss_target) if l == depth - 1 else None
        x_new, y, o, *lse = _outproj(x, z_g, y_abc, attn, w_out_l, f"outproj_{l}", head)
        if head is not None:
            *lse, loss, d_final_g = lse
        saved.append((x, z, z_g, qkv, (hs, lru_a, lru_mult), y, o, lse, mp, w_in_l, w_out_l))
        x = x_new
    dx = x
    token = None
    for l in reversed(range(depth)):
        x_l, z, z_g, qkv, lru, y, o, lse, mp, w_in_l, w_out_l = saved[l]
        if token is not None:
            mp = dict(mp, vec=mp["vec"] + token[0, 0])
        (dw_out, dz_abc, dz_g, do1, do4, do16, dl1, dl4, dl16, dwA, dwR, dvec, dwa, dwx, dws, dbs) = _outproj_mix_bwd(
            dx, y, w_out_l, z, z_g, *lru, o, mp, f"outproj_mix_bwd_{l}")
        token = emit_early(l, dw_out, [
            dwA[:conv_a_w.shape[1]], dwR[:conv_r_w.shape[1]], dvec[0], _diag_blocks(dwa), dvec[1], _diag_blocks(dwx),
            dvec[2], dvec[3], dvec[4], dws, jnp.transpose(dbs[:, :N_HEADS])])
        g_row = norm_g[l].reshape(1, D)
        if token is not None:
            g_row = g_row + token[0, 0]
        dqkv = [_attn_bwd(qkv[p], do, lse[p], dl, dil, f"attn_bwd_d{dil}_{l}")
                for p, (dil, do, dl) in enumerate(zip(ATTN_DILATIONS, (do1, do4, do16), (dl1, dl4, dl16)))]
        dx, dz, h, dng = _inproj_bwd(x_l, g_row, dx, dz_abc, dqkv, dz_g, w_in_l, f"inproj_bwd_{l}")
        dw_in = _inproj_wgrad(h, dz, f"inproj_wgrad_{l}")
        token = emit_late(l, dw_in, [dng[0]] + ([d_final_g[0]] if l == depth - 1 else []))
    return loss[0, 0], dx
WEIGHT_NAMES = ("norm_g", "w_in", "conv_a_w", "conv_r_w", "conv_r_b", "lru_wa", "lru_ba", "lru_wx", "lru_bx",
                "lru_lambda", "gmlp_norm_g", "gmlp_ws", "gmlp_bs", "w_out", "final_g")


def kernel(x, norm_g, w_in, conv_a_w, conv_r_w, conv_r_b, lru_wa, lru_ba, lru_wx, lru_bx, lru_lambda, gmlp_norm_g, gmlp_ws, gmlp_bs, w_out, final_g, loss_target, m_norm_g, m_w_in, m_conv_a_w, m_conv_r_w, m_conv_r_b, m_lru_wa, m_lru_ba, m_lru_wx, m_lru_bx, m_lru_lambda, m_gmlp_norm_g, m_gmlp_ws, m_gmlp_bs, m_w_out, m_final_g, v_norm_g, v_w_in, v_conv_a_w, v_conv_r_w, v_conv_r_b, v_lru_wa, v_lru_ba, v_lru_wx, v_lru_bx, v_lru_lambda, v_gmlp_norm_g, v_gmlp_ws, v_gmlp_bs, v_w_out, v_final_g):
    w = dict(norm_g=norm_g, w_in=w_in, conv_a_w=conv_a_w, conv_r_w=conv_r_w, conv_r_b=conv_r_b, lru_wa=lru_wa,
             lru_ba=lru_ba, lru_wx=lru_wx, lru_bx=lru_bx, lru_lambda=lru_lambda, gmlp_norm_g=gmlp_norm_g,
             gmlp_ws=gmlp_ws, gmlp_bs=gmlp_bs, w_out=w_out, final_g=final_g)
    m = dict(norm_g=m_norm_g, w_in=m_w_in, conv_a_w=m_conv_a_w, conv_r_w=m_conv_r_w, conv_r_b=m_conv_r_b,
             lru_wa=m_lru_wa, lru_ba=m_lru_ba, lru_wx=m_lru_wx, lru_bx=m_lru_bx, lru_lambda=m_lru_lambda,
             gmlp_norm_g=m_gmlp_norm_g, gmlp_ws=m_gmlp_ws, gmlp_bs=m_gmlp_bs, w_out=m_w_out, final_g=m_final_g)
    v = dict(norm_g=v_norm_g, w_in=v_w_in, conv_a_w=v_conv_a_w, conv_r_w=v_conv_r_w, conv_r_b=v_conv_r_b,
             lru_wa=v_lru_wa, lru_ba=v_lru_ba, lru_wx=v_lru_wx, lru_bx=v_lru_bx, lru_lambda=v_lru_lambda,
             gmlp_norm_g=v_gmlp_norm_g, gmlp_ws=v_gmlp_ws, gmlp_bs=v_gmlp_bs, w_out=v_w_out, final_g=v_final_g)
    depth, D, n_loc = w_in.shape
    e_loc = w_out.shape[1]
    cx, cy, cc = _my_place()
    me = 4 * cx + 2 * cy + cc

    transposed = lambda a: jnp.transpose(a, (0, 2, 1))
    w_in_t, m_w_in_t, v_w_in_t = transposed(w_in), transposed(m_w_in), transposed(v_w_in)
    w_in_w, w_out_w = w_in_t.astype(MXU_DTYPE), w_out.astype(MXU_DTYPE)
    c_loc = conv_a_w.shape[2]
    taps = (conv_a_w, conv_r_w)
    first, _ = _exchange_start([[(w_in_w[0], True), (_pack(taps), True)], [(w_out_w[0], True)]], "gather_start_first")
    full_in = lambda g: g.reshape(N_DEV * n_loc, D)
    full_out = lambda g: g.reshape(N_DEV * e_loc, D)

    g_in0, g_taps = _exchange_wait(first[0], x, "gather_wait_in_0")
    groups = [[(w_in_w[l], True), (w_out_w[l], True)] for l in range(1, depth)]
    gathers, rest_token = _exchange_start(groups, "gather_start_rest", after=g_taps)
    g_taps = g_taps.reshape(N_DEV, -1) + rest_token[0, 0]
    conv_full, off = [], 0
    for a in taps:
        part = g_taps[:, off:off + a.size].reshape((N_DEV,) + a.shape)
        conv_full.append(jnp.transpose(part, (1, 2, 0, 3)).reshape(a.shape[:2] + (N_DEV * c_loc,)))
        off += a.size
    conv_a_full, conv_r_full = conv_full
    later = {}

    def get_w_in(l, after):
        if l == 0:
            return full_in(g_in0)
        g_in, later[l] = _exchange_wait(gathers[l - 1], after, f"gather_wait_{l}")
        return full_in(g_in)

    def get_w_out(l, after):
        if l == 0:
            return full_out(_exchange_wait(first[1], after, "gather_wait_out_0")[0])
        return full_out(later[l])

    early, late, last_token, held = {}, {}, [None], {}

    def emit_early(l, dw_out, mixer_grads):
        group = [(dw_out.reshape(N_DEV, e_loc, D), False), (_pack(mixer_grads), True)]
        if l > 0:
            held[l] = (group, mixer_grads)
            return None
        handles, token = _exchange_start([group], f"early_start_{l}")
        early[l] = (handles[0], mixer_grads)
        return token

    def emit_late(l, dw_in, norm_grads):
        groups = [[(_pack(norm_grads), True)], [(dw_in, False)]] + ([held[l][0]] if l in held else [])
        handles, token = _exchange_start(groups, f"late_start_{l}")
        late[l] = (handles[0], handles[1], norm_grads)
        if l in held:
            early[l] = (handles[2], held[l][1])
        last_token[0] = token
        return token

    loss, grad_x = _local_step(
        x[0], loss_target[0], norm_g, get_w_in, get_w_out, emit_early, emit_late, conv_a_full, conv_r_full, conv_r_b,
        lru_wa, lru_ba, lru_wx, lru_bx, lru_lambda, gmlp_norm_g, gmlp_ws, gmlp_bs, final_g)
    loss = lax.psum(loss, ("x", "y", "c"))

    r_in, r_out, small_parts = {}, {}, []
    for l in reversed(range(depth)):
        r_out[l], r_mix = _exchange_wait(early[l][0], last_token[0], f"early_wait_{l}")
        (r_norm,) = _exchange_wait(late[l][0], last_token[0], f"late_wait_norm_{l}")
        small_parts += [r_mix, r_norm]
        if l > 0:
            (r_in[l],) = _exchange_wait(late[l][1], last_token[0], f"late_wait_{l}")
    big = {"w_out": _adamw_summed([r_out[l] for l in range(depth)], w_out, m_w_out, v_w_out, e_loc, "adamw_w_out")}

    sums = _sum_slots(small_parts, "sum_small_grads")
    by_layer = {}
    for i, l in enumerate(reversed(range(depth))):
        mix = _unpack(sums[2 * i], early[l][1])
        nrm = _unpack(sums[2 * i + 1], late[l][2])
        by_layer[l] = dict(zip(MIXER_NAMES, mix), norm_g=nrm[0])
        if l == depth - 1:
            g_final = nrm[1]
    g_small = {k: jnp.stack([by_layer[l][k] for l in range(depth)]) for k in ("norm_g",) + MIXER_NAMES}
    g_small["final_g"] = g_final
    for k in ("conv_a_w", "conv_r_w"):
        g_small[k] = lax.dynamic_slice_in_dim(g_small[k], me * c_loc, c_loc, axis=2)
    packs = [_pack([d[k] for k in SMALL_NAMES]) for d in (w, g_small, m, v)]
    res = _adamw_small(*packs, "adamw_small")
    like = [w[k] for k in SMALL_NAMES]
    d_s, m_s, v_s = (dict(zip(SMALL_NAMES, _unpack(r, like))) for r in res)

    (r_in[0],) = _exchange_wait(late[0][1], res[0], "late_wait_0")
    big["w_in"] = [transposed(a) for a in _adamw_summed(
        [r_in[l] for l in range(depth)], w_in_t, m_w_in_t, v_w_in_t, n_loc // 2, "adamw_w_in")]

    grad, delta, new_m, new_v = {}, {}, {}, {}
    for k in WEIGHT_NAMES:
        if k in big:
            grad[k], delta[k], new_m[k], new_v[k] = big[k]
        else:
            grad[k], delta[k], new_m[k], new_v[k] = g_small[k], d_s[k], m_s[k], v_s[k]
    return (loss, grad_x[None], *[grad[k] for k in WEIGHT_NAMES], *[delta[k] for k in WEIGHT_NAMES],
            *[new_m[k] for k in WEIGHT_NAMES], *[new_v[k] for k in WEIGHT_NAMES])
```
